```python
import jax
import jax.numpy as jnp
from jax import lax
import numpy as np

D_MODEL = 1024
BATCH = 8
SEQ = 2048
DEPTH = 1

N_MEM = 256
GRID_W = 64
EPS = 1e-6

ATT_HEADS = 8
ATT_KV_HEADS = 2
ATT_HEAD_DIM = 64
ATT_Q_DIM = ATT_HEADS * ATT_HEAD_DIM
ATT_KV_DIM = ATT_KV_HEADS * ATT_HEAD_DIM
ATT_BLOCK = 128
ROPE_THETA = 10000.0
ROPE_PAIRS_PER_AXIS = ATT_HEAD_DIM // 4

HG_HEADS = 4
HG_HEAD_K = 128
HG_HEAD_V = 128
HG_KEY_DIM = HG_HEADS * HG_HEAD_K
HG_VAL_DIM = HG_HEADS * HG_HEAD_V
HG_CHUNK = 32

MIX_WIDTH = ATT_Q_DIM + HG_VAL_DIM
IN_SPLITS = (ATT_Q_DIM, ATT_KV_DIM, ATT_KV_DIM, HG_KEY_DIM, HG_KEY_DIM, HG_KEY_DIM, HG_VAL_DIM, HG_VAL_DIM)
N_IN = ATT_Q_DIM + 2 * ATT_KV_DIM + 3 * HG_KEY_DIM + 2 * HG_VAL_DIM

X_HEADS = 4
X_HEAD_DIM = D_MODEL // X_HEADS

D_FF = 2816
CONV_W = 3

kernel_name = "hymba_axial_gqa_hgrn2_sandwich_convffn"


def rmsnorm(x, g):
    xf = x.astype(jnp.float32)
    y = xf * lax.rsqrt(jnp.mean(xf * xf, axis=-1, keepdims=True) + EPS)
    return (y * g.astype(jnp.float32)).astype(x.dtype)


def split_columns(p, sizes):
    idx = np.cumsum(np.array(sizes))[:-1].tolist()
    return jnp.split(p, idx, axis=-1)


def axial_rope_tables(n):
    rows = n // GRID_W
    r, c = jnp.meshgrid(jnp.arange(rows), jnp.arange(GRID_W), indexing="ij")
    inv = jnp.power(ROPE_THETA, -jnp.arange(ROPE_PAIRS_PER_AXIS, dtype=jnp.float32) / ROPE_PAIRS_PER_AXIS)
    ang = jnp.concatenate([r.reshape(-1, 1).astype(jnp.float32) * inv,
                           c.reshape(-1, 1).astype(jnp.float32) * inv], axis=-1)
    return jnp.cos(ang), jnp.sin(ang)


def apply_rope(x, cos, sin):
    b, n, h, d = x.shape
    xp = x.reshape(b, n, h, d // 2, 2)
    x0, x1 = xp[..., 0], xp[..., 1]
    c = cos[None, :, None, :].astype(x.dtype)
    s = sin[None, :, None, :].astype(x.dtype)
    return jnp.stack([x0 * c - x1 * s, x0 * s + x1 * c], axis=-1).reshape(b, n, h, d)


def axial_gqa_attention(q, k, v):
    b, n = q.shape[0], q.shape[1]
    grp = ATT_HEADS // ATT_KV_HEADS
    nb = n // ATT_BLOCK
    qb = jnp.moveaxis(q.reshape(b, nb, ATT_BLOCK, ATT_KV_HEADS, grp, ATT_HEAD_DIM), 1, 0)
    scale = ATT_HEAD_DIM ** -0.5

    def one_block(qblk):
        s = jnp.einsum("bqhgd,bkhd->bhgqk", qblk, k).astype(jnp.float32) * scale
        p = jax.nn.softmax(s, axis=-1).astype(v.dtype)
        return jnp.einsum("bhgqk,bkhd->bqhgd", p, v)

    o = lax.map(one_block, qb)
    return jnp.moveaxis(o, 0, 1).reshape(b, n, ATT_Q_DIM)


def bidirectional_gated_scan(q, logf_fwd, logf_bwd, v):
    b, n, h, kd = q.shape
    vd = v.shape[-1]
    nc = n // HG_CHUNK

    def flip(a):
        return a[:, ::-1]

    qs = jnp.stack([q, flip(q)])
    lf = jnp.stack([logf_fwd, flip(logf_bwd)])
    vs = jnp.stack([v, flip(v)])
    ks = -jnp.expm1(lf)

    def to_chunks(a):
        a = a.reshape(2, b, nc, HG_CHUNK, h, a.shape[-1])
        return jnp.transpose(a, (2, 0, 1, 4, 3, 5))

    tri = jnp.tril(jnp.ones((HG_CHUNK, HG_CHUNK), dtype=bool))[:, :, None]

    def step(S, xs):
        qx, kx, vx, lx = xs
        bcum = jnp.cumsum(lx, axis=-2)
        diff = bcum[..., :, None, :] - bcum[..., None, :, :]
        dec = jnp.exp(jnp.where(tri, diff, -jnp.inf))
        a = jnp.einsum("...tk,...sk,...tsk->...ts", qx, kx, dec)
        o_intra = jnp.einsum("...ts,...sv->...tv", a, vx)
        o_inter = jnp.einsum("...tk,...kv->...tv", qx * jnp.exp(bcum), S)
        b_last = bcum[..., -1:, :]
        k_dec = kx * jnp.exp(b_last - bcum)
        S_new = jnp.exp(b_last[..., 0, :])[..., :, None] * S + jnp.einsum("...sk,...sv->...kv", k_dec, vx)
        return S_new, o_intra + o_inter

    S0 = jnp.zeros((2, b, h, kd, vd), jnp.float32)
    _, out = lax.scan(step, S0, (to_chunks(qs), to_chunks(ks), to_chunks(vs), to_chunks(lf)))
    out = jnp.transpose(out, (1, 2, 0, 4, 3, 5)).reshape(2, b, n, h, vd)
    return out[0] + flip(out[1])


def hgrn2_group(q, zf_fwd, zf_bwd, i_in, g, lb, out_g):
    b, n = q.shape[0], q.shape[1]

    def heads(a, d):
        return a.reshape(b, n, HG_HEADS, d)

    def log_forget(z, lb_dir):
        f = lb_dir + (1.0 - lb_dir) * jax.nn.sigmoid(z.astype(jnp.float32))
        return heads(jnp.log(f), HG_HEAD_K)

    qf = heads(jax.nn.silu(q.astype(jnp.float32)), HG_HEAD_K)
    o = bidirectional_gated_scan(qf, log_forget(zf_fwd, lb[0]), log_forget(zf_bwd, lb[1]),
                                 heads(i_in.astype(jnp.float32), HG_HEAD_V))
    o = rmsnorm(o, out_g).reshape(b, n, HG_VAL_DIM)
    return (o * jax.nn.silu(g.astype(jnp.float32))).astype(g.dtype)


def memory_cross_attention(h, m, wq, wkv, wo):
    b, n = h.shape[0], h.shape[1]
    nm = m.shape[1]
    q = (h @ wq).reshape(b, n, X_HEADS, X_HEAD_DIM)
    kv = (m @ wkv).reshape(b, nm, 2, X_HEADS, X_HEAD_DIM)
    k, v = kv[:, :, 0], kv[:, :, 1]
    s = jnp.einsum("bqhd,bkhd->bhqk", q, k).astype(jnp.float32) * (X_HEAD_DIM ** -0.5)
    p = jax.nn.softmax(s, axis=-1).astype(v.dtype)
    o = jnp.einsum("bhqk,bkhd->bqhd", p, v).reshape(b, n, X_HEADS * X_HEAD_DIM)
    return o @ wo


def conv_ffn(h, w_up, conv_w, conv_b, w_down):
    n = h.shape[1]
    u = h @ w_up
    half = CONV_W // 2
    up = jnp.pad(u, ((0, 0), (half, half), (0, 0)))
    acc = conv_b
    for j in range(CONV_W):
        acc = acc + up[:, j:j + n] * conv_w[j]
    gate, val = jnp.split(acc, 2, axis=-1)
    return (jax.nn.silu(gate) * val) @ w_down


def _fwd_setup_inputs(seed: int = 0) -> dict:
    key = jax.random.key(seed)
    ks = jax.random.split(key, 22)
    f32 = jnp.float32
    L = DEPTH

    def nrm(k, shape, scale):
        return jax.random.normal(k, shape, f32) * scale

    def gain(k, shape):
        return 1.0 + nrm(k, shape, 0.05)

    return {
        "x": nrm(ks[0], (BATCH, SEQ, D_MODEL), 1.0),
        "mem": nrm(ks[1], (BATCH, N_MEM, D_MODEL), 1.0),
        "pre_mix_g": gain(ks[2], (L, D_MODEL)),
        "w_in": nrm(ks[3], (L, D_MODEL, N_IN), D_MODEL ** -0.5),
        "q_norm_g": gain(ks[4], (L, ATT_HEAD_DIM)),
        "k_norm_g": gain(ks[5], (L, ATT_HEAD_DIM)),
        "hg_lb": nrm(ks[6], (2, L + 1, HG_KEY_DIM), 0.5),
        "hg_out_norm_g": gain(ks[7], (L, HG_HEAD_V)),
        "w_out": nrm(ks[8], (L, MIX_WIDTH, D_MODEL), MIX_WIDTH ** -0.5),
        "post_mix_g": gain(ks[9], (L, D_MODEL)),
        "pre_x_g": gain(ks[10], (L, D_MODEL)),
        "mem_norm_g": gain(ks[11], (L, D_MODEL)),
        "w_xq": nrm(ks[12], (L, D_MODEL, D_MODEL), D_MODEL ** -0.5),
        "w_xkv": nrm(ks[13], (L, D_MODEL, 2 * D_MODEL), D_MODEL ** -0.5),
        "w_xo": nrm(ks[14], (L, D_MODEL, D_MODEL), D_MODEL ** -0.5),
        "post_x_g": gain(ks[15], (L, D_MODEL)),
        "pre_ffn_g": gain(ks[16], (L, D_MODEL)),
        "w_up": nrm(ks[17], (L, D_MODEL, 2 * D_FF), D_MODEL ** -0.5),
        "conv_w": nrm(ks[18], (L, CONV_W, 2 * D_FF), CONV_W ** -0.5),
        "conv_b": nrm(ks[19], (L, 2 * D_FF), 0.02),
        "w_down": nrm(ks[20], (L, D_FF, D_MODEL), D_FF ** -0.5),
        "post_ffn_g": gain(ks[21], (L, D_MODEL)),
    }


def _fwd_reference(x, mem, pre_mix_g, w_in, q_norm_g, k_norm_g, hg_lb, hg_out_norm_g, w_out, post_mix_g,
              pre_x_g, mem_norm_g, w_xq, w_xkv, w_xo, post_x_g, pre_ffn_g, w_up, conv_w, conv_b,
              w_down, post_ffn_g):
    b, n = x.shape[0], x.shape[1]
    cos, sin = axial_rope_tables(n)
    lb_all = jnp.cumsum(jax.nn.softmax(hg_lb.astype(jnp.float32), axis=1), axis=1)
    for l in range(DEPTH):
        h = rmsnorm(x, pre_mix_g[l])
        aq, ak, av, hq, hf_fwd, hf_bwd, hi, hg = split_columns(h @ w_in[l], IN_SPLITS)
        aq = rmsnorm(aq.reshape(b, n, ATT_HEADS, ATT_HEAD_DIM), q_norm_g[l])
        ak = rmsnorm(ak.reshape(b, n, ATT_KV_HEADS, ATT_HEAD_DIM), k_norm_g[l])
        av = av.reshape(b, n, ATT_KV_HEADS, ATT_HEAD_DIM)
        att = axial_gqa_attention(apply_rope(aq, cos, sin), apply_rope(ak, cos, sin), av)
        rec = hgrn2_group(hq, hf_fwd, hf_bwd, hi, hg, lb_all[:, l], hg_out_norm_g[l])
        mixed = jnp.concatenate([att, rec], axis=-1) @ w_out[l]
        x = x + rmsnorm(mixed, post_mix_g[l])
        h = rmsnorm(x, pre_x_g[l])
        m = rmsnorm(mem, mem_norm_g[l])
        x = x + rmsnorm(memory_cross_attention(h, m, w_xq[l], w_xkv[l], w_xo[l]), post_x_g[l])
        h = rmsnorm(x, pre_ffn_g[l])
        x = x + rmsnorm(conv_ffn(h, w_up[l], conv_w[l], conv_b[l], w_down[l]), post_ffn_g[l])
    return x


import jax as _jax
import jax.numpy as _jnp

TWIN_FORMAT = 'train_step'
FWD_PARAMS = ['x', 'mem', 'pre_mix_g', 'w_in', 'q_norm_g', 'k_norm_g', 'hg_lb', 'hg_out_norm_g', 'w_out', 'post_mix_g', 'pre_x_g', 'mem_norm_g', 'w_xq', 'w_xkv', 'w_xo', 'post_x_g', 'pre_ffn_g', 'w_up', 'conv_w', 'conv_b', 'w_down', 'post_ffn_g']
TWIN_WEIGHTS = ['pre_mix_g', 'w_in', 'q_norm_g', 'k_norm_g', 'hg_lb', 'hg_out_norm_g', 'w_out', 'post_mix_g', 'pre_x_g', 'mem_norm_g', 'w_xq', 'w_xkv', 'w_xo', 'post_x_g', 'pre_ffn_g', 'w_up', 'conv_w', 'conv_b', 'w_down', 'post_ffn_g']
TWIN_DIFF_INPUT = 'x'
TWIN_INPUTS = ['x', 'mem', 'pre_mix_g', 'w_in', 'q_norm_g', 'k_norm_g', 'hg_lb', 'hg_out_norm_g', 'w_out', 'post_mix_g', 'pre_x_g', 'mem_norm_g', 'w_xq', 'w_xkv', 'w_xo', 'post_x_g', 'pre_ffn_g', 'w_up', 'conv_w', 'conv_b', 'w_down', 'post_ffn_g', 'loss_target', 'm_pre_mix_g', 'm_w_in', 'm_q_norm_g', 'm_k_norm_g', 'm_hg_lb', 'm_hg_out_norm_g', 'm_w_out', 'm_post_mix_g', 'm_pre_x_g', 'm_mem_norm_g', 'm_w_xq', 'm_w_xkv', 'm_w_xo', 'm_post_x_g', 'm_pre_ffn_g', 'm_w_up', 'm_conv_w', 'm_conv_b', 'm_w_down', 'm_post_ffn_g', 'v_pre_mix_g', 'v_w_in', 'v_q_norm_g', 'v_k_norm_g', 'v_hg_lb', 'v_hg_out_norm_g', 'v_w_out', 'v_post_mix_g', 'v_pre_x_g', 'v_mem_norm_g', 'v_w_xq', 'v_w_xkv', 'v_w_xo', 'v_post_x_g', 'v_pre_ffn_g', 'v_w_up', 'v_conv_w', 'v_conv_b', 'v_w_down', 'v_post_ffn_g']
TWIN_OUTPUTS = ['loss', 'grad_x', 'grad_pre_mix_g', 'grad_w_in', 'grad_q_norm_g', 'grad_k_norm_g', 'grad_hg_lb', 'grad_hg_out_norm_g', 'grad_w_out', 'grad_post_mix_g', 'grad_pre_x_g', 'grad_mem_norm_g', 'grad_w_xq', 'grad_w_xkv', 'grad_w_xo', 'grad_post_x_g', 'grad_pre_ffn_g', 'grad_w_up', 'grad_conv_w', 'grad_conv_b', 'grad_w_down', 'grad_post_ffn_g', 'delta_pre_mix_g', 'delta_w_in', 'delta_q_norm_g', 'delta_k_norm_g', 'delta_hg_lb', 'delta_hg_out_norm_g', 'delta_w_out', 'delta_post_mix_g', 'delta_pre_x_g', 'delta_mem_norm_g', 'delta_w_xq', 'delta_w_xkv', 'delta_w_xo', 'delta_post_x_g', 'delta_pre_ffn_g', 'delta_w_up', 'delta_conv_w', 'delta_conv_b', 'delta_w_down', 'delta_post_ffn_g', 'new_m_pre_mix_g', 'new_m_w_in', 'new_m_q_norm_g', 'new_m_k_norm_g', 'new_m_hg_lb', 'new_m_hg_out_norm_g', 'new_m_w_out', 'new_m_post_mix_g', 'new_m_pre_x_g', 'new_m_mem_norm_g', 'new_m_w_xq', 'new_m_w_xkv', 'new_m_w_xo', 'new_m_post_x_g', 'new_m_pre_ffn_g', 'new_m_w_up', 'new_m_conv_w', 'new_m_conv_b', 'new_m_w_down', 'new_m_post_ffn_g', 'new_v_pre_mix_g', 'new_v_w_in', 'new_v_q_norm_g', 'new_v_k_norm_g', 'new_v_hg_lb', 'new_v_hg_out_norm_g', 'new_v_w_out', 'new_v_post_mix_g', 'new_v_pre_x_g', 'new_v_mem_norm_g', 'new_v_w_xq', 'new_v_w_xkv', 'new_v_w_xo', 'new_v_post_x_g', 'new_v_pre_ffn_g', 'new_v_w_up', 'new_v_conv_w', 'new_v_conv_b', 'new_v_w_down', 'new_v_post_ffn_g']
TWIN_LEAF_KINDS = {'loss': 'loss', 'grad_x': 'grad_x', 'grad_pre_mix_g': 'grad_w', 'grad_w_in': 'grad_w', 'grad_q_norm_g': 'grad_w', 'grad_k_norm_g': 'grad_w', 'grad_hg_lb': 'grad_w', 'grad_hg_out_norm_g': 'grad_w', 'grad_w_out': 'grad_w', 'grad_post_mix_g': 'grad_w', 'grad_pre_x_g': 'grad_w', 'grad_mem_norm_g': 'grad_w', 'grad_w_xq': 'grad_w', 'grad_w_xkv': 'grad_w', 'grad_w_xo': 'grad_w', 'grad_post_x_g': 'grad_w', 'grad_pre_ffn_g': 'grad_w', 'grad_w_up': 'grad_w', 'grad_conv_w': 'grad_w', 'grad_conv_b': 'grad_w', 'grad_w_down': 'grad_w', 'grad_post_ffn_g': 'grad_w', 'delta_pre_mix_g': 'delta_w', 'delta_w_in': 'delta_w', 'delta_q_norm_g': 'delta_w', 'delta_k_norm_g': 'delta_w', 'delta_hg_lb': 'delta_w', 'delta_hg_out_norm_g': 'delta_w', 'delta_w_out': 'delta_w', 'delta_post_mix_g': 'delta_w', 'delta_pre_x_g': 'delta_w', 'delta_mem_norm_g': 'delta_w', 'delta_w_xq': 'delta_w', 'delta_w_xkv': 'delta_w', 'delta_w_xo': 'delta_w', 'delta_post_x_g': 'delta_w', 'delta_pre_ffn_g': 'delta_w', 'delta_w_up': 'delta_w', 'delta_conv_w': 'delta_w', 'delta_conv_b': 'delta_w', 'delta_w_down': 'delta_w', 'delta_post_ffn_g': 'delta_w', 'new_m_pre_mix_g': 'new_m', 'new_m_w_in': 'new_m', 'new_m_q_norm_g': 'new_m', 'new_m_k_norm_g': 'new_m', 'new_m_hg_lb': 'new_m', 'new_m_hg_out_norm_g': 'new_m', 'new_m_w_out': 'new_m', 'new_m_post_mix_g': 'new_m', 'new_m_pre_x_g': 'new_m', 'new_m_mem_norm_g': 'new_m', 'new_m_w_xq': 'new_m', 'new_m_w_xkv': 'new_m', 'new_m_w_xo': 'new_m', 'new_m_post_x_g': 'new_m', 'new_m_pre_ffn_g': 'new_m', 'new_m_w_up': 'new_m', 'new_m_conv_w': 'new_m', 'new_m_conv_b': 'new_m', 'new_m_w_down': 'new_m', 'new_m_post_ffn_g': 'new_m', 'new_v_pre_mix_g': 'new_v', 'new_v_w_in': 'new_v', 'new_v_q_norm_g': 'new_v', 'new_v_k_norm_g': 'new_v', 'new_v_hg_lb': 'new_v', 'new_v_hg_out_norm_g': 'new_v', 'new_v_w_out': 'new_v', 'new_v_post_mix_g': 'new_v', 'new_v_pre_x_g': 'new_v', 'new_v_mem_norm_g': 'new_v', 'new_v_w_xq': 'new_v', 'new_v_w_xkv': 'new_v', 'new_v_w_xo': 'new_v', 'new_v_post_x_g': 'new_v', 'new_v_pre_ffn_g': 'new_v', 'new_v_w_up': 'new_v', 'new_v_conv_w': 'new_v', 'new_v_conv_b': 'new_v', 'new_v_w_down': 'new_v', 'new_v_post_ffn_g': 'new_v'}


def _forward(args):
    return _fwd_reference(*[args[k] for k in FWD_PARAMS])


def _output_shape():
    out = _jax.eval_shape(lambda: _forward(_fwd_setup_inputs(0)))
    return out.shape, out.dtype

N_MICROBATCH = 1
ADAM_LR = 0.001
ADAM_B1 = 0.9
ADAM_B2 = 0.999
ADAM_EPS = 1e-08
ADAM_WD = 0.01
ADAM_STEP = 10
PER_EXAMPLE_BATCH_AXIS = {'x': 0, 'mem': 0, 'loss_target': 0}
SHARED_INPUTS = []
_WEIGHT_DTYPES = {'pre_mix_g': _jnp.float32, 'w_in': _jnp.float32, 'q_norm_g': _jnp.float32, 'k_norm_g': _jnp.float32, 'hg_lb': _jnp.float32, 'hg_out_norm_g': _jnp.float32, 'w_out': _jnp.float32, 'post_mix_g': _jnp.float32, 'pre_x_g': _jnp.float32, 'mem_norm_g': _jnp.float32, 'w_xq': _jnp.float32, 'w_xkv': _jnp.float32, 'w_xo': _jnp.float32, 'post_x_g': _jnp.float32, 'pre_ffn_g': _jnp.float32, 'w_up': _jnp.float32, 'conv_w': _jnp.float32, 'conv_b': _jnp.float32, 'w_down': _jnp.float32, 'post_ffn_g': _jnp.float32}
MOMENT_SCALE = {'pre_mix_g': 6.480095e-01, 'w_in': 3.466369e-01, 'q_norm_g': 2.548348e-01, 'k_norm_g': 2.628589e-01, 'hg_lb': 2.849345e-02, 'hg_out_norm_g': 1.687252e+00, 'w_out': 5.133002e-01, 'post_mix_g': 1.600671e+01, 'pre_x_g': 4.061215e-01, 'mem_norm_g': 1.202978e+00, 'w_xq': 4.195691e-01, 'w_xkv': 8.163475e-01, 'w_xo': 1.126096e+00, 'post_x_g': 1.647897e+01, 'pre_ffn_g': 8.014605e-01, 'w_up': 3.561799e-01, 'conv_w': 4.161598e-01, 'conv_b': 1.071061e+00, 'w_down': 6.931750e-01, 'post_ffn_g': 1.602190e+01}


def _to_microbatches(a, axis):
    t = _jnp.moveaxis(a, axis, 0)
    t = t.reshape((N_MICROBATCH, t.shape[0] // N_MICROBATCH) + t.shape[1:])
    return _jnp.moveaxis(t, 1, axis + 1)


def setup_inputs(seed: int = 0) -> dict:
    inp = _fwd_setup_inputs(seed)
    key = _jax.random.fold_in(_jax.random.key(seed), 7919)
    shape, _ = _output_shape()
    out = dict(inp)
    out["loss_target"] = _jax.random.normal(_jax.random.fold_in(key, 0), shape, _jnp.float32)
    for i, name in enumerate(TWIN_WEIGHTS):
        w = inp[name].astype(_jnp.float32)
        if MOMENT_SCALE is None:
            s = _jnp.sqrt(_jnp.mean(_jnp.square(w)) + 1e-30)
        else:
            s = MOMENT_SCALE[name]
        km, kv = _jax.random.split(_jax.random.fold_in(key, i + 1))
        out[name] = w
        out["m_" + name] = s * _jax.random.normal(km, w.shape, _jnp.float32)
        out["v_" + name] = (s * s) * _jax.random.uniform(kv, w.shape, _jnp.float32, 0.5, 1.5)
    if N_MICROBATCH > 1:
        for name, axis in PER_EXAMPLE_BATCH_AXIS.items():
            out[name] = _to_microbatches(out[name], axis)
    return {'x': out['x'], 'mem': out['mem'], 'pre_mix_g': out['pre_mix_g'], 'w_in': out['w_in'], 'q_norm_g': out['q_norm_g'], 'k_norm_g': out['k_norm_g'], 'hg_lb': out['hg_lb'], 'hg_out_norm_g': out['hg_out_norm_g'], 'w_out': out['w_out'], 'post_mix_g': out['post_mix_g'], 'pre_x_g': out['pre_x_g'], 'mem_norm_g': out['mem_norm_g'], 'w_xq': out['w_xq'], 'w_xkv': out['w_xkv'], 'w_xo': out['w_xo'], 'post_x_g': out['post_x_g'], 'pre_ffn_g': out['pre_ffn_g'], 'w_up': out['w_up'], 'conv_w': out['conv_w'], 'conv_b': out['conv_b'], 'w_down': out['w_down'], 'post_ffn_g': out['post_ffn_g'], 'loss_target': out['loss_target'], 'm_pre_mix_g': out['m_pre_mix_g'], 'm_w_in': out['m_w_in'], 'm_q_norm_g': out['m_q_norm_g'], 'm_k_norm_g': out['m_k_norm_g'], 'm_hg_lb': out['m_hg_lb'], 'm_hg_out_norm_g': out['m_hg_out_norm_g'], 'm_w_out': out['m_w_out'], 'm_post_mix_g': out['m_post_mix_g'], 'm_pre_x_g': out['m_pre_x_g'], 'm_mem_norm_g': out['m_mem_norm_g'], 'm_w_xq': out['m_w_xq'], 'm_w_xkv': out['m_w_xkv'], 'm_w_xo': out['m_w_xo'], 'm_post_x_g': out['m_post_x_g'], 'm_pre_ffn_g': out['m_pre_ffn_g'], 'm_w_up': out['m_w_up'], 'm_conv_w': out['m_conv_w'], 'm_conv_b': out['m_conv_b'], 'm_w_down': out['m_w_down'], 'm_post_ffn_g': out['m_post_ffn_g'], 'v_pre_mix_g': out['v_pre_mix_g'], 'v_w_in': out['v_w_in'], 'v_q_norm_g': out['v_q_norm_g'], 'v_k_norm_g': out['v_k_norm_g'], 'v_hg_lb': out['v_hg_lb'], 'v_hg_out_norm_g': out['v_hg_out_norm_g'], 'v_w_out': out['v_w_out'], 'v_post_mix_g': out['v_post_mix_g'], 'v_pre_x_g': out['v_pre_x_g'], 'v_mem_norm_g': out['v_mem_norm_g'], 'v_w_xq': out['v_w_xq'], 'v_w_xkv': out['v_w_xkv'], 'v_w_xo': out['v_w_xo'], 'v_post_x_g': out['v_post_x_g'], 'v_pre_ffn_g': out['v_pre_ffn_g'], 'v_w_up': out['v_w_up'], 'v_conv_w': out['v_conv_w'], 'v_conv_b': out['v_conv_b'], 'v_w_down': out['v_w_down'], 'v_post_ffn_g': out['v_post_ffn_g']}


def _loss(weights, diff, rest, loss_target):
    with _jax.named_scope("forward"):
        args = {**rest, TWIN_DIFF_INPUT: diff, **{k: w.astype(_WEIGHT_DTYPES[k]) for k, w in weights.items()}}
        y = _forward(args)
    with _jax.named_scope("loss_head"):
        err = _jnp.square(y.astype(_jnp.float32) - loss_target)
        return 0.5 * _jnp.sum(_jnp.mean(err, axis=-1)) if err.ndim else 0.5 * err


def _adamw(w, g, m, v):
    m = ADAM_B1 * m + (1.0 - ADAM_B1) * g
    v = ADAM_B2 * v + (1.0 - ADAM_B2) * _jnp.square(g)
    m_hat = m / (1.0 - ADAM_B1 ** ADAM_STEP)
    v_hat = v / (1.0 - ADAM_B2 ** ADAM_STEP)
    delta = -ADAM_LR * (m_hat / (_jnp.sqrt(v_hat) + ADAM_EPS) + ADAM_WD * w)
    return delta, m, v


def reference(x, mem, pre_mix_g, w_in, q_norm_g, k_norm_g, hg_lb, hg_out_norm_g, w_out, post_mix_g, pre_x_g, mem_norm_g, w_xq, w_xkv, w_xo, post_x_g, pre_ffn_g, w_up, conv_w, conv_b, w_down, post_ffn_g, loss_target, m_pre_mix_g, m_w_in, m_q_norm_g, m_k_norm_g, m_hg_lb, m_hg_out_norm_g, m_w_out, m_post_mix_g, m_pre_x_g, m_mem_norm_g, m_w_xq, m_w_xkv, m_w_xo, m_post_x_g, m_pre_ffn_g, m_w_up, m_conv_w, m_conv_b, m_w_down, m_post_ffn_g, v_pre_mix_g, v_w_in, v_q_norm_g, v_k_norm_g, v_hg_lb, v_hg_out_norm_g, v_w_out, v_post_mix_g, v_pre_x_g, v_mem_norm_g, v_w_xq, v_w_xkv, v_w_xo, v_post_x_g, v_pre_ffn_g, v_w_up, v_conv_w, v_conv_b, v_w_down, v_post_ffn_g):
    given = dict(x=x, mem=mem, pre_mix_g=pre_mix_g, w_in=w_in, q_norm_g=q_norm_g, k_norm_g=k_norm_g, hg_lb=hg_lb, hg_out_norm_g=hg_out_norm_g, w_out=w_out, post_mix_g=post_mix_g, pre_x_g=pre_x_g, mem_norm_g=mem_norm_g, w_xq=w_xq, w_xkv=w_xkv, w_xo=w_xo, post_x_g=post_x_g, pre_ffn_g=pre_ffn_g, w_up=w_up, conv_w=conv_w, conv_b=conv_b, w_down=w_down, post_ffn_g=post_ffn_g, loss_target=loss_target, m_pre_mix_g=m_pre_mix_g, m_w_in=m_w_in, m_q_norm_g=m_q_norm_g, m_k_norm_g=m_k_norm_g, m_hg_lb=m_hg_lb, m_hg_out_norm_g=m_hg_out_norm_g, m_w_out=m_w_out, m_post_mix_g=m_post_mix_g, m_pre_x_g=m_pre_x_g, m_mem_norm_g=m_mem_norm_g, m_w_xq=m_w_xq, m_w_xkv=m_w_xkv, m_w_xo=m_w_xo, m_post_x_g=m_post_x_g, m_pre_ffn_g=m_pre_ffn_g, m_w_up=m_w_up, m_conv_w=m_conv_w, m_conv_b=m_conv_b, m_w_down=m_w_down, m_post_ffn_g=m_post_ffn_g, v_pre_mix_g=v_pre_mix_g, v_w_in=v_w_in, v_q_norm_g=v_q_norm_g, v_k_norm_g=v_k_norm_g, v_hg_lb=v_hg_lb, v_hg_out_norm_g=v_hg_out_norm_g, v_w_out=v_w_out, v_post_mix_g=v_post_mix_g, v_pre_x_g=v_pre_x_g, v_mem_norm_g=v_mem_norm_g, v_w_xq=v_w_xq, v_w_xkv=v_w_xkv, v_w_xo=v_w_xo, v_post_x_g=v_post_x_g, v_pre_ffn_g=v_pre_ffn_g, v_w_up=v_w_up, v_conv_w=v_conv_w, v_conv_b=v_conv_b, v_w_down=v_w_down, v_post_ffn_g=v_post_ffn_g)
    weights = {n: given[n] for n in TWIN_WEIGHTS}
    shared = {n: given[n] for n in SHARED_INPUTS}
    per_example = {n: given[n] for n in ['x', 'mem']}
    grad_fn = _jax.value_and_grad(_loss, argnums=(0, 1))

    def one_microbatch(ex, loss_target):
        ex = dict(ex)
        diff = ex.pop(TWIN_DIFF_INPUT)
        return grad_fn(weights, diff, {**shared, **ex}, loss_target)

    if N_MICROBATCH == 1:
        loss, (grad_w, grad_x) = one_microbatch(per_example, given["loss_target"])
    else:
        def body(carry, xs):
            loss_sum, grad_sum = carry
            l_k, (gw_k, gx_k) = one_microbatch(xs[0], xs[1])
            with _jax.named_scope("update"):
                return (loss_sum + l_k, _jax.tree.map(_jnp.add, grad_sum, gw_k)), gx_k

        init = (_jnp.zeros((), _jnp.float32), _jax.tree.map(_jnp.zeros_like, weights))
        (loss, grad_w), grad_x = _jax.lax.scan(body, init, (per_example, given["loss_target"]))
    with _jax.named_scope("update"):
        delta_w, new_m, new_v = {}, {}, {}
        for n in TWIN_WEIGHTS:
            delta_w[n], new_m[n], new_v[n] = _adamw(weights[n], grad_w[n], given["m_" + n], given["v_" + n])
    return (loss, grad_x, *[grad_w[n] for n in TWIN_WEIGHTS], *[delta_w[n] for n in TWIN_WEIGHTS],
            *[new_m[n] for n in TWIN_WEIGHTS], *[new_v[n] for n in TWIN_WEIGHTS])
```

```python
import functools

import numpy as np
import jax
import jax.numpy as jnp
from jax import lax
from jax.experimental import pallas as pl
from jax.experimental.pallas import tpu as pltpu

F32 = jnp.float32
MXU_DTYPE = jnp.bfloat16
WIRE_DTYPE = jnp.bfloat16
VMEM_LIMIT_BYTES = 56 * 1024 * 1024
EPS = 1e-6
MESH = pl.DeviceIdType.MESH

D_MODEL = 1024
GRID_W = 64
ATT_HEADS, ATT_KV_HEADS, ATT_HEAD_DIM = 8, 2, 64
ATT_GROUP = ATT_HEADS // ATT_KV_HEADS
ATT_Q_DIM, ATT_KV_DIM = 512, 128
ROPE_THETA = 10000.0
HG_HEADS, HG_HEAD_DIM, HG_DIM = 4, 128, 512
HG_CHUNK = 128
HG_LEVELS = 7
X_HEADS, X_HEAD_DIM = 4, 256
D_FF = 2816
FF_COLS = 256
N_IN = 3328
OFF_AQ, OFF_AK, OFF_AV, OFF_HQ, OFF_ZF, OFF_ZB, OFF_HI, OFF_HG = 0, 512, 640, 768, 1280, 1792, 2304, 2816

ADAM_LR, ADAM_B1, ADAM_B2, ADAM_EPS, ADAM_WD, ADAM_STEP = 0.001, 0.9, 0.999, 1e-08, 0.01, 10

SDS = jax.ShapeDtypeStruct


def _cp(*sem):
    return pltpu.CompilerParams(dimension_semantics=sem, vmem_limit_bytes=VMEM_LIMIT_BYTES)


def _dot(a, b, form="nn"):
    dims = {"nn": (((1,), (0,)), ((), ())), "nt": (((1,), (1,)), ((), ())), "tn": (((0,), (0,)), ((), ()))}[form]
    return lax.dot_general(a.astype(MXU_DTYPE), b.astype(MXU_DTYPE), dims, preferred_element_type=F32)


def _sigmoid(x):
    return 1.0 / (1.0 + jnp.exp(-x))


def _rstd(x):
    return lax.rsqrt(jnp.mean(x * x, axis=-1, keepdims=True) + EPS)


def _rms_bwd(x, g, dy):
    r = _rstd(x)
    xh = x * r
    dn = dy * g
    dx = r * (dn - xh * jnp.mean(dn * xh, axis=-1, keepdims=True))
    return dx, jnp.sum(dy * xh, axis=0, keepdims=True)


def _mm(a, b, form, out_dtype, tm, tn, name):
    if form == "nn":
        (m, k), n = a.shape, b.shape[1]
    elif form == "nt":
        (m, k), n = a.shape, b.shape[0]
    else:
        (k, m), n = a.shape, b.shape[1]
    tm, tn = min(tm, m), min(tn, n)
    assert m % tm == 0 and n % tn == 0, (name, m, n, tm, tn)

    def body(a_ref, b_ref, o_ref):
        o_ref[...] = _dot(a_ref[...], b_ref[...], form).astype(o_ref.dtype)

    a_spec = pl.BlockSpec((k, tm), lambda i, j: (0, i)) if form == "tn" else pl.BlockSpec((tm, k), lambda i, j: (i, 0))
    b_spec = pl.BlockSpec((tn, k), lambda i, j: (j, 0)) if form == "nt" else pl.BlockSpec((k, tn), lambda i, j: (0, j))
    return pl.pallas_call(
        body, name=name, grid=(m // tm, n // tn), in_specs=[a_spec, b_spec],
        out_specs=pl.BlockSpec((tm, tn), lambda i, j: (i, j)), out_shape=SDS((m, n), out_dtype),
        compiler_params=_cp("parallel", "parallel"))(a, b)


def _norm_mm(x, g, w, out_dtype, tm, tn, name):
    m, d = x.shape
    n = w.shape[1]
    tm, tn = min(tm, m), min(tn, n)
    assert m % tm == 0 and n % tn == 0, (name, m, n, tm, tn)

    def body(x_ref, g_ref, w_ref, o_ref, h_ref, hs):
        @pl.when(pl.program_id(1) == 0)
        def _():
            xv = x_ref[...]
            h = (xv * _rstd(xv) * g_ref[...]).astype(MXU_DTYPE)
            hs[...] = h
            h_ref[...] = h

        o_ref[...] = _dot(hs[...], w_ref[...]).astype(o_ref.dtype)

    return pl.pallas_call(
        body, name=name, grid=(m // tm, n // tn),
        in_specs=[pl.BlockSpec((tm, d), lambda i, j: (i, 0)), pl.BlockSpec((1, d), lambda i, j: (0, 0)),
                  pl.BlockSpec((d, tn), lambda i, j: (0, j))],
        out_specs=[pl.BlockSpec((tm, tn), lambda i, j: (i, j)), pl.BlockSpec((tm, d), lambda i, j: (i, 0))],
        out_shape=[SDS((m, n), out_dtype), SDS((m, d), MXU_DTYPE)],
        scratch_shapes=[pltpu.VMEM((tm, d), MXU_DTYPE)],
        compiler_params=_cp("parallel", "arbitrary"))(x, g, w)


ROW_TILE = 256


def _resid_norm(x, y, g, name):
    n, d = x.shape
    tr = min(ROW_TILE, n)

    def body(x_ref, y_ref, g_ref, o_ref):
        yv = y_ref[...]
        o_ref[...] = x_ref[...] + yv * _rstd(yv) * g_ref[...]

    row = pl.BlockSpec((tr, d), lambda i: (i, 0))
    return pl.pallas_call(
        body, name=name, grid=(n // tr,), in_specs=[row, row, pl.BlockSpec((1, d), lambda i: (0, 0))],
        out_specs=row, out_shape=SDS((n, d), F32), compiler_params=_cp("parallel"))(x, y, g)


def _norm_bwd(x, g, dy, res, out_dtype, name):
    n, d = x.shape
    tr = min(ROW_TILE, n)
    has_res = res is not None

    def body(*refs):
        x_ref, g_ref, dy_ref = refs[:3]
        dx_ref, dg_ref = refs[-2:]
        dx, dg = _rms_bwd(x_ref[...], g_ref[...], dy_ref[...].astype(F32))
        if has_res:
            dx = dx + refs[3][...]
        dx_ref[...] = dx.astype(dx_ref.dtype)

        @pl.when(pl.program_id(0) == 0)
        def _():
            dg_ref[...] = jnp.zeros_like(dg_ref)

        dg_ref[...] += dg

    row = pl.BlockSpec((tr, d), lambda i: (i, 0))
    vec = pl.BlockSpec((1, d), lambda i: (0, 0))
    ins = [x, g, dy] + ([res] if has_res else [])
    return pl.pallas_call(
        body, name=name, grid=(n // tr,), in_specs=[row, vec, row] + ([row] if has_res else []),
        out_specs=[row, vec], out_shape=[SDS((n, d), out_dtype), SDS((1, d), F32)],
        compiler_params=_cp("arbitrary"))(*ins)


def _final_loss(x, y, g, target, name):
    n, d = x.shape
    tr = min(ROW_TILE, n)

    def body(x_ref, y_ref, g_ref, t_ref, d_ref, l_ref):
        yv = y_ref[...]
        diff = x_ref[...] + yv * _rstd(yv) * g_ref[...] - t_ref[...]
        d_ref[...] = diff * (1.0 / d)

        @pl.when(pl.program_id(0) == 0)
        def _():
            l_ref[...] = jnp.zeros_like(l_ref)

        l_ref[...] += 0.5 * jnp.sum(jnp.mean(diff * diff, axis=-1, keepdims=True), axis=0, keepdims=True)

    row = pl.BlockSpec((tr, d), lambda i: (i, 0))
    return pl.pallas_call(
        body, name=name, grid=(n // tr,), in_specs=[row, row, pl.BlockSpec((1, d), lambda i: (0, 0)), row],
        out_specs=[row, pl.BlockSpec((1, 1), lambda i: (0, 0))], out_shape=[SDS((n, d), F32), SDS((1, 1), F32)],
        compiler_params=_cp("arbitrary"))(x, y, g, target)


def _rope_tables(n):
    pairs = ATT_HEAD_DIM // 4
    t = np.arange(n)
    inv = np.power(ROPE_THETA, -np.arange(pairs, dtype=np.float32) / pairs).astype(np.float32)
    ang = np.concatenate([(t // GRID_W)[:, None].astype(np.float32) * inv, (t % GRID_W)[:, None].astype(np.float32) * inv], axis=-1)
    cos = np.repeat(np.cos(ang), 2, axis=-1)
    sin = np.repeat(np.sin(ang), 2, axis=-1) * np.tile(np.array([-1.0, 1.0], np.float32), ATT_HEAD_DIM // 2)
    return jnp.asarray(np.tile(cos, 2), F32), jnp.asarray(np.tile(sin, 2), F32)


def _swap_pairs(x):
    lane = lax.broadcasted_iota(jnp.int32, x.shape, 1)
    return jnp.where((lane & 1) == 0, pltpu.roll(x, 127, axis=1), pltpu.roll(x, 1, axis=1))


def _head_mean(v):
    lane = lax.broadcasted_iota(jnp.int32, v.shape, 1)
    lo = jnp.where(lane < ATT_HEAD_DIM, v, 0.0)
    s0 = jnp.sum(lo, axis=-1, keepdims=True)
    s1 = jnp.sum(v - lo, axis=-1, keepdims=True)
    return jnp.where(lane < ATT_HEAD_DIM, s0, s1) * (1.0 / ATT_HEAD_DIM)


def _qk_prep(p, gq, gk, cos, sin, name):
    n = p.shape[0]
    tr = min(ROW_TILE, n)

    def one(xv, g, c, s):
        xn = xv * lax.rsqrt(_head_mean(xv * xv) + EPS) * g
        return xn * c + _swap_pairs(xn) * s

    def body(q_ref, k_ref, gq_ref, gk_ref, c_ref, s_ref, qo_ref, ko_ref):
        c, s = c_ref[...], s_ref[...]
        for j in range(ATT_Q_DIM // 128):
            qo_ref[:, j * 128:(j + 1) * 128] = one(q_ref[:, j * 128:(j + 1) * 128], gq_ref[...], c, s).astype(qo_ref.dtype)
        ko_ref[...] = one(k_ref[...], gk_ref[...], c, s).astype(ko_ref.dtype)

    vec = pl.BlockSpec((1, 128), lambda i: (0, 0))
    tab = pl.BlockSpec((tr, 128), lambda i: (i, 0))
    return pl.pallas_call(
        body, name=name, grid=(n // tr,),
        in_specs=[pl.BlockSpec((tr, ATT_Q_DIM), lambda i: (i, 0)), pl.BlockSpec((tr, 128), lambda i: (i, OFF_AK // 128)), vec, vec, tab, tab],
        out_specs=[pl.BlockSpec((tr, ATT_Q_DIM), lambda i: (i, 0)), tab],
        out_shape=[SDS((n, ATT_Q_DIM), MXU_DTYPE), SDS((n, ATT_KV_DIM), MXU_DTYPE)],
        compiler_params=_cp("parallel"))(p, p, gq, gk, cos, sin)


def _qk_prep_bwd(p, gq, gk, cos, sin, dq, dk, name):
    n = p.shape[0]
    tr = min(ROW_TILE, n)

    def one(xv, g, c, s, dout):
        dxn = dout * c + _swap_pairs(dout * s)
        r = lax.rsqrt(_head_mean(xv * xv) + EPS)
        xh = xv * r
        dn = dxn * g
        dx = r * (dn - xh * _head_mean(dn * xh))
        return dx, jnp.sum(dxn * xh, axis=0, keepdims=True)

    def body(q_ref, k_ref, gq_ref, gk_ref, c_ref, s_ref, dq_ref, dk_ref, dqo_ref, dko_ref, dgq_ref, dgk_ref):
        @pl.when(pl.program_id(0) == 0)
        def _():
            dgq_ref[...] = jnp.zeros_like(dgq_ref)
            dgk_ref[...] = jnp.zeros_like(dgk_ref)

        c, s = c_ref[...], s_ref[...]
        for j in range(ATT_Q_DIM // 128):
            sl = slice(j * 128, (j + 1) * 128)
            dx, dg = one(q_ref[:, sl], gq_ref[...], c, s, dq_ref[:, sl])
            dqo_ref[:, sl] = dx.astype(dqo_ref.dtype)
            dgq_ref[:, sl] += dg
        dx, dg = one(k_ref[...], gk_ref[...], c, s, dk_ref[...])
        dko_ref[...] = dx.astype(dko_ref.dtype)
        dgk_ref[...] += dg

    vec = pl.BlockSpec((1, 128), lambda i: (0, 0))
    tab = pl.BlockSpec((tr, 128), lambda i: (i, 0))
    qrow = pl.BlockSpec((tr, ATT_Q_DIM), lambda i: (i, 0))
    return pl.pallas_call(
        body, name=name, grid=(n // tr,),
        in_specs=[qrow, pl.BlockSpec((tr, 128), lambda i: (i, OFF_AK // 128)), vec, vec, tab, tab, qrow, tab],
        out_specs=[qrow, tab, pl.BlockSpec((1, ATT_Q_DIM), lambda i: (0, 0)), vec],
        out_shape=[SDS((n, ATT_Q_DIM), MXU_DTYPE), SDS((n, ATT_KV_DIM), MXU_DTYPE), SDS((1, ATT_Q_DIM), F32), SDS((1, 128), F32)],
        compiler_params=_cp("arbitrary"))(p, p, gq, gk, cos, sin, dq, dk)


ATT_TQ = 256


def _attn_fwd(q, k, v, name):
    n = q.shape[0]
    tq = min(ATT_TQ, n)
    scale = ATT_HEAD_DIM ** -0.5
    gw = ATT_GROUP * ATT_HEAD_DIM

    def body(q_ref, k_ref, v_ref, o_ref):
        kk, vv = k_ref[0], v_ref[0]
        outs = []
        for g in range(ATT_GROUP):
            s = _dot(q_ref[:, g * ATT_HEAD_DIM:(g + 1) * ATT_HEAD_DIM], kk, "nt") * scale
            e = jnp.exp(s - jnp.max(s, axis=-1, keepdims=True))
            p = e / jnp.sum(e, axis=-1, keepdims=True)
            outs.append(_dot(p, vv))
        o_ref[...] = jnp.concatenate(outs, axis=-1).astype(o_ref.dtype)

    kv = pl.BlockSpec((1, n, ATT_HEAD_DIM), lambda h, i: (h, 0, 0))
    return pl.pallas_call(
        body, name=name, grid=(ATT_KV_HEADS, n // tq),
        in_specs=[pl.BlockSpec((tq, gw), lambda h, i: (i, h)), kv, kv],
        out_specs=pl.BlockSpec((tq, gw), lambda h, i: (i, h)), out_shape=SDS((n, ATT_Q_DIM), MXU_DTYPE),
        compiler_params=_cp("parallel", "parallel"))(q, k, v)


def _attn_bwd(q, k, v, do, name):
    n = q.shape[0]
    tq = min(ATT_TQ, n)
    scale = ATT_HEAD_DIM ** -0.5
    gw = ATT_GROUP * ATT_HEAD_DIM

    def body(q_ref, k_ref, v_ref, do_ref, dq_ref, dk_ref, dv_ref):
        @pl.when(pl.program_id(1) == 0)
        def _():
            dk_ref[...] = jnp.zeros_like(dk_ref)
            dv_ref[...] = jnp.zeros_like(dv_ref)

        kk, vv = k_ref[0], v_ref[0]
        dqs = []
        dk_acc = jnp.zeros((n, ATT_HEAD_DIM), F32)
        dv_acc = jnp.zeros((n, ATT_HEAD_DIM), F32)
        for g in range(ATT_GROUP):
            sl = slice(g * ATT_HEAD_DIM, (g + 1) * ATT_HEAD_DIM)
            qg, dog = q_ref[:, sl], do_ref[:, sl]
            s = _dot(qg, kk, "nt") * scale
            e = jnp.exp(s - jnp.max(s, axis=-1, keepdims=True))
            p = e / jnp.sum(e, axis=-1, keepdims=True)
            dp = _dot(dog, vv, "nt")
            ds = p * (dp - jnp.sum(p * dp, axis=-1, keepdims=True)) * scale
            dqs.append(_dot(ds, kk))
            dk_acc += _dot(ds, qg, "tn")
            dv_acc += _dot(p, dog, "tn")
        dq_ref[...] = jnp.concatenate(dqs, axis=-1)
        dk_ref[0] += dk_acc
        dv_ref[0] += dv_acc

    kv = pl.BlockSpec((1, n, ATT_HEAD_DIM), lambda h, i: (h, 0, 0))
    qb = pl.BlockSpec((tq, gw), lambda h, i: (i, h))
    return pl.pallas_call(
        body, name=name, grid=(ATT_KV_HEADS, n // tq), in_specs=[qb, kv, kv, qb], out_specs=[qb, kv, kv],
        out_shape=[SDS((n, ATT_Q_DIM), F32), SDS((ATT_KV_HEADS, n, ATT_HEAD_DIM), F32), SDS((ATT_KV_HEADS, n, ATT_HEAD_DIM), F32)],
        compiler_params=_cp("parallel", "arbitrary"))(q, k, v, do)


def _hg_segments():
    c = HG_CHUNK
    t = np.arange(c)[:, None]
    r = np.arange(c)[None, :]
    mats = [(r <= t)]
    for lev in range(HG_LEVELS):
        h = c >> (lev + 1)
        mid = (t // (2 * h)) * (2 * h) + h - 1
        hi = (t // h) % 2 == 1
        mats.append(np.where(hi, (r > mid) & (r <= t), (r > t) & (r <= mid)))
    mats.append(r > t)
    return jnp.asarray(np.concatenate(mats, axis=0).astype(np.float32), MXU_DTYPE)


def _hg_pair_sums():
    c = HG_CHUNK
    r = np.arange(c)[:, None]
    t = np.arange(c)[None, :]
    gp, gn = [t >= r], [t < r]
    for lev in range(HG_LEVELS):
        sh = HG_LEVELS - 1 - lev
        same = (r >> sh) == (t >> sh)
        gp.append(same & (t >= r))
        gn.append(same & (t < r))
    as_op = lambda ms: jnp.asarray(np.concatenate(ms, axis=1).astype(np.float32), MXU_DTYPE)
    return as_op(gp), as_op(gn)


def _split_dot(mat, x):
    hi = x.astype(MXU_DTYPE)
    lo = (x - hi.astype(F32)).astype(MXU_DTYPE)
    return _dot(mat, hi) + _dot(mat, lo)


def _hg_gates(hq, z, a0, a1):
    q = hq * _sigmoid(hq)
    sg = _sigmoid(z)
    lb = _sigmoid(a0 - a1)
    f = lb + (1.0 - lb) * sg
    k = (1.0 - lb) * (1.0 - sg)
    return q, f, k, sg, lb


def _hg_level_masks():
    c = HG_CHUNK
    row = lax.broadcasted_iota(jnp.int32, (c, 1), 0)
    rr = lax.broadcasted_iota(jnp.int32, (c, c), 0)
    cc = lax.broadcasted_iota(jnp.int32, (c, c), 1)
    his, sames = [], []
    for lev in range(HG_LEVELS):
        sh = HG_LEVELS - 1 - lev
        his.append(((row >> sh) & 1) == 1)
        sames.append((rr >> (sh + 1)) == (cc >> (sh + 1)))
    return his, sames, rr == cc


def _hg_intra(q, k, ex, masks):
    his, sames, eye = masks
    a = jnp.where(eye, jnp.sum(q * k, axis=-1, keepdims=True), 0.0)
    for lev in range(HG_LEVELS):
        e = ex[lev + 1]
        qs = jnp.where(his[lev], q * e, 0.0)
        ks = jnp.where(his[lev], 0.0, k * e)
        a = a + jnp.where(sames[lev], _dot(qs, ks, "nt"), 0.0)
    return a


def _hg_specs(n):
    c = HG_CHUNK
    nc = n // c
    blk = lambda rev: pl.BlockSpec((1, c, HG_HEAD_DIM), (lambda d, h, i: (d, nc - 1 - i, h)) if rev else (lambda d, h, i: (d, i, h)))
    vec = pl.BlockSpec((1, 1, HG_HEAD_DIM), lambda d, h, i: (d, 0, h))
    seg = pl.BlockSpec(((HG_LEVELS + 2) * c, c), lambda d, h, i: (0, 0))
    sums = pl.BlockSpec((c, (HG_LEVELS + 1) * c), lambda d, h, i: (0, 0))
    return nc, blk, vec, seg, sums


def _hg_exps(seg_ref, f):
    lf = jnp.log(f)
    args = _split_dot(seg_ref[...], lf)
    c = HG_CHUNK
    return [jnp.exp(args[j * c:(j + 1) * c]) for j in range(HG_LEVELS + 2)]


def _hgrn_fwd(hq, z, v, a0, a1, seg, name):
    n = hq.shape[1]
    nc, blk, vec, segspec, _ = _hg_specs(n)

    def body(hq_ref, z_ref, v_ref, a0_ref, a1_ref, seg_ref, o_ref, st):
        @pl.when(pl.program_id(2) == 0)
        def _():
            st[...] = jnp.zeros_like(st)

        q, f, k, _, _ = _hg_gates(hq_ref[0], z_ref[0], a0_ref[0], a1_ref[0])
        vv = v_ref[0]
        ex = _hg_exps(seg_ref, f)
        a = _hg_intra(q, k, ex, _hg_level_masks())
        s_t = st[...]
        o_ref[0] = _dot(a, vv) + _dot(q * ex[0], s_t, "nt")
        st[...] = s_t * ex[0][HG_CHUNK - 1:HG_CHUNK, :] + _dot(vv, k * ex[HG_LEVELS + 1], "tn")

    return pl.pallas_call(
        body, name=name, grid=(2, HG_HEADS, nc), in_specs=[blk(False)] * 3 + [vec, vec, segspec],
        out_specs=blk(False), out_shape=SDS(hq.shape, F32),
        scratch_shapes=[pltpu.VMEM((HG_HEAD_DIM, HG_HEAD_DIM), F32)],
        compiler_params=_cp("parallel", "parallel", "arbitrary"))(hq, z, v, a0, a1, seg)


def _hgrn_bwd_q(hq, z, v, a0, a1, seg, gp, do, name):
    n = hq.shape[1]
    nc, blk, vec, segspec, sumspec = _hg_specs(n)


    def body(hq_ref, z_ref, v_ref, a0_ref, a1_ref, seg_ref, gp_ref, do_ref, dhq_ref, dlf_ref, s0_ref, st):
        @pl.when(pl.program_id(2) == 0)
        def _():
            st[...] = jnp.zeros_like(st)

        s_t = st[...]
        s0_ref[0, 0, 0] = s_t
        hqv = hq_ref[0]
        q, f, k, _, _ = _hg_gates(hqv, z_ref[0], a0_ref[0], a1_ref[0])
        vv, dov = v_ref[0], do_ref[0]
        ex = _hg_exps(seg_ref, f)
        his, sames, eye = _hg_level_masks()
        da = _dot(dov, vv, "nt")
        dq_inter = ex[0] * _dot(dov, s_t)
        dq = jnp.sum(dov * vv, axis=-1, keepdims=True) * k + dq_inter
        terms = [q * dq_inter]
        for lev in range(HG_LEVELS):
            e = ex[lev + 1]
            ks = jnp.where(his[lev], 0.0, k * e)
            part = jnp.where(his[lev], e, 0.0) * _dot(jnp.where(sames[lev], da, 0.0), ks)
            dq = dq + part
            terms.append(q * part)
        dlf_ref[0] = _dot(gp_ref[...], jnp.concatenate(terms, axis=0))
        sq = _sigmoid(hqv)
        dhq_ref[0] = dq * sq * (1.0 + hqv * (1.0 - sq))
        st[...] = s_t * ex[0][HG_CHUNK - 1:HG_CHUNK, :] + _dot(vv, k * ex[HG_LEVELS + 1], "tn")

    out = SDS(hq.shape, F32)
    return pl.pallas_call(
        body, name=name, grid=(2, HG_HEADS, nc), in_specs=[blk(False)] * 3 + [vec, vec, segspec, sumspec, blk(False)],
        out_specs=[blk(False), blk(False), pl.BlockSpec((1, 1, 1, HG_HEAD_DIM, HG_HEAD_DIM), lambda d, h, i: (d, h, i, 0, 0))],
        out_shape=[out, out, SDS((2, HG_HEADS, nc, HG_HEAD_DIM, HG_HEAD_DIM), F32)],
        scratch_shapes=[pltpu.VMEM((HG_HEAD_DIM, HG_HEAD_DIM), F32)],
        compiler_params=_cp("parallel", "parallel", "arbitrary"))(hq, z, v, a0, a1, seg, gp, do)


def _hgrn_bwd_kv(hq, z, v, a0, a1, seg, gn, do, dlf_q, s0, name):
    n = hq.shape[1]
    nc, blk, vec, segspec, sumspec = _hg_specs(n)
    c = HG_CHUNK

    def body(hq_ref, z_ref, v_ref, a0_ref, a1_ref, seg_ref, gn_ref, do_ref, dlfq_ref, s0_ref, dz_ref, dv_ref, dlb_ref, rt):
        @pl.when(pl.program_id(2) == 0)
        def _():
            rt[...] = jnp.zeros_like(rt)
            dlb_ref[...] = jnp.zeros_like(dlb_ref)

        q, f, k, sg, lb = _hg_gates(hq_ref[0], z_ref[0], a0_ref[0], a1_ref[0])
        vv, dov = v_ref[0], do_ref[0]
        ex = _hg_exps(seg_ref, f)
        masks = _hg_level_masks()
        his, sames, eye = masks
        a = _hg_intra(q, k, ex, masks)
        da = _dot(dov, vv, "nt")
        r_t = rt[...]
        k_end = k * ex[HG_LEVELS + 1]
        dv_ref[0] = _dot(a, dov, "tn") + _dot(k_end, r_t, "nt")
        dk_inter = ex[HG_LEVELS + 1] * _dot(vv, r_t)
        dk = jnp.sum(dov * vv, axis=-1, keepdims=True) * q + dk_inter
        terms = [k * dk_inter]
        for lev in range(HG_LEVELS):
            e = ex[lev + 1]
            qs = jnp.where(his[lev], q * e, 0.0)
            part = jnp.where(his[lev], 0.0, e) * _dot(jnp.where(sames[lev], da, 0.0), qs, "tn")
            dk = dk + part
            terms.append(k * part)
        decay = ex[0][c - 1:c, :]
        rt[...] = r_t * decay + _dot(dov, q * ex[0], "tn")
        later = decay * jnp.sum(s0_ref[0, 0, 0] * r_t, axis=0, keepdims=True)
        dlf = dlfq_ref[0] + _dot(gn_ref[...], jnp.concatenate(terms, axis=0)) + later
        df = dlf / f - dk
        dz_ref[0] = df * (1.0 - lb) * sg * (1.0 - sg)
        dlb_ref[0] += jnp.sum(df * (1.0 - sg), axis=0, keepdims=True)

    out = SDS(hq.shape, F32)
    return pl.pallas_call(
        body, name=name, grid=(2, HG_HEADS, nc),
        in_specs=[blk(True)] * 3 + [vec, vec, segspec, sumspec, blk(True), blk(True),
                                    pl.BlockSpec((1, 1, 1, HG_HEAD_DIM, HG_HEAD_DIM), lambda d, h, i: (d, h, nc - 1 - i, 0, 0))],
        out_specs=[blk(True), blk(True), vec], out_shape=[out, out, SDS((2, 1, HG_DIM), F32)],
        scratch_shapes=[pltpu.VMEM((HG_HEAD_DIM, HG_HEAD_DIM), F32)],
        compiler_params=_cp("parallel", "parallel", "arbitrary"))(hq, z, v, a0, a1, seg, gn, do, dlf_q, s0)


def _hg_post(o_f, o_b, p, g, name):
    n = o_f.shape[0]
    tr = min(ROW_TILE, n)
    w = 2 * HG_HEAD_DIM

    def body(of_ref, ob_ref, hg_ref, g_ref, o_ref):
        for j in range(2):
            sl = slice(j * HG_HEAD_DIM, (j + 1) * HG_HEAD_DIM)
            o = of_ref[:, sl] + ob_ref[:, sl]
            hg = hg_ref[:, sl]
            o_ref[:, sl] = (o * _rstd(o) * g_ref[...] * (hg * _sigmoid(hg))).astype(o_ref.dtype)

    blk = pl.BlockSpec((tr, w), lambda i, j: (i, j))
    return pl.pallas_call(
        body, name=name, grid=(n // tr, HG_DIM // w),
        in_specs=[blk, blk, pl.BlockSpec((tr, w), lambda i, j: (i, OFF_HG // w + j)), pl.BlockSpec((1, HG_HEAD_DIM), lambda i, j: (0, 0))],
        out_specs=blk, out_shape=SDS((n, HG_DIM), MXU_DTYPE), compiler_params=_cp("parallel", "parallel"))(o_f, o_b, p, g)


def _hg_post_bwd(o_f, o_b, p, g, dcat, name):
    n = o_f.shape[0]
    tr = min(ROW_TILE, n)
    w = 2 * HG_HEAD_DIM

    def body(of_ref, ob_ref, hg_ref, g_ref, d_ref, do_ref, dhg_ref, dg_ref):
        @pl.when(pl.program_id(1) == 0)
        def _():
            dg_ref[...] = jnp.zeros_like(dg_ref)

        for j in range(2):
            sl = slice(j * HG_HEAD_DIM, (j + 1) * HG_HEAD_DIM)
            o = of_ref[:, sl] + ob_ref[:, sl]
            hg = hg_ref[:, sl]
            d = d_ref[:, sl].astype(F32)
            sg = _sigmoid(hg)
            on = o * _rstd(o) * g_ref[...]
            dhg_ref[:, sl] = (d * on * sg * (1.0 + hg * (1.0 - sg))).astype(dhg_ref.dtype)
            dx, dg = _rms_bwd(o, g_ref[...], d * hg * sg)
            do_ref[:, sl] = dx
            dg_ref[0, :, sl] += dg

    blk = pl.BlockSpec((tr, w), lambda j, i: (i, j))
    return pl.pallas_call(
        body, name=name, grid=(HG_DIM // w, n // tr),
        in_specs=[blk, blk, pl.BlockSpec((tr, w), lambda j, i: (i, OFF_HG // w + j)), pl.BlockSpec((1, HG_HEAD_DIM), lambda j, i: (0, 0)),
                  pl.BlockSpec((tr, w), lambda j, i: (i, ATT_Q_DIM // w + j))],
        out_specs=[blk, blk, pl.BlockSpec((1, 1, w), lambda j, i: (j, 0, 0))],
        out_shape=[SDS((n, HG_DIM), F32), SDS((n, HG_DIM), MXU_DTYPE), SDS((HG_DIM // w, 1, w), F32)],
        compiler_params=_cp("parallel", "arbitrary"))(o_f, o_b, p, g, dcat)


XATT_TQ = 512


def _xattn_fwd(q, kv, name):
    n, nm = q.shape[0], kv.shape[0]
    tq = min(XATT_TQ, n)
    scale = X_HEAD_DIM ** -0.5

    def body(q_ref, k_ref, v_ref, o_ref):
        s = _dot(q_ref[...], k_ref[...], "nt") * scale
        e = jnp.exp(s - jnp.max(s, axis=-1, keepdims=True))
        o_ref[...] = _dot(e / jnp.sum(e, axis=-1, keepdims=True), v_ref[...]).astype(o_ref.dtype)

    qb = pl.BlockSpec((tq, X_HEAD_DIM), lambda h, i: (i, h))
    return pl.pallas_call(
        body, name=name, grid=(X_HEADS, n // tq),
        in_specs=[qb, pl.BlockSpec((nm, X_HEAD_DIM), lambda h, i: (0, h)), pl.BlockSpec((nm, X_HEAD_DIM), lambda h, i: (0, X_HEADS + h))],
        out_specs=qb, out_shape=SDS(q.shape, MXU_DTYPE), compiler_params=_cp("parallel", "parallel"))(q, kv, kv)


def _xattn_bwd(q, kv, do, name):
    n, nm = q.shape[0], kv.shape[0]
    tq = min(XATT_TQ, n)
    scale = X_HEAD_DIM ** -0.5

    def body(q_ref, k_ref, v_ref, do_ref, dq_ref, dk_ref, dv_ref):
        @pl.when(pl.program_id(1) == 0)
        def _():
            dk_ref[...] = jnp.zeros_like(dk_ref)
            dv_ref[...] = jnp.zeros_like(dv_ref)

        qv, dov = q_ref[...], do_ref[...]
        s = _dot(qv, k_ref[...], "nt") * scale
        e = jnp.exp(s - jnp.max(s, axis=-1, keepdims=True))
        p = e / jnp.sum(e, axis=-1, keepdims=True)
        dp = _dot(dov, v_ref[...], "nt")
        ds = p * (dp - jnp.sum(p * dp, axis=-1, keepdims=True)) * scale
        dq_ref[...] = _dot(ds, k_ref[...]).astype(dq_ref.dtype)
        dk_ref[...] += _dot(ds, qv, "tn")
        dv_ref[...] += _dot(p, dov, "tn")

    qb = pl.BlockSpec((tq, X_HEAD_DIM), lambda h, i: (i, h))
    kb = pl.BlockSpec((nm, X_HEAD_DIM), lambda h, i: (0, h))
    return pl.pallas_call(
        body, name=name, grid=(X_HEADS, n // tq),
        in_specs=[qb, kb, pl.BlockSpec((nm, X_HEAD_DIM), lambda h, i: (0, X_HEADS + h)), qb], out_specs=[qb, kb, kb],
        out_shape=[SDS(q.shape, MXU_DTYPE), SDS((nm, X_HEADS * X_HEAD_DIM), F32), SDS((nm, X_HEADS * X_HEAD_DIM), F32)],
        compiler_params=_cp("parallel", "arbitrary"))(q, kv, kv, do)


def _shift_rows(u, down):
    n = u.shape[0]
    row = lax.broadcasted_iota(jnp.int32, u.shape, 0)
    if down:
        return jnp.where(row == 0, 0.0, pltpu.roll(u, 1, axis=0))
    return jnp.where(row == n - 1, 0.0, pltpu.roll(u, n - 1, axis=0))


def _conv(u, w, b):
    return b + _shift_rows(u, True) * w[0:1, :] + u * w[1:2, :] + _shift_rows(u, False) * w[2:3, :]


def _conv_gate(u, cw, cb, name):
    n = u.shape[0]
    nb = D_FF // FF_COLS

    def body(ug_ref, uv_ref, wg_ref, wv_ref, bg_ref, bv_ref, o_ref):
        gate = _conv(ug_ref[...], wg_ref[...], bg_ref[...])
        val = _conv(uv_ref[...], wv_ref[...], bv_ref[...])
        o_ref[...] = (gate * _sigmoid(gate) * val).astype(o_ref.dtype)

    col = lambda rows, off: pl.BlockSpec((rows, FF_COLS), lambda j: (0, off + j))
    return pl.pallas_call(
        body, name=name, grid=(nb,), in_specs=[col(n, 0), col(n, nb), col(3, 0), col(3, nb), col(1, 0), col(1, nb)],
        out_specs=col(n, 0), out_shape=SDS((n, D_FF), MXU_DTYPE), compiler_params=_cp("parallel"))(u, u, cw, cw, cb, cb)


def _conv_gate_bwd(u, cw, cb, da, name):
    n = u.shape[0]
    nb = D_FF // FF_COLS

    def body(ug_ref, uv_ref, wg_ref, wv_ref, bg_ref, bv_ref, da_ref, du_ref, dw_ref, db_ref):
        is_gate = pl.program_id(0) < nb
        ug, uv = ug_ref[...], uv_ref[...]
        gate = _conv(ug, wg_ref[...], bg_ref[...])
        val = _conv(uv, wv_ref[...], bv_ref[...])
        sg = _sigmoid(gate)
        dav = da_ref[...].astype(F32)
        dacc = jnp.where(is_gate, dav * val * sg * (1.0 + gate * (1.0 - sg)), dav * gate * sg)
        u_own = jnp.where(is_gate, ug, uv)
        w_own = jnp.where(is_gate, wg_ref[...], wv_ref[...])
        du = _shift_rows(dacc, False) * w_own[0:1, :] + dacc * w_own[1:2, :] + _shift_rows(dacc, True) * w_own[2:3, :]
        du_ref[...] = du.astype(du_ref.dtype)
        db_ref[...] = jnp.sum(dacc, axis=0, keepdims=True)
        dw_ref[0:1, :] = jnp.sum(dacc * _shift_rows(u_own, True), axis=0, keepdims=True)
        dw_ref[1:2, :] = jnp.sum(dacc * u_own, axis=0, keepdims=True)
        dw_ref[2:3, :] = jnp.sum(dacc * _shift_rows(u_own, False), axis=0, keepdims=True)

    half = lambda rows, off: pl.BlockSpec((rows, FF_COLS), lambda j: (0, off + j % nb))
    own = lambda rows: pl.BlockSpec((rows, FF_COLS), lambda j: (0, j))
    return pl.pallas_call(
        body, name=name, grid=(2 * nb,),
        in_specs=[half(n, 0), half(n, nb), half(3, 0), half(3, nb), half(1, 0), half(1, nb), half(n, 0)],
        out_specs=[own(n), own(3), own(1)],
        out_shape=[SDS((n, 2 * D_FF), MXU_DTYPE), SDS((3, 2 * D_FF), F32), SDS((1, 2 * D_FF), F32)],
        compiler_params=_cp("parallel"))(u, u, cw, cw, cb, cb, da)


def _adamw(w, g, m, v, name):
    r, c = w.shape
    tr = r if r <= 512 else 256 if r % 256 == 0 else 88
    assert r % tr == 0, (name, r, tr)

    def body(w_ref, g_ref, m_ref, v_ref, d_ref, mo_ref, vo_ref):
        gv = g_ref[...]
        mn = ADAM_B1 * m_ref[...] + (1.0 - ADAM_B1) * gv
        vn = ADAM_B2 * v_ref[...] + (1.0 - ADAM_B2) * gv * gv
        m_hat = mn / (1.0 - ADAM_B1 ** ADAM_STEP)
        v_hat = vn / (1.0 - ADAM_B2 ** ADAM_STEP)
        d_ref[...] = -ADAM_LR * (m_hat / (jnp.sqrt(v_hat) + ADAM_EPS) + ADAM_WD * w_ref[...])
        mo_ref[...] = mn
        vo_ref[...] = vn

    blk = pl.BlockSpec((tr, c), lambda i: (i, 0))
    out = SDS((r, c), F32)
    return pl.pallas_call(body, name=name, grid=(r // tr,), in_specs=[blk] * 4, out_specs=[blk] * 3, out_shape=[out] * 3,
                          compiler_params=_cp("parallel"))(w, g, m, v)


def _add_halves(a, b, name):
    s, h, c = a.shape
    tr = h if h <= 512 else 256 if h % 256 == 0 else 176
    assert h % tr == 0, (name, h, tr)

    def body(a_ref, b_ref, o_ref):
        o_ref[...] = (a_ref[...].astype(F32) + b_ref[...].astype(F32)).astype(o_ref.dtype)

    blk = pl.BlockSpec((1, tr, c), lambda i, j: (i, j, 0))
    return pl.pallas_call(body, name=name, grid=(s, h // tr), in_specs=[blk, blk], out_specs=blk, out_shape=SDS(a.shape, WIRE_DTYPE),
                          compiler_params=_cp("parallel", "parallel"))(a, b)


def _sum_chips(a, name):
    s, h, c = a.shape
    tr = h if h <= 512 else 256 if h % 256 == 0 else 176
    assert h % tr == 0, (name, h, tr)

    def body(a_ref, o_ref):
        acc = a_ref[0].astype(F32)
        for j in range(1, s):
            acc = acc + a_ref[j].astype(F32)
        o_ref[...] = acc

    return pl.pallas_call(body, name=name, grid=(h // tr,), in_specs=[pl.BlockSpec((s, tr, c), lambda i: (0, i, 0))],
                          out_specs=pl.BlockSpec((tr, c), lambda i: (i, 0)), out_shape=SDS((h, c), F32),
                          compiler_params=_cp("parallel"))(a)


ANY = pl.BlockSpec(memory_space=pl.ANY)


def _place():
    x, y, c = lax.axis_index("x"), lax.axis_index("y"), lax.axis_index("c")
    return x, y, c, [(1 - x, y), (x, 1 - y), (1 - x, 1 - y)]


def _gather_shards(shards, name):
    nt = len(shards)

    def body(*refs):
        ins, outs = refs[:nt], refs[nt:2 * nt]
        send, recv, fsend, frecv, lsem = refs[2 * nt:]
        x, y, c, chips = _place()
        me = 2 * x + y

        def half(t, chip, cc):
            h = ins[t].shape[0] // 2
            return outs[t].at[chip, pl.ds(cc * h, h)]

        def ici(t, j, src=None):
            cx, cy = chips[j]
            return pltpu.make_async_remote_copy(src_ref=half(t, me, c) if src is None else src, dst_ref=half(t, me, c),
                                                send_sem=send.at[t, j], recv_sem=recv.at[t, j], device_id=(cx, cy, c), device_id_type=MESH)

        def landed(t, j):
            cx, cy = chips[j]
            blk = half(t, 2 * cx + cy, c)
            return pltpu.make_async_remote_copy(src_ref=blk, dst_ref=blk, send_sem=send.at[t, j], recv_sem=recv.at[t, j],
                                                device_id=(cx, cy, c), device_id_type=MESH)

        def d2d(t, j, cc):
            cx, cy = chips[j]
            blk = half(t, 2 * cx + cy, cc)
            return pltpu.make_async_remote_copy(src_ref=blk, dst_ref=blk, send_sem=fsend.at[t, j], recv_sem=frecv.at[t, j],
                                                device_id=(x, y, 1 - c), device_id_type=MESH)

        local = [pltpu.make_async_copy(ins[t], outs[t].at[me], lsem.at[t]) for t in range(nt)]
        for cp in local:
            cp.start()
        for t in range(nt):
            h = ins[t].shape[0] // 2
            for j in range(3):
                ici(t, j, src=ins[t].at[pl.ds(c * h, h)]).start()
        for t in range(nt):
            for j in range(3):
                landed(t, j).wait_recv()
                d2d(t, j, c).start()
        for t in range(nt):
            for j in range(3):
                d2d(t, j, 1 - c).wait_recv()
        for t in range(nt):
            h = ins[t].shape[0] // 2
            for j in range(3):
                ici(t, j, src=ins[t].at[pl.ds(c * h, h)]).wait_send()
                d2d(t, j, c).wait_send()
        for cp in local:
            cp.wait()

    return pl.pallas_call(
        body, name=name, in_specs=[ANY] * nt, out_specs=[ANY] * nt,
        out_shape=[SDS((4,) + s.shape, s.dtype) for s in shards],
        scratch_shapes=[pltpu.SemaphoreType.DMA((nt, 3))] * 4 + [pltpu.SemaphoreType.DMA((nt,))],
        compiler_params=pltpu.CompilerParams(has_side_effects=True))(*shards)


def _swap_halves(grads, name):
    nt = len(grads)

    def body(*refs):
        ins, outs = refs[:nt], refs[nt:2 * nt]
        send, recv = refs[2 * nt:]
        x, y, c, _ = _place()
        cps = []
        for t in range(nt):
            h = ins[t].shape[1] // 2
            cps.append(pltpu.make_async_remote_copy(src_ref=ins[t].at[pl.ds(0, 4), pl.ds((1 - c) * h, h)], dst_ref=outs[t], send_sem=send.at[t],
                                                    recv_sem=recv.at[t], device_id=(x, y, 1 - c), device_id_type=MESH))
        for cp in cps:
            cp.start()
        for cp in cps:
            cp.wait()

    return pl.pallas_call(
        body, name=name, in_specs=[ANY] * nt, out_specs=[ANY] * nt,
        out_shape=[SDS((g.shape[0], g.shape[1] // 2, g.shape[2]), g.dtype) for g in grads],
        scratch_shapes=[pltpu.SemaphoreType.DMA((nt,))] * 2,
        compiler_params=pltpu.CompilerParams(has_side_effects=True))(*grads)


def _send_to_owners(parts, name):
    nt = len(parts)

    def body(*refs):
        ins, outs = refs[:nt], refs[nt:2 * nt]
        send, recv, lsem = refs[2 * nt:]
        x, y, c, chips = _place()
        me = 2 * x + y

        def ici(t, j):
            cx, cy = chips[j]
            return pltpu.make_async_remote_copy(src_ref=ins[t].at[2 * cx + cy], dst_ref=outs[t].at[me], send_sem=send.at[t, j],
                                                recv_sem=recv.at[t, j], device_id=(cx, cy, c), device_id_type=MESH)

        def landed(t, j):
            cx, cy = chips[j]
            blk = outs[t].at[2 * cx + cy]
            return pltpu.make_async_remote_copy(src_ref=blk, dst_ref=blk, send_sem=send.at[t, j], recv_sem=recv.at[t, j],
                                                device_id=(cx, cy, c), device_id_type=MESH)

        local = [pltpu.make_async_copy(ins[t].at[me], outs[t].at[me], lsem.at[t]) for t in range(nt)]
        for cp in local:
            cp.start()
        for t in range(nt):
            for j in range(3):
                ici(t, j).start()
        for t in range(nt):
            for j in range(3):
                landed(t, j).wait_recv()
        for t in range(nt):
            for j in range(3):
                ici(t, j).wait_send()
        for cp in local:
            cp.wait()

    return pl.pallas_call(
        body, name=name, in_specs=[ANY] * nt, out_specs=[ANY] * nt, out_shape=[SDS(p.shape, p.dtype) for p in parts],
        scratch_shapes=[pltpu.SemaphoreType.DMA((nt, 3))] * 2 + [pltpu.SemaphoreType.DMA((nt,))],
        compiler_params=pltpu.CompilerParams(has_side_effects=True))(*parts)


def _join_halves(halves, name):
    nt = len(halves)

    def body(*refs):
        ins, outs = refs[:nt], refs[nt:2 * nt]
        send, recv, lsem = refs[2 * nt:]
        x, y, c, _ = _place()
        local, cps = [], []
        for t in range(nt):
            h = ins[t].shape[0]
            mine = outs[t].at[pl.ds(c * h, h)]
            local.append(pltpu.make_async_copy(ins[t], mine, lsem.at[t]))
            cps.append(pltpu.make_async_remote_copy(src_ref=ins[t], dst_ref=mine, send_sem=send.at[t], recv_sem=recv.at[t],
                                                    device_id=(x, y, 1 - c), device_id_type=MESH))
        for cp in local + cps:
            cp.start()
        for t in range(nt):
            h = ins[t].shape[0]
            theirs = outs[t].at[pl.ds((1 - c) * h, h)]
            pltpu.make_async_remote_copy(src_ref=theirs, dst_ref=theirs, send_sem=send.at[t], recv_sem=recv.at[t],
                                         device_id=(x, y, 1 - c), device_id_type=MESH).wait_recv()
        for cp in cps:
            cp.wait_send()
        for cp in local:
            cp.wait()

    return pl.pallas_call(
        body, name=name, in_specs=[ANY] * nt, out_specs=[ANY] * nt,
        out_shape=[SDS((2 * hh.shape[0], hh.shape[1]), hh.dtype) for hh in halves],
        scratch_shapes=[pltpu.SemaphoreType.DMA((nt,))] * 3,
        compiler_params=pltpu.CompilerParams(has_side_effects=True))(*halves)


def _exchange_small(v, reduce, name):
    rows = v.shape[0]

    def body(v_ref, o_ref, buf, send, recv):
        x, y, c, _ = _place()
        me = 4 * x + 2 * y + c
        buf[me] = v_ref[...]

        def peer(dx, dy, dc):
            return (1 - x if dx else x, 1 - y if dy else y, 1 - c if dc else c)

        peers = [(dx, dy, dc) for dx in range(2) for dy in range(2) for dc in range(2) if (dx, dy, dc) != (0, 0, 0)]
        cps = []
        for j, (dx, dy, dc) in enumerate(peers):
            cps.append(pltpu.make_async_remote_copy(src_ref=v_ref, dst_ref=buf.at[me], send_sem=send.at[j], recv_sem=recv.at[j],
                                                    device_id=peer(dx, dy, dc), device_id_type=MESH))
        for cp in cps:
            cp.start()
        for j, (dx, dy, dc) in enumerate(peers):
            px, py, pc = peer(dx, dy, dc)
            blk = buf.at[4 * px + 2 * py + pc]
            pltpu.make_async_remote_copy(src_ref=blk, dst_ref=blk, send_sem=send.at[j], recv_sem=recv.at[j],
                                         device_id=(px, py, pc), device_id_type=MESH).wait_recv()
        for cp in cps:
            cp.wait_send()
        if reduce:
            acc = buf[0]
            for j in range(1, 8):
                acc = acc + buf[j]
            o_ref[...] = acc
        else:
            o_ref[...] = buf[...]

    vm = pl.BlockSpec(memory_space=pltpu.VMEM)
    return pl.pallas_call(
        body, name=name, in_specs=[vm], out_specs=vm, out_shape=SDS((rows, 128) if reduce else (8, rows, 128), F32),
        scratch_shapes=[pltpu.VMEM((8, rows, 128), F32), pltpu.SemaphoreType.DMA((7,)), pltpu.SemaphoreType.DMA((7,))],
        compiler_params=pltpu.CompilerParams(has_side_effects=True))(v)


def _pack_small(parts):
    flat = jnp.concatenate([p.reshape(-1) for p in parts])
    total = flat.shape[0]
    rows = -(-total // 1024) * 8
    return jnp.pad(flat, (0, rows * 128 - total)).reshape(rows, 128)


def _unpack_small(packed, shapes):
    flat = packed.reshape(-1)
    out, off = [], 0
    for s in shapes:
        size = int(np.prod(s))
        out.append(flat[off:off + size].reshape(s))
        off += size
    return out


def _flip(a):
    return jnp.flip(a, axis=0)


def _local_step(x, mem, target, wts, gains, conv_w, conv_b, hg_lb):
    n = x.shape[0]
    w_in, w_out, w_xq, w_xkv, w_xo, w_up, w_down = (wts[k] for k in ("w_in", "w_out", "w_xq", "w_xkv", "w_xo", "w_up", "w_down"))
    cos, sin = _rope_tables(n)
    seg = _hg_segments()
    gq2 = jnp.tile(gains["q_norm_g"], (1, 2))
    gk2 = jnp.tile(gains["k_norm_g"], (1, 2))
    a0 = hg_lb[:, 0:1, :]
    a1 = hg_lb[:, 1:2, :]

    p, h1 = _norm_mm(x, gains["pre_mix_g"], w_in, F32, 512, 1664, "in_proj")
    qr, kr = _qk_prep(p, gq2, gk2, cos, sin, "qk_prep")
    heads = lambda a: a.reshape(n, ATT_KV_HEADS, ATT_HEAD_DIM).transpose(1, 0, 2)
    kh = heads(kr)
    vh = heads(p[:, OFF_AV:OFF_AV + ATT_KV_DIM].astype(MXU_DTYPE))
    att = _attn_fwd(qr, kh, vh, "attn_fwd")
    col = lambda off: p[:, off:off + HG_DIM]
    hq2 = jnp.stack([col(OFF_HQ), _flip(col(OFF_HQ))])
    z2 = jnp.stack([col(OFF_ZF), _flip(col(OFF_ZB))])
    hv2 = jnp.stack([col(OFF_HI), _flip(col(OFF_HI))])
    o2 = _hgrn_fwd(hq2, z2, hv2, a0, a1, seg, "hgrn_fwd")
    o_f, o_b = o2[0], _flip(o2[1])
    rec = _hg_post(o_f, o_b, p, gains["hg_out_norm_g"], "hg_post")
    cat = jnp.concatenate([att, rec], axis=1)
    mixed = _mm(cat, w_out, "nn", F32, 512, 1024, "out_proj")
    x1 = _resid_norm(x, mixed, gains["post_mix_g"], "mix_resid")
    xq, h2 = _norm_mm(x1, gains["pre_x_g"], w_xq, MXU_DTYPE, 512, 1024, "xq_proj")
    kv, mn = _norm_mm(mem, gains["mem_norm_g"], w_xkv, MXU_DTYPE, 256, 2048, "xkv_proj")
    ox = _xattn_fwd(xq, kv, "xattn_fwd")
    xo = _mm(ox, w_xo, "nn", F32, 512, 1024, "xo_proj")
    x2 = _resid_norm(x1, xo, gains["post_x_g"], "x_resid")
    u, h3 = _norm_mm(x2, gains["pre_ffn_g"], w_up, F32, 512, 512, "up_proj")
    act = _conv_gate(u, conv_w, conv_b, "conv_gate")
    dn = _mm(act, w_down, "nn", F32, 512, 1024, "down_proj")
    d3, loss = _final_loss(x2, dn, gains["post_ffn_g"], target, "ffn_resid_loss")

    gm, gs = {}, {}
    d_dn, gs["post_ffn_g"] = _norm_bwd(dn, gains["post_ffn_g"], d3, None, MXU_DTYPE, "ffn_post_bwd")
    gm["w_down"] = _mm(act, d_dn, "tn", WIRE_DTYPE, 1408, 1024, "down_dw")
    d_act = _mm(d_dn, w_down, "nt", F32, 512, 1408, "down_dx")
    du, gs["conv_w"], gs["conv_b"] = _conv_gate_bwd(u, conv_w, conv_b, d_act, "conv_gate_bwd")
    gm["w_up"] = _mm(h3, du, "tn", WIRE_DTYPE, 512, 512, "up_dw")
    d_h3 = _mm(du, w_up, "nt", F32, 512, 512, "up_dx")
    d2, gs["pre_ffn_g"] = _norm_bwd(x2, gains["pre_ffn_g"], d_h3, d3, F32, "ffn_pre_bwd")
    d_xo, gs["post_x_g"] = _norm_bwd(xo, gains["post_x_g"], d2, None, MXU_DTYPE, "x_post_bwd")
    gm["w_xo"] = _mm(ox, d_xo, "tn", WIRE_DTYPE, 512, 1024, "xo_dw")
    d_ox = _mm(d_xo, w_xo, "nt", MXU_DTYPE, 512, 1024, "xo_dx")
    d_xq, d_k, d_v = _xattn_bwd(xq, kv, d_ox, "xattn_bwd")
    d_kv = jnp.concatenate([d_k, d_v], axis=1).astype(MXU_DTYPE)
    gm["w_xq"] = _mm(h2, d_xq, "tn", WIRE_DTYPE, 512, 1024, "xq_dw")
    d_h2 = _mm(d_xq, w_xq, "nt", F32, 512, 1024, "xq_dx")
    gm["w_xkv"] = _mm(mn, d_kv, "tn", WIRE_DTYPE, 512, 1024, "xkv_dw")
    d_mn = _mm(d_kv, w_xkv, "nt", F32, 256, 1024, "xkv_dx")
    _, gs["mem_norm_g"] = _norm_bwd(mem, gains["mem_norm_g"], d_mn, None, MXU_DTYPE, "mem_norm_bwd")
    d1, gs["pre_x_g"] = _norm_bwd(x1, gains["pre_x_g"], d_h2, d2, F32, "x_pre_bwd")
    d_mixed, gs["post_mix_g"] = _norm_bwd(mixed, gains["post_mix_g"], d1, None, MXU_DTYPE, "mix_post_bwd")
    gm["w_out"] = _mm(cat, d_mixed, "tn", WIRE_DTYPE, 512, 1024, "out_dw")
    d_cat = _mm(d_mixed, w_out, "nt", MXU_DTYPE, 512, 1024, "out_dx")
    d_o, d_hg, dg_hg = _hg_post_bwd(o_f, o_b, p, gains["hg_out_norm_g"], d_cat, "hg_post_bwd")
    gs["hg_out_norm_g"] = dg_hg.reshape(HG_HEADS, HG_HEAD_DIM).sum(axis=0, keepdims=True)
    do2 = jnp.stack([d_o, _flip(d_o)])
    gp, gn = _hg_pair_sums()
    dhq2, dlf_q, s0 = _hgrn_bwd_q(hq2, z2, hv2, a0, a1, seg, gp, do2, "hgrn_bwd_q")
    dz2, dhv2, dlb = _hgrn_bwd_kv(hq2, z2, hv2, a0, a1, seg, gn, do2, dlf_q, s0, "hgrn_bwd_kv")
    lb = jax.nn.sigmoid(a0 - a1)
    da0 = dlb * lb * (1.0 - lb)
    gs["hg_lb"] = jnp.concatenate([da0, -da0], axis=1)
    d_qr, d_kh, d_vh = _attn_bwd(qr, kh, vh, d_cat, "attn_bwd")
    unheads = lambda a: a.transpose(1, 0, 2).reshape(n, ATT_KV_DIM)
    d_aq, d_ak, dgq, dgk = _qk_prep_bwd(p, gq2, gk2, cos, sin, d_qr, unheads(d_kh), "qk_prep_bwd")
    gs["q_norm_g"] = dgq.reshape(ATT_HEADS, ATT_HEAD_DIM).sum(axis=0, keepdims=True)
    gs["k_norm_g"] = dgk.reshape(ATT_KV_HEADS, ATT_HEAD_DIM).sum(axis=0, keepdims=True)
    d_p = jnp.concatenate([d_aq, d_ak, unheads(d_vh).astype(MXU_DTYPE), (dhq2[0] + _flip(dhq2[1])).astype(MXU_DTYPE),
                           dz2[0].astype(MXU_DTYPE), _flip(dz2[1]).astype(MXU_DTYPE), (dhv2[0] + _flip(dhv2[1])).astype(MXU_DTYPE), d_hg], axis=1)
    gm["w_in"] = _mm(h1, d_p, "tn", WIRE_DTYPE, 512, 1664, "in_dw")
    d_h1 = _mm(d_p, w_in, "nt", F32, 512, 1024, "in_dx")
    grad_x, gs["pre_mix_g"] = _norm_bwd(x, gains["pre_mix_g"], d_h1, d1, F32, "mix_pre_bwd")
    return loss, grad_x, gm, gs


MATS = ("w_in", "w_out", "w_xq", "w_xkv", "w_xo", "w_up", "w_down")
COL_SHARDED = ("w_in", "w_xkv", "w_up")
GAINS = ("pre_mix_g", "q_norm_g", "k_norm_g", "hg_out_norm_g", "post_mix_g", "pre_x_g", "mem_norm_g", "post_x_g", "pre_ffn_g", "post_ffn_g")
WEIGHTS = ('pre_mix_g', 'w_in', 'q_norm_g', 'k_norm_g', 'hg_lb', 'hg_out_norm_g', 'w_out', 'post_mix_g', 'pre_x_g', 'mem_norm_g', 'w_xq',
           'w_xkv', 'w_xo', 'post_x_g', 'pre_ffn_g', 'w_up', 'conv_w', 'conv_b', 'w_down', 'post_ffn_g')


def kernel(x, mem, pre_mix_g, w_in, q_norm_g, k_norm_g, hg_lb, hg_out_norm_g, w_out, post_mix_g, pre_x_g, mem_norm_g, w_xq, w_xkv, w_xo, post_x_g, pre_ffn_g, w_up, conv_w, conv_b, w_down, post_ffn_g, loss_target, m_pre_mix_g, m_w_in, m_q_norm_g, m_k_norm_g, m_hg_lb, m_hg_out_norm_g, m_w_out, m_post_mix_g, m_pre_x_g, m_mem_norm_g, m_w_xq, m_w_xkv, m_w_xo, m_post_x_g, m_pre_ffn_g, m_w_up, m_conv_w, m_conv_b, m_w_down, m_post_ffn_g, v_pre_mix_g, v_w_in, v_q_norm_g, v_k_norm_g, v_hg_lb, v_hg_out_norm_g, v_w_out, v_post_mix_g, v_pre_x_g, v_mem_norm_g, v_w_xq, v_w_xkv, v_w_xo, v_post_x_g, v_pre_ffn_g, v_w_up, v_conv_w, v_conv_b, v_w_down, v_post_ffn_g):
    args = dict(locals())
    w = {k: args[k] for k in WEIGHTS}
    m = {k: args["m_" + k] for k in WEIGHTS}
    v = {k: args["v_" + k] for k in WEIGHTS}
    chip = 2 * lax.axis_index("x") + lax.axis_index("y")
    core = lax.axis_index("c")

    shards = [w[k][0].astype(WIRE_DTYPE) for k in MATS]
    gathered = _gather_shards(shards, "gather_weights")
    wts = {}
    for k, g in zip(MATS, gathered):
        wts[k] = jnp.concatenate([g[s] for s in range(4)], axis=1) if k in COL_SHARDED else g.reshape(-1, g.shape[-1])
    small_in = _exchange_small(_pack_small([w["conv_w"][0], w["hg_lb"]]), False, "gather_small")
    cw_parts, lb_parts = [], []
    for s in range(4):
        cw_s, lb_s = _unpack_small(small_in[2 * s], [w["conv_w"][0].shape, w["hg_lb"].shape])
        cw_parts.append(cw_s)
        lb_parts.append(lb_s)
    conv_w_full = jnp.concatenate(cw_parts, axis=1)
    hg_lb_full = jnp.concatenate(lb_parts, axis=2)

    gains = {k: w[k] for k in GAINS}
    loss, grad_x, gm, gs = _local_step(x[0], mem[0], loss_target[0], wts, gains, conv_w_full, w["conv_b"], hg_lb_full)
    loss = lax.psum(loss[0, 0], ("x", "y", "c"))

    by_owner = []
    for k in MATS:
        g = gm[k]
        if k in COL_SHARDED:
            by_owner.append(g.reshape(g.shape[0], 4, g.shape[1] // 4).transpose(1, 0, 2))
        else:
            by_owner.append(g.reshape(4, g.shape[0] // 4, g.shape[1]))
    from_sibling = _swap_halves(by_owner, "grad_swap_halves")
    pair = []
    for k, g, r in zip(MATS, by_owner, from_sibling):
        h = g.shape[1] // 2
        pair.append(_add_halves(lax.dynamic_slice_in_dim(g, core * h, h, axis=1), r, "grad_pair_" + k))
    at_owner = _send_to_owners(pair, "grad_to_owner")
    halves = [_sum_chips(a, "grad_sum_" + k) for k, a in zip(MATS, at_owner)]
    reduced = _join_halves(halves, "grad_join_halves")
    grads = {k: r[None] for k, r in zip(MATS, reduced)}

    small_names = GAINS + ("conv_b", "conv_w", "hg_lb")
    small_shapes = [gs[k].shape for k in small_names]
    summed = _unpack_small(_exchange_small(_pack_small([gs[k] for k in small_names]), True, "reduce_small"), small_shapes)
    for k, g in zip(small_names, summed):
        grads[k] = g
    ncw = w["conv_w"].shape[2]
    grads["conv_w"] = lax.dynamic_slice_in_dim(grads["conv_w"], chip * ncw, ncw, axis=1)[None]
    nlb = w["hg_lb"].shape[2]
    grads["hg_lb"] = lax.dynamic_slice_in_dim(grads["hg_lb"], chip * nlb, nlb, axis=2)

    delta, new_m, new_v = {}, {}, {}
    for k in WEIGHTS:
        shape = w[k].shape
        two_d = lambda a: a.reshape(-1, shape[-1])
        d, mo, vo = _adamw(two_d(w[k]), two_d(grads[k]), two_d(m[k]), two_d(v[k]), "adamw_" + k)
        delta[k], new_m[k], new_v[k] = d.reshape(shape), mo.reshape(shape), vo.reshape(shape)
        grads[k] = grads[k].reshape(shape)
    return (loss, grad_x[None], *[grads[k] for k in WEIGHTS], *[delta[k] for k in WEIGHTS],
            *[new_m[k] for k in WEIGHTS], *[new_v[k] for k in WEIGHTS])
```

```python
import functools

import numpy as np
import jax
import jax.numpy as jnp
from jax import lax
from jax.experimental import pallas as pl
from jax.experimental.pallas import tpu as pltpu

F32 = jnp.float32
MXU_DTYPE = jnp.bfloat16
WIRE_DTYPE = jnp.bfloat16
VMEM_LIMIT_BYTES = 56 * 1024 * 1024
EPS = 1e-6
MESH = pl.DeviceIdType.MESH

D_MODEL = 1024
GRID_W = 64
ATT_HEADS, ATT_KV_HEADS, ATT_HEAD_DIM = 8, 2, 64
ATT_GROUP = ATT_HEADS // ATT_KV_HEADS
ATT_Q_DIM, ATT_KV_DIM = 512, 128
ROPE_THETA = 10000.0
HG_HEADS, HG_HEAD_DIM, HG_DIM = 4, 128, 512
HG_CHUNK = 128
HG_LEVELS = 7
X_HEADS, X_HEAD_DIM = 4, 256
D_FF = 2816
FF_COLS = 256
FF_BLOCKS = D_FF // FF_COLS
N_IN = 3328
OFF_AQ, OFF_AK, OFF_AV, OFF_HQ, OFF_ZF, OFF_ZB, OFF_HI, OFF_HG = 0, 512, 640, 768, 1280, 1792, 2304, 2816

ADAM_LR, ADAM_B1, ADAM_B2, ADAM_EPS, ADAM_WD, ADAM_STEP = 0.001, 0.9, 0.999, 1e-08, 0.01, 10

SDS = jax.ShapeDtypeStruct


def _cp(*sem):
    return pltpu.CompilerParams(dimension_semantics=sem, vmem_limit_bytes=VMEM_LIMIT_BYTES)


def _dot(a, b, form="nn"):
    dims = {"nn": (((1,), (0,)), ((), ())), "nt": (((1,), (1,)), ((), ())), "tn": (((0,), (0,)), ((), ()))}[form]
    return lax.dot_general(a.astype(MXU_DTYPE), b.astype(MXU_DTYPE), dims, preferred_element_type=F32)


def _sigmoid(x):
    return 1.0 / (1.0 + jnp.exp(-x))


def _rstd(x):
    return lax.rsqrt(jnp.mean(x * x, axis=-1, keepdims=True) + EPS)


def _rms_bwd(x, g, dy):
    r = _rstd(x)
    xh = x * r
    dn = dy * g
    dx = r * (dn - xh * jnp.mean(dn * xh, axis=-1, keepdims=True))
    return dx, jnp.sum(dy * xh, axis=0, keepdims=True)


def _mm(a, b, form, out_dtype, tm, tn, name):
    if form == "nn":
        (m, k), n = a.shape, b.shape[1]
    elif form == "nt":
        (m, k), n = a.shape, b.shape[0]
    else:
        (k, m), n = a.shape, b.shape[1]
    tm, tn = min(tm, m), min(tn, n)
    assert m % tm == 0 and n % tn == 0, (name, m, n, tm, tn)

    def body(a_ref, b_ref, o_ref):
        o_ref[...] = _dot(a_ref[...], b_ref[...], form).astype(o_ref.dtype)

    a_spec = pl.BlockSpec((k, tm), lambda i, j: (0, i)) if form == "tn" else pl.BlockSpec((tm, k), lambda i, j: (i, 0))
    b_spec = pl.BlockSpec((tn, k), lambda i, j: (j, 0)) if form == "nt" else pl.BlockSpec((k, tn), lambda i, j: (0, j))
    return pl.pallas_call(
        body, name=name, grid=(m // tm, n // tn), in_specs=[a_spec, b_spec],
        out_specs=pl.BlockSpec((tm, tn), lambda i, j: (i, j)), out_shape=SDS((m, n), out_dtype),
        compiler_params=_cp("parallel", "parallel"))(a, b)


def _norm_mm(x, g, w, out_dtype, tm, tn, name):
    m, d = x.shape
    n = w.shape[1]
    tm, tn = min(tm, m), min(tn, n)
    assert m % tm == 0 and n % tn == 0, (name, m, n, tm, tn)

    def body(x_ref, g_ref, w_ref, o_ref, h_ref, hs):
        @pl.when(pl.program_id(1) == 0)
        def _():
            xv = x_ref[...]
            h = (xv * _rstd(xv) * g_ref[...]).astype(MXU_DTYPE)
            hs[...] = h
            h_ref[...] = h

        o_ref[...] = _dot(hs[...], w_ref[...]).astype(o_ref.dtype)

    return pl.pallas_call(
        body, name=name, grid=(m // tm, n // tn),
        in_specs=[pl.BlockSpec((tm, d), lambda i, j: (i, 0)), pl.BlockSpec((1, d), lambda i, j: (0, 0)),
                  pl.BlockSpec((d, tn), lambda i, j: (0, j))],
        out_specs=[pl.BlockSpec((tm, tn), lambda i, j: (i, j)), pl.BlockSpec((tm, d), lambda i, j: (i, 0))],
        out_shape=[SDS((m, n), out_dtype), SDS((m, d), MXU_DTYPE)],
        scratch_shapes=[pltpu.VMEM((tm, d), MXU_DTYPE)],
        compiler_params=_cp("parallel", "arbitrary"))(x, g, w)


ROW_TILE = 256


def _resid_norm(x, y, g, name):
    n, d = x.shape
    tr = min(ROW_TILE, n)

    def body(x_ref, y_ref, g_ref, o_ref):
        yv = y_ref[...]
        o_ref[...] = x_ref[...] + yv * _rstd(yv) * g_ref[...]

    row = pl.BlockSpec((tr, d), lambda i: (i, 0))
    return pl.pallas_call(
        body, name=name, grid=(n // tr,), in_specs=[row, row, pl.BlockSpec((1, d), lambda i: (0, 0))],
        out_specs=row, out_shape=SDS((n, d), F32), compiler_params=_cp("parallel"))(x, y, g)


def _norm_bwd(x, g, dy, res, out_dtype, name):
    n, d = x.shape
    tr = min(ROW_TILE, n)
    has_res = res is not None

    def body(*refs):
        x_ref, g_ref, dy_ref = refs[:3]
        dx_ref, dg_ref = refs[-2:]
        dx, dg = _rms_bwd(x_ref[...], g_ref[...], dy_ref[...].astype(F32))
        if has_res:
            dx = dx + refs[3][...]
        dx_ref[...] = dx.astype(dx_ref.dtype)

        @pl.when(pl.program_id(0) == 0)
        def _():
            dg_ref[...] = jnp.zeros_like(dg_ref)

        dg_ref[...] += dg

    row = pl.BlockSpec((tr, d), lambda i: (i, 0))
    vec = pl.BlockSpec((1, d), lambda i: (0, 0))
    ins = [x, g, dy] + ([res] if has_res else [])
    return pl.pallas_call(
        body, name=name, grid=(n // tr,), in_specs=[row, vec, row] + ([row] if has_res else []),
        out_specs=[row, vec], out_shape=[SDS((n, d), out_dtype), SDS((1, d), F32)],
        compiler_params=_cp("arbitrary"))(*ins)


def _final_loss(x, y, g, target, name):
    n, d = x.shape
    tr = min(ROW_TILE, n)

    def body(x_ref, y_ref, g_ref, t_ref, d_ref, l_ref):
        yv = y_ref[...]
        diff = x_ref[...] + yv * _rstd(yv) * g_ref[...] - t_ref[...]
        d_ref[...] = diff * (1.0 / d)

        @pl.when(pl.program_id(0) == 0)
        def _():
            l_ref[...] = jnp.zeros_like(l_ref)

        l_ref[...] += 0.5 * jnp.sum(jnp.mean(diff * diff, axis=-1, keepdims=True), axis=0, keepdims=True)

    row = pl.BlockSpec((tr, d), lambda i: (i, 0))
    return pl.pallas_call(
        body, name=name, grid=(n // tr,), in_specs=[row, row, pl.BlockSpec((1, d), lambda i: (0, 0)), row],
        out_specs=[row, pl.BlockSpec((1, 1), lambda i: (0, 0))], out_shape=[SDS((n, d), F32), SDS((1, 1), F32)],
        compiler_params=_cp("arbitrary"))(x, y, g, target)


def _rope_tables(n):
    pairs = ATT_HEAD_DIM // 4
    t = np.arange(n)
    inv = np.power(ROPE_THETA, -np.arange(pairs, dtype=np.float32) / pairs).astype(np.float32)
    ang = np.concatenate([(t // GRID_W)[:, None].astype(np.float32) * inv, (t % GRID_W)[:, None].astype(np.float32) * inv], axis=-1)
    cos = np.repeat(np.cos(ang), 2, axis=-1)
    sin = np.repeat(np.sin(ang), 2, axis=-1) * np.tile(np.array([-1.0, 1.0], np.float32), ATT_HEAD_DIM // 2)
    return jnp.asarray(np.tile(cos, 2), F32), jnp.asarray(np.tile(sin, 2), F32)


def _swap_pairs(x):
    lane = lax.broadcasted_iota(jnp.int32, x.shape, 1)
    return jnp.where((lane & 1) == 0, pltpu.roll(x, 127, axis=1), pltpu.roll(x, 1, axis=1))


def _head_mean(v):
    lane = lax.broadcasted_iota(jnp.int32, v.shape, 1)
    lo = jnp.where(lane < ATT_HEAD_DIM, v, 0.0)
    s0 = jnp.sum(lo, axis=-1, keepdims=True)
    s1 = jnp.sum(v - lo, axis=-1, keepdims=True)
    return jnp.where(lane < ATT_HEAD_DIM, s0, s1) * (1.0 / ATT_HEAD_DIM)


def _qk_prep(p, gq, gk, cos, sin, name):
    n = p.shape[0]
    tr = min(ROW_TILE, n)

    def one(xv, g, c, s):
        xn = xv * lax.rsqrt(_head_mean(xv * xv) + EPS) * g
        return xn * c + _swap_pairs(xn) * s

    def body(q_ref, k_ref, gq_ref, gk_ref, c_ref, s_ref, qo_ref, ko_ref):
        c, s = c_ref[...], s_ref[...]
        for j in range(ATT_Q_DIM // 128):
            qo_ref[:, j * 128:(j + 1) * 128] = one(q_ref[:, j * 128:(j + 1) * 128], gq_ref[...], c, s).astype(qo_ref.dtype)
        ko_ref[...] = one(k_ref[...], gk_ref[...], c, s).astype(ko_ref.dtype)

    vec = pl.BlockSpec((1, 128), lambda i: (0, 0))
    tab = pl.BlockSpec((tr, 128), lambda i: (i, 0))
    return pl.pallas_call(
        body, name=name, grid=(n // tr,),
        in_specs=[pl.BlockSpec((tr, ATT_Q_DIM), lambda i: (i, 0)), pl.BlockSpec((tr, 128), lambda i: (i, OFF_AK // 128)), vec, vec, tab, tab],
        out_specs=[pl.BlockSpec((tr, ATT_Q_DIM), lambda i: (i, 0)), tab],
        out_shape=[SDS((n, ATT_Q_DIM), MXU_DTYPE), SDS((n, ATT_KV_DIM), MXU_DTYPE)],
        compiler_params=_cp("parallel"))(p, p, gq, gk, cos, sin)


def _qk_prep_bwd(p, gq, gk, cos, sin, dq, dk, name):
    n = p.shape[0]
    tr = min(ROW_TILE, n)

    def one(xv, g, c, s, dout):
        dxn = dout * c + _swap_pairs(dout * s)
        r = lax.rsqrt(_head_mean(xv * xv) + EPS)
        xh = xv * r
        dn = dxn * g
        dx = r * (dn - xh * _head_mean(dn * xh))
        return dx, jnp.sum(dxn * xh, axis=0, keepdims=True)

    def body(q_ref, k_ref, gq_ref, gk_ref, c_ref, s_ref, dq_ref, dk_ref, dqo_ref, dko_ref, dgq_ref, dgk_ref):
        @pl.when(pl.program_id(0) == 0)
        def _():
            dgq_ref[...] = jnp.zeros_like(dgq_ref)
            dgk_ref[...] = jnp.zeros_like(dgk_ref)

        c, s = c_ref[...], s_ref[...]
        for j in range(ATT_Q_DIM // 128):
            sl = slice(j * 128, (j + 1) * 128)
            dx, dg = one(q_ref[:, sl], gq_ref[...], c, s, dq_ref[:, sl])
            dqo_ref[:, sl] = dx.astype(dqo_ref.dtype)
            dgq_ref[:, sl] += dg
        dx, dg = one(k_ref[...], gk_ref[...], c, s, dk_ref[...])
        dko_ref[...] = dx.astype(dko_ref.dtype)
        dgk_ref[...] += dg

    vec = pl.BlockSpec((1, 128), lambda i: (0, 0))
    tab = pl.BlockSpec((tr, 128), lambda i: (i, 0))
    qrow = pl.BlockSpec((tr, ATT_Q_DIM), lambda i: (i, 0))
    return pl.pallas_call(
        body, name=name, grid=(n // tr,),
        in_specs=[qrow, pl.BlockSpec((tr, 128), lambda i: (i, OFF_AK // 128)), vec, vec, tab, tab, qrow, tab],
        out_specs=[qrow, tab, pl.BlockSpec((1, ATT_Q_DIM), lambda i: (0, 0)), vec],
        out_shape=[SDS((n, ATT_Q_DIM), MXU_DTYPE), SDS((n, ATT_KV_DIM), MXU_DTYPE), SDS((1, ATT_Q_DIM), F32), SDS((1, 128), F32)],
        compiler_params=_cp("arbitrary"))(p, p, gq, gk, cos, sin, dq, dk)


ATT_TQ = 256


def _attn_fwd(q, k, v, name):
    n = q.shape[0]
    tq = min(ATT_TQ, n)
    scale = ATT_HEAD_DIM ** -0.5
    gw = ATT_GROUP * ATT_HEAD_DIM

    def body(q_ref, k_ref, v_ref, o_ref):
        kk, vv = k_ref[0], v_ref[0]
        outs = []
        for g in range(ATT_GROUP):
            s = _dot(q_ref[:, g * ATT_HEAD_DIM:(g + 1) * ATT_HEAD_DIM], kk, "nt") * scale
            e = jnp.exp(s - jnp.max(s, axis=-1, keepdims=True))
            p = e / jnp.sum(e, axis=-1, keepdims=True)
            outs.append(_dot(p, vv))
        o_ref[...] = jnp.concatenate(outs, axis=-1).astype(o_ref.dtype)

    kv = pl.BlockSpec((1, n, ATT_HEAD_DIM), lambda h, i: (h, 0, 0))
    return pl.pallas_call(
        body, name=name, grid=(ATT_KV_HEADS, n // tq),
        in_specs=[pl.BlockSpec((tq, gw), lambda h, i: (i, h)), kv, kv],
        out_specs=pl.BlockSpec((tq, gw), lambda h, i: (i, h)), out_shape=SDS((n, ATT_Q_DIM), MXU_DTYPE),
        compiler_params=_cp("parallel", "parallel"))(q, k, v)


def _attn_bwd(q, k, v, do, name):
    n = q.shape[0]
    tq = min(ATT_TQ, n)
    scale = ATT_HEAD_DIM ** -0.5
    gw = ATT_GROUP * ATT_HEAD_DIM

    def body(q_ref, k_ref, v_ref, do_ref, dq_ref, dk_ref, dv_ref):
        @pl.when(pl.program_id(1) == 0)
        def _():
            dk_ref[...] = jnp.zeros_like(dk_ref)
            dv_ref[...] = jnp.zeros_like(dv_ref)

        kk, vv = k_ref[0], v_ref[0]
        dqs = []
        dk_acc = jnp.zeros((n, ATT_HEAD_DIM), F32)
        dv_acc = jnp.zeros((n, ATT_HEAD_DIM), F32)
        for g in range(ATT_GROUP):
            sl = slice(g * ATT_HEAD_DIM, (g + 1) * ATT_HEAD_DIM)
            qg, dog = q_ref[:, sl], do_ref[:, sl]
            s = _dot(qg, kk, "nt") * scale
            e = jnp.exp(s - jnp.max(s, axis=-1, keepdims=True))
            p = e / jnp.sum(e, axis=-1, keepdims=True)
            dp = _dot(dog, vv, "nt")
            ds = p * (dp - jnp.sum(p * dp, axis=-1, keepdims=True)) * scale
            dqs.append(_dot(ds, kk))
            dk_acc += _dot(ds, qg, "tn")
            dv_acc += _dot(p, dog, "tn")
        dq_ref[...] = jnp.concatenate(dqs, axis=-1)
        dk_ref[0] += dk_acc
        dv_ref[0] += dv_acc

    kv = pl.BlockSpec((1, n, ATT_HEAD_DIM), lambda h, i: (h, 0, 0))
    qb = pl.BlockSpec((tq, gw), lambda h, i: (i, h))
    return pl.pallas_call(
        body, name=name, grid=(ATT_KV_HEADS, n // tq), in_specs=[qb, kv, kv, qb], out_specs=[qb, kv, kv],
        out_shape=[SDS((n, ATT_Q_DIM), F32), SDS((ATT_KV_HEADS, n, ATT_HEAD_DIM), F32), SDS((ATT_KV_HEADS, n, ATT_HEAD_DIM), F32)],
        compiler_params=_cp("parallel", "arbitrary"))(q, k, v, do)


def _both_directions(mats, axis):
    fwd = np.concatenate(mats, axis=axis).astype(np.float32)
    bwd = np.concatenate([m[::-1, ::-1] for m in mats], axis=axis).astype(np.float32)
    return jnp.asarray(np.stack([fwd, bwd]), MXU_DTYPE)


def _hg_segments():
    c = HG_CHUNK
    t = np.arange(c)[:, None]
    r = np.arange(c)[None, :]
    mats = [(r <= t)]
    for lev in range(HG_LEVELS):
        h = c >> (lev + 1)
        mid = (t // (2 * h)) * (2 * h) + h - 1
        hi = (t // h) % 2 == 1
        mats.append(np.where(hi, (r > mid) & (r <= t), (r > t) & (r <= mid)))
    mats.append(r > t)
    return _both_directions(mats, 0)


def _hg_pair_sums():
    c = HG_CHUNK
    r = np.arange(c)[:, None]
    t = np.arange(c)[None, :]
    gp, gn = [t >= r], [t < r]
    for lev in range(HG_LEVELS):
        sh = HG_LEVELS - 1 - lev
        same = (r >> sh) == (t >> sh)
        gp.append(same & (t >= r))
        gn.append(same & (t < r))
    return _both_directions(gp, 1), _both_directions(gn, 1)


def _split_dot(mat, x):
    hi = x.astype(MXU_DTYPE)
    lo = (x - hi.astype(F32)).astype(MXU_DTYPE)
    return _dot(mat, hi) + _dot(mat, lo)


def _hg_gates(hq, z, a0, a1):
    q = hq * _sigmoid(hq)
    sg = _sigmoid(z)
    lb = _sigmoid(a0 - a1)
    f = lb + (1.0 - lb) * sg
    k = (1.0 - lb) * (1.0 - sg)
    return q, f, k, sg, lb


def _hg_level_masks(mirrored):
    c = HG_CHUNK
    row = lax.broadcasted_iota(jnp.int32, (c, 1), 0)
    rr = lax.broadcasted_iota(jnp.int32, (c, c), 0)
    cc = lax.broadcasted_iota(jnp.int32, (c, c), 1)
    his, sames = [], []
    for lev in range(HG_LEVELS):
        sh = HG_LEVELS - 1 - lev
        his.append(jnp.logical_xor(((row >> sh) & 1) == 1, mirrored))
        sames.append((rr >> (sh + 1)) == (cc >> (sh + 1)))
    return his, sames, rr == cc


def _hg_intra(q, k, ex, masks):
    his, sames, eye = masks
    a = jnp.where(eye, jnp.sum(q * k, axis=-1, keepdims=True), 0.0)
    for lev in range(HG_LEVELS):
        e = ex[lev + 1]
        qs = jnp.where(his[lev], q * e, 0.0)
        ks = jnp.where(his[lev], 0.0, k * e)
        a = a + jnp.where(sames[lev], _dot(qs, ks, "nt"), 0.0)
    return a


def _hg_specs(n, with_time):
    c = HG_CHUNK
    nc = n // c

    def chunk(d, i):
        first = d if with_time else 1 - d
        return i + first * (nc - 1 - 2 * i)

    pcol = lambda off: pl.BlockSpec((c, HG_HEAD_DIM), lambda d, h, i: (chunk(d, i), off // HG_HEAD_DIM + h))
    specs = dict(
        hq=pcol(OFF_HQ), v=pcol(OFF_HI),
        z=pl.BlockSpec((c, HG_HEAD_DIM), lambda d, h, i: (chunk(d, i), OFF_ZF // HG_HEAD_DIM + (OFF_ZB - OFF_ZF) // HG_HEAD_DIM * d + h)),
        shared=pl.BlockSpec((c, HG_HEAD_DIM), lambda d, h, i: (chunk(d, i), h)),
        per_dir=pl.BlockSpec((1, c, HG_HEAD_DIM), lambda d, h, i: (d, chunk(d, i), h)),
        vec=pl.BlockSpec((1, 1, HG_HEAD_DIM), lambda d, h, i: (d, 0, h)),
        seg=pl.BlockSpec((1, (HG_LEVELS + 2) * c, c), lambda d, h, i: (d, 0, 0)),
        sums=pl.BlockSpec((1, c, (HG_LEVELS + 1) * c), lambda d, h, i: (d, 0, 0)),
        state=pl.BlockSpec((1, 1, 1, HG_HEAD_DIM, HG_HEAD_DIM), lambda d, h, i: (d, h, chunk(d, i), 0, 0)))
    return nc, specs


def _hg_exps(seg_ref, f):
    lf = jnp.log(f)
    args = _split_dot(seg_ref[0], lf)
    c = HG_CHUNK
    return [jnp.exp(args[j * c:(j + 1) * c]) for j in range(HG_LEVELS + 2)]


def _hg_last_row(a, mirrored):
    return jnp.where(mirrored, a[0:1, :], a[HG_CHUNK - 1:HG_CHUNK, :])


def _hgrn_fwd(p, a0, a1, seg, name):
    n = p.shape[0]
    nc, sp = _hg_specs(n, True)

    def body(hq_ref, z_ref, v_ref, a0_ref, a1_ref, seg_ref, o_ref, st):
        @pl.when(pl.program_id(2) == 0)
        def _():
            st[...] = jnp.zeros_like(st)

        mirrored = pl.program_id(0) == 1
        q, f, k, _, _ = _hg_gates(hq_ref[...], z_ref[...], a0_ref[0], a1_ref[0])
        vv = v_ref[...]
        ex = _hg_exps(seg_ref, f)
        a = _hg_intra(q, k, ex, _hg_level_masks(mirrored))
        s_t = st[...]
        o_ref[0] = _dot(a, vv) + _dot(q * ex[0], s_t, "nt")
        st[...] = s_t * _hg_last_row(ex[0], mirrored) + _dot(vv, k * ex[HG_LEVELS + 1], "tn")

    return pl.pallas_call(
        body, name=name, grid=(2, HG_HEADS, nc), in_specs=[sp["hq"], sp["z"], sp["v"], sp["vec"], sp["vec"], sp["seg"]],
        out_specs=sp["per_dir"], out_shape=SDS((2, n, HG_DIM), F32),
        scratch_shapes=[pltpu.VMEM((HG_HEAD_DIM, HG_HEAD_DIM), F32)],
        compiler_params=_cp("parallel", "parallel", "arbitrary"))(p, p, p, a0, a1, seg)


def _hgrn_bwd_q(p, a0, a1, seg, gp, do, name):
    n = p.shape[0]
    nc, sp = _hg_specs(n, True)


    def body(hq_ref, z_ref, v_ref, a0_ref, a1_ref, seg_ref, gp_ref, do_ref, dhq_ref, dlf_ref, s0_ref, st):
        @pl.when(pl.program_id(2) == 0)
        def _():
            st[...] = jnp.zeros_like(st)

        mirrored = pl.program_id(0) == 1
        s_t = st[...]
        s0_ref[0, 0, 0] = s_t
        hqv = hq_ref[...]
        q, f, k, _, _ = _hg_gates(hqv, z_ref[...], a0_ref[0], a1_ref[0])
        vv, dov = v_ref[...], do_ref[...]
        ex = _hg_exps(seg_ref, f)
        his, sames, eye = _hg_level_masks(mirrored)
        da = _dot(dov, vv, "nt")
        dq_inter = ex[0] * _dot(dov, s_t)
        dq = jnp.sum(dov * vv, axis=-1, keepdims=True) * k + dq_inter
        terms = [q * dq_inter]
        for lev in range(HG_LEVELS):
            e = ex[lev + 1]
            ks = jnp.where(his[lev], 0.0, k * e)
            part = jnp.where(his[lev], e, 0.0) * _dot(jnp.where(sames[lev], da, 0.0), ks)
            dq = dq + part
            terms.append(q * part)
        dlf_ref[0] = _dot(gp_ref[0], jnp.concatenate(terms, axis=0))
        sq = _sigmoid(hqv)
        dhq_ref[0] = dq * sq * (1.0 + hqv * (1.0 - sq))
        st[...] = s_t * _hg_last_row(ex[0], mirrored) + _dot(vv, k * ex[HG_LEVELS + 1], "tn")

    out = SDS((2, n, HG_DIM), F32)
    return pl.pallas_call(
        body, name=name, grid=(2, HG_HEADS, nc),
        in_specs=[sp["hq"], sp["z"], sp["v"], sp["vec"], sp["vec"], sp["seg"], sp["sums"], sp["shared"]],
        out_specs=[sp["per_dir"], sp["per_dir"], sp["state"]],
        out_shape=[out, out, SDS((2, HG_HEADS, nc, HG_HEAD_DIM, HG_HEAD_DIM), F32)],
        scratch_shapes=[pltpu.VMEM((HG_HEAD_DIM, HG_HEAD_DIM), F32)],
        compiler_params=_cp("parallel", "parallel", "arbitrary"))(p, p, p, a0, a1, seg, gp, do)


def _hgrn_bwd_kv(p, a0, a1, seg, gn, do, dlf_q, s0, name):
    n = p.shape[0]
    nc, sp = _hg_specs(n, False)

    def body(hq_ref, z_ref, v_ref, a0_ref, a1_ref, seg_ref, gn_ref, do_ref, dlfq_ref, s0_ref, dz_ref, dv_ref, dlb_ref, rt):
        @pl.when(pl.program_id(2) == 0)
        def _():
            rt[...] = jnp.zeros_like(rt)
            dlb_ref[...] = jnp.zeros_like(dlb_ref)

        mirrored = pl.program_id(0) == 1
        q, f, k, sg, lb = _hg_gates(hq_ref[...], z_ref[...], a0_ref[0], a1_ref[0])
        vv, dov = v_ref[...], do_ref[...]
        ex = _hg_exps(seg_ref, f)
        masks = _hg_level_masks(mirrored)
        his, sames, eye = masks
        a = _hg_intra(q, k, ex, masks)
        da = _dot(dov, vv, "nt")
        r_t = rt[...]
        k_end = k * ex[HG_LEVELS + 1]
        dv_ref[0] = _dot(a, dov, "tn") + _dot(k_end, r_t, "nt")
        dk_inter = ex[HG_LEVELS + 1] * _dot(vv, r_t)
        dk = jnp.sum(dov * vv, axis=-1, keepdims=True) * q + dk_inter
        terms = [k * dk_inter]
        for lev in range(HG_LEVELS):
            e = ex[lev + 1]
            qs = jnp.where(his[lev], q * e, 0.0)
            part = jnp.where(his[lev], 0.0, e) * _dot(jnp.where(sames[lev], da, 0.0), qs, "tn")
            dk = dk + part
            terms.append(k * part)
        decay = _hg_last_row(ex[0], mirrored)
        rt[...] = r_t * decay + _dot(dov, q * ex[0], "tn")
        later = decay * jnp.sum(s0_ref[0, 0, 0] * r_t, axis=0, keepdims=True)
        dlf = dlfq_ref[0] + _dot(gn_ref[0], jnp.concatenate(terms, axis=0)) + later
        df = dlf / f - dk
        dz_ref[0] = df * (1.0 - lb) * sg * (1.0 - sg)
        dlb_ref[0] += jnp.sum(df * (1.0 - sg), axis=0, keepdims=True)

    out = SDS((2, n, HG_DIM), F32)
    return pl.pallas_call(
        body, name=name, grid=(2, HG_HEADS, nc),
        in_specs=[sp["hq"], sp["z"], sp["v"], sp["vec"], sp["vec"], sp["seg"], sp["sums"], sp["shared"], sp["per_dir"], sp["state"]],
        out_specs=[sp["per_dir"], sp["per_dir"], sp["vec"]], out_shape=[out, out, SDS((2, 1, HG_DIM), F32)],
        scratch_shapes=[pltpu.VMEM((HG_HEAD_DIM, HG_HEAD_DIM), F32)],
        compiler_params=_cp("parallel", "parallel", "arbitrary"))(p, p, p, a0, a1, seg, gn, do, dlf_q, s0)


def _hg_post(o2, p, g, name):
    n = p.shape[0]
    tr = min(ROW_TILE, n)
    w = 2 * HG_HEAD_DIM

    def body(of_ref, ob_ref, hg_ref, g_ref, o_ref):
        for j in range(2):
            sl = slice(j * HG_HEAD_DIM, (j + 1) * HG_HEAD_DIM)
            o = of_ref[0, :, sl] + ob_ref[0, :, sl]
            hg = hg_ref[:, sl]
            o_ref[:, sl] = (o * _rstd(o) * g_ref[...] * (hg * _sigmoid(hg))).astype(o_ref.dtype)

    blk = pl.BlockSpec((tr, w), lambda i, j: (i, j))
    dirs = [pl.BlockSpec((1, tr, w), lambda i, j, d=d: (d, i, j)) for d in range(2)]
    return pl.pallas_call(
        body, name=name, grid=(n // tr, HG_DIM // w),
        in_specs=dirs + [pl.BlockSpec((tr, w), lambda i, j: (i, OFF_HG // w + j)), pl.BlockSpec((1, HG_HEAD_DIM), lambda i, j: (0, 0))],
        out_specs=blk, out_shape=SDS((n, HG_DIM), MXU_DTYPE), compiler_params=_cp("parallel", "parallel"))(o2, o2, p, g)


def _hg_post_bwd(o2, p, g, dcat, name):
    n = p.shape[0]
    tr = min(ROW_TILE, n)
    w = 2 * HG_HEAD_DIM

    def body(of_ref, ob_ref, hg_ref, g_ref, d_ref, do_ref, dhg_ref, dg_ref):
        @pl.when(pl.program_id(1) == 0)
        def _():
            dg_ref[...] = jnp.zeros_like(dg_ref)

        for j in range(2):
            sl = slice(j * HG_HEAD_DIM, (j + 1) * HG_HEAD_DIM)
            o = of_ref[0, :, sl] + ob_ref[0, :, sl]
            hg = hg_ref[:, sl]
            d = d_ref[:, sl].astype(F32)
            sg = _sigmoid(hg)
            on = o * _rstd(o) * g_ref[...]
            dhg_ref[:, sl] = (d * on * sg * (1.0 + hg * (1.0 - sg))).astype(dhg_ref.dtype)
            dx, dg = _rms_bwd(o, g_ref[...], d * hg * sg)
            do_ref[:, sl] = dx
            dg_ref[0, :, sl] += dg

    blk = pl.BlockSpec((tr, w), lambda j, i: (i, j))
    dirs = [pl.BlockSpec((1, tr, w), lambda j, i, d=d: (d, i, j)) for d in range(2)]
    return pl.pallas_call(
        body, name=name, grid=(HG_DIM // w, n // tr),
        in_specs=dirs + [pl.BlockSpec((tr, w), lambda j, i: (i, OFF_HG // w + j)), pl.BlockSpec((1, HG_HEAD_DIM), lambda j, i: (0, 0)),
                         pl.BlockSpec((tr, w), lambda j, i: (i, ATT_Q_DIM // w + j))],
        out_specs=[blk, blk, pl.BlockSpec((1, 1, w), lambda j, i: (j, 0, 0))],
        out_shape=[SDS((n, HG_DIM), F32), SDS((n, HG_DIM), MXU_DTYPE), SDS((HG_DIM // w, 1, w), F32)],
        compiler_params=_cp("parallel", "arbitrary"))(o2, o2, p, g, dcat)


XATT_TQ = 512


def _xattn_fwd(q, kv, name):
    n, nm = q.shape[0], kv.shape[0]
    tq = min(XATT_TQ, n)
    scale = X_HEAD_DIM ** -0.5

    def body(q_ref, k_ref, v_ref, o_ref):
        s = _dot(q_ref[...], k_ref[...], "nt") * scale
        e = jnp.exp(s - jnp.max(s, axis=-1, keepdims=True))
        o_ref[...] = _dot(e / jnp.sum(e, axis=-1, keepdims=True), v_ref[...]).astype(o_ref.dtype)

    qb = pl.BlockSpec((tq, X_HEAD_DIM), lambda h, i: (i, h))
    return pl.pallas_call(
        body, name=name, grid=(X_HEADS, n // tq),
        in_specs=[qb, pl.BlockSpec((nm, X_HEAD_DIM), lambda h, i: (0, h)), pl.BlockSpec((nm, X_HEAD_DIM), lambda h, i: (0, X_HEADS + h))],
        out_specs=qb, out_shape=SDS(q.shape, MXU_DTYPE), compiler_params=_cp("parallel", "parallel"))(q, kv, kv)


def _xattn_bwd(q, kv, do, name):
    n, nm = q.shape[0], kv.shape[0]
    tq = min(XATT_TQ, n)
    scale = X_HEAD_DIM ** -0.5

    def body(q_ref, k_ref, v_ref, do_ref, dq_ref, dk_ref, dv_ref):
        @pl.when(pl.program_id(1) == 0)
        def _():
            dk_ref[...] = jnp.zeros_like(dk_ref)
            dv_ref[...] = jnp.zeros_like(dv_ref)

        qv, dov = q_ref[...], do_ref[...]
        s = _dot(qv, k_ref[...], "nt") * scale
        e = jnp.exp(s - jnp.max(s, axis=-1, keepdims=True))
        p = e / jnp.sum(e, axis=-1, keepdims=True)
        dp = _dot(dov, v_ref[...], "nt")
        ds = p * (dp - jnp.sum(p * dp, axis=-1, keepdims=True)) * scale
        dq_ref[...] = _dot(ds, k_ref[...]).astype(dq_ref.dtype)
        dk_ref[...] += _dot(ds, qv, "tn")
        dv_ref[...] += _dot(p, dov, "tn")

    qb = pl.BlockSpec((tq, X_HEAD_DIM), lambda h, i: (i, h))
    kb = pl.BlockSpec((nm, X_HEAD_DIM), lambda h, i: (0, h))
    return pl.pallas_call(
        body, name=name, grid=(X_HEADS, n // tq),
        in_specs=[qb, kb, pl.BlockSpec((nm, X_HEAD_DIM), lambda h, i: (0, X_HEADS + h)), qb], out_specs=[qb, kb, kb],
        out_shape=[SDS(q.shape, MXU_DTYPE), SDS((nm, X_HEADS * X_HEAD_DIM), F32), SDS((nm, X_HEADS * X_HEAD_DIM), F32)],
        compiler_params=_cp("parallel", "arbitrary"))(q, kv, kv, do)


def _pair_cols(a):
    lead = a.shape[:-1]
    return a.reshape(lead + (2, FF_BLOCKS, FF_COLS)).swapaxes(-3, -2).reshape(lead + (2 * D_FF,))


def _unpair_cols(a):
    lead = a.shape[:-1]
    return a.reshape(lead + (FF_BLOCKS, 2, FF_COLS)).swapaxes(-3, -2).reshape(lead + (2 * D_FF,))


def _shift_rows(u, down):
    n = u.shape[0]
    row = lax.broadcasted_iota(jnp.int32, u.shape, 0)
    if down:
        return jnp.where(row == 0, 0.0, pltpu.roll(u, 1, axis=0))
    return jnp.where(row == n - 1, 0.0, pltpu.roll(u, n - 1, axis=0))


def _conv(u, w, b):
    return b + _shift_rows(u, True) * w[0:1, :] + u * w[1:2, :] + _shift_rows(u, False) * w[2:3, :]


def _conv_gate(u, cw, cb, name):
    n = u.shape[0]

    def body(u_ref, w_ref, b_ref, o_ref):
        acc = _conv(u_ref[...], w_ref[...], b_ref[...])
        gate, val = acc[:, :FF_COLS], acc[:, FF_COLS:]
        o_ref[...] = (gate * _sigmoid(gate) * val).astype(o_ref.dtype)

    slab = lambda rows: pl.BlockSpec((rows, 2 * FF_COLS), lambda j: (0, j))
    return pl.pallas_call(
        body, name=name, grid=(FF_BLOCKS,), in_specs=[slab(n), slab(3), slab(1)],
        out_specs=pl.BlockSpec((n, FF_COLS), lambda j: (0, j)), out_shape=SDS((n, D_FF), MXU_DTYPE),
        compiler_params=_cp("parallel"))(u, cw, cb)


def _conv_gate_bwd(u, cw, cb, da, name):
    n = u.shape[0]

    def body(u_ref, w_ref, b_ref, da_ref, du_ref, dw_ref, db_ref):
        u, w = u_ref[...], w_ref[...]
        acc = _conv(u, w, b_ref[...])
        gate, val = acc[:, :FF_COLS], acc[:, FF_COLS:]
        sg = _sigmoid(gate)
        dav = da_ref[...].astype(F32)
        dacc = jnp.concatenate([dav * val * sg * (1.0 + gate * (1.0 - sg)), dav * gate * sg], axis=1)
        nxt, prv = _shift_rows(dacc, False), _shift_rows(dacc, True)
        du_ref[...] = (nxt * w[0:1, :] + dacc * w[1:2, :] + prv * w[2:3, :]).astype(du_ref.dtype)
        db_ref[...] = jnp.sum(dacc, axis=0, keepdims=True)
        dw_ref[0:1, :] = jnp.sum(nxt * u, axis=0, keepdims=True)
        dw_ref[1:2, :] = jnp.sum(dacc * u, axis=0, keepdims=True)
        dw_ref[2:3, :] = jnp.sum(prv * u, axis=0, keepdims=True)

    slab = lambda rows: pl.BlockSpec((rows, 2 * FF_COLS), lambda j: (0, j))
    return pl.pallas_call(
        body, name=name, grid=(FF_BLOCKS,), in_specs=[slab(n), slab(3), slab(1), pl.BlockSpec((n, FF_COLS), lambda j: (0, j))],
        out_specs=[slab(n), slab(3), slab(1)],
        out_shape=[SDS((n, 2 * D_FF), MXU_DTYPE), SDS((3, 2 * D_FF), F32), SDS((1, 2 * D_FF), F32)],
        compiler_params=_cp("parallel"))(u, cw, cb, da)


def _adamw(w, g, m, v, name):
    r, c = w.shape
    tr = r if r <= 512 else 256 if r % 256 == 0 else 88
    assert r % tr == 0, (name, r, tr)

    def body(w_ref, g_ref, m_ref, v_ref, d_ref, mo_ref, vo_ref):
        gv = g_ref[...]
        mn = ADAM_B1 * m_ref[...] + (1.0 - ADAM_B1) * gv
        vn = ADAM_B2 * v_ref[...] + (1.0 - ADAM_B2) * gv * gv
        m_hat = mn / (1.0 - ADAM_B1 ** ADAM_STEP)
        v_hat = vn / (1.0 - ADAM_B2 ** ADAM_STEP)
        d_ref[...] = -ADAM_LR * (m_hat / (jnp.sqrt(v_hat) + ADAM_EPS) + ADAM_WD * w_ref[...])
        mo_ref[...] = mn
        vo_ref[...] = vn

    blk = pl.BlockSpec((tr, c), lambda i: (i, 0))
    out = SDS((r, c), F32)
    return pl.pallas_call(body, name=name, grid=(r // tr,), in_specs=[blk] * 4, out_specs=[blk] * 3, out_shape=[out] * 3,
                          compiler_params=_cp("parallel"))(w, g, m, v)


def _half_tile(h):
    tr = h if h <= 512 else 256 if h % 256 == 0 else 176
    assert h % tr == 0, (h, tr)
    return tr


def _add_halves(g, r, core, name):
    s, h, c = r.shape
    tr = _half_tile(h)
    steps = h // tr

    def body(ix_ref, g_ref, r_ref, o_ref):
        o_ref[...] = (g_ref[...].astype(F32) + r_ref[...].astype(F32)).astype(o_ref.dtype)

    blk = pl.BlockSpec((1, tr, c), lambda i, j, ix: (i, j, 0))
    grid_spec = pltpu.PrefetchScalarGridSpec(
        num_scalar_prefetch=1, grid=(s, steps),
        in_specs=[pl.BlockSpec((1, tr, c), lambda i, j, ix: (i, ix[0] * steps + j, 0)), blk], out_specs=blk)
    return pl.pallas_call(body, name=name, grid_spec=grid_spec, out_shape=SDS(r.shape, WIRE_DTYPE),
                          compiler_params=_cp("parallel", "parallel"))(core.reshape(1), g, r)


def _sum_chips(own, recv, me, core, name):
    _, h, c = own.shape
    tr = _half_tile(h)

    def body(ix_ref, own_ref, recv_ref, o_ref):
        acc = own_ref[0].astype(F32)
        for j in range(3):
            acc = acc + recv_ref[j].astype(F32)
        o_ref[0] = acc

    grid_spec = pltpu.PrefetchScalarGridSpec(
        num_scalar_prefetch=1, grid=(h // tr,),
        in_specs=[pl.BlockSpec((1, tr, c), lambda i, ix: (ix[0], i, 0)), pl.BlockSpec((3, tr, c), lambda i, ix: (0, i, 0))],
        out_specs=pl.BlockSpec((1, tr, c), lambda i, ix: (ix[1], i, 0)))
    return pl.pallas_call(body, name=name, grid_spec=grid_spec, out_shape=SDS((2, h, c), F32),
                          compiler_params=_cp("parallel"))(jnp.stack([me, core]), own, recv)


ANY = pl.BlockSpec(memory_space=pl.ANY)


def _place():
    x, y, c = lax.axis_index("x"), lax.axis_index("y"), lax.axis_index("c")
    return x, y, c, [(1 - x, y), (x, 1 - y), (1 - x, 1 - y)]


def _gather_shards(shards, name):
    nt = len(shards)

    def body(*refs):
        ins, outs = refs[:nt], refs[nt:2 * nt]
        send, recv, fsend, frecv, osend, orecv = refs[2 * nt:]
        x, y, c, chips = _place()
        me = 2 * x + y

        def half(t, chip, cc):
            h = ins[t].shape[0] // 2
            return outs[t].at[chip, pl.ds(cc * h, h)]

        def ici(t, j):
            cx, cy = chips[j]
            h = ins[t].shape[0] // 2
            return pltpu.make_async_remote_copy(src_ref=ins[t].at[pl.ds(c * h, h)], dst_ref=half(t, me, c),
                                                send_sem=send.at[t, j], recv_sem=recv.at[t, j], device_id=(cx, cy, c), device_id_type=MESH)

        def landed(t, j):
            cx, cy = chips[j]
            blk = half(t, 2 * cx + cy, c)
            return pltpu.make_async_remote_copy(src_ref=blk, dst_ref=blk, send_sem=send.at[t, j], recv_sem=recv.at[t, j],
                                                device_id=(cx, cy, c), device_id_type=MESH)

        def d2d(t, j, cc):
            cx, cy = chips[j]
            blk = half(t, 2 * cx + cy, cc)
            return pltpu.make_async_remote_copy(src_ref=blk, dst_ref=blk, send_sem=fsend.at[t, j], recv_sem=frecv.at[t, j],
                                                device_id=(x, y, 1 - c), device_id_type=MESH)

        own = [pltpu.make_async_remote_copy(src_ref=ins[t], dst_ref=outs[t].at[me], send_sem=osend.at[t], recv_sem=orecv.at[t],
                                            device_id=(x, y, 1 - c), device_id_type=MESH) for t in range(nt)]
        for t in range(nt):
            for j in range(3):
                ici(t, j).start()
        for cp in own:
            cp.start()
        for t in range(nt):
            for j in range(3):
                landed(t, j).wait_recv()
                d2d(t, j, c).start()
        for t in range(nt):
            for j in range(3):
                d2d(t, j, 1 - c).wait_recv()
        for t in range(nt):
            for j in range(3):
                ici(t, j).wait_send()
                d2d(t, j, c).wait_send()
        for cp in own:
            cp.wait()

    return pl.pallas_call(
        body, name=name, in_specs=[ANY] * nt, out_specs=[ANY] * nt,
        out_shape=[SDS((4,) + s.shape, s.dtype) for s in shards],
        scratch_shapes=[pltpu.SemaphoreType.DMA((nt, 3))] * 4 + [pltpu.SemaphoreType.DMA((nt,))] * 2,
        compiler_params=pltpu.CompilerParams(has_side_effects=True))(*shards)


def _swap_halves(grads, name):
    nt = len(grads)

    def body(*refs):
        ins, outs = refs[:nt], refs[nt:2 * nt]
        send, recv = refs[2 * nt:]
        x, y, c, _ = _place()
        cps = []
        for t in range(nt):
            h = ins[t].shape[1] // 2
            cps.append(pltpu.make_async_remote_copy(src_ref=ins[t].at[pl.ds(0, 4), pl.ds((1 - c) * h, h)], dst_ref=outs[t], send_sem=send.at[t],
                                                    recv_sem=recv.at[t], device_id=(x, y, 1 - c), device_id_type=MESH))
        for cp in cps:
            cp.start()
        for cp in cps:
            cp.wait()

    return pl.pallas_call(
        body, name=name, in_specs=[ANY] * nt, out_specs=[ANY] * nt,
        out_shape=[SDS((g.shape[0], g.shape[1] // 2, g.shape[2]), g.dtype) for g in grads],
        scratch_shapes=[pltpu.SemaphoreType.DMA((nt,))] * 2,
        compiler_params=pltpu.CompilerParams(has_side_effects=True))(*grads)


def _send_to_owners(parts, name):
    nt = len(parts)

    def body(*refs):
        ins, outs = refs[:nt], refs[nt:2 * nt]
        send, recv = refs[2 * nt:]
        x, y, c, chips = _place()

        def ici(t, j):
            cx, cy = chips[j]
            return pltpu.make_async_remote_copy(src_ref=ins[t].at[2 * cx + cy], dst_ref=outs[t].at[j], send_sem=send.at[t, j],
                                                recv_sem=recv.at[t, j], device_id=(cx, cy, c), device_id_type=MESH)

        for t in range(nt):
            for j in range(3):
                ici(t, j).start()
        for t in range(nt):
            for j in range(3):
                ici(t, j).wait()

    return pl.pallas_call(
        body, name=name, in_specs=[ANY] * nt, out_specs=[ANY] * nt, out_shape=[SDS((3,) + p.shape[1:], p.dtype) for p in parts],
        scratch_shapes=[pltpu.SemaphoreType.DMA((nt, 3))] * 2,
        compiler_params=pltpu.CompilerParams(has_side_effects=True))(*parts)


def _join_halves(bufs, name):
    nt = len(bufs)

    def body(*refs):
        outs = refs[nt:2 * nt]
        send, recv = refs[2 * nt:]
        x, y, c, _ = _place()
        cps = [pltpu.make_async_remote_copy(src_ref=outs[t].at[c], dst_ref=outs[t].at[c], send_sem=send.at[t], recv_sem=recv.at[t],
                                            device_id=(x, y, 1 - c), device_id_type=MESH) for t in range(nt)]
        for cp in cps:
            cp.start()
        for t in range(nt):
            theirs = outs[t].at[1 - c]
            pltpu.make_async_remote_copy(src_ref=theirs, dst_ref=theirs, send_sem=send.at[t], recv_sem=recv.at[t],
                                         device_id=(x, y, 1 - c), device_id_type=MESH).wait_recv()
        for cp in cps:
            cp.wait_send()

    return pl.pallas_call(
        body, name=name, in_specs=[ANY] * nt, out_specs=[ANY] * nt, out_shape=[SDS(b.shape, b.dtype) for b in bufs],
        input_output_aliases={t: t for t in range(nt)},
        scratch_shapes=[pltpu.SemaphoreType.DMA((nt,))] * 2,
        compiler_params=pltpu.CompilerParams(has_side_effects=True))(*bufs)


def _exchange_small(v, reduce, name):
    rows = v.shape[0]

    def body(v_ref, o_ref, buf, send, recv):
        x, y, c, _ = _place()
        me = 4 * x + 2 * y + c
        buf[me] = v_ref[...]

        def peer(dx, dy, dc):
            return (1 - x if dx else x, 1 - y if dy else y, 1 - c if dc else c)

        peers = [(dx, dy, dc) for dx in range(2) for dy in range(2) for dc in range(2) if (dx, dy, dc) != (0, 0, 0)]
        cps = []
        for j, (dx, dy, dc) in enumerate(peers):
            cps.append(pltpu.make_async_remote_copy(src_ref=v_ref, dst_ref=buf.at[me], send_sem=send.at[j], recv_sem=recv.at[j],
                                                    device_id=peer(dx, dy, dc), device_id_type=MESH))
        for cp in cps:
            cp.start()
        for j, (dx, dy, dc) in enumerate(peers):
            px, py, pc = peer(dx, dy, dc)
            blk = buf.at[4 * px + 2 * py + pc]
            pltpu.make_async_remote_copy(src_ref=blk, dst_ref=blk, send_sem=send.at[j], recv_sem=recv.at[j],
                                         device_id=(px, py, pc), device_id_type=MESH).wait_recv()
        for cp in cps:
            cp.wait_send()
        if reduce:
            acc = buf[0]
            for j in range(1, 8):
                acc = acc + buf[j]
            o_ref[...] = acc
        else:
            o_ref[...] = buf[...]

    vm = pl.BlockSpec(memory_space=pltpu.VMEM)
    return pl.pallas_call(
        body, name=name, in_specs=[vm], out_specs=vm, out_shape=SDS((rows, 128) if reduce else (8, rows, 128), F32),
        scratch_shapes=[pltpu.VMEM((8, rows, 128), F32), pltpu.SemaphoreType.DMA((7,)), pltpu.SemaphoreType.DMA((7,))],
        compiler_params=pltpu.CompilerParams(has_side_effects=True))(v)


def _pack_small(parts):
    flat = jnp.concatenate([p.reshape(-1) for p in parts])
    total = flat.shape[0]
    rows = -(-total // 1024) * 8
    return jnp.pad(flat, (0, rows * 128 - total)).reshape(rows, 128)


def _unpack_small(packed, shapes):
    flat = packed.reshape(-1)
    out, off = [], 0
    for s in shapes:
        size = int(np.prod(s))
        out.append(flat[off:off + size].reshape(s))
        off += size
    return out


def _local_step(x, mem, target, wts, gains, conv_w, conv_b, hg_lb):
    n = x.shape[0]
    w_in, w_out, w_xq, w_xkv, w_xo, w_up, w_down = (wts[k] for k in ("w_in", "w_out", "w_xq", "w_xkv", "w_xo", "w_up", "w_down"))
    cos, sin = _rope_tables(n)
    seg = _hg_segments()
    gp, gn = _hg_pair_sums()
    gq2 = jnp.tile(gains["q_norm_g"], (1, 2))
    gk2 = jnp.tile(gains["k_norm_g"], (1, 2))
    a0 = hg_lb[:, 0:1, :]
    a1 = hg_lb[:, 1:2, :]

    p, h1 = _norm_mm(x, gains["pre_mix_g"], w_in, F32, 512, 1664, "in_proj")
    qr, kr = _qk_prep(p, gq2, gk2, cos, sin, "qk_prep")
    heads = lambda a: a.reshape(n, ATT_KV_HEADS, ATT_HEAD_DIM).transpose(1, 0, 2)
    kh = heads(kr)
    vh = heads(p[:, OFF_AV:OFF_AV + ATT_KV_DIM].astype(MXU_DTYPE))
    att = _attn_fwd(qr, kh, vh, "attn_fwd")
    o2 = _hgrn_fwd(p, a0, a1, seg, "hgrn_fwd")
    rec = _hg_post(o2, p, gains["hg_out_norm_g"], "hg_post")
    cat = jnp.concatenate([att, rec], axis=1)
    mixed = _mm(cat, w_out, "nn", F32, 512, 1024, "out_proj")
    x1 = _resid_norm(x, mixed, gains["post_mix_g"], "mix_resid")
    xq, h2 = _norm_mm(x1, gains["pre_x_g"], w_xq, MXU_DTYPE, 512, 1024, "xq_proj")
    kv, mn = _norm_mm(mem, gains["mem_norm_g"], w_xkv, MXU_DTYPE, 256, 2048, "xkv_proj")
    ox = _xattn_fwd(xq, kv, "xattn_fwd")
    xo = _mm(ox, w_xo, "nn", F32, 512, 1024, "xo_proj")
    x2 = _resid_norm(x1, xo, gains["post_x_g"], "x_resid")
    u, h3 = _norm_mm(x2, gains["pre_ffn_g"], w_up, F32, 512, 512, "up_proj")
    act = _conv_gate(u, conv_w, conv_b, "conv_gate")
    dn = _mm(act, w_down, "nn", F32, 512, 1024, "down_proj")
    d3, loss = _final_loss(x2, dn, gains["post_ffn_g"], target, "ffn_resid_loss")

    gm, gs = {}, {}
    d_dn, gs["post_ffn_g"] = _norm_bwd(dn, gains["post_ffn_g"], d3, None, MXU_DTYPE, "ffn_post_bwd")
    gm["w_down"] = _mm(act, d_dn, "tn", WIRE_DTYPE, 1408, 1024, "down_dw")
    d_act = _mm(d_dn, w_down, "nt", F32, 512, 1408, "down_dx")
    du, gs["conv_w"], gs["conv_b"] = _conv_gate_bwd(u, conv_w, conv_b, d_act, "conv_gate_bwd")
    gm["w_up"] = _mm(h3, du, "tn", WIRE_DTYPE, 512, 512, "up_dw")
    d_h3 = _mm(du, w_up, "nt", F32, 512, 512, "up_dx")
    d2, gs["pre_ffn_g"] = _norm_bwd(x2, gains["pre_ffn_g"], d_h3, d3, F32, "ffn_pre_bwd")
    d_xo, gs["post_x_g"] = _norm_bwd(xo, gains["post_x_g"], d2, None, MXU_DTYPE, "x_post_bwd")
    gm["w_xo"] = _mm(ox, d_xo, "tn", WIRE_DTYPE, 512, 1024, "xo_dw")
    d_ox = _mm(d_xo, w_xo, "nt", MXU_DTYPE, 512, 1024, "xo_dx")
    d_xq, d_k, d_v = _xattn_bwd(xq, kv, d_ox, "xattn_bwd")
    d_kv = jnp.concatenate([d_k, d_v], axis=1).astype(MXU_DTYPE)
    gm["w_xq"] = _mm(h2, d_xq, "tn", WIRE_DTYPE, 512, 1024, "xq_dw")
    d_h2 = _mm(d_xq, w_xq, "nt", F32, 512, 1024, "xq_dx")
    gm["w_xkv"] = _mm(mn, d_kv, "tn", WIRE_DTYPE, 512, 1024, "xkv_dw")
    d_mn = _mm(d_kv, w_xkv, "nt", F32, 256, 1024, "xkv_dx")
    _, gs["mem_norm_g"] = _norm_bwd(mem, gains["mem_norm_g"], d_mn, None, MXU_DTYPE, "mem_norm_bwd")
    d1, gs["pre_x_g"] = _norm_bwd(x1, gains["pre_x_g"], d_h2, d2, F32, "x_pre_bwd")
    d_mixed, gs["post_mix_g"] = _norm_bwd(mixed, gains["post_mix_g"], d1, None, MXU_DTYPE, "mix_post_bwd")
    gm["w_out"] = _mm(cat, d_mixed, "tn", WIRE_DTYPE, 512, 1024, "out_dw")
    d_cat = _mm(d_mixed, w_out, "nt", MXU_DTYPE, 512, 1024, "out_dx")
    d_o, d_hg, dg_hg = _hg_post_bwd(o2, p, gains["hg_out_norm_g"], d_cat, "hg_post_bwd")
    gs["hg_out_norm_g"] = dg_hg.reshape(HG_HEADS, HG_HEAD_DIM).sum(axis=0, keepdims=True)
    dhq2, dlf_q, s0 = _hgrn_bwd_q(p, a0, a1, seg, gp, d_o, "hgrn_bwd_q")
    dz2, dhv2, dlb = _hgrn_bwd_kv(p, a0, a1, seg, gn, d_o, dlf_q, s0, "hgrn_bwd_kv")
    lb = jax.nn.sigmoid(a0 - a1)
    da0 = dlb * lb * (1.0 - lb)
    gs["hg_lb"] = jnp.concatenate([da0, -da0], axis=1)
    d_qr, d_kh, d_vh = _attn_bwd(qr, kh, vh, d_cat, "attn_bwd")
    unheads = lambda a: a.transpose(1, 0, 2).reshape(n, ATT_KV_DIM)
    d_aq, d_ak, dgq, dgk = _qk_prep_bwd(p, gq2, gk2, cos, sin, d_qr, unheads(d_kh), "qk_prep_bwd")
    gs["q_norm_g"] = dgq.reshape(ATT_HEADS, ATT_HEAD_DIM).sum(axis=0, keepdims=True)
    gs["k_norm_g"] = dgk.reshape(ATT_KV_HEADS, ATT_HEAD_DIM).sum(axis=0, keepdims=True)
    d_p = jnp.concatenate([d_aq, d_ak, unheads(d_vh).astype(MXU_DTYPE), (dhq2[0] + dhq2[1]).astype(MXU_DTYPE),
                           dz2[0].astype(MXU_DTYPE), dz2[1].astype(MXU_DTYPE), (dhv2[0] + dhv2[1]).astype(MXU_DTYPE), d_hg], axis=1)
    gm["w_in"] = _mm(h1, d_p, "tn", WIRE_DTYPE, 512, 1664, "in_dw")
    d_h1 = _mm(d_p, w_in, "nt", F32, 512, 1024, "in_dx")
    grad_x, gs["pre_mix_g"] = _norm_bwd(x, gains["pre_mix_g"], d_h1, d1, F32, "mix_pre_bwd")
    return loss, grad_x, gm, gs


MATS = ("w_in", "w_out", "w_xq", "w_xkv", "w_xo", "w_up", "w_down")
COL_SHARDED = ("w_in", "w_xkv", "w_up")
GAINS = ("pre_mix_g", "q_norm_g", "k_norm_g", "hg_out_norm_g", "post_mix_g", "pre_x_g", "mem_norm_g", "post_x_g", "pre_ffn_g", "post_ffn_g")
WEIGHTS = ('pre_mix_g', 'w_in', 'q_norm_g', 'k_norm_g', 'hg_lb', 'hg_out_norm_g', 'w_out', 'post_mix_g', 'pre_x_g', 'mem_norm_g', 'w_xq',
           'w_xkv', 'w_xo', 'post_x_g', 'pre_ffn_g', 'w_up', 'conv_w', 'conv_b', 'w_down', 'post_ffn_g')


def kernel(x, mem, pre_mix_g, w_in, q_norm_g, k_norm_g, hg_lb, hg_out_norm_g, w_out, post_mix_g, pre_x_g, mem_norm_g, w_xq, w_xkv, w_xo, post_x_g, pre_ffn_g, w_up, conv_w, conv_b, w_down, post_ffn_g, loss_target, m_pre_mix_g, m_w_in, m_q_norm_g, m_k_norm_g, m_hg_lb, m_hg_out_norm_g, m_w_out, m_post_mix_g, m_pre_x_g, m_mem_norm_g, m_w_xq, m_w_xkv, m_w_xo, m_post_x_g, m_pre_ffn_g, m_w_up, m_conv_w, m_conv_b, m_w_down, m_post_ffn_g, v_pre_mix_g, v_w_in, v_q_norm_g, v_k_norm_g, v_hg_lb, v_hg_out_norm_g, v_w_out, v_post_mix_g, v_pre_x_g, v_mem_norm_g, v_w_xq, v_w_xkv, v_w_xo, v_post_x_g, v_pre_ffn_g, v_w_up, v_conv_w, v_conv_b, v_w_down, v_post_ffn_g):
    args = dict(locals())
    w = {k: args[k] for k in WEIGHTS}
    m = {k: args["m_" + k] for k in WEIGHTS}
    v = {k: args["v_" + k] for k in WEIGHTS}
    chip = 2 * lax.axis_index("x") + lax.axis_index("y")
    core = lax.axis_index("c")

    shards = [w[k][0].astype(WIRE_DTYPE) for k in MATS]
    gathered = _gather_shards(shards, "gather_weights")
    wts = {}
    for k, g in zip(MATS, gathered):
        wts[k] = jnp.concatenate([g[s] for s in range(4)], axis=1) if k in COL_SHARDED else g.reshape(-1, g.shape[-1])
    wts["w_up"] = _pair_cols(wts["w_up"])
    small_in = _exchange_small(_pack_small([w["conv_w"][0], w["hg_lb"]]), False, "gather_small")
    cw_parts, lb_parts = [], []
    for s in range(4):
        cw_s, lb_s = _unpack_small(small_in[2 * s], [w["conv_w"][0].shape, w["hg_lb"].shape])
        cw_parts.append(cw_s)
        lb_parts.append(lb_s)
    conv_w_full = jnp.concatenate(cw_parts, axis=1)
    hg_lb_full = jnp.concatenate(lb_parts, axis=2)

    gains = {k: w[k] for k in GAINS}
    loss, grad_x, gm, gs = _local_step(x[0], mem[0], loss_target[0], wts, gains, _pair_cols(conv_w_full), _pair_cols(w["conv_b"]), hg_lb_full)
    loss = lax.psum(loss[0, 0], ("x", "y", "c"))
    gm["w_up"] = _unpair_cols(gm["w_up"])
    gs["conv_w"] = _unpair_cols(gs["conv_w"])
    gs["conv_b"] = _unpair_cols(gs["conv_b"])

    by_owner = []
    for k in MATS:
        g = gm[k]
        if k in COL_SHARDED:
            by_owner.append(g.reshape(g.shape[0], 4, g.shape[1] // 4).transpose(1, 0, 2))
        else:
            by_owner.append(g.reshape(4, g.shape[0] // 4, g.shape[1]))
    from_sibling = _swap_halves(by_owner, "grad_swap_halves")
    pair = [_add_halves(g, r, core, "grad_pair_" + k) for k, g, r in zip(MATS, by_owner, from_sibling)]
    from_chips = _send_to_owners(pair, "grad_to_owner")
    halves = [_sum_chips(own, recv, chip, core, "grad_sum_" + k) for k, own, recv in zip(MATS, pair, from_chips)]
    reduced = _join_halves(halves, "grad_join_halves")
    grads = {k: r.reshape(1, -1, r.shape[-1]) for k, r in zip(MATS, reduced)}

    small_names = GAINS + ("conv_b", "conv_w", "hg_lb")
    small_shapes = [gs[k].shape for k in small_names]
    summed = _unpack_small(_exchange_small(_pack_small([gs[k] for k in small_names]), True, "reduce_small"), small_shapes)
    for k, g in zip(small_names, summed):
        grads[k] = g
    ncw = w["conv_w"].shape[2]
    grads["conv_w"] = lax.dynamic_slice_in_dim(grads["conv_w"], chip * ncw, ncw, axis=1)[None]
    nlb = w["hg_lb"].shape[2]
    grads["hg_lb"] = lax.dynamic_slice_in_dim(grads["hg_lb"], chip * nlb, nlb, axis=2)

    delta, new_m, new_v = {}, {}, {}
    for k in WEIGHTS:
        shape = w[k].shape
        two_d = lambda a: a.reshape(-1, shape[-1])
        d, mo, vo = _adamw(two_d(w[k]), two_d(grads[k]), two_d(m[k]), two_d(v[k]), "adamw_" + k)
        delta[k], new_m[k], new_v[k] = d.reshape(shape), mo.reshape(shape), vo.reshape(shape)
        grads[k] = grads[k].reshape(shape)
    return (loss, grad_x[None], *[grads[k] for k in WEIGHTS], *[delta[k] for k in WEIGHTS],
            *[new_m[k] for k in WEIGHTS], *[new_v[k] for k in WEIGHTS])
```

```python
import functools

import numpy as np
import jax
import jax.numpy as jnp
from jax import lax
from jax.experimental import pallas as pl
from jax.experimental.pallas import tpu as pltpu

F32 = jnp.float32
MXU_DTYPE = jnp.bfloat16
WIRE_DTYPE = jnp.bfloat16
VMEM_LIMIT_BYTES = 56 * 1024 * 1024
EPS = 1e-6
MESH = pl.DeviceIdType.MESH

D_MODEL = 1024
GRID_W = 64
ATT_HEADS, ATT_KV_HEADS, ATT_HEAD_DIM = 8, 2, 64
ATT_GROUP = ATT_HEADS // ATT_KV_HEADS
ATT_Q_DIM, ATT_KV_DIM = 512, 128
ROPE_THETA = 10000.0
HG_HEADS, HG_HEAD_DIM, HG_DIM = 4, 128, 512
HG_CHUNK = 128
HG_LEVELS = 7
HG_PAIR = 2 * HG_HEAD_DIM
X_HEADS, X_HEAD_DIM = 4, 256
D_FF = 2816
FF_COLS = 256
FF_BLOCKS = D_FF // FF_COLS
N_IN = 3328
OFF_AQ, OFF_AK, OFF_AV, OFF_HQ, OFF_ZF, OFF_ZB, OFF_HI, OFF_HG = 0, 512, 640, 768, 1280, 1792, 2304, 2816

ADAM_LR, ADAM_B1, ADAM_B2, ADAM_EPS, ADAM_WD, ADAM_STEP = 0.001, 0.9, 0.999, 1e-08, 0.01, 10

SDS = jax.ShapeDtypeStruct


def _cp(*sem):
    return pltpu.CompilerParams(dimension_semantics=sem, vmem_limit_bytes=VMEM_LIMIT_BYTES)


def _dot(a, b, form="nn"):
    dims = {"nn": (((1,), (0,)), ((), ())), "nt": (((1,), (1,)), ((), ())), "tn": (((0,), (0,)), ((), ()))}[form]
    return lax.dot_general(a.astype(MXU_DTYPE), b.astype(MXU_DTYPE), dims, preferred_element_type=F32)


def _sigmoid(x):
    return 1.0 / (1.0 + jnp.exp(-x))


def _rstd(x):
    return lax.rsqrt(jnp.mean(x * x, axis=-1, keepdims=True) + EPS)


def _rms_bwd(x, g, dy):
    r = _rstd(x)
    xh = x * r
    dn = dy * g
    dx = r * (dn - xh * jnp.mean(dn * xh, axis=-1, keepdims=True))
    return dx, jnp.sum(dy * xh, axis=0, keepdims=True)


def _mm(a, b, form, out_dtype, tm, tn, name):
    if form == "nn":
        (m, k), n = a.shape, b.shape[1]
    elif form == "nt":
        (m, k), n = a.shape, b.shape[0]
    else:
        (k, m), n = a.shape, b.shape[1]
    tm, tn = min(tm, m), min(tn, n)
    assert m % tm == 0 and n % tn == 0, (name, m, n, tm, tn)

    def body(a_ref, b_ref, o_ref):
        o_ref[...] = _dot(a_ref[...], b_ref[...], form).astype(o_ref.dtype)

    a_spec = pl.BlockSpec((k, tm), lambda i, j: (0, i)) if form == "tn" else pl.BlockSpec((tm, k), lambda i, j: (i, 0))
    b_spec = pl.BlockSpec((tn, k), lambda i, j: (j, 0)) if form == "nt" else pl.BlockSpec((k, tn), lambda i, j: (0, j))
    return pl.pallas_call(
        body, name=name, grid=(m // tm, n // tn), in_specs=[a_spec, b_spec],
        out_specs=pl.BlockSpec((tm, tn), lambda i, j: (i, j)), out_shape=SDS((m, n), out_dtype),
        compiler_params=_cp("parallel", "parallel"))(a, b)


def _mm_nt_halves(a0, a1, b, out_dtype, tm, tn, name):
    m, kh = a0.shape
    n = b.shape[0]
    tm, tn = min(tm, m), min(tn, n)
    assert m % tm == 0 and n % tn == 0 and b.shape[1] == 2 * kh, (name, m, n, tm, tn)

    def body(a0_ref, a1_ref, b0_ref, b1_ref, o_ref):
        o_ref[...] = (_dot(a0_ref[...], b0_ref[...], "nt") + _dot(a1_ref[...], b1_ref[...], "nt")).astype(o_ref.dtype)

    a_spec = pl.BlockSpec((tm, kh), lambda i, j: (i, 0))
    return pl.pallas_call(
        body, name=name, grid=(m // tm, n // tn),
        in_specs=[a_spec, a_spec, pl.BlockSpec((tn, kh), lambda i, j: (j, 0)), pl.BlockSpec((tn, kh), lambda i, j: (j, 1))],
        out_specs=pl.BlockSpec((tm, tn), lambda i, j: (i, j)), out_shape=SDS((m, n), out_dtype),
        compiler_params=_cp("parallel", "parallel"))(a0, a1, b, b)


def _norm_mm(x, g, w, out_dtype, tm, tn, name):
    m, d = x.shape
    n = w.shape[1]
    tm, tn = min(tm, m), min(tn, n)
    assert m % tm == 0 and n % tn == 0, (name, m, n, tm, tn)

    def body(x_ref, g_ref, w_ref, o_ref, h_ref, hs):
        @pl.when(pl.program_id(1) == 0)
        def _():
            xv = x_ref[...]
            h = (xv * _rstd(xv) * g_ref[...]).astype(MXU_DTYPE)
            hs[...] = h
            h_ref[...] = h

        o_ref[...] = _dot(hs[...], w_ref[...]).astype(o_ref.dtype)

    return pl.pallas_call(
        body, name=name, grid=(m // tm, n // tn),
        in_specs=[pl.BlockSpec((tm, d), lambda i, j: (i, 0)), pl.BlockSpec((1, d), lambda i, j: (0, 0)),
                  pl.BlockSpec((d, tn), lambda i, j: (0, j))],
        out_specs=[pl.BlockSpec((tm, tn), lambda i, j: (i, j)), pl.BlockSpec((tm, d), lambda i, j: (i, 0))],
        out_shape=[SDS((m, n), out_dtype), SDS((m, d), MXU_DTYPE)],
        scratch_shapes=[pltpu.VMEM((tm, d), MXU_DTYPE)],
        compiler_params=_cp("parallel", "arbitrary"))(x, g, w)


ROW_TILE = 256


def _resid_norm(x, y, g, name):
    n, d = x.shape
    tr = min(ROW_TILE, n)

    def body(x_ref, y_ref, g_ref, o_ref):
        yv = y_ref[...]
        o_ref[...] = x_ref[...] + yv * _rstd(yv) * g_ref[...]

    row = pl.BlockSpec((tr, d), lambda i: (i, 0))
    return pl.pallas_call(
        body, name=name, grid=(n // tr,), in_specs=[row, row, pl.BlockSpec((1, d), lambda i: (0, 0))],
        out_specs=row, out_shape=SDS((n, d), F32), compiler_params=_cp("parallel"))(x, y, g)


def _norm_bwd(x, g, dy, res, out_dtype, name):
    n, d = x.shape
    tr = min(ROW_TILE, n)
    has_res = res is not None

    def body(*refs):
        x_ref, g_ref, dy_ref = refs[:3]
        dx_ref, dg_ref = refs[-2:]
        dx, dg = _rms_bwd(x_ref[...], g_ref[...], dy_ref[...].astype(F32))
        if has_res:
            dx = dx + refs[3][...]
        dx_ref[...] = dx.astype(dx_ref.dtype)

        @pl.when(pl.program_id(0) == 0)
        def _():
            dg_ref[...] = jnp.zeros_like(dg_ref)

        dg_ref[...] += dg

    row = pl.BlockSpec((tr, d), lambda i: (i, 0))
    vec = pl.BlockSpec((1, d), lambda i: (0, 0))
    ins = [x, g, dy] + ([res] if has_res else [])
    return pl.pallas_call(
        body, name=name, grid=(n // tr,), in_specs=[row, vec, row] + ([row] if has_res else []),
        out_specs=[row, vec], out_shape=[SDS((n, d), out_dtype), SDS((1, d), F32)],
        compiler_params=_cp("arbitrary"))(*ins)


def _final_loss(x, y, g, target, name):
    n, d = x.shape
    tr = min(ROW_TILE, n)

    def body(x_ref, y_ref, g_ref, t_ref, d_ref, l_ref):
        yv = y_ref[...]
        diff = x_ref[...] + yv * _rstd(yv) * g_ref[...] - t_ref[...]
        d_ref[...] = diff * (1.0 / d)

        @pl.when(pl.program_id(0) == 0)
        def _():
            l_ref[...] = jnp.zeros_like(l_ref)

        l_ref[...] += 0.5 * jnp.sum(jnp.mean(diff * diff, axis=-1, keepdims=True), axis=0, keepdims=True)

    row = pl.BlockSpec((tr, d), lambda i: (i, 0))
    return pl.pallas_call(
        body, name=name, grid=(n // tr,), in_specs=[row, row, pl.BlockSpec((1, d), lambda i: (0, 0)), row],
        out_specs=[row, pl.BlockSpec((1, 1), lambda i: (0, 0))], out_shape=[SDS((n, d), F32), SDS((1, 1), F32)],
        compiler_params=_cp("arbitrary"))(x, y, g, target)


def _rope_tables(n):
    pairs = ATT_HEAD_DIM // 4
    t = np.arange(n)
    inv = np.power(ROPE_THETA, -np.arange(pairs, dtype=np.float32) / pairs).astype(np.float32)
    ang = np.concatenate([(t // GRID_W)[:, None].astype(np.float32) * inv, (t % GRID_W)[:, None].astype(np.float32) * inv], axis=-1)
    cos = np.repeat(np.cos(ang), 2, axis=-1)
    sin = np.repeat(np.sin(ang), 2, axis=-1) * np.tile(np.array([-1.0, 1.0], np.float32), ATT_HEAD_DIM // 2)
    return jnp.asarray(np.tile(cos, 2), F32), jnp.asarray(np.tile(sin, 2), F32)


def _swap_pairs(x):
    lane = lax.broadcasted_iota(jnp.int32, x.shape, 1)
    return jnp.where((lane & 1) == 0, pltpu.roll(x, 127, axis=1), pltpu.roll(x, 1, axis=1))


def _head_mean(v):
    lane = lax.broadcasted_iota(jnp.int32, v.shape, 1)
    lo = jnp.where(lane < ATT_HEAD_DIM, v, 0.0)
    s0 = jnp.sum(lo, axis=-1, keepdims=True)
    s1 = jnp.sum(v - lo, axis=-1, keepdims=True)
    return jnp.where(lane < ATT_HEAD_DIM, s0, s1) * (1.0 / ATT_HEAD_DIM)


def _qk_prep(p, gq, gk, cos, sin, name):
    n = p.shape[0]
    tr = min(ROW_TILE, n)

    def one(xv, g, c, s):
        xn = xv * lax.rsqrt(_head_mean(xv * xv) + EPS) * g
        return xn * c + _swap_pairs(xn) * s

    def body(q_ref, k_ref, gq_ref, gk_ref, c_ref, s_ref, qo_ref, ko_ref):
        c, s = c_ref[...], s_ref[...]
        for j in range(ATT_Q_DIM // 128):
            qo_ref[:, j * 128:(j + 1) * 128] = one(q_ref[:, j * 128:(j + 1) * 128], gq_ref[...], c, s).astype(qo_ref.dtype)
        ko_ref[...] = one(k_ref[...], gk_ref[...], c, s).astype(ko_ref.dtype)

    vec = pl.BlockSpec((1, 128), lambda i: (0, 0))
    tab = pl.BlockSpec((tr, 128), lambda i: (i, 0))
    return pl.pallas_call(
        body, name=name, grid=(n // tr,),
        in_specs=[pl.BlockSpec((tr, ATT_Q_DIM), lambda i: (i, 0)), pl.BlockSpec((tr, 128), lambda i: (i, OFF_AK // 128)), vec, vec, tab, tab],
        out_specs=[pl.BlockSpec((tr, ATT_Q_DIM), lambda i: (i, 0)), tab],
        out_shape=[SDS((n, ATT_Q_DIM), MXU_DTYPE), SDS((n, ATT_KV_DIM), MXU_DTYPE)],
        compiler_params=_cp("parallel"))(p, p, gq, gk, cos, sin)


def _qk_prep_bwd(p, gq, gk, cos, sin, dq, dk, name):
    n = p.shape[0]
    tr = min(ROW_TILE, n)

    def one(xv, g, c, s, dout):
        dxn = dout * c + _swap_pairs(dout * s)
        r = lax.rsqrt(_head_mean(xv * xv) + EPS)
        xh = xv * r
        dn = dxn * g
        dx = r * (dn - xh * _head_mean(dn * xh))
        return dx, jnp.sum(dxn * xh, axis=0, keepdims=True)

    def body(q_ref, k_ref, gq_ref, gk_ref, c_ref, s_ref, dq_ref, dk_ref, dqo_ref, dko_ref, dgq_ref, dgk_ref):
        @pl.when(pl.program_id(0) == 0)
        def _():
            dgq_ref[...] = jnp.zeros_like(dgq_ref)
            dgk_ref[...] = jnp.zeros_like(dgk_ref)

        c, s = c_ref[...], s_ref[...]
        for j in range(ATT_Q_DIM // 128):
            sl = slice(j * 128, (j + 1) * 128)
            dx, dg = one(q_ref[:, sl], gq_ref[...], c, s, dq_ref[:, sl])
            dqo_ref[:, sl] = dx.astype(dqo_ref.dtype)
            dgq_ref[:, sl] += dg
        dx, dg = one(k_ref[...], gk_ref[...], c, s, dk_ref[...])
        dko_ref[...] = dx.astype(dko_ref.dtype)
        dgk_ref[...] += dg

    vec = pl.BlockSpec((1, 128), lambda i: (0, 0))
    tab = pl.BlockSpec((tr, 128), lambda i: (i, 0))
    qrow = pl.BlockSpec((tr, ATT_Q_DIM), lambda i: (i, 0))
    return pl.pallas_call(
        body, name=name, grid=(n // tr,),
        in_specs=[qrow, pl.BlockSpec((tr, 128), lambda i: (i, OFF_AK // 128)), vec, vec, tab, tab, qrow, tab],
        out_specs=[qrow, tab, pl.BlockSpec((1, ATT_Q_DIM), lambda i: (0, 0)), vec],
        out_shape=[SDS((n, ATT_Q_DIM), MXU_DTYPE), SDS((n, ATT_KV_DIM), MXU_DTYPE), SDS((1, ATT_Q_DIM), F32), SDS((1, 128), F32)],
        compiler_params=_cp("arbitrary"))(p, p, gq, gk, cos, sin, dq, dk)


ATT_TQ = 256


def _attn_fwd(q, k, v, name):
    n = q.shape[0]
    tq = min(ATT_TQ, n)
    scale = ATT_HEAD_DIM ** -0.5
    gw = ATT_GROUP * ATT_HEAD_DIM

    def body(q_ref, k_ref, v_ref, o_ref):
        kk, vv = k_ref[0], v_ref[0]
        outs = []
        for g in range(ATT_GROUP):
            s = _dot(q_ref[:, g * ATT_HEAD_DIM:(g + 1) * ATT_HEAD_DIM] * scale, kk, "nt")
            e = jnp.exp(s - jnp.max(s, axis=-1, keepdims=True))
            outs.append(_dot(e, vv) / jnp.sum(e, axis=-1, keepdims=True))
        o_ref[...] = jnp.concatenate(outs, axis=-1).astype(o_ref.dtype)

    kv = pl.BlockSpec((1, n, ATT_HEAD_DIM), lambda h, i: (h, 0, 0))
    return pl.pallas_call(
        body, name=name, grid=(ATT_KV_HEADS, n // tq),
        in_specs=[pl.BlockSpec((tq, gw), lambda h, i: (i, h)), kv, kv],
        out_specs=pl.BlockSpec((tq, gw), lambda h, i: (i, h)), out_shape=SDS((n, ATT_Q_DIM), MXU_DTYPE),
        compiler_params=_cp("parallel", "parallel"))(q, k, v)


def _attn_bwd(q, k, v, do, name):
    n = q.shape[0]
    tq = min(ATT_TQ, n)
    scale = ATT_HEAD_DIM ** -0.5
    gw = ATT_GROUP * ATT_HEAD_DIM

    def body(q_ref, k_ref, v_ref, do_ref, dq_ref, dk_ref, dv_ref):
        @pl.when(pl.program_id(1) == 0)
        def _():
            dk_ref[...] = jnp.zeros_like(dk_ref)
            dv_ref[...] = jnp.zeros_like(dv_ref)

        kk, vv = k_ref[0], v_ref[0]
        dqs = []
        dk_acc = jnp.zeros((n, ATT_HEAD_DIM), F32)
        dv_acc = jnp.zeros((n, ATT_HEAD_DIM), F32)
        for g in range(ATT_GROUP):
            sl = slice(g * ATT_HEAD_DIM, (g + 1) * ATT_HEAD_DIM)
            qg, dog = q_ref[:, sl] * scale, do_ref[:, sl].astype(F32)
            s = _dot(qg, kk, "nt")
            e = jnp.exp(s - jnp.max(s, axis=-1, keepdims=True))
            inv = 1.0 / jnp.sum(e, axis=-1, keepdims=True)
            delta = jnp.sum(dog * (_dot(e, vv) * inv), axis=-1, keepdims=True)
            dse = e * (_dot(dog, vv, "nt") - delta)
            dqs.append(_dot(dse, kk) * (inv * scale))
            dk_acc += _dot(dse, qg.astype(F32) * inv, "tn")
            dv_acc += _dot(e, dog * inv, "tn")
        dq_ref[...] = jnp.concatenate(dqs, axis=-1)
        dk_ref[0] += dk_acc
        dv_ref[0] += dv_acc

    kv = pl.BlockSpec((1, n, ATT_HEAD_DIM), lambda h, i: (h, 0, 0))
    qb = pl.BlockSpec((tq, gw), lambda h, i: (i, h))
    return pl.pallas_call(
        body, name=name, grid=(ATT_KV_HEADS, n // tq), in_specs=[qb, kv, kv, qb], out_specs=[qb, kv, kv],
        out_shape=[SDS((n, ATT_Q_DIM), F32), SDS((ATT_KV_HEADS, n, ATT_HEAD_DIM), F32), SDS((ATT_KV_HEADS, n, ATT_HEAD_DIM), F32)],
        compiler_params=_cp("parallel", "arbitrary"))(q, k, v, do)


def _both_directions(mats, axis):
    fwd = np.concatenate(mats, axis=axis).astype(np.float32)
    bwd = np.concatenate([m[::-1, ::-1] for m in mats], axis=axis).astype(np.float32)
    return jnp.asarray(np.stack([fwd, bwd]), MXU_DTYPE)


def _hg_segments():
    c = HG_CHUNK
    t = np.arange(c)[:, None]
    r = np.arange(c)[None, :]
    mats = [(r <= t)]
    for lev in range(HG_LEVELS):
        h = c >> (lev + 1)
        mid = (t // (2 * h)) * (2 * h) + h - 1
        hi = (t // h) % 2 == 1
        mats.append(np.where(hi, (r > mid) & (r <= t), (r > t) & (r <= mid)))
    mats.append(r > t)
    return _both_directions(mats, 0)


def _hg_pair_sums():
    c = HG_CHUNK
    r = np.arange(c)[:, None]
    t = np.arange(c)[None, :]
    gp, gn = [t >= r], [t < r]
    for lev in range(HG_LEVELS):
        sh = HG_LEVELS - 1 - lev
        same = (r >> sh) == (t >> sh)
        gp.append(same & (t >= r))
        gn.append(same & (t < r))
    return _both_directions(gp, 1), _both_directions(gn, 1)


def _split_dot(mat, x):
    hi = x.astype(MXU_DTYPE)
    lo = (x - hi.astype(F32)).astype(MXU_DTYPE)
    return _dot(mat, hi) + _dot(mat, lo)


def _hg_gates(hq, z, a0, a1):
    q = hq * _sigmoid(hq)
    sg = _sigmoid(z)
    lb = _sigmoid(a0 - a1)
    f = lb + (1.0 - lb) * sg
    k = (1.0 - lb) * (1.0 - sg)
    return q, f, k, sg, lb


def _hg_level_masks(mirrored):
    c = HG_CHUNK
    row = lax.broadcasted_iota(jnp.int32, (c, 1), 0)
    rr = lax.broadcasted_iota(jnp.int32, (c, c), 0)
    cc = lax.broadcasted_iota(jnp.int32, (c, c), 1)
    his, sames = [], []
    for lev in range(HG_LEVELS):
        sh = HG_LEVELS - 1 - lev
        his.append(jnp.logical_xor(((row >> sh) & 1) == 1, mirrored))
        sames.append((rr >> (sh + 1)) == (cc >> (sh + 1)))
    return his, sames, rr == cc


def _hg_intra(q, k, ex, masks):
    his, sames, eye = masks
    a = jnp.where(eye, jnp.sum(q * k, axis=-1, keepdims=True), 0.0)
    for lev in range(HG_LEVELS):
        e = ex[lev + 1]
        qs = jnp.where(his[lev], q * e, 0.0)
        ks = jnp.where(his[lev], 0.0, k * e)
        a = a + jnp.where(sames[lev], _dot(qs, ks, "nt"), 0.0)
    return a


def _hg_specs(n, with_time):
    c = HG_CHUNK
    nc = n // c

    def chunk(d, i):
        first = d if with_time else 1 - d
        return i + first * (nc - 1 - 2 * i)

    def pcols(off, dir_stride=0):
        return [pl.BlockSpec((c, HG_PAIR), lambda d, i, j=j: (chunk(d, i), off // HG_PAIR + dir_stride // HG_PAIR * d + j)) for j in range(2)]

    specs = dict(
        hq=pcols(OFF_HQ), v=pcols(OFF_HI), z=pcols(OFF_ZF, OFF_ZB - OFF_ZF),
        shared=pl.BlockSpec((c, HG_DIM), lambda d, i: (chunk(d, i), 0)),
        per_dir=pl.BlockSpec((1, c, HG_DIM), lambda d, i: (d, chunk(d, i), 0)),
        vec=pl.BlockSpec((1, 1, HG_DIM), lambda d, i: (d, 0, 0)),
        seg=pl.BlockSpec((1, (HG_LEVELS + 2) * c, c), lambda d, i: (d, 0, 0)),
        sums=pl.BlockSpec((1, c, (HG_LEVELS + 1) * c), lambda d, i: (d, 0, 0)),
        state=pl.BlockSpec((1, HG_HEADS, 1, HG_HEAD_DIM, HG_HEAD_DIM), lambda d, i: (d, 0, chunk(d, i), 0, 0)))
    return nc, specs


def _hg_head(refs, hh):
    off = (hh % 2) * HG_HEAD_DIM
    return refs[hh // 2][:, off:off + HG_HEAD_DIM]


def _hg_lanes(hh):
    return slice(hh * HG_HEAD_DIM, (hh + 1) * HG_HEAD_DIM)


def _hg_exps(seg_ref, f):
    lf = jnp.log(f)
    args = _split_dot(seg_ref[0], lf)
    c = HG_CHUNK
    return [jnp.exp(args[j * c:(j + 1) * c]) for j in range(HG_LEVELS + 2)]


def _hg_last_row(a, mirrored):
    return jnp.where(mirrored, a[0:1, :], a[HG_CHUNK - 1:HG_CHUNK, :])


def _hgrn_fwd(p, a0, a1, seg, name):
    n = p.shape[0]
    nc, sp = _hg_specs(n, True)

    def body(hq0, hq1, z0, z1, v0, v1, a0_ref, a1_ref, seg_ref, o_ref, st):
        @pl.when(pl.program_id(1) == 0)
        def _():
            st[...] = jnp.zeros_like(st)

        mirrored = pl.program_id(0) == 1
        masks = _hg_level_masks(mirrored)
        for hh in range(HG_HEADS):
            ln = _hg_lanes(hh)
            q, f, k, _, _ = _hg_gates(_hg_head((hq0, hq1), hh), _hg_head((z0, z1), hh), a0_ref[0, :, ln], a1_ref[0, :, ln])
            vv = _hg_head((v0, v1), hh)
            ex = _hg_exps(seg_ref, f)
            a = _hg_intra(q, k, ex, masks)
            s_t = st[hh]
            o_ref[0, :, ln] = _dot(a, vv) + _dot(q * ex[0], s_t, "nt")
            st[hh] = s_t * _hg_last_row(ex[0], mirrored) + _dot(vv, k * ex[HG_LEVELS + 1], "tn")

    return pl.pallas_call(
        body, name=name, grid=(2, nc), in_specs=sp["hq"] + sp["z"] + sp["v"] + [sp["vec"], sp["vec"], sp["seg"]],
        out_specs=sp["per_dir"], out_shape=SDS((2, n, HG_DIM), F32),
        scratch_shapes=[pltpu.VMEM((HG_HEADS, HG_HEAD_DIM, HG_HEAD_DIM), F32)],
        compiler_params=_cp("parallel", "arbitrary"))(p, p, p, p, p, p, a0, a1, seg)


def _hgrn_bwd_q(p, a0, a1, seg, gp, do, name):
    n = p.shape[0]
    nc, sp = _hg_specs(n, True)


    def body(hq0, hq1, z0, z1, v0, v1, a0_ref, a1_ref, seg_ref, gp_ref, do_ref, dhq_ref, dlf_ref, s0_ref, st):
        @pl.when(pl.program_id(1) == 0)
        def _():
            st[...] = jnp.zeros_like(st)

        mirrored = pl.program_id(0) == 1
        his, sames, eye = _hg_level_masks(mirrored)
        for hh in range(HG_HEADS):
            ln = _hg_lanes(hh)
            s_t = st[hh]
            s0_ref[0, hh, 0] = s_t
            hqv = _hg_head((hq0, hq1), hh)
            q, f, k, _, _ = _hg_gates(hqv, _hg_head((z0, z1), hh), a0_ref[0, :, ln], a1_ref[0, :, ln])
            vv, dov = _hg_head((v0, v1), hh), do_ref[:, ln]
            ex = _hg_exps(seg_ref, f)
            da = _dot(dov, vv, "nt")
            dq_inter = ex[0] * _dot(dov, s_t)
            dq = jnp.sum(dov * vv, axis=-1, keepdims=True) * k + dq_inter
            terms = [q * dq_inter]
            for lev in range(HG_LEVELS):
                e = ex[lev + 1]
                ks = jnp.where(his[lev], 0.0, k * e)
                part = jnp.where(his[lev], e, 0.0) * _dot(jnp.where(sames[lev], da, 0.0), ks)
                dq = dq + part
                terms.append(q * part)
            dlf_ref[0, :, ln] = _dot(gp_ref[0], jnp.concatenate(terms, axis=0))
            sq = _sigmoid(hqv)
            dhq_ref[0, :, ln] = dq * sq * (1.0 + hqv * (1.0 - sq))
            st[hh] = s_t * _hg_last_row(ex[0], mirrored) + _dot(vv, k * ex[HG_LEVELS + 1], "tn")

    out = SDS((2, n, HG_DIM), F32)
    return pl.pallas_call(
        body, name=name, grid=(2, nc),
        in_specs=sp["hq"] + sp["z"] + sp["v"] + [sp["vec"], sp["vec"], sp["seg"], sp["sums"], sp["shared"]],
        out_specs=[sp["per_dir"], sp["per_dir"], sp["state"]],
        out_shape=[out, out, SDS((2, HG_HEADS, nc, HG_HEAD_DIM, HG_HEAD_DIM), F32)],
        scratch_shapes=[pltpu.VMEM((HG_HEADS, HG_HEAD_DIM, HG_HEAD_DIM), F32)],
        compiler_params=_cp("parallel", "arbitrary"))(p, p, p, p, p, p, a0, a1, seg, gp, do)


def _hgrn_bwd_kv(p, a0, a1, seg, gn, do, dlf_q, s0, name):
    n = p.shape[0]
    nc, sp = _hg_specs(n, False)

    def body(hq0, hq1, z0, z1, v0, v1, a0_ref, a1_ref, seg_ref, gn_ref, do_ref, dlfq_ref, s0_ref, dz_ref, dv_ref, dlb_ref, rt):
        @pl.when(pl.program_id(1) == 0)
        def _():
            rt[...] = jnp.zeros_like(rt)
            dlb_ref[...] = jnp.zeros_like(dlb_ref)

        mirrored = pl.program_id(0) == 1
        masks = _hg_level_masks(mirrored)
        his, sames, eye = masks
        for hh in range(HG_HEADS):
            ln = _hg_lanes(hh)
            q, f, k, sg, lb = _hg_gates(_hg_head((hq0, hq1), hh), _hg_head((z0, z1), hh), a0_ref[0, :, ln], a1_ref[0, :, ln])
            vv, dov = _hg_head((v0, v1), hh), do_ref[:, ln]
            ex = _hg_exps(seg_ref, f)
            a = _hg_intra(q, k, ex, masks)
            da = _dot(dov, vv, "nt")
            r_t = rt[hh]
            k_end = k * ex[HG_LEVELS + 1]
            dv_ref[0, :, ln] = _dot(a, dov, "tn") + _dot(k_end, r_t, "nt")
            dk_inter = ex[HG_LEVELS + 1] * _dot(vv, r_t)
            dk = jnp.sum(dov * vv, axis=-1, keepdims=True) * q + dk_inter
            terms = [k * dk_inter]
            for lev in range(HG_LEVELS):
                e = ex[lev + 1]
                qs = jnp.where(his[lev], q * e, 0.0)
                part = jnp.where(his[lev], 0.0, e) * _dot(jnp.where(sames[lev], da, 0.0), qs, "tn")
                dk = dk + part
                terms.append(k * part)
            decay = _hg_last_row(ex[0], mirrored)
            rt[hh] = r_t * decay + _dot(dov, q * ex[0], "tn")
            later = decay * jnp.sum(s0_ref[0, hh, 0] * r_t, axis=0, keepdims=True)
            dlf = dlfq_ref[0, :, ln] + _dot(gn_ref[0], jnp.concatenate(terms, axis=0)) + later
            df = dlf / f - dk
            dz_ref[0, :, ln] = df * (1.0 - lb) * sg * (1.0 - sg)
            dlb_ref[0, :, ln] += jnp.sum(df * (1.0 - sg), axis=0, keepdims=True)

    out = SDS((2, n, HG_DIM), F32)
    return pl.pallas_call(
        body, name=name, grid=(2, nc),
        in_specs=sp["hq"] + sp["z"] + sp["v"] + [sp["vec"], sp["vec"], sp["seg"], sp["sums"], sp["shared"], sp["per_dir"], sp["state"]],
        out_specs=[sp["per_dir"], sp["per_dir"], sp["vec"]], out_shape=[out, out, SDS((2, 1, HG_DIM), F32)],
        scratch_shapes=[pltpu.VMEM((HG_HEADS, HG_HEAD_DIM, HG_HEAD_DIM), F32)],
        compiler_params=_cp("parallel", "arbitrary"))(p, p, p, p, p, p, a0, a1, seg, gn, do, dlf_q, s0)


def _hg_post(o2, p, g, name):
    n = p.shape[0]
    tr = min(ROW_TILE, n)
    w = 2 * HG_HEAD_DIM

    def body(of_ref, ob_ref, hg_ref, g_ref, o_ref):
        for j in range(2):
            sl = slice(j * HG_HEAD_DIM, (j + 1) * HG_HEAD_DIM)
            o = of_ref[0, :, sl] + ob_ref[0, :, sl]
            hg = hg_ref[:, sl]
            o_ref[:, sl] = (o * _rstd(o) * g_ref[...] * (hg * _sigmoid(hg))).astype(o_ref.dtype)

    blk = pl.BlockSpec((tr, w), lambda i, j: (i, j))
    dirs = [pl.BlockSpec((1, tr, w), lambda i, j, d=d: (d, i, j)) for d in range(2)]
    return pl.pallas_call(
        body, name=name, grid=(n // tr, HG_DIM // w),
        in_specs=dirs + [pl.BlockSpec((tr, w), lambda i, j: (i, OFF_HG // w + j)), pl.BlockSpec((1, HG_HEAD_DIM), lambda i, j: (0, 0))],
        out_specs=blk, out_shape=SDS((n, HG_DIM), MXU_DTYPE), compiler_params=_cp("parallel", "parallel"))(o2, o2, p, g)


def _hg_post_bwd(o2, p, g, dcat, name):
    n = p.shape[0]
    tr = min(ROW_TILE, n)
    w = 2 * HG_HEAD_DIM

    def body(of_ref, ob_ref, hg_ref, g_ref, d_ref, do_ref, dhg_ref, dg_ref):
        @pl.when(pl.program_id(1) == 0)
        def _():
            dg_ref[...] = jnp.zeros_like(dg_ref)

        for j in range(2):
            sl = slice(j * HG_HEAD_DIM, (j + 1) * HG_HEAD_DIM)
            o = of_ref[0, :, sl] + ob_ref[0, :, sl]
            hg = hg_ref[:, sl]
            d = d_ref[:, sl].astype(F32)
            sg = _sigmoid(hg)
            on = o * _rstd(o) * g_ref[...]
            dhg_ref[:, sl] = (d * on * sg * (1.0 + hg * (1.0 - sg))).astype(dhg_ref.dtype)
            dx, dg = _rms_bwd(o, g_ref[...], d * hg * sg)
            do_ref[:, sl] = dx
            dg_ref[0, :, sl] += dg

    blk = pl.BlockSpec((tr, w), lambda j, i: (i, j))
    dirs = [pl.BlockSpec((1, tr, w), lambda j, i, d=d: (d, i, j)) for d in range(2)]
    return pl.pallas_call(
        body, name=name, grid=(HG_DIM // w, n // tr),
        in_specs=dirs + [pl.BlockSpec((tr, w), lambda j, i: (i, OFF_HG // w + j)), pl.BlockSpec((1, HG_HEAD_DIM), lambda j, i: (0, 0)),
                         pl.BlockSpec((tr, w), lambda j, i: (i, ATT_Q_DIM // w + j))],
        out_specs=[blk, blk, pl.BlockSpec((1, 1, w), lambda j, i: (j, 0, 0))],
        out_shape=[SDS((n, HG_DIM), F32), SDS((n, HG_DIM), MXU_DTYPE), SDS((HG_DIM // w, 1, w), F32)],
        compiler_params=_cp("parallel", "arbitrary"))(o2, o2, p, g, dcat)


XATT_TQ = 512


def _xattn_fwd(q, kv, name):
    n, nm = q.shape[0], kv.shape[0]
    tq = min(XATT_TQ, n)
    scale = X_HEAD_DIM ** -0.5

    def body(q_ref, k_ref, v_ref, o_ref):
        s = _dot(q_ref[...], k_ref[...], "nt") * scale
        e = jnp.exp(s - jnp.max(s, axis=-1, keepdims=True))
        o_ref[...] = _dot(e / jnp.sum(e, axis=-1, keepdims=True), v_ref[...]).astype(o_ref.dtype)

    qb = pl.BlockSpec((tq, X_HEAD_DIM), lambda h, i: (i, h))
    return pl.pallas_call(
        body, name=name, grid=(X_HEADS, n // tq),
        in_specs=[qb, pl.BlockSpec((nm, X_HEAD_DIM), lambda h, i: (0, h)), pl.BlockSpec((nm, X_HEAD_DIM), lambda h, i: (0, X_HEADS + h))],
        out_specs=qb, out_shape=SDS(q.shape, MXU_DTYPE), compiler_params=_cp("parallel", "parallel"))(q, kv, kv)


def _xattn_bwd(q, kv, do, name):
    n, nm = q.shape[0], kv.shape[0]
    tq = min(XATT_TQ, n)
    scale = X_HEAD_DIM ** -0.5

    def body(q_ref, k_ref, v_ref, do_ref, dq_ref, dk_ref, dv_ref):
        @pl.when(pl.program_id(1) == 0)
        def _():
            dk_ref[...] = jnp.zeros_like(dk_ref)
            dv_ref[...] = jnp.zeros_like(dv_ref)

        qv, dov = q_ref[...], do_ref[...]
        s = _dot(qv, k_ref[...], "nt") * scale
        e = jnp.exp(s - jnp.max(s, axis=-1, keepdims=True))
        p = e / jnp.sum(e, axis=-1, keepdims=True)
        dp = _dot(dov, v_ref[...], "nt")
        ds = p * (dp - jnp.sum(p * dp, axis=-1, keepdims=True)) * scale
        dq_ref[...] = _dot(ds, k_ref[...]).astype(dq_ref.dtype)
        dk_ref[...] += _dot(ds, qv, "tn")
        dv_ref[...] += _dot(p, dov, "tn")

    qb = pl.BlockSpec((tq, X_HEAD_DIM), lambda h, i: (i, h))
    kb = pl.BlockSpec((nm, X_HEAD_DIM), lambda h, i: (0, h))
    return pl.pallas_call(
        body, name=name, grid=(X_HEADS, n // tq),
        in_specs=[qb, kb, pl.BlockSpec((nm, X_HEAD_DIM), lambda h, i: (0, X_HEADS + h)), qb], out_specs=[qb, kb, kb],
        out_shape=[SDS(q.shape, MXU_DTYPE), SDS((nm, X_HEADS * X_HEAD_DIM), F32), SDS((nm, X_HEADS * X_HEAD_DIM), F32)],
        compiler_params=_cp("parallel", "arbitrary"))(q, kv, kv, do)


def _shift_rows(u, down):
    n = u.shape[0]
    row = lax.broadcasted_iota(jnp.int32, u.shape, 0)
    if down:
        return jnp.where(row == 0, 0.0, pltpu.roll(u, 1, axis=0))
    return jnp.where(row == n - 1, 0.0, pltpu.roll(u, n - 1, axis=0))


def _conv(u, w, b):
    return b + _shift_rows(u, True) * w[0:1, :] + u * w[1:2, :] + _shift_rows(u, False) * w[2:3, :]


def _ff_specs(n):
    gate = lambda rows: pl.BlockSpec((rows, FF_COLS), lambda j: (0, j))
    val = lambda rows: pl.BlockSpec((rows, FF_COLS), lambda j: (0, FF_BLOCKS + j))
    return [gate(n), val(n), gate(3), val(3), gate(1), val(1)], gate


def _conv_gate(u, cw, cb, name):
    n = u.shape[0]
    ins, gate_blk = _ff_specs(n)

    def body(ug_ref, uv_ref, wg_ref, wv_ref, bg_ref, bv_ref, o_ref):
        gate = _conv(ug_ref[...], wg_ref[...], bg_ref[...])
        val = _conv(uv_ref[...], wv_ref[...], bv_ref[...])
        o_ref[...] = (gate * _sigmoid(gate) * val).astype(o_ref.dtype)

    return pl.pallas_call(
        body, name=name, grid=(FF_BLOCKS,), in_specs=ins, out_specs=gate_blk(n), out_shape=SDS((n, D_FF), MXU_DTYPE),
        compiler_params=_cp("parallel"))(u, u, cw, cw, cb, cb)


def _conv_gate_bwd(u, cw, cb, da, name):
    n = u.shape[0]
    ins, gate_blk = _ff_specs(n)

    def side(dacc, u, w, du_ref, dw_ref, db_ref):
        nxt, prv = _shift_rows(dacc, False), _shift_rows(dacc, True)
        du_ref[...] = (nxt * w[0:1, :] + dacc * w[1:2, :] + prv * w[2:3, :]).astype(du_ref.dtype)
        db_ref[...] = jnp.sum(dacc, axis=0, keepdims=True)
        dw_ref[0:1, :] = jnp.sum(nxt * u, axis=0, keepdims=True)
        dw_ref[1:2, :] = jnp.sum(dacc * u, axis=0, keepdims=True)
        dw_ref[2:3, :] = jnp.sum(prv * u, axis=0, keepdims=True)

    def body(ug_ref, uv_ref, wg_ref, wv_ref, bg_ref, bv_ref, da_ref, dug_ref, duv_ref, dwg_ref, dwv_ref, dbg_ref, dbv_ref):
        ug, uv = ug_ref[...], uv_ref[...]
        gate = _conv(ug, wg_ref[...], bg_ref[...])
        val = _conv(uv, wv_ref[...], bv_ref[...])
        sg = _sigmoid(gate)
        dav = da_ref[...].astype(F32)
        side(dav * val * sg * (1.0 + gate * (1.0 - sg)), ug, wg_ref[...], dug_ref, dwg_ref, dbg_ref)
        side(dav * gate * sg, uv, wv_ref[...], duv_ref, dwv_ref, dbv_ref)

    return pl.pallas_call(
        body, name=name, grid=(FF_BLOCKS,), in_specs=ins + [gate_blk(n)],
        out_specs=[gate_blk(n), gate_blk(n), gate_blk(3), gate_blk(3), gate_blk(1), gate_blk(1)],
        out_shape=[SDS((n, D_FF), MXU_DTYPE)] * 2 + [SDS((3, D_FF), F32)] * 2 + [SDS((1, D_FF), F32)] * 2,
        compiler_params=_cp("parallel"))(u, u, cw, cw, cb, cb, da)


def _adamw(w, g, m, v, name):
    r, c = w.shape
    tr = r if r <= 512 else 256 if r % 256 == 0 else 88
    assert r % tr == 0, (name, r, tr)

    def body(w_ref, g_ref, m_ref, v_ref, d_ref, mo_ref, vo_ref):
        gv = g_ref[...]
        mn = ADAM_B1 * m_ref[...] + (1.0 - ADAM_B1) * gv
        vn = ADAM_B2 * v_ref[...] + (1.0 - ADAM_B2) * gv * gv
        m_hat = mn / (1.0 - ADAM_B1 ** ADAM_STEP)
        v_hat = vn / (1.0 - ADAM_B2 ** ADAM_STEP)
        d_ref[...] = -ADAM_LR * (m_hat / (jnp.sqrt(v_hat) + ADAM_EPS) + ADAM_WD * w_ref[...])
        mo_ref[...] = mn
        vo_ref[...] = vn

    blk = pl.BlockSpec((tr, c), lambda i: (i, 0))
    out = SDS((r, c), F32)
    return pl.pallas_call(body, name=name, grid=(r // tr,), in_specs=[blk] * 4, out_specs=[blk] * 3, out_shape=[out] * 3,
                          compiler_params=_cp("parallel"))(w, g, m, v)


def _half_tile(h):
    tr = h if h <= 512 else 256 if h % 256 == 0 else 176
    assert h % tr == 0, (h, tr)
    return tr


def _add_halves(g, r, core, name):
    s, h, c = r.shape
    tr = _half_tile(h)
    steps = h // tr

    def body(ix_ref, g_ref, r_ref, o_ref):
        o_ref[...] = (g_ref[...].astype(F32) + r_ref[...].astype(F32)).astype(o_ref.dtype)

    blk = pl.BlockSpec((1, tr, c), lambda i, j, ix: (i, j, 0))
    grid_spec = pltpu.PrefetchScalarGridSpec(
        num_scalar_prefetch=1, grid=(s, steps),
        in_specs=[pl.BlockSpec((1, tr, c), lambda i, j, ix: (i, ix[0] * steps + j, 0)), blk], out_specs=blk)
    return pl.pallas_call(body, name=name, grid_spec=grid_spec, out_shape=SDS(r.shape, WIRE_DTYPE),
                          compiler_params=_cp("parallel", "parallel"))(core.reshape(1), g, r)


def _sum_chips(own, recv, me, core, name):
    _, h, c = own.shape
    tr = _half_tile(h)

    def body(ix_ref, own_ref, recv_ref, o_ref):
        acc = own_ref[0].astype(F32)
        for j in range(3):
            acc = acc + recv_ref[j].astype(F32)
        o_ref[0] = acc

    grid_spec = pltpu.PrefetchScalarGridSpec(
        num_scalar_prefetch=1, grid=(h // tr,),
        in_specs=[pl.BlockSpec((1, tr, c), lambda i, ix: (ix[0], i, 0)), pl.BlockSpec((3, tr, c), lambda i, ix: (0, i, 0))],
        out_specs=pl.BlockSpec((1, tr, c), lambda i, ix: (ix[1], i, 0)))
    return pl.pallas_call(body, name=name, grid_spec=grid_spec, out_shape=SDS((2, h, c), F32),
                          compiler_params=_cp("parallel"))(jnp.stack([me, core]), own, recv)


ANY = pl.BlockSpec(memory_space=pl.ANY)


def _place():
    x, y, c = lax.axis_index("x"), lax.axis_index("y"), lax.axis_index("c")
    return x, y, c, [(1 - x, y), (x, 1 - y), (1 - x, 1 - y)]


def _gather_shards(shards, name):
    nt = len(shards)

    def body(*refs):
        ins, outs = refs[:nt], refs[nt:2 * nt]
        send, recv, fsend, frecv, osend, orecv = refs[2 * nt:]
        x, y, c, chips = _place()
        me = 2 * x + y

        def half(t, chip, cc):
            h = ins[t].shape[0] // 2
            return outs[t].at[chip, pl.ds(cc * h, h)]

        def ici(t, j):
            cx, cy = chips[j]
            h = ins[t].shape[0] // 2
            return pltpu.make_async_remote_copy(src_ref=ins[t].at[pl.ds(c * h, h)], dst_ref=half(t, me, c),
                                                send_sem=send.at[t, j], recv_sem=recv.at[t, j], device_id=(cx, cy, c), device_id_type=MESH)

        def landed(t, j):
            cx, cy = chips[j]
            blk = half(t, 2 * cx + cy, c)
            return pltpu.make_async_remote_copy(src_ref=blk, dst_ref=blk, send_sem=send.at[t, j], recv_sem=recv.at[t, j],
                                                device_id=(cx, cy, c), device_id_type=MESH)

        def d2d(t, j, cc):
            cx, cy = chips[j]
            blk = half(t, 2 * cx + cy, cc)
            return pltpu.make_async_remote_copy(src_ref=blk, dst_ref=blk, send_sem=fsend.at[t, j], recv_sem=frecv.at[t, j],
                                                device_id=(x, y, 1 - c), device_id_type=MESH)

        own = [pltpu.make_async_remote_copy(src_ref=ins[t], dst_ref=outs[t].at[me], send_sem=osend.at[t], recv_sem=orecv.at[t],
                                            device_id=(x, y, 1 - c), device_id_type=MESH) for t in range(nt)]
        for t in range(nt):
            for j in range(3):
                ici(t, j).start()
        for cp in own:
            cp.start()
        for t in range(nt):
            for j in range(3):
                landed(t, j).wait_recv()
                d2d(t, j, c).start()
        for t in range(nt):
            for j in range(3):
                d2d(t, j, 1 - c).wait_recv()
        for t in range(nt):
            for j in range(3):
                ici(t, j).wait_send()
                d2d(t, j, c).wait_send()
        for cp in own:
            cp.wait()

    return pl.pallas_call(
        body, name=name, in_specs=[ANY] * nt, out_specs=[ANY] * nt,
        out_shape=[SDS((4,) + s.shape, s.dtype) for s in shards],
        scratch_shapes=[pltpu.SemaphoreType.DMA((nt, 3))] * 4 + [pltpu.SemaphoreType.DMA((nt,))] * 2,
        compiler_params=pltpu.CompilerParams(has_side_effects=True))(*shards)


def _swap_halves(grads, name):
    nt = len(grads)

    def body(*refs):
        ins, outs = refs[:nt], refs[nt:2 * nt]
        send, recv = refs[2 * nt:]
        x, y, c, _ = _place()
        cps = []
        for t in range(nt):
            h = ins[t].shape[1] // 2
            cps.append(pltpu.make_async_remote_copy(src_ref=ins[t].at[pl.ds(0, 4), pl.ds((1 - c) * h, h)], dst_ref=outs[t], send_sem=send.at[t],
                                                    recv_sem=recv.at[t], device_id=(x, y, 1 - c), device_id_type=MESH))
        for cp in cps:
            cp.start()
        for cp in cps:
            cp.wait()

    return pl.pallas_call(
        body, name=name, in_specs=[ANY] * nt, out_specs=[ANY] * nt,
        out_shape=[SDS((g.shape[0], g.shape[1] // 2, g.shape[2]), g.dtype) for g in grads],
        scratch_shapes=[pltpu.SemaphoreType.DMA((nt,))] * 2,
        compiler_params=pltpu.CompilerParams(has_side_effects=True))(*grads)


def _send_to_owners(parts, name):
    nt = len(parts)

    def body(*refs):
        ins, outs = refs[:nt], refs[nt:2 * nt]
        send, recv = refs[2 * nt:]
        x, y, c, chips = _place()

        def ici(t, j):
            cx, cy = chips[j]
            return pltpu.make_async_remote_copy(src_ref=ins[t].at[2 * cx + cy], dst_ref=outs[t].at[j], send_sem=send.at[t, j],
                                                recv_sem=recv.at[t, j], device_id=(cx, cy, c), device_id_type=MESH)

        for t in range(nt):
            for j in range(3):
                ici(t, j).start()
        for t in range(nt):
            for j in range(3):
                ici(t, j).wait()

    return pl.pallas_call(
        body, name=name, in_specs=[ANY] * nt, out_specs=[ANY] * nt, out_shape=[SDS((3,) + p.shape[1:], p.dtype) for p in parts],
        scratch_shapes=[pltpu.SemaphoreType.DMA((nt, 3))] * 2,
        compiler_params=pltpu.CompilerParams(has_side_effects=True))(*parts)


def _join_halves(bufs, name):
    nt = len(bufs)

    def body(*refs):
        outs = refs[nt:2 * nt]
        send, recv = refs[2 * nt:]
        x, y, c, _ = _place()
        cps = [pltpu.make_async_remote_copy(src_ref=outs[t].at[c], dst_ref=outs[t].at[c], send_sem=send.at[t], recv_sem=recv.at[t],
                                            device_id=(x, y, 1 - c), device_id_type=MESH) for t in range(nt)]
        for cp in cps:
            cp.start()
        for t in range(nt):
            theirs = outs[t].at[1 - c]
            pltpu.make_async_remote_copy(src_ref=theirs, dst_ref=theirs, send_sem=send.at[t], recv_sem=recv.at[t],
                                         device_id=(x, y, 1 - c), device_id_type=MESH).wait_recv()
        for cp in cps:
            cp.wait_send()

    return pl.pallas_call(
        body, name=name, in_specs=[ANY] * nt, out_specs=[ANY] * nt, out_shape=[SDS(b.shape, b.dtype) for b in bufs],
        input_output_aliases={t: t for t in range(nt)},
        scratch_shapes=[pltpu.SemaphoreType.DMA((nt,))] * 2,
        compiler_params=pltpu.CompilerParams(has_side_effects=True))(*bufs)


def _exchange_small(v, reduce, name):
    rows = v.shape[0]

    def body(v_ref, o_ref, buf, send, recv):
        x, y, c, _ = _place()
        me = 4 * x + 2 * y + c
        buf[me] = v_ref[...]

        def peer(dx, dy, dc):
            return (1 - x if dx else x, 1 - y if dy else y, 1 - c if dc else c)

        peers = [(dx, dy, dc) for dx in range(2) for dy in range(2) for dc in range(2) if (dx, dy, dc) != (0, 0, 0)]
        cps = []
        for j, (dx, dy, dc) in enumerate(peers):
            cps.append(pltpu.make_async_remote_copy(src_ref=v_ref, dst_ref=buf.at[me], send_sem=send.at[j], recv_sem=recv.at[j],
                                                    device_id=peer(dx, dy, dc), device_id_type=MESH))
        for cp in cps:
            cp.start()
        for j, (dx, dy, dc) in enumerate(peers):
            px, py, pc = peer(dx, dy, dc)
            blk = buf.at[4 * px + 2 * py + pc]
            pltpu.make_async_remote_copy(src_ref=blk, dst_ref=blk, send_sem=send.at[j], recv_sem=recv.at[j],
                                         device_id=(px, py, pc), device_id_type=MESH).wait_recv()
        for cp in cps:
            cp.wait_send()
        if reduce:
            acc = buf[0]
            for j in range(1, 8):
                acc = acc + buf[j]
            o_ref[...] = acc
        else:
            o_ref[...] = buf[...]

    vm = pl.BlockSpec(memory_space=pltpu.VMEM)
    return pl.pallas_call(
        body, name=name, in_specs=[vm], out_specs=vm, out_shape=SDS((rows, 128) if reduce else (8, rows, 128), F32),
        scratch_shapes=[pltpu.VMEM((8, rows, 128), F32), pltpu.SemaphoreType.DMA((7,)), pltpu.SemaphoreType.DMA((7,))],
        compiler_params=pltpu.CompilerParams(has_side_effects=True))(v)


def _pack_small(parts):
    flat = jnp.concatenate([p.reshape(-1) for p in parts])
    total = flat.shape[0]
    rows = -(-total // 1024) * 8
    return jnp.pad(flat, (0, rows * 128 - total)).reshape(rows, 128)


def _unpack_small(packed, shapes):
    flat = packed.reshape(-1)
    out, off = [], 0
    for s in shapes:
        size = int(np.prod(s))
        out.append(flat[off:off + size].reshape(s))
        off += size
    return out


def _local_step(x, mem, target, wts, gains, conv_w, conv_b, hg_lb):
    n = x.shape[0]
    w_in, w_out, w_xq, w_xkv, w_xo, w_up, w_down = (wts[k] for k in ("w_in", "w_out", "w_xq", "w_xkv", "w_xo", "w_up", "w_down"))
    cos, sin = _rope_tables(n)
    seg = _hg_segments()
    gp, gn = _hg_pair_sums()
    gq2 = jnp.tile(gains["q_norm_g"], (1, 2))
    gk2 = jnp.tile(gains["k_norm_g"], (1, 2))
    a0 = hg_lb[:, 0:1, :]
    a1 = hg_lb[:, 1:2, :]

    p, h1 = _norm_mm(x, gains["pre_mix_g"], w_in, F32, 512, 1664, "in_proj")
    qr, kr = _qk_prep(p, gq2, gk2, cos, sin, "qk_prep")
    heads = lambda a: a.reshape(n, ATT_KV_HEADS, ATT_HEAD_DIM).transpose(1, 0, 2)
    kh = heads(kr)
    vh = heads(p[:, OFF_AV:OFF_AV + ATT_KV_DIM].astype(MXU_DTYPE))
    att = _attn_fwd(qr, kh, vh, "attn_fwd")
    o2 = _hgrn_fwd(p, a0, a1, seg, "hgrn_fwd")
    rec = _hg_post(o2, p, gains["hg_out_norm_g"], "hg_post")
    cat = jnp.concatenate([att, rec], axis=1)
    mixed = _mm(cat, w_out, "nn", F32, 512, 1024, "out_proj")
    x1 = _resid_norm(x, mixed, gains["post_mix_g"], "mix_resid")
    xq, h2 = _norm_mm(x1, gains["pre_x_g"], w_xq, MXU_DTYPE, 512, 1024, "xq_proj")
    kv, mn = _norm_mm(mem, gains["mem_norm_g"], w_xkv, MXU_DTYPE, 256, 2048, "xkv_proj")
    ox = _xattn_fwd(xq, kv, "xattn_fwd")
    xo = _mm(ox, w_xo, "nn", F32, 512, 1024, "xo_proj")
    x2 = _resid_norm(x1, xo, gains["post_x_g"], "x_resid")
    u, h3 = _norm_mm(x2, gains["pre_ffn_g"], w_up, F32, 512, 512, "up_proj")
    act = _conv_gate(u, conv_w, conv_b, "conv_gate")
    dn = _mm(act, w_down, "nn", F32, 512, 1024, "down_proj")
    d3, loss = _final_loss(x2, dn, gains["post_ffn_g"], target, "ffn_resid_loss")

    gm, gs = {}, {}
    d_dn, gs["post_ffn_g"] = _norm_bwd(dn, gains["post_ffn_g"], d3, None, MXU_DTYPE, "ffn_post_bwd")
    gm["w_down"] = _mm(act, d_dn, "tn", WIRE_DTYPE, 1408, 1024, "down_dw")
    d_act = _mm(d_dn, w_down, "nt", F32, 512, 1408, "down_dx")
    du_g, du_v, dcw_g, dcw_v, dcb_g, dcb_v = _conv_gate_bwd(u, conv_w, conv_b, d_act, "conv_gate_bwd")
    gs["conv_w"] = jnp.concatenate([dcw_g, dcw_v], axis=1)
    gs["conv_b"] = jnp.concatenate([dcb_g, dcb_v], axis=1)
    gm["w_up"] = (_mm(h3, du_g, "tn", WIRE_DTYPE, 512, 1408, "up_dw_gate"), _mm(h3, du_v, "tn", WIRE_DTYPE, 512, 1408, "up_dw_value"))
    d_h3 = _mm_nt_halves(du_g, du_v, w_up, F32, 512, 512, "up_dx")
    d2, gs["pre_ffn_g"] = _norm_bwd(x2, gains["pre_ffn_g"], d_h3, d3, F32, "ffn_pre_bwd")
    d_xo, gs["post_x_g"] = _norm_bwd(xo, gains["post_x_g"], d2, None, MXU_DTYPE, "x_post_bwd")
    gm["w_xo"] = _mm(ox, d_xo, "tn", WIRE_DTYPE, 512, 1024, "xo_dw")
    d_ox = _mm(d_xo, w_xo, "nt", MXU_DTYPE, 512, 1024, "xo_dx")
    d_xq, d_k, d_v = _xattn_bwd(xq, kv, d_ox, "xattn_bwd")
    d_kv = jnp.concatenate([d_k, d_v], axis=1).astype(MXU_DTYPE)
    gm["w_xq"] = _mm(h2, d_xq, "tn", WIRE_DTYPE, 512, 1024, "xq_dw")
    d_h2 = _mm(d_xq, w_xq, "nt", F32, 512, 1024, "xq_dx")
    gm["w_xkv"] = _mm(mn, d_kv, "tn", WIRE_DTYPE, 512, 1024, "xkv_dw")
    d_mn = _mm(d_kv, w_xkv, "nt", F32, 256, 1024, "xkv_dx")
    _, gs["mem_norm_g"] = _norm_bwd(mem, gains["mem_norm_g"], d_mn, None, MXU_DTYPE, "mem_norm_bwd")
    d1, gs["pre_x_g"] = _norm_bwd(x1, gains["pre_x_g"], d_h2, d2, F32, "x_pre_bwd")
    d_mixed, gs["post_mix_g"] = _norm_bwd(mixed, gains["post_mix_g"], d1, None, MXU_DTYPE, "mix_post_bwd")
    gm["w_out"] = _mm(cat, d_mixed, "tn", WIRE_DTYPE, 512, 1024, "out_dw")
    d_cat = _mm(d_mixed, w_out, "nt", MXU_DTYPE, 512, 1024, "out_dx")
    d_o, d_hg, dg_hg = _hg_post_bwd(o2, p, gains["hg_out_norm_g"], d_cat, "hg_post_bwd")
    gs["hg_out_norm_g"] = dg_hg.reshape(HG_HEADS, HG_HEAD_DIM).sum(axis=0, keepdims=True)
    dhq2, dlf_q, s0 = _hgrn_bwd_q(p, a0, a1, seg, gp, d_o, "hgrn_bwd_q")
    dz2, dhv2, dlb = _hgrn_bwd_kv(p, a0, a1, seg, gn, d_o, dlf_q, s0, "hgrn_bwd_kv")
    lb = jax.nn.sigmoid(a0 - a1)
    da0 = dlb * lb * (1.0 - lb)
    gs["hg_lb"] = jnp.concatenate([da0, -da0], axis=1)
    d_qr, d_kh, d_vh = _attn_bwd(qr, kh, vh, d_cat, "attn_bwd")
    unheads = lambda a: a.transpose(1, 0, 2).reshape(n, ATT_KV_DIM)
    d_aq, d_ak, dgq, dgk = _qk_prep_bwd(p, gq2, gk2, cos, sin, d_qr, unheads(d_kh), "qk_prep_bwd")
    gs["q_norm_g"] = dgq.reshape(ATT_HEADS, ATT_HEAD_DIM).sum(axis=0, keepdims=True)
    gs["k_norm_g"] = dgk.reshape(ATT_KV_HEADS, ATT_HEAD_DIM).sum(axis=0, keepdims=True)
    d_p = jnp.concatenate([d_aq, d_ak, unheads(d_vh).astype(MXU_DTYPE), (dhq2[0] + dhq2[1]).astype(MXU_DTYPE),
                           dz2[0].astype(MXU_DTYPE), dz2[1].astype(MXU_DTYPE), (dhv2[0] + dhv2[1]).astype(MXU_DTYPE), d_hg], axis=1)
    gm["w_in"] = _mm(h1, d_p, "tn", WIRE_DTYPE, 512, 1664, "in_dw")
    d_h1 = _mm(d_p, w_in, "nt", F32, 512, 1024, "in_dx")
    grad_x, gs["pre_mix_g"] = _norm_bwd(x, gains["pre_mix_g"], d_h1, d1, F32, "mix_pre_bwd")
    return loss, grad_x, gm, gs


MATS = ("w_in", "w_out", "w_xq", "w_xkv", "w_xo", "w_up", "w_down")
COL_SHARDED = ("w_in", "w_xkv", "w_up")
GAINS = ("pre_mix_g", "q_norm_g", "k_norm_g", "hg_out_norm_g", "post_mix_g", "pre_x_g", "mem_norm_g", "post_x_g", "pre_ffn_g", "post_ffn_g")
WEIGHTS = ('pre_mix_g', 'w_in', 'q_norm_g', 'k_norm_g', 'hg_lb', 'hg_out_norm_g', 'w_out', 'post_mix_g', 'pre_x_g', 'mem_norm_g', 'w_xq',
           'w_xkv', 'w_xo', 'post_x_g', 'pre_ffn_g', 'w_up', 'conv_w', 'conv_b', 'w_down', 'post_ffn_g')


def kernel(x, mem, pre_mix_g, w_in, q_norm_g, k_norm_g, hg_lb, hg_out_norm_g, w_out, post_mix_g, pre_x_g, mem_norm_g, w_xq, w_xkv, w_xo, post_x_g, pre_ffn_g, w_up, conv_w, conv_b, w_down, post_ffn_g, loss_target, m_pre_mix_g, m_w_in, m_q_norm_g, m_k_norm_g, m_hg_lb, m_hg_out_norm_g, m_w_out, m_post_mix_g, m_pre_x_g, m_mem_norm_g, m_w_xq, m_w_xkv, m_w_xo, m_post_x_g, m_pre_ffn_g, m_w_up, m_conv_w, m_conv_b, m_w_down, m_post_ffn_g, v_pre_mix_g, v_w_in, v_q_norm_g, v_k_norm_g, v_hg_lb, v_hg_out_norm_g, v_w_out, v_post_mix_g, v_pre_x_g, v_mem_norm_g, v_w_xq, v_w_xkv, v_w_xo, v_post_x_g, v_pre_ffn_g, v_w_up, v_conv_w, v_conv_b, v_w_down, v_post_ffn_g):
    args = dict(locals())
    w = {k: args[k] for k in WEIGHTS}
    m = {k: args["m_" + k] for k in WEIGHTS}
    v = {k: args["v_" + k] for k in WEIGHTS}
    chip = 2 * lax.axis_index("x") + lax.axis_index("y")
    core = lax.axis_index("c")

    shards = [w[k][0].astype(WIRE_DTYPE) for k in MATS]
    gathered = _gather_shards(shards, "gather_weights")
    wts = {}
    for k, g in zip(MATS, gathered):
        wts[k] = jnp.concatenate([g[s] for s in range(4)], axis=1) if k in COL_SHARDED else g.reshape(-1, g.shape[-1])
    small_in = _exchange_small(_pack_small([w["conv_w"][0], w["hg_lb"]]), False, "gather_small")
    cw_parts, lb_parts = [], []
    for s in range(4):
        cw_s, lb_s = _unpack_small(small_in[2 * s], [w["conv_w"][0].shape, w["hg_lb"].shape])
        cw_parts.append(cw_s)
        lb_parts.append(lb_s)
    conv_w_full = jnp.concatenate(cw_parts, axis=1)
    hg_lb_full = jnp.concatenate(lb_parts, axis=2)

    gains = {k: w[k] for k in GAINS}
    loss, grad_x, gm, gs = _local_step(x[0], mem[0], loss_target[0], wts, gains, conv_w_full, w["conv_b"], hg_lb_full)
    loss = lax.psum(loss[0, 0], ("x", "y", "c"))

    def cols_by_owner(g, chips):
        return g.reshape(g.shape[0], chips, g.shape[1] // chips).transpose(1, 0, 2)

    by_owner = []
    for k in MATS:
        g = gm[k]
        if k == "w_up":
            by_owner.append(jnp.concatenate([cols_by_owner(g[0], 2), cols_by_owner(g[1], 2)], axis=0))
        elif k in COL_SHARDED:
            by_owner.append(cols_by_owner(g, 4))
        else:
            by_owner.append(g.reshape(4, g.shape[0] // 4, g.shape[1]))
    from_sibling = _swap_halves(by_owner, "grad_swap_halves")
    pair = [_add_halves(g, r, core, "grad_pair_" + k) for k, g, r in zip(MATS, by_owner, from_sibling)]
    from_chips = _send_to_owners(pair, "grad_to_owner")
    halves = [_sum_chips(own, recv, chip, core, "grad_sum_" + k) for k, own, recv in zip(MATS, pair, from_chips)]
    reduced = _join_halves(halves, "grad_join_halves")
    grads = {k: r.reshape(1, -1, r.shape[-1]) for k, r in zip(MATS, reduced)}

    small_names = GAINS + ("conv_b", "conv_w", "hg_lb")
    small_shapes = [gs[k].shape for k in small_names]
    summed = _unpack_small(_exchange_small(_pack_small([gs[k] for k in small_names]), True, "reduce_small"), small_shapes)
    for k, g in zip(small_names, summed):
        grads[k] = g
    ncw = w["conv_w"].shape[2]
    grads["conv_w"] = lax.dynamic_slice_in_dim(grads["conv_w"], chip * ncw, ncw, axis=1)[None]
    nlb = w["hg_lb"].shape[2]
    grads["hg_lb"] = lax.dynamic_slice_in_dim(grads["hg_lb"], chip * nlb, nlb, axis=2)

    delta, new_m, new_v = {}, {}, {}
    for k in WEIGHTS:
        shape = w[k].shape
        two_d = lambda a: a.reshape(-1, shape[-1])
        d, mo, vo = _adamw(two_d(w[k]), two_d(grads[k]), two_d(m[k]), two_d(v[k]), "adamw_" + k)
        delta[k], new_m[k], new_v[k] = d.reshape(shape), mo.reshape(shape), vo.reshape(shape)
        grads[k] = grads[k].reshape(shape)
    return (loss, grad_x[None], *[grads[k] for k in WEIGHTS], *[delta[k] for k in WEIGHTS],
            *[new_m[k] for k in WEIGHTS], *[new_v[k] for k in WEIGHTS])
```

```python
import functools

import numpy as np
import jax
import jax.numpy as jnp
from jax import lax
from jax.experimental import pallas as pl
from jax.experimental.pallas import tpu as pltpu

F32 = jnp.float32
MXU_DTYPE = jnp.bfloat16
WIRE_DTYPE = jnp.bfloat16
VMEM_LIMIT_BYTES = 56 * 1024 * 1024
EPS = 1e-6
MESH = pl.DeviceIdType.MESH

D_MODEL = 1024
GRID_W = 64
ATT_HEADS, ATT_KV_HEADS, ATT_HEAD_DIM = 8, 2, 64
ATT_GROUP = ATT_HEADS // ATT_KV_HEADS
ATT_Q_DIM, ATT_KV_DIM = 512, 128
ROPE_THETA = 10000.0
HG_HEADS, HG_HEAD_DIM, HG_DIM = 4, 128, 512
HG_CHUNK = 128
HG_LEVELS = 7
HG_PAIR = 2 * HG_HEAD_DIM
X_HEADS, X_HEAD_DIM = 4, 256
D_FF = 2816
FF_COLS = 256
FF_BLOCKS = D_FF // FF_COLS
N_IN = 3328
OFF_AQ, OFF_AK, OFF_AV, OFF_HQ, OFF_ZF, OFF_ZB, OFF_HI, OFF_HG = 0, 512, 640, 768, 1280, 1792, 2304, 2816

ADAM_LR, ADAM_B1, ADAM_B2, ADAM_EPS, ADAM_WD, ADAM_STEP = 0.001, 0.9, 0.999, 1e-08, 0.01, 10

SDS = jax.ShapeDtypeStruct


def _cp(*sem):
    return pltpu.CompilerParams(dimension_semantics=sem, vmem_limit_bytes=VMEM_LIMIT_BYTES)


def _dot(a, b, form="nn"):
    dims = {"nn": (((1,), (0,)), ((), ())), "nt": (((1,), (1,)), ((), ())), "tn": (((0,), (0,)), ((), ()))}[form]
    return lax.dot_general(a.astype(MXU_DTYPE), b.astype(MXU_DTYPE), dims, preferred_element_type=F32)


def _sigmoid(x):
    return 1.0 / (1.0 + jnp.exp(-x))


def _rstd(x):
    return lax.rsqrt(jnp.mean(x * x, axis=-1, keepdims=True) + EPS)


def _rms_bwd(x, g, dy):
    r = _rstd(x)
    xh = x * r
    dn = dy * g
    dx = r * (dn - xh * jnp.mean(dn * xh, axis=-1, keepdims=True))
    return dx, jnp.sum(dy * xh, axis=0, keepdims=True)


def _mm(a, b, form, out_dtype, tm, tn, name):
    if form == "nn":
        (m, k), n = a.shape, b.shape[1]
    elif form == "nt":
        (m, k), n = a.shape, b.shape[0]
    else:
        (k, m), n = a.shape, b.shape[1]
    tm, tn = min(tm, m), min(tn, n)
    assert m % tm == 0 and n % tn == 0, (name, m, n, tm, tn)

    def body(a_ref, b_ref, o_ref):
        o_ref[...] = _dot(a_ref[...], b_ref[...], form).astype(o_ref.dtype)

    a_spec = pl.BlockSpec((k, tm), lambda i, j: (0, i)) if form == "tn" else pl.BlockSpec((tm, k), lambda i, j: (i, 0))
    b_spec = pl.BlockSpec((tn, k), lambda i, j: (j, 0)) if form == "nt" else pl.BlockSpec((k, tn), lambda i, j: (0, j))
    return pl.pallas_call(
        body, name=name, grid=(m // tm, n // tn), in_specs=[a_spec, b_spec],
        out_specs=pl.BlockSpec((tm, tn), lambda i, j: (i, j)), out_shape=SDS((m, n), out_dtype),
        compiler_params=_cp("parallel", "parallel"))(a, b)


def _mm_nt_halves(a0, a1, b, out_dtype, tm, tn, name):
    m, kh = a0.shape
    n = b.shape[0]
    tm, tn = min(tm, m), min(tn, n)
    assert m % tm == 0 and n % tn == 0 and b.shape[1] == 2 * kh, (name, m, n, tm, tn)

    def body(a0_ref, a1_ref, b0_ref, b1_ref, o_ref):
        o_ref[...] = (_dot(a0_ref[...], b0_ref[...], "nt") + _dot(a1_ref[...], b1_ref[...], "nt")).astype(o_ref.dtype)

    a_spec = pl.BlockSpec((tm, kh), lambda i, j: (i, 0))
    return pl.pallas_call(
        body, name=name, grid=(m // tm, n // tn),
        in_specs=[a_spec, a_spec, pl.BlockSpec((tn, kh), lambda i, j: (j, 0)), pl.BlockSpec((tn, kh), lambda i, j: (j, 1))],
        out_specs=pl.BlockSpec((tm, tn), lambda i, j: (i, j)), out_shape=SDS((m, n), out_dtype),
        compiler_params=_cp("parallel", "parallel"))(a0, a1, b, b)


def _norm_mm(x, g, w, out_dtype, tm, tn, name):
    m, d = x.shape
    n = w.shape[1]
    tm, tn = min(tm, m), min(tn, n)
    assert m % tm == 0 and n % tn == 0, (name, m, n, tm, tn)

    def body(x_ref, g_ref, w_ref, o_ref, h_ref, hs):
        @pl.when(pl.program_id(1) == 0)
        def _():
            xv = x_ref[...]
            h = (xv * _rstd(xv) * g_ref[...]).astype(MXU_DTYPE)
            hs[...] = h
            h_ref[...] = h

        o_ref[...] = _dot(hs[...], w_ref[...]).astype(o_ref.dtype)

    return pl.pallas_call(
        body, name=name, grid=(m // tm, n // tn),
        in_specs=[pl.BlockSpec((tm, d), lambda i, j: (i, 0)), pl.BlockSpec((1, d), lambda i, j: (0, 0)),
                  pl.BlockSpec((d, tn), lambda i, j: (0, j))],
        out_specs=[pl.BlockSpec((tm, tn), lambda i, j: (i, j)), pl.BlockSpec((tm, d), lambda i, j: (i, 0))],
        out_shape=[SDS((m, n), out_dtype), SDS((m, d), MXU_DTYPE)],
        scratch_shapes=[pltpu.VMEM((tm, d), MXU_DTYPE)],
        compiler_params=_cp("parallel", "arbitrary"))(x, g, w)


ROW_TILE = 256


def _resid_norm(x, y, g, name):
    n, d = x.shape
    tr = min(ROW_TILE, n)

    def body(x_ref, y_ref, g_ref, o_ref):
        yv = y_ref[...]
        o_ref[...] = x_ref[...] + yv * _rstd(yv) * g_ref[...]

    row = pl.BlockSpec((tr, d), lambda i: (i, 0))
    return pl.pallas_call(
        body, name=name, grid=(n // tr,), in_specs=[row, row, pl.BlockSpec((1, d), lambda i: (0, 0))],
        out_specs=row, out_shape=SDS((n, d), F32), compiler_params=_cp("parallel"))(x, y, g)


def _norm_bwd(x, g, dy, res, out_dtype, name):
    n, d = x.shape
    tr = min(ROW_TILE, n)
    has_res = res is not None

    def body(*refs):
        x_ref, g_ref, dy_ref = refs[:3]
        dx_ref, dg_ref = refs[-2:]
        dx, dg = _rms_bwd(x_ref[...], g_ref[...], dy_ref[...].astype(F32))
        if has_res:
            dx = dx + refs[3][...]
        dx_ref[...] = dx.astype(dx_ref.dtype)

        @pl.when(pl.program_id(0) == 0)
        def _():
            dg_ref[...] = jnp.zeros_like(dg_ref)

        dg_ref[...] += dg

    row = pl.BlockSpec((tr, d), lambda i: (i, 0))
    vec = pl.BlockSpec((1, d), lambda i: (0, 0))
    ins = [x, g, dy] + ([res] if has_res else [])
    return pl.pallas_call(
        body, name=name, grid=(n // tr,), in_specs=[row, vec, row] + ([row] if has_res else []),
        out_specs=[row, vec], out_shape=[SDS((n, d), out_dtype), SDS((1, d), F32)],
        compiler_params=_cp("arbitrary"))(*ins)


def _final_loss(x, y, g, target, name):
    n, d = x.shape
    tr = min(ROW_TILE, n)

    def body(x_ref, y_ref, g_ref, t_ref, d_ref, l_ref):
        yv = y_ref[...]
        diff = x_ref[...] + yv * _rstd(yv) * g_ref[...] - t_ref[...]
        d_ref[...] = diff * (1.0 / d)

        @pl.when(pl.program_id(0) == 0)
        def _():
            l_ref[...] = jnp.zeros_like(l_ref)

        l_ref[...] += 0.5 * jnp.sum(jnp.mean(diff * diff, axis=-1, keepdims=True), axis=0, keepdims=True)

    row = pl.BlockSpec((tr, d), lambda i: (i, 0))
    return pl.pallas_call(
        body, name=name, grid=(n // tr,), in_specs=[row, row, pl.BlockSpec((1, d), lambda i: (0, 0)), row],
        out_specs=[row, pl.BlockSpec((1, 1), lambda i: (0, 0))], out_shape=[SDS((n, d), F32), SDS((1, 1), F32)],
        compiler_params=_cp("arbitrary"))(x, y, g, target)


def _rope_tables(n):
    pairs = ATT_HEAD_DIM // 4
    t = np.arange(n)
    inv = np.power(ROPE_THETA, -np.arange(pairs, dtype=np.float32) / pairs).astype(np.float32)
    ang = np.concatenate([(t // GRID_W)[:, None].astype(np.float32) * inv, (t % GRID_W)[:, None].astype(np.float32) * inv], axis=-1)
    cos = np.repeat(np.cos(ang), 2, axis=-1)
    sin = np.repeat(np.sin(ang), 2, axis=-1) * np.tile(np.array([-1.0, 1.0], np.float32), ATT_HEAD_DIM // 2)
    return jnp.asarray(np.tile(cos, 2), F32), jnp.asarray(np.tile(sin, 2), F32)


def _swap_pairs(x):
    lane = lax.broadcasted_iota(jnp.int32, x.shape, 1)
    return jnp.where((lane & 1) == 0, pltpu.roll(x, 127, axis=1), pltpu.roll(x, 1, axis=1))


def _head_mean(v):
    lane = lax.broadcasted_iota(jnp.int32, v.shape, 1)
    lo = jnp.where(lane < ATT_HEAD_DIM, v, 0.0)
    s0 = jnp.sum(lo, axis=-1, keepdims=True)
    s1 = jnp.sum(v - lo, axis=-1, keepdims=True)
    return jnp.where(lane < ATT_HEAD_DIM, s0, s1) * (1.0 / ATT_HEAD_DIM)


def _qk_prep(p, gq, gk, cos, sin, name):
    n = p.shape[0]
    tr = min(ROW_TILE, n)

    def one(xv, g, c, s):
        xn = xv * lax.rsqrt(_head_mean(xv * xv) + EPS) * g
        return xn * c + _swap_pairs(xn) * s

    def body(q_ref, k_ref, gq_ref, gk_ref, c_ref, s_ref, qo_ref, ko_ref):
        c, s = c_ref[...], s_ref[...]
        for j in range(ATT_Q_DIM // 128):
            qo_ref[:, j * 128:(j + 1) * 128] = one(q_ref[:, j * 128:(j + 1) * 128], gq_ref[...], c, s).astype(qo_ref.dtype)
        ko_ref[...] = one(k_ref[...], gk_ref[...], c, s).astype(ko_ref.dtype)

    vec = pl.BlockSpec((1, 128), lambda i: (0, 0))
    tab = pl.BlockSpec((tr, 128), lambda i: (i, 0))
    return pl.pallas_call(
        body, name=name, grid=(n // tr,),
        in_specs=[pl.BlockSpec((tr, ATT_Q_DIM), lambda i: (i, 0)), pl.BlockSpec((tr, 128), lambda i: (i, OFF_AK // 128)), vec, vec, tab, tab],
        out_specs=[pl.BlockSpec((tr, ATT_Q_DIM), lambda i: (i, 0)), tab],
        out_shape=[SDS((n, ATT_Q_DIM), MXU_DTYPE), SDS((n, ATT_KV_DIM), MXU_DTYPE)],
        compiler_params=_cp("parallel"))(p, p, gq, gk, cos, sin)


def _qk_prep_bwd(p, gq, gk, cos, sin, dq, dk, name):
    n = p.shape[0]
    tr = min(ROW_TILE, n)

    def one(xv, g, c, s, dout):
        dxn = dout * c + _swap_pairs(dout * s)
        r = lax.rsqrt(_head_mean(xv * xv) + EPS)
        xh = xv * r
        dn = dxn * g
        dx = r * (dn - xh * _head_mean(dn * xh))
        return dx, jnp.sum(dxn * xh, axis=0, keepdims=True)

    def body(q_ref, k_ref, gq_ref, gk_ref, c_ref, s_ref, dq_ref, dk_ref, dqo_ref, dko_ref, dgq_ref, dgk_ref):
        @pl.when(pl.program_id(0) == 0)
        def _():
            dgq_ref[...] = jnp.zeros_like(dgq_ref)
            dgk_ref[...] = jnp.zeros_like(dgk_ref)

        c, s = c_ref[...], s_ref[...]
        for j in range(ATT_Q_DIM // 128):
            sl = slice(j * 128, (j + 1) * 128)
            dx, dg = one(q_ref[:, sl], gq_ref[...], c, s, dq_ref[:, sl])
            dqo_ref[:, sl] = dx.astype(dqo_ref.dtype)
            dgq_ref[:, sl] += dg
        dx, dg = one(k_ref[...], gk_ref[...], c, s, dk_ref[...])
        dko_ref[...] = dx.astype(dko_ref.dtype)
        dgk_ref[...] += dg

    vec = pl.BlockSpec((1, 128), lambda i: (0, 0))
    tab = pl.BlockSpec((tr, 128), lambda i: (i, 0))
    qrow = pl.BlockSpec((tr, ATT_Q_DIM), lambda i: (i, 0))
    return pl.pallas_call(
        body, name=name, grid=(n // tr,),
        in_specs=[qrow, pl.BlockSpec((tr, 128), lambda i: (i, OFF_AK // 128)), vec, vec, tab, tab, qrow, tab],
        out_specs=[qrow, tab, pl.BlockSpec((1, ATT_Q_DIM), lambda i: (0, 0)), vec],
        out_shape=[SDS((n, ATT_Q_DIM), MXU_DTYPE), SDS((n, ATT_KV_DIM), MXU_DTYPE), SDS((1, ATT_Q_DIM), F32), SDS((1, 128), F32)],
        compiler_params=_cp("arbitrary"))(p, p, gq, gk, cos, sin, dq, dk)


ATT_TQ = 256


def _attn_fwd(q, k, v, name):
    n = q.shape[0]
    tq = min(ATT_TQ, n)
    scale = ATT_HEAD_DIM ** -0.5
    gw = ATT_GROUP * ATT_HEAD_DIM

    def body(q_ref, k_ref, v_ref, o_ref):
        kk, vv = k_ref[0], v_ref[0]
        outs = []
        for g in range(ATT_GROUP):
            s = _dot(q_ref[:, g * ATT_HEAD_DIM:(g + 1) * ATT_HEAD_DIM] * scale, kk, "nt")
            e = jnp.exp(s - jnp.max(s, axis=-1, keepdims=True))
            outs.append(_dot(e, vv) / jnp.sum(e, axis=-1, keepdims=True))
        o_ref[...] = jnp.concatenate(outs, axis=-1).astype(o_ref.dtype)

    kv = pl.BlockSpec((1, n, ATT_HEAD_DIM), lambda h, i: (h, 0, 0))
    return pl.pallas_call(
        body, name=name, grid=(ATT_KV_HEADS, n // tq),
        in_specs=[pl.BlockSpec((tq, gw), lambda h, i: (i, h)), kv, kv],
        out_specs=pl.BlockSpec((tq, gw), lambda h, i: (i, h)), out_shape=SDS((n, ATT_Q_DIM), MXU_DTYPE),
        compiler_params=_cp("parallel", "parallel"))(q, k, v)


def _attn_bwd(q, k, v, do, name):
    n = q.shape[0]
    tq = min(ATT_TQ, n)
    scale = ATT_HEAD_DIM ** -0.5
    gw = ATT_GROUP * ATT_HEAD_DIM

    def body(q_ref, k_ref, v_ref, do_ref, dq_ref, dk_ref, dv_ref):
        @pl.when(pl.program_id(1) == 0)
        def _():
            dk_ref[...] = jnp.zeros_like(dk_ref)
            dv_ref[...] = jnp.zeros_like(dv_ref)

        kk, vv = k_ref[0], v_ref[0]
        dqs = []
        dk_acc = jnp.zeros((ATT_HEAD_DIM, n), F32)
        dv_acc = jnp.zeros((ATT_HEAD_DIM, n), F32)
        for g in range(ATT_GROUP):
            sl = slice(g * ATT_HEAD_DIM, (g + 1) * ATT_HEAD_DIM)
            qg, dog = q_ref[:, sl] * scale, do_ref[:, sl].astype(F32)
            s = _dot(qg, kk, "nt")
            e = jnp.exp(s - jnp.max(s, axis=-1, keepdims=True))
            inv = 1.0 / jnp.sum(e, axis=-1, keepdims=True)
            delta = jnp.sum(dog * (_dot(e, vv) * inv), axis=-1, keepdims=True)
            dse = e * (_dot(dog, vv, "nt") - delta)
            dqs.append(_dot(dse, kk) * (inv * scale))
            dk_acc += _dot(qg.astype(F32) * inv, dse, "tn")
            dv_acc += _dot(dog * inv, e, "tn")
        dq_ref[...] = jnp.concatenate(dqs, axis=-1)
        dk_ref[0] += dk_acc
        dv_ref[0] += dv_acc

    kv = pl.BlockSpec((1, n, ATT_HEAD_DIM), lambda h, i: (h, 0, 0))
    kvt = pl.BlockSpec((1, ATT_HEAD_DIM, n), lambda h, i: (h, 0, 0))
    qb = pl.BlockSpec((tq, gw), lambda h, i: (i, h))
    return pl.pallas_call(
        body, name=name, grid=(ATT_KV_HEADS, n // tq), in_specs=[qb, kv, kv, qb], out_specs=[qb, kvt, kvt],
        out_shape=[SDS((n, ATT_Q_DIM), F32), SDS((ATT_KV_HEADS, ATT_HEAD_DIM, n), F32), SDS((ATT_KV_HEADS, ATT_HEAD_DIM, n), F32)],
        compiler_params=_cp("parallel", "arbitrary"))(q, k, v, do)


def _both_directions(mats, axis):
    fwd = np.concatenate(mats, axis=axis).astype(np.float32)
    bwd = np.concatenate([m[::-1, ::-1] for m in mats], axis=axis).astype(np.float32)
    return jnp.asarray(np.stack([fwd, bwd]), MXU_DTYPE)


def _hg_segments():
    c = HG_CHUNK
    t = np.arange(c)[:, None]
    r = np.arange(c)[None, :]
    mats = [(r <= t)]
    for lev in range(HG_LEVELS):
        h = c >> (lev + 1)
        mid = (t // (2 * h)) * (2 * h) + h - 1
        hi = (t // h) % 2 == 1
        mats.append(np.where(hi, (r > mid) & (r <= t), (r > t) & (r <= mid)))
    mats.append(r > t)
    return _both_directions(mats, 0)


def _hg_pair_sums():
    c = HG_CHUNK
    r = np.arange(c)[:, None]
    t = np.arange(c)[None, :]
    gp, gn = [t >= r], [t < r]
    for lev in range(HG_LEVELS):
        sh = HG_LEVELS - 1 - lev
        same = (r >> sh) == (t >> sh)
        gp.append(same & (t >= r))
        gn.append(same & (t < r))
    return _both_directions(gp, 1), _both_directions(gn, 1)


def _split_dot(mat, x):
    hi = x.astype(MXU_DTYPE)
    lo = (x - hi.astype(F32)).astype(MXU_DTYPE)
    return _dot(mat, hi) + _dot(mat, lo)


def _hg_gates(hq, z, a0, a1):
    q = hq * _sigmoid(hq)
    sg = _sigmoid(z)
    lb = _sigmoid(a0 - a1)
    f = lb + (1.0 - lb) * sg
    k = (1.0 - lb) * (1.0 - sg)
    return q, f, k, sg, lb


def _hg_level_masks(mirrored):
    c = HG_CHUNK
    row = lax.broadcasted_iota(jnp.int32, (c, 1), 0)
    rr = lax.broadcasted_iota(jnp.int32, (c, c), 0)
    cc = lax.broadcasted_iota(jnp.int32, (c, c), 1)
    his, sames = [], []
    for lev in range(HG_LEVELS):
        sh = HG_LEVELS - 1 - lev
        his.append(jnp.logical_xor(((row >> sh) & 1) == 1, mirrored))
        sames.append((rr >> (sh + 1)) == (cc >> (sh + 1)))
    return his, sames, rr == cc


def _hg_intra(q, k, ex, masks):
    his, sames, eye = masks
    a = jnp.where(eye, jnp.sum(q * k, axis=-1, keepdims=True), 0.0)
    for lev in range(HG_LEVELS):
        e = ex[lev + 1]
        qs = jnp.where(his[lev], q * e, 0.0)
        ks = jnp.where(his[lev], 0.0, k * e)
        a = a + jnp.where(sames[lev], _dot(qs, ks, "nt"), 0.0)
    return a


def _hg_specs(n, with_time):
    c = HG_CHUNK
    nc = n // c

    def chunk(d, i):
        first = d if with_time else 1 - d
        return i + first * (nc - 1 - 2 * i)

    def pcols(off, dir_stride=0):
        return [pl.BlockSpec((c, HG_PAIR), lambda d, i, j=j: (chunk(d, i), off // HG_PAIR + dir_stride // HG_PAIR * d + j)) for j in range(2)]

    specs = dict(
        hq=pcols(OFF_HQ), v=pcols(OFF_HI), z=pcols(OFF_ZF, OFF_ZB - OFF_ZF),
        shared=pl.BlockSpec((c, HG_DIM), lambda d, i: (chunk(d, i), 0)),
        per_dir=pl.BlockSpec((1, c, HG_DIM), lambda d, i: (d, chunk(d, i), 0)),
        vec=pl.BlockSpec((1, 1, HG_DIM), lambda d, i: (d, 0, 0)),
        seg=pl.BlockSpec((1, (HG_LEVELS + 2) * c, c), lambda d, i: (d, 0, 0)),
        sums=pl.BlockSpec((1, c, (HG_LEVELS + 1) * c), lambda d, i: (d, 0, 0)),
        state=pl.BlockSpec((1, HG_HEADS, 1, HG_HEAD_DIM, HG_HEAD_DIM), lambda d, i: (d, 0, chunk(d, i), 0, 0)))
    return nc, specs


def _hg_head(refs, hh):
    off = (hh % 2) * HG_HEAD_DIM
    return refs[hh // 2][:, off:off + HG_HEAD_DIM]


def _hg_lanes(hh):
    return slice(hh * HG_HEAD_DIM, (hh + 1) * HG_HEAD_DIM)


def _hg_exps(seg_ref, f):
    lf = jnp.log(f)
    args = _split_dot(seg_ref[0], lf)
    c = HG_CHUNK
    return [jnp.exp(args[j * c:(j + 1) * c]) for j in range(HG_LEVELS + 2)]


def _hg_last_row(a, mirrored):
    return jnp.where(mirrored, a[0:1, :], a[HG_CHUNK - 1:HG_CHUNK, :])


def _hgrn_fwd(p, a0, a1, seg, name):
    n = p.shape[0]
    nc, sp = _hg_specs(n, True)

    def body(hq0, hq1, z0, z1, v0, v1, a0_ref, a1_ref, seg_ref, o_ref, st):
        @pl.when(pl.program_id(1) == 0)
        def _():
            st[...] = jnp.zeros_like(st)

        mirrored = pl.program_id(0) == 1
        masks = _hg_level_masks(mirrored)
        for hh in range(HG_HEADS):
            ln = _hg_lanes(hh)
            q, f, k, _, _ = _hg_gates(_hg_head((hq0, hq1), hh), _hg_head((z0, z1), hh), a0_ref[0, :, ln], a1_ref[0, :, ln])
            vv = _hg_head((v0, v1), hh)
            ex = _hg_exps(seg_ref, f)
            a = _hg_intra(q, k, ex, masks)
            s_t = st[hh]
            o_ref[0, :, ln] = _dot(a, vv) + _dot(q * ex[0], s_t, "nt")
            st[hh] = s_t * _hg_last_row(ex[0], mirrored) + _dot(vv, k * ex[HG_LEVELS + 1], "tn")

    return pl.pallas_call(
        body, name=name, grid=(2, nc), in_specs=sp["hq"] + sp["z"] + sp["v"] + [sp["vec"], sp["vec"], sp["seg"]],
        out_specs=sp["per_dir"], out_shape=SDS((2, n, HG_DIM), F32),
        scratch_shapes=[pltpu.VMEM((HG_HEADS, HG_HEAD_DIM, HG_HEAD_DIM), F32)],
        compiler_params=_cp("parallel", "arbitrary"))(p, p, p, p, p, p, a0, a1, seg)


def _hgrn_bwd_q(p, a0, a1, seg, gp, do, name):
    n = p.shape[0]
    nc, sp = _hg_specs(n, True)


    def body(hq0, hq1, z0, z1, v0, v1, a0_ref, a1_ref, seg_ref, gp_ref, do_ref, dhq_ref, dlf_ref, s0_ref, st):
        @pl.when(pl.program_id(1) == 0)
        def _():
            st[...] = jnp.zeros_like(st)

        mirrored = pl.program_id(0) == 1
        his, sames, eye = _hg_level_masks(mirrored)
        for hh in range(HG_HEADS):
            ln = _hg_lanes(hh)
            s_t = st[hh]
            s0_ref[0, hh, 0] = s_t
            hqv = _hg_head((hq0, hq1), hh)
            q, f, k, _, _ = _hg_gates(hqv, _hg_head((z0, z1), hh), a0_ref[0, :, ln], a1_ref[0, :, ln])
            vv, dov = _hg_head((v0, v1), hh), do_ref[:, ln]
            ex = _hg_exps(seg_ref, f)
            da = _dot(dov, vv, "nt")
            dq_inter = ex[0] * _dot(dov, s_t)
            dq = jnp.sum(dov * vv, axis=-1, keepdims=True) * k + dq_inter
            terms = [q * dq_inter]
            for lev in range(HG_LEVELS):
                e = ex[lev + 1]
                ks = jnp.where(his[lev], 0.0, k * e)
                part = jnp.where(his[lev], e, 0.0) * _dot(jnp.where(sames[lev], da, 0.0), ks)
                dq = dq + part
                terms.append(q * part)
            dlf_ref[0, :, ln] = _dot(gp_ref[0], jnp.concatenate(terms, axis=0))
            sq = _sigmoid(hqv)
            dhq_ref[0, :, ln] = dq * sq * (1.0 + hqv * (1.0 - sq))
            st[hh] = s_t * _hg_last_row(ex[0], mirrored) + _dot(vv, k * ex[HG_LEVELS + 1], "tn")

    out = SDS((2, n, HG_DIM), F32)
    return pl.pallas_call(
        body, name=name, grid=(2, nc),
        in_specs=sp["hq"] + sp["z"] + sp["v"] + [sp["vec"], sp["vec"], sp["seg"], sp["sums"], sp["shared"]],
        out_specs=[sp["per_dir"], sp["per_dir"], sp["state"]],
        out_shape=[out, out, SDS((2, HG_HEADS, nc, HG_HEAD_DIM, HG_HEAD_DIM), F32)],
        scratch_shapes=[pltpu.VMEM((HG_HEADS, HG_HEAD_DIM, HG_HEAD_DIM), F32)],
        compiler_params=_cp("parallel", "arbitrary"))(p, p, p, p, p, p, a0, a1, seg, gp, do)


def _hgrn_bwd_kv(p, a0, a1, seg, gn, do, dlf_q, s0, name):
    n = p.shape[0]
    nc, sp = _hg_specs(n, False)

    def body(hq0, hq1, z0, z1, v0, v1, a0_ref, a1_ref, seg_ref, gn_ref, do_ref, dlfq_ref, s0_ref, dz_ref, dv_ref, dlb_ref, rt):
        @pl.when(pl.program_id(1) == 0)
        def _():
            rt[...] = jnp.zeros_like(rt)
            dlb_ref[...] = jnp.zeros_like(dlb_ref)

        mirrored = pl.program_id(0) == 1
        masks = _hg_level_masks(mirrored)
        his, sames, eye = masks
        for hh in range(HG_HEADS):
            ln = _hg_lanes(hh)
            q, f, k, sg, lb = _hg_gates(_hg_head((hq0, hq1), hh), _hg_head((z0, z1), hh), a0_ref[0, :, ln], a1_ref[0, :, ln])
            vv, dov = _hg_head((v0, v1), hh), do_ref[:, ln]
            ex = _hg_exps(seg_ref, f)
            a = _hg_intra(q, k, ex, masks)
            da = _dot(dov, vv, "nt")
            r_t = rt[hh]
            k_end = k * ex[HG_LEVELS + 1]
            dv_ref[0, :, ln] = _dot(a, dov, "tn") + _dot(k_end, r_t, "nt")
            dk_inter = ex[HG_LEVELS + 1] * _dot(vv, r_t)
            dk = jnp.sum(dov * vv, axis=-1, keepdims=True) * q + dk_inter
            terms = [k * dk_inter]
            for lev in range(HG_LEVELS):
                e = ex[lev + 1]
                qs = jnp.where(his[lev], q * e, 0.0)
                part = jnp.where(his[lev], 0.0, e) * _dot(jnp.where(sames[lev], da, 0.0), qs, "tn")
                dk = dk + part
                terms.append(k * part)
            decay = _hg_last_row(ex[0], mirrored)
            rt[hh] = r_t * decay + _dot(dov, q * ex[0], "tn")
            later = decay * jnp.sum(s0_ref[0, hh, 0] * r_t, axis=0, keepdims=True)
            dlf = dlfq_ref[0, :, ln] + _dot(gn_ref[0], jnp.concatenate(terms, axis=0)) + later
            df = dlf / f - dk
            dz_ref[0, :, ln] = df * (1.0 - lb) * sg * (1.0 - sg)
            dlb_ref[0, :, ln] += jnp.sum(df * (1.0 - sg), axis=0, keepdims=True)

    out = SDS((2, n, HG_DIM), F32)
    return pl.pallas_call(
        body, name=name, grid=(2, nc),
        in_specs=sp["hq"] + sp["z"] + sp["v"] + [sp["vec"], sp["vec"], sp["seg"], sp["sums"], sp["shared"], sp["per_dir"], sp["state"]],
        out_specs=[sp["per_dir"], sp["per_dir"], sp["vec"]], out_shape=[out, out, SDS((2, 1, HG_DIM), F32)],
        scratch_shapes=[pltpu.VMEM((HG_HEADS, HG_HEAD_DIM, HG_HEAD_DIM), F32)],
        compiler_params=_cp("parallel", "arbitrary"))(p, p, p, p, p, p, a0, a1, seg, gn, do, dlf_q, s0)


def _hg_post(o2, p, g, name):
    n = p.shape[0]
    tr = min(ROW_TILE, n)
    w = 2 * HG_HEAD_DIM

    def body(of_ref, ob_ref, hg_ref, g_ref, o_ref):
        for j in range(2):
            sl = slice(j * HG_HEAD_DIM, (j + 1) * HG_HEAD_DIM)
            o = of_ref[0, :, sl] + ob_ref[0, :, sl]
            hg = hg_ref[:, sl]
            o_ref[:, sl] = (o * _rstd(o) * g_ref[...] * (hg * _sigmoid(hg))).astype(o_ref.dtype)

    blk = pl.BlockSpec((tr, w), lambda i, j: (i, j))
    dirs = [pl.BlockSpec((1, tr, w), lambda i, j, d=d: (d, i, j)) for d in range(2)]
    return pl.pallas_call(
        body, name=name, grid=(n // tr, HG_DIM // w),
        in_specs=dirs + [pl.BlockSpec((tr, w), lambda i, j: (i, OFF_HG // w + j)), pl.BlockSpec((1, HG_HEAD_DIM), lambda i, j: (0, 0))],
        out_specs=blk, out_shape=SDS((n, HG_DIM), MXU_DTYPE), compiler_params=_cp("parallel", "parallel"))(o2, o2, p, g)


def _hg_post_bwd(o2, p, g, dcat, name):
    n = p.shape[0]
    tr = min(ROW_TILE, n)
    w = 2 * HG_HEAD_DIM

    def body(of_ref, ob_ref, hg_ref, g_ref, d_ref, do_ref, dhg_ref, dg_ref):
        @pl.when(pl.program_id(1) == 0)
        def _():
            dg_ref[...] = jnp.zeros_like(dg_ref)

        for j in range(2):
            sl = slice(j * HG_HEAD_DIM, (j + 1) * HG_HEAD_DIM)
            o = of_ref[0, :, sl] + ob_ref[0, :, sl]
            hg = hg_ref[:, sl]
            d = d_ref[:, sl].astype(F32)
            sg = _sigmoid(hg)
            on = o * _rstd(o) * g_ref[...]
            dhg_ref[:, sl] = (d * on * sg * (1.0 + hg * (1.0 - sg))).astype(dhg_ref.dtype)
            dx, dg = _rms_bwd(o, g_ref[...], d * hg * sg)
            do_ref[:, sl] = dx
            dg_ref[0, :, sl] += dg

    blk = pl.BlockSpec((tr, w), lambda j, i: (i, j))
    dirs = [pl.BlockSpec((1, tr, w), lambda j, i, d=d: (d, i, j)) for d in range(2)]
    return pl.pallas_call(
        body, name=name, grid=(HG_DIM // w, n // tr),
        in_specs=dirs + [pl.BlockSpec((tr, w), lambda j, i: (i, OFF_HG // w + j)), pl.BlockSpec((1, HG_HEAD_DIM), lambda j, i: (0, 0)),
                         pl.BlockSpec((tr, w), lambda j, i: (i, ATT_Q_DIM // w + j))],
        out_specs=[blk, blk, pl.BlockSpec((1, 1, w), lambda j, i: (j, 0, 0))],
        out_shape=[SDS((n, HG_DIM), F32), SDS((n, HG_DIM), MXU_DTYPE), SDS((HG_DIM // w, 1, w), F32)],
        compiler_params=_cp("parallel", "arbitrary"))(o2, o2, p, g, dcat)


XATT_TQ = 512


def _xattn_fwd(q, kv, name):
    n, nm = q.shape[0], kv.shape[0]
    tq = min(XATT_TQ, n)
    scale = X_HEAD_DIM ** -0.5

    def body(q_ref, k_ref, v_ref, o_ref):
        s = _dot(q_ref[...], k_ref[...], "nt") * scale
        e = jnp.exp(s - jnp.max(s, axis=-1, keepdims=True))
        o_ref[...] = _dot(e / jnp.sum(e, axis=-1, keepdims=True), v_ref[...]).astype(o_ref.dtype)

    qb = pl.BlockSpec((tq, X_HEAD_DIM), lambda h, i: (i, h))
    return pl.pallas_call(
        body, name=name, grid=(X_HEADS, n // tq),
        in_specs=[qb, pl.BlockSpec((nm, X_HEAD_DIM), lambda h, i: (0, h)), pl.BlockSpec((nm, X_HEAD_DIM), lambda h, i: (0, X_HEADS + h))],
        out_specs=qb, out_shape=SDS(q.shape, MXU_DTYPE), compiler_params=_cp("parallel", "parallel"))(q, kv, kv)


def _xattn_bwd(q, kv, do, name):
    n, nm = q.shape[0], kv.shape[0]
    tq = min(XATT_TQ, n)
    scale = X_HEAD_DIM ** -0.5

    def body(q_ref, k_ref, v_ref, do_ref, dq_ref, dk_ref, dv_ref):
        @pl.when(pl.program_id(1) == 0)
        def _():
            dk_ref[...] = jnp.zeros_like(dk_ref)
            dv_ref[...] = jnp.zeros_like(dv_ref)

        qv, dov = q_ref[...], do_ref[...]
        s = _dot(qv, k_ref[...], "nt") * scale
        e = jnp.exp(s - jnp.max(s, axis=-1, keepdims=True))
        p = e / jnp.sum(e, axis=-1, keepdims=True)
        dp = _dot(dov, v_ref[...], "nt")
        ds = p * (dp - jnp.sum(p * dp, axis=-1, keepdims=True)) * scale
        dq_ref[...] = _dot(ds, k_ref[...]).astype(dq_ref.dtype)
        dk_ref[...] += _dot(ds, qv, "tn")
        dv_ref[...] += _dot(p, dov, "tn")

    qb = pl.BlockSpec((tq, X_HEAD_DIM), lambda h, i: (i, h))
    kb = pl.BlockSpec((nm, X_HEAD_DIM), lambda h, i: (0, h))
    return pl.pallas_call(
        body, name=name, grid=(X_HEADS, n // tq),
        in_specs=[qb, kb, pl.BlockSpec((nm, X_HEAD_DIM), lambda h, i: (0, X_HEADS + h)), qb], out_specs=[qb, kb, kb],
        out_shape=[SDS(q.shape, MXU_DTYPE), SDS((nm, X_HEADS * X_HEAD_DIM), F32), SDS((nm, X_HEADS * X_HEAD_DIM), F32)],
        compiler_params=_cp("parallel", "arbitrary"))(q, kv, kv, do)


def _shift_rows(u, down):
    n = u.shape[0]
    row = lax.broadcasted_iota(jnp.int32, u.shape, 0)
    if down:
        return jnp.where(row == 0, 0.0, pltpu.roll(u, 1, axis=0))
    return jnp.where(row == n - 1, 0.0, pltpu.roll(u, n - 1, axis=0))


def _conv(u, w, b):
    return b + _shift_rows(u, True) * w[0:1, :] + u * w[1:2, :] + _shift_rows(u, False) * w[2:3, :]


def _ff_specs(n):
    gate = lambda rows: pl.BlockSpec((rows, FF_COLS), lambda j: (0, j))
    val = lambda rows: pl.BlockSpec((rows, FF_COLS), lambda j: (0, FF_BLOCKS + j))
    return [gate(n), val(n), gate(3), val(3), gate(1), val(1)], gate


def _conv_gate(u, cw, cb, name):
    n = u.shape[0]
    ins, gate_blk = _ff_specs(n)

    def body(ug_ref, uv_ref, wg_ref, wv_ref, bg_ref, bv_ref, o_ref):
        gate = _conv(ug_ref[...], wg_ref[...], bg_ref[...])
        val = _conv(uv_ref[...], wv_ref[...], bv_ref[...])
        o_ref[...] = (gate * _sigmoid(gate) * val).astype(o_ref.dtype)

    return pl.pallas_call(
        body, name=name, grid=(FF_BLOCKS,), in_specs=ins, out_specs=gate_blk(n), out_shape=SDS((n, D_FF), MXU_DTYPE),
        compiler_params=_cp("parallel"))(u, u, cw, cw, cb, cb)


def _conv_gate_bwd(u, cw, cb, da, name):
    n = u.shape[0]
    ins, gate_blk = _ff_specs(n)

    def side(dacc, u, w, du_ref, dw_ref, db_ref):
        nxt, prv = _shift_rows(dacc, False), _shift_rows(dacc, True)
        du_ref[...] = (nxt * w[0:1, :] + dacc * w[1:2, :] + prv * w[2:3, :]).astype(du_ref.dtype)
        db_ref[...] = jnp.sum(dacc, axis=0, keepdims=True)
        dw_ref[0:1, :] = jnp.sum(nxt * u, axis=0, keepdims=True)
        dw_ref[1:2, :] = jnp.sum(dacc * u, axis=0, keepdims=True)
        dw_ref[2:3, :] = jnp.sum(prv * u, axis=0, keepdims=True)

    def body(ug_ref, uv_ref, wg_ref, wv_ref, bg_ref, bv_ref, da_ref, dug_ref, duv_ref, dwg_ref, dwv_ref, dbg_ref, dbv_ref):
        ug, uv = ug_ref[...], uv_ref[...]
        gate = _conv(ug, wg_ref[...], bg_ref[...])
        val = _conv(uv, wv_ref[...], bv_ref[...])
        sg = _sigmoid(gate)
        dav = da_ref[...].astype(F32)
        side(dav * val * sg * (1.0 + gate * (1.0 - sg)), ug, wg_ref[...], dug_ref, dwg_ref, dbg_ref)
        side(dav * gate * sg, uv, wv_ref[...], duv_ref, dwv_ref, dbv_ref)

    return pl.pallas_call(
        body, name=name, grid=(FF_BLOCKS,), in_specs=ins + [gate_blk(n)],
        out_specs=[gate_blk(n), gate_blk(n), gate_blk(3), gate_blk(3), gate_blk(1), gate_blk(1)],
        out_shape=[SDS((n, D_FF), MXU_DTYPE)] * 2 + [SDS((3, D_FF), F32)] * 2 + [SDS((1, D_FF), F32)] * 2,
        compiler_params=_cp("parallel"))(u, u, cw, cw, cb, cb, da)


def _adamw(w, g, m, v, name):
    r, c = w.shape
    tr = r if r <= 512 else 256 if r % 256 == 0 else 88
    assert r % tr == 0, (name, r, tr)

    def body(w_ref, g_ref, m_ref, v_ref, d_ref, mo_ref, vo_ref):
        gv = g_ref[...]
        mn = ADAM_B1 * m_ref[...] + (1.0 - ADAM_B1) * gv
        vn = ADAM_B2 * v_ref[...] + (1.0 - ADAM_B2) * gv * gv
        m_hat = mn / (1.0 - ADAM_B1 ** ADAM_STEP)
        v_hat = vn / (1.0 - ADAM_B2 ** ADAM_STEP)
        d_ref[...] = -ADAM_LR * (m_hat / (jnp.sqrt(v_hat) + ADAM_EPS) + ADAM_WD * w_ref[...])
        mo_ref[...] = mn
        vo_ref[...] = vn

    blk = pl.BlockSpec((tr, c), lambda i: (i, 0))
    out = SDS((r, c), F32)
    return pl.pallas_call(body, name=name, grid=(r // tr,), in_specs=[blk] * 4, out_specs=[blk] * 3, out_shape=[out] * 3,
                          compiler_params=_cp("parallel"))(w, g, m, v)


def _half_tile(h):
    tr = h if h <= 512 else 256 if h % 256 == 0 else 176
    assert h % tr == 0, (h, tr)
    return tr


def _add_halves(g, r, core, name):
    s, h, c = r.shape
    tr = _half_tile(h)
    steps = h // tr

    def body(ix_ref, g_ref, r_ref, o_ref):
        o_ref[...] = (g_ref[...].astype(F32) + r_ref[...].astype(F32)).astype(o_ref.dtype)

    blk = pl.BlockSpec((1, tr, c), lambda i, j, ix: (i, j, 0))
    grid_spec = pltpu.PrefetchScalarGridSpec(
        num_scalar_prefetch=1, grid=(s, steps),
        in_specs=[pl.BlockSpec((1, tr, c), lambda i, j, ix: (i, ix[0] * steps + j, 0)), blk], out_specs=blk)
    return pl.pallas_call(body, name=name, grid_spec=grid_spec, out_shape=SDS(r.shape, WIRE_DTYPE),
                          compiler_params=_cp("parallel", "parallel"))(core.reshape(1), g, r)


def _sum_chips(own, recv, me, core, name):
    _, h, c = own.shape
    tr = _half_tile(h)

    def body(ix_ref, own_ref, recv_ref, o_ref):
        acc = own_ref[0].astype(F32)
        for j in range(3):
            acc = acc + recv_ref[j].astype(F32)
        o_ref[0] = acc

    grid_spec = pltpu.PrefetchScalarGridSpec(
        num_scalar_prefetch=1, grid=(h // tr,),
        in_specs=[pl.BlockSpec((1, tr, c), lambda i, ix: (ix[0], i, 0)), pl.BlockSpec((3, tr, c), lambda i, ix: (0, i, 0))],
        out_specs=pl.BlockSpec((1, tr, c), lambda i, ix: (ix[1], i, 0)))
    return pl.pallas_call(body, name=name, grid_spec=grid_spec, out_shape=SDS((2, h, c), F32),
                          compiler_params=_cp("parallel"))(jnp.stack([me, core]), own, recv)


ANY = pl.BlockSpec(memory_space=pl.ANY)


def _place():
    x, y, c = lax.axis_index("x"), lax.axis_index("y"), lax.axis_index("c")
    return x, y, c, [(1 - x, y), (x, 1 - y), (1 - x, 1 - y)]


def _gather_shards(shards, name):
    nt = len(shards)

    def body(*refs):
        ins, outs = refs[:nt], refs[nt:2 * nt]
        send, recv, fsend, frecv, osend, orecv = refs[2 * nt:]
        x, y, c, chips = _place()
        me = 2 * x + y

        def half(t, chip, cc):
            h = ins[t].shape[0] // 2
            return outs[t].at[chip, pl.ds(cc * h, h)]

        def ici(t, j):
            cx, cy = chips[j]
            h = ins[t].shape[0] // 2
            return pltpu.make_async_remote_copy(src_ref=ins[t].at[pl.ds(c * h, h)], dst_ref=half(t, me, c),
                                                send_sem=send.at[t, j], recv_sem=recv.at[t, j], device_id=(cx, cy, c), device_id_type=MESH)

        def landed(t, j):
            cx, cy = chips[j]
            blk = half(t, 2 * cx + cy, c)
            return pltpu.make_async_remote_copy(src_ref=blk, dst_ref=blk, send_sem=send.at[t, j], recv_sem=recv.at[t, j],
                                                device_id=(cx, cy, c), device_id_type=MESH)

        def d2d(t, j, cc):
            cx, cy = chips[j]
            blk = half(t, 2 * cx + cy, cc)
            return pltpu.make_async_remote_copy(src_ref=blk, dst_ref=blk, send_sem=fsend.at[t, j], recv_sem=frecv.at[t, j],
                                                device_id=(x, y, 1 - c), device_id_type=MESH)

        own = [pltpu.make_async_remote_copy(src_ref=ins[t], dst_ref=outs[t].at[me], send_sem=osend.at[t], recv_sem=orecv.at[t],
                                            device_id=(x, y, 1 - c), device_id_type=MESH) for t in range(nt)]
        for t in range(nt):
            for j in range(3):
                ici(t, j).start()
        for cp in own:
            cp.start()
        for t in range(nt):
            for j in range(3):
                landed(t, j).wait_recv()
                d2d(t, j, c).start()
        for t in range(nt):
            for j in range(3):
                d2d(t, j, 1 - c).wait_recv()
        for t in range(nt):
            for j in range(3):
                ici(t, j).wait_send()
                d2d(t, j, c).wait_send()
        for cp in own:
            cp.wait()

    return pl.pallas_call(
        body, name=name, in_specs=[ANY] * nt, out_specs=[ANY] * nt,
        out_shape=[SDS((4,) + s.shape, s.dtype) for s in shards],
        scratch_shapes=[pltpu.SemaphoreType.DMA((nt, 3))] * 4 + [pltpu.SemaphoreType.DMA((nt,))] * 2,
        compiler_params=pltpu.CompilerParams(has_side_effects=True))(*shards)


def _swap_halves(grads, name):
    nt = len(grads)

    def body(*refs):
        ins, outs = refs[:nt], refs[nt:2 * nt]
        send, recv = refs[2 * nt:]
        x, y, c, _ = _place()
        cps = []
        for t in range(nt):
            h = ins[t].shape[1] // 2
            cps.append(pltpu.make_async_remote_copy(src_ref=ins[t].at[pl.ds(0, 4), pl.ds((1 - c) * h, h)], dst_ref=outs[t], send_sem=send.at[t],
                                                    recv_sem=recv.at[t], device_id=(x, y, 1 - c), device_id_type=MESH))
        for cp in cps:
            cp.start()
        for cp in cps:
            cp.wait()

    return pl.pallas_call(
        body, name=name, in_specs=[ANY] * nt, out_specs=[ANY] * nt,
        out_shape=[SDS((g.shape[0], g.shape[1] // 2, g.shape[2]), g.dtype) for g in grads],
        scratch_shapes=[pltpu.SemaphoreType.DMA((nt,))] * 2,
        compiler_params=pltpu.CompilerParams(has_side_effects=True))(*grads)


def _send_to_owners(parts, name):
    nt = len(parts)

    def body(*refs):
        ins, outs = refs[:nt], refs[nt:2 * nt]
        send, recv = refs[2 * nt:]
        x, y, c, chips = _place()

        def ici(t, j):
            cx, cy = chips[j]
            return pltpu.make_async_remote_copy(src_ref=ins[t].at[2 * cx + cy], dst_ref=outs[t].at[j], send_sem=send.at[t, j],
                                                recv_sem=recv.at[t, j], device_id=(cx, cy, c), device_id_type=MESH)

        for t in range(nt):
            for j in range(3):
                ici(t, j).start()
        for t in range(nt):
            for j in range(3):
                ici(t, j).wait()

    return pl.pallas_call(
        body, name=name, in_specs=[ANY] * nt, out_specs=[ANY] * nt, out_shape=[SDS((3,) + p.shape[1:], p.dtype) for p in parts],
        scratch_shapes=[pltpu.SemaphoreType.DMA((nt, 3))] * 2,
        compiler_params=pltpu.CompilerParams(has_side_effects=True))(*parts)


def _join_halves(bufs, name):
    nt = len(bufs)

    def body(*refs):
        outs = refs[nt:2 * nt]
        send, recv = refs[2 * nt:]
        x, y, c, _ = _place()
        cps = [pltpu.make_async_remote_copy(src_ref=outs[t].at[c], dst_ref=outs[t].at[c], send_sem=send.at[t], recv_sem=recv.at[t],
                                            device_id=(x, y, 1 - c), device_id_type=MESH) for t in range(nt)]
        for cp in cps:
            cp.start()
        for t in range(nt):
            theirs = outs[t].at[1 - c]
            pltpu.make_async_remote_copy(src_ref=theirs, dst_ref=theirs, send_sem=send.at[t], recv_sem=recv.at[t],
                                         device_id=(x, y, 1 - c), device_id_type=MESH).wait_recv()
        for cp in cps:
            cp.wait_send()

    return pl.pallas_call(
        body, name=name, in_specs=[ANY] * nt, out_specs=[ANY] * nt, out_shape=[SDS(b.shape, b.dtype) for b in bufs],
        input_output_aliases={t: t for t in range(nt)},
        scratch_shapes=[pltpu.SemaphoreType.DMA((nt,))] * 2,
        compiler_params=pltpu.CompilerParams(has_side_effects=True))(*bufs)


def _exchange_small(v, reduce, name):
    rows = v.shape[0]

    def body(v_ref, o_ref, buf, send, recv):
        x, y, c, _ = _place()
        me = 4 * x + 2 * y + c
        buf[me] = v_ref[...]

        def peer(dx, dy, dc):
            return (1 - x if dx else x, 1 - y if dy else y, 1 - c if dc else c)

        peers = [(dx, dy, dc) for dx in range(2) for dy in range(2) for dc in range(2) if (dx, dy, dc) != (0, 0, 0)]
        cps = []
        for j, (dx, dy, dc) in enumerate(peers):
            cps.append(pltpu.make_async_remote_copy(src_ref=v_ref, dst_ref=buf.at[me], send_sem=send.at[j], recv_sem=recv.at[j],
                                                    device_id=peer(dx, dy, dc), device_id_type=MESH))
        for cp in cps:
            cp.start()
        for j, (dx, dy, dc) in enumerate(peers):
            px, py, pc = peer(dx, dy, dc)
            blk = buf.at[4 * px + 2 * py + pc]
            pltpu.make_async_remote_copy(src_ref=blk, dst_ref=blk, send_sem=send.at[j], recv_sem=recv.at[j],
                                         device_id=(px, py, pc), device_id_type=MESH).wait_recv()
        for cp in cps:
            cp.wait_send()
        if reduce:
            acc = buf[0]
            for j in range(1, 8):
                acc = acc + buf[j]
            o_ref[...] = acc
        else:
            o_ref[...] = buf[...]

    vm = pl.BlockSpec(memory_space=pltpu.VMEM)
    return pl.pallas_call(
        body, name=name, in_specs=[vm], out_specs=vm, out_shape=SDS((rows, 128) if reduce else (8, rows, 128), F32),
        scratch_shapes=[pltpu.VMEM((8, rows, 128), F32), pltpu.SemaphoreType.DMA((7,)), pltpu.SemaphoreType.DMA((7,))],
        compiler_params=pltpu.CompilerParams(has_side_effects=True))(v)


HBM = pl.BlockSpec(memory_space=pltpu.HBM)
SEM = pl.BlockSpec(memory_space=pltpu.SEMAPHORE)
TOKEN = pl.BlockSpec(memory_space=pltpu.VMEM)
TOKEN_SHAPE = SDS((8, 128), F32)
PEERS = 7


def _in_hbm(a):
    return pltpu.with_memory_space_constraint(a, pltpu.HBM)


def _split_params():
    return pltpu.CompilerParams(has_side_effects=pltpu.SideEffectType.DATAFLOW_SIDE_EFFECTING)


def _after(a, token):
    return lax.optimization_barrier((a, token))[0]


def _gather_start(shards, name):
    nt = len(shards)

    def body(*refs):
        ins, lands = refs[:nt], refs[nt:2 * nt]
        send, recv = refs[2 * nt:2 * nt + 2]
        x, y, c, chips = _place()
        me = 2 * x + y
        for t in range(nt):
            h = ins[t].shape[0] // 2
            mine = pl.ds(c * h, h)
            for j, (cx, cy) in enumerate(chips):
                for dc in range(2):
                    pltpu.make_async_remote_copy(src_ref=ins[t].at[mine], dst_ref=lands[t].at[me, mine], send_sem=send.at[t, 2 * j + dc],
                                                 recv_sem=recv.at[t, 2 * j + c], device_id=(cx, cy, dc), device_id_type=MESH).start()
            pltpu.make_async_remote_copy(src_ref=ins[t], dst_ref=lands[t].at[me], send_sem=send.at[t, PEERS - 1], recv_sem=recv.at[t, PEERS - 1],
                                         device_id=(x, y, 1 - c), device_id_type=MESH).start()
        refs[-1][...] = jnp.zeros(TOKEN_SHAPE.shape, F32)

    lands = [lax.empty((4,) + s.shape, s.dtype) for s in shards]
    out = pl.pallas_call(
        body, name=name, in_specs=[HBM] * (2 * nt), out_specs=[SEM, SEM] + [HBM] * (2 * nt) + [TOKEN],
        out_shape=[pltpu.SemaphoreType.DMA((nt, PEERS)), pltpu.SemaphoreType.DMA((nt, PEERS))]
        + [pltpu.HBM(s.shape, s.dtype) for s in shards] + [pltpu.HBM(l.shape, l.dtype) for l in lands] + [TOKEN_SHAPE],
        input_output_aliases={t: 2 + t for t in range(2 * nt)}, compiler_params=_split_params())(
            *[_in_hbm(s) for s in shards], *[_in_hbm(l) for l in lands])
    return out[0], out[1], out[2:2 + nt], out[2 + nt:2 + 2 * nt], out[-1]


def _gather_wait(send, recv, shards, lands, after, name):
    nt = len(shards)

    def body(*refs):
        ins, lands_ref = refs[:nt], refs[nt:2 * nt]
        send_ref, recv_ref = refs[2 * nt:2 * nt + 2]
        x, y, c, chips = _place()
        for t in range(nt):
            h = ins[t].shape[0] // 2
            for j, (cx, cy) in enumerate(chips):
                for cs in range(2):
                    blk = lands_ref[t].at[2 * cx + cy, pl.ds(cs * h, h)]
                    pltpu.make_async_remote_copy(src_ref=blk, dst_ref=blk, send_sem=send_ref.at[t, 2 * j + cs], recv_sem=recv_ref.at[t, 2 * j + cs],
                                                 device_id=(cx, cy, cs), device_id_type=MESH).wait()
            blk = lands_ref[t].at[2 * x + y]
            pltpu.make_async_remote_copy(src_ref=blk, dst_ref=blk, send_sem=send_ref.at[t, PEERS - 1], recv_sem=recv_ref.at[t, PEERS - 1],
                                         device_id=(x, y, 1 - c), device_id_type=MESH).wait()

    out = pl.pallas_call(
        body, name=name, in_specs=[HBM] * (2 * nt) + [SEM, SEM, ANY], out_specs=[HBM] * (2 * nt),
        out_shape=[pltpu.HBM(s.shape, s.dtype) for s in shards] + [pltpu.HBM(l.shape, l.dtype) for l in lands],
        input_output_aliases={t: t for t in range(2 * nt)}, compiler_params=_split_params())(*shards, *lands, send, recv, after)
    return out[nt:]


def _scatter_start(g, name):
    _, r, c_ = g.shape
    h = r // 2

    def body(g_ref, land, send, recv, g_thru, land_thru, token):
        x, y, c, chips = _place()
        for j, (cx, cy) in enumerate(chips):
            for dc in range(2):
                pltpu.make_async_remote_copy(src_ref=g_ref.at[2 * cx + cy, pl.ds(dc * h, h)], dst_ref=land.at[2 * j + c], send_sem=send.at[2 * j + dc],
                                             recv_sem=recv.at[2 * j + c], device_id=(cx, cy, dc), device_id_type=MESH).start()
        pltpu.make_async_remote_copy(src_ref=g_ref.at[2 * x + y, pl.ds((1 - c) * h, h)], dst_ref=land.at[PEERS - 1], send_sem=send.at[PEERS - 1],
                                     recv_sem=recv.at[PEERS - 1], device_id=(x, y, 1 - c), device_id_type=MESH).start()
        token[...] = jnp.zeros(TOKEN_SHAPE.shape, F32)

    land = lax.empty((PEERS, h, c_), g.dtype)
    return pl.pallas_call(
        body, name=name, in_specs=[HBM, HBM], out_specs=[SEM, SEM, HBM, HBM, TOKEN],
        out_shape=[pltpu.SemaphoreType.DMA((PEERS,)), pltpu.SemaphoreType.DMA((PEERS,)), pltpu.HBM(g.shape, g.dtype),
                   pltpu.HBM(land.shape, land.dtype), TOKEN_SHAPE],
        input_output_aliases={0: 2, 1: 3}, compiler_params=_split_params())(_in_hbm(g), _in_hbm(land))


def _scatter_wait(started, after, name):
    nt = len(started)

    def body(*refs):
        lands = refs[nt:2 * nt]
        sends, recvs = refs[2 * nt:3 * nt], refs[3 * nt:4 * nt]
        x, y, c, chips = _place()
        peers = [(cx, cy, dc) for cx, cy in chips for dc in range(2)] + [(x, y, 1 - c)]
        for t in range(nt):
            for k, peer in enumerate(peers):
                blk = lands[t].at[k]
                pltpu.make_async_remote_copy(src_ref=blk, dst_ref=blk, send_sem=sends[t].at[k], recv_sem=recvs[t].at[k],
                                             device_id=peer, device_id_type=MESH).wait()

    gs, lands = [s[2] for s in started], [s[3] for s in started]
    out = pl.pallas_call(
        body, name=name, in_specs=[HBM] * (2 * nt) + [SEM] * (2 * nt) + [ANY], out_specs=[HBM] * (2 * nt),
        out_shape=[pltpu.HBM(a.shape, a.dtype) for a in gs + lands],
        input_output_aliases={t: t for t in range(2 * nt)}, compiler_params=_split_params())(
            *gs, *lands, *[s[0] for s in started], *[s[1] for s in started], after)
    return out[:nt], out[nt:]


def _sum_devices(g, land, me, core, name):
    npeer, h, c = land.shape
    tr = _half_tile(h)
    steps = h // tr

    def body(ix_ref, own_ref, land_ref, o_ref):
        acc = own_ref[0].astype(F32)
        for j in range(npeer):
            acc = acc + land_ref[j].astype(F32)
        o_ref[0] = acc

    grid_spec = pltpu.PrefetchScalarGridSpec(
        num_scalar_prefetch=1, grid=(steps,),
        in_specs=[pl.BlockSpec((1, tr, c), lambda i, ix: (ix[0], ix[1] * steps + i, 0)), pl.BlockSpec((npeer, tr, c), lambda i, ix: (0, i, 0))],
        out_specs=pl.BlockSpec((1, tr, c), lambda i, ix: (ix[1], i, 0)))
    return pl.pallas_call(body, name=name, grid_spec=grid_spec, out_shape=SDS((2, h, c), F32),
                          compiler_params=_cp("parallel"))(jnp.stack([me, core]), g, land)


def _pack_small(parts):
    flat = jnp.concatenate([p.reshape(-1) for p in parts])
    total = flat.shape[0]
    rows = -(-total // 1024) * 8
    return jnp.pad(flat, (0, rows * 128 - total)).reshape(rows, 128)


def _unpack_small(packed, shapes):
    flat = packed.reshape(-1)
    out, off = [], 0
    for s in shapes:
        size = int(np.prod(s))
        out.append(flat[off:off + size].reshape(s))
        off += size
    return out


def _local_step(x, mem, target, w_in, mid_weights, ffn_weights, on_grad, gains, conv_w, conv_b, hg_lb):
    n = x.shape[0]
    cos, sin = _rope_tables(n)
    seg = _hg_segments()
    gp, gn = _hg_pair_sums()
    gq2 = jnp.tile(gains["q_norm_g"], (1, 2))
    gk2 = jnp.tile(gains["k_norm_g"], (1, 2))
    a0 = hg_lb[:, 0:1, :]
    a1 = hg_lb[:, 1:2, :]

    p, h1 = _norm_mm(x, gains["pre_mix_g"], w_in, F32, 512, 1664, "in_proj")
    qr, kr = _qk_prep(p, gq2, gk2, cos, sin, "qk_prep")
    heads = lambda a: a.reshape(n, ATT_KV_HEADS, ATT_HEAD_DIM).transpose(1, 0, 2)
    kh = heads(kr)
    vh = heads(p[:, OFF_AV:OFF_AV + ATT_KV_DIM].astype(MXU_DTYPE))
    att = _attn_fwd(qr, kh, vh, "attn_fwd")
    o2 = _hgrn_fwd(p, a0, a1, seg, "hgrn_fwd")
    rec = _hg_post(o2, p, gains["hg_out_norm_g"], "hg_post")
    cat = jnp.concatenate([att, rec], axis=1)
    w_out, w_xq, w_xkv, w_xo = mid_weights(cat)
    mixed = _mm(cat, w_out, "nn", F32, 512, 1024, "out_proj")
    x1 = _resid_norm(x, mixed, gains["post_mix_g"], "mix_resid")
    xq, h2 = _norm_mm(x1, gains["pre_x_g"], w_xq, MXU_DTYPE, 512, 1024, "xq_proj")
    kv, mn = _norm_mm(mem, gains["mem_norm_g"], w_xkv, MXU_DTYPE, 256, 2048, "xkv_proj")
    ox = _xattn_fwd(xq, kv, "xattn_fwd")
    xo = _mm(ox, w_xo, "nn", F32, 512, 1024, "xo_proj")
    x2 = _resid_norm(x1, xo, gains["post_x_g"], "x_resid")
    w_up, w_down = ffn_weights(x2)
    u, h3 = _norm_mm(x2, gains["pre_ffn_g"], w_up, F32, 512, 512, "up_proj")
    act = _conv_gate(u, conv_w, conv_b, "conv_gate")
    dn = _mm(act, w_down, "nn", F32, 512, 1024, "down_proj")
    d3, loss = _final_loss(x2, dn, gains["post_ffn_g"], target, "ffn_resid_loss")

    gs = {}

    def hand_over(name, g, nxt):
        token = on_grad(name, g)
        return nxt if token is None else _after(nxt, token)

    d_dn, gs["post_ffn_g"] = _norm_bwd(dn, gains["post_ffn_g"], d3, None, MXU_DTYPE, "ffn_post_bwd")
    d_dn = hand_over("w_down", _mm(act, d_dn, "tn", WIRE_DTYPE, 1408, 1024, "down_dw"), d_dn)
    d_act = _mm(d_dn, w_down, "nt", F32, 512, 1408, "down_dx")
    du_g, du_v, dcw_g, dcw_v, dcb_g, dcb_v = _conv_gate_bwd(u, conv_w, conv_b, d_act, "conv_gate_bwd")
    gs["conv_w"] = jnp.concatenate([dcw_g, dcw_v], axis=1)
    gs["conv_b"] = jnp.concatenate([dcb_g, dcb_v], axis=1)
    g_up = (_mm(h3, du_g, "tn", WIRE_DTYPE, 512, 1408, "up_dw_gate"), _mm(h3, du_v, "tn", WIRE_DTYPE, 512, 1408, "up_dw_value"))
    du_g = hand_over("w_up", g_up, du_g)
    d_h3 = _mm_nt_halves(du_g, du_v, w_up, F32, 512, 512, "up_dx")
    d2, gs["pre_ffn_g"] = _norm_bwd(x2, gains["pre_ffn_g"], d_h3, d3, F32, "ffn_pre_bwd")
    d_xo, gs["post_x_g"] = _norm_bwd(xo, gains["post_x_g"], d2, None, MXU_DTYPE, "x_post_bwd")
    d_xo = hand_over("w_xo", _mm(ox, d_xo, "tn", WIRE_DTYPE, 512, 1024, "xo_dw"), d_xo)
    d_ox = _mm(d_xo, w_xo, "nt", MXU_DTYPE, 512, 1024, "xo_dx")
    d_xq, d_k, d_v = _xattn_bwd(xq, kv, d_ox, "xattn_bwd")
    d_kv = jnp.concatenate([d_k, d_v], axis=1).astype(MXU_DTYPE)
    d_xq = hand_over("w_xq", _mm(h2, d_xq, "tn", WIRE_DTYPE, 512, 1024, "xq_dw"), d_xq)
    d_h2 = _mm(d_xq, w_xq, "nt", F32, 512, 1024, "xq_dx")
    d_kv = hand_over("w_xkv", _mm(mn, d_kv, "tn", WIRE_DTYPE, 512, 1024, "xkv_dw"), d_kv)
    d_mn = _mm(d_kv, w_xkv, "nt", F32, 256, 1024, "xkv_dx")
    _, gs["mem_norm_g"] = _norm_bwd(mem, gains["mem_norm_g"], d_mn, None, MXU_DTYPE, "mem_norm_bwd")
    d1, gs["pre_x_g"] = _norm_bwd(x1, gains["pre_x_g"], d_h2, d2, F32, "x_pre_bwd")
    d_mixed, gs["post_mix_g"] = _norm_bwd(mixed, gains["post_mix_g"], d1, None, MXU_DTYPE, "mix_post_bwd")
    d_mixed = hand_over("w_out", _mm(cat, d_mixed, "tn", WIRE_DTYPE, 512, 1024, "out_dw"), d_mixed)
    d_cat = _mm(d_mixed, w_out, "nt", MXU_DTYPE, 512, 1024, "out_dx")
    d_o, d_hg, dg_hg = _hg_post_bwd(o2, p, gains["hg_out_norm_g"], d_cat, "hg_post_bwd")
    gs["hg_out_norm_g"] = dg_hg.reshape(HG_HEADS, HG_HEAD_DIM).sum(axis=0, keepdims=True)
    dhq2, dlf_q, s0 = _hgrn_bwd_q(p, a0, a1, seg, gp, d_o, "hgrn_bwd_q")
    dz2, dhv2, dlb = _hgrn_bwd_kv(p, a0, a1, seg, gn, d_o, dlf_q, s0, "hgrn_bwd_kv")
    lb = jax.nn.sigmoid(a0 - a1)
    da0 = dlb * lb * (1.0 - lb)
    gs["hg_lb"] = jnp.concatenate([da0, -da0], axis=1)
    d_qr, d_kh, d_vh = _attn_bwd(qr, kh, vh, d_cat, "attn_bwd")
    unheads = lambda a: a.transpose(2, 0, 1).reshape(n, ATT_KV_DIM)
    d_aq, d_ak, dgq, dgk = _qk_prep_bwd(p, gq2, gk2, cos, sin, d_qr, unheads(d_kh), "qk_prep_bwd")
    gs["q_norm_g"] = dgq.reshape(ATT_HEADS, ATT_HEAD_DIM).sum(axis=0, keepdims=True)
    gs["k_norm_g"] = dgk.reshape(ATT_KV_HEADS, ATT_HEAD_DIM).sum(axis=0, keepdims=True)
    d_p = jnp.concatenate([d_aq, d_ak, unheads(d_vh).astype(MXU_DTYPE), (dhq2[0] + dhq2[1]).astype(MXU_DTYPE),
                           dz2[0].astype(MXU_DTYPE), dz2[1].astype(MXU_DTYPE), (dhv2[0] + dhv2[1]).astype(MXU_DTYPE), d_hg], axis=1)
    d_p = hand_over("w_in", _mm(h1, d_p, "tn", WIRE_DTYPE, 512, 1664, "in_dw"), d_p)
    d_h1 = _mm(d_p, w_in, "nt", F32, 512, 1024, "in_dx")
    grad_x, gs["pre_mix_g"] = _norm_bwd(x, gains["pre_mix_g"], d_h1, d1, F32, "mix_pre_bwd")
    return loss, grad_x, gs


MATS = ("w_in", "w_out", "w_xq", "w_xkv", "w_xo", "w_up", "w_down")
COL_SHARDED = ("w_in", "w_xkv", "w_up")
GAINS = ("pre_mix_g", "q_norm_g", "k_norm_g", "hg_out_norm_g", "post_mix_g", "pre_x_g", "mem_norm_g", "post_x_g", "pre_ffn_g", "post_ffn_g")
WEIGHTS = ('pre_mix_g', 'w_in', 'q_norm_g', 'k_norm_g', 'hg_lb', 'hg_out_norm_g', 'w_out', 'post_mix_g', 'pre_x_g', 'mem_norm_g', 'w_xq',
           'w_xkv', 'w_xo', 'post_x_g', 'pre_ffn_g', 'w_up', 'conv_w', 'conv_b', 'w_down', 'post_ffn_g')


def kernel(x, mem, pre_mix_g, w_in, q_norm_g, k_norm_g, hg_lb, hg_out_norm_g, w_out, post_mix_g, pre_x_g, mem_norm_g, w_xq, w_xkv, w_xo, post_x_g, pre_ffn_g, w_up, conv_w, conv_b, w_down, post_ffn_g, loss_target, m_pre_mix_g, m_w_in, m_q_norm_g, m_k_norm_g, m_hg_lb, m_hg_out_norm_g, m_w_out, m_post_mix_g, m_pre_x_g, m_mem_norm_g, m_w_xq, m_w_xkv, m_w_xo, m_post_x_g, m_pre_ffn_g, m_w_up, m_conv_w, m_conv_b, m_w_down, m_post_ffn_g, v_pre_mix_g, v_w_in, v_q_norm_g, v_k_norm_g, v_hg_lb, v_hg_out_norm_g, v_w_out, v_post_mix_g, v_pre_x_g, v_mem_norm_g, v_w_xq, v_w_xkv, v_w_xo, v_post_x_g, v_pre_ffn_g, v_w_up, v_conv_w, v_conv_b, v_w_down, v_post_ffn_g):
    args = dict(locals())
    w = {k: args[k] for k in WEIGHTS}
    m = {k: args["m_" + k] for k in WEIGHTS}
    v = {k: args["v_" + k] for k in WEIGHTS}
    chip = 2 * lax.axis_index("x") + lax.axis_index("y")
    core = lax.axis_index("c")

    shards = {k: w[k][0].astype(WIRE_DTYPE) for k in MATS}

    def whole(k, g):
        return jnp.concatenate([g[s] for s in range(4)], axis=1) if k in COL_SHARDED else g.reshape(-1, g.shape[-1])

    mid_names, ffn_names = ("w_out", "w_xq", "w_xkv", "w_xo"), ("w_up", "w_down")
    gathered = dict(zip(MATS, _gather_shards([shards[k] for k in MATS], "gather_weights")))
    w_in_full = whole("w_in", gathered["w_in"])

    def mid_weights(after):
        return [whole(k, gathered[k]) for k in mid_names]

    def ffn_weights(after):
        return [whole(k, gathered[k]) for k in ffn_names]

    small_in = _exchange_small(_pack_small([w["conv_w"][0], w["hg_lb"]]), False, "gather_small")
    cw_parts, lb_parts = [], []
    for s in range(4):
        cw_s, lb_s = _unpack_small(small_in[2 * s], [w["conv_w"][0].shape, w["hg_lb"].shape])
        cw_parts.append(cw_s)
        lb_parts.append(lb_s)
    conv_w_full = jnp.concatenate(cw_parts, axis=1)
    hg_lb_full = jnp.concatenate(lb_parts, axis=2)

    def cols_by_owner(g, chips):
        return g.reshape(g.shape[0], chips, g.shape[1] // chips).transpose(1, 0, 2)

    started = {}

    def on_grad(k, g):
        if k == "w_up":
            by_owner = jnp.concatenate([cols_by_owner(g[0], 2), cols_by_owner(g[1], 2)], axis=0)
        elif k in COL_SHARDED:
            by_owner = cols_by_owner(g, 4)
        else:
            by_owner = g.reshape(4, g.shape[0] // 4, g.shape[1])
        *started[k], token = _scatter_start(by_owner, "grad_start_" + k)
        return token

    gains = {k: w[k] for k in GAINS}
    loss, grad_x, gs = _local_step(x[0], mem[0], loss_target[0], w_in_full, mid_weights, ffn_weights, on_grad, gains,
                                   conv_w_full, w["conv_b"], hg_lb_full)
    loss = lax.psum(loss[0, 0], ("x", "y", "c"))

    sent, landed = _scatter_wait([started[k] for k in MATS], grad_x, "grad_wait")
    halves = [_sum_devices(g, land, chip, core, "grad_sum_" + k) for k, g, land in zip(MATS, sent, landed)]
    reduced = _join_halves(halves, "grad_join_halves")
    grads = {k: r.reshape(1, -1, r.shape[-1]) for k, r in zip(MATS, reduced)}

    small_names = GAINS + ("conv_b", "conv_w", "hg_lb")
    small_shapes = [gs[k].shape for k in small_names]
    summed = _unpack_small(_exchange_small(_pack_small([gs[k] for k in small_names]), True, "reduce_small"), small_shapes)
    for k, g in zip(small_names, summed):
        grads[k] = g
    ncw = w["conv_w"].shape[2]
    grads["conv_w"] = lax.dynamic_slice_in_dim(grads["conv_w"], chip * ncw, ncw, axis=1)[None]
    nlb = w["hg_lb"].shape[2]
    grads["hg_lb"] = lax.dynamic_slice_in_dim(grads["hg_lb"], chip * nlb, nlb, axis=2)

    delta, new_m, new_v = {}, {}, {}
    for k in WEIGHTS:
        shape = w[k].shape
        two_d = lambda a: a.reshape(-1, shape[-1])
        d, mo, vo = _adamw(two_d(w[k]), two_d(grads[k]), two_d(m[k]), two_d(v[k]), "adamw_" + k)
        delta[k], new_m[k], new_v[k] = d.reshape(shape), mo.reshape(shape), vo.reshape(shape)
        grads[k] = grads[k].reshape(shape)
    return (loss, grad_x[None], *[grads[k] for k in WEIGHTS], *[delta[k] for k in WEIGHTS],
            *[new_m[k] for k in WEIGHTS], *[new_v[k] for k in WEIGHTS])
```

```python
import functools

import numpy as np
import jax
import jax.numpy as jnp
from jax import lax
from jax.experimental import pallas as pl
from jax.experimental.pallas import tpu as pltpu

F32 = jnp.float32
MXU_DTYPE = jnp.bfloat16
WIRE_DTYPE = jnp.bfloat16
VMEM_LIMIT_BYTES = 56 * 1024 * 1024
EPS = 1e-6
MESH = pl.DeviceIdType.MESH

D_MODEL = 1024
GRID_W = 64
ATT_HEADS, ATT_KV_HEADS, ATT_HEAD_DIM = 8, 2, 64
ATT_GROUP = ATT_HEADS // ATT_KV_HEADS
ATT_Q_DIM, ATT_KV_DIM = 512, 128
ROPE_THETA = 10000.0
HG_HEADS, HG_HEAD_DIM, HG_DIM = 4, 128, 512
HG_CHUNK = 128
HG_LEVELS = 7
HG_PAIR = 2 * HG_HEAD_DIM
X_HEADS, X_HEAD_DIM = 4, 256
D_FF = 2816
FF_COLS = 256
FF_BLOCKS = D_FF // FF_COLS
N_IN = 3328
OFF_AQ, OFF_AK, OFF_AV, OFF_HQ, OFF_ZF, OFF_ZB, OFF_HI, OFF_HG = 0, 512, 640, 768, 1280, 1792, 2304, 2816

ADAM_LR, ADAM_B1, ADAM_B2, ADAM_EPS, ADAM_WD, ADAM_STEP = 0.001, 0.9, 0.999, 1e-08, 0.01, 10

SDS = jax.ShapeDtypeStruct


def _cp(*sem):
    return pltpu.CompilerParams(dimension_semantics=sem, vmem_limit_bytes=VMEM_LIMIT_BYTES)


def _dot(a, b, form="nn"):
    dims = {"nn": (((1,), (0,)), ((), ())), "nt": (((1,), (1,)), ((), ())), "tn": (((0,), (0,)), ((), ()))}[form]
    return lax.dot_general(a.astype(MXU_DTYPE), b.astype(MXU_DTYPE), dims, preferred_element_type=F32)


def _sigmoid(x):
    return 1.0 / (1.0 + jnp.exp(-x))


def _rstd(x):
    return lax.rsqrt(jnp.mean(x * x, axis=-1, keepdims=True) + EPS)


def _rms_bwd(x, g, dy):
    r = _rstd(x)
    xh = x * r
    dn = dy * g
    dx = r * (dn - xh * jnp.mean(dn * xh, axis=-1, keepdims=True))
    return dx, jnp.sum(dy * xh, axis=0, keepdims=True)


def _unread(after):
    after = tuple(a for a in after if a is not None)
    return after, [pl.BlockSpec(memory_space=pl.ANY)] * len(after)


def _mm(a, b, form, out_dtype, tm, tn, name, after=()):
    after, after_specs = _unread(after)
    if form == "nn":
        (m, k), n = a.shape, b.shape[1]
    elif form == "nt":
        (m, k), n = a.shape, b.shape[0]
    else:
        (k, m), n = a.shape, b.shape[1]
    tm, tn = min(tm, m), min(tn, n)
    assert m % tm == 0 and n % tn == 0, (name, m, n, tm, tn)

    def body(a_ref, b_ref, *rest):
        o_ref = rest[-1]
        o_ref[...] = _dot(a_ref[...], b_ref[...], form).astype(o_ref.dtype)

    a_spec = pl.BlockSpec((k, tm), lambda i, j: (0, i)) if form == "tn" else pl.BlockSpec((tm, k), lambda i, j: (i, 0))
    b_spec = pl.BlockSpec((tn, k), lambda i, j: (j, 0)) if form == "nt" else pl.BlockSpec((k, tn), lambda i, j: (0, j))
    return pl.pallas_call(
        body, name=name, grid=(m // tm, n // tn), in_specs=[a_spec, b_spec] + after_specs,
        out_specs=pl.BlockSpec((tm, tn), lambda i, j: (i, j)), out_shape=SDS((m, n), out_dtype),
        compiler_params=_cp("parallel", "parallel"))(a, b, *after)


def _mm_nt_halves(a0, a1, b, out_dtype, tm, tn, name, after=()):
    after, after_specs = _unread(after)
    m, kh = a0.shape
    n = b.shape[0]
    tm, tn = min(tm, m), min(tn, n)
    assert m % tm == 0 and n % tn == 0 and b.shape[1] == 2 * kh, (name, m, n, tm, tn)

    def body(a0_ref, a1_ref, b0_ref, b1_ref, *rest):
        o_ref = rest[-1]
        o_ref[...] = (_dot(a0_ref[...], b0_ref[...], "nt") + _dot(a1_ref[...], b1_ref[...], "nt")).astype(o_ref.dtype)

    a_spec = pl.BlockSpec((tm, kh), lambda i, j: (i, 0))
    return pl.pallas_call(
        body, name=name, grid=(m // tm, n // tn),
        in_specs=[a_spec, a_spec, pl.BlockSpec((tn, kh), lambda i, j: (j, 0)), pl.BlockSpec((tn, kh), lambda i, j: (j, 1))] + after_specs,
        out_specs=pl.BlockSpec((tm, tn), lambda i, j: (i, j)), out_shape=SDS((m, n), out_dtype),
        compiler_params=_cp("parallel", "parallel"))(a0, a1, b, b, *after)


def _norm_mm(x, g, w, out_dtype, tm, tn, name, after=()):
    after, after_specs = _unread(after)
    m, d = x.shape
    n = w.shape[1]
    tm, tn = min(tm, m), min(tn, n)
    assert m % tm == 0 and n % tn == 0, (name, m, n, tm, tn)

    def body(x_ref, g_ref, w_ref, *rest):
        o_ref, h_ref, hs = rest[-3:]

        @pl.when(pl.program_id(1) == 0)
        def _():
            xv = x_ref[...]
            h = (xv * _rstd(xv) * g_ref[...]).astype(MXU_DTYPE)
            hs[...] = h
            h_ref[...] = h

        o_ref[...] = _dot(hs[...], w_ref[...]).astype(o_ref.dtype)

    return pl.pallas_call(
        body, name=name, grid=(m // tm, n // tn),
        in_specs=[pl.BlockSpec((tm, d), lambda i, j: (i, 0)), pl.BlockSpec((1, d), lambda i, j: (0, 0)),
                  pl.BlockSpec((d, tn), lambda i, j: (0, j))] + after_specs,
        out_specs=[pl.BlockSpec((tm, tn), lambda i, j: (i, j)), pl.BlockSpec((tm, d), lambda i, j: (i, 0))],
        out_shape=[SDS((m, n), out_dtype), SDS((m, d), MXU_DTYPE)],
        scratch_shapes=[pltpu.VMEM((tm, d), MXU_DTYPE)],
        compiler_params=_cp("parallel", "arbitrary"))(x, g, w, *after)


ROW_TILE = 256


def _resid_norm(x, y, g, name):
    n, d = x.shape
    tr = min(ROW_TILE, n)

    def body(x_ref, y_ref, g_ref, o_ref):
        yv = y_ref[...]
        o_ref[...] = x_ref[...] + yv * _rstd(yv) * g_ref[...]

    row = pl.BlockSpec((tr, d), lambda i: (i, 0))
    return pl.pallas_call(
        body, name=name, grid=(n // tr,), in_specs=[row, row, pl.BlockSpec((1, d), lambda i: (0, 0))],
        out_specs=row, out_shape=SDS((n, d), F32), compiler_params=_cp("parallel"))(x, y, g)


def _norm_bwd(x, g, dy, res, out_dtype, name):
    n, d = x.shape
    tr = min(ROW_TILE, n)
    has_res = res is not None

    def body(*refs):
        x_ref, g_ref, dy_ref = refs[:3]
        dx_ref, dg_ref = refs[-2:]
        dx, dg = _rms_bwd(x_ref[...], g_ref[...], dy_ref[...].astype(F32))
        if has_res:
            dx = dx + refs[3][...]
        dx_ref[...] = dx.astype(dx_ref.dtype)

        @pl.when(pl.program_id(0) == 0)
        def _():
            dg_ref[...] = jnp.zeros_like(dg_ref)

        dg_ref[...] += dg

    row = pl.BlockSpec((tr, d), lambda i: (i, 0))
    vec = pl.BlockSpec((1, d), lambda i: (0, 0))
    ins = [x, g, dy] + ([res] if has_res else [])
    return pl.pallas_call(
        body, name=name, grid=(n // tr,), in_specs=[row, vec, row] + ([row] if has_res else []),
        out_specs=[row, vec], out_shape=[SDS((n, d), out_dtype), SDS((1, d), F32)],
        compiler_params=_cp("arbitrary"))(*ins)


def _final_loss(x, y, g, target, name):
    n, d = x.shape
    tr = min(ROW_TILE, n)

    def body(x_ref, y_ref, g_ref, t_ref, d_ref, l_ref):
        yv = y_ref[...]
        diff = x_ref[...] + yv * _rstd(yv) * g_ref[...] - t_ref[...]
        d_ref[...] = diff * (1.0 / d)

        @pl.when(pl.program_id(0) == 0)
        def _():
            l_ref[...] = jnp.zeros_like(l_ref)

        l_ref[...] += 0.5 * jnp.sum(jnp.mean(diff * diff, axis=-1, keepdims=True), axis=0, keepdims=True)

    row = pl.BlockSpec((tr, d), lambda i: (i, 0))
    return pl.pallas_call(
        body, name=name, grid=(n // tr,), in_specs=[row, row, pl.BlockSpec((1, d), lambda i: (0, 0)), row],
        out_specs=[row, pl.BlockSpec((1, 1), lambda i: (0, 0))], out_shape=[SDS((n, d), F32), SDS((1, 1), F32)],
        compiler_params=_cp("arbitrary"))(x, y, g, target)


def _rope_tables(n):
    pairs = ATT_HEAD_DIM // 4
    t = np.arange(n)
    inv = np.power(ROPE_THETA, -np.arange(pairs, dtype=np.float32) / pairs).astype(np.float32)
    ang = np.concatenate([(t // GRID_W)[:, None].astype(np.float32) * inv, (t % GRID_W)[:, None].astype(np.float32) * inv], axis=-1)
    cos = np.repeat(np.cos(ang), 2, axis=-1)
    sin = np.repeat(np.sin(ang), 2, axis=-1) * np.tile(np.array([-1.0, 1.0], np.float32), ATT_HEAD_DIM // 2)
    return jnp.asarray(np.tile(cos, 2), F32), jnp.asarray(np.tile(sin, 2), F32)


def _swap_pairs(x):
    lane = lax.broadcasted_iota(jnp.int32, x.shape, 1)
    return jnp.where((lane & 1) == 0, pltpu.roll(x, 127, axis=1), pltpu.roll(x, 1, axis=1))


def _head_mean(v):
    lane = lax.broadcasted_iota(jnp.int32, v.shape, 1)
    lo = jnp.where(lane < ATT_HEAD_DIM, v, 0.0)
    s0 = jnp.sum(lo, axis=-1, keepdims=True)
    s1 = jnp.sum(v - lo, axis=-1, keepdims=True)
    return jnp.where(lane < ATT_HEAD_DIM, s0, s1) * (1.0 / ATT_HEAD_DIM)


def _qk_prep(p, gq, gk, cos, sin, name):
    n = p.shape[0]
    tr = min(ROW_TILE, n)

    def one(xv, g, c, s):
        xn = xv * lax.rsqrt(_head_mean(xv * xv) + EPS) * g
        return xn * c + _swap_pairs(xn) * s

    def body(q_ref, k_ref, gq_ref, gk_ref, c_ref, s_ref, qo_ref, ko_ref):
        c, s = c_ref[...], s_ref[...]
        for j in range(ATT_Q_DIM // 128):
            qo_ref[:, j * 128:(j + 1) * 128] = one(q_ref[:, j * 128:(j + 1) * 128], gq_ref[...], c, s).astype(qo_ref.dtype)
        ko_ref[...] = one(k_ref[...], gk_ref[...], c, s).astype(ko_ref.dtype)

    vec = pl.BlockSpec((1, 128), lambda i: (0, 0))
    tab = pl.BlockSpec((tr, 128), lambda i: (i, 0))
    return pl.pallas_call(
        body, name=name, grid=(n // tr,),
        in_specs=[pl.BlockSpec((tr, ATT_Q_DIM), lambda i: (i, 0)), pl.BlockSpec((tr, 128), lambda i: (i, OFF_AK // 128)), vec, vec, tab, tab],
        out_specs=[pl.BlockSpec((tr, ATT_Q_DIM), lambda i: (i, 0)), tab],
        out_shape=[SDS((n, ATT_Q_DIM), MXU_DTYPE), SDS((n, ATT_KV_DIM), MXU_DTYPE)],
        compiler_params=_cp("parallel"))(p, p, gq, gk, cos, sin)


def _qk_prep_bwd(p, gq, gk, cos, sin, dq, dk, name):
    n = p.shape[0]
    tr = min(ROW_TILE, n)

    def one(xv, g, c, s, dout):
        dxn = dout * c + _swap_pairs(dout * s)
        r = lax.rsqrt(_head_mean(xv * xv) + EPS)
        xh = xv * r
        dn = dxn * g
        dx = r * (dn - xh * _head_mean(dn * xh))
        return dx, jnp.sum(dxn * xh, axis=0, keepdims=True)

    def body(q_ref, k_ref, gq_ref, gk_ref, c_ref, s_ref, dq_ref, dk_ref, dqo_ref, dko_ref, dgq_ref, dgk_ref):
        @pl.when(pl.program_id(0) == 0)
        def _():
            dgq_ref[...] = jnp.zeros_like(dgq_ref)
            dgk_ref[...] = jnp.zeros_like(dgk_ref)

        c, s = c_ref[...], s_ref[...]
        for j in range(ATT_Q_DIM // 128):
            sl = slice(j * 128, (j + 1) * 128)
            dx, dg = one(q_ref[:, sl], gq_ref[...], c, s, dq_ref[:, sl])
            dqo_ref[:, sl] = dx.astype(dqo_ref.dtype)
            dgq_ref[:, sl] += dg
        dx, dg = one(k_ref[...], gk_ref[...], c, s, dk_ref[...])
        dko_ref[...] = dx.astype(dko_ref.dtype)
        dgk_ref[...] += dg

    vec = pl.BlockSpec((1, 128), lambda i: (0, 0))
    tab = pl.BlockSpec((tr, 128), lambda i: (i, 0))
    qrow = pl.BlockSpec((tr, ATT_Q_DIM), lambda i: (i, 0))
    return pl.pallas_call(
        body, name=name, grid=(n // tr,),
        in_specs=[qrow, pl.BlockSpec((tr, 128), lambda i: (i, OFF_AK // 128)), vec, vec, tab, tab, qrow, tab],
        out_specs=[qrow, tab, pl.BlockSpec((1, ATT_Q_DIM), lambda i: (0, 0)), vec],
        out_shape=[SDS((n, ATT_Q_DIM), MXU_DTYPE), SDS((n, ATT_KV_DIM), MXU_DTYPE), SDS((1, ATT_Q_DIM), F32), SDS((1, 128), F32)],
        compiler_params=_cp("arbitrary"))(p, p, gq, gk, cos, sin, dq, dk)


ATT_TQ = 256


def _attn_fwd(q, k, v, name):
    n = q.shape[0]
    tq = min(ATT_TQ, n)
    scale = ATT_HEAD_DIM ** -0.5
    gw = ATT_GROUP * ATT_HEAD_DIM

    def body(q_ref, k_ref, v_ref, o_ref):
        kk, vv = k_ref[0], v_ref[0]
        outs = []
        for g in range(ATT_GROUP):
            s = _dot(q_ref[:, g * ATT_HEAD_DIM:(g + 1) * ATT_HEAD_DIM] * scale, kk, "nt")
            e = jnp.exp(s - jnp.max(s, axis=-1, keepdims=True))
            outs.append(_dot(e, vv) / jnp.sum(e, axis=-1, keepdims=True))
        o_ref[...] = jnp.concatenate(outs, axis=-1).astype(o_ref.dtype)

    kv = pl.BlockSpec((1, n, ATT_HEAD_DIM), lambda h, i: (h, 0, 0))
    return pl.pallas_call(
        body, name=name, grid=(ATT_KV_HEADS, n // tq),
        in_specs=[pl.BlockSpec((tq, gw), lambda h, i: (i, h)), kv, kv],
        out_specs=pl.BlockSpec((tq, gw), lambda h, i: (i, h)), out_shape=SDS((n, ATT_Q_DIM), MXU_DTYPE),
        compiler_params=_cp("parallel", "parallel"))(q, k, v)


def _attn_bwd(q, k, v, do, name):
    n = q.shape[0]
    tq = min(ATT_TQ, n)
    scale = ATT_HEAD_DIM ** -0.5
    gw = ATT_GROUP * ATT_HEAD_DIM

    def body(q_ref, k_ref, v_ref, do_ref, dq_ref, dk_ref, dv_ref):
        @pl.when(pl.program_id(1) == 0)
        def _():
            dk_ref[...] = jnp.zeros_like(dk_ref)
            dv_ref[...] = jnp.zeros_like(dv_ref)

        kk, vv = k_ref[0], v_ref[0]
        dqs = []
        dk_acc = jnp.zeros((ATT_HEAD_DIM, n), F32)
        dv_acc = jnp.zeros((ATT_HEAD_DIM, n), F32)
        for g in range(ATT_GROUP):
            sl = slice(g * ATT_HEAD_DIM, (g + 1) * ATT_HEAD_DIM)
            qg, dog = q_ref[:, sl] * scale, do_ref[:, sl].astype(F32)
            s = _dot(qg, kk, "nt")
            e = jnp.exp(s - jnp.max(s, axis=-1, keepdims=True))
            inv = 1.0 / jnp.sum(e, axis=-1, keepdims=True)
            delta = jnp.sum(dog * (_dot(e, vv) * inv), axis=-1, keepdims=True)
            dse = e * (_dot(dog, vv, "nt") - delta)
            dqs.append(_dot(dse, kk) * (inv * scale))
            dk_acc += _dot(qg.astype(F32) * inv, dse, "tn")
            dv_acc += _dot(dog * inv, e, "tn")
        dq_ref[...] = jnp.concatenate(dqs, axis=-1)
        dk_ref[0] += dk_acc
        dv_ref[0] += dv_acc

    kv = pl.BlockSpec((1, n, ATT_HEAD_DIM), lambda h, i: (h, 0, 0))
    kvt = pl.BlockSpec((1, ATT_HEAD_DIM, n), lambda h, i: (h, 0, 0))
    qb = pl.BlockSpec((tq, gw), lambda h, i: (i, h))
    return pl.pallas_call(
        body, name=name, grid=(ATT_KV_HEADS, n // tq), in_specs=[qb, kv, kv, qb], out_specs=[qb, kvt, kvt],
        out_shape=[SDS((n, ATT_Q_DIM), F32), SDS((ATT_KV_HEADS, ATT_HEAD_DIM, n), F32), SDS((ATT_KV_HEADS, ATT_HEAD_DIM, n), F32)],
        compiler_params=_cp("parallel", "arbitrary"))(q, k, v, do)


def _both_directions(mats, axis):
    fwd = np.concatenate(mats, axis=axis).astype(np.float32)
    bwd = np.concatenate([m[::-1, ::-1] for m in mats], axis=axis).astype(np.float32)
    return jnp.asarray(np.stack([fwd, bwd]), MXU_DTYPE)


def _hg_segments():
    c = HG_CHUNK
    t = np.arange(c)[:, None]
    r = np.arange(c)[None, :]
    mats = [(r <= t)]
    for lev in range(HG_LEVELS):
        h = c >> (lev + 1)
        mid = (t // (2 * h)) * (2 * h) + h - 1
        hi = (t // h) % 2 == 1
        mats.append(np.where(hi, (r > mid) & (r <= t), (r > t) & (r <= mid)))
    mats.append(r > t)
    return _both_directions(mats, 0)


def _hg_pair_sums():
    c = HG_CHUNK
    r = np.arange(c)[:, None]
    t = np.arange(c)[None, :]
    gp, gn = [t >= r], [t < r]
    for lev in range(HG_LEVELS):
        sh = HG_LEVELS - 1 - lev
        same = (r >> sh) == (t >> sh)
        gp.append(same & (t >= r))
        gn.append(same & (t < r))
    return _both_directions(gp, 1), _both_directions(gn, 1)


def _split_dot(mat, x):
    hi = x.astype(MXU_DTYPE)
    lo = (x - hi.astype(F32)).astype(MXU_DTYPE)
    return _dot(mat, hi) + _dot(mat, lo)


def _hg_gates(hq, z, a0, a1):
    q = hq * _sigmoid(hq)
    sg = _sigmoid(z)
    lb = _sigmoid(a0 - a1)
    f = lb + (1.0 - lb) * sg
    k = (1.0 - lb) * (1.0 - sg)
    return q, f, k, sg, lb


def _hg_level_masks(mirrored):
    c = HG_CHUNK
    row = lax.broadcasted_iota(jnp.int32, (c, 1), 0)
    rr = lax.broadcasted_iota(jnp.int32, (c, c), 0)
    cc = lax.broadcasted_iota(jnp.int32, (c, c), 1)
    his, sames = [], []
    for lev in range(HG_LEVELS):
        sh = HG_LEVELS - 1 - lev
        his.append(jnp.logical_xor(((row >> sh) & 1) == 1, mirrored))
        sames.append((rr >> (sh + 1)) == (cc >> (sh + 1)))
    return his, sames, rr == cc


def _hg_intra(q, k, ex, masks):
    his, sames, eye = masks
    a = jnp.where(eye, jnp.sum(q * k, axis=-1, keepdims=True), 0.0)
    for lev in range(HG_LEVELS):
        e = ex[lev + 1]
        qs = jnp.where(his[lev], q * e, 0.0)
        ks = jnp.where(his[lev], 0.0, k * e)
        a = a + jnp.where(sames[lev], _dot(qs, ks, "nt"), 0.0)
    return a


def _hg_specs(n, with_time):
    c = HG_CHUNK
    nc = n // c

    def chunk(d, i):
        first = d if with_time else 1 - d
        return i + first * (nc - 1 - 2 * i)

    def pcols(off, dir_stride=0):
        return [pl.BlockSpec((c, HG_PAIR), lambda d, i, j=j: (chunk(d, i), off // HG_PAIR + dir_stride // HG_PAIR * d + j)) for j in range(2)]

    specs = dict(
        hq=pcols(OFF_HQ), v=pcols(OFF_HI), z=pcols(OFF_ZF, OFF_ZB - OFF_ZF),
        shared=pl.BlockSpec((c, HG_DIM), lambda d, i: (chunk(d, i), 0)),
        per_dir=pl.BlockSpec((1, c, HG_DIM), lambda d, i: (d, chunk(d, i), 0)),
        vec=pl.BlockSpec((1, 1, HG_DIM), lambda d, i: (d, 0, 0)),
        seg=pl.BlockSpec((1, (HG_LEVELS + 2) * c, c), lambda d, i: (d, 0, 0)),
        sums=pl.BlockSpec((1, c, (HG_LEVELS + 1) * c), lambda d, i: (d, 0, 0)),
        state=pl.BlockSpec((1, HG_HEADS, 1, HG_HEAD_DIM, HG_HEAD_DIM), lambda d, i: (d, 0, chunk(d, i), 0, 0)))
    return nc, specs


def _hg_head(refs, hh):
    off = (hh % 2) * HG_HEAD_DIM
    return refs[hh // 2][:, off:off + HG_HEAD_DIM]


def _hg_lanes(hh):
    return slice(hh * HG_HEAD_DIM, (hh + 1) * HG_HEAD_DIM)


def _hg_exps(seg_ref, f):
    lf = jnp.log(f)
    args = _split_dot(seg_ref[0], lf)
    c = HG_CHUNK
    return [jnp.exp(args[j * c:(j + 1) * c]) for j in range(HG_LEVELS + 2)]


def _hg_last_row(a, mirrored):
    return jnp.where(mirrored, a[0:1, :], a[HG_CHUNK - 1:HG_CHUNK, :])


def _hgrn_fwd(p, a0, a1, seg, name):
    n = p.shape[0]
    nc, sp = _hg_specs(n, True)

    def body(hq0, hq1, z0, z1, v0, v1, a0_ref, a1_ref, seg_ref, o_ref, st):
        @pl.when(pl.program_id(1) == 0)
        def _():
            st[...] = jnp.zeros_like(st)

        mirrored = pl.program_id(0) == 1
        masks = _hg_level_masks(mirrored)
        for hh in range(HG_HEADS):
            ln = _hg_lanes(hh)
            q, f, k, _, _ = _hg_gates(_hg_head((hq0, hq1), hh), _hg_head((z0, z1), hh), a0_ref[0, :, ln], a1_ref[0, :, ln])
            vv = _hg_head((v0, v1), hh)
            ex = _hg_exps(seg_ref, f)
            a = _hg_intra(q, k, ex, masks)
            s_t = st[hh]
            o_ref[0, :, ln] = _dot(a, vv) + _dot(q * ex[0], s_t, "nt")
            st[hh] = s_t * _hg_last_row(ex[0], mirrored) + _dot(vv, k * ex[HG_LEVELS + 1], "tn")

    return pl.pallas_call(
        body, name=name, grid=(2, nc), in_specs=sp["hq"] + sp["z"] + sp["v"] + [sp["vec"], sp["vec"], sp["seg"]],
        out_specs=sp["per_dir"], out_shape=SDS((2, n, HG_DIM), F32),
        scratch_shapes=[pltpu.VMEM((HG_HEADS, HG_HEAD_DIM, HG_HEAD_DIM), F32)],
        compiler_params=_cp("parallel", "arbitrary"))(p, p, p, p, p, p, a0, a1, seg)


def _hgrn_bwd_q(p, a0, a1, seg, gp, do, name):
    n = p.shape[0]
    nc, sp = _hg_specs(n, True)


    def body(hq0, hq1, z0, z1, v0, v1, a0_ref, a1_ref, seg_ref, gp_ref, do_ref, dhq_ref, dlf_ref, s0_ref, st):
        @pl.when(pl.program_id(1) == 0)
        def _():
            st[...] = jnp.zeros_like(st)

        mirrored = pl.program_id(0) == 1
        his, sames, eye = _hg_level_masks(mirrored)
        for hh in range(HG_HEADS):
            ln = _hg_lanes(hh)
            s_t = st[hh]
            s0_ref[0, hh, 0] = s_t
            hqv = _hg_head((hq0, hq1), hh)
            q, f, k, _, _ = _hg_gates(hqv, _hg_head((z0, z1), hh), a0_ref[0, :, ln], a1_ref[0, :, ln])
            vv, dov = _hg_head((v0, v1), hh), do_ref[:, ln]
            ex = _hg_exps(seg_ref, f)
            da = _dot(dov, vv, "nt")
            dq_inter = ex[0] * _dot(dov, s_t)
            dq = jnp.sum(dov * vv, axis=-1, keepdims=True) * k + dq_inter
            terms = [q * dq_inter]
            for lev in range(HG_LEVELS):
                e = ex[lev + 1]
                ks = jnp.where(his[lev], 0.0, k * e)
                part = jnp.where(his[lev], e, 0.0) * _dot(jnp.where(sames[lev], da, 0.0), ks)
                dq = dq + part
                terms.append(q * part)
            dlf_ref[0, :, ln] = _dot(gp_ref[0], jnp.concatenate(terms, axis=0))
            sq = _sigmoid(hqv)
            dhq_ref[0, :, ln] = dq * sq * (1.0 + hqv * (1.0 - sq))
            st[hh] = s_t * _hg_last_row(ex[0], mirrored) + _dot(vv, k * ex[HG_LEVELS + 1], "tn")

    out = SDS((2, n, HG_DIM), F32)
    return pl.pallas_call(
        body, name=name, grid=(2, nc),
        in_specs=sp["hq"] + sp["z"] + sp["v"] + [sp["vec"], sp["vec"], sp["seg"], sp["sums"], sp["shared"]],
        out_specs=[sp["per_dir"], sp["per_dir"], sp["state"]],
        out_shape=[out, out, SDS((2, HG_HEADS, nc, HG_HEAD_DIM, HG_HEAD_DIM), F32)],
        scratch_shapes=[pltpu.VMEM((HG_HEADS, HG_HEAD_DIM, HG_HEAD_DIM), F32)],
        compiler_params=_cp("parallel", "arbitrary"))(p, p, p, p, p, p, a0, a1, seg, gp, do)


def _hgrn_bwd_kv(p, a0, a1, seg, gn, do, dlf_q, s0, name):
    n = p.shape[0]
    nc, sp = _hg_specs(n, False)

    def body(hq0, hq1, z0, z1, v0, v1, a0_ref, a1_ref, seg_ref, gn_ref, do_ref, dlfq_ref, s0_ref, dz_ref, dv_ref, dlb_ref, rt):
        @pl.when(pl.program_id(1) == 0)
        def _():
            rt[...] = jnp.zeros_like(rt)
            dlb_ref[...] = jnp.zeros_like(dlb_ref)

        mirrored = pl.program_id(0) == 1
        masks = _hg_level_masks(mirrored)
        his, sames, eye = masks
        for hh in range(HG_HEADS):
            ln = _hg_lanes(hh)
            q, f, k, sg, lb = _hg_gates(_hg_head((hq0, hq1), hh), _hg_head((z0, z1), hh), a0_ref[0, :, ln], a1_ref[0, :, ln])
            vv, dov = _hg_head((v0, v1), hh), do_ref[:, ln]
            ex = _hg_exps(seg_ref, f)
            a = _hg_intra(q, k, ex, masks)
            da = _dot(dov, vv, "nt")
            r_t = rt[hh]
            k_end = k * ex[HG_LEVELS + 1]
            dv_ref[0, :, ln] = _dot(a, dov, "tn") + _dot(k_end, r_t, "nt")
            dk_inter = ex[HG_LEVELS + 1] * _dot(vv, r_t)
            dk = jnp.sum(dov * vv, axis=-1, keepdims=True) * q + dk_inter
            terms = [k * dk_inter]
            for lev in range(HG_LEVELS):
                e = ex[lev + 1]
                qs = jnp.where(his[lev], q * e, 0.0)
                part = jnp.where(his[lev], 0.0, e) * _dot(jnp.where(sames[lev], da, 0.0), qs, "tn")
                dk = dk + part
                terms.append(k * part)
            decay = _hg_last_row(ex[0], mirrored)
            rt[hh] = r_t * decay + _dot(dov, q * ex[0], "tn")
            later = decay * jnp.sum(s0_ref[0, hh, 0] * r_t, axis=0, keepdims=True)
            dlf = dlfq_ref[0, :, ln] + _dot(gn_ref[0], jnp.concatenate(terms, axis=0)) + later
            df = dlf / f - dk
            dz_ref[0, :, ln] = df * (1.0 - lb) * sg * (1.0 - sg)
            dlb_ref[0, :, ln] += jnp.sum(df * (1.0 - sg), axis=0, keepdims=True)

    out = SDS((2, n, HG_DIM), F32)
    return pl.pallas_call(
        body, name=name, grid=(2, nc),
        in_specs=sp["hq"] + sp["z"] + sp["v"] + [sp["vec"], sp["vec"], sp["seg"], sp["sums"], sp["shared"], sp["per_dir"], sp["state"]],
        out_specs=[sp["per_dir"], sp["per_dir"], sp["vec"]], out_shape=[out, out, SDS((2, 1, HG_DIM), F32)],
        scratch_shapes=[pltpu.VMEM((HG_HEADS, HG_HEAD_DIM, HG_HEAD_DIM), F32)],
        compiler_params=_cp("parallel", "arbitrary"))(p, p, p, p, p, p, a0, a1, seg, gn, do, dlf_q, s0)


def _hg_post(o2, p, g, name):
    n = p.shape[0]
    tr = min(ROW_TILE, n)
    w = 2 * HG_HEAD_DIM

    def body(of_ref, ob_ref, hg_ref, g_ref, o_ref):
        for j in range(2):
            sl = slice(j * HG_HEAD_DIM, (j + 1) * HG_HEAD_DIM)
            o = of_ref[0, :, sl] + ob_ref[0, :, sl]
            hg = hg_ref[:, sl]
            o_ref[:, sl] = (o * _rstd(o) * g_ref[...] * (hg * _sigmoid(hg))).astype(o_ref.dtype)

    blk = pl.BlockSpec((tr, w), lambda i, j: (i, j))
    dirs = [pl.BlockSpec((1, tr, w), lambda i, j, d=d: (d, i, j)) for d in range(2)]
    return pl.pallas_call(
        body, name=name, grid=(n // tr, HG_DIM // w),
        in_specs=dirs + [pl.BlockSpec((tr, w), lambda i, j: (i, OFF_HG // w + j)), pl.BlockSpec((1, HG_HEAD_DIM), lambda i, j: (0, 0))],
        out_specs=blk, out_shape=SDS((n, HG_DIM), MXU_DTYPE), compiler_params=_cp("parallel", "parallel"))(o2, o2, p, g)


def _hg_post_bwd(o2, p, g, dcat, name):
    n = p.shape[0]
    tr = min(ROW_TILE, n)
    w = 2 * HG_HEAD_DIM

    def body(of_ref, ob_ref, hg_ref, g_ref, d_ref, do_ref, dhg_ref, dg_ref):
        @pl.when(pl.program_id(1) == 0)
        def _():
            dg_ref[...] = jnp.zeros_like(dg_ref)

        for j in range(2):
            sl = slice(j * HG_HEAD_DIM, (j + 1) * HG_HEAD_DIM)
            o = of_ref[0, :, sl] + ob_ref[0, :, sl]
            hg = hg_ref[:, sl]
            d = d_ref[:, sl].astype(F32)
            sg = _sigmoid(hg)
            on = o * _rstd(o) * g_ref[...]
            dhg_ref[:, sl] = (d * on * sg * (1.0 + hg * (1.0 - sg))).astype(dhg_ref.dtype)
            dx, dg = _rms_bwd(o, g_ref[...], d * hg * sg)
            do_ref[:, sl] = dx
            dg_ref[0, :, sl] += dg

    blk = pl.BlockSpec((tr, w), lambda j, i: (i, j))
    dirs = [pl.BlockSpec((1, tr, w), lambda j, i, d=d: (d, i, j)) for d in range(2)]
    return pl.pallas_call(
        body, name=name, grid=(HG_DIM // w, n // tr),
        in_specs=dirs + [pl.BlockSpec((tr, w), lambda j, i: (i, OFF_HG // w + j)), pl.BlockSpec((1, HG_HEAD_DIM), lambda j, i: (0, 0)),
                         pl.BlockSpec((tr, w), lambda j, i: (i, ATT_Q_DIM // w + j))],
        out_specs=[blk, blk, pl.BlockSpec((1, 1, w), lambda j, i: (j, 0, 0))],
        out_shape=[SDS((n, HG_DIM), F32), SDS((n, HG_DIM), MXU_DTYPE), SDS((HG_DIM // w, 1, w), F32)],
        compiler_params=_cp("parallel", "arbitrary"))(o2, o2, p, g, dcat)


XATT_TQ = 512


def _xattn_fwd(q, kv, name):
    n, nm = q.shape[0], kv.shape[0]
    tq = min(XATT_TQ, n)
    scale = X_HEAD_DIM ** -0.5

    def body(q_ref, k_ref, v_ref, o_ref):
        s = _dot(q_ref[...], k_ref[...], "nt") * scale
        e = jnp.exp(s - jnp.max(s, axis=-1, keepdims=True))
        o_ref[...] = _dot(e / jnp.sum(e, axis=-1, keepdims=True), v_ref[...]).astype(o_ref.dtype)

    qb = pl.BlockSpec((tq, X_HEAD_DIM), lambda h, i: (i, h))
    return pl.pallas_call(
        body, name=name, grid=(X_HEADS, n // tq),
        in_specs=[qb, pl.BlockSpec((nm, X_HEAD_DIM), lambda h, i: (0, h)), pl.BlockSpec((nm, X_HEAD_DIM), lambda h, i: (0, X_HEADS + h))],
        out_specs=qb, out_shape=SDS(q.shape, MXU_DTYPE), compiler_params=_cp("parallel", "parallel"))(q, kv, kv)


def _xattn_bwd(q, kv, do, name):
    n, nm = q.shape[0], kv.shape[0]
    tq = min(XATT_TQ, n)
    scale = X_HEAD_DIM ** -0.5

    def body(q_ref, k_ref, v_ref, do_ref, dq_ref, dk_ref, dv_ref):
        @pl.when(pl.program_id(1) == 0)
        def _():
            dk_ref[...] = jnp.zeros_like(dk_ref)
            dv_ref[...] = jnp.zeros_like(dv_ref)

        qv, dov = q_ref[...], do_ref[...]
        s = _dot(qv, k_ref[...], "nt") * scale
        e = jnp.exp(s - jnp.max(s, axis=-1, keepdims=True))
        p = e / jnp.sum(e, axis=-1, keepdims=True)
        dp = _dot(dov, v_ref[...], "nt")
        ds = p * (dp - jnp.sum(p * dp, axis=-1, keepdims=True)) * scale
        dq_ref[...] = _dot(ds, k_ref[...]).astype(dq_ref.dtype)
        dk_ref[...] += _dot(ds, qv, "tn")
        dv_ref[...] += _dot(p, dov, "tn")

    qb = pl.BlockSpec((tq, X_HEAD_DIM), lambda h, i: (i, h))
    kb = pl.BlockSpec((nm, X_HEAD_DIM), lambda h, i: (0, h))
    return pl.pallas_call(
        body, name=name, grid=(X_HEADS, n // tq),
        in_specs=[qb, kb, pl.BlockSpec((nm, X_HEAD_DIM), lambda h, i: (0, X_HEADS + h)), qb], out_specs=[qb, kb, kb],
        out_shape=[SDS(q.shape, MXU_DTYPE), SDS((nm, X_HEADS * X_HEAD_DIM), F32), SDS((nm, X_HEADS * X_HEAD_DIM), F32)],
        compiler_params=_cp("parallel", "arbitrary"))(q, kv, kv, do)


def _shift_rows(u, down):
    n = u.shape[0]
    row = lax.broadcasted_iota(jnp.int32, u.shape, 0)
    if down:
        return jnp.where(row == 0, 0.0, pltpu.roll(u, 1, axis=0))
    return jnp.where(row == n - 1, 0.0, pltpu.roll(u, n - 1, axis=0))


def _conv(u, w, b):
    return b + _shift_rows(u, True) * w[0:1, :] + u * w[1:2, :] + _shift_rows(u, False) * w[2:3, :]


def _ff_specs(n):
    gate = lambda rows: pl.BlockSpec((rows, FF_COLS), lambda j: (0, j))
    val = lambda rows: pl.BlockSpec((rows, FF_COLS), lambda j: (0, FF_BLOCKS + j))
    return [gate(n), val(n), gate(3), val(3), gate(1), val(1)], gate


def _conv_gate(u, cw, cb, name):
    n = u.shape[0]
    ins, gate_blk = _ff_specs(n)

    def body(ug_ref, uv_ref, wg_ref, wv_ref, bg_ref, bv_ref, o_ref):
        gate = _conv(ug_ref[...], wg_ref[...], bg_ref[...])
        val = _conv(uv_ref[...], wv_ref[...], bv_ref[...])
        o_ref[...] = (gate * _sigmoid(gate) * val).astype(o_ref.dtype)

    return pl.pallas_call(
        body, name=name, grid=(FF_BLOCKS,), in_specs=ins, out_specs=gate_blk(n), out_shape=SDS((n, D_FF), MXU_DTYPE),
        compiler_params=_cp("parallel"))(u, u, cw, cw, cb, cb)


def _conv_gate_bwd(u, cw, cb, da, name):
    n = u.shape[0]
    ins, gate_blk = _ff_specs(n)

    def side(dacc, u, w, du_ref, dw_ref, db_ref):
        nxt, prv = _shift_rows(dacc, False), _shift_rows(dacc, True)
        du_ref[...] = (nxt * w[0:1, :] + dacc * w[1:2, :] + prv * w[2:3, :]).astype(du_ref.dtype)
        db_ref[...] = jnp.sum(dacc, axis=0, keepdims=True)
        dw_ref[0:1, :] = jnp.sum(nxt * u, axis=0, keepdims=True)
        dw_ref[1:2, :] = jnp.sum(dacc * u, axis=0, keepdims=True)
        dw_ref[2:3, :] = jnp.sum(prv * u, axis=0, keepdims=True)

    def body(ug_ref, uv_ref, wg_ref, wv_ref, bg_ref, bv_ref, da_ref, dug_ref, duv_ref, dwg_ref, dwv_ref, dbg_ref, dbv_ref):
        ug, uv = ug_ref[...], uv_ref[...]
        gate = _conv(ug, wg_ref[...], bg_ref[...])
        val = _conv(uv, wv_ref[...], bv_ref[...])
        sg = _sigmoid(gate)
        dav = da_ref[...].astype(F32)
        side(dav * val * sg * (1.0 + gate * (1.0 - sg)), ug, wg_ref[...], dug_ref, dwg_ref, dbg_ref)
        side(dav * gate * sg, uv, wv_ref[...], duv_ref, dwv_ref, dbv_ref)

    return pl.pallas_call(
        body, name=name, grid=(FF_BLOCKS,), in_specs=ins + [gate_blk(n)],
        out_specs=[gate_blk(n), gate_blk(n), gate_blk(3), gate_blk(3), gate_blk(1), gate_blk(1)],
        out_shape=[SDS((n, D_FF), MXU_DTYPE)] * 2 + [SDS((3, D_FF), F32)] * 2 + [SDS((1, D_FF), F32)] * 2,
        compiler_params=_cp("parallel"))(u, u, cw, cw, cb, cb, da)


def _adamw(w, g, m, v, name):
    r, c = w.shape
    tr = r if r <= 512 else 256 if r % 256 == 0 else 88
    assert r % tr == 0, (name, r, tr)

    def body(w_ref, g_ref, m_ref, v_ref, d_ref, mo_ref, vo_ref):
        gv = g_ref[...]
        mn = ADAM_B1 * m_ref[...] + (1.0 - ADAM_B1) * gv
        vn = ADAM_B2 * v_ref[...] + (1.0 - ADAM_B2) * gv * gv
        m_hat = mn / (1.0 - ADAM_B1 ** ADAM_STEP)
        v_hat = vn / (1.0 - ADAM_B2 ** ADAM_STEP)
        d_ref[...] = -ADAM_LR * (m_hat / (jnp.sqrt(v_hat) + ADAM_EPS) + ADAM_WD * w_ref[...])
        mo_ref[...] = mn
        vo_ref[...] = vn

    blk = pl.BlockSpec((tr, c), lambda i: (i, 0))
    out = SDS((r, c), F32)
    return pl.pallas_call(body, name=name, grid=(r // tr,), in_specs=[blk] * 4, out_specs=[blk] * 3, out_shape=[out] * 3,
                          compiler_params=_cp("parallel"))(w, g, m, v)


def _half_tile(h):
    tr = h if h <= 512 else 256 if h % 256 == 0 else 176
    assert h % tr == 0, (h, tr)
    return tr


def _add_halves(g, r, core, name):
    s, h, c = r.shape
    tr = _half_tile(h)
    steps = h // tr

    def body(ix_ref, g_ref, r_ref, o_ref):
        o_ref[...] = (g_ref[...].astype(F32) + r_ref[...].astype(F32)).astype(o_ref.dtype)

    blk = pl.BlockSpec((1, tr, c), lambda i, j, ix: (i, j, 0))
    grid_spec = pltpu.PrefetchScalarGridSpec(
        num_scalar_prefetch=1, grid=(s, steps),
        in_specs=[pl.BlockSpec((1, tr, c), lambda i, j, ix: (i, ix[0] * steps + j, 0)), blk], out_specs=blk)
    return pl.pallas_call(body, name=name, grid_spec=grid_spec, out_shape=SDS(r.shape, WIRE_DTYPE),
                          compiler_params=_cp("parallel", "parallel"))(core.reshape(1), g, r)


def _sum_chips(own, recv, me, core, name):
    _, h, c = own.shape
    tr = _half_tile(h)

    def body(ix_ref, own_ref, recv_ref, o_ref):
        acc = own_ref[0].astype(F32)
        for j in range(3):
            acc = acc + recv_ref[j].astype(F32)
        o_ref[0] = acc

    grid_spec = pltpu.PrefetchScalarGridSpec(
        num_scalar_prefetch=1, grid=(h // tr,),
        in_specs=[pl.BlockSpec((1, tr, c), lambda i, ix: (ix[0], i, 0)), pl.BlockSpec((3, tr, c), lambda i, ix: (0, i, 0))],
        out_specs=pl.BlockSpec((1, tr, c), lambda i, ix: (ix[1], i, 0)))
    return pl.pallas_call(body, name=name, grid_spec=grid_spec, out_shape=SDS((2, h, c), F32),
                          compiler_params=_cp("parallel"))(jnp.stack([me, core]), own, recv)


ANY = pl.BlockSpec(memory_space=pl.ANY)


def _place():
    x, y, c = lax.axis_index("x"), lax.axis_index("y"), lax.axis_index("c")
    return x, y, c, [(1 - x, y), (x, 1 - y), (1 - x, 1 - y)]


def _gather_shards(shards, name):
    nt = len(shards)

    def body(*refs):
        ins, outs = refs[:nt], refs[nt:2 * nt]
        send, recv, fsend, frecv, osend, orecv = refs[2 * nt:]
        x, y, c, chips = _place()
        me = 2 * x + y

        def half(t, chip, cc):
            h = ins[t].shape[0] // 2
            return outs[t].at[chip, pl.ds(cc * h, h)]

        def ici(t, j):
            cx, cy = chips[j]
            h = ins[t].shape[0] // 2
            return pltpu.make_async_remote_copy(src_ref=ins[t].at[pl.ds(c * h, h)], dst_ref=half(t, me, c),
                                                send_sem=send.at[t, j], recv_sem=recv.at[t, j], device_id=(cx, cy, c), device_id_type=MESH)

        def landed(t, j):
            cx, cy = chips[j]
            blk = half(t, 2 * cx + cy, c)
            return pltpu.make_async_remote_copy(src_ref=blk, dst_ref=blk, send_sem=send.at[t, j], recv_sem=recv.at[t, j],
                                                device_id=(cx, cy, c), device_id_type=MESH)

        def d2d(t, j, cc):
            cx, cy = chips[j]
            blk = half(t, 2 * cx + cy, cc)
            return pltpu.make_async_remote_copy(src_ref=blk, dst_ref=blk, send_sem=fsend.at[t, j], recv_sem=frecv.at[t, j],
                                                device_id=(x, y, 1 - c), device_id_type=MESH)

        own = [pltpu.make_async_remote_copy(src_ref=ins[t], dst_ref=outs[t].at[me], send_sem=osend.at[t], recv_sem=orecv.at[t],
                                            device_id=(x, y, 1 - c), device_id_type=MESH) for t in range(nt)]
        for t in range(nt):
            for j in range(3):
                ici(t, j).start()
        for cp in own:
            cp.start()
        for t in range(nt):
            for j in range(3):
                landed(t, j).wait_recv()
                d2d(t, j, c).start()
        for t in range(nt):
            for j in range(3):
                d2d(t, j, 1 - c).wait_recv()
        for t in range(nt):
            for j in range(3):
                ici(t, j).wait_send()
                d2d(t, j, c).wait_send()
        for cp in own:
            cp.wait()

    return pl.pallas_call(
        body, name=name, in_specs=[ANY] * nt, out_specs=[ANY] * nt,
        out_shape=[SDS((4,) + s.shape, s.dtype) for s in shards],
        scratch_shapes=[pltpu.SemaphoreType.DMA((nt, 3))] * 4 + [pltpu.SemaphoreType.DMA((nt,))] * 2,
        compiler_params=pltpu.CompilerParams(has_side_effects=True))(*shards)


def _swap_halves(grads, name):
    nt = len(grads)

    def body(*refs):
        ins, outs = refs[:nt], refs[nt:2 * nt]
        send, recv = refs[2 * nt:]
        x, y, c, _ = _place()
        cps = []
        for t in range(nt):
            h = ins[t].shape[1] // 2
            cps.append(pltpu.make_async_remote_copy(src_ref=ins[t].at[pl.ds(0, 4), pl.ds((1 - c) * h, h)], dst_ref=outs[t], send_sem=send.at[t],
                                                    recv_sem=recv.at[t], device_id=(x, y, 1 - c), device_id_type=MESH))
        for cp in cps:
            cp.start()
        for cp in cps:
            cp.wait()

    return pl.pallas_call(
        body, name=name, in_specs=[ANY] * nt, out_specs=[ANY] * nt,
        out_shape=[SDS((g.shape[0], g.shape[1] // 2, g.shape[2]), g.dtype) for g in grads],
        scratch_shapes=[pltpu.SemaphoreType.DMA((nt,))] * 2,
        compiler_params=pltpu.CompilerParams(has_side_effects=True))(*grads)


def _send_to_owners(parts, name):
    nt = len(parts)

    def body(*refs):
        ins, outs = refs[:nt], refs[nt:2 * nt]
        send, recv = refs[2 * nt:]
        x, y, c, chips = _place()

        def ici(t, j):
            cx, cy = chips[j]
            return pltpu.make_async_remote_copy(src_ref=ins[t].at[2 * cx + cy], dst_ref=outs[t].at[j], send_sem=send.at[t, j],
                                                recv_sem=recv.at[t, j], device_id=(cx, cy, c), device_id_type=MESH)

        for t in range(nt):
            for j in range(3):
                ici(t, j).start()
        for t in range(nt):
            for j in range(3):
                ici(t, j).wait()

    return pl.pallas_call(
        body, name=name, in_specs=[ANY] * nt, out_specs=[ANY] * nt, out_shape=[SDS((3,) + p.shape[1:], p.dtype) for p in parts],
        scratch_shapes=[pltpu.SemaphoreType.DMA((nt, 3))] * 2,
        compiler_params=pltpu.CompilerParams(has_side_effects=True))(*parts)


def _join_halves(bufs, name):
    nt = len(bufs)

    def body(*refs):
        outs = refs[nt:2 * nt]
        send, recv = refs[2 * nt:]
        x, y, c, _ = _place()
        cps = [pltpu.make_async_remote_copy(src_ref=outs[t].at[c], dst_ref=outs[t].at[c], send_sem=send.at[t], recv_sem=recv.at[t],
                                            device_id=(x, y, 1 - c), device_id_type=MESH) for t in range(nt)]
        for cp in cps:
            cp.start()
        for t in range(nt):
            theirs = outs[t].at[1 - c]
            pltpu.make_async_remote_copy(src_ref=theirs, dst_ref=theirs, send_sem=send.at[t], recv_sem=recv.at[t],
                                         device_id=(x, y, 1 - c), device_id_type=MESH).wait_recv()
        for cp in cps:
            cp.wait_send()

    return pl.pallas_call(
        body, name=name, in_specs=[ANY] * nt, out_specs=[ANY] * nt, out_shape=[SDS(b.shape, b.dtype) for b in bufs],
        input_output_aliases={t: t for t in range(nt)},
        scratch_shapes=[pltpu.SemaphoreType.DMA((nt,))] * 2,
        compiler_params=pltpu.CompilerParams(has_side_effects=True))(*bufs)


def _exchange_small(v, reduce, name):
    rows = v.shape[0]

    def body(v_ref, o_ref, buf, send, recv):
        x, y, c, _ = _place()
        me = 4 * x + 2 * y + c
        buf[me] = v_ref[...]

        def peer(dx, dy, dc):
            return (1 - x if dx else x, 1 - y if dy else y, 1 - c if dc else c)

        peers = [(dx, dy, dc) for dx in range(2) for dy in range(2) for dc in range(2) if (dx, dy, dc) != (0, 0, 0)]
        cps = []
        for j, (dx, dy, dc) in enumerate(peers):
            cps.append(pltpu.make_async_remote_copy(src_ref=v_ref, dst_ref=buf.at[me], send_sem=send.at[j], recv_sem=recv.at[j],
                                                    device_id=peer(dx, dy, dc), device_id_type=MESH))
        for cp in cps:
            cp.start()
        for j, (dx, dy, dc) in enumerate(peers):
            px, py, pc = peer(dx, dy, dc)
            blk = buf.at[4 * px + 2 * py + pc]
            pltpu.make_async_remote_copy(src_ref=blk, dst_ref=blk, send_sem=send.at[j], recv_sem=recv.at[j],
                                         device_id=(px, py, pc), device_id_type=MESH).wait_recv()
        for cp in cps:
            cp.wait_send()
        if reduce:
            acc = buf[0]
            for j in range(1, 8):
                acc = acc + buf[j]
            o_ref[...] = acc
        else:
            o_ref[...] = buf[...]

    vm = pl.BlockSpec(memory_space=pltpu.VMEM)
    return pl.pallas_call(
        body, name=name, in_specs=[vm], out_specs=vm, out_shape=SDS((rows, 128) if reduce else (8, rows, 128), F32),
        scratch_shapes=[pltpu.VMEM((8, rows, 128), F32), pltpu.SemaphoreType.DMA((7,)), pltpu.SemaphoreType.DMA((7,))],
        compiler_params=pltpu.CompilerParams(has_side_effects=True))(v)


HBM = pl.BlockSpec(memory_space=pltpu.HBM)
SEM = pl.BlockSpec(memory_space=pltpu.SEMAPHORE)
TOKEN = pl.BlockSpec(memory_space=pltpu.VMEM)
TOKEN_SHAPE = SDS((8, 128), F32)
PEERS = 7


def _in_hbm(a):
    return pltpu.with_memory_space_constraint(a, pltpu.HBM)


def _split_params():
    return pltpu.CompilerParams(has_side_effects=pltpu.SideEffectType.DATAFLOW_SIDE_EFFECTING)


def _gather_start(shards, name, after=()):
    nt = len(shards)
    after, after_specs = _unread(after)

    def body(*refs):
        ins, lands = refs[:nt], refs[nt:2 * nt]
        outs = refs[2 * nt + len(after):]
        sends, recvs = outs[:nt], outs[nt:2 * nt]
        x, y, c, chips = _place()
        me = 2 * x + y
        for t in range(nt):
            h = ins[t].shape[0] // 2
            mine = pl.ds(c * h, h)
            for j, (cx, cy) in enumerate(chips):
                for dc in range(2):
                    pltpu.make_async_remote_copy(src_ref=ins[t].at[mine], dst_ref=lands[t].at[me, mine], send_sem=sends[t].at[2 * j + dc],
                                                 recv_sem=recvs[t].at[2 * j + c], device_id=(cx, cy, dc), device_id_type=MESH).start()
            pltpu.make_async_remote_copy(src_ref=ins[t], dst_ref=lands[t].at[me], send_sem=sends[t].at[PEERS - 1], recv_sem=recvs[t].at[PEERS - 1],
                                         device_id=(x, y, 1 - c), device_id_type=MESH).start()
        outs[-1][...] = jnp.zeros(TOKEN_SHAPE.shape, F32)

    lands = [lax.empty((4,) + s.shape, s.dtype) for s in shards]
    out = pl.pallas_call(
        body, name=name, in_specs=[HBM] * (2 * nt) + after_specs, out_specs=[SEM] * (2 * nt) + [HBM] * (2 * nt) + [TOKEN],
        out_shape=[pltpu.SemaphoreType.DMA((PEERS,))] * (2 * nt)
        + [pltpu.HBM(s.shape, s.dtype) for s in shards] + [pltpu.HBM(l.shape, l.dtype) for l in lands] + [TOKEN_SHAPE],
        input_output_aliases={t: 2 * nt + t for t in range(2 * nt)}, compiler_params=_split_params())(
            *[_in_hbm(s) for s in shards], *[_in_hbm(l) for l in lands], *after)
    return out[:nt], out[nt:2 * nt], out[2 * nt:3 * nt], out[3 * nt:4 * nt], out[-1]


def _gather_wait(sends, recvs, shards, lands, after, name):
    nt = len(shards)

    def body(*refs):
        ins, lands_ref = refs[:nt], refs[nt:2 * nt]
        send_refs, recv_refs = refs[2 * nt:3 * nt], refs[3 * nt:4 * nt]
        x, y, c, chips = _place()
        for t in range(nt):
            h = ins[t].shape[0] // 2
            for j, (cx, cy) in enumerate(chips):
                for cs in range(2):
                    blk = lands_ref[t].at[2 * cx + cy, pl.ds(cs * h, h)]
                    pltpu.make_async_remote_copy(src_ref=blk, dst_ref=blk, send_sem=send_refs[t].at[2 * j + cs], recv_sem=recv_refs[t].at[2 * j + cs],
                                                 device_id=(cx, cy, cs), device_id_type=MESH).wait()
            blk = lands_ref[t].at[2 * x + y]
            pltpu.make_async_remote_copy(src_ref=blk, dst_ref=blk, send_sem=send_refs[t].at[PEERS - 1], recv_sem=recv_refs[t].at[PEERS - 1],
                                         device_id=(x, y, 1 - c), device_id_type=MESH).wait()

    out = pl.pallas_call(
        body, name=name, in_specs=[HBM] * (2 * nt) + [SEM] * (2 * nt) + [ANY], out_specs=[HBM] * (2 * nt),
        out_shape=[pltpu.HBM(s.shape, s.dtype) for s in shards] + [pltpu.HBM(l.shape, l.dtype) for l in lands],
        input_output_aliases={t: t for t in range(2 * nt)}, compiler_params=_split_params())(*shards, *lands, *sends, *recvs, after)
    return out[nt:]


def _scatter_start(g, name):
    _, r, c_ = g.shape
    h = r // 2

    def body(g_ref, land, send, recv, g_thru, land_thru, token):
        x, y, c, chips = _place()
        for j, (cx, cy) in enumerate(chips):
            for dc in range(2):
                pltpu.make_async_remote_copy(src_ref=g_ref.at[2 * cx + cy, pl.ds(dc * h, h)], dst_ref=land.at[2 * j + c], send_sem=send.at[2 * j + dc],
                                             recv_sem=recv.at[2 * j + c], device_id=(cx, cy, dc), device_id_type=MESH).start()
        pltpu.make_async_remote_copy(src_ref=g_ref.at[2 * x + y, pl.ds((1 - c) * h, h)], dst_ref=land.at[PEERS - 1], send_sem=send.at[PEERS - 1],
                                     recv_sem=recv.at[PEERS - 1], device_id=(x, y, 1 - c), device_id_type=MESH).start()
        token[...] = jnp.zeros(TOKEN_SHAPE.shape, F32)

    land = lax.empty((PEERS, h, c_), g.dtype)
    return pl.pallas_call(
        body, name=name, in_specs=[HBM, HBM], out_specs=[SEM, SEM, HBM, HBM, TOKEN],
        out_shape=[pltpu.SemaphoreType.DMA((PEERS,)), pltpu.SemaphoreType.DMA((PEERS,)), pltpu.HBM(g.shape, g.dtype),
                   pltpu.HBM(land.shape, land.dtype), TOKEN_SHAPE],
        input_output_aliases={0: 2, 1: 3}, compiler_params=_split_params())(_in_hbm(g), _in_hbm(land))


def _scatter_wait(started, after, name):
    nt = len(started)

    def body(*refs):
        lands = refs[nt:2 * nt]
        sends, recvs = refs[2 * nt:3 * nt], refs[3 * nt:4 * nt]
        x, y, c, chips = _place()
        peers = [(cx, cy, dc) for cx, cy in chips for dc in range(2)] + [(x, y, 1 - c)]
        for t in range(nt):
            for k, peer in enumerate(peers):
                blk = lands[t].at[k]
                pltpu.make_async_remote_copy(src_ref=blk, dst_ref=blk, send_sem=sends[t].at[k], recv_sem=recvs[t].at[k],
                                             device_id=peer, device_id_type=MESH).wait()

    gs, lands = [s[2] for s in started], [s[3] for s in started]
    out = pl.pallas_call(
        body, name=name, in_specs=[HBM] * (2 * nt) + [SEM] * (2 * nt) + [ANY], out_specs=[HBM] * (2 * nt),
        out_shape=[pltpu.HBM(a.shape, a.dtype) for a in gs + lands],
        input_output_aliases={t: t for t in range(2 * nt)}, compiler_params=_split_params())(
            *gs, *lands, *[s[0] for s in started], *[s[1] for s in started], after)
    return out[:nt], out[nt:]


def _sum_devices(g, land, me, core, name):
    npeer, h, c = land.shape
    tr = _half_tile(h)
    steps = h // tr

    def body(ix_ref, own_ref, land_ref, o_ref):
        acc = own_ref[0].astype(F32)
        for j in range(npeer):
            acc = acc + land_ref[j].astype(F32)
        o_ref[0] = acc

    grid_spec = pltpu.PrefetchScalarGridSpec(
        num_scalar_prefetch=1, grid=(steps,),
        in_specs=[pl.BlockSpec((1, tr, c), lambda i, ix: (ix[0], ix[1] * steps + i, 0)), pl.BlockSpec((npeer, tr, c), lambda i, ix: (0, i, 0))],
        out_specs=pl.BlockSpec((1, tr, c), lambda i, ix: (ix[1], i, 0)))
    return pl.pallas_call(body, name=name, grid_spec=grid_spec, out_shape=SDS((2, h, c), F32),
                          compiler_params=_cp("parallel"))(jnp.stack([me, core]), g, land)


def _pack_small(parts):
    flat = jnp.concatenate([p.reshape(-1) for p in parts])
    total = flat.shape[0]
    rows = -(-total // 1024) * 8
    return jnp.pad(flat, (0, rows * 128 - total)).reshape(rows, 128)


def _unpack_small(packed, shapes):
    flat = packed.reshape(-1)
    out, off = [], 0
    for s in shapes:
        size = int(np.prod(s))
        out.append(flat[off:off + size].reshape(s))
        off += size
    return out


def _local_step(x, mem, target, w_in, first_after, mid_weights, ffn_weights, on_grad, gains, conv_w, conv_b, hg_lb):
    n = x.shape[0]
    cos, sin = _rope_tables(n)
    seg = _hg_segments()
    gp, gn = _hg_pair_sums()
    gq2 = jnp.tile(gains["q_norm_g"], (1, 2))
    gk2 = jnp.tile(gains["k_norm_g"], (1, 2))
    a0 = hg_lb[:, 0:1, :]
    a1 = hg_lb[:, 1:2, :]

    p, h1 = _norm_mm(x, gains["pre_mix_g"], w_in, F32, 512, 1664, "in_proj", after=(first_after,))
    qr, kr = _qk_prep(p, gq2, gk2, cos, sin, "qk_prep")
    heads = lambda a: a.reshape(n, ATT_KV_HEADS, ATT_HEAD_DIM).transpose(1, 0, 2)
    kh = heads(kr)
    vh = heads(p[:, OFF_AV:OFF_AV + ATT_KV_DIM].astype(MXU_DTYPE))
    att = _attn_fwd(qr, kh, vh, "attn_fwd")
    o2 = _hgrn_fwd(p, a0, a1, seg, "hgrn_fwd")
    rec = _hg_post(o2, p, gains["hg_out_norm_g"], "hg_post")
    cat = jnp.concatenate([att, rec], axis=1)
    w_out, w_xq, w_xkv, w_xo = mid_weights(cat)
    mixed = _mm(cat, w_out, "nn", F32, 512, 1024, "out_proj")
    x1 = _resid_norm(x, mixed, gains["post_mix_g"], "mix_resid")
    xq, h2 = _norm_mm(x1, gains["pre_x_g"], w_xq, MXU_DTYPE, 512, 1024, "xq_proj")
    kv, mn = _norm_mm(mem, gains["mem_norm_g"], w_xkv, MXU_DTYPE, 256, 2048, "xkv_proj")
    ox = _xattn_fwd(xq, kv, "xattn_fwd")
    xo = _mm(ox, w_xo, "nn", F32, 512, 1024, "xo_proj")
    x2 = _resid_norm(x1, xo, gains["post_x_g"], "x_resid")
    w_up, w_down = ffn_weights(x2)
    u, h3 = _norm_mm(x2, gains["pre_ffn_g"], w_up, F32, 512, 512, "up_proj")
    act = _conv_gate(u, conv_w, conv_b, "conv_gate")
    dn = _mm(act, w_down, "nn", F32, 512, 1024, "down_proj")
    d3, loss = _final_loss(x2, dn, gains["post_ffn_g"], target, "ffn_resid_loss")

    gs = {}
    d_dn, gs["post_ffn_g"] = _norm_bwd(dn, gains["post_ffn_g"], d3, None, MXU_DTYPE, "ffn_post_bwd")
    tok = on_grad("w_down", _mm(act, d_dn, "tn", WIRE_DTYPE, 1408, 1024, "down_dw"))
    d_act = _mm(d_dn, w_down, "nt", F32, 512, 1408, "down_dx", after=(tok,))
    du_g, du_v, dcw_g, dcw_v, dcb_g, dcb_v = _conv_gate_bwd(u, conv_w, conv_b, d_act, "conv_gate_bwd")
    gs["conv_w"] = jnp.concatenate([dcw_g, dcw_v], axis=1)
    gs["conv_b"] = jnp.concatenate([dcb_g, dcb_v], axis=1)
    tok = on_grad("w_up", (_mm(h3, du_g, "tn", WIRE_DTYPE, 512, 1408, "up_dw_gate"), _mm(h3, du_v, "tn", WIRE_DTYPE, 512, 1408, "up_dw_value")))
    d_h3 = _mm_nt_halves(du_g, du_v, w_up, F32, 512, 512, "up_dx", after=(tok,))
    d2, gs["pre_ffn_g"] = _norm_bwd(x2, gains["pre_ffn_g"], d_h3, d3, F32, "ffn_pre_bwd")
    d_xo, gs["post_x_g"] = _norm_bwd(xo, gains["post_x_g"], d2, None, MXU_DTYPE, "x_post_bwd")
    tok = on_grad("w_xo", _mm(ox, d_xo, "tn", WIRE_DTYPE, 512, 1024, "xo_dw"))
    d_ox = _mm(d_xo, w_xo, "nt", MXU_DTYPE, 512, 1024, "xo_dx", after=(tok,))
    d_xq, d_k, d_v = _xattn_bwd(xq, kv, d_ox, "xattn_bwd")
    d_kv = jnp.concatenate([d_k, d_v], axis=1).astype(MXU_DTYPE)
    tok = on_grad("w_xq", _mm(h2, d_xq, "tn", WIRE_DTYPE, 512, 1024, "xq_dw"))
    d_h2 = _mm(d_xq, w_xq, "nt", F32, 512, 1024, "xq_dx", after=(tok,))
    tok = on_grad("w_xkv", _mm(mn, d_kv, "tn", WIRE_DTYPE, 512, 1024, "xkv_dw"))
    d_mn = _mm(d_kv, w_xkv, "nt", F32, 256, 1024, "xkv_dx", after=(tok,))
    _, gs["mem_norm_g"] = _norm_bwd(mem, gains["mem_norm_g"], d_mn, None, MXU_DTYPE, "mem_norm_bwd")
    d1, gs["pre_x_g"] = _norm_bwd(x1, gains["pre_x_g"], d_h2, d2, F32, "x_pre_bwd")
    d_mixed, gs["post_mix_g"] = _norm_bwd(mixed, gains["post_mix_g"], d1, None, MXU_DTYPE, "mix_post_bwd")
    tok = on_grad("w_out", _mm(cat, d_mixed, "tn", WIRE_DTYPE, 512, 1024, "out_dw"))
    d_cat = _mm(d_mixed, w_out, "nt", MXU_DTYPE, 512, 1024, "out_dx", after=(tok,))
    d_o, d_hg, dg_hg = _hg_post_bwd(o2, p, gains["hg_out_norm_g"], d_cat, "hg_post_bwd")
    gs["hg_out_norm_g"] = dg_hg.reshape(HG_HEADS, HG_HEAD_DIM).sum(axis=0, keepdims=True)
    dhq2, dlf_q, s0 = _hgrn_bwd_q(p, a0, a1, seg, gp, d_o, "hgrn_bwd_q")
    dz2, dhv2, dlb = _hgrn_bwd_kv(p, a0, a1, seg, gn, d_o, dlf_q, s0, "hgrn_bwd_kv")
    lb = jax.nn.sigmoid(a0 - a1)
    da0 = dlb * lb * (1.0 - lb)
    gs["hg_lb"] = jnp.concatenate([da0, -da0], axis=1)
    d_qr, d_kh, d_vh = _attn_bwd(qr, kh, vh, d_cat, "attn_bwd")
    unheads = lambda a: a.transpose(2, 0, 1).reshape(n, ATT_KV_DIM)
    d_aq, d_ak, dgq, dgk = _qk_prep_bwd(p, gq2, gk2, cos, sin, d_qr, unheads(d_kh), "qk_prep_bwd")
    gs["q_norm_g"] = dgq.reshape(ATT_HEADS, ATT_HEAD_DIM).sum(axis=0, keepdims=True)
    gs["k_norm_g"] = dgk.reshape(ATT_KV_HEADS, ATT_HEAD_DIM).sum(axis=0, keepdims=True)
    d_p = jnp.concatenate([d_aq, d_ak, unheads(d_vh).astype(MXU_DTYPE), (dhq2[0] + dhq2[1]).astype(MXU_DTYPE),
                           dz2[0].astype(MXU_DTYPE), dz2[1].astype(MXU_DTYPE), (dhv2[0] + dhv2[1]).astype(MXU_DTYPE), d_hg], axis=1)
    tok = on_grad("w_in", _mm(h1, d_p, "tn", WIRE_DTYPE, 512, 1664, "in_dw"))
    d_h1 = _mm(d_p, w_in, "nt", F32, 512, 1024, "in_dx", after=(tok,))
    grad_x, gs["pre_mix_g"] = _norm_bwd(x, gains["pre_mix_g"], d_h1, d1, F32, "mix_pre_bwd")
    return loss, grad_x, gs


MATS = ("w_in", "w_out", "w_xq", "w_xkv", "w_xo", "w_up", "w_down")
COL_SHARDED = ("w_in", "w_xkv", "w_up")
GAINS = ("pre_mix_g", "q_norm_g", "k_norm_g", "hg_out_norm_g", "post_mix_g", "pre_x_g", "mem_norm_g", "post_x_g", "pre_ffn_g", "post_ffn_g")
WEIGHTS = ('pre_mix_g', 'w_in', 'q_norm_g', 'k_norm_g', 'hg_lb', 'hg_out_norm_g', 'w_out', 'post_mix_g', 'pre_x_g', 'mem_norm_g', 'w_xq',
           'w_xkv', 'w_xo', 'post_x_g', 'pre_ffn_g', 'w_up', 'conv_w', 'conv_b', 'w_down', 'post_ffn_g')


def kernel(x, mem, pre_mix_g, w_in, q_norm_g, k_norm_g, hg_lb, hg_out_norm_g, w_out, post_mix_g, pre_x_g, mem_norm_g, w_xq, w_xkv, w_xo, post_x_g, pre_ffn_g, w_up, conv_w, conv_b, w_down, post_ffn_g, loss_target, m_pre_mix_g, m_w_in, m_q_norm_g, m_k_norm_g, m_hg_lb, m_hg_out_norm_g, m_w_out, m_post_mix_g, m_pre_x_g, m_mem_norm_g, m_w_xq, m_w_xkv, m_w_xo, m_post_x_g, m_pre_ffn_g, m_w_up, m_conv_w, m_conv_b, m_w_down, m_post_ffn_g, v_pre_mix_g, v_w_in, v_q_norm_g, v_k_norm_g, v_hg_lb, v_hg_out_norm_g, v_w_out, v_post_mix_g, v_pre_x_g, v_mem_norm_g, v_w_xq, v_w_xkv, v_w_xo, v_post_x_g, v_pre_ffn_g, v_w_up, v_conv_w, v_conv_b, v_w_down, v_post_ffn_g):
    args = dict(locals())
    w = {k: args[k] for k in WEIGHTS}
    m = {k: args["m_" + k] for k in WEIGHTS}
    v = {k: args["v_" + k] for k in WEIGHTS}
    chip = 2 * lax.axis_index("x") + lax.axis_index("y")
    core = lax.axis_index("c")

    shards = {k: w[k][0].astype(WIRE_DTYPE) for k in MATS}

    def whole(k, g):
        return jnp.concatenate([g[s] for s in range(4)], axis=1) if k in COL_SHARDED else g.reshape(-1, g.shape[-1])

    w_in_full = whole("w_in", _gather_shards([shards["w_in"]], "gather_w_in")[0])
    mid_names, ffn_names = ("w_out", "w_xq", "w_xkv", "w_xo"), ("w_up", "w_down")
    mid = _gather_start([shards[k] for k in mid_names], "gather_mid_start", after=(w_in_full,))
    ffn = _gather_start([shards[k] for k in ffn_names], "gather_ffn_start", after=(mid[4],))

    def mid_weights(after):
        return [whole(k, g) for k, g in zip(mid_names, _gather_wait(*mid[:4], after, "gather_mid_wait"))]

    def ffn_weights(after):
        return [whole(k, g) for k, g in zip(ffn_names, _gather_wait(*ffn[:4], after, "gather_ffn_wait"))]

    small_in = _exchange_small(_pack_small([w["conv_w"][0], w["hg_lb"]]), False, "gather_small")
    cw_parts, lb_parts = [], []
    for s in range(4):
        cw_s, lb_s = _unpack_small(small_in[2 * s], [w["conv_w"][0].shape, w["hg_lb"].shape])
        cw_parts.append(cw_s)
        lb_parts.append(lb_s)
    conv_w_full = jnp.concatenate(cw_parts, axis=1)
    hg_lb_full = jnp.concatenate(lb_parts, axis=2)

    def cols_by_owner(g, chips):
        return g.reshape(g.shape[0], chips, g.shape[1] // chips).transpose(1, 0, 2)

    started = {}

    def on_grad(k, g):
        if k == "w_up":
            by_owner = jnp.concatenate([cols_by_owner(g[0], 2), cols_by_owner(g[1], 2)], axis=0)
        elif k in COL_SHARDED:
            by_owner = cols_by_owner(g, 4)
        else:
            by_owner = g.reshape(4, g.shape[0] // 4, g.shape[1])
        *started[k], token = _scatter_start(by_owner, "grad_start_" + k)
        return token

    gains = {k: w[k] for k in GAINS}
    loss, grad_x, gs = _local_step(x[0], mem[0], loss_target[0], w_in_full, ffn[4], mid_weights, ffn_weights, on_grad, gains,
                                   conv_w_full, w["conv_b"], hg_lb_full)
    loss = lax.psum(loss[0, 0], ("x", "y", "c"))

    sent, landed = _scatter_wait([started[k] for k in MATS], grad_x, "grad_wait")
    halves = [_sum_devices(g, land, chip, core, "grad_sum_" + k) for k, g, land in zip(MATS, sent, landed)]
    reduced = _join_halves(halves, "grad_join_halves")
    grads = {k: r.reshape(1, -1, r.shape[-1]) for k, r in zip(MATS, reduced)}

    small_names = GAINS + ("conv_b", "conv_w", "hg_lb")
    small_shapes = [gs[k].shape for k in small_names]
    summed = _unpack_small(_exchange_small(_pack_small([gs[k] for k in small_names]), True, "reduce_small"), small_shapes)
    for k, g in zip(small_names, summed):
        grads[k] = g
    ncw = w["conv_w"].shape[2]
    grads["conv_w"] = lax.dynamic_slice_in_dim(grads["conv_w"], chip * ncw, ncw, axis=1)[None]
    nlb = w["hg_lb"].shape[2]
    grads["hg_lb"] = lax.dynamic_slice_in_dim(grads["hg_lb"], chip * nlb, nlb, axis=2)

    delta, new_m, new_v = {}, {}, {}
    for k in WEIGHTS:
        shape = w[k].shape
        two_d = lambda a: a.reshape(-1, shape[-1])
        d, mo, vo = _adamw(two_d(w[k]), two_d(grads[k]), two_d(m[k]), two_d(v[k]), "adamw_" + k)
        delta[k], new_m[k], new_v[k] = d.reshape(shape), mo.reshape(shape), vo.reshape(shape)
        grads[k] = grads[k].reshape(shape)
    return (loss, grad_x[None], *[grads[k] for k in WEIGHTS], *[delta[k] for k in WEIGHTS],
            *[new_m[k] for k in WEIGHTS], *[new_v[k] for k in WEIGHTS])
```

```python
import functools

import numpy as np
import jax
import jax.numpy as jnp
from jax import lax
from jax.experimental import pallas as pl
from jax.experimental.pallas import tpu as pltpu

F32 = jnp.float32
MXU_DTYPE = jnp.bfloat16
WIRE_DTYPE = jnp.bfloat16
VMEM_LIMIT_BYTES = 56 * 1024 * 1024
EPS = 1e-6
MESH = pl.DeviceIdType.MESH

D_MODEL = 1024
GRID_W = 64
ATT_HEADS, ATT_KV_HEADS, ATT_HEAD_DIM = 8, 2, 64
ATT_GROUP = ATT_HEADS // ATT_KV_HEADS
ATT_Q_DIM, ATT_KV_DIM = 512, 128
ROPE_THETA = 10000.0
HG_HEADS, HG_HEAD_DIM, HG_DIM = 4, 128, 512
HG_CHUNK = 128
HG_LEVELS = 7
HG_PAIR = 2 * HG_HEAD_DIM
X_HEADS, X_HEAD_DIM = 4, 256
D_FF = 2816
FF_COLS = 256
FF_BLOCKS = D_FF // FF_COLS
N_IN = 3328
OFF_AQ, OFF_AK, OFF_AV, OFF_HQ, OFF_ZF, OFF_ZB, OFF_HI, OFF_HG = 0, 512, 640, 768, 1280, 1792, 2304, 2816

ADAM_LR, ADAM_B1, ADAM_B2, ADAM_EPS, ADAM_WD, ADAM_STEP = 0.001, 0.9, 0.999, 1e-08, 0.01, 10

SDS = jax.ShapeDtypeStruct


def _cp(*sem):
    return pltpu.CompilerParams(dimension_semantics=sem, vmem_limit_bytes=VMEM_LIMIT_BYTES)


def _dot(a, b, form="nn"):
    dims = {"nn": (((1,), (0,)), ((), ())), "nt": (((1,), (1,)), ((), ())), "tn": (((0,), (0,)), ((), ()))}[form]
    return lax.dot_general(a.astype(MXU_DTYPE), b.astype(MXU_DTYPE), dims, preferred_element_type=F32)


def _sigmoid(x):
    return 1.0 / (1.0 + jnp.exp(-x))


def _rstd(x):
    return lax.rsqrt(jnp.mean(x * x, axis=-1, keepdims=True) + EPS)


def _rms_bwd(x, g, dy):
    r = _rstd(x)
    xh = x * r
    dn = dy * g
    dx = r * (dn - xh * jnp.mean(dn * xh, axis=-1, keepdims=True))
    return dx, jnp.sum(dy * xh, axis=0, keepdims=True)


def _unread(after):
    after = tuple(a for a in after if a is not None)
    return after, [pl.BlockSpec(memory_space=pl.ANY)] * len(after)


def _mm(a, b, form, out_dtype, tm, tn, name, after=()):
    after, after_specs = _unread(after)
    if form == "nn":
        (m, k), n = a.shape, b.shape[1]
    elif form == "nt":
        (m, k), n = a.shape, b.shape[0]
    else:
        (k, m), n = a.shape, b.shape[1]
    tm, tn = min(tm, m), min(tn, n)
    assert m % tm == 0 and n % tn == 0, (name, m, n, tm, tn)

    def body(a_ref, b_ref, *rest):
        o_ref = rest[-1]
        o_ref[...] = _dot(a_ref[...], b_ref[...], form).astype(o_ref.dtype)

    a_spec = pl.BlockSpec((k, tm), lambda i, j: (0, i)) if form == "tn" else pl.BlockSpec((tm, k), lambda i, j: (i, 0))
    b_spec = pl.BlockSpec((tn, k), lambda i, j: (j, 0)) if form == "nt" else pl.BlockSpec((k, tn), lambda i, j: (0, j))
    return pl.pallas_call(
        body, name=name, grid=(m // tm, n // tn), in_specs=[a_spec, b_spec] + after_specs,
        out_specs=pl.BlockSpec((tm, tn), lambda i, j: (i, j)), out_shape=SDS((m, n), out_dtype),
        compiler_params=_cp("parallel", "parallel"))(a, b, *after)


def _mm_nt_halves(a0, a1, b, out_dtype, tm, tn, name, after=()):
    after, after_specs = _unread(after)
    m, kh = a0.shape
    n = b.shape[0]
    tm, tn = min(tm, m), min(tn, n)
    assert m % tm == 0 and n % tn == 0 and b.shape[1] == 2 * kh, (name, m, n, tm, tn)

    def body(a0_ref, a1_ref, b0_ref, b1_ref, *rest):
        o_ref = rest[-1]
        o_ref[...] = (_dot(a0_ref[...], b0_ref[...], "nt") + _dot(a1_ref[...], b1_ref[...], "nt")).astype(o_ref.dtype)

    a_spec = pl.BlockSpec((tm, kh), lambda i, j: (i, 0))
    return pl.pallas_call(
        body, name=name, grid=(m // tm, n // tn),
        in_specs=[a_spec, a_spec, pl.BlockSpec((tn, kh), lambda i, j: (j, 0)), pl.BlockSpec((tn, kh), lambda i, j: (j, 1))] + after_specs,
        out_specs=pl.BlockSpec((tm, tn), lambda i, j: (i, j)), out_shape=SDS((m, n), out_dtype),
        compiler_params=_cp("parallel", "parallel"))(a0, a1, b, b, *after)


def _norm_mm(x, g, w, out_dtype, tm, tn, name, after=()):
    after, after_specs = _unread(after)
    m, d = x.shape
    n = w.shape[1]
    tm, tn = min(tm, m), min(tn, n)
    assert m % tm == 0 and n % tn == 0, (name, m, n, tm, tn)

    def body(x_ref, g_ref, w_ref, *rest):
        o_ref, h_ref, hs = rest[-3:]

        @pl.when(pl.program_id(1) == 0)
        def _():
            xv = x_ref[...]
            h = (xv * _rstd(xv) * g_ref[...]).astype(MXU_DTYPE)
            hs[...] = h
            h_ref[...] = h

        o_ref[...] = _dot(hs[...], w_ref[...]).astype(o_ref.dtype)

    return pl.pallas_call(
        body, name=name, grid=(m // tm, n // tn),
        in_specs=[pl.BlockSpec((tm, d), lambda i, j: (i, 0)), pl.BlockSpec((1, d), lambda i, j: (0, 0)),
                  pl.BlockSpec((d, tn), lambda i, j: (0, j))] + after_specs,
        out_specs=[pl.BlockSpec((tm, tn), lambda i, j: (i, j)), pl.BlockSpec((tm, d), lambda i, j: (i, 0))],
        out_shape=[SDS((m, n), out_dtype), SDS((m, d), MXU_DTYPE)],
        scratch_shapes=[pltpu.VMEM((tm, d), MXU_DTYPE)],
        compiler_params=_cp("parallel", "arbitrary"))(x, g, w, *after)


ROW_TILE = 256


def _resid_norm(x, y, g, name):
    n, d = x.shape
    tr = min(ROW_TILE, n)

    def body(x_ref, y_ref, g_ref, o_ref):
        yv = y_ref[...]
        o_ref[...] = x_ref[...] + yv * _rstd(yv) * g_ref[...]

    row = pl.BlockSpec((tr, d), lambda i: (i, 0))
    return pl.pallas_call(
        body, name=name, grid=(n // tr,), in_specs=[row, row, pl.BlockSpec((1, d), lambda i: (0, 0))],
        out_specs=row, out_shape=SDS((n, d), F32), compiler_params=_cp("parallel"))(x, y, g)


def _norm_bwd(x, g, dy, res, out_dtype, name):
    n, d = x.shape
    tr = min(ROW_TILE, n)
    has_res = res is not None

    def body(*refs):
        x_ref, g_ref, dy_ref = refs[:3]
        dx_ref, dg_ref = refs[-2:]
        dx, dg = _rms_bwd(x_ref[...], g_ref[...], dy_ref[...].astype(F32))
        if has_res:
            dx = dx + refs[3][...]
        dx_ref[...] = dx.astype(dx_ref.dtype)

        @pl.when(pl.program_id(0) == 0)
        def _():
            dg_ref[...] = jnp.zeros_like(dg_ref)

        dg_ref[...] += dg

    row = pl.BlockSpec((tr, d), lambda i: (i, 0))
    vec = pl.BlockSpec((1, d), lambda i: (0, 0))
    ins = [x, g, dy] + ([res] if has_res else [])
    return pl.pallas_call(
        body, name=name, grid=(n // tr,), in_specs=[row, vec, row] + ([row] if has_res else []),
        out_specs=[row, vec], out_shape=[SDS((n, d), out_dtype), SDS((1, d), F32)],
        compiler_params=_cp("arbitrary"))(*ins)


def _final_loss(x, y, g, target, name):
    n, d = x.shape
    tr = min(ROW_TILE, n)

    def body(x_ref, y_ref, g_ref, t_ref, d_ref, l_ref):
        yv = y_ref[...]
        diff = x_ref[...] + yv * _rstd(yv) * g_ref[...] - t_ref[...]
        d_ref[...] = diff * (1.0 / d)

        @pl.when(pl.program_id(0) == 0)
        def _():
            l_ref[...] = jnp.zeros_like(l_ref)

        l_ref[...] += 0.5 * jnp.sum(jnp.mean(diff * diff, axis=-1, keepdims=True), axis=0, keepdims=True)

    row = pl.BlockSpec((tr, d), lambda i: (i, 0))
    return pl.pallas_call(
        body, name=name, grid=(n // tr,), in_specs=[row, row, pl.BlockSpec((1, d), lambda i: (0, 0)), row],
        out_specs=[row, pl.BlockSpec((1, 1), lambda i: (0, 0))], out_shape=[SDS((n, d), F32), SDS((1, 1), F32)],
        compiler_params=_cp("arbitrary"))(x, y, g, target)


def _rope_tables(n):
    pairs = ATT_HEAD_DIM // 4
    t = np.arange(n)
    inv = np.power(ROPE_THETA, -np.arange(pairs, dtype=np.float32) / pairs).astype(np.float32)
    ang = np.concatenate([(t // GRID_W)[:, None].astype(np.float32) * inv, (t % GRID_W)[:, None].astype(np.float32) * inv], axis=-1)
    cos = np.repeat(np.cos(ang), 2, axis=-1)
    sin = np.repeat(np.sin(ang), 2, axis=-1) * np.tile(np.array([-1.0, 1.0], np.float32), ATT_HEAD_DIM // 2)
    return jnp.asarray(np.tile(cos, 2), F32), jnp.asarray(np.tile(sin, 2), F32)


def _swap_pairs(x):
    lane = lax.broadcasted_iota(jnp.int32, x.shape, 1)
    return jnp.where((lane & 1) == 0, pltpu.roll(x, 127, axis=1), pltpu.roll(x, 1, axis=1))


def _head_mean(v):
    lane = lax.broadcasted_iota(jnp.int32, v.shape, 1)
    lo = jnp.where(lane < ATT_HEAD_DIM, v, 0.0)
    s0 = jnp.sum(lo, axis=-1, keepdims=True)
    s1 = jnp.sum(v - lo, axis=-1, keepdims=True)
    return jnp.where(lane < ATT_HEAD_DIM, s0, s1) * (1.0 / ATT_HEAD_DIM)


def _qk_prep(p, gq, gk, cos, sin, name):
    n = p.shape[0]
    tr = min(ROW_TILE, n)

    def one(xv, g, c, s):
        xn = xv * lax.rsqrt(_head_mean(xv * xv) + EPS) * g
        return xn * c + _swap_pairs(xn) * s

    def body(q_ref, k_ref, gq_ref, gk_ref, c_ref, s_ref, qo_ref, ko_ref):
        c, s = c_ref[...], s_ref[...]
        for j in range(ATT_Q_DIM // 128):
            qo_ref[:, j * 128:(j + 1) * 128] = one(q_ref[:, j * 128:(j + 1) * 128], gq_ref[...], c, s).astype(qo_ref.dtype)
        ko_ref[...] = one(k_ref[...], gk_ref[...], c, s).astype(ko_ref.dtype)

    vec = pl.BlockSpec((1, 128), lambda i: (0, 0))
    tab = pl.BlockSpec((tr, 128), lambda i: (i, 0))
    return pl.pallas_call(
        body, name=name, grid=(n // tr,),
        in_specs=[pl.BlockSpec((tr, ATT_Q_DIM), lambda i: (i, 0)), pl.BlockSpec((tr, 128), lambda i: (i, OFF_AK // 128)), vec, vec, tab, tab],
        out_specs=[pl.BlockSpec((tr, ATT_Q_DIM), lambda i: (i, 0)), tab],
        out_shape=[SDS((n, ATT_Q_DIM), MXU_DTYPE), SDS((n, ATT_KV_DIM), MXU_DTYPE)],
        compiler_params=_cp("parallel"))(p, p, gq, gk, cos, sin)


def _qk_prep_bwd(p, gq, gk, cos, sin, dq, dk, name):
    n = p.shape[0]
    tr = min(ROW_TILE, n)

    def one(xv, g, c, s, dout):
        dxn = dout * c + _swap_pairs(dout * s)
        r = lax.rsqrt(_head_mean(xv * xv) + EPS)
        xh = xv * r
        dn = dxn * g
        dx = r * (dn - xh * _head_mean(dn * xh))
        return dx, jnp.sum(dxn * xh, axis=0, keepdims=True)

    def body(q_ref, k_ref, gq_ref, gk_ref, c_ref, s_ref, dq_ref, dk_ref, dqo_ref, dko_ref, dgq_ref, dgk_ref):
        @pl.when(pl.program_id(0) == 0)
        def _():
            dgq_ref[...] = jnp.zeros_like(dgq_ref)
            dgk_ref[...] = jnp.zeros_like(dgk_ref)

        c, s = c_ref[...], s_ref[...]
        for j in range(ATT_Q_DIM // 128):
            sl = slice(j * 128, (j + 1) * 128)
            dx, dg = one(q_ref[:, sl], gq_ref[...], c, s, dq_ref[:, sl])
            dqo_ref[:, sl] = dx.astype(dqo_ref.dtype)
            dgq_ref[:, sl] += dg
        dx, dg = one(k_ref[...], gk_ref[...], c, s, dk_ref[...])
        dko_ref[...] = dx.astype(dko_ref.dtype)
        dgk_ref[...] += dg

    vec = pl.BlockSpec((1, 128), lambda i: (0, 0))
    tab = pl.BlockSpec((tr, 128), lambda i: (i, 0))
    qrow = pl.BlockSpec((tr, ATT_Q_DIM), lambda i: (i, 0))
    return pl.pallas_call(
        body, name=name, grid=(n // tr,),
        in_specs=[qrow, pl.BlockSpec((tr, 128), lambda i: (i, OFF_AK // 128)), vec, vec, tab, tab, qrow, tab],
        out_specs=[qrow, tab, pl.BlockSpec((1, ATT_Q_DIM), lambda i: (0, 0)), vec],
        out_shape=[SDS((n, ATT_Q_DIM), MXU_DTYPE), SDS((n, ATT_KV_DIM), MXU_DTYPE), SDS((1, ATT_Q_DIM), F32), SDS((1, 128), F32)],
        compiler_params=_cp("arbitrary"))(p, p, gq, gk, cos, sin, dq, dk)


ATT_TQ = 256


def _attn_fwd(q, k, v, name):
    n = q.shape[0]
    tq = min(ATT_TQ, n)
    scale = ATT_HEAD_DIM ** -0.5
    gw = ATT_GROUP * ATT_HEAD_DIM

    def body(q_ref, k_ref, v_ref, o_ref):
        kk, vv = k_ref[0], v_ref[0]
        outs = []
        for g in range(ATT_GROUP):
            s = _dot(q_ref[:, g * ATT_HEAD_DIM:(g + 1) * ATT_HEAD_DIM] * scale, kk, "nt")
            e = jnp.exp(s - jnp.max(s, axis=-1, keepdims=True))
            outs.append(_dot(e, vv) / jnp.sum(e, axis=-1, keepdims=True))
        o_ref[...] = jnp.concatenate(outs, axis=-1).astype(o_ref.dtype)

    kv = pl.BlockSpec((1, n, ATT_HEAD_DIM), lambda h, i: (h, 0, 0))
    return pl.pallas_call(
        body, name=name, grid=(ATT_KV_HEADS, n // tq),
        in_specs=[pl.BlockSpec((tq, gw), lambda h, i: (i, h)), kv, kv],
        out_specs=pl.BlockSpec((tq, gw), lambda h, i: (i, h)), out_shape=SDS((n, ATT_Q_DIM), MXU_DTYPE),
        compiler_params=_cp("parallel", "parallel"))(q, k, v)


def _attn_bwd(q, k, v, do, name):
    n = q.shape[0]
    tq = min(ATT_TQ, n)
    scale = ATT_HEAD_DIM ** -0.5
    gw = ATT_GROUP * ATT_HEAD_DIM

    def body(q_ref, k_ref, v_ref, do_ref, dq_ref, dk_ref, dv_ref):
        @pl.when(pl.program_id(1) == 0)
        def _():
            dk_ref[...] = jnp.zeros_like(dk_ref)
            dv_ref[...] = jnp.zeros_like(dv_ref)

        kk, vv = k_ref[0], v_ref[0]
        dqs = []
        dk_acc = jnp.zeros((ATT_HEAD_DIM, n), F32)
        dv_acc = jnp.zeros((ATT_HEAD_DIM, n), F32)
        for g in range(ATT_GROUP):
            sl = slice(g * ATT_HEAD_DIM, (g + 1) * ATT_HEAD_DIM)
            qg, dog = q_ref[:, sl] * scale, do_ref[:, sl].astype(F32)
            s = _dot(qg, kk, "nt")
            e = jnp.exp(s - jnp.max(s, axis=-1, keepdims=True))
            inv = 1.0 / jnp.sum(e, axis=-1, keepdims=True)
            delta = jnp.sum(dog * (_dot(e, vv) * inv), axis=-1, keepdims=True)
            dse = e * (_dot(dog, vv, "nt") - delta)
            dqs.append(_dot(dse, kk) * (inv * scale))
            dk_acc += _dot(qg.astype(F32) * inv, dse, "tn")
            dv_acc += _dot(dog * inv, e, "tn")
        dq_ref[...] = jnp.concatenate(dqs, axis=-1)
        dk_ref[0] += dk_acc
        dv_ref[0] += dv_acc

    kv = pl.BlockSpec((1, n, ATT_HEAD_DIM), lambda h, i: (h, 0, 0))
    kvt = pl.BlockSpec((1, ATT_HEAD_DIM, n), lambda h, i: (h, 0, 0))
    qb = pl.BlockSpec((tq, gw), lambda h, i: (i, h))
    return pl.pallas_call(
        body, name=name, grid=(ATT_KV_HEADS, n // tq), in_specs=[qb, kv, kv, qb], out_specs=[qb, kvt, kvt],
        out_shape=[SDS((n, ATT_Q_DIM), F32), SDS((ATT_KV_HEADS, ATT_HEAD_DIM, n), F32), SDS((ATT_KV_HEADS, ATT_HEAD_DIM, n), F32)],
        compiler_params=_cp("parallel", "arbitrary"))(q, k, v, do)


def _both_directions(mats, axis):
    fwd = np.concatenate(mats, axis=axis).astype(np.float32)
    bwd = np.concatenate([m[::-1, ::-1] for m in mats], axis=axis).astype(np.float32)
    return jnp.asarray(np.stack([fwd, bwd]), MXU_DTYPE)


def _hg_segments():
    c = HG_CHUNK
    t = np.arange(c)[:, None]
    r = np.arange(c)[None, :]
    mats = [(r <= t)]
    for lev in range(HG_LEVELS):
        h = c >> (lev + 1)
        mid = (t // (2 * h)) * (2 * h) + h - 1
        hi = (t // h) % 2 == 1
        mats.append(np.where(hi, (r > mid) & (r <= t), (r > t) & (r <= mid)))
    mats.append(r > t)
    return _both_directions(mats, 0)


def _hg_pair_sums():
    c = HG_CHUNK
    r = np.arange(c)[:, None]
    t = np.arange(c)[None, :]
    gp, gn = [t >= r], [t < r]
    for lev in range(HG_LEVELS):
        sh = HG_LEVELS - 1 - lev
        same = (r >> sh) == (t >> sh)
        gp.append(same & (t >= r))
        gn.append(same & (t < r))
    return _both_directions(gp, 1), _both_directions(gn, 1)


def _split_dot(mat, x):
    hi = x.astype(MXU_DTYPE)
    lo = (x - hi.astype(F32)).astype(MXU_DTYPE)
    return _dot(mat, hi) + _dot(mat, lo)


def _hg_gates(hq, z, a0, a1):
    q = hq * _sigmoid(hq)
    sg = _sigmoid(z)
    lb = _sigmoid(a0 - a1)
    f = lb + (1.0 - lb) * sg
    k = (1.0 - lb) * (1.0 - sg)
    return q, f, k, sg, lb


def _hg_level_masks(mirrored):
    c = HG_CHUNK
    row = lax.broadcasted_iota(jnp.int32, (c, 1), 0)
    rr = lax.broadcasted_iota(jnp.int32, (c, c), 0)
    cc = lax.broadcasted_iota(jnp.int32, (c, c), 1)
    his, sames = [], []
    for lev in range(HG_LEVELS):
        sh = HG_LEVELS - 1 - lev
        his.append(jnp.logical_xor(((row >> sh) & 1) == 1, mirrored))
        sames.append((rr >> (sh + 1)) == (cc >> (sh + 1)))
    return his, sames, rr == cc


def _hg_intra(q, k, ex, masks):
    his, sames, eye = masks
    a = jnp.where(eye, jnp.sum(q * k, axis=-1, keepdims=True), 0.0)
    for lev in range(HG_LEVELS):
        e = ex[lev + 1]
        qs = jnp.where(his[lev], q * e, 0.0)
        ks = jnp.where(his[lev], 0.0, k * e)
        a = a + jnp.where(sames[lev], _dot(qs, ks, "nt"), 0.0)
    return a


def _hg_specs(n, with_time):
    c = HG_CHUNK
    nc = n // c

    def chunk(d, i):
        first = d if with_time else 1 - d
        return i + first * (nc - 1 - 2 * i)

    def pcols(off, dir_stride=0):
        return [pl.BlockSpec((c, HG_PAIR), lambda d, i, j=j: (chunk(d, i), off // HG_PAIR + dir_stride // HG_PAIR * d + j)) for j in range(2)]

    specs = dict(
        hq=pcols(OFF_HQ), v=pcols(OFF_HI), z=pcols(OFF_ZF, OFF_ZB - OFF_ZF),
        shared=pl.BlockSpec((c, HG_DIM), lambda d, i: (chunk(d, i), 0)),
        per_dir=pl.BlockSpec((1, c, HG_DIM), lambda d, i: (d, chunk(d, i), 0)),
        vec=pl.BlockSpec((1, 1, HG_DIM), lambda d, i: (d, 0, 0)),
        seg=pl.BlockSpec((1, (HG_LEVELS + 2) * c, c), lambda d, i: (d, 0, 0)),
        sums=pl.BlockSpec((1, c, (HG_LEVELS + 1) * c), lambda d, i: (d, 0, 0)),
        state=pl.BlockSpec((1, HG_HEADS, 1, HG_HEAD_DIM, HG_HEAD_DIM), lambda d, i: (d, 0, chunk(d, i), 0, 0)))
    return nc, specs


def _hg_head(refs, hh):
    off = (hh % 2) * HG_HEAD_DIM
    return refs[hh // 2][:, off:off + HG_HEAD_DIM]


def _hg_lanes(hh):
    return slice(hh * HG_HEAD_DIM, (hh + 1) * HG_HEAD_DIM)


def _hg_exps(seg_ref, f):
    lf = jnp.log(f)
    args = _split_dot(seg_ref[0], lf)
    c = HG_CHUNK
    return [jnp.exp(args[j * c:(j + 1) * c]) for j in range(HG_LEVELS + 2)]


def _hg_last_row(a, mirrored):
    return jnp.where(mirrored, a[0:1, :], a[HG_CHUNK - 1:HG_CHUNK, :])


def _hgrn_fwd(p, a0, a1, seg, name):
    n = p.shape[0]
    nc, sp = _hg_specs(n, True)

    def body(hq0, hq1, z0, z1, v0, v1, a0_ref, a1_ref, seg_ref, o_ref, st):
        @pl.when(pl.program_id(1) == 0)
        def _():
            st[...] = jnp.zeros_like(st)

        mirrored = pl.program_id(0) == 1
        masks = _hg_level_masks(mirrored)
        for hh in range(HG_HEADS):
            ln = _hg_lanes(hh)
            q, f, k, _, _ = _hg_gates(_hg_head((hq0, hq1), hh), _hg_head((z0, z1), hh), a0_ref[0, :, ln], a1_ref[0, :, ln])
            vv = _hg_head((v0, v1), hh)
            ex = _hg_exps(seg_ref, f)
            a = _hg_intra(q, k, ex, masks)
            s_t = st[hh]
            o_ref[0, :, ln] = _dot(a, vv) + _dot(q * ex[0], s_t, "nt")
            st[hh] = s_t * _hg_last_row(ex[0], mirrored) + _dot(vv, k * ex[HG_LEVELS + 1], "tn")

    return pl.pallas_call(
        body, name=name, grid=(2, nc), in_specs=sp["hq"] + sp["z"] + sp["v"] + [sp["vec"], sp["vec"], sp["seg"]],
        out_specs=sp["per_dir"], out_shape=SDS((2, n, HG_DIM), F32),
        scratch_shapes=[pltpu.VMEM((HG_HEADS, HG_HEAD_DIM, HG_HEAD_DIM), F32)],
        compiler_params=_cp("parallel", "arbitrary"))(p, p, p, p, p, p, a0, a1, seg)


def _hgrn_bwd_q(p, a0, a1, seg, gp, do, name):
    n = p.shape[0]
    nc, sp = _hg_specs(n, True)


    def body(hq0, hq1, z0, z1, v0, v1, a0_ref, a1_ref, seg_ref, gp_ref, do_ref, dhq_ref, dlf_ref, s0_ref, st):
        @pl.when(pl.program_id(1) == 0)
        def _():
            st[...] = jnp.zeros_like(st)

        mirrored = pl.program_id(0) == 1
        his, sames, eye = _hg_level_masks(mirrored)
        for hh in range(HG_HEADS):
            ln = _hg_lanes(hh)
            s_t = st[hh]
            s0_ref[0, hh, 0] = s_t
            hqv = _hg_head((hq0, hq1), hh)
            q, f, k, _, _ = _hg_gates(hqv, _hg_head((z0, z1), hh), a0_ref[0, :, ln], a1_ref[0, :, ln])
            vv, dov = _hg_head((v0, v1), hh), do_ref[:, ln]
            ex = _hg_exps(seg_ref, f)
            da = _dot(dov, vv, "nt")
            dq_inter = ex[0] * _dot(dov, s_t)
            dq = jnp.sum(dov * vv, axis=-1, keepdims=True) * k + dq_inter
            terms = [q * dq_inter]
            for lev in range(HG_LEVELS):
                e = ex[lev + 1]
                ks = jnp.where(his[lev], 0.0, k * e)
                part = jnp.where(his[lev], e, 0.0) * _dot(jnp.where(sames[lev], da, 0.0), ks)
                dq = dq + part
                terms.append(q * part)
            dlf_ref[0, :, ln] = _dot(gp_ref[0], jnp.concatenate(terms, axis=0))
            sq = _sigmoid(hqv)
            dhq_ref[0, :, ln] = dq * sq * (1.0 + hqv * (1.0 - sq))
            st[hh] = s_t * _hg_last_row(ex[0], mirrored) + _dot(vv, k * ex[HG_LEVELS + 1], "tn")

    out = SDS((2, n, HG_DIM), F32)
    return pl.pallas_call(
        body, name=name, grid=(2, nc),
        in_specs=sp["hq"] + sp["z"] + sp["v"] + [sp["vec"], sp["vec"], sp["seg"], sp["sums"], sp["shared"]],
        out_specs=[sp["per_dir"], sp["per_dir"], sp["state"]],
        out_shape=[out, out, SDS((2, HG_HEADS, nc, HG_HEAD_DIM, HG_HEAD_DIM), F32)],
        scratch_shapes=[pltpu.VMEM((HG_HEADS, HG_HEAD_DIM, HG_HEAD_DIM), F32)],
        compiler_params=_cp("parallel", "arbitrary"))(p, p, p, p, p, p, a0, a1, seg, gp, do)


def _hgrn_bwd_kv(p, a0, a1, seg, gn, do, dlf_q, s0, name):
    n = p.shape[0]
    nc, sp = _hg_specs(n, False)

    def body(hq0, hq1, z0, z1, v0, v1, a0_ref, a1_ref, seg_ref, gn_ref, do_ref, dlfq_ref, s0_ref, dz_ref, dv_ref, dlb_ref, rt):
        @pl.when(pl.program_id(1) == 0)
        def _():
            rt[...] = jnp.zeros_like(rt)
            dlb_ref[...] = jnp.zeros_like(dlb_ref)

        mirrored = pl.program_id(0) == 1
        masks = _hg_level_masks(mirrored)
        his, sames, eye = masks
        for hh in range(HG_HEADS):
            ln = _hg_lanes(hh)
            q, f, k, sg, lb = _hg_gates(_hg_head((hq0, hq1), hh), _hg_head((z0, z1), hh), a0_ref[0, :, ln], a1_ref[0, :, ln])
            vv, dov = _hg_head((v0, v1), hh), do_ref[:, ln]
            ex = _hg_exps(seg_ref, f)
            a = _hg_intra(q, k, ex, masks)
            da = _dot(dov, vv, "nt")
            r_t = rt[hh]
            k_end = k * ex[HG_LEVELS + 1]
            dv_ref[0, :, ln] = _dot(a, dov, "tn") + _dot(k_end, r_t, "nt")
            dk_inter = ex[HG_LEVELS + 1] * _dot(vv, r_t)
            dk = jnp.sum(dov * vv, axis=-1, keepdims=True) * q + dk_inter
            terms = [k * dk_inter]
            for lev in range(HG_LEVELS):
                e = ex[lev + 1]
                qs = jnp.where(his[lev], q * e, 0.0)
                part = jnp.where(his[lev], 0.0, e) * _dot(jnp.where(sames[lev], da, 0.0), qs, "tn")
                dk = dk + part
                terms.append(k * part)
            decay = _hg_last_row(ex[0], mirrored)
            rt[hh] = r_t * decay + _dot(dov, q * ex[0], "tn")
            later = decay * jnp.sum(s0_ref[0, hh, 0] * r_t, axis=0, keepdims=True)
            dlf = dlfq_ref[0, :, ln] + _dot(gn_ref[0], jnp.concatenate(terms, axis=0)) + later
            df = dlf / f - dk
            dz_ref[0, :, ln] = df * (1.0 - lb) * sg * (1.0 - sg)
            dlb_ref[0, :, ln] += jnp.sum(df * (1.0 - sg), axis=0, keepdims=True)

    out = SDS((2, n, HG_DIM), F32)
    return pl.pallas_call(
        body, name=name, grid=(2, nc),
        in_specs=sp["hq"] + sp["z"] + sp["v"] + [sp["vec"], sp["vec"], sp["seg"], sp["sums"], sp["shared"], sp["per_dir"], sp["state"]],
        out_specs=[sp["per_dir"], sp["per_dir"], sp["vec"]], out_shape=[out, out, SDS((2, 1, HG_DIM), F32)],
        scratch_shapes=[pltpu.VMEM((HG_HEADS, HG_HEAD_DIM, HG_HEAD_DIM), F32)],
        compiler_params=_cp("parallel", "arbitrary"))(p, p, p, p, p, p, a0, a1, seg, gn, do, dlf_q, s0)


def _hg_post(o2, p, g, name):
    n = p.shape[0]
    tr = min(ROW_TILE, n)
    w = 2 * HG_HEAD_DIM

    def body(of_ref, ob_ref, hg_ref, g_ref, o_ref):
        for j in range(2):
            sl = slice(j * HG_HEAD_DIM, (j + 1) * HG_HEAD_DIM)
            o = of_ref[0, :, sl] + ob_ref[0, :, sl]
            hg = hg_ref[:, sl]
            o_ref[:, sl] = (o * _rstd(o) * g_ref[...] * (hg * _sigmoid(hg))).astype(o_ref.dtype)

    blk = pl.BlockSpec((tr, w), lambda i, j: (i, j))
    dirs = [pl.BlockSpec((1, tr, w), lambda i, j, d=d: (d, i, j)) for d in range(2)]
    return pl.pallas_call(
        body, name=name, grid=(n // tr, HG_DIM // w),
        in_specs=dirs + [pl.BlockSpec((tr, w), lambda i, j: (i, OFF_HG // w + j)), pl.BlockSpec((1, HG_HEAD_DIM), lambda i, j: (0, 0))],
        out_specs=blk, out_shape=SDS((n, HG_DIM), MXU_DTYPE), compiler_params=_cp("parallel", "parallel"))(o2, o2, p, g)


def _hg_post_bwd(o2, p, g, dcat, name):
    n = p.shape[0]
    tr = min(ROW_TILE, n)
    w = 2 * HG_HEAD_DIM

    def body(of_ref, ob_ref, hg_ref, g_ref, d_ref, do_ref, dhg_ref, dg_ref):
        @pl.when(pl.program_id(1) == 0)
        def _():
            dg_ref[...] = jnp.zeros_like(dg_ref)

        for j in range(2):
            sl = slice(j * HG_HEAD_DIM, (j + 1) * HG_HEAD_DIM)
            o = of_ref[0, :, sl] + ob_ref[0, :, sl]
            hg = hg_ref[:, sl]
            d = d_ref[:, sl].astype(F32)
            sg = _sigmoid(hg)
            on = o * _rstd(o) * g_ref[...]
            dhg_ref[:, sl] = (d * on * sg * (1.0 + hg * (1.0 - sg))).astype(dhg_ref.dtype)
            dx, dg = _rms_bwd(o, g_ref[...], d * hg * sg)
            do_ref[:, sl] = dx
            dg_ref[0, :, sl] += dg

    blk = pl.BlockSpec((tr, w), lambda j, i: (i, j))
    dirs = [pl.BlockSpec((1, tr, w), lambda j, i, d=d: (d, i, j)) for d in range(2)]
    return pl.pallas_call(
        body, name=name, grid=(HG_DIM // w, n // tr),
        in_specs=dirs + [pl.BlockSpec((tr, w), lambda j, i: (i, OFF_HG // w + j)), pl.BlockSpec((1, HG_HEAD_DIM), lambda j, i: (0, 0)),
                         pl.BlockSpec((tr, w), lambda j, i: (i, ATT_Q_DIM // w + j))],
        out_specs=[blk, blk, pl.BlockSpec((1, 1, w), lambda j, i: (j, 0, 0))],
        out_shape=[SDS((n, HG_DIM), F32), SDS((n, HG_DIM), MXU_DTYPE), SDS((HG_DIM // w, 1, w), F32)],
        compiler_params=_cp("parallel", "arbitrary"))(o2, o2, p, g, dcat)


XATT_TQ = 512


def _xattn_fwd(q, kv, name):
    n, nm = q.shape[0], kv.shape[0]
    tq = min(XATT_TQ, n)
    scale = X_HEAD_DIM ** -0.5

    def body(q_ref, k_ref, v_ref, o_ref):
        s = _dot(q_ref[...], k_ref[...], "nt") * scale
        e = jnp.exp(s - jnp.max(s, axis=-1, keepdims=True))
        o_ref[...] = _dot(e / jnp.sum(e, axis=-1, keepdims=True), v_ref[...]).astype(o_ref.dtype)

    qb = pl.BlockSpec((tq, X_HEAD_DIM), lambda h, i: (i, h))
    return pl.pallas_call(
        body, name=name, grid=(X_HEADS, n // tq),
        in_specs=[qb, pl.BlockSpec((nm, X_HEAD_DIM), lambda h, i: (0, h)), pl.BlockSpec((nm, X_HEAD_DIM), lambda h, i: (0, X_HEADS + h))],
        out_specs=qb, out_shape=SDS(q.shape, MXU_DTYPE), compiler_params=_cp("parallel", "parallel"))(q, kv, kv)


def _xattn_bwd(q, kv, do, name):
    n, nm = q.shape[0], kv.shape[0]
    tq = min(XATT_TQ, n)
    scale = X_HEAD_DIM ** -0.5

    def body(q_ref, k_ref, v_ref, do_ref, dq_ref, dk_ref, dv_ref):
        @pl.when(pl.program_id(1) == 0)
        def _():
            dk_ref[...] = jnp.zeros_like(dk_ref)
            dv_ref[...] = jnp.zeros_like(dv_ref)

        qv, dov = q_ref[...], do_ref[...]
        s = _dot(qv, k_ref[...], "nt") * scale
        e = jnp.exp(s - jnp.max(s, axis=-1, keepdims=True))
        p = e / jnp.sum(e, axis=-1, keepdims=True)
        dp = _dot(dov, v_ref[...], "nt")
        ds = p * (dp - jnp.sum(p * dp, axis=-1, keepdims=True)) * scale
        dq_ref[...] = _dot(ds, k_ref[...]).astype(dq_ref.dtype)
        dk_ref[...] += _dot(ds, qv, "tn")
        dv_ref[...] += _dot(p, dov, "tn")

    qb = pl.BlockSpec((tq, X_HEAD_DIM), lambda h, i: (i, h))
    kb = pl.BlockSpec((nm, X_HEAD_DIM), lambda h, i: (0, h))
    return pl.pallas_call(
        body, name=name, grid=(X_HEADS, n // tq),
        in_specs=[qb, kb, pl.BlockSpec((nm, X_HEAD_DIM), lambda h, i: (0, X_HEADS + h)), qb], out_specs=[qb, kb, kb],
        out_shape=[SDS(q.shape, MXU_DTYPE), SDS((nm, X_HEADS * X_HEAD_DIM), F32), SDS((nm, X_HEADS * X_HEAD_DIM), F32)],
        compiler_params=_cp("parallel", "arbitrary"))(q, kv, kv, do)


def _shift_rows(u, down):
    n = u.shape[0]
    row = lax.broadcasted_iota(jnp.int32, u.shape, 0)
    if down:
        return jnp.where(row == 0, 0.0, pltpu.roll(u, 1, axis=0))
    return jnp.where(row == n - 1, 0.0, pltpu.roll(u, n - 1, axis=0))


def _conv(u, w, b):
    return b + _shift_rows(u, True) * w[0:1, :] + u * w[1:2, :] + _shift_rows(u, False) * w[2:3, :]


def _ff_specs(n):
    gate = lambda rows: pl.BlockSpec((rows, FF_COLS), lambda j: (0, j))
    val = lambda rows: pl.BlockSpec((rows, FF_COLS), lambda j: (0, FF_BLOCKS + j))
    return [gate(n), val(n), gate(3), val(3), gate(1), val(1)], gate


def _conv_gate(u, cw, cb, name):
    n = u.shape[0]
    ins, gate_blk = _ff_specs(n)

    def body(ug_ref, uv_ref, wg_ref, wv_ref, bg_ref, bv_ref, o_ref):
        gate = _conv(ug_ref[...], wg_ref[...], bg_ref[...])
        val = _conv(uv_ref[...], wv_ref[...], bv_ref[...])
        o_ref[...] = (gate * _sigmoid(gate) * val).astype(o_ref.dtype)

    return pl.pallas_call(
        body, name=name, grid=(FF_BLOCKS,), in_specs=ins, out_specs=gate_blk(n), out_shape=SDS((n, D_FF), MXU_DTYPE),
        compiler_params=_cp("parallel"))(u, u, cw, cw, cb, cb)


def _conv_gate_bwd(u, cw, cb, da, name):
    n = u.shape[0]
    ins, gate_blk = _ff_specs(n)

    def side(dacc, u, w, du_ref, dw_ref, db_ref):
        nxt, prv = _shift_rows(dacc, False), _shift_rows(dacc, True)
        du_ref[...] = (nxt * w[0:1, :] + dacc * w[1:2, :] + prv * w[2:3, :]).astype(du_ref.dtype)
        db_ref[...] = jnp.sum(dacc, axis=0, keepdims=True)
        dw_ref[0:1, :] = jnp.sum(nxt * u, axis=0, keepdims=True)
        dw_ref[1:2, :] = jnp.sum(dacc * u, axis=0, keepdims=True)
        dw_ref[2:3, :] = jnp.sum(prv * u, axis=0, keepdims=True)

    def body(ug_ref, uv_ref, wg_ref, wv_ref, bg_ref, bv_ref, da_ref, dug_ref, duv_ref, dwg_ref, dwv_ref, dbg_ref, dbv_ref):
        ug, uv = ug_ref[...], uv_ref[...]
        gate = _conv(ug, wg_ref[...], bg_ref[...])
        val = _conv(uv, wv_ref[...], bv_ref[...])
        sg = _sigmoid(gate)
        dav = da_ref[...].astype(F32)
        side(dav * val * sg * (1.0 + gate * (1.0 - sg)), ug, wg_ref[...], dug_ref, dwg_ref, dbg_ref)
        side(dav * gate * sg, uv, wv_ref[...], duv_ref, dwv_ref, dbv_ref)

    return pl.pallas_call(
        body, name=name, grid=(FF_BLOCKS,), in_specs=ins + [gate_blk(n)],
        out_specs=[gate_blk(n), gate_blk(n), gate_blk(3), gate_blk(3), gate_blk(1), gate_blk(1)],
        out_shape=[SDS((n, D_FF), MXU_DTYPE)] * 2 + [SDS((3, D_FF), F32)] * 2 + [SDS((1, D_FF), F32)] * 2,
        compiler_params=_cp("parallel"))(u, u, cw, cw, cb, cb, da)


def _adamw(w, g, m, v, name):
    r, c = w.shape
    tr = r if r <= 512 else 256 if r % 256 == 0 else 88
    assert r % tr == 0, (name, r, tr)

    def body(w_ref, g_ref, m_ref, v_ref, d_ref, mo_ref, vo_ref):
        gv = g_ref[...]
        mn = ADAM_B1 * m_ref[...] + (1.0 - ADAM_B1) * gv
        vn = ADAM_B2 * v_ref[...] + (1.0 - ADAM_B2) * gv * gv
        m_hat = mn / (1.0 - ADAM_B1 ** ADAM_STEP)
        v_hat = vn / (1.0 - ADAM_B2 ** ADAM_STEP)
        d_ref[...] = -ADAM_LR * (m_hat / (jnp.sqrt(v_hat) + ADAM_EPS) + ADAM_WD * w_ref[...])
        mo_ref[...] = mn
        vo_ref[...] = vn

    blk = pl.BlockSpec((tr, c), lambda i: (i, 0))
    out = SDS((r, c), F32)
    return pl.pallas_call(body, name=name, grid=(r // tr,), in_specs=[blk] * 4, out_specs=[blk] * 3, out_shape=[out] * 3,
                          compiler_params=_cp("parallel"))(w, g, m, v)


def _half_tile(h):
    tr = h if h <= 512 else 256 if h % 256 == 0 else 176
    assert h % tr == 0, (h, tr)
    return tr


def _add_halves(g, r, core, name):
    s, h, c = r.shape
    tr = _half_tile(h)
    steps = h // tr

    def body(ix_ref, g_ref, r_ref, o_ref):
        o_ref[...] = (g_ref[...].astype(F32) + r_ref[...].astype(F32)).astype(o_ref.dtype)

    blk = pl.BlockSpec((1, tr, c), lambda i, j, ix: (i, j, 0))
    grid_spec = pltpu.PrefetchScalarGridSpec(
        num_scalar_prefetch=1, grid=(s, steps),
        in_specs=[pl.BlockSpec((1, tr, c), lambda i, j, ix: (i, ix[0] * steps + j, 0)), blk], out_specs=blk)
    return pl.pallas_call(body, name=name, grid_spec=grid_spec, out_shape=SDS(r.shape, WIRE_DTYPE),
                          compiler_params=_cp("parallel", "parallel"))(core.reshape(1), g, r)


def _sum_chips(own, recv, me, core, name):
    _, h, c = own.shape
    tr = _half_tile(h)

    def body(ix_ref, own_ref, recv_ref, o_ref):
        acc = own_ref[0].astype(F32)
        for j in range(3):
            acc = acc + recv_ref[j].astype(F32)
        o_ref[0] = acc

    grid_spec = pltpu.PrefetchScalarGridSpec(
        num_scalar_prefetch=1, grid=(h // tr,),
        in_specs=[pl.BlockSpec((1, tr, c), lambda i, ix: (ix[0], i, 0)), pl.BlockSpec((3, tr, c), lambda i, ix: (0, i, 0))],
        out_specs=pl.BlockSpec((1, tr, c), lambda i, ix: (ix[1], i, 0)))
    return pl.pallas_call(body, name=name, grid_spec=grid_spec, out_shape=SDS((2, h, c), F32),
                          compiler_params=_cp("parallel"))(jnp.stack([me, core]), own, recv)


ANY = pl.BlockSpec(memory_space=pl.ANY)


def _place():
    x, y, c = lax.axis_index("x"), lax.axis_index("y"), lax.axis_index("c")
    return x, y, c, [(1 - x, y), (x, 1 - y), (1 - x, 1 - y)]


def _gather_shards(shards, name):
    nt = len(shards)

    def body(*refs):
        ins, outs = refs[:nt], refs[nt:2 * nt]
        send, recv, fsend, frecv, osend, orecv = refs[2 * nt:]
        x, y, c, chips = _place()
        me = 2 * x + y

        def half(t, chip, cc):
            h = ins[t].shape[0] // 2
            return outs[t].at[chip, pl.ds(cc * h, h)]

        def ici(t, j):
            cx, cy = chips[j]
            h = ins[t].shape[0] // 2
            return pltpu.make_async_remote_copy(src_ref=ins[t].at[pl.ds(c * h, h)], dst_ref=half(t, me, c),
                                                send_sem=send.at[t, j], recv_sem=recv.at[t, j], device_id=(cx, cy, c), device_id_type=MESH)

        def landed(t, j):
            cx, cy = chips[j]
            blk = half(t, 2 * cx + cy, c)
            return pltpu.make_async_remote_copy(src_ref=blk, dst_ref=blk, send_sem=send.at[t, j], recv_sem=recv.at[t, j],
                                                device_id=(cx, cy, c), device_id_type=MESH)

        def d2d(t, j, cc):
            cx, cy = chips[j]
            blk = half(t, 2 * cx + cy, cc)
            return pltpu.make_async_remote_copy(src_ref=blk, dst_ref=blk, send_sem=fsend.at[t, j], recv_sem=frecv.at[t, j],
                                                device_id=(x, y, 1 - c), device_id_type=MESH)

        own = [pltpu.make_async_remote_copy(src_ref=ins[t], dst_ref=outs[t].at[me], send_sem=osend.at[t], recv_sem=orecv.at[t],
                                            device_id=(x, y, 1 - c), device_id_type=MESH) for t in range(nt)]
        for t in range(nt):
            for j in range(3):
                ici(t, j).start()
        for cp in own:
            cp.start()
        for t in range(nt):
            for j in range(3):
                landed(t, j).wait_recv()
                d2d(t, j, c).start()
        for t in range(nt):
            for j in range(3):
                d2d(t, j, 1 - c).wait_recv()
        for t in range(nt):
            for j in range(3):
                ici(t, j).wait_send()
                d2d(t, j, c).wait_send()
        for cp in own:
            cp.wait()

    return pl.pallas_call(
        body, name=name, in_specs=[ANY] * nt, out_specs=[ANY] * nt,
        out_shape=[SDS((4,) + s.shape, s.dtype) for s in shards],
        scratch_shapes=[pltpu.SemaphoreType.DMA((nt, 3))] * 4 + [pltpu.SemaphoreType.DMA((nt,))] * 2,
        compiler_params=pltpu.CompilerParams(has_side_effects=True))(*shards)


def _swap_halves(grads, name):
    nt = len(grads)

    def body(*refs):
        ins, outs = refs[:nt], refs[nt:2 * nt]
        send, recv = refs[2 * nt:]
        x, y, c, _ = _place()
        cps = []
        for t in range(nt):
            h = ins[t].shape[1] // 2
            cps.append(pltpu.make_async_remote_copy(src_ref=ins[t].at[pl.ds(0, 4), pl.ds((1 - c) * h, h)], dst_ref=outs[t], send_sem=send.at[t],
                                                    recv_sem=recv.at[t], device_id=(x, y, 1 - c), device_id_type=MESH))
        for cp in cps:
            cp.start()
        for cp in cps:
            cp.wait()

    return pl.pallas_call(
        body, name=name, in_specs=[ANY] * nt, out_specs=[ANY] * nt,
        out_shape=[SDS((g.shape[0], g.shape[1] // 2, g.shape[2]), g.dtype) for g in grads],
        scratch_shapes=[pltpu.SemaphoreType.DMA((nt,))] * 2,
        compiler_params=pltpu.CompilerParams(has_side_effects=True))(*grads)


def _send_to_owners(parts, name):
    nt = len(parts)

    def body(*refs):
        ins, outs = refs[:nt], refs[nt:2 * nt]
        send, recv = refs[2 * nt:]
        x, y, c, chips = _place()

        def ici(t, j):
            cx, cy = chips[j]
            return pltpu.make_async_remote_copy(src_ref=ins[t].at[2 * cx + cy], dst_ref=outs[t].at[j], send_sem=send.at[t, j],
                                                recv_sem=recv.at[t, j], device_id=(cx, cy, c), device_id_type=MESH)

        for t in range(nt):
            for j in range(3):
                ici(t, j).start()
        for t in range(nt):
            for j in range(3):
                ici(t, j).wait()

    return pl.pallas_call(
        body, name=name, in_specs=[ANY] * nt, out_specs=[ANY] * nt, out_shape=[SDS((3,) + p.shape[1:], p.dtype) for p in parts],
        scratch_shapes=[pltpu.SemaphoreType.DMA((nt, 3))] * 2,
        compiler_params=pltpu.CompilerParams(has_side_effects=True))(*parts)


def _join_halves(bufs, name):
    nt = len(bufs)

    def body(*refs):
        outs = refs[nt:2 * nt]
        send, recv = refs[2 * nt:]
        x, y, c, _ = _place()
        cps = [pltpu.make_async_remote_copy(src_ref=outs[t].at[c], dst_ref=outs[t].at[c], send_sem=send.at[t], recv_sem=recv.at[t],
                                            device_id=(x, y, 1 - c), device_id_type=MESH) for t in range(nt)]
        for cp in cps:
            cp.start()
        for t in range(nt):
            theirs = outs[t].at[1 - c]
            pltpu.make_async_remote_copy(src_ref=theirs, dst_ref=theirs, send_sem=send.at[t], recv_sem=recv.at[t],
                                         device_id=(x, y, 1 - c), device_id_type=MESH).wait_recv()
        for cp in cps:
            cp.wait_send()

    return pl.pallas_call(
        body, name=name, in_specs=[ANY] * nt, out_specs=[ANY] * nt, out_shape=[SDS(b.shape, b.dtype) for b in bufs],
        input_output_aliases={t: t for t in range(nt)},
        scratch_shapes=[pltpu.SemaphoreType.DMA((nt,))] * 2,
        compiler_params=pltpu.CompilerParams(has_side_effects=True))(*bufs)


def _exchange_small(v, reduce, name):
    rows = v.shape[0]

    def body(v_ref, o_ref, buf, send, recv):
        x, y, c, _ = _place()
        me = 4 * x + 2 * y + c
        buf[me] = v_ref[...]

        def peer(dx, dy, dc):
            return (1 - x if dx else x, 1 - y if dy else y, 1 - c if dc else c)

        peers = [(dx, dy, dc) for dx in range(2) for dy in range(2) for dc in range(2) if (dx, dy, dc) != (0, 0, 0)]
        cps = []
        for j, (dx, dy, dc) in enumerate(peers):
            cps.append(pltpu.make_async_remote_copy(src_ref=v_ref, dst_ref=buf.at[me], send_sem=send.at[j], recv_sem=recv.at[j],
                                                    device_id=peer(dx, dy, dc), device_id_type=MESH))
        for cp in cps:
            cp.start()
        for j, (dx, dy, dc) in enumerate(peers):
            px, py, pc = peer(dx, dy, dc)
            blk = buf.at[4 * px + 2 * py + pc]
            pltpu.make_async_remote_copy(src_ref=blk, dst_ref=blk, send_sem=send.at[j], recv_sem=recv.at[j],
                                         device_id=(px, py, pc), device_id_type=MESH).wait_recv()
        for cp in cps:
            cp.wait_send()
        if reduce:
            acc = buf[0]
            for j in range(1, 8):
                acc = acc + buf[j]
            o_ref[...] = acc
        else:
            o_ref[...] = buf[...]

    vm = pl.BlockSpec(memory_space=pltpu.VMEM)
    return pl.pallas_call(
        body, name=name, in_specs=[vm], out_specs=vm, out_shape=SDS((rows, 128) if reduce else (8, rows, 128), F32),
        scratch_shapes=[pltpu.VMEM((8, rows, 128), F32), pltpu.SemaphoreType.DMA((7,)), pltpu.SemaphoreType.DMA((7,))],
        compiler_params=pltpu.CompilerParams(has_side_effects=True))(v)


HBM = pl.BlockSpec(memory_space=pltpu.HBM)
SEM = pl.BlockSpec(memory_space=pltpu.SEMAPHORE)
TOKEN = pl.BlockSpec(memory_space=pltpu.VMEM)
TOKEN_SHAPE = SDS((8, 128), F32)
PEERS = 7


def _in_hbm(a):
    return pltpu.with_memory_space_constraint(a, pltpu.HBM)


def _split_params():
    return pltpu.CompilerParams(has_side_effects=pltpu.SideEffectType.DATAFLOW_SIDE_EFFECTING)


def _gather_start(shards, name, after=()):
    nt = len(shards)
    after, after_specs = _unread(after)

    def body(*refs):
        ins, lands = refs[:nt], refs[nt:2 * nt]
        outs = refs[2 * nt + len(after):]
        sends, recvs = outs[:nt], outs[nt:2 * nt]
        x, y, c, chips = _place()
        me = 2 * x + y
        for t in range(nt):
            h = ins[t].shape[0] // 2
            mine = pl.ds(c * h, h)
            for j, (cx, cy) in enumerate(chips):
                for dc in range(2):
                    pltpu.make_async_remote_copy(src_ref=ins[t].at[mine], dst_ref=lands[t].at[me, mine], send_sem=sends[t].at[2 * j + dc],
                                                 recv_sem=recvs[t].at[2 * j + c], device_id=(cx, cy, dc), device_id_type=MESH).start()
            pltpu.make_async_remote_copy(src_ref=ins[t], dst_ref=lands[t].at[me], send_sem=sends[t].at[PEERS - 1], recv_sem=recvs[t].at[PEERS - 1],
                                         device_id=(x, y, 1 - c), device_id_type=MESH).start()
        outs[-1][...] = jnp.zeros(TOKEN_SHAPE.shape, F32)

    lands = [lax.empty((4,) + s.shape, s.dtype) for s in shards]
    out = pl.pallas_call(
        body, name=name, in_specs=[HBM] * (2 * nt) + after_specs, out_specs=[SEM] * (2 * nt) + [HBM] * (2 * nt) + [TOKEN],
        out_shape=[pltpu.SemaphoreType.DMA((PEERS,))] * (2 * nt)
        + [pltpu.HBM(s.shape, s.dtype) for s in shards] + [pltpu.HBM(l.shape, l.dtype) for l in lands] + [TOKEN_SHAPE],
        input_output_aliases={t: 2 * nt + t for t in range(2 * nt)}, compiler_params=_split_params())(
            *[_in_hbm(s) for s in shards], *[_in_hbm(l) for l in lands], *after)
    return out[:nt], out[nt:2 * nt], out[2 * nt:3 * nt], out[3 * nt:4 * nt], out[-1]


def _gather_wait(sends, recvs, shards, lands, after, name):
    nt = len(shards)

    def body(*refs):
        ins, lands_ref = refs[:nt], refs[nt:2 * nt]
        send_refs, recv_refs = refs[2 * nt:3 * nt], refs[3 * nt:4 * nt]
        x, y, c, chips = _place()
        for t in range(nt):
            h = ins[t].shape[0] // 2
            for j, (cx, cy) in enumerate(chips):
                for cs in range(2):
                    blk = lands_ref[t].at[2 * cx + cy, pl.ds(cs * h, h)]
                    pltpu.make_async_remote_copy(src_ref=blk, dst_ref=blk, send_sem=send_refs[t].at[2 * j + cs], recv_sem=recv_refs[t].at[2 * j + cs],
                                                 device_id=(cx, cy, cs), device_id_type=MESH).wait()
            blk = lands_ref[t].at[2 * x + y]
            pltpu.make_async_remote_copy(src_ref=blk, dst_ref=blk, send_sem=send_refs[t].at[PEERS - 1], recv_sem=recv_refs[t].at[PEERS - 1],
                                         device_id=(x, y, 1 - c), device_id_type=MESH).wait()

    out = pl.pallas_call(
        body, name=name, in_specs=[HBM] * (2 * nt) + [SEM] * (2 * nt) + [ANY], out_specs=[HBM] * (2 * nt),
        out_shape=[pltpu.HBM(s.shape, s.dtype) for s in shards] + [pltpu.HBM(l.shape, l.dtype) for l in lands],
        input_output_aliases={t: t for t in range(2 * nt)}, compiler_params=_split_params())(*shards, *lands, *sends, *recvs, after)
    return out[nt:]


def _scatter_start(g, name):
    _, r, c_ = g.shape
    h = r // 2

    def body(g_ref, land, send, recv, g_thru, land_thru, token):
        x, y, c, chips = _place()
        for j, (cx, cy) in enumerate(chips):
            for dc in range(2):
                pltpu.make_async_remote_copy(src_ref=g_ref.at[2 * cx + cy, pl.ds(dc * h, h)], dst_ref=land.at[2 * j + c], send_sem=send.at[2 * j + dc],
                                             recv_sem=recv.at[2 * j + c], device_id=(cx, cy, dc), device_id_type=MESH).start()
        pltpu.make_async_remote_copy(src_ref=g_ref.at[2 * x + y, pl.ds((1 - c) * h, h)], dst_ref=land.at[PEERS - 1], send_sem=send.at[PEERS - 1],
                                     recv_sem=recv.at[PEERS - 1], device_id=(x, y, 1 - c), device_id_type=MESH).start()
        token[...] = jnp.zeros(TOKEN_SHAPE.shape, F32)

    land = lax.empty((PEERS, h, c_), g.dtype)
    return pl.pallas_call(
        body, name=name, in_specs=[HBM, HBM], out_specs=[SEM, SEM, HBM, HBM, TOKEN],
        out_shape=[pltpu.SemaphoreType.DMA((PEERS,)), pltpu.SemaphoreType.DMA((PEERS,)), pltpu.HBM(g.shape, g.dtype),
                   pltpu.HBM(land.shape, land.dtype), TOKEN_SHAPE],
        input_output_aliases={0: 2, 1: 3}, compiler_params=_split_params())(_in_hbm(g), _in_hbm(land))


def _scatter_wait(started, after, name):
    nt = len(started)

    def body(*refs):
        lands = refs[nt:2 * nt]
        sends, recvs = refs[2 * nt:3 * nt], refs[3 * nt:4 * nt]
        x, y, c, chips = _place()
        peers = [(cx, cy, dc) for cx, cy in chips for dc in range(2)] + [(x, y, 1 - c)]
        for t in range(nt):
            for k, peer in enumerate(peers):
                blk = lands[t].at[k]
                pltpu.make_async_remote_copy(src_ref=blk, dst_ref=blk, send_sem=sends[t].at[k], recv_sem=recvs[t].at[k],
                                             device_id=peer, device_id_type=MESH).wait()

    gs, lands = [s[2] for s in started], [s[3] for s in started]
    after, after_specs = _unread(after)
    out = pl.pallas_call(
        body, name=name, in_specs=[HBM] * (2 * nt) + [SEM] * (2 * nt) + after_specs, out_specs=[HBM] * (2 * nt),
        out_shape=[pltpu.HBM(a.shape, a.dtype) for a in gs + lands],
        input_output_aliases={t: t for t in range(2 * nt)}, compiler_params=_split_params())(
            *gs, *lands, *[s[0] for s in started], *[s[1] for s in started], *after)
    return out[:nt], out[nt:]


def _sum_devices(g, land, me, core, name):
    npeer, h, c = land.shape
    tr = _half_tile(h)
    steps = h // tr

    def body(ix_ref, own_ref, land_ref, o_ref):
        acc = own_ref[0].astype(F32)
        for j in range(npeer):
            acc = acc + land_ref[j].astype(F32)
        o_ref[0] = acc

    grid_spec = pltpu.PrefetchScalarGridSpec(
        num_scalar_prefetch=1, grid=(steps,),
        in_specs=[pl.BlockSpec((1, tr, c), lambda i, ix: (ix[0], ix[1] * steps + i, 0)), pl.BlockSpec((npeer, tr, c), lambda i, ix: (0, i, 0))],
        out_specs=pl.BlockSpec((1, tr, c), lambda i, ix: (ix[1], i, 0)))
    return pl.pallas_call(body, name=name, grid_spec=grid_spec, out_shape=SDS((2, h, c), F32),
                          compiler_params=_cp("parallel"))(jnp.stack([me, core]), g, land)


def _pack_small(parts):
    flat = jnp.concatenate([p.reshape(-1) for p in parts])
    total = flat.shape[0]
    rows = -(-total // 1024) * 8
    return jnp.pad(flat, (0, rows * 128 - total)).reshape(rows, 128)


def _unpack_small(packed, shapes):
    flat = packed.reshape(-1)
    out, off = [], 0
    for s in shapes:
        size = int(np.prod(s))
        out.append(flat[off:off + size].reshape(s))
        off += size
    return out


def _local_step(x, mem, target, w_in, first_after, mid_weights, ffn_weights, on_grad, gains, conv_w, conv_b, hg_lb):
    n = x.shape[0]
    cos, sin = _rope_tables(n)
    seg = _hg_segments()
    gp, gn = _hg_pair_sums()
    gq2 = jnp.tile(gains["q_norm_g"], (1, 2))
    gk2 = jnp.tile(gains["k_norm_g"], (1, 2))
    a0 = hg_lb[:, 0:1, :]
    a1 = hg_lb[:, 1:2, :]

    p, h1 = _norm_mm(x, gains["pre_mix_g"], w_in, F32, 512, 1664, "in_proj", after=(first_after,))
    qr, kr = _qk_prep(p, gq2, gk2, cos, sin, "qk_prep")
    heads = lambda a: a.reshape(n, ATT_KV_HEADS, ATT_HEAD_DIM).transpose(1, 0, 2)
    kh = heads(kr)
    vh = heads(p[:, OFF_AV:OFF_AV + ATT_KV_DIM].astype(MXU_DTYPE))
    att = _attn_fwd(qr, kh, vh, "attn_fwd")
    o2 = _hgrn_fwd(p, a0, a1, seg, "hgrn_fwd")
    rec = _hg_post(o2, p, gains["hg_out_norm_g"], "hg_post")
    cat = jnp.concatenate([att, rec], axis=1)
    w_out, w_xq, w_xkv, w_xo = mid_weights(cat)
    mixed = _mm(cat, w_out, "nn", F32, 512, 1024, "out_proj")
    x1 = _resid_norm(x, mixed, gains["post_mix_g"], "mix_resid")
    xq, h2 = _norm_mm(x1, gains["pre_x_g"], w_xq, MXU_DTYPE, 512, 1024, "xq_proj")
    kv, mn = _norm_mm(mem, gains["mem_norm_g"], w_xkv, MXU_DTYPE, 256, 2048, "xkv_proj")
    ox = _xattn_fwd(xq, kv, "xattn_fwd")
    xo = _mm(ox, w_xo, "nn", F32, 512, 1024, "xo_proj")
    x2 = _resid_norm(x1, xo, gains["post_x_g"], "x_resid")
    w_up, w_down = ffn_weights(x2)
    u, h3 = _norm_mm(x2, gains["pre_ffn_g"], w_up, F32, 512, 1408, "up_proj")
    act = _conv_gate(u, conv_w, conv_b, "conv_gate")
    dn = _mm(act, w_down, "nn", F32, 512, 1024, "down_proj")
    d3, loss = _final_loss(x2, dn, gains["post_ffn_g"], target, "ffn_resid_loss")

    gs = {}
    d_dn, gs["post_ffn_g"] = _norm_bwd(dn, gains["post_ffn_g"], d3, None, MXU_DTYPE, "ffn_post_bwd")
    tok = on_grad("w_down", _mm(act, d_dn, "tn", WIRE_DTYPE, 1408, 1024, "down_dw"))
    d_act = _mm(d_dn, w_down, "nt", F32, 512, 1408, "down_dx", after=(tok,))
    du_g, du_v, dcw_g, dcw_v, dcb_g, dcb_v = _conv_gate_bwd(u, conv_w, conv_b, d_act, "conv_gate_bwd")
    gs["conv_w"] = jnp.concatenate([dcw_g, dcw_v], axis=1)
    gs["conv_b"] = jnp.concatenate([dcb_g, dcb_v], axis=1)
    tok = on_grad("w_up", (_mm(h3, du_g, "tn", WIRE_DTYPE, 512, 1408, "up_dw_gate"), _mm(h3, du_v, "tn", WIRE_DTYPE, 512, 1408, "up_dw_value")))
    d_h3 = _mm_nt_halves(du_g, du_v, w_up, F32, 512, 512, "up_dx", after=(tok,))
    d2, gs["pre_ffn_g"] = _norm_bwd(x2, gains["pre_ffn_g"], d_h3, d3, F32, "ffn_pre_bwd")
    d_xo, gs["post_x_g"] = _norm_bwd(xo, gains["post_x_g"], d2, None, MXU_DTYPE, "x_post_bwd")
    tok = on_grad("w_xo", _mm(ox, d_xo, "tn", WIRE_DTYPE, 512, 1024, "xo_dw"))
    d_ox = _mm(d_xo, w_xo, "nt", MXU_DTYPE, 512, 1024, "xo_dx", after=(tok,))
    d_xq, d_k, d_v = _xattn_bwd(xq, kv, d_ox, "xattn_bwd")
    d_kv = jnp.concatenate([d_k, d_v], axis=1).astype(MXU_DTYPE)
    tok = on_grad("w_xq", _mm(h2, d_xq, "tn", WIRE_DTYPE, 512, 1024, "xq_dw"))
    tok_kv = on_grad("w_xkv", _mm(mn, d_kv, "tn", WIRE_DTYPE, 512, 1024, "xkv_dw"))
    d_h2 = _mm(d_xq, w_xq, "nt", F32, 512, 1024, "xq_dx", after=(tok, tok_kv))
    d_mn = _mm(d_kv, w_xkv, "nt", F32, 256, 1024, "xkv_dx")
    _, gs["mem_norm_g"] = _norm_bwd(mem, gains["mem_norm_g"], d_mn, None, MXU_DTYPE, "mem_norm_bwd")
    d1, gs["pre_x_g"] = _norm_bwd(x1, gains["pre_x_g"], d_h2, d2, F32, "x_pre_bwd")
    d_mixed, gs["post_mix_g"] = _norm_bwd(mixed, gains["post_mix_g"], d1, None, MXU_DTYPE, "mix_post_bwd")
    tok = on_grad("w_out", _mm(cat, d_mixed, "tn", WIRE_DTYPE, 512, 1024, "out_dw"))
    d_cat = _mm(d_mixed, w_out, "nt", MXU_DTYPE, 512, 1024, "out_dx", after=(tok,))
    d_o, d_hg, dg_hg = _hg_post_bwd(o2, p, gains["hg_out_norm_g"], d_cat, "hg_post_bwd")
    gs["hg_out_norm_g"] = dg_hg.reshape(HG_HEADS, HG_HEAD_DIM).sum(axis=0, keepdims=True)
    dhq2, dlf_q, s0 = _hgrn_bwd_q(p, a0, a1, seg, gp, d_o, "hgrn_bwd_q")
    dz2, dhv2, dlb = _hgrn_bwd_kv(p, a0, a1, seg, gn, d_o, dlf_q, s0, "hgrn_bwd_kv")
    lb = jax.nn.sigmoid(a0 - a1)
    da0 = dlb * lb * (1.0 - lb)
    gs["hg_lb"] = jnp.concatenate([da0, -da0], axis=1)
    d_qr, d_kh, d_vh = _attn_bwd(qr, kh, vh, d_cat, "attn_bwd")
    unheads = lambda a: a.transpose(2, 0, 1).reshape(n, ATT_KV_DIM)
    d_aq, d_ak, dgq, dgk = _qk_prep_bwd(p, gq2, gk2, cos, sin, d_qr, unheads(d_kh), "qk_prep_bwd")
    gs["q_norm_g"] = dgq.reshape(ATT_HEADS, ATT_HEAD_DIM).sum(axis=0, keepdims=True)
    gs["k_norm_g"] = dgk.reshape(ATT_KV_HEADS, ATT_HEAD_DIM).sum(axis=0, keepdims=True)
    d_p = jnp.concatenate([d_aq, d_ak, unheads(d_vh).astype(MXU_DTYPE), (dhq2[0] + dhq2[1]).astype(MXU_DTYPE),
                           dz2[0].astype(MXU_DTYPE), dz2[1].astype(MXU_DTYPE), (dhv2[0] + dhv2[1]).astype(MXU_DTYPE), d_hg], axis=1)
    tok = on_grad("w_in", _mm(h1, d_p, "tn", WIRE_DTYPE, 512, 1664, "in_dw"))
    d_h1 = _mm(d_p, w_in, "nt", F32, 512, 1024, "in_dx", after=(tok,))
    grad_x, gs["pre_mix_g"] = _norm_bwd(x, gains["pre_mix_g"], d_h1, d1, F32, "mix_pre_bwd")
    return loss, grad_x, gs


MATS = ("w_in", "w_out", "w_xq", "w_xkv", "w_xo", "w_up", "w_down")
COL_SHARDED = ("w_in", "w_xkv", "w_up")
GAINS = ("pre_mix_g", "q_norm_g", "k_norm_g", "hg_out_norm_g", "post_mix_g", "pre_x_g", "mem_norm_g", "post_x_g", "pre_ffn_g", "post_ffn_g")
WEIGHTS = ('pre_mix_g', 'w_in', 'q_norm_g', 'k_norm_g', 'hg_lb', 'hg_out_norm_g', 'w_out', 'post_mix_g', 'pre_x_g', 'mem_norm_g', 'w_xq',
           'w_xkv', 'w_xo', 'post_x_g', 'pre_ffn_g', 'w_up', 'conv_w', 'conv_b', 'w_down', 'post_ffn_g')


def kernel(x, mem, pre_mix_g, w_in, q_norm_g, k_norm_g, hg_lb, hg_out_norm_g, w_out, post_mix_g, pre_x_g, mem_norm_g, w_xq, w_xkv, w_xo, post_x_g, pre_ffn_g, w_up, conv_w, conv_b, w_down, post_ffn_g, loss_target, m_pre_mix_g, m_w_in, m_q_norm_g, m_k_norm_g, m_hg_lb, m_hg_out_norm_g, m_w_out, m_post_mix_g, m_pre_x_g, m_mem_norm_g, m_w_xq, m_w_xkv, m_w_xo, m_post_x_g, m_pre_ffn_g, m_w_up, m_conv_w, m_conv_b, m_w_down, m_post_ffn_g, v_pre_mix_g, v_w_in, v_q_norm_g, v_k_norm_g, v_hg_lb, v_hg_out_norm_g, v_w_out, v_post_mix_g, v_pre_x_g, v_mem_norm_g, v_w_xq, v_w_xkv, v_w_xo, v_post_x_g, v_pre_ffn_g, v_w_up, v_conv_w, v_conv_b, v_w_down, v_post_ffn_g):
    args = dict(locals())
    w = {k: args[k] for k in WEIGHTS}
    m = {k: args["m_" + k] for k in WEIGHTS}
    v = {k: args["v_" + k] for k in WEIGHTS}
    chip = 2 * lax.axis_index("x") + lax.axis_index("y")
    core = lax.axis_index("c")

    shards = {k: w[k][0].astype(WIRE_DTYPE) for k in MATS}

    def whole(k, g):
        return jnp.concatenate([g[s] for s in range(4)], axis=1) if k in COL_SHARDED else g.reshape(-1, g.shape[-1])

    w_in_full = whole("w_in", _gather_shards([shards["w_in"]], "gather_w_in")[0])
    mid_names, ffn_names = ("w_out", "w_xq", "w_xkv", "w_xo"), ("w_up", "w_down")
    mid = _gather_start([shards[k] for k in mid_names], "gather_mid_start", after=(w_in_full,))
    ffn = _gather_start([shards[k] for k in ffn_names], "gather_ffn_start", after=(mid[4],))

    def mid_weights(after):
        return [whole(k, g) for k, g in zip(mid_names, _gather_wait(*mid[:4], after, "gather_mid_wait"))]

    def ffn_weights(after):
        return [whole(k, g) for k, g in zip(ffn_names, _gather_wait(*ffn[:4], after, "gather_ffn_wait"))]

    small_in = _exchange_small(_pack_small([w["conv_w"][0], w["hg_lb"]]), False, "gather_small")
    cw_parts, lb_parts = [], []
    for s in range(4):
        cw_s, lb_s = _unpack_small(small_in[2 * s], [w["conv_w"][0].shape, w["hg_lb"].shape])
        cw_parts.append(cw_s)
        lb_parts.append(lb_s)
    conv_w_full = jnp.concatenate(cw_parts, axis=1)
    hg_lb_full = jnp.concatenate(lb_parts, axis=2)

    def cols_by_owner(g, chips):
        return g.reshape(g.shape[0], chips, g.shape[1] // chips).transpose(1, 0, 2)

    started = {}

    def on_grad(k, g):
        if k == "w_up":
            by_owner = jnp.concatenate([cols_by_owner(g[0], 2), cols_by_owner(g[1], 2)], axis=0)
        elif k in COL_SHARDED:
            by_owner = cols_by_owner(g, 4)
        else:
            by_owner = g.reshape(4, g.shape[0] // 4, g.shape[1])
        *started[k], token = _scatter_start(by_owner, "grad_start_" + k)
        return token

    gains = {k: w[k] for k in GAINS}
    loss, grad_x, gs = _local_step(x[0], mem[0], loss_target[0], w_in_full, ffn[4], mid_weights, ffn_weights, on_grad, gains,
                                   conv_w_full, w["conv_b"], hg_lb_full)
    loss = lax.psum(loss[0, 0], ("x", "y", "c"))

    grads, delta, new_m, new_v = {}, {}, {}, {}

    def reduce_matrices(names, after, tag):
        sent, landed = _scatter_wait([started[k] for k in names], after, "grad_wait_" + tag)
        halves = [_sum_devices(g, land, chip, core, "grad_sum_" + k) for k, g, land in zip(names, sent, landed)]
        for k, r in zip(names, _join_halves(halves, "grad_join_" + tag)):
            grads[k] = r.reshape(1, -1, r.shape[-1])

    def adamw(names):
        for k in names:
            shape = w[k].shape
            two_d = lambda a: a.reshape(-1, shape[-1])
            d, mo, vo = _adamw(two_d(w[k]), two_d(grads[k]), two_d(m[k]), two_d(v[k]), "adamw_" + k)
            delta[k], new_m[k], new_v[k] = d.reshape(shape), mo.reshape(shape), vo.reshape(shape)
            grads[k] = grads[k].reshape(shape)

    early = tuple(k for k in MATS if k != "w_in")
    reduce_matrices(early, (grad_x,), "early")
    adamw(early)

    small_names = GAINS + ("conv_b", "conv_w", "hg_lb")
    small_shapes = [gs[k].shape for k in small_names]
    summed = _unpack_small(_exchange_small(_pack_small([gs[k] for k in small_names]), True, "reduce_small"), small_shapes)
    for k, g in zip(small_names, summed):
        grads[k] = g
    ncw = w["conv_w"].shape[2]
    grads["conv_w"] = lax.dynamic_slice_in_dim(grads["conv_w"], chip * ncw, ncw, axis=1)[None]
    nlb = w["hg_lb"].shape[2]
    grads["hg_lb"] = lax.dynamic_slice_in_dim(grads["hg_lb"], chip * nlb, nlb, axis=2)
    adamw(small_names)

    reduce_matrices(("w_in",), tuple(new_v[k] for k in early + small_names), "late")
    adamw(("w_in",))
    return (loss, grad_x[None], *[grads[k] for k in WEIGHTS], *[delta[k] for k in WEIGHTS],
            *[new_m[k] for k in WEIGHTS], *[new_v[k] for k in WEIGHTS])
```

```python
import functools

import numpy as np
import jax
import jax.numpy as jnp
from jax import lax
from jax.experimental import pallas as pl
from jax.experimental.pallas import tpu as pltpu

F32 = jnp.float32
MXU_DTYPE = jnp.bfloat16
WIRE_DTYPE = jnp.bfloat16
VMEM_LIMIT_BYTES = 56 * 1024 * 1024
EPS = 1e-6
MESH = pl.DeviceIdType.MESH

D_MODEL = 1024
GRID_W = 64
ATT_HEADS, ATT_KV_HEADS, ATT_HEAD_DIM = 8, 2, 64
ATT_GROUP = ATT_HEADS // ATT_KV_HEADS
ATT_Q_DIM, ATT_KV_DIM = 512, 128
ROPE_THETA = 10000.0
HG_HEADS, HG_HEAD_DIM, HG_DIM = 4, 128, 512
HG_CHUNK = 128
HG_LEVELS = 7
HG_PAIR = 2 * HG_HEAD_DIM
X_HEADS, X_HEAD_DIM = 4, 256
D_FF = 2816
FF_COLS = 256
FF_BLOCKS = D_FF // FF_COLS
N_IN = 3328
OFF_AQ, OFF_AK, OFF_AV, OFF_HQ, OFF_ZF, OFF_ZB, OFF_HI, OFF_HG = 0, 512, 640, 768, 1280, 1792, 2304, 2816

ADAM_LR, ADAM_B1, ADAM_B2, ADAM_EPS, ADAM_WD, ADAM_STEP = 0.001, 0.9, 0.999, 1e-08, 0.01, 10

SDS = jax.ShapeDtypeStruct


def _cp(*sem):
    return pltpu.CompilerParams(dimension_semantics=sem, vmem_limit_bytes=VMEM_LIMIT_BYTES)


def _dot(a, b, form="nn"):
    dims = {"nn": (((1,), (0,)), ((), ())), "nt": (((1,), (1,)), ((), ())), "tn": (((0,), (0,)), ((), ()))}[form]
    return lax.dot_general(a.astype(MXU_DTYPE), b.astype(MXU_DTYPE), dims, preferred_element_type=F32)


def _sigmoid(x):
    return 1.0 / (1.0 + jnp.exp(-x))


def _rstd(x):
    return lax.rsqrt(jnp.mean(x * x, axis=-1, keepdims=True) + EPS)


def _rms_bwd(x, g, dy):
    r = _rstd(x)
    xh = x * r
    dn = dy * g
    dx = r * (dn - xh * jnp.mean(dn * xh, axis=-1, keepdims=True))
    return dx, jnp.sum(dy * xh, axis=0, keepdims=True)


def _unread(after):
    after = tuple(a for a in after if a is not None)
    return after, [pl.BlockSpec(memory_space=pl.ANY)] * len(after)


def _mm(a, b, form, out_dtype, tm, tn, name, after=()):
    after, after_specs = _unread(after)
    if form == "nn":
        (m, k), n = a.shape, b.shape[1]
    elif form == "nt":
        (m, k), n = a.shape, b.shape[0]
    else:
        (k, m), n = a.shape, b.shape[1]
    tm, tn = min(tm, m), min(tn, n)
    assert m % tm == 0 and n % tn == 0, (name, m, n, tm, tn)

    def body(a_ref, b_ref, *rest):
        o_ref = rest[-1]
        o_ref[...] = _dot(a_ref[...], b_ref[...], form).astype(o_ref.dtype)

    a_spec = pl.BlockSpec((k, tm), lambda i, j: (0, i)) if form == "tn" else pl.BlockSpec((tm, k), lambda i, j: (i, 0))
    b_spec = pl.BlockSpec((tn, k), lambda i, j: (j, 0)) if form == "nt" else pl.BlockSpec((k, tn), lambda i, j: (0, j))
    return pl.pallas_call(
        body, name=name, grid=(m // tm, n // tn), in_specs=[a_spec, b_spec] + after_specs,
        out_specs=pl.BlockSpec((tm, tn), lambda i, j: (i, j)), out_shape=SDS((m, n), out_dtype),
        compiler_params=_cp("parallel", "parallel"))(a, b, *after)


def _mm_nt_parts(a, b, out_dtype, tm, tn, name, after=()):
    after, after_specs = _unread(after)
    parts, n, p = b.shape
    m = a.shape[-2]
    tm, tn = min(tm, m), min(tn, n)
    assert m % tm == 0 and n % tn == 0 and a.shape[-1] * (a.ndim - 1) == parts * p, (name, a.shape, b.shape)
    per = a.shape[-1] // p

    def body(*refs):
        o_ref = refs[-1]
        acc = None
        for s in range(parts):
            av = refs[s][...] if a.ndim == 2 else refs[s][0]
            term = _dot(av, refs[parts + s][0], "nt")
            acc = term if acc is None else acc + term
        o_ref[...] = acc.astype(o_ref.dtype)

    if a.ndim == 2:
        a_specs = [pl.BlockSpec((tm, p), lambda i, j, s=s: (i, s)) for s in range(parts)]
    else:
        a_specs = [pl.BlockSpec((1, tm, p), lambda i, j, s=s: (s // per, i, s % per)) for s in range(parts)]
    b_specs = [pl.BlockSpec((1, tn, p), lambda i, j, s=s: (s, j, 0)) for s in range(parts)]
    return pl.pallas_call(
        body, name=name, grid=(m // tm, n // tn), in_specs=a_specs + b_specs + after_specs,
        out_specs=pl.BlockSpec((tm, tn), lambda i, j: (i, j)), out_shape=SDS((m, n), out_dtype),
        compiler_params=_cp("parallel", "parallel"))(*([a] * parts), *([b] * parts), *after)


def _dw_by_owner(a, b, tm, name):
    k, m = a.shape
    n = b.shape[-1] * (b.ndim - 1)
    tn = n // 4
    tm = min(tm, m)
    assert m % tm == 0 and n % 4 == 0, (name, a.shape, b.shape)

    def body(a_ref, b_ref, o_ref):
        bv = b_ref[...] if b.ndim == 2 else b_ref[0]
        o_ref[0] = _dot(a_ref[...], bv, "tn").astype(o_ref.dtype)

    b_spec = pl.BlockSpec((k, tn), lambda i, j: (0, j)) if b.ndim == 2 else pl.BlockSpec((1, k, tn), lambda i, j: (j // 2, 0, j % 2))
    return pl.pallas_call(
        body, name=name, grid=(m // tm, 4), in_specs=[pl.BlockSpec((k, tm), lambda i, j: (0, i)), b_spec],
        out_specs=pl.BlockSpec((1, tm, tn), lambda i, j: (j, i, 0)), out_shape=SDS((4, m, tn), WIRE_DTYPE),
        compiler_params=_cp("parallel", "parallel"))(a, b)


def _norm_mm(x, g, w, out_dtype, tm, tn, name, after=()):
    after, after_specs = _unread(after)
    m, d = x.shape
    sharded = w.ndim == 3
    n = w.shape[-1] * (w.shape[0] if sharded else 1)
    tm, tn = min(tm, m), (w.shape[-1] if sharded else min(tn, n))
    assert m % tm == 0 and n % tn == 0, (name, m, n, tm, tn)

    def body(x_ref, g_ref, w_ref, *rest):
        o_ref, h_ref, hs = rest[-3:]

        @pl.when(pl.program_id(1) == 0)
        def _():
            xv = x_ref[...]
            h = (xv * _rstd(xv) * g_ref[...]).astype(MXU_DTYPE)
            hs[...] = h
            h_ref[...] = h

        o_ref[...] = _dot(hs[...], w_ref[0] if sharded else w_ref[...]).astype(o_ref.dtype)

    w_spec = pl.BlockSpec((1, d, tn), lambda i, j: (j, 0, 0)) if sharded else pl.BlockSpec((d, tn), lambda i, j: (0, j))
    return pl.pallas_call(
        body, name=name, grid=(m // tm, n // tn),
        in_specs=[pl.BlockSpec((tm, d), lambda i, j: (i, 0)), pl.BlockSpec((1, d), lambda i, j: (0, 0)), w_spec] + after_specs,
        out_specs=[pl.BlockSpec((tm, tn), lambda i, j: (i, j)), pl.BlockSpec((tm, d), lambda i, j: (i, 0))],
        out_shape=[SDS((m, n), out_dtype), SDS((m, d), MXU_DTYPE)],
        scratch_shapes=[pltpu.VMEM((tm, d), MXU_DTYPE)],
        compiler_params=_cp("parallel", "arbitrary"))(x, g, w, *after)


ROW_TILE = 256


def _resid_norm(x, y, g, name):
    n, d = x.shape
    tr = min(ROW_TILE, n)

    def body(x_ref, y_ref, g_ref, o_ref):
        yv = y_ref[...]
        o_ref[...] = x_ref[...] + yv * _rstd(yv) * g_ref[...]

    row = pl.BlockSpec((tr, d), lambda i: (i, 0))
    return pl.pallas_call(
        body, name=name, grid=(n // tr,), in_specs=[row, row, pl.BlockSpec((1, d), lambda i: (0, 0))],
        out_specs=row, out_shape=SDS((n, d), F32), compiler_params=_cp("parallel"))(x, y, g)


def _norm_bwd(x, g, dy, res, out_dtype, name):
    n, d = x.shape
    tr = min(ROW_TILE, n)
    has_res = res is not None

    def body(*refs):
        x_ref, g_ref, dy_ref = refs[:3]
        dx_ref, dg_ref = refs[-2:]
        dx, dg = _rms_bwd(x_ref[...], g_ref[...], dy_ref[...].astype(F32))
        if has_res:
            dx = dx + refs[3][...]
        dx_ref[...] = dx.astype(dx_ref.dtype)

        @pl.when(pl.program_id(0) == 0)
        def _():
            dg_ref[...] = jnp.zeros_like(dg_ref)

        dg_ref[...] += dg

    row = pl.BlockSpec((tr, d), lambda i: (i, 0))
    vec = pl.BlockSpec((1, d), lambda i: (0, 0))
    ins = [x, g, dy] + ([res] if has_res else [])
    return pl.pallas_call(
        body, name=name, grid=(n // tr,), in_specs=[row, vec, row] + ([row] if has_res else []),
        out_specs=[row, vec], out_shape=[SDS((n, d), out_dtype), SDS((1, d), F32)],
        compiler_params=_cp("arbitrary"))(*ins)


def _final_loss(x, y, g, target, name):
    n, d = x.shape
    tr = min(ROW_TILE, n)

    def body(x_ref, y_ref, g_ref, t_ref, d_ref, l_ref):
        yv = y_ref[...]
        diff = x_ref[...] + yv * _rstd(yv) * g_ref[...] - t_ref[...]
        d_ref[...] = diff * (1.0 / d)

        @pl.when(pl.program_id(0) == 0)
        def _():
            l_ref[...] = jnp.zeros_like(l_ref)

        l_ref[...] += 0.5 * jnp.sum(jnp.mean(diff * diff, axis=-1, keepdims=True), axis=0, keepdims=True)

    row = pl.BlockSpec((tr, d), lambda i: (i, 0))
    return pl.pallas_call(
        body, name=name, grid=(n // tr,), in_specs=[row, row, pl.BlockSpec((1, d), lambda i: (0, 0)), row],
        out_specs=[row, pl.BlockSpec((1, 1), lambda i: (0, 0))], out_shape=[SDS((n, d), F32), SDS((1, 1), F32)],
        compiler_params=_cp("arbitrary"))(x, y, g, target)


def _rope_tables(n):
    pairs = ATT_HEAD_DIM // 4
    t = np.arange(n)
    inv = np.power(ROPE_THETA, -np.arange(pairs, dtype=np.float32) / pairs).astype(np.float32)
    ang = np.concatenate([(t // GRID_W)[:, None].astype(np.float32) * inv, (t % GRID_W)[:, None].astype(np.float32) * inv], axis=-1)
    cos = np.repeat(np.cos(ang), 2, axis=-1)
    sin = np.repeat(np.sin(ang), 2, axis=-1) * np.tile(np.array([-1.0, 1.0], np.float32), ATT_HEAD_DIM // 2)
    return jnp.asarray(np.tile(cos, 2), F32), jnp.asarray(np.tile(sin, 2), F32)


def _swap_pairs(x):
    lane = lax.broadcasted_iota(jnp.int32, x.shape, 1)
    return jnp.where((lane & 1) == 0, pltpu.roll(x, 127, axis=1), pltpu.roll(x, 1, axis=1))


def _head_mean(v):
    lane = lax.broadcasted_iota(jnp.int32, v.shape, 1)
    lo = jnp.where(lane < ATT_HEAD_DIM, v, 0.0)
    s0 = jnp.sum(lo, axis=-1, keepdims=True)
    s1 = jnp.sum(v - lo, axis=-1, keepdims=True)
    return jnp.where(lane < ATT_HEAD_DIM, s0, s1) * (1.0 / ATT_HEAD_DIM)


def _qk_prep(p, gq, gk, cos, sin, name):
    n = p.shape[0]
    tr = min(ROW_TILE, n)

    def one(xv, g, c, s):
        xn = xv * lax.rsqrt(_head_mean(xv * xv) + EPS) * g
        return xn * c + _swap_pairs(xn) * s

    def body(q_ref, k_ref, gq_ref, gk_ref, c_ref, s_ref, qo_ref, ko_ref):
        c, s = c_ref[...], s_ref[...]
        for j in range(ATT_Q_DIM // 128):
            qo_ref[:, j * 128:(j + 1) * 128] = one(q_ref[:, j * 128:(j + 1) * 128], gq_ref[...], c, s).astype(qo_ref.dtype)
        ko_ref[...] = one(k_ref[...], gk_ref[...], c, s).astype(ko_ref.dtype)

    vec = pl.BlockSpec((1, 128), lambda i: (0, 0))
    tab = pl.BlockSpec((tr, 128), lambda i: (i, 0))
    return pl.pallas_call(
        body, name=name, grid=(n // tr,),
        in_specs=[pl.BlockSpec((tr, ATT_Q_DIM), lambda i: (i, 0)), pl.BlockSpec((tr, 128), lambda i: (i, OFF_AK // 128)), vec, vec, tab, tab],
        out_specs=[pl.BlockSpec((tr, ATT_Q_DIM), lambda i: (i, 0)), tab],
        out_shape=[SDS((n, ATT_Q_DIM), MXU_DTYPE), SDS((n, ATT_KV_DIM), MXU_DTYPE)],
        compiler_params=_cp("parallel"))(p, p, gq, gk, cos, sin)


def _qk_prep_bwd(p, gq, gk, cos, sin, dq, dk, name):
    n = p.shape[0]
    tr = min(ROW_TILE, n)

    def one(xv, g, c, s, dout):
        dxn = dout * c + _swap_pairs(dout * s)
        r = lax.rsqrt(_head_mean(xv * xv) + EPS)
        xh = xv * r
        dn = dxn * g
        dx = r * (dn - xh * _head_mean(dn * xh))
        return dx, jnp.sum(dxn * xh, axis=0, keepdims=True)

    def body(q_ref, k_ref, gq_ref, gk_ref, c_ref, s_ref, dq_ref, dk_ref, dqo_ref, dko_ref, dgq_ref, dgk_ref):
        @pl.when(pl.program_id(0) == 0)
        def _():
            dgq_ref[...] = jnp.zeros_like(dgq_ref)
            dgk_ref[...] = jnp.zeros_like(dgk_ref)

        c, s = c_ref[...], s_ref[...]
        for j in range(ATT_Q_DIM // 128):
            sl = slice(j * 128, (j + 1) * 128)
            dx, dg = one(q_ref[:, sl], gq_ref[...], c, s, dq_ref[:, sl])
            dqo_ref[:, sl] = dx.astype(dqo_ref.dtype)
            dgq_ref[:, sl] += dg
        dx, dg = one(k_ref[...], gk_ref[...], c, s, dk_ref[...])
        dko_ref[...] = dx.astype(dko_ref.dtype)
        dgk_ref[...] += dg

    vec = pl.BlockSpec((1, 128), lambda i: (0, 0))
    tab = pl.BlockSpec((tr, 128), lambda i: (i, 0))
    qrow = pl.BlockSpec((tr, ATT_Q_DIM), lambda i: (i, 0))
    return pl.pallas_call(
        body, name=name, grid=(n // tr,),
        in_specs=[qrow, pl.BlockSpec((tr, 128), lambda i: (i, OFF_AK // 128)), vec, vec, tab, tab, qrow, tab],
        out_specs=[qrow, tab, pl.BlockSpec((1, ATT_Q_DIM), lambda i: (0, 0)), vec],
        out_shape=[SDS((n, ATT_Q_DIM), MXU_DTYPE), SDS((n, ATT_KV_DIM), MXU_DTYPE), SDS((1, ATT_Q_DIM), F32), SDS((1, 128), F32)],
        compiler_params=_cp("arbitrary"))(p, p, gq, gk, cos, sin, dq, dk)


ATT_TQ = 256


def _attn_fwd(q, k, v, name):
    n = q.shape[0]
    tq = min(ATT_TQ, n)
    scale = ATT_HEAD_DIM ** -0.5
    gw = ATT_GROUP * ATT_HEAD_DIM

    def body(q_ref, k_ref, v_ref, o_ref):
        kk, vv = k_ref[0], v_ref[0]
        outs = []
        for g in range(ATT_GROUP):
            s = _dot(q_ref[:, g * ATT_HEAD_DIM:(g + 1) * ATT_HEAD_DIM] * scale, kk, "nt")
            e = jnp.exp(s - jnp.max(s, axis=-1, keepdims=True))
            outs.append(_dot(e, vv) / jnp.sum(e, axis=-1, keepdims=True))
        o_ref[...] = jnp.concatenate(outs, axis=-1).astype(o_ref.dtype)

    kv = pl.BlockSpec((1, n, ATT_HEAD_DIM), lambda h, i: (h, 0, 0))
    return pl.pallas_call(
        body, name=name, grid=(ATT_KV_HEADS, n // tq),
        in_specs=[pl.BlockSpec((tq, gw), lambda h, i: (i, h)), kv, kv],
        out_specs=pl.BlockSpec((tq, gw), lambda h, i: (i, h)), out_shape=SDS((n, ATT_Q_DIM), MXU_DTYPE),
        compiler_params=_cp("parallel", "parallel"))(q, k, v)


def _attn_bwd(q, k, v, do, name):
    n = q.shape[0]
    tq = min(ATT_TQ, n)
    scale = ATT_HEAD_DIM ** -0.5
    gw = ATT_GROUP * ATT_HEAD_DIM

    def body(q_ref, k_ref, v_ref, do_ref, dq_ref, dk_ref, dv_ref):
        @pl.when(pl.program_id(1) == 0)
        def _():
            dk_ref[...] = jnp.zeros_like(dk_ref)
            dv_ref[...] = jnp.zeros_like(dv_ref)

        kk, vv = k_ref[0], v_ref[0]
        dqs = []
        dk_acc = jnp.zeros((ATT_HEAD_DIM, n), F32)
        dv_acc = jnp.zeros((ATT_HEAD_DIM, n), F32)
        for g in range(ATT_GROUP):
            sl = slice(g * ATT_HEAD_DIM, (g + 1) * ATT_HEAD_DIM)
            qg, dog = q_ref[:, sl] * scale, do_ref[:, sl].astype(F32)
            s = _dot(qg, kk, "nt")
            e = jnp.exp(s - jnp.max(s, axis=-1, keepdims=True))
            inv = 1.0 / jnp.sum(e, axis=-1, keepdims=True)
            delta = jnp.sum(dog * (_dot(e, vv) * inv), axis=-1, keepdims=True)
            dse = e * (_dot(dog, vv, "nt") - delta)
            dqs.append(_dot(dse, kk) * (inv * scale))
            dk_acc += _dot(qg.astype(F32) * inv, dse, "tn")
            dv_acc += _dot(dog * inv, e, "tn")
        dq_ref[...] = jnp.concatenate(dqs, axis=-1)
        dk_ref[0] += dk_acc
        dv_ref[0] += dv_acc

    kv = pl.BlockSpec((1, n, ATT_HEAD_DIM), lambda h, i: (h, 0, 0))
    kvt = pl.BlockSpec((1, ATT_HEAD_DIM, n), lambda h, i: (h, 0, 0))
    qb = pl.BlockSpec((tq, gw), lambda h, i: (i, h))
    return pl.pallas_call(
        body, name=name, grid=(ATT_KV_HEADS, n // tq), in_specs=[qb, kv, kv, qb], out_specs=[qb, kvt, kvt],
        out_shape=[SDS((n, ATT_Q_DIM), F32), SDS((ATT_KV_HEADS, ATT_HEAD_DIM, n), F32), SDS((ATT_KV_HEADS, ATT_HEAD_DIM, n), F32)],
        compiler_params=_cp("parallel", "arbitrary"))(q, k, v, do)


def _both_directions(mats, axis):
    fwd = np.concatenate(mats, axis=axis).astype(np.float32)
    bwd = np.concatenate([m[::-1, ::-1] for m in mats], axis=axis).astype(np.float32)
    return jnp.asarray(np.stack([fwd, bwd]), MXU_DTYPE)


def _hg_segments():
    c = HG_CHUNK
    t = np.arange(c)[:, None]
    r = np.arange(c)[None, :]
    mats = [(r <= t)]
    for lev in range(HG_LEVELS):
        h = c >> (lev + 1)
        mid = (t // (2 * h)) * (2 * h) + h - 1
        hi = (t // h) % 2 == 1
        mats.append(np.where(hi, (r > mid) & (r <= t), (r > t) & (r <= mid)))
    mats.append(r > t)
    return _both_directions(mats, 0)


def _hg_pair_sums():
    c = HG_CHUNK
    r = np.arange(c)[:, None]
    t = np.arange(c)[None, :]
    gp, gn = [t >= r], [t < r]
    for lev in range(HG_LEVELS):
        sh = HG_LEVELS - 1 - lev
        same = (r >> sh) == (t >> sh)
        gp.append(same & (t >= r))
        gn.append(same & (t < r))
    return _both_directions(gp, 1), _both_directions(gn, 1)


def _split_dot(mat, x):
    hi = x.astype(MXU_DTYPE)
    lo = (x - hi.astype(F32)).astype(MXU_DTYPE)
    return _dot(mat, hi) + _dot(mat, lo)


def _hg_gates(hq, z, a0, a1):
    q = hq * _sigmoid(hq)
    sg = _sigmoid(z)
    lb = _sigmoid(a0 - a1)
    f = lb + (1.0 - lb) * sg
    k = (1.0 - lb) * (1.0 - sg)
    return q, f, k, sg, lb


def _hg_level_masks(mirrored):
    c = HG_CHUNK
    row = lax.broadcasted_iota(jnp.int32, (c, 1), 0)
    rr = lax.broadcasted_iota(jnp.int32, (c, c), 0)
    cc = lax.broadcasted_iota(jnp.int32, (c, c), 1)
    his, sames = [], []
    for lev in range(HG_LEVELS):
        sh = HG_LEVELS - 1 - lev
        his.append(jnp.logical_xor(((row >> sh) & 1) == 1, mirrored))
        sames.append((rr >> (sh + 1)) == (cc >> (sh + 1)))
    return his, sames, rr == cc


def _hg_intra(q, k, ex, masks):
    his, sames, eye = masks
    a = jnp.where(eye, jnp.sum(q * k, axis=-1, keepdims=True), 0.0)
    for lev in range(HG_LEVELS):
        e = ex[lev + 1]
        qs = jnp.where(his[lev], q * e, 0.0)
        ks = jnp.where(his[lev], 0.0, k * e)
        a = a + jnp.where(sames[lev], _dot(qs, ks, "nt"), 0.0)
    return a


def _hg_specs(n, with_time):
    c = HG_CHUNK
    nc = n // c

    def chunk(d, i):
        first = d if with_time else 1 - d
        return i + first * (nc - 1 - 2 * i)

    def pcols(off, dir_stride=0):
        return [pl.BlockSpec((c, HG_PAIR), lambda d, i, j=j: (chunk(d, i), off // HG_PAIR + dir_stride // HG_PAIR * d + j)) for j in range(2)]

    specs = dict(
        hq=pcols(OFF_HQ), v=pcols(OFF_HI), z=pcols(OFF_ZF, OFF_ZB - OFF_ZF),
        shared=pl.BlockSpec((c, HG_DIM), lambda d, i: (chunk(d, i), 0)),
        per_dir=pl.BlockSpec((1, c, HG_DIM), lambda d, i: (d, chunk(d, i), 0)),
        vec=pl.BlockSpec((1, 1, HG_DIM), lambda d, i: (d, 0, 0)),
        seg=pl.BlockSpec((1, (HG_LEVELS + 2) * c, c), lambda d, i: (d, 0, 0)),
        sums=pl.BlockSpec((1, c, (HG_LEVELS + 1) * c), lambda d, i: (d, 0, 0)),
        state=pl.BlockSpec((1, HG_HEADS, 1, HG_HEAD_DIM, HG_HEAD_DIM), lambda d, i: (d, 0, chunk(d, i), 0, 0)))
    return nc, specs


def _hg_head(refs, hh):
    off = (hh % 2) * HG_HEAD_DIM
    return refs[hh // 2][:, off:off + HG_HEAD_DIM]


def _hg_lanes(hh):
    return slice(hh * HG_HEAD_DIM, (hh + 1) * HG_HEAD_DIM)


def _hg_exps(seg_ref, f):
    lf = jnp.log(f)
    args = _split_dot(seg_ref[0], lf)
    c = HG_CHUNK
    return [jnp.exp(args[j * c:(j + 1) * c]) for j in range(HG_LEVELS + 2)]


def _hg_last_row(a, mirrored):
    return jnp.where(mirrored, a[0:1, :], a[HG_CHUNK - 1:HG_CHUNK, :])


def _hgrn_fwd(p, a0, a1, seg, name):
    n = p.shape[0]
    nc, sp = _hg_specs(n, True)

    def body(hq0, hq1, z0, z1, v0, v1, a0_ref, a1_ref, seg_ref, o_ref, st):
        @pl.when(pl.program_id(1) == 0)
        def _():
            st[...] = jnp.zeros_like(st)

        mirrored = pl.program_id(0) == 1
        masks = _hg_level_masks(mirrored)
        for hh in range(HG_HEADS):
            ln = _hg_lanes(hh)
            q, f, k, _, _ = _hg_gates(_hg_head((hq0, hq1), hh), _hg_head((z0, z1), hh), a0_ref[0, :, ln], a1_ref[0, :, ln])
            vv = _hg_head((v0, v1), hh)
            ex = _hg_exps(seg_ref, f)
            a = _hg_intra(q, k, ex, masks)
            s_t = st[hh]
            o_ref[0, :, ln] = _dot(a, vv) + _dot(q * ex[0], s_t, "nt")
            st[hh] = s_t * _hg_last_row(ex[0], mirrored) + _dot(vv, k * ex[HG_LEVELS + 1], "tn")

    return pl.pallas_call(
        body, name=name, grid=(2, nc), in_specs=sp["hq"] + sp["z"] + sp["v"] + [sp["vec"], sp["vec"], sp["seg"]],
        out_specs=sp["per_dir"], out_shape=SDS((2, n, HG_DIM), F32),
        scratch_shapes=[pltpu.VMEM((HG_HEADS, HG_HEAD_DIM, HG_HEAD_DIM), F32)],
        compiler_params=_cp("parallel", "arbitrary"))(p, p, p, p, p, p, a0, a1, seg)


def _hgrn_bwd_q(p, a0, a1, seg, gp, do, name):
    n = p.shape[0]
    nc, sp = _hg_specs(n, True)


    def body(hq0, hq1, z0, z1, v0, v1, a0_ref, a1_ref, seg_ref, gp_ref, do_ref, dhq_ref, dlf_ref, s0_ref, st):
        @pl.when(pl.program_id(1) == 0)
        def _():
            st[...] = jnp.zeros_like(st)

        mirrored = pl.program_id(0) == 1
        his, sames, eye = _hg_level_masks(mirrored)
        for hh in range(HG_HEADS):
            ln = _hg_lanes(hh)
            s_t = st[hh]
            s0_ref[0, hh, 0] = s_t
            hqv = _hg_head((hq0, hq1), hh)
            q, f, k, _, _ = _hg_gates(hqv, _hg_head((z0, z1), hh), a0_ref[0, :, ln], a1_ref[0, :, ln])
            vv, dov = _hg_head((v0, v1), hh), do_ref[:, ln]
            ex = _hg_exps(seg_ref, f)
            da = _dot(dov, vv, "nt")
            dq_inter = ex[0] * _dot(dov, s_t)
            dq = jnp.sum(dov * vv, axis=-1, keepdims=True) * k + dq_inter
            terms = [q * dq_inter]
            for lev in range(HG_LEVELS):
                e = ex[lev + 1]
                ks = jnp.where(his[lev], 0.0, k * e)
                part = jnp.where(his[lev], e, 0.0) * _dot(jnp.where(sames[lev], da, 0.0), ks)
                dq = dq + part
                terms.append(q * part)
            dlf_ref[0, :, ln] = _dot(gp_ref[0], jnp.concatenate(terms, axis=0))
            sq = _sigmoid(hqv)
            dhq_ref[0, :, ln] = dq * sq * (1.0 + hqv * (1.0 - sq))
            st[hh] = s_t * _hg_last_row(ex[0], mirrored) + _dot(vv, k * ex[HG_LEVELS + 1], "tn")

    out = SDS((2, n, HG_DIM), F32)
    return pl.pallas_call(
        body, name=name, grid=(2, nc),
        in_specs=sp["hq"] + sp["z"] + sp["v"] + [sp["vec"], sp["vec"], sp["seg"], sp["sums"], sp["shared"]],
        out_specs=[sp["per_dir"], sp["per_dir"], sp["state"]],
        out_shape=[out, out, SDS((2, HG_HEADS, nc, HG_HEAD_DIM, HG_HEAD_DIM), F32)],
        scratch_shapes=[pltpu.VMEM((HG_HEADS, HG_HEAD_DIM, HG_HEAD_DIM), F32)],
        compiler_params=_cp("parallel", "arbitrary"))(p, p, p, p, p, p, a0, a1, seg, gp, do)


def _hgrn_bwd_kv(p, a0, a1, seg, gn, do, dlf_q, s0, name):
    n = p.shape[0]
    nc, sp = _hg_specs(n, False)

    def body(hq0, hq1, z0, z1, v0, v1, a0_ref, a1_ref, seg_ref, gn_ref, do_ref, dlfq_ref, s0_ref, dz_ref, dv_ref, dlb_ref, rt):
        @pl.when(pl.program_id(1) == 0)
        def _():
            rt[...] = jnp.zeros_like(rt)
            dlb_ref[...] = jnp.zeros_like(dlb_ref)

        mirrored = pl.program_id(0) == 1
        masks = _hg_level_masks(mirrored)
        his, sames, eye = masks
        for hh in range(HG_HEADS):
            ln = _hg_lanes(hh)
            q, f, k, sg, lb = _hg_gates(_hg_head((hq0, hq1), hh), _hg_head((z0, z1), hh), a0_ref[0, :, ln], a1_ref[0, :, ln])
            vv, dov = _hg_head((v0, v1), hh), do_ref[:, ln]
            ex = _hg_exps(seg_ref, f)
            a = _hg_intra(q, k, ex, masks)
            da = _dot(dov, vv, "nt")
            r_t = rt[hh]
            k_end = k * ex[HG_LEVELS + 1]
            dv_ref[0, :, ln] = _dot(a, dov, "tn") + _dot(k_end, r_t, "nt")
            dk_inter = ex[HG_LEVELS + 1] * _dot(vv, r_t)
            dk = jnp.sum(dov * vv, axis=-1, keepdims=True) * q + dk_inter
            terms = [k * dk_inter]
            for lev in range(HG_LEVELS):
                e = ex[lev + 1]
                qs = jnp.where(his[lev], q * e, 0.0)
                part = jnp.where(his[lev], 0.0, e) * _dot(jnp.where(sames[lev], da, 0.0), qs, "tn")
                dk = dk + part
                terms.append(k * part)
            decay = _hg_last_row(ex[0], mirrored)
            rt[hh] = r_t * decay + _dot(dov, q * ex[0], "tn")
            later = decay * jnp.sum(s0_ref[0, hh, 0] * r_t, axis=0, keepdims=True)
            dlf = dlfq_ref[0, :, ln] + _dot(gn_ref[0], jnp.concatenate(terms, axis=0)) + later
            df = dlf / f - dk
            dz_ref[0, :, ln] = df * (1.0 - lb) * sg * (1.0 - sg)
            dlb_ref[0, :, ln] += jnp.sum(df * (1.0 - sg), axis=0, keepdims=True)

    out = SDS((2, n, HG_DIM), F32)
    return pl.pallas_call(
        body, name=name, grid=(2, nc),
        in_specs=sp["hq"] + sp["z"] + sp["v"] + [sp["vec"], sp["vec"], sp["seg"], sp["sums"], sp["shared"], sp["per_dir"], sp["state"]],
        out_specs=[sp["per_dir"], sp["per_dir"], sp["vec"]], out_shape=[out, out, SDS((2, 1, HG_DIM), F32)],
        scratch_shapes=[pltpu.VMEM((HG_HEADS, HG_HEAD_DIM, HG_HEAD_DIM), F32)],
        compiler_params=_cp("parallel", "arbitrary"))(p, p, p, p, p, p, a0, a1, seg, gn, do, dlf_q, s0)


def _hg_post(o2, p, g, name):
    n = p.shape[0]
    tr = min(ROW_TILE, n)
    w = 2 * HG_HEAD_DIM

    def body(of_ref, ob_ref, hg_ref, g_ref, o_ref):
        for j in range(2):
            sl = slice(j * HG_HEAD_DIM, (j + 1) * HG_HEAD_DIM)
            o = of_ref[0, :, sl] + ob_ref[0, :, sl]
            hg = hg_ref[:, sl]
            o_ref[:, sl] = (o * _rstd(o) * g_ref[...] * (hg * _sigmoid(hg))).astype(o_ref.dtype)

    blk = pl.BlockSpec((tr, w), lambda i, j: (i, j))
    dirs = [pl.BlockSpec((1, tr, w), lambda i, j, d=d: (d, i, j)) for d in range(2)]
    return pl.pallas_call(
        body, name=name, grid=(n // tr, HG_DIM // w),
        in_specs=dirs + [pl.BlockSpec((tr, w), lambda i, j: (i, OFF_HG // w + j)), pl.BlockSpec((1, HG_HEAD_DIM), lambda i, j: (0, 0))],
        out_specs=blk, out_shape=SDS((n, HG_DIM), MXU_DTYPE), compiler_params=_cp("parallel", "parallel"))(o2, o2, p, g)


def _hg_post_bwd(o2, p, g, dcat, name):
    n = p.shape[0]
    tr = min(ROW_TILE, n)
    w = 2 * HG_HEAD_DIM

    def body(of_ref, ob_ref, hg_ref, g_ref, d_ref, do_ref, dhg_ref, dg_ref):
        @pl.when(pl.program_id(1) == 0)
        def _():
            dg_ref[...] = jnp.zeros_like(dg_ref)

        for j in range(2):
            sl = slice(j * HG_HEAD_DIM, (j + 1) * HG_HEAD_DIM)
            o = of_ref[0, :, sl] + ob_ref[0, :, sl]
            hg = hg_ref[:, sl]
            d = d_ref[:, sl].astype(F32)
            sg = _sigmoid(hg)
            on = o * _rstd(o) * g_ref[...]
            dhg_ref[:, sl] = (d * on * sg * (1.0 + hg * (1.0 - sg))).astype(dhg_ref.dtype)
            dx, dg = _rms_bwd(o, g_ref[...], d * hg * sg)
            do_ref[:, sl] = dx
            dg_ref[0, :, sl] += dg

    blk = pl.BlockSpec((tr, w), lambda j, i: (i, j))
    dirs = [pl.BlockSpec((1, tr, w), lambda j, i, d=d: (d, i, j)) for d in range(2)]
    return pl.pallas_call(
        body, name=name, grid=(HG_DIM // w, n // tr),
        in_specs=dirs + [pl.BlockSpec((tr, w), lambda j, i: (i, OFF_HG // w + j)), pl.BlockSpec((1, HG_HEAD_DIM), lambda j, i: (0, 0)),
                         pl.BlockSpec((tr, w), lambda j, i: (i, ATT_Q_DIM // w + j))],
        out_specs=[blk, blk, pl.BlockSpec((1, 1, w), lambda j, i: (j, 0, 0))],
        out_shape=[SDS((n, HG_DIM), F32), SDS((n, HG_DIM), MXU_DTYPE), SDS((HG_DIM // w, 1, w), F32)],
        compiler_params=_cp("parallel", "arbitrary"))(o2, o2, p, g, dcat)


XATT_TQ = 512


def _xattn_fwd(q, kv, name):
    n, nm = q.shape[0], kv.shape[0]
    tq = min(XATT_TQ, n)
    scale = X_HEAD_DIM ** -0.5

    def body(q_ref, k_ref, v_ref, o_ref):
        s = _dot(q_ref[...], k_ref[...], "nt") * scale
        e = jnp.exp(s - jnp.max(s, axis=-1, keepdims=True))
        o_ref[...] = _dot(e / jnp.sum(e, axis=-1, keepdims=True), v_ref[...]).astype(o_ref.dtype)

    qb = pl.BlockSpec((tq, X_HEAD_DIM), lambda h, i: (i, h))
    return pl.pallas_call(
        body, name=name, grid=(X_HEADS, n // tq),
        in_specs=[qb, pl.BlockSpec((nm, X_HEAD_DIM), lambda h, i: (0, h)), pl.BlockSpec((nm, X_HEAD_DIM), lambda h, i: (0, X_HEADS + h))],
        out_specs=qb, out_shape=SDS(q.shape, MXU_DTYPE), compiler_params=_cp("parallel", "parallel"))(q, kv, kv)


def _xattn_bwd(q, kv, do, name):
    n, nm = q.shape[0], kv.shape[0]
    tq = min(XATT_TQ, n)
    scale = X_HEAD_DIM ** -0.5

    def body(q_ref, k_ref, v_ref, do_ref, dq_ref, dk_ref, dv_ref):
        @pl.when(pl.program_id(1) == 0)
        def _():
            dk_ref[...] = jnp.zeros_like(dk_ref)
            dv_ref[...] = jnp.zeros_like(dv_ref)

        qv, dov = q_ref[...], do_ref[...]
        s = _dot(qv, k_ref[...], "nt") * scale
        e = jnp.exp(s - jnp.max(s, axis=-1, keepdims=True))
        p = e / jnp.sum(e, axis=-1, keepdims=True)
        dp = _dot(dov, v_ref[...], "nt")
        ds = p * (dp - jnp.sum(p * dp, axis=-1, keepdims=True)) * scale
        dq_ref[...] = _dot(ds, k_ref[...]).astype(dq_ref.dtype)
        dk_ref[...] += _dot(ds, qv, "tn")
        dv_ref[...] += _dot(p, dov, "tn")

    qb = pl.BlockSpec((tq, X_HEAD_DIM), lambda h, i: (i, h))
    kb = pl.BlockSpec((nm, X_HEAD_DIM), lambda h, i: (0, h))
    return pl.pallas_call(
        body, name=name, grid=(X_HEADS, n // tq),
        in_specs=[qb, kb, pl.BlockSpec((nm, X_HEAD_DIM), lambda h, i: (0, X_HEADS + h)), qb], out_specs=[qb, kb, kb],
        out_shape=[SDS(q.shape, MXU_DTYPE), SDS((nm, X_HEADS * X_HEAD_DIM), F32), SDS((nm, X_HEADS * X_HEAD_DIM), F32)],
        compiler_params=_cp("parallel", "arbitrary"))(q, kv, kv, do)


def _shift_rows(u, down):
    n = u.shape[0]
    row = lax.broadcasted_iota(jnp.int32, u.shape, 0)
    if down:
        return jnp.where(row == 0, 0.0, pltpu.roll(u, 1, axis=0))
    return jnp.where(row == n - 1, 0.0, pltpu.roll(u, n - 1, axis=0))


def _conv(u, w, b):
    return b + _shift_rows(u, True) * w[0:1, :] + u * w[1:2, :] + _shift_rows(u, False) * w[2:3, :]


def _ff_specs(n):
    gate = lambda rows: pl.BlockSpec((rows, FF_COLS), lambda j: (0, j))
    val = lambda rows: pl.BlockSpec((rows, FF_COLS), lambda j: (0, FF_BLOCKS + j))
    return [gate(n), val(n), gate(3), val(3), gate(1), val(1)], gate


def _conv_gate(u, cw, cb, name):
    n = u.shape[0]
    ins, gate_blk = _ff_specs(n)

    def body(ug_ref, uv_ref, wg_ref, wv_ref, bg_ref, bv_ref, o_ref):
        gate = _conv(ug_ref[...], wg_ref[...], bg_ref[...])
        val = _conv(uv_ref[...], wv_ref[...], bv_ref[...])
        o_ref[...] = (gate * _sigmoid(gate) * val).astype(o_ref.dtype)

    return pl.pallas_call(
        body, name=name, grid=(FF_BLOCKS,), in_specs=ins, out_specs=gate_blk(n), out_shape=SDS((n, D_FF), MXU_DTYPE),
        compiler_params=_cp("parallel"))(u, u, cw, cw, cb, cb)


def _conv_gate_bwd(u, cw, cb, da, name):
    n = u.shape[0]
    ins, gate_blk = _ff_specs(n)

    def side(dacc, u, w, du_ref, which, dw_ref, db_ref):
        nxt, prv = _shift_rows(dacc, False), _shift_rows(dacc, True)
        du_ref[which] = (nxt * w[0:1, :] + dacc * w[1:2, :] + prv * w[2:3, :]).astype(du_ref.dtype)
        db_ref[...] = jnp.sum(dacc, axis=0, keepdims=True)
        dw_ref[0:1, :] = jnp.sum(nxt * u, axis=0, keepdims=True)
        dw_ref[1:2, :] = jnp.sum(dacc * u, axis=0, keepdims=True)
        dw_ref[2:3, :] = jnp.sum(prv * u, axis=0, keepdims=True)

    def body(ug_ref, uv_ref, wg_ref, wv_ref, bg_ref, bv_ref, da_ref, du_ref, dwg_ref, dwv_ref, dbg_ref, dbv_ref):
        ug, uv = ug_ref[...], uv_ref[...]
        gate = _conv(ug, wg_ref[...], bg_ref[...])
        val = _conv(uv, wv_ref[...], bv_ref[...])
        sg = _sigmoid(gate)
        dav = da_ref[...].astype(F32)
        side(dav * val * sg * (1.0 + gate * (1.0 - sg)), ug, wg_ref[...], du_ref, 0, dwg_ref, dbg_ref)
        side(dav * gate * sg, uv, wv_ref[...], du_ref, 1, dwv_ref, dbv_ref)

    return pl.pallas_call(
        body, name=name, grid=(FF_BLOCKS,), in_specs=ins + [gate_blk(n)],
        out_specs=[pl.BlockSpec((2, n, FF_COLS), lambda j: (0, 0, j)), gate_blk(3), gate_blk(3), gate_blk(1), gate_blk(1)],
        out_shape=[SDS((2, n, D_FF), MXU_DTYPE)] + [SDS((3, D_FF), F32)] * 2 + [SDS((1, D_FF), F32)] * 2,
        compiler_params=_cp("parallel"))(u, u, cw, cw, cb, cb, da)


def _adamw(w, g, m, v, name):
    r, c = w.shape
    tr = r if r <= 512 else 256 if r % 256 == 0 else 88
    assert r % tr == 0, (name, r, tr)

    def body(w_ref, g_ref, m_ref, v_ref, d_ref, mo_ref, vo_ref):
        gv = g_ref[...]
        mn = ADAM_B1 * m_ref[...] + (1.0 - ADAM_B1) * gv
        vn = ADAM_B2 * v_ref[...] + (1.0 - ADAM_B2) * gv * gv
        m_hat = mn / (1.0 - ADAM_B1 ** ADAM_STEP)
        v_hat = vn / (1.0 - ADAM_B2 ** ADAM_STEP)
        d_ref[...] = -ADAM_LR * (m_hat / (jnp.sqrt(v_hat) + ADAM_EPS) + ADAM_WD * w_ref[...])
        mo_ref[...] = mn
        vo_ref[...] = vn

    blk = pl.BlockSpec((tr, c), lambda i: (i, 0))
    out = SDS((r, c), F32)
    return pl.pallas_call(body, name=name, grid=(r // tr,), in_specs=[blk] * 4, out_specs=[blk] * 3, out_shape=[out] * 3,
                          compiler_params=_cp("parallel"))(w, g, m, v)


def _half_tile(h):
    tr = h if h <= 512 else 256 if h % 256 == 0 else 176
    assert h % tr == 0, (h, tr)
    return tr


def _add_halves(g, r, core, name):
    s, h, c = r.shape
    tr = _half_tile(h)
    steps = h // tr

    def body(ix_ref, g_ref, r_ref, o_ref):
        o_ref[...] = (g_ref[...].astype(F32) + r_ref[...].astype(F32)).astype(o_ref.dtype)

    blk = pl.BlockSpec((1, tr, c), lambda i, j, ix: (i, j, 0))
    grid_spec = pltpu.PrefetchScalarGridSpec(
        num_scalar_prefetch=1, grid=(s, steps),
        in_specs=[pl.BlockSpec((1, tr, c), lambda i, j, ix: (i, ix[0] * steps + j, 0)), blk], out_specs=blk)
    return pl.pallas_call(body, name=name, grid_spec=grid_spec, out_shape=SDS(r.shape, WIRE_DTYPE),
                          compiler_params=_cp("parallel", "parallel"))(core.reshape(1), g, r)


def _sum_chips(own, recv, me, core, name):
    _, h, c = own.shape
    tr = _half_tile(h)

    def body(ix_ref, own_ref, recv_ref, o_ref):
        acc = own_ref[0].astype(F32)
        for j in range(3):
            acc = acc + recv_ref[j].astype(F32)
        o_ref[0] = acc

    grid_spec = pltpu.PrefetchScalarGridSpec(
        num_scalar_prefetch=1, grid=(h // tr,),
        in_specs=[pl.BlockSpec((1, tr, c), lambda i, ix: (ix[0], i, 0)), pl.BlockSpec((3, tr, c), lambda i, ix: (0, i, 0))],
        out_specs=pl.BlockSpec((1, tr, c), lambda i, ix: (ix[1], i, 0)))
    return pl.pallas_call(body, name=name, grid_spec=grid_spec, out_shape=SDS((2, h, c), F32),
                          compiler_params=_cp("parallel"))(jnp.stack([me, core]), own, recv)


ANY = pl.BlockSpec(memory_space=pl.ANY)


def _place():
    x, y, c = lax.axis_index("x"), lax.axis_index("y"), lax.axis_index("c")
    return x, y, c, [(1 - x, y), (x, 1 - y), (1 - x, 1 - y)]


def _gather_shards(shards, name):
    nt = len(shards)

    def body(*refs):
        ins, outs = refs[:nt], refs[nt:2 * nt]
        send, recv, fsend, frecv, osend, orecv = refs[2 * nt:]
        x, y, c, chips = _place()
        me = 2 * x + y

        def half(t, chip, cc):
            h = ins[t].shape[0] // 2
            return outs[t].at[chip, pl.ds(cc * h, h)]

        def ici(t, j):
            cx, cy = chips[j]
            h = ins[t].shape[0] // 2
            return pltpu.make_async_remote_copy(src_ref=ins[t].at[pl.ds(c * h, h)], dst_ref=half(t, me, c),
                                                send_sem=send.at[t, j], recv_sem=recv.at[t, j], device_id=(cx, cy, c), device_id_type=MESH)

        def landed(t, j):
            cx, cy = chips[j]
            blk = half(t, 2 * cx + cy, c)
            return pltpu.make_async_remote_copy(src_ref=blk, dst_ref=blk, send_sem=send.at[t, j], recv_sem=recv.at[t, j],
                                                device_id=(cx, cy, c), device_id_type=MESH)

        def d2d(t, j, cc):
            cx, cy = chips[j]
            blk = half(t, 2 * cx + cy, cc)
            return pltpu.make_async_remote_copy(src_ref=blk, dst_ref=blk, send_sem=fsend.at[t, j], recv_sem=frecv.at[t, j],
                                                device_id=(x, y, 1 - c), device_id_type=MESH)

        own = [pltpu.make_async_remote_copy(src_ref=ins[t], dst_ref=outs[t].at[me], send_sem=osend.at[t], recv_sem=orecv.at[t],
                                            device_id=(x, y, 1 - c), device_id_type=MESH) for t in range(nt)]
        for t in range(nt):
            for j in range(3):
                ici(t, j).start()
        for cp in own:
            cp.start()
        for t in range(nt):
            for j in range(3):
                landed(t, j).wait_recv()
                d2d(t, j, c).start()
        for t in range(nt):
            for j in range(3):
                d2d(t, j, 1 - c).wait_recv()
        for t in range(nt):
            for j in range(3):
                ici(t, j).wait_send()
                d2d(t, j, c).wait_send()
        for cp in own:
            cp.wait()

    return pl.pallas_call(
        body, name=name, in_specs=[ANY] * nt, out_specs=[ANY] * nt,
        out_shape=[SDS((4,) + s.shape, s.dtype) for s in shards],
        scratch_shapes=[pltpu.SemaphoreType.DMA((nt, 3))] * 4 + [pltpu.SemaphoreType.DMA((nt,))] * 2,
        compiler_params=pltpu.CompilerParams(has_side_effects=True))(*shards)


def _swap_halves(grads, name):
    nt = len(grads)

    def body(*refs):
        ins, outs = refs[:nt], refs[nt:2 * nt]
        send, recv = refs[2 * nt:]
        x, y, c, _ = _place()
        cps = []
        for t in range(nt):
            h = ins[t].shape[1] // 2
            cps.append(pltpu.make_async_remote_copy(src_ref=ins[t].at[pl.ds(0, 4), pl.ds((1 - c) * h, h)], dst_ref=outs[t], send_sem=send.at[t],
                                                    recv_sem=recv.at[t], device_id=(x, y, 1 - c), device_id_type=MESH))
        for cp in cps:
            cp.start()
        for cp in cps:
            cp.wait()

    return pl.pallas_call(
        body, name=name, in_specs=[ANY] * nt, out_specs=[ANY] * nt,
        out_shape=[SDS((g.shape[0], g.shape[1] // 2, g.shape[2]), g.dtype) for g in grads],
        scratch_shapes=[pltpu.SemaphoreType.DMA((nt,))] * 2,
        compiler_params=pltpu.CompilerParams(has_side_effects=True))(*grads)


def _send_to_owners(parts, name):
    nt = len(parts)

    def body(*refs):
        ins, outs = refs[:nt], refs[nt:2 * nt]
        send, recv = refs[2 * nt:]
        x, y, c, chips = _place()

        def ici(t, j):
            cx, cy = chips[j]
            return pltpu.make_async_remote_copy(src_ref=ins[t].at[2 * cx + cy], dst_ref=outs[t].at[j], send_sem=send.at[t, j],
                                                recv_sem=recv.at[t, j], device_id=(cx, cy, c), device_id_type=MESH)

        for t in range(nt):
            for j in range(3):
                ici(t, j).start()
        for t in range(nt):
            for j in range(3):
                ici(t, j).wait()

    return pl.pallas_call(
        body, name=name, in_specs=[ANY] * nt, out_specs=[ANY] * nt, out_shape=[SDS((3,) + p.shape[1:], p.dtype) for p in parts],
        scratch_shapes=[pltpu.SemaphoreType.DMA((nt, 3))] * 2,
        compiler_params=pltpu.CompilerParams(has_side_effects=True))(*parts)


def _join_halves(bufs, name):
    nt = len(bufs)

    def body(*refs):
        outs = refs[nt:2 * nt]
        send, recv = refs[2 * nt:]
        x, y, c, _ = _place()
        cps = [pltpu.make_async_remote_copy(src_ref=outs[t].at[c], dst_ref=outs[t].at[c], send_sem=send.at[t], recv_sem=recv.at[t],
                                            device_id=(x, y, 1 - c), device_id_type=MESH) for t in range(nt)]
        for cp in cps:
            cp.start()
        for t in range(nt):
            theirs = outs[t].at[1 - c]
            pltpu.make_async_remote_copy(src_ref=theirs, dst_ref=theirs, send_sem=send.at[t], recv_sem=recv.at[t],
                                         device_id=(x, y, 1 - c), device_id_type=MESH).wait_recv()
        for cp in cps:
            cp.wait_send()

    return pl.pallas_call(
        body, name=name, in_specs=[ANY] * nt, out_specs=[ANY] * nt, out_shape=[SDS(b.shape, b.dtype) for b in bufs],
        input_output_aliases={t: t for t in range(nt)},
        scratch_shapes=[pltpu.SemaphoreType.DMA((nt,))] * 2,
        compiler_params=pltpu.CompilerParams(has_side_effects=True))(*bufs)


def _exchange_small(v, reduce, name, after=()):
    rows = v.shape[0]
    after, after_specs = _unread(after)

    def body(v_ref, *rest):
        o_ref, buf, send, recv = rest[-4:]
        x, y, c, _ = _place()
        me = 4 * x + 2 * y + c
        buf[me] = v_ref[...]

        def peer(dx, dy, dc):
            return (1 - x if dx else x, 1 - y if dy else y, 1 - c if dc else c)

        peers = [(dx, dy, dc) for dx in range(2) for dy in range(2) for dc in range(2) if (dx, dy, dc) != (0, 0, 0)]
        cps = []
        for j, (dx, dy, dc) in enumerate(peers):
            cps.append(pltpu.make_async_remote_copy(src_ref=v_ref, dst_ref=buf.at[me], send_sem=send.at[j], recv_sem=recv.at[j],
                                                    device_id=peer(dx, dy, dc), device_id_type=MESH))
        for cp in cps:
            cp.start()
        for j, (dx, dy, dc) in enumerate(peers):
            px, py, pc = peer(dx, dy, dc)
            blk = buf.at[4 * px + 2 * py + pc]
            pltpu.make_async_remote_copy(src_ref=blk, dst_ref=blk, send_sem=send.at[j], recv_sem=recv.at[j],
                                         device_id=(px, py, pc), device_id_type=MESH).wait_recv()
        for cp in cps:
            cp.wait_send()
        if reduce:
            acc = buf[0]
            for j in range(1, 8):
                acc = acc + buf[j]
            o_ref[...] = acc
        else:
            o_ref[...] = buf[...]

    vm = pl.BlockSpec(memory_space=pltpu.VMEM)
    return pl.pallas_call(
        body, name=name, in_specs=[vm] + after_specs, out_specs=vm, out_shape=SDS((rows, 128) if reduce else (8, rows, 128), F32),
        scratch_shapes=[pltpu.VMEM((8, rows, 128), F32), pltpu.SemaphoreType.DMA((7,)), pltpu.SemaphoreType.DMA((7,))],
        compiler_params=pltpu.CompilerParams(has_side_effects=True))(v, *after)


HBM = pl.BlockSpec(memory_space=pltpu.HBM)
SEM = pl.BlockSpec(memory_space=pltpu.SEMAPHORE)
TOKEN = pl.BlockSpec(memory_space=pltpu.VMEM)
TOKEN_SHAPE = SDS((8, 128), F32)
PEERS = 7


def _in_hbm(a):
    return pltpu.with_memory_space_constraint(a, pltpu.HBM)


def _split_params():
    return pltpu.CompilerParams(has_side_effects=pltpu.SideEffectType.DATAFLOW_SIDE_EFFECTING)


def _gather_start(shards, name, after=()):
    nt = len(shards)
    after, after_specs = _unread(after)

    def body(*refs):
        ins, lands = refs[:nt], refs[nt:2 * nt]
        outs = refs[2 * nt + len(after):]
        sends, recvs = outs[:nt], outs[nt:2 * nt]
        x, y, c, chips = _place()
        me = 2 * x + y
        for t in range(nt):
            h = ins[t].shape[0] // 2
            mine = pl.ds(c * h, h)
            for j, (cx, cy) in enumerate(chips):
                for dc in range(2):
                    pltpu.make_async_remote_copy(src_ref=ins[t].at[mine], dst_ref=lands[t].at[me, mine], send_sem=sends[t].at[2 * j + dc],
                                                 recv_sem=recvs[t].at[2 * j + c], device_id=(cx, cy, dc), device_id_type=MESH).start()
            pltpu.make_async_remote_copy(src_ref=ins[t], dst_ref=lands[t].at[me], send_sem=sends[t].at[PEERS - 1], recv_sem=recvs[t].at[PEERS - 1],
                                         device_id=(x, y, 1 - c), device_id_type=MESH).start()
        outs[-1][...] = jnp.zeros(TOKEN_SHAPE.shape, F32)

    lands = [lax.empty((4,) + s.shape, s.dtype) for s in shards]
    out = pl.pallas_call(
        body, name=name, in_specs=[HBM] * (2 * nt) + after_specs, out_specs=[SEM] * (2 * nt) + [HBM] * (2 * nt) + [TOKEN],
        out_shape=[pltpu.SemaphoreType.DMA((PEERS,))] * (2 * nt)
        + [pltpu.HBM(s.shape, s.dtype) for s in shards] + [pltpu.HBM(l.shape, l.dtype) for l in lands] + [TOKEN_SHAPE],
        input_output_aliases={t: 2 * nt + t for t in range(2 * nt)}, compiler_params=_split_params())(
            *[_in_hbm(s) for s in shards], *[_in_hbm(l) for l in lands], *after)
    return out[:nt], out[nt:2 * nt], out[2 * nt:3 * nt], out[3 * nt:4 * nt], out[-1]


def _gather_wait(sends, recvs, shards, lands, after, name):
    nt = len(shards)

    def body(*refs):
        ins, lands_ref = refs[:nt], refs[nt:2 * nt]
        send_refs, recv_refs = refs[2 * nt:3 * nt], refs[3 * nt:4 * nt]
        x, y, c, chips = _place()
        for t in range(nt):
            h = ins[t].shape[0] // 2
            for j, (cx, cy) in enumerate(chips):
                for cs in range(2):
                    blk = lands_ref[t].at[2 * cx + cy, pl.ds(cs * h, h)]
                    pltpu.make_async_remote_copy(src_ref=blk, dst_ref=blk, send_sem=send_refs[t].at[2 * j + cs], recv_sem=recv_refs[t].at[2 * j + cs],
                                                 device_id=(cx, cy, cs), device_id_type=MESH).wait()
            blk = lands_ref[t].at[2 * x + y]
            pltpu.make_async_remote_copy(src_ref=blk, dst_ref=blk, send_sem=send_refs[t].at[PEERS - 1], recv_sem=recv_refs[t].at[PEERS - 1],
                                         device_id=(x, y, 1 - c), device_id_type=MESH).wait()

    out = pl.pallas_call(
        body, name=name, in_specs=[HBM] * (2 * nt) + [SEM] * (2 * nt) + [ANY], out_specs=[HBM] * (2 * nt),
        out_shape=[pltpu.HBM(s.shape, s.dtype) for s in shards] + [pltpu.HBM(l.shape, l.dtype) for l in lands],
        input_output_aliases={t: t for t in range(2 * nt)}, compiler_params=_split_params())(*shards, *lands, *sends, *recvs, after)
    return out[nt:]


def _scatter_start(g, name):
    _, r, c_ = g.shape
    h = r // 2

    def body(g_ref, land, send, recv, g_thru, land_thru, token):
        x, y, c, chips = _place()
        for j, (cx, cy) in enumerate(chips):
            for dc in range(2):
                pltpu.make_async_remote_copy(src_ref=g_ref.at[2 * cx + cy, pl.ds(dc * h, h)], dst_ref=land.at[2 * j + c], send_sem=send.at[2 * j + dc],
                                             recv_sem=recv.at[2 * j + c], device_id=(cx, cy, dc), device_id_type=MESH).start()
        pltpu.make_async_remote_copy(src_ref=g_ref.at[2 * x + y, pl.ds((1 - c) * h, h)], dst_ref=land.at[PEERS - 1], send_sem=send.at[PEERS - 1],
                                     recv_sem=recv.at[PEERS - 1], device_id=(x, y, 1 - c), device_id_type=MESH).start()
        token[...] = jnp.zeros(TOKEN_SHAPE.shape, F32)

    land = lax.empty((PEERS, h, c_), g.dtype)
    return pl.pallas_call(
        body, name=name, in_specs=[HBM, HBM], out_specs=[SEM, SEM, HBM, HBM, TOKEN],
        out_shape=[pltpu.SemaphoreType.DMA((PEERS,)), pltpu.SemaphoreType.DMA((PEERS,)), pltpu.HBM(g.shape, g.dtype),
                   pltpu.HBM(land.shape, land.dtype), TOKEN_SHAPE],
        input_output_aliases={0: 2, 1: 3}, compiler_params=_split_params())(_in_hbm(g), _in_hbm(land))


def _scatter_wait(started, after, name):
    nt = len(started)

    def body(*refs):
        lands = refs[nt:2 * nt]
        sends, recvs = refs[2 * nt:3 * nt], refs[3 * nt:4 * nt]
        x, y, c, chips = _place()
        peers = [(cx, cy, dc) for cx, cy in chips for dc in range(2)] + [(x, y, 1 - c)]
        for t in range(nt):
            for k, peer in enumerate(peers):
                blk = lands[t].at[k]
                pltpu.make_async_remote_copy(src_ref=blk, dst_ref=blk, send_sem=sends[t].at[k], recv_sem=recvs[t].at[k],
                                             device_id=peer, device_id_type=MESH).wait()

    gs, lands = [s[2] for s in started], [s[3] for s in started]
    after, after_specs = _unread(after)
    out = pl.pallas_call(
        body, name=name, in_specs=[HBM] * (2 * nt) + [SEM] * (2 * nt) + after_specs, out_specs=[HBM] * (2 * nt),
        out_shape=[pltpu.HBM(a.shape, a.dtype) for a in gs + lands],
        input_output_aliases={t: t for t in range(2 * nt)}, compiler_params=_split_params())(
            *gs, *lands, *[s[0] for s in started], *[s[1] for s in started], *after)
    return out[:nt], out[nt:]


def _sum_devices(g, land, me, core, name):
    npeer, h, c = land.shape
    tr = _half_tile(h)
    steps = h // tr

    def body(ix_ref, own_ref, land_ref, o_ref):
        acc = own_ref[0].astype(F32)
        for j in range(npeer):
            acc = acc + land_ref[j].astype(F32)
        o_ref[0] = acc

    grid_spec = pltpu.PrefetchScalarGridSpec(
        num_scalar_prefetch=1, grid=(steps,),
        in_specs=[pl.BlockSpec((1, tr, c), lambda i, ix: (ix[0], ix[1] * steps + i, 0)), pl.BlockSpec((npeer, tr, c), lambda i, ix: (0, i, 0))],
        out_specs=pl.BlockSpec((1, tr, c), lambda i, ix: (ix[1], i, 0)))
    return pl.pallas_call(body, name=name, grid_spec=grid_spec, out_shape=SDS((2, h, c), F32),
                          compiler_params=_cp("parallel"))(jnp.stack([me, core]), g, land)


def _pack_small(parts):
    flat = jnp.concatenate([p.reshape(-1) for p in parts])
    total = flat.shape[0]
    rows = -(-total // 1024) * 8
    return jnp.pad(flat, (0, rows * 128 - total)).reshape(rows, 128)


def _unpack_small(packed, shapes):
    flat = packed.reshape(-1)
    out, off = [], 0
    for s in shapes:
        size = int(np.prod(s))
        out.append(flat[off:off + size].reshape(s))
        off += size
    return out


def _local_step(x, mem, target, w_in, first_after, mid_weights, ffn_weights, on_grad, gains, conv_w, conv_b, hg_lb):
    n = x.shape[0]
    cos, sin = _rope_tables(n)
    seg = _hg_segments()
    gp, gn = _hg_pair_sums()
    gq2 = jnp.tile(gains["q_norm_g"], (1, 2))
    gk2 = jnp.tile(gains["k_norm_g"], (1, 2))
    a0 = hg_lb[:, 0:1, :]
    a1 = hg_lb[:, 1:2, :]

    p, h1 = _norm_mm(x, gains["pre_mix_g"], w_in, F32, 512, 1664, "in_proj", after=(first_after,))
    qr, kr = _qk_prep(p, gq2, gk2, cos, sin, "qk_prep")
    heads = lambda a: a.reshape(n, ATT_KV_HEADS, ATT_HEAD_DIM).transpose(1, 0, 2)
    kh = heads(kr)
    vh = heads(p[:, OFF_AV:OFF_AV + ATT_KV_DIM].astype(MXU_DTYPE))
    att = _attn_fwd(qr, kh, vh, "attn_fwd")
    o2 = _hgrn_fwd(p, a0, a1, seg, "hgrn_fwd")
    rec = _hg_post(o2, p, gains["hg_out_norm_g"], "hg_post")
    cat = jnp.concatenate([att, rec], axis=1)
    w_out, w_xq, w_xkv, w_xo = mid_weights(cat)
    mixed = _mm(cat, w_out, "nn", F32, 512, 1024, "out_proj")
    x1 = _resid_norm(x, mixed, gains["post_mix_g"], "mix_resid")
    xq, h2 = _norm_mm(x1, gains["pre_x_g"], w_xq, MXU_DTYPE, 512, 1024, "xq_proj")
    kv, mn = _norm_mm(mem, gains["mem_norm_g"], w_xkv, MXU_DTYPE, 256, 2048, "xkv_proj")
    ox = _xattn_fwd(xq, kv, "xattn_fwd")
    xo = _mm(ox, w_xo, "nn", F32, 512, 1024, "xo_proj")
    x2 = _resid_norm(x1, xo, gains["post_x_g"], "x_resid")
    w_up, w_down = ffn_weights(x2)
    u, h3 = _norm_mm(x2, gains["pre_ffn_g"], w_up, F32, 512, 1408, "up_proj")
    act = _conv_gate(u, conv_w, conv_b, "conv_gate")
    dn = _mm(act, w_down, "nn", F32, 512, 1024, "down_proj")
    d3, loss = _final_loss(x2, dn, gains["post_ffn_g"], target, "ffn_resid_loss")

    gs = {}
    d_dn, gs["post_ffn_g"] = _norm_bwd(dn, gains["post_ffn_g"], d3, None, MXU_DTYPE, "ffn_post_bwd")
    tok = on_grad("w_down", _mm(act, d_dn, "tn", WIRE_DTYPE, 1408, 1024, "down_dw"))
    d_act = _mm(d_dn, w_down, "nt", F32, 512, 1408, "down_dx", after=(tok,))
    du, dcw_g, dcw_v, dcb_g, dcb_v = _conv_gate_bwd(u, conv_w, conv_b, d_act, "conv_gate_bwd")
    gs["conv_w"] = jnp.concatenate([dcw_g, dcw_v], axis=1)
    gs["conv_b"] = jnp.concatenate([dcb_g, dcb_v], axis=1)
    tok = on_grad("w_up", _dw_by_owner(h3, du, 512, "up_dw"))
    d_h3 = _mm_nt_parts(du, w_up, F32, 512, 512, "up_dx", after=(tok,))
    d2, gs["pre_ffn_g"] = _norm_bwd(x2, gains["pre_ffn_g"], d_h3, d3, F32, "ffn_pre_bwd")
    d_xo, gs["post_x_g"] = _norm_bwd(xo, gains["post_x_g"], d2, None, MXU_DTYPE, "x_post_bwd")
    tok = on_grad("w_xo", _mm(ox, d_xo, "tn", WIRE_DTYPE, 512, 1024, "xo_dw"))
    d_ox = _mm(d_xo, w_xo, "nt", MXU_DTYPE, 512, 1024, "xo_dx", after=(tok,))
    d_xq, d_k, d_v = _xattn_bwd(xq, kv, d_ox, "xattn_bwd")
    d_kv = jnp.concatenate([d_k, d_v], axis=1).astype(MXU_DTYPE)
    tok = on_grad("w_xq", _mm(h2, d_xq, "tn", WIRE_DTYPE, 512, 1024, "xq_dw"))
    tok_kv = on_grad("w_xkv", _dw_by_owner(mn, d_kv, 512, "xkv_dw"))
    d_h2 = _mm(d_xq, w_xq, "nt", F32, 512, 1024, "xq_dx", after=(tok, tok_kv))
    d_mn = _mm_nt_parts(d_kv, w_xkv, F32, 256, 1024, "xkv_dx")
    _, gs["mem_norm_g"] = _norm_bwd(mem, gains["mem_norm_g"], d_mn, None, MXU_DTYPE, "mem_norm_bwd")
    d1, gs["pre_x_g"] = _norm_bwd(x1, gains["pre_x_g"], d_h2, d2, F32, "x_pre_bwd")
    d_mixed, gs["post_mix_g"] = _norm_bwd(mixed, gains["post_mix_g"], d1, None, MXU_DTYPE, "mix_post_bwd")
    tok = on_grad("w_out", _mm(cat, d_mixed, "tn", WIRE_DTYPE, 512, 1024, "out_dw"))
    d_cat = _mm(d_mixed, w_out, "nt", MXU_DTYPE, 512, 1024, "out_dx", after=(tok,))
    d_o, d_hg, dg_hg = _hg_post_bwd(o2, p, gains["hg_out_norm_g"], d_cat, "hg_post_bwd")
    gs["hg_out_norm_g"] = dg_hg.reshape(HG_HEADS, HG_HEAD_DIM).sum(axis=0, keepdims=True)
    dhq2, dlf_q, s0 = _hgrn_bwd_q(p, a0, a1, seg, gp, d_o, "hgrn_bwd_q")
    dz2, dhv2, dlb = _hgrn_bwd_kv(p, a0, a1, seg, gn, d_o, dlf_q, s0, "hgrn_bwd_kv")
    lb = jax.nn.sigmoid(a0 - a1)
    da0 = dlb * lb * (1.0 - lb)
    gs["hg_lb"] = jnp.concatenate([da0, -da0], axis=1)
    d_qr, d_kh, d_vh = _attn_bwd(qr, kh, vh, d_cat, "attn_bwd")
    unheads = lambda a: a.transpose(2, 0, 1).reshape(n, ATT_KV_DIM)
    d_aq, d_ak, dgq, dgk = _qk_prep_bwd(p, gq2, gk2, cos, sin, d_qr, unheads(d_kh), "qk_prep_bwd")
    gs["q_norm_g"] = dgq.reshape(ATT_HEADS, ATT_HEAD_DIM).sum(axis=0, keepdims=True)
    gs["k_norm_g"] = dgk.reshape(ATT_KV_HEADS, ATT_HEAD_DIM).sum(axis=0, keepdims=True)
    d_p = jnp.concatenate([d_aq, d_ak, unheads(d_vh).astype(MXU_DTYPE), (dhq2[0] + dhq2[1]).astype(MXU_DTYPE),
                           dz2[0].astype(MXU_DTYPE), dz2[1].astype(MXU_DTYPE), (dhv2[0] + dhv2[1]).astype(MXU_DTYPE), d_hg], axis=1)
    tok = on_grad("w_in", _mm(h1, d_p, "tn", WIRE_DTYPE, 512, 1664, "in_dw"))
    d_h1 = _mm(d_p, w_in, "nt", F32, 512, 1024, "in_dx", after=(tok,))
    grad_x, gs["pre_mix_g"] = _norm_bwd(x, gains["pre_mix_g"], d_h1, d1, F32, "mix_pre_bwd")
    return loss, grad_x, gs


MATS = ("w_in", "w_out", "w_xq", "w_xkv", "w_xo", "w_up", "w_down")
GAINS = ("pre_mix_g", "q_norm_g", "k_norm_g", "hg_out_norm_g", "post_mix_g", "pre_x_g", "mem_norm_g", "post_x_g", "pre_ffn_g", "post_ffn_g")
WEIGHTS = ('pre_mix_g', 'w_in', 'q_norm_g', 'k_norm_g', 'hg_lb', 'hg_out_norm_g', 'w_out', 'post_mix_g', 'pre_x_g', 'mem_norm_g', 'w_xq',
           'w_xkv', 'w_xo', 'post_x_g', 'pre_ffn_g', 'w_up', 'conv_w', 'conv_b', 'w_down', 'post_ffn_g')


def kernel(x, mem, pre_mix_g, w_in, q_norm_g, k_norm_g, hg_lb, hg_out_norm_g, w_out, post_mix_g, pre_x_g, mem_norm_g, w_xq, w_xkv, w_xo, post_x_g, pre_ffn_g, w_up, conv_w, conv_b, w_down, post_ffn_g, loss_target, m_pre_mix_g, m_w_in, m_q_norm_g, m_k_norm_g, m_hg_lb, m_hg_out_norm_g, m_w_out, m_post_mix_g, m_pre_x_g, m_mem_norm_g, m_w_xq, m_w_xkv, m_w_xo, m_post_x_g, m_pre_ffn_g, m_w_up, m_conv_w, m_conv_b, m_w_down, m_post_ffn_g, v_pre_mix_g, v_w_in, v_q_norm_g, v_k_norm_g, v_hg_lb, v_hg_out_norm_g, v_w_out, v_post_mix_g, v_pre_x_g, v_mem_norm_g, v_w_xq, v_w_xkv, v_w_xo, v_post_x_g, v_pre_ffn_g, v_w_up, v_conv_w, v_conv_b, v_w_down, v_post_ffn_g):
    args = dict(locals())
    w = {k: args[k] for k in WEIGHTS}
    m = {k: args["m_" + k] for k in WEIGHTS}
    v = {k: args["v_" + k] for k in WEIGHTS}
    chip = 2 * lax.axis_index("x") + lax.axis_index("y")
    core = lax.axis_index("c")

    shards = {k: w[k][0].astype(WIRE_DTYPE) for k in MATS}

    def whole(k, g):
        return g if k in ("w_xkv", "w_up") else g.reshape(-1, g.shape[-1])

    w_in_shards = _gather_shards([shards["w_in"]], "gather_w_in")[0]
    w_in_full = jnp.concatenate([w_in_shards[s] for s in range(4)], axis=1)
    mid_names, ffn_names = ("w_out", "w_xq", "w_xkv", "w_xo"), ("w_up", "w_down")
    mid = _gather_start([shards[k] for k in mid_names], "gather_mid_start", after=(w_in_full,))
    ffn = _gather_start([shards[k] for k in ffn_names], "gather_ffn_start", after=(mid[4],))

    def mid_weights(after):
        return [whole(k, g) for k, g in zip(mid_names, _gather_wait(*mid[:4], after, "gather_mid_wait"))]

    def ffn_weights(after):
        return [whole(k, g) for k, g in zip(ffn_names, _gather_wait(*ffn[:4], after, "gather_ffn_wait"))]

    small_in = _exchange_small(_pack_small([w["conv_w"][0], w["hg_lb"]]), False, "gather_small")
    cw_parts, lb_parts = [], []
    for s in range(4):
        cw_s, lb_s = _unpack_small(small_in[2 * s], [w["conv_w"][0].shape, w["hg_lb"].shape])
        cw_parts.append(cw_s)
        lb_parts.append(lb_s)
    conv_w_full = jnp.concatenate(cw_parts, axis=1)
    hg_lb_full = jnp.concatenate(lb_parts, axis=2)

    started = {}

    def on_grad(k, g):
        if k == "w_in":
            g = g.reshape(g.shape[0], 4, g.shape[1] // 4).transpose(1, 0, 2)
        elif g.ndim == 2:
            g = g.reshape(4, g.shape[0] // 4, g.shape[1])
        *started[k], token = _scatter_start(g, "grad_start_" + k)
        return token

    gains = {k: w[k] for k in GAINS}
    loss_part, grad_x, gs = _local_step(x[0], mem[0], loss_target[0], w_in_full, ffn[4], mid_weights, ffn_weights, on_grad, gains,
                                        conv_w_full, w["conv_b"], hg_lb_full)
    gs["loss"] = loss_part

    grads, delta, new_m, new_v = {}, {}, {}, {}

    def reduce_matrices(names, after, tag):
        sent, landed = _scatter_wait([started[k] for k in names], after, "grad_wait_" + tag)
        halves = [_sum_devices(g, land, chip, core, "grad_sum_" + k) for k, g, land in zip(names, sent, landed)]
        for k, r in zip(names, _join_halves(halves, "grad_join_" + tag)):
            grads[k] = r.reshape(1, -1, r.shape[-1])

    def adamw(names):
        for k in names:
            shape = w[k].shape
            two_d = lambda a: a.reshape(-1, shape[-1])
            d, mo, vo = _adamw(two_d(w[k]), two_d(grads[k]), two_d(m[k]), two_d(v[k]), "adamw_" + k)
            delta[k], new_m[k], new_v[k] = d.reshape(shape), mo.reshape(shape), vo.reshape(shape)
            grads[k] = grads[k].reshape(shape)

    early = tuple(k for k in MATS if k != "w_in")
    reduce_matrices(early, (grad_x,), "early")
    adamw(early)

    small_names = GAINS + ("conv_b", "conv_w", "hg_lb")
    packed = _pack_small([gs[k] for k in small_names + ("loss",)])
    reduced_small = _exchange_small(packed, True, "reduce_small", after=tuple(new_v[k] for k in early))
    *summed, loss = _unpack_small(reduced_small, [gs[k].shape for k in small_names + ("loss",)])
    loss = loss[0, 0]
    for k, g in zip(small_names, summed):
        grads[k] = g
    ncw = w["conv_w"].shape[2]
    grads["conv_w"] = lax.dynamic_slice_in_dim(grads["conv_w"], chip * ncw, ncw, axis=1)[None]
    nlb = w["hg_lb"].shape[2]
    grads["hg_lb"] = lax.dynamic_slice_in_dim(grads["hg_lb"], chip * nlb, nlb, axis=2)
    adamw(small_names)

    reduce_matrices(("w_in",), tuple(new_v[k] for k in early + small_names), "late")
    adamw(("w_in",))
    return (loss, grad_x[None], *[grads[k] for k in WEIGHTS], *[delta[k] for k in WEIGHTS],
            *[new_m[k] for k in WEIGHTS], *[new_v[k] for k in WEIGHTS])
```

```python
import functools

import numpy as np
import jax
import jax.numpy as jnp
from jax import lax
from jax.experimental import pallas as pl
from jax.experimental.pallas import tpu as pltpu

F32 = jnp.float32
MXU_DTYPE = jnp.bfloat16
WIRE_DTYPE = jnp.bfloat16
VMEM_LIMIT_BYTES = 56 * 1024 * 1024
EPS = 1e-6
MESH = pl.DeviceIdType.MESH

D_MODEL = 1024
GRID_W = 64
ATT_HEADS, ATT_KV_HEADS, ATT_HEAD_DIM = 8, 2, 64
ATT_GROUP = ATT_HEADS // ATT_KV_HEADS
ATT_Q_DIM, ATT_KV_DIM = 512, 128
ROPE_THETA = 10000.0
HG_HEADS, HG_HEAD_DIM, HG_DIM = 4, 128, 512
HG_CHUNK = 128
HG_LEVELS = 7
HG_PAIR = 2 * HG_HEAD_DIM
X_HEADS, X_HEAD_DIM = 4, 256
D_FF = 2816
FF_COLS = 256
FF_BLOCKS = D_FF // FF_COLS
N_IN = 3328
OFF_AQ, OFF_AK, OFF_AV, OFF_HQ, OFF_ZF, OFF_ZB, OFF_HI, OFF_HG = 0, 512, 640, 768, 1280, 1792, 2304, 2816

ADAM_LR, ADAM_B1, ADAM_B2, ADAM_EPS, ADAM_WD, ADAM_STEP = 0.001, 0.9, 0.999, 1e-08, 0.01, 10

SDS = jax.ShapeDtypeStruct


def _cp(*sem):
    return pltpu.CompilerParams(dimension_semantics=sem, vmem_limit_bytes=VMEM_LIMIT_BYTES)


def _dot(a, b, form="nn"):
    dims = {"nn": (((1,), (0,)), ((), ())), "nt": (((1,), (1,)), ((), ())), "tn": (((0,), (0,)), ((), ()))}[form]
    return lax.dot_general(a.astype(MXU_DTYPE), b.astype(MXU_DTYPE), dims, preferred_element_type=F32)


def _sigmoid(x):
    return 1.0 / (1.0 + jnp.exp(-x))


def _rstd(x):
    return lax.rsqrt(jnp.mean(x * x, axis=-1, keepdims=True) + EPS)


def _rms_bwd(x, g, dy):
    r = _rstd(x)
    xh = x * r
    dn = dy * g
    dx = r * (dn - xh * jnp.mean(dn * xh, axis=-1, keepdims=True))
    return dx, jnp.sum(dy * xh, axis=0, keepdims=True)


def _unread(after):
    after = tuple(a for a in after if a is not None)
    return after, [pl.BlockSpec(memory_space=pl.ANY)] * len(after)


def _mm(a, b, form, out_dtype, tm, tn, name, after=()):
    after, after_specs = _unread(after)
    if form == "nn":
        (m, k), n = a.shape, b.shape[1]
    elif form == "nt":
        (m, k), n = a.shape, b.shape[0]
    else:
        (k, m), n = a.shape, b.shape[1]
    tm, tn = min(tm, m), min(tn, n)
    assert m % tm == 0 and n % tn == 0, (name, m, n, tm, tn)

    def body(a_ref, b_ref, *rest):
        o_ref = rest[-1]
        o_ref[...] = _dot(a_ref[...], b_ref[...], form).astype(o_ref.dtype)

    a_spec = pl.BlockSpec((k, tm), lambda i, j: (0, i)) if form == "tn" else pl.BlockSpec((tm, k), lambda i, j: (i, 0))
    b_spec = pl.BlockSpec((tn, k), lambda i, j: (j, 0)) if form == "nt" else pl.BlockSpec((k, tn), lambda i, j: (0, j))
    return pl.pallas_call(
        body, name=name, grid=(m // tm, n // tn), in_specs=[a_spec, b_spec] + after_specs,
        out_specs=pl.BlockSpec((tm, tn), lambda i, j: (i, j)), out_shape=SDS((m, n), out_dtype),
        compiler_params=_cp("parallel", "parallel"))(a, b, *after)


def _mm_nt_parts(a_parts, b, out_dtype, tm, tn, name, after=()):
    after, after_specs = _unread(after)
    parts, n, p = b.shape
    m = a_parts[0][0].shape[0]
    tm, tn = min(tm, m), min(tn, n)
    assert m % tm == 0 and n % tn == 0 and len(a_parts) == parts, (name, m, b.shape)

    def body(*refs):
        o_ref = refs[-1]
        acc = _dot(refs[0][...], refs[parts][0], "nt")
        for s in range(1, parts):
            acc = acc + _dot(refs[s][...], refs[parts + s][0], "nt")
        o_ref[...] = acc.astype(o_ref.dtype)

    a_specs = [pl.BlockSpec((tm, p), lambda i, j, cb=cb: (i, cb)) for _, cb in a_parts]
    b_specs = [pl.BlockSpec((1, tn, p), lambda i, j, s=s: (s, j, 0)) for s in range(parts)]
    return pl.pallas_call(
        body, name=name, grid=(m // tm, n // tn), in_specs=a_specs + b_specs + after_specs,
        out_specs=pl.BlockSpec((tm, tn), lambda i, j: (i, j)), out_shape=SDS((m, n), out_dtype),
        compiler_params=_cp("parallel", "parallel"))(*[arr for arr, _ in a_parts], *([b] * parts), *after)


def _dw_by_owner(a, b, tn, first, into, tm, name):
    k, m = a.shape
    cnt = b.shape[1] // tn
    tm = min(tm, m)
    assert m % tm == 0 and b.shape[1] == cnt * tn and first + cnt <= 4, (name, a.shape, b.shape)

    def body(a_ref, b_ref, *rest):
        rest[-1][0] = _dot(a_ref[...], b_ref[...], "tn").astype(rest[-1].dtype)

    extra = [] if into is None else [into]
    return pl.pallas_call(
        body, name=name, grid=(m // tm, cnt),
        in_specs=[pl.BlockSpec((k, tm), lambda i, j: (0, i)), pl.BlockSpec((k, tn), lambda i, j: (0, j))] + [pl.BlockSpec(memory_space=pl.ANY)] * len(extra),
        out_specs=pl.BlockSpec((1, tm, tn), lambda i, j: (first + j, i, 0)), out_shape=SDS((4, m, tn), WIRE_DTYPE),
        input_output_aliases={2: 0} if extra else {},
        compiler_params=_cp("parallel", "parallel"))(a, b, *extra)


def _norm_mm(x, g, w, out_dtype, tm, tn, name, after=()):
    after, after_specs = _unread(after)
    m, d = x.shape
    sharded = w.ndim == 3
    n = w.shape[-1] * (w.shape[0] if sharded else 1)
    tm, tn = min(tm, m), (w.shape[-1] if sharded else min(tn, n))
    assert m % tm == 0 and n % tn == 0, (name, m, n, tm, tn)

    def body(x_ref, g_ref, w_ref, *rest):
        o_ref, h_ref, hs = rest[-3:]

        @pl.when(pl.program_id(1) == 0)
        def _():
            xv = x_ref[...]
            h = (xv * _rstd(xv) * g_ref[...]).astype(MXU_DTYPE)
            hs[...] = h
            h_ref[...] = h

        o_ref[...] = _dot(hs[...], w_ref[0] if sharded else w_ref[...]).astype(o_ref.dtype)

    w_spec = pl.BlockSpec((1, d, tn), lambda i, j: (j, 0, 0)) if sharded else pl.BlockSpec((d, tn), lambda i, j: (0, j))
    return pl.pallas_call(
        body, name=name, grid=(m // tm, n // tn),
        in_specs=[pl.BlockSpec((tm, d), lambda i, j: (i, 0)), pl.BlockSpec((1, d), lambda i, j: (0, 0)), w_spec] + after_specs,
        out_specs=[pl.BlockSpec((tm, tn), lambda i, j: (i, j)), pl.BlockSpec((tm, d), lambda i, j: (i, 0))],
        out_shape=[SDS((m, n), out_dtype), SDS((m, d), MXU_DTYPE)],
        scratch_shapes=[pltpu.VMEM((tm, d), MXU_DTYPE)],
        compiler_params=_cp("parallel", "arbitrary"))(x, g, w, *after)


ROW_TILE = 256


def _resid_norm(x, y, g, name):
    n, d = x.shape
    tr = min(ROW_TILE, n)

    def body(x_ref, y_ref, g_ref, o_ref):
        yv = y_ref[...]
        o_ref[...] = x_ref[...] + yv * _rstd(yv) * g_ref[...]

    row = pl.BlockSpec((tr, d), lambda i: (i, 0))
    return pl.pallas_call(
        body, name=name, grid=(n // tr,), in_specs=[row, row, pl.BlockSpec((1, d), lambda i: (0, 0))],
        out_specs=row, out_shape=SDS((n, d), F32), compiler_params=_cp("parallel"))(x, y, g)


def _norm_bwd(x, g, dy, res, out_dtype, name):
    n, d = x.shape
    tr = min(ROW_TILE, n)
    has_res = res is not None

    def body(*refs):
        x_ref, g_ref, dy_ref = refs[:3]
        dx_ref, dg_ref = refs[-2:]
        dx, dg = _rms_bwd(x_ref[...], g_ref[...], dy_ref[...].astype(F32))
        if has_res:
            dx = dx + refs[3][...]
        dx_ref[...] = dx.astype(dx_ref.dtype)

        @pl.when(pl.program_id(0) == 0)
        def _():
            dg_ref[...] = jnp.zeros_like(dg_ref)

        dg_ref[...] += dg

    row = pl.BlockSpec((tr, d), lambda i: (i, 0))
    vec = pl.BlockSpec((1, d), lambda i: (0, 0))
    ins = [x, g, dy] + ([res] if has_res else [])
    return pl.pallas_call(
        body, name=name, grid=(n // tr,), in_specs=[row, vec, row] + ([row] if has_res else []),
        out_specs=[row, vec], out_shape=[SDS((n, d), out_dtype), SDS((1, d), F32)],
        compiler_params=_cp("arbitrary"))(*ins)


def _final_loss(x, y, g, target, name):
    n, d = x.shape
    tr = min(ROW_TILE, n)

    def body(x_ref, y_ref, g_ref, t_ref, d_ref, l_ref):
        yv = y_ref[...]
        diff = x_ref[...] + yv * _rstd(yv) * g_ref[...] - t_ref[...]
        d_ref[...] = diff * (1.0 / d)

        @pl.when(pl.program_id(0) == 0)
        def _():
            l_ref[...] = jnp.zeros_like(l_ref)

        l_ref[...] += 0.5 * jnp.sum(jnp.mean(diff * diff, axis=-1, keepdims=True), axis=0, keepdims=True)

    row = pl.BlockSpec((tr, d), lambda i: (i, 0))
    return pl.pallas_call(
        body, name=name, grid=(n // tr,), in_specs=[row, row, pl.BlockSpec((1, d), lambda i: (0, 0)), row],
        out_specs=[row, pl.BlockSpec((1, 1), lambda i: (0, 0))], out_shape=[SDS((n, d), F32), SDS((1, 1), F32)],
        compiler_params=_cp("arbitrary"))(x, y, g, target)


def _rope_tables(n):
    pairs = ATT_HEAD_DIM // 4
    t = np.arange(n)
    inv = np.power(ROPE_THETA, -np.arange(pairs, dtype=np.float32) / pairs).astype(np.float32)
    ang = np.concatenate([(t // GRID_W)[:, None].astype(np.float32) * inv, (t % GRID_W)[:, None].astype(np.float32) * inv], axis=-1)
    cos = np.repeat(np.cos(ang), 2, axis=-1)
    sin = np.repeat(np.sin(ang), 2, axis=-1) * np.tile(np.array([-1.0, 1.0], np.float32), ATT_HEAD_DIM // 2)
    return jnp.asarray(np.tile(cos, 2), F32), jnp.asarray(np.tile(sin, 2), F32)


def _swap_pairs(x):
    lane = lax.broadcasted_iota(jnp.int32, x.shape, 1)
    return jnp.where((lane & 1) == 0, pltpu.roll(x, 127, axis=1), pltpu.roll(x, 1, axis=1))


def _head_mean(v):
    lane = lax.broadcasted_iota(jnp.int32, v.shape, 1)
    lo = jnp.where(lane < ATT_HEAD_DIM, v, 0.0)
    s0 = jnp.sum(lo, axis=-1, keepdims=True)
    s1 = jnp.sum(v - lo, axis=-1, keepdims=True)
    return jnp.where(lane < ATT_HEAD_DIM, s0, s1) * (1.0 / ATT_HEAD_DIM)


def _qk_prep(p, gq, gk, cos, sin, name):
    n = p.shape[0]
    tr = min(ROW_TILE, n)

    def one(xv, g, c, s):
        xn = xv * lax.rsqrt(_head_mean(xv * xv) + EPS) * g
        return xn * c + _swap_pairs(xn) * s

    def body(q_ref, k_ref, gq_ref, gk_ref, c_ref, s_ref, qo_ref, ko_ref):
        c, s = c_ref[...], s_ref[...]
        for j in range(ATT_Q_DIM // 128):
            qo_ref[:, j * 128:(j + 1) * 128] = one(q_ref[:, j * 128:(j + 1) * 128], gq_ref[...], c, s).astype(qo_ref.dtype)
        ko_ref[...] = one(k_ref[...], gk_ref[...], c, s).astype(ko_ref.dtype)

    vec = pl.BlockSpec((1, 128), lambda i: (0, 0))
    tab = pl.BlockSpec((tr, 128), lambda i: (i, 0))
    return pl.pallas_call(
        body, name=name, grid=(n // tr,),
        in_specs=[pl.BlockSpec((tr, ATT_Q_DIM), lambda i: (i, 0)), pl.BlockSpec((tr, 128), lambda i: (i, OFF_AK // 128)), vec, vec, tab, tab],
        out_specs=[pl.BlockSpec((tr, ATT_Q_DIM), lambda i: (i, 0)), tab],
        out_shape=[SDS((n, ATT_Q_DIM), MXU_DTYPE), SDS((n, ATT_KV_DIM), MXU_DTYPE)],
        compiler_params=_cp("parallel"))(p, p, gq, gk, cos, sin)


def _qk_prep_bwd(p, gq, gk, cos, sin, dq, dk, name):
    n = p.shape[0]
    tr = min(ROW_TILE, n)

    def one(xv, g, c, s, dout):
        dxn = dout * c + _swap_pairs(dout * s)
        r = lax.rsqrt(_head_mean(xv * xv) + EPS)
        xh = xv * r
        dn = dxn * g
        dx = r * (dn - xh * _head_mean(dn * xh))
        return dx, jnp.sum(dxn * xh, axis=0, keepdims=True)

    def body(q_ref, k_ref, gq_ref, gk_ref, c_ref, s_ref, dq_ref, dk_ref, dqo_ref, dko_ref, dgq_ref, dgk_ref):
        @pl.when(pl.program_id(0) == 0)
        def _():
            dgq_ref[...] = jnp.zeros_like(dgq_ref)
            dgk_ref[...] = jnp.zeros_like(dgk_ref)

        c, s = c_ref[...], s_ref[...]
        for j in range(ATT_Q_DIM // 128):
            sl = slice(j * 128, (j + 1) * 128)
            dx, dg = one(q_ref[:, sl], gq_ref[...], c, s, dq_ref[:, sl])
            dqo_ref[:, sl] = dx.astype(dqo_ref.dtype)
            dgq_ref[:, sl] += dg
        dx, dg = one(k_ref[...], gk_ref[...], c, s, dk_ref[...])
        dko_ref[...] = dx.astype(dko_ref.dtype)
        dgk_ref[...] += dg

    vec = pl.BlockSpec((1, 128), lambda i: (0, 0))
    tab = pl.BlockSpec((tr, 128), lambda i: (i, 0))
    qrow = pl.BlockSpec((tr, ATT_Q_DIM), lambda i: (i, 0))
    return pl.pallas_call(
        body, name=name, grid=(n // tr,),
        in_specs=[qrow, pl.BlockSpec((tr, 128), lambda i: (i, OFF_AK // 128)), vec, vec, tab, tab, qrow, tab],
        out_specs=[qrow, tab, pl.BlockSpec((1, ATT_Q_DIM), lambda i: (0, 0)), vec],
        out_shape=[SDS((n, ATT_Q_DIM), MXU_DTYPE), SDS((n, ATT_KV_DIM), MXU_DTYPE), SDS((1, ATT_Q_DIM), F32), SDS((1, 128), F32)],
        compiler_params=_cp("arbitrary"))(p, p, gq, gk, cos, sin, dq, dk)


ATT_TQ = 256


def _attn_fwd(q, k, v, name):
    n = q.shape[0]
    tq = min(ATT_TQ, n)
    scale = ATT_HEAD_DIM ** -0.5
    gw = ATT_GROUP * ATT_HEAD_DIM

    def body(q_ref, k_ref, v_ref, o_ref):
        kk, vv = k_ref[0], v_ref[0]
        outs = []
        for g in range(ATT_GROUP):
            s = _dot(q_ref[:, g * ATT_HEAD_DIM:(g + 1) * ATT_HEAD_DIM] * scale, kk, "nt")
            e = jnp.exp(s - jnp.max(s, axis=-1, keepdims=True))
            outs.append(_dot(e, vv) / jnp.sum(e, axis=-1, keepdims=True))
        o_ref[...] = jnp.concatenate(outs, axis=-1).astype(o_ref.dtype)

    kv = pl.BlockSpec((1, n, ATT_HEAD_DIM), lambda h, i: (h, 0, 0))
    return pl.pallas_call(
        body, name=name, grid=(ATT_KV_HEADS, n // tq),
        in_specs=[pl.BlockSpec((tq, gw), lambda h, i: (i, h)), kv, kv],
        out_specs=pl.BlockSpec((tq, gw), lambda h, i: (i, h)), out_shape=SDS((n, ATT_Q_DIM), MXU_DTYPE),
        compiler_params=_cp("parallel", "parallel"))(q, k, v)


def _attn_bwd(q, k, v, do, name):
    n = q.shape[0]
    tq = min(ATT_TQ, n)
    scale = ATT_HEAD_DIM ** -0.5
    gw = ATT_GROUP * ATT_HEAD_DIM

    def body(q_ref, k_ref, v_ref, do_ref, dq_ref, dk_ref, dv_ref):
        @pl.when(pl.program_id(1) == 0)
        def _():
            dk_ref[...] = jnp.zeros_like(dk_ref)
            dv_ref[...] = jnp.zeros_like(dv_ref)

        kk, vv = k_ref[0], v_ref[0]
        dqs = []
        dk_acc = jnp.zeros((ATT_HEAD_DIM, n), F32)
        dv_acc = jnp.zeros((ATT_HEAD_DIM, n), F32)
        for g in range(ATT_GROUP):
            sl = slice(g * ATT_HEAD_DIM, (g + 1) * ATT_HEAD_DIM)
            qg, dog = q_ref[:, sl] * scale, do_ref[:, sl].astype(F32)
            s = _dot(qg, kk, "nt")
            e = jnp.exp(s - jnp.max(s, axis=-1, keepdims=True))
            inv = 1.0 / jnp.sum(e, axis=-1, keepdims=True)
            delta = jnp.sum(dog * (_dot(e, vv) * inv), axis=-1, keepdims=True)
            dse = e * (_dot(dog, vv, "nt") - delta)
            dqs.append(_dot(dse, kk) * (inv * scale))
            dk_acc += _dot(qg.astype(F32) * inv, dse, "tn")
            dv_acc += _dot(dog * inv, e, "tn")
        dq_ref[...] = jnp.concatenate(dqs, axis=-1)
        dk_ref[0] += dk_acc
        dv_ref[0] += dv_acc

    kv = pl.BlockSpec((1, n, ATT_HEAD_DIM), lambda h, i: (h, 0, 0))
    kvt = pl.BlockSpec((1, ATT_HEAD_DIM, n), lambda h, i: (h, 0, 0))
    qb = pl.BlockSpec((tq, gw), lambda h, i: (i, h))
    return pl.pallas_call(
        body, name=name, grid=(ATT_KV_HEADS, n // tq), in_specs=[qb, kv, kv, qb], out_specs=[qb, kvt, kvt],
        out_shape=[SDS((n, ATT_Q_DIM), F32), SDS((ATT_KV_HEADS, ATT_HEAD_DIM, n), F32), SDS((ATT_KV_HEADS, ATT_HEAD_DIM, n), F32)],
        compiler_params=_cp("parallel", "arbitrary"))(q, k, v, do)


def _both_directions(mats, axis):
    fwd = np.concatenate(mats, axis=axis).astype(np.float32)
    bwd = np.concatenate([m[::-1, ::-1] for m in mats], axis=axis).astype(np.float32)
    return jnp.asarray(np.stack([fwd, bwd]), MXU_DTYPE)


def _hg_segments():
    c = HG_CHUNK
    t = np.arange(c)[:, None]
    r = np.arange(c)[None, :]
    mats = [(r <= t)]
    for lev in range(HG_LEVELS):
        h = c >> (lev + 1)
        mid = (t // (2 * h)) * (2 * h) + h - 1
        hi = (t // h) % 2 == 1
        mats.append(np.where(hi, (r > mid) & (r <= t), (r > t) & (r <= mid)))
    mats.append(r > t)
    return _both_directions(mats, 0)


def _hg_pair_sums():
    c = HG_CHUNK
    r = np.arange(c)[:, None]
    t = np.arange(c)[None, :]
    gp, gn = [t >= r], [t < r]
    for lev in range(HG_LEVELS):
        sh = HG_LEVELS - 1 - lev
        same = (r >> sh) == (t >> sh)
        gp.append(same & (t >= r))
        gn.append(same & (t < r))
    return _both_directions(gp, 1), _both_directions(gn, 1)


def _split_dot(mat, x):
    hi = x.astype(MXU_DTYPE)
    lo = (x - hi.astype(F32)).astype(MXU_DTYPE)
    return _dot(mat, hi) + _dot(mat, lo)


def _hg_gates(hq, z, a0, a1):
    q = hq * _sigmoid(hq)
    sg = _sigmoid(z)
    lb = _sigmoid(a0 - a1)
    f = lb + (1.0 - lb) * sg
    k = (1.0 - lb) * (1.0 - sg)
    return q, f, k, sg, lb


def _hg_level_masks(mirrored):
    c = HG_CHUNK
    row = lax.broadcasted_iota(jnp.int32, (c, 1), 0)
    rr = lax.broadcasted_iota(jnp.int32, (c, c), 0)
    cc = lax.broadcasted_iota(jnp.int32, (c, c), 1)
    his, sames = [], []
    for lev in range(HG_LEVELS):
        sh = HG_LEVELS - 1 - lev
        his.append(jnp.logical_xor(((row >> sh) & 1) == 1, mirrored))
        sames.append((rr >> (sh + 1)) == (cc >> (sh + 1)))
    return his, sames, rr == cc


def _hg_intra(q, k, ex, masks):
    his, sames, eye = masks
    a = jnp.where(eye, jnp.sum(q * k, axis=-1, keepdims=True), 0.0)
    for lev in range(HG_LEVELS):
        e = ex[lev + 1]
        qs = jnp.where(his[lev], q * e, 0.0)
        ks = jnp.where(his[lev], 0.0, k * e)
        a = a + jnp.where(sames[lev], _dot(qs, ks, "nt"), 0.0)
    return a


def _hg_specs(n, with_time):
    c = HG_CHUNK
    nc = n // c

    def chunk(d, i):
        first = d if with_time else 1 - d
        return i + first * (nc - 1 - 2 * i)

    def pcols(off, dir_stride=0):
        return [pl.BlockSpec((c, HG_PAIR), lambda d, i, j=j: (chunk(d, i), off // HG_PAIR + dir_stride // HG_PAIR * d + j)) for j in range(2)]

    specs = dict(
        hq=pcols(OFF_HQ), v=pcols(OFF_HI), z=pcols(OFF_ZF, OFF_ZB - OFF_ZF),
        shared=pl.BlockSpec((c, HG_DIM), lambda d, i: (chunk(d, i), 0)),
        per_dir=pl.BlockSpec((1, c, HG_DIM), lambda d, i: (d, chunk(d, i), 0)),
        vec=pl.BlockSpec((1, 1, HG_DIM), lambda d, i: (d, 0, 0)),
        seg=pl.BlockSpec((1, (HG_LEVELS + 2) * c, c), lambda d, i: (d, 0, 0)),
        sums=pl.BlockSpec((1, c, (HG_LEVELS + 1) * c), lambda d, i: (d, 0, 0)),
        state=pl.BlockSpec((1, HG_HEADS, 1, HG_HEAD_DIM, HG_HEAD_DIM), lambda d, i: (d, 0, chunk(d, i), 0, 0)))
    return nc, specs


def _hg_head(refs, hh):
    off = (hh % 2) * HG_HEAD_DIM
    return refs[hh // 2][:, off:off + HG_HEAD_DIM]


def _hg_lanes(hh):
    return slice(hh * HG_HEAD_DIM, (hh + 1) * HG_HEAD_DIM)


def _hg_exps(seg_ref, f):
    lf = jnp.log(f)
    args = _split_dot(seg_ref[0], lf)
    c = HG_CHUNK
    return [jnp.exp(args[j * c:(j + 1) * c]) for j in range(HG_LEVELS + 2)]


def _hg_last_row(a, mirrored):
    return jnp.where(mirrored, a[0:1, :], a[HG_CHUNK - 1:HG_CHUNK, :])


def _hgrn_fwd(p, a0, a1, seg, name):
    n = p.shape[0]
    nc, sp = _hg_specs(n, True)

    def body(hq0, hq1, z0, z1, v0, v1, a0_ref, a1_ref, seg_ref, o_ref, st):
        @pl.when(pl.program_id(1) == 0)
        def _():
            st[...] = jnp.zeros_like(st)

        mirrored = pl.program_id(0) == 1
        masks = _hg_level_masks(mirrored)
        for hh in range(HG_HEADS):
            ln = _hg_lanes(hh)
            q, f, k, _, _ = _hg_gates(_hg_head((hq0, hq1), hh), _hg_head((z0, z1), hh), a0_ref[0, :, ln], a1_ref[0, :, ln])
            vv = _hg_head((v0, v1), hh)
            ex = _hg_exps(seg_ref, f)
            a = _hg_intra(q, k, ex, masks)
            s_t = st[hh]
            o_ref[0, :, ln] = _dot(a, vv) + _dot(q * ex[0], s_t, "nt")
            st[hh] = s_t * _hg_last_row(ex[0], mirrored) + _dot(vv, k * ex[HG_LEVELS + 1], "tn")

    return pl.pallas_call(
        body, name=name, grid=(2, nc), in_specs=sp["hq"] + sp["z"] + sp["v"] + [sp["vec"], sp["vec"], sp["seg"]],
        out_specs=sp["per_dir"], out_shape=SDS((2, n, HG_DIM), F32),
        scratch_shapes=[pltpu.VMEM((HG_HEADS, HG_HEAD_DIM, HG_HEAD_DIM), F32)],
        compiler_params=_cp("parallel", "arbitrary"))(p, p, p, p, p, p, a0, a1, seg)


def _hgrn_bwd_q(p, a0, a1, seg, gp, do, name):
    n = p.shape[0]
    nc, sp = _hg_specs(n, True)


    def body(hq0, hq1, z0, z1, v0, v1, a0_ref, a1_ref, seg_ref, gp_ref, do_ref, dhq_ref, dlf_ref, s0_ref, st):
        @pl.when(pl.program_id(1) == 0)
        def _():
            st[...] = jnp.zeros_like(st)

        mirrored = pl.program_id(0) == 1
        his, sames, eye = _hg_level_masks(mirrored)
        for hh in range(HG_HEADS):
            ln = _hg_lanes(hh)
            s_t = st[hh]
            s0_ref[0, hh, 0] = s_t
            hqv = _hg_head((hq0, hq1), hh)
            q, f, k, _, _ = _hg_gates(hqv, _hg_head((z0, z1), hh), a0_ref[0, :, ln], a1_ref[0, :, ln])
            vv, dov = _hg_head((v0, v1), hh), do_ref[:, ln]
            ex = _hg_exps(seg_ref, f)
            da = _dot(dov, vv, "nt")
            dq_inter = ex[0] * _dot(dov, s_t)
            dq = jnp.sum(dov * vv, axis=-1, keepdims=True) * k + dq_inter
            terms = [q * dq_inter]
            for lev in range(HG_LEVELS):
                e = ex[lev + 1]
                ks = jnp.where(his[lev], 0.0, k * e)
                part = jnp.where(his[lev], e, 0.0) * _dot(jnp.where(sames[lev], da, 0.0), ks)
                dq = dq + part
                terms.append(q * part)
            dlf_ref[0, :, ln] = _dot(gp_ref[0], jnp.concatenate(terms, axis=0))
            sq = _sigmoid(hqv)
            dhq_ref[0, :, ln] = dq * sq * (1.0 + hqv * (1.0 - sq))
            st[hh] = s_t * _hg_last_row(ex[0], mirrored) + _dot(vv, k * ex[HG_LEVELS + 1], "tn")

    out = SDS((2, n, HG_DIM), F32)
    return pl.pallas_call(
        body, name=name, grid=(2, nc),
        in_specs=sp["hq"] + sp["z"] + sp["v"] + [sp["vec"], sp["vec"], sp["seg"], sp["sums"], sp["shared"]],
        out_specs=[sp["per_dir"], sp["per_dir"], sp["state"]],
        out_shape=[out, out, SDS((2, HG_HEADS, nc, HG_HEAD_DIM, HG_HEAD_DIM), F32)],
        scratch_shapes=[pltpu.VMEM((HG_HEADS, HG_HEAD_DIM, HG_HEAD_DIM), F32)],
        compiler_params=_cp("parallel", "arbitrary"))(p, p, p, p, p, p, a0, a1, seg, gp, do)


def _hgrn_bwd_kv(p, a0, a1, seg, gn, do, dlf_q, s0, name):
    n = p.shape[0]
    nc, sp = _hg_specs(n, False)

    def body(hq0, hq1, z0, z1, v0, v1, a0_ref, a1_ref, seg_ref, gn_ref, do_ref, dlfq_ref, s0_ref, dz_ref, dv_ref, dlb_ref, rt):
        @pl.when(pl.program_id(1) == 0)
        def _():
            rt[...] = jnp.zeros_like(rt)
            dlb_ref[...] = jnp.zeros_like(dlb_ref)

        mirrored = pl.program_id(0) == 1
        masks = _hg_level_masks(mirrored)
        his, sames, eye = masks
        for hh in range(HG_HEADS):
            ln = _hg_lanes(hh)
            q, f, k, sg, lb = _hg_gates(_hg_head((hq0, hq1), hh), _hg_head((z0, z1), hh), a0_ref[0, :, ln], a1_ref[0, :, ln])
            vv, dov = _hg_head((v0, v1), hh), do_ref[:, ln]
            ex = _hg_exps(seg_ref, f)
            a = _hg_intra(q, k, ex, masks)
            da = _dot(dov, vv, "nt")
            r_t = rt[hh]
            k_end = k * ex[HG_LEVELS + 1]
            dv_ref[0, :, ln] = _dot(a, dov, "tn") + _dot(k_end, r_t, "nt")
            dk_inter = ex[HG_LEVELS + 1] * _dot(vv, r_t)
            dk = jnp.sum(dov * vv, axis=-1, keepdims=True) * q + dk_inter
            terms = [k * dk_inter]
            for lev in range(HG_LEVELS):
                e = ex[lev + 1]
                qs = jnp.where(his[lev], q * e, 0.0)
                part = jnp.where(his[lev], 0.0, e) * _dot(jnp.where(sames[lev], da, 0.0), qs, "tn")
                dk = dk + part
                terms.append(k * part)
            decay = _hg_last_row(ex[0], mirrored)
            rt[hh] = r_t * decay + _dot(dov, q * ex[0], "tn")
            later = decay * jnp.sum(s0_ref[0, hh, 0] * r_t, axis=0, keepdims=True)
            dlf = dlfq_ref[0, :, ln] + _dot(gn_ref[0], jnp.concatenate(terms, axis=0)) + later
            df = dlf / f - dk
            dz_ref[0, :, ln] = df * (1.0 - lb) * sg * (1.0 - sg)
            dlb_ref[0, :, ln] += jnp.sum(df * (1.0 - sg), axis=0, keepdims=True)

    out = SDS((2, n, HG_DIM), F32)
    return pl.pallas_call(
        body, name=name, grid=(2, nc),
        in_specs=sp["hq"] + sp["z"] + sp["v"] + [sp["vec"], sp["vec"], sp["seg"], sp["sums"], sp["shared"], sp["per_dir"], sp["state"]],
        out_specs=[sp["per_dir"], sp["per_dir"], sp["vec"]], out_shape=[out, out, SDS((2, 1, HG_DIM), F32)],
        scratch_shapes=[pltpu.VMEM((HG_HEADS, HG_HEAD_DIM, HG_HEAD_DIM), F32)],
        compiler_params=_cp("parallel", "arbitrary"))(p, p, p, p, p, p, a0, a1, seg, gn, do, dlf_q, s0)


def _hg_post(o2, p, g, name):
    n = p.shape[0]
    tr = min(ROW_TILE, n)
    w = 2 * HG_HEAD_DIM

    def body(of_ref, ob_ref, hg_ref, g_ref, o_ref):
        for j in range(2):
            sl = slice(j * HG_HEAD_DIM, (j + 1) * HG_HEAD_DIM)
            o = of_ref[0, :, sl] + ob_ref[0, :, sl]
            hg = hg_ref[:, sl]
            o_ref[:, sl] = (o * _rstd(o) * g_ref[...] * (hg * _sigmoid(hg))).astype(o_ref.dtype)

    blk = pl.BlockSpec((tr, w), lambda i, j: (i, j))
    dirs = [pl.BlockSpec((1, tr, w), lambda i, j, d=d: (d, i, j)) for d in range(2)]
    return pl.pallas_call(
        body, name=name, grid=(n // tr, HG_DIM // w),
        in_specs=dirs + [pl.BlockSpec((tr, w), lambda i, j: (i, OFF_HG // w + j)), pl.BlockSpec((1, HG_HEAD_DIM), lambda i, j: (0, 0))],
        out_specs=blk, out_shape=SDS((n, HG_DIM), MXU_DTYPE), compiler_params=_cp("parallel", "parallel"))(o2, o2, p, g)


def _hg_post_bwd(o2, p, g, dcat, name):
    n = p.shape[0]
    tr = min(ROW_TILE, n)
    w = 2 * HG_HEAD_DIM

    def body(of_ref, ob_ref, hg_ref, g_ref, d_ref, do_ref, dhg_ref, dg_ref):
        @pl.when(pl.program_id(1) == 0)
        def _():
            dg_ref[...] = jnp.zeros_like(dg_ref)

        for j in range(2):
            sl = slice(j * HG_HEAD_DIM, (j + 1) * HG_HEAD_DIM)
            o = of_ref[0, :, sl] + ob_ref[0, :, sl]
            hg = hg_ref[:, sl]
            d = d_ref[:, sl].astype(F32)
            sg = _sigmoid(hg)
            on = o * _rstd(o) * g_ref[...]
            dhg_ref[:, sl] = (d * on * sg * (1.0 + hg * (1.0 - sg))).astype(dhg_ref.dtype)
            dx, dg = _rms_bwd(o, g_ref[...], d * hg * sg)
            do_ref[:, sl] = dx
            dg_ref[0, :, sl] += dg

    blk = pl.BlockSpec((tr, w), lambda j, i: (i, j))
    dirs = [pl.BlockSpec((1, tr, w), lambda j, i, d=d: (d, i, j)) for d in range(2)]
    return pl.pallas_call(
        body, name=name, grid=(HG_DIM // w, n // tr),
        in_specs=dirs + [pl.BlockSpec((tr, w), lambda j, i: (i, OFF_HG // w + j)), pl.BlockSpec((1, HG_HEAD_DIM), lambda j, i: (0, 0)),
                         pl.BlockSpec((tr, w), lambda j, i: (i, ATT_Q_DIM // w + j))],
        out_specs=[blk, blk, pl.BlockSpec((1, 1, w), lambda j, i: (j, 0, 0))],
        out_shape=[SDS((n, HG_DIM), F32), SDS((n, HG_DIM), MXU_DTYPE), SDS((HG_DIM // w, 1, w), F32)],
        compiler_params=_cp("parallel", "arbitrary"))(o2, o2, p, g, dcat)


XATT_TQ = 512


def _xattn_fwd(q, kv, name):
    n, nm = q.shape[0], kv.shape[0]
    tq = min(XATT_TQ, n)
    scale = X_HEAD_DIM ** -0.5

    def body(q_ref, k_ref, v_ref, o_ref):
        s = _dot(q_ref[...], k_ref[...], "nt") * scale
        e = jnp.exp(s - jnp.max(s, axis=-1, keepdims=True))
        o_ref[...] = _dot(e / jnp.sum(e, axis=-1, keepdims=True), v_ref[...]).astype(o_ref.dtype)

    qb = pl.BlockSpec((tq, X_HEAD_DIM), lambda h, i: (i, h))
    return pl.pallas_call(
        body, name=name, grid=(X_HEADS, n // tq),
        in_specs=[qb, pl.BlockSpec((nm, X_HEAD_DIM), lambda h, i: (0, h)), pl.BlockSpec((nm, X_HEAD_DIM), lambda h, i: (0, X_HEADS + h))],
        out_specs=qb, out_shape=SDS(q.shape, MXU_DTYPE), compiler_params=_cp("parallel", "parallel"))(q, kv, kv)


def _xattn_bwd(q, kv, do, name):
    n, nm = q.shape[0], kv.shape[0]
    tq = min(XATT_TQ, n)
    scale = X_HEAD_DIM ** -0.5

    def body(q_ref, k_ref, v_ref, do_ref, dq_ref, dk_ref, dv_ref):
        @pl.when(pl.program_id(1) == 0)
        def _():
            dk_ref[...] = jnp.zeros_like(dk_ref)
            dv_ref[...] = jnp.zeros_like(dv_ref)

        qv, dov = q_ref[...], do_ref[...]
        s = _dot(qv, k_ref[...], "nt") * scale
        e = jnp.exp(s - jnp.max(s, axis=-1, keepdims=True))
        p = e / jnp.sum(e, axis=-1, keepdims=True)
        dp = _dot(dov, v_ref[...], "nt")
        ds = p * (dp - jnp.sum(p * dp, axis=-1, keepdims=True)) * scale
        dq_ref[...] = _dot(ds, k_ref[...]).astype(dq_ref.dtype)
        dk_ref[...] += _dot(ds, qv, "tn")
        dv_ref[...] += _dot(p, dov, "tn")

    qb = pl.BlockSpec((tq, X_HEAD_DIM), lambda h, i: (i, h))
    kb = pl.BlockSpec((nm, X_HEAD_DIM), lambda h, i: (0, h))
    return pl.pallas_call(
        body, name=name, grid=(X_HEADS, n // tq),
        in_specs=[qb, kb, pl.BlockSpec((nm, X_HEAD_DIM), lambda h, i: (0, X_HEADS + h)), qb], out_specs=[qb, kb, kb],
        out_shape=[SDS(q.shape, MXU_DTYPE), SDS((nm, X_HEADS * X_HEAD_DIM), F32), SDS((nm, X_HEADS * X_HEAD_DIM), F32)],
        compiler_params=_cp("parallel", "arbitrary"))(q, kv, kv, do)


def _shift_rows(u, down):
    n = u.shape[0]
    row = lax.broadcasted_iota(jnp.int32, u.shape, 0)
    if down:
        return jnp.where(row == 0, 0.0, pltpu.roll(u, 1, axis=0))
    return jnp.where(row == n - 1, 0.0, pltpu.roll(u, n - 1, axis=0))


def _conv(u, w, b):
    return b + _shift_rows(u, True) * w[0:1, :] + u * w[1:2, :] + _shift_rows(u, False) * w[2:3, :]


def _ff_specs(n):
    gate = lambda rows: pl.BlockSpec((rows, FF_COLS), lambda j: (0, j))
    val = lambda rows: pl.BlockSpec((rows, FF_COLS), lambda j: (0, FF_BLOCKS + j))
    return [gate(n), val(n), gate(3), val(3), gate(1), val(1)], gate


def _conv_gate(u, cw, cb, name):
    n = u.shape[0]
    ins, gate_blk = _ff_specs(n)

    def body(ug_ref, uv_ref, wg_ref, wv_ref, bg_ref, bv_ref, o_ref):
        gate = _conv(ug_ref[...], wg_ref[...], bg_ref[...])
        val = _conv(uv_ref[...], wv_ref[...], bv_ref[...])
        o_ref[...] = (gate * _sigmoid(gate) * val).astype(o_ref.dtype)

    return pl.pallas_call(
        body, name=name, grid=(FF_BLOCKS,), in_specs=ins, out_specs=gate_blk(n), out_shape=SDS((n, D_FF), MXU_DTYPE),
        compiler_params=_cp("parallel"))(u, u, cw, cw, cb, cb)


def _conv_gate_bwd(u, cw, cb, da, name):
    n = u.shape[0]
    ins, gate_blk = _ff_specs(n)

    def side(dacc, u, w, du_ref, dw_ref, db_ref):
        nxt, prv = _shift_rows(dacc, False), _shift_rows(dacc, True)
        du_ref[...] = (nxt * w[0:1, :] + dacc * w[1:2, :] + prv * w[2:3, :]).astype(du_ref.dtype)
        db_ref[...] = jnp.sum(dacc, axis=0, keepdims=True)
        dw_ref[0:1, :] = jnp.sum(nxt * u, axis=0, keepdims=True)
        dw_ref[1:2, :] = jnp.sum(dacc * u, axis=0, keepdims=True)
        dw_ref[2:3, :] = jnp.sum(prv * u, axis=0, keepdims=True)

    def body(ug_ref, uv_ref, wg_ref, wv_ref, bg_ref, bv_ref, da_ref, dug_ref, duv_ref, dwg_ref, dwv_ref, dbg_ref, dbv_ref):
        ug, uv = ug_ref[...], uv_ref[...]
        gate = _conv(ug, wg_ref[...], bg_ref[...])
        val = _conv(uv, wv_ref[...], bv_ref[...])
        sg = _sigmoid(gate)
        dav = da_ref[...].astype(F32)
        side(dav * val * sg * (1.0 + gate * (1.0 - sg)), ug, wg_ref[...], dug_ref, dwg_ref, dbg_ref)
        side(dav * gate * sg, uv, wv_ref[...], duv_ref, dwv_ref, dbv_ref)

    return pl.pallas_call(
        body, name=name, grid=(FF_BLOCKS,), in_specs=ins + [gate_blk(n)],
        out_specs=[gate_blk(n), gate_blk(n), gate_blk(3), gate_blk(3), gate_blk(1), gate_blk(1)],
        out_shape=[SDS((n, D_FF), MXU_DTYPE)] * 2 + [SDS((3, D_FF), F32)] * 2 + [SDS((1, D_FF), F32)] * 2,
        compiler_params=_cp("parallel"))(u, u, cw, cw, cb, cb, da)


def _adamw(w, g, m, v, name):
    r, c = w.shape
    tr = r if r <= 512 else 256 if r % 256 == 0 else 88
    assert r % tr == 0, (name, r, tr)

    def body(w_ref, g_ref, m_ref, v_ref, d_ref, mo_ref, vo_ref):
        gv = g_ref[...]
        mn = ADAM_B1 * m_ref[...] + (1.0 - ADAM_B1) * gv
        vn = ADAM_B2 * v_ref[...] + (1.0 - ADAM_B2) * gv * gv
        m_hat = mn / (1.0 - ADAM_B1 ** ADAM_STEP)
        v_hat = vn / (1.0 - ADAM_B2 ** ADAM_STEP)
        d_ref[...] = -ADAM_LR * (m_hat / (jnp.sqrt(v_hat) + ADAM_EPS) + ADAM_WD * w_ref[...])
        mo_ref[...] = mn
        vo_ref[...] = vn

    blk = pl.BlockSpec((tr, c), lambda i: (i, 0))
    out = SDS((r, c), F32)
    return pl.pallas_call(body, name=name, grid=(r // tr,), in_specs=[blk] * 4, out_specs=[blk] * 3, out_shape=[out] * 3,
                          compiler_params=_cp("parallel"))(w, g, m, v)


def _half_tile(h):
    tr = h if h <= 512 else 256 if h % 256 == 0 else 176
    assert h % tr == 0, (h, tr)
    return tr


def _add_halves(g, r, core, name):
    s, h, c = r.shape
    tr = _half_tile(h)
    steps = h // tr

    def body(ix_ref, g_ref, r_ref, o_ref):
        o_ref[...] = (g_ref[...].astype(F32) + r_ref[...].astype(F32)).astype(o_ref.dtype)

    blk = pl.BlockSpec((1, tr, c), lambda i, j, ix: (i, j, 0))
    grid_spec = pltpu.PrefetchScalarGridSpec(
        num_scalar_prefetch=1, grid=(s, steps),
        in_specs=[pl.BlockSpec((1, tr, c), lambda i, j, ix: (i, ix[0] * steps + j, 0)), blk], out_specs=blk)
    return pl.pallas_call(body, name=name, grid_spec=grid_spec, out_shape=SDS(r.shape, WIRE_DTYPE),
                          compiler_params=_cp("parallel", "parallel"))(core.reshape(1), g, r)


def _sum_chips(own, recv, me, core, name):
    _, h, c = own.shape
    tr = _half_tile(h)

    def body(ix_ref, own_ref, recv_ref, o_ref):
        acc = own_ref[0].astype(F32)
        for j in range(3):
            acc = acc + recv_ref[j].astype(F32)
        o_ref[0] = acc

    grid_spec = pltpu.PrefetchScalarGridSpec(
        num_scalar_prefetch=1, grid=(h // tr,),
        in_specs=[pl.BlockSpec((1, tr, c), lambda i, ix: (ix[0], i, 0)), pl.BlockSpec((3, tr, c), lambda i, ix: (0, i, 0))],
        out_specs=pl.BlockSpec((1, tr, c), lambda i, ix: (ix[1], i, 0)))
    return pl.pallas_call(body, name=name, grid_spec=grid_spec, out_shape=SDS((2, h, c), F32),
                          compiler_params=_cp("parallel"))(jnp.stack([me, core]), own, recv)


ANY = pl.BlockSpec(memory_space=pl.ANY)


def _place():
    x, y, c = lax.axis_index("x"), lax.axis_index("y"), lax.axis_index("c")
    return x, y, c, [(1 - x, y), (x, 1 - y), (1 - x, 1 - y)]


def _gather_shards(shards, name):
    nt = len(shards)

    def body(*refs):
        ins, outs = refs[:nt], refs[nt:2 * nt]
        send, recv, fsend, frecv, osend, orecv = refs[2 * nt:]
        x, y, c, chips = _place()
        me = 2 * x + y

        def half(t, chip, cc):
            h = ins[t].shape[0] // 2
            return outs[t].at[chip, pl.ds(cc * h, h)]

        def ici(t, j):
            cx, cy = chips[j]
            h = ins[t].shape[0] // 2
            return pltpu.make_async_remote_copy(src_ref=ins[t].at[pl.ds(c * h, h)], dst_ref=half(t, me, c),
                                                send_sem=send.at[t, j], recv_sem=recv.at[t, j], device_id=(cx, cy, c), device_id_type=MESH)

        def landed(t, j):
            cx, cy = chips[j]
            blk = half(t, 2 * cx + cy, c)
            return pltpu.make_async_remote_copy(src_ref=blk, dst_ref=blk, send_sem=send.at[t, j], recv_sem=recv.at[t, j],
                                                device_id=(cx, cy, c), device_id_type=MESH)

        def d2d(t, j, cc):
            cx, cy = chips[j]
            blk = half(t, 2 * cx + cy, cc)
            return pltpu.make_async_remote_copy(src_ref=blk, dst_ref=blk, send_sem=fsend.at[t, j], recv_sem=frecv.at[t, j],
                                                device_id=(x, y, 1 - c), device_id_type=MESH)

        own = [pltpu.make_async_remote_copy(src_ref=ins[t], dst_ref=outs[t].at[me], send_sem=osend.at[t], recv_sem=orecv.at[t],
                                            device_id=(x, y, 1 - c), device_id_type=MESH) for t in range(nt)]
        for t in range(nt):
            for j in range(3):
                ici(t, j).start()
        for cp in own:
            cp.start()
        for t in range(nt):
            for j in range(3):
                landed(t, j).wait_recv()
                d2d(t, j, c).start()
        for t in range(nt):
            for j in range(3):
                d2d(t, j, 1 - c).wait_recv()
        for t in range(nt):
            for j in range(3):
                ici(t, j).wait_send()
                d2d(t, j, c).wait_send()
        for cp in own:
            cp.wait()

    return pl.pallas_call(
        body, name=name, in_specs=[ANY] * nt, out_specs=[ANY] * nt,
        out_shape=[SDS((4,) + s.shape, s.dtype) for s in shards],
        scratch_shapes=[pltpu.SemaphoreType.DMA((nt, 3))] * 4 + [pltpu.SemaphoreType.DMA((nt,))] * 2,
        compiler_params=pltpu.CompilerParams(has_side_effects=True))(*shards)


def _swap_halves(grads, name):
    nt = len(grads)

    def body(*refs):
        ins, outs = refs[:nt], refs[nt:2 * nt]
        send, recv = refs[2 * nt:]
        x, y, c, _ = _place()
        cps = []
        for t in range(nt):
            h = ins[t].shape[1] // 2
            cps.append(pltpu.make_async_remote_copy(src_ref=ins[t].at[pl.ds(0, 4), pl.ds((1 - c) * h, h)], dst_ref=outs[t], send_sem=send.at[t],
                                                    recv_sem=recv.at[t], device_id=(x, y, 1 - c), device_id_type=MESH))
        for cp in cps:
            cp.start()
        for cp in cps:
            cp.wait()

    return pl.pallas_call(
        body, name=name, in_specs=[ANY] * nt, out_specs=[ANY] * nt,
        out_shape=[SDS((g.shape[0], g.shape[1] // 2, g.shape[2]), g.dtype) for g in grads],
        scratch_shapes=[pltpu.SemaphoreType.DMA((nt,))] * 2,
        compiler_params=pltpu.CompilerParams(has_side_effects=True))(*grads)


def _send_to_owners(parts, name):
    nt = len(parts)

    def body(*refs):
        ins, outs = refs[:nt], refs[nt:2 * nt]
        send, recv = refs[2 * nt:]
        x, y, c, chips = _place()

        def ici(t, j):
            cx, cy = chips[j]
            return pltpu.make_async_remote_copy(src_ref=ins[t].at[2 * cx + cy], dst_ref=outs[t].at[j], send_sem=send.at[t, j],
                                                recv_sem=recv.at[t, j], device_id=(cx, cy, c), device_id_type=MESH)

        for t in range(nt):
            for j in range(3):
                ici(t, j).start()
        for t in range(nt):
            for j in range(3):
                ici(t, j).wait()

    return pl.pallas_call(
        body, name=name, in_specs=[ANY] * nt, out_specs=[ANY] * nt, out_shape=[SDS((3,) + p.shape[1:], p.dtype) for p in parts],
        scratch_shapes=[pltpu.SemaphoreType.DMA((nt, 3))] * 2,
        compiler_params=pltpu.CompilerParams(has_side_effects=True))(*parts)


def _join_halves(bufs, name):
    nt = len(bufs)

    def body(*refs):
        outs = refs[nt:2 * nt]
        send, recv = refs[2 * nt:]
        x, y, c, _ = _place()
        cps = [pltpu.make_async_remote_copy(src_ref=outs[t].at[c], dst_ref=outs[t].at[c], send_sem=send.at[t], recv_sem=recv.at[t],
                                            device_id=(x, y, 1 - c), device_id_type=MESH) for t in range(nt)]
        for cp in cps:
            cp.start()
        for t in range(nt):
            theirs = outs[t].at[1 - c]
            pltpu.make_async_remote_copy(src_ref=theirs, dst_ref=theirs, send_sem=send.at[t], recv_sem=recv.at[t],
                                         device_id=(x, y, 1 - c), device_id_type=MESH).wait_recv()
        for cp in cps:
            cp.wait_send()

    return pl.pallas_call(
        body, name=name, in_specs=[ANY] * nt, out_specs=[ANY] * nt, out_shape=[SDS(b.shape, b.dtype) for b in bufs],
        input_output_aliases={t: t for t in range(nt)},
        scratch_shapes=[pltpu.SemaphoreType.DMA((nt,))] * 2,
        compiler_params=pltpu.CompilerParams(has_side_effects=True))(*bufs)


def _exchange_small(v, reduce, name, after=()):
    rows = v.shape[0]
    after, after_specs = _unread(after)

    def body(v_ref, *rest):
        o_ref, buf, send, recv = rest[-4:]
        x, y, c, _ = _place()
        me = 4 * x + 2 * y + c
        buf[me] = v_ref[...]

        def peer(dx, dy, dc):
            return (1 - x if dx else x, 1 - y if dy else y, 1 - c if dc else c)

        peers = [(dx, dy, dc) for dx in range(2) for dy in range(2) for dc in range(2) if (dx, dy, dc) != (0, 0, 0)]
        cps = []
        for j, (dx, dy, dc) in enumerate(peers):
            cps.append(pltpu.make_async_remote_copy(src_ref=v_ref, dst_ref=buf.at[me], send_sem=send.at[j], recv_sem=recv.at[j],
                                                    device_id=peer(dx, dy, dc), device_id_type=MESH))
        for cp in cps:
            cp.start()
        for j, (dx, dy, dc) in enumerate(peers):
            px, py, pc = peer(dx, dy, dc)
            blk = buf.at[4 * px + 2 * py + pc]
            pltpu.make_async_remote_copy(src_ref=blk, dst_ref=blk, send_sem=send.at[j], recv_sem=recv.at[j],
                                         device_id=(px, py, pc), device_id_type=MESH).wait_recv()
        for cp in cps:
            cp.wait_send()
        if reduce:
            acc = buf[0]
            for j in range(1, 8):
                acc = acc + buf[j]
            o_ref[...] = acc
        else:
            o_ref[...] = buf[...]

    vm = pl.BlockSpec(memory_space=pltpu.VMEM)
    return pl.pallas_call(
        body, name=name, in_specs=[vm] + after_specs, out_specs=vm, out_shape=SDS((rows, 128) if reduce else (8, rows, 128), F32),
        scratch_shapes=[pltpu.VMEM((8, rows, 128), F32), pltpu.SemaphoreType.DMA((7,)), pltpu.SemaphoreType.DMA((7,))],
        compiler_params=pltpu.CompilerParams(has_side_effects=True))(v, *after)


HBM = pl.BlockSpec(memory_space=pltpu.HBM)
SEM = pl.BlockSpec(memory_space=pltpu.SEMAPHORE)
TOKEN = pl.BlockSpec(memory_space=pltpu.VMEM)
TOKEN_SHAPE = SDS((8, 128), F32)
PEERS = 7


def _in_hbm(a):
    return pltpu.with_memory_space_constraint(a, pltpu.HBM)


def _split_params():
    return pltpu.CompilerParams(has_side_effects=pltpu.SideEffectType.DATAFLOW_SIDE_EFFECTING)


def _gather_start(shards, name, after=()):
    nt = len(shards)
    after, after_specs = _unread(after)

    def body(*refs):
        ins, lands = refs[:nt], refs[nt:2 * nt]
        outs = refs[2 * nt + len(after):]
        sends, recvs = outs[:nt], outs[nt:2 * nt]
        x, y, c, chips = _place()
        me = 2 * x + y
        for t in range(nt):
            h = ins[t].shape[0] // 2
            mine = pl.ds(c * h, h)
            for j, (cx, cy) in enumerate(chips):
                for dc in range(2):
                    pltpu.make_async_remote_copy(src_ref=ins[t].at[mine], dst_ref=lands[t].at[me, mine], send_sem=sends[t].at[2 * j + dc],
                                                 recv_sem=recvs[t].at[2 * j + c], device_id=(cx, cy, dc), device_id_type=MESH).start()
            pltpu.make_async_remote_copy(src_ref=ins[t], dst_ref=lands[t].at[me], send_sem=sends[t].at[PEERS - 1], recv_sem=recvs[t].at[PEERS - 1],
                                         device_id=(x, y, 1 - c), device_id_type=MESH).start()
        outs[-1][...] = jnp.zeros(TOKEN_SHAPE.shape, F32)

    lands = [lax.empty((4,) + s.shape, s.dtype) for s in shards]
    out = pl.pallas_call(
        body, name=name, in_specs=[HBM] * (2 * nt) + after_specs, out_specs=[SEM] * (2 * nt) + [HBM] * (2 * nt) + [TOKEN],
        out_shape=[pltpu.SemaphoreType.DMA((PEERS,))] * (2 * nt)
        + [pltpu.HBM(s.shape, s.dtype) for s in shards] + [pltpu.HBM(l.shape, l.dtype) for l in lands] + [TOKEN_SHAPE],
        input_output_aliases={t: 2 * nt + t for t in range(2 * nt)}, compiler_params=_split_params())(
            *[_in_hbm(s) for s in shards], *[_in_hbm(l) for l in lands], *after)
    return out[:nt], out[nt:2 * nt], out[2 * nt:3 * nt], out[3 * nt:4 * nt], out[-1]


def _gather_wait(sends, recvs, shards, lands, after, name):
    nt = len(shards)

    def body(*refs):
        ins, lands_ref = refs[:nt], refs[nt:2 * nt]
        send_refs, recv_refs = refs[2 * nt:3 * nt], refs[3 * nt:4 * nt]
        x, y, c, chips = _place()
        for t in range(nt):
            h = ins[t].shape[0] // 2
            for j, (cx, cy) in enumerate(chips):
                for cs in range(2):
                    blk = lands_ref[t].at[2 * cx + cy, pl.ds(cs * h, h)]
                    pltpu.make_async_remote_copy(src_ref=blk, dst_ref=blk, send_sem=send_refs[t].at[2 * j + cs], recv_sem=recv_refs[t].at[2 * j + cs],
                                                 device_id=(cx, cy, cs), device_id_type=MESH).wait()
            blk = lands_ref[t].at[2 * x + y]
            pltpu.make_async_remote_copy(src_ref=blk, dst_ref=blk, send_sem=send_refs[t].at[PEERS - 1], recv_sem=recv_refs[t].at[PEERS - 1],
                                         device_id=(x, y, 1 - c), device_id_type=MESH).wait()

    out = pl.pallas_call(
        body, name=name, in_specs=[HBM] * (2 * nt) + [SEM] * (2 * nt) + [ANY], out_specs=[HBM] * (2 * nt),
        out_shape=[pltpu.HBM(s.shape, s.dtype) for s in shards] + [pltpu.HBM(l.shape, l.dtype) for l in lands],
        input_output_aliases={t: t for t in range(2 * nt)}, compiler_params=_split_params())(*shards, *lands, *sends, *recvs, after)
    return out[nt:]


def _scatter_start(g, name):
    _, r, c_ = g.shape
    h = r // 2

    def body(g_ref, land, send, recv, g_thru, land_thru, token):
        x, y, c, chips = _place()
        for j, (cx, cy) in enumerate(chips):
            for dc in range(2):
                pltpu.make_async_remote_copy(src_ref=g_ref.at[2 * cx + cy, pl.ds(dc * h, h)], dst_ref=land.at[2 * j + c], send_sem=send.at[2 * j + dc],
                                             recv_sem=recv.at[2 * j + c], device_id=(cx, cy, dc), device_id_type=MESH).start()
        pltpu.make_async_remote_copy(src_ref=g_ref.at[2 * x + y, pl.ds((1 - c) * h, h)], dst_ref=land.at[PEERS - 1], send_sem=send.at[PEERS - 1],
                                     recv_sem=recv.at[PEERS - 1], device_id=(x, y, 1 - c), device_id_type=MESH).start()
        token[...] = jnp.zeros(TOKEN_SHAPE.shape, F32)

    land = lax.empty((PEERS, h, c_), g.dtype)
    return pl.pallas_call(
        body, name=name, in_specs=[HBM, HBM], out_specs=[SEM, SEM, HBM, HBM, TOKEN],
        out_shape=[pltpu.SemaphoreType.DMA((PEERS,)), pltpu.SemaphoreType.DMA((PEERS,)), pltpu.HBM(g.shape, g.dtype),
                   pltpu.HBM(land.shape, land.dtype), TOKEN_SHAPE],
        input_output_aliases={0: 2, 1: 3}, compiler_params=_split_params())(_in_hbm(g), _in_hbm(land))


def _scatter_wait(started, after, name):
    nt = len(started)

    def body(*refs):
        lands = refs[nt:2 * nt]
        sends, recvs = refs[2 * nt:3 * nt], refs[3 * nt:4 * nt]
        x, y, c, chips = _place()
        peers = [(cx, cy, dc) for cx, cy in chips for dc in range(2)] + [(x, y, 1 - c)]
        for t in range(nt):
            for k, peer in enumerate(peers):
                blk = lands[t].at[k]
                pltpu.make_async_remote_copy(src_ref=blk, dst_ref=blk, send_sem=sends[t].at[k], recv_sem=recvs[t].at[k],
                                             device_id=peer, device_id_type=MESH).wait()

    gs, lands = [s[2] for s in started], [s[3] for s in started]
    after, after_specs = _unread(after)
    out = pl.pallas_call(
        body, name=name, in_specs=[HBM] * (2 * nt) + [SEM] * (2 * nt) + after_specs, out_specs=[HBM] * (2 * nt),
        out_shape=[pltpu.HBM(a.shape, a.dtype) for a in gs + lands],
        input_output_aliases={t: t for t in range(2 * nt)}, compiler_params=_split_params())(
            *gs, *lands, *[s[0] for s in started], *[s[1] for s in started], *after)
    return out[:nt], out[nt:]


def _sum_devices(g, land, me, core, name):
    npeer, h, c = land.shape
    tr = _half_tile(h)
    steps = h // tr

    def body(ix_ref, own_ref, land_ref, o_ref):
        acc = own_ref[0].astype(F32)
        for j in range(npeer):
            acc = acc + land_ref[j].astype(F32)
        o_ref[0] = acc

    grid_spec = pltpu.PrefetchScalarGridSpec(
        num_scalar_prefetch=1, grid=(steps,),
        in_specs=[pl.BlockSpec((1, tr, c), lambda i, ix: (ix[0], ix[1] * steps + i, 0)), pl.BlockSpec((npeer, tr, c), lambda i, ix: (0, i, 0))],
        out_specs=pl.BlockSpec((1, tr, c), lambda i, ix: (ix[1], i, 0)))
    return pl.pallas_call(body, name=name, grid_spec=grid_spec, out_shape=SDS((2, h, c), F32),
                          compiler_params=_cp("parallel"))(jnp.stack([me, core]), g, land)


def _pack_small(parts):
    flat = jnp.concatenate([p.reshape(-1) for p in parts])
    total = flat.shape[0]
    rows = -(-total // 1024) * 8
    return jnp.pad(flat, (0, rows * 128 - total)).reshape(rows, 128)


def _unpack_small(packed, shapes):
    flat = packed.reshape(-1)
    out, off = [], 0
    for s in shapes:
        size = int(np.prod(s))
        out.append(flat[off:off + size].reshape(s))
        off += size
    return out


def _local_step(x, mem, target, w_in, first_after, mid_weights, ffn_weights, on_grad, gains, conv_w, conv_b, hg_lb):
    n = x.shape[0]
    cos, sin = _rope_tables(n)
    seg = _hg_segments()
    gp, gn = _hg_pair_sums()
    gq2 = jnp.tile(gains["q_norm_g"], (1, 2))
    gk2 = jnp.tile(gains["k_norm_g"], (1, 2))
    a0 = hg_lb[:, 0:1, :]
    a1 = hg_lb[:, 1:2, :]

    p, h1 = _norm_mm(x, gains["pre_mix_g"], w_in, F32, 512, 1664, "in_proj", after=(first_after,))
    qr, kr = _qk_prep(p, gq2, gk2, cos, sin, "qk_prep")
    heads = lambda a: a.reshape(n, ATT_KV_HEADS, ATT_HEAD_DIM).transpose(1, 0, 2)
    kh = heads(kr)
    vh = heads(p[:, OFF_AV:OFF_AV + ATT_KV_DIM].astype(MXU_DTYPE))
    att = _attn_fwd(qr, kh, vh, "attn_fwd")
    o2 = _hgrn_fwd(p, a0, a1, seg, "hgrn_fwd")
    rec = _hg_post(o2, p, gains["hg_out_norm_g"], "hg_post")
    cat = jnp.concatenate([att, rec], axis=1)
    w_out, w_xq, w_xkv, w_xo = mid_weights(cat)
    mixed = _mm(cat, w_out, "nn", F32, 512, 1024, "out_proj")
    x1 = _resid_norm(x, mixed, gains["post_mix_g"], "mix_resid")
    xq, h2 = _norm_mm(x1, gains["pre_x_g"], w_xq, MXU_DTYPE, 512, 1024, "xq_proj")
    kv, mn = _norm_mm(mem, gains["mem_norm_g"], w_xkv, MXU_DTYPE, 256, 2048, "xkv_proj")
    ox = _xattn_fwd(xq, kv, "xattn_fwd")
    xo = _mm(ox, w_xo, "nn", F32, 512, 1024, "xo_proj")
    x2 = _resid_norm(x1, xo, gains["post_x_g"], "x_resid")
    w_up, w_down = ffn_weights(x2)
    u, h3 = _norm_mm(x2, gains["pre_ffn_g"], w_up, F32, 512, 1408, "up_proj")
    act = _conv_gate(u, conv_w, conv_b, "conv_gate")
    dn = _mm(act, w_down, "nn", F32, 512, 1024, "down_proj")
    d3, loss = _final_loss(x2, dn, gains["post_ffn_g"], target, "ffn_resid_loss")

    gs = {}
    d_dn, gs["post_ffn_g"] = _norm_bwd(dn, gains["post_ffn_g"], d3, None, MXU_DTYPE, "ffn_post_bwd")
    tok = on_grad("w_down", _mm(act, d_dn, "tn", WIRE_DTYPE, 1408, 1024, "down_dw"))
    d_act = _mm(d_dn, w_down, "nt", F32, 512, 1408, "down_dx", after=(tok,))
    du_g, du_v, dcw_g, dcw_v, dcb_g, dcb_v = _conv_gate_bwd(u, conv_w, conv_b, d_act, "conv_gate_bwd")
    gs["conv_w"] = jnp.concatenate([dcw_g, dcw_v], axis=1)
    gs["conv_b"] = jnp.concatenate([dcb_g, dcb_v], axis=1)
    ff_shard = w_up.shape[2]
    g_up = _dw_by_owner(h3, du_g, ff_shard, 0, None, 512, "up_dw_gate")
    tok = on_grad("w_up", _dw_by_owner(h3, du_v, ff_shard, 2, g_up, 512, "up_dw_value"))
    d_h3 = _mm_nt_parts([(du_g, 0), (du_g, 1), (du_v, 0), (du_v, 1)], w_up, F32, 512, 512, "up_dx", after=(tok,))
    d2, gs["pre_ffn_g"] = _norm_bwd(x2, gains["pre_ffn_g"], d_h3, d3, F32, "ffn_pre_bwd")
    d_xo, gs["post_x_g"] = _norm_bwd(xo, gains["post_x_g"], d2, None, MXU_DTYPE, "x_post_bwd")
    tok = on_grad("w_xo", _mm(ox, d_xo, "tn", WIRE_DTYPE, 512, 1024, "xo_dw"))
    d_ox = _mm(d_xo, w_xo, "nt", MXU_DTYPE, 512, 1024, "xo_dx", after=(tok,))
    d_xq, d_k, d_v = _xattn_bwd(xq, kv, d_ox, "xattn_bwd")
    d_kv = jnp.concatenate([d_k, d_v], axis=1).astype(MXU_DTYPE)
    tok = on_grad("w_xq", _mm(h2, d_xq, "tn", WIRE_DTYPE, 512, 1024, "xq_dw"))
    tok_kv = on_grad("w_xkv", _dw_by_owner(mn, d_kv, w_xkv.shape[2], 0, None, 512, "xkv_dw"))
    d_h2 = _mm(d_xq, w_xq, "nt", F32, 512, 1024, "xq_dx", after=(tok, tok_kv))
    d_mn = _mm_nt_parts([(d_kv, s) for s in range(4)], w_xkv, F32, 256, 1024, "xkv_dx")
    _, gs["mem_norm_g"] = _norm_bwd(mem, gains["mem_norm_g"], d_mn, None, MXU_DTYPE, "mem_norm_bwd")
    d1, gs["pre_x_g"] = _norm_bwd(x1, gains["pre_x_g"], d_h2, d2, F32, "x_pre_bwd")
    d_mixed, gs["post_mix_g"] = _norm_bwd(mixed, gains["post_mix_g"], d1, None, MXU_DTYPE, "mix_post_bwd")
    tok = on_grad("w_out", _mm(cat, d_mixed, "tn", WIRE_DTYPE, 512, 1024, "out_dw"))
    d_cat = _mm(d_mixed, w_out, "nt", MXU_DTYPE, 512, 1024, "out_dx", after=(tok,))
    d_o, d_hg, dg_hg = _hg_post_bwd(o2, p, gains["hg_out_norm_g"], d_cat, "hg_post_bwd")
    gs["hg_out_norm_g"] = dg_hg.reshape(HG_HEADS, HG_HEAD_DIM).sum(axis=0, keepdims=True)
    dhq2, dlf_q, s0 = _hgrn_bwd_q(p, a0, a1, seg, gp, d_o, "hgrn_bwd_q")
    dz2, dhv2, dlb = _hgrn_bwd_kv(p, a0, a1, seg, gn, d_o, dlf_q, s0, "hgrn_bwd_kv")
    lb = jax.nn.sigmoid(a0 - a1)
    da0 = dlb * lb * (1.0 - lb)
    gs["hg_lb"] = jnp.concatenate([da0, -da0], axis=1)
    d_qr, d_kh, d_vh = _attn_bwd(qr, kh, vh, d_cat, "attn_bwd")
    unheads = lambda a: a.transpose(2, 0, 1).reshape(n, ATT_KV_DIM)
    d_aq, d_ak, dgq, dgk = _qk_prep_bwd(p, gq2, gk2, cos, sin, d_qr, unheads(d_kh), "qk_prep_bwd")
    gs["q_norm_g"] = dgq.reshape(ATT_HEADS, ATT_HEAD_DIM).sum(axis=0, keepdims=True)
    gs["k_norm_g"] = dgk.reshape(ATT_KV_HEADS, ATT_HEAD_DIM).sum(axis=0, keepdims=True)
    d_p = jnp.concatenate([d_aq, d_ak, unheads(d_vh).astype(MXU_DTYPE), (dhq2[0] + dhq2[1]).astype(MXU_DTYPE),
                           dz2[0].astype(MXU_DTYPE), dz2[1].astype(MXU_DTYPE), (dhv2[0] + dhv2[1]).astype(MXU_DTYPE), d_hg], axis=1)
    tok = on_grad("w_in", _mm(h1, d_p, "tn", WIRE_DTYPE, 512, 1664, "in_dw"))
    d_h1 = _mm(d_p, w_in, "nt", F32, 512, 1024, "in_dx", after=(tok,))
    grad_x, gs["pre_mix_g"] = _norm_bwd(x, gains["pre_mix_g"], d_h1, d1, F32, "mix_pre_bwd")
    return loss, grad_x, gs


MATS = ("w_in", "w_out", "w_xq", "w_xkv", "w_xo", "w_up", "w_down")
GAINS = ("pre_mix_g", "q_norm_g", "k_norm_g", "hg_out_norm_g", "post_mix_g", "pre_x_g", "mem_norm_g", "post_x_g", "pre_ffn_g", "post_ffn_g")
WEIGHTS = ('pre_mix_g', 'w_in', 'q_norm_g', 'k_norm_g', 'hg_lb', 'hg_out_norm_g', 'w_out', 'post_mix_g', 'pre_x_g', 'mem_norm_g', 'w_xq',
           'w_xkv', 'w_xo', 'post_x_g', 'pre_ffn_g', 'w_up', 'conv_w', 'conv_b', 'w_down', 'post_ffn_g')


def kernel(x, mem, pre_mix_g, w_in, q_norm_g, k_norm_g, hg_lb, hg_out_norm_g, w_out, post_mix_g, pre_x_g, mem_norm_g, w_xq, w_xkv, w_xo, post_x_g, pre_ffn_g, w_up, conv_w, conv_b, w_down, post_ffn_g, loss_target, m_pre_mix_g, m_w_in, m_q_norm_g, m_k_norm_g, m_hg_lb, m_hg_out_norm_g, m_w_out, m_post_mix_g, m_pre_x_g, m_mem_norm_g, m_w_xq, m_w_xkv, m_w_xo, m_post_x_g, m_pre_ffn_g, m_w_up, m_conv_w, m_conv_b, m_w_down, m_post_ffn_g, v_pre_mix_g, v_w_in, v_q_norm_g, v_k_norm_g, v_hg_lb, v_hg_out_norm_g, v_w_out, v_post_mix_g, v_pre_x_g, v_mem_norm_g, v_w_xq, v_w_xkv, v_w_xo, v_post_x_g, v_pre_ffn_g, v_w_up, v_conv_w, v_conv_b, v_w_down, v_post_ffn_g):
    args = dict(locals())
    w = {k: args[k] for k in WEIGHTS}
    m = {k: args["m_" + k] for k in WEIGHTS}
    v = {k: args["v_" + k] for k in WEIGHTS}
    chip = 2 * lax.axis_index("x") + lax.axis_index("y")
    core = lax.axis_index("c")

    shards = {k: w[k][0].astype(WIRE_DTYPE) for k in MATS}

    def whole(k, g):
        return g if k in ("w_xkv", "w_up") else g.reshape(-1, g.shape[-1])

    w_in_shards = _gather_shards([shards["w_in"]], "gather_w_in")[0]
    w_in_full = jnp.concatenate([w_in_shards[s] for s in range(4)], axis=1)
    small_in = _exchange_small(_pack_small([w["conv_w"][0], w["hg_lb"]]), False, "gather_small")
    mid_names, ffn_names = ("w_out", "w_xq", "w_xkv", "w_xo"), ("w_up", "w_down")
    mid = _gather_start([shards[k] for k in mid_names], "gather_mid_start", after=(w_in_full, small_in))
    ffn = _gather_start([shards[k] for k in ffn_names], "gather_ffn_start", after=(mid[4],))

    def mid_weights(after):
        return [whole(k, g) for k, g in zip(mid_names, _gather_wait(*mid[:4], after, "gather_mid_wait"))]

    def ffn_weights(after):
        return [whole(k, g) for k, g in zip(ffn_names, _gather_wait(*ffn[:4], after, "gather_ffn_wait"))]

    cw_parts, lb_parts = [], []
    for s in range(4):
        cw_s, lb_s = _unpack_small(small_in[2 * s], [w["conv_w"][0].shape, w["hg_lb"].shape])
        cw_parts.append(cw_s)
        lb_parts.append(lb_s)
    conv_w_full = jnp.concatenate(cw_parts, axis=1)
    hg_lb_full = jnp.concatenate(lb_parts, axis=2)

    started = {}

    def on_grad(k, g):
        if k == "w_in":
            g = g.reshape(g.shape[0], 4, g.shape[1] // 4).transpose(1, 0, 2)
        elif g.ndim == 2:
            g = g.reshape(4, g.shape[0] // 4, g.shape[1])
        *started[k], token = _scatter_start(g, "grad_start_" + k)
        return token

    gains = {k: w[k] for k in GAINS}
    loss_part, grad_x, gs = _local_step(x[0], mem[0], loss_target[0], w_in_full, ffn[4], mid_weights, ffn_weights, on_grad, gains,
                                        conv_w_full, w["conv_b"], hg_lb_full)
    gs["loss"] = loss_part

    grads, delta, new_m, new_v = {}, {}, {}, {}

    def reduce_matrices(names, after, tag):
        sent, landed = _scatter_wait([started[k] for k in names], after, "grad_wait_" + tag)
        halves = [_sum_devices(g, land, chip, core, "grad_sum_" + k) for k, g, land in zip(names, sent, landed)]
        for k, r in zip(names, _join_halves(halves, "grad_join_" + tag)):
            grads[k] = r.reshape(1, -1, r.shape[-1])

    def adamw(names):
        for k in names:
            shape = w[k].shape
            two_d = lambda a: a.reshape(-1, shape[-1])
            d, mo, vo = _adamw(two_d(w[k]), two_d(grads[k]), two_d(m[k]), two_d(v[k]), "adamw_" + k)
            delta[k], new_m[k], new_v[k] = d.reshape(shape), mo.reshape(shape), vo.reshape(shape)
            grads[k] = grads[k].reshape(shape)

    early = tuple(k for k in MATS if k != "w_in")
    reduce_matrices(early, (grad_x,), "early")
    adamw(early)

    small_names = GAINS + ("conv_b", "conv_w", "hg_lb")
    packed = _pack_small([gs[k] for k in small_names + ("loss",)])
    reduced_small = _exchange_small(packed, True, "reduce_small", after=tuple(new_v[k] for k in early))
    *summed, loss = _unpack_small(reduced_small, [gs[k].shape for k in small_names + ("loss",)])
    loss = loss[0, 0]
    for k, g in zip(small_names, summed):
        grads[k] = g
    ncw = w["conv_w"].shape[2]
    grads["conv_w"] = lax.dynamic_slice_in_dim(grads["conv_w"], chip * ncw, ncw, axis=1)[None]
    nlb = w["hg_lb"].shape[2]
    grads["hg_lb"] = lax.dynamic_slice_in_dim(grads["hg_lb"], chip * nlb, nlb, axis=2)
    replicated = GAINS + ("conv_b",)
    shapes = [w[k].shape for k in replicated]
    rows = sum(int(np.prod(s)) for s in shapes) // 128
    pack = lambda d: jnp.concatenate([d[k].reshape(-1) for k in replicated]).reshape(rows, 128)
    outs = _adamw(pack(w), reduced_small[:rows], pack(m), pack(v), "adamw_replicated")
    for into, packed_out in zip((delta, new_m, new_v), outs):
        for k, a in zip(replicated, _unpack_small(packed_out, shapes)):
            into[k] = a
    adamw(("conv_w", "hg_lb"))

    reduce_matrices(("w_in",), tuple(new_v[k] for k in early + small_names), "late")
    adamw(("w_in",))
    return (loss, grad_x[None], *[grads[k] for k in WEIGHTS], *[delta[k] for k in WEIGHTS],
            *[new_m[k] for k in WEIGHTS], *[new_v[k] for k in WEIGHTS])
```

```python
import functools

import numpy as np
import jax
import jax.numpy as jnp
from jax import lax
from jax.experimental import pallas as pl
from jax.experimental.pallas import tpu as pltpu

F32 = jnp.float32
MXU_DTYPE = jnp.bfloat16
WIRE_DTYPE = jnp.bfloat16
VMEM_LIMIT_BYTES = 56 * 1024 * 1024
EPS = 1e-6
MESH = pl.DeviceIdType.MESH

D_MODEL = 1024
GRID_W = 64
ATT_HEADS, ATT_KV_HEADS, ATT_HEAD_DIM = 8, 2, 64
ATT_GROUP = ATT_HEADS // ATT_KV_HEADS
ATT_Q_DIM, ATT_KV_DIM = 512, 128
ROPE_THETA = 10000.0
HG_HEADS, HG_HEAD_DIM, HG_DIM = 4, 128, 512
HG_CHUNK = 128
HG_LEVELS = 7
HG_PAIR = 2 * HG_HEAD_DIM
X_HEADS, X_HEAD_DIM = 4, 256
D_FF = 2816
FF_COLS = 256
FF_BLOCKS = D_FF // FF_COLS
N_IN = 3328
OFF_AQ, OFF_AK, OFF_AV, OFF_HQ, OFF_ZF, OFF_ZB, OFF_HI, OFF_HG = 0, 512, 640, 768, 1280, 1792, 2304, 2816

ADAM_LR, ADAM_B1, ADAM_B2, ADAM_EPS, ADAM_WD, ADAM_STEP = 0.001, 0.9, 0.999, 1e-08, 0.01, 10

SDS = jax.ShapeDtypeStruct


def _cp(*sem):
    return pltpu.CompilerParams(dimension_semantics=sem, vmem_limit_bytes=VMEM_LIMIT_BYTES)


def _dot(a, b, form="nn"):
    dims = {"nn": (((1,), (0,)), ((), ())), "nt": (((1,), (1,)), ((), ())), "tn": (((0,), (0,)), ((), ()))}[form]
    return lax.dot_general(a.astype(MXU_DTYPE), b.astype(MXU_DTYPE), dims, preferred_element_type=F32)


def _sigmoid(x):
    return 1.0 / (1.0 + jnp.exp(-x))


def _rstd(x):
    return lax.rsqrt(jnp.mean(x * x, axis=-1, keepdims=True) + EPS)


def _rms_bwd(x, g, dy):
    r = _rstd(x)
    xh = x * r
    dn = dy * g
    dx = r * (dn - xh * jnp.mean(dn * xh, axis=-1, keepdims=True))
    return dx, jnp.sum(dy * xh, axis=0, keepdims=True)


def _unread(after):
    after = tuple(a for a in after if a is not None)
    return after, [pl.BlockSpec(memory_space=pl.ANY)] * len(after)


def _mm(a, b, form, out_dtype, tm, tn, name, after=()):
    after, after_specs = _unread(after)
    if form == "nn":
        (m, k), n = a.shape, b.shape[1]
    elif form == "nt":
        (m, k), n = a.shape, b.shape[0]
    else:
        (k, m), n = a.shape, b.shape[1]
    tm, tn = min(tm, m), min(tn, n)
    assert m % tm == 0 and n % tn == 0, (name, m, n, tm, tn)

    def body(a_ref, b_ref, *rest):
        o_ref = rest[-1]
        o_ref[...] = _dot(a_ref[...], b_ref[...], form).astype(o_ref.dtype)

    a_spec = pl.BlockSpec((k, tm), lambda i, j: (0, i)) if form == "tn" else pl.BlockSpec((tm, k), lambda i, j: (i, 0))
    b_spec = pl.BlockSpec((tn, k), lambda i, j: (j, 0)) if form == "nt" else pl.BlockSpec((k, tn), lambda i, j: (0, j))
    return pl.pallas_call(
        body, name=name, grid=(m // tm, n // tn), in_specs=[a_spec, b_spec] + after_specs,
        out_specs=pl.BlockSpec((tm, tn), lambda i, j: (i, j)), out_shape=SDS((m, n), out_dtype),
        compiler_params=_cp("parallel", "parallel"))(a, b, *after)


def _mm_nt_parts(a_parts, b, out_dtype, tm, tn, name, after=()):
    after, after_specs = _unread(after)
    parts, n, p = b.shape
    m = a_parts[0][0].shape[0]
    tm, tn = min(tm, m), min(tn, n)
    assert m % tm == 0 and n % tn == 0 and len(a_parts) == parts, (name, m, b.shape)

    def body(*refs):
        o_ref = refs[-1]
        acc = _dot(refs[0][...], refs[parts][0], "nt")
        for s in range(1, parts):
            acc = acc + _dot(refs[s][...], refs[parts + s][0], "nt")
        o_ref[...] = acc.astype(o_ref.dtype)

    a_specs = [pl.BlockSpec((tm, p), lambda i, j, cb=cb: (i, cb)) for _, cb in a_parts]
    b_specs = [pl.BlockSpec((1, tn, p), lambda i, j, s=s: (s, j, 0)) for s in range(parts)]
    return pl.pallas_call(
        body, name=name, grid=(m // tm, n // tn), in_specs=a_specs + b_specs + after_specs,
        out_specs=pl.BlockSpec((tm, tn), lambda i, j: (i, j)), out_shape=SDS((m, n), out_dtype),
        compiler_params=_cp("parallel", "parallel"))(*[arr for arr, _ in a_parts], *([b] * parts), *after)


def _dw_by_owner(a, b, tn, first, into, tm, name):
    k, m = a.shape
    cnt = b.shape[1] // tn
    tm = min(tm, m)
    assert m % tm == 0 and b.shape[1] == cnt * tn and first + cnt <= 4, (name, a.shape, b.shape)

    def body(a_ref, b_ref, *rest):
        rest[-1][0] = _dot(a_ref[...], b_ref[...], "tn").astype(rest[-1].dtype)

    extra = [] if into is None else [into]
    return pl.pallas_call(
        body, name=name, grid=(m // tm, cnt),
        in_specs=[pl.BlockSpec((k, tm), lambda i, j: (0, i)), pl.BlockSpec((k, tn), lambda i, j: (0, j))] + [pl.BlockSpec(memory_space=pl.ANY)] * len(extra),
        out_specs=pl.BlockSpec((1, tm, tn), lambda i, j: (first + j, i, 0)), out_shape=SDS((4, m, tn), WIRE_DTYPE),
        input_output_aliases={2: 0} if extra else {},
        compiler_params=_cp("parallel", "parallel"))(a, b, *extra)


def _norm_mm(x, g, w, out_dtype, tm, tn, name, after=()):
    after, after_specs = _unread(after)
    m, d = x.shape
    sharded = w.ndim == 3
    n = w.shape[-1] * (w.shape[0] if sharded else 1)
    tm, tn = min(tm, m), (w.shape[-1] if sharded else min(tn, n))
    assert m % tm == 0 and n % tn == 0, (name, m, n, tm, tn)

    def body(x_ref, g_ref, w_ref, *rest):
        o_ref, h_ref, hs = rest[-3:]

        @pl.when(pl.program_id(1) == 0)
        def _():
            xv = x_ref[...]
            h = (xv * _rstd(xv) * g_ref[...]).astype(MXU_DTYPE)
            hs[...] = h
            h_ref[...] = h

        o_ref[...] = _dot(hs[...], w_ref[0] if sharded else w_ref[...]).astype(o_ref.dtype)

    w_spec = pl.BlockSpec((1, d, tn), lambda i, j: (j, 0, 0)) if sharded else pl.BlockSpec((d, tn), lambda i, j: (0, j))
    return pl.pallas_call(
        body, name=name, grid=(m // tm, n // tn),
        in_specs=[pl.BlockSpec((tm, d), lambda i, j: (i, 0)), pl.BlockSpec((1, d), lambda i, j: (0, 0)), w_spec] + after_specs,
        out_specs=[pl.BlockSpec((tm, tn), lambda i, j: (i, j)), pl.BlockSpec((tm, d), lambda i, j: (i, 0))],
        out_shape=[SDS((m, n), out_dtype), SDS((m, d), MXU_DTYPE)],
        scratch_shapes=[pltpu.VMEM((tm, d), MXU_DTYPE)],
        compiler_params=_cp("parallel", "arbitrary"))(x, g, w, *after)


ROW_TILE = 256
TOKEN_TILE = 1024


def _resid_norm(x, y, g, name):
    n, d = x.shape
    tr = min(ROW_TILE, n)

    def body(x_ref, y_ref, g_ref, o_ref):
        yv = y_ref[...]
        o_ref[...] = x_ref[...] + yv * _rstd(yv) * g_ref[...]

    row = pl.BlockSpec((tr, d), lambda i: (i, 0))
    return pl.pallas_call(
        body, name=name, grid=(n // tr,), in_specs=[row, row, pl.BlockSpec((1, d), lambda i: (0, 0))],
        out_specs=row, out_shape=SDS((n, d), F32), compiler_params=_cp("parallel"))(x, y, g)


def _norm_bwd(x, g, dy, res, out_dtype, name):
    n, d = x.shape
    tr = min(ROW_TILE, n)
    has_res = res is not None

    def body(*refs):
        x_ref, g_ref, dy_ref = refs[:3]
        dx_ref, dg_ref = refs[-2:]
        dx, dg = _rms_bwd(x_ref[...], g_ref[...], dy_ref[...].astype(F32))
        if has_res:
            dx = dx + refs[3][...]
        dx_ref[...] = dx.astype(dx_ref.dtype)

        @pl.when(pl.program_id(0) == 0)
        def _():
            dg_ref[...] = jnp.zeros_like(dg_ref)

        dg_ref[...] += dg

    row = pl.BlockSpec((tr, d), lambda i: (i, 0))
    vec = pl.BlockSpec((1, d), lambda i: (0, 0))
    ins = [x, g, dy] + ([res] if has_res else [])
    return pl.pallas_call(
        body, name=name, grid=(n // tr,), in_specs=[row, vec, row] + ([row] if has_res else []),
        out_specs=[row, vec], out_shape=[SDS((n, d), out_dtype), SDS((1, d), F32)],
        compiler_params=_cp("arbitrary"))(*ins)


def _final_loss(x, y, g, target, name):
    n, d = x.shape
    tr = min(ROW_TILE, n)

    def body(x_ref, y_ref, g_ref, t_ref, d_ref, l_ref):
        yv = y_ref[...]
        diff = x_ref[...] + yv * _rstd(yv) * g_ref[...] - t_ref[...]
        d_ref[...] = diff * (1.0 / d)

        @pl.when(pl.program_id(0) == 0)
        def _():
            l_ref[...] = jnp.zeros_like(l_ref)

        l_ref[...] += 0.5 * jnp.sum(jnp.mean(diff * diff, axis=-1, keepdims=True), axis=0, keepdims=True)

    row = pl.BlockSpec((tr, d), lambda i: (i, 0))
    return pl.pallas_call(
        body, name=name, grid=(n // tr,), in_specs=[row, row, pl.BlockSpec((1, d), lambda i: (0, 0)), row],
        out_specs=[row, pl.BlockSpec((1, 1), lambda i: (0, 0))], out_shape=[SDS((n, d), F32), SDS((1, 1), F32)],
        compiler_params=_cp("arbitrary"))(x, y, g, target)


def _rope_tables(n):
    pairs = ATT_HEAD_DIM // 4
    t = np.arange(n)
    inv = np.power(ROPE_THETA, -np.arange(pairs, dtype=np.float32) / pairs).astype(np.float32)
    ang = np.concatenate([(t // GRID_W)[:, None].astype(np.float32) * inv, (t % GRID_W)[:, None].astype(np.float32) * inv], axis=-1)
    cos = np.repeat(np.cos(ang), 2, axis=-1)
    sin = np.repeat(np.sin(ang), 2, axis=-1) * np.tile(np.array([-1.0, 1.0], np.float32), ATT_HEAD_DIM // 2)
    return jnp.asarray(np.tile(cos, 2), F32), jnp.asarray(np.tile(sin, 2), F32)


def _swap_pairs(x):
    lane = lax.broadcasted_iota(jnp.int32, x.shape, 1)
    return jnp.where((lane & 1) == 0, pltpu.roll(x, 127, axis=1), pltpu.roll(x, 1, axis=1))


def _head_mean(v):
    lane = lax.broadcasted_iota(jnp.int32, v.shape, 1)
    lo = jnp.where(lane < ATT_HEAD_DIM, v, 0.0)
    s0 = jnp.sum(lo, axis=-1, keepdims=True)
    s1 = jnp.sum(v - lo, axis=-1, keepdims=True)
    return jnp.where(lane < ATT_HEAD_DIM, s0, s1) * (1.0 / ATT_HEAD_DIM)


def _qk_prep(p, gq, gk, cos, sin, name):
    n = p.shape[0]
    tr = min(ROW_TILE, n)

    def one(xv, g, c, s):
        xn = xv * lax.rsqrt(_head_mean(xv * xv) + EPS) * g
        return xn * c + _swap_pairs(xn) * s

    def body(q_ref, k_ref, gq_ref, gk_ref, c_ref, s_ref, qo_ref, ko_ref):
        c, s = c_ref[...], s_ref[...]
        for j in range(ATT_Q_DIM // 128):
            qo_ref[:, j * 128:(j + 1) * 128] = one(q_ref[:, j * 128:(j + 1) * 128], gq_ref[...], c, s).astype(qo_ref.dtype)
        ko_ref[...] = one(k_ref[...], gk_ref[...], c, s).astype(ko_ref.dtype)

    vec = pl.BlockSpec((1, 128), lambda i: (0, 0))
    tab = pl.BlockSpec((tr, 128), lambda i: (i, 0))
    return pl.pallas_call(
        body, name=name, grid=(n // tr,),
        in_specs=[pl.BlockSpec((tr, ATT_Q_DIM), lambda i: (i, 0)), pl.BlockSpec((tr, 128), lambda i: (i, OFF_AK // 128)), vec, vec, tab, tab],
        out_specs=[pl.BlockSpec((tr, ATT_Q_DIM), lambda i: (i, 0)), tab],
        out_shape=[SDS((n, ATT_Q_DIM), MXU_DTYPE), SDS((n, ATT_KV_DIM), MXU_DTYPE)],
        compiler_params=_cp("parallel"))(p, p, gq, gk, cos, sin)


def _qk_prep_bwd(p, gq, gk, cos, sin, dq, dk, name):
    n = p.shape[0]
    tr = min(ROW_TILE, n)

    def one(xv, g, c, s, dout):
        dxn = dout * c + _swap_pairs(dout * s)
        r = lax.rsqrt(_head_mean(xv * xv) + EPS)
        xh = xv * r
        dn = dxn * g
        dx = r * (dn - xh * _head_mean(dn * xh))
        return dx, jnp.sum(dxn * xh, axis=0, keepdims=True)

    def body(q_ref, k_ref, gq_ref, gk_ref, c_ref, s_ref, dq_ref, dk_ref, dqo_ref, dko_ref, dgq_ref, dgk_ref):
        @pl.when(pl.program_id(0) == 0)
        def _():
            dgq_ref[...] = jnp.zeros_like(dgq_ref)
            dgk_ref[...] = jnp.zeros_like(dgk_ref)

        c, s = c_ref[...], s_ref[...]
        for j in range(ATT_Q_DIM // 128):
            sl = slice(j * 128, (j + 1) * 128)
            dx, dg = one(q_ref[:, sl], gq_ref[...], c, s, dq_ref[:, sl])
            dqo_ref[:, sl] = dx.astype(dqo_ref.dtype)
            dgq_ref[:, sl] += dg
        dx, dg = one(k_ref[...], gk_ref[...], c, s, dk_ref[...])
        dko_ref[...] = dx.astype(dko_ref.dtype)
        dgk_ref[...] += dg

    vec = pl.BlockSpec((1, 128), lambda i: (0, 0))
    tab = pl.BlockSpec((tr, 128), lambda i: (i, 0))
    qrow = pl.BlockSpec((tr, ATT_Q_DIM), lambda i: (i, 0))
    return pl.pallas_call(
        body, name=name, grid=(n // tr,),
        in_specs=[qrow, pl.BlockSpec((tr, 128), lambda i: (i, OFF_AK // 128)), vec, vec, tab, tab, qrow, tab],
        out_specs=[qrow, tab, pl.BlockSpec((1, ATT_Q_DIM), lambda i: (0, 0)), vec],
        out_shape=[SDS((n, ATT_Q_DIM), MXU_DTYPE), SDS((n, ATT_KV_DIM), MXU_DTYPE), SDS((1, ATT_Q_DIM), F32), SDS((1, 128), F32)],
        compiler_params=_cp("arbitrary"))(p, p, gq, gk, cos, sin, dq, dk)


ATT_TQ = 256


def _attn_fwd(q, k, v, name):
    n = q.shape[0]
    tq = min(ATT_TQ, n)
    scale = ATT_HEAD_DIM ** -0.5
    gw = ATT_GROUP * ATT_HEAD_DIM

    def body(q_ref, k_ref, v_ref, o_ref):
        kk, vv = k_ref[0], v_ref[0]
        outs = []
        for g in range(ATT_GROUP):
            s = _dot(q_ref[:, g * ATT_HEAD_DIM:(g + 1) * ATT_HEAD_DIM] * scale, kk, "nt")
            e = jnp.exp(s - jnp.max(s, axis=-1, keepdims=True))
            outs.append(_dot(e, vv) / jnp.sum(e, axis=-1, keepdims=True))
        o_ref[...] = jnp.concatenate(outs, axis=-1).astype(o_ref.dtype)

    kv = pl.BlockSpec((1, n, ATT_HEAD_DIM), lambda h, i: (h, 0, 0))
    return pl.pallas_call(
        body, name=name, grid=(ATT_KV_HEADS, n // tq),
        in_specs=[pl.BlockSpec((tq, gw), lambda h, i: (i, h)), kv, kv],
        out_specs=pl.BlockSpec((tq, gw), lambda h, i: (i, h)), out_shape=SDS((n, ATT_Q_DIM), MXU_DTYPE),
        compiler_params=_cp("parallel", "parallel"))(q, k, v)


def _attn_bwd(q, k, v, do, name):
    n = q.shape[0]
    tq = min(ATT_TQ, n)
    scale = ATT_HEAD_DIM ** -0.5
    gw = ATT_GROUP * ATT_HEAD_DIM

    def body(q_ref, k_ref, v_ref, do_ref, dq_ref, dk_ref, dv_ref):
        @pl.when(pl.program_id(1) == 0)
        def _():
            dk_ref[...] = jnp.zeros_like(dk_ref)
            dv_ref[...] = jnp.zeros_like(dv_ref)

        kk, vv = k_ref[0], v_ref[0]
        dqs = []
        dk_acc = jnp.zeros((ATT_HEAD_DIM, n), F32)
        dv_acc = jnp.zeros((ATT_HEAD_DIM, n), F32)
        for g in range(ATT_GROUP):
            sl = slice(g * ATT_HEAD_DIM, (g + 1) * ATT_HEAD_DIM)
            qg, dog = q_ref[:, sl] * scale, do_ref[:, sl].astype(F32)
            s = _dot(qg, kk, "nt")
            e = jnp.exp(s - jnp.max(s, axis=-1, keepdims=True))
            inv = 1.0 / jnp.sum(e, axis=-1, keepdims=True)
            delta = jnp.sum(dog * (_dot(e, vv) * inv), axis=-1, keepdims=True)
            dse = e * (_dot(dog, vv, "nt") - delta)
            dqs.append(_dot(dse, kk) * (inv * scale))
            dk_acc += _dot(qg.astype(F32) * inv, dse, "tn")
            dv_acc += _dot(dog * inv, e, "tn")
        dq_ref[...] = jnp.concatenate(dqs, axis=-1)
        dk_ref[0] += dk_acc
        dv_ref[0] += dv_acc

    kv = pl.BlockSpec((1, n, ATT_HEAD_DIM), lambda h, i: (h, 0, 0))
    kvt = pl.BlockSpec((1, ATT_HEAD_DIM, n), lambda h, i: (h, 0, 0))
    qb = pl.BlockSpec((tq, gw), lambda h, i: (i, h))
    return pl.pallas_call(
        body, name=name, grid=(ATT_KV_HEADS, n // tq), in_specs=[qb, kv, kv, qb], out_specs=[qb, kvt, kvt],
        out_shape=[SDS((n, ATT_Q_DIM), F32), SDS((ATT_KV_HEADS, ATT_HEAD_DIM, n), F32), SDS((ATT_KV_HEADS, ATT_HEAD_DIM, n), F32)],
        compiler_params=_cp("parallel", "arbitrary"))(q, k, v, do)


def _both_directions(mats, axis):
    fwd = np.concatenate(mats, axis=axis).astype(np.float32)
    bwd = np.concatenate([m[::-1, ::-1] for m in mats], axis=axis).astype(np.float32)
    return jnp.asarray(np.stack([fwd, bwd]), MXU_DTYPE)


def _hg_segments():
    c = HG_CHUNK
    t = np.arange(c)[:, None]
    r = np.arange(c)[None, :]
    mats = [(r <= t)]
    for lev in range(HG_LEVELS):
        h = c >> (lev + 1)
        mid = (t // (2 * h)) * (2 * h) + h - 1
        hi = (t // h) % 2 == 1
        mats.append(np.where(hi, (r > mid) & (r <= t), (r > t) & (r <= mid)))
    mats.append(r > t)
    return _both_directions(mats, 0)


def _hg_pair_sums():
    c = HG_CHUNK
    r = np.arange(c)[:, None]
    t = np.arange(c)[None, :]
    gp, gn = [t >= r], [t < r]
    for lev in range(HG_LEVELS):
        sh = HG_LEVELS - 1 - lev
        same = (r >> sh) == (t >> sh)
        gp.append(same & (t >= r))
        gn.append(same & (t < r))
    return _both_directions(gp, 1), _both_directions(gn, 1)


def _split_dot(mat, x):
    hi = x.astype(MXU_DTYPE)
    lo = (x - hi.astype(F32)).astype(MXU_DTYPE)
    return _dot(mat, hi) + _dot(mat, lo)


def _hg_gates(hq, z, a0, a1):
    q = hq * _sigmoid(hq)
    sg = _sigmoid(z)
    lb = _sigmoid(a0 - a1)
    f = lb + (1.0 - lb) * sg
    k = (1.0 - lb) * (1.0 - sg)
    return q, f, k, sg, lb


def _hg_level_masks(mirrored):
    c = HG_CHUNK
    row = lax.broadcasted_iota(jnp.int32, (c, 1), 0)
    rr = lax.broadcasted_iota(jnp.int32, (c, c), 0)
    cc = lax.broadcasted_iota(jnp.int32, (c, c), 1)
    his, sames = [], []
    for lev in range(HG_LEVELS):
        sh = HG_LEVELS - 1 - lev
        his.append(jnp.logical_xor(((row >> sh) & 1) == 1, mirrored))
        sames.append((rr >> (sh + 1)) == (cc >> (sh + 1)))
    return his, sames, rr == cc


def _hg_intra(q, k, ex, masks):
    his, sames, eye = masks
    a = jnp.where(eye, jnp.sum(q * k, axis=-1, keepdims=True), 0.0)
    for lev in range(HG_LEVELS):
        e = ex[lev + 1]
        qs = jnp.where(his[lev], q * e, 0.0)
        ks = jnp.where(his[lev], 0.0, k * e)
        a = a + jnp.where(sames[lev], _dot(qs, ks, "nt"), 0.0)
    return a


def _hg_specs(n, with_time):
    c = HG_CHUNK
    nc = n // c

    def chunk(d, i):
        first = d if with_time else 1 - d
        return i + first * (nc - 1 - 2 * i)

    def pcols(off, dir_stride=0):
        return [pl.BlockSpec((c, HG_PAIR), lambda d, i, j=j: (chunk(d, i), off // HG_PAIR + dir_stride // HG_PAIR * d + j)) for j in range(2)]

    specs = dict(
        hq=pcols(OFF_HQ), v=pcols(OFF_HI), z=pcols(OFF_ZF, OFF_ZB - OFF_ZF),
        shared=pl.BlockSpec((c, HG_DIM), lambda d, i: (chunk(d, i), 0)),
        per_dir=pl.BlockSpec((1, c, HG_DIM), lambda d, i: (d, chunk(d, i), 0)),
        vec=pl.BlockSpec((1, 1, HG_DIM), lambda d, i: (d, 0, 0)),
        seg=pl.BlockSpec((1, (HG_LEVELS + 2) * c, c), lambda d, i: (d, 0, 0)),
        sums=pl.BlockSpec((1, c, (HG_LEVELS + 1) * c), lambda d, i: (d, 0, 0)),
        state=pl.BlockSpec((1, HG_HEADS, 1, HG_HEAD_DIM, HG_HEAD_DIM), lambda d, i: (d, 0, chunk(d, i), 0, 0)))
    return nc, specs


def _hg_head(refs, hh):
    off = (hh % 2) * HG_HEAD_DIM
    return refs[hh // 2][:, off:off + HG_HEAD_DIM]


def _hg_lanes(hh):
    return slice(hh * HG_HEAD_DIM, (hh + 1) * HG_HEAD_DIM)


def _hg_exps(seg_ref, f):
    lf = jnp.log(f)
    args = _split_dot(seg_ref[0], lf)
    c = HG_CHUNK
    return [jnp.exp(args[j * c:(j + 1) * c]) for j in range(HG_LEVELS + 2)]


def _hg_last_row(a, mirrored):
    return jnp.where(mirrored, a[0:1, :], a[HG_CHUNK - 1:HG_CHUNK, :])


def _hgrn_fwd(p, a0, a1, seg, name):
    n = p.shape[0]
    nc, sp = _hg_specs(n, True)

    def body(hq0, hq1, z0, z1, v0, v1, a0_ref, a1_ref, seg_ref, o_ref, st):
        @pl.when(pl.program_id(1) == 0)
        def _():
            st[...] = jnp.zeros_like(st)

        mirrored = pl.program_id(0) == 1
        masks = _hg_level_masks(mirrored)
        for hh in range(HG_HEADS):
            ln = _hg_lanes(hh)
            q, f, k, _, _ = _hg_gates(_hg_head((hq0, hq1), hh), _hg_head((z0, z1), hh), a0_ref[0, :, ln], a1_ref[0, :, ln])
            vv = _hg_head((v0, v1), hh)
            ex = _hg_exps(seg_ref, f)
            a = _hg_intra(q, k, ex, masks)
            s_t = st[hh]
            o_ref[0, :, ln] = _dot(a, vv) + _dot(q * ex[0], s_t, "nt")
            st[hh] = s_t * _hg_last_row(ex[0], mirrored) + _dot(vv, k * ex[HG_LEVELS + 1], "tn")

    return pl.pallas_call(
        body, name=name, grid=(2, nc), in_specs=sp["hq"] + sp["z"] + sp["v"] + [sp["vec"], sp["vec"], sp["seg"]],
        out_specs=sp["per_dir"], out_shape=SDS((2, n, HG_DIM), F32),
        scratch_shapes=[pltpu.VMEM((HG_HEADS, HG_HEAD_DIM, HG_HEAD_DIM), F32)],
        compiler_params=_cp("parallel", "arbitrary"))(p, p, p, p, p, p, a0, a1, seg)


def _hgrn_bwd_q(p, a0, a1, seg, gp, do, name):
    n = p.shape[0]
    nc, sp = _hg_specs(n, True)


    def body(hq0, hq1, z0, z1, v0, v1, a0_ref, a1_ref, seg_ref, gp_ref, do_ref, dhq_ref, dlf_ref, s0_ref, st):
        @pl.when(pl.program_id(1) == 0)
        def _():
            st[...] = jnp.zeros_like(st)

        mirrored = pl.program_id(0) == 1
        his, sames, eye = _hg_level_masks(mirrored)
        for hh in range(HG_HEADS):
            ln = _hg_lanes(hh)
            s_t = st[hh]
            s0_ref[0, hh, 0] = s_t
            hqv = _hg_head((hq0, hq1), hh)
            q, f, k, _, _ = _hg_gates(hqv, _hg_head((z0, z1), hh), a0_ref[0, :, ln], a1_ref[0, :, ln])
            vv, dov = _hg_head((v0, v1), hh), do_ref[:, ln]
            ex = _hg_exps(seg_ref, f)
            da = _dot(dov, vv, "nt")
            dq_inter = ex[0] * _dot(dov, s_t)
            dq = jnp.sum(dov * vv, axis=-1, keepdims=True) * k + dq_inter
            terms = [q * dq_inter]
            for lev in range(HG_LEVELS):
                e = ex[lev + 1]
                ks = jnp.where(his[lev], 0.0, k * e)
                part = jnp.where(his[lev], e, 0.0) * _dot(jnp.where(sames[lev], da, 0.0), ks)
                dq = dq + part
                terms.append(q * part)
            dlf_ref[0, :, ln] = _dot(gp_ref[0], jnp.concatenate(terms, axis=0))
            sq = _sigmoid(hqv)
            dhq_ref[0, :, ln] = dq * sq * (1.0 + hqv * (1.0 - sq))
            st[hh] = s_t * _hg_last_row(ex[0], mirrored) + _dot(vv, k * ex[HG_LEVELS + 1], "tn")

    out = SDS((2, n, HG_DIM), F32)
    return pl.pallas_call(
        body, name=name, grid=(2, nc),
        in_specs=sp["hq"] + sp["z"] + sp["v"] + [sp["vec"], sp["vec"], sp["seg"], sp["sums"], sp["shared"]],
        out_specs=[sp["per_dir"], sp["per_dir"], sp["state"]],
        out_shape=[out, out, SDS((2, HG_HEADS, nc, HG_HEAD_DIM, HG_HEAD_DIM), F32)],
        scratch_shapes=[pltpu.VMEM((HG_HEADS, HG_HEAD_DIM, HG_HEAD_DIM), F32)],
        compiler_params=_cp("parallel", "arbitrary"))(p, p, p, p, p, p, a0, a1, seg, gp, do)


def _hgrn_bwd_kv(p, a0, a1, seg, gn, do, dlf_q, s0, name):
    n = p.shape[0]
    nc, sp = _hg_specs(n, False)

    def body(hq0, hq1, z0, z1, v0, v1, a0_ref, a1_ref, seg_ref, gn_ref, do_ref, dlfq_ref, s0_ref, dz_ref, dv_ref, dlb_ref, rt):
        @pl.when(pl.program_id(1) == 0)
        def _():
            rt[...] = jnp.zeros_like(rt)
            dlb_ref[...] = jnp.zeros_like(dlb_ref)

        mirrored = pl.program_id(0) == 1
        masks = _hg_level_masks(mirrored)
        his, sames, eye = masks
        for hh in range(HG_HEADS):
            ln = _hg_lanes(hh)
            q, f, k, sg, lb = _hg_gates(_hg_head((hq0, hq1), hh), _hg_head((z0, z1), hh), a0_ref[0, :, ln], a1_ref[0, :, ln])
            vv, dov = _hg_head((v0, v1), hh), do_ref[:, ln]
            ex = _hg_exps(seg_ref, f)
            a = _hg_intra(q, k, ex, masks)
            da = _dot(dov, vv, "nt")
            r_t = rt[hh]
            k_end = k * ex[HG_LEVELS + 1]
            dv_ref[0, :, ln] = _dot(a, dov, "tn") + _dot(k_end, r_t, "nt")
            dk_inter = ex[HG_LEVELS + 1] * _dot(vv, r_t)
            dk = jnp.sum(dov * vv, axis=-1, keepdims=True) * q + dk_inter
            terms = [k * dk_inter]
            for lev in range(HG_LEVELS):
                e = ex[lev + 1]
                qs = jnp.where(his[lev], q * e, 0.0)
                part = jnp.where(his[lev], 0.0, e) * _dot(jnp.where(sames[lev], da, 0.0), qs, "tn")
                dk = dk + part
                terms.append(k * part)
            decay = _hg_last_row(ex[0], mirrored)
            rt[hh] = r_t * decay + _dot(dov, q * ex[0], "tn")
            later = decay * jnp.sum(s0_ref[0, hh, 0] * r_t, axis=0, keepdims=True)
            dlf = dlfq_ref[0, :, ln] + _dot(gn_ref[0], jnp.concatenate(terms, axis=0)) + later
            df = dlf / f - dk
            dz_ref[0, :, ln] = df * (1.0 - lb) * sg * (1.0 - sg)
            dlb_ref[0, :, ln] += jnp.sum(df * (1.0 - sg), axis=0, keepdims=True)

    out = SDS((2, n, HG_DIM), F32)
    return pl.pallas_call(
        body, name=name, grid=(2, nc),
        in_specs=sp["hq"] + sp["z"] + sp["v"] + [sp["vec"], sp["vec"], sp["seg"], sp["sums"], sp["shared"], sp["per_dir"], sp["state"]],
        out_specs=[sp["per_dir"], sp["per_dir"], sp["vec"]], out_shape=[out, out, SDS((2, 1, HG_DIM), F32)],
        scratch_shapes=[pltpu.VMEM((HG_HEADS, HG_HEAD_DIM, HG_HEAD_DIM), F32)],
        compiler_params=_cp("parallel", "arbitrary"))(p, p, p, p, p, p, a0, a1, seg, gn, do, dlf_q, s0)


def _hg_post(o2, p, g, name):
    n = p.shape[0]
    tr = min(ROW_TILE, n)
    w = 2 * HG_HEAD_DIM

    def body(of_ref, ob_ref, hg_ref, g_ref, o_ref):
        for j in range(2):
            sl = slice(j * HG_HEAD_DIM, (j + 1) * HG_HEAD_DIM)
            o = of_ref[0, :, sl] + ob_ref[0, :, sl]
            hg = hg_ref[:, sl]
            o_ref[:, sl] = (o * _rstd(o) * g_ref[...] * (hg * _sigmoid(hg))).astype(o_ref.dtype)

    blk = pl.BlockSpec((tr, w), lambda i, j: (i, j))
    dirs = [pl.BlockSpec((1, tr, w), lambda i, j, d=d: (d, i, j)) for d in range(2)]
    return pl.pallas_call(
        body, name=name, grid=(n // tr, HG_DIM // w),
        in_specs=dirs + [pl.BlockSpec((tr, w), lambda i, j: (i, OFF_HG // w + j)), pl.BlockSpec((1, HG_HEAD_DIM), lambda i, j: (0, 0))],
        out_specs=blk, out_shape=SDS((n, HG_DIM), MXU_DTYPE), compiler_params=_cp("parallel", "parallel"))(o2, o2, p, g)


def _hg_post_bwd(o2, p, g, dcat, name):
    n = p.shape[0]
    tr = min(ROW_TILE, n)
    w = 2 * HG_HEAD_DIM

    def body(of_ref, ob_ref, hg_ref, g_ref, d_ref, do_ref, dhg_ref, dg_ref):
        @pl.when(pl.program_id(1) == 0)
        def _():
            dg_ref[...] = jnp.zeros_like(dg_ref)

        for j in range(2):
            sl = slice(j * HG_HEAD_DIM, (j + 1) * HG_HEAD_DIM)
            o = of_ref[0, :, sl] + ob_ref[0, :, sl]
            hg = hg_ref[:, sl]
            d = d_ref[:, sl].astype(F32)
            sg = _sigmoid(hg)
            on = o * _rstd(o) * g_ref[...]
            dhg_ref[:, sl] = (d * on * sg * (1.0 + hg * (1.0 - sg))).astype(dhg_ref.dtype)
            dx, dg = _rms_bwd(o, g_ref[...], d * hg * sg)
            do_ref[:, sl] = dx
            dg_ref[0, :, sl] += dg

    blk = pl.BlockSpec((tr, w), lambda j, i: (i, j))
    dirs = [pl.BlockSpec((1, tr, w), lambda j, i, d=d: (d, i, j)) for d in range(2)]
    return pl.pallas_call(
        body, name=name, grid=(HG_DIM // w, n // tr),
        in_specs=dirs + [pl.BlockSpec((tr, w), lambda j, i: (i, OFF_HG // w + j)), pl.BlockSpec((1, HG_HEAD_DIM), lambda j, i: (0, 0)),
                         pl.BlockSpec((tr, w), lambda j, i: (i, ATT_Q_DIM // w + j))],
        out_specs=[blk, blk, pl.BlockSpec((1, 1, w), lambda j, i: (j, 0, 0))],
        out_shape=[SDS((n, HG_DIM), F32), SDS((n, HG_DIM), MXU_DTYPE), SDS((HG_DIM // w, 1, w), F32)],
        compiler_params=_cp("parallel", "arbitrary"))(o2, o2, p, g, dcat)


XATT_TQ = 512


def _xattn_fwd(q, kv, name):
    n, nm = q.shape[0], kv.shape[0]
    tq = min(XATT_TQ, n)
    scale = X_HEAD_DIM ** -0.5

    def body(q_ref, k_ref, v_ref, o_ref):
        s = _dot(q_ref[...], k_ref[...], "nt") * scale
        e = jnp.exp(s - jnp.max(s, axis=-1, keepdims=True))
        o_ref[...] = _dot(e / jnp.sum(e, axis=-1, keepdims=True), v_ref[...]).astype(o_ref.dtype)

    qb = pl.BlockSpec((tq, X_HEAD_DIM), lambda h, i: (i, h))
    return pl.pallas_call(
        body, name=name, grid=(X_HEADS, n // tq),
        in_specs=[qb, pl.BlockSpec((nm, X_HEAD_DIM), lambda h, i: (0, h)), pl.BlockSpec((nm, X_HEAD_DIM), lambda h, i: (0, X_HEADS + h))],
        out_specs=qb, out_shape=SDS(q.shape, MXU_DTYPE), compiler_params=_cp("parallel", "parallel"))(q, kv, kv)


def _xattn_bwd(q, kv, do, name):
    n, nm = q.shape[0], kv.shape[0]
    tq = min(XATT_TQ, n)
    scale = X_HEAD_DIM ** -0.5

    def body(q_ref, k_ref, v_ref, do_ref, dq_ref, dk_ref, dv_ref):
        @pl.when(pl.program_id(1) == 0)
        def _():
            dk_ref[...] = jnp.zeros_like(dk_ref)
            dv_ref[...] = jnp.zeros_like(dv_ref)

        qv, dov = q_ref[...], do_ref[...]
        s = _dot(qv, k_ref[...], "nt") * scale
        e = jnp.exp(s - jnp.max(s, axis=-1, keepdims=True))
        p = e / jnp.sum(e, axis=-1, keepdims=True)
        dp = _dot(dov, v_ref[...], "nt")
        ds = p * (dp - jnp.sum(p * dp, axis=-1, keepdims=True)) * scale
        dq_ref[...] = _dot(ds, k_ref[...]).astype(dq_ref.dtype)
        dk_ref[...] += _dot(ds, qv, "tn")
        dv_ref[...] += _dot(p, dov, "tn")

    qb = pl.BlockSpec((tq, X_HEAD_DIM), lambda h, i: (i, h))
    kb = pl.BlockSpec((nm, X_HEAD_DIM), lambda h, i: (0, h))
    return pl.pallas_call(
        body, name=name, grid=(X_HEADS, n // tq),
        in_specs=[qb, kb, pl.BlockSpec((nm, X_HEAD_DIM), lambda h, i: (0, X_HEADS + h)), qb], out_specs=[qb, kb, kb],
        out_shape=[SDS(q.shape, MXU_DTYPE), SDS((nm, X_HEADS * X_HEAD_DIM), F32), SDS((nm, X_HEADS * X_HEAD_DIM), F32)],
        compiler_params=_cp("parallel", "arbitrary"))(q, kv, kv, do)


def _shift_rows(u, down):
    n = u.shape[0]
    row = lax.broadcasted_iota(jnp.int32, u.shape, 0)
    if down:
        return jnp.where(row == 0, 0.0, pltpu.roll(u, 1, axis=0))
    return jnp.where(row == n - 1, 0.0, pltpu.roll(u, n - 1, axis=0))


def _conv(u, w, b):
    return b + _shift_rows(u, True) * w[0:1, :] + u * w[1:2, :] + _shift_rows(u, False) * w[2:3, :]


def _ff_specs(n):
    gate = lambda rows: pl.BlockSpec((rows, FF_COLS), lambda j: (0, j))
    val = lambda rows: pl.BlockSpec((rows, FF_COLS), lambda j: (0, FF_BLOCKS + j))
    return [gate(n), val(n), gate(3), val(3), gate(1), val(1)], gate


def _conv_gate(u, cw, cb, name):
    n = u.shape[0]
    ins, gate_blk = _ff_specs(n)

    def body(ug_ref, uv_ref, wg_ref, wv_ref, bg_ref, bv_ref, o_ref):
        gate = _conv(ug_ref[...], wg_ref[...], bg_ref[...])
        val = _conv(uv_ref[...], wv_ref[...], bv_ref[...])
        o_ref[...] = (gate * _sigmoid(gate) * val).astype(o_ref.dtype)

    return pl.pallas_call(
        body, name=name, grid=(FF_BLOCKS,), in_specs=ins, out_specs=gate_blk(n), out_shape=SDS((n, D_FF), MXU_DTYPE),
        compiler_params=_cp("parallel"))(u, u, cw, cw, cb, cb)


def _conv_gate_bwd(u, cw, cb, da, name):
    n = u.shape[0]
    ins, gate_blk = _ff_specs(n)

    def side(dacc, u, w, du_ref, dw_ref, db_ref):
        nxt, prv = _shift_rows(dacc, False), _shift_rows(dacc, True)
        du_ref[...] = (nxt * w[0:1, :] + dacc * w[1:2, :] + prv * w[2:3, :]).astype(du_ref.dtype)
        db_ref[...] = jnp.sum(dacc, axis=0, keepdims=True)
        dw_ref[0:1, :] = jnp.sum(nxt * u, axis=0, keepdims=True)
        dw_ref[1:2, :] = jnp.sum(dacc * u, axis=0, keepdims=True)
        dw_ref[2:3, :] = jnp.sum(prv * u, axis=0, keepdims=True)

    def body(ug_ref, uv_ref, wg_ref, wv_ref, bg_ref, bv_ref, da_ref, dug_ref, duv_ref, dwg_ref, dwv_ref, dbg_ref, dbv_ref):
        ug, uv = ug_ref[...], uv_ref[...]
        gate = _conv(ug, wg_ref[...], bg_ref[...])
        val = _conv(uv, wv_ref[...], bv_ref[...])
        sg = _sigmoid(gate)
        dav = da_ref[...].astype(F32)
        side(dav * val * sg * (1.0 + gate * (1.0 - sg)), ug, wg_ref[...], dug_ref, dwg_ref, dbg_ref)
        side(dav * gate * sg, uv, wv_ref[...], duv_ref, dwv_ref, dbv_ref)

    return pl.pallas_call(
        body, name=name, grid=(FF_BLOCKS,), in_specs=ins + [gate_blk(n)],
        out_specs=[gate_blk(n), gate_blk(n), gate_blk(3), gate_blk(3), gate_blk(1), gate_blk(1)],
        out_shape=[SDS((n, D_FF), MXU_DTYPE)] * 2 + [SDS((3, D_FF), F32)] * 2 + [SDS((1, D_FF), F32)] * 2,
        compiler_params=_cp("parallel"))(u, u, cw, cw, cb, cb, da)


def _adamw(w, g, m, v, name):
    r, c = w.shape
    tr = r if r <= 512 else 256 if r % 256 == 0 else 88
    assert r % tr == 0, (name, r, tr)

    def body(w_ref, g_ref, m_ref, v_ref, d_ref, mo_ref, vo_ref, go_ref):
        gv = g_ref[...]
        go_ref[...] = gv
        mn = ADAM_B1 * m_ref[...] + (1.0 - ADAM_B1) * gv
        vn = ADAM_B2 * v_ref[...] + (1.0 - ADAM_B2) * gv * gv
        m_hat = mn / (1.0 - ADAM_B1 ** ADAM_STEP)
        v_hat = vn / (1.0 - ADAM_B2 ** ADAM_STEP)
        d_ref[...] = -ADAM_LR * (m_hat / (jnp.sqrt(v_hat) + ADAM_EPS) + ADAM_WD * w_ref[...])
        mo_ref[...] = mn
        vo_ref[...] = vn

    blk = pl.BlockSpec((tr, c), lambda i: (i, 0))
    out = SDS((r, c), F32)
    return pl.pallas_call(body, name=name, grid=(r // tr,), in_specs=[blk] * 4, out_specs=[blk] * 4, out_shape=[out] * 4,
                          compiler_params=_cp("parallel"))(w, g, m, v)


def _half_tile(h):
    tr = h if h <= 512 else 256 if h % 256 == 0 else 176
    assert h % tr == 0, (h, tr)
    return tr


def _add_halves(g, r, core, name):
    s, h, c = r.shape
    tr = _half_tile(h)
    steps = h // tr

    def body(ix_ref, g_ref, r_ref, o_ref):
        o_ref[...] = (g_ref[...].astype(F32) + r_ref[...].astype(F32)).astype(o_ref.dtype)

    blk = pl.BlockSpec((1, tr, c), lambda i, j, ix: (i, j, 0))
    grid_spec = pltpu.PrefetchScalarGridSpec(
        num_scalar_prefetch=1, grid=(s, steps),
        in_specs=[pl.BlockSpec((1, tr, c), lambda i, j, ix: (i, ix[0] * steps + j, 0)), blk], out_specs=blk)
    return pl.pallas_call(body, name=name, grid_spec=grid_spec, out_shape=SDS(r.shape, WIRE_DTYPE),
                          compiler_params=_cp("parallel", "parallel"))(core.reshape(1), g, r)


def _sum_chips(own, recv, me, core, name):
    _, h, c = own.shape
    tr = _half_tile(h)

    def body(ix_ref, own_ref, recv_ref, o_ref):
        acc = own_ref[0].astype(F32)
        for j in range(3):
            acc = acc + recv_ref[j].astype(F32)
        o_ref[0] = acc

    grid_spec = pltpu.PrefetchScalarGridSpec(
        num_scalar_prefetch=1, grid=(h // tr,),
        in_specs=[pl.BlockSpec((1, tr, c), lambda i, ix: (ix[0], i, 0)), pl.BlockSpec((3, tr, c), lambda i, ix: (0, i, 0))],
        out_specs=pl.BlockSpec((1, tr, c), lambda i, ix: (ix[1], i, 0)))
    return pl.pallas_call(body, name=name, grid_spec=grid_spec, out_shape=SDS((2, h, c), F32),
                          compiler_params=_cp("parallel"))(jnp.stack([me, core]), own, recv)


ANY = pl.BlockSpec(memory_space=pl.ANY)


def _place():
    x, y, c = lax.axis_index("x"), lax.axis_index("y"), lax.axis_index("c")
    return x, y, c, [(1 - x, y), (x, 1 - y), (1 - x, 1 - y)]


def _gather_shards(shards, name):
    nt = len(shards)

    def body(*refs):
        ins, outs = refs[:nt], refs[nt:2 * nt]
        send, recv, fsend, frecv, osend, orecv = refs[2 * nt:]
        x, y, c, chips = _place()
        me = 2 * x + y

        def half(t, chip, cc):
            h = ins[t].shape[0] // 2
            return outs[t].at[chip, pl.ds(cc * h, h)]

        def ici(t, j):
            cx, cy = chips[j]
            h = ins[t].shape[0] // 2
            return pltpu.make_async_remote_copy(src_ref=ins[t].at[pl.ds(c * h, h)], dst_ref=half(t, me, c),
                                                send_sem=send.at[t, j], recv_sem=recv.at[t, j], device_id=(cx, cy, c), device_id_type=MESH)

        def landed(t, j):
            cx, cy = chips[j]
            blk = half(t, 2 * cx + cy, c)
            return pltpu.make_async_remote_copy(src_ref=blk, dst_ref=blk, send_sem=send.at[t, j], recv_sem=recv.at[t, j],
                                                device_id=(cx, cy, c), device_id_type=MESH)

        def d2d(t, j, cc):
            cx, cy = chips[j]
            blk = half(t, 2 * cx + cy, cc)
            return pltpu.make_async_remote_copy(src_ref=blk, dst_ref=blk, send_sem=fsend.at[t, j], recv_sem=frecv.at[t, j],
                                                device_id=(x, y, 1 - c), device_id_type=MESH)

        own = [pltpu.make_async_remote_copy(src_ref=ins[t], dst_ref=outs[t].at[me], send_sem=osend.at[t], recv_sem=orecv.at[t],
                                            device_id=(x, y, 1 - c), device_id_type=MESH) for t in range(nt)]
        for t in range(nt):
            for j in range(3):
                ici(t, j).start()
        for cp in own:
            cp.start()
        for t in range(nt):
            for j in range(3):
                landed(t, j).wait_recv()
                d2d(t, j, c).start()
        for t in range(nt):
            for j in range(3):
                d2d(t, j, 1 - c).wait_recv()
        for t in range(nt):
            for j in range(3):
                ici(t, j).wait_send()
                d2d(t, j, c).wait_send()
        for cp in own:
            cp.wait()

    return pl.pallas_call(
        body, name=name, in_specs=[ANY] * nt, out_specs=[ANY] * nt,
        out_shape=[SDS((4,) + s.shape, s.dtype) for s in shards],
        scratch_shapes=[pltpu.SemaphoreType.DMA((nt, 3))] * 4 + [pltpu.SemaphoreType.DMA((nt,))] * 2,
        compiler_params=pltpu.CompilerParams(has_side_effects=True))(*shards)


def _swap_halves(grads, name):
    nt = len(grads)

    def body(*refs):
        ins, outs = refs[:nt], refs[nt:2 * nt]
        send, recv = refs[2 * nt:]
        x, y, c, _ = _place()
        cps = []
        for t in range(nt):
            h = ins[t].shape[1] // 2
            cps.append(pltpu.make_async_remote_copy(src_ref=ins[t].at[pl.ds(0, 4), pl.ds((1 - c) * h, h)], dst_ref=outs[t], send_sem=send.at[t],
                                                    recv_sem=recv.at[t], device_id=(x, y, 1 - c), device_id_type=MESH))
        for cp in cps:
            cp.start()
        for cp in cps:
            cp.wait()

    return pl.pallas_call(
        body, name=name, in_specs=[ANY] * nt, out_specs=[ANY] * nt,
        out_shape=[SDS((g.shape[0], g.shape[1] // 2, g.shape[2]), g.dtype) for g in grads],
        scratch_shapes=[pltpu.SemaphoreType.DMA((nt,))] * 2,
        compiler_params=pltpu.CompilerParams(has_side_effects=True))(*grads)


def _send_to_owners(parts, name):
    nt = len(parts)

    def body(*refs):
        ins, outs = refs[:nt], refs[nt:2 * nt]
        send, recv = refs[2 * nt:]
        x, y, c, chips = _place()

        def ici(t, j):
            cx, cy = chips[j]
            return pltpu.make_async_remote_copy(src_ref=ins[t].at[2 * cx + cy], dst_ref=outs[t].at[j], send_sem=send.at[t, j],
                                                recv_sem=recv.at[t, j], device_id=(cx, cy, c), device_id_type=MESH)

        for t in range(nt):
            for j in range(3):
                ici(t, j).start()
        for t in range(nt):
            for j in range(3):
                ici(t, j).wait()

    return pl.pallas_call(
        body, name=name, in_specs=[ANY] * nt, out_specs=[ANY] * nt, out_shape=[SDS((3,) + p.shape[1:], p.dtype) for p in parts],
        scratch_shapes=[pltpu.SemaphoreType.DMA((nt, 3))] * 2,
        compiler_params=pltpu.CompilerParams(has_side_effects=True))(*parts)


def _join_halves(bufs, name):
    nt = len(bufs)

    def body(*refs):
        outs = refs[nt:2 * nt]
        send, recv = refs[2 * nt:]
        x, y, c, _ = _place()
        cps = [pltpu.make_async_remote_copy(src_ref=outs[t].at[c], dst_ref=outs[t].at[c], send_sem=send.at[t], recv_sem=recv.at[t],
                                            device_id=(x, y, 1 - c), device_id_type=MESH) for t in range(nt)]
        for cp in cps:
            cp.start()
        for t in range(nt):
            theirs = outs[t].at[1 - c]
            pltpu.make_async_remote_copy(src_ref=theirs, dst_ref=theirs, send_sem=send.at[t], recv_sem=recv.at[t],
                                         device_id=(x, y, 1 - c), device_id_type=MESH).wait_recv()
        for cp in cps:
            cp.wait_send()

    return pl.pallas_call(
        body, name=name, in_specs=[ANY] * nt, out_specs=[ANY] * nt, out_shape=[SDS(b.shape, b.dtype) for b in bufs],
        input_output_aliases={t: t for t in range(nt)},
        scratch_shapes=[pltpu.SemaphoreType.DMA((nt,))] * 2,
        compiler_params=pltpu.CompilerParams(has_side_effects=True))(*bufs)


def _exchange_small(v, reduce, name, after=()):
    rows = v.shape[0]
    after, after_specs = _unread(after)

    def body(v_ref, *rest):
        o_ref, buf, send, recv = rest[-4:]
        x, y, c, _ = _place()
        me = 4 * x + 2 * y + c
        buf[me] = v_ref[...]

        def peer(dx, dy, dc):
            return (1 - x if dx else x, 1 - y if dy else y, 1 - c if dc else c)

        peers = [(dx, dy, dc) for dx in range(2) for dy in range(2) for dc in range(2) if (dx, dy, dc) != (0, 0, 0)]
        cps = []
        for j, (dx, dy, dc) in enumerate(peers):
            cps.append(pltpu.make_async_remote_copy(src_ref=v_ref, dst_ref=buf.at[me], send_sem=send.at[j], recv_sem=recv.at[j],
                                                    device_id=peer(dx, dy, dc), device_id_type=MESH))
        for cp in cps:
            cp.start()
        for j, (dx, dy, dc) in enumerate(peers):
            px, py, pc = peer(dx, dy, dc)
            blk = buf.at[4 * px + 2 * py + pc]
            pltpu.make_async_remote_copy(src_ref=blk, dst_ref=blk, send_sem=send.at[j], recv_sem=recv.at[j],
                                         device_id=(px, py, pc), device_id_type=MESH).wait_recv()
        for cp in cps:
            cp.wait_send()
        if reduce:
            acc = buf[0]
            for j in range(1, 8):
                acc = acc + buf[j]
            o_ref[...] = acc
        else:
            o_ref[...] = buf[...]

    vm = pl.BlockSpec(memory_space=pltpu.VMEM)
    return pl.pallas_call(
        body, name=name, in_specs=[vm] + after_specs, out_specs=vm, out_shape=SDS((rows, 128) if reduce else (8, rows, 128), F32),
        scratch_shapes=[pltpu.VMEM((8, rows, 128), F32), pltpu.SemaphoreType.DMA((7,)), pltpu.SemaphoreType.DMA((7,))],
        compiler_params=pltpu.CompilerParams(has_side_effects=True))(v, *after)


HBM = pl.BlockSpec(memory_space=pltpu.HBM)
SEM = pl.BlockSpec(memory_space=pltpu.SEMAPHORE)
TOKEN = pl.BlockSpec(memory_space=pltpu.VMEM)
TOKEN_SHAPE = SDS((8, 128), F32)
PEERS = 7


def _in_hbm(a):
    return pltpu.with_memory_space_constraint(a, pltpu.HBM)


def _split_params():
    return pltpu.CompilerParams(has_side_effects=pltpu.SideEffectType.DATAFLOW_SIDE_EFFECTING)


def _gather_start(shards, name, after=()):
    nt = len(shards)
    after, after_specs = _unread(after)

    def body(*refs):
        ins, lands = refs[:nt], refs[nt:2 * nt]
        outs = refs[2 * nt + len(after):]
        sends, recvs = outs[:nt], outs[nt:2 * nt]
        x, y, c, chips = _place()
        me = 2 * x + y
        for t in range(nt):
            h = ins[t].shape[0] // 2
            mine = pl.ds(c * h, h)
            for j, (cx, cy) in enumerate(chips):
                for dc in range(2):
                    pltpu.make_async_remote_copy(src_ref=ins[t].at[mine], dst_ref=lands[t].at[me, mine], send_sem=sends[t].at[2 * j + dc],
                                                 recv_sem=recvs[t].at[2 * j + c], device_id=(cx, cy, dc), device_id_type=MESH).start()
            pltpu.make_async_remote_copy(src_ref=ins[t], dst_ref=lands[t].at[me], send_sem=sends[t].at[PEERS - 1], recv_sem=recvs[t].at[PEERS - 1],
                                         device_id=(x, y, 1 - c), device_id_type=MESH).start()
        outs[-1][...] = jnp.zeros(TOKEN_SHAPE.shape, F32)

    lands = [lax.empty((4,) + s.shape, s.dtype) for s in shards]
    out = pl.pallas_call(
        body, name=name, in_specs=[HBM] * (2 * nt) + after_specs, out_specs=[SEM] * (2 * nt) + [HBM] * (2 * nt) + [TOKEN],
        out_shape=[pltpu.SemaphoreType.DMA((PEERS,))] * (2 * nt)
        + [pltpu.HBM(s.shape, s.dtype) for s in shards] + [pltpu.HBM(l.shape, l.dtype) for l in lands] + [TOKEN_SHAPE],
        input_output_aliases={t: 2 * nt + t for t in range(2 * nt)}, compiler_params=_split_params())(
            *[_in_hbm(s) for s in shards], *[_in_hbm(l) for l in lands], *after)
    return out[:nt], out[nt:2 * nt], out[2 * nt:3 * nt], out[3 * nt:4 * nt], out[-1]


def _gather_wait(sends, recvs, shards, lands, after, name):
    nt = len(shards)

    def body(*refs):
        ins, lands_ref = refs[:nt], refs[nt:2 * nt]
        send_refs, recv_refs = refs[2 * nt:3 * nt], refs[3 * nt:4 * nt]
        x, y, c, chips = _place()
        for t in range(nt):
            h = ins[t].shape[0] // 2
            for j, (cx, cy) in enumerate(chips):
                for cs in range(2):
                    blk = lands_ref[t].at[2 * cx + cy, pl.ds(cs * h, h)]
                    pltpu.make_async_remote_copy(src_ref=blk, dst_ref=blk, send_sem=send_refs[t].at[2 * j + cs], recv_sem=recv_refs[t].at[2 * j + cs],
                                                 device_id=(cx, cy, cs), device_id_type=MESH).wait()
            blk = lands_ref[t].at[2 * x + y]
            pltpu.make_async_remote_copy(src_ref=blk, dst_ref=blk, send_sem=send_refs[t].at[PEERS - 1], recv_sem=recv_refs[t].at[PEERS - 1],
                                         device_id=(x, y, 1 - c), device_id_type=MESH).wait()

    out = pl.pallas_call(
        body, name=name, in_specs=[HBM] * (2 * nt) + [SEM] * (2 * nt) + [ANY], out_specs=[HBM] * (2 * nt),
        out_shape=[pltpu.HBM(s.shape, s.dtype) for s in shards] + [pltpu.HBM(l.shape, l.dtype) for l in lands],
        input_output_aliases={t: t for t in range(2 * nt)}, compiler_params=_split_params())(*shards, *lands, *sends, *recvs, after)
    return out[nt:]


def _scatter_start(g, name):
    _, r, c_ = g.shape
    h = r // 2

    def body(g_ref, land, send, recv, g_thru, land_thru, token):
        x, y, c, chips = _place()
        for j, (cx, cy) in enumerate(chips):
            for dc in range(2):
                pltpu.make_async_remote_copy(src_ref=g_ref.at[2 * cx + cy, pl.ds(dc * h, h)], dst_ref=land.at[2 * j + c], send_sem=send.at[2 * j + dc],
                                             recv_sem=recv.at[2 * j + c], device_id=(cx, cy, dc), device_id_type=MESH).start()
        pltpu.make_async_remote_copy(src_ref=g_ref.at[2 * x + y, pl.ds((1 - c) * h, h)], dst_ref=land.at[PEERS - 1], send_sem=send.at[PEERS - 1],
                                     recv_sem=recv.at[PEERS - 1], device_id=(x, y, 1 - c), device_id_type=MESH).start()
        token[...] = jnp.zeros(TOKEN_SHAPE.shape, F32)

    land = lax.empty((PEERS, h, c_), g.dtype)
    return pl.pallas_call(
        body, name=name, in_specs=[HBM, HBM], out_specs=[SEM, SEM, HBM, HBM, TOKEN],
        out_shape=[pltpu.SemaphoreType.DMA((PEERS,)), pltpu.SemaphoreType.DMA((PEERS,)), pltpu.HBM(g.shape, g.dtype),
                   pltpu.HBM(land.shape, land.dtype), TOKEN_SHAPE],
        input_output_aliases={0: 2, 1: 3}, compiler_params=_split_params())(_in_hbm(g), _in_hbm(land))


def _scatter_wait(started, after, name):
    nt = len(started)

    def body(*refs):
        lands = refs[nt:2 * nt]
        sends, recvs = refs[2 * nt:3 * nt], refs[3 * nt:4 * nt]
        x, y, c, chips = _place()
        peers = [(cx, cy, dc) for cx, cy in chips for dc in range(2)] + [(x, y, 1 - c)]
        for t in range(nt):
            for k, peer in enumerate(peers):
                blk = lands[t].at[k]
                pltpu.make_async_remote_copy(src_ref=blk, dst_ref=blk, send_sem=sends[t].at[k], recv_sem=recvs[t].at[k],
                                             device_id=peer, device_id_type=MESH).wait()

    gs, lands = [s[2] for s in started], [s[3] for s in started]
    after, after_specs = _unread(after)
    out = pl.pallas_call(
        body, name=name, in_specs=[HBM] * (2 * nt) + [SEM] * (2 * nt) + after_specs, out_specs=[HBM] * (2 * nt),
        out_shape=[pltpu.HBM(a.shape, a.dtype) for a in gs + lands],
        input_output_aliases={t: t for t in range(2 * nt)}, compiler_params=_split_params())(
            *gs, *lands, *[s[0] for s in started], *[s[1] for s in started], *after)
    return out[:nt], out[nt:]


def _sum_devices(g, land, me, core, name):
    npeer, h, c = land.shape
    tr = _half_tile(h)
    steps = h // tr

    def body(ix_ref, own_ref, land_ref, o_ref):
        acc = own_ref[0].astype(F32)
        for j in range(npeer):
            acc = acc + land_ref[j].astype(F32)
        o_ref[0] = acc

    grid_spec = pltpu.PrefetchScalarGridSpec(
        num_scalar_prefetch=1, grid=(steps,),
        in_specs=[pl.BlockSpec((1, tr, c), lambda i, ix: (ix[0], ix[1] * steps + i, 0)), pl.BlockSpec((npeer, tr, c), lambda i, ix: (0, i, 0))],
        out_specs=pl.BlockSpec((1, tr, c), lambda i, ix: (ix[1], i, 0)))
    return pl.pallas_call(body, name=name, grid_spec=grid_spec, out_shape=SDS((2, h, c), F32),
                          compiler_params=_cp("parallel"))(jnp.stack([me, core]), g, land)


def _pack_small(parts):
    flat = jnp.concatenate([p.reshape(-1) for p in parts])
    total = flat.shape[0]
    rows = -(-total // 1024) * 8
    return jnp.pad(flat, (0, rows * 128 - total)).reshape(rows, 128)


def _unpack_small(packed, shapes):
    flat = packed.reshape(-1)
    out, off = [], 0
    for s in shapes:
        size = int(np.prod(s))
        out.append(flat[off:off + size].reshape(s))
        off += size
    return out


def _local_step(x, mem, target, w_in, first_after, mid_weights, ffn_weights, on_grad, gains, conv_w, conv_b, hg_lb):
    n = x.shape[0]
    cos, sin = _rope_tables(n)
    seg = _hg_segments()
    gp, gn = _hg_pair_sums()
    gq2 = jnp.tile(gains["q_norm_g"], (1, 2))
    gk2 = jnp.tile(gains["k_norm_g"], (1, 2))
    a0 = hg_lb[:, 0:1, :]
    a1 = hg_lb[:, 1:2, :]

    p, h1 = _norm_mm(x, gains["pre_mix_g"], w_in, F32, TOKEN_TILE, 1664, "in_proj", after=(first_after,))
    qr, kr = _qk_prep(p, gq2, gk2, cos, sin, "qk_prep")
    heads = lambda a: a.reshape(n, ATT_KV_HEADS, ATT_HEAD_DIM).transpose(1, 0, 2)
    kh = heads(kr)
    vh = heads(p[:, OFF_AV:OFF_AV + ATT_KV_DIM].astype(MXU_DTYPE))
    att = _attn_fwd(qr, kh, vh, "attn_fwd")
    o2 = _hgrn_fwd(p, a0, a1, seg, "hgrn_fwd")
    rec = _hg_post(o2, p, gains["hg_out_norm_g"], "hg_post")
    cat = jnp.concatenate([att, rec], axis=1)
    w_out, w_xq, w_xkv, w_xo = mid_weights(cat)
    mixed = _mm(cat, w_out, "nn", F32, TOKEN_TILE, 1024, "out_proj")
    x1 = _resid_norm(x, mixed, gains["post_mix_g"], "mix_resid")
    xq, h2 = _norm_mm(x1, gains["pre_x_g"], w_xq, MXU_DTYPE, TOKEN_TILE, 1024, "xq_proj")
    kv, mn = _norm_mm(mem, gains["mem_norm_g"], w_xkv, MXU_DTYPE, 256, 2048, "xkv_proj")
    ox = _xattn_fwd(xq, kv, "xattn_fwd")
    xo = _mm(ox, w_xo, "nn", F32, TOKEN_TILE, 1024, "xo_proj")
    x2 = _resid_norm(x1, xo, gains["post_x_g"], "x_resid")
    w_up, w_down = ffn_weights(x2)
    u, h3 = _norm_mm(x2, gains["pre_ffn_g"], w_up, F32, TOKEN_TILE, 1408, "up_proj")
    act = _conv_gate(u, conv_w, conv_b, "conv_gate")
    dn = _mm(act, w_down, "nn", F32, TOKEN_TILE, 1024, "down_proj")
    d3, loss = _final_loss(x2, dn, gains["post_ffn_g"], target, "ffn_resid_loss")

    gs = {}
    d_dn, gs["post_ffn_g"] = _norm_bwd(dn, gains["post_ffn_g"], d3, None, MXU_DTYPE, "ffn_post_bwd")
    tok = on_grad("w_down", _mm(act, d_dn, "tn", WIRE_DTYPE, 1408, 1024, "down_dw"))
    d_act = _mm(d_dn, w_down, "nt", F32, TOKEN_TILE, 1408, "down_dx", after=(tok,))
    du_g, du_v, dcw_g, dcw_v, dcb_g, dcb_v = _conv_gate_bwd(u, conv_w, conv_b, d_act, "conv_gate_bwd")
    gs["conv_w"] = jnp.concatenate([dcw_g, dcw_v], axis=1)
    gs["conv_b"] = jnp.concatenate([dcb_g, dcb_v], axis=1)
    ff_shard = w_up.shape[2]
    g_up = _dw_by_owner(h3, du_g, ff_shard, 0, None, 512, "up_dw_gate")
    tok = on_grad("w_up", _dw_by_owner(h3, du_v, ff_shard, 2, g_up, 512, "up_dw_value"))
    d_h3 = _mm_nt_parts([(du_g, 0), (du_g, 1), (du_v, 0), (du_v, 1)], w_up, F32, TOKEN_TILE, 512, "up_dx", after=(tok,))
    d2, gs["pre_ffn_g"] = _norm_bwd(x2, gains["pre_ffn_g"], d_h3, d3, F32, "ffn_pre_bwd")
    d_xo, gs["post_x_g"] = _norm_bwd(xo, gains["post_x_g"], d2, None, MXU_DTYPE, "x_post_bwd")
    tok = on_grad("w_xo", _mm(ox, d_xo, "tn", WIRE_DTYPE, 512, 1024, "xo_dw"))
    d_ox = _mm(d_xo, w_xo, "nt", MXU_DTYPE, TOKEN_TILE, 1024, "xo_dx", after=(tok,))
    d_xq, d_k, d_v = _xattn_bwd(xq, kv, d_ox, "xattn_bwd")
    d_kv = jnp.concatenate([d_k, d_v], axis=1).astype(MXU_DTYPE)
    tok = on_grad("w_xq", _mm(h2, d_xq, "tn", WIRE_DTYPE, 512, 1024, "xq_dw"))
    tok_kv = on_grad("w_xkv", _dw_by_owner(mn, d_kv, w_xkv.shape[2], 0, None, 512, "xkv_dw"))
    d_h2 = _mm(d_xq, w_xq, "nt", F32, TOKEN_TILE, 1024, "xq_dx", after=(tok, tok_kv))
    d_mn = _mm_nt_parts([(d_kv, s) for s in range(4)], w_xkv, F32, 256, 1024, "xkv_dx")
    _, gs["mem_norm_g"] = _norm_bwd(mem, gains["mem_norm_g"], d_mn, None, MXU_DTYPE, "mem_norm_bwd")
    d1, gs["pre_x_g"] = _norm_bwd(x1, gains["pre_x_g"], d_h2, d2, F32, "x_pre_bwd")
    d_mixed, gs["post_mix_g"] = _norm_bwd(mixed, gains["post_mix_g"], d1, None, MXU_DTYPE, "mix_post_bwd")
    tok = on_grad("w_out", _mm(cat, d_mixed, "tn", WIRE_DTYPE, 512, 1024, "out_dw"))
    d_cat = _mm(d_mixed, w_out, "nt", MXU_DTYPE, TOKEN_TILE, 1024, "out_dx", after=(tok,))
    d_o, d_hg, dg_hg = _hg_post_bwd(o2, p, gains["hg_out_norm_g"], d_cat, "hg_post_bwd")
    gs["hg_out_norm_g"] = dg_hg.reshape(HG_HEADS, HG_HEAD_DIM).sum(axis=0, keepdims=True)
    dhq2, dlf_q, s0 = _hgrn_bwd_q(p, a0, a1, seg, gp, d_o, "hgrn_bwd_q")
    dz2, dhv2, dlb = _hgrn_bwd_kv(p, a0, a1, seg, gn, d_o, dlf_q, s0, "hgrn_bwd_kv")
    lb = jax.nn.sigmoid(a0 - a1)
    da0 = dlb * lb * (1.0 - lb)
    gs["hg_lb"] = jnp.concatenate([da0, -da0], axis=1)
    d_qr, d_kh, d_vh = _attn_bwd(qr, kh, vh, d_cat, "attn_bwd")
    unheads = lambda a: a.transpose(2, 0, 1).reshape(n, ATT_KV_DIM)
    d_aq, d_ak, dgq, dgk = _qk_prep_bwd(p, gq2, gk2, cos, sin, d_qr, unheads(d_kh), "qk_prep_bwd")
    gs["q_norm_g"] = dgq.reshape(ATT_HEADS, ATT_HEAD_DIM).sum(axis=0, keepdims=True)
    gs["k_norm_g"] = dgk.reshape(ATT_KV_HEADS, ATT_HEAD_DIM).sum(axis=0, keepdims=True)
    d_p = jnp.concatenate([d_aq, d_ak, unheads(d_vh).astype(MXU_DTYPE), (dhq2[0] + dhq2[1]).astype(MXU_DTYPE),
                           dz2[0].astype(MXU_DTYPE), dz2[1].astype(MXU_DTYPE), (dhv2[0] + dhv2[1]).astype(MXU_DTYPE), d_hg], axis=1)
    tok = on_grad("w_in", _mm(h1, d_p, "tn", WIRE_DTYPE, 512, 1664, "in_dw"))
    d_h1 = _mm(d_p, w_in, "nt", F32, TOKEN_TILE, 1024, "in_dx", after=(tok,))
    grad_x, gs["pre_mix_g"] = _norm_bwd(x, gains["pre_mix_g"], d_h1, d1, F32, "mix_pre_bwd")
    return loss, grad_x, gs


MATS = ("w_in", "w_out", "w_xq", "w_xkv", "w_xo", "w_up", "w_down")
GAINS = ("pre_mix_g", "q_norm_g", "k_norm_g", "hg_out_norm_g", "post_mix_g", "pre_x_g", "mem_norm_g", "post_x_g", "pre_ffn_g", "post_ffn_g")
WEIGHTS = ('pre_mix_g', 'w_in', 'q_norm_g', 'k_norm_g', 'hg_lb', 'hg_out_norm_g', 'w_out', 'post_mix_g', 'pre_x_g', 'mem_norm_g', 'w_xq',
           'w_xkv', 'w_xo', 'post_x_g', 'pre_ffn_g', 'w_up', 'conv_w', 'conv_b', 'w_down', 'post_ffn_g')


def kernel(x, mem, pre_mix_g, w_in, q_norm_g, k_norm_g, hg_lb, hg_out_norm_g, w_out, post_mix_g, pre_x_g, mem_norm_g, w_xq, w_xkv, w_xo, post_x_g, pre_ffn_g, w_up, conv_w, conv_b, w_down, post_ffn_g, loss_target, m_pre_mix_g, m_w_in, m_q_norm_g, m_k_norm_g, m_hg_lb, m_hg_out_norm_g, m_w_out, m_post_mix_g, m_pre_x_g, m_mem_norm_g, m_w_xq, m_w_xkv, m_w_xo, m_post_x_g, m_pre_ffn_g, m_w_up, m_conv_w, m_conv_b, m_w_down, m_post_ffn_g, v_pre_mix_g, v_w_in, v_q_norm_g, v_k_norm_g, v_hg_lb, v_hg_out_norm_g, v_w_out, v_post_mix_g, v_pre_x_g, v_mem_norm_g, v_w_xq, v_w_xkv, v_w_xo, v_post_x_g, v_pre_ffn_g, v_w_up, v_conv_w, v_conv_b, v_w_down, v_post_ffn_g):
    args = dict(locals())
    w = {k: args[k] for k in WEIGHTS}
    m = {k: args["m_" + k] for k in WEIGHTS}
    v = {k: args["v_" + k] for k in WEIGHTS}
    chip = 2 * lax.axis_index("x") + lax.axis_index("y")
    core = lax.axis_index("c")

    shards = {k: w[k][0].astype(WIRE_DTYPE) for k in MATS}

    def whole(k, g):
        return g if k in ("w_xkv", "w_up") else g.reshape(-1, g.shape[-1])

    w_in_shards = _gather_shards([shards["w_in"]], "gather_w_in")[0]
    w_in_full = jnp.concatenate([w_in_shards[s] for s in range(4)], axis=1)
    small_in = _exchange_small(_pack_small([w["conv_w"][0], w["hg_lb"]]), False, "gather_small")
    mid_names, ffn_names = ("w_out", "w_xq", "w_xkv", "w_xo"), ("w_up", "w_down")
    mid = _gather_start([shards[k] for k in mid_names], "gather_mid_start", after=(w_in_full, small_in))
    ffn = _gather_start([shards[k] for k in ffn_names], "gather_ffn_start", after=(mid[4],))

    def mid_weights(after):
        return [whole(k, g) for k, g in zip(mid_names, _gather_wait(*mid[:4], after, "gather_mid_wait"))]

    def ffn_weights(after):
        return [whole(k, g) for k, g in zip(ffn_names, _gather_wait(*ffn[:4], after, "gather_ffn_wait"))]

    cw_parts, lb_parts = [], []
    for s in range(4):
        cw_s, lb_s = _unpack_small(small_in[2 * s], [w["conv_w"][0].shape, w["hg_lb"].shape])
        cw_parts.append(cw_s)
        lb_parts.append(lb_s)
    conv_w_full = jnp.concatenate(cw_parts, axis=1)
    hg_lb_full = jnp.concatenate(lb_parts, axis=2)

    started = {}

    def on_grad(k, g):
        if k == "w_in":
            g = g.reshape(g.shape[0], 4, g.shape[1] // 4).transpose(1, 0, 2)
        elif g.ndim == 2:
            g = g.reshape(4, g.shape[0] // 4, g.shape[1])
        *started[k], token = _scatter_start(g, "grad_start_" + k)
        return token

    gains = {k: w[k] for k in GAINS}
    loss_part, grad_x, gs = _local_step(x[0], mem[0], loss_target[0], w_in_full, ffn[4], mid_weights, ffn_weights, on_grad, gains,
                                        conv_w_full, w["conv_b"], hg_lb_full)
    gs["loss"] = loss_part

    grads, delta, new_m, new_v = {}, {}, {}, {}

    def reduce_matrices(names, after, tag):
        sent, landed = _scatter_wait([started[k] for k in names], after, "grad_wait_" + tag)
        halves = [_sum_devices(g, land, chip, core, "grad_sum_" + k) for k, g, land in zip(names, sent, landed)]
        for k, r in zip(names, _join_halves(halves, "grad_join_" + tag)):
            grads[k] = r.reshape(1, -1, r.shape[-1])

    def adamw(names):
        for k in names:
            shape = w[k].shape
            two_d = lambda a: a.reshape(-1, shape[-1])
            d, mo, vo, go = _adamw(two_d(w[k]), two_d(grads[k]), two_d(m[k]), two_d(v[k]), "adamw_" + k)
            delta[k], new_m[k], new_v[k], grads[k] = d.reshape(shape), mo.reshape(shape), vo.reshape(shape), go.reshape(shape)

    early = tuple(k for k in MATS if k != "w_in")
    reduce_matrices(early, (grad_x,), "early")
    adamw(early)

    small_names = GAINS + ("conv_b", "conv_w", "hg_lb")
    packed = _pack_small([gs[k] for k in small_names + ("loss",)])
    reduced_small = _exchange_small(packed, True, "reduce_small", after=tuple(new_v[k] for k in early))
    *summed, loss = _unpack_small(reduced_small, [gs[k].shape for k in small_names + ("loss",)])
    loss = loss[0, 0]
    for k, g in zip(small_names, summed):
        grads[k] = g
    ncw = w["conv_w"].shape[2]
    grads["conv_w"] = lax.dynamic_slice_in_dim(grads["conv_w"], chip * ncw, ncw, axis=1)[None]
    nlb = w["hg_lb"].shape[2]
    grads["hg_lb"] = lax.dynamic_slice_in_dim(grads["hg_lb"], chip * nlb, nlb, axis=2)
    replicated = GAINS + ("conv_b",)
    shapes = [w[k].shape for k in replicated]
    rows = sum(int(np.prod(s)) for s in shapes) // 128
    pack = lambda d: jnp.concatenate([d[k].reshape(-1) for k in replicated]).reshape(rows, 128)
    outs = _adamw(pack(w), reduced_small[:rows], pack(m), pack(v), "adamw_replicated")
    for into, packed_out in zip((delta, new_m, new_v, grads), outs):
        for k, a in zip(replicated, _unpack_small(packed_out, shapes)):
            into[k] = a
    adamw(("conv_w", "hg_lb"))

    reduce_matrices(("w_in",), tuple(new_v[k] for k in early + small_names), "late")
    adamw(("w_in",))
    return (loss, grad_x[None], *[grads[k] for k in WEIGHTS], *[delta[k] for k in WEIGHTS],
            *[new_m[k] for k in WEIGHTS], *[new_v[k] for k in WEIGHTS])
```

```python
import functools

import numpy as np
import jax
import jax.numpy as jnp
from jax import lax
from jax.experimental import pallas as pl
from jax.experimental.pallas import tpu as pltpu

F32 = jnp.float32
MXU_DTYPE = jnp.bfloat16
WIRE_DTYPE = jnp.bfloat16
VMEM_LIMIT_BYTES = 56 * 1024 * 1024
EPS = 1e-6
MESH = pl.DeviceIdType.MESH

D_MODEL = 1024
GRID_W = 64
ATT_HEADS, ATT_KV_HEADS, ATT_HEAD_DIM = 8, 2, 64
ATT_GROUP = ATT_HEADS // ATT_KV_HEADS
ATT_Q_DIM, ATT_KV_DIM = 512, 128
ROPE_THETA = 10000.0
HG_HEADS, HG_HEAD_DIM, HG_DIM = 4, 128, 512
HG_CHUNK = 128
HG_LEVELS = 7
HG_PAIR = 2 * HG_HEAD_DIM
X_HEADS, X_HEAD_DIM = 4, 256
D_FF = 2816
FF_COLS = 256
FF_BLOCKS = D_FF // FF_COLS
N_IN = 3328
OFF_AQ, OFF_AK, OFF_AV, OFF_HQ, OFF_ZF, OFF_ZB, OFF_HI, OFF_HG = 0, 512, 640, 768, 1280, 1792, 2304, 2816

ADAM_LR, ADAM_B1, ADAM_B2, ADAM_EPS, ADAM_WD, ADAM_STEP = 0.001, 0.9, 0.999, 1e-08, 0.01, 10

SDS = jax.ShapeDtypeStruct


def _cp(*sem):
    return pltpu.CompilerParams(dimension_semantics=sem, vmem_limit_bytes=VMEM_LIMIT_BYTES)


def _dot(a, b, form="nn"):
    dims = {"nn": (((1,), (0,)), ((), ())), "nt": (((1,), (1,)), ((), ())), "tn": (((0,), (0,)), ((), ()))}[form]
    return lax.dot_general(a.astype(MXU_DTYPE), b.astype(MXU_DTYPE), dims, preferred_element_type=F32)


def _sigmoid(x):
    return 1.0 / (1.0 + jnp.exp(-x))


def _rstd(x):
    return lax.rsqrt(jnp.mean(x * x, axis=-1, keepdims=True) + EPS)


def _rms_bwd(x, g, dy):
    r = _rstd(x)
    xh = x * r
    dn = dy * g
    dx = r * (dn - xh * jnp.mean(dn * xh, axis=-1, keepdims=True))
    return dx, jnp.sum(dy * xh, axis=0, keepdims=True)


def _unread(after):
    after = tuple(a for a in after if a is not None)
    return after, [pl.BlockSpec(memory_space=pl.ANY)] * len(after)


def _mm(a, b, form, out_dtype, tm, tn, name, after=()):
    after, after_specs = _unread(after)
    if form == "nn":
        (m, k), n = a.shape, b.shape[1]
    elif form == "nt":
        (m, k), n = a.shape, b.shape[0]
    else:
        (k, m), n = a.shape, b.shape[1]
    tm, tn = min(tm, m), min(tn, n)
    assert m % tm == 0 and n % tn == 0, (name, m, n, tm, tn)

    def body(a_ref, b_ref, *rest):
        o_ref = rest[-1]
        o_ref[...] = _dot(a_ref[...], b_ref[...], form).astype(o_ref.dtype)

    a_spec = pl.BlockSpec((k, tm), lambda i, j: (0, i)) if form == "tn" else pl.BlockSpec((tm, k), lambda i, j: (i, 0))
    b_spec = pl.BlockSpec((tn, k), lambda i, j: (j, 0)) if form == "nt" else pl.BlockSpec((k, tn), lambda i, j: (0, j))
    return pl.pallas_call(
        body, name=name, grid=(m // tm, n // tn), in_specs=[a_spec, b_spec] + after_specs,
        out_specs=pl.BlockSpec((tm, tn), lambda i, j: (i, j)), out_shape=SDS((m, n), out_dtype),
        compiler_params=_cp("parallel", "parallel"))(a, b, *after)


def _mm_nt_parts(a_parts, b, out_dtype, tm, tn, name, after=()):
    after, after_specs = _unread(after)
    parts, n, p = b.shape
    m = a_parts[0][0].shape[0]
    tm, tn = min(tm, m), min(tn, n)
    assert m % tm == 0 and n % tn == 0 and len(a_parts) == parts, (name, m, b.shape)

    def body(*refs):
        o_ref = refs[-1]
        acc = _dot(refs[0][...], refs[parts][0], "nt")
        for s in range(1, parts):
            acc = acc + _dot(refs[s][...], refs[parts + s][0], "nt")
        o_ref[...] = acc.astype(o_ref.dtype)

    a_specs = [pl.BlockSpec((tm, p), lambda i, j, cb=cb: (i, cb)) for _, cb in a_parts]
    b_specs = [pl.BlockSpec((1, tn, p), lambda i, j, s=s: (s, j, 0)) for s in range(parts)]
    return pl.pallas_call(
        body, name=name, grid=(m // tm, n // tn), in_specs=a_specs + b_specs + after_specs,
        out_specs=pl.BlockSpec((tm, tn), lambda i, j: (i, j)), out_shape=SDS((m, n), out_dtype),
        compiler_params=_cp("parallel", "parallel"))(*[arr for arr, _ in a_parts], *([b] * parts), *after)


def _dw_by_owner(a, b, tn, first, into, tm, name):
    k, m = a.shape
    cnt = b.shape[1] // tn
    tm = min(tm, m)
    assert m % tm == 0 and b.shape[1] == cnt * tn and first + cnt <= 4, (name, a.shape, b.shape)

    def body(a_ref, b_ref, *rest):
        rest[-1][0] = _dot(a_ref[...], b_ref[...], "tn").astype(rest[-1].dtype)

    extra = [] if into is None else [into]
    return pl.pallas_call(
        body, name=name, grid=(m // tm, cnt),
        in_specs=[pl.BlockSpec((k, tm), lambda i, j: (0, i)), pl.BlockSpec((k, tn), lambda i, j: (0, j))] + [pl.BlockSpec(memory_space=pl.ANY)] * len(extra),
        out_specs=pl.BlockSpec((1, tm, tn), lambda i, j: (first + j, i, 0)), out_shape=SDS((4, m, tn), WIRE_DTYPE),
        input_output_aliases={2: 0} if extra else {},
        compiler_params=_cp("parallel", "parallel"))(a, b, *extra)


def _norm_mm(x, g, w, out_dtype, tm, tn, name, after=()):
    after, after_specs = _unread(after)
    m, d = x.shape
    sharded = w.ndim == 3
    n = w.shape[-1] * (w.shape[0] if sharded else 1)
    tm, tn = min(tm, m), (w.shape[-1] if sharded else min(tn, n))
    assert m % tm == 0 and n % tn == 0, (name, m, n, tm, tn)

    def body(x_ref, g_ref, w_ref, *rest):
        o_ref, h_ref, hs = rest[-3:]

        @pl.when(pl.program_id(1) == 0)
        def _():
            xv = x_ref[...]
            h = (xv * _rstd(xv) * g_ref[...]).astype(MXU_DTYPE)
            hs[...] = h
            h_ref[...] = h

        o_ref[...] = _dot(hs[...], w_ref[0] if sharded else w_ref[...]).astype(o_ref.dtype)

    w_spec = pl.BlockSpec((1, d, tn), lambda i, j: (j, 0, 0)) if sharded else pl.BlockSpec((d, tn), lambda i, j: (0, j))
    return pl.pallas_call(
        body, name=name, grid=(m // tm, n // tn),
        in_specs=[pl.BlockSpec((tm, d), lambda i, j: (i, 0)), pl.BlockSpec((1, d), lambda i, j: (0, 0)), w_spec] + after_specs,
        out_specs=[pl.BlockSpec((tm, tn), lambda i, j: (i, j)), pl.BlockSpec((tm, d), lambda i, j: (i, 0))],
        out_shape=[SDS((m, n), out_dtype), SDS((m, d), MXU_DTYPE)],
        scratch_shapes=[pltpu.VMEM((tm, d), MXU_DTYPE)],
        compiler_params=_cp("parallel", "arbitrary"))(x, g, w, *after)


ROW_TILE = 256
TOKEN_TILE = 1024


def _resid_norm(x, y, g, name):
    n, d = x.shape
    tr = min(ROW_TILE, n)

    def body(x_ref, y_ref, g_ref, o_ref):
        yv = y_ref[...]
        o_ref[...] = x_ref[...] + yv * _rstd(yv) * g_ref[...]

    row = pl.BlockSpec((tr, d), lambda i: (i, 0))
    return pl.pallas_call(
        body, name=name, grid=(n // tr,), in_specs=[row, row, pl.BlockSpec((1, d), lambda i: (0, 0))],
        out_specs=row, out_shape=SDS((n, d), F32), compiler_params=_cp("parallel"))(x, y, g)


def _norm_bwd(x, g, dy, res, out_dtype, name):
    n, d = x.shape
    tr = min(ROW_TILE, n)
    has_res = res is not None

    def body(*refs):
        x_ref, g_ref, dy_ref = refs[:3]
        dx_ref, dg_ref = refs[-2:]
        dx, dg = _rms_bwd(x_ref[...], g_ref[...], dy_ref[...].astype(F32))
        if has_res:
            dx = dx + refs[3][...]
        dx_ref[...] = dx.astype(dx_ref.dtype)

        @pl.when(pl.program_id(0) == 0)
        def _():
            dg_ref[...] = jnp.zeros_like(dg_ref)

        dg_ref[...] += dg

    row = pl.BlockSpec((tr, d), lambda i: (i, 0))
    vec = pl.BlockSpec((1, d), lambda i: (0, 0))
    ins = [x, g, dy] + ([res] if has_res else [])
    return pl.pallas_call(
        body, name=name, grid=(n // tr,), in_specs=[row, vec, row] + ([row] if has_res else []),
        out_specs=[row, vec], out_shape=[SDS((n, d), out_dtype), SDS((1, d), F32)],
        compiler_params=_cp("arbitrary"))(*ins)


def _final_loss(x, y, g, target, name):
    n, d = x.shape
    tr = min(ROW_TILE, n)

    def body(x_ref, y_ref, g_ref, t_ref, d_ref, l_ref):
        yv = y_ref[...]
        diff = x_ref[...] + yv * _rstd(yv) * g_ref[...] - t_ref[...]
        d_ref[...] = diff * (1.0 / d)

        @pl.when(pl.program_id(0) == 0)
        def _():
            l_ref[...] = jnp.zeros_like(l_ref)

        l_ref[...] += 0.5 * jnp.sum(jnp.mean(diff * diff, axis=-1, keepdims=True), axis=0, keepdims=True)

    row = pl.BlockSpec((tr, d), lambda i: (i, 0))
    return pl.pallas_call(
        body, name=name, grid=(n // tr,), in_specs=[row, row, pl.BlockSpec((1, d), lambda i: (0, 0)), row],
        out_specs=[row, pl.BlockSpec((1, 1), lambda i: (0, 0))], out_shape=[SDS((n, d), F32), SDS((1, 1), F32)],
        compiler_params=_cp("arbitrary"))(x, y, g, target)


def _rope_tables(n):
    pairs = ATT_HEAD_DIM // 4
    t = np.arange(n)
    inv = np.power(ROPE_THETA, -np.arange(pairs, dtype=np.float32) / pairs).astype(np.float32)
    ang = np.concatenate([(t // GRID_W)[:, None].astype(np.float32) * inv, (t % GRID_W)[:, None].astype(np.float32) * inv], axis=-1)
    cos = np.repeat(np.cos(ang), 2, axis=-1)
    sin = np.repeat(np.sin(ang), 2, axis=-1) * np.tile(np.array([-1.0, 1.0], np.float32), ATT_HEAD_DIM // 2)
    return jnp.asarray(np.tile(cos, 2), F32), jnp.asarray(np.tile(sin, 2), F32)


def _swap_pairs(x):
    lane = lax.broadcasted_iota(jnp.int32, x.shape, 1)
    return jnp.where((lane & 1) == 0, pltpu.roll(x, 127, axis=1), pltpu.roll(x, 1, axis=1))


def _head_mean(v):
    lane = lax.broadcasted_iota(jnp.int32, v.shape, 1)
    lo = jnp.where(lane < ATT_HEAD_DIM, v, 0.0)
    s0 = jnp.sum(lo, axis=-1, keepdims=True)
    s1 = jnp.sum(v - lo, axis=-1, keepdims=True)
    return jnp.where(lane < ATT_HEAD_DIM, s0, s1) * (1.0 / ATT_HEAD_DIM)


def _qk_prep(p, gq, gk, cos, sin, name):
    n = p.shape[0]
    tr = min(ROW_TILE, n)

    def one(xv, g, c, s):
        xn = xv * lax.rsqrt(_head_mean(xv * xv) + EPS) * g
        return xn * c + _swap_pairs(xn) * s

    def body(q_ref, k_ref, gq_ref, gk_ref, c_ref, s_ref, qo_ref, ko_ref):
        c, s = c_ref[...], s_ref[...]
        for j in range(ATT_Q_DIM // 128):
            qo_ref[:, j * 128:(j + 1) * 128] = one(q_ref[:, j * 128:(j + 1) * 128], gq_ref[...], c, s).astype(qo_ref.dtype)
        ko_ref[...] = one(k_ref[...], gk_ref[...], c, s).astype(ko_ref.dtype)

    vec = pl.BlockSpec((1, 128), lambda i: (0, 0))
    tab = pl.BlockSpec((tr, 128), lambda i: (i, 0))
    return pl.pallas_call(
        body, name=name, grid=(n // tr,),
        in_specs=[pl.BlockSpec((tr, ATT_Q_DIM), lambda i: (i, 0)), pl.BlockSpec((tr, 128), lambda i: (i, OFF_AK // 128)), vec, vec, tab, tab],
        out_specs=[pl.BlockSpec((tr, ATT_Q_DIM), lambda i: (i, 0)), tab],
        out_shape=[SDS((n, ATT_Q_DIM), MXU_DTYPE), SDS((n, ATT_KV_DIM), MXU_DTYPE)],
        compiler_params=_cp("parallel"))(p, p, gq, gk, cos, sin)


def _qk_prep_bwd(p, gq, gk, cos, sin, dq, dk, name):
    n = p.shape[0]
    tr = min(ROW_TILE, n)

    def one(xv, g, c, s, dout):
        dxn = dout * c + _swap_pairs(dout * s)
        r = lax.rsqrt(_head_mean(xv * xv) + EPS)
        xh = xv * r
        dn = dxn * g
        dx = r * (dn - xh * _head_mean(dn * xh))
        return dx, jnp.sum(dxn * xh, axis=0, keepdims=True)

    def body(q_ref, k_ref, gq_ref, gk_ref, c_ref, s_ref, dq_ref, dk_ref, dqo_ref, dko_ref, dgq_ref, dgk_ref):
        @pl.when(pl.program_id(0) == 0)
        def _():
            dgq_ref[...] = jnp.zeros_like(dgq_ref)
            dgk_ref[...] = jnp.zeros_like(dgk_ref)

        c, s = c_ref[...], s_ref[...]
        for j in range(ATT_Q_DIM // 128):
            sl = slice(j * 128, (j + 1) * 128)
            dx, dg = one(q_ref[:, sl], gq_ref[...], c, s, dq_ref[:, sl])
            dqo_ref[:, sl] = dx.astype(dqo_ref.dtype)
            dgq_ref[:, sl] += dg
        dx, dg = one(k_ref[...], gk_ref[...], c, s, dk_ref[...])
        dko_ref[...] = dx.astype(dko_ref.dtype)
        dgk_ref[...] += dg

    vec = pl.BlockSpec((1, 128), lambda i: (0, 0))
    tab = pl.BlockSpec((tr, 128), lambda i: (i, 0))
    qrow = pl.BlockSpec((tr, ATT_Q_DIM), lambda i: (i, 0))
    return pl.pallas_call(
        body, name=name, grid=(n // tr,),
        in_specs=[qrow, pl.BlockSpec((tr, 128), lambda i: (i, OFF_AK // 128)), vec, vec, tab, tab, qrow, tab],
        out_specs=[qrow, tab, pl.BlockSpec((1, ATT_Q_DIM), lambda i: (0, 0)), vec],
        out_shape=[SDS((n, ATT_Q_DIM), MXU_DTYPE), SDS((n, ATT_KV_DIM), MXU_DTYPE), SDS((1, ATT_Q_DIM), F32), SDS((1, 128), F32)],
        compiler_params=_cp("arbitrary"))(p, p, gq, gk, cos, sin, dq, dk)


ATT_TQ = 256


def _attn_fwd(q, k, v, name):
    n = q.shape[0]
    tq = min(ATT_TQ, n)
    scale = ATT_HEAD_DIM ** -0.5
    gw = ATT_GROUP * ATT_HEAD_DIM

    def body(q_ref, k_ref, v_ref, o_ref):
        kk, vv = k_ref[0], v_ref[0]
        outs = []
        for g in range(ATT_GROUP):
            s = _dot(q_ref[:, g * ATT_HEAD_DIM:(g + 1) * ATT_HEAD_DIM] * scale, kk, "nt")
            e = jnp.exp(s - jnp.max(s, axis=-1, keepdims=True))
            outs.append(_dot(e, vv) / jnp.sum(e, axis=-1, keepdims=True))
        o_ref[...] = jnp.concatenate(outs, axis=-1).astype(o_ref.dtype)

    kv = pl.BlockSpec((1, n, ATT_HEAD_DIM), lambda h, i: (h, 0, 0))
    return pl.pallas_call(
        body, name=name, grid=(ATT_KV_HEADS, n // tq),
        in_specs=[pl.BlockSpec((tq, gw), lambda h, i: (i, h)), kv, kv],
        out_specs=pl.BlockSpec((tq, gw), lambda h, i: (i, h)), out_shape=SDS((n, ATT_Q_DIM), MXU_DTYPE),
        compiler_params=_cp("parallel", "parallel"))(q, k, v)


def _attn_bwd(q, k, v, o, do, name):
    n = q.shape[0]
    tq = min(ATT_TQ, n)
    scale = ATT_HEAD_DIM ** -0.5
    gw = ATT_GROUP * ATT_HEAD_DIM

    def body(q_ref, k_ref, v_ref, o_ref, do_ref, dq_ref, dk_ref, dv_ref):
        @pl.when(pl.program_id(1) == 0)
        def _():
            dk_ref[...] = jnp.zeros_like(dk_ref)
            dv_ref[...] = jnp.zeros_like(dv_ref)

        kk, vv = k_ref[0], v_ref[0]
        dqs = []
        dk_acc = jnp.zeros((ATT_HEAD_DIM, n), F32)
        dv_acc = jnp.zeros((ATT_HEAD_DIM, n), F32)
        for g in range(ATT_GROUP):
            sl = slice(g * ATT_HEAD_DIM, (g + 1) * ATT_HEAD_DIM)
            qg, dog = q_ref[:, sl] * scale, do_ref[:, sl].astype(F32)
            s = _dot(qg, kk, "nt")
            e = jnp.exp(s - jnp.max(s, axis=-1, keepdims=True))
            inv = 1.0 / jnp.sum(e, axis=-1, keepdims=True)
            delta = jnp.sum(dog * o_ref[:, sl].astype(F32), axis=-1, keepdims=True)
            dse = e * (_dot(dog, vv, "nt") - delta)
            dqs.append(_dot(dse, kk) * (inv * scale))
            dk_acc += _dot(qg.astype(F32) * inv, dse, "tn")
            dv_acc += _dot(dog * inv, e, "tn")
        dq_ref[...] = jnp.concatenate(dqs, axis=-1)
        dk_ref[0] += dk_acc
        dv_ref[0] += dv_acc

    kv = pl.BlockSpec((1, n, ATT_HEAD_DIM), lambda h, i: (h, 0, 0))
    kvt = pl.BlockSpec((1, ATT_HEAD_DIM, n), lambda h, i: (h, 0, 0))
    qb = pl.BlockSpec((tq, gw), lambda h, i: (i, h))
    return pl.pallas_call(
        body, name=name, grid=(ATT_KV_HEADS, n // tq), in_specs=[qb, kv, kv, qb, qb], out_specs=[qb, kvt, kvt],
        out_shape=[SDS((n, ATT_Q_DIM), F32), SDS((ATT_KV_HEADS, ATT_HEAD_DIM, n), F32), SDS((ATT_KV_HEADS, ATT_HEAD_DIM, n), F32)],
        compiler_params=_cp("parallel", "arbitrary"))(q, k, v, o, do)


def _both_directions(mats, axis):
    fwd = np.concatenate(mats, axis=axis).astype(np.float32)
    bwd = np.concatenate([m[::-1, ::-1] for m in mats], axis=axis).astype(np.float32)
    return jnp.asarray(np.stack([fwd, bwd]), MXU_DTYPE)


def _hg_segments():
    c = HG_CHUNK
    t = np.arange(c)[:, None]
    r = np.arange(c)[None, :]
    mats = [(r <= t)]
    for lev in range(HG_LEVELS):
        h = c >> (lev + 1)
        mid = (t // (2 * h)) * (2 * h) + h - 1
        hi = (t // h) % 2 == 1
        mats.append(np.where(hi, (r > mid) & (r <= t), (r > t) & (r <= mid)))
    mats.append(r > t)
    return _both_directions(mats, 0)


def _hg_pair_sums():
    c = HG_CHUNK
    r = np.arange(c)[:, None]
    t = np.arange(c)[None, :]
    gp, gn = [t >= r], [t < r]
    for lev in range(HG_LEVELS):
        sh = HG_LEVELS - 1 - lev
        same = (r >> sh) == (t >> sh)
        gp.append(same & (t >= r))
        gn.append(same & (t < r))
    return _both_directions(gp, 1), _both_directions(gn, 1)


def _split_dot(mat, x):
    hi = x.astype(MXU_DTYPE)
    lo = (x - hi.astype(F32)).astype(MXU_DTYPE)
    return _dot(mat, hi) + _dot(mat, lo)


def _hg_gates(hq, z, a0, a1):
    q = hq * _sigmoid(hq)
    sg = _sigmoid(z)
    lb = _sigmoid(a0 - a1)
    f = lb + (1.0 - lb) * sg
    k = (1.0 - lb) * (1.0 - sg)
    return q, f, k, sg, lb


def _hg_level_masks(mirrored):
    c = HG_CHUNK
    row = lax.broadcasted_iota(jnp.int32, (c, 1), 0)
    rr = lax.broadcasted_iota(jnp.int32, (c, c), 0)
    cc = lax.broadcasted_iota(jnp.int32, (c, c), 1)
    his, sames = [], []
    for lev in range(HG_LEVELS):
        sh = HG_LEVELS - 1 - lev
        his.append(jnp.logical_xor(((row >> sh) & 1) == 1, mirrored))
        sames.append((rr >> (sh + 1)) == (cc >> (sh + 1)))
    return his, sames, rr == cc


def _hg_intra(q, k, ex, masks):
    his, sames, eye = masks
    a = jnp.where(eye, jnp.sum(q * k, axis=-1, keepdims=True), 0.0)
    for lev in range(HG_LEVELS):
        e = ex[lev + 1]
        qs = jnp.where(his[lev], q * e, 0.0)
        ks = jnp.where(his[lev], 0.0, k * e)
        a = a + jnp.where(sames[lev], _dot(qs, ks, "nt"), 0.0)
    return a


def _hg_specs(n, with_time):
    c = HG_CHUNK
    nc = n // c

    def chunk(d, i):
        first = d if with_time else 1 - d
        return i + first * (nc - 1 - 2 * i)

    def pcols(off, dir_stride=0):
        return [pl.BlockSpec((c, HG_PAIR), lambda d, i, j=j: (chunk(d, i), off // HG_PAIR + dir_stride // HG_PAIR * d + j)) for j in range(2)]

    specs = dict(
        hq=pcols(OFF_HQ), v=pcols(OFF_HI), z=pcols(OFF_ZF, OFF_ZB - OFF_ZF),
        shared=pl.BlockSpec((c, HG_DIM), lambda d, i: (chunk(d, i), 0)),
        per_dir=pl.BlockSpec((1, c, HG_DIM), lambda d, i: (d, chunk(d, i), 0)),
        vec=pl.BlockSpec((1, 1, HG_DIM), lambda d, i: (d, 0, 0)),
        seg=pl.BlockSpec((1, (HG_LEVELS + 2) * c, c), lambda d, i: (d, 0, 0)),
        sums=pl.BlockSpec((1, c, (HG_LEVELS + 1) * c), lambda d, i: (d, 0, 0)),
        state=pl.BlockSpec((1, HG_HEADS, 1, HG_HEAD_DIM, HG_HEAD_DIM), lambda d, i: (d, 0, chunk(d, i), 0, 0)))
    return nc, specs


def _hg_head(refs, hh):
    off = (hh % 2) * HG_HEAD_DIM
    return refs[hh // 2][:, off:off + HG_HEAD_DIM]


def _hg_lanes(hh):
    return slice(hh * HG_HEAD_DIM, (hh + 1) * HG_HEAD_DIM)


def _hg_exps(seg_ref, f):
    lf = jnp.log(f)
    args = _split_dot(seg_ref[0], lf)
    c = HG_CHUNK
    return [jnp.exp(args[j * c:(j + 1) * c]) for j in range(HG_LEVELS + 2)]


def _hg_last_row(a, mirrored):
    return jnp.where(mirrored, a[0:1, :], a[HG_CHUNK - 1:HG_CHUNK, :])


def _hgrn_fwd(p, a0, a1, seg, name):
    n = p.shape[0]
    nc, sp = _hg_specs(n, True)

    def body(hq0, hq1, z0, z1, v0, v1, a0_ref, a1_ref, seg_ref, o_ref, s0_ref, st):
        @pl.when(pl.program_id(1) == 0)
        def _():
            st[...] = jnp.zeros_like(st)

        mirrored = pl.program_id(0) == 1
        masks = _hg_level_masks(mirrored)
        for hh in range(HG_HEADS):
            ln = _hg_lanes(hh)
            q, f, k, _, _ = _hg_gates(_hg_head((hq0, hq1), hh), _hg_head((z0, z1), hh), a0_ref[0, :, ln], a1_ref[0, :, ln])
            vv = _hg_head((v0, v1), hh)
            ex = _hg_exps(seg_ref, f)
            a = _hg_intra(q, k, ex, masks)
            s_t = st[hh]
            s0_ref[0, hh, 0] = s_t
            o_ref[0, :, ln] = _dot(a, vv) + _dot(q * ex[0], s_t, "nt")
            st[hh] = s_t * _hg_last_row(ex[0], mirrored) + _dot(vv, k * ex[HG_LEVELS + 1], "tn")

    return pl.pallas_call(
        body, name=name, grid=(2, nc), in_specs=sp["hq"] + sp["z"] + sp["v"] + [sp["vec"], sp["vec"], sp["seg"]],
        out_specs=[sp["per_dir"], sp["state"]],
        out_shape=[SDS((2, n, HG_DIM), F32), SDS((2, HG_HEADS, nc, HG_HEAD_DIM, HG_HEAD_DIM), F32)],
        scratch_shapes=[pltpu.VMEM((HG_HEADS, HG_HEAD_DIM, HG_HEAD_DIM), F32)],
        compiler_params=_cp("parallel", "arbitrary"))(p, p, p, p, p, p, a0, a1, seg)


def _hgrn_bwd(p, a0, a1, seg, gp, gn, do, s0, name):
    n = p.shape[0]
    nc, sp = _hg_specs(n, False)


    def body(hq0, hq1, z0, z1, v0, v1, a0_ref, a1_ref, seg_ref, gp_ref, gn_ref, do_ref, s0_ref, dhq_ref, dz_ref, dv_ref, dlb_ref, rt):
        @pl.when(pl.program_id(1) == 0)
        def _():
            rt[...] = jnp.zeros_like(rt)
            dlb_ref[...] = jnp.zeros_like(dlb_ref)

        mirrored = pl.program_id(0) == 1
        masks = _hg_level_masks(mirrored)
        his, sames, eye = masks
        for hh in range(HG_HEADS):
            ln = _hg_lanes(hh)
            hqv = _hg_head((hq0, hq1), hh)
            q, f, k, sg, lb = _hg_gates(hqv, _hg_head((z0, z1), hh), a0_ref[0, :, ln], a1_ref[0, :, ln])
            vv, dov = _hg_head((v0, v1), hh), do_ref[:, ln]
            ex = _hg_exps(seg_ref, f)
            a = _hg_intra(q, k, ex, masks)
            da = _dot(dov, vv, "nt")
            diag = jnp.sum(dov * vv, axis=-1, keepdims=True)
            s_t = s0_ref[0, hh, 0]
            r_t = rt[hh]
            k_end = k * ex[HG_LEVELS + 1]
            dv_ref[0, :, ln] = _dot(a, dov, "tn") + _dot(k_end, r_t, "nt")
            dq_inter = ex[0] * _dot(dov, s_t)
            dk_inter = ex[HG_LEVELS + 1] * _dot(vv, r_t)
            dq = diag * k + dq_inter
            dk = diag * q + dk_inter
            q_terms, k_terms = [q * dq_inter], [k * dk_inter]
            for lev in range(HG_LEVELS):
                e = ex[lev + 1]
                pairs = jnp.where(sames[lev], da, 0.0)
                q_part = jnp.where(his[lev], e, 0.0) * _dot(pairs, jnp.where(his[lev], 0.0, k * e))
                k_part = jnp.where(his[lev], 0.0, e) * _dot(pairs, jnp.where(his[lev], q * e, 0.0), "tn")
                dq, dk = dq + q_part, dk + k_part
                q_terms.append(q * q_part)
                k_terms.append(k * k_part)
            decay = _hg_last_row(ex[0], mirrored)
            rt[hh] = r_t * decay + _dot(dov, q * ex[0], "tn")
            later = decay * jnp.sum(s_t * r_t, axis=0, keepdims=True)
            dlf = _dot(gp_ref[0], jnp.concatenate(q_terms, axis=0)) + _dot(gn_ref[0], jnp.concatenate(k_terms, axis=0)) + later
            df = dlf / f - dk
            dz_ref[0, :, ln] = df * (1.0 - lb) * sg * (1.0 - sg)
            dlb_ref[0, :, ln] += jnp.sum(df * (1.0 - sg), axis=0, keepdims=True)
            sq = _sigmoid(hqv)
            dhq_ref[0, :, ln] = dq * sq * (1.0 + hqv * (1.0 - sq))

    out = SDS((2, n, HG_DIM), F32)
    return pl.pallas_call(
        body, name=name, grid=(2, nc),
        in_specs=sp["hq"] + sp["z"] + sp["v"] + [sp["vec"], sp["vec"], sp["seg"], sp["sums"], sp["sums"], sp["shared"], sp["state"]],
        out_specs=[sp["per_dir"], sp["per_dir"], sp["per_dir"], sp["vec"]], out_shape=[out, out, out, SDS((2, 1, HG_DIM), F32)],
        scratch_shapes=[pltpu.VMEM((HG_HEADS, HG_HEAD_DIM, HG_HEAD_DIM), F32)],
        compiler_params=_cp("parallel", "arbitrary"))(p, p, p, p, p, p, a0, a1, seg, gp, gn, do, s0)


def _hg_post(o2, p, g, name):
    n = p.shape[0]
    tr = min(ROW_TILE, n)
    w = 2 * HG_HEAD_DIM

    def body(of_ref, ob_ref, hg_ref, g_ref, o_ref):
        for j in range(2):
            sl = slice(j * HG_HEAD_DIM, (j + 1) * HG_HEAD_DIM)
            o = of_ref[0, :, sl] + ob_ref[0, :, sl]
            hg = hg_ref[:, sl]
            o_ref[:, sl] = (o * _rstd(o) * g_ref[...] * (hg * _sigmoid(hg))).astype(o_ref.dtype)

    blk = pl.BlockSpec((tr, w), lambda i, j: (i, j))
    dirs = [pl.BlockSpec((1, tr, w), lambda i, j, d=d: (d, i, j)) for d in range(2)]
    return pl.pallas_call(
        body, name=name, grid=(n // tr, HG_DIM // w),
        in_specs=dirs + [pl.BlockSpec((tr, w), lambda i, j: (i, OFF_HG // w + j)), pl.BlockSpec((1, HG_HEAD_DIM), lambda i, j: (0, 0))],
        out_specs=blk, out_shape=SDS((n, HG_DIM), MXU_DTYPE), compiler_params=_cp("parallel", "parallel"))(o2, o2, p, g)


def _hg_post_bwd(o2, p, g, dcat, name):
    n = p.shape[0]
    tr = min(ROW_TILE, n)
    w = 2 * HG_HEAD_DIM

    def body(of_ref, ob_ref, hg_ref, g_ref, d_ref, do_ref, dhg_ref, dg_ref):
        @pl.when(pl.program_id(1) == 0)
        def _():
            dg_ref[...] = jnp.zeros_like(dg_ref)

        for j in range(2):
            sl = slice(j * HG_HEAD_DIM, (j + 1) * HG_HEAD_DIM)
            o = of_ref[0, :, sl] + ob_ref[0, :, sl]
            hg = hg_ref[:, sl]
            d = d_ref[:, sl].astype(F32)
            sg = _sigmoid(hg)
            on = o * _rstd(o) * g_ref[...]
            dhg_ref[:, sl] = (d * on * sg * (1.0 + hg * (1.0 - sg))).astype(dhg_ref.dtype)
            dx, dg = _rms_bwd(o, g_ref[...], d * hg * sg)
            do_ref[:, sl] = dx
            dg_ref[0, :, sl] += dg

    blk = pl.BlockSpec((tr, w), lambda j, i: (i, j))
    dirs = [pl.BlockSpec((1, tr, w), lambda j, i, d=d: (d, i, j)) for d in range(2)]
    return pl.pallas_call(
        body, name=name, grid=(HG_DIM // w, n // tr),
        in_specs=dirs + [pl.BlockSpec((tr, w), lambda j, i: (i, OFF_HG // w + j)), pl.BlockSpec((1, HG_HEAD_DIM), lambda j, i: (0, 0)),
                         pl.BlockSpec((tr, w), lambda j, i: (i, ATT_Q_DIM // w + j))],
        out_specs=[blk, blk, pl.BlockSpec((1, 1, w), lambda j, i: (j, 0, 0))],
        out_shape=[SDS((n, HG_DIM), F32), SDS((n, HG_DIM), MXU_DTYPE), SDS((HG_DIM // w, 1, w), F32)],
        compiler_params=_cp("parallel", "arbitrary"))(o2, o2, p, g, dcat)


XATT_TQ = 512


def _xattn_fwd(q, kv, name):
    n, nm = q.shape[0], kv.shape[0]
    tq = min(XATT_TQ, n)
    scale = X_HEAD_DIM ** -0.5

    def body(q_ref, k_ref, v_ref, o_ref):
        s = _dot(q_ref[...], k_ref[...], "nt") * scale
        e = jnp.exp(s - jnp.max(s, axis=-1, keepdims=True))
        o_ref[...] = _dot(e / jnp.sum(e, axis=-1, keepdims=True), v_ref[...]).astype(o_ref.dtype)

    qb = pl.BlockSpec((tq, X_HEAD_DIM), lambda h, i: (i, h))
    return pl.pallas_call(
        body, name=name, grid=(X_HEADS, n // tq),
        in_specs=[qb, pl.BlockSpec((nm, X_HEAD_DIM), lambda h, i: (0, h)), pl.BlockSpec((nm, X_HEAD_DIM), lambda h, i: (0, X_HEADS + h))],
        out_specs=qb, out_shape=SDS(q.shape, MXU_DTYPE), compiler_params=_cp("parallel", "parallel"))(q, kv, kv)


def _xattn_bwd(q, kv, do, name):
    n, nm = q.shape[0], kv.shape[0]
    tq = min(XATT_TQ, n)
    scale = X_HEAD_DIM ** -0.5

    def body(q_ref, k_ref, v_ref, do_ref, dq_ref, dk_ref, dv_ref):
        @pl.when(pl.program_id(1) == 0)
        def _():
            dk_ref[...] = jnp.zeros_like(dk_ref)
            dv_ref[...] = jnp.zeros_like(dv_ref)

        qv, dov = q_ref[...], do_ref[...]
        s = _dot(qv, k_ref[...], "nt") * scale
        e = jnp.exp(s - jnp.max(s, axis=-1, keepdims=True))
        p = e / jnp.sum(e, axis=-1, keepdims=True)
        dp = _dot(dov, v_ref[...], "nt")
        ds = p * (dp - jnp.sum(p * dp, axis=-1, keepdims=True)) * scale
        dq_ref[...] = _dot(ds, k_ref[...]).astype(dq_ref.dtype)
        dk_ref[...] += _dot(ds, qv, "tn")
        dv_ref[...] += _dot(p, dov, "tn")

    qb = pl.BlockSpec((tq, X_HEAD_DIM), lambda h, i: (i, h))
    kb = pl.BlockSpec((nm, X_HEAD_DIM), lambda h, i: (0, h))
    return pl.pallas_call(
        body, name=name, grid=(X_HEADS, n // tq),
        in_specs=[qb, kb, pl.BlockSpec((nm, X_HEAD_DIM), lambda h, i: (0, X_HEADS + h)), qb], out_specs=[qb, kb, kb],
        out_shape=[SDS(q.shape, MXU_DTYPE), SDS((nm, X_HEADS * X_HEAD_DIM), F32), SDS((nm, X_HEADS * X_HEAD_DIM), F32)],
        compiler_params=_cp("parallel", "arbitrary"))(q, kv, kv, do)


def _shift_rows(u, down):
    n = u.shape[0]
    row = lax.broadcasted_iota(jnp.int32, u.shape, 0)
    if down:
        return jnp.where(row == 0, 0.0, pltpu.roll(u, 1, axis=0))
    return jnp.where(row == n - 1, 0.0, pltpu.roll(u, n - 1, axis=0))


def _conv(u, w, b):
    return b + _shift_rows(u, True) * w[0:1, :] + u * w[1:2, :] + _shift_rows(u, False) * w[2:3, :]


def _ff_specs(n):
    gate = lambda rows: pl.BlockSpec((rows, FF_COLS), lambda j: (0, j))
    val = lambda rows: pl.BlockSpec((rows, FF_COLS), lambda j: (0, FF_BLOCKS + j))
    return [gate(n), val(n), gate(3), val(3), gate(1), val(1)], gate


def _conv_gate(u, cw, cb, name):
    n = u.shape[0]
    ins, gate_blk = _ff_specs(n)

    def body(ug_ref, uv_ref, wg_ref, wv_ref, bg_ref, bv_ref, o_ref):
        gate = _conv(ug_ref[...], wg_ref[...], bg_ref[...])
        val = _conv(uv_ref[...], wv_ref[...], bv_ref[...])
        o_ref[...] = (gate * _sigmoid(gate) * val).astype(o_ref.dtype)

    return pl.pallas_call(
        body, name=name, grid=(FF_BLOCKS,), in_specs=ins, out_specs=gate_blk(n), out_shape=SDS((n, D_FF), MXU_DTYPE),
        compiler_params=_cp("parallel"))(u, u, cw, cw, cb, cb)


def _conv_gate_bwd(u, cw, cb, da, name):
    n = u.shape[0]
    ins, gate_blk = _ff_specs(n)

    def side(dacc, u, w, du_ref, dw_ref, db_ref):
        nxt, prv = _shift_rows(dacc, False), _shift_rows(dacc, True)
        du_ref[...] = (nxt * w[0:1, :] + dacc * w[1:2, :] + prv * w[2:3, :]).astype(du_ref.dtype)
        db_ref[...] = jnp.sum(dacc, axis=0, keepdims=True)
        dw_ref[0:1, :] = jnp.sum(nxt * u, axis=0, keepdims=True)
        dw_ref[1:2, :] = jnp.sum(dacc * u, axis=0, keepdims=True)
        dw_ref[2:3, :] = jnp.sum(prv * u, axis=0, keepdims=True)

    def body(ug_ref, uv_ref, wg_ref, wv_ref, bg_ref, bv_ref, da_ref, dug_ref, duv_ref, dwg_ref, dwv_ref, dbg_ref, dbv_ref):
        ug, uv = ug_ref[...], uv_ref[...]
        gate = _conv(ug, wg_ref[...], bg_ref[...])
        val = _conv(uv, wv_ref[...], bv_ref[...])
        sg = _sigmoid(gate)
        dav = da_ref[...].astype(F32)
        side(dav * val * sg * (1.0 + gate * (1.0 - sg)), ug, wg_ref[...], dug_ref, dwg_ref, dbg_ref)
        side(dav * gate * sg, uv, wv_ref[...], duv_ref, dwv_ref, dbv_ref)

    return pl.pallas_call(
        body, name=name, grid=(FF_BLOCKS,), in_specs=ins + [gate_blk(n)],
        out_specs=[gate_blk(n), gate_blk(n), gate_blk(3), gate_blk(3), gate_blk(1), gate_blk(1)],
        out_shape=[SDS((n, D_FF), MXU_DTYPE)] * 2 + [SDS((3, D_FF), F32)] * 2 + [SDS((1, D_FF), F32)] * 2,
        compiler_params=_cp("parallel"))(u, u, cw, cw, cb, cb, da)


def _adamw(w, g, m, v, name):
    r, c = w.shape
    tr = r if r <= 512 else 256 if r % 256 == 0 else 88
    assert r % tr == 0, (name, r, tr)

    def body(w_ref, g_ref, m_ref, v_ref, d_ref, mo_ref, vo_ref, go_ref):
        gv = g_ref[...]
        go_ref[...] = gv
        mn = ADAM_B1 * m_ref[...] + (1.0 - ADAM_B1) * gv
        vn = ADAM_B2 * v_ref[...] + (1.0 - ADAM_B2) * gv * gv
        m_hat = mn / (1.0 - ADAM_B1 ** ADAM_STEP)
        v_hat = vn / (1.0 - ADAM_B2 ** ADAM_STEP)
        d_ref[...] = -ADAM_LR * (m_hat / (jnp.sqrt(v_hat) + ADAM_EPS) + ADAM_WD * w_ref[...])
        mo_ref[...] = mn
        vo_ref[...] = vn

    blk = pl.BlockSpec((tr, c), lambda i: (i, 0))
    out = SDS((r, c), F32)
    return pl.pallas_call(body, name=name, grid=(r // tr,), in_specs=[blk] * 4, out_specs=[blk] * 4, out_shape=[out] * 4,
                          compiler_params=_cp("parallel"))(w, g, m, v)


def _half_tile(h):
    tr = h if h <= 512 else 256 if h % 256 == 0 else 176
    assert h % tr == 0, (h, tr)
    return tr


def _add_halves(g, r, core, name):
    s, h, c = r.shape
    tr = _half_tile(h)
    steps = h // tr

    def body(ix_ref, g_ref, r_ref, o_ref):
        o_ref[...] = (g_ref[...].astype(F32) + r_ref[...].astype(F32)).astype(o_ref.dtype)

    blk = pl.BlockSpec((1, tr, c), lambda i, j, ix: (i, j, 0))
    grid_spec = pltpu.PrefetchScalarGridSpec(
        num_scalar_prefetch=1, grid=(s, steps),
        in_specs=[pl.BlockSpec((1, tr, c), lambda i, j, ix: (i, ix[0] * steps + j, 0)), blk], out_specs=blk)
    return pl.pallas_call(body, name=name, grid_spec=grid_spec, out_shape=SDS(r.shape, WIRE_DTYPE),
                          compiler_params=_cp("parallel", "parallel"))(core.reshape(1), g, r)


def _sum_chips(own, recv, me, core, name):
    _, h, c = own.shape
    tr = _half_tile(h)

    def body(ix_ref, own_ref, recv_ref, o_ref):
        acc = own_ref[0].astype(F32)
        for j in range(3):
            acc = acc + recv_ref[j].astype(F32)
        o_ref[0] = acc

    grid_spec = pltpu.PrefetchScalarGridSpec(
        num_scalar_prefetch=1, grid=(h // tr,),
        in_specs=[pl.BlockSpec((1, tr, c), lambda i, ix: (ix[0], i, 0)), pl.BlockSpec((3, tr, c), lambda i, ix: (0, i, 0))],
        out_specs=pl.BlockSpec((1, tr, c), lambda i, ix: (ix[1], i, 0)))
    return pl.pallas_call(body, name=name, grid_spec=grid_spec, out_shape=SDS((2, h, c), F32),
                          compiler_params=_cp("parallel"))(jnp.stack([me, core]), own, recv)


ANY = pl.BlockSpec(memory_space=pl.ANY)


def _place():
    x, y, c = lax.axis_index("x"), lax.axis_index("y"), lax.axis_index("c")
    return x, y, c, [(1 - x, y), (x, 1 - y), (1 - x, 1 - y)]


def _gather_shards(shards, name):
    nt = len(shards)

    def body(*refs):
        ins, outs = refs[:nt], refs[nt:2 * nt]
        send, recv, fsend, frecv, osend, orecv = refs[2 * nt:]
        x, y, c, chips = _place()
        me = 2 * x + y

        def half(t, chip, cc):
            h = ins[t].shape[0] // 2
            return outs[t].at[chip, pl.ds(cc * h, h)]

        def ici(t, j):
            cx, cy = chips[j]
            h = ins[t].shape[0] // 2
            return pltpu.make_async_remote_copy(src_ref=ins[t].at[pl.ds(c * h, h)], dst_ref=half(t, me, c),
                                                send_sem=send.at[t, j], recv_sem=recv.at[t, j], device_id=(cx, cy, c), device_id_type=MESH)

        def landed(t, j):
            cx, cy = chips[j]
            blk = half(t, 2 * cx + cy, c)
            return pltpu.make_async_remote_copy(src_ref=blk, dst_ref=blk, send_sem=send.at[t, j], recv_sem=recv.at[t, j],
                                                device_id=(cx, cy, c), device_id_type=MESH)

        def d2d(t, j, cc):
            cx, cy = chips[j]
            blk = half(t, 2 * cx + cy, cc)
            return pltpu.make_async_remote_copy(src_ref=blk, dst_ref=blk, send_sem=fsend.at[t, j], recv_sem=frecv.at[t, j],
                                                device_id=(x, y, 1 - c), device_id_type=MESH)

        own = [pltpu.make_async_remote_copy(src_ref=ins[t], dst_ref=outs[t].at[me], send_sem=osend.at[t], recv_sem=orecv.at[t],
                                            device_id=(x, y, 1 - c), device_id_type=MESH) for t in range(nt)]
        for t in range(nt):
            for j in range(3):
                ici(t, j).start()
        for cp in own:
            cp.start()
        for t in range(nt):
            for j in range(3):
                landed(t, j).wait_recv()
                d2d(t, j, c).start()
        for t in range(nt):
            for j in range(3):
                d2d(t, j, 1 - c).wait_recv()
        for t in range(nt):
            for j in range(3):
                ici(t, j).wait_send()
                d2d(t, j, c).wait_send()
        for cp in own:
            cp.wait()

    return pl.pallas_call(
        body, name=name, in_specs=[ANY] * nt, out_specs=[ANY] * nt,
        out_shape=[SDS((4,) + s.shape, s.dtype) for s in shards],
        scratch_shapes=[pltpu.SemaphoreType.DMA((nt, 3))] * 4 + [pltpu.SemaphoreType.DMA((nt,))] * 2,
        compiler_params=pltpu.CompilerParams(has_side_effects=True))(*shards)


def _swap_halves(grads, name):
    nt = len(grads)

    def body(*refs):
        ins, outs = refs[:nt], refs[nt:2 * nt]
        send, recv = refs[2 * nt:]
        x, y, c, _ = _place()
        cps = []
        for t in range(nt):
            h = ins[t].shape[1] // 2
            cps.append(pltpu.make_async_remote_copy(src_ref=ins[t].at[pl.ds(0, 4), pl.ds((1 - c) * h, h)], dst_ref=outs[t], send_sem=send.at[t],
                                                    recv_sem=recv.at[t], device_id=(x, y, 1 - c), device_id_type=MESH))
        for cp in cps:
            cp.start()
        for cp in cps:
            cp.wait()

    return pl.pallas_call(
        body, name=name, in_specs=[ANY] * nt, out_specs=[ANY] * nt,
        out_shape=[SDS((g.shape[0], g.shape[1] // 2, g.shape[2]), g.dtype) for g in grads],
        scratch_shapes=[pltpu.SemaphoreType.DMA((nt,))] * 2,
        compiler_params=pltpu.CompilerParams(has_side_effects=True))(*grads)


def _send_to_owners(parts, name):
    nt = len(parts)

    def body(*refs):
        ins, outs = refs[:nt], refs[nt:2 * nt]
        send, recv = refs[2 * nt:]
        x, y, c, chips = _place()

        def ici(t, j):
            cx, cy = chips[j]
            return pltpu.make_async_remote_copy(src_ref=ins[t].at[2 * cx + cy], dst_ref=outs[t].at[j], send_sem=send.at[t, j],
                                                recv_sem=recv.at[t, j], device_id=(cx, cy, c), device_id_type=MESH)

        for t in range(nt):
            for j in range(3):
                ici(t, j).start()
        for t in range(nt):
            for j in range(3):
                ici(t, j).wait()

    return pl.pallas_call(
        body, name=name, in_specs=[ANY] * nt, out_specs=[ANY] * nt, out_shape=[SDS((3,) + p.shape[1:], p.dtype) for p in parts],
        scratch_shapes=[pltpu.SemaphoreType.DMA((nt, 3))] * 2,
        compiler_params=pltpu.CompilerParams(has_side_effects=True))(*parts)


def _join_halves(bufs, name):
    nt = len(bufs)

    def body(*refs):
        outs = refs[nt:2 * nt]
        send, recv = refs[2 * nt:]
        x, y, c, _ = _place()
        cps = [pltpu.make_async_remote_copy(src_ref=outs[t].at[c], dst_ref=outs[t].at[c], send_sem=send.at[t], recv_sem=recv.at[t],
                                            device_id=(x, y, 1 - c), device_id_type=MESH) for t in range(nt)]
        for cp in cps:
            cp.start()
        for t in range(nt):
            theirs = outs[t].at[1 - c]
            pltpu.make_async_remote_copy(src_ref=theirs, dst_ref=theirs, send_sem=send.at[t], recv_sem=recv.at[t],
                                         device_id=(x, y, 1 - c), device_id_type=MESH).wait_recv()
        for cp in cps:
            cp.wait_send()

    return pl.pallas_call(
        body, name=name, in_specs=[ANY] * nt, out_specs=[ANY] * nt, out_shape=[SDS(b.shape, b.dtype) for b in bufs],
        input_output_aliases={t: t for t in range(nt)},
        scratch_shapes=[pltpu.SemaphoreType.DMA((nt,))] * 2,
        compiler_params=pltpu.CompilerParams(has_side_effects=True))(*bufs)


def _exchange_small(v, reduce, name, after=()):
    rows = v.shape[0]
    after, after_specs = _unread(after)

    def body(v_ref, *rest):
        o_ref, buf, send, recv = rest[-4:]
        x, y, c, _ = _place()
        me = 4 * x + 2 * y + c
        buf[me] = v_ref[...]

        def peer(dx, dy, dc):
            return (1 - x if dx else x, 1 - y if dy else y, 1 - c if dc else c)

        peers = [(dx, dy, dc) for dx in range(2) for dy in range(2) for dc in range(2) if (dx, dy, dc) != (0, 0, 0)]
        cps = []
        for j, (dx, dy, dc) in enumerate(peers):
            cps.append(pltpu.make_async_remote_copy(src_ref=v_ref, dst_ref=buf.at[me], send_sem=send.at[j], recv_sem=recv.at[j],
                                                    device_id=peer(dx, dy, dc), device_id_type=MESH))
        for cp in cps:
            cp.start()
        for j, (dx, dy, dc) in enumerate(peers):
            px, py, pc = peer(dx, dy, dc)
            blk = buf.at[4 * px + 2 * py + pc]
            pltpu.make_async_remote_copy(src_ref=blk, dst_ref=blk, send_sem=send.at[j], recv_sem=recv.at[j],
                                         device_id=(px, py, pc), device_id_type=MESH).wait_recv()
        for cp in cps:
            cp.wait_send()
        if reduce:
            acc = buf[0]
            for j in range(1, 8):
                acc = acc + buf[j]
            o_ref[...] = acc
        else:
            o_ref[...] = buf[...]

    vm = pl.BlockSpec(memory_space=pltpu.VMEM)
    return pl.pallas_call(
        body, name=name, in_specs=[vm] + after_specs, out_specs=vm, out_shape=SDS((rows, 128) if reduce else (8, rows, 128), F32),
        scratch_shapes=[pltpu.VMEM((8, rows, 128), F32), pltpu.SemaphoreType.DMA((7,)), pltpu.SemaphoreType.DMA((7,))],
        compiler_params=pltpu.CompilerParams(has_side_effects=True))(v, *after)


HBM = pl.BlockSpec(memory_space=pltpu.HBM)
SEM = pl.BlockSpec(memory_space=pltpu.SEMAPHORE)
TOKEN = pl.BlockSpec(memory_space=pltpu.VMEM)
TOKEN_SHAPE = SDS((8, 128), F32)
PEERS = 7


def _in_hbm(a):
    return pltpu.with_memory_space_constraint(a, pltpu.HBM)


def _split_params():
    return pltpu.CompilerParams(has_side_effects=pltpu.SideEffectType.DATAFLOW_SIDE_EFFECTING)


def _gather_start(shards, name, after=()):
    nt = len(shards)
    after, after_specs = _unread(after)

    def body(*refs):
        ins, lands = refs[:nt], refs[nt:2 * nt]
        outs = refs[2 * nt + len(after):]
        sends, recvs = outs[:nt], outs[nt:2 * nt]
        x, y, c, chips = _place()
        me = 2 * x + y
        for t in range(nt):
            h = ins[t].shape[0] // 2
            mine = pl.ds(c * h, h)
            for j, (cx, cy) in enumerate(chips):
                for dc in range(2):
                    pltpu.make_async_remote_copy(src_ref=ins[t].at[mine], dst_ref=lands[t].at[me, mine], send_sem=sends[t].at[2 * j + dc],
                                                 recv_sem=recvs[t].at[2 * j + c], device_id=(cx, cy, dc), device_id_type=MESH).start()
            pltpu.make_async_remote_copy(src_ref=ins[t], dst_ref=lands[t].at[me], send_sem=sends[t].at[PEERS - 1], recv_sem=recvs[t].at[PEERS - 1],
                                         device_id=(x, y, 1 - c), device_id_type=MESH).start()
        outs[-1][...] = jnp.zeros(TOKEN_SHAPE.shape, F32)

    lands = [lax.empty((4,) + s.shape, s.dtype) for s in shards]
    out = pl.pallas_call(
        body, name=name, in_specs=[HBM] * (2 * nt) + after_specs, out_specs=[SEM] * (2 * nt) + [HBM] * (2 * nt) + [TOKEN],
        out_shape=[pltpu.SemaphoreType.DMA((PEERS,))] * (2 * nt)
        + [pltpu.HBM(s.shape, s.dtype) for s in shards] + [pltpu.HBM(l.shape, l.dtype) for l in lands] + [TOKEN_SHAPE],
        input_output_aliases={t: 2 * nt + t for t in range(2 * nt)}, compiler_params=_split_params())(
            *[_in_hbm(s) for s in shards], *[_in_hbm(l) for l in lands], *after)
    return out[:nt], out[nt:2 * nt], out[2 * nt:3 * nt], out[3 * nt:4 * nt], out[-1]


def _gather_wait(sends, recvs, shards, lands, after, name):
    nt = len(shards)

    def body(*refs):
        ins, lands_ref = refs[:nt], refs[nt:2 * nt]
        send_refs, recv_refs = refs[2 * nt:3 * nt], refs[3 * nt:4 * nt]
        x, y, c, chips = _place()
        for t in range(nt):
            h = ins[t].shape[0] // 2
            for j, (cx, cy) in enumerate(chips):
                for cs in range(2):
                    blk = lands_ref[t].at[2 * cx + cy, pl.ds(cs * h, h)]
                    pltpu.make_async_remote_copy(src_ref=blk, dst_ref=blk, send_sem=send_refs[t].at[2 * j + cs], recv_sem=recv_refs[t].at[2 * j + cs],
                                                 device_id=(cx, cy, cs), device_id_type=MESH).wait()
            blk = lands_ref[t].at[2 * x + y]
            pltpu.make_async_remote_copy(src_ref=blk, dst_ref=blk, send_sem=send_refs[t].at[PEERS - 1], recv_sem=recv_refs[t].at[PEERS - 1],
                                         device_id=(x, y, 1 - c), device_id_type=MESH).wait()

    out = pl.pallas_call(
        body, name=name, in_specs=[HBM] * (2 * nt) + [SEM] * (2 * nt) + [ANY], out_specs=[HBM] * (2 * nt),
        out_shape=[pltpu.HBM(s.shape, s.dtype) for s in shards] + [pltpu.HBM(l.shape, l.dtype) for l in lands],
        input_output_aliases={t: t for t in range(2 * nt)}, compiler_params=_split_params())(*shards, *lands, *sends, *recvs, after)
    return out[nt:]


def _scatter_start(g, name):
    _, r, c_ = g.shape
    h = r // 2

    def body(g_ref, land, send, recv, g_thru, land_thru, token):
        x, y, c, chips = _place()
        for j, (cx, cy) in enumerate(chips):
            for dc in range(2):
                pltpu.make_async_remote_copy(src_ref=g_ref.at[2 * cx + cy, pl.ds(dc * h, h)], dst_ref=land.at[2 * j + c], send_sem=send.at[2 * j + dc],
                                             recv_sem=recv.at[2 * j + c], device_id=(cx, cy, dc), device_id_type=MESH).start()
        pltpu.make_async_remote_copy(src_ref=g_ref.at[2 * x + y, pl.ds((1 - c) * h, h)], dst_ref=land.at[PEERS - 1], send_sem=send.at[PEERS - 1],
                                     recv_sem=recv.at[PEERS - 1], device_id=(x, y, 1 - c), device_id_type=MESH).start()
        token[...] = jnp.zeros(TOKEN_SHAPE.shape, F32)

    land = lax.empty((PEERS, h, c_), g.dtype)
    return pl.pallas_call(
        body, name=name, in_specs=[HBM, HBM], out_specs=[SEM, SEM, HBM, HBM, TOKEN],
        out_shape=[pltpu.SemaphoreType.DMA((PEERS,)), pltpu.SemaphoreType.DMA((PEERS,)), pltpu.HBM(g.shape, g.dtype),
                   pltpu.HBM(land.shape, land.dtype), TOKEN_SHAPE],
        input_output_aliases={0: 2, 1: 3}, compiler_params=_split_params())(_in_hbm(g), _in_hbm(land))


def _scatter_wait(started, after, name):
    nt = len(started)

    def body(*refs):
        lands = refs[nt:2 * nt]
        sends, recvs = refs[2 * nt:3 * nt], refs[3 * nt:4 * nt]
        x, y, c, chips = _place()
        peers = [(cx, cy, dc) for cx, cy in chips for dc in range(2)] + [(x, y, 1 - c)]
        for t in range(nt):
            for k, peer in enumerate(peers):
                blk = lands[t].at[k]
                pltpu.make_async_remote_copy(src_ref=blk, dst_ref=blk, send_sem=sends[t].at[k], recv_sem=recvs[t].at[k],
                                             device_id=peer, device_id_type=MESH).wait()

    gs, lands = [s[2] for s in started], [s[3] for s in started]
    after, after_specs = _unread(after)
    out = pl.pallas_call(
        body, name=name, in_specs=[HBM] * (2 * nt) + [SEM] * (2 * nt) + after_specs, out_specs=[HBM] * (2 * nt),
        out_shape=[pltpu.HBM(a.shape, a.dtype) for a in gs + lands],
        input_output_aliases={t: t for t in range(2 * nt)}, compiler_params=_split_params())(
            *gs, *lands, *[s[0] for s in started], *[s[1] for s in started], *after)
    return out[:nt], out[nt:]


def _sum_devices(g, land, me, core, name):
    npeer, h, c = land.shape
    tr = _half_tile(h)
    steps = h // tr

    def body(ix_ref, own_ref, land_ref, o_ref):
        acc = own_ref[0].astype(F32)
        for j in range(npeer):
            acc = acc + land_ref[j].astype(F32)
        o_ref[0] = acc

    grid_spec = pltpu.PrefetchScalarGridSpec(
        num_scalar_prefetch=1, grid=(steps,),
        in_specs=[pl.BlockSpec((1, tr, c), lambda i, ix: (ix[0], ix[1] * steps + i, 0)), pl.BlockSpec((npeer, tr, c), lambda i, ix: (0, i, 0))],
        out_specs=pl.BlockSpec((1, tr, c), lambda i, ix: (ix[1], i, 0)))
    return pl.pallas_call(body, name=name, grid_spec=grid_spec, out_shape=SDS((2, h, c), F32),
                          compiler_params=_cp("parallel"))(jnp.stack([me, core]), g, land)


def _pack_small(parts):
    flat = jnp.concatenate([p.reshape(-1) for p in parts])
    total = flat.shape[0]
    rows = -(-total // 1024) * 8
    return jnp.pad(flat, (0, rows * 128 - total)).reshape(rows, 128)


def _unpack_small(packed, shapes):
    flat = packed.reshape(-1)
    out, off = [], 0
    for s in shapes:
        size = int(np.prod(s))
        out.append(flat[off:off + size].reshape(s))
        off += size
    return out


def _local_step(x, mem, target, w_in, first_after, mid_weights, ffn_weights, on_grad, gains, conv_w, conv_b, hg_lb):
    n = x.shape[0]
    cos, sin = _rope_tables(n)
    seg = _hg_segments()
    gp, gn = _hg_pair_sums()
    gq2 = jnp.tile(gains["q_norm_g"], (1, 2))
    gk2 = jnp.tile(gains["k_norm_g"], (1, 2))
    a0 = hg_lb[:, 0:1, :]
    a1 = hg_lb[:, 1:2, :]

    p, h1 = _norm_mm(x, gains["pre_mix_g"], w_in, F32, TOKEN_TILE, 1664, "in_proj", after=(first_after,))
    qr, kr = _qk_prep(p, gq2, gk2, cos, sin, "qk_prep")
    heads = lambda a: a.reshape(n, ATT_KV_HEADS, ATT_HEAD_DIM).transpose(1, 0, 2)
    kh = heads(kr)
    vh = heads(p[:, OFF_AV:OFF_AV + ATT_KV_DIM].astype(MXU_DTYPE))
    att = _attn_fwd(qr, kh, vh, "attn_fwd")
    o2, s0 = _hgrn_fwd(p, a0, a1, seg, "hgrn_fwd")
    rec = _hg_post(o2, p, gains["hg_out_norm_g"], "hg_post")
    cat = jnp.concatenate([att, rec], axis=1)
    w_out, w_xq, w_xkv, w_xo = mid_weights(cat)
    mixed = _mm(cat, w_out, "nn", F32, TOKEN_TILE, 1024, "out_proj")
    x1 = _resid_norm(x, mixed, gains["post_mix_g"], "mix_resid")
    xq, h2 = _norm_mm(x1, gains["pre_x_g"], w_xq, MXU_DTYPE, TOKEN_TILE, 1024, "xq_proj")
    kv, mn = _norm_mm(mem, gains["mem_norm_g"], w_xkv, MXU_DTYPE, 256, 2048, "xkv_proj")
    ox = _xattn_fwd(xq, kv, "xattn_fwd")
    xo = _mm(ox, w_xo, "nn", F32, TOKEN_TILE, 1024, "xo_proj")
    x2 = _resid_norm(x1, xo, gains["post_x_g"], "x_resid")
    w_up, w_down = ffn_weights(x2)
    u, h3 = _norm_mm(x2, gains["pre_ffn_g"], w_up, F32, TOKEN_TILE, 1408, "up_proj")
    act = _conv_gate(u, conv_w, conv_b, "conv_gate")
    dn = _mm(act, w_down, "nn", F32, TOKEN_TILE, 1024, "down_proj")
    d3, loss = _final_loss(x2, dn, gains["post_ffn_g"], target, "ffn_resid_loss")

    gs = {}
    d_dn, gs["post_ffn_g"] = _norm_bwd(dn, gains["post_ffn_g"], d3, None, MXU_DTYPE, "ffn_post_bwd")
    tok = on_grad("w_down", _mm(act, d_dn, "tn", WIRE_DTYPE, 1408, 1024, "down_dw"))
    d_act = _mm(d_dn, w_down, "nt", F32, TOKEN_TILE, 1408, "down_dx", after=(tok,))
    du_g, du_v, dcw_g, dcw_v, dcb_g, dcb_v = _conv_gate_bwd(u, conv_w, conv_b, d_act, "conv_gate_bwd")
    gs["conv_w"] = jnp.concatenate([dcw_g, dcw_v], axis=1)
    gs["conv_b"] = jnp.concatenate([dcb_g, dcb_v], axis=1)
    ff_shard = w_up.shape[2]
    g_up = _dw_by_owner(h3, du_g, ff_shard, 0, None, 512, "up_dw_gate")
    tok = on_grad("w_up", _dw_by_owner(h3, du_v, ff_shard, 2, g_up, 512, "up_dw_value"))
    d_h3 = _mm_nt_parts([(du_g, 0), (du_g, 1), (du_v, 0), (du_v, 1)], w_up, F32, TOKEN_TILE, 512, "up_dx", after=(tok,))
    d2, gs["pre_ffn_g"] = _norm_bwd(x2, gains["pre_ffn_g"], d_h3, d3, F32, "ffn_pre_bwd")
    d_xo, gs["post_x_g"] = _norm_bwd(xo, gains["post_x_g"], d2, None, MXU_DTYPE, "x_post_bwd")
    tok = on_grad("w_xo", _mm(ox, d_xo, "tn", WIRE_DTYPE, 512, 1024, "xo_dw"))
    d_ox = _mm(d_xo, w_xo, "nt", MXU_DTYPE, TOKEN_TILE, 1024, "xo_dx", after=(tok,))
    d_xq, d_k, d_v = _xattn_bwd(xq, kv, d_ox, "xattn_bwd")
    d_kv = jnp.concatenate([d_k, d_v], axis=1).astype(MXU_DTYPE)
    tok = on_grad("w_xq", _mm(h2, d_xq, "tn", WIRE_DTYPE, 512, 1024, "xq_dw"))
    tok_kv = on_grad("w_xkv", _dw_by_owner(mn, d_kv, w_xkv.shape[2], 0, None, 512, "xkv_dw"))
    d_h2 = _mm(d_xq, w_xq, "nt", F32, TOKEN_TILE, 1024, "xq_dx", after=(tok, tok_kv))
    d_mn = _mm_nt_parts([(d_kv, s) for s in range(4)], w_xkv, F32, 256, 1024, "xkv_dx")
    _, gs["mem_norm_g"] = _norm_bwd(mem, gains["mem_norm_g"], d_mn, None, MXU_DTYPE, "mem_norm_bwd")
    d1, gs["pre_x_g"] = _norm_bwd(x1, gains["pre_x_g"], d_h2, d2, F32, "x_pre_bwd")
    d_mixed, gs["post_mix_g"] = _norm_bwd(mixed, gains["post_mix_g"], d1, None, MXU_DTYPE, "mix_post_bwd")
    tok = on_grad("w_out", _mm(cat, d_mixed, "tn", WIRE_DTYPE, 512, 1024, "out_dw"))
    d_cat = _mm(d_mixed, w_out, "nt", MXU_DTYPE, TOKEN_TILE, 1024, "out_dx", after=(tok,))
    d_o, d_hg, dg_hg = _hg_post_bwd(o2, p, gains["hg_out_norm_g"], d_cat, "hg_post_bwd")
    gs["hg_out_norm_g"] = dg_hg.reshape(HG_HEADS, HG_HEAD_DIM).sum(axis=0, keepdims=True)
    dhq2, dz2, dhv2, dlb = _hgrn_bwd(p, a0, a1, seg, gp, gn, d_o, s0, "hgrn_bwd")
    lb = jax.nn.sigmoid(a0 - a1)
    da0 = dlb * lb * (1.0 - lb)
    gs["hg_lb"] = jnp.concatenate([da0, -da0], axis=1)
    d_qr, d_kh, d_vh = _attn_bwd(qr, kh, vh, cat, d_cat, "attn_bwd")
    unheads = lambda a: a.transpose(2, 0, 1).reshape(n, ATT_KV_DIM)
    d_aq, d_ak, dgq, dgk = _qk_prep_bwd(p, gq2, gk2, cos, sin, d_qr, unheads(d_kh), "qk_prep_bwd")
    gs["q_norm_g"] = dgq.reshape(ATT_HEADS, ATT_HEAD_DIM).sum(axis=0, keepdims=True)
    gs["k_norm_g"] = dgk.reshape(ATT_KV_HEADS, ATT_HEAD_DIM).sum(axis=0, keepdims=True)
    d_p = jnp.concatenate([d_aq, d_ak, unheads(d_vh).astype(MXU_DTYPE), (dhq2[0] + dhq2[1]).astype(MXU_DTYPE),
                           dz2[0].astype(MXU_DTYPE), dz2[1].astype(MXU_DTYPE), (dhv2[0] + dhv2[1]).astype(MXU_DTYPE), d_hg], axis=1)
    tok = on_grad("w_in", _mm(h1, d_p, "tn", WIRE_DTYPE, 512, 1664, "in_dw"))
    d_h1 = _mm(d_p, w_in, "nt", F32, TOKEN_TILE, 1024, "in_dx", after=(tok,))
    grad_x, gs["pre_mix_g"] = _norm_bwd(x, gains["pre_mix_g"], d_h1, d1, F32, "mix_pre_bwd")
    return loss, grad_x, gs


MATS = ("w_in", "w_out", "w_xq", "w_xkv", "w_xo", "w_up", "w_down")
GAINS = ("pre_mix_g", "q_norm_g", "k_norm_g", "hg_out_norm_g", "post_mix_g", "pre_x_g", "mem_norm_g", "post_x_g", "pre_ffn_g", "post_ffn_g")
WEIGHTS = ('pre_mix_g', 'w_in', 'q_norm_g', 'k_norm_g', 'hg_lb', 'hg_out_norm_g', 'w_out', 'post_mix_g', 'pre_x_g', 'mem_norm_g', 'w_xq',
           'w_xkv', 'w_xo', 'post_x_g', 'pre_ffn_g', 'w_up', 'conv_w', 'conv_b', 'w_down', 'post_ffn_g')


def kernel(x, mem, pre_mix_g, w_in, q_norm_g, k_norm_g, hg_lb, hg_out_norm_g, w_out, post_mix_g, pre_x_g, mem_norm_g, w_xq, w_xkv, w_xo, post_x_g, pre_ffn_g, w_up, conv_w, conv_b, w_down, post_ffn_g, loss_target, m_pre_mix_g, m_w_in, m_q_norm_g, m_k_norm_g, m_hg_lb, m_hg_out_norm_g, m_w_out, m_post_mix_g, m_pre_x_g, m_mem_norm_g, m_w_xq, m_w_xkv, m_w_xo, m_post_x_g, m_pre_ffn_g, m_w_up, m_conv_w, m_conv_b, m_w_down, m_post_ffn_g, v_pre_mix_g, v_w_in, v_q_norm_g, v_k_norm_g, v_hg_lb, v_hg_out_norm_g, v_w_out, v_post_mix_g, v_pre_x_g, v_mem_norm_g, v_w_xq, v_w_xkv, v_w_xo, v_post_x_g, v_pre_ffn_g, v_w_up, v_conv_w, v_conv_b, v_w_down, v_post_ffn_g):
    args = dict(locals())
    w = {k: args[k] for k in WEIGHTS}
    m = {k: args["m_" + k] for k in WEIGHTS}
    v = {k: args["v_" + k] for k in WEIGHTS}
    chip = 2 * lax.axis_index("x") + lax.axis_index("y")
    core = lax.axis_index("c")

    shards = {k: w[k][0].astype(WIRE_DTYPE) for k in MATS}

    def whole(k, g):
        return g if k in ("w_xkv", "w_up") else g.reshape(-1, g.shape[-1])

    w_in_shards = _gather_shards([shards["w_in"]], "gather_w_in")[0]
    w_in_full = jnp.concatenate([w_in_shards[s] for s in range(4)], axis=1)
    small_in = _exchange_small(_pack_small([w["conv_w"][0], w["hg_lb"]]), False, "gather_small")
    mid_names, ffn_names = ("w_out", "w_xq", "w_xkv", "w_xo"), ("w_up", "w_down")
    mid = _gather_start([shards[k] for k in mid_names], "gather_mid_start", after=(w_in_full, small_in))
    ffn = _gather_start([shards[k] for k in ffn_names], "gather_ffn_start", after=(mid[4],))

    def mid_weights(after):
        return [whole(k, g) for k, g in zip(mid_names, _gather_wait(*mid[:4], after, "gather_mid_wait"))]

    def ffn_weights(after):
        return [whole(k, g) for k, g in zip(ffn_names, _gather_wait(*ffn[:4], after, "gather_ffn_wait"))]

    cw_parts, lb_parts = [], []
    for s in range(4):
        cw_s, lb_s = _unpack_small(small_in[2 * s], [w["conv_w"][0].shape, w["hg_lb"].shape])
        cw_parts.append(cw_s)
        lb_parts.append(lb_s)
    conv_w_full = jnp.concatenate(cw_parts, axis=1)
    hg_lb_full = jnp.concatenate(lb_parts, axis=2)

    started = {}

    def on_grad(k, g):
        if k == "w_in":
            g = g.reshape(g.shape[0], 4, g.shape[1] // 4).transpose(1, 0, 2)
        elif g.ndim == 2:
            g = g.reshape(4, g.shape[0] // 4, g.shape[1])
        *started[k], token = _scatter_start(g, "grad_start_" + k)
        return token

    gains = {k: w[k] for k in GAINS}
    loss_part, grad_x, gs = _local_step(x[0], mem[0], loss_target[0], w_in_full, ffn[4], mid_weights, ffn_weights, on_grad, gains,
                                        conv_w_full, w["conv_b"], hg_lb_full)
    gs["loss"] = loss_part

    grads, delta, new_m, new_v = {}, {}, {}, {}

    def reduce_matrices(names, after, tag):
        sent, landed = _scatter_wait([started[k] for k in names], after, "grad_wait_" + tag)
        halves = [_sum_devices(g, land, chip, core, "grad_sum_" + k) for k, g, land in zip(names, sent, landed)]
        for k, r in zip(names, _join_halves(halves, "grad_join_" + tag)):
            grads[k] = r.reshape(1, -1, r.shape[-1])

    def adamw(names):
        for k in names:
            shape = w[k].shape
            two_d = lambda a: a.reshape(-1, shape[-1])
            d, mo, vo, go = _adamw(two_d(w[k]), two_d(grads[k]), two_d(m[k]), two_d(v[k]), "adamw_" + k)
            delta[k], new_m[k], new_v[k], grads[k] = d.reshape(shape), mo.reshape(shape), vo.reshape(shape), go.reshape(shape)

    early = tuple(k for k in MATS if k != "w_in")
    reduce_matrices(early, (grad_x,), "early")
    adamw(early)

    small_names = GAINS + ("conv_b", "conv_w", "hg_lb")
    packed = _pack_small([gs[k] for k in small_names + ("loss",)])
    reduced_small = _exchange_small(packed, True, "reduce_small", after=tuple(new_v[k] for k in early))
    *summed, loss = _unpack_small(reduced_small, [gs[k].shape for k in small_names + ("loss",)])
    loss = loss[0, 0]
    for k, g in zip(small_names, summed):
        grads[k] = g
    ncw = w["conv_w"].shape[2]
    grads["conv_w"] = lax.dynamic_slice_in_dim(grads["conv_w"], chip * ncw, ncw, axis=1)[None]
    nlb = w["hg_lb"].shape[2]
    grads["hg_lb"] = lax.dynamic_slice_in_dim(grads["hg_lb"], chip * nlb, nlb, axis=2)
    replicated = GAINS + ("conv_b",)
    shapes = [w[k].shape for k in replicated]
    rows = sum(int(np.prod(s)) for s in shapes) // 128
    pack = lambda d: jnp.concatenate([d[k].reshape(-1) for k in replicated]).reshape(rows, 128)
    outs = _adamw(pack(w), reduced_small[:rows], pack(m), pack(v), "adamw_replicated")
    for into, packed_out in zip((delta, new_m, new_v, grads), outs):
        for k, a in zip(replicated, _unpack_small(packed_out, shapes)):
            into[k] = a
    adamw(("conv_w", "hg_lb"))

    reduce_matrices(("w_in",), tuple(new_v[k] for k in early + small_names), "late")
    adamw(("w_in",))
    return (loss, grad_x[None], *[grads[k] for k in WEIGHTS], *[delta[k] for k in WEIGHTS],
            *[new_m[k] for k in WEIGHTS], *[new_v[k] for k in WEIGHTS])
```

```python
import functools

import numpy as np
import jax
import jax.numpy as jnp
from jax import lax
from jax.experimental import pallas as pl
from jax.experimental.pallas import tpu as pltpu

F32 = jnp.float32
MXU_DTYPE = jnp.bfloat16
WIRE_DTYPE = jnp.bfloat16
VMEM_LIMIT_BYTES = 56 * 1024 * 1024
EPS = 1e-6
MESH = pl.DeviceIdType.MESH

D_MODEL = 1024
GRID_W = 64
ATT_HEADS, ATT_KV_HEADS, ATT_HEAD_DIM = 8, 2, 64
ATT_GROUP = ATT_HEADS // ATT_KV_HEADS
ATT_Q_DIM, ATT_KV_DIM = 512, 128
ROPE_THETA = 10000.0
HG_HEADS, HG_HEAD_DIM, HG_DIM = 4, 128, 512
HG_CHUNK = 128
HG_LEVELS = 7
HG_PAIR = 2 * HG_HEAD_DIM
X_HEADS, X_HEAD_DIM = 4, 256
D_FF = 2816
FF_COLS = 256
FF_BLOCKS = D_FF // FF_COLS
N_IN = 3328
OFF_AQ, OFF_AK, OFF_AV, OFF_HQ, OFF_ZF, OFF_ZB, OFF_HI, OFF_HG = 0, 512, 640, 768, 1280, 1792, 2304, 2816

ADAM_LR, ADAM_B1, ADAM_B2, ADAM_EPS, ADAM_WD, ADAM_STEP = 0.001, 0.9, 0.999, 1e-08, 0.01, 10

SDS = jax.ShapeDtypeStruct


def _cp(*sem):
    return pltpu.CompilerParams(dimension_semantics=sem, vmem_limit_bytes=VMEM_LIMIT_BYTES)


def _dot(a, b, form="nn"):
    dims = {"nn": (((1,), (0,)), ((), ())), "nt": (((1,), (1,)), ((), ())), "tn": (((0,), (0,)), ((), ()))}[form]
    return lax.dot_general(a.astype(MXU_DTYPE), b.astype(MXU_DTYPE), dims, preferred_element_type=F32)


def _sigmoid(x):
    return 1.0 / (1.0 + jnp.exp(-x))


def _rstd(x):
    return lax.rsqrt(jnp.mean(x * x, axis=-1, keepdims=True) + EPS)


def _rms_bwd(x, g, dy):
    r = _rstd(x)
    xh = x * r
    dn = dy * g
    dx = r * (dn - xh * jnp.mean(dn * xh, axis=-1, keepdims=True))
    return dx, jnp.sum(dy * xh, axis=0, keepdims=True)


def _unread(after):
    after = tuple(a for a in after if a is not None)
    return after, [pl.BlockSpec(memory_space=pl.ANY)] * len(after)


def _mm(a, b, form, out_dtype, tm, tn, name, after=()):
    after, after_specs = _unread(after)
    if form == "nn":
        (m, k), n = a.shape, b.shape[1]
    elif form == "nt":
        (m, k), n = a.shape, b.shape[0]
    else:
        (k, m), n = a.shape, b.shape[1]
    tm, tn = min(tm, m), min(tn, n)
    assert m % tm == 0 and n % tn == 0, (name, m, n, tm, tn)

    def body(a_ref, b_ref, *rest):
        o_ref = rest[-1]
        o_ref[...] = _dot(a_ref[...], b_ref[...], form).astype(o_ref.dtype)

    a_spec = pl.BlockSpec((k, tm), lambda i, j: (0, i)) if form == "tn" else pl.BlockSpec((tm, k), lambda i, j: (i, 0))
    b_spec = pl.BlockSpec((tn, k), lambda i, j: (j, 0)) if form == "nt" else pl.BlockSpec((k, tn), lambda i, j: (0, j))
    return pl.pallas_call(
        body, name=name, grid=(m // tm, n // tn), in_specs=[a_spec, b_spec] + after_specs,
        out_specs=pl.BlockSpec((tm, tn), lambda i, j: (i, j)), out_shape=SDS((m, n), out_dtype),
        compiler_params=_cp("parallel", "parallel"))(a, b, *after)


def _mm_nt_parts(a_parts, b, out_dtype, tm, tn, name, after=()):
    after, after_specs = _unread(after)
    parts, n, p = b.shape
    m = a_parts[0][0].shape[0]
    tm, tn = min(tm, m), min(tn, n)
    assert m % tm == 0 and n % tn == 0 and len(a_parts) == parts, (name, m, b.shape)

    def body(*refs):
        o_ref = refs[-1]
        acc = _dot(refs[0][...], refs[parts][0], "nt")
        for s in range(1, parts):
            acc = acc + _dot(refs[s][...], refs[parts + s][0], "nt")
        o_ref[...] = acc.astype(o_ref.dtype)

    a_specs = [pl.BlockSpec((tm, p), lambda i, j, cb=cb: (i, cb)) for _, cb in a_parts]
    b_specs = [pl.BlockSpec((1, tn, p), lambda i, j, s=s: (s, j, 0)) for s in range(parts)]
    return pl.pallas_call(
        body, name=name, grid=(m // tm, n // tn), in_specs=a_specs + b_specs + after_specs,
        out_specs=pl.BlockSpec((tm, tn), lambda i, j: (i, j)), out_shape=SDS((m, n), out_dtype),
        compiler_params=_cp("parallel", "parallel"))(*[arr for arr, _ in a_parts], *([b] * parts), *after)


def _dx_norm_bwd(a_parts, b, x, g, res, tm, name, after=()):
    after, after_specs = _unread(after)
    parts, d, p = b.shape
    n = x.shape[0]
    tm = min(tm, n)
    assert n % tm == 0 and len(a_parts) == parts and x.shape[1] == d, (name, x.shape, b.shape)

    def body(*refs):
        x_ref, g_ref, res_ref = refs[2 * parts:2 * parts + 3]
        dx_ref, dg_ref = refs[-2:]
        dh = _dot(refs[0][...], refs[parts][0], "nt")
        for s in range(1, parts):
            dh = dh + _dot(refs[s][...], refs[parts + s][0], "nt")
        dx, dg = _rms_bwd(x_ref[...], g_ref[...], dh)
        dx_ref[...] = dx + res_ref[...]

        @pl.when(pl.program_id(0) == 0)
        def _():
            dg_ref[...] = jnp.zeros_like(dg_ref)

        dg_ref[...] += dg

    a_specs = [pl.BlockSpec((tm, p), lambda i, cb=cb: (i, cb)) for _, cb in a_parts]
    b_specs = [pl.BlockSpec((1, d, p), lambda i, s=s: (s, 0, 0)) for s in range(parts)]
    row = pl.BlockSpec((tm, d), lambda i: (i, 0))
    vec = pl.BlockSpec((1, d), lambda i: (0, 0))
    return pl.pallas_call(
        body, name=name, grid=(n // tm,), in_specs=a_specs + b_specs + [row, vec, row] + after_specs,
        out_specs=[row, vec], out_shape=[SDS((n, d), F32), SDS((1, d), F32)],
        compiler_params=_cp("arbitrary"))(*[arr for arr, _ in a_parts], *([b] * parts), x, g, res, *after)


def _dw_by_owner(a, b, tn, first, into, tm, name):
    k, m = a.shape
    cnt = b.shape[1] // tn
    tm = min(tm, m)
    assert m % tm == 0 and b.shape[1] == cnt * tn and first + cnt <= 4, (name, a.shape, b.shape)

    def body(a_ref, b_ref, *rest):
        rest[-1][0] = _dot(a_ref[...], b_ref[...], "tn").astype(rest[-1].dtype)

    extra = [] if into is None else [into]
    return pl.pallas_call(
        body, name=name, grid=(m // tm, cnt),
        in_specs=[pl.BlockSpec((k, tm), lambda i, j: (0, i)), pl.BlockSpec((k, tn), lambda i, j: (0, j))] + [pl.BlockSpec(memory_space=pl.ANY)] * len(extra),
        out_specs=pl.BlockSpec((1, tm, tn), lambda i, j: (first + j, i, 0)), out_shape=SDS((4, m, tn), WIRE_DTYPE),
        input_output_aliases={2: 0} if extra else {},
        compiler_params=_cp("parallel", "parallel"))(a, b, *extra)


def _norm_mm(x, g, w, out_dtype, tm, tn, name, after=()):
    after, after_specs = _unread(after)
    m, d = x.shape
    sharded = w.ndim == 3
    n = w.shape[-1] * (w.shape[0] if sharded else 1)
    tm, tn = min(tm, m), (w.shape[-1] if sharded else min(tn, n))
    assert m % tm == 0 and n % tn == 0, (name, m, n, tm, tn)

    def body(x_ref, g_ref, w_ref, *rest):
        o_ref, h_ref, hs = rest[-3:]

        @pl.when(pl.program_id(1) == 0)
        def _():
            xv = x_ref[...]
            h = (xv * _rstd(xv) * g_ref[...]).astype(MXU_DTYPE)
            hs[...] = h
            h_ref[...] = h

        o_ref[...] = _dot(hs[...], w_ref[0] if sharded else w_ref[...]).astype(o_ref.dtype)

    w_spec = pl.BlockSpec((1, d, tn), lambda i, j: (j, 0, 0)) if sharded else pl.BlockSpec((d, tn), lambda i, j: (0, j))
    return pl.pallas_call(
        body, name=name, grid=(m // tm, n // tn),
        in_specs=[pl.BlockSpec((tm, d), lambda i, j: (i, 0)), pl.BlockSpec((1, d), lambda i, j: (0, 0)), w_spec] + after_specs,
        out_specs=[pl.BlockSpec((tm, tn), lambda i, j: (i, j)), pl.BlockSpec((tm, d), lambda i, j: (i, 0))],
        out_shape=[SDS((m, n), out_dtype), SDS((m, d), MXU_DTYPE)],
        scratch_shapes=[pltpu.VMEM((tm, d), MXU_DTYPE)],
        compiler_params=_cp("parallel", "arbitrary"))(x, g, w, *after)


ROW_TILE = 256
TOKEN_TILE = 1024


def _resid_norm(x, y, g, name):
    n, d = x.shape
    tr = min(ROW_TILE, n)

    def body(x_ref, y_ref, g_ref, o_ref):
        yv = y_ref[...]
        o_ref[...] = x_ref[...] + yv * _rstd(yv) * g_ref[...]

    row = pl.BlockSpec((tr, d), lambda i: (i, 0))
    return pl.pallas_call(
        body, name=name, grid=(n // tr,), in_specs=[row, row, pl.BlockSpec((1, d), lambda i: (0, 0))],
        out_specs=row, out_shape=SDS((n, d), F32), compiler_params=_cp("parallel"))(x, y, g)


def _norm_bwd(x, g, dy, res, out_dtype, name):
    n, d = x.shape
    tr = min(ROW_TILE, n)
    has_res = res is not None

    def body(*refs):
        x_ref, g_ref, dy_ref = refs[:3]
        dx_ref, dg_ref = refs[-2:]
        dx, dg = _rms_bwd(x_ref[...], g_ref[...], dy_ref[...].astype(F32))
        if has_res:
            dx = dx + refs[3][...]
        dx_ref[...] = dx.astype(dx_ref.dtype)

        @pl.when(pl.program_id(0) == 0)
        def _():
            dg_ref[...] = jnp.zeros_like(dg_ref)

        dg_ref[...] += dg

    row = pl.BlockSpec((tr, d), lambda i: (i, 0))
    vec = pl.BlockSpec((1, d), lambda i: (0, 0))
    ins = [x, g, dy] + ([res] if has_res else [])
    return pl.pallas_call(
        body, name=name, grid=(n // tr,), in_specs=[row, vec, row] + ([row] if has_res else []),
        out_specs=[row, vec], out_shape=[SDS((n, d), out_dtype), SDS((1, d), F32)],
        compiler_params=_cp("arbitrary"))(*ins)


def _final_loss(x, y, g, target, name):
    n, d = x.shape
    tr = min(ROW_TILE, n)

    def body(x_ref, y_ref, g_ref, t_ref, d_ref, l_ref):
        yv = y_ref[...]
        diff = x_ref[...] + yv * _rstd(yv) * g_ref[...] - t_ref[...]
        d_ref[...] = diff * (1.0 / d)

        @pl.when(pl.program_id(0) == 0)
        def _():
            l_ref[...] = jnp.zeros_like(l_ref)

        l_ref[...] += 0.5 * jnp.sum(jnp.mean(diff * diff, axis=-1, keepdims=True), axis=0, keepdims=True)

    row = pl.BlockSpec((tr, d), lambda i: (i, 0))
    return pl.pallas_call(
        body, name=name, grid=(n // tr,), in_specs=[row, row, pl.BlockSpec((1, d), lambda i: (0, 0)), row],
        out_specs=[row, pl.BlockSpec((1, 1), lambda i: (0, 0))], out_shape=[SDS((n, d), F32), SDS((1, 1), F32)],
        compiler_params=_cp("arbitrary"))(x, y, g, target)


def _rope_tables(n):
    pairs = ATT_HEAD_DIM // 4
    t = np.arange(n)
    inv = np.power(ROPE_THETA, -np.arange(pairs, dtype=np.float32) / pairs).astype(np.float32)
    ang = np.concatenate([(t // GRID_W)[:, None].astype(np.float32) * inv, (t % GRID_W)[:, None].astype(np.float32) * inv], axis=-1)
    cos = np.repeat(np.cos(ang), 2, axis=-1)
    sin = np.repeat(np.sin(ang), 2, axis=-1) * np.tile(np.array([-1.0, 1.0], np.float32), ATT_HEAD_DIM // 2)
    return jnp.asarray(np.tile(cos, 2), F32), jnp.asarray(np.tile(sin, 2), F32)


def _swap_pairs(x):
    lane = lax.broadcasted_iota(jnp.int32, x.shape, 1)
    return jnp.where((lane & 1) == 0, pltpu.roll(x, 127, axis=1), pltpu.roll(x, 1, axis=1))


def _head_mean(v):
    lane = lax.broadcasted_iota(jnp.int32, v.shape, 1)
    lo = jnp.where(lane < ATT_HEAD_DIM, v, 0.0)
    s0 = jnp.sum(lo, axis=-1, keepdims=True)
    s1 = jnp.sum(v - lo, axis=-1, keepdims=True)
    return jnp.where(lane < ATT_HEAD_DIM, s0, s1) * (1.0 / ATT_HEAD_DIM)


def _qk_prep(p, gq, gk, cos, sin, name):
    n = p.shape[0]
    tr = min(ROW_TILE, n)

    def one(xv, g, c, s):
        xn = xv * lax.rsqrt(_head_mean(xv * xv) + EPS) * g
        return xn * c + _swap_pairs(xn) * s

    def body(q_ref, k_ref, gq_ref, gk_ref, c_ref, s_ref, qo_ref, ko_ref):
        c, s = c_ref[...], s_ref[...]
        for j in range(ATT_Q_DIM // 128):
            qo_ref[:, j * 128:(j + 1) * 128] = one(q_ref[:, j * 128:(j + 1) * 128], gq_ref[...], c, s).astype(qo_ref.dtype)
        ko_ref[...] = one(k_ref[...], gk_ref[...], c, s).astype(ko_ref.dtype)

    vec = pl.BlockSpec((1, 128), lambda i: (0, 0))
    tab = pl.BlockSpec((tr, 128), lambda i: (i, 0))
    return pl.pallas_call(
        body, name=name, grid=(n // tr,),
        in_specs=[pl.BlockSpec((tr, ATT_Q_DIM), lambda i: (i, 0)), pl.BlockSpec((tr, 128), lambda i: (i, OFF_AK // 128)), vec, vec, tab, tab],
        out_specs=[pl.BlockSpec((tr, ATT_Q_DIM), lambda i: (i, 0)), tab],
        out_shape=[SDS((n, ATT_Q_DIM), MXU_DTYPE), SDS((n, ATT_KV_DIM), MXU_DTYPE)],
        compiler_params=_cp("parallel"))(p, p, gq, gk, cos, sin)


def _qk_prep_bwd(p, gq, gk, cos, sin, dq, dk, name):
    n = p.shape[0]
    tr = min(ROW_TILE, n)

    def one(xv, g, c, s, dout):
        dxn = dout * c + _swap_pairs(dout * s)
        r = lax.rsqrt(_head_mean(xv * xv) + EPS)
        xh = xv * r
        dn = dxn * g
        dx = r * (dn - xh * _head_mean(dn * xh))
        return dx, jnp.sum(dxn * xh, axis=0, keepdims=True)

    def body(q_ref, k_ref, gq_ref, gk_ref, c_ref, s_ref, dq_ref, dk_ref, dqo_ref, dko_ref, dgq_ref, dgk_ref):
        @pl.when(pl.program_id(0) == 0)
        def _():
            dgq_ref[...] = jnp.zeros_like(dgq_ref)
            dgk_ref[...] = jnp.zeros_like(dgk_ref)

        c, s = c_ref[...], s_ref[...]
        for j in range(ATT_Q_DIM // 128):
            sl = slice(j * 128, (j + 1) * 128)
            dx, dg = one(q_ref[:, sl], gq_ref[...], c, s, dq_ref[:, sl])
            dqo_ref[:, sl] = dx.astype(dqo_ref.dtype)
            dgq_ref[:, sl] += dg
        dx, dg = one(k_ref[...], gk_ref[...], c, s, dk_ref[...])
        dko_ref[...] = dx.astype(dko_ref.dtype)
        dgk_ref[...] += dg

    vec = pl.BlockSpec((1, 128), lambda i: (0, 0))
    tab = pl.BlockSpec((tr, 128), lambda i: (i, 0))
    qrow = pl.BlockSpec((tr, ATT_Q_DIM), lambda i: (i, 0))
    return pl.pallas_call(
        body, name=name, grid=(n // tr,),
        in_specs=[qrow, pl.BlockSpec((tr, 128), lambda i: (i, OFF_AK // 128)), vec, vec, tab, tab, qrow, tab],
        out_specs=[qrow, tab, pl.BlockSpec((1, ATT_Q_DIM), lambda i: (0, 0)), vec],
        out_shape=[SDS((n, ATT_Q_DIM), MXU_DTYPE), SDS((n, ATT_KV_DIM), MXU_DTYPE), SDS((1, ATT_Q_DIM), F32), SDS((1, 128), F32)],
        compiler_params=_cp("arbitrary"))(p, p, gq, gk, cos, sin, dq, dk)


ATT_TQ = 256


def _attn_fwd(q, k, v, name):
    n = q.shape[0]
    tq = min(ATT_TQ, n)
    scale = ATT_HEAD_DIM ** -0.5
    gw = ATT_GROUP * ATT_HEAD_DIM

    def body(q_ref, k_ref, v_ref, o_ref):
        kk, vv = k_ref[0], v_ref[0]
        outs = []
        for g in range(ATT_GROUP):
            s = _dot(q_ref[:, g * ATT_HEAD_DIM:(g + 1) * ATT_HEAD_DIM] * scale, kk, "nt")
            e = jnp.exp(s - jnp.max(s, axis=-1, keepdims=True))
            outs.append(_dot(e, vv) / jnp.sum(e, axis=-1, keepdims=True))
        o_ref[...] = jnp.concatenate(outs, axis=-1).astype(o_ref.dtype)

    kv = pl.BlockSpec((1, n, ATT_HEAD_DIM), lambda h, i: (h, 0, 0))
    return pl.pallas_call(
        body, name=name, grid=(ATT_KV_HEADS, n // tq),
        in_specs=[pl.BlockSpec((tq, gw), lambda h, i: (i, h)), kv, kv],
        out_specs=pl.BlockSpec((tq, gw), lambda h, i: (i, h)), out_shape=SDS((n, ATT_Q_DIM), MXU_DTYPE),
        compiler_params=_cp("parallel", "parallel"))(q, k, v)


def _attn_bwd(q, k, v, o, do, name):
    n = q.shape[0]
    tq = min(ATT_TQ, n)
    scale = ATT_HEAD_DIM ** -0.5
    gw = ATT_GROUP * ATT_HEAD_DIM

    def body(q_ref, k_ref, v_ref, o_ref, do_ref, dq_ref, dk_ref, dv_ref):
        @pl.when(pl.program_id(1) == 0)
        def _():
            dk_ref[...] = jnp.zeros_like(dk_ref)
            dv_ref[...] = jnp.zeros_like(dv_ref)

        kk, vv = k_ref[0], v_ref[0]
        dqs = []
        dk_acc = jnp.zeros((ATT_HEAD_DIM, n), F32)
        dv_acc = jnp.zeros((ATT_HEAD_DIM, n), F32)
        for g in range(ATT_GROUP):
            sl = slice(g * ATT_HEAD_DIM, (g + 1) * ATT_HEAD_DIM)
            qg, dog = q_ref[:, sl] * scale, do_ref[:, sl].astype(F32)
            s = _dot(qg, kk, "nt")
            e = jnp.exp(s - jnp.max(s, axis=-1, keepdims=True))
            inv = 1.0 / jnp.sum(e, axis=-1, keepdims=True)
            delta = jnp.sum(dog * o_ref[:, sl].astype(F32), axis=-1, keepdims=True)
            dse = e * (_dot(dog, vv, "nt") - delta)
            dqs.append(_dot(dse, kk) * (inv * scale))
            dk_acc += _dot(qg.astype(F32) * inv, dse, "tn")
            dv_acc += _dot(dog * inv, e, "tn")
        dq_ref[...] = jnp.concatenate(dqs, axis=-1)
        dk_ref[0] += dk_acc
        dv_ref[0] += dv_acc

    kv = pl.BlockSpec((1, n, ATT_HEAD_DIM), lambda h, i: (h, 0, 0))
    kvt = pl.BlockSpec((1, ATT_HEAD_DIM, n), lambda h, i: (h, 0, 0))
    qb = pl.BlockSpec((tq, gw), lambda h, i: (i, h))
    return pl.pallas_call(
        body, name=name, grid=(ATT_KV_HEADS, n // tq), in_specs=[qb, kv, kv, qb, qb], out_specs=[qb, kvt, kvt],
        out_shape=[SDS((n, ATT_Q_DIM), F32), SDS((ATT_KV_HEADS, ATT_HEAD_DIM, n), F32), SDS((ATT_KV_HEADS, ATT_HEAD_DIM, n), F32)],
        compiler_params=_cp("parallel", "arbitrary"))(q, k, v, o, do)


def _both_directions(mats, axis):
    fwd = np.concatenate(mats, axis=axis).astype(np.float32)
    bwd = np.concatenate([m[::-1, ::-1] for m in mats], axis=axis).astype(np.float32)
    return jnp.asarray(np.stack([fwd, bwd]), MXU_DTYPE)


def _hg_segments():
    c = HG_CHUNK
    t = np.arange(c)[:, None]
    r = np.arange(c)[None, :]
    mats = [(r <= t)]
    for lev in range(HG_LEVELS):
        h = c >> (lev + 1)
        mid = (t // (2 * h)) * (2 * h) + h - 1
        hi = (t // h) % 2 == 1
        mats.append(np.where(hi, (r > mid) & (r <= t), (r > t) & (r <= mid)))
    mats.append(r > t)
    return _both_directions(mats, 0)


def _hg_pair_sums():
    c = HG_CHUNK
    r = np.arange(c)[:, None]
    t = np.arange(c)[None, :]
    gp, gn = [t >= r], [t < r]
    for lev in range(HG_LEVELS):
        sh = HG_LEVELS - 1 - lev
        same = (r >> sh) == (t >> sh)
        gp.append(same & (t >= r))
        gn.append(same & (t < r))
    return _both_directions(gp, 1), _both_directions(gn, 1)


def _split_dot(mat, x):
    hi = x.astype(MXU_DTYPE)
    lo = (x - hi.astype(F32)).astype(MXU_DTYPE)
    return _dot(mat, hi) + _dot(mat, lo)


def _hg_gates(hq, z, a0, a1):
    q = hq * _sigmoid(hq)
    sg = _sigmoid(z)
    lb = _sigmoid(a0 - a1)
    f = lb + (1.0 - lb) * sg
    k = (1.0 - lb) * (1.0 - sg)
    return q, f, k, sg, lb


def _hg_level_masks(mirrored):
    c = HG_CHUNK
    row = lax.broadcasted_iota(jnp.int32, (c, 1), 0)
    rr = lax.broadcasted_iota(jnp.int32, (c, c), 0)
    cc = lax.broadcasted_iota(jnp.int32, (c, c), 1)
    his, sames = [], []
    for lev in range(HG_LEVELS):
        sh = HG_LEVELS - 1 - lev
        his.append(jnp.logical_xor(((row >> sh) & 1) == 1, mirrored))
        sames.append((rr >> (sh + 1)) == (cc >> (sh + 1)))
    return his, sames, rr == cc


def _hg_intra(q, k, ex, masks):
    his, sames, eye = masks
    a = jnp.where(eye, jnp.sum(q * k, axis=-1, keepdims=True), 0.0)
    for lev in range(HG_LEVELS):
        e = ex[lev + 1]
        qs = jnp.where(his[lev], q * e, 0.0)
        ks = jnp.where(his[lev], 0.0, k * e)
        a = a + jnp.where(sames[lev], _dot(qs, ks, "nt"), 0.0)
    return a


def _hg_specs(n, with_time):
    c = HG_CHUNK
    nc = n // c

    def chunk(d, i):
        first = d if with_time else 1 - d
        return i + first * (nc - 1 - 2 * i)

    def pcols(off, dir_stride=0):
        return [pl.BlockSpec((c, HG_PAIR), lambda d, i, j=j: (chunk(d, i), off // HG_PAIR + dir_stride // HG_PAIR * d + j)) for j in range(2)]

    specs = dict(
        hq=pcols(OFF_HQ), v=pcols(OFF_HI), z=pcols(OFF_ZF, OFF_ZB - OFF_ZF),
        shared=pl.BlockSpec((c, HG_DIM), lambda d, i: (chunk(d, i), 0)),
        per_dir=pl.BlockSpec((1, c, HG_DIM), lambda d, i: (d, chunk(d, i), 0)),
        vec=pl.BlockSpec((1, 1, HG_DIM), lambda d, i: (d, 0, 0)),
        seg=pl.BlockSpec((1, (HG_LEVELS + 2) * c, c), lambda d, i: (d, 0, 0)),
        sums=pl.BlockSpec((1, c, (HG_LEVELS + 1) * c), lambda d, i: (d, 0, 0)),
        state=pl.BlockSpec((1, HG_HEADS, 1, HG_HEAD_DIM, HG_HEAD_DIM), lambda d, i: (d, 0, chunk(d, i), 0, 0)))
    return nc, specs


def _hg_head(refs, hh):
    off = (hh % 2) * HG_HEAD_DIM
    return refs[hh // 2][:, off:off + HG_HEAD_DIM]


def _hg_lanes(hh):
    return slice(hh * HG_HEAD_DIM, (hh + 1) * HG_HEAD_DIM)


def _hg_exps(seg_ref, f):
    lf = jnp.log(f)
    args = _split_dot(seg_ref[0], lf)
    c = HG_CHUNK
    return [jnp.exp(args[j * c:(j + 1) * c]) for j in range(HG_LEVELS + 2)]


def _hg_last_row(a, mirrored):
    return jnp.where(mirrored, a[0:1, :], a[HG_CHUNK - 1:HG_CHUNK, :])


def _hgrn_fwd(p, a0, a1, seg, name):
    n = p.shape[0]
    nc, sp = _hg_specs(n, True)

    def body(hq0, hq1, z0, z1, v0, v1, a0_ref, a1_ref, seg_ref, o_ref, s0_ref, st):
        @pl.when(pl.program_id(1) == 0)
        def _():
            st[...] = jnp.zeros_like(st)

        mirrored = pl.program_id(0) == 1
        masks = _hg_level_masks(mirrored)
        for hh in range(HG_HEADS):
            ln = _hg_lanes(hh)
            q, f, k, _, _ = _hg_gates(_hg_head((hq0, hq1), hh), _hg_head((z0, z1), hh), a0_ref[0, :, ln], a1_ref[0, :, ln])
            vv = _hg_head((v0, v1), hh)
            ex = _hg_exps(seg_ref, f)
            a = _hg_intra(q, k, ex, masks)
            s_t = st[hh]
            s0_ref[0, hh, 0] = s_t
            o_ref[0, :, ln] = _dot(a, vv) + _dot(q * ex[0], s_t, "nt")
            st[hh] = s_t * _hg_last_row(ex[0], mirrored) + _dot(vv, k * ex[HG_LEVELS + 1], "tn")

    return pl.pallas_call(
        body, name=name, grid=(2, nc), in_specs=sp["hq"] + sp["z"] + sp["v"] + [sp["vec"], sp["vec"], sp["seg"]],
        out_specs=[sp["per_dir"], sp["state"]],
        out_shape=[SDS((2, n, HG_DIM), F32), SDS((2, HG_HEADS, nc, HG_HEAD_DIM, HG_HEAD_DIM), F32)],
        scratch_shapes=[pltpu.VMEM((HG_HEADS, HG_HEAD_DIM, HG_HEAD_DIM), F32)],
        compiler_params=_cp("parallel", "arbitrary"))(p, p, p, p, p, p, a0, a1, seg)


def _hgrn_bwd(p, a0, a1, seg, gp, gn, do, s0, name):
    n = p.shape[0]
    nc, sp = _hg_specs(n, False)


    def body(hq0, hq1, z0, z1, v0, v1, a0_ref, a1_ref, seg_ref, gp_ref, gn_ref, do_ref, s0_ref, dhq_ref, dz_ref, dv_ref, dlb_ref, rt):
        @pl.when(pl.program_id(1) == 0)
        def _():
            rt[...] = jnp.zeros_like(rt)
            dlb_ref[...] = jnp.zeros_like(dlb_ref)

        mirrored = pl.program_id(0) == 1
        masks = _hg_level_masks(mirrored)
        his, sames, eye = masks
        for hh in range(HG_HEADS):
            ln = _hg_lanes(hh)
            hqv = _hg_head((hq0, hq1), hh)
            q, f, k, sg, lb = _hg_gates(hqv, _hg_head((z0, z1), hh), a0_ref[0, :, ln], a1_ref[0, :, ln])
            vv, dov = _hg_head((v0, v1), hh), do_ref[:, ln]
            ex = _hg_exps(seg_ref, f)
            a = _hg_intra(q, k, ex, masks)
            da = _dot(dov, vv, "nt")
            diag = jnp.sum(dov * vv, axis=-1, keepdims=True)
            s_t = s0_ref[0, hh, 0]
            r_t = rt[hh]
            k_end = k * ex[HG_LEVELS + 1]
            dv_ref[0, :, ln] = _dot(a, dov, "tn") + _dot(k_end, r_t, "nt")
            dq_inter = ex[0] * _dot(dov, s_t)
            dk_inter = ex[HG_LEVELS + 1] * _dot(vv, r_t)
            dq = diag * k + dq_inter
            dk = diag * q + dk_inter
            q_terms, k_terms = [q * dq_inter], [k * dk_inter]
            for lev in range(HG_LEVELS):
                e = ex[lev + 1]
                pairs = jnp.where(sames[lev], da, 0.0)
                q_part = jnp.where(his[lev], e, 0.0) * _dot(pairs, jnp.where(his[lev], 0.0, k * e))
                k_part = jnp.where(his[lev], 0.0, e) * _dot(pairs, jnp.where(his[lev], q * e, 0.0), "tn")
                dq, dk = dq + q_part, dk + k_part
                q_terms.append(q * q_part)
                k_terms.append(k * k_part)
            decay = _hg_last_row(ex[0], mirrored)
            rt[hh] = r_t * decay + _dot(dov, q * ex[0], "tn")
            later = decay * jnp.sum(s_t * r_t, axis=0, keepdims=True)
            dlf = _dot(gp_ref[0], jnp.concatenate(q_terms, axis=0)) + _dot(gn_ref[0], jnp.concatenate(k_terms, axis=0)) + later
            df = dlf / f - dk
            dz_ref[0, :, ln] = df * (1.0 - lb) * sg * (1.0 - sg)
            dlb_ref[0, :, ln] += jnp.sum(df * (1.0 - sg), axis=0, keepdims=True)
            sq = _sigmoid(hqv)
            dhq_ref[0, :, ln] = dq * sq * (1.0 + hqv * (1.0 - sq))

    out = SDS((2, n, HG_DIM), F32)
    return pl.pallas_call(
        body, name=name, grid=(2, nc),
        in_specs=sp["hq"] + sp["z"] + sp["v"] + [sp["vec"], sp["vec"], sp["seg"], sp["sums"], sp["sums"], sp["shared"], sp["state"]],
        out_specs=[sp["per_dir"], sp["per_dir"], sp["per_dir"], sp["vec"]], out_shape=[out, out, out, SDS((2, 1, HG_DIM), F32)],
        scratch_shapes=[pltpu.VMEM((HG_HEADS, HG_HEAD_DIM, HG_HEAD_DIM), F32)],
        compiler_params=_cp("parallel", "arbitrary"))(p, p, p, p, p, p, a0, a1, seg, gp, gn, do, s0)


def _hg_post(o2, p, g, name):
    n = p.shape[0]
    tr = min(ROW_TILE, n)
    w = 2 * HG_HEAD_DIM

    def body(of_ref, ob_ref, hg_ref, g_ref, o_ref):
        for j in range(2):
            sl = slice(j * HG_HEAD_DIM, (j + 1) * HG_HEAD_DIM)
            o = of_ref[0, :, sl] + ob_ref[0, :, sl]
            hg = hg_ref[:, sl]
            o_ref[:, sl] = (o * _rstd(o) * g_ref[...] * (hg * _sigmoid(hg))).astype(o_ref.dtype)

    blk = pl.BlockSpec((tr, w), lambda i, j: (i, j))
    dirs = [pl.BlockSpec((1, tr, w), lambda i, j, d=d: (d, i, j)) for d in range(2)]
    return pl.pallas_call(
        body, name=name, grid=(n // tr, HG_DIM // w),
        in_specs=dirs + [pl.BlockSpec((tr, w), lambda i, j: (i, OFF_HG // w + j)), pl.BlockSpec((1, HG_HEAD_DIM), lambda i, j: (0, 0))],
        out_specs=blk, out_shape=SDS((n, HG_DIM), MXU_DTYPE), compiler_params=_cp("parallel", "parallel"))(o2, o2, p, g)


def _hg_post_bwd(o2, p, g, dcat, name):
    n = p.shape[0]
    tr = min(ROW_TILE, n)
    w = 2 * HG_HEAD_DIM

    def body(of_ref, ob_ref, hg_ref, g_ref, d_ref, do_ref, dhg_ref, dg_ref):
        @pl.when(pl.program_id(1) == 0)
        def _():
            dg_ref[...] = jnp.zeros_like(dg_ref)

        for j in range(2):
            sl = slice(j * HG_HEAD_DIM, (j + 1) * HG_HEAD_DIM)
            o = of_ref[0, :, sl] + ob_ref[0, :, sl]
            hg = hg_ref[:, sl]
            d = d_ref[:, sl].astype(F32)
            sg = _sigmoid(hg)
            on = o * _rstd(o) * g_ref[...]
            dhg_ref[:, sl] = (d * on * sg * (1.0 + hg * (1.0 - sg))).astype(dhg_ref.dtype)
            dx, dg = _rms_bwd(o, g_ref[...], d * hg * sg)
            do_ref[:, sl] = dx
            dg_ref[0, :, sl] += dg

    blk = pl.BlockSpec((tr, w), lambda j, i: (i, j))
    dirs = [pl.BlockSpec((1, tr, w), lambda j, i, d=d: (d, i, j)) for d in range(2)]
    return pl.pallas_call(
        body, name=name, grid=(HG_DIM // w, n // tr),
        in_specs=dirs + [pl.BlockSpec((tr, w), lambda j, i: (i, OFF_HG // w + j)), pl.BlockSpec((1, HG_HEAD_DIM), lambda j, i: (0, 0)),
                         pl.BlockSpec((tr, w), lambda j, i: (i, ATT_Q_DIM // w + j))],
        out_specs=[blk, blk, pl.BlockSpec((1, 1, w), lambda j, i: (j, 0, 0))],
        out_shape=[SDS((n, HG_DIM), F32), SDS((n, HG_DIM), MXU_DTYPE), SDS((HG_DIM // w, 1, w), F32)],
        compiler_params=_cp("parallel", "arbitrary"))(o2, o2, p, g, dcat)


XATT_TQ = 512


def _xattn_fwd(q, kv, name):
    n, nm = q.shape[0], kv.shape[0]
    tq = min(XATT_TQ, n)
    scale = X_HEAD_DIM ** -0.5

    def body(q_ref, k_ref, v_ref, o_ref):
        s = _dot(q_ref[...], k_ref[...], "nt") * scale
        e = jnp.exp(s - jnp.max(s, axis=-1, keepdims=True))
        o_ref[...] = _dot(e / jnp.sum(e, axis=-1, keepdims=True), v_ref[...]).astype(o_ref.dtype)

    qb = pl.BlockSpec((tq, X_HEAD_DIM), lambda h, i: (i, h))
    return pl.pallas_call(
        body, name=name, grid=(X_HEADS, n // tq),
        in_specs=[qb, pl.BlockSpec((nm, X_HEAD_DIM), lambda h, i: (0, h)), pl.BlockSpec((nm, X_HEAD_DIM), lambda h, i: (0, X_HEADS + h))],
        out_specs=qb, out_shape=SDS(q.shape, MXU_DTYPE), compiler_params=_cp("parallel", "parallel"))(q, kv, kv)


def _xattn_bwd(q, kv, do, name):
    n, nm = q.shape[0], kv.shape[0]
    tq = min(XATT_TQ, n)
    scale = X_HEAD_DIM ** -0.5

    def body(q_ref, k_ref, v_ref, do_ref, dq_ref, dk_ref, dv_ref):
        @pl.when(pl.program_id(1) == 0)
        def _():
            dk_ref[...] = jnp.zeros_like(dk_ref)
            dv_ref[...] = jnp.zeros_like(dv_ref)

        qv, dov = q_ref[...], do_ref[...]
        s = _dot(qv, k_ref[...], "nt") * scale
        e = jnp.exp(s - jnp.max(s, axis=-1, keepdims=True))
        p = e / jnp.sum(e, axis=-1, keepdims=True)
        dp = _dot(dov, v_ref[...], "nt")
        ds = p * (dp - jnp.sum(p * dp, axis=-1, keepdims=True)) * scale
        dq_ref[...] = _dot(ds, k_ref[...]).astype(dq_ref.dtype)
        dk_ref[...] += _dot(ds, qv, "tn")
        dv_ref[...] += _dot(p, dov, "tn")

    qb = pl.BlockSpec((tq, X_HEAD_DIM), lambda h, i: (i, h))
    kb = pl.BlockSpec((nm, X_HEAD_DIM), lambda h, i: (0, h))
    return pl.pallas_call(
        body, name=name, grid=(X_HEADS, n // tq),
        in_specs=[qb, kb, pl.BlockSpec((nm, X_HEAD_DIM), lambda h, i: (0, X_HEADS + h)), qb], out_specs=[qb, kb, kb],
        out_shape=[SDS(q.shape, MXU_DTYPE), SDS((nm, X_HEADS * X_HEAD_DIM), F32), SDS((nm, X_HEADS * X_HEAD_DIM), F32)],
        compiler_params=_cp("parallel", "arbitrary"))(q, kv, kv, do)


def _edge_rows(shape):
    row = lax.broadcasted_iota(jnp.int32, shape, 0)
    return row == 0, row == shape[0] - 1


def _shift_rows(u, down, edges):
    if down:
        return jnp.where(edges[0], 0.0, pltpu.roll(u, 1, axis=0))
    return jnp.where(edges[1], 0.0, pltpu.roll(u, u.shape[0] - 1, axis=0))


def _conv(u, w, b, edges):
    return b + _shift_rows(u, True, edges) * w[0:1, :] + u * w[1:2, :] + _shift_rows(u, False, edges) * w[2:3, :]


def _ff_specs(n):
    gate = lambda rows: pl.BlockSpec((rows, FF_COLS), lambda j: (0, j))
    val = lambda rows: pl.BlockSpec((rows, FF_COLS), lambda j: (0, FF_BLOCKS + j))
    return [gate(n), val(n), gate(3), val(3), gate(1), val(1)], gate


def _conv_gate(u, cw, cb, name):
    n = u.shape[0]
    ins, gate_blk = _ff_specs(n)

    def body(ug_ref, uv_ref, wg_ref, wv_ref, bg_ref, bv_ref, o_ref):
        edges = _edge_rows(ug_ref.shape)
        gate = _conv(ug_ref[...], wg_ref[...], bg_ref[...], edges)
        val = _conv(uv_ref[...], wv_ref[...], bv_ref[...], edges)
        o_ref[...] = (gate * _sigmoid(gate) * val).astype(o_ref.dtype)

    return pl.pallas_call(
        body, name=name, grid=(FF_BLOCKS,), in_specs=ins, out_specs=gate_blk(n), out_shape=SDS((n, D_FF), MXU_DTYPE),
        compiler_params=_cp("parallel"))(u, u, cw, cw, cb, cb)


def _conv_gate_bwd(u, cw, cb, da, name):
    n = u.shape[0]
    ins, gate_blk = _ff_specs(n)

    def side(dacc, u, w, edges, du_ref, dw_ref, db_ref):
        nxt, prv = _shift_rows(dacc, False, edges), _shift_rows(dacc, True, edges)
        du_ref[...] = (nxt * w[0:1, :] + dacc * w[1:2, :] + prv * w[2:3, :]).astype(du_ref.dtype)
        db_ref[...] = jnp.sum(dacc, axis=0, keepdims=True)
        dw_ref[0:1, :] = jnp.sum(nxt * u, axis=0, keepdims=True)
        dw_ref[1:2, :] = jnp.sum(dacc * u, axis=0, keepdims=True)
        dw_ref[2:3, :] = jnp.sum(prv * u, axis=0, keepdims=True)

    def body(ug_ref, uv_ref, wg_ref, wv_ref, bg_ref, bv_ref, da_ref, dug_ref, duv_ref, dwg_ref, dwv_ref, dbg_ref, dbv_ref):
        ug, uv = ug_ref[...], uv_ref[...]
        edges = _edge_rows(ug.shape)
        gate = _conv(ug, wg_ref[...], bg_ref[...], edges)
        val = _conv(uv, wv_ref[...], bv_ref[...], edges)
        sg = _sigmoid(gate)
        dav = da_ref[...].astype(F32)
        side(dav * val * sg * (1.0 + gate * (1.0 - sg)), ug, wg_ref[...], edges, dug_ref, dwg_ref, dbg_ref)
        side(dav * gate * sg, uv, wv_ref[...], edges, duv_ref, dwv_ref, dbv_ref)

    return pl.pallas_call(
        body, name=name, grid=(FF_BLOCKS,), in_specs=ins + [gate_blk(n)],
        out_specs=[gate_blk(n), gate_blk(n), gate_blk(3), gate_blk(3), gate_blk(1), gate_blk(1)],
        out_shape=[SDS((n, D_FF), MXU_DTYPE)] * 2 + [SDS((3, D_FF), F32)] * 2 + [SDS((1, D_FF), F32)] * 2,
        compiler_params=_cp("parallel"))(u, u, cw, cw, cb, cb, da)


def _adamw(w, g, m, v, name):
    r, c = w.shape
    tr = r if r <= 512 else 256 if r % 256 == 0 else 88
    assert r % tr == 0, (name, r, tr)

    def body(w_ref, g_ref, m_ref, v_ref, d_ref, mo_ref, vo_ref, go_ref):
        gv = g_ref[...]
        go_ref[...] = gv
        mn = ADAM_B1 * m_ref[...] + (1.0 - ADAM_B1) * gv
        vn = ADAM_B2 * v_ref[...] + (1.0 - ADAM_B2) * gv * gv
        m_hat = mn / (1.0 - ADAM_B1 ** ADAM_STEP)
        v_hat = vn / (1.0 - ADAM_B2 ** ADAM_STEP)
        d_ref[...] = -ADAM_LR * (m_hat / (jnp.sqrt(v_hat) + ADAM_EPS) + ADAM_WD * w_ref[...])
        mo_ref[...] = mn
        vo_ref[...] = vn

    blk = pl.BlockSpec((tr, c), lambda i: (i, 0))
    out = SDS((r, c), F32)
    return pl.pallas_call(body, name=name, grid=(r // tr,), in_specs=[blk] * 4, out_specs=[blk] * 4, out_shape=[out] * 4,
                          compiler_params=_cp("parallel"))(w, g, m, v)


def _half_tile(h):
    tr = h if h <= 512 else 256 if h % 256 == 0 else 176
    assert h % tr == 0, (h, tr)
    return tr


ANY = pl.BlockSpec(memory_space=pl.ANY)


def _place():
    x, y, c = lax.axis_index("x"), lax.axis_index("y"), lax.axis_index("c")
    return x, y, c, [(1 - x, y), (x, 1 - y), (1 - x, 1 - y)]


def _gather_shards(shards, name):
    nt = len(shards)

    def body(*refs):
        ins, outs = refs[:nt], refs[nt:2 * nt]
        send, recv, fsend, frecv, osend, orecv = refs[2 * nt:]
        x, y, c, chips = _place()
        me = 2 * x + y

        def half(t, chip, cc):
            h = ins[t].shape[0] // 2
            return outs[t].at[chip, pl.ds(cc * h, h)]

        def ici(t, j):
            cx, cy = chips[j]
            h = ins[t].shape[0] // 2
            return pltpu.make_async_remote_copy(src_ref=ins[t].at[pl.ds(c * h, h)], dst_ref=half(t, me, c),
                                                send_sem=send.at[t, j], recv_sem=recv.at[t, j], device_id=(cx, cy, c), device_id_type=MESH)

        def landed(t, j):
            cx, cy = chips[j]
            blk = half(t, 2 * cx + cy, c)
            return pltpu.make_async_remote_copy(src_ref=blk, dst_ref=blk, send_sem=send.at[t, j], recv_sem=recv.at[t, j],
                                                device_id=(cx, cy, c), device_id_type=MESH)

        def d2d(t, j, cc):
            cx, cy = chips[j]
            blk = half(t, 2 * cx + cy, cc)
            return pltpu.make_async_remote_copy(src_ref=blk, dst_ref=blk, send_sem=fsend.at[t, j], recv_sem=frecv.at[t, j],
                                                device_id=(x, y, 1 - c), device_id_type=MESH)

        own = [pltpu.make_async_remote_copy(src_ref=ins[t], dst_ref=outs[t].at[me], send_sem=osend.at[t], recv_sem=orecv.at[t],
                                            device_id=(x, y, 1 - c), device_id_type=MESH) for t in range(nt)]
        for t in range(nt):
            for j in range(3):
                ici(t, j).start()
        for cp in own:
            cp.start()
        for t in range(nt):
            for j in range(3):
                landed(t, j).wait_recv()
                d2d(t, j, c).start()
        for t in range(nt):
            for j in range(3):
                d2d(t, j, 1 - c).wait_recv()
        for t in range(nt):
            for j in range(3):
                ici(t, j).wait_send()
                d2d(t, j, c).wait_send()
        for cp in own:
            cp.wait()

    return pl.pallas_call(
        body, name=name, in_specs=[ANY] * nt, out_specs=[ANY] * nt,
        out_shape=[SDS((4,) + s.shape, s.dtype) for s in shards],
        scratch_shapes=[pltpu.SemaphoreType.DMA((nt, 3))] * 4 + [pltpu.SemaphoreType.DMA((nt,))] * 2,
        compiler_params=pltpu.CompilerParams(has_side_effects=True))(*shards)


def _join_halves(bufs, name):
    nt = len(bufs)

    def body(*refs):
        outs = refs[nt:2 * nt]
        send, recv = refs[2 * nt:]
        x, y, c, _ = _place()
        cps = [pltpu.make_async_remote_copy(src_ref=outs[t].at[c], dst_ref=outs[t].at[c], send_sem=send.at[t], recv_sem=recv.at[t],
                                            device_id=(x, y, 1 - c), device_id_type=MESH) for t in range(nt)]
        for cp in cps:
            cp.start()
        for t in range(nt):
            theirs = outs[t].at[1 - c]
            pltpu.make_async_remote_copy(src_ref=theirs, dst_ref=theirs, send_sem=send.at[t], recv_sem=recv.at[t],
                                         device_id=(x, y, 1 - c), device_id_type=MESH).wait_recv()
        for cp in cps:
            cp.wait_send()

    return pl.pallas_call(
        body, name=name, in_specs=[ANY] * nt, out_specs=[ANY] * nt, out_shape=[SDS(b.shape, b.dtype) for b in bufs],
        input_output_aliases={t: t for t in range(nt)},
        scratch_shapes=[pltpu.SemaphoreType.DMA((nt,))] * 2,
        compiler_params=pltpu.CompilerParams(has_side_effects=True))(*bufs)


def _exchange_small(v, reduce, name, after=()):
    rows = v.shape[0]
    after, after_specs = _unread(after)

    def body(v_ref, *rest):
        o_ref, buf, send, recv = rest[-4:]
        x, y, c, _ = _place()
        me = 4 * x + 2 * y + c
        buf[me] = v_ref[...]

        def peer(dx, dy, dc):
            return (1 - x if dx else x, 1 - y if dy else y, 1 - c if dc else c)

        peers = [(dx, dy, dc) for dx in range(2) for dy in range(2) for dc in range(2) if (dx, dy, dc) != (0, 0, 0)]
        cps = []
        for j, (dx, dy, dc) in enumerate(peers):
            cps.append(pltpu.make_async_remote_copy(src_ref=v_ref, dst_ref=buf.at[me], send_sem=send.at[j], recv_sem=recv.at[j],
                                                    device_id=peer(dx, dy, dc), device_id_type=MESH))
        for cp in cps:
            cp.start()
        for j, (dx, dy, dc) in enumerate(peers):
            px, py, pc = peer(dx, dy, dc)
            blk = buf.at[4 * px + 2 * py + pc]
            pltpu.make_async_remote_copy(src_ref=blk, dst_ref=blk, send_sem=send.at[j], recv_sem=recv.at[j],
                                         device_id=(px, py, pc), device_id_type=MESH).wait_recv()
        for cp in cps:
            cp.wait_send()
        if reduce:
            acc = buf[0]
            for j in range(1, 8):
                acc = acc + buf[j]
            o_ref[...] = acc
        else:
            o_ref[...] = buf[...]

    vm = pl.BlockSpec(memory_space=pltpu.VMEM)
    return pl.pallas_call(
        body, name=name, in_specs=[vm] + after_specs, out_specs=vm, out_shape=SDS((rows, 128) if reduce else (8, rows, 128), F32),
        scratch_shapes=[pltpu.VMEM((8, rows, 128), F32), pltpu.SemaphoreType.DMA((7,)), pltpu.SemaphoreType.DMA((7,))],
        compiler_params=pltpu.CompilerParams(has_side_effects=True))(v, *after)


HBM = pl.BlockSpec(memory_space=pltpu.HBM)
SEM = pl.BlockSpec(memory_space=pltpu.SEMAPHORE)
TOKEN = pl.BlockSpec(memory_space=pltpu.VMEM)
TOKEN_SHAPE = SDS((8, 128), F32)
PEERS = 7


def _in_hbm(a):
    return pltpu.with_memory_space_constraint(a, pltpu.HBM)


def _split_params():
    return pltpu.CompilerParams(has_side_effects=pltpu.SideEffectType.DATAFLOW_SIDE_EFFECTING)


def _gather_start(shards, name, after=()):
    nt = len(shards)
    after, after_specs = _unread(after)

    def body(*refs):
        ins, lands = refs[:nt], refs[nt:2 * nt]
        outs = refs[2 * nt + len(after):]
        sends, recvs = outs[:nt], outs[nt:2 * nt]
        x, y, c, chips = _place()
        me = 2 * x + y
        for t in range(nt):
            h = ins[t].shape[0] // 2
            mine = pl.ds(c * h, h)
            for j, (cx, cy) in enumerate(chips):
                for dc in range(2):
                    pltpu.make_async_remote_copy(src_ref=ins[t].at[mine], dst_ref=lands[t].at[me, mine], send_sem=sends[t].at[2 * j + dc],
                                                 recv_sem=recvs[t].at[2 * j + c], device_id=(cx, cy, dc), device_id_type=MESH).start()
            pltpu.make_async_remote_copy(src_ref=ins[t], dst_ref=lands[t].at[me], send_sem=sends[t].at[PEERS - 1], recv_sem=recvs[t].at[PEERS - 1],
                                         device_id=(x, y, 1 - c), device_id_type=MESH).start()
        outs[-1][...] = jnp.zeros(TOKEN_SHAPE.shape, F32)

    lands = [lax.empty((4,) + s.shape, s.dtype) for s in shards]
    out = pl.pallas_call(
        body, name=name, in_specs=[HBM] * (2 * nt) + after_specs, out_specs=[SEM] * (2 * nt) + [HBM] * (2 * nt) + [TOKEN],
        out_shape=[pltpu.SemaphoreType.DMA((PEERS,))] * (2 * nt)
        + [pltpu.HBM(s.shape, s.dtype) for s in shards] + [pltpu.HBM(l.shape, l.dtype) for l in lands] + [TOKEN_SHAPE],
        input_output_aliases={t: 2 * nt + t for t in range(2 * nt)}, compiler_params=_split_params())(
            *[_in_hbm(s) for s in shards], *[_in_hbm(l) for l in lands], *after)
    return out[:nt], out[nt:2 * nt], out[2 * nt:3 * nt], out[3 * nt:4 * nt], out[-1]


def _gather_wait(sends, recvs, shards, lands, after, name):
    nt = len(shards)

    def body(*refs):
        ins, lands_ref = refs[:nt], refs[nt:2 * nt]
        send_refs, recv_refs = refs[2 * nt:3 * nt], refs[3 * nt:4 * nt]
        x, y, c, chips = _place()
        for t in range(nt):
            h = ins[t].shape[0] // 2
            for j, (cx, cy) in enumerate(chips):
                for cs in range(2):
                    blk = lands_ref[t].at[2 * cx + cy, pl.ds(cs * h, h)]
                    pltpu.make_async_remote_copy(src_ref=blk, dst_ref=blk, send_sem=send_refs[t].at[2 * j + cs], recv_sem=recv_refs[t].at[2 * j + cs],
                                                 device_id=(cx, cy, cs), device_id_type=MESH).wait()
            blk = lands_ref[t].at[2 * x + y]
            pltpu.make_async_remote_copy(src_ref=blk, dst_ref=blk, send_sem=send_refs[t].at[PEERS - 1], recv_sem=recv_refs[t].at[PEERS - 1],
                                         device_id=(x, y, 1 - c), device_id_type=MESH).wait()

    out = pl.pallas_call(
        body, name=name, in_specs=[HBM] * (2 * nt) + [SEM] * (2 * nt) + [ANY], out_specs=[HBM] * (2 * nt),
        out_shape=[pltpu.HBM(s.shape, s.dtype) for s in shards] + [pltpu.HBM(l.shape, l.dtype) for l in lands],
        input_output_aliases={t: t for t in range(2 * nt)}, compiler_params=_split_params())(*shards, *lands, *sends, *recvs, after)
    return out[nt:]


def _scatter_start(g, name):
    _, r, c_ = g.shape
    h = r // 2

    def body(g_ref, land, send, recv, g_thru, land_thru, token):
        x, y, c, chips = _place()
        for j, (cx, cy) in enumerate(chips):
            for dc in range(2):
                pltpu.make_async_remote_copy(src_ref=g_ref.at[2 * cx + cy, pl.ds(dc * h, h)], dst_ref=land.at[2 * j + c], send_sem=send.at[2 * j + dc],
                                             recv_sem=recv.at[2 * j + c], device_id=(cx, cy, dc), device_id_type=MESH).start()
        pltpu.make_async_remote_copy(src_ref=g_ref.at[2 * x + y, pl.ds((1 - c) * h, h)], dst_ref=land.at[PEERS - 1], send_sem=send.at[PEERS - 1],
                                     recv_sem=recv.at[PEERS - 1], device_id=(x, y, 1 - c), device_id_type=MESH).start()
        token[...] = jnp.zeros(TOKEN_SHAPE.shape, F32)

    land = lax.empty((PEERS, h, c_), g.dtype)
    return pl.pallas_call(
        body, name=name, in_specs=[HBM, HBM], out_specs=[SEM, SEM, HBM, HBM, TOKEN],
        out_shape=[pltpu.SemaphoreType.DMA((PEERS,)), pltpu.SemaphoreType.DMA((PEERS,)), pltpu.HBM(g.shape, g.dtype),
                   pltpu.HBM(land.shape, land.dtype), TOKEN_SHAPE],
        input_output_aliases={0: 2, 1: 3}, compiler_params=_split_params())(_in_hbm(g), _in_hbm(land))


def _scatter_wait(started, after, name):
    nt = len(started)

    def body(*refs):
        lands = refs[nt:2 * nt]
        sends, recvs = refs[2 * nt:3 * nt], refs[3 * nt:4 * nt]
        x, y, c, chips = _place()
        peers = [(cx, cy, dc) for cx, cy in chips for dc in range(2)] + [(x, y, 1 - c)]
        for t in range(nt):
            for k, peer in enumerate(peers):
                blk = lands[t].at[k]
                pltpu.make_async_remote_copy(src_ref=blk, dst_ref=blk, send_sem=sends[t].at[k], recv_sem=recvs[t].at[k],
                                             device_id=peer, device_id_type=MESH).wait()

    gs, lands = [s[2] for s in started], [s[3] for s in started]
    after, after_specs = _unread(after)
    out = pl.pallas_call(
        body, name=name, in_specs=[HBM] * (2 * nt) + [SEM] * (2 * nt) + after_specs, out_specs=[HBM] * (2 * nt),
        out_shape=[pltpu.HBM(a.shape, a.dtype) for a in gs + lands],
        input_output_aliases={t: t for t in range(2 * nt)}, compiler_params=_split_params())(
            *gs, *lands, *[s[0] for s in started], *[s[1] for s in started], *after)
    return out[:nt], out[nt:]


def _sum_devices(g, land, me, core, name):
    npeer, h, c = land.shape
    tr = _half_tile(h)
    steps = h // tr

    def body(ix_ref, own_ref, land_ref, o_ref):
        acc = own_ref[0].astype(F32)
        for j in range(npeer):
            acc = acc + land_ref[j].astype(F32)
        o_ref[0] = acc

    grid_spec = pltpu.PrefetchScalarGridSpec(
        num_scalar_prefetch=1, grid=(steps,),
        in_specs=[pl.BlockSpec((1, tr, c), lambda i, ix: (ix[0], ix[1] * steps + i, 0)), pl.BlockSpec((npeer, tr, c), lambda i, ix: (0, i, 0))],
        out_specs=pl.BlockSpec((1, tr, c), lambda i, ix: (ix[1], i, 0)))
    return pl.pallas_call(body, name=name, grid_spec=grid_spec, out_shape=SDS((2, h, c), F32),
                          compiler_params=_cp("parallel"))(jnp.stack([me, core]), g, land)


def _pack_small(parts):
    flat = jnp.concatenate([p.reshape(-1) for p in parts])
    total = flat.shape[0]
    rows = -(-total // 1024) * 8
    return jnp.pad(flat, (0, rows * 128 - total)).reshape(rows, 128)


def _unpack_small(packed, shapes):
    flat = packed.reshape(-1)
    out, off = [], 0
    for s in shapes:
        size = int(np.prod(s))
        out.append(flat[off:off + size].reshape(s))
        off += size
    return out


def _local_step(x, mem, target, w_in, first_after, mid_weights, ffn_weights, on_grad, gains, conv_w, conv_b, hg_lb):
    n = x.shape[0]
    cos, sin = _rope_tables(n)
    seg = _hg_segments()
    gp, gn = _hg_pair_sums()
    gq2 = jnp.tile(gains["q_norm_g"], (1, 2))
    gk2 = jnp.tile(gains["k_norm_g"], (1, 2))
    a0 = hg_lb[:, 0:1, :]
    a1 = hg_lb[:, 1:2, :]

    p, h1 = _norm_mm(x, gains["pre_mix_g"], w_in, F32, TOKEN_TILE, 1664, "in_proj", after=(first_after,))
    qr, kr = _qk_prep(p, gq2, gk2, cos, sin, "qk_prep")
    heads = lambda a: a.reshape(n, ATT_KV_HEADS, ATT_HEAD_DIM).transpose(1, 0, 2)
    kh = heads(kr)
    vh = heads(p[:, OFF_AV:OFF_AV + ATT_KV_DIM].astype(MXU_DTYPE))
    att = _attn_fwd(qr, kh, vh, "attn_fwd")
    o2, s0 = _hgrn_fwd(p, a0, a1, seg, "hgrn_fwd")
    rec = _hg_post(o2, p, gains["hg_out_norm_g"], "hg_post")
    cat = jnp.concatenate([att, rec], axis=1)
    w_out, w_xq, w_xkv, w_xo = mid_weights(cat)
    mixed = _mm(cat, w_out, "nn", F32, TOKEN_TILE, 1024, "out_proj")
    x1 = _resid_norm(x, mixed, gains["post_mix_g"], "mix_resid")
    xq, h2 = _norm_mm(x1, gains["pre_x_g"], w_xq, MXU_DTYPE, TOKEN_TILE, 1024, "xq_proj")
    kv, mn = _norm_mm(mem, gains["mem_norm_g"], w_xkv, MXU_DTYPE, 256, 2048, "xkv_proj")
    ox = _xattn_fwd(xq, kv, "xattn_fwd")
    xo = _mm(ox, w_xo, "nn", F32, TOKEN_TILE, 1024, "xo_proj")
    x2 = _resid_norm(x1, xo, gains["post_x_g"], "x_resid")
    w_up, w_down = ffn_weights(x2)
    u, h3 = _norm_mm(x2, gains["pre_ffn_g"], w_up, F32, TOKEN_TILE, 1408, "up_proj")
    act = _conv_gate(u, conv_w, conv_b, "conv_gate")
    dn = _mm(act, w_down, "nn", F32, TOKEN_TILE, 1024, "down_proj")
    d3, loss = _final_loss(x2, dn, gains["post_ffn_g"], target, "ffn_resid_loss")

    gs = {}
    d_dn, gs["post_ffn_g"] = _norm_bwd(dn, gains["post_ffn_g"], d3, None, MXU_DTYPE, "ffn_post_bwd")
    tok = on_grad("w_down", _mm(act, d_dn, "tn", WIRE_DTYPE, 1408, 1024, "down_dw"))
    d_act = _mm(d_dn, w_down, "nt", F32, TOKEN_TILE, 1408, "down_dx", after=(tok,))
    du_g, du_v, dcw_g, dcw_v, dcb_g, dcb_v = _conv_gate_bwd(u, conv_w, conv_b, d_act, "conv_gate_bwd")
    gs["conv_w"] = jnp.concatenate([dcw_g, dcw_v], axis=1)
    gs["conv_b"] = jnp.concatenate([dcb_g, dcb_v], axis=1)
    ff_shard = w_up.shape[2]
    g_up = _dw_by_owner(h3, du_g, ff_shard, 0, None, 512, "up_dw_gate")
    tok = on_grad("w_up", _dw_by_owner(h3, du_v, ff_shard, 2, g_up, 512, "up_dw_value"))
    d2, gs["pre_ffn_g"] = _dx_norm_bwd([(du_g, 0), (du_g, 1), (du_v, 0), (du_v, 1)], w_up, x2, gains["pre_ffn_g"], d3, 512,
                                       "up_dx_pre_bwd", after=(tok,))
    d_xo, gs["post_x_g"] = _norm_bwd(xo, gains["post_x_g"], d2, None, MXU_DTYPE, "x_post_bwd")
    tok = on_grad("w_xo", _mm(ox, d_xo, "tn", WIRE_DTYPE, 512, 1024, "xo_dw"))
    d_ox = _mm(d_xo, w_xo, "nt", MXU_DTYPE, TOKEN_TILE, 1024, "xo_dx", after=(tok,))
    d_xq, d_k, d_v = _xattn_bwd(xq, kv, d_ox, "xattn_bwd")
    d_kv = jnp.concatenate([d_k, d_v], axis=1).astype(MXU_DTYPE)
    tok = on_grad("w_xq", _mm(h2, d_xq, "tn", WIRE_DTYPE, 512, 1024, "xq_dw"))
    tok_kv = on_grad("w_xkv", _dw_by_owner(mn, d_kv, w_xkv.shape[2], 0, None, 512, "xkv_dw"))
    d1, gs["pre_x_g"] = _dx_norm_bwd([(d_xq, 0)], w_xq[None], x1, gains["pre_x_g"], d2, 512, "xq_dx_pre_bwd", after=(tok, tok_kv))
    d_mn = _mm_nt_parts([(d_kv, s) for s in range(4)], w_xkv, F32, 256, 1024, "xkv_dx")
    _, gs["mem_norm_g"] = _norm_bwd(mem, gains["mem_norm_g"], d_mn, None, MXU_DTYPE, "mem_norm_bwd")
    d_mixed, gs["post_mix_g"] = _norm_bwd(mixed, gains["post_mix_g"], d1, None, MXU_DTYPE, "mix_post_bwd")
    tok = on_grad("w_out", _mm(cat, d_mixed, "tn", WIRE_DTYPE, 512, 1024, "out_dw"))
    d_cat = _mm(d_mixed, w_out, "nt", MXU_DTYPE, TOKEN_TILE, 1024, "out_dx", after=(tok,))
    d_o, d_hg, dg_hg = _hg_post_bwd(o2, p, gains["hg_out_norm_g"], d_cat, "hg_post_bwd")
    gs["hg_out_norm_g"] = dg_hg.reshape(HG_HEADS, HG_HEAD_DIM).sum(axis=0, keepdims=True)
    dhq2, dz2, dhv2, dlb = _hgrn_bwd(p, a0, a1, seg, gp, gn, d_o, s0, "hgrn_bwd")
    lb = jax.nn.sigmoid(a0 - a1)
    da0 = dlb * lb * (1.0 - lb)
    gs["hg_lb"] = jnp.concatenate([da0, -da0], axis=1)
    d_qr, d_kh, d_vh = _attn_bwd(qr, kh, vh, cat, d_cat, "attn_bwd")
    unheads = lambda a: a.transpose(2, 0, 1).reshape(n, ATT_KV_DIM)
    d_aq, d_ak, dgq, dgk = _qk_prep_bwd(p, gq2, gk2, cos, sin, d_qr, unheads(d_kh), "qk_prep_bwd")
    gs["q_norm_g"] = dgq.reshape(ATT_HEADS, ATT_HEAD_DIM).sum(axis=0, keepdims=True)
    gs["k_norm_g"] = dgk.reshape(ATT_KV_HEADS, ATT_HEAD_DIM).sum(axis=0, keepdims=True)
    d_p = jnp.concatenate([d_aq, d_ak, unheads(d_vh).astype(MXU_DTYPE), (dhq2[0] + dhq2[1]).astype(MXU_DTYPE),
                           dz2[0].astype(MXU_DTYPE), dz2[1].astype(MXU_DTYPE), (dhv2[0] + dhv2[1]).astype(MXU_DTYPE), d_hg], axis=1)
    tok = on_grad("w_in", _mm(h1, d_p, "tn", WIRE_DTYPE, 512, 1664, "in_dw"))
    grad_x, gs["pre_mix_g"] = _dx_norm_bwd([(d_p, 0)], w_in[None], x, gains["pre_mix_g"], d1, 512, "in_dx_pre_bwd", after=(tok,))
    return loss, grad_x, gs


MATS = ("w_in", "w_out", "w_xq", "w_xkv", "w_xo", "w_up", "w_down")
GAINS = ("pre_mix_g", "q_norm_g", "k_norm_g", "hg_out_norm_g", "post_mix_g", "pre_x_g", "mem_norm_g", "post_x_g", "pre_ffn_g", "post_ffn_g")
WEIGHTS = ('pre_mix_g', 'w_in', 'q_norm_g', 'k_norm_g', 'hg_lb', 'hg_out_norm_g', 'w_out', 'post_mix_g', 'pre_x_g', 'mem_norm_g', 'w_xq',
           'w_xkv', 'w_xo', 'post_x_g', 'pre_ffn_g', 'w_up', 'conv_w', 'conv_b', 'w_down', 'post_ffn_g')


def kernel(x, mem, pre_mix_g, w_in, q_norm_g, k_norm_g, hg_lb, hg_out_norm_g, w_out, post_mix_g, pre_x_g, mem_norm_g, w_xq, w_xkv, w_xo, post_x_g, pre_ffn_g, w_up, conv_w, conv_b, w_down, post_ffn_g, loss_target, m_pre_mix_g, m_w_in, m_q_norm_g, m_k_norm_g, m_hg_lb, m_hg_out_norm_g, m_w_out, m_post_mix_g, m_pre_x_g, m_mem_norm_g, m_w_xq, m_w_xkv, m_w_xo, m_post_x_g, m_pre_ffn_g, m_w_up, m_conv_w, m_conv_b, m_w_down, m_post_ffn_g, v_pre_mix_g, v_w_in, v_q_norm_g, v_k_norm_g, v_hg_lb, v_hg_out_norm_g, v_w_out, v_post_mix_g, v_pre_x_g, v_mem_norm_g, v_w_xq, v_w_xkv, v_w_xo, v_post_x_g, v_pre_ffn_g, v_w_up, v_conv_w, v_conv_b, v_w_down, v_post_ffn_g):
    args = dict(locals())
    w = {k: args[k] for k in WEIGHTS}
    m = {k: args["m_" + k] for k in WEIGHTS}
    v = {k: args["v_" + k] for k in WEIGHTS}
    chip = 2 * lax.axis_index("x") + lax.axis_index("y")
    core = lax.axis_index("c")

    shards = {k: w[k][0].astype(WIRE_DTYPE) for k in MATS}

    def whole(k, g):
        return g if k in ("w_xkv", "w_up") else g.reshape(-1, g.shape[-1])

    w_in_shards = _gather_shards([shards["w_in"]], "gather_w_in")[0]
    w_in_full = jnp.concatenate([w_in_shards[s] for s in range(4)], axis=1)
    small_in = _exchange_small(_pack_small([w["conv_w"][0], w["hg_lb"]]), False, "gather_small")
    mid_names, ffn_names = ("w_out", "w_xq", "w_xkv", "w_xo"), ("w_up", "w_down")
    mid = _gather_start([shards[k] for k in mid_names], "gather_mid_start", after=(w_in_full, small_in))
    ffn = _gather_start([shards[k] for k in ffn_names], "gather_ffn_start", after=(mid[4],))

    def mid_weights(after):
        return [whole(k, g) for k, g in zip(mid_names, _gather_wait(*mid[:4], after, "gather_mid_wait"))]

    def ffn_weights(after):
        return [whole(k, g) for k, g in zip(ffn_names, _gather_wait(*ffn[:4], after, "gather_ffn_wait"))]

    cw_parts, lb_parts = [], []
    for s in range(4):
        cw_s, lb_s = _unpack_small(small_in[2 * s], [w["conv_w"][0].shape, w["hg_lb"].shape])
        cw_parts.append(cw_s)
        lb_parts.append(lb_s)
    conv_w_full = jnp.concatenate(cw_parts, axis=1)
    hg_lb_full = jnp.concatenate(lb_parts, axis=2)

    started = {}

    def on_grad(k, g):
        if k == "w_in":
            g = g.reshape(g.shape[0], 4, g.shape[1] // 4).transpose(1, 0, 2)
        elif g.ndim == 2:
            g = g.reshape(4, g.shape[0] // 4, g.shape[1])
        *started[k], token = _scatter_start(g, "grad_start_" + k)
        return token

    gains = {k: w[k] for k in GAINS}
    loss_part, grad_x, gs = _local_step(x[0], mem[0], loss_target[0], w_in_full, ffn[4], mid_weights, ffn_weights, on_grad, gains,
                                        conv_w_full, w["conv_b"], hg_lb_full)
    gs["loss"] = loss_part

    grads, delta, new_m, new_v = {}, {}, {}, {}

    def reduce_matrices(names, after, tag):
        sent, landed = _scatter_wait([started[k] for k in names], after, "grad_wait_" + tag)
        halves = [_sum_devices(g, land, chip, core, "grad_sum_" + k) for k, g, land in zip(names, sent, landed)]
        for k, r in zip(names, _join_halves(halves, "grad_join_" + tag)):
            grads[k] = r.reshape(1, -1, r.shape[-1])

    def adamw(names):
        for k in names:
            shape = w[k].shape
            two_d = lambda a: a.reshape(-1, shape[-1])
            d, mo, vo, go = _adamw(two_d(w[k]), two_d(grads[k]), two_d(m[k]), two_d(v[k]), "adamw_" + k)
            delta[k], new_m[k], new_v[k], grads[k] = d.reshape(shape), mo.reshape(shape), vo.reshape(shape), go.reshape(shape)

    early = tuple(k for k in MATS if k != "w_in")
    reduce_matrices(early, (grad_x,), "early")
    adamw(early)

    small_names = GAINS + ("conv_b", "conv_w", "hg_lb")
    packed = _pack_small([gs[k] for k in small_names + ("loss",)])
    reduced_small = _exchange_small(packed, True, "reduce_small", after=tuple(new_v[k] for k in early))
    *summed, loss = _unpack_small(reduced_small, [gs[k].shape for k in small_names + ("loss",)])
    loss = loss[0, 0]
    for k, g in zip(small_names, summed):
        grads[k] = g
    ncw = w["conv_w"].shape[2]
    grads["conv_w"] = lax.dynamic_slice_in_dim(grads["conv_w"], chip * ncw, ncw, axis=1)[None]
    nlb = w["hg_lb"].shape[2]
    grads["hg_lb"] = lax.dynamic_slice_in_dim(grads["hg_lb"], chip * nlb, nlb, axis=2)
    replicated = GAINS + ("conv_b",)
    shapes = [w[k].shape for k in replicated]
    rows = sum(int(np.prod(s)) for s in shapes) // 128
    pack = lambda d: jnp.concatenate([d[k].reshape(-1) for k in replicated]).reshape(rows, 128)
    outs = _adamw(pack(w), reduced_small[:rows], pack(m), pack(v), "adamw_replicated")
    for into, packed_out in zip((delta, new_m, new_v, grads), outs):
        for k, a in zip(replicated, _unpack_small(packed_out, shapes)):
            into[k] = a
    adamw(("conv_w", "hg_lb"))

    reduce_matrices(("w_in",), tuple(new_v[k] for k in early + small_names), "late")
    adamw(("w_in",))
    return (loss, grad_x[None], *[grads[k] for k in WEIGHTS], *[delta[k] for k in WEIGHTS],
            *[new_m[k] for k in WEIGHTS], *[new_v[k] for k in WEIGHTS])
```

```python
import functools

import numpy as np
import jax
import jax.numpy as jnp
from jax import lax
from jax.experimental import pallas as pl
from jax.experimental.pallas import tpu as pltpu

F32 = jnp.float32
MXU_DTYPE = jnp.bfloat16
WIRE_DTYPE = jnp.bfloat16
VMEM_LIMIT_BYTES = 56 * 1024 * 1024
EPS = 1e-6
MESH = pl.DeviceIdType.MESH

D_MODEL = 1024
GRID_W = 64
ATT_HEADS, ATT_KV_HEADS, ATT_HEAD_DIM = 8, 2, 64
ATT_GROUP = ATT_HEADS // ATT_KV_HEADS
ATT_Q_DIM, ATT_KV_DIM = 512, 128
ROPE_THETA = 10000.0
HG_HEADS, HG_HEAD_DIM, HG_DIM = 4, 128, 512
HG_CHUNK = 128
HG_LEVELS = 7
HG_PAIR = 2 * HG_HEAD_DIM
X_HEADS, X_HEAD_DIM = 4, 256
D_FF = 2816
FF_COLS = 256
FF_BLOCKS = D_FF // FF_COLS
N_IN = 3328
OFF_AQ, OFF_AK, OFF_AV, OFF_HQ, OFF_ZF, OFF_ZB, OFF_HI, OFF_HG = 0, 512, 640, 768, 1280, 1792, 2304, 2816

ADAM_LR, ADAM_B1, ADAM_B2, ADAM_EPS, ADAM_WD, ADAM_STEP = 0.001, 0.9, 0.999, 1e-08, 0.01, 10

SDS = jax.ShapeDtypeStruct


def _cp(*sem):
    return pltpu.CompilerParams(dimension_semantics=sem, vmem_limit_bytes=VMEM_LIMIT_BYTES)


def _dot(a, b, form="nn"):
    dims = {"nn": (((1,), (0,)), ((), ())), "nt": (((1,), (1,)), ((), ())), "tn": (((0,), (0,)), ((), ()))}[form]
    return lax.dot_general(a.astype(MXU_DTYPE), b.astype(MXU_DTYPE), dims, preferred_element_type=F32)


def _sigmoid(x):
    return 1.0 / (1.0 + jnp.exp(-x))


def _rstd(x):
    return lax.rsqrt(jnp.mean(x * x, axis=-1, keepdims=True) + EPS)


def _rms_bwd(x, g, dy):
    r = _rstd(x)
    xh = x * r
    dn = dy * g
    dx = r * (dn - xh * jnp.mean(dn * xh, axis=-1, keepdims=True))
    return dx, jnp.sum(dy * xh, axis=0, keepdims=True)


def _unread(after):
    after = tuple(a for a in after if a is not None)
    return after, [pl.BlockSpec(memory_space=pl.ANY)] * len(after)


def _mm(a, b, form, out_dtype, tm, tn, name, after=()):
    after, after_specs = _unread(after)
    if form == "nn":
        (m, k), n = a.shape, b.shape[1]
    elif form == "nt":
        (m, k), n = a.shape, b.shape[0]
    else:
        (k, m), n = a.shape, b.shape[1]
    tm, tn = min(tm, m), min(tn, n)
    assert m % tm == 0 and n % tn == 0, (name, m, n, tm, tn)

    def body(a_ref, b_ref, *rest):
        o_ref = rest[-1]
        o_ref[...] = _dot(a_ref[...], b_ref[...], form).astype(o_ref.dtype)

    a_spec = pl.BlockSpec((k, tm), lambda i, j: (0, i)) if form == "tn" else pl.BlockSpec((tm, k), lambda i, j: (i, 0))
    b_spec = pl.BlockSpec((tn, k), lambda i, j: (j, 0)) if form == "nt" else pl.BlockSpec((k, tn), lambda i, j: (0, j))
    return pl.pallas_call(
        body, name=name, grid=(m // tm, n // tn), in_specs=[a_spec, b_spec] + after_specs,
        out_specs=pl.BlockSpec((tm, tn), lambda i, j: (i, j)), out_shape=SDS((m, n), out_dtype),
        compiler_params=_cp("parallel", "parallel"))(a, b, *after)


def _mm_nt_parts(a_parts, b, out_dtype, tm, tn, name, after=()):
    after, after_specs = _unread(after)
    parts, n, p = b.shape
    m = a_parts[0][0].shape[0]
    tm, tn = min(tm, m), min(tn, n)
    assert m % tm == 0 and n % tn == 0 and len(a_parts) == parts, (name, m, b.shape)

    def body(*refs):
        o_ref = refs[-1]
        acc = _dot(refs[0][...], refs[parts][0], "nt")
        for s in range(1, parts):
            acc = acc + _dot(refs[s][...], refs[parts + s][0], "nt")
        o_ref[...] = acc.astype(o_ref.dtype)

    a_specs = [pl.BlockSpec((tm, p), lambda i, j, cb=cb: (i, cb)) for _, cb in a_parts]
    b_specs = [pl.BlockSpec((1, tn, p), lambda i, j, s=s: (s, j, 0)) for s in range(parts)]
    return pl.pallas_call(
        body, name=name, grid=(m // tm, n // tn), in_specs=a_specs + b_specs + after_specs,
        out_specs=pl.BlockSpec((tm, tn), lambda i, j: (i, j)), out_shape=SDS((m, n), out_dtype),
        compiler_params=_cp("parallel", "parallel"))(*[arr for arr, _ in a_parts], *([b] * parts), *after)


def _mm_resid_norm(a, b, x, g, tm, name, target=None):
    n, k = a.shape
    d = b.shape[1]
    tm = min(tm, n)
    assert n % tm == 0 and x.shape == (n, d), (name, a.shape, b.shape)
    with_loss = target is not None

    def body(a_ref, b_ref, x_ref, g_ref, *rest):
        y = _dot(a_ref[...], b_ref[...])
        out = x_ref[...] + y * _rstd(y) * g_ref[...]
        if not with_loss:
            y_ref, o_ref = rest
            y_ref[...] = y
            o_ref[...] = out
            return
        t_ref, y_ref, d_ref, l_ref = rest
        y_ref[...] = y
        diff = out - t_ref[...]
        d_ref[...] = diff * (1.0 / d)

        @pl.when(pl.program_id(0) == 0)
        def _():
            l_ref[...] = jnp.zeros_like(l_ref)

        l_ref[...] += 0.5 * jnp.sum(jnp.mean(diff * diff, axis=-1, keepdims=True), axis=0, keepdims=True)

    row = pl.BlockSpec((tm, d), lambda i: (i, 0))
    ins = [pl.BlockSpec((tm, k), lambda i: (i, 0)), pl.BlockSpec((k, d), lambda i: (0, 0)), row, pl.BlockSpec((1, d), lambda i: (0, 0))]
    out = SDS((n, d), F32)
    if with_loss:
        return pl.pallas_call(body, name=name, grid=(n // tm,), in_specs=ins + [row], out_specs=[row, row, pl.BlockSpec((1, 1), lambda i: (0, 0))],
                              out_shape=[out, out, SDS((1, 1), F32)], compiler_params=_cp("arbitrary"))(a, b, x, g, target)
    return pl.pallas_call(body, name=name, grid=(n // tm,), in_specs=ins, out_specs=[row, row], out_shape=[out, out],
                          compiler_params=_cp("parallel"))(a, b, x, g)


def _dx_norm_bwd(a_parts, b, x, g, res, tm, name, after=()):
    after, after_specs = _unread(after)
    parts, d, p = b.shape
    n = x.shape[0]
    tm = min(tm, n)
    assert n % tm == 0 and len(a_parts) == parts and x.shape[1] == d, (name, x.shape, b.shape)

    def body(*refs):
        x_ref, g_ref, res_ref = refs[2 * parts:2 * parts + 3]
        dx_ref, dg_ref = refs[-2:]
        dh = _dot(refs[0][...], refs[parts][0], "nt")
        for s in range(1, parts):
            dh = dh + _dot(refs[s][...], refs[parts + s][0], "nt")
        dx, dg = _rms_bwd(x_ref[...], g_ref[...], dh)
        dx_ref[...] = dx + res_ref[...]

        @pl.when(pl.program_id(0) == 0)
        def _():
            dg_ref[...] = jnp.zeros_like(dg_ref)

        dg_ref[...] += dg

    a_specs = [pl.BlockSpec((tm, p), lambda i, cb=cb: (i, cb)) for _, cb in a_parts]
    b_specs = [pl.BlockSpec((1, d, p), lambda i, s=s: (s, 0, 0)) for s in range(parts)]
    row = pl.BlockSpec((tm, d), lambda i: (i, 0))
    vec = pl.BlockSpec((1, d), lambda i: (0, 0))
    return pl.pallas_call(
        body, name=name, grid=(n // tm,), in_specs=a_specs + b_specs + [row, vec, row] + after_specs,
        out_specs=[row, vec], out_shape=[SDS((n, d), F32), SDS((1, d), F32)],
        compiler_params=_cp("arbitrary"))(*[arr for arr, _ in a_parts], *([b] * parts), x, g, res, *after)


def _dw_by_owner(a, b, tn, first, into, tm, name):
    k, m = a.shape
    cnt = b.shape[1] // tn
    tm = min(tm, m)
    assert m % tm == 0 and b.shape[1] == cnt * tn and first + cnt <= 4, (name, a.shape, b.shape)

    def body(a_ref, b_ref, *rest):
        rest[-1][0] = _dot(a_ref[...], b_ref[...], "tn").astype(rest[-1].dtype)

    extra = [] if into is None else [into]
    return pl.pallas_call(
        body, name=name, grid=(m // tm, cnt),
        in_specs=[pl.BlockSpec((k, tm), lambda i, j: (0, i)), pl.BlockSpec((k, tn), lambda i, j: (0, j))] + [pl.BlockSpec(memory_space=pl.ANY)] * len(extra),
        out_specs=pl.BlockSpec((1, tm, tn), lambda i, j: (first + j, i, 0)), out_shape=SDS((4, m, tn), WIRE_DTYPE),
        input_output_aliases={2: 0} if extra else {},
        compiler_params=_cp("parallel", "parallel"))(a, b, *extra)


def _norm_mm(x, g, w, out_dtype, tm, tn, name, after=()):
    after, after_specs = _unread(after)
    m, d = x.shape
    sharded = w.ndim == 3
    n = w.shape[-1] * (w.shape[0] if sharded else 1)
    tm, tn = min(tm, m), (w.shape[-1] if sharded else min(tn, n))
    assert m % tm == 0 and n % tn == 0, (name, m, n, tm, tn)

    def body(x_ref, g_ref, w_ref, *rest):
        o_ref, h_ref, hs = rest[-3:]

        @pl.when(pl.program_id(1) == 0)
        def _():
            xv = x_ref[...]
            h = (xv * _rstd(xv) * g_ref[...]).astype(MXU_DTYPE)
            hs[...] = h
            h_ref[...] = h

        o_ref[...] = _dot(hs[...], w_ref[0] if sharded else w_ref[...]).astype(o_ref.dtype)

    w_spec = pl.BlockSpec((1, d, tn), lambda i, j: (j, 0, 0)) if sharded else pl.BlockSpec((d, tn), lambda i, j: (0, j))
    return pl.pallas_call(
        body, name=name, grid=(m // tm, n // tn),
        in_specs=[pl.BlockSpec((tm, d), lambda i, j: (i, 0)), pl.BlockSpec((1, d), lambda i, j: (0, 0)), w_spec] + after_specs,
        out_specs=[pl.BlockSpec((tm, tn), lambda i, j: (i, j)), pl.BlockSpec((tm, d), lambda i, j: (i, 0))],
        out_shape=[SDS((m, n), out_dtype), SDS((m, d), MXU_DTYPE)],
        scratch_shapes=[pltpu.VMEM((tm, d), MXU_DTYPE)],
        compiler_params=_cp("parallel", "arbitrary"))(x, g, w, *after)


ROW_TILE = 256
TOKEN_TILE = 1024


def _norm_bwd(x, g, dy, res, out_dtype, name):
    n, d = x.shape
    tr = min(ROW_TILE, n)
    has_res = res is not None

    def body(*refs):
        x_ref, g_ref, dy_ref = refs[:3]
        dx_ref, dg_ref = refs[-2:]
        dx, dg = _rms_bwd(x_ref[...], g_ref[...], dy_ref[...].astype(F32))
        if has_res:
            dx = dx + refs[3][...]
        dx_ref[...] = dx.astype(dx_ref.dtype)

        @pl.when(pl.program_id(0) == 0)
        def _():
            dg_ref[...] = jnp.zeros_like(dg_ref)

        dg_ref[...] += dg

    row = pl.BlockSpec((tr, d), lambda i: (i, 0))
    vec = pl.BlockSpec((1, d), lambda i: (0, 0))
    ins = [x, g, dy] + ([res] if has_res else [])
    return pl.pallas_call(
        body, name=name, grid=(n // tr,), in_specs=[row, vec, row] + ([row] if has_res else []),
        out_specs=[row, vec], out_shape=[SDS((n, d), out_dtype), SDS((1, d), F32)],
        compiler_params=_cp("arbitrary"))(*ins)


def _rope_tables(n):
    pairs = ATT_HEAD_DIM // 4
    t = np.arange(n)
    inv = np.power(ROPE_THETA, -np.arange(pairs, dtype=np.float32) / pairs).astype(np.float32)
    ang = np.concatenate([(t // GRID_W)[:, None].astype(np.float32) * inv, (t % GRID_W)[:, None].astype(np.float32) * inv], axis=-1)
    cos = np.repeat(np.cos(ang), 2, axis=-1)
    sin = np.repeat(np.sin(ang), 2, axis=-1) * np.tile(np.array([-1.0, 1.0], np.float32), ATT_HEAD_DIM // 2)
    return jnp.asarray(np.tile(cos, 2), F32), jnp.asarray(np.tile(sin, 2), F32)


def _swap_pairs(x):
    lane = lax.broadcasted_iota(jnp.int32, x.shape, 1)
    return jnp.where((lane & 1) == 0, pltpu.roll(x, 127, axis=1), pltpu.roll(x, 1, axis=1))


def _head_mean(v):
    lane = lax.broadcasted_iota(jnp.int32, v.shape, 1)
    lo = jnp.where(lane < ATT_HEAD_DIM, v, 0.0)
    s0 = jnp.sum(lo, axis=-1, keepdims=True)
    s1 = jnp.sum(v - lo, axis=-1, keepdims=True)
    return jnp.where(lane < ATT_HEAD_DIM, s0, s1) * (1.0 / ATT_HEAD_DIM)


def _qk_prep(p, gq, gk, cos, sin, name):
    n = p.shape[0]
    tr = min(ROW_TILE, n)

    def one(xv, g, c, s):
        xn = xv * lax.rsqrt(_head_mean(xv * xv) + EPS) * g
        return xn * c + _swap_pairs(xn) * s

    def body(q_ref, k_ref, gq_ref, gk_ref, c_ref, s_ref, qo_ref, ko_ref):
        c, s = c_ref[...], s_ref[...]
        for j in range(ATT_Q_DIM // 128):
            qo_ref[:, j * 128:(j + 1) * 128] = one(q_ref[:, j * 128:(j + 1) * 128], gq_ref[...], c, s).astype(qo_ref.dtype)
        ko_ref[...] = one(k_ref[...], gk_ref[...], c, s).astype(ko_ref.dtype)

    vec = pl.BlockSpec((1, 128), lambda i: (0, 0))
    tab = pl.BlockSpec((tr, 128), lambda i: (i, 0))
    return pl.pallas_call(
        body, name=name, grid=(n // tr,),
        in_specs=[pl.BlockSpec((tr, ATT_Q_DIM), lambda i: (i, 0)), pl.BlockSpec((tr, 128), lambda i: (i, OFF_AK // 128)), vec, vec, tab, tab],
        out_specs=[pl.BlockSpec((tr, ATT_Q_DIM), lambda i: (i, 0)), tab],
        out_shape=[SDS((n, ATT_Q_DIM), MXU_DTYPE), SDS((n, ATT_KV_DIM), MXU_DTYPE)],
        compiler_params=_cp("parallel"))(p, p, gq, gk, cos, sin)


def _qk_prep_bwd(p, gq, gk, cos, sin, dq, dk, name):
    n = p.shape[0]
    tr = min(ROW_TILE, n)

    def one(xv, g, c, s, dout):
        dxn = dout * c + _swap_pairs(dout * s)
        r = lax.rsqrt(_head_mean(xv * xv) + EPS)
        xh = xv * r
        dn = dxn * g
        dx = r * (dn - xh * _head_mean(dn * xh))
        return dx, jnp.sum(dxn * xh, axis=0, keepdims=True)

    def body(q_ref, k_ref, gq_ref, gk_ref, c_ref, s_ref, dq_ref, dk_ref, dqo_ref, dko_ref, dgq_ref, dgk_ref):
        @pl.when(pl.program_id(0) == 0)
        def _():
            dgq_ref[...] = jnp.zeros_like(dgq_ref)
            dgk_ref[...] = jnp.zeros_like(dgk_ref)

        c, s = c_ref[...], s_ref[...]
        for j in range(ATT_Q_DIM // 128):
            sl = slice(j * 128, (j + 1) * 128)
            dx, dg = one(q_ref[:, sl], gq_ref[...], c, s, dq_ref[:, sl])
            dqo_ref[:, sl] = dx.astype(dqo_ref.dtype)
            dgq_ref[:, sl] += dg
        dx, dg = one(k_ref[...], gk_ref[...], c, s, dk_ref[...])
        dko_ref[...] = dx.astype(dko_ref.dtype)
        dgk_ref[...] += dg

    vec = pl.BlockSpec((1, 128), lambda i: (0, 0))
    tab = pl.BlockSpec((tr, 128), lambda i: (i, 0))
    qrow = pl.BlockSpec((tr, ATT_Q_DIM), lambda i: (i, 0))
    return pl.pallas_call(
        body, name=name, grid=(n // tr,),
        in_specs=[qrow, pl.BlockSpec((tr, 128), lambda i: (i, OFF_AK // 128)), vec, vec, tab, tab, qrow, tab],
        out_specs=[qrow, tab, pl.BlockSpec((1, ATT_Q_DIM), lambda i: (0, 0)), vec],
        out_shape=[SDS((n, ATT_Q_DIM), MXU_DTYPE), SDS((n, ATT_KV_DIM), MXU_DTYPE), SDS((1, ATT_Q_DIM), F32), SDS((1, 128), F32)],
        compiler_params=_cp("arbitrary"))(p, p, gq, gk, cos, sin, dq, dk)


ATT_TQ = 256


def _attn_fwd(q, k, v, name):
    n = q.shape[0]
    tq = min(ATT_TQ, n)
    scale = ATT_HEAD_DIM ** -0.5
    gw = ATT_GROUP * ATT_HEAD_DIM

    def body(q_ref, k_ref, v_ref, o_ref):
        kk, vv = k_ref[0], v_ref[0]
        outs = []
        for g in range(ATT_GROUP):
            s = _dot(q_ref[:, g * ATT_HEAD_DIM:(g + 1) * ATT_HEAD_DIM] * scale, kk, "nt")
            e = jnp.exp(s - jnp.max(s, axis=-1, keepdims=True))
            outs.append(_dot(e, vv) / jnp.sum(e, axis=-1, keepdims=True))
        o_ref[...] = jnp.concatenate(outs, axis=-1).astype(o_ref.dtype)

    kv = pl.BlockSpec((1, n, ATT_HEAD_DIM), lambda h, i: (h, 0, 0))
    return pl.pallas_call(
        body, name=name, grid=(ATT_KV_HEADS, n // tq),
        in_specs=[pl.BlockSpec((tq, gw), lambda h, i: (i, h)), kv, kv],
        out_specs=pl.BlockSpec((tq, gw), lambda h, i: (i, h)), out_shape=SDS((n, ATT_Q_DIM), MXU_DTYPE),
        compiler_params=_cp("parallel", "parallel"))(q, k, v)


def _attn_bwd(q, k, v, o, do, name):
    n = q.shape[0]
    tq = min(ATT_TQ, n)
    scale = ATT_HEAD_DIM ** -0.5
    gw = ATT_GROUP * ATT_HEAD_DIM

    def body(q_ref, k_ref, v_ref, o_ref, do_ref, dq_ref, dk_ref, dv_ref):
        @pl.when(pl.program_id(1) == 0)
        def _():
            dk_ref[...] = jnp.zeros_like(dk_ref)
            dv_ref[...] = jnp.zeros_like(dv_ref)

        kk, vv = k_ref[0], v_ref[0]
        dqs = []
        dk_acc = jnp.zeros((ATT_HEAD_DIM, n), F32)
        dv_acc = jnp.zeros((ATT_HEAD_DIM, n), F32)
        for g in range(ATT_GROUP):
            sl = slice(g * ATT_HEAD_DIM, (g + 1) * ATT_HEAD_DIM)
            qg, dog = q_ref[:, sl] * scale, do_ref[:, sl].astype(F32)
            s = _dot(qg, kk, "nt")
            e = jnp.exp(s - jnp.max(s, axis=-1, keepdims=True))
            inv = 1.0 / jnp.sum(e, axis=-1, keepdims=True)
            delta = jnp.sum(dog * o_ref[:, sl].astype(F32), axis=-1, keepdims=True)
            dse = e * (_dot(dog, vv, "nt") - delta)
            dqs.append(_dot(dse, kk) * (inv * scale))
            dk_acc += _dot(qg.astype(F32) * inv, dse, "tn")
            dv_acc += _dot(dog * inv, e, "tn")
        dq_ref[...] = jnp.concatenate(dqs, axis=-1)
        dk_ref[0] += dk_acc
        dv_ref[0] += dv_acc

    kv = pl.BlockSpec((1, n, ATT_HEAD_DIM), lambda h, i: (h, 0, 0))
    kvt = pl.BlockSpec((1, ATT_HEAD_DIM, n), lambda h, i: (h, 0, 0))
    qb = pl.BlockSpec((tq, gw), lambda h, i: (i, h))
    return pl.pallas_call(
        body, name=name, grid=(ATT_KV_HEADS, n // tq), in_specs=[qb, kv, kv, qb, qb], out_specs=[qb, kvt, kvt],
        out_shape=[SDS((n, ATT_Q_DIM), F32), SDS((ATT_KV_HEADS, ATT_HEAD_DIM, n), F32), SDS((ATT_KV_HEADS, ATT_HEAD_DIM, n), F32)],
        compiler_params=_cp("parallel", "arbitrary"))(q, k, v, o, do)


def _both_directions(mats, axis):
    fwd = np.concatenate(mats, axis=axis).astype(np.float32)
    bwd = np.concatenate([m[::-1, ::-1] for m in mats], axis=axis).astype(np.float32)
    return jnp.asarray(np.stack([fwd, bwd]), MXU_DTYPE)


def _hg_segments():
    c = HG_CHUNK
    t = np.arange(c)[:, None]
    r = np.arange(c)[None, :]
    mats = [(r <= t)]
    for lev in range(HG_LEVELS):
        h = c >> (lev + 1)
        mid = (t // (2 * h)) * (2 * h) + h - 1
        hi = (t // h) % 2 == 1
        mats.append(np.where(hi, (r > mid) & (r <= t), (r > t) & (r <= mid)))
    mats.append(r > t)
    return _both_directions(mats, 0)


def _hg_pair_sums():
    c = HG_CHUNK
    r = np.arange(c)[:, None]
    t = np.arange(c)[None, :]
    gp, gn = [t >= r], [t < r]
    for lev in range(HG_LEVELS):
        sh = HG_LEVELS - 1 - lev
        same = (r >> sh) == (t >> sh)
        gp.append(same & (t >= r))
        gn.append(same & (t < r))
    return _both_directions(gp, 1), _both_directions(gn, 1)


def _split_dot(mat, x):
    hi = x.astype(MXU_DTYPE)
    lo = (x - hi.astype(F32)).astype(MXU_DTYPE)
    return _dot(mat, hi) + _dot(mat, lo)


def _hg_gates(hq, z, a0, a1):
    q = hq * _sigmoid(hq)
    sg = _sigmoid(z)
    lb = _sigmoid(a0 - a1)
    f = lb + (1.0 - lb) * sg
    k = (1.0 - lb) * (1.0 - sg)
    return q, f, k, sg, lb


def _hg_level_masks(mirrored):
    c = HG_CHUNK
    row = lax.broadcasted_iota(jnp.int32, (c, 1), 0)
    rr = lax.broadcasted_iota(jnp.int32, (c, c), 0)
    cc = lax.broadcasted_iota(jnp.int32, (c, c), 1)
    his, sames = [], []
    for lev in range(HG_LEVELS):
        sh = HG_LEVELS - 1 - lev
        his.append(jnp.logical_xor(((row >> sh) & 1) == 1, mirrored))
        sames.append((rr >> (sh + 1)) == (cc >> (sh + 1)))
    return his, sames, rr == cc


def _hg_intra(q, k, ex, masks):
    his, sames, eye = masks
    a = jnp.where(eye, jnp.sum(q * k, axis=-1, keepdims=True), 0.0)
    for lev in range(HG_LEVELS):
        e = ex[lev + 1]
        qs = jnp.where(his[lev], q * e, 0.0)
        ks = jnp.where(his[lev], 0.0, k * e)
        a = a + jnp.where(sames[lev], _dot(qs, ks, "nt"), 0.0)
    return a


def _hg_specs(n, with_time):
    c = HG_CHUNK
    nc = n // c

    def chunk(d, i):
        first = d if with_time else 1 - d
        return i + first * (nc - 1 - 2 * i)

    def pcols(off, dir_stride=0):
        return [pl.BlockSpec((c, HG_PAIR), lambda d, i, j=j: (chunk(d, i), off // HG_PAIR + dir_stride // HG_PAIR * d + j)) for j in range(2)]

    specs = dict(
        hq=pcols(OFF_HQ), v=pcols(OFF_HI), z=pcols(OFF_ZF, OFF_ZB - OFF_ZF),
        shared=pl.BlockSpec((c, HG_DIM), lambda d, i: (chunk(d, i), 0)),
        per_dir=pl.BlockSpec((1, c, HG_DIM), lambda d, i: (d, chunk(d, i), 0)),
        vec=pl.BlockSpec((1, 1, HG_DIM), lambda d, i: (d, 0, 0)),
        seg=pl.BlockSpec((1, (HG_LEVELS + 2) * c, c), lambda d, i: (d, 0, 0)),
        sums=pl.BlockSpec((1, c, (HG_LEVELS + 1) * c), lambda d, i: (d, 0, 0)),
        state=pl.BlockSpec((1, HG_HEADS, 1, HG_HEAD_DIM, HG_HEAD_DIM), lambda d, i: (d, 0, chunk(d, i), 0, 0)))
    return nc, specs


def _hg_head(refs, hh):
    off = (hh % 2) * HG_HEAD_DIM
    return refs[hh // 2][:, off:off + HG_HEAD_DIM]


def _hg_lanes(hh):
    return slice(hh * HG_HEAD_DIM, (hh + 1) * HG_HEAD_DIM)


def _hg_exps(seg_ref, f):
    lf = jnp.log(f)
    args = _split_dot(seg_ref[0], lf)
    c = HG_CHUNK
    return [jnp.exp(args[j * c:(j + 1) * c]) for j in range(HG_LEVELS + 2)]


def _hg_last_row(a, mirrored):
    return jnp.where(mirrored, a[0:1, :], a[HG_CHUNK - 1:HG_CHUNK, :])


def _hgrn_fwd(p, a0, a1, seg, name):
    n = p.shape[0]
    nc, sp = _hg_specs(n, True)

    def body(hq0, hq1, z0, z1, v0, v1, a0_ref, a1_ref, seg_ref, o_ref, s0_ref, st):
        @pl.when(pl.program_id(1) == 0)
        def _():
            st[...] = jnp.zeros_like(st)

        mirrored = pl.program_id(0) == 1
        masks = _hg_level_masks(mirrored)
        for hh in range(HG_HEADS):
            ln = _hg_lanes(hh)
            q, f, k, _, _ = _hg_gates(_hg_head((hq0, hq1), hh), _hg_head((z0, z1), hh), a0_ref[0, :, ln], a1_ref[0, :, ln])
            vv = _hg_head((v0, v1), hh)
            ex = _hg_exps(seg_ref, f)
            a = _hg_intra(q, k, ex, masks)
            s_t = st[hh]
            s0_ref[0, hh, 0] = s_t
            o_ref[0, :, ln] = _dot(a, vv) + _dot(q * ex[0], s_t, "nt")
            st[hh] = s_t * _hg_last_row(ex[0], mirrored) + _dot(vv, k * ex[HG_LEVELS + 1], "tn")

    return pl.pallas_call(
        body, name=name, grid=(2, nc), in_specs=sp["hq"] + sp["z"] + sp["v"] + [sp["vec"], sp["vec"], sp["seg"]],
        out_specs=[sp["per_dir"], sp["state"]],
        out_shape=[SDS((2, n, HG_DIM), F32), SDS((2, HG_HEADS, nc, HG_HEAD_DIM, HG_HEAD_DIM), F32)],
        scratch_shapes=[pltpu.VMEM((HG_HEADS, HG_HEAD_DIM, HG_HEAD_DIM), F32)],
        compiler_params=_cp("parallel", "arbitrary"))(p, p, p, p, p, p, a0, a1, seg)


def _hgrn_bwd(p, a0, a1, seg, gp, gn, do, s0, name):
    n = p.shape[0]
    nc, sp = _hg_specs(n, False)


    def body(hq0, hq1, z0, z1, v0, v1, a0_ref, a1_ref, seg_ref, gp_ref, gn_ref, do_ref, s0_ref, dhq_ref, dz_ref, dv_ref, dlb_ref, rt):
        @pl.when(pl.program_id(1) == 0)
        def _():
            rt[...] = jnp.zeros_like(rt)
            dlb_ref[...] = jnp.zeros_like(dlb_ref)

        mirrored = pl.program_id(0) == 1
        masks = _hg_level_masks(mirrored)
        his, sames, eye = masks
        for hh in range(HG_HEADS):
            ln = _hg_lanes(hh)
            hqv = _hg_head((hq0, hq1), hh)
            q, f, k, sg, lb = _hg_gates(hqv, _hg_head((z0, z1), hh), a0_ref[0, :, ln], a1_ref[0, :, ln])
            vv, dov = _hg_head((v0, v1), hh), do_ref[:, ln]
            ex = _hg_exps(seg_ref, f)
            a = _hg_intra(q, k, ex, masks)
            da = _dot(dov, vv, "nt")
            diag = jnp.sum(dov * vv, axis=-1, keepdims=True)
            s_t = s0_ref[0, hh, 0]
            r_t = rt[hh]
            k_end = k * ex[HG_LEVELS + 1]
            dv_ref[0, :, ln] = _dot(a, dov, "tn") + _dot(k_end, r_t, "nt")
            dq_inter = ex[0] * _dot(dov, s_t)
            dk_inter = ex[HG_LEVELS + 1] * _dot(vv, r_t)
            dq = diag * k + dq_inter
            dk = diag * q + dk_inter
            q_terms, k_terms = [q * dq_inter], [k * dk_inter]
            for lev in range(HG_LEVELS):
                e = ex[lev + 1]
                pairs = jnp.where(sames[lev], da, 0.0)
                q_part = jnp.where(his[lev], e, 0.0) * _dot(pairs, jnp.where(his[lev], 0.0, k * e))
                k_part = jnp.where(his[lev], 0.0, e) * _dot(pairs, jnp.where(his[lev], q * e, 0.0), "tn")
                dq, dk = dq + q_part, dk + k_part
                q_terms.append(q * q_part)
                k_terms.append(k * k_part)
            decay = _hg_last_row(ex[0], mirrored)
            rt[hh] = r_t * decay + _dot(dov, q * ex[0], "tn")
            later = decay * jnp.sum(s_t * r_t, axis=0, keepdims=True)
            dlf = _dot(gp_ref[0], jnp.concatenate(q_terms, axis=0)) + _dot(gn_ref[0], jnp.concatenate(k_terms, axis=0)) + later
            df = dlf / f - dk
            dz_ref[0, :, ln] = df * (1.0 - lb) * sg * (1.0 - sg)
            dlb_ref[0, :, ln] += jnp.sum(df * (1.0 - sg), axis=0, keepdims=True)
            sq = _sigmoid(hqv)
            dhq_ref[0, :, ln] = dq * sq * (1.0 + hqv * (1.0 - sq))

    out = SDS((2, n, HG_DIM), F32)
    return pl.pallas_call(
        body, name=name, grid=(2, nc),
        in_specs=sp["hq"] + sp["z"] + sp["v"] + [sp["vec"], sp["vec"], sp["seg"], sp["sums"], sp["sums"], sp["shared"], sp["state"]],
        out_specs=[sp["per_dir"], sp["per_dir"], sp["per_dir"], sp["vec"]], out_shape=[out, out, out, SDS((2, 1, HG_DIM), F32)],
        scratch_shapes=[pltpu.VMEM((HG_HEADS, HG_HEAD_DIM, HG_HEAD_DIM), F32)],
        compiler_params=_cp("parallel", "arbitrary"))(p, p, p, p, p, p, a0, a1, seg, gp, gn, do, s0)


def _hg_post(o2, p, g, name):
    n = p.shape[0]
    tr = min(ROW_TILE, n)
    w = 2 * HG_HEAD_DIM

    def body(of_ref, ob_ref, hg_ref, g_ref, o_ref):
        for j in range(2):
            sl = slice(j * HG_HEAD_DIM, (j + 1) * HG_HEAD_DIM)
            o = of_ref[0, :, sl] + ob_ref[0, :, sl]
            hg = hg_ref[:, sl]
            o_ref[:, sl] = (o * _rstd(o) * g_ref[...] * (hg * _sigmoid(hg))).astype(o_ref.dtype)

    blk = pl.BlockSpec((tr, w), lambda i, j: (i, j))
    dirs = [pl.BlockSpec((1, tr, w), lambda i, j, d=d: (d, i, j)) for d in range(2)]
    return pl.pallas_call(
        body, name=name, grid=(n // tr, HG_DIM // w),
        in_specs=dirs + [pl.BlockSpec((tr, w), lambda i, j: (i, OFF_HG // w + j)), pl.BlockSpec((1, HG_HEAD_DIM), lambda i, j: (0, 0))],
        out_specs=blk, out_shape=SDS((n, HG_DIM), MXU_DTYPE), compiler_params=_cp("parallel", "parallel"))(o2, o2, p, g)


def _hg_post_bwd(o2, p, g, dcat, name):
    n = p.shape[0]
    tr = min(ROW_TILE, n)
    w = 2 * HG_HEAD_DIM

    def body(of_ref, ob_ref, hg_ref, g_ref, d_ref, do_ref, dhg_ref, dg_ref):
        @pl.when(pl.program_id(1) == 0)
        def _():
            dg_ref[...] = jnp.zeros_like(dg_ref)

        for j in range(2):
            sl = slice(j * HG_HEAD_DIM, (j + 1) * HG_HEAD_DIM)
            o = of_ref[0, :, sl] + ob_ref[0, :, sl]
            hg = hg_ref[:, sl]
            d = d_ref[:, sl].astype(F32)
            sg = _sigmoid(hg)
            on = o * _rstd(o) * g_ref[...]
            dhg_ref[:, sl] = (d * on * sg * (1.0 + hg * (1.0 - sg))).astype(dhg_ref.dtype)
            dx, dg = _rms_bwd(o, g_ref[...], d * hg * sg)
            do_ref[:, sl] = dx
            dg_ref[0, :, sl] += dg

    blk = pl.BlockSpec((tr, w), lambda j, i: (i, j))
    dirs = [pl.BlockSpec((1, tr, w), lambda j, i, d=d: (d, i, j)) for d in range(2)]
    return pl.pallas_call(
        body, name=name, grid=(HG_DIM // w, n // tr),
        in_specs=dirs + [pl.BlockSpec((tr, w), lambda j, i: (i, OFF_HG // w + j)), pl.BlockSpec((1, HG_HEAD_DIM), lambda j, i: (0, 0)),
                         pl.BlockSpec((tr, w), lambda j, i: (i, ATT_Q_DIM // w + j))],
        out_specs=[blk, blk, pl.BlockSpec((1, 1, w), lambda j, i: (j, 0, 0))],
        out_shape=[SDS((n, HG_DIM), F32), SDS((n, HG_DIM), MXU_DTYPE), SDS((HG_DIM // w, 1, w), F32)],
        compiler_params=_cp("parallel", "arbitrary"))(o2, o2, p, g, dcat)


XATT_TQ = 512


def _xattn_fwd(q, kv, name):
    n, nm = q.shape[0], kv.shape[0]
    tq = min(XATT_TQ, n)
    scale = X_HEAD_DIM ** -0.5

    def body(q_ref, k_ref, v_ref, o_ref):
        s = _dot(q_ref[...], k_ref[...], "nt") * scale
        e = jnp.exp(s - jnp.max(s, axis=-1, keepdims=True))
        o_ref[...] = _dot(e / jnp.sum(e, axis=-1, keepdims=True), v_ref[...]).astype(o_ref.dtype)

    qb = pl.BlockSpec((tq, X_HEAD_DIM), lambda h, i: (i, h))
    return pl.pallas_call(
        body, name=name, grid=(X_HEADS, n // tq),
        in_specs=[qb, pl.BlockSpec((nm, X_HEAD_DIM), lambda h, i: (0, h)), pl.BlockSpec((nm, X_HEAD_DIM), lambda h, i: (0, X_HEADS + h))],
        out_specs=qb, out_shape=SDS(q.shape, MXU_DTYPE), compiler_params=_cp("parallel", "parallel"))(q, kv, kv)


def _xattn_bwd(q, kv, do, name):
    n, nm = q.shape[0], kv.shape[0]
    tq = min(XATT_TQ, n)
    scale = X_HEAD_DIM ** -0.5

    def body(q_ref, k_ref, v_ref, do_ref, dq_ref, dk_ref, dv_ref):
        @pl.when(pl.program_id(1) == 0)
        def _():
            dk_ref[...] = jnp.zeros_like(dk_ref)
            dv_ref[...] = jnp.zeros_like(dv_ref)

        qv, dov = q_ref[...], do_ref[...]
        s = _dot(qv, k_ref[...], "nt") * scale
        e = jnp.exp(s - jnp.max(s, axis=-1, keepdims=True))
        p = e / jnp.sum(e, axis=-1, keepdims=True)
        dp = _dot(dov, v_ref[...], "nt")
        ds = p * (dp - jnp.sum(p * dp, axis=-1, keepdims=True)) * scale
        dq_ref[...] = _dot(ds, k_ref[...]).astype(dq_ref.dtype)
        dk_ref[...] += _dot(ds, qv, "tn")
        dv_ref[...] += _dot(p, dov, "tn")

    qb = pl.BlockSpec((tq, X_HEAD_DIM), lambda h, i: (i, h))
    kb = pl.BlockSpec((nm, X_HEAD_DIM), lambda h, i: (0, h))
    return pl.pallas_call(
        body, name=name, grid=(X_HEADS, n // tq),
        in_specs=[qb, kb, pl.BlockSpec((nm, X_HEAD_DIM), lambda h, i: (0, X_HEADS + h)), qb], out_specs=[qb, kb, kb],
        out_shape=[SDS(q.shape, MXU_DTYPE), SDS((nm, X_HEADS * X_HEAD_DIM), F32), SDS((nm, X_HEADS * X_HEAD_DIM), F32)],
        compiler_params=_cp("parallel", "arbitrary"))(q, kv, kv, do)


def _edge_rows(shape):
    row = lax.broadcasted_iota(jnp.int32, shape, 0)
    return row == 0, row == shape[0] - 1


def _shift_rows(u, down, edges):
    if down:
        return jnp.where(edges[0], 0.0, pltpu.roll(u, 1, axis=0))
    return jnp.where(edges[1], 0.0, pltpu.roll(u, u.shape[0] - 1, axis=0))


def _conv(u, w, b, edges):
    return b + _shift_rows(u, True, edges) * w[0:1, :] + u * w[1:2, :] + _shift_rows(u, False, edges) * w[2:3, :]


def _ff_specs(n):
    gate = lambda rows: pl.BlockSpec((rows, FF_COLS), lambda j: (0, j))
    val = lambda rows: pl.BlockSpec((rows, FF_COLS), lambda j: (0, FF_BLOCKS + j))
    return [gate(n), val(n), gate(3), val(3), gate(1), val(1)], gate


def _conv_gate(u, cw, cb, name):
    n = u.shape[0]
    ins, gate_blk = _ff_specs(n)

    def body(ug_ref, uv_ref, wg_ref, wv_ref, bg_ref, bv_ref, o_ref):
        edges = _edge_rows(ug_ref.shape)
        gate = _conv(ug_ref[...], wg_ref[...], bg_ref[...], edges)
        val = _conv(uv_ref[...], wv_ref[...], bv_ref[...], edges)
        o_ref[...] = (gate * _sigmoid(gate) * val).astype(o_ref.dtype)

    return pl.pallas_call(
        body, name=name, grid=(FF_BLOCKS,), in_specs=ins, out_specs=gate_blk(n), out_shape=SDS((n, D_FF), MXU_DTYPE),
        compiler_params=_cp("parallel"))(u, u, cw, cw, cb, cb)


def _conv_gate_bwd(u, cw, cb, da, name):
    n = u.shape[0]
    ins, gate_blk = _ff_specs(n)

    def side(dacc, u, w, edges, du_ref, dw_ref, db_ref):
        nxt, prv = _shift_rows(dacc, False, edges), _shift_rows(dacc, True, edges)
        du_ref[...] = (nxt * w[0:1, :] + dacc * w[1:2, :] + prv * w[2:3, :]).astype(du_ref.dtype)
        db_ref[...] = jnp.sum(dacc, axis=0, keepdims=True)
        dw_ref[0:1, :] = jnp.sum(nxt * u, axis=0, keepdims=True)
        dw_ref[1:2, :] = jnp.sum(dacc * u, axis=0, keepdims=True)
        dw_ref[2:3, :] = jnp.sum(prv * u, axis=0, keepdims=True)

    def body(ug_ref, uv_ref, wg_ref, wv_ref, bg_ref, bv_ref, da_ref, dug_ref, duv_ref, dwg_ref, dwv_ref, dbg_ref, dbv_ref):
        ug, uv = ug_ref[...], uv_ref[...]
        edges = _edge_rows(ug.shape)
        gate = _conv(ug, wg_ref[...], bg_ref[...], edges)
        val = _conv(uv, wv_ref[...], bv_ref[...], edges)
        sg = _sigmoid(gate)
        dav = da_ref[...].astype(F32)
        side(dav * val * sg * (1.0 + gate * (1.0 - sg)), ug, wg_ref[...], edges, dug_ref, dwg_ref, dbg_ref)
        side(dav * gate * sg, uv, wv_ref[...], edges, duv_ref, dwv_ref, dbv_ref)

    return pl.pallas_call(
        body, name=name, grid=(FF_BLOCKS,), in_specs=ins + [gate_blk(n)],
        out_specs=[gate_blk(n), gate_blk(n), gate_blk(3), gate_blk(3), gate_blk(1), gate_blk(1)],
        out_shape=[SDS((n, D_FF), MXU_DTYPE)] * 2 + [SDS((3, D_FF), F32)] * 2 + [SDS((1, D_FF), F32)] * 2,
        compiler_params=_cp("parallel"))(u, u, cw, cw, cb, cb, da)


def _adamw(w, g, m, v, name):
    r, c = w.shape
    tr = r if r <= 512 else 256 if r % 256 == 0 else 88
    assert r % tr == 0, (name, r, tr)

    def body(w_ref, g_ref, m_ref, v_ref, d_ref, mo_ref, vo_ref, go_ref):
        gv = g_ref[...]
        go_ref[...] = gv
        mn = ADAM_B1 * m_ref[...] + (1.0 - ADAM_B1) * gv
        vn = ADAM_B2 * v_ref[...] + (1.0 - ADAM_B2) * gv * gv
        m_hat = mn / (1.0 - ADAM_B1 ** ADAM_STEP)
        v_hat = vn / (1.0 - ADAM_B2 ** ADAM_STEP)
        d_ref[...] = -ADAM_LR * (m_hat / (jnp.sqrt(v_hat) + ADAM_EPS) + ADAM_WD * w_ref[...])
        mo_ref[...] = mn
        vo_ref[...] = vn

    blk = pl.BlockSpec((tr, c), lambda i: (i, 0))
    out = SDS((r, c), F32)
    return pl.pallas_call(body, name=name, grid=(r // tr,), in_specs=[blk] * 4, out_specs=[blk] * 4, out_shape=[out] * 4,
                          compiler_params=_cp("parallel"))(w, g, m, v)


def _half_tile(h):
    tr = h if h <= 512 else 256 if h % 256 == 0 else 176
    assert h % tr == 0, (h, tr)
    return tr


ANY = pl.BlockSpec(memory_space=pl.ANY)


def _place():
    x, y, c = lax.axis_index("x"), lax.axis_index("y"), lax.axis_index("c")
    return x, y, c, [(1 - x, y), (x, 1 - y), (1 - x, 1 - y)]


def _gather_shards(shards, name):
    nt = len(shards)

    def body(*refs):
        ins, outs = refs[:nt], refs[nt:2 * nt]
        send, recv, fsend, frecv, osend, orecv = refs[2 * nt:]
        x, y, c, chips = _place()
        me = 2 * x + y

        def half(t, chip, cc):
            h = ins[t].shape[0] // 2
            return outs[t].at[chip, pl.ds(cc * h, h)]

        def ici(t, j):
            cx, cy = chips[j]
            h = ins[t].shape[0] // 2
            return pltpu.make_async_remote_copy(src_ref=ins[t].at[pl.ds(c * h, h)], dst_ref=half(t, me, c),
                                                send_sem=send.at[t, j], recv_sem=recv.at[t, j], device_id=(cx, cy, c), device_id_type=MESH)

        def landed(t, j):
            cx, cy = chips[j]
            blk = half(t, 2 * cx + cy, c)
            return pltpu.make_async_remote_copy(src_ref=blk, dst_ref=blk, send_sem=send.at[t, j], recv_sem=recv.at[t, j],
                                                device_id=(cx, cy, c), device_id_type=MESH)

        def d2d(t, j, cc):
            cx, cy = chips[j]
            blk = half(t, 2 * cx + cy, cc)
            return pltpu.make_async_remote_copy(src_ref=blk, dst_ref=blk, send_sem=fsend.at[t, j], recv_sem=frecv.at[t, j],
                                                device_id=(x, y, 1 - c), device_id_type=MESH)

        own = [pltpu.make_async_remote_copy(src_ref=ins[t], dst_ref=outs[t].at[me], send_sem=osend.at[t], recv_sem=orecv.at[t],
                                            device_id=(x, y, 1 - c), device_id_type=MESH) for t in range(nt)]
        for t in range(nt):
            for j in range(3):
                ici(t, j).start()
        for cp in own:
            cp.start()
        for t in range(nt):
            for j in range(3):
                landed(t, j).wait_recv()
                d2d(t, j, c).start()
        for t in range(nt):
            for j in range(3):
                d2d(t, j, 1 - c).wait_recv()
        for t in range(nt):
            for j in range(3):
                ici(t, j).wait_send()
                d2d(t, j, c).wait_send()
        for cp in own:
            cp.wait()

    return pl.pallas_call(
        body, name=name, in_specs=[ANY] * nt, out_specs=[ANY] * nt,
        out_shape=[SDS((4,) + s.shape, s.dtype) for s in shards],
        scratch_shapes=[pltpu.SemaphoreType.DMA((nt, 3))] * 4 + [pltpu.SemaphoreType.DMA((nt,))] * 2,
        compiler_params=pltpu.CompilerParams(has_side_effects=True))(*shards)


def _join_halves(bufs, name):
    nt = len(bufs)

    def body(*refs):
        outs = refs[nt:2 * nt]
        send, recv = refs[2 * nt:]
        x, y, c, _ = _place()
        cps = [pltpu.make_async_remote_copy(src_ref=outs[t].at[c], dst_ref=outs[t].at[c], send_sem=send.at[t], recv_sem=recv.at[t],
                                            device_id=(x, y, 1 - c), device_id_type=MESH) for t in range(nt)]
        for cp in cps:
            cp.start()
        for t in range(nt):
            theirs = outs[t].at[1 - c]
            pltpu.make_async_remote_copy(src_ref=theirs, dst_ref=theirs, send_sem=send.at[t], recv_sem=recv.at[t],
                                         device_id=(x, y, 1 - c), device_id_type=MESH).wait_recv()
        for cp in cps:
            cp.wait_send()

    return pl.pallas_call(
        body, name=name, in_specs=[ANY] * nt, out_specs=[ANY] * nt, out_shape=[SDS(b.shape, b.dtype) for b in bufs],
        input_output_aliases={t: t for t in range(nt)},
        scratch_shapes=[pltpu.SemaphoreType.DMA((nt,))] * 2,
        compiler_params=pltpu.CompilerParams(has_side_effects=True))(*bufs)


def _exchange_small(v, reduce, name, after=()):
    rows = v.shape[0]
    after, after_specs = _unread(after)

    def body(v_ref, *rest):
        o_ref, buf, send, recv = rest[-4:]
        x, y, c, _ = _place()
        me = 4 * x + 2 * y + c
        buf[me] = v_ref[...]

        def peer(dx, dy, dc):
            return (1 - x if dx else x, 1 - y if dy else y, 1 - c if dc else c)

        peers = [(dx, dy, dc) for dx in range(2) for dy in range(2) for dc in range(2) if (dx, dy, dc) != (0, 0, 0)]
        cps = []
        for j, (dx, dy, dc) in enumerate(peers):
            cps.append(pltpu.make_async_remote_copy(src_ref=v_ref, dst_ref=buf.at[me], send_sem=send.at[j], recv_sem=recv.at[j],
                                                    device_id=peer(dx, dy, dc), device_id_type=MESH))
        for cp in cps:
            cp.start()
        for j, (dx, dy, dc) in enumerate(peers):
            px, py, pc = peer(dx, dy, dc)
            blk = buf.at[4 * px + 2 * py + pc]
            pltpu.make_async_remote_copy(src_ref=blk, dst_ref=blk, send_sem=send.at[j], recv_sem=recv.at[j],
                                         device_id=(px, py, pc), device_id_type=MESH).wait_recv()
        for cp in cps:
            cp.wait_send()
        if reduce:
            acc = buf[0]
            for j in range(1, 8):
                acc = acc + buf[j]
            o_ref[...] = acc
        else:
            o_ref[...] = buf[...]

    vm = pl.BlockSpec(memory_space=pltpu.VMEM)
    return pl.pallas_call(
        body, name=name, in_specs=[vm] + after_specs, out_specs=vm, out_shape=SDS((rows, 128) if reduce else (8, rows, 128), F32),
        scratch_shapes=[pltpu.VMEM((8, rows, 128), F32), pltpu.SemaphoreType.DMA((7,)), pltpu.SemaphoreType.DMA((7,))],
        compiler_params=pltpu.CompilerParams(has_side_effects=True))(v, *after)


HBM = pl.BlockSpec(memory_space=pltpu.HBM)
SEM = pl.BlockSpec(memory_space=pltpu.SEMAPHORE)
TOKEN = pl.BlockSpec(memory_space=pltpu.VMEM)
TOKEN_SHAPE = SDS((8, 128), F32)
PEERS = 7


def _in_hbm(a):
    return pltpu.with_memory_space_constraint(a, pltpu.HBM)


def _split_params():
    return pltpu.CompilerParams(has_side_effects=pltpu.SideEffectType.DATAFLOW_SIDE_EFFECTING)


def _gather_start(shards, name, after=()):
    nt = len(shards)
    after, after_specs = _unread(after)

    def body(*refs):
        ins, lands = refs[:nt], refs[nt:2 * nt]
        outs = refs[2 * nt + len(after):]
        sends, recvs = outs[:nt], outs[nt:2 * nt]
        x, y, c, chips = _place()
        me = 2 * x + y
        for t in range(nt):
            h = ins[t].shape[0] // 2
            mine = pl.ds(c * h, h)
            for j, (cx, cy) in enumerate(chips):
                for dc in range(2):
                    pltpu.make_async_remote_copy(src_ref=ins[t].at[mine], dst_ref=lands[t].at[me, mine], send_sem=sends[t].at[2 * j + dc],
                                                 recv_sem=recvs[t].at[2 * j + c], device_id=(cx, cy, dc), device_id_type=MESH).start()
            pltpu.make_async_remote_copy(src_ref=ins[t], dst_ref=lands[t].at[me], send_sem=sends[t].at[PEERS - 1], recv_sem=recvs[t].at[PEERS - 1],
                                         device_id=(x, y, 1 - c), device_id_type=MESH).start()
        outs[-1][...] = jnp.zeros(TOKEN_SHAPE.shape, F32)

    lands = [lax.empty((4,) + s.shape, s.dtype) for s in shards]
    out = pl.pallas_call(
        body, name=name, in_specs=[HBM] * (2 * nt) + after_specs, out_specs=[SEM] * (2 * nt) + [HBM] * (2 * nt) + [TOKEN],
        out_shape=[pltpu.SemaphoreType.DMA((PEERS,))] * (2 * nt)
        + [pltpu.HBM(s.shape, s.dtype) for s in shards] + [pltpu.HBM(l.shape, l.dtype) for l in lands] + [TOKEN_SHAPE],
        input_output_aliases={t: 2 * nt + t for t in range(2 * nt)}, compiler_params=_split_params())(
            *[_in_hbm(s) for s in shards], *[_in_hbm(l) for l in lands], *after)
    return out[:nt], out[nt:2 * nt], out[2 * nt:3 * nt], out[3 * nt:4 * nt], out[-1]


def _gather_wait(sends, recvs, shards, lands, after, name):
    nt = len(shards)

    def body(*refs):
        ins, lands_ref = refs[:nt], refs[nt:2 * nt]
        send_refs, recv_refs = refs[2 * nt:3 * nt], refs[3 * nt:4 * nt]
        x, y, c, chips = _place()
        for t in range(nt):
            h = ins[t].shape[0] // 2
            for j, (cx, cy) in enumerate(chips):
                for cs in range(2):
                    blk = lands_ref[t].at[2 * cx + cy, pl.ds(cs * h, h)]
                    pltpu.make_async_remote_copy(src_ref=blk, dst_ref=blk, send_sem=send_refs[t].at[2 * j + cs], recv_sem=recv_refs[t].at[2 * j + cs],
                                                 device_id=(cx, cy, cs), device_id_type=MESH).wait()
            blk = lands_ref[t].at[2 * x + y]
            pltpu.make_async_remote_copy(src_ref=blk, dst_ref=blk, send_sem=send_refs[t].at[PEERS - 1], recv_sem=recv_refs[t].at[PEERS - 1],
                                         device_id=(x, y, 1 - c), device_id_type=MESH).wait()

    out = pl.pallas_call(
        body, name=name, in_specs=[HBM] * (2 * nt) + [SEM] * (2 * nt) + [ANY], out_specs=[HBM] * (2 * nt),
        out_shape=[pltpu.HBM(s.shape, s.dtype) for s in shards] + [pltpu.HBM(l.shape, l.dtype) for l in lands],
        input_output_aliases={t: t for t in range(2 * nt)}, compiler_params=_split_params())(*shards, *lands, *sends, *recvs, after)
    return out[nt:]


def _scatter_start(g, name):
    _, r, c_ = g.shape
    h = r // 2

    def body(g_ref, land, send, recv, g_thru, land_thru, token):
        x, y, c, chips = _place()
        for j, (cx, cy) in enumerate(chips):
            for dc in range(2):
                pltpu.make_async_remote_copy(src_ref=g_ref.at[2 * cx + cy, pl.ds(dc * h, h)], dst_ref=land.at[2 * j + c], send_sem=send.at[2 * j + dc],
                                             recv_sem=recv.at[2 * j + c], device_id=(cx, cy, dc), device_id_type=MESH).start()
        pltpu.make_async_remote_copy(src_ref=g_ref.at[2 * x + y, pl.ds((1 - c) * h, h)], dst_ref=land.at[PEERS - 1], send_sem=send.at[PEERS - 1],
                                     recv_sem=recv.at[PEERS - 1], device_id=(x, y, 1 - c), device_id_type=MESH).start()
        token[...] = jnp.zeros(TOKEN_SHAPE.shape, F32)

    land = lax.empty((PEERS, h, c_), g.dtype)
    return pl.pallas_call(
        body, name=name, in_specs=[HBM, HBM], out_specs=[SEM, SEM, HBM, HBM, TOKEN],
        out_shape=[pltpu.SemaphoreType.DMA((PEERS,)), pltpu.SemaphoreType.DMA((PEERS,)), pltpu.HBM(g.shape, g.dtype),
                   pltpu.HBM(land.shape, land.dtype), TOKEN_SHAPE],
        input_output_aliases={0: 2, 1: 3}, compiler_params=_split_params())(_in_hbm(g), _in_hbm(land))


def _scatter_wait(started, after, name):
    nt = len(started)

    def body(*refs):
        lands = refs[nt:2 * nt]
        sends, recvs = refs[2 * nt:3 * nt], refs[3 * nt:4 * nt]
        x, y, c, chips = _place()
        peers = [(cx, cy, dc) for cx, cy in chips for dc in range(2)] + [(x, y, 1 - c)]
        for t in range(nt):
            for k, peer in enumerate(peers):
                blk = lands[t].at[k]
                pltpu.make_async_remote_copy(src_ref=blk, dst_ref=blk, send_sem=sends[t].at[k], recv_sem=recvs[t].at[k],
                                             device_id=peer, device_id_type=MESH).wait()

    gs, lands = [s[2] for s in started], [s[3] for s in started]
    after, after_specs = _unread(after)
    out = pl.pallas_call(
        body, name=name, in_specs=[HBM] * (2 * nt) + [SEM] * (2 * nt) + after_specs, out_specs=[HBM] * (2 * nt),
        out_shape=[pltpu.HBM(a.shape, a.dtype) for a in gs + lands],
        input_output_aliases={t: t for t in range(2 * nt)}, compiler_params=_split_params())(
            *gs, *lands, *[s[0] for s in started], *[s[1] for s in started], *after)
    return out[:nt], out[nt:]


def _sum_devices(g, land, me, core, name):
    npeer, h, c = land.shape
    tr = _half_tile(h)
    steps = h // tr

    def body(ix_ref, own_ref, land_ref, o_ref):
        acc = own_ref[0].astype(F32)
        for j in range(npeer):
            acc = acc + land_ref[j].astype(F32)
        o_ref[0] = acc

    grid_spec = pltpu.PrefetchScalarGridSpec(
        num_scalar_prefetch=1, grid=(steps,),
        in_specs=[pl.BlockSpec((1, tr, c), lambda i, ix: (ix[0], ix[1] * steps + i, 0)), pl.BlockSpec((npeer, tr, c), lambda i, ix: (0, i, 0))],
        out_specs=pl.BlockSpec((1, tr, c), lambda i, ix: (ix[1], i, 0)))
    return pl.pallas_call(body, name=name, grid_spec=grid_spec, out_shape=SDS((2, h, c), F32),
                          compiler_params=_cp("parallel"))(jnp.stack([me, core]), g, land)


def _pack_small(parts):
    flat = jnp.concatenate([p.reshape(-1) for p in parts])
    total = flat.shape[0]
    rows = -(-total // 1024) * 8
    return jnp.pad(flat, (0, rows * 128 - total)).reshape(rows, 128)


def _unpack_small(packed, shapes):
    flat = packed.reshape(-1)
    out, off = [], 0
    for s in shapes:
        size = int(np.prod(s))
        out.append(flat[off:off + size].reshape(s))
        off += size
    return out


def _local_step(x, mem, target, w_in, first_after, mid_weights, ffn_weights, on_grad, gains, conv_w, conv_b, hg_lb):
    n = x.shape[0]
    cos, sin = _rope_tables(n)
    seg = _hg_segments()
    gp, gn = _hg_pair_sums()
    gq2 = jnp.tile(gains["q_norm_g"], (1, 2))
    gk2 = jnp.tile(gains["k_norm_g"], (1, 2))
    a0 = hg_lb[:, 0:1, :]
    a1 = hg_lb[:, 1:2, :]

    p, h1 = _norm_mm(x, gains["pre_mix_g"], w_in, F32, TOKEN_TILE, 1664, "in_proj", after=(first_after,))
    qr, kr = _qk_prep(p, gq2, gk2, cos, sin, "qk_prep")
    heads = lambda a: a.reshape(n, ATT_KV_HEADS, ATT_HEAD_DIM).transpose(1, 0, 2)
    kh = heads(kr)
    vh = heads(p[:, OFF_AV:OFF_AV + ATT_KV_DIM].astype(MXU_DTYPE))
    att = _attn_fwd(qr, kh, vh, "attn_fwd")
    o2, s0 = _hgrn_fwd(p, a0, a1, seg, "hgrn_fwd")
    rec = _hg_post(o2, p, gains["hg_out_norm_g"], "hg_post")
    cat = jnp.concatenate([att, rec], axis=1)
    w_out, w_xq, w_xkv, w_xo = mid_weights(cat)
    mixed, x1 = _mm_resid_norm(cat, w_out, x, gains["post_mix_g"], 512, "out_proj_resid")
    xq, h2 = _norm_mm(x1, gains["pre_x_g"], w_xq, MXU_DTYPE, TOKEN_TILE, 1024, "xq_proj")
    kv, mn = _norm_mm(mem, gains["mem_norm_g"], w_xkv, MXU_DTYPE, 256, 2048, "xkv_proj")
    ox = _xattn_fwd(xq, kv, "xattn_fwd")
    xo, x2 = _mm_resid_norm(ox, w_xo, x1, gains["post_x_g"], 512, "xo_proj_resid")
    w_up = ffn_weights("w_up", x2)
    u, h3 = _norm_mm(x2, gains["pre_ffn_g"], w_up, F32, TOKEN_TILE, 1408, "up_proj")
    act = _conv_gate(u, conv_w, conv_b, "conv_gate")
    w_down = ffn_weights("w_down", act)
    dn, d3, loss = _mm_resid_norm(act, w_down, x2, gains["post_ffn_g"], 512, "down_proj_resid_loss", target=target)

    gs = {}
    d_dn, gs["post_ffn_g"] = _norm_bwd(dn, gains["post_ffn_g"], d3, None, MXU_DTYPE, "ffn_post_bwd")
    tok = on_grad("w_down", _mm(act, d_dn, "tn", WIRE_DTYPE, 1408, 1024, "down_dw"))
    d_act = _mm(d_dn, w_down, "nt", F32, TOKEN_TILE, 1408, "down_dx", after=(tok,))
    du_g, du_v, dcw_g, dcw_v, dcb_g, dcb_v = _conv_gate_bwd(u, conv_w, conv_b, d_act, "conv_gate_bwd")
    gs["conv_w"] = jnp.concatenate([dcw_g, dcw_v], axis=1)
    gs["conv_b"] = jnp.concatenate([dcb_g, dcb_v], axis=1)
    ff_shard = w_up.shape[2]
    g_up = _dw_by_owner(h3, du_g, ff_shard, 0, None, 512, "up_dw_gate")
    tok = on_grad("w_up", _dw_by_owner(h3, du_v, ff_shard, 2, g_up, 512, "up_dw_value"))
    d2, gs["pre_ffn_g"] = _dx_norm_bwd([(du_g, 0), (du_g, 1), (du_v, 0), (du_v, 1)], w_up, x2, gains["pre_ffn_g"], d3, 512,
                                       "up_dx_pre_bwd", after=(tok,))
    d_xo, gs["post_x_g"] = _norm_bwd(xo, gains["post_x_g"], d2, None, MXU_DTYPE, "x_post_bwd")
    tok = on_grad("w_xo", _mm(ox, d_xo, "tn", WIRE_DTYPE, 512, 1024, "xo_dw"))
    d_ox = _mm(d_xo, w_xo, "nt", MXU_DTYPE, TOKEN_TILE, 1024, "xo_dx", after=(tok,))
    d_xq, d_k, d_v = _xattn_bwd(xq, kv, d_ox, "xattn_bwd")
    d_kv = jnp.concatenate([d_k, d_v], axis=1).astype(MXU_DTYPE)
    tok = on_grad("w_xq", _mm(h2, d_xq, "tn", WIRE_DTYPE, 512, 1024, "xq_dw"))
    tok_kv = on_grad("w_xkv", _dw_by_owner(mn, d_kv, w_xkv.shape[2], 0, None, 512, "xkv_dw"))
    d1, gs["pre_x_g"] = _dx_norm_bwd([(d_xq, 0)], w_xq[None], x1, gains["pre_x_g"], d2, 512, "xq_dx_pre_bwd", after=(tok, tok_kv))
    d_mn = _mm_nt_parts([(d_kv, s) for s in range(4)], w_xkv, F32, 256, 1024, "xkv_dx")
    _, gs["mem_norm_g"] = _norm_bwd(mem, gains["mem_norm_g"], d_mn, None, MXU_DTYPE, "mem_norm_bwd")
    d_mixed, gs["post_mix_g"] = _norm_bwd(mixed, gains["post_mix_g"], d1, None, MXU_DTYPE, "mix_post_bwd")
    tok = on_grad("w_out", _mm(cat, d_mixed, "tn", WIRE_DTYPE, 512, 1024, "out_dw"))
    d_cat = _mm(d_mixed, w_out, "nt", MXU_DTYPE, TOKEN_TILE, 1024, "out_dx", after=(tok,))
    d_o, d_hg, dg_hg = _hg_post_bwd(o2, p, gains["hg_out_norm_g"], d_cat, "hg_post_bwd")
    gs["hg_out_norm_g"] = dg_hg.reshape(HG_HEADS, HG_HEAD_DIM).sum(axis=0, keepdims=True)
    dhq2, dz2, dhv2, dlb = _hgrn_bwd(p, a0, a1, seg, gp, gn, d_o, s0, "hgrn_bwd")
    lb = jax.nn.sigmoid(a0 - a1)
    da0 = dlb * lb * (1.0 - lb)
    gs["hg_lb"] = jnp.concatenate([da0, -da0], axis=1)
    d_qr, d_kh, d_vh = _attn_bwd(qr, kh, vh, cat, d_cat, "attn_bwd")
    unheads = lambda a: a.transpose(2, 0, 1).reshape(n, ATT_KV_DIM)
    d_aq, d_ak, dgq, dgk = _qk_prep_bwd(p, gq2, gk2, cos, sin, d_qr, unheads(d_kh), "qk_prep_bwd")
    gs["q_norm_g"] = dgq.reshape(ATT_HEADS, ATT_HEAD_DIM).sum(axis=0, keepdims=True)
    gs["k_norm_g"] = dgk.reshape(ATT_KV_HEADS, ATT_HEAD_DIM).sum(axis=0, keepdims=True)
    d_p = jnp.concatenate([d_aq, d_ak, unheads(d_vh).astype(MXU_DTYPE), (dhq2[0] + dhq2[1]).astype(MXU_DTYPE),
                           dz2[0].astype(MXU_DTYPE), dz2[1].astype(MXU_DTYPE), (dhv2[0] + dhv2[1]).astype(MXU_DTYPE), d_hg], axis=1)
    tok = on_grad("w_in", _mm(h1, d_p, "tn", WIRE_DTYPE, 512, 1664, "in_dw"))
    grad_x, gs["pre_mix_g"] = _dx_norm_bwd([(d_p, 0)], w_in[None], x, gains["pre_mix_g"], d1, 512, "in_dx_pre_bwd", after=(tok,))
    return loss, grad_x, gs


MATS = ("w_in", "w_out", "w_xq", "w_xkv", "w_xo", "w_up", "w_down")
GAINS = ("pre_mix_g", "q_norm_g", "k_norm_g", "hg_out_norm_g", "post_mix_g", "pre_x_g", "mem_norm_g", "post_x_g", "pre_ffn_g", "post_ffn_g")
WEIGHTS = ('pre_mix_g', 'w_in', 'q_norm_g', 'k_norm_g', 'hg_lb', 'hg_out_norm_g', 'w_out', 'post_mix_g', 'pre_x_g', 'mem_norm_g', 'w_xq',
           'w_xkv', 'w_xo', 'post_x_g', 'pre_ffn_g', 'w_up', 'conv_w', 'conv_b', 'w_down', 'post_ffn_g')


def kernel(x, mem, pre_mix_g, w_in, q_norm_g, k_norm_g, hg_lb, hg_out_norm_g, w_out, post_mix_g, pre_x_g, mem_norm_g, w_xq, w_xkv, w_xo, post_x_g, pre_ffn_g, w_up, conv_w, conv_b, w_down, post_ffn_g, loss_target, m_pre_mix_g, m_w_in, m_q_norm_g, m_k_norm_g, m_hg_lb, m_hg_out_norm_g, m_w_out, m_post_mix_g, m_pre_x_g, m_mem_norm_g, m_w_xq, m_w_xkv, m_w_xo, m_post_x_g, m_pre_ffn_g, m_w_up, m_conv_w, m_conv_b, m_w_down, m_post_ffn_g, v_pre_mix_g, v_w_in, v_q_norm_g, v_k_norm_g, v_hg_lb, v_hg_out_norm_g, v_w_out, v_post_mix_g, v_pre_x_g, v_mem_norm_g, v_w_xq, v_w_xkv, v_w_xo, v_post_x_g, v_pre_ffn_g, v_w_up, v_conv_w, v_conv_b, v_w_down, v_post_ffn_g):
    args = dict(locals())
    w = {k: args[k] for k in WEIGHTS}
    m = {k: args["m_" + k] for k in WEIGHTS}
    v = {k: args["v_" + k] for k in WEIGHTS}
    chip = 2 * lax.axis_index("x") + lax.axis_index("y")
    core = lax.axis_index("c")

    shards = {k: w[k][0].astype(WIRE_DTYPE) for k in MATS}

    def whole(k, g):
        return g if k in ("w_xkv", "w_up") else g.reshape(-1, g.shape[-1])

    w_in_shards = _gather_shards([shards["w_in"]], "gather_w_in")[0]
    w_in_full = jnp.concatenate([w_in_shards[s] for s in range(4)], axis=1)
    small_in = _exchange_small(_pack_small([w["conv_w"][0], w["hg_lb"]]), False, "gather_small")
    mid_names, ffn_names = ("w_out", "w_xq", "w_xkv", "w_xo"), ("w_up", "w_down")
    mid = _gather_start([shards[k] for k in mid_names], "gather_mid_start", after=(w_in_full, small_in))
    ffn = _gather_start([shards[k] for k in ffn_names], "gather_ffn_start", after=(mid[4],))

    def mid_weights(after):
        return [whole(k, g) for k, g in zip(mid_names, _gather_wait(*mid[:4], after, "gather_mid_wait"))]

    def ffn_weights(k, after):
        t = ffn_names.index(k)
        return whole(k, _gather_wait(*[part[t:t + 1] for part in ffn[:4]], after, "gather_wait_" + k)[0])

    cw_parts, lb_parts = [], []
    for s in range(4):
        cw_s, lb_s = _unpack_small(small_in[2 * s], [w["conv_w"][0].shape, w["hg_lb"].shape])
        cw_parts.append(cw_s)
        lb_parts.append(lb_s)
    conv_w_full = jnp.concatenate(cw_parts, axis=1)
    hg_lb_full = jnp.concatenate(lb_parts, axis=2)

    started = {}

    def on_grad(k, g):
        if k == "w_in":
            g = g.reshape(g.shape[0], 4, g.shape[1] // 4).transpose(1, 0, 2)
        elif g.ndim == 2:
            g = g.reshape(4, g.shape[0] // 4, g.shape[1])
        *started[k], token = _scatter_start(g, "grad_start_" + k)
        return token

    gains = {k: w[k] for k in GAINS}
    loss_part, grad_x, gs = _local_step(x[0], mem[0], loss_target[0], w_in_full, ffn[4], mid_weights, ffn_weights, on_grad, gains,
                                        conv_w_full, w["conv_b"], hg_lb_full)
    gs["loss"] = loss_part

    grads, delta, new_m, new_v = {}, {}, {}, {}

    def reduce_matrices(names, after, tag):
        sent, landed = _scatter_wait([started[k] for k in names], after, "grad_wait_" + tag)
        halves = [_sum_devices(g, land, chip, core, "grad_sum_" + k) for k, g, land in zip(names, sent, landed)]
        for k, r in zip(names, _join_halves(halves, "grad_join_" + tag)):
            grads[k] = r.reshape(1, -1, r.shape[-1])

    def adamw(names):
        for k in names:
            shape = w[k].shape
            two_d = lambda a: a.reshape(-1, shape[-1])
            d, mo, vo, go = _adamw(two_d(w[k]), two_d(grads[k]), two_d(m[k]), two_d(v[k]), "adamw_" + k)
            delta[k], new_m[k], new_v[k], grads[k] = d.reshape(shape), mo.reshape(shape), vo.reshape(shape), go.reshape(shape)

    early = tuple(k for k in MATS if k != "w_in")
    reduce_matrices(early, (grad_x,), "early")
    adamw(early)

    small_names = GAINS + ("conv_b", "conv_w", "hg_lb")
    packed = _pack_small([gs[k] for k in small_names + ("loss",)])
    reduced_small = _exchange_small(packed, True, "reduce_small", after=tuple(new_v[k] for k in early))
    *summed, loss = _unpack_small(reduced_small, [gs[k].shape for k in small_names + ("loss",)])
    loss = loss[0, 0]
    for k, g in zip(small_names, summed):
        grads[k] = g
    ncw = w["conv_w"].shape[2]
    grads["conv_w"] = lax.dynamic_slice_in_dim(grads["conv_w"], chip * ncw, ncw, axis=1)[None]
    nlb = w["hg_lb"].shape[2]
    grads["hg_lb"] = lax.dynamic_slice_in_dim(grads["hg_lb"], chip * nlb, nlb, axis=2)
    replicated = GAINS + ("conv_b",)
    shapes = [w[k].shape for k in replicated]
    rows = sum(int(np.prod(s)) for s in shapes) // 128
    pack = lambda d: jnp.concatenate([d[k].reshape(-1) for k in replicated]).reshape(rows, 128)
    outs = _adamw(pack(w), reduced_small[:rows], pack(m), pack(v), "adamw_replicated")
    for into, packed_out in zip((delta, new_m, new_v, grads), outs):
        for k, a in zip(replicated, _unpack_small(packed_out, shapes)):
            into[k] = a
    adamw(("conv_w", "hg_lb"))

    reduce_matrices(("w_in",), tuple(new_v[k] for k in early + small_names), "late")
    adamw(("w_in",))
    return (loss, grad_x[None], *[grads[k] for k in WEIGHTS], *[delta[k] for k in WEIGHTS],
            *[new_m[k] for k in WEIGHTS], *[new_v[k] for k in WEIGHTS])
```

```python
import functools

import numpy as np
import jax
import jax.numpy as jnp
from jax import lax
from jax.experimental import pallas as pl
from jax.experimental.pallas import tpu as pltpu

F32 = jnp.float32
MXU_DTYPE = jnp.bfloat16
WIRE_DTYPE = jnp.bfloat16
VMEM_LIMIT_BYTES = 56 * 1024 * 1024
EPS = 1e-6
MESH = pl.DeviceIdType.MESH

D_MODEL = 1024
GRID_W = 64
ATT_HEADS, ATT_KV_HEADS, ATT_HEAD_DIM = 8, 2, 64
ATT_GROUP = ATT_HEADS // ATT_KV_HEADS
ATT_Q_DIM, ATT_KV_DIM = 512, 128
ROPE_THETA = 10000.0
HG_HEADS, HG_HEAD_DIM, HG_DIM = 4, 128, 512
HG_CHUNK = 128
HG_LEVELS = 7
HG_PAIR = 2 * HG_HEAD_DIM
X_HEADS, X_HEAD_DIM = 4, 256
D_FF = 2816
FF_COLS = 256
FF_BLOCKS = D_FF // FF_COLS
N_IN = 3328
OFF_AQ, OFF_AK, OFF_AV, OFF_HQ, OFF_ZF, OFF_ZB, OFF_HI, OFF_HG = 0, 512, 640, 768, 1280, 1792, 2304, 2816

ADAM_LR, ADAM_B1, ADAM_B2, ADAM_EPS, ADAM_WD, ADAM_STEP = 0.001, 0.9, 0.999, 1e-08, 0.01, 10

SDS = jax.ShapeDtypeStruct


def _cp(*sem):
    return pltpu.CompilerParams(dimension_semantics=sem, vmem_limit_bytes=VMEM_LIMIT_BYTES)


def _dot(a, b, form="nn"):
    dims = {"nn": (((1,), (0,)), ((), ())), "nt": (((1,), (1,)), ((), ())), "tn": (((0,), (0,)), ((), ()))}[form]
    return lax.dot_general(a.astype(MXU_DTYPE), b.astype(MXU_DTYPE), dims, preferred_element_type=F32)


def _sigmoid(x):
    return 1.0 / (1.0 + jnp.exp(-x))


def _rstd(x):
    return lax.rsqrt(jnp.mean(x * x, axis=-1, keepdims=True) + EPS)


def _rms_bwd(x, g, dy):
    r = _rstd(x)
    xh = x * r
    dn = dy * g
    dx = r * (dn - xh * jnp.mean(dn * xh, axis=-1, keepdims=True))
    return dx, jnp.sum(dy * xh, axis=0, keepdims=True)


def _unread(after):
    after = tuple(a for a in after if a is not None)
    return after, [pl.BlockSpec(memory_space=pl.ANY)] * len(after)


def _mm(a, b, form, out_dtype, tm, tn, name, after=()):
    after, after_specs = _unread(after)
    if form == "nn":
        (m, k), n = a.shape, b.shape[1]
    elif form == "nt":
        (m, k), n = a.shape, b.shape[0]
    else:
        (k, m), n = a.shape, b.shape[1]
    tm, tn = min(tm, m), min(tn, n)
    assert m % tm == 0 and n % tn == 0, (name, m, n, tm, tn)

    def body(a_ref, b_ref, *rest):
        o_ref = rest[-1]
        o_ref[...] = _dot(a_ref[...], b_ref[...], form).astype(o_ref.dtype)

    a_spec = pl.BlockSpec((k, tm), lambda i, j: (0, i)) if form == "tn" else pl.BlockSpec((tm, k), lambda i, j: (i, 0))
    b_spec = pl.BlockSpec((tn, k), lambda i, j: (j, 0)) if form == "nt" else pl.BlockSpec((k, tn), lambda i, j: (0, j))
    return pl.pallas_call(
        body, name=name, grid=(m // tm, n // tn), in_specs=[a_spec, b_spec] + after_specs,
        out_specs=pl.BlockSpec((tm, tn), lambda i, j: (i, j)), out_shape=SDS((m, n), out_dtype),
        compiler_params=_cp("parallel", "parallel"))(a, b, *after)


def _mm_nt_parts(a_parts, b, out_dtype, tm, tn, name, after=()):
    after, after_specs = _unread(after)
    parts, n, p = b.shape
    m = a_parts[0][0].shape[0]
    tm, tn = min(tm, m), min(tn, n)
    assert m % tm == 0 and n % tn == 0 and len(a_parts) == parts, (name, m, b.shape)

    def body(*refs):
        o_ref = refs[-1]
        acc = _dot(refs[0][...], refs[parts][0], "nt")
        for s in range(1, parts):
            acc = acc + _dot(refs[s][...], refs[parts + s][0], "nt")
        o_ref[...] = acc.astype(o_ref.dtype)

    a_specs = [pl.BlockSpec((tm, p), lambda i, j, cb=cb: (i, cb)) for _, cb in a_parts]
    b_specs = [pl.BlockSpec((1, tn, p), lambda i, j, s=s: (s, j, 0)) for s in range(parts)]
    return pl.pallas_call(
        body, name=name, grid=(m // tm, n // tn), in_specs=a_specs + b_specs + after_specs,
        out_specs=pl.BlockSpec((tm, tn), lambda i, j: (i, j)), out_shape=SDS((m, n), out_dtype),
        compiler_params=_cp("parallel", "parallel"))(*[arr for arr, _ in a_parts], *([b] * parts), *after)


def _norm_bwd_mm(y, g, d, w, out_dtype, tm, tn, name):
    n, dm = y.shape
    nn = w.shape[0]
    tm, tn = min(tm, n), min(tn, nn)
    assert n % tm == 0 and nn % tn == 0 and w.shape[1] == dm, (name, y.shape, w.shape)

    def body(y_ref, g_ref, d_ref, w_ref, dx_ref, dy_ref, dg_ref, dys):
        i, j = pl.program_id(0), pl.program_id(1)

        @pl.when(jnp.logical_and(i == 0, j == 0))
        def _():
            dg_ref[...] = jnp.zeros_like(dg_ref)

        @pl.when(j == 0)
        def _():
            dy, dg = _rms_bwd(y_ref[...], g_ref[...], d_ref[...])
            dy = dy.astype(MXU_DTYPE)
            dys[...] = dy
            dy_ref[...] = dy
            dg_ref[...] += dg

        dx_ref[...] = _dot(dys[...], w_ref[...], "nt").astype(dx_ref.dtype)

    row = pl.BlockSpec((tm, dm), lambda i, j: (i, 0))
    vec = pl.BlockSpec((1, dm), lambda i, j: (0, 0))
    return pl.pallas_call(
        body, name=name, grid=(n // tm, nn // tn), in_specs=[row, vec, row, pl.BlockSpec((tn, dm), lambda i, j: (j, 0))],
        out_specs=[pl.BlockSpec((tm, tn), lambda i, j: (i, j)), row, vec],
        out_shape=[SDS((n, nn), out_dtype), SDS((n, dm), MXU_DTYPE), SDS((1, dm), F32)],
        scratch_shapes=[pltpu.VMEM((tm, dm), MXU_DTYPE)],
        compiler_params=_cp("arbitrary", "arbitrary"))(y, g, d, w)


def _mm_resid_norm(a, b, x, g, tm, name, target=None):
    n, k = a.shape
    d = b.shape[1]
    tm = min(tm, n)
    assert n % tm == 0 and x.shape == (n, d), (name, a.shape, b.shape)
    with_loss = target is not None

    def body(a_ref, b_ref, x_ref, g_ref, *rest):
        y = _dot(a_ref[...], b_ref[...])
        out = x_ref[...] + y * _rstd(y) * g_ref[...]
        if not with_loss:
            y_ref, o_ref = rest
            y_ref[...] = y
            o_ref[...] = out
            return
        t_ref, y_ref, d_ref, l_ref = rest
        y_ref[...] = y
        diff = out - t_ref[...]
        d_ref[...] = diff * (1.0 / d)

        @pl.when(pl.program_id(0) == 0)
        def _():
            l_ref[...] = jnp.zeros_like(l_ref)

        l_ref[...] += 0.5 * jnp.sum(jnp.mean(diff * diff, axis=-1, keepdims=True), axis=0, keepdims=True)

    row = pl.BlockSpec((tm, d), lambda i: (i, 0))
    ins = [pl.BlockSpec((tm, k), lambda i: (i, 0)), pl.BlockSpec((k, d), lambda i: (0, 0)), row, pl.BlockSpec((1, d), lambda i: (0, 0))]
    out = SDS((n, d), F32)
    if with_loss:
        return pl.pallas_call(body, name=name, grid=(n // tm,), in_specs=ins + [row], out_specs=[row, row, pl.BlockSpec((1, 1), lambda i: (0, 0))],
                              out_shape=[out, out, SDS((1, 1), F32)], compiler_params=_cp("arbitrary"))(a, b, x, g, target)
    return pl.pallas_call(body, name=name, grid=(n // tm,), in_specs=ins, out_specs=[row, row], out_shape=[out, out],
                          compiler_params=_cp("parallel"))(a, b, x, g)


def _dx_norm_bwd(a_parts, b, x, g, res, tm, name, after=()):
    after, after_specs = _unread(after)
    parts, d, p = b.shape
    n = x.shape[0]
    tm = min(tm, n)
    assert n % tm == 0 and len(a_parts) == parts and x.shape[1] == d, (name, x.shape, b.shape)

    def body(*refs):
        x_ref, g_ref, res_ref = refs[2 * parts:2 * parts + 3]
        dx_ref, dg_ref = refs[-2:]
        dh = _dot(refs[0][...], refs[parts][0], "nt")
        for s in range(1, parts):
            dh = dh + _dot(refs[s][...], refs[parts + s][0], "nt")
        dx, dg = _rms_bwd(x_ref[...], g_ref[...], dh)
        dx_ref[...] = dx + res_ref[...]

        @pl.when(pl.program_id(0) == 0)
        def _():
            dg_ref[...] = jnp.zeros_like(dg_ref)

        dg_ref[...] += dg

    a_specs = [pl.BlockSpec((tm, p), lambda i, cb=cb: (i, cb)) for _, cb in a_parts]
    b_specs = [pl.BlockSpec((1, d, p), lambda i, s=s: (s, 0, 0)) for s in range(parts)]
    row = pl.BlockSpec((tm, d), lambda i: (i, 0))
    vec = pl.BlockSpec((1, d), lambda i: (0, 0))
    return pl.pallas_call(
        body, name=name, grid=(n // tm,), in_specs=a_specs + b_specs + [row, vec, row] + after_specs,
        out_specs=[row, vec], out_shape=[SDS((n, d), F32), SDS((1, d), F32)],
        compiler_params=_cp("arbitrary"))(*[arr for arr, _ in a_parts], *([b] * parts), x, g, res, *after)


def _dw_by_owner(a, b, tn, first, into, tm, name):
    k, m = a.shape
    cnt = b.shape[1] // tn
    tm = min(tm, m)
    assert m % tm == 0 and b.shape[1] == cnt * tn and first + cnt <= 4, (name, a.shape, b.shape)

    def body(a_ref, b_ref, *rest):
        rest[-1][0] = _dot(a_ref[...], b_ref[...], "tn").astype(rest[-1].dtype)

    extra = [] if into is None else [into]
    return pl.pallas_call(
        body, name=name, grid=(m // tm, cnt),
        in_specs=[pl.BlockSpec((k, tm), lambda i, j: (0, i)), pl.BlockSpec((k, tn), lambda i, j: (0, j))] + [pl.BlockSpec(memory_space=pl.ANY)] * len(extra),
        out_specs=pl.BlockSpec((1, tm, tn), lambda i, j: (first + j, i, 0)), out_shape=SDS((4, m, tn), WIRE_DTYPE),
        input_output_aliases={2: 0} if extra else {},
        compiler_params=_cp("parallel", "parallel"))(a, b, *extra)


def _norm_mm(x, g, w, out_dtype, tm, tn, name, after=()):
    after, after_specs = _unread(after)
    m, d = x.shape
    sharded = w.ndim == 3
    n = w.shape[-1] * (w.shape[0] if sharded else 1)
    tm, tn = min(tm, m), (w.shape[-1] if sharded else min(tn, n))
    assert m % tm == 0 and n % tn == 0, (name, m, n, tm, tn)

    def body(x_ref, g_ref, w_ref, *rest):
        o_ref, h_ref, hs = rest[-3:]

        @pl.when(pl.program_id(1) == 0)
        def _():
            xv = x_ref[...]
            h = (xv * _rstd(xv) * g_ref[...]).astype(MXU_DTYPE)
            hs[...] = h
            h_ref[...] = h

        o_ref[...] = _dot(hs[...], w_ref[0] if sharded else w_ref[...]).astype(o_ref.dtype)

    w_spec = pl.BlockSpec((1, d, tn), lambda i, j: (j, 0, 0)) if sharded else pl.BlockSpec((d, tn), lambda i, j: (0, j))
    return pl.pallas_call(
        body, name=name, grid=(m // tm, n // tn),
        in_specs=[pl.BlockSpec((tm, d), lambda i, j: (i, 0)), pl.BlockSpec((1, d), lambda i, j: (0, 0)), w_spec] + after_specs,
        out_specs=[pl.BlockSpec((tm, tn), lambda i, j: (i, j)), pl.BlockSpec((tm, d), lambda i, j: (i, 0))],
        out_shape=[SDS((m, n), out_dtype), SDS((m, d), MXU_DTYPE)],
        scratch_shapes=[pltpu.VMEM((tm, d), MXU_DTYPE)],
        compiler_params=_cp("parallel", "arbitrary"))(x, g, w, *after)


ROW_TILE = 256
TOKEN_TILE = 1024


def _norm_bwd(x, g, dy, res, out_dtype, name):
    n, d = x.shape
    tr = min(ROW_TILE, n)
    has_res = res is not None

    def body(*refs):
        x_ref, g_ref, dy_ref = refs[:3]
        dx_ref, dg_ref = refs[-2:]
        dx, dg = _rms_bwd(x_ref[...], g_ref[...], dy_ref[...].astype(F32))
        if has_res:
            dx = dx + refs[3][...]
        dx_ref[...] = dx.astype(dx_ref.dtype)

        @pl.when(pl.program_id(0) == 0)
        def _():
            dg_ref[...] = jnp.zeros_like(dg_ref)

        dg_ref[...] += dg

    row = pl.BlockSpec((tr, d), lambda i: (i, 0))
    vec = pl.BlockSpec((1, d), lambda i: (0, 0))
    ins = [x, g, dy] + ([res] if has_res else [])
    return pl.pallas_call(
        body, name=name, grid=(n // tr,), in_specs=[row, vec, row] + ([row] if has_res else []),
        out_specs=[row, vec], out_shape=[SDS((n, d), out_dtype), SDS((1, d), F32)],
        compiler_params=_cp("arbitrary"))(*ins)


def _rope_tables(n):
    pairs = ATT_HEAD_DIM // 4
    t = np.arange(n)
    inv = np.power(ROPE_THETA, -np.arange(pairs, dtype=np.float32) / pairs).astype(np.float32)
    ang = np.concatenate([(t // GRID_W)[:, None].astype(np.float32) * inv, (t % GRID_W)[:, None].astype(np.float32) * inv], axis=-1)
    cos = np.repeat(np.cos(ang), 2, axis=-1)
    sin = np.repeat(np.sin(ang), 2, axis=-1) * np.tile(np.array([-1.0, 1.0], np.float32), ATT_HEAD_DIM // 2)
    return jnp.asarray(np.tile(cos, 2), F32), jnp.asarray(np.tile(sin, 2), F32)


def _swap_pairs(x):
    lane = lax.broadcasted_iota(jnp.int32, x.shape, 1)
    return jnp.where((lane & 1) == 0, pltpu.roll(x, 127, axis=1), pltpu.roll(x, 1, axis=1))


def _head_mean(v):
    lane = lax.broadcasted_iota(jnp.int32, v.shape, 1)
    lo = jnp.where(lane < ATT_HEAD_DIM, v, 0.0)
    s0 = jnp.sum(lo, axis=-1, keepdims=True)
    s1 = jnp.sum(v - lo, axis=-1, keepdims=True)
    return jnp.where(lane < ATT_HEAD_DIM, s0, s1) * (1.0 / ATT_HEAD_DIM)


def _qk_prep(p, gq, gk, cos, sin, name):
    n = p.shape[0]
    tr = min(ROW_TILE, n)

    def one(xv, g, c, s):
        xn = xv * lax.rsqrt(_head_mean(xv * xv) + EPS) * g
        return xn * c + _swap_pairs(xn) * s

    def body(q_ref, k_ref, gq_ref, gk_ref, c_ref, s_ref, qo_ref, ko_ref):
        c, s = c_ref[...], s_ref[...]
        for j in range(ATT_Q_DIM // 128):
            qo_ref[:, j * 128:(j + 1) * 128] = one(q_ref[:, j * 128:(j + 1) * 128], gq_ref[...], c, s).astype(qo_ref.dtype)
        ko_ref[...] = one(k_ref[...], gk_ref[...], c, s).astype(ko_ref.dtype)

    vec = pl.BlockSpec((1, 128), lambda i: (0, 0))
    tab = pl.BlockSpec((tr, 128), lambda i: (i, 0))
    return pl.pallas_call(
        body, name=name, grid=(n // tr,),
        in_specs=[pl.BlockSpec((tr, ATT_Q_DIM), lambda i: (i, 0)), pl.BlockSpec((tr, 128), lambda i: (i, OFF_AK // 128)), vec, vec, tab, tab],
        out_specs=[pl.BlockSpec((tr, ATT_Q_DIM), lambda i: (i, 0)), tab],
        out_shape=[SDS((n, ATT_Q_DIM), MXU_DTYPE), SDS((n, ATT_KV_DIM), MXU_DTYPE)],
        compiler_params=_cp("parallel"))(p, p, gq, gk, cos, sin)


def _qk_prep_bwd(p, gq, gk, cos, sin, dq, dk, name):
    n = p.shape[0]
    tr = min(ROW_TILE, n)

    def one(xv, g, c, s, dout):
        dxn = dout * c + _swap_pairs(dout * s)
        r = lax.rsqrt(_head_mean(xv * xv) + EPS)
        xh = xv * r
        dn = dxn * g
        dx = r * (dn - xh * _head_mean(dn * xh))
        return dx, jnp.sum(dxn * xh, axis=0, keepdims=True)

    def body(q_ref, k_ref, gq_ref, gk_ref, c_ref, s_ref, dq_ref, dk_ref, dqo_ref, dko_ref, dgq_ref, dgk_ref):
        @pl.when(pl.program_id(0) == 0)
        def _():
            dgq_ref[...] = jnp.zeros_like(dgq_ref)
            dgk_ref[...] = jnp.zeros_like(dgk_ref)

        c, s = c_ref[...], s_ref[...]
        for j in range(ATT_Q_DIM // 128):
            sl = slice(j * 128, (j + 1) * 128)
            dx, dg = one(q_ref[:, sl], gq_ref[...], c, s, dq_ref[:, sl])
            dqo_ref[:, sl] = dx.astype(dqo_ref.dtype)
            dgq_ref[:, sl] += dg
        dx, dg = one(k_ref[...], gk_ref[...], c, s, dk_ref[...])
        dko_ref[...] = dx.astype(dko_ref.dtype)
        dgk_ref[...] += dg

    vec = pl.BlockSpec((1, 128), lambda i: (0, 0))
    tab = pl.BlockSpec((tr, 128), lambda i: (i, 0))
    qrow = pl.BlockSpec((tr, ATT_Q_DIM), lambda i: (i, 0))
    return pl.pallas_call(
        body, name=name, grid=(n // tr,),
        in_specs=[qrow, pl.BlockSpec((tr, 128), lambda i: (i, OFF_AK // 128)), vec, vec, tab, tab, qrow, tab],
        out_specs=[qrow, tab, pl.BlockSpec((1, ATT_Q_DIM), lambda i: (0, 0)), vec],
        out_shape=[SDS((n, ATT_Q_DIM), MXU_DTYPE), SDS((n, ATT_KV_DIM), MXU_DTYPE), SDS((1, ATT_Q_DIM), F32), SDS((1, 128), F32)],
        compiler_params=_cp("arbitrary"))(p, p, gq, gk, cos, sin, dq, dk)


ATT_TQ = 256


def _attn_fwd(q, k, v, name):
    n = q.shape[0]
    tq = min(ATT_TQ, n)
    scale = ATT_HEAD_DIM ** -0.5
    gw = ATT_GROUP * ATT_HEAD_DIM

    def body(q_ref, k_ref, v_ref, o_ref):
        kk, vv = k_ref[0], v_ref[0]
        v_ones = jnp.concatenate([vv, jnp.ones_like(vv)], axis=1)
        outs = []
        for g in range(ATT_GROUP):
            s = _dot(q_ref[:, g * ATT_HEAD_DIM:(g + 1) * ATT_HEAD_DIM] * scale, kk, "nt")
            e = jnp.exp(s - jnp.max(s, axis=-1, keepdims=True))
            ov = _dot(e, v_ones)
            outs.append(ov[:, :ATT_HEAD_DIM] / ov[:, ATT_HEAD_DIM:])
        o_ref[...] = jnp.concatenate(outs, axis=-1).astype(o_ref.dtype)

    kv = pl.BlockSpec((1, n, ATT_HEAD_DIM), lambda h, i: (h, 0, 0))
    return pl.pallas_call(
        body, name=name, grid=(ATT_KV_HEADS, n // tq),
        in_specs=[pl.BlockSpec((tq, gw), lambda h, i: (i, h)), kv, kv],
        out_specs=pl.BlockSpec((tq, gw), lambda h, i: (i, h)), out_shape=SDS((n, ATT_Q_DIM), MXU_DTYPE),
        compiler_params=_cp("parallel", "parallel"))(q, k, v)


def _attn_bwd(q, k, v, o, do, name):
    n = q.shape[0]
    tq = min(ATT_TQ, n)
    scale = ATT_HEAD_DIM ** -0.5
    gw = ATT_GROUP * ATT_HEAD_DIM

    def body(q_ref, k_ref, v_ref, o_ref, do_ref, dq_ref, dk_ref, dv_ref):
        @pl.when(pl.program_id(1) == 0)
        def _():
            dk_ref[...] = jnp.zeros_like(dk_ref)
            dv_ref[...] = jnp.zeros_like(dv_ref)

        kk, vv = k_ref[0], v_ref[0]
        dqs = []
        dk_acc = jnp.zeros((ATT_HEAD_DIM, n), F32)
        dv_acc = jnp.zeros((ATT_HEAD_DIM, n), F32)
        for g in range(ATT_GROUP):
            sl = slice(g * ATT_HEAD_DIM, (g + 1) * ATT_HEAD_DIM)
            qg, dog = q_ref[:, sl] * scale, do_ref[:, sl].astype(F32)
            s = _dot(qg, kk, "nt")
            e = jnp.exp(s - jnp.max(s, axis=-1, keepdims=True))
            inv = 1.0 / jnp.sum(e, axis=-1, keepdims=True)
            delta = jnp.sum(dog * o_ref[:, sl].astype(F32), axis=-1, keepdims=True)
            dse = e * (_dot(dog, vv, "nt") - delta)
            dqs.append(_dot(dse, kk) * (inv * scale))
            dk_acc += _dot(qg.astype(F32) * inv, dse, "tn")
            dv_acc += _dot(dog * inv, e, "tn")
        dq_ref[...] = jnp.concatenate(dqs, axis=-1)
        dk_ref[0] += dk_acc
        dv_ref[0] += dv_acc

    kv = pl.BlockSpec((1, n, ATT_HEAD_DIM), lambda h, i: (h, 0, 0))
    kvt = pl.BlockSpec((1, ATT_HEAD_DIM, n), lambda h, i: (h, 0, 0))
    qb = pl.BlockSpec((tq, gw), lambda h, i: (i, h))
    return pl.pallas_call(
        body, name=name, grid=(ATT_KV_HEADS, n // tq), in_specs=[qb, kv, kv, qb, qb], out_specs=[qb, kvt, kvt],
        out_shape=[SDS((n, ATT_Q_DIM), F32), SDS((ATT_KV_HEADS, ATT_HEAD_DIM, n), F32), SDS((ATT_KV_HEADS, ATT_HEAD_DIM, n), F32)],
        compiler_params=_cp("parallel", "arbitrary"))(q, k, v, o, do)


def _both_directions(mats, axis):
    fwd = np.concatenate(mats, axis=axis).astype(np.float32)
    bwd = np.concatenate([m[::-1, ::-1] for m in mats], axis=axis).astype(np.float32)
    return jnp.asarray(np.stack([fwd, bwd]), MXU_DTYPE)


def _hg_segments():
    c = HG_CHUNK
    t = np.arange(c)[:, None]
    r = np.arange(c)[None, :]
    mats = [(r <= t)]
    for lev in range(HG_LEVELS):
        h = c >> (lev + 1)
        mid = (t // (2 * h)) * (2 * h) + h - 1
        hi = (t // h) % 2 == 1
        mats.append(np.where(hi, (r > mid) & (r <= t), (r > t) & (r <= mid)))
    mats.append(r > t)
    return _both_directions(mats, 0)


def _hg_pair_sums():
    c = HG_CHUNK
    r = np.arange(c)[:, None]
    t = np.arange(c)[None, :]
    gp, gn = [t >= r], [t < r]
    for lev in range(HG_LEVELS):
        sh = HG_LEVELS - 1 - lev
        same = (r >> sh) == (t >> sh)
        gp.append(same & (t >= r))
        gn.append(same & (t < r))
    return _both_directions(gp, 1), _both_directions(gn, 1)


def _split_dot(mat, x):
    hi = x.astype(MXU_DTYPE)
    lo = (x - hi.astype(F32)).astype(MXU_DTYPE)
    return _dot(mat, hi) + _dot(mat, lo)


def _hg_gates(hq, z, a0, a1):
    q = hq * _sigmoid(hq)
    sg = _sigmoid(z)
    lb = _sigmoid(a0 - a1)
    f = lb + (1.0 - lb) * sg
    k = (1.0 - lb) * (1.0 - sg)
    return q, f, k, sg, lb


def _hg_level_masks():
    c = HG_CHUNK
    t = np.arange(c)
    later, same = [], []
    for lev in range(HG_LEVELS):
        sh = HG_LEVELS - 1 - lev
        later.append(np.broadcast_to((((t >> sh) & 1) == 1)[:, None], (c, HG_HEAD_DIM)))
        same.append((t[:, None] >> (sh + 1)) == (t[None, :] >> (sh + 1)))
    same.append(t[:, None] == t[None, :])
    later = np.stack(later).astype(np.float32)
    return jnp.asarray(np.stack([later, 1.0 - later]), F32), jnp.asarray(np.stack(same).astype(np.float32), F32)


def _hg_level(q, k, ex, later_ref, lev):
    e = ex[lev + 1]
    e_q = e * later_ref[0, lev]
    e_k = e - e_q
    return q * e_q, k * e_k, e_q, e_k


def _hg_intra(q, k, ex, later_ref, same_ref):
    a = same_ref[HG_LEVELS] * jnp.sum(q * k, axis=-1, keepdims=True)
    for lev in range(HG_LEVELS):
        qs, ks, _, _ = _hg_level(q, k, ex, later_ref, lev)
        a = a + same_ref[lev] * _dot(qs, ks, "nt")
    return a


def _hg_specs(n, with_time):
    c = HG_CHUNK
    nc = n // c

    def chunk(d, i):
        first = d if with_time else 1 - d
        return i + first * (nc - 1 - 2 * i)

    def pcols(off, dir_stride=0):
        return [pl.BlockSpec((c, HG_PAIR), lambda d, i, j=j: (chunk(d, i), off // HG_PAIR + dir_stride // HG_PAIR * d + j)) for j in range(2)]

    specs = dict(
        hq=pcols(OFF_HQ), v=pcols(OFF_HI), z=pcols(OFF_ZF, OFF_ZB - OFF_ZF),
        shared=pl.BlockSpec((c, HG_DIM), lambda d, i: (chunk(d, i), 0)),
        per_dir=pl.BlockSpec((1, c, HG_DIM), lambda d, i: (d, chunk(d, i), 0)),
        vec=pl.BlockSpec((1, 1, HG_DIM), lambda d, i: (d, 0, 0)),
        seg=pl.BlockSpec((1, (HG_LEVELS + 2) * c, c), lambda d, i: (d, 0, 0)),
        sums=pl.BlockSpec((1, c, (HG_LEVELS + 1) * c), lambda d, i: (d, 0, 0)),
        later=pl.BlockSpec((1, HG_LEVELS, c, HG_HEAD_DIM), lambda d, i: (d, 0, 0, 0)),
        same=pl.BlockSpec((HG_LEVELS + 1, c, c), lambda d, i: (0, 0, 0)),
        state=pl.BlockSpec((1, HG_HEADS, 1, HG_HEAD_DIM, HG_HEAD_DIM), lambda d, i: (d, 0, chunk(d, i), 0, 0)))
    return nc, specs


def _hg_head(refs, hh):
    off = (hh % 2) * HG_HEAD_DIM
    return refs[hh // 2][:, off:off + HG_HEAD_DIM]


def _hg_lanes(hh):
    return slice(hh * HG_HEAD_DIM, (hh + 1) * HG_HEAD_DIM)


def _hg_exps(seg_ref, f):
    lf = jnp.log(f)
    args = _split_dot(seg_ref[0], lf)
    c = HG_CHUNK
    return [jnp.exp(args[j * c:(j + 1) * c]) for j in range(HG_LEVELS + 2)]


def _hg_last_row(a, mirrored):
    return jnp.where(mirrored, a[0:1, :], a[HG_CHUNK - 1:HG_CHUNK, :])


def _hgrn_fwd(p, a0, a1, seg, masks, name):
    n = p.shape[0]
    nc, sp = _hg_specs(n, True)

    def body(hq0, hq1, z0, z1, v0, v1, a0_ref, a1_ref, seg_ref, later_ref, same_ref, o_ref, s0_ref, st):
        @pl.when(pl.program_id(1) == 0)
        def _():
            st[...] = jnp.zeros_like(st)

        mirrored = pl.program_id(0) == 1
        for hh in range(HG_HEADS):
            ln = _hg_lanes(hh)
            q, f, k, _, _ = _hg_gates(_hg_head((hq0, hq1), hh), _hg_head((z0, z1), hh), a0_ref[0, :, ln], a1_ref[0, :, ln])
            vv = _hg_head((v0, v1), hh)
            ex = _hg_exps(seg_ref, f)
            a = _hg_intra(q, k, ex, later_ref, same_ref)
            s_t = st[hh]
            s0_ref[0, hh, 0] = s_t
            o_ref[0, :, ln] = _dot(a, vv) + _dot(q * ex[0], s_t, "nt")
            st[hh] = s_t * _hg_last_row(ex[0], mirrored) + _dot(vv, k * ex[HG_LEVELS + 1], "tn")

    return pl.pallas_call(
        body, name=name, grid=(2, nc), in_specs=sp["hq"] + sp["z"] + sp["v"] + [sp["vec"], sp["vec"], sp["seg"], sp["later"], sp["same"]],
        out_specs=[sp["per_dir"], sp["state"]],
        out_shape=[SDS((2, n, HG_DIM), F32), SDS((2, HG_HEADS, nc, HG_HEAD_DIM, HG_HEAD_DIM), F32)],
        scratch_shapes=[pltpu.VMEM((HG_HEADS, HG_HEAD_DIM, HG_HEAD_DIM), F32)],
        compiler_params=_cp("parallel", "arbitrary"))(p, p, p, p, p, p, a0, a1, seg, *masks)


def _hgrn_bwd(p, a0, a1, seg, masks, gp, gn, do, s0, name):
    n = p.shape[0]
    nc, sp = _hg_specs(n, False)


    def body(hq0, hq1, z0, z1, v0, v1, a0_ref, a1_ref, seg_ref, later_ref, same_ref, gp_ref, gn_ref, do_ref, s0_ref,
             dhq_ref, dz_ref, dv_ref, dlb_ref, rt):
        @pl.when(pl.program_id(1) == 0)
        def _():
            rt[...] = jnp.zeros_like(rt)
            dlb_ref[...] = jnp.zeros_like(dlb_ref)

        mirrored = pl.program_id(0) == 1
        for hh in range(HG_HEADS):
            ln = _hg_lanes(hh)
            hqv = _hg_head((hq0, hq1), hh)
            q, f, k, sg, lb = _hg_gates(hqv, _hg_head((z0, z1), hh), a0_ref[0, :, ln], a1_ref[0, :, ln])
            vv, dov = _hg_head((v0, v1), hh), do_ref[:, ln]
            ex = _hg_exps(seg_ref, f)
            a = _hg_intra(q, k, ex, later_ref, same_ref)
            da = _dot(dov, vv, "nt")
            diag = jnp.sum(dov * vv, axis=-1, keepdims=True)
            s_t = s0_ref[0, hh, 0]
            r_t = rt[hh]
            k_end = k * ex[HG_LEVELS + 1]
            dv_ref[0, :, ln] = _dot(a, dov, "tn") + _dot(k_end, r_t, "nt")
            dq_inter = ex[0] * _dot(dov, s_t)
            dk_inter = ex[HG_LEVELS + 1] * _dot(vv, r_t)
            dq = diag * k + dq_inter
            dk = diag * q + dk_inter
            q_terms, k_terms = [q * dq_inter], [k * dk_inter]
            for lev in range(HG_LEVELS):
                qs, ks, e_q, e_k = _hg_level(q, k, ex, later_ref, lev)
                pairs = da * same_ref[lev]
                q_part = e_q * _dot(pairs, ks)
                k_part = e_k * _dot(pairs, qs, "tn")
                dq, dk = dq + q_part, dk + k_part
                q_terms.append(q * q_part)
                k_terms.append(k * k_part)
            decay = _hg_last_row(ex[0], mirrored)
            rt[hh] = r_t * decay + _dot(dov, q * ex[0], "tn")
            later = decay * jnp.sum(s_t * r_t, axis=0, keepdims=True)
            dlf = _dot(gp_ref[0], jnp.concatenate(q_terms, axis=0)) + _dot(gn_ref[0], jnp.concatenate(k_terms, axis=0)) + later
            df = dlf / f - dk
            dz_ref[0, :, ln] = df * (1.0 - lb) * sg * (1.0 - sg)
            dlb_ref[0, :, ln] += jnp.sum(df * (1.0 - sg), axis=0, keepdims=True)
            sq = _sigmoid(hqv)
            dhq_ref[0, :, ln] = dq * sq * (1.0 + hqv * (1.0 - sq))

    out = SDS((2, n, HG_DIM), F32)
    return pl.pallas_call(
        body, name=name, grid=(2, nc),
        in_specs=sp["hq"] + sp["z"] + sp["v"] + [sp["vec"], sp["vec"], sp["seg"], sp["later"], sp["same"], sp["sums"], sp["sums"],
                                                 sp["shared"], sp["state"]],
        out_specs=[sp["per_dir"], sp["per_dir"], sp["per_dir"], sp["vec"]], out_shape=[out, out, out, SDS((2, 1, HG_DIM), F32)],
        scratch_shapes=[pltpu.VMEM((HG_HEADS, HG_HEAD_DIM, HG_HEAD_DIM), F32)],
        compiler_params=_cp("parallel", "arbitrary"))(p, p, p, p, p, p, a0, a1, seg, *masks, gp, gn, do, s0)


def _hg_post(o2, p, g, name):
    n = p.shape[0]
    tr = min(ROW_TILE, n)
    w = 2 * HG_HEAD_DIM

    def body(of_ref, ob_ref, hg_ref, g_ref, o_ref):
        for j in range(2):
            sl = slice(j * HG_HEAD_DIM, (j + 1) * HG_HEAD_DIM)
            o = of_ref[0, :, sl] + ob_ref[0, :, sl]
            hg = hg_ref[:, sl]
            o_ref[:, sl] = (o * _rstd(o) * g_ref[...] * (hg * _sigmoid(hg))).astype(o_ref.dtype)

    blk = pl.BlockSpec((tr, w), lambda i, j: (i, j))
    dirs = [pl.BlockSpec((1, tr, w), lambda i, j, d=d: (d, i, j)) for d in range(2)]
    return pl.pallas_call(
        body, name=name, grid=(n // tr, HG_DIM // w),
        in_specs=dirs + [pl.BlockSpec((tr, w), lambda i, j: (i, OFF_HG // w + j)), pl.BlockSpec((1, HG_HEAD_DIM), lambda i, j: (0, 0))],
        out_specs=blk, out_shape=SDS((n, HG_DIM), MXU_DTYPE), compiler_params=_cp("parallel", "parallel"))(o2, o2, p, g)


def _hg_post_bwd(o2, p, g, dcat, name, after=()):
    n = p.shape[0]
    tr = min(ROW_TILE, n)
    w = 2 * HG_HEAD_DIM
    after, after_specs = _unread(after)

    def body(of_ref, ob_ref, hg_ref, g_ref, d_ref, *rest):
        do_ref, dhg_ref, dg_ref = rest[len(after):]

        @pl.when(pl.program_id(1) == 0)
        def _():
            dg_ref[...] = jnp.zeros_like(dg_ref)

        for j in range(2):
            sl = slice(j * HG_HEAD_DIM, (j + 1) * HG_HEAD_DIM)
            o = of_ref[0, :, sl] + ob_ref[0, :, sl]
            hg = hg_ref[:, sl]
            d = d_ref[:, sl].astype(F32)
            sg = _sigmoid(hg)
            on = o * _rstd(o) * g_ref[...]
            dhg_ref[:, sl] = (d * on * sg * (1.0 + hg * (1.0 - sg))).astype(dhg_ref.dtype)
            dx, dg = _rms_bwd(o, g_ref[...], d * hg * sg)
            do_ref[:, sl] = dx
            dg_ref[0, :, sl] += dg

    blk = pl.BlockSpec((tr, w), lambda j, i: (i, j))
    dirs = [pl.BlockSpec((1, tr, w), lambda j, i, d=d: (d, i, j)) for d in range(2)]
    return pl.pallas_call(
        body, name=name, grid=(HG_DIM // w, n // tr),
        in_specs=dirs + [pl.BlockSpec((tr, w), lambda j, i: (i, OFF_HG // w + j)), pl.BlockSpec((1, HG_HEAD_DIM), lambda j, i: (0, 0)),
                         pl.BlockSpec((tr, w), lambda j, i: (i, ATT_Q_DIM // w + j))] + after_specs,
        out_specs=[blk, blk, pl.BlockSpec((1, 1, w), lambda j, i: (j, 0, 0))],
        out_shape=[SDS((n, HG_DIM), F32), SDS((n, HG_DIM), MXU_DTYPE), SDS((HG_DIM // w, 1, w), F32)],
        compiler_params=_cp("parallel", "arbitrary"))(o2, o2, p, g, dcat, *after)


XATT_TQ = 512


def _xattn_fwd(q, kv, name):
    n, nm = q.shape[0], kv.shape[0]
    tq = min(XATT_TQ, n)
    scale = X_HEAD_DIM ** -0.5

    def body(q_ref, k_ref, v_ref, o_ref):
        s = _dot(q_ref[...], k_ref[...], "nt") * scale
        e = jnp.exp(s - jnp.max(s, axis=-1, keepdims=True))
        o_ref[...] = _dot(e / jnp.sum(e, axis=-1, keepdims=True), v_ref[...]).astype(o_ref.dtype)

    qb = pl.BlockSpec((tq, X_HEAD_DIM), lambda h, i: (i, h))
    return pl.pallas_call(
        body, name=name, grid=(X_HEADS, n // tq),
        in_specs=[qb, pl.BlockSpec((nm, X_HEAD_DIM), lambda h, i: (0, h)), pl.BlockSpec((nm, X_HEAD_DIM), lambda h, i: (0, X_HEADS + h))],
        out_specs=qb, out_shape=SDS(q.shape, MXU_DTYPE), compiler_params=_cp("parallel", "parallel"))(q, kv, kv)


def _xattn_bwd(q, kv, do, name, after=()):
    n, nm = q.shape[0], kv.shape[0]
    tq = min(XATT_TQ, n)
    scale = X_HEAD_DIM ** -0.5
    after, after_specs = _unread(after)

    def body(q_ref, k_ref, v_ref, do_ref, *rest):
        dq_ref, dk_ref, dv_ref = rest[len(after):]

        @pl.when(pl.program_id(1) == 0)
        def _():
            dk_ref[...] = jnp.zeros_like(dk_ref)
            dv_ref[...] = jnp.zeros_like(dv_ref)

        qv, dov = q_ref[...], do_ref[...]
        s = _dot(qv, k_ref[...], "nt") * scale
        e = jnp.exp(s - jnp.max(s, axis=-1, keepdims=True))
        p = e / jnp.sum(e, axis=-1, keepdims=True)
        dp = _dot(dov, v_ref[...], "nt")
        ds = p * (dp - jnp.sum(p * dp, axis=-1, keepdims=True)) * scale
        dq_ref[...] = _dot(ds, k_ref[...]).astype(dq_ref.dtype)
        dk_ref[...] += _dot(ds, qv, "tn")
        dv_ref[...] += _dot(p, dov, "tn")

    qb = pl.BlockSpec((tq, X_HEAD_DIM), lambda h, i: (i, h))
    kb = pl.BlockSpec((nm, X_HEAD_DIM), lambda h, i: (0, h))
    return pl.pallas_call(
        body, name=name, grid=(X_HEADS, n // tq),
        in_specs=[qb, kb, pl.BlockSpec((nm, X_HEAD_DIM), lambda h, i: (0, X_HEADS + h)), qb] + after_specs, out_specs=[qb, kb, kb],
        out_shape=[SDS(q.shape, MXU_DTYPE), SDS((nm, X_HEADS * X_HEAD_DIM), F32), SDS((nm, X_HEADS * X_HEAD_DIM), F32)],
        compiler_params=_cp("parallel", "arbitrary"))(q, kv, kv, do, *after)


def _edge_rows(shape):
    row = lax.broadcasted_iota(jnp.int32, shape, 0)
    return row == 0, row == shape[0] - 1


def _shift_rows(u, down, edges):
    if down:
        return jnp.where(edges[0], 0.0, pltpu.roll(u, 1, axis=0))
    return jnp.where(edges[1], 0.0, pltpu.roll(u, u.shape[0] - 1, axis=0))


def _conv(u, w, b, edges):
    return b + _shift_rows(u, True, edges) * w[0:1, :] + u * w[1:2, :] + _shift_rows(u, False, edges) * w[2:3, :]


def _ff_specs(n):
    gate = lambda rows: pl.BlockSpec((rows, FF_COLS), lambda j: (0, j))
    val = lambda rows: pl.BlockSpec((rows, FF_COLS), lambda j: (0, FF_BLOCKS + j))
    return [gate(n), val(n), gate(3), val(3), gate(1), val(1)], gate


def _conv_gate(u, cw, cb, name):
    n = u.shape[0]
    ins, gate_blk = _ff_specs(n)

    def body(ug_ref, uv_ref, wg_ref, wv_ref, bg_ref, bv_ref, o_ref):
        edges = _edge_rows(ug_ref.shape)
        gate = _conv(ug_ref[...], wg_ref[...], bg_ref[...], edges)
        val = _conv(uv_ref[...], wv_ref[...], bv_ref[...], edges)
        o_ref[...] = (gate * _sigmoid(gate) * val).astype(o_ref.dtype)

    return pl.pallas_call(
        body, name=name, grid=(FF_BLOCKS,), in_specs=ins, out_specs=gate_blk(n), out_shape=SDS((n, D_FF), MXU_DTYPE),
        compiler_params=_cp("parallel"))(u, u, cw, cw, cb, cb)


def _conv_gate_bwd(u, cw, cb, da, name, after=()):
    n = u.shape[0]
    ins, gate_blk = _ff_specs(n)
    after, after_specs = _unread(after)

    def side(dacc, u, w, edges, du_ref, dw_ref, db_ref):
        nxt, prv = _shift_rows(dacc, False, edges), _shift_rows(dacc, True, edges)
        du_ref[...] = (nxt * w[0:1, :] + dacc * w[1:2, :] + prv * w[2:3, :]).astype(du_ref.dtype)
        db_ref[...] = jnp.sum(dacc, axis=0, keepdims=True)
        dw_ref[0:1, :] = jnp.sum(nxt * u, axis=0, keepdims=True)
        dw_ref[1:2, :] = jnp.sum(dacc * u, axis=0, keepdims=True)
        dw_ref[2:3, :] = jnp.sum(prv * u, axis=0, keepdims=True)

    def body(ug_ref, uv_ref, wg_ref, wv_ref, bg_ref, bv_ref, da_ref, *rest):
        dug_ref, duv_ref, dwg_ref, dwv_ref, dbg_ref, dbv_ref = rest[len(after):]
        ug, uv = ug_ref[...], uv_ref[...]
        edges = _edge_rows(ug.shape)
        gate = _conv(ug, wg_ref[...], bg_ref[...], edges)
        val = _conv(uv, wv_ref[...], bv_ref[...], edges)
        sg = _sigmoid(gate)
        dav = da_ref[...].astype(F32)
        side(dav * val * sg * (1.0 + gate * (1.0 - sg)), ug, wg_ref[...], edges, dug_ref, dwg_ref, dbg_ref)
        side(dav * gate * sg, uv, wv_ref[...], edges, duv_ref, dwv_ref, dbv_ref)

    return pl.pallas_call(
        body, name=name, grid=(FF_BLOCKS,), in_specs=ins + [gate_blk(n)] + after_specs,
        out_specs=[gate_blk(n), gate_blk(n), gate_blk(3), gate_blk(3), gate_blk(1), gate_blk(1)],
        out_shape=[SDS((n, D_FF), MXU_DTYPE)] * 2 + [SDS((3, D_FF), F32)] * 2 + [SDS((1, D_FF), F32)] * 2,
        compiler_params=_cp("parallel"))(u, u, cw, cw, cb, cb, da, *after)


def _adamw(w, g, m, v, name):
    r, c = w.shape[-2:]
    tr = r if r <= 512 else 256 if r % 256 == 0 else 88
    assert r % tr == 0 and (w.ndim == 2 or w.shape[:-2] == (1,)), (name, w.shape, tr)

    def body(w_ref, g_ref, m_ref, v_ref, d_ref, mo_ref, vo_ref, go_ref):
        gv = g_ref[...]
        go_ref[...] = gv
        mn = ADAM_B1 * m_ref[...] + (1.0 - ADAM_B1) * gv
        vn = ADAM_B2 * v_ref[...] + (1.0 - ADAM_B2) * gv * gv
        m_hat = mn / (1.0 - ADAM_B1 ** ADAM_STEP)
        v_hat = vn / (1.0 - ADAM_B2 ** ADAM_STEP)
        d_ref[...] = -ADAM_LR * (m_hat / (jnp.sqrt(v_hat) + ADAM_EPS) + ADAM_WD * w_ref[...])
        mo_ref[...] = mn
        vo_ref[...] = vn

    blk = pl.BlockSpec((tr, c), lambda i: (i, 0)) if w.ndim == 2 else pl.BlockSpec((1, tr, c), lambda i: (0, i, 0))
    out = SDS(w.shape, F32)
    return pl.pallas_call(body, name=name, grid=(r // tr,), in_specs=[blk] * 4, out_specs=[blk] * 4, out_shape=[out] * 4,
                          compiler_params=_cp("parallel"))(w, g, m, v)


def _half_tile(h):
    tr = h if h <= 512 else 256 if h % 256 == 0 else 176
    assert h % tr == 0, (h, tr)
    return tr


ANY = pl.BlockSpec(memory_space=pl.ANY)


def _place():
    x, y, c = lax.axis_index("x"), lax.axis_index("y"), lax.axis_index("c")
    return x, y, c, [(1 - x, y), (x, 1 - y), (1 - x, 1 - y)]


def _gather_shards(shards, name):
    nt = len(shards)

    def body(*refs):
        ins, outs = refs[:nt], refs[nt:2 * nt]
        send, recv, fsend, frecv, osend, orecv = refs[2 * nt:]
        x, y, c, chips = _place()
        me = 2 * x + y

        def half(t, chip, cc):
            h = ins[t].shape[0] // 2
            return outs[t].at[chip, pl.ds(cc * h, h)]

        def ici(t, j):
            cx, cy = chips[j]
            h = ins[t].shape[0] // 2
            return pltpu.make_async_remote_copy(src_ref=ins[t].at[pl.ds(c * h, h)], dst_ref=half(t, me, c),
                                                send_sem=send.at[t, j], recv_sem=recv.at[t, j], device_id=(cx, cy, c), device_id_type=MESH)

        def landed(t, j):
            cx, cy = chips[j]
            blk = half(t, 2 * cx + cy, c)
            return pltpu.make_async_remote_copy(src_ref=blk, dst_ref=blk, send_sem=send.at[t, j], recv_sem=recv.at[t, j],
                                                device_id=(cx, cy, c), device_id_type=MESH)

        def d2d(t, j, cc):
            cx, cy = chips[j]
            blk = half(t, 2 * cx + cy, cc)
            return pltpu.make_async_remote_copy(src_ref=blk, dst_ref=blk, send_sem=fsend.at[t, j], recv_sem=frecv.at[t, j],
                                                device_id=(x, y, 1 - c), device_id_type=MESH)

        own = [pltpu.make_async_remote_copy(src_ref=ins[t], dst_ref=outs[t].at[me], send_sem=osend.at[t], recv_sem=orecv.at[t],
                                            device_id=(x, y, 1 - c), device_id_type=MESH) for t in range(nt)]
        for t in range(nt):
            for j in range(3):
                ici(t, j).start()
        for cp in own:
            cp.start()
        for t in range(nt):
            for j in range(3):
                landed(t, j).wait_recv()
                d2d(t, j, c).start()
        for t in range(nt):
            for j in range(3):
                d2d(t, j, 1 - c).wait_recv()
        for t in range(nt):
            for j in range(3):
                ici(t, j).wait_send()
                d2d(t, j, c).wait_send()
        for cp in own:
            cp.wait()

    return pl.pallas_call(
        body, name=name, in_specs=[ANY] * nt, out_specs=[ANY] * nt,
        out_shape=[SDS((4,) + s.shape, s.dtype) for s in shards],
        scratch_shapes=[pltpu.SemaphoreType.DMA((nt, 3))] * 4 + [pltpu.SemaphoreType.DMA((nt,))] * 2,
        compiler_params=pltpu.CompilerParams(has_side_effects=True))(*shards)


def _join_halves(bufs, name):
    nt = len(bufs)

    def body(*refs):
        outs = refs[nt:2 * nt]
        send, recv = refs[2 * nt:]
        x, y, c, _ = _place()
        cps = [pltpu.make_async_remote_copy(src_ref=outs[t].at[c], dst_ref=outs[t].at[c], send_sem=send.at[t], recv_sem=recv.at[t],
                                            device_id=(x, y, 1 - c), device_id_type=MESH) for t in range(nt)]
        for cp in cps:
            cp.start()
        for t in range(nt):
            theirs = outs[t].at[1 - c]
            pltpu.make_async_remote_copy(src_ref=theirs, dst_ref=theirs, send_sem=send.at[t], recv_sem=recv.at[t],
                                         device_id=(x, y, 1 - c), device_id_type=MESH).wait_recv()
        for cp in cps:
            cp.wait_send()

    return pl.pallas_call(
        body, name=name, in_specs=[ANY] * nt, out_specs=[ANY] * nt, out_shape=[SDS(b.shape, b.dtype) for b in bufs],
        input_output_aliases={t: t for t in range(nt)},
        scratch_shapes=[pltpu.SemaphoreType.DMA((nt,))] * 2,
        compiler_params=pltpu.CompilerParams(has_side_effects=True))(*bufs)


def _exchange_small(v, reduce, name, after=()):
    rows = v.shape[0]
    after, after_specs = _unread(after)

    def body(v_ref, *rest):
        o_ref, buf, send, recv = rest[-4:]
        x, y, c, _ = _place()
        me = 4 * x + 2 * y + c
        buf[me] = v_ref[...]

        def peer(dx, dy, dc):
            return (1 - x if dx else x, 1 - y if dy else y, 1 - c if dc else c)

        peers = [(dx, dy, dc) for dx in range(2) for dy in range(2) for dc in range(2) if (dx, dy, dc) != (0, 0, 0)]
        cps = []
        for j, (dx, dy, dc) in enumerate(peers):
            cps.append(pltpu.make_async_remote_copy(src_ref=v_ref, dst_ref=buf.at[me], send_sem=send.at[j], recv_sem=recv.at[j],
                                                    device_id=peer(dx, dy, dc), device_id_type=MESH))
        for cp in cps:
            cp.start()
        for j, (dx, dy, dc) in enumerate(peers):
            px, py, pc = peer(dx, dy, dc)
            blk = buf.at[4 * px + 2 * py + pc]
            pltpu.make_async_remote_copy(src_ref=blk, dst_ref=blk, send_sem=send.at[j], recv_sem=recv.at[j],
                                         device_id=(px, py, pc), device_id_type=MESH).wait_recv()
        for cp in cps:
            cp.wait_send()
        if reduce:
            acc = buf[0]
            for j in range(1, 8):
                acc = acc + buf[j]
            o_ref[...] = acc
        else:
            o_ref[...] = buf[...]

    vm = pl.BlockSpec(memory_space=pltpu.VMEM)
    return pl.pallas_call(
        body, name=name, in_specs=[vm] + after_specs, out_specs=vm, out_shape=SDS((rows, 128) if reduce else (8, rows, 128), F32),
        scratch_shapes=[pltpu.VMEM((8, rows, 128), F32), pltpu.SemaphoreType.DMA((7,)), pltpu.SemaphoreType.DMA((7,))],
        compiler_params=pltpu.CompilerParams(has_side_effects=True))(v, *after)


HBM = pl.BlockSpec(memory_space=pltpu.HBM)
SEM = pl.BlockSpec(memory_space=pltpu.SEMAPHORE)
TOKEN = pl.BlockSpec(memory_space=pltpu.VMEM)
TOKEN_SHAPE = SDS((8, 128), F32)
PEERS = 7


def _in_hbm(a):
    return pltpu.with_memory_space_constraint(a, pltpu.HBM)


def _split_params():
    return pltpu.CompilerParams(has_side_effects=pltpu.SideEffectType.DATAFLOW_SIDE_EFFECTING)


def _gather_start(shards, name, after=()):
    nt = len(shards)
    after, after_specs = _unread(after)

    def body(*refs):
        ins, lands = refs[:nt], refs[nt:2 * nt]
        outs = refs[2 * nt + len(after):]
        sends, recvs = outs[:nt], outs[nt:2 * nt]
        x, y, c, chips = _place()
        me = 2 * x + y
        for t in range(nt):
            h = ins[t].shape[0] // 2
            mine = pl.ds(c * h, h)
            for j, (cx, cy) in enumerate(chips):
                for dc in range(2):
                    pltpu.make_async_remote_copy(src_ref=ins[t].at[mine], dst_ref=lands[t].at[me, mine], send_sem=sends[t].at[2 * j + dc],
                                                 recv_sem=recvs[t].at[2 * j + c], device_id=(cx, cy, dc), device_id_type=MESH).start()
            pltpu.make_async_remote_copy(src_ref=ins[t], dst_ref=lands[t].at[me], send_sem=sends[t].at[PEERS - 1], recv_sem=recvs[t].at[PEERS - 1],
                                         device_id=(x, y, 1 - c), device_id_type=MESH).start()
        outs[-1][...] = jnp.zeros(TOKEN_SHAPE.shape, F32)

    lands = [lax.empty((4,) + s.shape, s.dtype) for s in shards]
    out = pl.pallas_call(
        body, name=name, in_specs=[HBM] * (2 * nt) + after_specs, out_specs=[SEM] * (2 * nt) + [HBM] * (2 * nt) + [TOKEN],
        out_shape=[pltpu.SemaphoreType.DMA((PEERS,))] * (2 * nt)
        + [pltpu.HBM(s.shape, s.dtype) for s in shards] + [pltpu.HBM(l.shape, l.dtype) for l in lands] + [TOKEN_SHAPE],
        input_output_aliases={t: 2 * nt + t for t in range(2 * nt)}, compiler_params=_split_params())(
            *[_in_hbm(s) for s in shards], *[_in_hbm(l) for l in lands], *after)
    return out[:nt], out[nt:2 * nt], out[2 * nt:3 * nt], out[3 * nt:4 * nt], out[-1]


def _gather_wait(sends, recvs, shards, lands, after, name):
    nt = len(shards)

    def body(*refs):
        ins, lands_ref = refs[:nt], refs[nt:2 * nt]
        send_refs, recv_refs = refs[2 * nt:3 * nt], refs[3 * nt:4 * nt]
        x, y, c, chips = _place()
        for t in range(nt):
            h = ins[t].shape[0] // 2
            for j, (cx, cy) in enumerate(chips):
                for cs in range(2):
                    blk = lands_ref[t].at[2 * cx + cy, pl.ds(cs * h, h)]
                    pltpu.make_async_remote_copy(src_ref=blk, dst_ref=blk, send_sem=send_refs[t].at[2 * j + cs], recv_sem=recv_refs[t].at[2 * j + cs],
                                                 device_id=(cx, cy, cs), device_id_type=MESH).wait()
            blk = lands_ref[t].at[2 * x + y]
            pltpu.make_async_remote_copy(src_ref=blk, dst_ref=blk, send_sem=send_refs[t].at[PEERS - 1], recv_sem=recv_refs[t].at[PEERS - 1],
                                         device_id=(x, y, 1 - c), device_id_type=MESH).wait()

    out = pl.pallas_call(
        body, name=name, in_specs=[HBM] * (2 * nt) + [SEM] * (2 * nt) + [ANY], out_specs=[HBM] * (2 * nt),
        out_shape=[pltpu.HBM(s.shape, s.dtype) for s in shards] + [pltpu.HBM(l.shape, l.dtype) for l in lands],
        input_output_aliases={t: t for t in range(2 * nt)}, compiler_params=_split_params())(*shards, *lands, *sends, *recvs, after)
    return out[nt:]


def _scatter_start(g, name):
    _, r, c_ = g.shape
    h = r // 2

    def body(g_ref, land, send, recv, g_thru, land_thru, token):
        x, y, c, chips = _place()
        for j, (cx, cy) in enumerate(chips):
            for dc in range(2):
                pltpu.make_async_remote_copy(src_ref=g_ref.at[2 * cx + cy, pl.ds(dc * h, h)], dst_ref=land.at[2 * j + c], send_sem=send.at[2 * j + dc],
                                             recv_sem=recv.at[2 * j + c], device_id=(cx, cy, dc), device_id_type=MESH).start()
        pltpu.make_async_remote_copy(src_ref=g_ref.at[2 * x + y, pl.ds((1 - c) * h, h)], dst_ref=land.at[PEERS - 1], send_sem=send.at[PEERS - 1],
                                     recv_sem=recv.at[PEERS - 1], device_id=(x, y, 1 - c), device_id_type=MESH).start()
        token[...] = jnp.zeros(TOKEN_SHAPE.shape, F32)

    land = lax.empty((PEERS, h, c_), g.dtype)
    return pl.pallas_call(
        body, name=name, in_specs=[HBM, HBM], out_specs=[SEM, SEM, HBM, HBM, TOKEN],
        out_shape=[pltpu.SemaphoreType.DMA((PEERS,)), pltpu.SemaphoreType.DMA((PEERS,)), pltpu.HBM(g.shape, g.dtype),
                   pltpu.HBM(land.shape, land.dtype), TOKEN_SHAPE],
        input_output_aliases={0: 2, 1: 3}, compiler_params=_split_params())(_in_hbm(g), _in_hbm(land))


def _scatter_wait(started, after, name):
    nt = len(started)

    def body(*refs):
        lands = refs[nt:2 * nt]
        sends, recvs = refs[2 * nt:3 * nt], refs[3 * nt:4 * nt]
        x, y, c, chips = _place()
        peers = [(cx, cy, dc) for cx, cy in chips for dc in range(2)] + [(x, y, 1 - c)]
        for t in range(nt):
            for k, peer in enumerate(peers):
                blk = lands[t].at[k]
                pltpu.make_async_remote_copy(src_ref=blk, dst_ref=blk, send_sem=sends[t].at[k], recv_sem=recvs[t].at[k],
                                             device_id=peer, device_id_type=MESH).wait()

    gs, lands = [s[2] for s in started], [s[3] for s in started]
    after, after_specs = _unread(after)
    out = pl.pallas_call(
        body, name=name, in_specs=[HBM] * (2 * nt) + [SEM] * (2 * nt) + after_specs, out_specs=[HBM] * (2 * nt),
        out_shape=[pltpu.HBM(a.shape, a.dtype) for a in gs + lands],
        input_output_aliases={t: t for t in range(2 * nt)}, compiler_params=_split_params())(
            *gs, *lands, *[s[0] for s in started], *[s[1] for s in started], *after)
    return out[:nt], out[nt:]


def _sum_devices(g, land, me, core, name):
    npeer, h, c = land.shape
    tr = _half_tile(h)
    steps = h // tr

    def body(ix_ref, own_ref, land_ref, o_ref):
        acc = own_ref[0].astype(F32)
        for j in range(npeer):
            acc = acc + land_ref[j].astype(F32)
        o_ref[0] = acc

    grid_spec = pltpu.PrefetchScalarGridSpec(
        num_scalar_prefetch=1, grid=(steps,),
        in_specs=[pl.BlockSpec((1, tr, c), lambda i, ix: (ix[0], ix[1] * steps + i, 0)), pl.BlockSpec((npeer, tr, c), lambda i, ix: (0, i, 0))],
        out_specs=pl.BlockSpec((1, tr, c), lambda i, ix: (ix[1], i, 0)))
    return pl.pallas_call(body, name=name, grid_spec=grid_spec, out_shape=SDS((2, h, c), F32),
                          compiler_params=_cp("parallel"))(jnp.stack([me, core]), g, land)


def _pack_small(parts):
    flat = jnp.concatenate([p.reshape(-1) for p in parts])
    total = flat.shape[0]
    rows = -(-total // 1024) * 8
    return jnp.pad(flat, (0, rows * 128 - total)).reshape(rows, 128)


def _unpack_small(packed, shapes):
    flat = packed.reshape(-1)
    out, off = [], 0
    for s in shapes:
        size = int(np.prod(s))
        out.append(flat[off:off + size].reshape(s))
        off += size
    return out


def _local_step(x, mem, target, w_in, first_after, mid_weights, ffn_weights, on_grad, gains, conv_w, conv_b, hg_lb):
    n = x.shape[0]
    cos, sin = _rope_tables(n)
    seg = _hg_segments()
    gp, gn = _hg_pair_sums()
    masks = _hg_level_masks()
    gq2 = jnp.tile(gains["q_norm_g"], (1, 2))
    gk2 = jnp.tile(gains["k_norm_g"], (1, 2))
    a0 = hg_lb[:, 0:1, :]
    a1 = hg_lb[:, 1:2, :]

    p, h1 = _norm_mm(x, gains["pre_mix_g"], w_in, F32, TOKEN_TILE, 1664, "in_proj", after=(first_after,))
    qr, kr = _qk_prep(p, gq2, gk2, cos, sin, "qk_prep")
    heads = lambda a: a.reshape(n, ATT_KV_HEADS, ATT_HEAD_DIM).transpose(1, 0, 2)
    kh = heads(kr)
    vh = heads(p[:, OFF_AV:OFF_AV + ATT_KV_DIM].astype(MXU_DTYPE))
    att = _attn_fwd(qr, kh, vh, "attn_fwd")
    o2, s0 = _hgrn_fwd(p, a0, a1, seg, masks, "hgrn_fwd")
    rec = _hg_post(o2, p, gains["hg_out_norm_g"], "hg_post")
    cat = jnp.concatenate([att, rec], axis=1)
    w_out, w_xq, w_xkv, w_xo = mid_weights(cat)
    mixed, x1 = _mm_resid_norm(cat, w_out, x, gains["post_mix_g"], 512, "out_proj_resid")
    xq, h2 = _norm_mm(x1, gains["pre_x_g"], w_xq, MXU_DTYPE, TOKEN_TILE, 1024, "xq_proj")
    kv, mn = _norm_mm(mem, gains["mem_norm_g"], w_xkv, MXU_DTYPE, 256, 2048, "xkv_proj")
    ox = _xattn_fwd(xq, kv, "xattn_fwd")
    xo, x2 = _mm_resid_norm(ox, w_xo, x1, gains["post_x_g"], 512, "xo_proj_resid")
    w_up = ffn_weights("w_up", x2)
    u, h3 = _norm_mm(x2, gains["pre_ffn_g"], w_up, F32, TOKEN_TILE, 1408, "up_proj")
    act = _conv_gate(u, conv_w, conv_b, "conv_gate")
    w_down = ffn_weights("w_down", act)
    dn, d3, loss = _mm_resid_norm(act, w_down, x2, gains["post_ffn_g"], 512, "down_proj_resid_loss", target=target)

    gs = {}
    d_act, d_dn, gs["post_ffn_g"] = _norm_bwd_mm(dn, gains["post_ffn_g"], d3, w_down, F32, 512, 1408, "ffn_post_bwd_down_dx")
    tok = on_grad("w_down", _mm(act, d_dn, "tn", WIRE_DTYPE, 1408, 1024, "down_dw"))
    du_g, du_v, dcw_g, dcw_v, dcb_g, dcb_v = _conv_gate_bwd(u, conv_w, conv_b, d_act, "conv_gate_bwd", after=(tok,))
    gs["conv_w"] = jnp.concatenate([dcw_g, dcw_v], axis=1)
    gs["conv_b"] = jnp.concatenate([dcb_g, dcb_v], axis=1)
    ff_shard = w_up.shape[2]
    g_up = _dw_by_owner(h3, du_g, ff_shard, 0, None, 512, "up_dw_gate")
    tok = on_grad("w_up", _dw_by_owner(h3, du_v, ff_shard, 2, g_up, 512, "up_dw_value"))
    d2, gs["pre_ffn_g"] = _dx_norm_bwd([(du_g, 0), (du_g, 1), (du_v, 0), (du_v, 1)], w_up, x2, gains["pre_ffn_g"], d3, 512,
                                       "up_dx_pre_bwd", after=(tok,))
    d_ox, d_xo, gs["post_x_g"] = _norm_bwd_mm(xo, gains["post_x_g"], d2, w_xo, MXU_DTYPE, 512, 1024, "x_post_bwd_xo_dx")
    tok = on_grad("w_xo", _mm(ox, d_xo, "tn", WIRE_DTYPE, 512, 1024, "xo_dw"))
    d_xq, d_k, d_v = _xattn_bwd(xq, kv, d_ox, "xattn_bwd", after=(tok,))
    d_kv = jnp.concatenate([d_k, d_v], axis=1).astype(MXU_DTYPE)
    tok = on_grad("w_xq", _mm(h2, d_xq, "tn", WIRE_DTYPE, 512, 1024, "xq_dw"))
    tok_kv = on_grad("w_xkv", _dw_by_owner(mn, d_kv, w_xkv.shape[2], 0, None, 512, "xkv_dw"))
    d1, gs["pre_x_g"] = _dx_norm_bwd([(d_xq, 0)], w_xq[None], x1, gains["pre_x_g"], d2, 512, "xq_dx_pre_bwd", after=(tok, tok_kv))
    d_mn = _mm_nt_parts([(d_kv, s) for s in range(4)], w_xkv, F32, 256, 1024, "xkv_dx")
    _, gs["mem_norm_g"] = _norm_bwd(mem, gains["mem_norm_g"], d_mn, None, MXU_DTYPE, "mem_norm_bwd")
    d_cat, d_mixed, gs["post_mix_g"] = _norm_bwd_mm(mixed, gains["post_mix_g"], d1, w_out, MXU_DTYPE, 512, 1024, "mix_post_bwd_out_dx")
    tok = on_grad("w_out", _mm(cat, d_mixed, "tn", WIRE_DTYPE, 512, 1024, "out_dw"))
    d_o, d_hg, dg_hg = _hg_post_bwd(o2, p, gains["hg_out_norm_g"], d_cat, "hg_post_bwd", after=(tok,))
    gs["hg_out_norm_g"] = dg_hg.reshape(HG_HEADS, HG_HEAD_DIM).sum(axis=0, keepdims=True)
    dhq2, dz2, dhv2, dlb = _hgrn_bwd(p, a0, a1, seg, masks, gp, gn, d_o, s0, "hgrn_bwd")
    lb = jax.nn.sigmoid(a0 - a1)
    da0 = dlb * lb * (1.0 - lb)
    gs["hg_lb"] = jnp.concatenate([da0, -da0], axis=1)
    d_qr, d_kh, d_vh = _attn_bwd(qr, kh, vh, cat, d_cat, "attn_bwd")
    unheads = lambda a: a.transpose(2, 0, 1).reshape(n, ATT_KV_DIM)
    d_aq, d_ak, dgq, dgk = _qk_prep_bwd(p, gq2, gk2, cos, sin, d_qr, unheads(d_kh), "qk_prep_bwd")
    gs["q_norm_g"] = dgq.reshape(ATT_HEADS, ATT_HEAD_DIM).sum(axis=0, keepdims=True)
    gs["k_norm_g"] = dgk.reshape(ATT_KV_HEADS, ATT_HEAD_DIM).sum(axis=0, keepdims=True)
    d_p = jnp.concatenate([d_aq, d_ak, unheads(d_vh).astype(MXU_DTYPE), (dhq2[0] + dhq2[1]).astype(MXU_DTYPE),
                           dz2[0].astype(MXU_DTYPE), dz2[1].astype(MXU_DTYPE), (dhv2[0] + dhv2[1]).astype(MXU_DTYPE), d_hg], axis=1)
    tok = on_grad("w_in", _mm(h1, d_p, "tn", WIRE_DTYPE, 512, 1664, "in_dw"))
    grad_x, gs["pre_mix_g"] = _dx_norm_bwd([(d_p, 0)], w_in[None], x, gains["pre_mix_g"], d1, 512, "in_dx_pre_bwd", after=(tok,))
    return loss, grad_x, gs


MATS = ("w_in", "w_out", "w_xq", "w_xkv", "w_xo", "w_up", "w_down")
GAINS = ("pre_mix_g", "q_norm_g", "k_norm_g", "hg_out_norm_g", "post_mix_g", "pre_x_g", "mem_norm_g", "post_x_g", "pre_ffn_g", "post_ffn_g")
WEIGHTS = ('pre_mix_g', 'w_in', 'q_norm_g', 'k_norm_g', 'hg_lb', 'hg_out_norm_g', 'w_out', 'post_mix_g', 'pre_x_g', 'mem_norm_g', 'w_xq',
           'w_xkv', 'w_xo', 'post_x_g', 'pre_ffn_g', 'w_up', 'conv_w', 'conv_b', 'w_down', 'post_ffn_g')


def kernel(x, mem, pre_mix_g, w_in, q_norm_g, k_norm_g, hg_lb, hg_out_norm_g, w_out, post_mix_g, pre_x_g, mem_norm_g, w_xq, w_xkv, w_xo, post_x_g, pre_ffn_g, w_up, conv_w, conv_b, w_down, post_ffn_g, loss_target, m_pre_mix_g, m_w_in, m_q_norm_g, m_k_norm_g, m_hg_lb, m_hg_out_norm_g, m_w_out, m_post_mix_g, m_pre_x_g, m_mem_norm_g, m_w_xq, m_w_xkv, m_w_xo, m_post_x_g, m_pre_ffn_g, m_w_up, m_conv_w, m_conv_b, m_w_down, m_post_ffn_g, v_pre_mix_g, v_w_in, v_q_norm_g, v_k_norm_g, v_hg_lb, v_hg_out_norm_g, v_w_out, v_post_mix_g, v_pre_x_g, v_mem_norm_g, v_w_xq, v_w_xkv, v_w_xo, v_post_x_g, v_pre_ffn_g, v_w_up, v_conv_w, v_conv_b, v_w_down, v_post_ffn_g):
    args = dict(locals())
    w = {k: args[k] for k in WEIGHTS}
    m = {k: args["m_" + k] for k in WEIGHTS}
    v = {k: args["v_" + k] for k in WEIGHTS}
    chip = 2 * lax.axis_index("x") + lax.axis_index("y")
    core = lax.axis_index("c")

    shards = {k: w[k][0].astype(WIRE_DTYPE) for k in MATS}

    def whole(k, g):
        return g if k in ("w_xkv", "w_up") else g.reshape(-1, g.shape[-1])

    w_in_shards = _gather_shards([shards["w_in"]], "gather_w_in")[0]
    w_in_full = jnp.concatenate([w_in_shards[s] for s in range(4)], axis=1)
    small_in = _exchange_small(_pack_small([w["conv_w"][0], w["hg_lb"]]), False, "gather_small")
    mid_names, ffn_names = ("w_out", "w_xq", "w_xkv", "w_xo"), ("w_up", "w_down")
    mid = _gather_start([shards[k] for k in mid_names], "gather_mid_start", after=(w_in_full, small_in))
    ffn = _gather_start([shards[k] for k in ffn_names], "gather_ffn_start", after=(mid[4],))

    def mid_weights(after):
        return [whole(k, g) for k, g in zip(mid_names, _gather_wait(*mid[:4], after, "gather_mid_wait"))]

    def ffn_weights(k, after):
        t = ffn_names.index(k)
        return whole(k, _gather_wait(*[part[t:t + 1] for part in ffn[:4]], after, "gather_wait_" + k)[0])

    cw_parts, lb_parts = [], []
    for s in range(4):
        cw_s, lb_s = _unpack_small(small_in[2 * s], [w["conv_w"][0].shape, w["hg_lb"].shape])
        cw_parts.append(cw_s)
        lb_parts.append(lb_s)
    conv_w_full = jnp.concatenate(cw_parts, axis=1)
    hg_lb_full = jnp.concatenate(lb_parts, axis=2)

    started = {}

    def on_grad(k, g):
        if k == "w_in":
            g = g.reshape(g.shape[0], 4, g.shape[1] // 4).transpose(1, 0, 2)
        elif g.ndim == 2:
            g = g.reshape(4, g.shape[0] // 4, g.shape[1])
        *started[k], token = _scatter_start(g, "grad_start_" + k)
        return token

    gains = {k: w[k] for k in GAINS}
    loss_part, grad_x, gs = _local_step(x[0], mem[0], loss_target[0], w_in_full, ffn[4], mid_weights, ffn_weights, on_grad, gains,
                                        conv_w_full, w["conv_b"], hg_lb_full)
    gs["loss"] = loss_part

    grads, delta, new_m, new_v = {}, {}, {}, {}

    def reduce_matrices(names, after, tag):
        sent, landed = _scatter_wait([started[k] for k in names], after, "grad_wait_" + tag)
        halves = [_sum_devices(g, land, chip, core, "grad_sum_" + k) for k, g, land in zip(names, sent, landed)]
        for k, r in zip(names, _join_halves(halves, "grad_join_" + tag)):
            grads[k] = r.reshape(1, -1, r.shape[-1])

    def adamw(names):
        for k in names:
            shape = w[k].shape
            keep = len(shape) == 3 and shape[0] == 1
            two_d = lambda a: a.reshape(shape) if keep else a.reshape(-1, shape[-1])
            d, mo, vo, go = _adamw(two_d(w[k]), two_d(grads[k]), two_d(m[k]), two_d(v[k]), "adamw_" + k)
            delta[k], new_m[k], new_v[k], grads[k] = d.reshape(shape), mo.reshape(shape), vo.reshape(shape), go.reshape(shape)

    early = tuple(k for k in MATS if k != "w_in")
    reduce_matrices(early, (grad_x,), "early")
    adamw(early)

    small_names = GAINS + ("conv_b", "conv_w", "hg_lb")
    packed = _pack_small([gs[k] for k in small_names + ("loss",)])
    reduced_small = _exchange_small(packed, True, "reduce_small", after=tuple(new_v[k] for k in early))
    *summed, loss = _unpack_small(reduced_small, [gs[k].shape for k in small_names + ("loss",)])
    loss = loss[0, 0]
    for k, g in zip(small_names, summed):
        grads[k] = g
    ncw = w["conv_w"].shape[2]
    grads["conv_w"] = lax.dynamic_slice_in_dim(grads["conv_w"], chip * ncw, ncw, axis=1)[None]
    nlb = w["hg_lb"].shape[2]
    grads["hg_lb"] = lax.dynamic_slice_in_dim(grads["hg_lb"], chip * nlb, nlb, axis=2)
    replicated = GAINS + ("conv_b",)
    shapes = [w[k].shape for k in replicated]
    rows = sum(int(np.prod(s)) for s in shapes) // 128
    pack = lambda d: jnp.concatenate([d[k].reshape(-1) for k in replicated]).reshape(rows, 128)
    outs = _adamw(pack(w), reduced_small[:rows], pack(m), pack(v), "adamw_replicated")
    for into, packed_out in zip((delta, new_m, new_v, grads), outs):
        for k, a in zip(replicated, _unpack_small(packed_out, shapes)):
            into[k] = a
    adamw(("conv_w", "hg_lb"))

    reduce_matrices(("w_in",), tuple(new_v[k] for k in early + small_names), "late")
    adamw(("w_in",))
    return (loss, grad_x[None], *[grads[k] for k in WEIGHTS], *[delta[k] for k in WEIGHTS],
            *[new_m[k] for k in WEIGHTS], *[new_v[k] for k in WEIGHTS])
```

```python
import numpy as np
import jax
import jax.numpy as jnp
from jax import lax
from jax.experimental import pallas as pl
from jax.experimental.pallas import tpu as pltpu

F32 = jnp.float32
MXU_DTYPE = jnp.bfloat16
WIRE_DTYPE = jnp.bfloat16
VMEM_LIMIT_BYTES = 56 * 1024 * 1024
EPS = 1e-6
MESH = pl.DeviceIdType.MESH

GRID_W = 64
ATT_HEADS, ATT_KV_HEADS, ATT_HEAD_DIM = 8, 2, 64
ATT_GROUP = ATT_HEADS // ATT_KV_HEADS
ATT_Q_DIM, ATT_KV_DIM = 512, 128
ROPE_THETA = 10000.0
HG_HEADS, HG_HEAD_DIM, HG_DIM = 4, 128, 512
HG_CHUNK = 128
HG_LEVELS = 7
HG_PAIR = 2 * HG_HEAD_DIM
X_HEADS, X_HEAD_DIM = 4, 256
D_FF = 2816
FF_COLS = 256
FF_BLOCKS = D_FF // FF_COLS
OFF_AK, OFF_AV, OFF_HQ, OFF_ZF, OFF_ZB, OFF_HI, OFF_HG = 512, 640, 768, 1280, 1792, 2304, 2816

ADAM_LR, ADAM_B1, ADAM_B2, ADAM_EPS, ADAM_WD, ADAM_STEP = 0.001, 0.9, 0.999, 1e-08, 0.01, 10

SDS = jax.ShapeDtypeStruct


def _cp(*sem):
    return pltpu.CompilerParams(dimension_semantics=sem, vmem_limit_bytes=VMEM_LIMIT_BYTES)


def _dot(a, b, form="nn"):
    dims = {"nn": (((1,), (0,)), ((), ())), "nt": (((1,), (1,)), ((), ())), "tn": (((0,), (0,)), ((), ()))}[form]
    return lax.dot_general(a.astype(MXU_DTYPE), b.astype(MXU_DTYPE), dims, preferred_element_type=F32)


def _sigmoid(x):
    return 1.0 / (1.0 + jnp.exp(-x))


def _rstd(x):
    return lax.rsqrt(jnp.mean(x * x, axis=-1, keepdims=True) + EPS)


def _rms_bwd(x, g, dy):
    r = _rstd(x)
    xh = x * r
    dn = dy * g
    dx = r * (dn - xh * jnp.mean(dn * xh, axis=-1, keepdims=True))
    return dx, jnp.sum(dy * xh, axis=0, keepdims=True)


def _unread(after):
    after = tuple(a for a in after if a is not None)
    return after, [pl.BlockSpec(memory_space=pl.ANY)] * len(after)


def _mm(a, b, form, out_dtype, tm, tn, name, after=()):
    after, after_specs = _unread(after)
    if form == "nn":
        (m, k), n = a.shape, b.shape[1]
    elif form == "nt":
        (m, k), n = a.shape, b.shape[0]
    else:
        (k, m), n = a.shape, b.shape[1]
    tm, tn = min(tm, m), min(tn, n)
    assert m % tm == 0 and n % tn == 0, (name, m, n, tm, tn)

    def body(a_ref, b_ref, *rest):
        o_ref = rest[-1]
        o_ref[...] = _dot(a_ref[...], b_ref[...], form).astype(o_ref.dtype)

    a_spec = pl.BlockSpec((k, tm), lambda i, j: (0, i)) if form == "tn" else pl.BlockSpec((tm, k), lambda i, j: (i, 0))
    b_spec = pl.BlockSpec((tn, k), lambda i, j: (j, 0)) if form == "nt" else pl.BlockSpec((k, tn), lambda i, j: (0, j))
    return pl.pallas_call(
        body, name=name, grid=(m // tm, n // tn), in_specs=[a_spec, b_spec] + after_specs,
        out_specs=pl.BlockSpec((tm, tn), lambda i, j: (i, j)), out_shape=SDS((m, n), out_dtype),
        compiler_params=_cp("parallel", "parallel"))(a, b, *after)


def _mm_nt_parts(a_parts, b, out_dtype, tm, tn, name, after=()):
    after, after_specs = _unread(after)
    parts, n, p = b.shape
    m = a_parts[0][0].shape[0]
    tm, tn = min(tm, m), min(tn, n)
    assert m % tm == 0 and n % tn == 0 and len(a_parts) == parts, (name, m, b.shape)

    def body(*refs):
        o_ref = refs[-1]
        acc = _dot(refs[0][...], refs[parts][0], "nt")
        for s in range(1, parts):
            acc = acc + _dot(refs[s][...], refs[parts + s][0], "nt")
        o_ref[...] = acc.astype(o_ref.dtype)

    a_specs = [pl.BlockSpec((tm, p), lambda i, j, cb=cb: (i, cb)) for _, cb in a_parts]
    b_specs = [pl.BlockSpec((1, tn, p), lambda i, j, s=s: (s, j, 0)) for s in range(parts)]
    return pl.pallas_call(
        body, name=name, grid=(m // tm, n // tn), in_specs=a_specs + b_specs + after_specs,
        out_specs=pl.BlockSpec((tm, tn), lambda i, j: (i, j)), out_shape=SDS((m, n), out_dtype),
        compiler_params=_cp("parallel", "parallel"))(*[arr for arr, _ in a_parts], *([b] * parts), *after)


def _norm_bwd_mm(y, g, d, w, out_dtype, tm, tn, name):
    n, dm = y.shape
    nn = w.shape[0]
    tm, tn = min(tm, n), min(tn, nn)
    assert n % tm == 0 and nn % tn == 0 and w.shape[1] == dm, (name, y.shape, w.shape)

    def body(y_ref, g_ref, d_ref, w_ref, dx_ref, dy_ref, dg_ref, dys):
        i, j = pl.program_id(0), pl.program_id(1)

        @pl.when(jnp.logical_and(i == 0, j == 0))
        def _():
            dg_ref[...] = jnp.zeros_like(dg_ref)

        @pl.when(j == 0)
        def _():
            dy, dg = _rms_bwd(y_ref[...], g_ref[...], d_ref[...])
            dy = dy.astype(MXU_DTYPE)
            dys[...] = dy
            dy_ref[...] = dy
            dg_ref[...] += dg

        dx_ref[...] = _dot(dys[...], w_ref[...], "nt").astype(dx_ref.dtype)

    row = pl.BlockSpec((tm, dm), lambda i, j: (i, 0))
    vec = pl.BlockSpec((1, dm), lambda i, j: (0, 0))
    return pl.pallas_call(
        body, name=name, grid=(n // tm, nn // tn), in_specs=[row, vec, row, pl.BlockSpec((tn, dm), lambda i, j: (j, 0))],
        out_specs=[pl.BlockSpec((tm, tn), lambda i, j: (i, j)), row, vec],
        out_shape=[SDS((n, nn), out_dtype), SDS((n, dm), MXU_DTYPE), SDS((1, dm), F32)],
        scratch_shapes=[pltpu.VMEM((tm, dm), MXU_DTYPE)],
        compiler_params=_cp("arbitrary", "arbitrary"))(y, g, d, w)


def _mm_resid_norm(a, b, x, g, tm, name, target=None):
    n, k = a.shape
    d = b.shape[1]
    tm = min(tm, n)
    assert n % tm == 0 and x.shape == (n, d), (name, a.shape, b.shape)
    with_loss = target is not None

    def body(a_ref, b_ref, x_ref, g_ref, *rest):
        y = _dot(a_ref[...], b_ref[...])
        out = x_ref[...] + y * _rstd(y) * g_ref[...]
        if not with_loss:
            y_ref, o_ref = rest
            y_ref[...] = y
            o_ref[...] = out
            return
        t_ref, y_ref, d_ref, l_ref = rest
        y_ref[...] = y
        diff = out - t_ref[...]
        d_ref[...] = diff * (1.0 / d)

        @pl.when(pl.program_id(0) == 0)
        def _():
            l_ref[...] = jnp.zeros_like(l_ref)

        l_ref[...] += 0.5 * jnp.sum(jnp.mean(diff * diff, axis=-1, keepdims=True), axis=0, keepdims=True)

    row = pl.BlockSpec((tm, d), lambda i: (i, 0))
    ins = [pl.BlockSpec((tm, k), lambda i: (i, 0)), pl.BlockSpec((k, d), lambda i: (0, 0)), row, pl.BlockSpec((1, d), lambda i: (0, 0))]
    out = SDS((n, d), F32)
    if with_loss:
        return pl.pallas_call(body, name=name, grid=(n // tm,), in_specs=ins + [row], out_specs=[row, row, pl.BlockSpec((1, 1), lambda i: (0, 0))],
                              out_shape=[out, out, SDS((1, 1), F32)], compiler_params=_cp("arbitrary"))(a, b, x, g, target)
    return pl.pallas_call(body, name=name, grid=(n // tm,), in_specs=ins, out_specs=[row, row], out_shape=[out, out],
                          compiler_params=_cp("parallel"))(a, b, x, g)


def _dx_norm_bwd(a_parts, b, x, g, res, tm, name, after=()):
    after, after_specs = _unread(after)
    parts, d, p = b.shape
    n = x.shape[0]
    tm = min(tm, n)
    assert n % tm == 0 and len(a_parts) == parts and x.shape[1] == d, (name, x.shape, b.shape)

    def body(*refs):
        x_ref, g_ref, res_ref = refs[2 * parts:2 * parts + 3]
        dx_ref, dg_ref = refs[-2:]
        dh = _dot(refs[0][...], refs[parts][0], "nt")
        for s in range(1, parts):
            dh = dh + _dot(refs[s][...], refs[parts + s][0], "nt")
        dx, dg = _rms_bwd(x_ref[...], g_ref[...], dh)
        dx_ref[...] = dx + res_ref[...]

        @pl.when(pl.program_id(0) == 0)
        def _():
            dg_ref[...] = jnp.zeros_like(dg_ref)

        dg_ref[...] += dg

    a_specs = [pl.BlockSpec((tm, p), lambda i, cb=cb: (i, cb)) for _, cb in a_parts]
    b_specs = [pl.BlockSpec((1, d, p), lambda i, s=s: (s, 0, 0)) for s in range(parts)]
    row = pl.BlockSpec((tm, d), lambda i: (i, 0))
    vec = pl.BlockSpec((1, d), lambda i: (0, 0))
    return pl.pallas_call(
        body, name=name, grid=(n // tm,), in_specs=a_specs + b_specs + [row, vec, row] + after_specs,
        out_specs=[row, vec], out_shape=[SDS((n, d), F32), SDS((1, d), F32)],
        compiler_params=_cp("arbitrary"))(*[arr for arr, _ in a_parts], *([b] * parts), x, g, res, *after)


def _dw_by_owner(a, b, tn, first, into, tm, name):
    k, m = a.shape
    cnt = b.shape[1] // tn
    tm = min(tm, m)
    assert m % tm == 0 and b.shape[1] == cnt * tn and first + cnt <= 4, (name, a.shape, b.shape)

    def body(a_ref, b_ref, *rest):
        rest[-1][0] = _dot(a_ref[...], b_ref[...], "tn").astype(rest[-1].dtype)

    extra = [] if into is None else [into]
    return pl.pallas_call(
        body, name=name, grid=(m // tm, cnt),
        in_specs=[pl.BlockSpec((k, tm), lambda i, j: (0, i)), pl.BlockSpec((k, tn), lambda i, j: (0, j))] + [pl.BlockSpec(memory_space=pl.ANY)] * len(extra),
        out_specs=pl.BlockSpec((1, tm, tn), lambda i, j: (first + j, i, 0)), out_shape=SDS((4, m, tn), WIRE_DTYPE),
        input_output_aliases={2: 0} if extra else {},
        compiler_params=_cp("parallel", "parallel"))(a, b, *extra)


def _norm_mm(x, g, w, out_dtype, tm, tn, name, after=()):
    after, after_specs = _unread(after)
    m, d = x.shape
    sharded = w.ndim == 3
    n = w.shape[-1] * (w.shape[0] if sharded else 1)
    tm, tn = min(tm, m), (w.shape[-1] if sharded else min(tn, n))
    assert m % tm == 0 and n % tn == 0, (name, m, n, tm, tn)

    def body(x_ref, g_ref, w_ref, *rest):
        o_ref, h_ref, hs = rest[-3:]

        @pl.when(pl.program_id(1) == 0)
        def _():
            xv = x_ref[...]
            h = (xv * _rstd(xv) * g_ref[...]).astype(MXU_DTYPE)
            hs[...] = h
            h_ref[...] = h

        o_ref[...] = _dot(hs[...], w_ref[0] if sharded else w_ref[...]).astype(o_ref.dtype)

    w_spec = pl.BlockSpec((1, d, tn), lambda i, j: (j, 0, 0)) if sharded else pl.BlockSpec((d, tn), lambda i, j: (0, j))
    return pl.pallas_call(
        body, name=name, grid=(m // tm, n // tn),
        in_specs=[pl.BlockSpec((tm, d), lambda i, j: (i, 0)), pl.BlockSpec((1, d), lambda i, j: (0, 0)), w_spec] + after_specs,
        out_specs=[pl.BlockSpec((tm, tn), lambda i, j: (i, j)), pl.BlockSpec((tm, d), lambda i, j: (i, 0))],
        out_shape=[SDS((m, n), out_dtype), SDS((m, d), MXU_DTYPE)],
        scratch_shapes=[pltpu.VMEM((tm, d), MXU_DTYPE)],
        compiler_params=_cp("parallel", "arbitrary"))(x, g, w, *after)


ROW_TILE = 512
TOKEN_TILE = 1024


def _norm_bwd(x, g, dy, res, out_dtype, name):
    n, d = x.shape
    tr = min(ROW_TILE, n)
    has_res = res is not None

    def body(*refs):
        x_ref, g_ref, dy_ref = refs[:3]
        dx_ref, dg_ref = refs[-2:]
        dx, dg = _rms_bwd(x_ref[...], g_ref[...], dy_ref[...].astype(F32))
        if has_res:
            dx = dx + refs[3][...]
        dx_ref[...] = dx.astype(dx_ref.dtype)

        @pl.when(pl.program_id(0) == 0)
        def _():
            dg_ref[...] = jnp.zeros_like(dg_ref)

        dg_ref[...] += dg

    row = pl.BlockSpec((tr, d), lambda i: (i, 0))
    vec = pl.BlockSpec((1, d), lambda i: (0, 0))
    ins = [x, g, dy] + ([res] if has_res else [])
    return pl.pallas_call(
        body, name=name, grid=(n // tr,), in_specs=[row, vec, row] + ([row] if has_res else []),
        out_specs=[row, vec], out_shape=[SDS((n, d), out_dtype), SDS((1, d), F32)],
        compiler_params=_cp("arbitrary"))(*ins)


def _rope_tables(n):
    pairs = ATT_HEAD_DIM // 4
    t = np.arange(n)
    inv = np.power(ROPE_THETA, -np.arange(pairs, dtype=np.float32) / pairs).astype(np.float32)
    ang = np.concatenate([(t // GRID_W)[:, None].astype(np.float32) * inv, (t % GRID_W)[:, None].astype(np.float32) * inv], axis=-1)
    cos = np.repeat(np.cos(ang), 2, axis=-1)
    sin = np.repeat(np.sin(ang), 2, axis=-1) * np.tile(np.array([-1.0, 1.0], np.float32), ATT_HEAD_DIM // 2)
    return jnp.asarray(np.tile(cos, 2), F32), jnp.asarray(np.tile(sin, 2), F32)


def _swap_pairs(x):
    lane = lax.broadcasted_iota(jnp.int32, x.shape, 1)
    return jnp.where((lane & 1) == 0, pltpu.roll(x, 127, axis=1), pltpu.roll(x, 1, axis=1))


def _head_mean(v):
    lane = lax.broadcasted_iota(jnp.int32, v.shape, 1)
    lo = jnp.where(lane < ATT_HEAD_DIM, v, 0.0)
    s0 = jnp.sum(lo, axis=-1, keepdims=True)
    s1 = jnp.sum(v - lo, axis=-1, keepdims=True)
    return jnp.where(lane < ATT_HEAD_DIM, s0, s1) * (1.0 / ATT_HEAD_DIM)


def _qk_prep(p, gq, gk, cos, sin, name):
    n = p.shape[0]
    tr = min(ROW_TILE, n)

    def one(xv, g, c, s):
        xn = xv * lax.rsqrt(_head_mean(xv * xv) + EPS) * g
        return xn * c + _swap_pairs(xn) * s

    def body(q_ref, k_ref, gq_ref, gk_ref, c_ref, s_ref, qo_ref, ko_ref):
        c, s = c_ref[...], s_ref[...]
        for j in range(ATT_Q_DIM // 128):
            qo_ref[:, j * 128:(j + 1) * 128] = one(q_ref[:, j * 128:(j + 1) * 128], gq_ref[...], c, s).astype(qo_ref.dtype)
        ko_ref[...] = one(k_ref[...], gk_ref[...], c, s).astype(ko_ref.dtype)

    vec = pl.BlockSpec((1, 128), lambda i: (0, 0))
    tab = pl.BlockSpec((tr, 128), lambda i: (i, 0))
    return pl.pallas_call(
        body, name=name, grid=(n // tr,),
        in_specs=[pl.BlockSpec((tr, ATT_Q_DIM), lambda i: (i, 0)), pl.BlockSpec((tr, 128), lambda i: (i, OFF_AK // 128)), vec, vec, tab, tab],
        out_specs=[pl.BlockSpec((tr, ATT_Q_DIM), lambda i: (i, 0)), tab],
        out_shape=[SDS((n, ATT_Q_DIM), MXU_DTYPE), SDS((n, ATT_KV_DIM), MXU_DTYPE)],
        compiler_params=_cp("parallel"))(p, p, gq, gk, cos, sin)


def _qk_prep_bwd(p, gq, gk, cos, sin, dq, dk, name):
    n = p.shape[0]
    tr = min(ROW_TILE, n)

    def one(xv, g, c, s, dout):
        dxn = dout * c + _swap_pairs(dout * s)
        r = lax.rsqrt(_head_mean(xv * xv) + EPS)
        xh = xv * r
        dn = dxn * g
        dx = r * (dn - xh * _head_mean(dn * xh))
        return dx, jnp.sum(dxn * xh, axis=0, keepdims=True)

    def body(q_ref, k_ref, gq_ref, gk_ref, c_ref, s_ref, dq_ref, dk_ref, dqo_ref, dko_ref, dgq_ref, dgk_ref):
        @pl.when(pl.program_id(0) == 0)
        def _():
            dgq_ref[...] = jnp.zeros_like(dgq_ref)
            dgk_ref[...] = jnp.zeros_like(dgk_ref)

        c, s = c_ref[...], s_ref[...]
        for j in range(ATT_Q_DIM // 128):
            sl = slice(j * 128, (j + 1) * 128)
            dx, dg = one(q_ref[:, sl], gq_ref[...], c, s, dq_ref[:, sl])
            dqo_ref[:, sl] = dx.astype(dqo_ref.dtype)
            dgq_ref[:, sl] += dg
        dx, dg = one(k_ref[...], gk_ref[...], c, s, dk_ref[...])
        dko_ref[...] = dx.astype(dko_ref.dtype)
        dgk_ref[...] += dg

    vec = pl.BlockSpec((1, 128), lambda i: (0, 0))
    tab = pl.BlockSpec((tr, 128), lambda i: (i, 0))
    qrow = pl.BlockSpec((tr, ATT_Q_DIM), lambda i: (i, 0))
    return pl.pallas_call(
        body, name=name, grid=(n // tr,),
        in_specs=[qrow, pl.BlockSpec((tr, 128), lambda i: (i, OFF_AK // 128)), vec, vec, tab, tab, qrow, tab],
        out_specs=[qrow, tab, pl.BlockSpec((1, ATT_Q_DIM), lambda i: (0, 0)), vec],
        out_shape=[SDS((n, ATT_Q_DIM), MXU_DTYPE), SDS((n, ATT_KV_DIM), MXU_DTYPE), SDS((1, ATT_Q_DIM), F32), SDS((1, 128), F32)],
        compiler_params=_cp("arbitrary"))(p, p, gq, gk, cos, sin, dq, dk)


ATT_TQ = 256


def _attn_fwd(q, k, v, name):
    n = q.shape[0]
    tq = min(ATT_TQ, n)
    scale = ATT_HEAD_DIM ** -0.5
    gw = ATT_GROUP * ATT_HEAD_DIM

    def body(q_ref, k_ref, v_ref, o_ref):
        kk, vv = k_ref[0], v_ref[0]
        v_ones = jnp.concatenate([vv, jnp.ones_like(vv)], axis=1)
        outs = []
        for g in range(ATT_GROUP):
            s = _dot(q_ref[:, g * ATT_HEAD_DIM:(g + 1) * ATT_HEAD_DIM] * scale, kk, "nt")
            e = jnp.exp(s - jnp.max(s, axis=-1, keepdims=True))
            ov = _dot(e, v_ones)
            outs.append(ov[:, :ATT_HEAD_DIM] / ov[:, ATT_HEAD_DIM:])
        o_ref[...] = jnp.concatenate(outs, axis=-1).astype(o_ref.dtype)

    kv = pl.BlockSpec((1, n, ATT_HEAD_DIM), lambda h, i: (h, 0, 0))
    return pl.pallas_call(
        body, name=name, grid=(ATT_KV_HEADS, n // tq),
        in_specs=[pl.BlockSpec((tq, gw), lambda h, i: (i, h)), kv, kv],
        out_specs=pl.BlockSpec((tq, gw), lambda h, i: (i, h)), out_shape=SDS((n, ATT_Q_DIM), MXU_DTYPE),
        compiler_params=_cp("parallel", "parallel"))(q, k, v)


def _attn_bwd(q, k, v, o, do, name):
    n = q.shape[0]
    tq = min(ATT_TQ, n)
    scale = ATT_HEAD_DIM ** -0.5
    gw = ATT_GROUP * ATT_HEAD_DIM

    def body(q_ref, k_ref, v_ref, o_ref, do_ref, dq_ref, dk_ref, dv_ref):
        @pl.when(pl.program_id(1) == 0)
        def _():
            dk_ref[...] = jnp.zeros_like(dk_ref)
            dv_ref[...] = jnp.zeros_like(dv_ref)

        kk, vv = k_ref[0], v_ref[0]
        dqs = []
        dk_acc = jnp.zeros((ATT_HEAD_DIM, n), F32)
        dv_acc = jnp.zeros((ATT_HEAD_DIM, n), F32)
        for g in range(ATT_GROUP):
            sl = slice(g * ATT_HEAD_DIM, (g + 1) * ATT_HEAD_DIM)
            qg, dog = q_ref[:, sl] * scale, do_ref[:, sl].astype(F32)
            s = _dot(qg, kk, "nt")
            e = jnp.exp(s - jnp.max(s, axis=-1, keepdims=True))
            inv = 1.0 / jnp.sum(e, axis=-1, keepdims=True)
            delta = jnp.sum(dog * o_ref[:, sl].astype(F32), axis=-1, keepdims=True)
            dse = e * (_dot(dog, vv, "nt") - delta)
            dqs.append(_dot(dse, kk) * (inv * scale))
            dk_acc += _dot(qg.astype(F32) * inv, dse, "tn")
            dv_acc += _dot(dog * inv, e, "tn")
        dq_ref[...] = jnp.concatenate(dqs, axis=-1)
        dk_ref[0] += dk_acc
        dv_ref[0] += dv_acc

    kv = pl.BlockSpec((1, n, ATT_HEAD_DIM), lambda h, i: (h, 0, 0))
    kvt = pl.BlockSpec((1, ATT_HEAD_DIM, n), lambda h, i: (h, 0, 0))
    qb = pl.BlockSpec((tq, gw), lambda h, i: (i, h))
    return pl.pallas_call(
        body, name=name, grid=(ATT_KV_HEADS, n // tq), in_specs=[qb, kv, kv, qb, qb], out_specs=[qb, kvt, kvt],
        out_shape=[SDS((n, ATT_Q_DIM), F32), SDS((ATT_KV_HEADS, ATT_HEAD_DIM, n), F32), SDS((ATT_KV_HEADS, ATT_HEAD_DIM, n), F32)],
        compiler_params=_cp("parallel", "arbitrary"))(q, k, v, o, do)


def _both_directions(mats, axis):
    fwd = np.concatenate(mats, axis=axis).astype(np.float32)
    bwd = np.concatenate([m[::-1, ::-1] for m in mats], axis=axis).astype(np.float32)
    return jnp.asarray(np.stack([fwd, bwd]), MXU_DTYPE)


def _hg_segments():
    c = HG_CHUNK
    t = np.arange(c)[:, None]
    r = np.arange(c)[None, :]
    mats = [(r <= t)]
    for lev in range(HG_LEVELS):
        h = c >> (lev + 1)
        mid = (t // (2 * h)) * (2 * h) + h - 1
        hi = (t // h) % 2 == 1
        mats.append(np.where(hi, (r > mid) & (r <= t), (r > t) & (r <= mid)))
    mats.append(r > t)
    return _both_directions(mats, 0)


def _hg_pair_sums():
    c = HG_CHUNK
    r = np.arange(c)[:, None]
    t = np.arange(c)[None, :]
    gp, gn = [t >= r], [t < r]
    for lev in range(HG_LEVELS):
        sh = HG_LEVELS - 1 - lev
        same = (r >> sh) == (t >> sh)
        gp.append(same & (t >= r))
        gn.append(same & (t < r))
    return _both_directions(gp, 1), _both_directions(gn, 1)


def _split_dot(mat, x):
    hi = x.astype(MXU_DTYPE)
    lo = (x - hi.astype(F32)).astype(MXU_DTYPE)
    return _dot(mat, hi) + _dot(mat, lo)


def _hg_gates(hq, z, a0, a1):
    q = hq * _sigmoid(hq)
    sg = _sigmoid(z)
    lb = _sigmoid(a0 - a1)
    f = lb + (1.0 - lb) * sg
    k = (1.0 - lb) * (1.0 - sg)
    return q, f, k, sg, lb


def _hg_level_masks():
    c = HG_CHUNK
    t = np.arange(c)
    later, same = [], []
    for lev in range(HG_LEVELS):
        sh = HG_LEVELS - 1 - lev
        later.append(np.broadcast_to((((t >> sh) & 1) == 1)[:, None], (c, HG_HEAD_DIM)))
        same.append((t[:, None] >> (sh + 1)) == (t[None, :] >> (sh + 1)))
    same.append(t[:, None] == t[None, :])
    later = np.stack(later).astype(np.float32)
    return jnp.asarray(np.stack([later, 1.0 - later]), F32), jnp.asarray(np.stack(same).astype(np.float32), F32)


def _hg_level(q, k, ex, later_ref, lev):
    e = ex[lev + 1]
    e_q = e * later_ref[0, lev]
    e_k = e - e_q
    return q * e_q, k * e_k, e_q, e_k


def _hg_intra(q, k, ex, later_ref, same_ref):
    a = same_ref[HG_LEVELS] * jnp.sum(q * k, axis=-1, keepdims=True)
    for lev in range(HG_LEVELS):
        qs, ks, _, _ = _hg_level(q, k, ex, later_ref, lev)
        a = a + same_ref[lev] * _dot(qs, ks, "nt")
    return a


def _hg_specs(n, with_time):
    c = HG_CHUNK
    nc = n // c

    def chunk(d, i):
        first = d if with_time else 1 - d
        return i + first * (nc - 1 - 2 * i)

    def pcols(off, dir_stride=0):
        return [pl.BlockSpec((c, HG_PAIR), lambda d, i, j=j: (chunk(d, i), off // HG_PAIR + dir_stride // HG_PAIR * d + j)) for j in range(2)]

    specs = dict(
        hq=pcols(OFF_HQ), v=pcols(OFF_HI), z=pcols(OFF_ZF, OFF_ZB - OFF_ZF),
        shared=pl.BlockSpec((c, HG_DIM), lambda d, i: (chunk(d, i), 0)),
        per_dir=pl.BlockSpec((1, c, HG_DIM), lambda d, i: (d, chunk(d, i), 0)),
        vec=pl.BlockSpec((1, 1, HG_DIM), lambda d, i: (d, 0, 0)),
        seg=pl.BlockSpec((1, (HG_LEVELS + 2) * c, c), lambda d, i: (d, 0, 0)),
        sums=pl.BlockSpec((1, c, (HG_LEVELS + 1) * c), lambda d, i: (d, 0, 0)),
        later=pl.BlockSpec((1, HG_LEVELS, c, HG_HEAD_DIM), lambda d, i: (d, 0, 0, 0)),
        same=pl.BlockSpec((HG_LEVELS + 1, c, c), lambda d, i: (0, 0, 0)),
        state=pl.BlockSpec((1, HG_HEADS, 1, HG_HEAD_DIM, HG_HEAD_DIM), lambda d, i: (d, 0, chunk(d, i), 0, 0)))
    return nc, specs


def _hg_head(refs, hh):
    off = (hh % 2) * HG_HEAD_DIM
    return refs[hh // 2][:, off:off + HG_HEAD_DIM]


def _hg_lanes(hh):
    return slice(hh * HG_HEAD_DIM, (hh + 1) * HG_HEAD_DIM)


def _hg_exps(seg_ref, f):
    lf = jnp.log(f)
    args = _split_dot(seg_ref[0], lf)
    c = HG_CHUNK
    return [jnp.exp(args[j * c:(j + 1) * c]) for j in range(HG_LEVELS + 2)]


def _hg_last_row(a, mirrored):
    return jnp.where(mirrored, a[0:1, :], a[HG_CHUNK - 1:HG_CHUNK, :])


def _hgrn_fwd(p, a0, a1, seg, masks, name):
    n = p.shape[0]
    nc, sp = _hg_specs(n, True)

    def body(hq0, hq1, z0, z1, v0, v1, a0_ref, a1_ref, seg_ref, later_ref, same_ref, o_ref, s0_ref, st):
        @pl.when(pl.program_id(1) == 0)
        def _():
            st[...] = jnp.zeros_like(st)

        mirrored = pl.program_id(0) == 1
        for hh in range(HG_HEADS):
            ln = _hg_lanes(hh)
            q, f, k, _, _ = _hg_gates(_hg_head((hq0, hq1), hh), _hg_head((z0, z1), hh), a0_ref[0, :, ln], a1_ref[0, :, ln])
            vv = _hg_head((v0, v1), hh)
            ex = _hg_exps(seg_ref, f)
            a = _hg_intra(q, k, ex, later_ref, same_ref)
            s_t = st[hh]
            s0_ref[0, hh, 0] = s_t
            o_ref[0, :, ln] = _dot(a, vv) + _dot(q * ex[0], s_t, "nt")
            st[hh] = s_t * _hg_last_row(ex[0], mirrored) + _dot(vv, k * ex[HG_LEVELS + 1], "tn")

    return pl.pallas_call(
        body, name=name, grid=(2, nc), in_specs=sp["hq"] + sp["z"] + sp["v"] + [sp["vec"], sp["vec"], sp["seg"], sp["later"], sp["same"]],
        out_specs=[sp["per_dir"], sp["state"]],
        out_shape=[SDS((2, n, HG_DIM), F32), SDS((2, HG_HEADS, nc, HG_HEAD_DIM, HG_HEAD_DIM), F32)],
        scratch_shapes=[pltpu.VMEM((HG_HEADS, HG_HEAD_DIM, HG_HEAD_DIM), F32)],
        compiler_params=_cp("parallel", "arbitrary"))(p, p, p, p, p, p, a0, a1, seg, *masks)


def _hgrn_bwd(p, a0, a1, seg, masks, gp, gn, do, s0, name):
    n = p.shape[0]
    nc, sp = _hg_specs(n, False)


    def body(hq0, hq1, z0, z1, v0, v1, a0_ref, a1_ref, seg_ref, later_ref, same_ref, gp_ref, gn_ref, do_ref, s0_ref,
             dhq_ref, dz_ref, dv_ref, dlb_ref, rt):
        @pl.when(pl.program_id(1) == 0)
        def _():
            rt[...] = jnp.zeros_like(rt)
            dlb_ref[...] = jnp.zeros_like(dlb_ref)

        mirrored = pl.program_id(0) == 1
        for hh in range(HG_HEADS):
            ln = _hg_lanes(hh)
            hqv = _hg_head((hq0, hq1), hh)
            q, f, k, sg, lb = _hg_gates(hqv, _hg_head((z0, z1), hh), a0_ref[0, :, ln], a1_ref[0, :, ln])
            vv, dov = _hg_head((v0, v1), hh), do_ref[:, ln]
            ex = _hg_exps(seg_ref, f)
            a = _hg_intra(q, k, ex, later_ref, same_ref)
            da = _dot(dov, vv, "nt")
            diag = jnp.sum(dov * vv, axis=-1, keepdims=True)
            s_t = s0_ref[0, hh, 0]
            r_t = rt[hh]
            k_end = k * ex[HG_LEVELS + 1]
            dv_ref[0, :, ln] = _dot(a, dov, "tn") + _dot(k_end, r_t, "nt")
            dq_inter = ex[0] * _dot(dov, s_t)
            dk_inter = ex[HG_LEVELS + 1] * _dot(vv, r_t)
            dq = diag * k + dq_inter
            dk = diag * q + dk_inter
            q_terms, k_terms = [q * dq_inter], [k * dk_inter]
            for lev in range(HG_LEVELS):
                qs, ks, e_q, e_k = _hg_level(q, k, ex, later_ref, lev)
                pairs = da * same_ref[lev]
                q_part = e_q * _dot(pairs, ks)
                k_part = e_k * _dot(pairs, qs, "tn")
                dq, dk = dq + q_part, dk + k_part
                q_terms.append(q * q_part)
                k_terms.append(k * k_part)
            decay = _hg_last_row(ex[0], mirrored)
            rt[hh] = r_t * decay + _dot(dov, q * ex[0], "tn")
            later = decay * jnp.sum(s_t * r_t, axis=0, keepdims=True)
            dlf = _dot(gp_ref[0], jnp.concatenate(q_terms, axis=0)) + _dot(gn_ref[0], jnp.concatenate(k_terms, axis=0)) + later
            df = dlf / f - dk
            dz_ref[0, :, ln] = df * (1.0 - lb) * sg * (1.0 - sg)
            dlb_ref[0, :, ln] += jnp.sum(df * (1.0 - sg), axis=0, keepdims=True)
            sq = _sigmoid(hqv)
            dhq_ref[0, :, ln] = dq * sq * (1.0 + hqv * (1.0 - sq))

    out = SDS((2, n, HG_DIM), F32)
    return pl.pallas_call(
        body, name=name, grid=(2, nc),
        in_specs=sp["hq"] + sp["z"] + sp["v"] + [sp["vec"], sp["vec"], sp["seg"], sp["later"], sp["same"], sp["sums"], sp["sums"],
                                                 sp["shared"], sp["state"]],
        out_specs=[sp["per_dir"], sp["per_dir"], sp["per_dir"], sp["vec"]], out_shape=[out, out, out, SDS((2, 1, HG_DIM), F32)],
        scratch_shapes=[pltpu.VMEM((HG_HEADS, HG_HEAD_DIM, HG_HEAD_DIM), F32)],
        compiler_params=_cp("parallel", "arbitrary"))(p, p, p, p, p, p, a0, a1, seg, *masks, gp, gn, do, s0)


def _hg_post(o2, p, g, name):
    n = p.shape[0]
    tr = min(ROW_TILE, n)
    w = 2 * HG_HEAD_DIM

    def body(of_ref, ob_ref, hg_ref, g_ref, o_ref):
        for j in range(2):
            sl = slice(j * HG_HEAD_DIM, (j + 1) * HG_HEAD_DIM)
            o = of_ref[0, :, sl] + ob_ref[0, :, sl]
            hg = hg_ref[:, sl]
            o_ref[:, sl] = (o * _rstd(o) * g_ref[...] * (hg * _sigmoid(hg))).astype(o_ref.dtype)

    blk = pl.BlockSpec((tr, w), lambda i, j: (i, j))
    dirs = [pl.BlockSpec((1, tr, w), lambda i, j, d=d: (d, i, j)) for d in range(2)]
    return pl.pallas_call(
        body, name=name, grid=(n // tr, HG_DIM // w),
        in_specs=dirs + [pl.BlockSpec((tr, w), lambda i, j: (i, OFF_HG // w + j)), pl.BlockSpec((1, HG_HEAD_DIM), lambda i, j: (0, 0))],
        out_specs=blk, out_shape=SDS((n, HG_DIM), MXU_DTYPE), compiler_params=_cp("parallel", "parallel"))(o2, o2, p, g)


def _hg_post_bwd(o2, p, g, dcat, name, after=()):
    n = p.shape[0]
    tr = min(ROW_TILE, n)
    w = 2 * HG_HEAD_DIM
    after, after_specs = _unread(after)

    def body(of_ref, ob_ref, hg_ref, g_ref, d_ref, *rest):
        do_ref, dhg_ref, dg_ref = rest[len(after):]

        @pl.when(pl.program_id(1) == 0)
        def _():
            dg_ref[...] = jnp.zeros_like(dg_ref)

        for j in range(2):
            sl = slice(j * HG_HEAD_DIM, (j + 1) * HG_HEAD_DIM)
            o = of_ref[0, :, sl] + ob_ref[0, :, sl]
            hg = hg_ref[:, sl]
            d = d_ref[:, sl].astype(F32)
            sg = _sigmoid(hg)
            on = o * _rstd(o) * g_ref[...]
            dhg_ref[:, sl] = (d * on * sg * (1.0 + hg * (1.0 - sg))).astype(dhg_ref.dtype)
            dx, dg = _rms_bwd(o, g_ref[...], d * hg * sg)
            do_ref[:, sl] = dx
            dg_ref[0, :, sl] += dg

    blk = pl.BlockSpec((tr, w), lambda j, i: (i, j))
    dirs = [pl.BlockSpec((1, tr, w), lambda j, i, d=d: (d, i, j)) for d in range(2)]
    return pl.pallas_call(
        body, name=name, grid=(HG_DIM // w, n // tr),
        in_specs=dirs + [pl.BlockSpec((tr, w), lambda j, i: (i, OFF_HG // w + j)), pl.BlockSpec((1, HG_HEAD_DIM), lambda j, i: (0, 0)),
                         pl.BlockSpec((tr, w), lambda j, i: (i, ATT_Q_DIM // w + j))] + after_specs,
        out_specs=[blk, blk, pl.BlockSpec((1, 1, w), lambda j, i: (j, 0, 0))],
        out_shape=[SDS((n, HG_DIM), F32), SDS((n, HG_DIM), MXU_DTYPE), SDS((HG_DIM // w, 1, w), F32)],
        compiler_params=_cp("parallel", "arbitrary"))(o2, o2, p, g, dcat, *after)


XATT_TQ = 512


def _xattn_fwd(q, kv, name):
    n, nm = q.shape[0], kv.shape[0]
    tq = min(XATT_TQ, n)
    scale = X_HEAD_DIM ** -0.5

    def body(q_ref, k_ref, v_ref, o_ref):
        s = _dot(q_ref[...], k_ref[...], "nt") * scale
        e = jnp.exp(s - jnp.max(s, axis=-1, keepdims=True))
        o_ref[...] = _dot(e / jnp.sum(e, axis=-1, keepdims=True), v_ref[...]).astype(o_ref.dtype)

    qb = pl.BlockSpec((tq, X_HEAD_DIM), lambda h, i: (i, h))
    return pl.pallas_call(
        body, name=name, grid=(X_HEADS, n // tq),
        in_specs=[qb, pl.BlockSpec((nm, X_HEAD_DIM), lambda h, i: (0, h)), pl.BlockSpec((nm, X_HEAD_DIM), lambda h, i: (0, X_HEADS + h))],
        out_specs=qb, out_shape=SDS(q.shape, MXU_DTYPE), compiler_params=_cp("parallel", "parallel"))(q, kv, kv)


def _xattn_bwd(q, kv, do, name, after=()):
    n, nm = q.shape[0], kv.shape[0]
    tq = min(XATT_TQ, n)
    scale = X_HEAD_DIM ** -0.5
    after, after_specs = _unread(after)

    def body(q_ref, k_ref, v_ref, do_ref, *rest):
        dq_ref, dk_ref, dv_ref = rest[len(after):]

        @pl.when(pl.program_id(1) == 0)
        def _():
            dk_ref[...] = jnp.zeros_like(dk_ref)
            dv_ref[...] = jnp.zeros_like(dv_ref)

        qv, dov = q_ref[...], do_ref[...]
        s = _dot(qv, k_ref[...], "nt") * scale
        e = jnp.exp(s - jnp.max(s, axis=-1, keepdims=True))
        p = e / jnp.sum(e, axis=-1, keepdims=True)
        dp = _dot(dov, v_ref[...], "nt")
        ds = p * (dp - jnp.sum(p * dp, axis=-1, keepdims=True)) * scale
        dq_ref[...] = _dot(ds, k_ref[...]).astype(dq_ref.dtype)
        dk_ref[...] += _dot(ds, qv, "tn")
        dv_ref[...] += _dot(p, dov, "tn")

    qb = pl.BlockSpec((tq, X_HEAD_DIM), lambda h, i: (i, h))
    kb = pl.BlockSpec((nm, X_HEAD_DIM), lambda h, i: (0, h))
    return pl.pallas_call(
        body, name=name, grid=(X_HEADS, n // tq),
        in_specs=[qb, kb, pl.BlockSpec((nm, X_HEAD_DIM), lambda h, i: (0, X_HEADS + h)), qb] + after_specs, out_specs=[qb, kb, kb],
        out_shape=[SDS(q.shape, MXU_DTYPE), SDS((nm, X_HEADS * X_HEAD_DIM), F32), SDS((nm, X_HEADS * X_HEAD_DIM), F32)],
        compiler_params=_cp("parallel", "arbitrary"))(q, kv, kv, do, *after)


def _edge_rows(shape):
    row = lax.broadcasted_iota(jnp.int32, shape, 0)
    return row == 0, row == shape[0] - 1


def _shift_rows(u, down, edges):
    if down:
        return jnp.where(edges[0], 0.0, pltpu.roll(u, 1, axis=0))
    return jnp.where(edges[1], 0.0, pltpu.roll(u, u.shape[0] - 1, axis=0))


def _conv(u, w, b, edges):
    return b + _shift_rows(u, True, edges) * w[0:1, :] + u * w[1:2, :] + _shift_rows(u, False, edges) * w[2:3, :]


def _ff_specs(n):
    gate = lambda rows: pl.BlockSpec((rows, FF_COLS), lambda j: (0, j))
    val = lambda rows: pl.BlockSpec((rows, FF_COLS), lambda j: (0, FF_BLOCKS + j))
    return [gate(n), val(n), gate(3), val(3), gate(1), val(1)], gate


def _conv_gate(u, cw, cb, name):
    n = u.shape[0]
    ins, gate_blk = _ff_specs(n)

    def body(ug_ref, uv_ref, wg_ref, wv_ref, bg_ref, bv_ref, o_ref):
        edges = _edge_rows(ug_ref.shape)
        gate = _conv(ug_ref[...], wg_ref[...], bg_ref[...], edges)
        val = _conv(uv_ref[...], wv_ref[...], bv_ref[...], edges)
        o_ref[...] = (gate * _sigmoid(gate) * val).astype(o_ref.dtype)

    return pl.pallas_call(
        body, name=name, grid=(FF_BLOCKS,), in_specs=ins, out_specs=gate_blk(n), out_shape=SDS((n, D_FF), MXU_DTYPE),
        compiler_params=_cp("parallel"))(u, u, cw, cw, cb, cb)


def _conv_gate_bwd(u, cw, cb, da, name, after=()):
    n = u.shape[0]
    ins, gate_blk = _ff_specs(n)
    after, after_specs = _unread(after)

    def side(dacc, u, w, edges, du_ref, dw_ref, db_ref):
        nxt, prv = _shift_rows(dacc, False, edges), _shift_rows(dacc, True, edges)
        du_ref[...] = (nxt * w[0:1, :] + dacc * w[1:2, :] + prv * w[2:3, :]).astype(du_ref.dtype)
        db_ref[...] = jnp.sum(dacc, axis=0, keepdims=True)
        dw_ref[0:1, :] = jnp.sum(nxt * u, axis=0, keepdims=True)
        dw_ref[1:2, :] = jnp.sum(dacc * u, axis=0, keepdims=True)
        dw_ref[2:3, :] = jnp.sum(prv * u, axis=0, keepdims=True)

    def body(ug_ref, uv_ref, wg_ref, wv_ref, bg_ref, bv_ref, da_ref, *rest):
        dug_ref, duv_ref, dwg_ref, dwv_ref, dbg_ref, dbv_ref = rest[len(after):]
        ug, uv = ug_ref[...], uv_ref[...]
        edges = _edge_rows(ug.shape)
        gate = _conv(ug, wg_ref[...], bg_ref[...], edges)
        val = _conv(uv, wv_ref[...], bv_ref[...], edges)
        sg = _sigmoid(gate)
        dav = da_ref[...].astype(F32)
        side(dav * val * sg * (1.0 + gate * (1.0 - sg)), ug, wg_ref[...], edges, dug_ref, dwg_ref, dbg_ref)
        side(dav * gate * sg, uv, wv_ref[...], edges, duv_ref, dwv_ref, dbv_ref)

    return pl.pallas_call(
        body, name=name, grid=(FF_BLOCKS,), in_specs=ins + [gate_blk(n)] + after_specs,
        out_specs=[gate_blk(n), gate_blk(n), gate_blk(3), gate_blk(3), gate_blk(1), gate_blk(1)],
        out_shape=[SDS((n, D_FF), MXU_DTYPE)] * 2 + [SDS((3, D_FF), F32)] * 2 + [SDS((1, D_FF), F32)] * 2,
        compiler_params=_cp("parallel"))(u, u, cw, cw, cb, cb, da, *after)


def _adamw(w, g, m, v, name):
    r, c = w.shape[-2:]
    tr = r if r <= 512 else 256 if r % 256 == 0 else 88
    assert r % tr == 0 and (w.ndim == 2 or w.shape[:-2] == (1,)), (name, w.shape, tr)

    def body(w_ref, g_ref, m_ref, v_ref, d_ref, mo_ref, vo_ref, go_ref):
        gv = g_ref[...]
        go_ref[...] = gv
        mn = ADAM_B1 * m_ref[...] + (1.0 - ADAM_B1) * gv
        vn = ADAM_B2 * v_ref[...] + (1.0 - ADAM_B2) * gv * gv
        m_hat = mn / (1.0 - ADAM_B1 ** ADAM_STEP)
        v_hat = vn / (1.0 - ADAM_B2 ** ADAM_STEP)
        d_ref[...] = -ADAM_LR * (m_hat / (jnp.sqrt(v_hat) + ADAM_EPS) + ADAM_WD * w_ref[...])
        mo_ref[...] = mn
        vo_ref[...] = vn

    blk = pl.BlockSpec((tr, c), lambda i: (i, 0)) if w.ndim == 2 else pl.BlockSpec((1, tr, c), lambda i: (0, i, 0))
    out = SDS(w.shape, F32)
    return pl.pallas_call(body, name=name, grid=(r // tr,), in_specs=[blk] * 4, out_specs=[blk] * 4, out_shape=[out] * 4,
                          compiler_params=_cp("parallel"))(w, g, m, v)


def _half_tile(h):
    tr = h if h <= 512 else 256 if h % 256 == 0 else 176
    assert h % tr == 0, (h, tr)
    return tr


ANY = pl.BlockSpec(memory_space=pl.ANY)


def _place():
    x, y, c = lax.axis_index("x"), lax.axis_index("y"), lax.axis_index("c")
    return x, y, c, [(1 - x, y), (x, 1 - y), (1 - x, 1 - y)]


def _gather_shards(shards, name):
    nt = len(shards)

    def body(*refs):
        ins, outs = refs[:nt], refs[nt:2 * nt]
        send, recv, fsend, frecv, osend, orecv = refs[2 * nt:]
        x, y, c, chips = _place()
        me = 2 * x + y

        def half(t, chip, cc):
            h = ins[t].shape[0] // 2
            return outs[t].at[chip, pl.ds(cc * h, h)]

        def ici(t, j):
            cx, cy = chips[j]
            h = ins[t].shape[0] // 2
            return pltpu.make_async_remote_copy(src_ref=ins[t].at[pl.ds(c * h, h)], dst_ref=half(t, me, c),
                                                send_sem=send.at[t, j], recv_sem=recv.at[t, j], device_id=(cx, cy, c), device_id_type=MESH)

        def landed(t, j):
            cx, cy = chips[j]
            blk = half(t, 2 * cx + cy, c)
            return pltpu.make_async_remote_copy(src_ref=blk, dst_ref=blk, send_sem=send.at[t, j], recv_sem=recv.at[t, j],
                                                device_id=(cx, cy, c), device_id_type=MESH)

        def d2d(t, j, cc):
            cx, cy = chips[j]
            blk = half(t, 2 * cx + cy, cc)
            return pltpu.make_async_remote_copy(src_ref=blk, dst_ref=blk, send_sem=fsend.at[t, j], recv_sem=frecv.at[t, j],
                                                device_id=(x, y, 1 - c), device_id_type=MESH)

        own = [pltpu.make_async_remote_copy(src_ref=ins[t], dst_ref=outs[t].at[me], send_sem=osend.at[t], recv_sem=orecv.at[t],
                                            device_id=(x, y, 1 - c), device_id_type=MESH) for t in range(nt)]
        for t in range(nt):
            for j in range(3):
                ici(t, j).start()
        for cp in own:
            cp.start()
        for t in range(nt):
            for j in range(3):
                landed(t, j).wait_recv()
                d2d(t, j, c).start()
        for t in range(nt):
            for j in range(3):
                d2d(t, j, 1 - c).wait_recv()
        for t in range(nt):
            for j in range(3):
                ici(t, j).wait_send()
                d2d(t, j, c).wait_send()
        for cp in own:
            cp.wait()

    return pl.pallas_call(
        body, name=name, in_specs=[ANY] * nt, out_specs=[ANY] * nt,
        out_shape=[SDS((4,) + s.shape, s.dtype) for s in shards],
        scratch_shapes=[pltpu.SemaphoreType.DMA((nt, 3))] * 4 + [pltpu.SemaphoreType.DMA((nt,))] * 2,
        compiler_params=pltpu.CompilerParams(has_side_effects=True))(*shards)


def _join_halves(bufs, name):
    nt = len(bufs)

    def body(*refs):
        outs = refs[nt:2 * nt]
        send, recv = refs[2 * nt:]
        x, y, c, _ = _place()
        cps = [pltpu.make_async_remote_copy(src_ref=outs[t].at[c], dst_ref=outs[t].at[c], send_sem=send.at[t], recv_sem=recv.at[t],
                                            device_id=(x, y, 1 - c), device_id_type=MESH) for t in range(nt)]
        for cp in cps:
            cp.start()
        for t in range(nt):
            theirs = outs[t].at[1 - c]
            pltpu.make_async_remote_copy(src_ref=theirs, dst_ref=theirs, send_sem=send.at[t], recv_sem=recv.at[t],
                                         device_id=(x, y, 1 - c), device_id_type=MESH).wait_recv()
        for cp in cps:
            cp.wait_send()

    return pl.pallas_call(
        body, name=name, in_specs=[ANY] * nt, out_specs=[ANY] * nt, out_shape=[SDS(b.shape, b.dtype) for b in bufs],
        input_output_aliases={t: t for t in range(nt)},
        scratch_shapes=[pltpu.SemaphoreType.DMA((nt,))] * 2,
        compiler_params=pltpu.CompilerParams(has_side_effects=True))(*bufs)


def _exchange_small(v, reduce, name, after=()):
    rows = v.shape[0]
    after, after_specs = _unread(after)

    def body(v_ref, *rest):
        o_ref, buf, send, recv = rest[-4:]
        x, y, c, _ = _place()
        me = 4 * x + 2 * y + c
        buf[me] = v_ref[...]

        def peer(dx, dy, dc):
            return (1 - x if dx else x, 1 - y if dy else y, 1 - c if dc else c)

        peers = [(dx, dy, dc) for dx in range(2) for dy in range(2) for dc in range(2) if (dx, dy, dc) != (0, 0, 0)]
        cps = []
        for j, (dx, dy, dc) in enumerate(peers):
            cps.append(pltpu.make_async_remote_copy(src_ref=v_ref, dst_ref=buf.at[me], send_sem=send.at[j], recv_sem=recv.at[j],
                                                    device_id=peer(dx, dy, dc), device_id_type=MESH))
        for cp in cps:
            cp.start()
        for j, (dx, dy, dc) in enumerate(peers):
            px, py, pc = peer(dx, dy, dc)
            blk = buf.at[4 * px + 2 * py + pc]
            pltpu.make_async_remote_copy(src_ref=blk, dst_ref=blk, send_sem=send.at[j], recv_sem=recv.at[j],
                                         device_id=(px, py, pc), device_id_type=MESH).wait_recv()
        for cp in cps:
            cp.wait_send()
        if reduce:
            acc = buf[0]
            for j in range(1, 8):
                acc = acc + buf[j]
            o_ref[...] = acc
        else:
            o_ref[...] = buf[...]

    vm = pl.BlockSpec(memory_space=pltpu.VMEM)
    return pl.pallas_call(
        body, name=name, in_specs=[vm] + after_specs, out_specs=vm, out_shape=SDS((rows, 128) if reduce else (8, rows, 128), F32),
        scratch_shapes=[pltpu.VMEM((8, rows, 128), F32), pltpu.SemaphoreType.DMA((7,)), pltpu.SemaphoreType.DMA((7,))],
        compiler_params=pltpu.CompilerParams(has_side_effects=True))(v, *after)


HBM = pl.BlockSpec(memory_space=pltpu.HBM)
SEM = pl.BlockSpec(memory_space=pltpu.SEMAPHORE)
TOKEN = pl.BlockSpec(memory_space=pltpu.VMEM)
TOKEN_SHAPE = SDS((8, 128), F32)
PEERS = 7


def _in_hbm(a):
    return pltpu.with_memory_space_constraint(a, pltpu.HBM)


def _split_params():
    return pltpu.CompilerParams(has_side_effects=pltpu.SideEffectType.DATAFLOW_SIDE_EFFECTING)


def _gather_start(shards, name, after=()):
    nt = len(shards)
    after, after_specs = _unread(after)

    def body(*refs):
        ins, lands = refs[:nt], refs[nt:2 * nt]
        outs = refs[2 * nt + len(after):]
        sends, recvs = outs[:nt], outs[nt:2 * nt]
        x, y, c, chips = _place()
        me = 2 * x + y
        for t in range(nt):
            h = ins[t].shape[0] // 2
            mine = pl.ds(c * h, h)
            for j, (cx, cy) in enumerate(chips):
                for dc in range(2):
                    pltpu.make_async_remote_copy(src_ref=ins[t].at[mine], dst_ref=lands[t].at[me, mine], send_sem=sends[t].at[2 * j + dc],
                                                 recv_sem=recvs[t].at[2 * j + c], device_id=(cx, cy, dc), device_id_type=MESH).start()
            pltpu.make_async_remote_copy(src_ref=ins[t], dst_ref=lands[t].at[me], send_sem=sends[t].at[PEERS - 1], recv_sem=recvs[t].at[PEERS - 1],
                                         device_id=(x, y, 1 - c), device_id_type=MESH).start()
        outs[-1][...] = jnp.zeros(TOKEN_SHAPE.shape, F32)

    lands = [lax.empty((4,) + s.shape, s.dtype) for s in shards]
    out = pl.pallas_call(
        body, name=name, in_specs=[HBM] * (2 * nt) + after_specs, out_specs=[SEM] * (2 * nt) + [HBM] * (2 * nt) + [TOKEN],
        out_shape=[pltpu.SemaphoreType.DMA((PEERS,))] * (2 * nt)
        + [pltpu.HBM(s.shape, s.dtype) for s in shards] + [pltpu.HBM(l.shape, l.dtype) for l in lands] + [TOKEN_SHAPE],
        input_output_aliases={t: 2 * nt + t for t in range(2 * nt)}, compiler_params=_split_params())(
            *[_in_hbm(s) for s in shards], *[_in_hbm(l) for l in lands], *after)
    return out[:nt], out[nt:2 * nt], out[2 * nt:3 * nt], out[3 * nt:4 * nt], out[-1]


def _gather_wait(sends, recvs, shards, lands, after, name):
    nt = len(shards)

    def body(*refs):
        ins, lands_ref = refs[:nt], refs[nt:2 * nt]
        send_refs, recv_refs = refs[2 * nt:3 * nt], refs[3 * nt:4 * nt]
        x, y, c, chips = _place()
        for t in range(nt):
            h = ins[t].shape[0] // 2
            for j, (cx, cy) in enumerate(chips):
                for cs in range(2):
                    blk = lands_ref[t].at[2 * cx + cy, pl.ds(cs * h, h)]
                    pltpu.make_async_remote_copy(src_ref=blk, dst_ref=blk, send_sem=send_refs[t].at[2 * j + cs], recv_sem=recv_refs[t].at[2 * j + cs],
                                                 device_id=(cx, cy, cs), device_id_type=MESH).wait()
            blk = lands_ref[t].at[2 * x + y]
            pltpu.make_async_remote_copy(src_ref=blk, dst_ref=blk, send_sem=send_refs[t].at[PEERS - 1], recv_sem=recv_refs[t].at[PEERS - 1],
                                         device_id=(x, y, 1 - c), device_id_type=MESH).wait()

    out = pl.pallas_call(
        body, name=name, in_specs=[HBM] * (2 * nt) + [SEM] * (2 * nt) + [ANY], out_specs=[HBM] * (2 * nt),
        out_shape=[pltpu.HBM(s.shape, s.dtype) for s in shards] + [pltpu.HBM(l.shape, l.dtype) for l in lands],
        input_output_aliases={t: t for t in range(2 * nt)}, compiler_params=_split_params())(*shards, *lands, *sends, *recvs, after)
    return out[nt:]


def _scatter_start(g, name):
    _, r, c_ = g.shape
    h = r // 2

    def body(g_ref, land, send, recv, g_thru, land_thru, token):
        x, y, c, chips = _place()
        for j, (cx, cy) in enumerate(chips):
            for dc in range(2):
                pltpu.make_async_remote_copy(src_ref=g_ref.at[2 * cx + cy, pl.ds(dc * h, h)], dst_ref=land.at[2 * j + c], send_sem=send.at[2 * j + dc],
                                             recv_sem=recv.at[2 * j + c], device_id=(cx, cy, dc), device_id_type=MESH).start()
        pltpu.make_async_remote_copy(src_ref=g_ref.at[2 * x + y, pl.ds((1 - c) * h, h)], dst_ref=land.at[PEERS - 1], send_sem=send.at[PEERS - 1],
                                     recv_sem=recv.at[PEERS - 1], device_id=(x, y, 1 - c), device_id_type=MESH).start()
        token[...] = jnp.zeros(TOKEN_SHAPE.shape, F32)

    land = lax.empty((PEERS, h, c_), g.dtype)
    return pl.pallas_call(
        body, name=name, in_specs=[HBM, HBM], out_specs=[SEM, SEM, HBM, HBM, TOKEN],
        out_shape=[pltpu.SemaphoreType.DMA((PEERS,)), pltpu.SemaphoreType.DMA((PEERS,)), pltpu.HBM(g.shape, g.dtype),
                   pltpu.HBM(land.shape, land.dtype), TOKEN_SHAPE],
        input_output_aliases={0: 2, 1: 3}, compiler_params=_split_params())(_in_hbm(g), _in_hbm(land))


def _scatter_wait(started, after, name):
    nt = len(started)

    def body(*refs):
        lands = refs[nt:2 * nt]
        sends, recvs = refs[2 * nt:3 * nt], refs[3 * nt:4 * nt]
        x, y, c, chips = _place()
        peers = [(cx, cy, dc) for cx, cy in chips for dc in range(2)] + [(x, y, 1 - c)]
        for t in range(nt):
            for k, peer in enumerate(peers):
                blk = lands[t].at[k]
                pltpu.make_async_remote_copy(src_ref=blk, dst_ref=blk, send_sem=sends[t].at[k], recv_sem=recvs[t].at[k],
                                             device_id=peer, device_id_type=MESH).wait()

    gs, lands = [s[2] for s in started], [s[3] for s in started]
    after, after_specs = _unread(after)
    out = pl.pallas_call(
        body, name=name, in_specs=[HBM] * (2 * nt) + [SEM] * (2 * nt) + after_specs, out_specs=[HBM] * (2 * nt),
        out_shape=[pltpu.HBM(a.shape, a.dtype) for a in gs + lands],
        input_output_aliases={t: t for t in range(2 * nt)}, compiler_params=_split_params())(
            *gs, *lands, *[s[0] for s in started], *[s[1] for s in started], *after)
    return out[:nt], out[nt:]


def _sum_devices(g, land, me, core, name):
    npeer, h, c = land.shape
    tr = _half_tile(h)
    steps = h // tr

    def body(ix_ref, own_ref, land_ref, o_ref):
        acc = own_ref[0].astype(F32)
        for j in range(npeer):
            acc = acc + land_ref[j].astype(F32)
        o_ref[0] = acc

    grid_spec = pltpu.PrefetchScalarGridSpec(
        num_scalar_prefetch=1, grid=(steps,),
        in_specs=[pl.BlockSpec((1, tr, c), lambda i, ix: (ix[0], ix[1] * steps + i, 0)), pl.BlockSpec((npeer, tr, c), lambda i, ix: (0, i, 0))],
        out_specs=pl.BlockSpec((1, tr, c), lambda i, ix: (ix[1], i, 0)))
    return pl.pallas_call(body, name=name, grid_spec=grid_spec, out_shape=SDS((2, h, c), F32),
                          compiler_params=_cp("parallel"))(jnp.stack([me, core]), g, land)


def _pack_small(parts):
    flat = jnp.concatenate([p.reshape(-1) for p in parts])
    total = flat.shape[0]
    rows = -(-total // 1024) * 8
    return jnp.pad(flat, (0, rows * 128 - total)).reshape(rows, 128)


def _unpack_small(packed, shapes):
    flat = packed.reshape(-1)
    out, off = [], 0
    for s in shapes:
        size = int(np.prod(s))
        out.append(flat[off:off + size].reshape(s))
        off += size
    return out


def _local_step(x, mem, target, w_in, first_after, mid_weights, ffn_weights, on_grad, gains, conv_w, conv_b, hg_lb):
    n = x.shape[0]
    cos, sin = _rope_tables(n)
    seg = _hg_segments()
    gp, gn = _hg_pair_sums()
    masks = _hg_level_masks()
    gq2 = jnp.tile(gains["q_norm_g"], (1, 2))
    gk2 = jnp.tile(gains["k_norm_g"], (1, 2))
    a0 = hg_lb[:, 0:1, :]
    a1 = hg_lb[:, 1:2, :]

    p, h1 = _norm_mm(x, gains["pre_mix_g"], w_in, F32, TOKEN_TILE, 1664, "in_proj", after=(first_after,))
    qr, kr = _qk_prep(p, gq2, gk2, cos, sin, "qk_prep")
    heads = lambda a: a.reshape(n, ATT_KV_HEADS, ATT_HEAD_DIM).transpose(1, 0, 2)
    kh = heads(kr)
    vh = heads(p[:, OFF_AV:OFF_AV + ATT_KV_DIM].astype(MXU_DTYPE))
    att = _attn_fwd(qr, kh, vh, "attn_fwd")
    o2, s0 = _hgrn_fwd(p, a0, a1, seg, masks, "hgrn_fwd")
    rec = _hg_post(o2, p, gains["hg_out_norm_g"], "hg_post")
    cat = jnp.concatenate([att, rec], axis=1)
    w_out, w_xq, w_xkv, w_xo = mid_weights(cat)
    mixed, x1 = _mm_resid_norm(cat, w_out, x, gains["post_mix_g"], 512, "out_proj_resid")
    xq, h2 = _norm_mm(x1, gains["pre_x_g"], w_xq, MXU_DTYPE, TOKEN_TILE, 1024, "xq_proj")
    kv, mn = _norm_mm(mem, gains["mem_norm_g"], w_xkv, MXU_DTYPE, 256, 2048, "xkv_proj")
    ox = _xattn_fwd(xq, kv, "xattn_fwd")
    xo, x2 = _mm_resid_norm(ox, w_xo, x1, gains["post_x_g"], 512, "xo_proj_resid")
    w_up = ffn_weights("w_up", x2)
    u, h3 = _norm_mm(x2, gains["pre_ffn_g"], w_up, F32, TOKEN_TILE, 1408, "up_proj")
    act = _conv_gate(u, conv_w, conv_b, "conv_gate")
    w_down = ffn_weights("w_down", act)
    dn, d3, loss = _mm_resid_norm(act, w_down, x2, gains["post_ffn_g"], 512, "down_proj_resid_loss", target=target)

    gs = {}
    d_act, d_dn, gs["post_ffn_g"] = _norm_bwd_mm(dn, gains["post_ffn_g"], d3, w_down, F32, 512, 1408, "ffn_post_bwd_down_dx")
    tok = on_grad("w_down", _mm(act, d_dn, "tn", WIRE_DTYPE, 1408, 1024, "down_dw"))
    du_g, du_v, dcw_g, dcw_v, dcb_g, dcb_v = _conv_gate_bwd(u, conv_w, conv_b, d_act, "conv_gate_bwd", after=(tok,))
    gs["conv_w"] = jnp.concatenate([dcw_g, dcw_v], axis=1)
    gs["conv_b"] = jnp.concatenate([dcb_g, dcb_v], axis=1)
    ff_shard = w_up.shape[2]
    g_up = _dw_by_owner(h3, du_g, ff_shard, 0, None, 512, "up_dw_gate")
    tok = on_grad("w_up", _dw_by_owner(h3, du_v, ff_shard, 2, g_up, 512, "up_dw_value"))
    d2, gs["pre_ffn_g"] = _dx_norm_bwd([(du_g, 0), (du_g, 1), (du_v, 0), (du_v, 1)], w_up, x2, gains["pre_ffn_g"], d3, 512,
                                       "up_dx_pre_bwd", after=(tok,))
    d_ox, d_xo, gs["post_x_g"] = _norm_bwd_mm(xo, gains["post_x_g"], d2, w_xo, MXU_DTYPE, 512, 1024, "x_post_bwd_xo_dx")
    tok = on_grad("w_xo", _mm(ox, d_xo, "tn", WIRE_DTYPE, 512, 1024, "xo_dw"))
    d_xq, d_k, d_v = _xattn_bwd(xq, kv, d_ox, "xattn_bwd", after=(tok,))
    d_kv = jnp.concatenate([d_k, d_v], axis=1).astype(MXU_DTYPE)
    tok = on_grad("w_xq", _mm(h2, d_xq, "tn", WIRE_DTYPE, 512, 1024, "xq_dw"))
    tok_kv = on_grad("w_xkv", _dw_by_owner(mn, d_kv, w_xkv.shape[2], 0, None, 512, "xkv_dw"))
    d1, gs["pre_x_g"] = _dx_norm_bwd([(d_xq, 0)], w_xq[None], x1, gains["pre_x_g"], d2, 512, "xq_dx_pre_bwd", after=(tok, tok_kv))
    d_mn = _mm_nt_parts([(d_kv, s) for s in range(4)], w_xkv, F32, 256, 1024, "xkv_dx")
    _, gs["mem_norm_g"] = _norm_bwd(mem, gains["mem_norm_g"], d_mn, None, MXU_DTYPE, "mem_norm_bwd")
    d_cat, d_mixed, gs["post_mix_g"] = _norm_bwd_mm(mixed, gains["post_mix_g"], d1, w_out, MXU_DTYPE, 512, 1024, "mix_post_bwd_out_dx")
    tok = on_grad("w_out", _mm(cat, d_mixed, "tn", WIRE_DTYPE, 512, 1024, "out_dw"))
    d_o, d_hg, dg_hg = _hg_post_bwd(o2, p, gains["hg_out_norm_g"], d_cat, "hg_post_bwd", after=(tok,))
    gs["hg_out_norm_g"] = dg_hg.reshape(HG_HEADS, HG_HEAD_DIM).sum(axis=0, keepdims=True)
    dhq2, dz2, dhv2, dlb = _hgrn_bwd(p, a0, a1, seg, masks, gp, gn, d_o, s0, "hgrn_bwd")
    lb = jax.nn.sigmoid(a0 - a1)
    da0 = dlb * lb * (1.0 - lb)
    gs["hg_lb"] = jnp.concatenate([da0, -da0], axis=1)
    d_qr, d_kh, d_vh = _attn_bwd(qr, kh, vh, cat, d_cat, "attn_bwd")
    unheads = lambda a: a.transpose(2, 0, 1).reshape(n, ATT_KV_DIM)
    d_aq, d_ak, dgq, dgk = _qk_prep_bwd(p, gq2, gk2, cos, sin, d_qr, unheads(d_kh), "qk_prep_bwd")
    gs["q_norm_g"] = dgq.reshape(ATT_HEADS, ATT_HEAD_DIM).sum(axis=0, keepdims=True)
    gs["k_norm_g"] = dgk.reshape(ATT_KV_HEADS, ATT_HEAD_DIM).sum(axis=0, keepdims=True)
    d_p = jnp.concatenate([d_aq, d_ak, unheads(d_vh).astype(MXU_DTYPE), (dhq2[0] + dhq2[1]).astype(MXU_DTYPE),
                           dz2[0].astype(MXU_DTYPE), dz2[1].astype(MXU_DTYPE), (dhv2[0] + dhv2[1]).astype(MXU_DTYPE), d_hg], axis=1)
    tok = on_grad("w_in", _mm(h1, d_p, "tn", WIRE_DTYPE, 512, 1664, "in_dw"))
    grad_x, gs["pre_mix_g"] = _dx_norm_bwd([(d_p, 0)], w_in[None], x, gains["pre_mix_g"], d1, 512, "in_dx_pre_bwd", after=(tok,))
    return loss, grad_x, gs


MATS = ("w_in", "w_out", "w_xq", "w_xkv", "w_xo", "w_up", "w_down")
GAINS = ("pre_mix_g", "q_norm_g", "k_norm_g", "hg_out_norm_g", "post_mix_g", "pre_x_g", "mem_norm_g", "post_x_g", "pre_ffn_g", "post_ffn_g")
WEIGHTS = ('pre_mix_g', 'w_in', 'q_norm_g', 'k_norm_g', 'hg_lb', 'hg_out_norm_g', 'w_out', 'post_mix_g', 'pre_x_g', 'mem_norm_g', 'w_xq',
           'w_xkv', 'w_xo', 'post_x_g', 'pre_ffn_g', 'w_up', 'conv_w', 'conv_b', 'w_down', 'post_ffn_g')


def kernel(x, mem, pre_mix_g, w_in, q_norm_g, k_norm_g, hg_lb, hg_out_norm_g, w_out, post_mix_g, pre_x_g, mem_norm_g, w_xq, w_xkv, w_xo, post_x_g, pre_ffn_g, w_up, conv_w, conv_b, w_down, post_ffn_g, loss_target, m_pre_mix_g, m_w_in, m_q_norm_g, m_k_norm_g, m_hg_lb, m_hg_out_norm_g, m_w_out, m_post_mix_g, m_pre_x_g, m_mem_norm_g, m_w_xq, m_w_xkv, m_w_xo, m_post_x_g, m_pre_ffn_g, m_w_up, m_conv_w, m_conv_b, m_w_down, m_post_ffn_g, v_pre_mix_g, v_w_in, v_q_norm_g, v_k_norm_g, v_hg_lb, v_hg_out_norm_g, v_w_out, v_post_mix_g, v_pre_x_g, v_mem_norm_g, v_w_xq, v_w_xkv, v_w_xo, v_post_x_g, v_pre_ffn_g, v_w_up, v_conv_w, v_conv_b, v_w_down, v_post_ffn_g):
    args = dict(locals())
    w = {k: args[k] for k in WEIGHTS}
    m = {k: args["m_" + k] for k in WEIGHTS}
    v = {k: args["v_" + k] for k in WEIGHTS}
    chip = 2 * lax.axis_index("x") + lax.axis_index("y")
    core = lax.axis_index("c")

    shards = {k: w[k][0].astype(WIRE_DTYPE) for k in MATS}

    def whole(k, g):
        return g if k in ("w_xkv", "w_up") else g.reshape(-1, g.shape[-1])

    w_in_shards = _gather_shards([shards["w_in"]], "gather_w_in")[0]
    w_in_full = jnp.concatenate([w_in_shards[s] for s in range(4)], axis=1)
    small_in = _exchange_small(_pack_small([w["conv_w"][0], w["hg_lb"]]), False, "gather_small")
    mid_names, ffn_names = ("w_out", "w_xq", "w_xkv", "w_xo"), ("w_up", "w_down")
    mid = _gather_start([shards[k] for k in mid_names], "gather_mid_start", after=(w_in_full, small_in))
    ffn = _gather_start([shards[k] for k in ffn_names], "gather_ffn_start", after=(mid[4],))

    def mid_weights(after):
        return [whole(k, g) for k, g in zip(mid_names, _gather_wait(*mid[:4], after, "gather_mid_wait"))]

    def ffn_weights(k, after):
        t = ffn_names.index(k)
        return whole(k, _gather_wait(*[part[t:t + 1] for part in ffn[:4]], after, "gather_wait_" + k)[0])

    cw_parts, lb_parts = [], []
    for s in range(4):
        cw_s, lb_s = _unpack_small(small_in[2 * s], [w["conv_w"][0].shape, w["hg_lb"].shape])
        cw_parts.append(cw_s)
        lb_parts.append(lb_s)
    conv_w_full = jnp.concatenate(cw_parts, axis=1)
    hg_lb_full = jnp.concatenate(lb_parts, axis=2)

    started = {}

    def on_grad(k, g):
        if k == "w_in":
            g = g.reshape(g.shape[0], 4, g.shape[1] // 4).transpose(1, 0, 2)
        elif g.ndim == 2:
            g = g.reshape(4, g.shape[0] // 4, g.shape[1])
        *started[k], token = _scatter_start(g, "grad_start_" + k)
        return token

    gains = {k: w[k] for k in GAINS}
    loss_part, grad_x, gs = _local_step(x[0], mem[0], loss_target[0], w_in_full, ffn[4], mid_weights, ffn_weights, on_grad, gains,
                                        conv_w_full, w["conv_b"], hg_lb_full)
    gs["loss"] = loss_part

    grads, delta, new_m, new_v = {}, {}, {}, {}

    def reduce_matrices(names, after, tag):
        sent, landed = _scatter_wait([started[k] for k in names], after, "grad_wait_" + tag)
        halves = [_sum_devices(g, land, chip, core, "grad_sum_" + k) for k, g, land in zip(names, sent, landed)]
        for k, r in zip(names, _join_halves(halves, "grad_join_" + tag)):
            grads[k] = r.reshape(1, -1, r.shape[-1])

    def adamw(names):
        for k in names:
            shape = w[k].shape
            keep = len(shape) == 3 and shape[0] == 1
            two_d = lambda a: a.reshape(shape) if keep else a.reshape(-1, shape[-1])
            d, mo, vo, go = _adamw(two_d(w[k]), two_d(grads[k]), two_d(m[k]), two_d(v[k]), "adamw_" + k)
            delta[k], new_m[k], new_v[k], grads[k] = d.reshape(shape), mo.reshape(shape), vo.reshape(shape), go.reshape(shape)

    early = tuple(k for k in MATS if k != "w_in")
    reduce_matrices(early, (grad_x,), "early")
    adamw(early)

    small_names = GAINS + ("conv_b", "conv_w", "hg_lb")
    packed = _pack_small([gs[k] for k in small_names + ("loss",)])
    reduced_small = _exchange_small(packed, True, "reduce_small", after=tuple(new_v[k] for k in early))
    *summed, loss = _unpack_small(reduced_small, [gs[k].shape for k in small_names + ("loss",)])
    loss = loss[0, 0]
    for k, g in zip(small_names, summed):
        grads[k] = g
    ncw = w["conv_w"].shape[2]
    grads["conv_w"] = lax.dynamic_slice_in_dim(grads["conv_w"], chip * ncw, ncw, axis=1)[None]
    nlb = w["hg_lb"].shape[2]
    grads["hg_lb"] = lax.dynamic_slice_in_dim(grads["hg_lb"], chip * nlb, nlb, axis=2)
    replicated = GAINS + ("conv_b",)
    shapes = [w[k].shape for k in replicated]
    rows = sum(int(np.prod(s)) for s in shapes) // 128
    pack = lambda d: jnp.concatenate([d[k].reshape(-1) for k in replicated]).reshape(rows, 128)
    outs = _adamw(pack(w), reduced_small[:rows], pack(m), pack(v), "adamw_replicated")
    for into, packed_out in zip((delta, new_m, new_v, grads), outs):
        for k, a in zip(replicated, _unpack_small(packed_out, shapes)):
            into[k] = a
    adamw(("conv_w", "hg_lb"))

    reduce_matrices(("w_in",), tuple(new_v[k] for k in early + small_names), "late")
    adamw(("w_in",))
    return (loss, grad_x[None], *[grads[k] for k in WEIGHTS], *[delta[k] for k in WEIGHTS],
            *[new_m[k] for k in WEIGHTS], *[new_v[k] for k in WEIGHTS])
```

```python
import numpy as np
import jax
import jax.numpy as jnp
from jax import lax
from jax.experimental import pallas as pl
from jax.experimental.pallas import tpu as pltpu

F32 = jnp.float32
MXU_DTYPE = jnp.bfloat16
WIRE_DTYPE = jnp.bfloat16
VMEM_LIMIT_BYTES = 56 * 1024 * 1024
EPS = 1e-6
MESH = pl.DeviceIdType.MESH

GRID_W = 64
ATT_HEADS, ATT_KV_HEADS, ATT_HEAD_DIM = 8, 2, 64
ATT_GROUP = ATT_HEADS // ATT_KV_HEADS
ATT_Q_DIM, ATT_KV_DIM = 512, 128
ROPE_THETA = 10000.0
HG_HEADS, HG_HEAD_DIM, HG_DIM = 4, 128, 512
HG_CHUNK = 128
HG_LEVELS = 7
HG_PAIR = 2 * HG_HEAD_DIM
X_HEADS, X_HEAD_DIM = 4, 256
D_FF = 2816
FF_COLS = 256
FF_BLOCKS = D_FF // FF_COLS
OFF_AK, OFF_AV, OFF_HQ, OFF_ZF, OFF_ZB, OFF_HI, OFF_HG = 512, 640, 768, 1280, 1792, 2304, 2816

ADAM_LR, ADAM_B1, ADAM_B2, ADAM_EPS, ADAM_WD, ADAM_STEP = 0.001, 0.9, 0.999, 1e-08, 0.01, 10

SDS = jax.ShapeDtypeStruct


def _cp(*sem):
    return pltpu.CompilerParams(dimension_semantics=sem, vmem_limit_bytes=VMEM_LIMIT_BYTES)


def _dot(a, b, form="nn"):
    dims = {"nn": (((1,), (0,)), ((), ())), "nt": (((1,), (1,)), ((), ())), "tn": (((0,), (0,)), ((), ()))}[form]
    return lax.dot_general(a.astype(MXU_DTYPE), b.astype(MXU_DTYPE), dims, preferred_element_type=F32)


def _sigmoid(x):
    return 1.0 / (1.0 + jnp.exp(-x))


def _rstd(x):
    return lax.rsqrt(jnp.mean(x * x, axis=-1, keepdims=True) + EPS)


def _rms_bwd(x, g, dy):
    r = _rstd(x)
    xh = x * r
    dn = dy * g
    dx = r * (dn - xh * jnp.mean(dn * xh, axis=-1, keepdims=True))
    return dx, jnp.sum(dy * xh, axis=0, keepdims=True)


def _unread(after):
    after = tuple(a for a in after if a is not None)
    return after, [pl.BlockSpec(memory_space=pl.ANY)] * len(after)


def _mm(a, b, form, out_dtype, tm, tn, name, after=()):
    after, after_specs = _unread(after)
    if form == "nn":
        (m, k), n = a.shape, b.shape[1]
    elif form == "nt":
        (m, k), n = a.shape, b.shape[0]
    else:
        (k, m), n = a.shape, b.shape[1]
    tm, tn = min(tm, m), min(tn, n)
    assert m % tm == 0 and n % tn == 0, (name, m, n, tm, tn)

    def body(a_ref, b_ref, *rest):
        o_ref = rest[-1]
        o_ref[...] = _dot(a_ref[...], b_ref[...], form).astype(o_ref.dtype)

    a_spec = pl.BlockSpec((k, tm), lambda i, j: (0, i)) if form == "tn" else pl.BlockSpec((tm, k), lambda i, j: (i, 0))
    b_spec = pl.BlockSpec((tn, k), lambda i, j: (j, 0)) if form == "nt" else pl.BlockSpec((k, tn), lambda i, j: (0, j))
    return pl.pallas_call(
        body, name=name, grid=(m // tm, n // tn), in_specs=[a_spec, b_spec] + after_specs,
        out_specs=pl.BlockSpec((tm, tn), lambda i, j: (i, j)), out_shape=SDS((m, n), out_dtype),
        compiler_params=_cp("parallel", "parallel"))(a, b, *after)


def _mm_nt_parts(a_parts, b, out_dtype, tm, tn, name, after=()):
    after, after_specs = _unread(after)
    parts, n, p = b.shape
    m = a_parts[0][0].shape[0]
    tm, tn = min(tm, m), min(tn, n)
    assert m % tm == 0 and n % tn == 0 and len(a_parts) == parts, (name, m, b.shape)

    def body(*refs):
        o_ref = refs[-1]
        acc = _dot(refs[0][...], refs[parts][0], "nt")
        for s in range(1, parts):
            acc = acc + _dot(refs[s][...], refs[parts + s][0], "nt")
        o_ref[...] = acc.astype(o_ref.dtype)

    a_specs = [pl.BlockSpec((tm, p), lambda i, j, cb=cb: (i, cb)) for _, cb in a_parts]
    b_specs = [pl.BlockSpec((1, tn, p), lambda i, j, s=s: (s, j, 0)) for s in range(parts)]
    return pl.pallas_call(
        body, name=name, grid=(m // tm, n // tn), in_specs=a_specs + b_specs + after_specs,
        out_specs=pl.BlockSpec((tm, tn), lambda i, j: (i, j)), out_shape=SDS((m, n), out_dtype),
        compiler_params=_cp("parallel", "parallel"))(*[arr for arr, _ in a_parts], *([b] * parts), *after)


def _norm_bwd_mm(y, g, d, w, out_dtype, tm, tn, name):
    n, dm = y.shape
    nn = w.shape[0]
    tm, tn = min(tm, n), min(tn, nn)
    assert n % tm == 0 and nn % tn == 0 and w.shape[1] == dm, (name, y.shape, w.shape)

    def body(y_ref, g_ref, d_ref, w_ref, dx_ref, dy_ref, dg_ref, dys):
        i, j = pl.program_id(0), pl.program_id(1)

        @pl.when(jnp.logical_and(i == 0, j == 0))
        def _():
            dg_ref[...] = jnp.zeros_like(dg_ref)

        @pl.when(j == 0)
        def _():
            dy, dg = _rms_bwd(y_ref[...], g_ref[...], d_ref[...])
            dy = dy.astype(MXU_DTYPE)
            dys[...] = dy
            dy_ref[...] = dy
            dg_ref[...] += dg

        dx_ref[...] = _dot(dys[...], w_ref[...], "nt").astype(dx_ref.dtype)

    row = pl.BlockSpec((tm, dm), lambda i, j: (i, 0))
    vec = pl.BlockSpec((1, dm), lambda i, j: (0, 0))
    return pl.pallas_call(
        body, name=name, grid=(n // tm, nn // tn), in_specs=[row, vec, row, pl.BlockSpec((tn, dm), lambda i, j: (j, 0))],
        out_specs=[pl.BlockSpec((tm, tn), lambda i, j: (i, j)), row, vec],
        out_shape=[SDS((n, nn), out_dtype), SDS((n, dm), MXU_DTYPE), SDS((1, dm), F32)],
        scratch_shapes=[pltpu.VMEM((tm, dm), MXU_DTYPE)],
        compiler_params=_cp("arbitrary", "arbitrary"))(y, g, d, w)


def _mm_resid_norm(a, b, x, g, tm, name, target=None):
    n, k = a.shape
    d = b.shape[1]
    tm = min(tm, n)
    assert n % tm == 0 and x.shape == (n, d), (name, a.shape, b.shape)
    with_loss = target is not None

    def body(a_ref, b_ref, x_ref, g_ref, *rest):
        y = _dot(a_ref[...], b_ref[...])
        out = x_ref[...] + y * _rstd(y) * g_ref[...]
        if not with_loss:
            y_ref, o_ref = rest
            y_ref[...] = y
            o_ref[...] = out
            return
        t_ref, y_ref, d_ref, l_ref = rest
        y_ref[...] = y
        diff = out - t_ref[...]
        d_ref[...] = diff * (1.0 / d)

        @pl.when(pl.program_id(0) == 0)
        def _():
            l_ref[...] = jnp.zeros_like(l_ref)

        l_ref[...] += 0.5 * jnp.sum(jnp.mean(diff * diff, axis=-1, keepdims=True), axis=0, keepdims=True)

    row = pl.BlockSpec((tm, d), lambda i: (i, 0))
    ins = [pl.BlockSpec((tm, k), lambda i: (i, 0)), pl.BlockSpec((k, d), lambda i: (0, 0)), row, pl.BlockSpec((1, d), lambda i: (0, 0))]
    out = SDS((n, d), F32)
    if with_loss:
        return pl.pallas_call(body, name=name, grid=(n // tm,), in_specs=ins + [row], out_specs=[row, row, pl.BlockSpec((1, 1), lambda i: (0, 0))],
                              out_shape=[out, out, SDS((1, 1), F32)], compiler_params=_cp("arbitrary"))(a, b, x, g, target)
    return pl.pallas_call(body, name=name, grid=(n // tm,), in_specs=ins, out_specs=[row, row], out_shape=[out, out],
                          compiler_params=_cp("parallel"))(a, b, x, g)


def _dx_norm_bwd(a_parts, b, x, g, res, tm, name, after=()):
    after, after_specs = _unread(after)
    parts, d, p = b.shape
    n = x.shape[0]
    tm = min(tm, n)
    assert n % tm == 0 and len(a_parts) == parts and x.shape[1] == d, (name, x.shape, b.shape)

    def body(*refs):
        x_ref, g_ref, res_ref = refs[2 * parts:2 * parts + 3]
        dx_ref, dg_ref = refs[-2:]
        dh = _dot(refs[0][...], refs[parts][0], "nt")
        for s in range(1, parts):
            dh = dh + _dot(refs[s][...], refs[parts + s][0], "nt")
        dx, dg = _rms_bwd(x_ref[...], g_ref[...], dh)
        dx_ref[...] = dx + res_ref[...]

        @pl.when(pl.program_id(0) == 0)
        def _():
            dg_ref[...] = jnp.zeros_like(dg_ref)

        dg_ref[...] += dg

    a_specs = [pl.BlockSpec((tm, p), lambda i, cb=cb: (i, cb)) for _, cb in a_parts]
    b_specs = [pl.BlockSpec((1, d, p), lambda i, s=s: (s, 0, 0)) for s in range(parts)]
    row = pl.BlockSpec((tm, d), lambda i: (i, 0))
    vec = pl.BlockSpec((1, d), lambda i: (0, 0))
    return pl.pallas_call(
        body, name=name, grid=(n // tm,), in_specs=a_specs + b_specs + [row, vec, row] + after_specs,
        out_specs=[row, vec], out_shape=[SDS((n, d), F32), SDS((1, d), F32)],
        compiler_params=_cp("arbitrary"))(*[arr for arr, _ in a_parts], *([b] * parts), x, g, res, *after)


def _dw_by_owner(a, b, tn, first, into, tm, name):
    k, m = a.shape
    cnt = b.shape[1] // tn
    tm = min(tm, m)
    assert m % tm == 0 and b.shape[1] == cnt * tn and first + cnt <= 4, (name, a.shape, b.shape)

    def body(a_ref, b_ref, *rest):
        rest[-1][0] = _dot(a_ref[...], b_ref[...], "tn").astype(rest[-1].dtype)

    extra = [] if into is None else [into]
    return pl.pallas_call(
        body, name=name, grid=(m // tm, cnt),
        in_specs=[pl.BlockSpec((k, tm), lambda i, j: (0, i)), pl.BlockSpec((k, tn), lambda i, j: (0, j))] + [pl.BlockSpec(memory_space=pl.ANY)] * len(extra),
        out_specs=pl.BlockSpec((1, tm, tn), lambda i, j: (first + j, i, 0)), out_shape=SDS((4, m, tn), WIRE_DTYPE),
        input_output_aliases={2: 0} if extra else {},
        compiler_params=_cp("parallel", "parallel"))(a, b, *extra)


def _norm_mm(x, g, w, out_dtype, tm, tn, name, after=()):
    after, after_specs = _unread(after)
    m, d = x.shape
    sharded = w.ndim == 3
    n = w.shape[-1] * (w.shape[0] if sharded else 1)
    tm, tn = min(tm, m), (w.shape[-1] if sharded else min(tn, n))
    assert m % tm == 0 and n % tn == 0, (name, m, n, tm, tn)

    def body(x_ref, g_ref, w_ref, *rest):
        o_ref, h_ref, hs = rest[-3:]

        @pl.when(pl.program_id(1) == 0)
        def _():
            xv = x_ref[...]
            h = (xv * _rstd(xv) * g_ref[...]).astype(MXU_DTYPE)
            hs[...] = h
            h_ref[...] = h

        o_ref[...] = _dot(hs[...], w_ref[0] if sharded else w_ref[...]).astype(o_ref.dtype)

    w_spec = pl.BlockSpec((1, d, tn), lambda i, j: (j, 0, 0)) if sharded else pl.BlockSpec((d, tn), lambda i, j: (0, j))
    return pl.pallas_call(
        body, name=name, grid=(m // tm, n // tn),
        in_specs=[pl.BlockSpec((tm, d), lambda i, j: (i, 0)), pl.BlockSpec((1, d), lambda i, j: (0, 0)), w_spec] + after_specs,
        out_specs=[pl.BlockSpec((tm, tn), lambda i, j: (i, j)), pl.BlockSpec((tm, d), lambda i, j: (i, 0))],
        out_shape=[SDS((m, n), out_dtype), SDS((m, d), MXU_DTYPE)],
        scratch_shapes=[pltpu.VMEM((tm, d), MXU_DTYPE)],
        compiler_params=_cp("parallel", "arbitrary"))(x, g, w, *after)


ROW_TILE = 512
TOKEN_TILE = 1024


def _norm_bwd(x, g, dy, res, out_dtype, name):
    n, d = x.shape
    tr = min(ROW_TILE, n)
    has_res = res is not None

    def body(*refs):
        x_ref, g_ref, dy_ref = refs[:3]
        dx_ref, dg_ref = refs[-2:]
        dx, dg = _rms_bwd(x_ref[...], g_ref[...], dy_ref[...].astype(F32))
        if has_res:
            dx = dx + refs[3][...]
        dx_ref[...] = dx.astype(dx_ref.dtype)

        @pl.when(pl.program_id(0) == 0)
        def _():
            dg_ref[...] = jnp.zeros_like(dg_ref)

        dg_ref[...] += dg

    row = pl.BlockSpec((tr, d), lambda i: (i, 0))
    vec = pl.BlockSpec((1, d), lambda i: (0, 0))
    ins = [x, g, dy] + ([res] if has_res else [])
    return pl.pallas_call(
        body, name=name, grid=(n // tr,), in_specs=[row, vec, row] + ([row] if has_res else []),
        out_specs=[row, vec], out_shape=[SDS((n, d), out_dtype), SDS((1, d), F32)],
        compiler_params=_cp("arbitrary"))(*ins)


def _rope_tables(n):
    pairs = ATT_HEAD_DIM // 4
    t = np.arange(n)
    inv = np.power(ROPE_THETA, -np.arange(pairs, dtype=np.float32) / pairs).astype(np.float32)
    ang = np.concatenate([(t // GRID_W)[:, None].astype(np.float32) * inv, (t % GRID_W)[:, None].astype(np.float32) * inv], axis=-1)
    cos = np.repeat(np.cos(ang), 2, axis=-1)
    sin = np.repeat(np.sin(ang), 2, axis=-1) * np.tile(np.array([-1.0, 1.0], np.float32), ATT_HEAD_DIM // 2)
    return jnp.asarray(np.tile(cos, 2), F32), jnp.asarray(np.tile(sin, 2), F32)


def _swap_pairs(x):
    lane = lax.broadcasted_iota(jnp.int32, x.shape, 1)
    return jnp.where((lane & 1) == 0, pltpu.roll(x, 127, axis=1), pltpu.roll(x, 1, axis=1))


def _head_mean(v):
    lane = lax.broadcasted_iota(jnp.int32, v.shape, 1)
    lo = jnp.where(lane < ATT_HEAD_DIM, v, 0.0)
    s0 = jnp.sum(lo, axis=-1, keepdims=True)
    s1 = jnp.sum(v - lo, axis=-1, keepdims=True)
    return jnp.where(lane < ATT_HEAD_DIM, s0, s1) * (1.0 / ATT_HEAD_DIM)


def _qk_prep(p, gq, gk, cos, sin, name):
    n = p.shape[0]
    tr = min(ROW_TILE, n)

    def one(xv, g, c, s):
        xn = xv * lax.rsqrt(_head_mean(xv * xv) + EPS) * g
        return xn * c + _swap_pairs(xn) * s

    def body(q_ref, k_ref, gq_ref, gk_ref, c_ref, s_ref, qo_ref, ko_ref):
        c, s = c_ref[...], s_ref[...]
        for j in range(ATT_Q_DIM // 128):
            qo_ref[:, j * 128:(j + 1) * 128] = one(q_ref[:, j * 128:(j + 1) * 128], gq_ref[...], c, s).astype(qo_ref.dtype)
        ko_ref[...] = one(k_ref[...], gk_ref[...], c, s).astype(ko_ref.dtype)

    vec = pl.BlockSpec((1, 128), lambda i: (0, 0))
    tab = pl.BlockSpec((tr, 128), lambda i: (i, 0))
    return pl.pallas_call(
        body, name=name, grid=(n // tr,),
        in_specs=[pl.BlockSpec((tr, ATT_Q_DIM), lambda i: (i, 0)), pl.BlockSpec((tr, 128), lambda i: (i, OFF_AK // 128)), vec, vec, tab, tab],
        out_specs=[pl.BlockSpec((tr, ATT_Q_DIM), lambda i: (i, 0)), tab],
        out_shape=[SDS((n, ATT_Q_DIM), MXU_DTYPE), SDS((n, ATT_KV_DIM), MXU_DTYPE)],
        compiler_params=_cp("parallel"))(p, p, gq, gk, cos, sin)


def _qk_prep_bwd(p, gq, gk, cos, sin, dq, dk, name):
    n = p.shape[0]
    tr = min(ROW_TILE, n)

    def one(xv, g, c, s, dout):
        dxn = dout * c + _swap_pairs(dout * s)
        r = lax.rsqrt(_head_mean(xv * xv) + EPS)
        xh = xv * r
        dn = dxn * g
        dx = r * (dn - xh * _head_mean(dn * xh))
        return dx, jnp.sum(dxn * xh, axis=0, keepdims=True)

    def body(q_ref, k_ref, gq_ref, gk_ref, c_ref, s_ref, dq_ref, dk_ref, dqo_ref, dko_ref, dgq_ref, dgk_ref):
        @pl.when(pl.program_id(0) == 0)
        def _():
            dgq_ref[...] = jnp.zeros_like(dgq_ref)
            dgk_ref[...] = jnp.zeros_like(dgk_ref)

        c, s = c_ref[...], s_ref[...]
        for j in range(ATT_Q_DIM // 128):
            sl = slice(j * 128, (j + 1) * 128)
            dx, dg = one(q_ref[:, sl], gq_ref[...], c, s, dq_ref[:, sl])
            dqo_ref[:, sl] = dx.astype(dqo_ref.dtype)
            dgq_ref[:, sl] += dg
        dx, dg = one(k_ref[...], gk_ref[...], c, s, dk_ref[...])
        dko_ref[...] = dx.astype(dko_ref.dtype)
        dgk_ref[...] += dg

    vec = pl.BlockSpec((1, 128), lambda i: (0, 0))
    tab = pl.BlockSpec((tr, 128), lambda i: (i, 0))
    qrow = pl.BlockSpec((tr, ATT_Q_DIM), lambda i: (i, 0))
    return pl.pallas_call(
        body, name=name, grid=(n // tr,),
        in_specs=[qrow, pl.BlockSpec((tr, 128), lambda i: (i, OFF_AK // 128)), vec, vec, tab, tab, qrow, tab],
        out_specs=[qrow, tab, pl.BlockSpec((1, ATT_Q_DIM), lambda i: (0, 0)), vec],
        out_shape=[SDS((n, ATT_Q_DIM), MXU_DTYPE), SDS((n, ATT_KV_DIM), MXU_DTYPE), SDS((1, ATT_Q_DIM), F32), SDS((1, 128), F32)],
        compiler_params=_cp("arbitrary"))(p, p, gq, gk, cos, sin, dq, dk)


ATT_TQ = 256


def _attn_fwd(q, k, v, name):
    n = q.shape[0]
    tq = min(ATT_TQ, n)
    scale = ATT_HEAD_DIM ** -0.5
    gw = ATT_GROUP * ATT_HEAD_DIM

    def body(q_ref, k_ref, v_ref, o_ref):
        kk, vv = k_ref[0], v_ref[0]
        v_ones = jnp.concatenate([vv, jnp.ones_like(vv)], axis=1)
        outs = []
        for g in range(ATT_GROUP):
            s = _dot(q_ref[:, g * ATT_HEAD_DIM:(g + 1) * ATT_HEAD_DIM] * scale, kk, "nt")
            e = jnp.exp(s - jnp.max(s, axis=-1, keepdims=True))
            ov = _dot(e, v_ones)
            outs.append(ov[:, :ATT_HEAD_DIM] / ov[:, ATT_HEAD_DIM:])
        o_ref[...] = jnp.concatenate(outs, axis=-1).astype(o_ref.dtype)

    kv = pl.BlockSpec((1, n, ATT_HEAD_DIM), lambda h, i: (h, 0, 0))
    return pl.pallas_call(
        body, name=name, grid=(ATT_KV_HEADS, n // tq),
        in_specs=[pl.BlockSpec((tq, gw), lambda h, i: (i, h)), kv, kv],
        out_specs=pl.BlockSpec((tq, gw), lambda h, i: (i, h)), out_shape=SDS((n, ATT_Q_DIM), MXU_DTYPE),
        compiler_params=_cp("parallel", "parallel"))(q, k, v)


def _attn_bwd(q, k, v, o, do, name):
    n = q.shape[0]
    tq = min(ATT_TQ, n)
    scale = ATT_HEAD_DIM ** -0.5
    gw = ATT_GROUP * ATT_HEAD_DIM

    def body(q_ref, k_ref, v_ref, o_ref, do_ref, dq_ref, dk_ref, dv_ref):
        @pl.when(pl.program_id(1) == 0)
        def _():
            dk_ref[...] = jnp.zeros_like(dk_ref)
            dv_ref[...] = jnp.zeros_like(dv_ref)

        kk, vv = k_ref[0], v_ref[0]
        dqs = []
        dk_acc = jnp.zeros((ATT_HEAD_DIM, n), F32)
        dv_acc = jnp.zeros((ATT_HEAD_DIM, n), F32)
        for g in range(ATT_GROUP):
            sl = slice(g * ATT_HEAD_DIM, (g + 1) * ATT_HEAD_DIM)
            qg, dog = q_ref[:, sl] * scale, do_ref[:, sl].astype(F32)
            s = _dot(qg, kk, "nt")
            e = jnp.exp(s - jnp.max(s, axis=-1, keepdims=True))
            inv = 1.0 / jnp.sum(e, axis=-1, keepdims=True)
            delta = jnp.sum(dog * o_ref[:, sl].astype(F32), axis=-1, keepdims=True)
            dse = e * (_dot(dog, vv, "nt") - delta)
            dqs.append(_dot(dse, kk) * (inv * scale))
            dk_acc += _dot(qg.astype(F32) * inv, dse, "tn")
            dv_acc += _dot(dog * inv, e, "tn")
        dq_ref[...] = jnp.concatenate(dqs, axis=-1)
        dk_ref[0] += dk_acc
        dv_ref[0] += dv_acc

    kv = pl.BlockSpec((1, n, ATT_HEAD_DIM), lambda h, i: (h, 0, 0))
    kvt = pl.BlockSpec((1, ATT_HEAD_DIM, n), lambda h, i: (h, 0, 0))
    qb = pl.BlockSpec((tq, gw), lambda h, i: (i, h))
    return pl.pallas_call(
        body, name=name, grid=(ATT_KV_HEADS, n // tq), in_specs=[qb, kv, kv, qb, qb], out_specs=[qb, kvt, kvt],
        out_shape=[SDS((n, ATT_Q_DIM), F32), SDS((ATT_KV_HEADS, ATT_HEAD_DIM, n), F32), SDS((ATT_KV_HEADS, ATT_HEAD_DIM, n), F32)],
        compiler_params=_cp("parallel", "arbitrary"))(q, k, v, o, do)


def _both_directions(mats, axis):
    fwd = np.concatenate(mats, axis=axis).astype(np.float32)
    bwd = np.concatenate([m[::-1, ::-1] for m in mats], axis=axis).astype(np.float32)
    return jnp.asarray(np.stack([fwd, bwd]), MXU_DTYPE)


def _hg_segments():
    c = HG_CHUNK
    t = np.arange(c)[:, None]
    r = np.arange(c)[None, :]
    mats = [(r <= t)]
    for lev in range(HG_LEVELS):
        h = c >> (lev + 1)
        mid = (t // (2 * h)) * (2 * h) + h - 1
        hi = (t // h) % 2 == 1
        mats.append(np.where(hi, (r > mid) & (r <= t), (r > t) & (r <= mid)))
    mats.append(r > t)
    return _both_directions(mats, 0)


def _hg_pair_sums():
    c = HG_CHUNK
    r = np.arange(c)[:, None]
    t = np.arange(c)[None, :]
    gp, gn = [t >= r], [t < r]
    for lev in range(HG_LEVELS):
        sh = HG_LEVELS - 1 - lev
        same = (r >> sh) == (t >> sh)
        gp.append(same & (t >= r))
        gn.append(same & (t < r))
    return _both_directions(gp, 1), _both_directions(gn, 1)


def _split_dot(mat, x):
    hi = x.astype(MXU_DTYPE)
    lo = (x - hi.astype(F32)).astype(MXU_DTYPE)
    return _dot(mat, hi) + _dot(mat, lo)


def _hg_gates(hq, z, a0, a1):
    q = hq * _sigmoid(hq)
    sg = _sigmoid(z)
    lb = _sigmoid(a0 - a1)
    f = lb + (1.0 - lb) * sg
    k = (1.0 - lb) * (1.0 - sg)
    return q, f, k, sg, lb


def _hg_level_masks():
    c = HG_CHUNK
    t = np.arange(c)
    later, same = [], []
    for lev in range(HG_LEVELS):
        sh = HG_LEVELS - 1 - lev
        later.append(np.broadcast_to((((t >> sh) & 1) == 1)[:, None], (c, HG_HEAD_DIM)))
        same.append((t[:, None] >> (sh + 1)) == (t[None, :] >> (sh + 1)))
    same.append(t[:, None] == t[None, :])
    later = np.stack(later).astype(np.float32)
    return jnp.asarray(np.stack([later, 1.0 - later]), F32), jnp.asarray(np.stack(same).astype(np.float32), F32)


def _hg_level(q, k, ex, later_ref, lev):
    e = ex[lev + 1]
    e_q = e * later_ref[0, lev]
    e_k = e - e_q
    return q * e_q, k * e_k, e_q, e_k


def _hg_intra(q, k, ex, later_ref, same_ref):
    a = same_ref[HG_LEVELS] * jnp.sum(q * k, axis=-1, keepdims=True)
    for lev in range(HG_LEVELS):
        qs, ks, _, _ = _hg_level(q, k, ex, later_ref, lev)
        a = a + same_ref[lev] * _dot(qs, ks, "nt")
    return a


def _hg_specs(n, with_time):
    c = HG_CHUNK
    nc = n // c

    def chunk(d, i):
        first = d if with_time else 1 - d
        return i + first * (nc - 1 - 2 * i)

    def pcols(off, dir_stride=0):
        return [pl.BlockSpec((c, HG_PAIR), lambda d, i, j=j: (chunk(d, i), off // HG_PAIR + dir_stride // HG_PAIR * d + j)) for j in range(2)]

    specs = dict(
        hq=pcols(OFF_HQ), v=pcols(OFF_HI), z=pcols(OFF_ZF, OFF_ZB - OFF_ZF),
        shared=pl.BlockSpec((c, HG_DIM), lambda d, i: (chunk(d, i), 0)),
        per_dir=pl.BlockSpec((1, c, HG_DIM), lambda d, i: (d, chunk(d, i), 0)),
        vec=pl.BlockSpec((1, 1, HG_DIM), lambda d, i: (d, 0, 0)),
        seg=pl.BlockSpec((1, (HG_LEVELS + 2) * c, c), lambda d, i: (d, 0, 0)),
        sums=pl.BlockSpec((1, c, (HG_LEVELS + 1) * c), lambda d, i: (d, 0, 0)),
        later=pl.BlockSpec((1, HG_LEVELS, c, HG_HEAD_DIM), lambda d, i: (d, 0, 0, 0)),
        same=pl.BlockSpec((HG_LEVELS + 1, c, c), lambda d, i: (0, 0, 0)),
        state=pl.BlockSpec((1, HG_HEADS, 1, HG_HEAD_DIM, HG_HEAD_DIM), lambda d, i: (d, 0, chunk(d, i), 0, 0)),
        weights=pl.BlockSpec((1, HG_HEADS, 1, c, c), lambda d, i: (d, 0, chunk(d, i), 0, 0)))
    return nc, specs


def _hg_head(refs, hh):
    off = (hh % 2) * HG_HEAD_DIM
    return refs[hh // 2][:, off:off + HG_HEAD_DIM]


def _hg_lanes(hh):
    return slice(hh * HG_HEAD_DIM, (hh + 1) * HG_HEAD_DIM)


def _hg_exps(seg_ref, f):
    lf = jnp.log(f)
    args = _split_dot(seg_ref[0], lf)
    c = HG_CHUNK
    return [jnp.exp(args[j * c:(j + 1) * c]) for j in range(HG_LEVELS + 2)]


def _hg_last_row(a, mirrored):
    return jnp.where(mirrored, a[0:1, :], a[HG_CHUNK - 1:HG_CHUNK, :])


def _hgrn_fwd(p, a0, a1, seg, masks, name):
    n = p.shape[0]
    nc, sp = _hg_specs(n, True)

    def body(hq0, hq1, z0, z1, v0, v1, a0_ref, a1_ref, seg_ref, later_ref, same_ref, o_ref, s0_ref, a_ref, st):
        @pl.when(pl.program_id(1) == 0)
        def _():
            st[...] = jnp.zeros_like(st)

        mirrored = pl.program_id(0) == 1
        for hh in range(HG_HEADS):
            ln = _hg_lanes(hh)
            q, f, k, _, _ = _hg_gates(_hg_head((hq0, hq1), hh), _hg_head((z0, z1), hh), a0_ref[0, :, ln], a1_ref[0, :, ln])
            vv = _hg_head((v0, v1), hh)
            ex = _hg_exps(seg_ref, f)
            a = _hg_intra(q, k, ex, later_ref, same_ref).astype(MXU_DTYPE)
            a_ref[0, hh, 0] = a
            s_t = st[hh]
            s0_ref[0, hh, 0] = s_t
            o_ref[0, :, ln] = _dot(a, vv) + _dot(q * ex[0], s_t, "nt")
            st[hh] = s_t * _hg_last_row(ex[0], mirrored) + _dot(vv, k * ex[HG_LEVELS + 1], "tn")

    return pl.pallas_call(
        body, name=name, grid=(2, nc), in_specs=sp["hq"] + sp["z"] + sp["v"] + [sp["vec"], sp["vec"], sp["seg"], sp["later"], sp["same"]],
        out_specs=[sp["per_dir"], sp["state"], sp["weights"]],
        out_shape=[SDS((2, n, HG_DIM), F32), SDS((2, HG_HEADS, nc, HG_HEAD_DIM, HG_HEAD_DIM), F32),
                   SDS((2, HG_HEADS, nc, HG_CHUNK, HG_CHUNK), MXU_DTYPE)],
        scratch_shapes=[pltpu.VMEM((HG_HEADS, HG_HEAD_DIM, HG_HEAD_DIM), F32)],
        compiler_params=_cp("parallel", "arbitrary"))(p, p, p, p, p, p, a0, a1, seg, *masks)


def _hgrn_bwd(p, a0, a1, seg, masks, gp, gn, do, s0, a, name):
    n = p.shape[0]
    nc, sp = _hg_specs(n, False)


    def body(hq0, hq1, z0, z1, v0, v1, a0_ref, a1_ref, seg_ref, later_ref, same_ref, gp_ref, gn_ref, do_ref, s0_ref, a_ref,
             dhq_ref, dz_ref, dv_ref, dlb_ref, rt):
        @pl.when(pl.program_id(1) == 0)
        def _():
            rt[...] = jnp.zeros_like(rt)
            dlb_ref[...] = jnp.zeros_like(dlb_ref)

        mirrored = pl.program_id(0) == 1
        for hh in range(HG_HEADS):
            ln = _hg_lanes(hh)
            hqv = _hg_head((hq0, hq1), hh)
            q, f, k, sg, lb = _hg_gates(hqv, _hg_head((z0, z1), hh), a0_ref[0, :, ln], a1_ref[0, :, ln])
            vv, dov = _hg_head((v0, v1), hh), do_ref[:, ln]
            ex = _hg_exps(seg_ref, f)
            a = a_ref[0, hh, 0]
            da = _dot(dov, vv, "nt")
            diag = jnp.sum(dov * vv, axis=-1, keepdims=True)
            s_t = s0_ref[0, hh, 0]
            r_t = rt[hh]
            k_end = k * ex[HG_LEVELS + 1]
            dv_ref[0, :, ln] = _dot(a, dov, "tn") + _dot(k_end, r_t, "nt")
            dq_inter = ex[0] * _dot(dov, s_t)
            dk_inter = ex[HG_LEVELS + 1] * _dot(vv, r_t)
            dq = diag * k + dq_inter
            dk = diag * q + dk_inter
            q_terms, k_terms = [q * dq_inter], [k * dk_inter]
            for lev in range(HG_LEVELS):
                qs, ks, e_q, e_k = _hg_level(q, k, ex, later_ref, lev)
                pairs = da * same_ref[lev]
                q_part = e_q * _dot(pairs, ks)
                k_part = e_k * _dot(pairs, qs, "tn")
                dq, dk = dq + q_part, dk + k_part
                q_terms.append(q * q_part)
                k_terms.append(k * k_part)
            decay = _hg_last_row(ex[0], mirrored)
            rt[hh] = r_t * decay + _dot(dov, q * ex[0], "tn")
            later = decay * jnp.sum(s_t * r_t, axis=0, keepdims=True)
            dlf = _dot(gp_ref[0], jnp.concatenate(q_terms, axis=0)) + _dot(gn_ref[0], jnp.concatenate(k_terms, axis=0)) + later
            df = dlf / f - dk
            dz_ref[0, :, ln] = df * (1.0 - lb) * sg * (1.0 - sg)
            dlb_ref[0, :, ln] += jnp.sum(df * (1.0 - sg), axis=0, keepdims=True)
            sq = _sigmoid(hqv)
            dhq_ref[0, :, ln] = dq * sq * (1.0 + hqv * (1.0 - sq))

    out = SDS((2, n, HG_DIM), F32)
    return pl.pallas_call(
        body, name=name, grid=(2, nc),
        in_specs=sp["hq"] + sp["z"] + sp["v"] + [sp["vec"], sp["vec"], sp["seg"], sp["later"], sp["same"], sp["sums"], sp["sums"],
                                                 sp["shared"], sp["state"], sp["weights"]],
        out_specs=[sp["per_dir"], sp["per_dir"], sp["per_dir"], sp["vec"]], out_shape=[out, out, out, SDS((2, 1, HG_DIM), F32)],
        scratch_shapes=[pltpu.VMEM((HG_HEADS, HG_HEAD_DIM, HG_HEAD_DIM), F32)],
        compiler_params=_cp("parallel", "arbitrary"))(p, p, p, p, p, p, a0, a1, seg, *masks, gp, gn, do, s0, a)


def _hg_post(o2, p, g, name):
    n = p.shape[0]
    tr = min(ROW_TILE, n)
    w = 2 * HG_HEAD_DIM

    def body(of_ref, ob_ref, hg_ref, g_ref, o_ref):
        for j in range(2):
            sl = slice(j * HG_HEAD_DIM, (j + 1) * HG_HEAD_DIM)
            o = of_ref[0, :, sl] + ob_ref[0, :, sl]
            hg = hg_ref[:, sl]
            o_ref[:, sl] = (o * _rstd(o) * g_ref[...] * (hg * _sigmoid(hg))).astype(o_ref.dtype)

    blk = pl.BlockSpec((tr, w), lambda i, j: (i, j))
    dirs = [pl.BlockSpec((1, tr, w), lambda i, j, d=d: (d, i, j)) for d in range(2)]
    return pl.pallas_call(
        body, name=name, grid=(n // tr, HG_DIM // w),
        in_specs=dirs + [pl.BlockSpec((tr, w), lambda i, j: (i, OFF_HG // w + j)), pl.BlockSpec((1, HG_HEAD_DIM), lambda i, j: (0, 0))],
        out_specs=blk, out_shape=SDS((n, HG_DIM), MXU_DTYPE), compiler_params=_cp("parallel", "parallel"))(o2, o2, p, g)


def _hg_post_bwd(o2, p, g, dcat, name, after=()):
    n = p.shape[0]
    tr = min(ROW_TILE, n)
    w = 2 * HG_HEAD_DIM
    after, after_specs = _unread(after)

    def body(of_ref, ob_ref, hg_ref, g_ref, d_ref, *rest):
        do_ref, dhg_ref, dg_ref = rest[len(after):]

        @pl.when(pl.program_id(1) == 0)
        def _():
            dg_ref[...] = jnp.zeros_like(dg_ref)

        for j in range(2):
            sl = slice(j * HG_HEAD_DIM, (j + 1) * HG_HEAD_DIM)
            o = of_ref[0, :, sl] + ob_ref[0, :, sl]
            hg = hg_ref[:, sl]
            d = d_ref[:, sl].astype(F32)
            sg = _sigmoid(hg)
            on = o * _rstd(o) * g_ref[...]
            dhg_ref[:, sl] = (d * on * sg * (1.0 + hg * (1.0 - sg))).astype(dhg_ref.dtype)
            dx, dg = _rms_bwd(o, g_ref[...], d * hg * sg)
            do_ref[:, sl] = dx
            dg_ref[0, :, sl] += dg

    blk = pl.BlockSpec((tr, w), lambda j, i: (i, j))
    dirs = [pl.BlockSpec((1, tr, w), lambda j, i, d=d: (d, i, j)) for d in range(2)]
    return pl.pallas_call(
        body, name=name, grid=(HG_DIM // w, n // tr),
        in_specs=dirs + [pl.BlockSpec((tr, w), lambda j, i: (i, OFF_HG // w + j)), pl.BlockSpec((1, HG_HEAD_DIM), lambda j, i: (0, 0)),
                         pl.BlockSpec((tr, w), lambda j, i: (i, ATT_Q_DIM // w + j))] + after_specs,
        out_specs=[blk, blk, pl.BlockSpec((1, 1, w), lambda j, i: (j, 0, 0))],
        out_shape=[SDS((n, HG_DIM), F32), SDS((n, HG_DIM), MXU_DTYPE), SDS((HG_DIM // w, 1, w), F32)],
        compiler_params=_cp("parallel", "arbitrary"))(o2, o2, p, g, dcat, *after)


XATT_TQ = 512


def _xattn_fwd(q, kv, name):
    n, nm = q.shape[0], kv.shape[0]
    tq = min(XATT_TQ, n)
    scale = X_HEAD_DIM ** -0.5

    def body(q_ref, k_ref, v_ref, o_ref):
        s = _dot(q_ref[...], k_ref[...], "nt") * scale
        e = jnp.exp(s - jnp.max(s, axis=-1, keepdims=True))
        o_ref[...] = _dot(e / jnp.sum(e, axis=-1, keepdims=True), v_ref[...]).astype(o_ref.dtype)

    qb = pl.BlockSpec((tq, X_HEAD_DIM), lambda h, i: (i, h))
    return pl.pallas_call(
        body, name=name, grid=(X_HEADS, n // tq),
        in_specs=[qb, pl.BlockSpec((nm, X_HEAD_DIM), lambda h, i: (0, h)), pl.BlockSpec((nm, X_HEAD_DIM), lambda h, i: (0, X_HEADS + h))],
        out_specs=qb, out_shape=SDS(q.shape, MXU_DTYPE), compiler_params=_cp("parallel", "parallel"))(q, kv, kv)


def _xattn_bwd(q, kv, do, name, after=()):
    n, nm = q.shape[0], kv.shape[0]
    tq = min(XATT_TQ, n)
    scale = X_HEAD_DIM ** -0.5
    after, after_specs = _unread(after)

    def body(q_ref, k_ref, v_ref, do_ref, *rest):
        dq_ref, dk_ref, dv_ref = rest[len(after):]

        @pl.when(pl.program_id(1) == 0)
        def _():
            dk_ref[...] = jnp.zeros_like(dk_ref)
            dv_ref[...] = jnp.zeros_like(dv_ref)

        qv, dov = q_ref[...], do_ref[...]
        s = _dot(qv, k_ref[...], "nt") * scale
        e = jnp.exp(s - jnp.max(s, axis=-1, keepdims=True))
        p = e / jnp.sum(e, axis=-1, keepdims=True)
        dp = _dot(dov, v_ref[...], "nt")
        ds = p * (dp - jnp.sum(p * dp, axis=-1, keepdims=True)) * scale
        dq_ref[...] = _dot(ds, k_ref[...]).astype(dq_ref.dtype)
        dk_ref[...] += _dot(ds, qv, "tn")
        dv_ref[...] += _dot(p, dov, "tn")

    qb = pl.BlockSpec((tq, X_HEAD_DIM), lambda h, i: (i, h))
    kb = pl.BlockSpec((nm, X_HEAD_DIM), lambda h, i: (0, h))
    return pl.pallas_call(
        body, name=name, grid=(X_HEADS, n // tq),
        in_specs=[qb, kb, pl.BlockSpec((nm, X_HEAD_DIM), lambda h, i: (0, X_HEADS + h)), qb] + after_specs, out_specs=[qb, kb, kb],
        out_shape=[SDS(q.shape, MXU_DTYPE), SDS((nm, X_HEADS * X_HEAD_DIM), F32), SDS((nm, X_HEADS * X_HEAD_DIM), F32)],
        compiler_params=_cp("parallel", "arbitrary"))(q, kv, kv, do, *after)


def _edge_rows(shape):
    row = lax.broadcasted_iota(jnp.int32, shape, 0)
    return row == 0, row == shape[0] - 1


def _shift_rows(u, down, edges):
    if down:
        return jnp.where(edges[0], 0.0, pltpu.roll(u, 1, axis=0))
    return jnp.where(edges[1], 0.0, pltpu.roll(u, u.shape[0] - 1, axis=0))


def _conv(u, w, b, edges):
    return b + _shift_rows(u, True, edges) * w[0:1, :] + u * w[1:2, :] + _shift_rows(u, False, edges) * w[2:3, :]


def _ff_specs(n):
    gate = lambda rows: pl.BlockSpec((rows, FF_COLS), lambda j: (0, j))
    val = lambda rows: pl.BlockSpec((rows, FF_COLS), lambda j: (0, FF_BLOCKS + j))
    return [gate(n), val(n), gate(3), val(3), gate(1), val(1)], gate


def _conv_gate(u, cw, cb, name):
    n = u.shape[0]
    ins, gate_blk = _ff_specs(n)

    def body(ug_ref, uv_ref, wg_ref, wv_ref, bg_ref, bv_ref, o_ref):
        edges = _edge_rows(ug_ref.shape)
        gate = _conv(ug_ref[...], wg_ref[...], bg_ref[...], edges)
        val = _conv(uv_ref[...], wv_ref[...], bv_ref[...], edges)
        o_ref[...] = (gate * _sigmoid(gate) * val).astype(o_ref.dtype)

    return pl.pallas_call(
        body, name=name, grid=(FF_BLOCKS,), in_specs=ins, out_specs=gate_blk(n), out_shape=SDS((n, D_FF), MXU_DTYPE),
        compiler_params=_cp("parallel"))(u, u, cw, cw, cb, cb)


def _conv_gate_bwd(u, cw, cb, da, name, after=()):
    n = u.shape[0]
    ins, gate_blk = _ff_specs(n)
    after, after_specs = _unread(after)

    def side(dacc, u, w, edges, du_ref, dw_ref, db_ref):
        nxt, prv = _shift_rows(dacc, False, edges), _shift_rows(dacc, True, edges)
        du_ref[...] = (nxt * w[0:1, :] + dacc * w[1:2, :] + prv * w[2:3, :]).astype(du_ref.dtype)
        db_ref[...] = jnp.sum(dacc, axis=0, keepdims=True)
        dw_ref[0:1, :] = jnp.sum(nxt * u, axis=0, keepdims=True)
        dw_ref[1:2, :] = jnp.sum(dacc * u, axis=0, keepdims=True)
        dw_ref[2:3, :] = jnp.sum(prv * u, axis=0, keepdims=True)

    def body(ug_ref, uv_ref, wg_ref, wv_ref, bg_ref, bv_ref, da_ref, *rest):
        dug_ref, duv_ref, dwg_ref, dwv_ref, dbg_ref, dbv_ref = rest[len(after):]
        ug, uv = ug_ref[...], uv_ref[...]
        edges = _edge_rows(ug.shape)
        gate = _conv(ug, wg_ref[...], bg_ref[...], edges)
        val = _conv(uv, wv_ref[...], bv_ref[...], edges)
        sg = _sigmoid(gate)
        dav = da_ref[...].astype(F32)
        side(dav * val * sg * (1.0 + gate * (1.0 - sg)), ug, wg_ref[...], edges, dug_ref, dwg_ref, dbg_ref)
        side(dav * gate * sg, uv, wv_ref[...], edges, duv_ref, dwv_ref, dbv_ref)

    return pl.pallas_call(
        body, name=name, grid=(FF_BLOCKS,), in_specs=ins + [gate_blk(n)] + after_specs,
        out_specs=[gate_blk(n), gate_blk(n), gate_blk(3), gate_blk(3), gate_blk(1), gate_blk(1)],
        out_shape=[SDS((n, D_FF), MXU_DTYPE)] * 2 + [SDS((3, D_FF), F32)] * 2 + [SDS((1, D_FF), F32)] * 2,
        compiler_params=_cp("parallel"))(u, u, cw, cw, cb, cb, da, *after)


def _adamw(w, g, m, v, name):
    r, c = w.shape[-2:]
    tr = r if r <= 512 else 256 if r % 256 == 0 else 88
    assert r % tr == 0 and (w.ndim == 2 or w.shape[:-2] == (1,)), (name, w.shape, tr)

    def body(w_ref, g_ref, m_ref, v_ref, d_ref, mo_ref, vo_ref, go_ref):
        gv = g_ref[...]
        go_ref[...] = gv
        mn = ADAM_B1 * m_ref[...] + (1.0 - ADAM_B1) * gv
        vn = ADAM_B2 * v_ref[...] + (1.0 - ADAM_B2) * gv * gv
        m_hat = mn / (1.0 - ADAM_B1 ** ADAM_STEP)
        v_hat = vn / (1.0 - ADAM_B2 ** ADAM_STEP)
        d_ref[...] = -ADAM_LR * (m_hat / (jnp.sqrt(v_hat) + ADAM_EPS) + ADAM_WD * w_ref[...])
        mo_ref[...] = mn
        vo_ref[...] = vn

    blk = pl.BlockSpec((tr, c), lambda i: (i, 0)) if w.ndim == 2 else pl.BlockSpec((1, tr, c), lambda i: (0, i, 0))
    out = SDS(w.shape, F32)
    return pl.pallas_call(body, name=name, grid=(r // tr,), in_specs=[blk] * 4, out_specs=[blk] * 4, out_shape=[out] * 4,
                          compiler_params=_cp("parallel"))(w, g, m, v)


def _half_tile(h):
    tr = h if h <= 512 else 256 if h % 256 == 0 else 176
    assert h % tr == 0, (h, tr)
    return tr


ANY = pl.BlockSpec(memory_space=pl.ANY)


def _place():
    x, y, c = lax.axis_index("x"), lax.axis_index("y"), lax.axis_index("c")
    return x, y, c, [(1 - x, y), (x, 1 - y), (1 - x, 1 - y)]


def _gather_shards(shards, name):
    nt = len(shards)

    def body(*refs):
        ins, outs = refs[:nt], refs[nt:2 * nt]
        send, recv, fsend, frecv, osend, orecv = refs[2 * nt:]
        x, y, c, chips = _place()
        me = 2 * x + y

        def half(t, chip, cc):
            h = ins[t].shape[0] // 2
            return outs[t].at[chip, pl.ds(cc * h, h)]

        def ici(t, j):
            cx, cy = chips[j]
            h = ins[t].shape[0] // 2
            return pltpu.make_async_remote_copy(src_ref=ins[t].at[pl.ds(c * h, h)], dst_ref=half(t, me, c),
                                                send_sem=send.at[t, j], recv_sem=recv.at[t, j], device_id=(cx, cy, c), device_id_type=MESH)

        def landed(t, j):
            cx, cy = chips[j]
            blk = half(t, 2 * cx + cy, c)
            return pltpu.make_async_remote_copy(src_ref=blk, dst_ref=blk, send_sem=send.at[t, j], recv_sem=recv.at[t, j],
                                                device_id=(cx, cy, c), device_id_type=MESH)

        def d2d(t, j, cc):
            cx, cy = chips[j]
            blk = half(t, 2 * cx + cy, cc)
            return pltpu.make_async_remote_copy(src_ref=blk, dst_ref=blk, send_sem=fsend.at[t, j], recv_sem=frecv.at[t, j],
                                                device_id=(x, y, 1 - c), device_id_type=MESH)

        own = [pltpu.make_async_remote_copy(src_ref=ins[t], dst_ref=outs[t].at[me], send_sem=osend.at[t], recv_sem=orecv.at[t],
                                            device_id=(x, y, 1 - c), device_id_type=MESH) for t in range(nt)]
        for t in range(nt):
            for j in range(3):
                ici(t, j).start()
        for cp in own:
            cp.start()
        for t in range(nt):
            for j in range(3):
                landed(t, j).wait_recv()
                d2d(t, j, c).start()
        for t in range(nt):
            for j in range(3):
                d2d(t, j, 1 - c).wait_recv()
        for t in range(nt):
            for j in range(3):
                ici(t, j).wait_send()
                d2d(t, j, c).wait_send()
        for cp in own:
            cp.wait()

    return pl.pallas_call(
        body, name=name, in_specs=[ANY] * nt, out_specs=[ANY] * nt,
        out_shape=[SDS((4,) + s.shape, s.dtype) for s in shards],
        scratch_shapes=[pltpu.SemaphoreType.DMA((nt, 3))] * 4 + [pltpu.SemaphoreType.DMA((nt,))] * 2,
        compiler_params=pltpu.CompilerParams(has_side_effects=True))(*shards)


def _join_halves(bufs, name):
    nt = len(bufs)

    def body(*refs):
        outs = refs[nt:2 * nt]
        send, recv = refs[2 * nt:]
        x, y, c, _ = _place()
        cps = [pltpu.make_async_remote_copy(src_ref=outs[t].at[c], dst_ref=outs[t].at[c], send_sem=send.at[t], recv_sem=recv.at[t],
                                            device_id=(x, y, 1 - c), device_id_type=MESH) for t in range(nt)]
        for cp in cps:
            cp.start()
        for t in range(nt):
            theirs = outs[t].at[1 - c]
            pltpu.make_async_remote_copy(src_ref=theirs, dst_ref=theirs, send_sem=send.at[t], recv_sem=recv.at[t],
                                         device_id=(x, y, 1 - c), device_id_type=MESH).wait_recv()
        for cp in cps:
            cp.wait_send()

    return pl.pallas_call(
        body, name=name, in_specs=[ANY] * nt, out_specs=[ANY] * nt, out_shape=[SDS(b.shape, b.dtype) for b in bufs],
        input_output_aliases={t: t for t in range(nt)},
        scratch_shapes=[pltpu.SemaphoreType.DMA((nt,))] * 2,
        compiler_params=pltpu.CompilerParams(has_side_effects=True))(*bufs)


def _exchange_small(v, reduce, name, after=()):
    rows = v.shape[0]
    after, after_specs = _unread(after)

    def body(v_ref, *rest):
        o_ref, buf, send, recv = rest[-4:]
        x, y, c, _ = _place()
        me = 4 * x + 2 * y + c
        buf[me] = v_ref[...]

        def peer(dx, dy, dc):
            return (1 - x if dx else x, 1 - y if dy else y, 1 - c if dc else c)

        peers = [(dx, dy, dc) for dx in range(2) for dy in range(2) for dc in range(2) if (dx, dy, dc) != (0, 0, 0)]
        cps = []
        for j, (dx, dy, dc) in enumerate(peers):
            cps.append(pltpu.make_async_remote_copy(src_ref=v_ref, dst_ref=buf.at[me], send_sem=send.at[j], recv_sem=recv.at[j],
                                                    device_id=peer(dx, dy, dc), device_id_type=MESH))
        for cp in cps:
            cp.start()
        for j, (dx, dy, dc) in enumerate(peers):
            px, py, pc = peer(dx, dy, dc)
            blk = buf.at[4 * px + 2 * py + pc]
            pltpu.make_async_remote_copy(src_ref=blk, dst_ref=blk, send_sem=send.at[j], recv_sem=recv.at[j],
                                         device_id=(px, py, pc), device_id_type=MESH).wait_recv()
        for cp in cps:
            cp.wait_send()
        if reduce:
            acc = buf[0]
            for j in range(1, 8):
                acc = acc + buf[j]
            o_ref[...] = acc
        else:
            o_ref[...] = buf[...]

    vm = pl.BlockSpec(memory_space=pltpu.VMEM)
    return pl.pallas_call(
        body, name=name, in_specs=[vm] + after_specs, out_specs=vm, out_shape=SDS((rows, 128) if reduce else (8, rows, 128), F32),
        scratch_shapes=[pltpu.VMEM((8, rows, 128), F32), pltpu.SemaphoreType.DMA((7,)), pltpu.SemaphoreType.DMA((7,))],
        compiler_params=pltpu.CompilerParams(has_side_effects=True))(v, *after)


HBM = pl.BlockSpec(memory_space=pltpu.HBM)
SEM = pl.BlockSpec(memory_space=pltpu.SEMAPHORE)
TOKEN = pl.BlockSpec(memory_space=pltpu.VMEM)
TOKEN_SHAPE = SDS((8, 128), F32)
PEERS = 7


def _in_hbm(a):
    return pltpu.with_memory_space_constraint(a, pltpu.HBM)


def _split_params():
    return pltpu.CompilerParams(has_side_effects=pltpu.SideEffectType.DATAFLOW_SIDE_EFFECTING)


def _gather_start(shards, name, after=()):
    nt = len(shards)
    after, after_specs = _unread(after)

    def body(*refs):
        ins, lands = refs[:nt], refs[nt:2 * nt]
        outs = refs[2 * nt + len(after):]
        sends, recvs = outs[:nt], outs[nt:2 * nt]
        x, y, c, chips = _place()
        me = 2 * x + y
        for t in range(nt):
            h = ins[t].shape[0] // 2
            mine = pl.ds(c * h, h)
            for j, (cx, cy) in enumerate(chips):
                for dc in range(2):
                    pltpu.make_async_remote_copy(src_ref=ins[t].at[mine], dst_ref=lands[t].at[me, mine], send_sem=sends[t].at[2 * j + dc],
                                                 recv_sem=recvs[t].at[2 * j + c], device_id=(cx, cy, dc), device_id_type=MESH).start()
            pltpu.make_async_remote_copy(src_ref=ins[t], dst_ref=lands[t].at[me], send_sem=sends[t].at[PEERS - 1], recv_sem=recvs[t].at[PEERS - 1],
                                         device_id=(x, y, 1 - c), device_id_type=MESH).start()
        outs[-1][...] = jnp.zeros(TOKEN_SHAPE.shape, F32)

    lands = [lax.empty((4,) + s.shape, s.dtype) for s in shards]
    out = pl.pallas_call(
        body, name=name, in_specs=[HBM] * (2 * nt) + after_specs, out_specs=[SEM] * (2 * nt) + [HBM] * (2 * nt) + [TOKEN],
        out_shape=[pltpu.SemaphoreType.DMA((PEERS,))] * (2 * nt)
        + [pltpu.HBM(s.shape, s.dtype) for s in shards] + [pltpu.HBM(l.shape, l.dtype) for l in lands] + [TOKEN_SHAPE],
        input_output_aliases={t: 2 * nt + t for t in range(2 * nt)}, compiler_params=_split_params())(
            *[_in_hbm(s) for s in shards], *[_in_hbm(l) for l in lands], *after)
    return out[:nt], out[nt:2 * nt], out[2 * nt:3 * nt], out[3 * nt:4 * nt], out[-1]


def _gather_wait(sends, recvs, shards, lands, after, name):
    nt = len(shards)

    def body(*refs):
        ins, lands_ref = refs[:nt], refs[nt:2 * nt]
        send_refs, recv_refs = refs[2 * nt:3 * nt], refs[3 * nt:4 * nt]
        x, y, c, chips = _place()
        for t in range(nt):
            h = ins[t].shape[0] // 2
            for j, (cx, cy) in enumerate(chips):
                for cs in range(2):
                    blk = lands_ref[t].at[2 * cx + cy, pl.ds(cs * h, h)]
                    pltpu.make_async_remote_copy(src_ref=blk, dst_ref=blk, send_sem=send_refs[t].at[2 * j + cs], recv_sem=recv_refs[t].at[2 * j + cs],
                                                 device_id=(cx, cy, cs), device_id_type=MESH).wait()
            blk = lands_ref[t].at[2 * x + y]
            pltpu.make_async_remote_copy(src_ref=blk, dst_ref=blk, send_sem=send_refs[t].at[PEERS - 1], recv_sem=recv_refs[t].at[PEERS - 1],
                                         device_id=(x, y, 1 - c), device_id_type=MESH).wait()

    out = pl.pallas_call(
        body, name=name, in_specs=[HBM] * (2 * nt) + [SEM] * (2 * nt) + [ANY], out_specs=[HBM] * (2 * nt),
        out_shape=[pltpu.HBM(s.shape, s.dtype) for s in shards] + [pltpu.HBM(l.shape, l.dtype) for l in lands],
        input_output_aliases={t: t for t in range(2 * nt)}, compiler_params=_split_params())(*shards, *lands, *sends, *recvs, after)
    return out[nt:]


def _scatter_start(g, name):
    _, r, c_ = g.shape
    h = r // 2

    def body(g_ref, land, send, recv, g_thru, land_thru, token):
        x, y, c, chips = _place()
        for j, (cx, cy) in enumerate(chips):
            for dc in range(2):
                pltpu.make_async_remote_copy(src_ref=g_ref.at[2 * cx + cy, pl.ds(dc * h, h)], dst_ref=land.at[2 * j + c], send_sem=send.at[2 * j + dc],
                                             recv_sem=recv.at[2 * j + c], device_id=(cx, cy, dc), device_id_type=MESH).start()
        pltpu.make_async_remote_copy(src_ref=g_ref.at[2 * x + y, pl.ds((1 - c) * h, h)], dst_ref=land.at[PEERS - 1], send_sem=send.at[PEERS - 1],
                                     recv_sem=recv.at[PEERS - 1], device_id=(x, y, 1 - c), device_id_type=MESH).start()
        token[...] = jnp.zeros(TOKEN_SHAPE.shape, F32)

    land = lax.empty((PEERS, h, c_), g.dtype)
    return pl.pallas_call(
        body, name=name, in_specs=[HBM, HBM], out_specs=[SEM, SEM, HBM, HBM, TOKEN],
        out_shape=[pltpu.SemaphoreType.DMA((PEERS,)), pltpu.SemaphoreType.DMA((PEERS,)), pltpu.HBM(g.shape, g.dtype),
                   pltpu.HBM(land.shape, land.dtype), TOKEN_SHAPE],
        input_output_aliases={0: 2, 1: 3}, compiler_params=_split_params())(_in_hbm(g), _in_hbm(land))


def _scatter_wait(started, after, name):
    nt = len(started)

    def body(*refs):
        lands = refs[nt:2 * nt]
        sends, recvs = refs[2 * nt:3 * nt], refs[3 * nt:4 * nt]
        x, y, c, chips = _place()
        peers = [(cx, cy, dc) for cx, cy in chips for dc in range(2)] + [(x, y, 1 - c)]
        for t in range(nt):
            for k, peer in enumerate(peers):
                blk = lands[t].at[k]
                pltpu.make_async_remote_copy(src_ref=blk, dst_ref=blk, send_sem=sends[t].at[k], recv_sem=recvs[t].at[k],
                                             device_id=peer, device_id_type=MESH).wait()

    gs, lands = [s[2] for s in started], [s[3] for s in started]
    after, after_specs = _unread(after)
    out = pl.pallas_call(
        body, name=name, in_specs=[HBM] * (2 * nt) + [SEM] * (2 * nt) + after_specs, out_specs=[HBM] * (2 * nt),
        out_shape=[pltpu.HBM(a.shape, a.dtype) for a in gs + lands],
        input_output_aliases={t: t for t in range(2 * nt)}, compiler_params=_split_params())(
            *gs, *lands, *[s[0] for s in started], *[s[1] for s in started], *after)
    return out[:nt], out[nt:]


def _sum_devices(g, land, me, core, name):
    npeer, h, c = land.shape
    tr = _half_tile(h)
    steps = h // tr

    def body(ix_ref, own_ref, land_ref, o_ref):
        acc = own_ref[0].astype(F32)
        for j in range(npeer):
            acc = acc + land_ref[j].astype(F32)
        o_ref[0] = acc

    grid_spec = pltpu.PrefetchScalarGridSpec(
        num_scalar_prefetch=1, grid=(steps,),
        in_specs=[pl.BlockSpec((1, tr, c), lambda i, ix: (ix[0], ix[1] * steps + i, 0)), pl.BlockSpec((npeer, tr, c), lambda i, ix: (0, i, 0))],
        out_specs=pl.BlockSpec((1, tr, c), lambda i, ix: (ix[1], i, 0)))
    return pl.pallas_call(body, name=name, grid_spec=grid_spec, out_shape=SDS((2, h, c), F32),
                          compiler_params=_cp("parallel"))(jnp.stack([me, core]), g, land)


def _pack_small(parts):
    flat = jnp.concatenate([p.reshape(-1) for p in parts])
    total = flat.shape[0]
    rows = -(-total // 1024) * 8
    return jnp.pad(flat, (0, rows * 128 - total)).reshape(rows, 128)


def _unpack_small(packed, shapes):
    flat = packed.reshape(-1)
    out, off = [], 0
    for s in shapes:
        size = int(np.prod(s))
        out.append(flat[off:off + size].reshape(s))
        off += size
    return out


def _local_step(x, mem, target, w_in, first_after, mid_weights, ffn_weights, on_grad, gains, conv_w, conv_b, hg_lb):
    n = x.shape[0]
    cos, sin = _rope_tables(n)
    seg = _hg_segments()
    gp, gn = _hg_pair_sums()
    masks = _hg_level_masks()
    gq2 = jnp.tile(gains["q_norm_g"], (1, 2))
    gk2 = jnp.tile(gains["k_norm_g"], (1, 2))
    a0 = hg_lb[:, 0:1, :]
    a1 = hg_lb[:, 1:2, :]

    p, h1 = _norm_mm(x, gains["pre_mix_g"], w_in, F32, TOKEN_TILE, 1664, "in_proj", after=(first_after,))
    qr, kr = _qk_prep(p, gq2, gk2, cos, sin, "qk_prep")
    heads = lambda a: a.reshape(n, ATT_KV_HEADS, ATT_HEAD_DIM).transpose(1, 0, 2)
    kh = heads(kr)
    vh = heads(p[:, OFF_AV:OFF_AV + ATT_KV_DIM].astype(MXU_DTYPE))
    att = _attn_fwd(qr, kh, vh, "attn_fwd")
    o2, s0, hg_a = _hgrn_fwd(p, a0, a1, seg, masks, "hgrn_fwd")
    rec = _hg_post(o2, p, gains["hg_out_norm_g"], "hg_post")
    cat = jnp.concatenate([att, rec], axis=1)
    w_out, w_xq, w_xkv, w_xo = mid_weights(cat)
    mixed, x1 = _mm_resid_norm(cat, w_out, x, gains["post_mix_g"], 512, "out_proj_resid")
    xq, h2 = _norm_mm(x1, gains["pre_x_g"], w_xq, MXU_DTYPE, TOKEN_TILE, 1024, "xq_proj")
    kv, mn = _norm_mm(mem, gains["mem_norm_g"], w_xkv, MXU_DTYPE, 256, 2048, "xkv_proj")
    ox = _xattn_fwd(xq, kv, "xattn_fwd")
    xo, x2 = _mm_resid_norm(ox, w_xo, x1, gains["post_x_g"], 512, "xo_proj_resid")
    w_up = ffn_weights("w_up", x2)
    u, h3 = _norm_mm(x2, gains["pre_ffn_g"], w_up, F32, TOKEN_TILE, 1408, "up_proj")
    act = _conv_gate(u, conv_w, conv_b, "conv_gate")
    w_down = ffn_weights("w_down", act)
    dn, d3, loss = _mm_resid_norm(act, w_down, x2, gains["post_ffn_g"], 512, "down_proj_resid_loss", target=target)

    gs = {}
    d_act, d_dn, gs["post_ffn_g"] = _norm_bwd_mm(dn, gains["post_ffn_g"], d3, w_down, F32, 512, 1408, "ffn_post_bwd_down_dx")
    tok = on_grad("w_down", _mm(act, d_dn, "tn", WIRE_DTYPE, 1408, 1024, "down_dw"))
    du_g, du_v, dcw_g, dcw_v, dcb_g, dcb_v = _conv_gate_bwd(u, conv_w, conv_b, d_act, "conv_gate_bwd", after=(tok,))
    gs["conv_w"] = jnp.concatenate([dcw_g, dcw_v], axis=1)
    gs["conv_b"] = jnp.concatenate([dcb_g, dcb_v], axis=1)
    ff_shard = w_up.shape[2]
    g_up = _dw_by_owner(h3, du_g, ff_shard, 0, None, 512, "up_dw_gate")
    tok = on_grad("w_up", _dw_by_owner(h3, du_v, ff_shard, 2, g_up, 512, "up_dw_value"))
    d2, gs["pre_ffn_g"] = _dx_norm_bwd([(du_g, 0), (du_g, 1), (du_v, 0), (du_v, 1)], w_up, x2, gains["pre_ffn_g"], d3, 512,
                                       "up_dx_pre_bwd", after=(tok,))
    d_ox, d_xo, gs["post_x_g"] = _norm_bwd_mm(xo, gains["post_x_g"], d2, w_xo, MXU_DTYPE, 512, 1024, "x_post_bwd_xo_dx")
    tok = on_grad("w_xo", _mm(ox, d_xo, "tn", WIRE_DTYPE, 512, 1024, "xo_dw"))
    d_xq, d_k, d_v = _xattn_bwd(xq, kv, d_ox, "xattn_bwd", after=(tok,))
    d_kv = jnp.concatenate([d_k, d_v], axis=1).astype(MXU_DTYPE)
    tok = on_grad("w_xq", _mm(h2, d_xq, "tn", WIRE_DTYPE, 512, 1024, "xq_dw"))
    tok_kv = on_grad("w_xkv", _dw_by_owner(mn, d_kv, w_xkv.shape[2], 0, None, 512, "xkv_dw"))
    d1, gs["pre_x_g"] = _dx_norm_bwd([(d_xq, 0)], w_xq[None], x1, gains["pre_x_g"], d2, 512, "xq_dx_pre_bwd", after=(tok, tok_kv))
    d_mn = _mm_nt_parts([(d_kv, s) for s in range(4)], w_xkv, F32, 256, 1024, "xkv_dx")
    _, gs["mem_norm_g"] = _norm_bwd(mem, gains["mem_norm_g"], d_mn, None, MXU_DTYPE, "mem_norm_bwd")
    d_cat, d_mixed, gs["post_mix_g"] = _norm_bwd_mm(mixed, gains["post_mix_g"], d1, w_out, MXU_DTYPE, 512, 1024, "mix_post_bwd_out_dx")
    tok = on_grad("w_out", _mm(cat, d_mixed, "tn", WIRE_DTYPE, 512, 1024, "out_dw"))
    d_o, d_hg, dg_hg = _hg_post_bwd(o2, p, gains["hg_out_norm_g"], d_cat, "hg_post_bwd", after=(tok,))
    gs["hg_out_norm_g"] = dg_hg.reshape(HG_HEADS, HG_HEAD_DIM).sum(axis=0, keepdims=True)
    dhq2, dz2, dhv2, dlb = _hgrn_bwd(p, a0, a1, seg, masks, gp, gn, d_o, s0, hg_a, "hgrn_bwd")
    lb = jax.nn.sigmoid(a0 - a1)
    da0 = dlb * lb * (1.0 - lb)
    gs["hg_lb"] = jnp.concatenate([da0, -da0], axis=1)
    d_qr, d_kh, d_vh = _attn_bwd(qr, kh, vh, cat, d_cat, "attn_bwd")
    unheads = lambda a: a.transpose(2, 0, 1).reshape(n, ATT_KV_DIM)
    d_aq, d_ak, dgq, dgk = _qk_prep_bwd(p, gq2, gk2, cos, sin, d_qr, unheads(d_kh), "qk_prep_bwd")
    gs["q_norm_g"] = dgq.reshape(ATT_HEADS, ATT_HEAD_DIM).sum(axis=0, keepdims=True)
    gs["k_norm_g"] = dgk.reshape(ATT_KV_HEADS, ATT_HEAD_DIM).sum(axis=0, keepdims=True)
    d_p = jnp.concatenate([d_aq, d_ak, unheads(d_vh).astype(MXU_DTYPE), (dhq2[0] + dhq2[1]).astype(MXU_DTYPE),
                           dz2[0].astype(MXU_DTYPE), dz2[1].astype(MXU_DTYPE), (dhv2[0] + dhv2[1]).astype(MXU_DTYPE), d_hg], axis=1)
    tok = on_grad("w_in", _mm(h1, d_p, "tn", WIRE_DTYPE, 512, 1664, "in_dw"))
    grad_x, gs["pre_mix_g"] = _dx_norm_bwd([(d_p, 0)], w_in[None], x, gains["pre_mix_g"], d1, 512, "in_dx_pre_bwd", after=(tok,))
    return loss, grad_x, gs


MATS = ("w_in", "w_out", "w_xq", "w_xkv", "w_xo", "w_up", "w_down")
GAINS = ("pre_mix_g", "q_norm_g", "k_norm_g", "hg_out_norm_g", "post_mix_g", "pre_x_g", "mem_norm_g", "post_x_g", "pre_ffn_g", "post_ffn_g")
WEIGHTS = ('pre_mix_g', 'w_in', 'q_norm_g', 'k_norm_g', 'hg_lb', 'hg_out_norm_g', 'w_out', 'post_mix_g', 'pre_x_g', 'mem_norm_g', 'w_xq',
           'w_xkv', 'w_xo', 'post_x_g', 'pre_ffn_g', 'w_up', 'conv_w', 'conv_b', 'w_down', 'post_ffn_g')


def kernel(x, mem, pre_mix_g, w_in, q_norm_g, k_norm_g, hg_lb, hg_out_norm_g, w_out, post_mix_g, pre_x_g, mem_norm_g, w_xq, w_xkv, w_xo, post_x_g, pre_ffn_g, w_up, conv_w, conv_b, w_down, post_ffn_g, loss_target, m_pre_mix_g, m_w_in, m_q_norm_g, m_k_norm_g, m_hg_lb, m_hg_out_norm_g, m_w_out, m_post_mix_g, m_pre_x_g, m_mem_norm_g, m_w_xq, m_w_xkv, m_w_xo, m_post_x_g, m_pre_ffn_g, m_w_up, m_conv_w, m_conv_b, m_w_down, m_post_ffn_g, v_pre_mix_g, v_w_in, v_q_norm_g, v_k_norm_g, v_hg_lb, v_hg_out_norm_g, v_w_out, v_post_mix_g, v_pre_x_g, v_mem_norm_g, v_w_xq, v_w_xkv, v_w_xo, v_post_x_g, v_pre_ffn_g, v_w_up, v_conv_w, v_conv_b, v_w_down, v_post_ffn_g):
    args = dict(locals())
    w = {k: args[k] for k in WEIGHTS}
    m = {k: args["m_" + k] for k in WEIGHTS}
    v = {k: args["v_" + k] for k in WEIGHTS}
    chip = 2 * lax.axis_index("x") + lax.axis_index("y")
    core = lax.axis_index("c")

    shards = {k: w[k][0].astype(WIRE_DTYPE) for k in MATS}

    def whole(k, g):
        return g if k in ("w_xkv", "w_up") else g.reshape(-1, g.shape[-1])

    w_in_shards = _gather_shards([shards["w_in"]], "gather_w_in")[0]
    w_in_full = jnp.concatenate([w_in_shards[s] for s in range(4)], axis=1)
    small_in = _exchange_small(_pack_small([w["conv_w"][0], w["hg_lb"]]), False, "gather_small")
    mid_names, ffn_names = ("w_out", "w_xq", "w_xkv", "w_xo"), ("w_up", "w_down")
    mid = _gather_start([shards[k] for k in mid_names], "gather_mid_start", after=(w_in_full, small_in))
    ffn = _gather_start([shards[k] for k in ffn_names], "gather_ffn_start", after=(mid[4],))

    def mid_weights(after):
        return [whole(k, g) for k, g in zip(mid_names, _gather_wait(*mid[:4], after, "gather_mid_wait"))]

    def ffn_weights(k, after):
        t = ffn_names.index(k)
        return whole(k, _gather_wait(*[part[t:t + 1] for part in ffn[:4]], after, "gather_wait_" + k)[0])

    cw_parts, lb_parts = [], []
    for s in range(4):
        cw_s, lb_s = _unpack_small(small_in[2 * s], [w["conv_w"][0].shape, w["hg_lb"].shape])
        cw_parts.append(cw_s)
        lb_parts.append(lb_s)
    conv_w_full = jnp.concatenate(cw_parts, axis=1)
    hg_lb_full = jnp.concatenate(lb_parts, axis=2)

    started = {}

    def on_grad(k, g):
        if k == "w_in":
            g = g.reshape(g.shape[0], 4, g.shape[1] // 4).transpose(1, 0, 2)
        elif g.ndim == 2:
            g = g.reshape(4, g.shape[0] // 4, g.shape[1])
        *started[k], token = _scatter_start(g, "grad_start_" + k)
        return token

    gains = {k: w[k] for k in GAINS}
    loss_part, grad_x, gs = _local_step(x[0], mem[0], loss_target[0], w_in_full, ffn[4], mid_weights, ffn_weights, on_grad, gains,
                                        conv_w_full, w["conv_b"], hg_lb_full)
    gs["loss"] = loss_part

    grads, delta, new_m, new_v = {}, {}, {}, {}

    def reduce_matrices(names, after, tag):
        sent, landed = _scatter_wait([started[k] for k in names], after, "grad_wait_" + tag)
        halves = [_sum_devices(g, land, chip, core, "grad_sum_" + k) for k, g, land in zip(names, sent, landed)]
        for k, r in zip(names, _join_halves(halves, "grad_join_" + tag)):
            grads[k] = r.reshape(1, -1, r.shape[-1])

    def adamw(names):
        for k in names:
            shape = w[k].shape
            keep = len(shape) == 3 and shape[0] == 1
            two_d = lambda a: a.reshape(shape) if keep else a.reshape(-1, shape[-1])
            d, mo, vo, go = _adamw(two_d(w[k]), two_d(grads[k]), two_d(m[k]), two_d(v[k]), "adamw_" + k)
            delta[k], new_m[k], new_v[k], grads[k] = d.reshape(shape), mo.reshape(shape), vo.reshape(shape), go.reshape(shape)

    early = tuple(k for k in MATS if k != "w_in")
    reduce_matrices(early, (grad_x,), "early")
    adamw(early)

    small_names = GAINS + ("conv_b", "conv_w", "hg_lb")
    packed = _pack_small([gs[k] for k in small_names + ("loss",)])
    reduced_small = _exchange_small(packed, True, "reduce_small", after=tuple(new_v[k] for k in early))
    *summed, loss = _unpack_small(reduced_small, [gs[k].shape for k in small_names + ("loss",)])
    loss = loss[0, 0]
    for k, g in zip(small_names, summed):
        grads[k] = g
    ncw = w["conv_w"].shape[2]
    grads["conv_w"] = lax.dynamic_slice_in_dim(grads["conv_w"], chip * ncw, ncw, axis=1)[None]
    nlb = w["hg_lb"].shape[2]
    grads["hg_lb"] = lax.dynamic_slice_in_dim(grads["hg_lb"], chip * nlb, nlb, axis=2)
    replicated = GAINS + ("conv_b",)
    shapes = [w[k].shape for k in replicated]
    rows = sum(int(np.prod(s)) for s in shapes) // 128
    pack = lambda d: jnp.concatenate([d[k].reshape(-1) for k in replicated]).reshape(rows, 128)
    outs = _adamw(pack(w), reduced_small[:rows], pack(m), pack(v), "adamw_replicated")
    for into, packed_out in zip((delta, new_m, new_v, grads), outs):
        for k, a in zip(replicated, _unpack_small(packed_out, shapes)):
            into[k] = a
    adamw(("conv_w", "hg_lb"))

    reduce_matrices(("w_in",), tuple(new_v[k] for k in early + small_names), "late")
    adamw(("w_in",))
    return (loss, grad_x[None], *[grads[k] for k in WEIGHTS], *[delta[k] for k in WEIGHTS],
            *[new_m[k] for k in WEIGHTS], *[new_v[k] for k in WEIGHTS])
```

```python
import numpy as np
import jax
import jax.numpy as jnp
from jax import lax
from jax.experimental import pallas as pl
from jax.experimental.pallas import tpu as pltpu

F32 = jnp.float32
MXU_DTYPE = jnp.bfloat16
WIRE_DTYPE = jnp.bfloat16
VMEM_LIMIT_BYTES = 56 * 1024 * 1024
EPS = 1e-6
MESH = pl.DeviceIdType.MESH

GRID_W = 64
ATT_HEADS, ATT_KV_HEADS, ATT_HEAD_DIM = 8, 2, 64
ATT_GROUP = ATT_HEADS // ATT_KV_HEADS
ATT_Q_DIM, ATT_KV_DIM = 512, 128
ROPE_THETA = 10000.0
HG_HEADS, HG_HEAD_DIM, HG_DIM = 4, 128, 512
HG_CHUNK = 128
HG_LEVELS = 7
HG_PAIR = 2 * HG_HEAD_DIM
X_HEADS, X_HEAD_DIM = 4, 256
D_FF = 2816
FF_COLS = 256
FF_BLOCKS = D_FF // FF_COLS
OFF_AK, OFF_AV, OFF_HQ, OFF_ZF, OFF_ZB, OFF_HI, OFF_HG = 512, 640, 768, 1280, 1792, 2304, 2816

ADAM_LR, ADAM_B1, ADAM_B2, ADAM_EPS, ADAM_WD, ADAM_STEP = 0.001, 0.9, 0.999, 1e-08, 0.01, 10

SDS = jax.ShapeDtypeStruct


def _cp(*sem):
    return pltpu.CompilerParams(dimension_semantics=sem, vmem_limit_bytes=VMEM_LIMIT_BYTES)


def _dot(a, b, form="nn"):
    dims = {"nn": (((1,), (0,)), ((), ())), "nt": (((1,), (1,)), ((), ())), "tn": (((0,), (0,)), ((), ()))}[form]
    return lax.dot_general(a.astype(MXU_DTYPE), b.astype(MXU_DTYPE), dims, preferred_element_type=F32)


def _sigmoid(x):
    return 1.0 / (1.0 + jnp.exp(-x))


def _rstd(x):
    return lax.rsqrt(jnp.mean(x * x, axis=-1, keepdims=True) + EPS)


def _rms_bwd(x, g, dy):
    r = _rstd(x)
    xh = x * r
    dn = dy * g
    dx = r * (dn - xh * jnp.mean(dn * xh, axis=-1, keepdims=True))
    return dx, jnp.sum(dy * xh, axis=0, keepdims=True)


def _unread(after):
    after = tuple(a for a in after if a is not None)
    return after, [pl.BlockSpec(memory_space=pl.ANY)] * len(after)


def _mm(a, b, form, out_dtype, tm, tn, name, after=()):
    after, after_specs = _unread(after)
    if form == "nn":
        (m, k), n = a.shape, b.shape[1]
    elif form == "nt":
        (m, k), n = a.shape, b.shape[0]
    else:
        (k, m), n = a.shape, b.shape[1]
    tm, tn = min(tm, m), min(tn, n)
    assert m % tm == 0 and n % tn == 0, (name, m, n, tm, tn)

    def body(a_ref, b_ref, *rest):
        o_ref = rest[-1]
        o_ref[...] = _dot(a_ref[...], b_ref[...], form).astype(o_ref.dtype)

    a_spec = pl.BlockSpec((k, tm), lambda i, j: (0, i)) if form == "tn" else pl.BlockSpec((tm, k), lambda i, j: (i, 0))
    b_spec = pl.BlockSpec((tn, k), lambda i, j: (j, 0)) if form == "nt" else pl.BlockSpec((k, tn), lambda i, j: (0, j))
    return pl.pallas_call(
        body, name=name, grid=(m // tm, n // tn), in_specs=[a_spec, b_spec] + after_specs,
        out_specs=pl.BlockSpec((tm, tn), lambda i, j: (i, j)), out_shape=SDS((m, n), out_dtype),
        compiler_params=_cp("parallel", "parallel"))(a, b, *after)


def _mm_nt_parts(a_parts, b, out_dtype, tm, tn, name, after=()):
    after, after_specs = _unread(after)
    parts, n, p = b.shape
    m = a_parts[0][0].shape[0]
    tm, tn = min(tm, m), min(tn, n)
    assert m % tm == 0 and n % tn == 0 and len(a_parts) == parts, (name, m, b.shape)

    def body(*refs):
        o_ref = refs[-1]
        acc = _dot(refs[0][...], refs[parts][0], "nt")
        for s in range(1, parts):
            acc = acc + _dot(refs[s][...], refs[parts + s][0], "nt")
        o_ref[...] = acc.astype(o_ref.dtype)

    a_specs = [pl.BlockSpec((tm, p), lambda i, j, cb=cb: (i, cb)) for _, cb in a_parts]
    b_specs = [pl.BlockSpec((1, tn, p), lambda i, j, s=s: (s, j, 0)) for s in range(parts)]
    return pl.pallas_call(
        body, name=name, grid=(m // tm, n // tn), in_specs=a_specs + b_specs + after_specs,
        out_specs=pl.BlockSpec((tm, tn), lambda i, j: (i, j)), out_shape=SDS((m, n), out_dtype),
        compiler_params=_cp("parallel", "parallel"))(*[arr for arr, _ in a_parts], *([b] * parts), *after)


def _norm_bwd_mm(y, g, d, w, out_dtype, tm, tn, name):
    n, dm = y.shape
    nn = w.shape[0]
    tm, tn = min(tm, n), min(tn, nn)
    assert n % tm == 0 and nn % tn == 0 and w.shape[1] == dm, (name, y.shape, w.shape)

    def body(y_ref, g_ref, d_ref, w_ref, dx_ref, dy_ref, dg_ref, dys):
        i, j = pl.program_id(0), pl.program_id(1)

        @pl.when(jnp.logical_and(i == 0, j == 0))
        def _():
            dg_ref[...] = jnp.zeros_like(dg_ref)

        @pl.when(j == 0)
        def _():
            dy, dg = _rms_bwd(y_ref[...], g_ref[...], d_ref[...])
            dy = dy.astype(MXU_DTYPE)
            dys[...] = dy
            dy_ref[...] = dy
            dg_ref[...] += dg

        dx_ref[...] = _dot(dys[...], w_ref[...], "nt").astype(dx_ref.dtype)

    row = pl.BlockSpec((tm, dm), lambda i, j: (i, 0))
    vec = pl.BlockSpec((1, dm), lambda i, j: (0, 0))
    return pl.pallas_call(
        body, name=name, grid=(n // tm, nn // tn), in_specs=[row, vec, row, pl.BlockSpec((tn, dm), lambda i, j: (j, 0))],
        out_specs=[pl.BlockSpec((tm, tn), lambda i, j: (i, j)), row, vec],
        out_shape=[SDS((n, nn), out_dtype), SDS((n, dm), MXU_DTYPE), SDS((1, dm), F32)],
        scratch_shapes=[pltpu.VMEM((tm, dm), MXU_DTYPE)],
        compiler_params=_cp("arbitrary", "arbitrary"))(y, g, d, w)


def _mm_resid_norm(a, b, x, g, tm, name, target=None):
    n, k = a.shape
    d = b.shape[1]
    tm = min(tm, n)
    assert n % tm == 0 and x.shape == (n, d), (name, a.shape, b.shape)
    with_loss = target is not None

    def body(a_ref, b_ref, x_ref, g_ref, *rest):
        y = _dot(a_ref[...], b_ref[...])
        out = x_ref[...] + y * _rstd(y) * g_ref[...]
        if not with_loss:
            y_ref, o_ref = rest
            y_ref[...] = y
            o_ref[...] = out
            return
        t_ref, y_ref, d_ref, l_ref = rest
        y_ref[...] = y
        diff = out - t_ref[...]
        d_ref[...] = diff * (1.0 / d)

        @pl.when(pl.program_id(0) == 0)
        def _():
            l_ref[...] = jnp.zeros_like(l_ref)

        l_ref[...] += 0.5 * jnp.sum(jnp.mean(diff * diff, axis=-1, keepdims=True), axis=0, keepdims=True)

    row = pl.BlockSpec((tm, d), lambda i: (i, 0))
    ins = [pl.BlockSpec((tm, k), lambda i: (i, 0)), pl.BlockSpec((k, d), lambda i: (0, 0)), row, pl.BlockSpec((1, d), lambda i: (0, 0))]
    out = SDS((n, d), F32)
    if with_loss:
        return pl.pallas_call(body, name=name, grid=(n // tm,), in_specs=ins + [row], out_specs=[row, row, pl.BlockSpec((1, 1), lambda i: (0, 0))],
                              out_shape=[out, out, SDS((1, 1), F32)], compiler_params=_cp("arbitrary"))(a, b, x, g, target)
    return pl.pallas_call(body, name=name, grid=(n // tm,), in_specs=ins, out_specs=[row, row], out_shape=[out, out],
                          compiler_params=_cp("parallel"))(a, b, x, g)


def _dx_norm_bwd(a_parts, b, x, g, res, tm, name, after=()):
    after, after_specs = _unread(after)
    parts, d, p = b.shape
    n = x.shape[0]
    tm = min(tm, n)
    assert n % tm == 0 and len(a_parts) == parts and x.shape[1] == d, (name, x.shape, b.shape)

    def body(*refs):
        x_ref, g_ref, res_ref = refs[2 * parts:2 * parts + 3]
        dx_ref, dg_ref = refs[-2:]
        dh = _dot(refs[0][...], refs[parts][0], "nt")
        for s in range(1, parts):
            dh = dh + _dot(refs[s][...], refs[parts + s][0], "nt")
        dx, dg = _rms_bwd(x_ref[...], g_ref[...], dh)
        dx_ref[...] = dx + res_ref[...]

        @pl.when(pl.program_id(0) == 0)
        def _():
            dg_ref[...] = jnp.zeros_like(dg_ref)

        dg_ref[...] += dg

    a_specs = [pl.BlockSpec((tm, p), lambda i, cb=cb: (i, cb)) for _, cb in a_parts]
    b_specs = [pl.BlockSpec((1, d, p), lambda i, s=s: (s, 0, 0)) for s in range(parts)]
    row = pl.BlockSpec((tm, d), lambda i: (i, 0))
    vec = pl.BlockSpec((1, d), lambda i: (0, 0))
    return pl.pallas_call(
        body, name=name, grid=(n // tm,), in_specs=a_specs + b_specs + [row, vec, row] + after_specs,
        out_specs=[row, vec], out_shape=[SDS((n, d), F32), SDS((1, d), F32)],
        compiler_params=_cp("arbitrary"))(*[arr for arr, _ in a_parts], *([b] * parts), x, g, res, *after)


def _dw_by_owner(a, b, tn, first, into, tm, name):
    k, m = a.shape
    cnt = b.shape[1] // tn
    tm = min(tm, m)
    assert m % tm == 0 and b.shape[1] == cnt * tn and first + cnt <= 4, (name, a.shape, b.shape)

    def body(a_ref, b_ref, *rest):
        rest[-1][0] = _dot(a_ref[...], b_ref[...], "tn").astype(rest[-1].dtype)

    extra = [] if into is None else [into]
    return pl.pallas_call(
        body, name=name, grid=(m // tm, cnt),
        in_specs=[pl.BlockSpec((k, tm), lambda i, j: (0, i)), pl.BlockSpec((k, tn), lambda i, j: (0, j))] + [pl.BlockSpec(memory_space=pl.ANY)] * len(extra),
        out_specs=pl.BlockSpec((1, tm, tn), lambda i, j: (first + j, i, 0)), out_shape=SDS((4, m, tn), WIRE_DTYPE),
        input_output_aliases={2: 0} if extra else {},
        compiler_params=_cp("parallel", "parallel"))(a, b, *extra)


def _norm_mm(x, g, w, out_dtype, tm, tn, name, after=()):
    after, after_specs = _unread(after)
    m, d = x.shape
    sharded = w.ndim == 3
    n = w.shape[-1] * (w.shape[0] if sharded else 1)
    tm, tn = min(tm, m), (w.shape[-1] if sharded else min(tn, n))
    assert m % tm == 0 and n % tn == 0, (name, m, n, tm, tn)

    def body(x_ref, g_ref, w_ref, *rest):
        o_ref, h_ref, hs = rest[-3:]

        @pl.when(pl.program_id(1) == 0)
        def _():
            xv = x_ref[...]
            h = (xv * _rstd(xv) * g_ref[...]).astype(MXU_DTYPE)
            hs[...] = h
            h_ref[...] = h

        o_ref[...] = _dot(hs[...], w_ref[0] if sharded else w_ref[...]).astype(o_ref.dtype)

    w_spec = pl.BlockSpec((1, d, tn), lambda i, j: (j, 0, 0)) if sharded else pl.BlockSpec((d, tn), lambda i, j: (0, j))
    return pl.pallas_call(
        body, name=name, grid=(m // tm, n // tn),
        in_specs=[pl.BlockSpec((tm, d), lambda i, j: (i, 0)), pl.BlockSpec((1, d), lambda i, j: (0, 0)), w_spec] + after_specs,
        out_specs=[pl.BlockSpec((tm, tn), lambda i, j: (i, j)), pl.BlockSpec((tm, d), lambda i, j: (i, 0))],
        out_shape=[SDS((m, n), out_dtype), SDS((m, d), MXU_DTYPE)],
        scratch_shapes=[pltpu.VMEM((tm, d), MXU_DTYPE)],
        compiler_params=_cp("parallel", "arbitrary"))(x, g, w, *after)


ROW_TILE = 512
TOKEN_TILE = 1024


def _norm_bwd(x, g, dy, res, out_dtype, name):
    n, d = x.shape
    tr = min(ROW_TILE, n)
    has_res = res is not None

    def body(*refs):
        x_ref, g_ref, dy_ref = refs[:3]
        dx_ref, dg_ref = refs[-2:]
        dx, dg = _rms_bwd(x_ref[...], g_ref[...], dy_ref[...].astype(F32))
        if has_res:
            dx = dx + refs[3][...]
        dx_ref[...] = dx.astype(dx_ref.dtype)

        @pl.when(pl.program_id(0) == 0)
        def _():
            dg_ref[...] = jnp.zeros_like(dg_ref)

        dg_ref[...] += dg

    row = pl.BlockSpec((tr, d), lambda i: (i, 0))
    vec = pl.BlockSpec((1, d), lambda i: (0, 0))
    ins = [x, g, dy] + ([res] if has_res else [])
    return pl.pallas_call(
        body, name=name, grid=(n // tr,), in_specs=[row, vec, row] + ([row] if has_res else []),
        out_specs=[row, vec], out_shape=[SDS((n, d), out_dtype), SDS((1, d), F32)],
        compiler_params=_cp("arbitrary"))(*ins)


def _rope_tables(n):
    pairs = ATT_HEAD_DIM // 4
    t = np.arange(n)
    inv = np.power(ROPE_THETA, -np.arange(pairs, dtype=np.float32) / pairs).astype(np.float32)
    ang = np.concatenate([(t // GRID_W)[:, None].astype(np.float32) * inv, (t % GRID_W)[:, None].astype(np.float32) * inv], axis=-1)
    cos = np.repeat(np.cos(ang), 2, axis=-1)
    sin = np.repeat(np.sin(ang), 2, axis=-1) * np.tile(np.array([-1.0, 1.0], np.float32), ATT_HEAD_DIM // 2)
    return jnp.asarray(np.tile(cos, 2), F32), jnp.asarray(np.tile(sin, 2), F32)


def _swap_pairs(x):
    lane = lax.broadcasted_iota(jnp.int32, x.shape, 1)
    return jnp.where((lane & 1) == 0, pltpu.roll(x, 127, axis=1), pltpu.roll(x, 1, axis=1))


def _head_mean(v):
    lane = lax.broadcasted_iota(jnp.int32, v.shape, 1)
    lo = jnp.where(lane < ATT_HEAD_DIM, v, 0.0)
    s0 = jnp.sum(lo, axis=-1, keepdims=True)
    s1 = jnp.sum(v - lo, axis=-1, keepdims=True)
    return jnp.where(lane < ATT_HEAD_DIM, s0, s1) * (1.0 / ATT_HEAD_DIM)


def _qk_prep(p, gq, gk, cos, sin, name):
    n = p.shape[0]
    tr = min(ROW_TILE, n)

    def one(xv, g, c, s):
        xn = xv * lax.rsqrt(_head_mean(xv * xv) + EPS) * g
        return xn * c + _swap_pairs(xn) * s

    def body(q_ref, k_ref, gq_ref, gk_ref, c_ref, s_ref, qo_ref, ko_ref):
        c, s = c_ref[...], s_ref[...]
        for j in range(ATT_Q_DIM // 128):
            qo_ref[:, j * 128:(j + 1) * 128] = one(q_ref[:, j * 128:(j + 1) * 128], gq_ref[...], c, s).astype(qo_ref.dtype)
        ko_ref[...] = one(k_ref[...], gk_ref[...], c, s).astype(ko_ref.dtype)

    vec = pl.BlockSpec((1, 128), lambda i: (0, 0))
    tab = pl.BlockSpec((tr, 128), lambda i: (i, 0))
    return pl.pallas_call(
        body, name=name, grid=(n // tr,),
        in_specs=[pl.BlockSpec((tr, ATT_Q_DIM), lambda i: (i, 0)), pl.BlockSpec((tr, 128), lambda i: (i, OFF_AK // 128)), vec, vec, tab, tab],
        out_specs=[pl.BlockSpec((tr, ATT_Q_DIM), lambda i: (i, 0)), tab],
        out_shape=[SDS((n, ATT_Q_DIM), MXU_DTYPE), SDS((n, ATT_KV_DIM), MXU_DTYPE)],
        compiler_params=_cp("parallel"))(p, p, gq, gk, cos, sin)


def _qk_prep_bwd(p, gq, gk, cos, sin, dq, dk, name):
    n = p.shape[0]
    tr = min(ROW_TILE, n)

    def one(xv, g, c, s, dout):
        dxn = dout * c + _swap_pairs(dout * s)
        r = lax.rsqrt(_head_mean(xv * xv) + EPS)
        xh = xv * r
        dn = dxn * g
        dx = r * (dn - xh * _head_mean(dn * xh))
        return dx, jnp.sum(dxn * xh, axis=0, keepdims=True)

    def body(q_ref, k_ref, gq_ref, gk_ref, c_ref, s_ref, dq_ref, dk_ref, dqo_ref, dko_ref, dgq_ref, dgk_ref):
        @pl.when(pl.program_id(0) == 0)
        def _():
            dgq_ref[...] = jnp.zeros_like(dgq_ref)
            dgk_ref[...] = jnp.zeros_like(dgk_ref)

        c, s = c_ref[...], s_ref[...]
        for j in range(ATT_Q_DIM // 128):
            sl = slice(j * 128, (j + 1) * 128)
            dx, dg = one(q_ref[:, sl], gq_ref[...], c, s, dq_ref[:, sl])
            dqo_ref[:, sl] = dx.astype(dqo_ref.dtype)
            dgq_ref[:, sl] += dg
        dx, dg = one(k_ref[...], gk_ref[...], c, s, dk_ref[...])
        dko_ref[...] = dx.astype(dko_ref.dtype)
        dgk_ref[...] += dg

    vec = pl.BlockSpec((1, 128), lambda i: (0, 0))
    tab = pl.BlockSpec((tr, 128), lambda i: (i, 0))
    qrow = pl.BlockSpec((tr, ATT_Q_DIM), lambda i: (i, 0))
    return pl.pallas_call(
        body, name=name, grid=(n // tr,),
        in_specs=[qrow, pl.BlockSpec((tr, 128), lambda i: (i, OFF_AK // 128)), vec, vec, tab, tab, qrow, tab],
        out_specs=[qrow, tab, pl.BlockSpec((1, ATT_Q_DIM), lambda i: (0, 0)), vec],
        out_shape=[SDS((n, ATT_Q_DIM), MXU_DTYPE), SDS((n, ATT_KV_DIM), MXU_DTYPE), SDS((1, ATT_Q_DIM), F32), SDS((1, 128), F32)],
        compiler_params=_cp("arbitrary"))(p, p, gq, gk, cos, sin, dq, dk)


ATT_TQ = 256


def _attn_fwd(q, k, v, name):
    n = q.shape[0]
    tq = min(ATT_TQ, n)
    scale = ATT_HEAD_DIM ** -0.5
    gw = ATT_GROUP * ATT_HEAD_DIM

    def body(q_ref, k_ref, v_ref, o_ref):
        kk, vv = k_ref[0], v_ref[0]
        v_ones = jnp.concatenate([vv, jnp.ones_like(vv)], axis=1)
        outs = []
        for g in range(ATT_GROUP):
            s = _dot(q_ref[:, g * ATT_HEAD_DIM:(g + 1) * ATT_HEAD_DIM] * scale, kk, "nt")
            e = jnp.exp(s - jnp.max(s, axis=-1, keepdims=True))
            ov = _dot(e, v_ones)
            outs.append(ov[:, :ATT_HEAD_DIM] / ov[:, ATT_HEAD_DIM:])
        o_ref[...] = jnp.concatenate(outs, axis=-1).astype(o_ref.dtype)

    kv = pl.BlockSpec((1, n, ATT_HEAD_DIM), lambda h, i: (h, 0, 0))
    return pl.pallas_call(
        body, name=name, grid=(ATT_KV_HEADS, n // tq),
        in_specs=[pl.BlockSpec((tq, gw), lambda h, i: (i, h)), kv, kv],
        out_specs=pl.BlockSpec((tq, gw), lambda h, i: (i, h)), out_shape=SDS((n, ATT_Q_DIM), MXU_DTYPE),
        compiler_params=_cp("parallel", "parallel"))(q, k, v)


def _attn_bwd(q, k, v, o, do, name):
    n = q.shape[0]
    tq = min(ATT_TQ, n)
    scale = ATT_HEAD_DIM ** -0.5
    gw = ATT_GROUP * ATT_HEAD_DIM

    def body(q_ref, k_ref, v_ref, o_ref, do_ref, dq_ref, dk_ref, dv_ref):
        @pl.when(pl.program_id(1) == 0)
        def _():
            dk_ref[...] = jnp.zeros_like(dk_ref)
            dv_ref[...] = jnp.zeros_like(dv_ref)

        kk, vv = k_ref[0], v_ref[0]
        dqs = []
        dk_acc = jnp.zeros((ATT_HEAD_DIM, n), F32)
        dv_acc = jnp.zeros((ATT_HEAD_DIM, n), F32)
        for g in range(ATT_GROUP):
            sl = slice(g * ATT_HEAD_DIM, (g + 1) * ATT_HEAD_DIM)
            qg, dog = q_ref[:, sl] * scale, do_ref[:, sl].astype(F32)
            s = _dot(qg, kk, "nt")
            e = jnp.exp(s - jnp.max(s, axis=-1, keepdims=True))
            inv = 1.0 / jnp.sum(e, axis=-1, keepdims=True)
            delta = jnp.sum(dog * o_ref[:, sl].astype(F32), axis=-1, keepdims=True)
            dse = e * (_dot(dog, vv, "nt") - delta)
            dqs.append(_dot(dse, kk) * (inv * scale))
            dk_acc += _dot(qg.astype(F32) * inv, dse, "tn")
            dv_acc += _dot(dog * inv, e, "tn")
        dq_ref[...] = jnp.concatenate(dqs, axis=-1)
        dk_ref[0] += dk_acc
        dv_ref[0] += dv_acc

    kv = pl.BlockSpec((1, n, ATT_HEAD_DIM), lambda h, i: (h, 0, 0))
    kvt = pl.BlockSpec((1, ATT_HEAD_DIM, n), lambda h, i: (h, 0, 0))
    qb = pl.BlockSpec((tq, gw), lambda h, i: (i, h))
    return pl.pallas_call(
        body, name=name, grid=(ATT_KV_HEADS, n // tq), in_specs=[qb, kv, kv, qb, qb], out_specs=[qb, kvt, kvt],
        out_shape=[SDS((n, ATT_Q_DIM), F32), SDS((ATT_KV_HEADS, ATT_HEAD_DIM, n), F32), SDS((ATT_KV_HEADS, ATT_HEAD_DIM, n), F32)],
        compiler_params=_cp("parallel", "arbitrary"))(q, k, v, o, do)


def _both_directions(mats, axis):
    fwd = np.concatenate(mats, axis=axis).astype(np.float32)
    bwd = np.concatenate([m[::-1, ::-1] for m in mats], axis=axis).astype(np.float32)
    return jnp.asarray(np.stack([fwd, bwd]), MXU_DTYPE)


def _hg_segments():
    c = HG_CHUNK
    t = np.arange(c)[:, None]
    r = np.arange(c)[None, :]
    mats = [(r <= t)]
    for lev in range(HG_LEVELS):
        h = c >> (lev + 1)
        mid = (t // (2 * h)) * (2 * h) + h - 1
        hi = (t // h) % 2 == 1
        mats.append(np.where(hi, (r > mid) & (r <= t), (r > t) & (r <= mid)))
    mats.append(r > t)
    return _both_directions(mats, 0)


def _hg_pair_sums():
    c = HG_CHUNK
    r = np.arange(c)[:, None]
    t = np.arange(c)[None, :]
    gp, gn = [t >= r], [t < r]
    for lev in range(HG_LEVELS):
        sh = HG_LEVELS - 1 - lev
        same = (r >> sh) == (t >> sh)
        gp.append(same & (t >= r))
        gn.append(same & (t < r))
    return _both_directions(gp, 1), _both_directions(gn, 1)


def _split_dot(mat, x):
    hi = x.astype(MXU_DTYPE)
    lo = (x - hi.astype(F32)).astype(MXU_DTYPE)
    return _dot(mat, hi) + _dot(mat, lo)


def _hg_gates(hq, z, a0, a1):
    q = hq * _sigmoid(hq)
    sg = _sigmoid(z)
    lb = _sigmoid(a0 - a1)
    f = lb + (1.0 - lb) * sg
    k = (1.0 - lb) * (1.0 - sg)
    return q, f, k, sg, lb


def _hg_level_masks():
    c = HG_CHUNK
    t = np.arange(c)
    later, same = [], []
    for lev in range(HG_LEVELS):
        sh = HG_LEVELS - 1 - lev
        later.append(np.broadcast_to((((t >> sh) & 1) == 1)[:, None], (c, HG_HEAD_DIM)))
        same.append((t[:, None] >> (sh + 1)) == (t[None, :] >> (sh + 1)))
    same.append(t[:, None] == t[None, :])
    later = np.stack(later).astype(np.float32)
    return jnp.asarray(np.stack([later, 1.0 - later]), F32), jnp.asarray(np.stack(same).astype(np.float32), F32)


def _hg_level(q, k, ex, later_ref, lev):
    e = ex[lev + 1]
    e_q = e * later_ref[0, lev]
    e_k = e - e_q
    return q * e_q, k * e_k, e_q, e_k


def _hg_intra(q, k, ex, later_ref, same_ref):
    a = same_ref[HG_LEVELS] * jnp.sum(q * k, axis=-1, keepdims=True)
    for lev in range(HG_LEVELS):
        qs, ks, _, _ = _hg_level(q, k, ex, later_ref, lev)
        a = a + same_ref[lev] * _dot(qs, ks, "nt")
    return a


def _hg_specs(n, with_time):
    c = HG_CHUNK
    nc = n // c

    def chunk(d, i):
        first = d if with_time else 1 - d
        return i + first * (nc - 1 - 2 * i)

    def pcols(off, dir_stride=0):
        return [pl.BlockSpec((c, HG_PAIR), lambda d, i, j=j: (chunk(d, i), off // HG_PAIR + dir_stride // HG_PAIR * d + j)) for j in range(2)]

    specs = dict(
        hq=pcols(OFF_HQ), v=pcols(OFF_HI), z=pcols(OFF_ZF, OFF_ZB - OFF_ZF),
        shared=pl.BlockSpec((c, HG_DIM), lambda d, i: (chunk(d, i), 0)),
        per_dir=pl.BlockSpec((1, c, HG_DIM), lambda d, i: (d, chunk(d, i), 0)),
        vec=pl.BlockSpec((1, 1, HG_DIM), lambda d, i: (d, 0, 0)),
        seg=pl.BlockSpec((1, (HG_LEVELS + 2) * c, c), lambda d, i: (d, 0, 0)),
        sums=pl.BlockSpec((1, c, (HG_LEVELS + 1) * c), lambda d, i: (d, 0, 0)),
        later=pl.BlockSpec((1, HG_LEVELS, c, HG_HEAD_DIM), lambda d, i: (d, 0, 0, 0)),
        same=pl.BlockSpec((HG_LEVELS + 1, c, c), lambda d, i: (0, 0, 0)),
        state=pl.BlockSpec((1, HG_HEADS, 1, HG_HEAD_DIM, HG_HEAD_DIM), lambda d, i: (d, 0, chunk(d, i), 0, 0)),
        weights=pl.BlockSpec((1, HG_HEADS, 1, c, c), lambda d, i: (d, 0, chunk(d, i), 0, 0)),
        levels=pl.BlockSpec((1, HG_HEADS, 1, HG_LEVELS, c, HG_HEAD_DIM), lambda d, i: (d, 0, chunk(d, i), 0, 0, 0)))
    return nc, specs


def _hg_head(refs, hh):
    off = (hh % 2) * HG_HEAD_DIM
    return refs[hh // 2][:, off:off + HG_HEAD_DIM]


def _hg_lanes(hh):
    return slice(hh * HG_HEAD_DIM, (hh + 1) * HG_HEAD_DIM)


def _hg_exps(seg_ref, f, kept=None):
    c = HG_CHUNK
    lf = jnp.log(f)
    if kept is None:
        args = _split_dot(seg_ref[0], lf)
        return [jnp.exp(args[j * c:(j + 1) * c]) for j in range(HG_LEVELS + 2)]
    chunk_wide = jnp.concatenate([seg_ref[0, 0:c], seg_ref[0, (HG_LEVELS + 1) * c:(HG_LEVELS + 2) * c]], axis=0)
    args = _split_dot(chunk_wide, lf)
    return [jnp.exp(args[:c])] + [e.astype(F32) for e in kept] + [jnp.exp(args[c:])]


def _hg_last_row(a, mirrored):
    return jnp.where(mirrored, a[0:1, :], a[HG_CHUNK - 1:HG_CHUNK, :])


def _hgrn_fwd(p, a0, a1, seg, masks, name):
    n = p.shape[0]
    nc, sp = _hg_specs(n, True)

    def body(hq0, hq1, z0, z1, v0, v1, a0_ref, a1_ref, seg_ref, later_ref, same_ref, o_ref, s0_ref, a_ref, e_ref, st):
        @pl.when(pl.program_id(1) == 0)
        def _():
            st[...] = jnp.zeros_like(st)

        mirrored = pl.program_id(0) == 1
        for hh in range(HG_HEADS):
            ln = _hg_lanes(hh)
            q, f, k, _, _ = _hg_gates(_hg_head((hq0, hq1), hh), _hg_head((z0, z1), hh), a0_ref[0, :, ln], a1_ref[0, :, ln])
            vv = _hg_head((v0, v1), hh)
            ex = _hg_exps(seg_ref, f)
            for lev in range(HG_LEVELS):
                e_ref[0, hh, 0, lev] = ex[lev + 1].astype(e_ref.dtype)
            a = _hg_intra(q, k, ex, later_ref, same_ref).astype(MXU_DTYPE)
            a_ref[0, hh, 0] = a
            s_t = st[hh]
            s0_ref[0, hh, 0] = s_t
            o_ref[0, :, ln] = _dot(a, vv) + _dot(q * ex[0], s_t, "nt")
            st[hh] = s_t * _hg_last_row(ex[0], mirrored) + _dot(vv, k * ex[HG_LEVELS + 1], "tn")

    return pl.pallas_call(
        body, name=name, grid=(2, nc), in_specs=sp["hq"] + sp["z"] + sp["v"] + [sp["vec"], sp["vec"], sp["seg"], sp["later"], sp["same"]],
        out_specs=[sp["per_dir"], sp["state"], sp["weights"], sp["levels"]],
        out_shape=[SDS((2, n, HG_DIM), F32), SDS((2, HG_HEADS, nc, HG_HEAD_DIM, HG_HEAD_DIM), F32),
                   SDS((2, HG_HEADS, nc, HG_CHUNK, HG_CHUNK), MXU_DTYPE),
                   SDS((2, HG_HEADS, nc, HG_LEVELS, HG_CHUNK, HG_HEAD_DIM), MXU_DTYPE)],
        scratch_shapes=[pltpu.VMEM((HG_HEADS, HG_HEAD_DIM, HG_HEAD_DIM), F32)],
        compiler_params=_cp("parallel", "arbitrary"))(p, p, p, p, p, p, a0, a1, seg, *masks)


def _hgrn_bwd(p, a0, a1, seg, masks, gp, gn, do, s0, a, e, name):
    n = p.shape[0]
    nc, sp = _hg_specs(n, False)


    def body(hq0, hq1, z0, z1, v0, v1, a0_ref, a1_ref, seg_ref, later_ref, same_ref, gp_ref, gn_ref, do_ref, s0_ref, a_ref, e_ref,
             dhq_ref, dz_ref, dv_ref, dlb_ref, rt):
        @pl.when(pl.program_id(1) == 0)
        def _():
            rt[...] = jnp.zeros_like(rt)
            dlb_ref[...] = jnp.zeros_like(dlb_ref)

        mirrored = pl.program_id(0) == 1
        for hh in range(HG_HEADS):
            ln = _hg_lanes(hh)
            hqv = _hg_head((hq0, hq1), hh)
            q, f, k, sg, lb = _hg_gates(hqv, _hg_head((z0, z1), hh), a0_ref[0, :, ln], a1_ref[0, :, ln])
            vv, dov = _hg_head((v0, v1), hh), do_ref[:, ln]
            ex = _hg_exps(seg_ref, f, kept=[e_ref[0, hh, 0, lev] for lev in range(HG_LEVELS)])
            a = a_ref[0, hh, 0]
            da = _dot(dov, vv, "nt")
            diag = jnp.sum(dov * vv, axis=-1, keepdims=True)
            s_t = s0_ref[0, hh, 0]
            r_t = rt[hh]
            k_end = k * ex[HG_LEVELS + 1]
            dv_ref[0, :, ln] = _dot(a, dov, "tn") + _dot(k_end, r_t, "nt")
            dq_inter = ex[0] * _dot(dov, s_t)
            dk_inter = ex[HG_LEVELS + 1] * _dot(vv, r_t)
            dq = diag * k + dq_inter
            dk = diag * q + dk_inter
            q_terms, k_terms = [q * dq_inter], [k * dk_inter]
            for lev in range(HG_LEVELS):
                qs, ks, e_q, e_k = _hg_level(q, k, ex, later_ref, lev)
                pairs = da * same_ref[lev]
                q_part = e_q * _dot(pairs, ks)
                k_part = e_k * _dot(pairs, qs, "tn")
                dq, dk = dq + q_part, dk + k_part
                q_terms.append(q * q_part)
                k_terms.append(k * k_part)
            decay = _hg_last_row(ex[0], mirrored)
            rt[hh] = r_t * decay + _dot(dov, q * ex[0], "tn")
            later = decay * jnp.sum(s_t * r_t, axis=0, keepdims=True)
            dlf = _dot(gp_ref[0], jnp.concatenate(q_terms, axis=0)) + _dot(gn_ref[0], jnp.concatenate(k_terms, axis=0)) + later
            df = dlf / f - dk
            dz_ref[0, :, ln] = df * (1.0 - lb) * sg * (1.0 - sg)
            dlb_ref[0, :, ln] += jnp.sum(df * (1.0 - sg), axis=0, keepdims=True)
            sq = _sigmoid(hqv)
            dhq_ref[0, :, ln] = dq * sq * (1.0 + hqv * (1.0 - sq))

    out = SDS((2, n, HG_DIM), F32)
    return pl.pallas_call(
        body, name=name, grid=(2, nc),
        in_specs=sp["hq"] + sp["z"] + sp["v"] + [sp["vec"], sp["vec"], sp["seg"], sp["later"], sp["same"], sp["sums"], sp["sums"],
                                                 sp["shared"], sp["state"], sp["weights"], sp["levels"]],
        out_specs=[sp["per_dir"], sp["per_dir"], sp["per_dir"], sp["vec"]], out_shape=[out, out, out, SDS((2, 1, HG_DIM), F32)],
        scratch_shapes=[pltpu.VMEM((HG_HEADS, HG_HEAD_DIM, HG_HEAD_DIM), F32)],
        compiler_params=_cp("parallel", "arbitrary"))(p, p, p, p, p, p, a0, a1, seg, *masks, gp, gn, do, s0, a, e)


def _hg_post(o2, p, g, name):
    n = p.shape[0]
    tr = min(ROW_TILE, n)
    w = 2 * HG_HEAD_DIM

    def body(of_ref, ob_ref, hg_ref, g_ref, o_ref):
        for j in range(2):
            sl = slice(j * HG_HEAD_DIM, (j + 1) * HG_HEAD_DIM)
            o = of_ref[0, :, sl] + ob_ref[0, :, sl]
            hg = hg_ref[:, sl]
            o_ref[:, sl] = (o * _rstd(o) * g_ref[...] * (hg * _sigmoid(hg))).astype(o_ref.dtype)

    blk = pl.BlockSpec((tr, w), lambda i, j: (i, j))
    dirs = [pl.BlockSpec((1, tr, w), lambda i, j, d=d: (d, i, j)) for d in range(2)]
    return pl.pallas_call(
        body, name=name, grid=(n // tr, HG_DIM // w),
        in_specs=dirs + [pl.BlockSpec((tr, w), lambda i, j: (i, OFF_HG // w + j)), pl.BlockSpec((1, HG_HEAD_DIM), lambda i, j: (0, 0))],
        out_specs=blk, out_shape=SDS((n, HG_DIM), MXU_DTYPE), compiler_params=_cp("parallel", "parallel"))(o2, o2, p, g)


def _hg_post_bwd(o2, p, g, dcat, name, after=()):
    n = p.shape[0]
    tr = min(ROW_TILE, n)
    w = 2 * HG_HEAD_DIM
    after, after_specs = _unread(after)

    def body(of_ref, ob_ref, hg_ref, g_ref, d_ref, *rest):
        do_ref, dhg_ref, dg_ref = rest[len(after):]

        @pl.when(pl.program_id(1) == 0)
        def _():
            dg_ref[...] = jnp.zeros_like(dg_ref)

        for j in range(2):
            sl = slice(j * HG_HEAD_DIM, (j + 1) * HG_HEAD_DIM)
            o = of_ref[0, :, sl] + ob_ref[0, :, sl]
            hg = hg_ref[:, sl]
            d = d_ref[:, sl].astype(F32)
            sg = _sigmoid(hg)
            on = o * _rstd(o) * g_ref[...]
            dhg_ref[:, sl] = (d * on * sg * (1.0 + hg * (1.0 - sg))).astype(dhg_ref.dtype)
            dx, dg = _rms_bwd(o, g_ref[...], d * hg * sg)
            do_ref[:, sl] = dx
            dg_ref[0, :, sl] += dg

    blk = pl.BlockSpec((tr, w), lambda j, i: (i, j))
    dirs = [pl.BlockSpec((1, tr, w), lambda j, i, d=d: (d, i, j)) for d in range(2)]
    return pl.pallas_call(
        body, name=name, grid=(HG_DIM // w, n // tr),
        in_specs=dirs + [pl.BlockSpec((tr, w), lambda j, i: (i, OFF_HG // w + j)), pl.BlockSpec((1, HG_HEAD_DIM), lambda j, i: (0, 0)),
                         pl.BlockSpec((tr, w), lambda j, i: (i, ATT_Q_DIM // w + j))] + after_specs,
        out_specs=[blk, blk, pl.BlockSpec((1, 1, w), lambda j, i: (j, 0, 0))],
        out_shape=[SDS((n, HG_DIM), F32), SDS((n, HG_DIM), MXU_DTYPE), SDS((HG_DIM // w, 1, w), F32)],
        compiler_params=_cp("parallel", "arbitrary"))(o2, o2, p, g, dcat, *after)


XATT_TQ = 512


def _xattn_fwd(q, kv, name):
    n, nm = q.shape[0], kv.shape[0]
    tq = min(XATT_TQ, n)
    scale = X_HEAD_DIM ** -0.5

    def body(q_ref, k_ref, v_ref, o_ref):
        s = _dot(q_ref[...], k_ref[...], "nt") * scale
        e = jnp.exp(s - jnp.max(s, axis=-1, keepdims=True))
        o_ref[...] = _dot(e / jnp.sum(e, axis=-1, keepdims=True), v_ref[...]).astype(o_ref.dtype)

    qb = pl.BlockSpec((tq, X_HEAD_DIM), lambda h, i: (i, h))
    return pl.pallas_call(
        body, name=name, grid=(X_HEADS, n // tq),
        in_specs=[qb, pl.BlockSpec((nm, X_HEAD_DIM), lambda h, i: (0, h)), pl.BlockSpec((nm, X_HEAD_DIM), lambda h, i: (0, X_HEADS + h))],
        out_specs=qb, out_shape=SDS(q.shape, MXU_DTYPE), compiler_params=_cp("parallel", "parallel"))(q, kv, kv)


def _xattn_bwd(q, kv, do, name, after=()):
    n, nm = q.shape[0], kv.shape[0]
    tq = min(XATT_TQ, n)
    scale = X_HEAD_DIM ** -0.5
    after, after_specs = _unread(after)

    def body(q_ref, k_ref, v_ref, do_ref, *rest):
        dq_ref, dk_ref, dv_ref = rest[len(after):]

        @pl.when(pl.program_id(1) == 0)
        def _():
            dk_ref[...] = jnp.zeros_like(dk_ref)
            dv_ref[...] = jnp.zeros_like(dv_ref)

        qv, dov = q_ref[...], do_ref[...]
        s = _dot(qv, k_ref[...], "nt") * scale
        e = jnp.exp(s - jnp.max(s, axis=-1, keepdims=True))
        p = e / jnp.sum(e, axis=-1, keepdims=True)
        dp = _dot(dov, v_ref[...], "nt")
        ds = p * (dp - jnp.sum(p * dp, axis=-1, keepdims=True)) * scale
        dq_ref[...] = _dot(ds, k_ref[...]).astype(dq_ref.dtype)
        dk_ref[...] += _dot(ds, qv, "tn")
        dv_ref[...] += _dot(p, dov, "tn")

    qb = pl.BlockSpec((tq, X_HEAD_DIM), lambda h, i: (i, h))
    kb = pl.BlockSpec((nm, X_HEAD_DIM), lambda h, i: (0, h))
    return pl.pallas_call(
        body, name=name, grid=(X_HEADS, n // tq),
        in_specs=[qb, kb, pl.BlockSpec((nm, X_HEAD_DIM), lambda h, i: (0, X_HEADS + h)), qb] + after_specs, out_specs=[qb, kb, kb],
        out_shape=[SDS(q.shape, MXU_DTYPE), SDS((nm, X_HEADS * X_HEAD_DIM), F32), SDS((nm, X_HEADS * X_HEAD_DIM), F32)],
        compiler_params=_cp("parallel", "arbitrary"))(q, kv, kv, do, *after)


def _edge_rows(shape):
    row = lax.broadcasted_iota(jnp.int32, shape, 0)
    return row == 0, row == shape[0] - 1


def _shift_rows(u, down, edges):
    if down:
        return jnp.where(edges[0], 0.0, pltpu.roll(u, 1, axis=0))
    return jnp.where(edges[1], 0.0, pltpu.roll(u, u.shape[0] - 1, axis=0))


def _conv(u, w, b, edges):
    return b + _shift_rows(u, True, edges) * w[0:1, :] + u * w[1:2, :] + _shift_rows(u, False, edges) * w[2:3, :]


def _ff_specs(n):
    gate = lambda rows: pl.BlockSpec((rows, FF_COLS), lambda j: (0, j))
    val = lambda rows: pl.BlockSpec((rows, FF_COLS), lambda j: (0, FF_BLOCKS + j))
    return [gate(n), val(n), gate(3), val(3), gate(1), val(1)], gate


def _conv_gate(u, cw, cb, name):
    n = u.shape[0]
    ins, gate_blk = _ff_specs(n)

    def body(ug_ref, uv_ref, wg_ref, wv_ref, bg_ref, bv_ref, o_ref):
        edges = _edge_rows(ug_ref.shape)
        gate = _conv(ug_ref[...], wg_ref[...], bg_ref[...], edges)
        val = _conv(uv_ref[...], wv_ref[...], bv_ref[...], edges)
        o_ref[...] = (gate * _sigmoid(gate) * val).astype(o_ref.dtype)

    return pl.pallas_call(
        body, name=name, grid=(FF_BLOCKS,), in_specs=ins, out_specs=gate_blk(n), out_shape=SDS((n, D_FF), MXU_DTYPE),
        compiler_params=_cp("parallel"))(u, u, cw, cw, cb, cb)


def _conv_gate_bwd(u, cw, cb, da, name, after=()):
    n = u.shape[0]
    ins, gate_blk = _ff_specs(n)
    after, after_specs = _unread(after)

    def side(dacc, u, w, edges, du_ref, dw_ref, db_ref):
        nxt, prv = _shift_rows(dacc, False, edges), _shift_rows(dacc, True, edges)
        du_ref[...] = (nxt * w[0:1, :] + dacc * w[1:2, :] + prv * w[2:3, :]).astype(du_ref.dtype)
        db_ref[...] = jnp.sum(dacc, axis=0, keepdims=True)
        dw_ref[0:1, :] = jnp.sum(nxt * u, axis=0, keepdims=True)
        dw_ref[1:2, :] = jnp.sum(dacc * u, axis=0, keepdims=True)
        dw_ref[2:3, :] = jnp.sum(prv * u, axis=0, keepdims=True)

    def body(ug_ref, uv_ref, wg_ref, wv_ref, bg_ref, bv_ref, da_ref, *rest):
        dug_ref, duv_ref, dwg_ref, dwv_ref, dbg_ref, dbv_ref = rest[len(after):]
        ug, uv = ug_ref[...], uv_ref[...]
        edges = _edge_rows(ug.shape)
        gate = _conv(ug, wg_ref[...], bg_ref[...], edges)
        val = _conv(uv, wv_ref[...], bv_ref[...], edges)
        sg = _sigmoid(gate)
        dav = da_ref[...].astype(F32)
        side(dav * val * sg * (1.0 + gate * (1.0 - sg)), ug, wg_ref[...], edges, dug_ref, dwg_ref, dbg_ref)
        side(dav * gate * sg, uv, wv_ref[...], edges, duv_ref, dwv_ref, dbv_ref)

    return pl.pallas_call(
        body, name=name, grid=(FF_BLOCKS,), in_specs=ins + [gate_blk(n)] + after_specs,
        out_specs=[gate_blk(n), gate_blk(n), gate_blk(3), gate_blk(3), gate_blk(1), gate_blk(1)],
        out_shape=[SDS((n, D_FF), MXU_DTYPE)] * 2 + [SDS((3, D_FF), F32)] * 2 + [SDS((1, D_FF), F32)] * 2,
        compiler_params=_cp("parallel"))(u, u, cw, cw, cb, cb, da, *after)


def _adamw(w, g, m, v, name):
    r, c = w.shape[-2:]
    tr = r if r <= 512 else 256 if r % 256 == 0 else 88
    assert r % tr == 0 and (w.ndim == 2 or w.shape[:-2] == (1,)), (name, w.shape, tr)

    def body(w_ref, g_ref, m_ref, v_ref, d_ref, mo_ref, vo_ref, go_ref):
        gv = g_ref[...]
        go_ref[...] = gv
        mn = ADAM_B1 * m_ref[...] + (1.0 - ADAM_B1) * gv
        vn = ADAM_B2 * v_ref[...] + (1.0 - ADAM_B2) * gv * gv
        m_hat = mn / (1.0 - ADAM_B1 ** ADAM_STEP)
        v_hat = vn / (1.0 - ADAM_B2 ** ADAM_STEP)
        d_ref[...] = -ADAM_LR * (m_hat / (jnp.sqrt(v_hat) + ADAM_EPS) + ADAM_WD * w_ref[...])
        mo_ref[...] = mn
        vo_ref[...] = vn

    blk = pl.BlockSpec((tr, c), lambda i: (i, 0)) if w.ndim == 2 else pl.BlockSpec((1, tr, c), lambda i: (0, i, 0))
    out = SDS(w.shape, F32)
    return pl.pallas_call(body, name=name, grid=(r // tr,), in_specs=[blk] * 4, out_specs=[blk] * 4, out_shape=[out] * 4,
                          compiler_params=_cp("parallel"))(w, g, m, v)


def _half_tile(h):
    tr = h if h <= 512 else 256 if h % 256 == 0 else 176
    assert h % tr == 0, (h, tr)
    return tr


ANY = pl.BlockSpec(memory_space=pl.ANY)


def _place():
    x, y, c = lax.axis_index("x"), lax.axis_index("y"), lax.axis_index("c")
    return x, y, c, [(1 - x, y), (x, 1 - y), (1 - x, 1 - y)]


def _gather_shards(shards, name):
    nt = len(shards)

    def body(*refs):
        ins, outs = refs[:nt], refs[nt:2 * nt]
        send, recv, fsend, frecv, osend, orecv = refs[2 * nt:]
        x, y, c, chips = _place()
        me = 2 * x + y

        def half(t, chip, cc):
            h = ins[t].shape[0] // 2
            return outs[t].at[chip, pl.ds(cc * h, h)]

        def ici(t, j):
            cx, cy = chips[j]
            h = ins[t].shape[0] // 2
            return pltpu.make_async_remote_copy(src_ref=ins[t].at[pl.ds(c * h, h)], dst_ref=half(t, me, c),
                                                send_sem=send.at[t, j], recv_sem=recv.at[t, j], device_id=(cx, cy, c), device_id_type=MESH)

        def landed(t, j):
            cx, cy = chips[j]
            blk = half(t, 2 * cx + cy, c)
            return pltpu.make_async_remote_copy(src_ref=blk, dst_ref=blk, send_sem=send.at[t, j], recv_sem=recv.at[t, j],
                                                device_id=(cx, cy, c), device_id_type=MESH)

        def d2d(t, j, cc):
            cx, cy = chips[j]
            blk = half(t, 2 * cx + cy, cc)
            return pltpu.make_async_remote_copy(src_ref=blk, dst_ref=blk, send_sem=fsend.at[t, j], recv_sem=frecv.at[t, j],
                                                device_id=(x, y, 1 - c), device_id_type=MESH)

        own = [pltpu.make_async_remote_copy(src_ref=ins[t], dst_ref=outs[t].at[me], send_sem=osend.at[t], recv_sem=orecv.at[t],
                                            device_id=(x, y, 1 - c), device_id_type=MESH) for t in range(nt)]
        for t in range(nt):
            for j in range(3):
                ici(t, j).start()
        for cp in own:
            cp.start()
        for t in range(nt):
            for j in range(3):
                landed(t, j).wait_recv()
                d2d(t, j, c).start()
        for t in range(nt):
            for j in range(3):
                d2d(t, j, 1 - c).wait_recv()
        for t in range(nt):
            for j in range(3):
                ici(t, j).wait_send()
                d2d(t, j, c).wait_send()
        for cp in own:
            cp.wait()

    return pl.pallas_call(
        body, name=name, in_specs=[ANY] * nt, out_specs=[ANY] * nt,
        out_shape=[SDS((4,) + s.shape, s.dtype) for s in shards],
        scratch_shapes=[pltpu.SemaphoreType.DMA((nt, 3))] * 4 + [pltpu.SemaphoreType.DMA((nt,))] * 2,
        compiler_params=pltpu.CompilerParams(has_side_effects=True))(*shards)


def _join_halves(bufs, name):
    nt = len(bufs)

    def body(*refs):
        outs = refs[nt:2 * nt]
        send, recv = refs[2 * nt:]
        x, y, c, _ = _place()
        cps = [pltpu.make_async_remote_copy(src_ref=outs[t].at[c], dst_ref=outs[t].at[c], send_sem=send.at[t], recv_sem=recv.at[t],
                                            device_id=(x, y, 1 - c), device_id_type=MESH) for t in range(nt)]
        for cp in cps:
            cp.start()
        for t in range(nt):
            theirs = outs[t].at[1 - c]
            pltpu.make_async_remote_copy(src_ref=theirs, dst_ref=theirs, send_sem=send.at[t], recv_sem=recv.at[t],
                                         device_id=(x, y, 1 - c), device_id_type=MESH).wait_recv()
        for cp in cps:
            cp.wait_send()

    return pl.pallas_call(
        body, name=name, in_specs=[ANY] * nt, out_specs=[ANY] * nt, out_shape=[SDS(b.shape, b.dtype) for b in bufs],
        input_output_aliases={t: t for t in range(nt)},
        scratch_shapes=[pltpu.SemaphoreType.DMA((nt,))] * 2,
        compiler_params=pltpu.CompilerParams(has_side_effects=True))(*bufs)


def _exchange_small(v, reduce, name, after=()):
    rows = v.shape[0]
    after, after_specs = _unread(after)

    def body(v_ref, *rest):
        o_ref, buf, send, recv = rest[-4:]
        x, y, c, _ = _place()
        me = 4 * x + 2 * y + c
        buf[me] = v_ref[...]

        def peer(dx, dy, dc):
            return (1 - x if dx else x, 1 - y if dy else y, 1 - c if dc else c)

        peers = [(dx, dy, dc) for dx in range(2) for dy in range(2) for dc in range(2) if (dx, dy, dc) != (0, 0, 0)]
        cps = []
        for j, (dx, dy, dc) in enumerate(peers):
            cps.append(pltpu.make_async_remote_copy(src_ref=v_ref, dst_ref=buf.at[me], send_sem=send.at[j], recv_sem=recv.at[j],
                                                    device_id=peer(dx, dy, dc), device_id_type=MESH))
        for cp in cps:
            cp.start()
        for j, (dx, dy, dc) in enumerate(peers):
            px, py, pc = peer(dx, dy, dc)
            blk = buf.at[4 * px + 2 * py + pc]
            pltpu.make_async_remote_copy(src_ref=blk, dst_ref=blk, send_sem=send.at[j], recv_sem=recv.at[j],
                                         device_id=(px, py, pc), device_id_type=MESH).wait_recv()
        for cp in cps:
            cp.wait_send()
        if reduce:
            acc = buf[0]
            for j in range(1, 8):
                acc = acc + buf[j]
            o_ref[...] = acc
        else:
            o_ref[...] = buf[...]

    vm = pl.BlockSpec(memory_space=pltpu.VMEM)
    return pl.pallas_call(
        body, name=name, in_specs=[vm] + after_specs, out_specs=vm, out_shape=SDS((rows, 128) if reduce else (8, rows, 128), F32),
        scratch_shapes=[pltpu.VMEM((8, rows, 128), F32), pltpu.SemaphoreType.DMA((7,)), pltpu.SemaphoreType.DMA((7,))],
        compiler_params=pltpu.CompilerParams(has_side_effects=True))(v, *after)


HBM = pl.BlockSpec(memory_space=pltpu.HBM)
SEM = pl.BlockSpec(memory_space=pltpu.SEMAPHORE)
TOKEN = pl.BlockSpec(memory_space=pltpu.VMEM)
TOKEN_SHAPE = SDS((8, 128), F32)
PEERS = 7


def _in_hbm(a):
    return pltpu.with_memory_space_constraint(a, pltpu.HBM)


def _split_params():
    return pltpu.CompilerParams(has_side_effects=pltpu.SideEffectType.DATAFLOW_SIDE_EFFECTING)


def _gather_start(shards, name, after=()):
    nt = len(shards)
    after, after_specs = _unread(after)

    def body(*refs):
        ins, lands = refs[:nt], refs[nt:2 * nt]
        outs = refs[2 * nt + len(after):]
        sends, recvs = outs[:nt], outs[nt:2 * nt]
        x, y, c, chips = _place()
        me = 2 * x + y
        for t in range(nt):
            h = ins[t].shape[0] // 2
            mine = pl.ds(c * h, h)
            for j, (cx, cy) in enumerate(chips):
                for dc in range(2):
                    pltpu.make_async_remote_copy(src_ref=ins[t].at[mine], dst_ref=lands[t].at[me, mine], send_sem=sends[t].at[2 * j + dc],
                                                 recv_sem=recvs[t].at[2 * j + c], device_id=(cx, cy, dc), device_id_type=MESH).start()
            pltpu.make_async_remote_copy(src_ref=ins[t], dst_ref=lands[t].at[me], send_sem=sends[t].at[PEERS - 1], recv_sem=recvs[t].at[PEERS - 1],
                                         device_id=(x, y, 1 - c), device_id_type=MESH).start()
        outs[-1][...] = jnp.zeros(TOKEN_SHAPE.shape, F32)

    lands = [lax.empty((4,) + s.shape, s.dtype) for s in shards]
    out = pl.pallas_call(
        body, name=name, in_specs=[HBM] * (2 * nt) + after_specs, out_specs=[SEM] * (2 * nt) + [HBM] * (2 * nt) + [TOKEN],
        out_shape=[pltpu.SemaphoreType.DMA((PEERS,))] * (2 * nt)
        + [pltpu.HBM(s.shape, s.dtype) for s in shards] + [pltpu.HBM(l.shape, l.dtype) for l in lands] + [TOKEN_SHAPE],
        input_output_aliases={t: 2 * nt + t for t in range(2 * nt)}, compiler_params=_split_params())(
            *[_in_hbm(s) for s in shards], *[_in_hbm(l) for l in lands], *after)
    return out[:nt], out[nt:2 * nt], out[2 * nt:3 * nt], out[3 * nt:4 * nt], out[-1]


def _gather_wait(sends, recvs, shards, lands, after, name):
    nt = len(shards)

    def body(*refs):
        ins, lands_ref = refs[:nt], refs[nt:2 * nt]
        send_refs, recv_refs = refs[2 * nt:3 * nt], refs[3 * nt:4 * nt]
        x, y, c, chips = _place()
        for t in range(nt):
            h = ins[t].shape[0] // 2
            for j, (cx, cy) in enumerate(chips):
                for cs in range(2):
                    blk = lands_ref[t].at[2 * cx + cy, pl.ds(cs * h, h)]
                    pltpu.make_async_remote_copy(src_ref=blk, dst_ref=blk, send_sem=send_refs[t].at[2 * j + cs], recv_sem=recv_refs[t].at[2 * j + cs],
                                                 device_id=(cx, cy, cs), device_id_type=MESH).wait()
            blk = lands_ref[t].at[2 * x + y]
            pltpu.make_async_remote_copy(src_ref=blk, dst_ref=blk, send_sem=send_refs[t].at[PEERS - 1], recv_sem=recv_refs[t].at[PEERS - 1],
                                         device_id=(x, y, 1 - c), device_id_type=MESH).wait()

    out = pl.pallas_call(
        body, name=name, in_specs=[HBM] * (2 * nt) + [SEM] * (2 * nt) + [ANY], out_specs=[HBM] * (2 * nt),
        out_shape=[pltpu.HBM(s.shape, s.dtype) for s in shards] + [pltpu.HBM(l.shape, l.dtype) for l in lands],
        input_output_aliases={t: t for t in range(2 * nt)}, compiler_params=_split_params())(*shards, *lands, *sends, *recvs, after)
    return out[nt:]


def _scatter_start(g, name):
    _, r, c_ = g.shape
    h = r // 2

    def body(g_ref, land, send, recv, g_thru, land_thru, token):
        x, y, c, chips = _place()
        for j, (cx, cy) in enumerate(chips):
            for dc in range(2):
                pltpu.make_async_remote_copy(src_ref=g_ref.at[2 * cx + cy, pl.ds(dc * h, h)], dst_ref=land.at[2 * j + c], send_sem=send.at[2 * j + dc],
                                             recv_sem=recv.at[2 * j + c], device_id=(cx, cy, dc), device_id_type=MESH).start()
        pltpu.make_async_remote_copy(src_ref=g_ref.at[2 * x + y, pl.ds((1 - c) * h, h)], dst_ref=land.at[PEERS - 1], send_sem=send.at[PEERS - 1],
                                     recv_sem=recv.at[PEERS - 1], device_id=(x, y, 1 - c), device_id_type=MESH).start()
        token[...] = jnp.zeros(TOKEN_SHAPE.shape, F32)

    land = lax.empty((PEERS, h, c_), g.dtype)
    return pl.pallas_call(
        body, name=name, in_specs=[HBM, HBM], out_specs=[SEM, SEM, HBM, HBM, TOKEN],
        out_shape=[pltpu.SemaphoreType.DMA((PEERS,)), pltpu.SemaphoreType.DMA((PEERS,)), pltpu.HBM(g.shape, g.dtype),
                   pltpu.HBM(land.shape, land.dtype), TOKEN_SHAPE],
        input_output_aliases={0: 2, 1: 3}, compiler_params=_split_params())(_in_hbm(g), _in_hbm(land))


def _scatter_wait(started, after, name):
    nt = len(started)

    def body(*refs):
        lands = refs[nt:2 * nt]
        sends, recvs = refs[2 * nt:3 * nt], refs[3 * nt:4 * nt]
        x, y, c, chips = _place()
        peers = [(cx, cy, dc) for cx, cy in chips for dc in range(2)] + [(x, y, 1 - c)]
        for t in range(nt):
            for k, peer in enumerate(peers):
                blk = lands[t].at[k]
                pltpu.make_async_remote_copy(src_ref=blk, dst_ref=blk, send_sem=sends[t].at[k], recv_sem=recvs[t].at[k],
                                             device_id=peer, device_id_type=MESH).wait()

    gs, lands = [s[2] for s in started], [s[3] for s in started]
    after, after_specs = _unread(after)
    out = pl.pallas_call(
        body, name=name, in_specs=[HBM] * (2 * nt) + [SEM] * (2 * nt) + after_specs, out_specs=[HBM] * (2 * nt),
        out_shape=[pltpu.HBM(a.shape, a.dtype) for a in gs + lands],
        input_output_aliases={t: t for t in range(2 * nt)}, compiler_params=_split_params())(
            *gs, *lands, *[s[0] for s in started], *[s[1] for s in started], *after)
    return out[:nt], out[nt:]


def _sum_devices(g, land, me, core, name):
    npeer, h, c = land.shape
    tr = _half_tile(h)
    steps = h // tr

    def body(ix_ref, own_ref, land_ref, o_ref):
        acc = own_ref[0].astype(F32)
        for j in range(npeer):
            acc = acc + land_ref[j].astype(F32)
        o_ref[0] = acc

    grid_spec = pltpu.PrefetchScalarGridSpec(
        num_scalar_prefetch=1, grid=(steps,),
        in_specs=[pl.BlockSpec((1, tr, c), lambda i, ix: (ix[0], ix[1] * steps + i, 0)), pl.BlockSpec((npeer, tr, c), lambda i, ix: (0, i, 0))],
        out_specs=pl.BlockSpec((1, tr, c), lambda i, ix: (ix[1], i, 0)))
    return pl.pallas_call(body, name=name, grid_spec=grid_spec, out_shape=SDS((2, h, c), F32),
                          compiler_params=_cp("parallel"))(jnp.stack([me, core]), g, land)


def _pack_small(parts):
    flat = jnp.concatenate([p.reshape(-1) for p in parts])
    total = flat.shape[0]
    rows = -(-total // 1024) * 8
    return jnp.pad(flat, (0, rows * 128 - total)).reshape(rows, 128)


def _unpack_small(packed, shapes):
    flat = packed.reshape(-1)
    out, off = [], 0
    for s in shapes:
        size = int(np.prod(s))
        out.append(flat[off:off + size].reshape(s))
        off += size
    return out


def _local_step(x, mem, target, w_in, first_after, mid_weights, ffn_weights, on_grad, gains, conv_w, conv_b, hg_lb):
    n = x.shape[0]
    cos, sin = _rope_tables(n)
    seg = _hg_segments()
    gp, gn = _hg_pair_sums()
    masks = _hg_level_masks()
    gq2 = jnp.tile(gains["q_norm_g"], (1, 2))
    gk2 = jnp.tile(gains["k_norm_g"], (1, 2))
    a0 = hg_lb[:, 0:1, :]
    a1 = hg_lb[:, 1:2, :]

    p, h1 = _norm_mm(x, gains["pre_mix_g"], w_in, F32, TOKEN_TILE, 1664, "in_proj", after=(first_after,))
    qr, kr = _qk_prep(p, gq2, gk2, cos, sin, "qk_prep")
    heads = lambda a: a.reshape(n, ATT_KV_HEADS, ATT_HEAD_DIM).transpose(1, 0, 2)
    kh = heads(kr)
    vh = heads(p[:, OFF_AV:OFF_AV + ATT_KV_DIM].astype(MXU_DTYPE))
    att = _attn_fwd(qr, kh, vh, "attn_fwd")
    o2, s0, hg_a, hg_e = _hgrn_fwd(p, a0, a1, seg, masks, "hgrn_fwd")
    rec = _hg_post(o2, p, gains["hg_out_norm_g"], "hg_post")
    cat = jnp.concatenate([att, rec], axis=1)
    w_out, w_xq, w_xkv, w_xo = mid_weights(cat)
    mixed, x1 = _mm_resid_norm(cat, w_out, x, gains["post_mix_g"], 512, "out_proj_resid")
    xq, h2 = _norm_mm(x1, gains["pre_x_g"], w_xq, MXU_DTYPE, TOKEN_TILE, 1024, "xq_proj")
    kv, mn = _norm_mm(mem, gains["mem_norm_g"], w_xkv, MXU_DTYPE, 256, 2048, "xkv_proj")
    ox = _xattn_fwd(xq, kv, "xattn_fwd")
    xo, x2 = _mm_resid_norm(ox, w_xo, x1, gains["post_x_g"], 512, "xo_proj_resid")
    w_up = ffn_weights("w_up", x2)
    u, h3 = _norm_mm(x2, gains["pre_ffn_g"], w_up, F32, TOKEN_TILE, 1408, "up_proj")
    act = _conv_gate(u, conv_w, conv_b, "conv_gate")
    w_down = ffn_weights("w_down", act)
    dn, d3, loss = _mm_resid_norm(act, w_down, x2, gains["post_ffn_g"], 512, "down_proj_resid_loss", target=target)

    gs = {}
    d_act, d_dn, gs["post_ffn_g"] = _norm_bwd_mm(dn, gains["post_ffn_g"], d3, w_down, F32, 512, 1408, "ffn_post_bwd_down_dx")
    tok = on_grad("w_down", _mm(act, d_dn, "tn", WIRE_DTYPE, 1408, 1024, "down_dw"))
    du_g, du_v, dcw_g, dcw_v, dcb_g, dcb_v = _conv_gate_bwd(u, conv_w, conv_b, d_act, "conv_gate_bwd", after=(tok,))
    gs["conv_w"] = jnp.concatenate([dcw_g, dcw_v], axis=1)
    gs["conv_b"] = jnp.concatenate([dcb_g, dcb_v], axis=1)
    ff_shard = w_up.shape[2]
    g_up = _dw_by_owner(h3, du_g, ff_shard, 0, None, 512, "up_dw_gate")
    tok = on_grad("w_up", _dw_by_owner(h3, du_v, ff_shard, 2, g_up, 512, "up_dw_value"))
    d2, gs["pre_ffn_g"] = _dx_norm_bwd([(du_g, 0), (du_g, 1), (du_v, 0), (du_v, 1)], w_up, x2, gains["pre_ffn_g"], d3, 512,
                                       "up_dx_pre_bwd", after=(tok,))
    d_ox, d_xo, gs["post_x_g"] = _norm_bwd_mm(xo, gains["post_x_g"], d2, w_xo, MXU_DTYPE, 512, 1024, "x_post_bwd_xo_dx")
    tok = on_grad("w_xo", _mm(ox, d_xo, "tn", WIRE_DTYPE, 512, 1024, "xo_dw"))
    d_xq, d_k, d_v = _xattn_bwd(xq, kv, d_ox, "xattn_bwd", after=(tok,))
    d_kv = jnp.concatenate([d_k, d_v], axis=1).astype(MXU_DTYPE)
    tok = on_grad("w_xq", _mm(h2, d_xq, "tn", WIRE_DTYPE, 512, 1024, "xq_dw"))
    tok_kv = on_grad("w_xkv", _dw_by_owner(mn, d_kv, w_xkv.shape[2], 0, None, 512, "xkv_dw"))
    d1, gs["pre_x_g"] = _dx_norm_bwd([(d_xq, 0)], w_xq[None], x1, gains["pre_x_g"], d2, 512, "xq_dx_pre_bwd", after=(tok, tok_kv))
    d_mn = _mm_nt_parts([(d_kv, s) for s in range(4)], w_xkv, F32, 256, 1024, "xkv_dx")
    _, gs["mem_norm_g"] = _norm_bwd(mem, gains["mem_norm_g"], d_mn, None, MXU_DTYPE, "mem_norm_bwd")
    d_cat, d_mixed, gs["post_mix_g"] = _norm_bwd_mm(mixed, gains["post_mix_g"], d1, w_out, MXU_DTYPE, 512, 1024, "mix_post_bwd_out_dx")
    tok = on_grad("w_out", _mm(cat, d_mixed, "tn", WIRE_DTYPE, 512, 1024, "out_dw"))
    d_o, d_hg, dg_hg = _hg_post_bwd(o2, p, gains["hg_out_norm_g"], d_cat, "hg_post_bwd", after=(tok,))
    gs["hg_out_norm_g"] = dg_hg.reshape(HG_HEADS, HG_HEAD_DIM).sum(axis=0, keepdims=True)
    dhq2, dz2, dhv2, dlb = _hgrn_bwd(p, a0, a1, seg, masks, gp, gn, d_o, s0, hg_a, hg_e, "hgrn_bwd")
    lb = jax.nn.sigmoid(a0 - a1)
    da0 = dlb * lb * (1.0 - lb)
    gs["hg_lb"] = jnp.concatenate([da0, -da0], axis=1)
    d_qr, d_kh, d_vh = _attn_bwd(qr, kh, vh, cat, d_cat, "attn_bwd")
    unheads = lambda a: a.transpose(2, 0, 1).reshape(n, ATT_KV_DIM)
    d_aq, d_ak, dgq, dgk = _qk_prep_bwd(p, gq2, gk2, cos, sin, d_qr, unheads(d_kh), "qk_prep_bwd")
    gs["q_norm_g"] = dgq.reshape(ATT_HEADS, ATT_HEAD_DIM).sum(axis=0, keepdims=True)
    gs["k_norm_g"] = dgk.reshape(ATT_KV_HEADS, ATT_HEAD_DIM).sum(axis=0, keepdims=True)
    d_p = jnp.concatenate([d_aq, d_ak, unheads(d_vh).astype(MXU_DTYPE), (dhq2[0] + dhq2[1]).astype(MXU_DTYPE),
                           dz2[0].astype(MXU_DTYPE), dz2[1].astype(MXU_DTYPE), (dhv2[0] + dhv2[1]).astype(MXU_DTYPE), d_hg], axis=1)
    tok = on_grad("w_in", _mm(h1, d_p, "tn", WIRE_DTYPE, 512, 1664, "in_dw"))
    grad_x, gs["pre_mix_g"] = _dx_norm_bwd([(d_p, 0)], w_in[None], x, gains["pre_mix_g"], d1, 512, "in_dx_pre_bwd", after=(tok,))
    return loss, grad_x, gs


MATS = ("w_in", "w_out", "w_xq", "w_xkv", "w_xo", "w_up", "w_down")
GAINS = ("pre_mix_g", "q_norm_g", "k_norm_g", "hg_out_norm_g", "post_mix_g", "pre_x_g", "mem_norm_g", "post_x_g", "pre_ffn_g", "post_ffn_g")
WEIGHTS = ('pre_mix_g', 'w_in', 'q_norm_g', 'k_norm_g', 'hg_lb', 'hg_out_norm_g', 'w_out', 'post_mix_g', 'pre_x_g', 'mem_norm_g', 'w_xq',
           'w_xkv', 'w_xo', 'post_x_g', 'pre_ffn_g', 'w_up', 'conv_w', 'conv_b', 'w_down', 'post_ffn_g')


def kernel(x, mem, pre_mix_g, w_in, q_norm_g, k_norm_g, hg_lb, hg_out_norm_g, w_out, post_mix_g, pre_x_g, mem_norm_g, w_xq, w_xkv, w_xo, post_x_g, pre_ffn_g, w_up, conv_w, conv_b, w_down, post_ffn_g, loss_target, m_pre_mix_g, m_w_in, m_q_norm_g, m_k_norm_g, m_hg_lb, m_hg_out_norm_g, m_w_out, m_post_mix_g, m_pre_x_g, m_mem_norm_g, m_w_xq, m_w_xkv, m_w_xo, m_post_x_g, m_pre_ffn_g, m_w_up, m_conv_w, m_conv_b, m_w_down, m_post_ffn_g, v_pre_mix_g, v_w_in, v_q_norm_g, v_k_norm_g, v_hg_lb, v_hg_out_norm_g, v_w_out, v_post_mix_g, v_pre_x_g, v_mem_norm_g, v_w_xq, v_w_xkv, v_w_xo, v_post_x_g, v_pre_ffn_g, v_w_up, v_conv_w, v_conv_b, v_w_down, v_post_ffn_g):
    args = dict(locals())
    w = {k: args[k] for k in WEIGHTS}
    m = {k: args["m_" + k] for k in WEIGHTS}
    v = {k: args["v_" + k] for k in WEIGHTS}
    chip = 2 * lax.axis_index("x") + lax.axis_index("y")
    core = lax.axis_index("c")

    shards = {k: w[k][0].astype(WIRE_DTYPE) for k in MATS}

    def whole(k, g):
        return g if k in ("w_xkv", "w_up") else g.reshape(-1, g.shape[-1])

    w_in_shards = _gather_shards([shards["w_in"]], "gather_w_in")[0]
    w_in_full = jnp.concatenate([w_in_shards[s] for s in range(4)], axis=1)
    small_in = _exchange_small(_pack_small([w["conv_w"][0], w["hg_lb"]]), False, "gather_small")
    mid_names, ffn_names = ("w_out", "w_xq", "w_xkv", "w_xo"), ("w_up", "w_down")
    mid = _gather_start([shards[k] for k in mid_names], "gather_mid_start", after=(w_in_full, small_in))
    ffn = _gather_start([shards[k] for k in ffn_names], "gather_ffn_start", after=(mid[4],))

    def mid_weights(after):
        return [whole(k, g) for k, g in zip(mid_names, _gather_wait(*mid[:4], after, "gather_mid_wait"))]

    def ffn_weights(k, after):
        t = ffn_names.index(k)
        return whole(k, _gather_wait(*[part[t:t + 1] for part in ffn[:4]], after, "gather_wait_" + k)[0])

    cw_parts, lb_parts = [], []
    for s in range(4):
        cw_s, lb_s = _unpack_small(small_in[2 * s], [w["conv_w"][0].shape, w["hg_lb"].shape])
        cw_parts.append(cw_s)
        lb_parts.append(lb_s)
    conv_w_full = jnp.concatenate(cw_parts, axis=1)
    hg_lb_full = jnp.concatenate(lb_parts, axis=2)

    started = {}

    def on_grad(k, g):
        if k == "w_in":
            g = g.reshape(g.shape[0], 4, g.shape[1] // 4).transpose(1, 0, 2)
        elif g.ndim == 2:
            g = g.reshape(4, g.shape[0] // 4, g.shape[1])
        *started[k], token = _scatter_start(g, "grad_start_" + k)
        return token

    gains = {k: w[k] for k in GAINS}
    loss_part, grad_x, gs = _local_step(x[0], mem[0], loss_target[0], w_in_full, ffn[4], mid_weights, ffn_weights, on_grad, gains,
                                        conv_w_full, w["conv_b"], hg_lb_full)
    gs["loss"] = loss_part

    grads, delta, new_m, new_v = {}, {}, {}, {}

    def reduce_matrices(names, after, tag):
        sent, landed = _scatter_wait([started[k] for k in names], after, "grad_wait_" + tag)
        halves = [_sum_devices(g, land, chip, core, "grad_sum_" + k) for k, g, land in zip(names, sent, landed)]
        for k, r in zip(names, _join_halves(halves, "grad_join_" + tag)):
            grads[k] = r.reshape(1, -1, r.shape[-1])

    def adamw(names):
        for k in names:
            shape = w[k].shape
            keep = len(shape) == 3 and shape[0] == 1
            two_d = lambda a: a.reshape(shape) if keep else a.reshape(-1, shape[-1])
            d, mo, vo, go = _adamw(two_d(w[k]), two_d(grads[k]), two_d(m[k]), two_d(v[k]), "adamw_" + k)
            delta[k], new_m[k], new_v[k], grads[k] = d.reshape(shape), mo.reshape(shape), vo.reshape(shape), go.reshape(shape)

    early = tuple(k for k in MATS if k != "w_in")
    reduce_matrices(early, (grad_x,), "early")
    adamw(early)

    small_names = GAINS + ("conv_b", "conv_w", "hg_lb")
    packed = _pack_small([gs[k] for k in small_names + ("loss",)])
    reduced_small = _exchange_small(packed, True, "reduce_small", after=tuple(new_v[k] for k in early))
    *summed, loss = _unpack_small(reduced_small, [gs[k].shape for k in small_names + ("loss",)])
    loss = loss[0, 0]
    for k, g in zip(small_names, summed):
        grads[k] = g
    ncw = w["conv_w"].shape[2]
    grads["conv_w"] = lax.dynamic_slice_in_dim(grads["conv_w"], chip * ncw, ncw, axis=1)[None]
    nlb = w["hg_lb"].shape[2]
    grads["hg_lb"] = lax.dynamic_slice_in_dim(grads["hg_lb"], chip * nlb, nlb, axis=2)
    replicated = GAINS + ("conv_b",)
    shapes = [w[k].shape for k in replicated]
    rows = sum(int(np.prod(s)) for s in shapes) // 128
    pack = lambda d: jnp.concatenate([d[k].reshape(-1) for k in replicated]).reshape(rows, 128)
    outs = _adamw(pack(w), reduced_small[:rows], pack(m), pack(v), "adamw_replicated")
    for into, packed_out in zip((delta, new_m, new_v, grads), outs):
        for k, a in zip(replicated, _unpack_small(packed_out, shapes)):
            into[k] = a
    adamw(("conv_w", "hg_lb"))

    reduce_matrices(("w_in",), tuple(new_v[k] for k in early + small_names), "late")
    adamw(("w_in",))
    return (loss, grad_x[None], *[grads[k] for k in WEIGHTS], *[delta[k] for k in WEIGHTS],
            *[new_m[k] for k in WEIGHTS], *[new_v[k] for k in WEIGHTS])
```

```python
import numpy as np
import jax
import jax.numpy as jnp
from jax import lax
from jax.experimental import pallas as pl
from jax.experimental.pallas import tpu as pltpu

F32 = jnp.float32
MXU_DTYPE = jnp.bfloat16
WIRE_DTYPE = jnp.bfloat16
VMEM_LIMIT_BYTES = 56 * 1024 * 1024
EPS = 1e-6
MESH = pl.DeviceIdType.MESH

GRID_W = 64
ATT_HEADS, ATT_KV_HEADS, ATT_HEAD_DIM = 8, 2, 64
ATT_GROUP = ATT_HEADS // ATT_KV_HEADS
ATT_Q_DIM, ATT_KV_DIM = 512, 128
ROPE_THETA = 10000.0
HG_HEADS, HG_HEAD_DIM, HG_DIM = 4, 128, 512
HG_CHUNK = 128
HG_LEVELS = 7
HG_PAIR = 2 * HG_HEAD_DIM
X_HEADS, X_HEAD_DIM = 4, 256
D_FF = 2816
FF_COLS = 256
FF_BLOCKS = D_FF // FF_COLS
OFF_AK, OFF_AV, OFF_HQ, OFF_ZF, OFF_ZB, OFF_HI, OFF_HG = 512, 640, 768, 1280, 1792, 2304, 2816

ADAM_LR, ADAM_B1, ADAM_B2, ADAM_EPS, ADAM_WD, ADAM_STEP = 0.001, 0.9, 0.999, 1e-08, 0.01, 10

SDS = jax.ShapeDtypeStruct


def _cp(*sem):
    return pltpu.CompilerParams(dimension_semantics=sem, vmem_limit_bytes=VMEM_LIMIT_BYTES)


def _dot(a, b, form="nn"):
    dims = {"nn": (((1,), (0,)), ((), ())), "nt": (((1,), (1,)), ((), ())), "tn": (((0,), (0,)), ((), ()))}[form]
    return lax.dot_general(a.astype(MXU_DTYPE), b.astype(MXU_DTYPE), dims, preferred_element_type=F32)


def _sigmoid(x):
    return 1.0 / (1.0 + jnp.exp(-x))


def _rstd(x):
    return lax.rsqrt(jnp.mean(x * x, axis=-1, keepdims=True) + EPS)


def _rms_bwd(x, g, dy):
    r = _rstd(x)
    xh = x * r
    dn = dy * g
    dx = r * (dn - xh * jnp.mean(dn * xh, axis=-1, keepdims=True))
    return dx, jnp.sum(dy * xh, axis=0, keepdims=True)


def _unread(after):
    after = tuple(a for a in after if a is not None)
    return after, [pl.BlockSpec(memory_space=pl.ANY)] * len(after)


def _mm(a, b, form, out_dtype, tm, tn, name, after=()):
    after, after_specs = _unread(after)
    if form == "nn":
        (m, k), n = a.shape, b.shape[1]
    elif form == "nt":
        (m, k), n = a.shape, b.shape[0]
    else:
        (k, m), n = a.shape, b.shape[1]
    tm, tn = min(tm, m), min(tn, n)
    assert m % tm == 0 and n % tn == 0, (name, m, n, tm, tn)

    def body(a_ref, b_ref, *rest):
        o_ref = rest[-1]
        o_ref[...] = _dot(a_ref[...], b_ref[...], form).astype(o_ref.dtype)

    a_spec = pl.BlockSpec((k, tm), lambda i, j: (0, i)) if form == "tn" else pl.BlockSpec((tm, k), lambda i, j: (i, 0))
    b_spec = pl.BlockSpec((tn, k), lambda i, j: (j, 0)) if form == "nt" else pl.BlockSpec((k, tn), lambda i, j: (0, j))
    return pl.pallas_call(
        body, name=name, grid=(m // tm, n // tn), in_specs=[a_spec, b_spec] + after_specs,
        out_specs=pl.BlockSpec((tm, tn), lambda i, j: (i, j)), out_shape=SDS((m, n), out_dtype),
        compiler_params=_cp("parallel", "parallel"))(a, b, *after)


def _mm_nt_parts(a_parts, b, out_dtype, tm, tn, name, after=()):
    after, after_specs = _unread(after)
    parts, n, p = b.shape
    m = a_parts[0][0].shape[0]
    tm, tn = min(tm, m), min(tn, n)
    assert m % tm == 0 and n % tn == 0 and len(a_parts) == parts, (name, m, b.shape)

    def body(*refs):
        o_ref = refs[-1]
        acc = _dot(refs[0][...], refs[parts][0], "nt")
        for s in range(1, parts):
            acc = acc + _dot(refs[s][...], refs[parts + s][0], "nt")
        o_ref[...] = acc.astype(o_ref.dtype)

    a_specs = [pl.BlockSpec((tm, p), lambda i, j, cb=cb: (i, cb)) for _, cb in a_parts]
    b_specs = [pl.BlockSpec((1, tn, p), lambda i, j, s=s: (s, j, 0)) for s in range(parts)]
    return pl.pallas_call(
        body, name=name, grid=(m // tm, n // tn), in_specs=a_specs + b_specs + after_specs,
        out_specs=pl.BlockSpec((tm, tn), lambda i, j: (i, j)), out_shape=SDS((m, n), out_dtype),
        compiler_params=_cp("parallel", "parallel"))(*[arr for arr, _ in a_parts], *([b] * parts), *after)


def _norm_bwd_mm(y, g, d, w, out_dtype, tm, tn, name):
    n, dm = y.shape
    nn = w.shape[0]
    tm, tn = min(tm, n), min(tn, nn)
    assert n % tm == 0 and nn % tn == 0 and w.shape[1] == dm, (name, y.shape, w.shape)

    def body(y_ref, g_ref, d_ref, w_ref, dx_ref, dy_ref, dg_ref, dys):
        i, j = pl.program_id(0), pl.program_id(1)

        @pl.when(jnp.logical_and(i == 0, j == 0))
        def _():
            dg_ref[...] = jnp.zeros_like(dg_ref)

        @pl.when(j == 0)
        def _():
            dy, dg = _rms_bwd(y_ref[...], g_ref[...], d_ref[...])
            dy = dy.astype(MXU_DTYPE)
            dys[...] = dy
            dy_ref[...] = dy
            dg_ref[...] += dg

        dx_ref[...] = _dot(dys[...], w_ref[...], "nt").astype(dx_ref.dtype)

    row = pl.BlockSpec((tm, dm), lambda i, j: (i, 0))
    vec = pl.BlockSpec((1, dm), lambda i, j: (0, 0))
    return pl.pallas_call(
        body, name=name, grid=(n // tm, nn // tn), in_specs=[row, vec, row, pl.BlockSpec((tn, dm), lambda i, j: (j, 0))],
        out_specs=[pl.BlockSpec((tm, tn), lambda i, j: (i, j)), row, vec],
        out_shape=[SDS((n, nn), out_dtype), SDS((n, dm), MXU_DTYPE), SDS((1, dm), F32)],
        scratch_shapes=[pltpu.VMEM((tm, dm), MXU_DTYPE)],
        compiler_params=_cp("arbitrary", "arbitrary"))(y, g, d, w)


def _mm_resid_norm(a, b, x, g, tm, name, target=None):
    n, k = a.shape
    d = b.shape[1]
    tm = min(tm, n)
    assert n % tm == 0 and x.shape == (n, d), (name, a.shape, b.shape)
    with_loss = target is not None

    def body(a_ref, b_ref, x_ref, g_ref, *rest):
        y = _dot(a_ref[...], b_ref[...])
        out = x_ref[...] + y * _rstd(y) * g_ref[...]
        if not with_loss:
            y_ref, o_ref = rest
            y_ref[...] = y
            o_ref[...] = out
            return
        t_ref, y_ref, d_ref, l_ref = rest
        y_ref[...] = y
        diff = out - t_ref[...]
        d_ref[...] = diff * (1.0 / d)

        @pl.when(pl.program_id(0) == 0)
        def _():
            l_ref[...] = jnp.zeros_like(l_ref)

        l_ref[...] += 0.5 * jnp.sum(jnp.mean(diff * diff, axis=-1, keepdims=True), axis=0, keepdims=True)

    row = pl.BlockSpec((tm, d), lambda i: (i, 0))
    ins = [pl.BlockSpec((tm, k), lambda i: (i, 0)), pl.BlockSpec((k, d), lambda i: (0, 0)), row, pl.BlockSpec((1, d), lambda i: (0, 0))]
    out = SDS((n, d), F32)
    if with_loss:
        return pl.pallas_call(body, name=name, grid=(n // tm,), in_specs=ins + [row], out_specs=[row, row, pl.BlockSpec((1, 1), lambda i: (0, 0))],
                              out_shape=[out, out, SDS((1, 1), F32)], compiler_params=_cp("arbitrary"))(a, b, x, g, target)
    return pl.pallas_call(body, name=name, grid=(n // tm,), in_specs=ins, out_specs=[row, row], out_shape=[out, out],
                          compiler_params=_cp("parallel"))(a, b, x, g)


def _dx_norm_bwd(a_parts, b, x, g, res, tm, name, after=()):
    after, after_specs = _unread(after)
    parts, d, p = b.shape
    n = x.shape[0]
    tm = min(tm, n)
    assert n % tm == 0 and len(a_parts) == parts and x.shape[1] == d, (name, x.shape, b.shape)

    def body(*refs):
        x_ref, g_ref, res_ref = refs[2 * parts:2 * parts + 3]
        dx_ref, dg_ref = refs[-2:]
        dh = _dot(refs[0][...], refs[parts][0], "nt")
        for s in range(1, parts):
            dh = dh + _dot(refs[s][...], refs[parts + s][0], "nt")
        dx, dg = _rms_bwd(x_ref[...], g_ref[...], dh)
        dx_ref[...] = dx + res_ref[...]

        @pl.when(pl.program_id(0) == 0)
        def _():
            dg_ref[...] = jnp.zeros_like(dg_ref)

        dg_ref[...] += dg

    a_specs = [pl.BlockSpec((tm, p), lambda i, cb=cb: (i, cb)) for _, cb in a_parts]
    b_specs = [pl.BlockSpec((1, d, p), lambda i, s=s: (s, 0, 0)) for s in range(parts)]
    row = pl.BlockSpec((tm, d), lambda i: (i, 0))
    vec = pl.BlockSpec((1, d), lambda i: (0, 0))
    return pl.pallas_call(
        body, name=name, grid=(n // tm,), in_specs=a_specs + b_specs + [row, vec, row] + after_specs,
        out_specs=[row, vec], out_shape=[SDS((n, d), F32), SDS((1, d), F32)],
        compiler_params=_cp("arbitrary"))(*[arr for arr, _ in a_parts], *([b] * parts), x, g, res, *after)


def _dw_by_owner(a, b, tn, first, into, tm, name):
    k, m = a.shape
    cnt = b.shape[1] // tn
    tm = min(tm, m)
    assert m % tm == 0 and b.shape[1] == cnt * tn and first + cnt <= 4, (name, a.shape, b.shape)

    def body(a_ref, b_ref, *rest):
        rest[-1][0] = _dot(a_ref[...], b_ref[...], "tn").astype(rest[-1].dtype)

    extra = [] if into is None else [into]
    return pl.pallas_call(
        body, name=name, grid=(m // tm, cnt),
        in_specs=[pl.BlockSpec((k, tm), lambda i, j: (0, i)), pl.BlockSpec((k, tn), lambda i, j: (0, j))] + [pl.BlockSpec(memory_space=pl.ANY)] * len(extra),
        out_specs=pl.BlockSpec((1, tm, tn), lambda i, j: (first + j, i, 0)), out_shape=SDS((4, m, tn), WIRE_DTYPE),
        input_output_aliases={2: 0} if extra else {},
        compiler_params=_cp("parallel", "parallel"))(a, b, *extra)


def _norm_mm(x, g, w, out_dtype, tm, tn, name, after=()):
    after, after_specs = _unread(after)
    m, d = x.shape
    sharded = w.ndim == 3
    n = w.shape[-1] * (w.shape[0] if sharded else 1)
    tm, tn = min(tm, m), (w.shape[-1] if sharded else min(tn, n))
    assert m % tm == 0 and n % tn == 0, (name, m, n, tm, tn)

    def body(x_ref, g_ref, w_ref, *rest):
        o_ref, h_ref, hs = rest[-3:]

        @pl.when(pl.program_id(1) == 0)
        def _():
            xv = x_ref[...]
            h = (xv * _rstd(xv) * g_ref[...]).astype(MXU_DTYPE)
            hs[...] = h
            h_ref[...] = h

        o_ref[...] = _dot(hs[...], w_ref[0] if sharded else w_ref[...]).astype(o_ref.dtype)

    w_spec = pl.BlockSpec((1, d, tn), lambda i, j: (j, 0, 0)) if sharded else pl.BlockSpec((d, tn), lambda i, j: (0, j))
    return pl.pallas_call(
        body, name=name, grid=(m // tm, n // tn),
        in_specs=[pl.BlockSpec((tm, d), lambda i, j: (i, 0)), pl.BlockSpec((1, d), lambda i, j: (0, 0)), w_spec] + after_specs,
        out_specs=[pl.BlockSpec((tm, tn), lambda i, j: (i, j)), pl.BlockSpec((tm, d), lambda i, j: (i, 0))],
        out_shape=[SDS((m, n), out_dtype), SDS((m, d), MXU_DTYPE)],
        scratch_shapes=[pltpu.VMEM((tm, d), MXU_DTYPE)],
        compiler_params=_cp("parallel", "arbitrary"))(x, g, w, *after)


ROW_TILE = 512
TOKEN_TILE = 1024


def _norm_bwd(x, g, dy, res, out_dtype, name):
    n, d = x.shape
    tr = min(ROW_TILE, n)
    has_res = res is not None

    def body(*refs):
        x_ref, g_ref, dy_ref = refs[:3]
        dx_ref, dg_ref = refs[-2:]
        dx, dg = _rms_bwd(x_ref[...], g_ref[...], dy_ref[...].astype(F32))
        if has_res:
            dx = dx + refs[3][...]
        dx_ref[...] = dx.astype(dx_ref.dtype)

        @pl.when(pl.program_id(0) == 0)
        def _():
            dg_ref[...] = jnp.zeros_like(dg_ref)

        dg_ref[...] += dg

    row = pl.BlockSpec((tr, d), lambda i: (i, 0))
    vec = pl.BlockSpec((1, d), lambda i: (0, 0))
    ins = [x, g, dy] + ([res] if has_res else [])
    return pl.pallas_call(
        body, name=name, grid=(n // tr,), in_specs=[row, vec, row] + ([row] if has_res else []),
        out_specs=[row, vec], out_shape=[SDS((n, d), out_dtype), SDS((1, d), F32)],
        compiler_params=_cp("arbitrary"))(*ins)


def _rope_tables(n):
    pairs = ATT_HEAD_DIM // 4
    t = np.arange(n)
    inv = np.power(ROPE_THETA, -np.arange(pairs, dtype=np.float32) / pairs).astype(np.float32)
    ang = np.concatenate([(t // GRID_W)[:, None].astype(np.float32) * inv, (t % GRID_W)[:, None].astype(np.float32) * inv], axis=-1)
    cos = np.repeat(np.cos(ang), 2, axis=-1)
    sin = np.repeat(np.sin(ang), 2, axis=-1) * np.tile(np.array([-1.0, 1.0], np.float32), ATT_HEAD_DIM // 2)
    return jnp.asarray(np.tile(cos, 2), F32), jnp.asarray(np.tile(sin, 2), F32)


def _swap_pairs(x):
    lane = lax.broadcasted_iota(jnp.int32, x.shape, 1)
    return jnp.where((lane & 1) == 0, pltpu.roll(x, 127, axis=1), pltpu.roll(x, 1, axis=1))


def _head_mean(v):
    lane = lax.broadcasted_iota(jnp.int32, v.shape, 1)
    lo = jnp.where(lane < ATT_HEAD_DIM, v, 0.0)
    s0 = jnp.sum(lo, axis=-1, keepdims=True)
    s1 = jnp.sum(v - lo, axis=-1, keepdims=True)
    return jnp.where(lane < ATT_HEAD_DIM, s0, s1) * (1.0 / ATT_HEAD_DIM)


def _qk_prep(p, gq, gk, cos, sin, name):
    n = p.shape[0]
    tr = min(ROW_TILE, n)

    def one(xv, g, c, s):
        xn = xv * lax.rsqrt(_head_mean(xv * xv) + EPS) * g
        return xn * c + _swap_pairs(xn) * s

    def body(q_ref, k_ref, gq_ref, gk_ref, c_ref, s_ref, qo_ref, ko_ref):
        c, s = c_ref[...], s_ref[...]
        for j in range(ATT_Q_DIM // 128):
            qo_ref[:, j * 128:(j + 1) * 128] = one(q_ref[:, j * 128:(j + 1) * 128], gq_ref[...], c, s).astype(qo_ref.dtype)
        ko_ref[...] = one(k_ref[...], gk_ref[...], c, s).astype(ko_ref.dtype)

    vec = pl.BlockSpec((1, 128), lambda i: (0, 0))
    tab = pl.BlockSpec((tr, 128), lambda i: (i, 0))
    return pl.pallas_call(
        body, name=name, grid=(n // tr,),
        in_specs=[pl.BlockSpec((tr, ATT_Q_DIM), lambda i: (i, 0)), pl.BlockSpec((tr, 128), lambda i: (i, OFF_AK // 128)), vec, vec, tab, tab],
        out_specs=[pl.BlockSpec((tr, ATT_Q_DIM), lambda i: (i, 0)), tab],
        out_shape=[SDS((n, ATT_Q_DIM), MXU_DTYPE), SDS((n, ATT_KV_DIM), MXU_DTYPE)],
        compiler_params=_cp("parallel"))(p, p, gq, gk, cos, sin)


def _qk_prep_bwd(p, gq, gk, cos, sin, dq, dk, name):
    n = p.shape[0]
    tr = min(ROW_TILE, n)

    def one(xv, g, c, s, dout):
        dxn = dout * c + _swap_pairs(dout * s)
        r = lax.rsqrt(_head_mean(xv * xv) + EPS)
        xh = xv * r
        dn = dxn * g
        dx = r * (dn - xh * _head_mean(dn * xh))
        return dx, jnp.sum(dxn * xh, axis=0, keepdims=True)

    def body(q_ref, k_ref, gq_ref, gk_ref, c_ref, s_ref, dq_ref, dk_ref, dqo_ref, dko_ref, dgq_ref, dgk_ref):
        @pl.when(pl.program_id(0) == 0)
        def _():
            dgq_ref[...] = jnp.zeros_like(dgq_ref)
            dgk_ref[...] = jnp.zeros_like(dgk_ref)

        c, s = c_ref[...], s_ref[...]
        for j in range(ATT_Q_DIM // 128):
            sl = slice(j * 128, (j + 1) * 128)
            dx, dg = one(q_ref[:, sl], gq_ref[...], c, s, dq_ref[:, sl])
            dqo_ref[:, sl] = dx.astype(dqo_ref.dtype)
            dgq_ref[:, sl] += dg
        dx, dg = one(k_ref[...], gk_ref[...], c, s, dk_ref[...])
        dko_ref[...] = dx.astype(dko_ref.dtype)
        dgk_ref[...] += dg

    vec = pl.BlockSpec((1, 128), lambda i: (0, 0))
    tab = pl.BlockSpec((tr, 128), lambda i: (i, 0))
    qrow = pl.BlockSpec((tr, ATT_Q_DIM), lambda i: (i, 0))
    return pl.pallas_call(
        body, name=name, grid=(n // tr,),
        in_specs=[qrow, pl.BlockSpec((tr, 128), lambda i: (i, OFF_AK // 128)), vec, vec, tab, tab, qrow, tab],
        out_specs=[qrow, tab, pl.BlockSpec((1, ATT_Q_DIM), lambda i: (0, 0)), vec],
        out_shape=[SDS((n, ATT_Q_DIM), MXU_DTYPE), SDS((n, ATT_KV_DIM), MXU_DTYPE), SDS((1, ATT_Q_DIM), F32), SDS((1, 128), F32)],
        compiler_params=_cp("arbitrary"))(p, p, gq, gk, cos, sin, dq, dk)


ATT_TQ = 256


def _attn_fwd(q, k, v, name):
    n = q.shape[0]
    tq = min(ATT_TQ, n)
    scale = ATT_HEAD_DIM ** -0.5
    gw = ATT_GROUP * ATT_HEAD_DIM

    def body(q_ref, k_ref, v_ref, o_ref):
        kk, vv = k_ref[0], v_ref[0]
        v_ones = jnp.concatenate([vv, jnp.ones_like(vv)], axis=1)
        outs = []
        for g in range(ATT_GROUP):
            s = _dot(q_ref[:, g * ATT_HEAD_DIM:(g + 1) * ATT_HEAD_DIM] * scale, kk, "nt")
            e = jnp.exp(s - jnp.max(s, axis=-1, keepdims=True))
            ov = _dot(e, v_ones)
            outs.append(ov[:, :ATT_HEAD_DIM] / ov[:, ATT_HEAD_DIM:])
        o_ref[...] = jnp.concatenate(outs, axis=-1).astype(o_ref.dtype)

    kv = pl.BlockSpec((1, n, ATT_HEAD_DIM), lambda h, i: (h, 0, 0))
    return pl.pallas_call(
        body, name=name, grid=(ATT_KV_HEADS, n // tq),
        in_specs=[pl.BlockSpec((tq, gw), lambda h, i: (i, h)), kv, kv],
        out_specs=pl.BlockSpec((tq, gw), lambda h, i: (i, h)), out_shape=SDS((n, ATT_Q_DIM), MXU_DTYPE),
        compiler_params=_cp("parallel", "parallel"))(q, k, v)


def _attn_bwd(q, k, v, o, do, name):
    n = q.shape[0]
    tq = min(ATT_TQ, n)
    scale = ATT_HEAD_DIM ** -0.5
    gw = ATT_GROUP * ATT_HEAD_DIM

    def body(q_ref, k_ref, v_ref, o_ref, do_ref, dq_ref, dk_ref, dv_ref):
        @pl.when(pl.program_id(1) == 0)
        def _():
            dk_ref[...] = jnp.zeros_like(dk_ref)
            dv_ref[...] = jnp.zeros_like(dv_ref)

        kk, vv = k_ref[0], v_ref[0]
        dqs = []
        dk_acc = jnp.zeros((ATT_HEAD_DIM, n), F32)
        dv_acc = jnp.zeros((ATT_HEAD_DIM, n), F32)
        for g in range(ATT_GROUP):
            sl = slice(g * ATT_HEAD_DIM, (g + 1) * ATT_HEAD_DIM)
            qg, dog = q_ref[:, sl] * scale, do_ref[:, sl].astype(F32)
            s = _dot(qg, kk, "nt")
            e = jnp.exp(s - jnp.max(s, axis=-1, keepdims=True))
            inv = 1.0 / jnp.sum(e, axis=-1, keepdims=True)
            delta = jnp.sum(dog * o_ref[:, sl].astype(F32), axis=-1, keepdims=True)
            dse = e * (_dot(dog, vv, "nt") - delta)
            dqs.append(_dot(dse, kk) * (inv * scale))
            dk_acc += _dot(qg.astype(F32) * inv, dse, "tn")
            dv_acc += _dot(dog * inv, e, "tn")
        dq_ref[...] = jnp.concatenate(dqs, axis=-1)
        dk_ref[0] += dk_acc
        dv_ref[0] += dv_acc

    kv = pl.BlockSpec((1, n, ATT_HEAD_DIM), lambda h, i: (h, 0, 0))
    kvt = pl.BlockSpec((1, ATT_HEAD_DIM, n), lambda h, i: (h, 0, 0))
    qb = pl.BlockSpec((tq, gw), lambda h, i: (i, h))
    return pl.pallas_call(
        body, name=name, grid=(ATT_KV_HEADS, n // tq), in_specs=[qb, kv, kv, qb, qb], out_specs=[qb, kvt, kvt],
        out_shape=[SDS((n, ATT_Q_DIM), F32), SDS((ATT_KV_HEADS, ATT_HEAD_DIM, n), F32), SDS((ATT_KV_HEADS, ATT_HEAD_DIM, n), F32)],
        compiler_params=_cp("parallel", "arbitrary"))(q, k, v, o, do)


def _both_directions(mats, axis):
    fwd = np.concatenate(mats, axis=axis).astype(np.float32)
    bwd = np.concatenate([m[::-1, ::-1] for m in mats], axis=axis).astype(np.float32)
    return jnp.asarray(np.stack([fwd, bwd]), MXU_DTYPE)


def _hg_segments():
    c = HG_CHUNK
    t = np.arange(c)[:, None]
    r = np.arange(c)[None, :]
    mats = [(r <= t)]
    for lev in range(HG_LEVELS):
        h = c >> (lev + 1)
        mid = (t // (2 * h)) * (2 * h) + h - 1
        hi = (t // h) % 2 == 1
        mats.append(np.where(hi, (r > mid) & (r <= t), (r > t) & (r <= mid)))
    mats.append(r > t)
    return _both_directions(mats, 0)


def _hg_pair_sums():
    c = HG_CHUNK
    r = np.arange(c)[:, None]
    t = np.arange(c)[None, :]
    gp, gn = [t >= r], [t < r]
    for lev in range(HG_LEVELS):
        sh = HG_LEVELS - 1 - lev
        same = (r >> sh) == (t >> sh)
        gp.append(same & (t >= r))
        gn.append(same & (t < r))
    return _both_directions(gp, 1), _both_directions(gn, 1)


def _split_dot(mat, x):
    hi = x.astype(MXU_DTYPE)
    lo = (x - hi.astype(F32)).astype(MXU_DTYPE)
    return _dot(mat, hi) + _dot(mat, lo)


def _hg_gates(hq, z, a0, a1):
    q = hq * _sigmoid(hq)
    sg = _sigmoid(z)
    lb = _sigmoid(a0 - a1)
    f = lb + (1.0 - lb) * sg
    k = (1.0 - lb) * (1.0 - sg)
    return q, f, k, sg, lb


def _hg_level_masks():
    c = HG_CHUNK
    t = np.arange(c)
    later, same = [], []
    for lev in range(HG_LEVELS):
        sh = HG_LEVELS - 1 - lev
        later.append(np.broadcast_to((((t >> sh) & 1) == 1)[:, None], (c, HG_HEAD_DIM)))
        same.append((t[:, None] >> (sh + 1)) == (t[None, :] >> (sh + 1)))
    same.append(t[:, None] == t[None, :])
    later = np.stack(later).astype(np.float32)
    return jnp.asarray(np.stack([later, 1.0 - later]), F32), jnp.asarray(np.stack(same).astype(np.float32), F32)


def _hg_level(q, k, ex, later_ref, lev):
    e = ex[lev + 1]
    e_q = e * later_ref[0, lev]
    e_k = e - e_q
    return q * e_q, k * e_k, e_q, e_k


def _hg_intra(q, k, ex, later_ref, same_ref):
    a = same_ref[HG_LEVELS] * jnp.sum(q * k, axis=-1, keepdims=True)
    for lev in range(HG_LEVELS):
        qs, ks, _, _ = _hg_level(q, k, ex, later_ref, lev)
        a = a + same_ref[lev] * _dot(qs, ks, "nt")
    return a


def _hg_specs(n, with_time):
    c = HG_CHUNK
    nc = n // c

    def chunk(d, i):
        first = d if with_time else 1 - d
        return i + first * (nc - 1 - 2 * i)

    def pcols(off, dir_stride=0):
        return [pl.BlockSpec((c, HG_PAIR), lambda d, i, j=j: (chunk(d, i), off // HG_PAIR + dir_stride // HG_PAIR * d + j)) for j in range(2)]

    specs = dict(
        hq=pcols(OFF_HQ), v=pcols(OFF_HI), z=pcols(OFF_ZF, OFF_ZB - OFF_ZF),
        shared=pl.BlockSpec((c, HG_DIM), lambda d, i: (chunk(d, i), 0)),
        per_dir=pl.BlockSpec((1, c, HG_DIM), lambda d, i: (d, chunk(d, i), 0)),
        vec=pl.BlockSpec((1, 1, HG_DIM), lambda d, i: (d, 0, 0)),
        seg=pl.BlockSpec((1, (HG_LEVELS + 2) * c, c), lambda d, i: (d, 0, 0)),
        sums=pl.BlockSpec((1, c, (HG_LEVELS + 1) * c), lambda d, i: (d, 0, 0)),
        later=pl.BlockSpec((1, HG_LEVELS, c, HG_HEAD_DIM), lambda d, i: (d, 0, 0, 0)),
        same=pl.BlockSpec((HG_LEVELS + 1, c, c), lambda d, i: (0, 0, 0)),
        state=pl.BlockSpec((1, HG_HEADS, 1, HG_HEAD_DIM, HG_HEAD_DIM), lambda d, i: (d, 0, chunk(d, i), 0, 0)),
        weights=pl.BlockSpec((1, HG_HEADS, 1, c, c), lambda d, i: (d, 0, chunk(d, i), 0, 0)),
        levels=pl.BlockSpec((1, HG_HEADS, 1, HG_LEVELS, c, HG_HEAD_DIM), lambda d, i: (d, 0, chunk(d, i), 0, 0, 0)))
    return nc, specs


def _hg_head(refs, hh):
    off = (hh % 2) * HG_HEAD_DIM
    return refs[hh // 2][:, off:off + HG_HEAD_DIM]


def _hg_lanes(hh):
    return slice(hh * HG_HEAD_DIM, (hh + 1) * HG_HEAD_DIM)


def _hg_exps(seg_ref, f, kept=None):
    c = HG_CHUNK
    lf = jnp.log(f)
    if kept is None:
        args = _split_dot(seg_ref[0], lf)
        return [jnp.exp(args[j * c:(j + 1) * c]) for j in range(HG_LEVELS + 2)]
    chunk_wide = jnp.concatenate([seg_ref[0, 0:c], seg_ref[0, (HG_LEVELS + 1) * c:(HG_LEVELS + 2) * c]], axis=0)
    args = _split_dot(chunk_wide, lf)
    return [jnp.exp(args[:c])] + [e.astype(F32) for e in kept] + [jnp.exp(args[c:])]


def _hg_last_row(a, mirrored):
    return jnp.where(mirrored, a[0:1, :], a[HG_CHUNK - 1:HG_CHUNK, :])


def _hgrn_fwd(p, a0, a1, seg, masks, name):
    n = p.shape[0]
    nc, sp = _hg_specs(n, True)

    def body(hq0, hq1, z0, z1, v0, v1, a0_ref, a1_ref, seg_ref, later_ref, same_ref, o_ref, s0_ref, a_ref, e_ref, st):
        @pl.when(pl.program_id(1) == 0)
        def _():
            st[...] = jnp.zeros_like(st)

        mirrored = pl.program_id(0) == 1
        for hh in range(HG_HEADS):
            ln = _hg_lanes(hh)
            q, f, k, _, _ = _hg_gates(_hg_head((hq0, hq1), hh), _hg_head((z0, z1), hh), a0_ref[0, :, ln], a1_ref[0, :, ln])
            vv = _hg_head((v0, v1), hh)
            ex = _hg_exps(seg_ref, f)
            for lev in range(HG_LEVELS):
                e_ref[0, hh, 0, lev] = ex[lev + 1].astype(e_ref.dtype)
            a = _hg_intra(q, k, ex, later_ref, same_ref).astype(MXU_DTYPE)
            a_ref[0, hh, 0] = a
            s_t = st[hh]
            s0_ref[0, hh, 0] = s_t
            o_ref[0, :, ln] = _dot(a, vv) + _dot(q * ex[0], s_t, "nt")
            st[hh] = s_t * _hg_last_row(ex[0], mirrored) + _dot(vv, k * ex[HG_LEVELS + 1], "tn")

    return pl.pallas_call(
        body, name=name, grid=(2, nc), in_specs=sp["hq"] + sp["z"] + sp["v"] + [sp["vec"], sp["vec"], sp["seg"], sp["later"], sp["same"]],
        out_specs=[sp["per_dir"], sp["state"], sp["weights"], sp["levels"]],
        out_shape=[SDS((2, n, HG_DIM), F32), SDS((2, HG_HEADS, nc, HG_HEAD_DIM, HG_HEAD_DIM), F32),
                   SDS((2, HG_HEADS, nc, HG_CHUNK, HG_CHUNK), MXU_DTYPE),
                   SDS((2, HG_HEADS, nc, HG_LEVELS, HG_CHUNK, HG_HEAD_DIM), MXU_DTYPE)],
        scratch_shapes=[pltpu.VMEM((HG_HEADS, HG_HEAD_DIM, HG_HEAD_DIM), F32)],
        compiler_params=_cp("parallel", "arbitrary"))(p, p, p, p, p, p, a0, a1, seg, *masks)


def _hgrn_bwd(p, a0, a1, seg, masks, gp, gn, do, s0, a, e, name):
    n = p.shape[0]
    nc, sp = _hg_specs(n, False)


    def body(hq0, hq1, z0, z1, v0, v1, a0_ref, a1_ref, seg_ref, later_ref, same_ref, gp_ref, gn_ref, do_ref, s0_ref, a_ref, e_ref,
             dhq_ref, dz_ref, dv_ref, dlb_ref, rt):
        @pl.when(pl.program_id(1) == 0)
        def _():
            rt[...] = jnp.zeros_like(rt)
            dlb_ref[...] = jnp.zeros_like(dlb_ref)

        mirrored = pl.program_id(0) == 1
        for hh in range(HG_HEADS):
            ln = _hg_lanes(hh)
            hqv = _hg_head((hq0, hq1), hh)
            q, f, k, sg, lb = _hg_gates(hqv, _hg_head((z0, z1), hh), a0_ref[0, :, ln], a1_ref[0, :, ln])
            vv, dov = _hg_head((v0, v1), hh), do_ref[:, ln]
            ex = _hg_exps(seg_ref, f, kept=[e_ref[0, hh, 0, lev] for lev in range(HG_LEVELS)])
            a = a_ref[0, hh, 0]
            da = _dot(dov, vv, "nt")
            diag = jnp.sum(dov * vv, axis=-1, keepdims=True)
            s_t = s0_ref[0, hh, 0]
            r_t = rt[hh]
            k_end = k * ex[HG_LEVELS + 1]
            dv_ref[0, :, ln] = _dot(a, dov, "tn") + _dot(k_end, r_t, "nt")
            dq_inter = ex[0] * _dot(dov, s_t)
            dk_inter = ex[HG_LEVELS + 1] * _dot(vv, r_t)
            dq = diag * k + dq_inter
            dk = diag * q + dk_inter
            q_terms, k_terms = [q * dq_inter], [k * dk_inter]
            for lev in range(HG_LEVELS):
                qs, ks, e_q, e_k = _hg_level(q, k, ex, later_ref, lev)
                pairs = da * same_ref[lev]
                q_part = e_q * _dot(pairs, ks)
                k_part = e_k * _dot(pairs, qs, "tn")
                dq, dk = dq + q_part, dk + k_part
                q_terms.append(q * q_part)
                k_terms.append(k * k_part)
            decay = _hg_last_row(ex[0], mirrored)
            rt[hh] = r_t * decay + _dot(dov, q * ex[0], "tn")
            later = decay * jnp.sum(s_t * r_t, axis=0, keepdims=True)
            dlf = _dot(gp_ref[0], jnp.concatenate(q_terms, axis=0)) + _dot(gn_ref[0], jnp.concatenate(k_terms, axis=0)) + later
            df = dlf / f - dk
            dz_ref[0, :, ln] = df * (1.0 - lb) * sg * (1.0 - sg)
            dlb_ref[0, :, ln] += jnp.sum(df * (1.0 - sg), axis=0, keepdims=True)
            sq = _sigmoid(hqv)
            dhq_ref[0, :, ln] = dq * sq * (1.0 + hqv * (1.0 - sq))

    out = SDS((2, n, HG_DIM), F32)
    return pl.pallas_call(
        body, name=name, grid=(2, nc),
        in_specs=sp["hq"] + sp["z"] + sp["v"] + [sp["vec"], sp["vec"], sp["seg"], sp["later"], sp["same"], sp["sums"], sp["sums"],
                                                 sp["shared"], sp["state"], sp["weights"], sp["levels"]],
        out_specs=[sp["per_dir"], sp["per_dir"], sp["per_dir"], sp["vec"]], out_shape=[out, out, out, SDS((2, 1, HG_DIM), F32)],
        scratch_shapes=[pltpu.VMEM((HG_HEADS, HG_HEAD_DIM, HG_HEAD_DIM), F32)],
        compiler_params=_cp("parallel", "arbitrary"))(p, p, p, p, p, p, a0, a1, seg, *masks, gp, gn, do, s0, a, e)


def _hg_post(o2, p, g, name):
    n = p.shape[0]
    tr = min(ROW_TILE, n)
    w = 2 * HG_HEAD_DIM

    def body(of_ref, ob_ref, hg_ref, g_ref, o_ref):
        for j in range(2):
            sl = slice(j * HG_HEAD_DIM, (j + 1) * HG_HEAD_DIM)
            o = of_ref[0, :, sl] + ob_ref[0, :, sl]
            hg = hg_ref[:, sl]
            o_ref[:, sl] = (o * _rstd(o) * g_ref[...] * (hg * _sigmoid(hg))).astype(o_ref.dtype)

    blk = pl.BlockSpec((tr, w), lambda i, j: (i, j))
    dirs = [pl.BlockSpec((1, tr, w), lambda i, j, d=d: (d, i, j)) for d in range(2)]
    return pl.pallas_call(
        body, name=name, grid=(n // tr, HG_DIM // w),
        in_specs=dirs + [pl.BlockSpec((tr, w), lambda i, j: (i, OFF_HG // w + j)), pl.BlockSpec((1, HG_HEAD_DIM), lambda i, j: (0, 0))],
        out_specs=blk, out_shape=SDS((n, HG_DIM), MXU_DTYPE), compiler_params=_cp("parallel", "parallel"))(o2, o2, p, g)


def _hg_post_bwd(o2, p, g, dcat, name, after=()):
    n = p.shape[0]
    tr = min(ROW_TILE, n)
    w = 2 * HG_HEAD_DIM
    after, after_specs = _unread(after)

    def body(of_ref, ob_ref, hg_ref, g_ref, d_ref, *rest):
        do_ref, dhg_ref, dg_ref = rest[len(after):]

        @pl.when(pl.program_id(1) == 0)
        def _():
            dg_ref[...] = jnp.zeros_like(dg_ref)

        for j in range(2):
            sl = slice(j * HG_HEAD_DIM, (j + 1) * HG_HEAD_DIM)
            o = of_ref[0, :, sl] + ob_ref[0, :, sl]
            hg = hg_ref[:, sl]
            d = d_ref[:, sl].astype(F32)
            sg = _sigmoid(hg)
            on = o * _rstd(o) * g_ref[...]
            dhg_ref[:, sl] = (d * on * sg * (1.0 + hg * (1.0 - sg))).astype(dhg_ref.dtype)
            dx, dg = _rms_bwd(o, g_ref[...], d * hg * sg)
            do_ref[:, sl] = dx
            dg_ref[0, :, sl] += dg

    blk = pl.BlockSpec((tr, w), lambda j, i: (i, j))
    dirs = [pl.BlockSpec((1, tr, w), lambda j, i, d=d: (d, i, j)) for d in range(2)]
    return pl.pallas_call(
        body, name=name, grid=(HG_DIM // w, n // tr),
        in_specs=dirs + [pl.BlockSpec((tr, w), lambda j, i: (i, OFF_HG // w + j)), pl.BlockSpec((1, HG_HEAD_DIM), lambda j, i: (0, 0)),
                         pl.BlockSpec((tr, w), lambda j, i: (i, ATT_Q_DIM // w + j))] + after_specs,
        out_specs=[blk, blk, pl.BlockSpec((1, 1, w), lambda j, i: (j, 0, 0))],
        out_shape=[SDS((n, HG_DIM), F32), SDS((n, HG_DIM), MXU_DTYPE), SDS((HG_DIM // w, 1, w), F32)],
        compiler_params=_cp("parallel", "arbitrary"))(o2, o2, p, g, dcat, *after)


XATT_TQ = 512


def _xattn_fwd(q, kv, name):
    n, nm = q.shape[0], kv.shape[0]
    tq = min(XATT_TQ, n)
    scale = X_HEAD_DIM ** -0.5

    def body(q_ref, k_ref, v_ref, o_ref):
        s = _dot(q_ref[...], k_ref[...], "nt") * scale
        e = jnp.exp(s - jnp.max(s, axis=-1, keepdims=True))
        o_ref[...] = _dot(e / jnp.sum(e, axis=-1, keepdims=True), v_ref[...]).astype(o_ref.dtype)

    qb = pl.BlockSpec((tq, X_HEAD_DIM), lambda h, i: (i, h))
    return pl.pallas_call(
        body, name=name, grid=(X_HEADS, n // tq),
        in_specs=[qb, pl.BlockSpec((nm, X_HEAD_DIM), lambda h, i: (0, h)), pl.BlockSpec((nm, X_HEAD_DIM), lambda h, i: (0, X_HEADS + h))],
        out_specs=qb, out_shape=SDS(q.shape, MXU_DTYPE), compiler_params=_cp("parallel", "parallel"))(q, kv, kv)


def _xattn_bwd(q, kv, do, name, after=()):
    n, nm = q.shape[0], kv.shape[0]
    tq = min(XATT_TQ, n)
    scale = X_HEAD_DIM ** -0.5
    after, after_specs = _unread(after)

    def body(q_ref, k_ref, v_ref, do_ref, *rest):
        dq_ref, dk_ref, dv_ref = rest[len(after):]

        @pl.when(pl.program_id(1) == 0)
        def _():
            dk_ref[...] = jnp.zeros_like(dk_ref)
            dv_ref[...] = jnp.zeros_like(dv_ref)

        qv, dov = q_ref[...], do_ref[...]
        s = _dot(qv, k_ref[...], "nt") * scale
        e = jnp.exp(s - jnp.max(s, axis=-1, keepdims=True))
        p = e / jnp.sum(e, axis=-1, keepdims=True)
        dp = _dot(dov, v_ref[...], "nt")
        ds = p * (dp - jnp.sum(p * dp, axis=-1, keepdims=True)) * scale
        dq_ref[...] = _dot(ds, k_ref[...]).astype(dq_ref.dtype)
        dk_ref[...] += _dot(ds, qv, "tn")
        dv_ref[...] += _dot(p, dov, "tn")

    qb = pl.BlockSpec((tq, X_HEAD_DIM), lambda h, i: (i, h))
    kb = pl.BlockSpec((nm, X_HEAD_DIM), lambda h, i: (0, h))
    return pl.pallas_call(
        body, name=name, grid=(X_HEADS, n // tq),
        in_specs=[qb, kb, pl.BlockSpec((nm, X_HEAD_DIM), lambda h, i: (0, X_HEADS + h)), qb] + after_specs, out_specs=[qb, kb, kb],
        out_shape=[SDS(q.shape, MXU_DTYPE), SDS((nm, X_HEADS * X_HEAD_DIM), F32), SDS((nm, X_HEADS * X_HEAD_DIM), F32)],
        compiler_params=_cp("parallel", "arbitrary"))(q, kv, kv, do, *after)


def _edge_rows(shape):
    row = lax.broadcasted_iota(jnp.int32, shape, 0)
    return row == 0, row == shape[0] - 1


def _shift_rows(u, down, edges):
    if down:
        return jnp.where(edges[0], 0.0, pltpu.roll(u, 1, axis=0))
    return jnp.where(edges[1], 0.0, pltpu.roll(u, u.shape[0] - 1, axis=0))


def _conv(u, w, b, edges):
    return b + _shift_rows(u, True, edges) * w[0:1, :] + u * w[1:2, :] + _shift_rows(u, False, edges) * w[2:3, :]


def _ff_specs(n):
    gate = lambda rows: pl.BlockSpec((rows, FF_COLS), lambda j: (0, j))
    val = lambda rows: pl.BlockSpec((rows, FF_COLS), lambda j: (0, FF_BLOCKS + j))
    return [gate(n), val(n), gate(3), val(3), gate(1), val(1)], gate


def _conv_gate(u, cw, cb, name):
    n = u.shape[0]
    ins, gate_blk = _ff_specs(n)

    def body(ug_ref, uv_ref, wg_ref, wv_ref, bg_ref, bv_ref, o_ref, gate_ref, val_ref):
        edges = _edge_rows(ug_ref.shape)
        gate = _conv(ug_ref[...], wg_ref[...], bg_ref[...], edges)
        val = _conv(uv_ref[...], wv_ref[...], bv_ref[...], edges)
        gate_ref[...] = gate
        val_ref[...] = val
        o_ref[...] = (gate * _sigmoid(gate) * val).astype(o_ref.dtype)

    return pl.pallas_call(
        body, name=name, grid=(FF_BLOCKS,), in_specs=ins, out_specs=[gate_blk(n)] * 3,
        out_shape=[SDS((n, D_FF), MXU_DTYPE), SDS((n, D_FF), F32), SDS((n, D_FF), F32)],
        compiler_params=_cp("parallel"))(u, u, cw, cw, cb, cb)


def _conv_gate_bwd(u, cw, gate, val, da, name, after=()):
    n = u.shape[0]
    ins, gate_blk = _ff_specs(n)
    after, after_specs = _unread(after)

    def side(dacc, u, w, edges, du_ref, dw_ref, db_ref):
        nxt, prv = _shift_rows(dacc, False, edges), _shift_rows(dacc, True, edges)
        du_ref[...] = (nxt * w[0:1, :] + dacc * w[1:2, :] + prv * w[2:3, :]).astype(du_ref.dtype)
        db_ref[...] = jnp.sum(dacc, axis=0, keepdims=True)
        dw_ref[0:1, :] = jnp.sum(nxt * u, axis=0, keepdims=True)
        dw_ref[1:2, :] = jnp.sum(dacc * u, axis=0, keepdims=True)
        dw_ref[2:3, :] = jnp.sum(prv * u, axis=0, keepdims=True)

    def body(ug_ref, uv_ref, wg_ref, wv_ref, gate_ref, val_ref, da_ref, *rest):
        dug_ref, duv_ref, dwg_ref, dwv_ref, dbg_ref, dbv_ref = rest[len(after):]
        edges = _edge_rows(ug_ref.shape)
        gate, val = gate_ref[...], val_ref[...]
        sg = _sigmoid(gate)
        dav = da_ref[...].astype(F32)
        side(dav * val * sg * (1.0 + gate * (1.0 - sg)), ug_ref[...], wg_ref[...], edges, dug_ref, dwg_ref, dbg_ref)
        side(dav * gate * sg, uv_ref[...], wv_ref[...], edges, duv_ref, dwv_ref, dbv_ref)

    return pl.pallas_call(
        body, name=name, grid=(FF_BLOCKS,), in_specs=ins[:4] + [gate_blk(n)] * 3 + after_specs,
        out_specs=[gate_blk(n), gate_blk(n), gate_blk(3), gate_blk(3), gate_blk(1), gate_blk(1)],
        out_shape=[SDS((n, D_FF), MXU_DTYPE)] * 2 + [SDS((3, D_FF), F32)] * 2 + [SDS((1, D_FF), F32)] * 2,
        compiler_params=_cp("parallel"))(u, u, cw, cw, gate, val, da, *after)


def _adamw(w, g, m, v, name):
    r, c = w.shape[-2:]
    tr = r if r <= 512 else 256 if r % 256 == 0 else 88
    assert r % tr == 0 and (w.ndim == 2 or w.shape[:-2] == (1,)), (name, w.shape, tr)

    def body(w_ref, g_ref, m_ref, v_ref, d_ref, mo_ref, vo_ref, go_ref):
        gv = g_ref[...]
        go_ref[...] = gv
        mn = ADAM_B1 * m_ref[...] + (1.0 - ADAM_B1) * gv
        vn = ADAM_B2 * v_ref[...] + (1.0 - ADAM_B2) * gv * gv
        m_hat = mn / (1.0 - ADAM_B1 ** ADAM_STEP)
        v_hat = vn / (1.0 - ADAM_B2 ** ADAM_STEP)
        d_ref[...] = -ADAM_LR * (m_hat / (jnp.sqrt(v_hat) + ADAM_EPS) + ADAM_WD * w_ref[...])
        mo_ref[...] = mn
        vo_ref[...] = vn

    blk = pl.BlockSpec((tr, c), lambda i: (i, 0)) if w.ndim == 2 else pl.BlockSpec((1, tr, c), lambda i: (0, i, 0))
    out = SDS(w.shape, F32)
    return pl.pallas_call(body, name=name, grid=(r // tr,), in_specs=[blk] * 4, out_specs=[blk] * 4, out_shape=[out] * 4,
                          compiler_params=_cp("parallel"))(w, g, m, v)


def _half_tile(h):
    tr = h if h <= 512 else 256 if h % 256 == 0 else 176
    assert h % tr == 0, (h, tr)
    return tr


ANY = pl.BlockSpec(memory_space=pl.ANY)


def _place():
    x, y, c = lax.axis_index("x"), lax.axis_index("y"), lax.axis_index("c")
    return x, y, c, [(1 - x, y), (x, 1 - y), (1 - x, 1 - y)]


def _gather_shards(shards, name):
    nt = len(shards)

    def body(*refs):
        ins, outs = refs[:nt], refs[nt:2 * nt]
        send, recv, fsend, frecv, osend, orecv = refs[2 * nt:]
        x, y, c, chips = _place()
        me = 2 * x + y

        def half(t, chip, cc):
            h = ins[t].shape[0] // 2
            return outs[t].at[chip, pl.ds(cc * h, h)]

        def ici(t, j):
            cx, cy = chips[j]
            h = ins[t].shape[0] // 2
            return pltpu.make_async_remote_copy(src_ref=ins[t].at[pl.ds(c * h, h)], dst_ref=half(t, me, c),
                                                send_sem=send.at[t, j], recv_sem=recv.at[t, j], device_id=(cx, cy, c), device_id_type=MESH)

        def landed(t, j):
            cx, cy = chips[j]
            blk = half(t, 2 * cx + cy, c)
            return pltpu.make_async_remote_copy(src_ref=blk, dst_ref=blk, send_sem=send.at[t, j], recv_sem=recv.at[t, j],
                                                device_id=(cx, cy, c), device_id_type=MESH)

        def d2d(t, j, cc):
            cx, cy = chips[j]
            blk = half(t, 2 * cx + cy, cc)
            return pltpu.make_async_remote_copy(src_ref=blk, dst_ref=blk, send_sem=fsend.at[t, j], recv_sem=frecv.at[t, j],
                                                device_id=(x, y, 1 - c), device_id_type=MESH)

        own = [pltpu.make_async_remote_copy(src_ref=ins[t], dst_ref=outs[t].at[me], send_sem=osend.at[t], recv_sem=orecv.at[t],
                                            device_id=(x, y, 1 - c), device_id_type=MESH) for t in range(nt)]
        for t in range(nt):
            for j in range(3):
                ici(t, j).start()
        for cp in own:
            cp.start()
        for t in range(nt):
            for j in range(3):
                landed(t, j).wait_recv()
                d2d(t, j, c).start()
        for t in range(nt):
            for j in range(3):
                d2d(t, j, 1 - c).wait_recv()
        for t in range(nt):
            for j in range(3):
                ici(t, j).wait_send()
                d2d(t, j, c).wait_send()
        for cp in own:
            cp.wait()

    return pl.pallas_call(
        body, name=name, in_specs=[ANY] * nt, out_specs=[ANY] * nt,
        out_shape=[SDS((4,) + s.shape, s.dtype) for s in shards],
        scratch_shapes=[pltpu.SemaphoreType.DMA((nt, 3))] * 4 + [pltpu.SemaphoreType.DMA((nt,))] * 2,
        compiler_params=pltpu.CompilerParams(has_side_effects=True))(*shards)


def _join_halves(bufs, name):
    nt = len(bufs)

    def body(*refs):
        outs = refs[nt:2 * nt]
        send, recv = refs[2 * nt:]
        x, y, c, _ = _place()
        cps = [pltpu.make_async_remote_copy(src_ref=outs[t].at[c], dst_ref=outs[t].at[c], send_sem=send.at[t], recv_sem=recv.at[t],
                                            device_id=(x, y, 1 - c), device_id_type=MESH) for t in range(nt)]
        for cp in cps:
            cp.start()
        for t in range(nt):
            theirs = outs[t].at[1 - c]
            pltpu.make_async_remote_copy(src_ref=theirs, dst_ref=theirs, send_sem=send.at[t], recv_sem=recv.at[t],
                                         device_id=(x, y, 1 - c), device_id_type=MESH).wait_recv()
        for cp in cps:
            cp.wait_send()

    return pl.pallas_call(
        body, name=name, in_specs=[ANY] * nt, out_specs=[ANY] * nt, out_shape=[SDS(b.shape, b.dtype) for b in bufs],
        input_output_aliases={t: t for t in range(nt)},
        scratch_shapes=[pltpu.SemaphoreType.DMA((nt,))] * 2,
        compiler_params=pltpu.CompilerParams(has_side_effects=True))(*bufs)


def _exchange_small(v, reduce, name, after=()):
    rows = v.shape[0]
    after, after_specs = _unread(after)

    def body(v_ref, *rest):
        o_ref, buf, send, recv = rest[-4:]
        x, y, c, _ = _place()
        me = 4 * x + 2 * y + c
        buf[me] = v_ref[...]

        def peer(dx, dy, dc):
            return (1 - x if dx else x, 1 - y if dy else y, 1 - c if dc else c)

        peers = [(dx, dy, dc) for dx in range(2) for dy in range(2) for dc in range(2) if (dx, dy, dc) != (0, 0, 0)]
        cps = []
        for j, (dx, dy, dc) in enumerate(peers):
            cps.append(pltpu.make_async_remote_copy(src_ref=v_ref, dst_ref=buf.at[me], send_sem=send.at[j], recv_sem=recv.at[j],
                                                    device_id=peer(dx, dy, dc), device_id_type=MESH))
        for cp in cps:
            cp.start()
        for j, (dx, dy, dc) in enumerate(peers):
            px, py, pc = peer(dx, dy, dc)
            blk = buf.at[4 * px + 2 * py + pc]
            pltpu.make_async_remote_copy(src_ref=blk, dst_ref=blk, send_sem=send.at[j], recv_sem=recv.at[j],
                                         device_id=(px, py, pc), device_id_type=MESH).wait_recv()
        for cp in cps:
            cp.wait_send()
        if reduce:
            acc = buf[0]
            for j in range(1, 8):
                acc = acc + buf[j]
            o_ref[...] = acc
        else:
            o_ref[...] = buf[...]

    vm = pl.BlockSpec(memory_space=pltpu.VMEM)
    return pl.pallas_call(
        body, name=name, in_specs=[vm] + after_specs, out_specs=vm, out_shape=SDS((rows, 128) if reduce else (8, rows, 128), F32),
        scratch_shapes=[pltpu.VMEM((8, rows, 128), F32), pltpu.SemaphoreType.DMA((7,)), pltpu.SemaphoreType.DMA((7,))],
        compiler_params=pltpu.CompilerParams(has_side_effects=True))(v, *after)


HBM = pl.BlockSpec(memory_space=pltpu.HBM)
SEM = pl.BlockSpec(memory_space=pltpu.SEMAPHORE)
TOKEN = pl.BlockSpec(memory_space=pltpu.VMEM)
TOKEN_SHAPE = SDS((8, 128), F32)
PEERS = 7


def _in_hbm(a):
    return pltpu.with_memory_space_constraint(a, pltpu.HBM)


def _split_params():
    return pltpu.CompilerParams(has_side_effects=pltpu.SideEffectType.DATAFLOW_SIDE_EFFECTING)


def _gather_start(shards, name, after=()):
    nt = len(shards)
    after, after_specs = _unread(after)

    def body(*refs):
        ins, lands = refs[:nt], refs[nt:2 * nt]
        outs = refs[2 * nt + len(after):]
        sends, recvs = outs[:nt], outs[nt:2 * nt]
        x, y, c, chips = _place()
        me = 2 * x + y
        for t in range(nt):
            h = ins[t].shape[0] // 2
            mine = pl.ds(c * h, h)
            for j, (cx, cy) in enumerate(chips):
                for dc in range(2):
                    pltpu.make_async_remote_copy(src_ref=ins[t].at[mine], dst_ref=lands[t].at[me, mine], send_sem=sends[t].at[2 * j + dc],
                                                 recv_sem=recvs[t].at[2 * j + c], device_id=(cx, cy, dc), device_id_type=MESH).start()
            pltpu.make_async_remote_copy(src_ref=ins[t], dst_ref=lands[t].at[me], send_sem=sends[t].at[PEERS - 1], recv_sem=recvs[t].at[PEERS - 1],
                                         device_id=(x, y, 1 - c), device_id_type=MESH).start()
        outs[-1][...] = jnp.zeros(TOKEN_SHAPE.shape, F32)

    lands = [lax.empty((4,) + s.shape, s.dtype) for s in shards]
    out = pl.pallas_call(
        body, name=name, in_specs=[HBM] * (2 * nt) + after_specs, out_specs=[SEM] * (2 * nt) + [HBM] * (2 * nt) + [TOKEN],
        out_shape=[pltpu.SemaphoreType.DMA((PEERS,))] * (2 * nt)
        + [pltpu.HBM(s.shape, s.dtype) for s in shards] + [pltpu.HBM(l.shape, l.dtype) for l in lands] + [TOKEN_SHAPE],
        input_output_aliases={t: 2 * nt + t for t in range(2 * nt)}, compiler_params=_split_params())(
            *[_in_hbm(s) for s in shards], *[_in_hbm(l) for l in lands], *after)
    return out[:nt], out[nt:2 * nt], out[2 * nt:3 * nt], out[3 * nt:4 * nt], out[-1]


def _gather_wait(sends, recvs, shards, lands, after, name):
    nt = len(shards)

    def body(*refs):
        ins, lands_ref = refs[:nt], refs[nt:2 * nt]
        send_refs, recv_refs = refs[2 * nt:3 * nt], refs[3 * nt:4 * nt]
        x, y, c, chips = _place()
        for t in range(nt):
            h = ins[t].shape[0] // 2
            for j, (cx, cy) in enumerate(chips):
                for cs in range(2):
                    blk = lands_ref[t].at[2 * cx + cy, pl.ds(cs * h, h)]
                    pltpu.make_async_remote_copy(src_ref=blk, dst_ref=blk, send_sem=send_refs[t].at[2 * j + cs], recv_sem=recv_refs[t].at[2 * j + cs],
                                                 device_id=(cx, cy, cs), device_id_type=MESH).wait()
            blk = lands_ref[t].at[2 * x + y]
            pltpu.make_async_remote_copy(src_ref=blk, dst_ref=blk, send_sem=send_refs[t].at[PEERS - 1], recv_sem=recv_refs[t].at[PEERS - 1],
                                         device_id=(x, y, 1 - c), device_id_type=MESH).wait()

    out = pl.pallas_call(
        body, name=name, in_specs=[HBM] * (2 * nt) + [SEM] * (2 * nt) + [ANY], out_specs=[HBM] * (2 * nt),
        out_shape=[pltpu.HBM(s.shape, s.dtype) for s in shards] + [pltpu.HBM(l.shape, l.dtype) for l in lands],
        input_output_aliases={t: t for t in range(2 * nt)}, compiler_params=_split_params())(*shards, *lands, *sends, *recvs, after)
    return out[nt:]


def _scatter_start(g, name):
    _, r, c_ = g.shape
    h = r // 2

    def body(g_ref, land, send, recv, g_thru, land_thru, token):
        x, y, c, chips = _place()
        for j, (cx, cy) in enumerate(chips):
            for dc in range(2):
                pltpu.make_async_remote_copy(src_ref=g_ref.at[2 * cx + cy, pl.ds(dc * h, h)], dst_ref=land.at[2 * j + c], send_sem=send.at[2 * j + dc],
                                             recv_sem=recv.at[2 * j + c], device_id=(cx, cy, dc), device_id_type=MESH).start()
        pltpu.make_async_remote_copy(src_ref=g_ref.at[2 * x + y, pl.ds((1 - c) * h, h)], dst_ref=land.at[PEERS - 1], send_sem=send.at[PEERS - 1],
                                     recv_sem=recv.at[PEERS - 1], device_id=(x, y, 1 - c), device_id_type=MESH).start()
        token[...] = jnp.zeros(TOKEN_SHAPE.shape, F32)

    land = lax.empty((PEERS, h, c_), g.dtype)
    return pl.pallas_call(
        body, name=name, in_specs=[HBM, HBM], out_specs=[SEM, SEM, HBM, HBM, TOKEN],
        out_shape=[pltpu.SemaphoreType.DMA((PEERS,)), pltpu.SemaphoreType.DMA((PEERS,)), pltpu.HBM(g.shape, g.dtype),
                   pltpu.HBM(land.shape, land.dtype), TOKEN_SHAPE],
        input_output_aliases={0: 2, 1: 3}, compiler_params=_split_params())(_in_hbm(g), _in_hbm(land))


def _scatter_wait(started, after, name):
    nt = len(started)

    def body(*refs):
        lands = refs[nt:2 * nt]
        sends, recvs = refs[2 * nt:3 * nt], refs[3 * nt:4 * nt]
        x, y, c, chips = _place()
        peers = [(cx, cy, dc) for cx, cy in chips for dc in range(2)] + [(x, y, 1 - c)]
        for t in range(nt):
            for k, peer in enumerate(peers):
                blk = lands[t].at[k]
                pltpu.make_async_remote_copy(src_ref=blk, dst_ref=blk, send_sem=sends[t].at[k], recv_sem=recvs[t].at[k],
                                             device_id=peer, device_id_type=MESH).wait()

    gs, lands = [s[2] for s in started], [s[3] for s in started]
    after, after_specs = _unread(after)
    out = pl.pallas_call(
        body, name=name, in_specs=[HBM] * (2 * nt) + [SEM] * (2 * nt) + after_specs, out_specs=[HBM] * (2 * nt),
        out_shape=[pltpu.HBM(a.shape, a.dtype) for a in gs + lands],
        input_output_aliases={t: t for t in range(2 * nt)}, compiler_params=_split_params())(
            *gs, *lands, *[s[0] for s in started], *[s[1] for s in started], *after)
    return out[:nt], out[nt:]


def _sum_devices(g, land, me, core, name):
    npeer, h, c = land.shape
    tr = _half_tile(h)
    steps = h // tr

    def body(ix_ref, own_ref, land_ref, o_ref):
        acc = own_ref[0].astype(F32)
        for j in range(npeer):
            acc = acc + land_ref[j].astype(F32)
        o_ref[0] = acc

    grid_spec = pltpu.PrefetchScalarGridSpec(
        num_scalar_prefetch=1, grid=(steps,),
        in_specs=[pl.BlockSpec((1, tr, c), lambda i, ix: (ix[0], ix[1] * steps + i, 0)), pl.BlockSpec((npeer, tr, c), lambda i, ix: (0, i, 0))],
        out_specs=pl.BlockSpec((1, tr, c), lambda i, ix: (ix[1], i, 0)))
    return pl.pallas_call(body, name=name, grid_spec=grid_spec, out_shape=SDS((2, h, c), F32),
                          compiler_params=_cp("parallel"))(jnp.stack([me, core]), g, land)


def _pack_small(parts):
    flat = jnp.concatenate([p.reshape(-1) for p in parts])
    total = flat.shape[0]
    rows = -(-total // 1024) * 8
    return jnp.pad(flat, (0, rows * 128 - total)).reshape(rows, 128)


def _unpack_small(packed, shapes):
    flat = packed.reshape(-1)
    out, off = [], 0
    for s in shapes:
        size = int(np.prod(s))
        out.append(flat[off:off + size].reshape(s))
        off += size
    return out


def _local_step(x, mem, target, w_in, first_after, mid_weights, ffn_weights, on_grad, gains, conv_w, conv_b, hg_lb):
    n = x.shape[0]
    cos, sin = _rope_tables(n)
    seg = _hg_segments()
    gp, gn = _hg_pair_sums()
    masks = _hg_level_masks()
    gq2 = jnp.tile(gains["q_norm_g"], (1, 2))
    gk2 = jnp.tile(gains["k_norm_g"], (1, 2))
    a0 = hg_lb[:, 0:1, :]
    a1 = hg_lb[:, 1:2, :]

    p, h1 = _norm_mm(x, gains["pre_mix_g"], w_in, F32, TOKEN_TILE, 1664, "in_proj", after=(first_after,))
    qr, kr = _qk_prep(p, gq2, gk2, cos, sin, "qk_prep")
    heads = lambda a: a.reshape(n, ATT_KV_HEADS, ATT_HEAD_DIM).transpose(1, 0, 2)
    kh = heads(kr)
    vh = heads(p[:, OFF_AV:OFF_AV + ATT_KV_DIM].astype(MXU_DTYPE))
    att = _attn_fwd(qr, kh, vh, "attn_fwd")
    o2, s0, hg_a, hg_e = _hgrn_fwd(p, a0, a1, seg, masks, "hgrn_fwd")
    rec = _hg_post(o2, p, gains["hg_out_norm_g"], "hg_post")
    cat = jnp.concatenate([att, rec], axis=1)
    w_out, w_xq, w_xkv, w_xo = mid_weights(cat)
    mixed, x1 = _mm_resid_norm(cat, w_out, x, gains["post_mix_g"], 512, "out_proj_resid")
    xq, h2 = _norm_mm(x1, gains["pre_x_g"], w_xq, MXU_DTYPE, TOKEN_TILE, 1024, "xq_proj")
    kv, mn = _norm_mm(mem, gains["mem_norm_g"], w_xkv, MXU_DTYPE, 256, 2048, "xkv_proj")
    ox = _xattn_fwd(xq, kv, "xattn_fwd")
    xo, x2 = _mm_resid_norm(ox, w_xo, x1, gains["post_x_g"], 512, "xo_proj_resid")
    w_up = ffn_weights("w_up", x2)
    u, h3 = _norm_mm(x2, gains["pre_ffn_g"], w_up, F32, TOKEN_TILE, 1408, "up_proj")
    act, conv_gate_out, conv_val_out = _conv_gate(u, conv_w, conv_b, "conv_gate")
    w_down = ffn_weights("w_down", act)
    dn, d3, loss = _mm_resid_norm(act, w_down, x2, gains["post_ffn_g"], 512, "down_proj_resid_loss", target=target)

    gs = {}
    d_act, d_dn, gs["post_ffn_g"] = _norm_bwd_mm(dn, gains["post_ffn_g"], d3, w_down, F32, 512, 1408, "ffn_post_bwd_down_dx")
    tok = on_grad("w_down", _mm(act, d_dn, "tn", WIRE_DTYPE, 1408, 1024, "down_dw"))
    du_g, du_v, dcw_g, dcw_v, dcb_g, dcb_v = _conv_gate_bwd(u, conv_w, conv_gate_out, conv_val_out, d_act, "conv_gate_bwd", after=(tok,))
    gs["conv_w"] = jnp.concatenate([dcw_g, dcw_v], axis=1)
    gs["conv_b"] = jnp.concatenate([dcb_g, dcb_v], axis=1)
    ff_shard = w_up.shape[2]
    g_up = _dw_by_owner(h3, du_g, ff_shard, 0, None, 512, "up_dw_gate")
    tok = on_grad("w_up", _dw_by_owner(h3, du_v, ff_shard, 2, g_up, 512, "up_dw_value"))
    d2, gs["pre_ffn_g"] = _dx_norm_bwd([(du_g, 0), (du_g, 1), (du_v, 0), (du_v, 1)], w_up, x2, gains["pre_ffn_g"], d3, 512,
                                       "up_dx_pre_bwd", after=(tok,))
    d_ox, d_xo, gs["post_x_g"] = _norm_bwd_mm(xo, gains["post_x_g"], d2, w_xo, MXU_DTYPE, 512, 1024, "x_post_bwd_xo_dx")
    tok = on_grad("w_xo", _mm(ox, d_xo, "tn", WIRE_DTYPE, 512, 1024, "xo_dw"))
    d_xq, d_k, d_v = _xattn_bwd(xq, kv, d_ox, "xattn_bwd", after=(tok,))
    d_kv = jnp.concatenate([d_k, d_v], axis=1).astype(MXU_DTYPE)
    tok = on_grad("w_xq", _mm(h2, d_xq, "tn", WIRE_DTYPE, 512, 1024, "xq_dw"))
    tok_kv = on_grad("w_xkv", _dw_by_owner(mn, d_kv, w_xkv.shape[2], 0, None, 512, "xkv_dw"))
    d1, gs["pre_x_g"] = _dx_norm_bwd([(d_xq, 0)], w_xq[None], x1, gains["pre_x_g"], d2, 512, "xq_dx_pre_bwd", after=(tok, tok_kv))
    d_mn = _mm_nt_parts([(d_kv, s) for s in range(4)], w_xkv, F32, 256, 1024, "xkv_dx")
    _, gs["mem_norm_g"] = _norm_bwd(mem, gains["mem_norm_g"], d_mn, None, MXU_DTYPE, "mem_norm_bwd")
    d_cat, d_mixed, gs["post_mix_g"] = _norm_bwd_mm(mixed, gains["post_mix_g"], d1, w_out, MXU_DTYPE, 512, 1024, "mix_post_bwd_out_dx")
    tok = on_grad("w_out", _mm(cat, d_mixed, "tn", WIRE_DTYPE, 512, 1024, "out_dw"))
    d_o, d_hg, dg_hg = _hg_post_bwd(o2, p, gains["hg_out_norm_g"], d_cat, "hg_post_bwd", after=(tok,))
    gs["hg_out_norm_g"] = dg_hg.reshape(HG_HEADS, HG_HEAD_DIM).sum(axis=0, keepdims=True)
    dhq2, dz2, dhv2, dlb = _hgrn_bwd(p, a0, a1, seg, masks, gp, gn, d_o, s0, hg_a, hg_e, "hgrn_bwd")
    lb = jax.nn.sigmoid(a0 - a1)
    da0 = dlb * lb * (1.0 - lb)
    gs["hg_lb"] = jnp.concatenate([da0, -da0], axis=1)
    d_qr, d_kh, d_vh = _attn_bwd(qr, kh, vh, cat, d_cat, "attn_bwd")
    unheads = lambda a: a.transpose(2, 0, 1).reshape(n, ATT_KV_DIM)
    d_aq, d_ak, dgq, dgk = _qk_prep_bwd(p, gq2, gk2, cos, sin, d_qr, unheads(d_kh), "qk_prep_bwd")
    gs["q_norm_g"] = dgq.reshape(ATT_HEADS, ATT_HEAD_DIM).sum(axis=0, keepdims=True)
    gs["k_norm_g"] = dgk.reshape(ATT_KV_HEADS, ATT_HEAD_DIM).sum(axis=0, keepdims=True)
    d_p = jnp.concatenate([d_aq, d_ak, unheads(d_vh).astype(MXU_DTYPE), (dhq2[0] + dhq2[1]).astype(MXU_DTYPE),
                           dz2[0].astype(MXU_DTYPE), dz2[1].astype(MXU_DTYPE), (dhv2[0] + dhv2[1]).astype(MXU_DTYPE), d_hg], axis=1)
    tok = on_grad("w_in", _mm(h1, d_p, "tn", WIRE_DTYPE, 512, 1664, "in_dw"))
    grad_x, gs["pre_mix_g"] = _dx_norm_bwd([(d_p, 0)], w_in[None], x, gains["pre_mix_g"], d1, 512, "in_dx_pre_bwd", after=(tok,))
    return loss, grad_x, gs


MATS = ("w_in", "w_out", "w_xq", "w_xkv", "w_xo", "w_up", "w_down")
GAINS = ("pre_mix_g", "q_norm_g", "k_norm_g", "hg_out_norm_g", "post_mix_g", "pre_x_g", "mem_norm_g", "post_x_g", "pre_ffn_g", "post_ffn_g")
WEIGHTS = ('pre_mix_g', 'w_in', 'q_norm_g', 'k_norm_g', 'hg_lb', 'hg_out_norm_g', 'w_out', 'post_mix_g', 'pre_x_g', 'mem_norm_g', 'w_xq',
           'w_xkv', 'w_xo', 'post_x_g', 'pre_ffn_g', 'w_up', 'conv_w', 'conv_b', 'w_down', 'post_ffn_g')


def kernel(x, mem, pre_mix_g, w_in, q_norm_g, k_norm_g, hg_lb, hg_out_norm_g, w_out, post_mix_g, pre_x_g, mem_norm_g, w_xq, w_xkv, w_xo, post_x_g, pre_ffn_g, w_up, conv_w, conv_b, w_down, post_ffn_g, loss_target, m_pre_mix_g, m_w_in, m_q_norm_g, m_k_norm_g, m_hg_lb, m_hg_out_norm_g, m_w_out, m_post_mix_g, m_pre_x_g, m_mem_norm_g, m_w_xq, m_w_xkv, m_w_xo, m_post_x_g, m_pre_ffn_g, m_w_up, m_conv_w, m_conv_b, m_w_down, m_post_ffn_g, v_pre_mix_g, v_w_in, v_q_norm_g, v_k_norm_g, v_hg_lb, v_hg_out_norm_g, v_w_out, v_post_mix_g, v_pre_x_g, v_mem_norm_g, v_w_xq, v_w_xkv, v_w_xo, v_post_x_g, v_pre_ffn_g, v_w_up, v_conv_w, v_conv_b, v_w_down, v_post_ffn_g):
    args = dict(locals())
    w = {k: args[k] for k in WEIGHTS}
    m = {k: args["m_" + k] for k in WEIGHTS}
    v = {k: args["v_" + k] for k in WEIGHTS}
    chip = 2 * lax.axis_index("x") + lax.axis_index("y")
    core = lax.axis_index("c")

    shards = {k: w[k][0].astype(WIRE_DTYPE) for k in MATS}

    def whole(k, g):
        return g if k in ("w_xkv", "w_up") else g.reshape(-1, g.shape[-1])

    w_in_shards = _gather_shards([shards["w_in"]], "gather_w_in")[0]
    w_in_full = jnp.concatenate([w_in_shards[s] for s in range(4)], axis=1)
    small_in = _exchange_small(_pack_small([w["conv_w"][0], w["hg_lb"]]), False, "gather_small")
    mid_names, ffn_names = ("w_out", "w_xq", "w_xkv", "w_xo"), ("w_up", "w_down")
    mid = _gather_start([shards[k] for k in mid_names], "gather_mid_start", after=(w_in_full, small_in))
    ffn = _gather_start([shards[k] for k in ffn_names], "gather_ffn_start", after=(mid[4],))

    def mid_weights(after):
        return [whole(k, g) for k, g in zip(mid_names, _gather_wait(*mid[:4], after, "gather_mid_wait"))]

    def ffn_weights(k, after):
        t = ffn_names.index(k)
        return whole(k, _gather_wait(*[part[t:t + 1] for part in ffn[:4]], after, "gather_wait_" + k)[0])

    cw_parts, lb_parts = [], []
    for s in range(4):
        cw_s, lb_s = _unpack_small(small_in[2 * s], [w["conv_w"][0].shape, w["hg_lb"].shape])
        cw_parts.append(cw_s)
        lb_parts.append(lb_s)
    conv_w_full = jnp.concatenate(cw_parts, axis=1)
    hg_lb_full = jnp.concatenate(lb_parts, axis=2)

    started = {}

    def on_grad(k, g):
        if k == "w_in":
            g = g.reshape(g.shape[0], 4, g.shape[1] // 4).transpose(1, 0, 2)
        elif g.ndim == 2:
            g = g.reshape(4, g.shape[0] // 4, g.shape[1])
        *started[k], token = _scatter_start(g, "grad_start_" + k)
        return token

    gains = {k: w[k] for k in GAINS}
    loss_part, grad_x, gs = _local_step(x[0], mem[0], loss_target[0], w_in_full, ffn[4], mid_weights, ffn_weights, on_grad, gains,
                                        conv_w_full, w["conv_b"], hg_lb_full)
    gs["loss"] = loss_part

    grads, delta, new_m, new_v = {}, {}, {}, {}

    def reduce_matrices(names, after, tag):
        sent, landed = _scatter_wait([started[k] for k in names], after, "grad_wait_" + tag)
        halves = [_sum_devices(g, land, chip, core, "grad_sum_" + k) for k, g, land in zip(names, sent, landed)]
        for k, r in zip(names, _join_halves(halves, "grad_join_" + tag)):
            grads[k] = r.reshape(1, -1, r.shape[-1])

    def adamw(names):
        for k in names:
            shape = w[k].shape
            keep = len(shape) == 3 and shape[0] == 1
            two_d = lambda a: a.reshape(shape) if keep else a.reshape(-1, shape[-1])
            d, mo, vo, go = _adamw(two_d(w[k]), two_d(grads[k]), two_d(m[k]), two_d(v[k]), "adamw_" + k)
            delta[k], new_m[k], new_v[k], grads[k] = d.reshape(shape), mo.reshape(shape), vo.reshape(shape), go.reshape(shape)

    early = tuple(k for k in MATS if k != "w_in")
    reduce_matrices(early, (grad_x,), "early")
    adamw(early)

    small_names = GAINS + ("conv_b", "conv_w", "hg_lb")
    packed = _pack_small([gs[k] for k in small_names + ("loss",)])
    reduced_small = _exchange_small(packed, True, "reduce_small", after=tuple(new_v[k] for k in early))
    *summed, loss = _unpack_small(reduced_small, [gs[k].shape for k in small_names + ("loss",)])
    loss = loss[0, 0]
    for k, g in zip(small_names, summed):
        grads[k] = g
    ncw = w["conv_w"].shape[2]
    grads["conv_w"] = lax.dynamic_slice_in_dim(grads["conv_w"], chip * ncw, ncw, axis=1)[None]
    nlb = w["hg_lb"].shape[2]
    grads["hg_lb"] = lax.dynamic_slice_in_dim(grads["hg_lb"], chip * nlb, nlb, axis=2)
    replicated = GAINS + ("conv_b",)
    shapes = [w[k].shape for k in replicated]
    rows = sum(int(np.prod(s)) for s in shapes) // 128
    pack = lambda d: jnp.concatenate([d[k].reshape(-1) for k in replicated]).reshape(rows, 128)
    outs = _adamw(pack(w), reduced_small[:rows], pack(m), pack(v), "adamw_replicated")
    for into, packed_out in zip((delta, new_m, new_v, grads), outs):
        for k, a in zip(replicated, _unpack_small(packed_out, shapes)):
            into[k] = a
    adamw(("conv_w", "hg_lb"))

    reduce_matrices(("w_in",), tuple(new_v[k] for k in early + small_names), "late")
    adamw(("w_in",))
    return (loss, grad_x[None], *[grads[k] for k in WEIGHTS], *[delta[k] for k in WEIGHTS],
            *[new_m[k] for k in WEIGHTS], *[new_v[k] for k in WEIGHTS])
```

```python
import numpy as np
import jax
import jax.numpy as jnp
from jax import lax
from jax.experimental import pallas as pl
from jax.experimental.pallas import tpu as pltpu

F32 = jnp.float32
MXU_DTYPE = jnp.bfloat16
WIRE_DTYPE = jnp.bfloat16
VMEM_LIMIT_BYTES = 56 * 1024 * 1024
EPS = 1e-6
MESH = pl.DeviceIdType.MESH

GRID_W = 64
ATT_HEADS, ATT_KV_HEADS, ATT_HEAD_DIM = 8, 2, 64
ATT_GROUP = ATT_HEADS // ATT_KV_HEADS
ATT_Q_DIM, ATT_KV_DIM = 512, 128
ROPE_THETA = 10000.0
HG_HEADS, HG_HEAD_DIM, HG_DIM = 4, 128, 512
HG_CHUNK = 128
HG_LEVELS = 7
HG_PAIR = 2 * HG_HEAD_DIM
X_HEADS, X_HEAD_DIM = 4, 256
D_FF = 2816
FF_COLS = 256
FF_BLOCKS = D_FF // FF_COLS
OFF_AK, OFF_AV, OFF_HQ, OFF_ZF, OFF_ZB, OFF_HI, OFF_HG = 512, 640, 768, 1280, 1792, 2304, 2816

ADAM_LR, ADAM_B1, ADAM_B2, ADAM_EPS, ADAM_WD, ADAM_STEP = 0.001, 0.9, 0.999, 1e-08, 0.01, 10

SDS = jax.ShapeDtypeStruct


def _cp(*sem):
    return pltpu.CompilerParams(dimension_semantics=sem, vmem_limit_bytes=VMEM_LIMIT_BYTES)


def _dot(a, b, form="nn"):
    dims = {"nn": (((1,), (0,)), ((), ())), "nt": (((1,), (1,)), ((), ())), "tn": (((0,), (0,)), ((), ()))}[form]
    return lax.dot_general(a.astype(MXU_DTYPE), b.astype(MXU_DTYPE), dims, preferred_element_type=F32)


def _sigmoid(x):
    return 0.5 * jnp.tanh(0.5 * x) + 0.5


def _rstd(x):
    return lax.rsqrt(jnp.mean(x * x, axis=-1, keepdims=True) + EPS)


def _rms_bwd(x, g, dy):
    r = _rstd(x)
    xh = x * r
    dn = dy * g
    dx = r * (dn - xh * jnp.mean(dn * xh, axis=-1, keepdims=True))
    return dx, jnp.sum(dy * xh, axis=0, keepdims=True)


def _unread(after):
    after = tuple(a for a in after if a is not None)
    return after, [pl.BlockSpec(memory_space=pl.ANY)] * len(after)


def _mm(a, b, form, out_dtype, tm, tn, name, after=()):
    after, after_specs = _unread(after)
    if form == "nn":
        (m, k), n = a.shape, b.shape[1]
    elif form == "nt":
        (m, k), n = a.shape, b.shape[0]
    else:
        (k, m), n = a.shape, b.shape[1]
    tm, tn = min(tm, m), min(tn, n)
    assert m % tm == 0 and n % tn == 0, (name, m, n, tm, tn)

    def body(a_ref, b_ref, *rest):
        o_ref = rest[-1]
        o_ref[...] = _dot(a_ref[...], b_ref[...], form).astype(o_ref.dtype)

    a_spec = pl.BlockSpec((k, tm), lambda i, j: (0, i)) if form == "tn" else pl.BlockSpec((tm, k), lambda i, j: (i, 0))
    b_spec = pl.BlockSpec((tn, k), lambda i, j: (j, 0)) if form == "nt" else pl.BlockSpec((k, tn), lambda i, j: (0, j))
    return pl.pallas_call(
        body, name=name, grid=(m // tm, n // tn), in_specs=[a_spec, b_spec] + after_specs,
        out_specs=pl.BlockSpec((tm, tn), lambda i, j: (i, j)), out_shape=SDS((m, n), out_dtype),
        compiler_params=_cp("parallel", "parallel"))(a, b, *after)


def _mm_nt_parts(a_parts, b, out_dtype, tm, tn, name, after=()):
    after, after_specs = _unread(after)
    parts, n, p = b.shape
    m = a_parts[0][0].shape[0]
    tm, tn = min(tm, m), min(tn, n)
    assert m % tm == 0 and n % tn == 0 and len(a_parts) == parts, (name, m, b.shape)

    def body(*refs):
        o_ref = refs[-1]
        acc = _dot(refs[0][...], refs[parts][0], "nt")
        for s in range(1, parts):
            acc = acc + _dot(refs[s][...], refs[parts + s][0], "nt")
        o_ref[...] = acc.astype(o_ref.dtype)

    a_specs = [pl.BlockSpec((tm, p), lambda i, j, cb=cb: (i, cb)) for _, cb in a_parts]
    b_specs = [pl.BlockSpec((1, tn, p), lambda i, j, s=s: (s, j, 0)) for s in range(parts)]
    return pl.pallas_call(
        body, name=name, grid=(m // tm, n // tn), in_specs=a_specs + b_specs + after_specs,
        out_specs=pl.BlockSpec((tm, tn), lambda i, j: (i, j)), out_shape=SDS((m, n), out_dtype),
        compiler_params=_cp("parallel", "parallel"))(*[arr for arr, _ in a_parts], *([b] * parts), *after)


def _norm_bwd_mm(y, g, d, w, out_dtype, tm, tn, name):
    n, dm = y.shape
    nn = w.shape[0]
    tm, tn = min(tm, n), min(tn, nn)
    assert n % tm == 0 and nn % tn == 0 and w.shape[1] == dm, (name, y.shape, w.shape)

    def body(y_ref, g_ref, d_ref, w_ref, dx_ref, dy_ref, dg_ref, dys):
        i, j = pl.program_id(0), pl.program_id(1)

        @pl.when(jnp.logical_and(i == 0, j == 0))
        def _():
            dg_ref[...] = jnp.zeros_like(dg_ref)

        @pl.when(j == 0)
        def _():
            dy, dg = _rms_bwd(y_ref[...], g_ref[...], d_ref[...])
            dy = dy.astype(MXU_DTYPE)
            dys[...] = dy
            dy_ref[...] = dy
            dg_ref[...] += dg

        dx_ref[...] = _dot(dys[...], w_ref[...], "nt").astype(dx_ref.dtype)

    row = pl.BlockSpec((tm, dm), lambda i, j: (i, 0))
    vec = pl.BlockSpec((1, dm), lambda i, j: (0, 0))
    return pl.pallas_call(
        body, name=name, grid=(n // tm, nn // tn), in_specs=[row, vec, row, pl.BlockSpec((tn, dm), lambda i, j: (j, 0))],
        out_specs=[pl.BlockSpec((tm, tn), lambda i, j: (i, j)), row, vec],
        out_shape=[SDS((n, nn), out_dtype), SDS((n, dm), MXU_DTYPE), SDS((1, dm), F32)],
        scratch_shapes=[pltpu.VMEM((tm, dm), MXU_DTYPE)],
        compiler_params=_cp("arbitrary", "arbitrary"))(y, g, d, w)


def _mm_resid_norm(a, b, x, g, tm, name, target=None):
    n, k = a.shape
    d = b.shape[1]
    tm = min(tm, n)
    assert n % tm == 0 and x.shape == (n, d), (name, a.shape, b.shape)
    with_loss = target is not None

    def body(a_ref, b_ref, x_ref, g_ref, *rest):
        y = _dot(a_ref[...], b_ref[...])
        out = x_ref[...] + y * _rstd(y) * g_ref[...]
        if not with_loss:
            y_ref, o_ref = rest
            y_ref[...] = y
            o_ref[...] = out
            return
        t_ref, y_ref, d_ref, l_ref = rest
        y_ref[...] = y
        diff = out - t_ref[...]
        d_ref[...] = diff * (1.0 / d)

        @pl.when(pl.program_id(0) == 0)
        def _():
            l_ref[...] = jnp.zeros_like(l_ref)

        l_ref[...] += 0.5 * jnp.sum(jnp.mean(diff * diff, axis=-1, keepdims=True), axis=0, keepdims=True)

    row = pl.BlockSpec((tm, d), lambda i: (i, 0))
    ins = [pl.BlockSpec((tm, k), lambda i: (i, 0)), pl.BlockSpec((k, d), lambda i: (0, 0)), row, pl.BlockSpec((1, d), lambda i: (0, 0))]
    out = SDS((n, d), F32)
    if with_loss:
        return pl.pallas_call(body, name=name, grid=(n // tm,), in_specs=ins + [row], out_specs=[row, row, pl.BlockSpec((1, 1), lambda i: (0, 0))],
                              out_shape=[out, out, SDS((1, 1), F32)], compiler_params=_cp("arbitrary"))(a, b, x, g, target)
    return pl.pallas_call(body, name=name, grid=(n // tm,), in_specs=ins, out_specs=[row, row], out_shape=[out, out],
                          compiler_params=_cp("parallel"))(a, b, x, g)


def _dx_norm_bwd(a_parts, b, x, g, res, tm, name, after=()):
    after, after_specs = _unread(after)
    parts, d, p = b.shape
    n = x.shape[0]
    tm = min(tm, n)
    assert n % tm == 0 and len(a_parts) == parts and x.shape[1] == d, (name, x.shape, b.shape)

    def body(*refs):
        x_ref, g_ref, res_ref = refs[2 * parts:2 * parts + 3]
        dx_ref, dg_ref = refs[-2:]
        dh = _dot(refs[0][...], refs[parts][0], "nt")
        for s in range(1, parts):
            dh = dh + _dot(refs[s][...], refs[parts + s][0], "nt")
        dx, dg = _rms_bwd(x_ref[...], g_ref[...], dh)
        dx_ref[...] = dx + res_ref[...]

        @pl.when(pl.program_id(0) == 0)
        def _():
            dg_ref[...] = jnp.zeros_like(dg_ref)

        dg_ref[...] += dg

    a_specs = [pl.BlockSpec((tm, p), lambda i, cb=cb: (i, cb)) for _, cb in a_parts]
    b_specs = [pl.BlockSpec((1, d, p), lambda i, s=s: (s, 0, 0)) for s in range(parts)]
    row = pl.BlockSpec((tm, d), lambda i: (i, 0))
    vec = pl.BlockSpec((1, d), lambda i: (0, 0))
    return pl.pallas_call(
        body, name=name, grid=(n // tm,), in_specs=a_specs + b_specs + [row, vec, row] + after_specs,
        out_specs=[row, vec], out_shape=[SDS((n, d), F32), SDS((1, d), F32)],
        compiler_params=_cp("arbitrary"))(*[arr for arr, _ in a_parts], *([b] * parts), x, g, res, *after)


def _dw_by_owner(a, b, tn, first, into, tm, name):
    k, m = a.shape
    cnt = b.shape[1] // tn
    tm = min(tm, m)
    assert m % tm == 0 and b.shape[1] == cnt * tn and first + cnt <= 4, (name, a.shape, b.shape)

    def body(a_ref, b_ref, *rest):
        rest[-1][0] = _dot(a_ref[...], b_ref[...], "tn").astype(rest[-1].dtype)

    extra = [] if into is None else [into]
    return pl.pallas_call(
        body, name=name, grid=(m // tm, cnt),
        in_specs=[pl.BlockSpec((k, tm), lambda i, j: (0, i)), pl.BlockSpec((k, tn), lambda i, j: (0, j))] + [pl.BlockSpec(memory_space=pl.ANY)] * len(extra),
        out_specs=pl.BlockSpec((1, tm, tn), lambda i, j: (first + j, i, 0)), out_shape=SDS((4, m, tn), WIRE_DTYPE),
        input_output_aliases={2: 0} if extra else {},
        compiler_params=_cp("parallel", "parallel"))(a, b, *extra)


def _norm_mm(x, g, w, out_dtype, tm, tn, name, after=()):
    after, after_specs = _unread(after)
    m, d = x.shape
    sharded = w.ndim == 3
    n = w.shape[-1] * (w.shape[0] if sharded else 1)
    tm, tn = min(tm, m), (w.shape[-1] if sharded else min(tn, n))
    assert m % tm == 0 and n % tn == 0, (name, m, n, tm, tn)

    def body(x_ref, g_ref, w_ref, *rest):
        o_ref, h_ref, hs = rest[-3:]

        @pl.when(pl.program_id(1) == 0)
        def _():
            xv = x_ref[...]
            h = (xv * _rstd(xv) * g_ref[...]).astype(MXU_DTYPE)
            hs[...] = h
            h_ref[...] = h

        o_ref[...] = _dot(hs[...], w_ref[0] if sharded else w_ref[...]).astype(o_ref.dtype)

    w_spec = pl.BlockSpec((1, d, tn), lambda i, j: (j, 0, 0)) if sharded else pl.BlockSpec((d, tn), lambda i, j: (0, j))
    return pl.pallas_call(
        body, name=name, grid=(m // tm, n // tn),
        in_specs=[pl.BlockSpec((tm, d), lambda i, j: (i, 0)), pl.BlockSpec((1, d), lambda i, j: (0, 0)), w_spec] + after_specs,
        out_specs=[pl.BlockSpec((tm, tn), lambda i, j: (i, j)), pl.BlockSpec((tm, d), lambda i, j: (i, 0))],
        out_shape=[SDS((m, n), out_dtype), SDS((m, d), MXU_DTYPE)],
        scratch_shapes=[pltpu.VMEM((tm, d), MXU_DTYPE)],
        compiler_params=_cp("parallel", "arbitrary"))(x, g, w, *after)


ROW_TILE = 512
TOKEN_TILE = 1024


def _norm_bwd(x, g, dy, res, out_dtype, name):
    n, d = x.shape
    tr = min(ROW_TILE, n)
    has_res = res is not None

    def body(*refs):
        x_ref, g_ref, dy_ref = refs[:3]
        dx_ref, dg_ref = refs[-2:]
        dx, dg = _rms_bwd(x_ref[...], g_ref[...], dy_ref[...].astype(F32))
        if has_res:
            dx = dx + refs[3][...]
        dx_ref[...] = dx.astype(dx_ref.dtype)

        @pl.when(pl.program_id(0) == 0)
        def _():
            dg_ref[...] = jnp.zeros_like(dg_ref)

        dg_ref[...] += dg

    row = pl.BlockSpec((tr, d), lambda i: (i, 0))
    vec = pl.BlockSpec((1, d), lambda i: (0, 0))
    ins = [x, g, dy] + ([res] if has_res else [])
    return pl.pallas_call(
        body, name=name, grid=(n // tr,), in_specs=[row, vec, row] + ([row] if has_res else []),
        out_specs=[row, vec], out_shape=[SDS((n, d), out_dtype), SDS((1, d), F32)],
        compiler_params=_cp("arbitrary"))(*ins)


def _rope_tables(n):
    pairs = ATT_HEAD_DIM // 4
    t = np.arange(n)
    inv = np.power(ROPE_THETA, -np.arange(pairs, dtype=np.float32) / pairs).astype(np.float32)
    ang = np.concatenate([(t // GRID_W)[:, None].astype(np.float32) * inv, (t % GRID_W)[:, None].astype(np.float32) * inv], axis=-1)
    cos = np.repeat(np.cos(ang), 2, axis=-1)
    sin = np.repeat(np.sin(ang), 2, axis=-1) * np.tile(np.array([-1.0, 1.0], np.float32), ATT_HEAD_DIM // 2)
    return jnp.asarray(np.tile(cos, 2), F32), jnp.asarray(np.tile(sin, 2), F32)


def _swap_pairs(x):
    lane = lax.broadcasted_iota(jnp.int32, x.shape, 1)
    return jnp.where((lane & 1) == 0, pltpu.roll(x, 127, axis=1), pltpu.roll(x, 1, axis=1))


def _head_mean(v):
    lane = lax.broadcasted_iota(jnp.int32, v.shape, 1)
    lo = jnp.where(lane < ATT_HEAD_DIM, v, 0.0)
    s0 = jnp.sum(lo, axis=-1, keepdims=True)
    s1 = jnp.sum(v - lo, axis=-1, keepdims=True)
    return jnp.where(lane < ATT_HEAD_DIM, s0, s1) * (1.0 / ATT_HEAD_DIM)


def _qk_prep(p, gq, gk, cos, sin, name):
    n = p.shape[0]
    tr = min(ROW_TILE, n)

    def one(xv, g, c, s):
        xn = xv * lax.rsqrt(_head_mean(xv * xv) + EPS) * g
        return xn * c + _swap_pairs(xn) * s

    def body(q_ref, k_ref, gq_ref, gk_ref, c_ref, s_ref, qo_ref, ko_ref):
        c, s = c_ref[...], s_ref[...]
        for j in range(ATT_Q_DIM // 128):
            qo_ref[:, j * 128:(j + 1) * 128] = one(q_ref[:, j * 128:(j + 1) * 128], gq_ref[...], c, s).astype(qo_ref.dtype)
        ko_ref[...] = one(k_ref[...], gk_ref[...], c, s).astype(ko_ref.dtype)

    vec = pl.BlockSpec((1, 128), lambda i: (0, 0))
    tab = pl.BlockSpec((tr, 128), lambda i: (i, 0))
    return pl.pallas_call(
        body, name=name, grid=(n // tr,),
        in_specs=[pl.BlockSpec((tr, ATT_Q_DIM), lambda i: (i, 0)), pl.BlockSpec((tr, 128), lambda i: (i, OFF_AK // 128)), vec, vec, tab, tab],
        out_specs=[pl.BlockSpec((tr, ATT_Q_DIM), lambda i: (i, 0)), tab],
        out_shape=[SDS((n, ATT_Q_DIM), MXU_DTYPE), SDS((n, ATT_KV_DIM), MXU_DTYPE)],
        compiler_params=_cp("parallel"))(p, p, gq, gk, cos, sin)


def _qk_prep_bwd(p, gq, gk, cos, sin, dq, dk, name):
    n = p.shape[0]
    tr = min(ROW_TILE, n)

    def one(xv, g, c, s, dout):
        dxn = dout * c + _swap_pairs(dout * s)
        r = lax.rsqrt(_head_mean(xv * xv) + EPS)
        xh = xv * r
        dn = dxn * g
        dx = r * (dn - xh * _head_mean(dn * xh))
        return dx, jnp.sum(dxn * xh, axis=0, keepdims=True)

    def body(q_ref, k_ref, gq_ref, gk_ref, c_ref, s_ref, dq_ref, dk_ref, dqo_ref, dko_ref, dgq_ref, dgk_ref):
        @pl.when(pl.program_id(0) == 0)
        def _():
            dgq_ref[...] = jnp.zeros_like(dgq_ref)
            dgk_ref[...] = jnp.zeros_like(dgk_ref)

        c, s = c_ref[...], s_ref[...]
        for j in range(ATT_Q_DIM // 128):
            sl = slice(j * 128, (j + 1) * 128)
            dx, dg = one(q_ref[:, sl], gq_ref[...], c, s, dq_ref[:, sl])
            dqo_ref[:, sl] = dx.astype(dqo_ref.dtype)
            dgq_ref[:, sl] += dg
        dx, dg = one(k_ref[...], gk_ref[...], c, s, dk_ref[...])
        dko_ref[...] = dx.astype(dko_ref.dtype)
        dgk_ref[...] += dg

    vec = pl.BlockSpec((1, 128), lambda i: (0, 0))
    tab = pl.BlockSpec((tr, 128), lambda i: (i, 0))
    qrow = pl.BlockSpec((tr, ATT_Q_DIM), lambda i: (i, 0))
    return pl.pallas_call(
        body, name=name, grid=(n // tr,),
        in_specs=[qrow, pl.BlockSpec((tr, 128), lambda i: (i, OFF_AK // 128)), vec, vec, tab, tab, qrow, tab],
        out_specs=[qrow, tab, pl.BlockSpec((1, ATT_Q_DIM), lambda i: (0, 0)), vec],
        out_shape=[SDS((n, ATT_Q_DIM), MXU_DTYPE), SDS((n, ATT_KV_DIM), MXU_DTYPE), SDS((1, ATT_Q_DIM), F32), SDS((1, 128), F32)],
        compiler_params=_cp("arbitrary"))(p, p, gq, gk, cos, sin, dq, dk)


ATT_TQ = 256


def _attn_fwd(q, k, v, name):
    n = q.shape[0]
    tq = min(ATT_TQ, n)
    scale = ATT_HEAD_DIM ** -0.5
    gw = ATT_GROUP * ATT_HEAD_DIM

    def body(q_ref, k_ref, v_ref, o_ref):
        kk, vv = k_ref[0], v_ref[0]
        v_ones = jnp.concatenate([vv, jnp.ones_like(vv)], axis=1)
        outs = []
        for g in range(ATT_GROUP):
            s = _dot(q_ref[:, g * ATT_HEAD_DIM:(g + 1) * ATT_HEAD_DIM] * scale, kk, "nt")
            e = jnp.exp(s - jnp.max(s, axis=-1, keepdims=True))
            ov = _dot(e, v_ones)
            outs.append(ov[:, :ATT_HEAD_DIM] / ov[:, ATT_HEAD_DIM:])
        o_ref[...] = jnp.concatenate(outs, axis=-1).astype(o_ref.dtype)

    kv = pl.BlockSpec((1, n, ATT_HEAD_DIM), lambda h, i: (h, 0, 0))
    return pl.pallas_call(
        body, name=name, grid=(ATT_KV_HEADS, n // tq),
        in_specs=[pl.BlockSpec((tq, gw), lambda h, i: (i, h)), kv, kv],
        out_specs=pl.BlockSpec((tq, gw), lambda h, i: (i, h)), out_shape=SDS((n, ATT_Q_DIM), MXU_DTYPE),
        compiler_params=_cp("parallel", "parallel"))(q, k, v)


def _attn_bwd(q, k, v, o, do, name):
    n = q.shape[0]
    tq = min(ATT_TQ, n)
    scale = ATT_HEAD_DIM ** -0.5
    gw = ATT_GROUP * ATT_HEAD_DIM

    def body(q_ref, k_ref, v_ref, o_ref, do_ref, dq_ref, dk_ref, dv_ref):
        @pl.when(pl.program_id(1) == 0)
        def _():
            dk_ref[...] = jnp.zeros_like(dk_ref)
            dv_ref[...] = jnp.zeros_like(dv_ref)

        kk, vv = k_ref[0], v_ref[0]
        dqs = []
        dk_acc = jnp.zeros((ATT_HEAD_DIM, n), F32)
        dv_acc = jnp.zeros((ATT_HEAD_DIM, n), F32)
        for g in range(ATT_GROUP):
            sl = slice(g * ATT_HEAD_DIM, (g + 1) * ATT_HEAD_DIM)
            qg, dog = q_ref[:, sl] * scale, do_ref[:, sl].astype(F32)
            s = _dot(qg, kk, "nt")
            e = jnp.exp(s - jnp.max(s, axis=-1, keepdims=True))
            inv = 1.0 / jnp.sum(e, axis=-1, keepdims=True)
            delta = jnp.sum(dog * o_ref[:, sl].astype(F32), axis=-1, keepdims=True)
            dse = e * (_dot(dog, vv, "nt") - delta)
            dqs.append(_dot(dse, kk) * (inv * scale))
            dk_acc += _dot(qg.astype(F32) * inv, dse, "tn")
            dv_acc += _dot(dog * inv, e, "tn")
        dq_ref[...] = jnp.concatenate(dqs, axis=-1)
        dk_ref[0] += dk_acc
        dv_ref[0] += dv_acc

    kv = pl.BlockSpec((1, n, ATT_HEAD_DIM), lambda h, i: (h, 0, 0))
    kvt = pl.BlockSpec((1, ATT_HEAD_DIM, n), lambda h, i: (h, 0, 0))
    qb = pl.BlockSpec((tq, gw), lambda h, i: (i, h))
    return pl.pallas_call(
        body, name=name, grid=(ATT_KV_HEADS, n // tq), in_specs=[qb, kv, kv, qb, qb], out_specs=[qb, kvt, kvt],
        out_shape=[SDS((n, ATT_Q_DIM), F32), SDS((ATT_KV_HEADS, ATT_HEAD_DIM, n), F32), SDS((ATT_KV_HEADS, ATT_HEAD_DIM, n), F32)],
        compiler_params=_cp("parallel", "arbitrary"))(q, k, v, o, do)


def _both_directions(mats, axis):
    fwd = np.concatenate(mats, axis=axis).astype(np.float32)
    bwd = np.concatenate([m[::-1, ::-1] for m in mats], axis=axis).astype(np.float32)
    return jnp.asarray(np.stack([fwd, bwd]), MXU_DTYPE)


def _hg_segments():
    c = HG_CHUNK
    t = np.arange(c)[:, None]
    r = np.arange(c)[None, :]
    mats = [(r <= t)]
    for lev in range(HG_LEVELS):
        h = c >> (lev + 1)
        mid = (t // (2 * h)) * (2 * h) + h - 1
        hi = (t // h) % 2 == 1
        mats.append(np.where(hi, (r > mid) & (r <= t), (r > t) & (r <= mid)))
    mats.append(r > t)
    return _both_directions(mats, 0)


def _hg_pair_sums():
    c = HG_CHUNK
    r = np.arange(c)[:, None]
    t = np.arange(c)[None, :]
    gp, gn = [t >= r], [t < r]
    for lev in range(HG_LEVELS):
        sh = HG_LEVELS - 1 - lev
        same = (r >> sh) == (t >> sh)
        gp.append(same & (t >= r))
        gn.append(same & (t < r))
    return _both_directions(gp, 1), _both_directions(gn, 1)


def _split_dot(mat, x):
    hi = x.astype(MXU_DTYPE)
    lo = (x - hi.astype(F32)).astype(MXU_DTYPE)
    return _dot(mat, hi) + _dot(mat, lo)


def _hg_gates(hq, z, a0, a1):
    q = hq * _sigmoid(hq)
    sg = _sigmoid(z)
    lb = _sigmoid(a0 - a1)
    f = lb + (1.0 - lb) * sg
    k = (1.0 - lb) * (1.0 - sg)
    return q, f, k, sg, lb


def _hg_level_masks():
    c = HG_CHUNK
    t = np.arange(c)
    later, same = [], []
    for lev in range(HG_LEVELS):
        sh = HG_LEVELS - 1 - lev
        later.append(np.broadcast_to((((t >> sh) & 1) == 1)[:, None], (c, HG_HEAD_DIM)))
        same.append((t[:, None] >> (sh + 1)) == (t[None, :] >> (sh + 1)))
    same.append(t[:, None] == t[None, :])
    later = np.stack(later).astype(np.float32)
    return jnp.asarray(np.stack([later, 1.0 - later]), F32), jnp.asarray(np.stack(same).astype(np.float32), F32)


def _hg_level(q, k, ex, later_ref, lev):
    e = ex[lev + 1]
    e_q = e * later_ref[0, lev]
    e_k = e - e_q
    return q * e_q, k * e_k, e_q, e_k


def _hg_intra(q, k, ex, later_ref, same_ref):
    a = same_ref[HG_LEVELS] * jnp.sum(q * k, axis=-1, keepdims=True)
    for lev in range(HG_LEVELS):
        qs, ks, _, _ = _hg_level(q, k, ex, later_ref, lev)
        a = a + same_ref[lev] * _dot(qs, ks, "nt")
    return a


def _hg_specs(n, with_time):
    c = HG_CHUNK
    nc = n // c

    def chunk(d, i):
        first = d if with_time else 1 - d
        return i + first * (nc - 1 - 2 * i)

    def pcols(off, dir_stride=0):
        return [pl.BlockSpec((c, HG_PAIR), lambda d, i, j=j: (chunk(d, i), off // HG_PAIR + dir_stride // HG_PAIR * d + j)) for j in range(2)]

    specs = dict(
        hq=pcols(OFF_HQ), v=pcols(OFF_HI), z=pcols(OFF_ZF, OFF_ZB - OFF_ZF),
        shared=pl.BlockSpec((c, HG_DIM), lambda d, i: (chunk(d, i), 0)),
        per_dir=pl.BlockSpec((1, c, HG_DIM), lambda d, i: (d, chunk(d, i), 0)),
        vec=pl.BlockSpec((1, 1, HG_DIM), lambda d, i: (d, 0, 0)),
        seg=pl.BlockSpec((1, (HG_LEVELS + 2) * c, c), lambda d, i: (d, 0, 0)),
        sums=pl.BlockSpec((1, c, (HG_LEVELS + 1) * c), lambda d, i: (d, 0, 0)),
        later=pl.BlockSpec((1, HG_LEVELS, c, HG_HEAD_DIM), lambda d, i: (d, 0, 0, 0)),
        same=pl.BlockSpec((HG_LEVELS + 1, c, c), lambda d, i: (0, 0, 0)),
        state=pl.BlockSpec((1, HG_HEADS, 1, HG_HEAD_DIM, HG_HEAD_DIM), lambda d, i: (d, 0, chunk(d, i), 0, 0)),
        weights=pl.BlockSpec((1, HG_HEADS, 1, c, c), lambda d, i: (d, 0, chunk(d, i), 0, 0)),
        levels=pl.BlockSpec((1, HG_HEADS, 1, HG_LEVELS, c, HG_HEAD_DIM), lambda d, i: (d, 0, chunk(d, i), 0, 0, 0)))
    return nc, specs


def _hg_head(refs, hh):
    off = (hh % 2) * HG_HEAD_DIM
    return refs[hh // 2][:, off:off + HG_HEAD_DIM]


def _hg_lanes(hh):
    return slice(hh * HG_HEAD_DIM, (hh + 1) * HG_HEAD_DIM)


def _hg_exps(seg_ref, f, kept=None):
    c = HG_CHUNK
    lf = jnp.log(f)
    if kept is None:
        args = _split_dot(seg_ref[0], lf)
        return [jnp.exp(args[j * c:(j + 1) * c]) for j in range(HG_LEVELS + 2)]
    chunk_wide = jnp.concatenate([seg_ref[0, 0:c], seg_ref[0, (HG_LEVELS + 1) * c:(HG_LEVELS + 2) * c]], axis=0)
    args = _split_dot(chunk_wide, lf)
    return [jnp.exp(args[:c])] + [e.astype(F32) for e in kept] + [jnp.exp(args[c:])]


def _hg_last_row(a, mirrored):
    return jnp.where(mirrored, a[0:1, :], a[HG_CHUNK - 1:HG_CHUNK, :])


def _hgrn_fwd(p, a0, a1, seg, masks, name):
    n = p.shape[0]
    nc, sp = _hg_specs(n, True)

    def body(hq0, hq1, z0, z1, v0, v1, a0_ref, a1_ref, seg_ref, later_ref, same_ref, o_ref, s0_ref, a_ref, e_ref, st):
        @pl.when(pl.program_id(1) == 0)
        def _():
            st[...] = jnp.zeros_like(st)

        mirrored = pl.program_id(0) == 1
        for hh in range(HG_HEADS):
            ln = _hg_lanes(hh)
            q, f, k, _, _ = _hg_gates(_hg_head((hq0, hq1), hh), _hg_head((z0, z1), hh), a0_ref[0, :, ln], a1_ref[0, :, ln])
            vv = _hg_head((v0, v1), hh)
            ex = _hg_exps(seg_ref, f)
            for lev in range(HG_LEVELS):
                e_ref[0, hh, 0, lev] = ex[lev + 1].astype(e_ref.dtype)
            a = _hg_intra(q, k, ex, later_ref, same_ref).astype(MXU_DTYPE)
            a_ref[0, hh, 0] = a
            s_t = st[hh]
            s0_ref[0, hh, 0] = s_t
            o_ref[0, :, ln] = _dot(a, vv) + _dot(q * ex[0], s_t, "nt")
            st[hh] = s_t * _hg_last_row(ex[0], mirrored) + _dot(vv, k * ex[HG_LEVELS + 1], "tn")

    return pl.pallas_call(
        body, name=name, grid=(2, nc), in_specs=sp["hq"] + sp["z"] + sp["v"] + [sp["vec"], sp["vec"], sp["seg"], sp["later"], sp["same"]],
        out_specs=[sp["per_dir"], sp["state"], sp["weights"], sp["levels"]],
        out_shape=[SDS((2, n, HG_DIM), F32), SDS((2, HG_HEADS, nc, HG_HEAD_DIM, HG_HEAD_DIM), F32),
                   SDS((2, HG_HEADS, nc, HG_CHUNK, HG_CHUNK), MXU_DTYPE),
                   SDS((2, HG_HEADS, nc, HG_LEVELS, HG_CHUNK, HG_HEAD_DIM), MXU_DTYPE)],
        scratch_shapes=[pltpu.VMEM((HG_HEADS, HG_HEAD_DIM, HG_HEAD_DIM), F32)],
        compiler_params=_cp("parallel", "arbitrary"))(p, p, p, p, p, p, a0, a1, seg, *masks)


def _hgrn_bwd(p, a0, a1, seg, masks, gp, gn, do, s0, a, e, name):
    n = p.shape[0]
    nc, sp = _hg_specs(n, False)


    def body(hq0, hq1, z0, z1, v0, v1, a0_ref, a1_ref, seg_ref, later_ref, same_ref, gp_ref, gn_ref, do_ref, s0_ref, a_ref, e_ref,
             dhq_ref, dz_ref, dv_ref, dlb_ref, rt):
        @pl.when(pl.program_id(1) == 0)
        def _():
            rt[...] = jnp.zeros_like(rt)
            dlb_ref[...] = jnp.zeros_like(dlb_ref)

        mirrored = pl.program_id(0) == 1
        for hh in range(HG_HEADS):
            ln = _hg_lanes(hh)
            hqv = _hg_head((hq0, hq1), hh)
            q, f, k, sg, lb = _hg_gates(hqv, _hg_head((z0, z1), hh), a0_ref[0, :, ln], a1_ref[0, :, ln])
            vv, dov = _hg_head((v0, v1), hh), do_ref[:, ln]
            ex = _hg_exps(seg_ref, f, kept=[e_ref[0, hh, 0, lev] for lev in range(HG_LEVELS)])
            a = a_ref[0, hh, 0]
            da = _dot(dov, vv, "nt")
            diag = jnp.sum(dov * vv, axis=-1, keepdims=True)
            s_t = s0_ref[0, hh, 0]
            r_t = rt[hh]
            k_end = k * ex[HG_LEVELS + 1]
            dv_ref[0, :, ln] = _dot(a, dov, "tn") + _dot(k_end, r_t, "nt")
            dq_inter = ex[0] * _dot(dov, s_t)
            dk_inter = ex[HG_LEVELS + 1] * _dot(vv, r_t)
            dq = diag * k + dq_inter
            dk = diag * q + dk_inter
            q_terms, k_terms = [q * dq_inter], [k * dk_inter]
            for lev in range(HG_LEVELS):
                qs, ks, e_q, e_k = _hg_level(q, k, ex, later_ref, lev)
                pairs = da * same_ref[lev]
                q_part = e_q * _dot(pairs, ks)
                k_part = e_k * _dot(pairs, qs, "tn")
                dq, dk = dq + q_part, dk + k_part
                q_terms.append(q * q_part)
                k_terms.append(k * k_part)
            decay = _hg_last_row(ex[0], mirrored)
            rt[hh] = r_t * decay + _dot(dov, q * ex[0], "tn")
            later = decay * jnp.sum(s_t * r_t, axis=0, keepdims=True)
            dlf = _dot(gp_ref[0], jnp.concatenate(q_terms, axis=0)) + _dot(gn_ref[0], jnp.concatenate(k_terms, axis=0)) + later
            df = dlf / f - dk
            dz_ref[0, :, ln] = df * (1.0 - lb) * sg * (1.0 - sg)
            dlb_ref[0, :, ln] += jnp.sum(df * (1.0 - sg), axis=0, keepdims=True)
            sq = _sigmoid(hqv)
            dhq_ref[0, :, ln] = dq * sq * (1.0 + hqv * (1.0 - sq))

    out = SDS((2, n, HG_DIM), F32)
    return pl.pallas_call(
        body, name=name, grid=(2, nc),
        in_specs=sp["hq"] + sp["z"] + sp["v"] + [sp["vec"], sp["vec"], sp["seg"], sp["later"], sp["same"], sp["sums"], sp["sums"],
                                                 sp["shared"], sp["state"], sp["weights"], sp["levels"]],
        out_specs=[sp["per_dir"], sp["per_dir"], sp["per_dir"], sp["vec"]], out_shape=[out, out, out, SDS((2, 1, HG_DIM), F32)],
        scratch_shapes=[pltpu.VMEM((HG_HEADS, HG_HEAD_DIM, HG_HEAD_DIM), F32)],
        compiler_params=_cp("parallel", "arbitrary"))(p, p, p, p, p, p, a0, a1, seg, *masks, gp, gn, do, s0, a, e)


def _hg_post(o2, p, g, name):
    n = p.shape[0]
    tr = min(ROW_TILE, n)
    w = 2 * HG_HEAD_DIM

    def body(of_ref, ob_ref, hg_ref, g_ref, o_ref):
        for j in range(2):
            sl = slice(j * HG_HEAD_DIM, (j + 1) * HG_HEAD_DIM)
            o = of_ref[0, :, sl] + ob_ref[0, :, sl]
            hg = hg_ref[:, sl]
            o_ref[:, sl] = (o * _rstd(o) * g_ref[...] * (hg * _sigmoid(hg))).astype(o_ref.dtype)

    blk = pl.BlockSpec((tr, w), lambda i, j: (i, j))
    dirs = [pl.BlockSpec((1, tr, w), lambda i, j, d=d: (d, i, j)) for d in range(2)]
    return pl.pallas_call(
        body, name=name, grid=(n // tr, HG_DIM // w),
        in_specs=dirs + [pl.BlockSpec((tr, w), lambda i, j: (i, OFF_HG // w + j)), pl.BlockSpec((1, HG_HEAD_DIM), lambda i, j: (0, 0))],
        out_specs=blk, out_shape=SDS((n, HG_DIM), MXU_DTYPE), compiler_params=_cp("parallel", "parallel"))(o2, o2, p, g)


def _hg_post_bwd(o2, p, g, dcat, name, after=()):
    n = p.shape[0]
    tr = min(ROW_TILE, n)
    w = 2 * HG_HEAD_DIM
    after, after_specs = _unread(after)

    def body(of_ref, ob_ref, hg_ref, g_ref, d_ref, *rest):
        do_ref, dhg_ref, dg_ref = rest[len(after):]

        @pl.when(pl.program_id(1) == 0)
        def _():
            dg_ref[...] = jnp.zeros_like(dg_ref)

        for j in range(2):
            sl = slice(j * HG_HEAD_DIM, (j + 1) * HG_HEAD_DIM)
            o = of_ref[0, :, sl] + ob_ref[0, :, sl]
            hg = hg_ref[:, sl]
            d = d_ref[:, sl].astype(F32)
            sg = _sigmoid(hg)
            on = o * _rstd(o) * g_ref[...]
            dhg_ref[:, sl] = (d * on * sg * (1.0 + hg * (1.0 - sg))).astype(dhg_ref.dtype)
            dx, dg = _rms_bwd(o, g_ref[...], d * hg * sg)
            do_ref[:, sl] = dx
            dg_ref[0, :, sl] += dg

    blk = pl.BlockSpec((tr, w), lambda j, i: (i, j))
    dirs = [pl.BlockSpec((1, tr, w), lambda j, i, d=d: (d, i, j)) for d in range(2)]
    return pl.pallas_call(
        body, name=name, grid=(HG_DIM // w, n // tr),
        in_specs=dirs + [pl.BlockSpec((tr, w), lambda j, i: (i, OFF_HG // w + j)), pl.BlockSpec((1, HG_HEAD_DIM), lambda j, i: (0, 0)),
                         pl.BlockSpec((tr, w), lambda j, i: (i, ATT_Q_DIM // w + j))] + after_specs,
        out_specs=[blk, blk, pl.BlockSpec((1, 1, w), lambda j, i: (j, 0, 0))],
        out_shape=[SDS((n, HG_DIM), F32), SDS((n, HG_DIM), MXU_DTYPE), SDS((HG_DIM // w, 1, w), F32)],
        compiler_params=_cp("parallel", "arbitrary"))(o2, o2, p, g, dcat, *after)


XATT_TQ = 512


def _xattn_fwd(q, kv, name):
    n, nm = q.shape[0], kv.shape[0]
    tq = min(XATT_TQ, n)
    scale = X_HEAD_DIM ** -0.5

    def body(q_ref, k_ref, v_ref, o_ref):
        s = _dot(q_ref[...], k_ref[...], "nt") * scale
        e = jnp.exp(s - jnp.max(s, axis=-1, keepdims=True))
        o_ref[...] = _dot(e / jnp.sum(e, axis=-1, keepdims=True), v_ref[...]).astype(o_ref.dtype)

    qb = pl.BlockSpec((tq, X_HEAD_DIM), lambda h, i: (i, h))
    return pl.pallas_call(
        body, name=name, grid=(X_HEADS, n // tq),
        in_specs=[qb, pl.BlockSpec((nm, X_HEAD_DIM), lambda h, i: (0, h)), pl.BlockSpec((nm, X_HEAD_DIM), lambda h, i: (0, X_HEADS + h))],
        out_specs=qb, out_shape=SDS(q.shape, MXU_DTYPE), compiler_params=_cp("parallel", "parallel"))(q, kv, kv)


def _xattn_bwd(q, kv, do, name, after=()):
    n, nm = q.shape[0], kv.shape[0]
    tq = min(XATT_TQ, n)
    scale = X_HEAD_DIM ** -0.5
    after, after_specs = _unread(after)

    def body(q_ref, k_ref, v_ref, do_ref, *rest):
        dq_ref, dk_ref, dv_ref = rest[len(after):]

        @pl.when(pl.program_id(1) == 0)
        def _():
            dk_ref[...] = jnp.zeros_like(dk_ref)
            dv_ref[...] = jnp.zeros_like(dv_ref)

        qv, dov = q_ref[...], do_ref[...]
        s = _dot(qv, k_ref[...], "nt") * scale
        e = jnp.exp(s - jnp.max(s, axis=-1, keepdims=True))
        p = e / jnp.sum(e, axis=-1, keepdims=True)
        dp = _dot(dov, v_ref[...], "nt")
        ds = p * (dp - jnp.sum(p * dp, axis=-1, keepdims=True)) * scale
        dq_ref[...] = _dot(ds, k_ref[...]).astype(dq_ref.dtype)
        dk_ref[...] += _dot(ds, qv, "tn")
        dv_ref[...] += _dot(p, dov, "tn")

    qb = pl.BlockSpec((tq, X_HEAD_DIM), lambda h, i: (i, h))
    kb = pl.BlockSpec((nm, X_HEAD_DIM), lambda h, i: (0, h))
    return pl.pallas_call(
        body, name=name, grid=(X_HEADS, n // tq),
        in_specs=[qb, kb, pl.BlockSpec((nm, X_HEAD_DIM), lambda h, i: (0, X_HEADS + h)), qb] + after_specs, out_specs=[qb, kb, kb],
        out_shape=[SDS(q.shape, MXU_DTYPE), SDS((nm, X_HEADS * X_HEAD_DIM), F32), SDS((nm, X_HEADS * X_HEAD_DIM), F32)],
        compiler_params=_cp("parallel", "arbitrary"))(q, kv, kv, do, *after)


def _edge_rows(shape):
    row = lax.broadcasted_iota(jnp.int32, shape, 0)
    return row == 0, row == shape[0] - 1


def _shift_rows(u, down, edges):
    if down:
        return jnp.where(edges[0], 0.0, pltpu.roll(u, 1, axis=0))
    return jnp.where(edges[1], 0.0, pltpu.roll(u, u.shape[0] - 1, axis=0))


def _conv(u, w, b, edges):
    return b + _shift_rows(u, True, edges) * w[0:1, :] + u * w[1:2, :] + _shift_rows(u, False, edges) * w[2:3, :]


def _ff_specs(n):
    gate = lambda rows: pl.BlockSpec((rows, FF_COLS), lambda j: (0, j))
    val = lambda rows: pl.BlockSpec((rows, FF_COLS), lambda j: (0, FF_BLOCKS + j))
    return [gate(n), val(n), gate(3), val(3), gate(1), val(1)], gate


def _conv_gate(u, cw, cb, name):
    n = u.shape[0]
    ins, gate_blk = _ff_specs(n)

    def body(ug_ref, uv_ref, wg_ref, wv_ref, bg_ref, bv_ref, o_ref):
        edges = _edge_rows(ug_ref.shape)
        gate = _conv(ug_ref[...], wg_ref[...], bg_ref[...], edges)
        val = _conv(uv_ref[...], wv_ref[...], bv_ref[...], edges)
        o_ref[...] = (gate * _sigmoid(gate) * val).astype(o_ref.dtype)

    return pl.pallas_call(
        body, name=name, grid=(FF_BLOCKS,), in_specs=ins, out_specs=gate_blk(n), out_shape=SDS((n, D_FF), MXU_DTYPE),
        compiler_params=_cp("parallel"))(u, u, cw, cw, cb, cb)


def _conv_gate_bwd(u, cw, cb, da, name, after=()):
    n = u.shape[0]
    ins, gate_blk = _ff_specs(n)
    after, after_specs = _unread(after)

    def side(dacc, u, w, edges, du_ref, dw_ref, db_ref):
        nxt, prv = _shift_rows(dacc, False, edges), _shift_rows(dacc, True, edges)
        du_ref[...] = (nxt * w[0:1, :] + dacc * w[1:2, :] + prv * w[2:3, :]).astype(du_ref.dtype)
        db_ref[...] = jnp.sum(dacc, axis=0, keepdims=True)
        dw_ref[0:1, :] = jnp.sum(nxt * u, axis=0, keepdims=True)
        dw_ref[1:2, :] = jnp.sum(dacc * u, axis=0, keepdims=True)
        dw_ref[2:3, :] = jnp.sum(prv * u, axis=0, keepdims=True)

    def body(ug_ref, uv_ref, wg_ref, wv_ref, bg_ref, bv_ref, da_ref, *rest):
        dug_ref, duv_ref, dwg_ref, dwv_ref, dbg_ref, dbv_ref = rest[len(after):]
        ug, uv = ug_ref[...], uv_ref[...]
        edges = _edge_rows(ug.shape)
        gate = _conv(ug, wg_ref[...], bg_ref[...], edges)
        val = _conv(uv, wv_ref[...], bv_ref[...], edges)
        sg = _sigmoid(gate)
        dav = da_ref[...].astype(F32)
        side(dav * val * sg * (1.0 + gate * (1.0 - sg)), ug, wg_ref[...], edges, dug_ref, dwg_ref, dbg_ref)
        side(dav * gate * sg, uv, wv_ref[...], edges, duv_ref, dwv_ref, dbv_ref)

    return pl.pallas_call(
        body, name=name, grid=(FF_BLOCKS,), in_specs=ins + [gate_blk(n)] + after_specs,
        out_specs=[gate_blk(n), gate_blk(n), gate_blk(3), gate_blk(3), gate_blk(1), gate_blk(1)],
        out_shape=[SDS((n, D_FF), MXU_DTYPE)] * 2 + [SDS((3, D_FF), F32)] * 2 + [SDS((1, D_FF), F32)] * 2,
        compiler_params=_cp("parallel"))(u, u, cw, cw, cb, cb, da, *after)


def _adamw(w, g, m, v, name):
    r, c = w.shape[-2:]
    tr = r if r <= 512 else 256 if r % 256 == 0 else 88
    assert r % tr == 0 and (w.ndim == 2 or w.shape[:-2] == (1,)), (name, w.shape, tr)

    def body(w_ref, g_ref, m_ref, v_ref, d_ref, mo_ref, vo_ref, go_ref):
        gv = g_ref[...]
        go_ref[...] = gv
        mn = ADAM_B1 * m_ref[...] + (1.0 - ADAM_B1) * gv
        vn = ADAM_B2 * v_ref[...] + (1.0 - ADAM_B2) * gv * gv
        m_hat = mn / (1.0 - ADAM_B1 ** ADAM_STEP)
        v_hat = vn / (1.0 - ADAM_B2 ** ADAM_STEP)
        d_ref[...] = -ADAM_LR * (m_hat / (jnp.sqrt(v_hat) + ADAM_EPS) + ADAM_WD * w_ref[...])
        mo_ref[...] = mn
        vo_ref[...] = vn

    blk = pl.BlockSpec((tr, c), lambda i: (i, 0)) if w.ndim == 2 else pl.BlockSpec((1, tr, c), lambda i: (0, i, 0))
    out = SDS(w.shape, F32)
    return pl.pallas_call(body, name=name, grid=(r // tr,), in_specs=[blk] * 4, out_specs=[blk] * 4, out_shape=[out] * 4,
                          compiler_params=_cp("parallel"))(w, g, m, v)


def _half_tile(h):
    tr = h if h <= 512 else 256 if h % 256 == 0 else 176
    assert h % tr == 0, (h, tr)
    return tr


ANY = pl.BlockSpec(memory_space=pl.ANY)


def _place():
    x, y, c = lax.axis_index("x"), lax.axis_index("y"), lax.axis_index("c")
    return x, y, c, [(1 - x, y), (x, 1 - y), (1 - x, 1 - y)]


def _gather_shards(shards, name):
    nt = len(shards)

    def body(*refs):
        ins, outs = refs[:nt], refs[nt:2 * nt]
        send, recv, fsend, frecv, osend, orecv = refs[2 * nt:]
        x, y, c, chips = _place()
        me = 2 * x + y

        def half(t, chip, cc):
            h = ins[t].shape[0] // 2
            return outs[t].at[chip, pl.ds(cc * h, h)]

        def ici(t, j):
            cx, cy = chips[j]
            h = ins[t].shape[0] // 2
            return pltpu.make_async_remote_copy(src_ref=ins[t].at[pl.ds(c * h, h)], dst_ref=half(t, me, c),
                                                send_sem=send.at[t, j], recv_sem=recv.at[t, j], device_id=(cx, cy, c), device_id_type=MESH)

        def landed(t, j):
            cx, cy = chips[j]
            blk = half(t, 2 * cx + cy, c)
            return pltpu.make_async_remote_copy(src_ref=blk, dst_ref=blk, send_sem=send.at[t, j], recv_sem=recv.at[t, j],
                                                device_id=(cx, cy, c), device_id_type=MESH)

        def d2d(t, j, cc):
            cx, cy = chips[j]
            blk = half(t, 2 * cx + cy, cc)
            return pltpu.make_async_remote_copy(src_ref=blk, dst_ref=blk, send_sem=fsend.at[t, j], recv_sem=frecv.at[t, j],
                                                device_id=(x, y, 1 - c), device_id_type=MESH)

        own = [pltpu.make_async_remote_copy(src_ref=ins[t], dst_ref=outs[t].at[me], send_sem=osend.at[t], recv_sem=orecv.at[t],
                                            device_id=(x, y, 1 - c), device_id_type=MESH) for t in range(nt)]
        for t in range(nt):
            for j in range(3):
                ici(t, j).start()
        for cp in own:
            cp.start()
        for t in range(nt):
            for j in range(3):
                landed(t, j).wait_recv()
                d2d(t, j, c).start()
        for t in range(nt):
            for j in range(3):
                d2d(t, j, 1 - c).wait_recv()
        for t in range(nt):
            for j in range(3):
                ici(t, j).wait_send()
                d2d(t, j, c).wait_send()
        for cp in own:
            cp.wait()

    return pl.pallas_call(
        body, name=name, in_specs=[ANY] * nt, out_specs=[ANY] * nt,
        out_shape=[SDS((4,) + s.shape, s.dtype) for s in shards],
        scratch_shapes=[pltpu.SemaphoreType.DMA((nt, 3))] * 4 + [pltpu.SemaphoreType.DMA((nt,))] * 2,
        compiler_params=pltpu.CompilerParams(has_side_effects=True))(*shards)


def _join_halves(bufs, name):
    nt = len(bufs)

    def body(*refs):
        outs = refs[nt:2 * nt]
        send, recv = refs[2 * nt:]
        x, y, c, _ = _place()
        cps = [pltpu.make_async_remote_copy(src_ref=outs[t].at[c], dst_ref=outs[t].at[c], send_sem=send.at[t], recv_sem=recv.at[t],
                                            device_id=(x, y, 1 - c), device_id_type=MESH) for t in range(nt)]
        for cp in cps:
            cp.start()
        for t in range(nt):
            theirs = outs[t].at[1 - c]
            pltpu.make_async_remote_copy(src_ref=theirs, dst_ref=theirs, send_sem=send.at[t], recv_sem=recv.at[t],
                                         device_id=(x, y, 1 - c), device_id_type=MESH).wait_recv()
        for cp in cps:
            cp.wait_send()

    return pl.pallas_call(
        body, name=name, in_specs=[ANY] * nt, out_specs=[ANY] * nt, out_shape=[SDS(b.shape, b.dtype) for b in bufs],
        input_output_aliases={t: t for t in range(nt)},
        scratch_shapes=[pltpu.SemaphoreType.DMA((nt,))] * 2,
        compiler_params=pltpu.CompilerParams(has_side_effects=True))(*bufs)


def _exchange_small(v, reduce, name, after=()):
    rows = v.shape[0]
    after, after_specs = _unread(after)

    def body(v_ref, *rest):
        o_ref, buf, send, recv = rest[-4:]
        x, y, c, _ = _place()
        me = 4 * x + 2 * y + c
        buf[me] = v_ref[...]

        def peer(dx, dy, dc):
            return (1 - x if dx else x, 1 - y if dy else y, 1 - c if dc else c)

        peers = [(dx, dy, dc) for dx in range(2) for dy in range(2) for dc in range(2) if (dx, dy, dc) != (0, 0, 0)]
        cps = []
        for j, (dx, dy, dc) in enumerate(peers):
            cps.append(pltpu.make_async_remote_copy(src_ref=v_ref, dst_ref=buf.at[me], send_sem=send.at[j], recv_sem=recv.at[j],
                                                    device_id=peer(dx, dy, dc), device_id_type=MESH))
        for cp in cps:
            cp.start()
        for j, (dx, dy, dc) in enumerate(peers):
            px, py, pc = peer(dx, dy, dc)
            blk = buf.at[4 * px + 2 * py + pc]
            pltpu.make_async_remote_copy(src_ref=blk, dst_ref=blk, send_sem=send.at[j], recv_sem=recv.at[j],
                                         device_id=(px, py, pc), device_id_type=MESH).wait_recv()
        for cp in cps:
            cp.wait_send()
        if reduce:
            acc = buf[0]
            for j in range(1, 8):
                acc = acc + buf[j]
            o_ref[...] = acc
        else:
            o_ref[...] = buf[...]

    vm = pl.BlockSpec(memory_space=pltpu.VMEM)
    return pl.pallas_call(
        body, name=name, in_specs=[vm] + after_specs, out_specs=vm, out_shape=SDS((rows, 128) if reduce else (8, rows, 128), F32),
        scratch_shapes=[pltpu.VMEM((8, rows, 128), F32), pltpu.SemaphoreType.DMA((7,)), pltpu.SemaphoreType.DMA((7,))],
        compiler_params=pltpu.CompilerParams(has_side_effects=True))(v, *after)


HBM = pl.BlockSpec(memory_space=pltpu.HBM)
SEM = pl.BlockSpec(memory_space=pltpu.SEMAPHORE)
TOKEN = pl.BlockSpec(memory_space=pltpu.VMEM)
TOKEN_SHAPE = SDS((8, 128), F32)
PEERS = 7


def _in_hbm(a):
    return pltpu.with_memory_space_constraint(a, pltpu.HBM)


def _split_params():
    return pltpu.CompilerParams(has_side_effects=pltpu.SideEffectType.DATAFLOW_SIDE_EFFECTING)


def _gather_start(shards, name, after=()):
    nt = len(shards)
    after, after_specs = _unread(after)

    def body(*refs):
        ins, lands = refs[:nt], refs[nt:2 * nt]
        outs = refs[2 * nt + len(after):]
        sends, recvs = outs[:nt], outs[nt:2 * nt]
        x, y, c, chips = _place()
        me = 2 * x + y
        for t in range(nt):
            h = ins[t].shape[0] // 2
            mine = pl.ds(c * h, h)
            for j, (cx, cy) in enumerate(chips):
                for dc in range(2):
                    pltpu.make_async_remote_copy(src_ref=ins[t].at[mine], dst_ref=lands[t].at[me, mine], send_sem=sends[t].at[2 * j + dc],
                                                 recv_sem=recvs[t].at[2 * j + c], device_id=(cx, cy, dc), device_id_type=MESH).start()
            pltpu.make_async_remote_copy(src_ref=ins[t], dst_ref=lands[t].at[me], send_sem=sends[t].at[PEERS - 1], recv_sem=recvs[t].at[PEERS - 1],
                                         device_id=(x, y, 1 - c), device_id_type=MESH).start()
        outs[-1][...] = jnp.zeros(TOKEN_SHAPE.shape, F32)

    lands = [lax.empty((4,) + s.shape, s.dtype) for s in shards]
    out = pl.pallas_call(
        body, name=name, in_specs=[HBM] * (2 * nt) + after_specs, out_specs=[SEM] * (2 * nt) + [HBM] * (2 * nt) + [TOKEN],
        out_shape=[pltpu.SemaphoreType.DMA((PEERS,))] * (2 * nt)
        + [pltpu.HBM(s.shape, s.dtype) for s in shards] + [pltpu.HBM(l.shape, l.dtype) for l in lands] + [TOKEN_SHAPE],
        input_output_aliases={t: 2 * nt + t for t in range(2 * nt)}, compiler_params=_split_params())(
            *[_in_hbm(s) for s in shards], *[_in_hbm(l) for l in lands], *after)
    return out[:nt], out[nt:2 * nt], out[2 * nt:3 * nt], out[3 * nt:4 * nt], out[-1]


def _gather_wait(sends, recvs, shards, lands, after, name):
    nt = len(shards)

    def body(*refs):
        ins, lands_ref = refs[:nt], refs[nt:2 * nt]
        send_refs, recv_refs = refs[2 * nt:3 * nt], refs[3 * nt:4 * nt]
        x, y, c, chips = _place()
        for t in range(nt):
            h = ins[t].shape[0] // 2
            for j, (cx, cy) in enumerate(chips):
                for cs in range(2):
                    blk = lands_ref[t].at[2 * cx + cy, pl.ds(cs * h, h)]
                    pltpu.make_async_remote_copy(src_ref=blk, dst_ref=blk, send_sem=send_refs[t].at[2 * j + cs], recv_sem=recv_refs[t].at[2 * j + cs],
                                                 device_id=(cx, cy, cs), device_id_type=MESH).wait()
            blk = lands_ref[t].at[2 * x + y]
            pltpu.make_async_remote_copy(src_ref=blk, dst_ref=blk, send_sem=send_refs[t].at[PEERS - 1], recv_sem=recv_refs[t].at[PEERS - 1],
                                         device_id=(x, y, 1 - c), device_id_type=MESH).wait()

    out = pl.pallas_call(
        body, name=name, in_specs=[HBM] * (2 * nt) + [SEM] * (2 * nt) + [ANY], out_specs=[HBM] * (2 * nt),
        out_shape=[pltpu.HBM(s.shape, s.dtype) for s in shards] + [pltpu.HBM(l.shape, l.dtype) for l in lands],
        input_output_aliases={t: t for t in range(2 * nt)}, compiler_params=_split_params())(*shards, *lands, *sends, *recvs, after)
    return out[nt:]


def _scatter_start(g, name):
    _, r, c_ = g.shape
    h = r // 2

    def body(g_ref, land, send, recv, g_thru, land_thru, token):
        x, y, c, chips = _place()
        for j, (cx, cy) in enumerate(chips):
            for dc in range(2):
                pltpu.make_async_remote_copy(src_ref=g_ref.at[2 * cx + cy, pl.ds(dc * h, h)], dst_ref=land.at[2 * j + c], send_sem=send.at[2 * j + dc],
                                             recv_sem=recv.at[2 * j + c], device_id=(cx, cy, dc), device_id_type=MESH).start()
        pltpu.make_async_remote_copy(src_ref=g_ref.at[2 * x + y, pl.ds((1 - c) * h, h)], dst_ref=land.at[PEERS - 1], send_sem=send.at[PEERS - 1],
                                     recv_sem=recv.at[PEERS - 1], device_id=(x, y, 1 - c), device_id_type=MESH).start()
        token[...] = jnp.zeros(TOKEN_SHAPE.shape, F32)

    land = lax.empty((PEERS, h, c_), g.dtype)
    return pl.pallas_call(
        body, name=name, in_specs=[HBM, HBM], out_specs=[SEM, SEM, HBM, HBM, TOKEN],
        out_shape=[pltpu.SemaphoreType.DMA((PEERS,)), pltpu.SemaphoreType.DMA((PEERS,)), pltpu.HBM(g.shape, g.dtype),
                   pltpu.HBM(land.shape, land.dtype), TOKEN_SHAPE],
        input_output_aliases={0: 2, 1: 3}, compiler_params=_split_params())(_in_hbm(g), _in_hbm(land))


def _scatter_wait(started, after, name):
    nt = len(started)

    def body(*refs):
        lands = refs[nt:2 * nt]
        sends, recvs = refs[2 * nt:3 * nt], refs[3 * nt:4 * nt]
        x, y, c, chips = _place()
        peers = [(cx, cy, dc) for cx, cy in chips for dc in range(2)] + [(x, y, 1 - c)]
        for t in range(nt):
            for k, peer in enumerate(peers):
                blk = lands[t].at[k]
                pltpu.make_async_remote_copy(src_ref=blk, dst_ref=blk, send_sem=sends[t].at[k], recv_sem=recvs[t].at[k],
                                             device_id=peer, device_id_type=MESH).wait()

    gs, lands = [s[2] for s in started], [s[3] for s in started]
    after, after_specs = _unread(after)
    out = pl.pallas_call(
        body, name=name, in_specs=[HBM] * (2 * nt) + [SEM] * (2 * nt) + after_specs, out_specs=[HBM] * (2 * nt),
        out_shape=[pltpu.HBM(a.shape, a.dtype) for a in gs + lands],
        input_output_aliases={t: t for t in range(2 * nt)}, compiler_params=_split_params())(
            *gs, *lands, *[s[0] for s in started], *[s[1] for s in started], *after)
    return out[:nt], out[nt:]


def _sum_devices(g, land, me, core, name):
    npeer, h, c = land.shape
    tr = _half_tile(h)
    steps = h // tr

    def body(ix_ref, own_ref, land_ref, o_ref):
        acc = own_ref[0].astype(F32)
        for j in range(npeer):
            acc = acc + land_ref[j].astype(F32)
        o_ref[0] = acc

    grid_spec = pltpu.PrefetchScalarGridSpec(
        num_scalar_prefetch=1, grid=(steps,),
        in_specs=[pl.BlockSpec((1, tr, c), lambda i, ix: (ix[0], ix[1] * steps + i, 0)), pl.BlockSpec((npeer, tr, c), lambda i, ix: (0, i, 0))],
        out_specs=pl.BlockSpec((1, tr, c), lambda i, ix: (ix[1], i, 0)))
    return pl.pallas_call(body, name=name, grid_spec=grid_spec, out_shape=SDS((2, h, c), F32),
                          compiler_params=_cp("parallel"))(jnp.stack([me, core]), g, land)


def _pack_small(parts):
    flat = jnp.concatenate([p.reshape(-1) for p in parts])
    total = flat.shape[0]
    rows = -(-total // 1024) * 8
    return jnp.pad(flat, (0, rows * 128 - total)).reshape(rows, 128)


def _unpack_small(packed, shapes):
    flat = packed.reshape(-1)
    out, off = [], 0
    for s in shapes:
        size = int(np.prod(s))
        out.append(flat[off:off + size].reshape(s))
        off += size
    return out


def _local_step(x, mem, target, w_in, first_after, mid_weights, ffn_weights, on_grad, gains, conv_w, conv_b, hg_lb):
    n = x.shape[0]
    cos, sin = _rope_tables(n)
    seg = _hg_segments()
    gp, gn = _hg_pair_sums()
    masks = _hg_level_masks()
    gq2 = jnp.tile(gains["q_norm_g"], (1, 2))
    gk2 = jnp.tile(gains["k_norm_g"], (1, 2))
    a0 = hg_lb[:, 0:1, :]
    a1 = hg_lb[:, 1:2, :]

    p, h1 = _norm_mm(x, gains["pre_mix_g"], w_in, F32, TOKEN_TILE, 1664, "in_proj", after=(first_after,))
    qr, kr = _qk_prep(p, gq2, gk2, cos, sin, "qk_prep")
    heads = lambda a: a.reshape(n, ATT_KV_HEADS, ATT_HEAD_DIM).transpose(1, 0, 2)
    kh = heads(kr)
    vh = heads(p[:, OFF_AV:OFF_AV + ATT_KV_DIM].astype(MXU_DTYPE))
    att = _attn_fwd(qr, kh, vh, "attn_fwd")
    o2, s0, hg_a, hg_e = _hgrn_fwd(p, a0, a1, seg, masks, "hgrn_fwd")
    rec = _hg_post(o2, p, gains["hg_out_norm_g"], "hg_post")
    cat = jnp.concatenate([att, rec], axis=1)
    w_out, w_xq, w_xkv, w_xo = mid_weights(cat)
    mixed, x1 = _mm_resid_norm(cat, w_out, x, gains["post_mix_g"], 512, "out_proj_resid")
    xq, h2 = _norm_mm(x1, gains["pre_x_g"], w_xq, MXU_DTYPE, TOKEN_TILE, 1024, "xq_proj")
    kv, mn = _norm_mm(mem, gains["mem_norm_g"], w_xkv, MXU_DTYPE, 256, 2048, "xkv_proj")
    ox = _xattn_fwd(xq, kv, "xattn_fwd")
    xo, x2 = _mm_resid_norm(ox, w_xo, x1, gains["post_x_g"], 512, "xo_proj_resid")
    w_up = ffn_weights("w_up", x2)
    u, h3 = _norm_mm(x2, gains["pre_ffn_g"], w_up, F32, TOKEN_TILE, 1408, "up_proj")
    act = _conv_gate(u, conv_w, conv_b, "conv_gate")
    w_down = ffn_weights("w_down", act)
    dn, d3, loss = _mm_resid_norm(act, w_down, x2, gains["post_ffn_g"], 512, "down_proj_resid_loss", target=target)

    gs = {}
    d_act, d_dn, gs["post_ffn_g"] = _norm_bwd_mm(dn, gains["post_ffn_g"], d3, w_down, F32, 512, 1408, "ffn_post_bwd_down_dx")
    tok = on_grad("w_down", _mm(act, d_dn, "tn", WIRE_DTYPE, 1408, 1024, "down_dw"))
    du_g, du_v, dcw_g, dcw_v, dcb_g, dcb_v = _conv_gate_bwd(u, conv_w, conv_b, d_act, "conv_gate_bwd", after=(tok,))
    gs["conv_w"] = jnp.concatenate([dcw_g, dcw_v], axis=1)
    gs["conv_b"] = jnp.concatenate([dcb_g, dcb_v], axis=1)
    ff_shard = w_up.shape[2]
    g_up = _dw_by_owner(h3, du_g, ff_shard, 0, None, 512, "up_dw_gate")
    tok = on_grad("w_up", _dw_by_owner(h3, du_v, ff_shard, 2, g_up, 512, "up_dw_value"))
    d2, gs["pre_ffn_g"] = _dx_norm_bwd([(du_g, 0), (du_g, 1), (du_v, 0), (du_v, 1)], w_up, x2, gains["pre_ffn_g"], d3, 512,
                                       "up_dx_pre_bwd", after=(tok,))
    d_ox, d_xo, gs["post_x_g"] = _norm_bwd_mm(xo, gains["post_x_g"], d2, w_xo, MXU_DTYPE, 512, 1024, "x_post_bwd_xo_dx")
    tok = on_grad("w_xo", _mm(ox, d_xo, "tn", WIRE_DTYPE, 512, 1024, "xo_dw"))
    d_xq, d_k, d_v = _xattn_bwd(xq, kv, d_ox, "xattn_bwd", after=(tok,))
    d_kv = jnp.concatenate([d_k, d_v], axis=1).astype(MXU_DTYPE)
    tok = on_grad("w_xq", _mm(h2, d_xq, "tn", WIRE_DTYPE, 512, 1024, "xq_dw"))
    tok_kv = on_grad("w_xkv", _dw_by_owner(mn, d_kv, w_xkv.shape[2], 0, None, 512, "xkv_dw"))
    d1, gs["pre_x_g"] = _dx_norm_bwd([(d_xq, 0)], w_xq[None], x1, gains["pre_x_g"], d2, 512, "xq_dx_pre_bwd", after=(tok, tok_kv))
    d_mn = _mm_nt_parts([(d_kv, s) for s in range(4)], w_xkv, F32, 256, 1024, "xkv_dx")
    _, gs["mem_norm_g"] = _norm_bwd(mem, gains["mem_norm_g"], d_mn, None, MXU_DTYPE, "mem_norm_bwd")
    d_cat, d_mixed, gs["post_mix_g"] = _norm_bwd_mm(mixed, gains["post_mix_g"], d1, w_out, MXU_DTYPE, 512, 1024, "mix_post_bwd_out_dx")
    tok = on_grad("w_out", _mm(cat, d_mixed, "tn", WIRE_DTYPE, 512, 1024, "out_dw"))
    d_o, d_hg, dg_hg = _hg_post_bwd(o2, p, gains["hg_out_norm_g"], d_cat, "hg_post_bwd", after=(tok,))
    gs["hg_out_norm_g"] = dg_hg.reshape(HG_HEADS, HG_HEAD_DIM).sum(axis=0, keepdims=True)
    dhq2, dz2, dhv2, dlb = _hgrn_bwd(p, a0, a1, seg, masks, gp, gn, d_o, s0, hg_a, hg_e, "hgrn_bwd")
    lb = jax.nn.sigmoid(a0 - a1)
    da0 = dlb * lb * (1.0 - lb)
    gs["hg_lb"] = jnp.concatenate([da0, -da0], axis=1)
    d_qr, d_kh, d_vh = _attn_bwd(qr, kh, vh, cat, d_cat, "attn_bwd")
    unheads = lambda a: a.transpose(2, 0, 1).reshape(n, ATT_KV_DIM)
    d_aq, d_ak, dgq, dgk = _qk_prep_bwd(p, gq2, gk2, cos, sin, d_qr, unheads(d_kh), "qk_prep_bwd")
    gs["q_norm_g"] = dgq.reshape(ATT_HEADS, ATT_HEAD_DIM).sum(axis=0, keepdims=True)
    gs["k_norm_g"] = dgk.reshape(ATT_KV_HEADS, ATT_HEAD_DIM).sum(axis=0, keepdims=True)
    d_p = jnp.concatenate([d_aq, d_ak, unheads(d_vh).astype(MXU_DTYPE), (dhq2[0] + dhq2[1]).astype(MXU_DTYPE),
                           dz2[0].astype(MXU_DTYPE), dz2[1].astype(MXU_DTYPE), (dhv2[0] + dhv2[1]).astype(MXU_DTYPE), d_hg], axis=1)
    tok = on_grad("w_in", _mm(h1, d_p, "tn", WIRE_DTYPE, 512, 1664, "in_dw"))
    grad_x, gs["pre_mix_g"] = _dx_norm_bwd([(d_p, 0)], w_in[None], x, gains["pre_mix_g"], d1, 512, "in_dx_pre_bwd", after=(tok,))
    return loss, grad_x, gs


MATS = ("w_in", "w_out", "w_xq", "w_xkv", "w_xo", "w_up", "w_down")
GAINS = ("pre_mix_g", "q_norm_g", "k_norm_g", "hg_out_norm_g", "post_mix_g", "pre_x_g", "mem_norm_g", "post_x_g", "pre_ffn_g", "post_ffn_g")
WEIGHTS = ('pre_mix_g', 'w_in', 'q_norm_g', 'k_norm_g', 'hg_lb', 'hg_out_norm_g', 'w_out', 'post_mix_g', 'pre_x_g', 'mem_norm_g', 'w_xq',
           'w_xkv', 'w_xo', 'post_x_g', 'pre_ffn_g', 'w_up', 'conv_w', 'conv_b', 'w_down', 'post_ffn_g')


def kernel(x, mem, pre_mix_g, w_in, q_norm_g, k_norm_g, hg_lb, hg_out_norm_g, w_out, post_mix_g, pre_x_g, mem_norm_g, w_xq, w_xkv, w_xo, post_x_g, pre_ffn_g, w_up, conv_w, conv_b, w_down, post_ffn_g, loss_target, m_pre_mix_g, m_w_in, m_q_norm_g, m_k_norm_g, m_hg_lb, m_hg_out_norm_g, m_w_out, m_post_mix_g, m_pre_x_g, m_mem_norm_g, m_w_xq, m_w_xkv, m_w_xo, m_post_x_g, m_pre_ffn_g, m_w_up, m_conv_w, m_conv_b, m_w_down, m_post_ffn_g, v_pre_mix_g, v_w_in, v_q_norm_g, v_k_norm_g, v_hg_lb, v_hg_out_norm_g, v_w_out, v_post_mix_g, v_pre_x_g, v_mem_norm_g, v_w_xq, v_w_xkv, v_w_xo, v_post_x_g, v_pre_ffn_g, v_w_up, v_conv_w, v_conv_b, v_w_down, v_post_ffn_g):
    args = dict(locals())
    w = {k: args[k] for k in WEIGHTS}
    m = {k: args["m_" + k] for k in WEIGHTS}
    v = {k: args["v_" + k] for k in WEIGHTS}
    chip = 2 * lax.axis_index("x") + lax.axis_index("y")
    core = lax.axis_index("c")

    shards = {k: w[k][0].astype(WIRE_DTYPE) for k in MATS}

    def whole(k, g):
        return g if k in ("w_xkv", "w_up") else g.reshape(-1, g.shape[-1])

    w_in_shards = _gather_shards([shards["w_in"]], "gather_w_in")[0]
    w_in_full = jnp.concatenate([w_in_shards[s] for s in range(4)], axis=1)
    small_in = _exchange_small(_pack_small([w["conv_w"][0], w["hg_lb"]]), False, "gather_small")
    mid_names, ffn_names = ("w_out", "w_xq", "w_xkv", "w_xo"), ("w_up", "w_down")
    mid = _gather_start([shards[k] for k in mid_names], "gather_mid_start", after=(w_in_full, small_in))
    ffn = _gather_start([shards[k] for k in ffn_names], "gather_ffn_start", after=(mid[4],))

    def mid_weights(after):
        return [whole(k, g) for k, g in zip(mid_names, _gather_wait(*mid[:4], after, "gather_mid_wait"))]

    def ffn_weights(k, after):
        t = ffn_names.index(k)
        return whole(k, _gather_wait(*[part[t:t + 1] for part in ffn[:4]], after, "gather_wait_" + k)[0])

    cw_parts, lb_parts = [], []
    for s in range(4):
        cw_s, lb_s = _unpack_small(small_in[2 * s], [w["conv_w"][0].shape, w["hg_lb"].shape])
        cw_parts.append(cw_s)
        lb_parts.append(lb_s)
    conv_w_full = jnp.concatenate(cw_parts, axis=1)
    hg_lb_full = jnp.concatenate(lb_parts, axis=2)

    started = {}

    def on_grad(k, g):
        if k == "w_in":
            g = g.reshape(g.shape[0], 4, g.shape[1] // 4).transpose(1, 0, 2)
        elif g.ndim == 2:
            g = g.reshape(4, g.shape[0] // 4, g.shape[1])
        *started[k], token = _scatter_start(g, "grad_start_" + k)
        return token

    gains = {k: w[k] for k in GAINS}
    loss_part, grad_x, gs = _local_step(x[0], mem[0], loss_target[0], w_in_full, ffn[4], mid_weights, ffn_weights, on_grad, gains,
                                        conv_w_full, w["conv_b"], hg_lb_full)
    gs["loss"] = loss_part

    grads, delta, new_m, new_v = {}, {}, {}, {}

    def reduce_matrices(names, after, tag):
        sent, landed = _scatter_wait([started[k] for k in names], after, "grad_wait_" + tag)
        halves = [_sum_devices(g, land, chip, core, "grad_sum_" + k) for k, g, land in zip(names, sent, landed)]
        for k, r in zip(names, _join_halves(halves, "grad_join_" + tag)):
            grads[k] = r.reshape(1, -1, r.shape[-1])

    def adamw(names):
        for k in names:
            shape = w[k].shape
            keep = len(shape) == 3 and shape[0] == 1
            two_d = lambda a: a.reshape(shape) if keep else a.reshape(-1, shape[-1])
            d, mo, vo, go = _adamw(two_d(w[k]), two_d(grads[k]), two_d(m[k]), two_d(v[k]), "adamw_" + k)
            delta[k], new_m[k], new_v[k], grads[k] = d.reshape(shape), mo.reshape(shape), vo.reshape(shape), go.reshape(shape)

    early = tuple(k for k in MATS if k != "w_in")
    reduce_matrices(early, (grad_x,), "early")
    adamw(early)

    small_names = GAINS + ("conv_b", "conv_w", "hg_lb")
    packed = _pack_small([gs[k] for k in small_names + ("loss",)])
    reduced_small = _exchange_small(packed, True, "reduce_small", after=tuple(new_v[k] for k in early))
    *summed, loss = _unpack_small(reduced_small, [gs[k].shape for k in small_names + ("loss",)])
    loss = loss[0, 0]
    for k, g in zip(small_names, summed):
        grads[k] = g
    ncw = w["conv_w"].shape[2]
    grads["conv_w"] = lax.dynamic_slice_in_dim(grads["conv_w"], chip * ncw, ncw, axis=1)[None]
    nlb = w["hg_lb"].shape[2]
    grads["hg_lb"] = lax.dynamic_slice_in_dim(grads["hg_lb"], chip * nlb, nlb, axis=2)
    replicated = GAINS + ("conv_b",)
    shapes = [w[k].shape for k in replicated]
    rows = sum(int(np.prod(s)) for s in shapes) // 128
    pack = lambda d: jnp.concatenate([d[k].reshape(-1) for k in replicated]).reshape(rows, 128)
    outs = _adamw(pack(w), reduced_small[:rows], pack(m), pack(v), "adamw_replicated")
    for into, packed_out in zip((delta, new_m, new_v, grads), outs):
        for k, a in zip(replicated, _unpack_small(packed_out, shapes)):
            into[k] = a
    adamw(("conv_w", "hg_lb"))

    reduce_matrices(("w_in",), tuple(new_v[k] for k in early + small_names), "late")
    adamw(("w_in",))
    return (loss, grad_x[None], *[grads[k] for k in WEIGHTS], *[delta[k] for k in WEIGHTS],
            *[new_m[k] for k in WEIGHTS], *[new_v[k] for k in WEIGHTS])
```

```python
import numpy as np
import jax
import jax.numpy as jnp
from jax import lax
from jax.experimental import pallas as pl
from jax.experimental.pallas import tpu as pltpu

F32 = jnp.float32
MXU_DTYPE = jnp.bfloat16
WIRE_DTYPE = jnp.bfloat16
VMEM_LIMIT_BYTES = 56 * 1024 * 1024
ROWS_PER_16BIT_TILE = 16
ELEMENTWISE_ROWS = 256
EPS = 1e-6
MESH = pl.DeviceIdType.MESH

GRID_W = 64
ATT_HEADS, ATT_KV_HEADS, ATT_HEAD_DIM = 8, 2, 64
ATT_GROUP = ATT_HEADS // ATT_KV_HEADS
ATT_Q_DIM, ATT_KV_DIM = 512, 128
ROPE_THETA = 10000.0
HG_HEADS, HG_HEAD_DIM, HG_DIM = 4, 128, 512
HG_CHUNK = 128
HG_LEVELS = 7
HG_PAIR = 2 * HG_HEAD_DIM
X_HEADS, X_HEAD_DIM = 4, 256
D_FF = 2816
FF_COLS = 256
FF_BLOCKS = D_FF // FF_COLS
OFF_AK, OFF_AV, OFF_HQ, OFF_ZF, OFF_ZB, OFF_HI, OFF_HG = 512, 640, 768, 1280, 1792, 2304, 2816

ADAM_LR, ADAM_B1, ADAM_B2, ADAM_EPS, ADAM_WD, ADAM_STEP = 0.001, 0.9, 0.999, 1e-08, 0.01, 10

SDS = jax.ShapeDtypeStruct


def _cp(*sem):
    return pltpu.CompilerParams(dimension_semantics=sem, vmem_limit_bytes=VMEM_LIMIT_BYTES)


def _row_tile(rows, cap):
    if rows <= cap:
        return rows
    return max(t for t in range(ROWS_PER_16BIT_TILE, cap + 1, ROWS_PER_16BIT_TILE) if rows % t == 0)


def _dot(a, b, form="nn"):
    dims = {"nn": (((1,), (0,)), ((), ())), "nt": (((1,), (1,)), ((), ())), "tn": (((0,), (0,)), ((), ()))}[form]
    return lax.dot_general(a.astype(MXU_DTYPE), b.astype(MXU_DTYPE), dims, preferred_element_type=F32)


def _sigmoid(x):
    return 1.0 / (1.0 + jnp.exp(-x))


def _rstd(x):
    return lax.rsqrt(jnp.mean(x * x, axis=-1, keepdims=True) + EPS)


def _rms_bwd(x, g, dy):
    r = _rstd(x)
    xh = x * r
    dn = dy * g
    dx = r * (dn - xh * jnp.mean(dn * xh, axis=-1, keepdims=True))
    return dx, jnp.sum(dy * xh, axis=0, keepdims=True)


def _unread(after):
    after = tuple(a for a in after if a is not None)
    return after, [pl.BlockSpec(memory_space=pl.ANY)] * len(after)


def _mm(a, b, form, out_dtype, tm, tn, name, after=()):
    after, after_specs = _unread(after)
    if form == "nn":
        (m, k), n = a.shape, b.shape[1]
    elif form == "nt":
        (m, k), n = a.shape, b.shape[0]
    else:
        (k, m), n = a.shape, b.shape[1]
    tm, tn = min(tm, m), min(tn, n)
    assert m % tm == 0 and n % tn == 0, (name, m, n, tm, tn)

    def body(a_ref, b_ref, *rest):
        o_ref = rest[-1]
        o_ref[...] = _dot(a_ref[...], b_ref[...], form).astype(o_ref.dtype)

    a_spec = pl.BlockSpec((k, tm), lambda i, j: (0, i)) if form == "tn" else pl.BlockSpec((tm, k), lambda i, j: (i, 0))
    b_spec = pl.BlockSpec((tn, k), lambda i, j: (j, 0)) if form == "nt" else pl.BlockSpec((k, tn), lambda i, j: (0, j))
    return pl.pallas_call(
        body, name=name, grid=(m // tm, n // tn), in_specs=[a_spec, b_spec] + after_specs,
        out_specs=pl.BlockSpec((tm, tn), lambda i, j: (i, j)), out_shape=SDS((m, n), out_dtype),
        compiler_params=_cp("parallel", "parallel"))(a, b, *after)


def _mm_nt_parts(a_parts, b, out_dtype, tm, tn, name, after=()):
    after, after_specs = _unread(after)
    parts, n, p = b.shape
    m = a_parts[0][0].shape[0]
    tm, tn = min(tm, m), min(tn, n)
    assert m % tm == 0 and n % tn == 0 and len(a_parts) == parts, (name, m, b.shape)

    def body(*refs):
        o_ref = refs[-1]
        acc = _dot(refs[0][...], refs[parts][0], "nt")
        for s in range(1, parts):
            acc = acc + _dot(refs[s][...], refs[parts + s][0], "nt")
        o_ref[...] = acc.astype(o_ref.dtype)

    a_specs = [pl.BlockSpec((tm, p), lambda i, j, cb=cb: (i, cb)) for _, cb in a_parts]
    b_specs = [pl.BlockSpec((1, tn, p), lambda i, j, s=s: (s, j, 0)) for s in range(parts)]
    return pl.pallas_call(
        body, name=name, grid=(m // tm, n // tn), in_specs=a_specs + b_specs + after_specs,
        out_specs=pl.BlockSpec((tm, tn), lambda i, j: (i, j)), out_shape=SDS((m, n), out_dtype),
        compiler_params=_cp("parallel", "parallel"))(*[arr for arr, _ in a_parts], *([b] * parts), *after)


def _norm_bwd_mm(y, g, d, w, out_dtype, tm, tn, name):
    n, dm = y.shape
    nn = w.shape[0]
    tm, tn = min(tm, n), min(tn, nn)
    assert n % tm == 0 and nn % tn == 0 and w.shape[1] == dm, (name, y.shape, w.shape)

    def body(y_ref, g_ref, d_ref, w_ref, dx_ref, dy_ref, dg_ref, dys):
        i, j = pl.program_id(0), pl.program_id(1)

        @pl.when(jnp.logical_and(i == 0, j == 0))
        def _():
            dg_ref[...] = jnp.zeros_like(dg_ref)

        @pl.when(j == 0)
        def _():
            dy, dg = _rms_bwd(y_ref[...], g_ref[...], d_ref[...])
            dy = dy.astype(MXU_DTYPE)
            dys[...] = dy
            dy_ref[...] = dy
            dg_ref[...] += dg

        dx_ref[...] = _dot(dys[...], w_ref[...], "nt").astype(dx_ref.dtype)

    row = pl.BlockSpec((tm, dm), lambda i, j: (i, 0))
    vec = pl.BlockSpec((1, dm), lambda i, j: (0, 0))
    return pl.pallas_call(
        body, name=name, grid=(n // tm, nn // tn), in_specs=[row, vec, row, pl.BlockSpec((tn, dm), lambda i, j: (j, 0))],
        out_specs=[pl.BlockSpec((tm, tn), lambda i, j: (i, j)), row, vec],
        out_shape=[SDS((n, nn), out_dtype), SDS((n, dm), MXU_DTYPE), SDS((1, dm), F32)],
        scratch_shapes=[pltpu.VMEM((tm, dm), MXU_DTYPE)],
        compiler_params=_cp("arbitrary", "arbitrary"))(y, g, d, w)


def _mm_resid_norm(a, b, x, g, tm, name, target=None):
    n, k = a.shape
    d = b.shape[1]
    tm = min(tm, n)
    assert n % tm == 0 and x.shape == (n, d), (name, a.shape, b.shape)
    with_loss = target is not None

    def body(a_ref, b_ref, x_ref, g_ref, *rest):
        y = _dot(a_ref[...], b_ref[...])
        out = x_ref[...] + y * _rstd(y) * g_ref[...]
        if not with_loss:
            y_ref, o_ref = rest
            y_ref[...] = y
            o_ref[...] = out
            return
        t_ref, y_ref, d_ref, l_ref = rest
        y_ref[...] = y
        diff = out - t_ref[...]
        d_ref[...] = diff * (1.0 / d)

        @pl.when(pl.program_id(0) == 0)
        def _():
            l_ref[...] = jnp.zeros_like(l_ref)

        l_ref[...] += 0.5 * jnp.sum(jnp.mean(diff * diff, axis=-1, keepdims=True), axis=0, keepdims=True)

    row = pl.BlockSpec((tm, d), lambda i: (i, 0))
    ins = [pl.BlockSpec((tm, k), lambda i: (i, 0)), pl.BlockSpec((k, d), lambda i: (0, 0)), row, pl.BlockSpec((1, d), lambda i: (0, 0))]
    out = SDS((n, d), F32)
    if with_loss:
        return pl.pallas_call(body, name=name, grid=(n // tm,), in_specs=ins + [row], out_specs=[row, row, pl.BlockSpec((1, 1), lambda i: (0, 0))],
                              out_shape=[out, out, SDS((1, 1), F32)], compiler_params=_cp("arbitrary"))(a, b, x, g, target)
    return pl.pallas_call(body, name=name, grid=(n // tm,), in_specs=ins, out_specs=[row, row], out_shape=[out, out],
                          compiler_params=_cp("parallel"))(a, b, x, g)


def _dx_norm_bwd(a_parts, b, x, g, res, tm, name, after=()):
    after, after_specs = _unread(after)
    parts, d, p = b.shape
    n = x.shape[0]
    tm = min(tm, n)
    assert n % tm == 0 and len(a_parts) == parts and x.shape[1] == d, (name, x.shape, b.shape)

    def body(*refs):
        x_ref, g_ref, res_ref = refs[2 * parts:2 * parts + 3]
        dx_ref, dg_ref = refs[-2:]
        dh = _dot(refs[0][...], refs[parts][0], "nt")
        for s in range(1, parts):
            dh = dh + _dot(refs[s][...], refs[parts + s][0], "nt")
        dx, dg = _rms_bwd(x_ref[...], g_ref[...], dh)
        dx_ref[...] = dx + res_ref[...]

        @pl.when(pl.program_id(0) == 0)
        def _():
            dg_ref[...] = jnp.zeros_like(dg_ref)

        dg_ref[...] += dg

    a_specs = [pl.BlockSpec((tm, p), lambda i, cb=cb: (i, cb)) for _, cb in a_parts]
    b_specs = [pl.BlockSpec((1, d, p), lambda i, s=s: (s, 0, 0)) for s in range(parts)]
    row = pl.BlockSpec((tm, d), lambda i: (i, 0))
    vec = pl.BlockSpec((1, d), lambda i: (0, 0))
    return pl.pallas_call(
        body, name=name, grid=(n // tm,), in_specs=a_specs + b_specs + [row, vec, row] + after_specs,
        out_specs=[row, vec], out_shape=[SDS((n, d), F32), SDS((1, d), F32)],
        compiler_params=_cp("arbitrary"))(*[arr for arr, _ in a_parts], *([b] * parts), x, g, res, *after)


def _dw_by_owner(a, b, tn, first, into, tm, name):
    k, m = a.shape
    cnt = b.shape[1] // tn
    tm = min(tm, m)
    assert m % tm == 0 and b.shape[1] == cnt * tn and first + cnt <= 4, (name, a.shape, b.shape)

    def body(a_ref, b_ref, *rest):
        rest[-1][0] = _dot(a_ref[...], b_ref[...], "tn").astype(rest[-1].dtype)

    extra = [] if into is None else [into]
    return pl.pallas_call(
        body, name=name, grid=(m // tm, cnt),
        in_specs=[pl.BlockSpec((k, tm), lambda i, j: (0, i)), pl.BlockSpec((k, tn), lambda i, j: (0, j))] + [pl.BlockSpec(memory_space=pl.ANY)] * len(extra),
        out_specs=pl.BlockSpec((1, tm, tn), lambda i, j: (first + j, i, 0)), out_shape=SDS((4, m, tn), WIRE_DTYPE),
        input_output_aliases={2: 0} if extra else {},
        compiler_params=_cp("parallel", "parallel"))(a, b, *extra)


def _norm_mm(x, g, w, out_dtype, tm, tn, name, after=()):
    after, after_specs = _unread(after)
    m, d = x.shape
    sharded = w.ndim == 3
    n = w.shape[-1] * (w.shape[0] if sharded else 1)
    tm, tn = min(tm, m), (w.shape[-1] if sharded else min(tn, n))
    assert m % tm == 0 and n % tn == 0, (name, m, n, tm, tn)

    def body(x_ref, g_ref, w_ref, *rest):
        o_ref, h_ref, hs = rest[-3:]

        @pl.when(pl.program_id(1) == 0)
        def _():
            xv = x_ref[...]
            h = (xv * _rstd(xv) * g_ref[...]).astype(MXU_DTYPE)
            hs[...] = h
            h_ref[...] = h

        o_ref[...] = _dot(hs[...], w_ref[0] if sharded else w_ref[...]).astype(o_ref.dtype)

    w_spec = pl.BlockSpec((1, d, tn), lambda i, j: (j, 0, 0)) if sharded else pl.BlockSpec((d, tn), lambda i, j: (0, j))
    return pl.pallas_call(
        body, name=name, grid=(m // tm, n // tn),
        in_specs=[pl.BlockSpec((tm, d), lambda i, j: (i, 0)), pl.BlockSpec((1, d), lambda i, j: (0, 0)), w_spec] + after_specs,
        out_specs=[pl.BlockSpec((tm, tn), lambda i, j: (i, j)), pl.BlockSpec((tm, d), lambda i, j: (i, 0))],
        out_shape=[SDS((m, n), out_dtype), SDS((m, d), MXU_DTYPE)],
        scratch_shapes=[pltpu.VMEM((tm, d), MXU_DTYPE)],
        compiler_params=_cp("parallel", "arbitrary"))(x, g, w, *after)


ROW_TILE = 512
TOKEN_TILE = 1024


def _norm_bwd(x, g, dy, res, out_dtype, name):
    n, d = x.shape
    tr = min(ROW_TILE, n)
    has_res = res is not None

    def body(*refs):
        x_ref, g_ref, dy_ref = refs[:3]
        dx_ref, dg_ref = refs[-2:]
        dx, dg = _rms_bwd(x_ref[...], g_ref[...], dy_ref[...].astype(F32))
        if has_res:
            dx = dx + refs[3][...]
        dx_ref[...] = dx.astype(dx_ref.dtype)

        @pl.when(pl.program_id(0) == 0)
        def _():
            dg_ref[...] = jnp.zeros_like(dg_ref)

        dg_ref[...] += dg

    row = pl.BlockSpec((tr, d), lambda i: (i, 0))
    vec = pl.BlockSpec((1, d), lambda i: (0, 0))
    ins = [x, g, dy] + ([res] if has_res else [])
    return pl.pallas_call(
        body, name=name, grid=(n // tr,), in_specs=[row, vec, row] + ([row] if has_res else []),
        out_specs=[row, vec], out_shape=[SDS((n, d), out_dtype), SDS((1, d), F32)],
        compiler_params=_cp("arbitrary"))(*ins)


def _rope_tables(n):
    pairs = ATT_HEAD_DIM // 4
    t = np.arange(n)
    inv = np.power(ROPE_THETA, -np.arange(pairs, dtype=np.float32) / pairs).astype(np.float32)
    ang = np.concatenate([(t // GRID_W)[:, None].astype(np.float32) * inv, (t % GRID_W)[:, None].astype(np.float32) * inv], axis=-1)
    cos = np.repeat(np.cos(ang), 2, axis=-1)
    sin = np.repeat(np.sin(ang), 2, axis=-1) * np.tile(np.array([-1.0, 1.0], np.float32), ATT_HEAD_DIM // 2)
    return jnp.asarray(np.tile(cos, 2), F32), jnp.asarray(np.tile(sin, 2), F32)


def _swap_pairs(x):
    lane = lax.broadcasted_iota(jnp.int32, x.shape, 1)
    return jnp.where((lane & 1) == 0, pltpu.roll(x, 127, axis=1), pltpu.roll(x, 1, axis=1))


def _head_mean(v):
    lane = lax.broadcasted_iota(jnp.int32, v.shape, 1)
    lo = jnp.where(lane < ATT_HEAD_DIM, v, 0.0)
    s0 = jnp.sum(lo, axis=-1, keepdims=True)
    s1 = jnp.sum(v - lo, axis=-1, keepdims=True)
    return jnp.where(lane < ATT_HEAD_DIM, s0, s1) * (1.0 / ATT_HEAD_DIM)


def _qk_prep(p, gq, gk, cos, sin, name):
    n = p.shape[0]
    tr = min(ROW_TILE, n)

    def one(xv, g, c, s):
        xn = xv * lax.rsqrt(_head_mean(xv * xv) + EPS) * g
        return xn * c + _swap_pairs(xn) * s

    def body(q_ref, k_ref, gq_ref, gk_ref, c_ref, s_ref, qo_ref, ko_ref):
        c, s = c_ref[...], s_ref[...]
        for j in range(ATT_Q_DIM // 128):
            qo_ref[:, j * 128:(j + 1) * 128] = one(q_ref[:, j * 128:(j + 1) * 128], gq_ref[...], c, s).astype(qo_ref.dtype)
        ko_ref[...] = one(k_ref[...], gk_ref[...], c, s).astype(ko_ref.dtype)

    vec = pl.BlockSpec((1, 128), lambda i: (0, 0))
    tab = pl.BlockSpec((tr, 128), lambda i: (i, 0))
    return pl.pallas_call(
        body, name=name, grid=(n // tr,),
        in_specs=[pl.BlockSpec((tr, ATT_Q_DIM), lambda i: (i, 0)), pl.BlockSpec((tr, 128), lambda i: (i, OFF_AK // 128)), vec, vec, tab, tab],
        out_specs=[pl.BlockSpec((tr, ATT_Q_DIM), lambda i: (i, 0)), tab],
        out_shape=[SDS((n, ATT_Q_DIM), MXU_DTYPE), SDS((n, ATT_KV_DIM), MXU_DTYPE)],
        compiler_params=_cp("parallel"))(p, p, gq, gk, cos, sin)


def _qk_prep_bwd(p, gq, gk, cos, sin, dq, dk, name):
    n = p.shape[0]
    tr = min(ROW_TILE, n)

    def one(xv, g, c, s, dout):
        dxn = dout * c + _swap_pairs(dout * s)
        r = lax.rsqrt(_head_mean(xv * xv) + EPS)
        xh = xv * r
        dn = dxn * g
        dx = r * (dn - xh * _head_mean(dn * xh))
        return dx, jnp.sum(dxn * xh, axis=0, keepdims=True)

    def body(q_ref, k_ref, gq_ref, gk_ref, c_ref, s_ref, dq_ref, dk_ref, dqo_ref, dko_ref, dgq_ref, dgk_ref):
        @pl.when(pl.program_id(0) == 0)
        def _():
            dgq_ref[...] = jnp.zeros_like(dgq_ref)
            dgk_ref[...] = jnp.zeros_like(dgk_ref)

        c, s = c_ref[...], s_ref[...]
        for j in range(ATT_Q_DIM // 128):
            sl = slice(j * 128, (j + 1) * 128)
            dx, dg = one(q_ref[:, sl], gq_ref[...], c, s, dq_ref[:, sl])
            dqo_ref[:, sl] = dx.astype(dqo_ref.dtype)
            dgq_ref[:, sl] += dg
        dx, dg = one(k_ref[...], gk_ref[...], c, s, dk_ref[...])
        dko_ref[...] = dx.astype(dko_ref.dtype)
        dgk_ref[...] += dg

    vec = pl.BlockSpec((1, 128), lambda i: (0, 0))
    tab = pl.BlockSpec((tr, 128), lambda i: (i, 0))
    qrow = pl.BlockSpec((tr, ATT_Q_DIM), lambda i: (i, 0))
    return pl.pallas_call(
        body, name=name, grid=(n // tr,),
        in_specs=[qrow, pl.BlockSpec((tr, 128), lambda i: (i, OFF_AK // 128)), vec, vec, tab, tab, qrow, tab],
        out_specs=[qrow, tab, pl.BlockSpec((1, ATT_Q_DIM), lambda i: (0, 0)), vec],
        out_shape=[SDS((n, ATT_Q_DIM), MXU_DTYPE), SDS((n, ATT_KV_DIM), MXU_DTYPE), SDS((1, ATT_Q_DIM), F32), SDS((1, 128), F32)],
        compiler_params=_cp("arbitrary"))(p, p, gq, gk, cos, sin, dq, dk)


ATT_TQ = 256


def _attn_fwd(q, k, v, name):
    n = q.shape[0]
    tq = min(ATT_TQ, n)
    scale = ATT_HEAD_DIM ** -0.5
    gw = ATT_GROUP * ATT_HEAD_DIM

    def body(q_ref, k_ref, v_ref, o_ref):
        kk, vv = k_ref[0], v_ref[0]
        v_ones = jnp.concatenate([vv, jnp.ones_like(vv)], axis=1)
        outs = []
        for g in range(ATT_GROUP):
            s = _dot(q_ref[:, g * ATT_HEAD_DIM:(g + 1) * ATT_HEAD_DIM] * scale, kk, "nt")
            e = jnp.exp(s - jnp.max(s, axis=-1, keepdims=True))
            ov = _dot(e, v_ones)
            outs.append(ov[:, :ATT_HEAD_DIM] / ov[:, ATT_HEAD_DIM:])
        o_ref[...] = jnp.concatenate(outs, axis=-1).astype(o_ref.dtype)

    kv = pl.BlockSpec((1, n, ATT_HEAD_DIM), lambda h, i: (h, 0, 0))
    return pl.pallas_call(
        body, name=name, grid=(ATT_KV_HEADS, n // tq),
        in_specs=[pl.BlockSpec((tq, gw), lambda h, i: (i, h)), kv, kv],
        out_specs=pl.BlockSpec((tq, gw), lambda h, i: (i, h)), out_shape=SDS((n, ATT_Q_DIM), MXU_DTYPE),
        compiler_params=_cp("parallel", "parallel"))(q, k, v)


def _attn_bwd(q, k, v, o, do, name):
    n = q.shape[0]
    tq = min(ATT_TQ, n)
    scale = ATT_HEAD_DIM ** -0.5
    gw = ATT_GROUP * ATT_HEAD_DIM

    def body(q_ref, k_ref, v_ref, o_ref, do_ref, dq_ref, dk_ref, dv_ref):
        @pl.when(pl.program_id(1) == 0)
        def _():
            dk_ref[...] = jnp.zeros_like(dk_ref)
            dv_ref[...] = jnp.zeros_like(dv_ref)

        kk, vv = k_ref[0], v_ref[0]
        dqs = []
        dk_acc = jnp.zeros((ATT_HEAD_DIM, n), F32)
        dv_acc = jnp.zeros((ATT_HEAD_DIM, n), F32)
        for g in range(ATT_GROUP):
            sl = slice(g * ATT_HEAD_DIM, (g + 1) * ATT_HEAD_DIM)
            qg, dog = q_ref[:, sl] * scale, do_ref[:, sl].astype(F32)
            s = _dot(qg, kk, "nt")
            e = jnp.exp(s - jnp.max(s, axis=-1, keepdims=True))
            inv = 1.0 / jnp.sum(e, axis=-1, keepdims=True)
            delta = jnp.sum(dog * o_ref[:, sl].astype(F32), axis=-1, keepdims=True)
            dse = e * (_dot(dog, vv, "nt") - delta)
            dqs.append(_dot(dse, kk) * (inv * scale))
            dk_acc += _dot(qg.astype(F32) * inv, dse, "tn")
            dv_acc += _dot(dog * inv, e, "tn")
        dq_ref[...] = jnp.concatenate(dqs, axis=-1)
        dk_ref[0] += dk_acc
        dv_ref[0] += dv_acc

    kv = pl.BlockSpec((1, n, ATT_HEAD_DIM), lambda h, i: (h, 0, 0))
    kvt = pl.BlockSpec((1, ATT_HEAD_DIM, n), lambda h, i: (h, 0, 0))
    qb = pl.BlockSpec((tq, gw), lambda h, i: (i, h))
    return pl.pallas_call(
        body, name=name, grid=(ATT_KV_HEADS, n // tq), in_specs=[qb, kv, kv, qb, qb], out_specs=[qb, kvt, kvt],
        out_shape=[SDS((n, ATT_Q_DIM), F32), SDS((ATT_KV_HEADS, ATT_HEAD_DIM, n), F32), SDS((ATT_KV_HEADS, ATT_HEAD_DIM, n), F32)],
        compiler_params=_cp("parallel", "arbitrary"))(q, k, v, o, do)


def _both_directions(mats, axis):
    fwd = np.concatenate(mats, axis=axis).astype(np.float32)
    bwd = np.concatenate([m[::-1, ::-1] for m in mats], axis=axis).astype(np.float32)
    return jnp.asarray(np.stack([fwd, bwd]), MXU_DTYPE)


def _hg_segments():
    c = HG_CHUNK
    t = np.arange(c)[:, None]
    r = np.arange(c)[None, :]
    mats = [(r <= t)]
    for lev in range(HG_LEVELS):
        h = c >> (lev + 1)
        mid = (t // (2 * h)) * (2 * h) + h - 1
        hi = (t // h) % 2 == 1
        mats.append(np.where(hi, (r > mid) & (r <= t), (r > t) & (r <= mid)))
    mats.append(r > t)
    return _both_directions(mats, 0)


def _hg_pair_sums():
    c = HG_CHUNK
    r = np.arange(c)[:, None]
    t = np.arange(c)[None, :]
    gp, gn = [t >= r], [t < r]
    for lev in range(HG_LEVELS):
        sh = HG_LEVELS - 1 - lev
        same = (r >> sh) == (t >> sh)
        gp.append(same & (t >= r))
        gn.append(same & (t < r))
    return _both_directions(gp, 1), _both_directions(gn, 1)


def _split_dot(mat, x):
    hi = x.astype(MXU_DTYPE)
    lo = (x - hi.astype(F32)).astype(MXU_DTYPE)
    return _dot(mat, hi) + _dot(mat, lo)


def _hg_gates(hq, z, a0, a1):
    q = hq * _sigmoid(hq)
    sg = _sigmoid(z)
    lb = _sigmoid(a0 - a1)
    f = lb + (1.0 - lb) * sg
    k = (1.0 - lb) * (1.0 - sg)
    return q, f, k, sg, lb


def _hg_level_masks():
    c = HG_CHUNK
    t = np.arange(c)
    later, same = [], []
    for lev in range(HG_LEVELS):
        sh = HG_LEVELS - 1 - lev
        later.append(np.broadcast_to((((t >> sh) & 1) == 1)[:, None], (c, HG_HEAD_DIM)))
        same.append((t[:, None] >> (sh + 1)) == (t[None, :] >> (sh + 1)))
    same.append(t[:, None] == t[None, :])
    later = np.stack(later).astype(np.float32)
    return jnp.asarray(np.stack([later, 1.0 - later]), F32), jnp.asarray(np.stack(same).astype(np.float32), F32)


def _hg_level(q, k, ex, later_ref, lev):
    e = ex[lev + 1]
    e_q = e * later_ref[0, lev]
    e_k = e - e_q
    return q * e_q, k * e_k, e_q, e_k


def _hg_intra(q, k, ex, later_ref, same_ref):
    a = same_ref[HG_LEVELS] * jnp.sum(q * k, axis=-1, keepdims=True)
    for lev in range(HG_LEVELS):
        qs, ks, _, _ = _hg_level(q, k, ex, later_ref, lev)
        a = a + same_ref[lev] * _dot(qs, ks, "nt")
    return a


def _hg_specs(n, with_time):
    c = HG_CHUNK
    nc = n // c

    def chunk(d, i):
        first = d if with_time else 1 - d
        return i + first * (nc - 1 - 2 * i)

    def pcols(off, dir_stride=0):
        return [pl.BlockSpec((c, HG_PAIR), lambda d, i, j=j: (chunk(d, i), off // HG_PAIR + dir_stride // HG_PAIR * d + j)) for j in range(2)]

    specs = dict(
        hq=pcols(OFF_HQ), v=pcols(OFF_HI), z=pcols(OFF_ZF, OFF_ZB - OFF_ZF),
        shared=pl.BlockSpec((c, HG_DIM), lambda d, i: (chunk(d, i), 0)),
        per_dir=pl.BlockSpec((1, c, HG_DIM), lambda d, i: (d, chunk(d, i), 0)),
        vec=pl.BlockSpec((1, 1, HG_DIM), lambda d, i: (d, 0, 0)),
        seg=pl.BlockSpec((1, (HG_LEVELS + 2) * c, c), lambda d, i: (d, 0, 0)),
        sums=pl.BlockSpec((1, c, (HG_LEVELS + 1) * c), lambda d, i: (d, 0, 0)),
        later=pl.BlockSpec((1, HG_LEVELS, c, HG_HEAD_DIM), lambda d, i: (d, 0, 0, 0)),
        same=pl.BlockSpec((HG_LEVELS + 1, c, c), lambda d, i: (0, 0, 0)),
        state=pl.BlockSpec((1, HG_HEADS, 1, HG_HEAD_DIM, HG_HEAD_DIM), lambda d, i: (d, 0, chunk(d, i), 0, 0)),
        weights=pl.BlockSpec((1, HG_HEADS, 1, c, c), lambda d, i: (d, 0, chunk(d, i), 0, 0)),
        levels=pl.BlockSpec((1, HG_HEADS, 1, HG_LEVELS, c, HG_HEAD_DIM), lambda d, i: (d, 0, chunk(d, i), 0, 0, 0)))
    return nc, specs


def _hg_head(refs, hh):
    off = (hh % 2) * HG_HEAD_DIM
    return refs[hh // 2][:, off:off + HG_HEAD_DIM]


def _hg_lanes(hh):
    return slice(hh * HG_HEAD_DIM, (hh + 1) * HG_HEAD_DIM)


def _hg_exps(seg_ref, f, kept=None):
    c = HG_CHUNK
    lf = jnp.log(f)
    if kept is None:
        args = _split_dot(seg_ref[0], lf)
        return [jnp.exp(args[j * c:(j + 1) * c]) for j in range(HG_LEVELS + 2)]
    chunk_wide = jnp.concatenate([seg_ref[0, 0:c], seg_ref[0, (HG_LEVELS + 1) * c:(HG_LEVELS + 2) * c]], axis=0)
    args = _split_dot(chunk_wide, lf)
    return [jnp.exp(args[:c])] + [e.astype(F32) for e in kept] + [jnp.exp(args[c:])]


def _hg_last_row(a, mirrored):
    return jnp.where(mirrored, a[0:1, :], a[HG_CHUNK - 1:HG_CHUNK, :])


def _hgrn_fwd(p, a0, a1, seg, masks, name):
    n = p.shape[0]
    nc, sp = _hg_specs(n, True)

    def body(hq0, hq1, z0, z1, v0, v1, a0_ref, a1_ref, seg_ref, later_ref, same_ref, o_ref, s0_ref, a_ref, e_ref, st):
        @pl.when(pl.program_id(1) == 0)
        def _():
            st[...] = jnp.zeros_like(st)

        mirrored = pl.program_id(0) == 1
        for hh in range(HG_HEADS):
            ln = _hg_lanes(hh)
            q, f, k, _, _ = _hg_gates(_hg_head((hq0, hq1), hh), _hg_head((z0, z1), hh), a0_ref[0, :, ln], a1_ref[0, :, ln])
            vv = _hg_head((v0, v1), hh)
            ex = _hg_exps(seg_ref, f)
            for lev in range(HG_LEVELS):
                e_ref[0, hh, 0, lev] = ex[lev + 1].astype(e_ref.dtype)
            a = _hg_intra(q, k, ex, later_ref, same_ref).astype(MXU_DTYPE)
            a_ref[0, hh, 0] = a
            s_t = st[hh]
            s0_ref[0, hh, 0] = s_t
            o_ref[0, :, ln] = _dot(a, vv) + _dot(q * ex[0], s_t, "nt")
            st[hh] = s_t * _hg_last_row(ex[0], mirrored) + _dot(vv, k * ex[HG_LEVELS + 1], "tn")

    return pl.pallas_call(
        body, name=name, grid=(2, nc), in_specs=sp["hq"] + sp["z"] + sp["v"] + [sp["vec"], sp["vec"], sp["seg"], sp["later"], sp["same"]],
        out_specs=[sp["per_dir"], sp["state"], sp["weights"], sp["levels"]],
        out_shape=[SDS((2, n, HG_DIM), F32), SDS((2, HG_HEADS, nc, HG_HEAD_DIM, HG_HEAD_DIM), F32),
                   SDS((2, HG_HEADS, nc, HG_CHUNK, HG_CHUNK), MXU_DTYPE),
                   SDS((2, HG_HEADS, nc, HG_LEVELS, HG_CHUNK, HG_HEAD_DIM), MXU_DTYPE)],
        scratch_shapes=[pltpu.VMEM((HG_HEADS, HG_HEAD_DIM, HG_HEAD_DIM), F32)],
        compiler_params=_cp("parallel", "arbitrary"))(p, p, p, p, p, p, a0, a1, seg, *masks)


def _hgrn_bwd(p, a0, a1, seg, masks, gp, gn, do, s0, a, e, name):
    n = p.shape[0]
    nc, sp = _hg_specs(n, False)


    def body(hq0, hq1, z0, z1, v0, v1, a0_ref, a1_ref, seg_ref, later_ref, same_ref, gp_ref, gn_ref, do_ref, s0_ref, a_ref, e_ref,
             dhq_ref, dz_ref, dv_ref, dlb_ref, rt):
        @pl.when(pl.program_id(1) == 0)
        def _():
            rt[...] = jnp.zeros_like(rt)
            dlb_ref[...] = jnp.zeros_like(dlb_ref)

        mirrored = pl.program_id(0) == 1
        for hh in range(HG_HEADS):
            ln = _hg_lanes(hh)
            hqv = _hg_head((hq0, hq1), hh)
            q, f, k, sg, lb = _hg_gates(hqv, _hg_head((z0, z1), hh), a0_ref[0, :, ln], a1_ref[0, :, ln])
            vv, dov = _hg_head((v0, v1), hh), do_ref[:, ln]
            ex = _hg_exps(seg_ref, f, kept=[e_ref[0, hh, 0, lev] for lev in range(HG_LEVELS)])
            a = a_ref[0, hh, 0]
            da = _dot(dov, vv, "nt")
            diag = jnp.sum(dov * vv, axis=-1, keepdims=True)
            s_t = s0_ref[0, hh, 0]
            r_t = rt[hh]
            k_end = k * ex[HG_LEVELS + 1]
            dv_ref[0, :, ln] = _dot(a, dov, "tn") + _dot(k_end, r_t, "nt")
            dq_inter = ex[0] * _dot(dov, s_t)
            dk_inter = ex[HG_LEVELS + 1] * _dot(vv, r_t)
            dq = diag * k + dq_inter
            dk = diag * q + dk_inter
            q_terms, k_terms = [q * dq_inter], [k * dk_inter]
            for lev in range(HG_LEVELS):
                qs, ks, e_q, e_k = _hg_level(q, k, ex, later_ref, lev)
                pairs = da * same_ref[lev]
                q_part = e_q * _dot(pairs, ks)
                k_part = e_k * _dot(pairs, qs, "tn")
                dq, dk = dq + q_part, dk + k_part
                q_terms.append(q * q_part)
                k_terms.append(k * k_part)
            decay = _hg_last_row(ex[0], mirrored)
            rt[hh] = r_t * decay + _dot(dov, q * ex[0], "tn")
            later = decay * jnp.sum(s_t * r_t, axis=0, keepdims=True)
            dlf = _dot(gp_ref[0], jnp.concatenate(q_terms, axis=0)) + _dot(gn_ref[0], jnp.concatenate(k_terms, axis=0)) + later
            df = dlf / f - dk
            dz_ref[0, :, ln] = df * (1.0 - lb) * sg * (1.0 - sg)
            dlb_ref[0, :, ln] += jnp.sum(df * (1.0 - sg), axis=0, keepdims=True)
            sq = _sigmoid(hqv)
            dhq_ref[0, :, ln] = dq * sq * (1.0 + hqv * (1.0 - sq))

    out = SDS((2, n, HG_DIM), F32)
    return pl.pallas_call(
        body, name=name, grid=(2, nc),
        in_specs=sp["hq"] + sp["z"] + sp["v"] + [sp["vec"], sp["vec"], sp["seg"], sp["later"], sp["same"], sp["sums"], sp["sums"],
                                                 sp["shared"], sp["state"], sp["weights"], sp["levels"]],
        out_specs=[sp["per_dir"], sp["per_dir"], sp["per_dir"], sp["vec"]], out_shape=[out, out, out, SDS((2, 1, HG_DIM), F32)],
        scratch_shapes=[pltpu.VMEM((HG_HEADS, HG_HEAD_DIM, HG_HEAD_DIM), F32)],
        compiler_params=_cp("parallel", "arbitrary"))(p, p, p, p, p, p, a0, a1, seg, *masks, gp, gn, do, s0, a, e)


def _hg_post(o2, p, g, name):
    n = p.shape[0]
    tr = min(ROW_TILE, n)
    w = 2 * HG_HEAD_DIM

    def body(of_ref, ob_ref, hg_ref, g_ref, o_ref):
        for j in range(2):
            sl = slice(j * HG_HEAD_DIM, (j + 1) * HG_HEAD_DIM)
            o = of_ref[0, :, sl] + ob_ref[0, :, sl]
            hg = hg_ref[:, sl]
            o_ref[:, sl] = (o * _rstd(o) * g_ref[...] * (hg * _sigmoid(hg))).astype(o_ref.dtype)

    blk = pl.BlockSpec((tr, w), lambda i, j: (i, j))
    dirs = [pl.BlockSpec((1, tr, w), lambda i, j, d=d: (d, i, j)) for d in range(2)]
    return pl.pallas_call(
        body, name=name, grid=(n // tr, HG_DIM // w),
        in_specs=dirs + [pl.BlockSpec((tr, w), lambda i, j: (i, OFF_HG // w + j)), pl.BlockSpec((1, HG_HEAD_DIM), lambda i, j: (0, 0))],
        out_specs=blk, out_shape=SDS((n, HG_DIM), MXU_DTYPE), compiler_params=_cp("parallel", "parallel"))(o2, o2, p, g)


def _hg_post_bwd(o2, p, g, dcat, name, after=()):
    n = p.shape[0]
    tr = min(ROW_TILE, n)
    w = 2 * HG_HEAD_DIM
    after, after_specs = _unread(after)

    def body(of_ref, ob_ref, hg_ref, g_ref, d_ref, *rest):
        do_ref, dhg_ref, dg_ref = rest[len(after):]

        @pl.when(pl.program_id(1) == 0)
        def _():
            dg_ref[...] = jnp.zeros_like(dg_ref)

        for j in range(2):
            sl = slice(j * HG_HEAD_DIM, (j + 1) * HG_HEAD_DIM)
            o = of_ref[0, :, sl] + ob_ref[0, :, sl]
            hg = hg_ref[:, sl]
            d = d_ref[:, sl].astype(F32)
            sg = _sigmoid(hg)
            on = o * _rstd(o) * g_ref[...]
            dhg_ref[:, sl] = (d * on * sg * (1.0 + hg * (1.0 - sg))).astype(dhg_ref.dtype)
            dx, dg = _rms_bwd(o, g_ref[...], d * hg * sg)
            do_ref[:, sl] = dx
            dg_ref[0, :, sl] += dg

    blk = pl.BlockSpec((tr, w), lambda j, i: (i, j))
    dirs = [pl.BlockSpec((1, tr, w), lambda j, i, d=d: (d, i, j)) for d in range(2)]
    return pl.pallas_call(
        body, name=name, grid=(HG_DIM // w, n // tr),
        in_specs=dirs + [pl.BlockSpec((tr, w), lambda j, i: (i, OFF_HG // w + j)), pl.BlockSpec((1, HG_HEAD_DIM), lambda j, i: (0, 0)),
                         pl.BlockSpec((tr, w), lambda j, i: (i, ATT_Q_DIM // w + j))] + after_specs,
        out_specs=[blk, blk, pl.BlockSpec((1, 1, w), lambda j, i: (j, 0, 0))],
        out_shape=[SDS((n, HG_DIM), F32), SDS((n, HG_DIM), MXU_DTYPE), SDS((HG_DIM // w, 1, w), F32)],
        compiler_params=_cp("parallel", "arbitrary"))(o2, o2, p, g, dcat, *after)


XATT_TQ = 512


def _xattn_fwd(q, kv, name):
    n, nm = q.shape[0], kv.shape[0]
    tq = min(XATT_TQ, n)
    scale = X_HEAD_DIM ** -0.5

    def body(q_ref, k_ref, v_ref, o_ref):
        s = _dot(q_ref[...], k_ref[...], "nt") * scale
        e = jnp.exp(s - jnp.max(s, axis=-1, keepdims=True))
        o_ref[...] = _dot(e / jnp.sum(e, axis=-1, keepdims=True), v_ref[...]).astype(o_ref.dtype)

    qb = pl.BlockSpec((tq, X_HEAD_DIM), lambda h, i: (i, h))
    return pl.pallas_call(
        body, name=name, grid=(X_HEADS, n // tq),
        in_specs=[qb, pl.BlockSpec((nm, X_HEAD_DIM), lambda h, i: (0, h)), pl.BlockSpec((nm, X_HEAD_DIM), lambda h, i: (0, X_HEADS + h))],
        out_specs=qb, out_shape=SDS(q.shape, MXU_DTYPE), compiler_params=_cp("parallel", "parallel"))(q, kv, kv)


def _xattn_bwd(q, kv, do, name, after=()):
    n, nm = q.shape[0], kv.shape[0]
    tq = min(XATT_TQ, n)
    scale = X_HEAD_DIM ** -0.5
    after, after_specs = _unread(after)

    def body(q_ref, k_ref, v_ref, do_ref, *rest):
        dq_ref, dk_ref, dv_ref = rest[len(after):]

        @pl.when(pl.program_id(1) == 0)
        def _():
            dk_ref[...] = jnp.zeros_like(dk_ref)
            dv_ref[...] = jnp.zeros_like(dv_ref)

        qv, dov = q_ref[...], do_ref[...]
        s = _dot(qv, k_ref[...], "nt") * scale
        e = jnp.exp(s - jnp.max(s, axis=-1, keepdims=True))
        p = e / jnp.sum(e, axis=-1, keepdims=True)
        dp = _dot(dov, v_ref[...], "nt")
        ds = p * (dp - jnp.sum(p * dp, axis=-1, keepdims=True)) * scale
        dq_ref[...] = _dot(ds, k_ref[...]).astype(dq_ref.dtype)
        dk_ref[...] += _dot(ds, qv, "tn")
        dv_ref[...] += _dot(p, dov, "tn")

    qb = pl.BlockSpec((tq, X_HEAD_DIM), lambda h, i: (i, h))
    kb = pl.BlockSpec((nm, X_HEAD_DIM), lambda h, i: (0, h))
    return pl.pallas_call(
        body, name=name, grid=(X_HEADS, n // tq),
        in_specs=[qb, kb, pl.BlockSpec((nm, X_HEAD_DIM), lambda h, i: (0, X_HEADS + h)), qb] + after_specs, out_specs=[qb, kb, kb],
        out_shape=[SDS(q.shape, MXU_DTYPE), SDS((nm, X_HEADS * X_HEAD_DIM), F32), SDS((nm, X_HEADS * X_HEAD_DIM), F32)],
        compiler_params=_cp("parallel", "arbitrary"))(q, kv, kv, do, *after)


def _edge_rows(shape):
    row = lax.broadcasted_iota(jnp.int32, shape, 0)
    return row == 0, row == shape[0] - 1


def _shift_rows(u, down, edges):
    if down:
        return jnp.where(edges[0], 0.0, pltpu.roll(u, 1, axis=0))
    return jnp.where(edges[1], 0.0, pltpu.roll(u, u.shape[0] - 1, axis=0))


def _conv(u, w, b, edges):
    return b + _shift_rows(u, True, edges) * w[0:1, :] + u * w[1:2, :] + _shift_rows(u, False, edges) * w[2:3, :]


def _ff_specs(n):
    gate = lambda rows: pl.BlockSpec((rows, FF_COLS), lambda j: (0, j))
    val = lambda rows: pl.BlockSpec((rows, FF_COLS), lambda j: (0, FF_BLOCKS + j))
    return [gate(n), val(n), gate(3), val(3), gate(1), val(1)], gate


def _conv_gate(u, cw, cb, name):
    n = u.shape[0]
    ins, gate_blk = _ff_specs(n)

    def body(ug_ref, uv_ref, wg_ref, wv_ref, bg_ref, bv_ref, o_ref):
        edges = _edge_rows(ug_ref.shape)
        gate = _conv(ug_ref[...], wg_ref[...], bg_ref[...], edges)
        val = _conv(uv_ref[...], wv_ref[...], bv_ref[...], edges)
        o_ref[...] = (gate * _sigmoid(gate) * val).astype(o_ref.dtype)

    return pl.pallas_call(
        body, name=name, grid=(FF_BLOCKS,), in_specs=ins, out_specs=gate_blk(n), out_shape=SDS((n, D_FF), MXU_DTYPE),
        compiler_params=_cp("parallel"))(u, u, cw, cw, cb, cb)


def _conv_gate_bwd(u, cw, cb, da, name, after=()):
    n = u.shape[0]
    ins, gate_blk = _ff_specs(n)
    after, after_specs = _unread(after)

    def side(dacc, u, w, edges, du_ref, dw_ref, db_ref):
        nxt, prv = _shift_rows(dacc, False, edges), _shift_rows(dacc, True, edges)
        du_ref[...] = (nxt * w[0:1, :] + dacc * w[1:2, :] + prv * w[2:3, :]).astype(du_ref.dtype)
        db_ref[...] = jnp.sum(dacc, axis=0, keepdims=True)
        dw_ref[0:1, :] = jnp.sum(nxt * u, axis=0, keepdims=True)
        dw_ref[1:2, :] = jnp.sum(dacc * u, axis=0, keepdims=True)
        dw_ref[2:3, :] = jnp.sum(prv * u, axis=0, keepdims=True)

    def body(ug_ref, uv_ref, wg_ref, wv_ref, bg_ref, bv_ref, da_ref, *rest):
        dug_ref, duv_ref, dwg_ref, dwv_ref, dbg_ref, dbv_ref = rest[len(after):]
        ug, uv = ug_ref[...], uv_ref[...]
        edges = _edge_rows(ug.shape)
        gate = _conv(ug, wg_ref[...], bg_ref[...], edges)
        val = _conv(uv, wv_ref[...], bv_ref[...], edges)
        sg = _sigmoid(gate)
        dav = da_ref[...].astype(F32)
        side(dav * val * sg * (1.0 + gate * (1.0 - sg)), ug, wg_ref[...], edges, dug_ref, dwg_ref, dbg_ref)
        side(dav * gate * sg, uv, wv_ref[...], edges, duv_ref, dwv_ref, dbv_ref)

    return pl.pallas_call(
        body, name=name, grid=(FF_BLOCKS,), in_specs=ins + [gate_blk(n)] + after_specs,
        out_specs=[gate_blk(n), gate_blk(n), gate_blk(3), gate_blk(3), gate_blk(1), gate_blk(1)],
        out_shape=[SDS((n, D_FF), MXU_DTYPE)] * 2 + [SDS((3, D_FF), F32)] * 2 + [SDS((1, D_FF), F32)] * 2,
        compiler_params=_cp("parallel"))(u, u, cw, cw, cb, cb, da, *after)


def _adamw(w, g, m, v, name):
    r, c = w.shape[-2:]
    tr = _row_tile(r, ELEMENTWISE_ROWS)
    assert w.ndim == 2 or w.shape[:-2] == (1,), (name, w.shape)

    def body(w_ref, g_ref, m_ref, v_ref, d_ref, mo_ref, vo_ref, go_ref):
        gv = g_ref[...]
        go_ref[...] = gv
        mn = ADAM_B1 * m_ref[...] + (1.0 - ADAM_B1) * gv
        vn = ADAM_B2 * v_ref[...] + (1.0 - ADAM_B2) * gv * gv
        m_hat = mn / (1.0 - ADAM_B1 ** ADAM_STEP)
        v_hat = vn / (1.0 - ADAM_B2 ** ADAM_STEP)
        d_ref[...] = -ADAM_LR * (m_hat / (jnp.sqrt(v_hat) + ADAM_EPS) + ADAM_WD * w_ref[...])
        mo_ref[...] = mn
        vo_ref[...] = vn

    blk = pl.BlockSpec((tr, c), lambda i: (i, 0)) if w.ndim == 2 else pl.BlockSpec((1, tr, c), lambda i: (0, i, 0))
    out = SDS(w.shape, F32)
    return pl.pallas_call(body, name=name, grid=(r // tr,), in_specs=[blk] * 4, out_specs=[blk] * 4, out_shape=[out] * 4,
                          compiler_params=_cp("parallel"))(w, g, m, v)


ANY = pl.BlockSpec(memory_space=pl.ANY)


def _place():
    x, y, c = lax.axis_index("x"), lax.axis_index("y"), lax.axis_index("c")
    return x, y, c, [(1 - x, y), (x, 1 - y), (1 - x, 1 - y)]


def _gather_shards(shards, name):
    nt = len(shards)

    def body(*refs):
        ins, outs = refs[:nt], refs[nt:2 * nt]
        send, recv, fsend, frecv, osend, orecv = refs[2 * nt:]
        x, y, c, chips = _place()
        me = 2 * x + y

        def half(t, chip, cc):
            h = ins[t].shape[0] // 2
            return outs[t].at[chip, pl.ds(cc * h, h)]

        def ici(t, j):
            cx, cy = chips[j]
            h = ins[t].shape[0] // 2
            return pltpu.make_async_remote_copy(src_ref=ins[t].at[pl.ds(c * h, h)], dst_ref=half(t, me, c),
                                                send_sem=send.at[t, j], recv_sem=recv.at[t, j], device_id=(cx, cy, c), device_id_type=MESH)

        def landed(t, j):
            cx, cy = chips[j]
            blk = half(t, 2 * cx + cy, c)
            return pltpu.make_async_remote_copy(src_ref=blk, dst_ref=blk, send_sem=send.at[t, j], recv_sem=recv.at[t, j],
                                                device_id=(cx, cy, c), device_id_type=MESH)

        def d2d(t, j, cc):
            cx, cy = chips[j]
            blk = half(t, 2 * cx + cy, cc)
            return pltpu.make_async_remote_copy(src_ref=blk, dst_ref=blk, send_sem=fsend.at[t, j], recv_sem=frecv.at[t, j],
                                                device_id=(x, y, 1 - c), device_id_type=MESH)

        own = [pltpu.make_async_remote_copy(src_ref=ins[t], dst_ref=outs[t].at[me], send_sem=osend.at[t], recv_sem=orecv.at[t],
                                            device_id=(x, y, 1 - c), device_id_type=MESH) for t in range(nt)]
        for t in range(nt):
            for j in range(3):
                ici(t, j).start()
        for cp in own:
            cp.start()
        for t in range(nt):
            for j in range(3):
                landed(t, j).wait_recv()
                d2d(t, j, c).start()
        for t in range(nt):
            for j in range(3):
                d2d(t, j, 1 - c).wait_recv()
        for t in range(nt):
            for j in range(3):
                ici(t, j).wait_send()
                d2d(t, j, c).wait_send()
        for cp in own:
            cp.wait()

    return pl.pallas_call(
        body, name=name, in_specs=[ANY] * nt, out_specs=[ANY] * nt,
        out_shape=[SDS((4,) + s.shape, s.dtype) for s in shards],
        scratch_shapes=[pltpu.SemaphoreType.DMA((nt, 3))] * 4 + [pltpu.SemaphoreType.DMA((nt,))] * 2,
        compiler_params=pltpu.CompilerParams(has_side_effects=True))(*shards)


def _join_halves(bufs, name):
    nt = len(bufs)

    def body(*refs):
        outs = refs[nt:2 * nt]
        send, recv = refs[2 * nt:]
        x, y, c, _ = _place()
        cps = [pltpu.make_async_remote_copy(src_ref=outs[t].at[c], dst_ref=outs[t].at[c], send_sem=send.at[t], recv_sem=recv.at[t],
                                            device_id=(x, y, 1 - c), device_id_type=MESH) for t in range(nt)]
        for cp in cps:
            cp.start()
        for t in range(nt):
            theirs = outs[t].at[1 - c]
            pltpu.make_async_remote_copy(src_ref=theirs, dst_ref=theirs, send_sem=send.at[t], recv_sem=recv.at[t],
                                         device_id=(x, y, 1 - c), device_id_type=MESH).wait_recv()
        for cp in cps:
            cp.wait_send()

    return pl.pallas_call(
        body, name=name, in_specs=[ANY] * nt, out_specs=[ANY] * nt, out_shape=[SDS(b.shape, b.dtype) for b in bufs],
        input_output_aliases={t: t for t in range(nt)},
        scratch_shapes=[pltpu.SemaphoreType.DMA((nt,))] * 2,
        compiler_params=pltpu.CompilerParams(has_side_effects=True))(*bufs)


def _exchange_small(v, reduce, name, after=()):
    rows = v.shape[0]
    after, after_specs = _unread(after)

    def body(v_ref, *rest):
        o_ref, buf, send, recv = rest[-4:]
        x, y, c, _ = _place()
        me = 4 * x + 2 * y + c
        buf[me] = v_ref[...]

        def peer(dx, dy, dc):
            return (1 - x if dx else x, 1 - y if dy else y, 1 - c if dc else c)

        peers = [(dx, dy, dc) for dx in range(2) for dy in range(2) for dc in range(2) if (dx, dy, dc) != (0, 0, 0)]
        cps = []
        for j, (dx, dy, dc) in enumerate(peers):
            cps.append(pltpu.make_async_remote_copy(src_ref=v_ref, dst_ref=buf.at[me], send_sem=send.at[j], recv_sem=recv.at[j],
                                                    device_id=peer(dx, dy, dc), device_id_type=MESH))
        for cp in cps:
            cp.start()
        for j, (dx, dy, dc) in enumerate(peers):
            px, py, pc = peer(dx, dy, dc)
            blk = buf.at[4 * px + 2 * py + pc]
            pltpu.make_async_remote_copy(src_ref=blk, dst_ref=blk, send_sem=send.at[j], recv_sem=recv.at[j],
                                         device_id=(px, py, pc), device_id_type=MESH).wait_recv()
        for cp in cps:
            cp.wait_send()
        if reduce:
            acc = buf[0]
            for j in range(1, 8):
                acc = acc + buf[j]
            o_ref[...] = acc
        else:
            o_ref[...] = buf[...]

    vm = pl.BlockSpec(memory_space=pltpu.VMEM)
    return pl.pallas_call(
        body, name=name, in_specs=[vm] + after_specs, out_specs=vm, out_shape=SDS((rows, 128) if reduce else (8, rows, 128), F32),
        scratch_shapes=[pltpu.VMEM((8, rows, 128), F32), pltpu.SemaphoreType.DMA((7,)), pltpu.SemaphoreType.DMA((7,))],
        compiler_params=pltpu.CompilerParams(has_side_effects=True))(v, *after)


HBM = pl.BlockSpec(memory_space=pltpu.HBM)
SEM = pl.BlockSpec(memory_space=pltpu.SEMAPHORE)
TOKEN = pl.BlockSpec(memory_space=pltpu.VMEM)
TOKEN_SHAPE = SDS((8, 128), F32)
PEERS = 7


def _in_hbm(a):
    return pltpu.with_memory_space_constraint(a, pltpu.HBM)


def _split_params():
    return pltpu.CompilerParams(has_side_effects=pltpu.SideEffectType.DATAFLOW_SIDE_EFFECTING)


def _gather_start(shards, name, after=()):
    nt = len(shards)
    after, after_specs = _unread(after)

    def body(*refs):
        ins, lands = refs[:nt], refs[nt:2 * nt]
        outs = refs[2 * nt + len(after):]
        sends, recvs = outs[:nt], outs[nt:2 * nt]
        x, y, c, chips = _place()
        me = 2 * x + y
        for t in range(nt):
            h = ins[t].shape[0] // 2
            mine = pl.ds(c * h, h)
            for j, (cx, cy) in enumerate(chips):
                for dc in range(2):
                    pltpu.make_async_remote_copy(src_ref=ins[t].at[mine], dst_ref=lands[t].at[me, mine], send_sem=sends[t].at[2 * j + dc],
                                                 recv_sem=recvs[t].at[2 * j + c], device_id=(cx, cy, dc), device_id_type=MESH).start()
            pltpu.make_async_remote_copy(src_ref=ins[t], dst_ref=lands[t].at[me], send_sem=sends[t].at[PEERS - 1], recv_sem=recvs[t].at[PEERS - 1],
                                         device_id=(x, y, 1 - c), device_id_type=MESH).start()
        outs[-1][...] = jnp.zeros(TOKEN_SHAPE.shape, F32)

    lands = [lax.empty((4,) + s.shape, s.dtype) for s in shards]
    out = pl.pallas_call(
        body, name=name, in_specs=[HBM] * (2 * nt) + after_specs, out_specs=[SEM] * (2 * nt) + [HBM] * (2 * nt) + [TOKEN],
        out_shape=[pltpu.SemaphoreType.DMA((PEERS,))] * (2 * nt)
        + [pltpu.HBM(s.shape, s.dtype) for s in shards] + [pltpu.HBM(l.shape, l.dtype) for l in lands] + [TOKEN_SHAPE],
        input_output_aliases={t: 2 * nt + t for t in range(2 * nt)}, compiler_params=_split_params())(
            *[_in_hbm(s) for s in shards], *[_in_hbm(l) for l in lands], *after)
    return out[:nt], out[nt:2 * nt], out[2 * nt:3 * nt], out[3 * nt:4 * nt], out[-1]


def _gather_wait(sends, recvs, shards, lands, after, name):
    nt = len(shards)

    def body(*refs):
        ins, lands_ref = refs[:nt], refs[nt:2 * nt]
        send_refs, recv_refs = refs[2 * nt:3 * nt], refs[3 * nt:4 * nt]
        x, y, c, chips = _place()
        for t in range(nt):
            h = ins[t].shape[0] // 2
            for j, (cx, cy) in enumerate(chips):
                for cs in range(2):
                    blk = lands_ref[t].at[2 * cx + cy, pl.ds(cs * h, h)]
                    pltpu.make_async_remote_copy(src_ref=blk, dst_ref=blk, send_sem=send_refs[t].at[2 * j + cs], recv_sem=recv_refs[t].at[2 * j + cs],
                                                 device_id=(cx, cy, cs), device_id_type=MESH).wait()
            blk = lands_ref[t].at[2 * x + y]
            pltpu.make_async_remote_copy(src_ref=blk, dst_ref=blk, send_sem=send_refs[t].at[PEERS - 1], recv_sem=recv_refs[t].at[PEERS - 1],
                                         device_id=(x, y, 1 - c), device_id_type=MESH).wait()

    out = pl.pallas_call(
        body, name=name, in_specs=[HBM] * (2 * nt) + [SEM] * (2 * nt) + [ANY], out_specs=[HBM] * (2 * nt),
        out_shape=[pltpu.HBM(s.shape, s.dtype) for s in shards] + [pltpu.HBM(l.shape, l.dtype) for l in lands],
        input_output_aliases={t: t for t in range(2 * nt)}, compiler_params=_split_params())(*shards, *lands, *sends, *recvs, after)
    return out[nt:]


def _scatter_start(g, name):
    _, r, c_ = g.shape
    h = r // 2

    def body(g_ref, land, send, recv, g_thru, land_thru, token):
        x, y, c, chips = _place()
        for j, (cx, cy) in enumerate(chips):
            for dc in range(2):
                pltpu.make_async_remote_copy(src_ref=g_ref.at[2 * cx + cy, pl.ds(dc * h, h)], dst_ref=land.at[2 * j + c], send_sem=send.at[2 * j + dc],
                                             recv_sem=recv.at[2 * j + c], device_id=(cx, cy, dc), device_id_type=MESH).start()
        pltpu.make_async_remote_copy(src_ref=g_ref.at[2 * x + y, pl.ds((1 - c) * h, h)], dst_ref=land.at[PEERS - 1], send_sem=send.at[PEERS - 1],
                                     recv_sem=recv.at[PEERS - 1], device_id=(x, y, 1 - c), device_id_type=MESH).start()
        token[...] = jnp.zeros(TOKEN_SHAPE.shape, F32)

    land = lax.empty((PEERS, h, c_), g.dtype)
    return pl.pallas_call(
        body, name=name, in_specs=[HBM, HBM], out_specs=[SEM, SEM, HBM, HBM, TOKEN],
        out_shape=[pltpu.SemaphoreType.DMA((PEERS,)), pltpu.SemaphoreType.DMA((PEERS,)), pltpu.HBM(g.shape, g.dtype),
                   pltpu.HBM(land.shape, land.dtype), TOKEN_SHAPE],
        input_output_aliases={0: 2, 1: 3}, compiler_params=_split_params())(_in_hbm(g), _in_hbm(land))


def _scatter_wait(started, after, name):
    nt = len(started)

    def body(*refs):
        lands = refs[nt:2 * nt]
        sends, recvs = refs[2 * nt:3 * nt], refs[3 * nt:4 * nt]
        x, y, c, chips = _place()
        peers = [(cx, cy, dc) for cx, cy in chips for dc in range(2)] + [(x, y, 1 - c)]
        for t in range(nt):
            for k, peer in enumerate(peers):
                blk = lands[t].at[k]
                pltpu.make_async_remote_copy(src_ref=blk, dst_ref=blk, send_sem=sends[t].at[k], recv_sem=recvs[t].at[k],
                                             device_id=peer, device_id_type=MESH).wait()

    gs, lands = [s[2] for s in started], [s[3] for s in started]
    after, after_specs = _unread(after)
    out = pl.pallas_call(
        body, name=name, in_specs=[HBM] * (2 * nt) + [SEM] * (2 * nt) + after_specs, out_specs=[HBM] * (2 * nt),
        out_shape=[pltpu.HBM(a.shape, a.dtype) for a in gs + lands],
        input_output_aliases={t: t for t in range(2 * nt)}, compiler_params=_split_params())(
            *gs, *lands, *[s[0] for s in started], *[s[1] for s in started], *after)
    return out[:nt], out[nt:]


def _sum_devices(g, land, me, core, name):
    npeer, h, c = land.shape
    tr = _row_tile(h, 2 * ELEMENTWISE_ROWS)
    steps = h // tr

    def body(ix_ref, own_ref, land_ref, o_ref):
        acc = own_ref[0].astype(F32)
        for j in range(npeer):
            acc = acc + land_ref[j].astype(F32)
        o_ref[0] = acc

    grid_spec = pltpu.PrefetchScalarGridSpec(
        num_scalar_prefetch=1, grid=(steps,),
        in_specs=[pl.BlockSpec((1, tr, c), lambda i, ix: (ix[0], ix[1] * steps + i, 0)), pl.BlockSpec((npeer, tr, c), lambda i, ix: (0, i, 0))],
        out_specs=pl.BlockSpec((1, tr, c), lambda i, ix: (ix[1], i, 0)))
    return pl.pallas_call(body, name=name, grid_spec=grid_spec, out_shape=SDS((2, h, c), F32),
                          compiler_params=_cp("parallel"))(jnp.stack([me, core]), g, land)


def _pack_small(parts):
    flat = jnp.concatenate([p.reshape(-1) for p in parts])
    total = flat.shape[0]
    rows = -(-total // 1024) * 8
    return jnp.pad(flat, (0, rows * 128 - total)).reshape(rows, 128)


def _unpack_small(packed, shapes):
    flat = packed.reshape(-1)
    out, off = [], 0
    for s in shapes:
        size = int(np.prod(s))
        out.append(flat[off:off + size].reshape(s))
        off += size
    return out


def _local_step(x, mem, target, w_in, first_after, mid_weights, ffn_weights, on_grad, gains, conv_w, conv_b, hg_lb):
    n = x.shape[0]
    cos, sin = _rope_tables(n)
    seg = _hg_segments()
    gp, gn = _hg_pair_sums()
    masks = _hg_level_masks()
    gq2 = jnp.tile(gains["q_norm_g"], (1, 2))
    gk2 = jnp.tile(gains["k_norm_g"], (1, 2))
    a0 = hg_lb[:, 0:1, :]
    a1 = hg_lb[:, 1:2, :]

    p, h1 = _norm_mm(x, gains["pre_mix_g"], w_in, F32, TOKEN_TILE, 1664, "in_proj", after=(first_after,))
    qr, kr = _qk_prep(p, gq2, gk2, cos, sin, "qk_prep")
    heads = lambda a: a.reshape(n, ATT_KV_HEADS, ATT_HEAD_DIM).transpose(1, 0, 2)
    kh = heads(kr)
    vh = heads(p[:, OFF_AV:OFF_AV + ATT_KV_DIM].astype(MXU_DTYPE))
    att = _attn_fwd(qr, kh, vh, "attn_fwd")
    o2, s0, hg_a, hg_e = _hgrn_fwd(p, a0, a1, seg, masks, "hgrn_fwd")
    rec = _hg_post(o2, p, gains["hg_out_norm_g"], "hg_post")
    cat = jnp.concatenate([att, rec], axis=1)
    w_out, w_xq, w_xkv, w_xo = mid_weights(cat)
    mixed, x1 = _mm_resid_norm(cat, w_out, x, gains["post_mix_g"], 512, "out_proj_resid")
    xq, h2 = _norm_mm(x1, gains["pre_x_g"], w_xq, MXU_DTYPE, TOKEN_TILE, 1024, "xq_proj")
    kv, mn = _norm_mm(mem, gains["mem_norm_g"], w_xkv, MXU_DTYPE, 256, 2048, "xkv_proj")
    ox = _xattn_fwd(xq, kv, "xattn_fwd")
    xo, x2 = _mm_resid_norm(ox, w_xo, x1, gains["post_x_g"], 512, "xo_proj_resid")
    w_up = ffn_weights("w_up", x2)
    u, h3 = _norm_mm(x2, gains["pre_ffn_g"], w_up, F32, TOKEN_TILE, 1408, "up_proj")
    act = _conv_gate(u, conv_w, conv_b, "conv_gate")
    w_down = ffn_weights("w_down", act)
    dn, d3, loss = _mm_resid_norm(act, w_down, x2, gains["post_ffn_g"], 512, "down_proj_resid_loss", target=target)

    gs = {}
    d_act, d_dn, gs["post_ffn_g"] = _norm_bwd_mm(dn, gains["post_ffn_g"], d3, w_down, F32, 512, 1408, "ffn_post_bwd_down_dx")
    tok = on_grad("w_down", _mm(act, d_dn, "tn", WIRE_DTYPE, 1408, 1024, "down_dw"))
    du_g, du_v, dcw_g, dcw_v, dcb_g, dcb_v = _conv_gate_bwd(u, conv_w, conv_b, d_act, "conv_gate_bwd", after=(tok,))
    gs["conv_w"] = jnp.concatenate([dcw_g, dcw_v], axis=1)
    gs["conv_b"] = jnp.concatenate([dcb_g, dcb_v], axis=1)
    ff_shard = w_up.shape[2]
    g_up = _dw_by_owner(h3, du_g, ff_shard, 0, None, 512, "up_dw_gate")
    tok = on_grad("w_up", _dw_by_owner(h3, du_v, ff_shard, 2, g_up, 512, "up_dw_value"))
    d2, gs["pre_ffn_g"] = _dx_norm_bwd([(du_g, 0), (du_g, 1), (du_v, 0), (du_v, 1)], w_up, x2, gains["pre_ffn_g"], d3, 512,
                                       "up_dx_pre_bwd", after=(tok,))
    d_ox, d_xo, gs["post_x_g"] = _norm_bwd_mm(xo, gains["post_x_g"], d2, w_xo, MXU_DTYPE, 512, 1024, "x_post_bwd_xo_dx")
    tok = on_grad("w_xo", _mm(ox, d_xo, "tn", WIRE_DTYPE, 512, 1024, "xo_dw"))
    d_xq, d_k, d_v = _xattn_bwd(xq, kv, d_ox, "xattn_bwd", after=(tok,))
    d_kv = jnp.concatenate([d_k, d_v], axis=1).astype(MXU_DTYPE)
    tok = on_grad("w_xq", _mm(h2, d_xq, "tn", WIRE_DTYPE, 512, 1024, "xq_dw"))
    tok_kv = on_grad("w_xkv", _dw_by_owner(mn, d_kv, w_xkv.shape[2], 0, None, 512, "xkv_dw"))
    d1, gs["pre_x_g"] = _dx_norm_bwd([(d_xq, 0)], w_xq[None], x1, gains["pre_x_g"], d2, 512, "xq_dx_pre_bwd", after=(tok, tok_kv))
    d_mn = _mm_nt_parts([(d_kv, s) for s in range(4)], w_xkv, F32, 256, 1024, "xkv_dx")
    _, gs["mem_norm_g"] = _norm_bwd(mem, gains["mem_norm_g"], d_mn, None, MXU_DTYPE, "mem_norm_bwd")
    d_cat, d_mixed, gs["post_mix_g"] = _norm_bwd_mm(mixed, gains["post_mix_g"], d1, w_out, MXU_DTYPE, 512, 1024, "mix_post_bwd_out_dx")
    tok = on_grad("w_out", _mm(cat, d_mixed, "tn", WIRE_DTYPE, 512, 1024, "out_dw"))
    d_o, d_hg, dg_hg = _hg_post_bwd(o2, p, gains["hg_out_norm_g"], d_cat, "hg_post_bwd", after=(tok,))
    gs["hg_out_norm_g"] = dg_hg.reshape(HG_HEADS, HG_HEAD_DIM).sum(axis=0, keepdims=True)
    dhq2, dz2, dhv2, dlb = _hgrn_bwd(p, a0, a1, seg, masks, gp, gn, d_o, s0, hg_a, hg_e, "hgrn_bwd")
    lb = jax.nn.sigmoid(a0 - a1)
    da0 = dlb * lb * (1.0 - lb)
    gs["hg_lb"] = jnp.concatenate([da0, -da0], axis=1)
    d_qr, d_kh, d_vh = _attn_bwd(qr, kh, vh, cat, d_cat, "attn_bwd")
    unheads = lambda a: a.transpose(2, 0, 1).reshape(n, ATT_KV_DIM)
    d_aq, d_ak, dgq, dgk = _qk_prep_bwd(p, gq2, gk2, cos, sin, d_qr, unheads(d_kh), "qk_prep_bwd")
    gs["q_norm_g"] = dgq.reshape(ATT_HEADS, ATT_HEAD_DIM).sum(axis=0, keepdims=True)
    gs["k_norm_g"] = dgk.reshape(ATT_KV_HEADS, ATT_HEAD_DIM).sum(axis=0, keepdims=True)
    d_p = jnp.concatenate([d_aq, d_ak, unheads(d_vh).astype(MXU_DTYPE), (dhq2[0] + dhq2[1]).astype(MXU_DTYPE),
                           dz2[0].astype(MXU_DTYPE), dz2[1].astype(MXU_DTYPE), (dhv2[0] + dhv2[1]).astype(MXU_DTYPE), d_hg], axis=1)
    tok = on_grad("w_in", _mm(h1, d_p, "tn", WIRE_DTYPE, 512, 1664, "in_dw"))
    grad_x, gs["pre_mix_g"] = _dx_norm_bwd([(d_p, 0)], w_in[None], x, gains["pre_mix_g"], d1, 512, "in_dx_pre_bwd", after=(tok,))
    return loss, grad_x, gs


MATS = ("w_in", "w_out", "w_xq", "w_xkv", "w_xo", "w_up", "w_down")
GAINS = ("pre_mix_g", "q_norm_g", "k_norm_g", "hg_out_norm_g", "post_mix_g", "pre_x_g", "mem_norm_g", "post_x_g", "pre_ffn_g", "post_ffn_g")
WEIGHTS = ('pre_mix_g', 'w_in', 'q_norm_g', 'k_norm_g', 'hg_lb', 'hg_out_norm_g', 'w_out', 'post_mix_g', 'pre_x_g', 'mem_norm_g', 'w_xq',
           'w_xkv', 'w_xo', 'post_x_g', 'pre_ffn_g', 'w_up', 'conv_w', 'conv_b', 'w_down', 'post_ffn_g')


def kernel(x, mem, pre_mix_g, w_in, q_norm_g, k_norm_g, hg_lb, hg_out_norm_g, w_out, post_mix_g, pre_x_g, mem_norm_g, w_xq, w_xkv, w_xo, post_x_g, pre_ffn_g, w_up, conv_w, conv_b, w_down, post_ffn_g, loss_target, m_pre_mix_g, m_w_in, m_q_norm_g, m_k_norm_g, m_hg_lb, m_hg_out_norm_g, m_w_out, m_post_mix_g, m_pre_x_g, m_mem_norm_g, m_w_xq, m_w_xkv, m_w_xo, m_post_x_g, m_pre_ffn_g, m_w_up, m_conv_w, m_conv_b, m_w_down, m_post_ffn_g, v_pre_mix_g, v_w_in, v_q_norm_g, v_k_norm_g, v_hg_lb, v_hg_out_norm_g, v_w_out, v_post_mix_g, v_pre_x_g, v_mem_norm_g, v_w_xq, v_w_xkv, v_w_xo, v_post_x_g, v_pre_ffn_g, v_w_up, v_conv_w, v_conv_b, v_w_down, v_post_ffn_g):
    args = dict(locals())
    w = {k: args[k] for k in WEIGHTS}
    m = {k: args["m_" + k] for k in WEIGHTS}
    v = {k: args["v_" + k] for k in WEIGHTS}
    chip = 2 * lax.axis_index("x") + lax.axis_index("y")
    core = lax.axis_index("c")

    shards = {k: w[k][0].astype(WIRE_DTYPE) for k in MATS}

    def whole(k, g):
        return g if k in ("w_xkv", "w_up") else g.reshape(-1, g.shape[-1])

    w_in_shards = _gather_shards([shards["w_in"]], "gather_w_in")[0]
    w_in_full = jnp.concatenate([w_in_shards[s] for s in range(4)], axis=1)
    small_in = _exchange_small(_pack_small([w["conv_w"][0], w["hg_lb"]]), False, "gather_small")
    mid_names, ffn_names = ("w_out", "w_xq", "w_xkv", "w_xo"), ("w_up", "w_down")
    mid = _gather_start([shards[k] for k in mid_names], "gather_mid_start", after=(w_in_full, small_in))
    ffn = _gather_start([shards[k] for k in ffn_names], "gather_ffn_start", after=(mid[4],))

    def mid_weights(after):
        return [whole(k, g) for k, g in zip(mid_names, _gather_wait(*mid[:4], after, "gather_mid_wait"))]

    def ffn_weights(k, after):
        t = ffn_names.index(k)
        return whole(k, _gather_wait(*[part[t:t + 1] for part in ffn[:4]], after, "gather_wait_" + k)[0])

    cw_parts, lb_parts = [], []
    for s in range(4):
        cw_s, lb_s = _unpack_small(small_in[2 * s], [w["conv_w"][0].shape, w["hg_lb"].shape])
        cw_parts.append(cw_s)
        lb_parts.append(lb_s)
    conv_w_full = jnp.concatenate(cw_parts, axis=1)
    hg_lb_full = jnp.concatenate(lb_parts, axis=2)

    started = {}

    def on_grad(k, g):
        if k == "w_in":
            g = g.reshape(g.shape[0], 4, g.shape[1] // 4).transpose(1, 0, 2)
        elif g.ndim == 2:
            g = g.reshape(4, g.shape[0] // 4, g.shape[1])
        *started[k], token = _scatter_start(g, "grad_start_" + k)
        return token

    gains = {k: w[k] for k in GAINS}
    loss_part, grad_x, gs = _local_step(x[0], mem[0], loss_target[0], w_in_full, ffn[4], mid_weights, ffn_weights, on_grad, gains,
                                        conv_w_full, w["conv_b"], hg_lb_full)
    gs["loss"] = loss_part

    grads, delta, new_m, new_v = {}, {}, {}, {}

    def reduce_matrices(names, after, tag):
        sent, landed = _scatter_wait([started[k] for k in names], after, "grad_wait_" + tag)
        halves = [_sum_devices(g, land, chip, core, "grad_sum_" + k) for k, g, land in zip(names, sent, landed)]
        for k, r in zip(names, _join_halves(halves, "grad_join_" + tag)):
            grads[k] = r.reshape(1, -1, r.shape[-1])

    def adamw(names):
        for k in names:
            shape = w[k].shape
            keep = len(shape) == 3 and shape[0] == 1
            two_d = lambda a: a.reshape(shape) if keep else a.reshape(-1, shape[-1])
            d, mo, vo, go = _adamw(two_d(w[k]), two_d(grads[k]), two_d(m[k]), two_d(v[k]), "adamw_" + k)
            delta[k], new_m[k], new_v[k], grads[k] = d.reshape(shape), mo.reshape(shape), vo.reshape(shape), go.reshape(shape)

    early = tuple(k for k in MATS if k != "w_in")
    reduce_matrices(early, (grad_x,), "early")
    adamw(early)

    small_names = GAINS + ("conv_b", "conv_w", "hg_lb")
    packed = _pack_small([gs[k] for k in small_names + ("loss",)])
    reduced_small = _exchange_small(packed, True, "reduce_small", after=tuple(new_v[k] for k in early))
    *summed, loss = _unpack_small(reduced_small, [gs[k].shape for k in small_names + ("loss",)])
    loss = loss[0, 0]
    for k, g in zip(small_names, summed):
        grads[k] = g
    ncw = w["conv_w"].shape[2]
    grads["conv_w"] = lax.dynamic_slice_in_dim(grads["conv_w"], chip * ncw, ncw, axis=1)[None]
    nlb = w["hg_lb"].shape[2]
    grads["hg_lb"] = lax.dynamic_slice_in_dim(grads["hg_lb"], chip * nlb, nlb, axis=2)
    replicated = GAINS + ("conv_b",)
    shapes = [w[k].shape for k in replicated]
    rows = sum(int(np.prod(s)) for s in shapes) // 128
    pack = lambda d: jnp.concatenate([d[k].reshape(-1) for k in replicated]).reshape(rows, 128)
    outs = _adamw(pack(w), reduced_small[:rows], pack(m), pack(v), "adamw_replicated")
    for into, packed_out in zip((delta, new_m, new_v, grads), outs):
        for k, a in zip(replicated, _unpack_small(packed_out, shapes)):
            into[k] = a
    adamw(("conv_w", "hg_lb"))

    reduce_matrices(("w_in",), tuple(new_v[k] for k in early + small_names), "late")
    adamw(("w_in",))
    return (loss, grad_x[None], *[grads[k] for k in WEIGHTS], *[delta[k] for k in WEIGHTS],
            *[new_m[k] for k in WEIGHTS], *[new_v[k] for k in WEIGHTS])
```

```python
import numpy as np
import jax
import jax.numpy as jnp
from jax import lax
from jax.experimental import pallas as pl
from jax.experimental.pallas import tpu as pltpu

F32 = jnp.float32
MXU_DTYPE = jnp.bfloat16
WIRE_DTYPE = jnp.bfloat16
VMEM_LIMIT_BYTES = 56 * 1024 * 1024
ROWS_PER_16BIT_TILE = 16
ELEMENTWISE_ROWS = 256
EPS = 1e-6
MESH = pl.DeviceIdType.MESH

GRID_W = 64
ATT_HEADS, ATT_KV_HEADS, ATT_HEAD_DIM = 8, 2, 64
ATT_GROUP = ATT_HEADS // ATT_KV_HEADS
ATT_Q_DIM, ATT_KV_DIM = 512, 128
ROPE_THETA = 10000.0
HG_HEADS, HG_HEAD_DIM, HG_DIM = 4, 128, 512
HG_CHUNK = 128
HG_LEVELS = 7
HG_PAIR = 2 * HG_HEAD_DIM
HG_KEPT = 7
X_HEADS, X_HEAD_DIM = 4, 256
D_FF = 2816
FF_COLS = 256
FF_BLOCKS = D_FF // FF_COLS
OFF_AK, OFF_AV, OFF_HQ, OFF_ZF, OFF_ZB, OFF_HI, OFF_HG = 512, 640, 768, 1280, 1792, 2304, 2816

ADAM_LR, ADAM_B1, ADAM_B2, ADAM_EPS, ADAM_WD, ADAM_STEP = 0.001, 0.9, 0.999, 1e-08, 0.01, 10

SDS = jax.ShapeDtypeStruct


def _cp(*sem):
    return pltpu.CompilerParams(dimension_semantics=sem, vmem_limit_bytes=VMEM_LIMIT_BYTES)


def _row_tile(rows, cap):
    if rows <= cap:
        return rows
    return max(t for t in range(ROWS_PER_16BIT_TILE, cap + 1, ROWS_PER_16BIT_TILE) if rows % t == 0)


def _dot(a, b, form="nn"):
    dims = {"nn": (((1,), (0,)), ((), ())), "nt": (((1,), (1,)), ((), ())), "tn": (((0,), (0,)), ((), ()))}[form]
    return lax.dot_general(a.astype(MXU_DTYPE), b.astype(MXU_DTYPE), dims, preferred_element_type=F32)


def _sigmoid(x):
    return 1.0 / (1.0 + jnp.exp(-x))


def _rstd(x):
    return lax.rsqrt(jnp.mean(x * x, axis=-1, keepdims=True) + EPS)


def _rms_bwd(x, g, dy):
    r = _rstd(x)
    xh = x * r
    dn = dy * g
    dx = r * (dn - xh * jnp.mean(dn * xh, axis=-1, keepdims=True))
    return dx, jnp.sum(dy * xh, axis=0, keepdims=True)


def _unread(after):
    after = tuple(a for a in after if a is not None)
    return after, [pl.BlockSpec(memory_space=pl.ANY)] * len(after)


def _mm(a, b, form, out_dtype, tm, tn, name, after=()):
    after, after_specs = _unread(after)
    if form == "nn":
        (m, k), n = a.shape, b.shape[1]
    elif form == "nt":
        (m, k), n = a.shape, b.shape[0]
    else:
        (k, m), n = a.shape, b.shape[1]
    tm, tn = min(tm, m), min(tn, n)
    assert m % tm == 0 and n % tn == 0, (name, m, n, tm, tn)

    def body(a_ref, b_ref, *rest):
        o_ref = rest[-1]
        o_ref[...] = _dot(a_ref[...], b_ref[...], form).astype(o_ref.dtype)

    a_spec = pl.BlockSpec((k, tm), lambda i, j: (0, i)) if form == "tn" else pl.BlockSpec((tm, k), lambda i, j: (i, 0))
    b_spec = pl.BlockSpec((tn, k), lambda i, j: (j, 0)) if form == "nt" else pl.BlockSpec((k, tn), lambda i, j: (0, j))
    return pl.pallas_call(
        body, name=name, grid=(m // tm, n // tn), in_specs=[a_spec, b_spec] + after_specs,
        out_specs=pl.BlockSpec((tm, tn), lambda i, j: (i, j)), out_shape=SDS((m, n), out_dtype),
        compiler_params=_cp("parallel", "parallel"))(a, b, *after)


def _mm_nt_parts(a_parts, b, out_dtype, tm, tn, name, after=()):
    after, after_specs = _unread(after)
    parts, n, p = b.shape
    m = a_parts[0][0].shape[0]
    tm, tn = min(tm, m), min(tn, n)
    assert m % tm == 0 and n % tn == 0 and len(a_parts) == parts, (name, m, b.shape)

    def body(*refs):
        o_ref = refs[-1]
        acc = _dot(refs[0][...], refs[parts][0], "nt")
        for s in range(1, parts):
            acc = acc + _dot(refs[s][...], refs[parts + s][0], "nt")
        o_ref[...] = acc.astype(o_ref.dtype)

    a_specs = [pl.BlockSpec((tm, p), lambda i, j, cb=cb: (i, cb)) for _, cb in a_parts]
    b_specs = [pl.BlockSpec((1, tn, p), lambda i, j, s=s: (s, j, 0)) for s in range(parts)]
    return pl.pallas_call(
        body, name=name, grid=(m // tm, n // tn), in_specs=a_specs + b_specs + after_specs,
        out_specs=pl.BlockSpec((tm, tn), lambda i, j: (i, j)), out_shape=SDS((m, n), out_dtype),
        compiler_params=_cp("parallel", "parallel"))(*[arr for arr, _ in a_parts], *([b] * parts), *after)


def _norm_bwd_mm(y, g, d, w, out_dtype, tm, tn, name):
    n, dm = y.shape
    nn = w.shape[0]
    tm, tn = min(tm, n), min(tn, nn)
    assert n % tm == 0 and nn % tn == 0 and w.shape[1] == dm, (name, y.shape, w.shape)

    def body(y_ref, g_ref, d_ref, w_ref, dx_ref, dy_ref, dg_ref, dys):
        i, j = pl.program_id(0), pl.program_id(1)

        @pl.when(jnp.logical_and(i == 0, j == 0))
        def _():
            dg_ref[...] = jnp.zeros_like(dg_ref)

        @pl.when(j == 0)
        def _():
            dy, dg = _rms_bwd(y_ref[...], g_ref[...], d_ref[...])
            dy = dy.astype(MXU_DTYPE)
            dys[...] = dy
            dy_ref[...] = dy
            dg_ref[...] += dg

        dx_ref[...] = _dot(dys[...], w_ref[...], "nt").astype(dx_ref.dtype)

    row = pl.BlockSpec((tm, dm), lambda i, j: (i, 0))
    vec = pl.BlockSpec((1, dm), lambda i, j: (0, 0))
    return pl.pallas_call(
        body, name=name, grid=(n // tm, nn // tn), in_specs=[row, vec, row, pl.BlockSpec((tn, dm), lambda i, j: (j, 0))],
        out_specs=[pl.BlockSpec((tm, tn), lambda i, j: (i, j)), row, vec],
        out_shape=[SDS((n, nn), out_dtype), SDS((n, dm), MXU_DTYPE), SDS((1, dm), F32)],
        scratch_shapes=[pltpu.VMEM((tm, dm), MXU_DTYPE)],
        compiler_params=_cp("arbitrary", "arbitrary"))(y, g, d, w)


def _mm_resid_norm(a, b, x, g, tm, name, target=None):
    n, k = a.shape
    d = b.shape[1]
    tm = min(tm, n)
    assert n % tm == 0 and x.shape == (n, d), (name, a.shape, b.shape)
    with_loss = target is not None

    def body(a_ref, b_ref, x_ref, g_ref, *rest):
        y = _dot(a_ref[...], b_ref[...])
        out = x_ref[...] + y * _rstd(y) * g_ref[...]
        if not with_loss:
            y_ref, o_ref = rest
            y_ref[...] = y
            o_ref[...] = out
            return
        t_ref, y_ref, d_ref, l_ref = rest
        y_ref[...] = y
        diff = out - t_ref[...]
        d_ref[...] = diff * (1.0 / d)

        @pl.when(pl.program_id(0) == 0)
        def _():
            l_ref[...] = jnp.zeros_like(l_ref)

        l_ref[...] += 0.5 * jnp.sum(jnp.mean(diff * diff, axis=-1, keepdims=True), axis=0, keepdims=True)

    row = pl.BlockSpec((tm, d), lambda i: (i, 0))
    ins = [pl.BlockSpec((tm, k), lambda i: (i, 0)), pl.BlockSpec((k, d), lambda i: (0, 0)), row, pl.BlockSpec((1, d), lambda i: (0, 0))]
    out = SDS((n, d), F32)
    if with_loss:
        return pl.pallas_call(body, name=name, grid=(n // tm,), in_specs=ins + [row], out_specs=[row, row, pl.BlockSpec((1, 1), lambda i: (0, 0))],
                              out_shape=[out, out, SDS((1, 1), F32)], compiler_params=_cp("arbitrary"))(a, b, x, g, target)
    return pl.pallas_call(body, name=name, grid=(n // tm,), in_specs=ins, out_specs=[row, row], out_shape=[out, out],
                          compiler_params=_cp("parallel"))(a, b, x, g)


def _dx_norm_bwd(a_parts, b, x, g, res, tm, name, after=()):
    after, after_specs = _unread(after)
    parts, d, p = b.shape
    n = x.shape[0]
    tm = min(tm, n)
    assert n % tm == 0 and len(a_parts) == parts and x.shape[1] == d, (name, x.shape, b.shape)

    def body(*refs):
        x_ref, g_ref, res_ref = refs[2 * parts:2 * parts + 3]
        dx_ref, dg_ref = refs[-2:]
        dh = _dot(refs[0][...], refs[parts][0], "nt")
        for s in range(1, parts):
            dh = dh + _dot(refs[s][...], refs[parts + s][0], "nt")
        dx, dg = _rms_bwd(x_ref[...], g_ref[...], dh)
        dx_ref[...] = dx + res_ref[...]

        @pl.when(pl.program_id(0) == 0)
        def _():
            dg_ref[...] = jnp.zeros_like(dg_ref)

        dg_ref[...] += dg

    a_specs = [pl.BlockSpec((tm, p), lambda i, cb=cb: (i, cb)) for _, cb in a_parts]
    b_specs = [pl.BlockSpec((1, d, p), lambda i, s=s: (s, 0, 0)) for s in range(parts)]
    row = pl.BlockSpec((tm, d), lambda i: (i, 0))
    vec = pl.BlockSpec((1, d), lambda i: (0, 0))
    return pl.pallas_call(
        body, name=name, grid=(n // tm,), in_specs=a_specs + b_specs + [row, vec, row] + after_specs,
        out_specs=[row, vec], out_shape=[SDS((n, d), F32), SDS((1, d), F32)],
        compiler_params=_cp("arbitrary"))(*[arr for arr, _ in a_parts], *([b] * parts), x, g, res, *after)


def _dw_by_owner(a, b, tn, first, into, tm, name):
    k, m = a.shape
    cnt = b.shape[1] // tn
    tm = min(tm, m)
    assert m % tm == 0 and b.shape[1] == cnt * tn and first + cnt <= 4, (name, a.shape, b.shape)

    def body(a_ref, b_ref, *rest):
        rest[-1][0] = _dot(a_ref[...], b_ref[...], "tn").astype(rest[-1].dtype)

    extra = [] if into is None else [into]
    return pl.pallas_call(
        body, name=name, grid=(m // tm, cnt),
        in_specs=[pl.BlockSpec((k, tm), lambda i, j: (0, i)), pl.BlockSpec((k, tn), lambda i, j: (0, j))] + [pl.BlockSpec(memory_space=pl.ANY)] * len(extra),
        out_specs=pl.BlockSpec((1, tm, tn), lambda i, j: (first + j, i, 0)), out_shape=SDS((4, m, tn), WIRE_DTYPE),
        input_output_aliases={2: 0} if extra else {},
        compiler_params=_cp("parallel", "parallel"))(a, b, *extra)


def _norm_mm(x, g, w, out_dtype, tm, tn, name, after=()):
    after, after_specs = _unread(after)
    m, d = x.shape
    sharded = w.ndim == 3
    n = w.shape[-1] * (w.shape[0] if sharded else 1)
    tm, tn = min(tm, m), (w.shape[-1] if sharded else min(tn, n))
    assert m % tm == 0 and n % tn == 0, (name, m, n, tm, tn)

    def body(x_ref, g_ref, w_ref, *rest):
        o_ref, h_ref, hs = rest[-3:]

        @pl.when(pl.program_id(1) == 0)
        def _():
            xv = x_ref[...]
            h = (xv * _rstd(xv) * g_ref[...]).astype(MXU_DTYPE)
            hs[...] = h
            h_ref[...] = h

        o_ref[...] = _dot(hs[...], w_ref[0] if sharded else w_ref[...]).astype(o_ref.dtype)

    w_spec = pl.BlockSpec((1, d, tn), lambda i, j: (j, 0, 0)) if sharded else pl.BlockSpec((d, tn), lambda i, j: (0, j))
    return pl.pallas_call(
        body, name=name, grid=(m // tm, n // tn),
        in_specs=[pl.BlockSpec((tm, d), lambda i, j: (i, 0)), pl.BlockSpec((1, d), lambda i, j: (0, 0)), w_spec] + after_specs,
        out_specs=[pl.BlockSpec((tm, tn), lambda i, j: (i, j)), pl.BlockSpec((tm, d), lambda i, j: (i, 0))],
        out_shape=[SDS((m, n), out_dtype), SDS((m, d), MXU_DTYPE)],
        scratch_shapes=[pltpu.VMEM((tm, d), MXU_DTYPE)],
        compiler_params=_cp("parallel", "arbitrary"))(x, g, w, *after)


ROW_TILE = 512
TOKEN_TILE = 1024


def _norm_bwd(x, g, dy, res, out_dtype, name):
    n, d = x.shape
    tr = min(ROW_TILE, n)
    has_res = res is not None

    def body(*refs):
        x_ref, g_ref, dy_ref = refs[:3]
        dx_ref, dg_ref = refs[-2:]
        dx, dg = _rms_bwd(x_ref[...], g_ref[...], dy_ref[...].astype(F32))
        if has_res:
            dx = dx + refs[3][...]
        dx_ref[...] = dx.astype(dx_ref.dtype)

        @pl.when(pl.program_id(0) == 0)
        def _():
            dg_ref[...] = jnp.zeros_like(dg_ref)

        dg_ref[...] += dg

    row = pl.BlockSpec((tr, d), lambda i: (i, 0))
    vec = pl.BlockSpec((1, d), lambda i: (0, 0))
    ins = [x, g, dy] + ([res] if has_res else [])
    return pl.pallas_call(
        body, name=name, grid=(n // tr,), in_specs=[row, vec, row] + ([row] if has_res else []),
        out_specs=[row, vec], out_shape=[SDS((n, d), out_dtype), SDS((1, d), F32)],
        compiler_params=_cp("arbitrary"))(*ins)


def _rope_tables(n):
    pairs = ATT_HEAD_DIM // 4
    t = np.arange(n)
    inv = np.power(ROPE_THETA, -np.arange(pairs, dtype=np.float32) / pairs).astype(np.float32)
    ang = np.concatenate([(t // GRID_W)[:, None].astype(np.float32) * inv, (t % GRID_W)[:, None].astype(np.float32) * inv], axis=-1)
    cos = np.repeat(np.cos(ang), 2, axis=-1)
    sin = np.repeat(np.sin(ang), 2, axis=-1) * np.tile(np.array([-1.0, 1.0], np.float32), ATT_HEAD_DIM // 2)
    return jnp.asarray(np.tile(cos, 2), F32), jnp.asarray(np.tile(sin, 2), F32)


def _swap_pairs(x):
    lane = lax.broadcasted_iota(jnp.int32, x.shape, 1)
    return jnp.where((lane & 1) == 0, pltpu.roll(x, 127, axis=1), pltpu.roll(x, 1, axis=1))


def _head_mean(v):
    lane = lax.broadcasted_iota(jnp.int32, v.shape, 1)
    lo = jnp.where(lane < ATT_HEAD_DIM, v, 0.0)
    s0 = jnp.sum(lo, axis=-1, keepdims=True)
    s1 = jnp.sum(v - lo, axis=-1, keepdims=True)
    return jnp.where(lane < ATT_HEAD_DIM, s0, s1) * (1.0 / ATT_HEAD_DIM)


def _qk_prep(p, gq, gk, cos, sin, name):
    n = p.shape[0]
    tr = min(ROW_TILE, n)

    def one(xv, g, c, s):
        xn = xv * lax.rsqrt(_head_mean(xv * xv) + EPS) * g
        return xn * c + _swap_pairs(xn) * s

    def body(q_ref, k_ref, gq_ref, gk_ref, c_ref, s_ref, qo_ref, ko_ref):
        c, s = c_ref[...], s_ref[...]
        for j in range(ATT_Q_DIM // 128):
            qo_ref[:, j * 128:(j + 1) * 128] = one(q_ref[:, j * 128:(j + 1) * 128], gq_ref[...], c, s).astype(qo_ref.dtype)
        ko_ref[...] = one(k_ref[...], gk_ref[...], c, s).astype(ko_ref.dtype)

    vec = pl.BlockSpec((1, 128), lambda i: (0, 0))
    tab = pl.BlockSpec((tr, 128), lambda i: (i, 0))
    return pl.pallas_call(
        body, name=name, grid=(n // tr,),
        in_specs=[pl.BlockSpec((tr, ATT_Q_DIM), lambda i: (i, 0)), pl.BlockSpec((tr, 128), lambda i: (i, OFF_AK // 128)), vec, vec, tab, tab],
        out_specs=[pl.BlockSpec((tr, ATT_Q_DIM), lambda i: (i, 0)), tab],
        out_shape=[SDS((n, ATT_Q_DIM), MXU_DTYPE), SDS((n, ATT_KV_DIM), MXU_DTYPE)],
        compiler_params=_cp("parallel"))(p, p, gq, gk, cos, sin)


def _qk_prep_bwd(p, gq, gk, cos, sin, dq, dk, name):
    n = p.shape[0]
    tr = min(ROW_TILE, n)

    def one(xv, g, c, s, dout):
        dxn = dout * c + _swap_pairs(dout * s)
        r = lax.rsqrt(_head_mean(xv * xv) + EPS)
        xh = xv * r
        dn = dxn * g
        dx = r * (dn - xh * _head_mean(dn * xh))
        return dx, jnp.sum(dxn * xh, axis=0, keepdims=True)

    def body(q_ref, k_ref, gq_ref, gk_ref, c_ref, s_ref, dq_ref, dk_ref, dqo_ref, dko_ref, dgq_ref, dgk_ref):
        @pl.when(pl.program_id(0) == 0)
        def _():
            dgq_ref[...] = jnp.zeros_like(dgq_ref)
            dgk_ref[...] = jnp.zeros_like(dgk_ref)

        c, s = c_ref[...], s_ref[...]
        for j in range(ATT_Q_DIM // 128):
            sl = slice(j * 128, (j + 1) * 128)
            dx, dg = one(q_ref[:, sl], gq_ref[...], c, s, dq_ref[:, sl])
            dqo_ref[:, sl] = dx.astype(dqo_ref.dtype)
            dgq_ref[:, sl] += dg
        dx, dg = one(k_ref[...], gk_ref[...], c, s, dk_ref[...])
        dko_ref[...] = dx.astype(dko_ref.dtype)
        dgk_ref[...] += dg

    vec = pl.BlockSpec((1, 128), lambda i: (0, 0))
    tab = pl.BlockSpec((tr, 128), lambda i: (i, 0))
    qrow = pl.BlockSpec((tr, ATT_Q_DIM), lambda i: (i, 0))
    return pl.pallas_call(
        body, name=name, grid=(n // tr,),
        in_specs=[qrow, pl.BlockSpec((tr, 128), lambda i: (i, OFF_AK // 128)), vec, vec, tab, tab, qrow, tab],
        out_specs=[qrow, tab, pl.BlockSpec((1, ATT_Q_DIM), lambda i: (0, 0)), vec],
        out_shape=[SDS((n, ATT_Q_DIM), MXU_DTYPE), SDS((n, ATT_KV_DIM), MXU_DTYPE), SDS((1, ATT_Q_DIM), F32), SDS((1, 128), F32)],
        compiler_params=_cp("arbitrary"))(p, p, gq, gk, cos, sin, dq, dk)


ATT_TQ = 256


def _attn_fwd(q, k, v, name):
    n = q.shape[0]
    tq = min(ATT_TQ, n)
    scale = ATT_HEAD_DIM ** -0.5
    gw = ATT_GROUP * ATT_HEAD_DIM

    def body(q_ref, k_ref, v_ref, o_ref):
        kk, vv = k_ref[0], v_ref[0]
        v_ones = jnp.concatenate([vv, jnp.ones_like(vv)], axis=1)
        outs = []
        for g in range(ATT_GROUP):
            s = _dot(q_ref[:, g * ATT_HEAD_DIM:(g + 1) * ATT_HEAD_DIM] * scale, kk, "nt")
            e = jnp.exp(s - jnp.max(s, axis=-1, keepdims=True))
            ov = _dot(e, v_ones)
            outs.append(ov[:, :ATT_HEAD_DIM] / ov[:, ATT_HEAD_DIM:])
        o_ref[...] = jnp.concatenate(outs, axis=-1).astype(o_ref.dtype)

    kv = pl.BlockSpec((1, n, ATT_HEAD_DIM), lambda h, i: (h, 0, 0))
    return pl.pallas_call(
        body, name=name, grid=(ATT_KV_HEADS, n // tq),
        in_specs=[pl.BlockSpec((tq, gw), lambda h, i: (i, h)), kv, kv],
        out_specs=pl.BlockSpec((tq, gw), lambda h, i: (i, h)), out_shape=SDS((n, ATT_Q_DIM), MXU_DTYPE),
        compiler_params=_cp("parallel", "parallel"))(q, k, v)


def _attn_bwd(q, k, v, o, do, name):
    n = q.shape[0]
    tq = min(ATT_TQ, n)
    scale = ATT_HEAD_DIM ** -0.5
    gw = ATT_GROUP * ATT_HEAD_DIM

    def body(q_ref, k_ref, v_ref, o_ref, do_ref, dq_ref, dk_ref, dv_ref):
        @pl.when(pl.program_id(1) == 0)
        def _():
            dk_ref[...] = jnp.zeros_like(dk_ref)
            dv_ref[...] = jnp.zeros_like(dv_ref)

        kk, vv = k_ref[0], v_ref[0]
        dqs = []
        dk_acc = jnp.zeros((ATT_HEAD_DIM, n), F32)
        dv_acc = jnp.zeros((ATT_HEAD_DIM, n), F32)
        for g in range(ATT_GROUP):
            sl = slice(g * ATT_HEAD_DIM, (g + 1) * ATT_HEAD_DIM)
            qg, dog = q_ref[:, sl] * scale, do_ref[:, sl].astype(F32)
            s = _dot(qg, kk, "nt")
            e = jnp.exp(s - jnp.max(s, axis=-1, keepdims=True))
            inv = 1.0 / jnp.sum(e, axis=-1, keepdims=True)
            delta = jnp.sum(dog * o_ref[:, sl].astype(F32), axis=-1, keepdims=True)
            dse = e * (_dot(dog, vv, "nt") - delta)
            dqs.append(_dot(dse, kk) * (inv * scale))
            dk_acc += _dot(qg.astype(F32) * inv, dse, "tn")
            dv_acc += _dot(dog * inv, e, "tn")
        dq_ref[...] = jnp.concatenate(dqs, axis=-1)
        dk_ref[0] += dk_acc
        dv_ref[0] += dv_acc

    kv = pl.BlockSpec((1, n, ATT_HEAD_DIM), lambda h, i: (h, 0, 0))
    kvt = pl.BlockSpec((1, ATT_HEAD_DIM, n), lambda h, i: (h, 0, 0))
    qb = pl.BlockSpec((tq, gw), lambda h, i: (i, h))
    return pl.pallas_call(
        body, name=name, grid=(ATT_KV_HEADS, n // tq), in_specs=[qb, kv, kv, qb, qb], out_specs=[qb, kvt, kvt],
        out_shape=[SDS((n, ATT_Q_DIM), F32), SDS((ATT_KV_HEADS, ATT_HEAD_DIM, n), F32), SDS((ATT_KV_HEADS, ATT_HEAD_DIM, n), F32)],
        compiler_params=_cp("parallel", "arbitrary"))(q, k, v, o, do)


def _both_directions(mats, axis):
    fwd = np.concatenate(mats, axis=axis).astype(np.float32)
    bwd = np.concatenate([m[::-1, ::-1] for m in mats], axis=axis).astype(np.float32)
    return jnp.asarray(np.stack([fwd, bwd]), MXU_DTYPE)


def _hg_segments():
    c = HG_CHUNK
    t = np.arange(c)[:, None]
    r = np.arange(c)[None, :]
    mats = [(r <= t)]
    for lev in range(HG_LEVELS):
        h = c >> (lev + 1)
        mid = (t // (2 * h)) * (2 * h) + h - 1
        hi = (t // h) % 2 == 1
        mats.append(np.where(hi, (r > mid) & (r <= t), (r > t) & (r <= mid)))
    mats.append(r > t)
    return _both_directions(mats, 0)


def _hg_pair_sums():
    c = HG_CHUNK
    r = np.arange(c)[:, None]
    t = np.arange(c)[None, :]
    gp, gn = [t >= r], [t < r]
    for lev in range(HG_LEVELS):
        sh = HG_LEVELS - 1 - lev
        same = (r >> sh) == (t >> sh)
        gp.append(same & (t >= r))
        gn.append(same & (t < r))
    return _both_directions(gp, 1), _both_directions(gn, 1)


def _split_dot(mat, x):
    hi = x.astype(MXU_DTYPE)
    lo = (x - hi.astype(F32)).astype(MXU_DTYPE)
    return _dot(mat, hi) + _dot(mat, lo)


def _hg_gates(hq, z, a0, a1):
    q = hq * _sigmoid(hq)
    sg = _sigmoid(z)
    lb = _sigmoid(a0 - a1)
    f = lb + (1.0 - lb) * sg
    k = (1.0 - lb) * (1.0 - sg)
    return q, f, k, sg, lb


def _hg_level_masks():
    c = HG_CHUNK
    t = np.arange(c)
    later, same = [], []
    for lev in range(HG_LEVELS):
        sh = HG_LEVELS - 1 - lev
        later.append(np.broadcast_to((((t >> sh) & 1) == 1)[:, None], (c, HG_HEAD_DIM)))
        same.append((t[:, None] >> (sh + 1)) == (t[None, :] >> (sh + 1)))
    same.append(t[:, None] == t[None, :])
    later = np.stack(later).astype(np.float32)
    return jnp.asarray(np.stack([later, 1.0 - later]), F32), jnp.asarray(np.stack(same).astype(np.float32), F32)


def _hg_level(q, k, ex, later_ref, lev):
    e = ex[lev + 1]
    e_q = e * later_ref[0, lev]
    e_k = e - e_q
    return q * e_q, k * e_k, e_q, e_k


def _hg_intra(q, k, ex, later_ref, same_ref):
    a = same_ref[HG_LEVELS] * jnp.sum(q * k, axis=-1, keepdims=True)
    for lev in range(HG_LEVELS):
        qs, ks, _, _ = _hg_level(q, k, ex, later_ref, lev)
        a = a + same_ref[lev] * _dot(qs, ks, "nt")
    return a


def _hg_specs(n, with_time):
    c = HG_CHUNK
    nc = n // c

    def chunk(d, i):
        first = d if with_time else 1 - d
        return i + first * (nc - 1 - 2 * i)

    def pcols(off, dir_stride=0):
        return [pl.BlockSpec((c, HG_PAIR), lambda d, i, j=j: (chunk(d, i), off // HG_PAIR + dir_stride // HG_PAIR * d + j)) for j in range(2)]

    specs = dict(
        hq=pcols(OFF_HQ), v=pcols(OFF_HI), z=pcols(OFF_ZF, OFF_ZB - OFF_ZF),
        shared=pl.BlockSpec((c, HG_DIM), lambda d, i: (chunk(d, i), 0)),
        per_dir=pl.BlockSpec((1, c, HG_DIM), lambda d, i: (d, chunk(d, i), 0)),
        vec=pl.BlockSpec((1, 1, HG_DIM), lambda d, i: (d, 0, 0)),
        seg=pl.BlockSpec((1, (HG_LEVELS + 2) * c, c), lambda d, i: (d, 0, 0)),
        sums=pl.BlockSpec((1, c, (HG_LEVELS + 1) * c), lambda d, i: (d, 0, 0)),
        later=pl.BlockSpec((1, HG_LEVELS, c, HG_HEAD_DIM), lambda d, i: (d, 0, 0, 0)),
        same=pl.BlockSpec((HG_LEVELS + 1, c, c), lambda d, i: (0, 0, 0)),
        state=pl.BlockSpec((1, HG_HEADS, 1, HG_HEAD_DIM, HG_HEAD_DIM), lambda d, i: (d, 0, chunk(d, i), 0, 0)),
        weights=pl.BlockSpec((1, HG_HEADS, 1, c, c), lambda d, i: (d, 0, chunk(d, i), 0, 0)),
        levels=pl.BlockSpec((1, HG_HEADS, 1, HG_LEVELS, c, HG_HEAD_DIM), lambda d, i: (d, 0, chunk(d, i), 0, 0, 0)),
        kept=pl.BlockSpec((1, HG_KEPT, c, HG_DIM), lambda d, i: (d, 0, chunk(d, i), 0)))
    return nc, specs


def _hg_head(refs, hh):
    off = (hh % 2) * HG_HEAD_DIM
    return refs[hh // 2][:, off:off + HG_HEAD_DIM]


def _hg_lanes(hh):
    return slice(hh * HG_HEAD_DIM, (hh + 1) * HG_HEAD_DIM)


def _hg_exps(seg_ref, f):
    c = HG_CHUNK
    args = _split_dot(seg_ref[0], jnp.log(f))
    return [jnp.exp(args[j * c:(j + 1) * c]) for j in range(HG_LEVELS + 2)]


def _hg_last_row(a, mirrored):
    return jnp.where(mirrored, a[0:1, :], a[HG_CHUNK - 1:HG_CHUNK, :])


def _hgrn_fwd(p, a0, a1, seg, masks, name):
    n = p.shape[0]
    nc, sp = _hg_specs(n, True)

    def body(hq0, hq1, z0, z1, v0, v1, a0_ref, a1_ref, seg_ref, later_ref, same_ref, o_ref, s0_ref, a_ref, e_ref, g_ref, st):
        @pl.when(pl.program_id(1) == 0)
        def _():
            st[...] = jnp.zeros_like(st)

        mirrored = pl.program_id(0) == 1
        for hh in range(HG_HEADS):
            ln = _hg_lanes(hh)
            hqv = _hg_head((hq0, hq1), hh)
            q, f, k, sg, _ = _hg_gates(hqv, _hg_head((z0, z1), hh), a0_ref[0, :, ln], a1_ref[0, :, ln])
            vv = _hg_head((v0, v1), hh)
            ex = _hg_exps(seg_ref, f)
            for lev in range(HG_LEVELS):
                e_ref[0, hh, 0, lev] = ex[lev + 1].astype(e_ref.dtype)
            sq = _sigmoid(hqv)
            for j, kept in enumerate((q, k, f, sg, sq * (1.0 + hqv * (1.0 - sq)), ex[0], ex[HG_LEVELS + 1])):
                g_ref[0, j, :, ln] = kept
            a = _hg_intra(q, k, ex, later_ref, same_ref).astype(MXU_DTYPE)
            a_ref[0, hh, 0] = a
            s_t = st[hh]
            s0_ref[0, hh, 0] = s_t
            o_ref[0, :, ln] = _dot(a, vv) + _dot(q * ex[0], s_t, "nt")
            st[hh] = s_t * _hg_last_row(ex[0], mirrored) + _dot(vv, k * ex[HG_LEVELS + 1], "tn")

    return pl.pallas_call(
        body, name=name, grid=(2, nc), in_specs=sp["hq"] + sp["z"] + sp["v"] + [sp["vec"], sp["vec"], sp["seg"], sp["later"], sp["same"]],
        out_specs=[sp["per_dir"], sp["state"], sp["weights"], sp["levels"], sp["kept"]],
        out_shape=[SDS((2, n, HG_DIM), F32), SDS((2, HG_HEADS, nc, HG_HEAD_DIM, HG_HEAD_DIM), F32),
                   SDS((2, HG_HEADS, nc, HG_CHUNK, HG_CHUNK), MXU_DTYPE),
                   SDS((2, HG_HEADS, nc, HG_LEVELS, HG_CHUNK, HG_HEAD_DIM), MXU_DTYPE), SDS((2, HG_KEPT, n, HG_DIM), F32)],
        scratch_shapes=[pltpu.VMEM((HG_HEADS, HG_HEAD_DIM, HG_HEAD_DIM), F32)],
        compiler_params=_cp("parallel", "arbitrary"))(p, p, p, p, p, p, a0, a1, seg, *masks)


def _hgrn_bwd(p, a0, a1, masks, gp, gn, do, s0, a, e, kept, name):
    n = p.shape[0]
    nc, sp = _hg_specs(n, False)


    def body(v0, v1, a0_ref, a1_ref, later_ref, same_ref, gp_ref, gn_ref, do_ref, s0_ref, a_ref, e_ref, g_ref,
             dhq_ref, dz_ref, dv_ref, dlb_ref, rt):
        @pl.when(pl.program_id(1) == 0)
        def _():
            rt[...] = jnp.zeros_like(rt)
            dlb_ref[...] = jnp.zeros_like(dlb_ref)

        mirrored = pl.program_id(0) == 1
        for hh in range(HG_HEADS):
            ln = _hg_lanes(hh)
            q, k, f, sg, dsilu, e_first, e_last = (g_ref[0, j, :, ln] for j in range(HG_KEPT))
            lb = _sigmoid(a0_ref[0, :, ln] - a1_ref[0, :, ln])
            vv, dov = _hg_head((v0, v1), hh), do_ref[:, ln]
            ex = [e_first] + [e_ref[0, hh, 0, lev].astype(F32) for lev in range(HG_LEVELS)] + [e_last]
            a = a_ref[0, hh, 0]
            da = _dot(dov, vv, "nt")
            diag = jnp.sum(dov * vv, axis=-1, keepdims=True)
            s_t = s0_ref[0, hh, 0]
            r_t = rt[hh]
            k_end = k * ex[HG_LEVELS + 1]
            dv_ref[0, :, ln] = _dot(a, dov, "tn") + _dot(k_end, r_t, "nt")
            dq_inter = ex[0] * _dot(dov, s_t)
            dk_inter = ex[HG_LEVELS + 1] * _dot(vv, r_t)
            dq = diag * k + dq_inter
            dk = diag * q + dk_inter
            q_terms, k_terms = [q * dq_inter], [k * dk_inter]
            for lev in range(HG_LEVELS):
                qs, ks, e_q, e_k = _hg_level(q, k, ex, later_ref, lev)
                pairs = da * same_ref[lev]
                q_part = e_q * _dot(pairs, ks)
                k_part = e_k * _dot(pairs, qs, "tn")
                dq, dk = dq + q_part, dk + k_part
                q_terms.append(q * q_part)
                k_terms.append(k * k_part)
            decay = _hg_last_row(ex[0], mirrored)
            rt[hh] = r_t * decay + _dot(dov, q * ex[0], "tn")
            later = decay * jnp.sum(s_t * r_t, axis=0, keepdims=True)
            dlf = _dot(gp_ref[0], jnp.concatenate(q_terms, axis=0)) + _dot(gn_ref[0], jnp.concatenate(k_terms, axis=0)) + later
            df = dlf / f - dk
            dz_ref[0, :, ln] = df * (1.0 - lb) * sg * (1.0 - sg)
            dlb_ref[0, :, ln] += jnp.sum(df * (1.0 - sg), axis=0, keepdims=True)
            dhq_ref[0, :, ln] = dq * dsilu

    out = SDS((2, n, HG_DIM), F32)
    return pl.pallas_call(
        body, name=name, grid=(2, nc),
        in_specs=sp["v"] + [sp["vec"], sp["vec"], sp["later"], sp["same"], sp["sums"], sp["sums"],
                            sp["shared"], sp["state"], sp["weights"], sp["levels"], sp["kept"]],
        out_specs=[sp["per_dir"], sp["per_dir"], sp["per_dir"], sp["vec"]], out_shape=[out, out, out, SDS((2, 1, HG_DIM), F32)],
        scratch_shapes=[pltpu.VMEM((HG_HEADS, HG_HEAD_DIM, HG_HEAD_DIM), F32)],
        compiler_params=_cp("parallel", "arbitrary"))(p, p, a0, a1, *masks, gp, gn, do, s0, a, e, kept)


def _hg_post(o2, p, g, name):
    n = p.shape[0]
    tr = min(ROW_TILE, n)
    w = 2 * HG_HEAD_DIM

    def body(of_ref, ob_ref, hg_ref, g_ref, o_ref):
        for j in range(2):
            sl = slice(j * HG_HEAD_DIM, (j + 1) * HG_HEAD_DIM)
            o = of_ref[0, :, sl] + ob_ref[0, :, sl]
            hg = hg_ref[:, sl]
            o_ref[:, sl] = (o * _rstd(o) * g_ref[...] * (hg * _sigmoid(hg))).astype(o_ref.dtype)

    blk = pl.BlockSpec((tr, w), lambda i, j: (i, j))
    dirs = [pl.BlockSpec((1, tr, w), lambda i, j, d=d: (d, i, j)) for d in range(2)]
    return pl.pallas_call(
        body, name=name, grid=(n // tr, HG_DIM // w),
        in_specs=dirs + [pl.BlockSpec((tr, w), lambda i, j: (i, OFF_HG // w + j)), pl.BlockSpec((1, HG_HEAD_DIM), lambda i, j: (0, 0))],
        out_specs=blk, out_shape=SDS((n, HG_DIM), MXU_DTYPE), compiler_params=_cp("parallel", "parallel"))(o2, o2, p, g)


def _hg_post_bwd(o2, p, g, dcat, name, after=()):
    n = p.shape[0]
    tr = min(ROW_TILE, n)
    w = 2 * HG_HEAD_DIM
    after, after_specs = _unread(after)

    def body(of_ref, ob_ref, hg_ref, g_ref, d_ref, *rest):
        do_ref, dhg_ref, dg_ref = rest[len(after):]

        @pl.when(pl.program_id(1) == 0)
        def _():
            dg_ref[...] = jnp.zeros_like(dg_ref)

        for j in range(2):
            sl = slice(j * HG_HEAD_DIM, (j + 1) * HG_HEAD_DIM)
            o = of_ref[0, :, sl] + ob_ref[0, :, sl]
            hg = hg_ref[:, sl]
            d = d_ref[:, sl].astype(F32)
            sg = _sigmoid(hg)
            on = o * _rstd(o) * g_ref[...]
            dhg_ref[:, sl] = (d * on * sg * (1.0 + hg * (1.0 - sg))).astype(dhg_ref.dtype)
            dx, dg = _rms_bwd(o, g_ref[...], d * hg * sg)
            do_ref[:, sl] = dx
            dg_ref[0, :, sl] += dg

    blk = pl.BlockSpec((tr, w), lambda j, i: (i, j))
    dirs = [pl.BlockSpec((1, tr, w), lambda j, i, d=d: (d, i, j)) for d in range(2)]
    return pl.pallas_call(
        body, name=name, grid=(HG_DIM // w, n // tr),
        in_specs=dirs + [pl.BlockSpec((tr, w), lambda j, i: (i, OFF_HG // w + j)), pl.BlockSpec((1, HG_HEAD_DIM), lambda j, i: (0, 0)),
                         pl.BlockSpec((tr, w), lambda j, i: (i, ATT_Q_DIM // w + j))] + after_specs,
        out_specs=[blk, blk, pl.BlockSpec((1, 1, w), lambda j, i: (j, 0, 0))],
        out_shape=[SDS((n, HG_DIM), F32), SDS((n, HG_DIM), MXU_DTYPE), SDS((HG_DIM // w, 1, w), F32)],
        compiler_params=_cp("parallel", "arbitrary"))(o2, o2, p, g, dcat, *after)


XATT_TQ = 512


def _xattn_fwd(q, kv, name):
    n, nm = q.shape[0], kv.shape[0]
    tq = min(XATT_TQ, n)
    scale = X_HEAD_DIM ** -0.5

    def body(q_ref, k_ref, v_ref, o_ref):
        s = _dot(q_ref[...], k_ref[...], "nt") * scale
        e = jnp.exp(s - jnp.max(s, axis=-1, keepdims=True))
        o_ref[...] = _dot(e / jnp.sum(e, axis=-1, keepdims=True), v_ref[...]).astype(o_ref.dtype)

    qb = pl.BlockSpec((tq, X_HEAD_DIM), lambda h, i: (i, h))
    return pl.pallas_call(
        body, name=name, grid=(X_HEADS, n // tq),
        in_specs=[qb, pl.BlockSpec((nm, X_HEAD_DIM), lambda h, i: (0, h)), pl.BlockSpec((nm, X_HEAD_DIM), lambda h, i: (0, X_HEADS + h))],
        out_specs=qb, out_shape=SDS(q.shape, MXU_DTYPE), compiler_params=_cp("parallel", "parallel"))(q, kv, kv)


def _xattn_bwd(q, kv, do, name, after=()):
    n, nm = q.shape[0], kv.shape[0]
    tq = min(XATT_TQ, n)
    scale = X_HEAD_DIM ** -0.5
    after, after_specs = _unread(after)

    def body(q_ref, k_ref, v_ref, do_ref, *rest):
        dq_ref, dk_ref, dv_ref = rest[len(after):]

        @pl.when(pl.program_id(1) == 0)
        def _():
            dk_ref[...] = jnp.zeros_like(dk_ref)
            dv_ref[...] = jnp.zeros_like(dv_ref)

        qv, dov = q_ref[...], do_ref[...]
        s = _dot(qv, k_ref[...], "nt") * scale
        e = jnp.exp(s - jnp.max(s, axis=-1, keepdims=True))
        p = e / jnp.sum(e, axis=-1, keepdims=True)
        dp = _dot(dov, v_ref[...], "nt")
        ds = p * (dp - jnp.sum(p * dp, axis=-1, keepdims=True)) * scale
        dq_ref[...] = _dot(ds, k_ref[...]).astype(dq_ref.dtype)
        dk_ref[...] += _dot(ds, qv, "tn")
        dv_ref[...] += _dot(p, dov, "tn")

    qb = pl.BlockSpec((tq, X_HEAD_DIM), lambda h, i: (i, h))
    kb = pl.BlockSpec((nm, X_HEAD_DIM), lambda h, i: (0, h))
    return pl.pallas_call(
        body, name=name, grid=(X_HEADS, n // tq),
        in_specs=[qb, kb, pl.BlockSpec((nm, X_HEAD_DIM), lambda h, i: (0, X_HEADS + h)), qb] + after_specs, out_specs=[qb, kb, kb],
        out_shape=[SDS(q.shape, MXU_DTYPE), SDS((nm, X_HEADS * X_HEAD_DIM), F32), SDS((nm, X_HEADS * X_HEAD_DIM), F32)],
        compiler_params=_cp("parallel", "arbitrary"))(q, kv, kv, do, *after)


def _edge_rows(shape):
    row = lax.broadcasted_iota(jnp.int32, shape, 0)
    return row == 0, row == shape[0] - 1


def _shift_rows(u, down, edges):
    if down:
        return jnp.where(edges[0], 0.0, pltpu.roll(u, 1, axis=0))
    return jnp.where(edges[1], 0.0, pltpu.roll(u, u.shape[0] - 1, axis=0))


def _conv(u, w, b, edges):
    return b + _shift_rows(u, True, edges) * w[0:1, :] + u * w[1:2, :] + _shift_rows(u, False, edges) * w[2:3, :]


def _ff_specs(n):
    gate = lambda rows: pl.BlockSpec((rows, FF_COLS), lambda j: (0, j))
    val = lambda rows: pl.BlockSpec((rows, FF_COLS), lambda j: (0, FF_BLOCKS + j))
    return [gate(n), val(n), gate(3), val(3), gate(1), val(1)], gate


def _conv_gate(u, cw, cb, name):
    n = u.shape[0]
    ins, gate_blk = _ff_specs(n)

    def body(ug_ref, uv_ref, wg_ref, wv_ref, bg_ref, bv_ref, o_ref):
        edges = _edge_rows(ug_ref.shape)
        gate = _conv(ug_ref[...], wg_ref[...], bg_ref[...], edges)
        val = _conv(uv_ref[...], wv_ref[...], bv_ref[...], edges)
        o_ref[...] = (gate * _sigmoid(gate) * val).astype(o_ref.dtype)

    return pl.pallas_call(
        body, name=name, grid=(FF_BLOCKS,), in_specs=ins, out_specs=gate_blk(n), out_shape=SDS((n, D_FF), MXU_DTYPE),
        compiler_params=_cp("parallel"))(u, u, cw, cw, cb, cb)


def _conv_gate_bwd(u, cw, cb, da, name, after=()):
    n = u.shape[0]
    ins, gate_blk = _ff_specs(n)
    after, after_specs = _unread(after)

    def side(dacc, u, w, edges, du_ref, dw_ref, db_ref):
        nxt, prv = _shift_rows(dacc, False, edges), _shift_rows(dacc, True, edges)
        du_ref[...] = (nxt * w[0:1, :] + dacc * w[1:2, :] + prv * w[2:3, :]).astype(du_ref.dtype)
        db_ref[...] = jnp.sum(dacc, axis=0, keepdims=True)
        dw_ref[0:1, :] = jnp.sum(nxt * u, axis=0, keepdims=True)
        dw_ref[1:2, :] = jnp.sum(dacc * u, axis=0, keepdims=True)
        dw_ref[2:3, :] = jnp.sum(prv * u, axis=0, keepdims=True)

    def body(ug_ref, uv_ref, wg_ref, wv_ref, bg_ref, bv_ref, da_ref, *rest):
        dug_ref, duv_ref, dwg_ref, dwv_ref, dbg_ref, dbv_ref = rest[len(after):]
        ug, uv = ug_ref[...], uv_ref[...]
        edges = _edge_rows(ug.shape)
        gate = _conv(ug, wg_ref[...], bg_ref[...], edges)
        val = _conv(uv, wv_ref[...], bv_ref[...], edges)
        sg = _sigmoid(gate)
        dav = da_ref[...].astype(F32)
        side(dav * val * sg * (1.0 + gate * (1.0 - sg)), ug, wg_ref[...], edges, dug_ref, dwg_ref, dbg_ref)
        side(dav * gate * sg, uv, wv_ref[...], edges, duv_ref, dwv_ref, dbv_ref)

    return pl.pallas_call(
        body, name=name, grid=(FF_BLOCKS,), in_specs=ins + [gate_blk(n)] + after_specs,
        out_specs=[gate_blk(n), gate_blk(n), gate_blk(3), gate_blk(3), gate_blk(1), gate_blk(1)],
        out_shape=[SDS((n, D_FF), MXU_DTYPE)] * 2 + [SDS((3, D_FF), F32)] * 2 + [SDS((1, D_FF), F32)] * 2,
        compiler_params=_cp("parallel"))(u, u, cw, cw, cb, cb, da, *after)


def _adamw(w, g, m, v, name):
    r, c = w.shape[-2:]
    tr = _row_tile(r, ELEMENTWISE_ROWS)
    assert w.ndim == 2 or w.shape[:-2] == (1,), (name, w.shape)

    def body(w_ref, g_ref, m_ref, v_ref, d_ref, mo_ref, vo_ref, go_ref):
        gv = g_ref[...]
        go_ref[...] = gv
        mn = ADAM_B1 * m_ref[...] + (1.0 - ADAM_B1) * gv
        vn = ADAM_B2 * v_ref[...] + (1.0 - ADAM_B2) * gv * gv
        m_hat = mn / (1.0 - ADAM_B1 ** ADAM_STEP)
        v_hat = vn / (1.0 - ADAM_B2 ** ADAM_STEP)
        d_ref[...] = -ADAM_LR * (m_hat / (jnp.sqrt(v_hat) + ADAM_EPS) + ADAM_WD * w_ref[...])
        mo_ref[...] = mn
        vo_ref[...] = vn

    blk = pl.BlockSpec((tr, c), lambda i: (i, 0)) if w.ndim == 2 else pl.BlockSpec((1, tr, c), lambda i: (0, i, 0))
    out = SDS(w.shape, F32)
    return pl.pallas_call(body, name=name, grid=(r // tr,), in_specs=[blk] * 4, out_specs=[blk] * 4, out_shape=[out] * 4,
                          compiler_params=_cp("parallel"))(w, g, m, v)


ANY = pl.BlockSpec(memory_space=pl.ANY)


def _place():
    x, y, c = lax.axis_index("x"), lax.axis_index("y"), lax.axis_index("c")
    return x, y, c, [(1 - x, y), (x, 1 - y), (1 - x, 1 - y)]


def _gather_shards(shards, name):
    nt = len(shards)

    def body(*refs):
        ins, outs = refs[:nt], refs[nt:2 * nt]
        send, recv, fsend, frecv, osend, orecv = refs[2 * nt:]
        x, y, c, chips = _place()
        me = 2 * x + y

        def half(t, chip, cc):
            h = ins[t].shape[0] // 2
            return outs[t].at[chip, pl.ds(cc * h, h)]

        def ici(t, j):
            cx, cy = chips[j]
            h = ins[t].shape[0] // 2
            return pltpu.make_async_remote_copy(src_ref=ins[t].at[pl.ds(c * h, h)], dst_ref=half(t, me, c),
                                                send_sem=send.at[t, j], recv_sem=recv.at[t, j], device_id=(cx, cy, c), device_id_type=MESH)

        def landed(t, j):
            cx, cy = chips[j]
            blk = half(t, 2 * cx + cy, c)
            return pltpu.make_async_remote_copy(src_ref=blk, dst_ref=blk, send_sem=send.at[t, j], recv_sem=recv.at[t, j],
                                                device_id=(cx, cy, c), device_id_type=MESH)

        def d2d(t, j, cc):
            cx, cy = chips[j]
            blk = half(t, 2 * cx + cy, cc)
            return pltpu.make_async_remote_copy(src_ref=blk, dst_ref=blk, send_sem=fsend.at[t, j], recv_sem=frecv.at[t, j],
                                                device_id=(x, y, 1 - c), device_id_type=MESH)

        own = [pltpu.make_async_remote_copy(src_ref=ins[t], dst_ref=outs[t].at[me], send_sem=osend.at[t], recv_sem=orecv.at[t],
                                            device_id=(x, y, 1 - c), device_id_type=MESH) for t in range(nt)]
        for t in range(nt):
            for j in range(3):
                ici(t, j).start()
        for cp in own:
            cp.start()
        for t in range(nt):
            for j in range(3):
                landed(t, j).wait_recv()
                d2d(t, j, c).start()
        for t in range(nt):
            for j in range(3):
                d2d(t, j, 1 - c).wait_recv()
        for t in range(nt):
            for j in range(3):
                ici(t, j).wait_send()
                d2d(t, j, c).wait_send()
        for cp in own:
            cp.wait()

    return pl.pallas_call(
        body, name=name, in_specs=[ANY] * nt, out_specs=[ANY] * nt,
        out_shape=[SDS((4,) + s.shape, s.dtype) for s in shards],
        scratch_shapes=[pltpu.SemaphoreType.DMA((nt, 3))] * 4 + [pltpu.SemaphoreType.DMA((nt,))] * 2,
        compiler_params=pltpu.CompilerParams(has_side_effects=True))(*shards)


def _join_halves(bufs, name):
    nt = len(bufs)

    def body(*refs):
        outs = refs[nt:2 * nt]
        send, recv = refs[2 * nt:]
        x, y, c, _ = _place()
        cps = [pltpu.make_async_remote_copy(src_ref=outs[t].at[c], dst_ref=outs[t].at[c], send_sem=send.at[t], recv_sem=recv.at[t],
                                            device_id=(x, y, 1 - c), device_id_type=MESH) for t in range(nt)]
        for cp in cps:
            cp.start()
        for t in range(nt):
            theirs = outs[t].at[1 - c]
            pltpu.make_async_remote_copy(src_ref=theirs, dst_ref=theirs, send_sem=send.at[t], recv_sem=recv.at[t],
                                         device_id=(x, y, 1 - c), device_id_type=MESH).wait_recv()
        for cp in cps:
            cp.wait_send()

    return pl.pallas_call(
        body, name=name, in_specs=[ANY] * nt, out_specs=[ANY] * nt, out_shape=[SDS(b.shape, b.dtype) for b in bufs],
        input_output_aliases={t: t for t in range(nt)},
        scratch_shapes=[pltpu.SemaphoreType.DMA((nt,))] * 2,
        compiler_params=pltpu.CompilerParams(has_side_effects=True))(*bufs)


def _exchange_small(v, reduce, name, after=()):
    rows = v.shape[0]
    after, after_specs = _unread(after)

    def body(v_ref, *rest):
        o_ref, buf, send, recv = rest[-4:]
        x, y, c, _ = _place()
        me = 4 * x + 2 * y + c
        buf[me] = v_ref[...]

        def peer(dx, dy, dc):
            return (1 - x if dx else x, 1 - y if dy else y, 1 - c if dc else c)

        peers = [(dx, dy, dc) for dx in range(2) for dy in range(2) for dc in range(2) if (dx, dy, dc) != (0, 0, 0)]
        cps = []
        for j, (dx, dy, dc) in enumerate(peers):
            cps.append(pltpu.make_async_remote_copy(src_ref=v_ref, dst_ref=buf.at[me], send_sem=send.at[j], recv_sem=recv.at[j],
                                                    device_id=peer(dx, dy, dc), device_id_type=MESH))
        for cp in cps:
            cp.start()
        for j, (dx, dy, dc) in enumerate(peers):
            px, py, pc = peer(dx, dy, dc)
            blk = buf.at[4 * px + 2 * py + pc]
            pltpu.make_async_remote_copy(src_ref=blk, dst_ref=blk, send_sem=send.at[j], recv_sem=recv.at[j],
                                         device_id=(px, py, pc), device_id_type=MESH).wait_recv()
        for cp in cps:
            cp.wait_send()
        if reduce:
            acc = buf[0]
            for j in range(1, 8):
                acc = acc + buf[j]
            o_ref[...] = acc
        else:
            o_ref[...] = buf[...]

    vm = pl.BlockSpec(memory_space=pltpu.VMEM)
    return pl.pallas_call(
        body, name=name, in_specs=[vm] + after_specs, out_specs=vm, out_shape=SDS((rows, 128) if reduce else (8, rows, 128), F32),
        scratch_shapes=[pltpu.VMEM((8, rows, 128), F32), pltpu.SemaphoreType.DMA((7,)), pltpu.SemaphoreType.DMA((7,))],
        compiler_params=pltpu.CompilerParams(has_side_effects=True))(v, *after)


HBM = pl.BlockSpec(memory_space=pltpu.HBM)
SEM = pl.BlockSpec(memory_space=pltpu.SEMAPHORE)
TOKEN = pl.BlockSpec(memory_space=pltpu.VMEM)
TOKEN_SHAPE = SDS((8, 128), F32)
PEERS = 7


def _in_hbm(a):
    return pltpu.with_memory_space_constraint(a, pltpu.HBM)


def _split_params():
    return pltpu.CompilerParams(has_side_effects=pltpu.SideEffectType.DATAFLOW_SIDE_EFFECTING)


def _gather_start(shards, name, after=()):
    nt = len(shards)
    after, after_specs = _unread(after)

    def body(*refs):
        ins, lands = refs[:nt], refs[nt:2 * nt]
        outs = refs[2 * nt + len(after):]
        sends, recvs = outs[:nt], outs[nt:2 * nt]
        x, y, c, chips = _place()
        me = 2 * x + y
        for t in range(nt):
            h = ins[t].shape[0] // 2
            mine = pl.ds(c * h, h)
            for j, (cx, cy) in enumerate(chips):
                for dc in range(2):
                    pltpu.make_async_remote_copy(src_ref=ins[t].at[mine], dst_ref=lands[t].at[me, mine], send_sem=sends[t].at[2 * j + dc],
                                                 recv_sem=recvs[t].at[2 * j + c], device_id=(cx, cy, dc), device_id_type=MESH).start()
            pltpu.make_async_remote_copy(src_ref=ins[t], dst_ref=lands[t].at[me], send_sem=sends[t].at[PEERS - 1], recv_sem=recvs[t].at[PEERS - 1],
                                         device_id=(x, y, 1 - c), device_id_type=MESH).start()
        outs[-1][...] = jnp.zeros(TOKEN_SHAPE.shape, F32)

    lands = [lax.empty((4,) + s.shape, s.dtype) for s in shards]
    out = pl.pallas_call(
        body, name=name, in_specs=[HBM] * (2 * nt) + after_specs, out_specs=[SEM] * (2 * nt) + [HBM] * (2 * nt) + [TOKEN],
        out_shape=[pltpu.SemaphoreType.DMA((PEERS,))] * (2 * nt)
        + [pltpu.HBM(s.shape, s.dtype) for s in shards] + [pltpu.HBM(l.shape, l.dtype) for l in lands] + [TOKEN_SHAPE],
        input_output_aliases={t: 2 * nt + t for t in range(2 * nt)}, compiler_params=_split_params())(
            *[_in_hbm(s) for s in shards], *[_in_hbm(l) for l in lands], *after)
    return out[:nt], out[nt:2 * nt], out[2 * nt:3 * nt], out[3 * nt:4 * nt], out[-1]


def _gather_wait(sends, recvs, shards, lands, after, name):
    nt = len(shards)

    def body(*refs):
        ins, lands_ref = refs[:nt], refs[nt:2 * nt]
        send_refs, recv_refs = refs[2 * nt:3 * nt], refs[3 * nt:4 * nt]
        x, y, c, chips = _place()
        for t in range(nt):
            h = ins[t].shape[0] // 2
            for j, (cx, cy) in enumerate(chips):
                for cs in range(2):
                    blk = lands_ref[t].at[2 * cx + cy, pl.ds(cs * h, h)]
                    pltpu.make_async_remote_copy(src_ref=blk, dst_ref=blk, send_sem=send_refs[t].at[2 * j + cs], recv_sem=recv_refs[t].at[2 * j + cs],
                                                 device_id=(cx, cy, cs), device_id_type=MESH).wait()
            blk = lands_ref[t].at[2 * x + y]
            pltpu.make_async_remote_copy(src_ref=blk, dst_ref=blk, send_sem=send_refs[t].at[PEERS - 1], recv_sem=recv_refs[t].at[PEERS - 1],
                                         device_id=(x, y, 1 - c), device_id_type=MESH).wait()

    out = pl.pallas_call(
        body, name=name, in_specs=[HBM] * (2 * nt) + [SEM] * (2 * nt) + [ANY], out_specs=[HBM] * (2 * nt),
        out_shape=[pltpu.HBM(s.shape, s.dtype) for s in shards] + [pltpu.HBM(l.shape, l.dtype) for l in lands],
        input_output_aliases={t: t for t in range(2 * nt)}, compiler_params=_split_params())(*shards, *lands, *sends, *recvs, after)
    return out[nt:]


def _scatter_start(g, name):
    _, r, c_ = g.shape
    h = r // 2

    def body(g_ref, land, send, recv, g_thru, land_thru, token):
        x, y, c, chips = _place()
        for j, (cx, cy) in enumerate(chips):
            for dc in range(2):
                pltpu.make_async_remote_copy(src_ref=g_ref.at[2 * cx + cy, pl.ds(dc * h, h)], dst_ref=land.at[2 * j + c], send_sem=send.at[2 * j + dc],
                                             recv_sem=recv.at[2 * j + c], device_id=(cx, cy, dc), device_id_type=MESH).start()
        pltpu.make_async_remote_copy(src_ref=g_ref.at[2 * x + y, pl.ds((1 - c) * h, h)], dst_ref=land.at[PEERS - 1], send_sem=send.at[PEERS - 1],
                                     recv_sem=recv.at[PEERS - 1], device_id=(x, y, 1 - c), device_id_type=MESH).start()
        token[...] = jnp.zeros(TOKEN_SHAPE.shape, F32)

    land = lax.empty((PEERS, h, c_), g.dtype)
    return pl.pallas_call(
        body, name=name, in_specs=[HBM, HBM], out_specs=[SEM, SEM, HBM, HBM, TOKEN],
        out_shape=[pltpu.SemaphoreType.DMA((PEERS,)), pltpu.SemaphoreType.DMA((PEERS,)), pltpu.HBM(g.shape, g.dtype),
                   pltpu.HBM(land.shape, land.dtype), TOKEN_SHAPE],
        input_output_aliases={0: 2, 1: 3}, compiler_params=_split_params())(_in_hbm(g), _in_hbm(land))


def _scatter_wait(started, after, name):
    nt = len(started)

    def body(*refs):
        lands = refs[nt:2 * nt]
        sends, recvs = refs[2 * nt:3 * nt], refs[3 * nt:4 * nt]
        x, y, c, chips = _place()
        peers = [(cx, cy, dc) for cx, cy in chips for dc in range(2)] + [(x, y, 1 - c)]
        for t in range(nt):
            for k, peer in enumerate(peers):
                blk = lands[t].at[k]
                pltpu.make_async_remote_copy(src_ref=blk, dst_ref=blk, send_sem=sends[t].at[k], recv_sem=recvs[t].at[k],
                                             device_id=peer, device_id_type=MESH).wait()

    gs, lands = [s[2] for s in started], [s[3] for s in started]
    after, after_specs = _unread(after)
    out = pl.pallas_call(
        body, name=name, in_specs=[HBM] * (2 * nt) + [SEM] * (2 * nt) + after_specs, out_specs=[HBM] * (2 * nt),
        out_shape=[pltpu.HBM(a.shape, a.dtype) for a in gs + lands],
        input_output_aliases={t: t for t in range(2 * nt)}, compiler_params=_split_params())(
            *gs, *lands, *[s[0] for s in started], *[s[1] for s in started], *after)
    return out[:nt], out[nt:]


def _sum_devices(g, land, me, core, name):
    npeer, h, c = land.shape
    tr = _row_tile(h, 2 * ELEMENTWISE_ROWS)
    steps = h // tr

    def body(ix_ref, own_ref, land_ref, o_ref):
        acc = own_ref[0].astype(F32)
        for j in range(npeer):
            acc = acc + land_ref[j].astype(F32)
        o_ref[0] = acc

    grid_spec = pltpu.PrefetchScalarGridSpec(
        num_scalar_prefetch=1, grid=(steps,),
        in_specs=[pl.BlockSpec((1, tr, c), lambda i, ix: (ix[0], ix[1] * steps + i, 0)), pl.BlockSpec((npeer, tr, c), lambda i, ix: (0, i, 0))],
        out_specs=pl.BlockSpec((1, tr, c), lambda i, ix: (ix[1], i, 0)))
    return pl.pallas_call(body, name=name, grid_spec=grid_spec, out_shape=SDS((2, h, c), F32),
                          compiler_params=_cp("parallel"))(jnp.stack([me, core]), g, land)


def _pack_small(parts):
    flat = jnp.concatenate([p.reshape(-1) for p in parts])
    total = flat.shape[0]
    rows = -(-total // 1024) * 8
    return jnp.pad(flat, (0, rows * 128 - total)).reshape(rows, 128)


def _unpack_small(packed, shapes):
    flat = packed.reshape(-1)
    out, off = [], 0
    for s in shapes:
        size = int(np.prod(s))
        out.append(flat[off:off + size].reshape(s))
        off += size
    return out


def _local_step(x, mem, target, w_in, first_after, mid_weights, ffn_weights, on_grad, gains, conv_w, conv_b, hg_lb):
    n = x.shape[0]
    cos, sin = _rope_tables(n)
    seg = _hg_segments()
    gp, gn = _hg_pair_sums()
    masks = _hg_level_masks()
    gq2 = jnp.tile(gains["q_norm_g"], (1, 2))
    gk2 = jnp.tile(gains["k_norm_g"], (1, 2))
    a0 = hg_lb[:, 0:1, :]
    a1 = hg_lb[:, 1:2, :]

    p, h1 = _norm_mm(x, gains["pre_mix_g"], w_in, F32, TOKEN_TILE, 1664, "in_proj", after=(first_after,))
    qr, kr = _qk_prep(p, gq2, gk2, cos, sin, "qk_prep")
    heads = lambda a: a.reshape(n, ATT_KV_HEADS, ATT_HEAD_DIM).transpose(1, 0, 2)
    kh = heads(kr)
    vh = heads(p[:, OFF_AV:OFF_AV + ATT_KV_DIM].astype(MXU_DTYPE))
    att = _attn_fwd(qr, kh, vh, "attn_fwd")
    o2, s0, hg_a, hg_e, hg_kept = _hgrn_fwd(p, a0, a1, seg, masks, "hgrn_fwd")
    rec = _hg_post(o2, p, gains["hg_out_norm_g"], "hg_post")
    cat = jnp.concatenate([att, rec], axis=1)
    w_out, w_xq, w_xkv, w_xo = mid_weights(cat)
    mixed, x1 = _mm_resid_norm(cat, w_out, x, gains["post_mix_g"], 512, "out_proj_resid")
    xq, h2 = _norm_mm(x1, gains["pre_x_g"], w_xq, MXU_DTYPE, TOKEN_TILE, 1024, "xq_proj")
    kv, mn = _norm_mm(mem, gains["mem_norm_g"], w_xkv, MXU_DTYPE, 256, 2048, "xkv_proj")
    ox = _xattn_fwd(xq, kv, "xattn_fwd")
    xo, x2 = _mm_resid_norm(ox, w_xo, x1, gains["post_x_g"], 512, "xo_proj_resid")
    w_up = ffn_weights("w_up", x2)
    u, h3 = _norm_mm(x2, gains["pre_ffn_g"], w_up, F32, TOKEN_TILE, 1408, "up_proj")
    act = _conv_gate(u, conv_w, conv_b, "conv_gate")
    w_down = ffn_weights("w_down", act)
    dn, d3, loss = _mm_resid_norm(act, w_down, x2, gains["post_ffn_g"], 512, "down_proj_resid_loss", target=target)

    gs = {}
    d_act, d_dn, gs["post_ffn_g"] = _norm_bwd_mm(dn, gains["post_ffn_g"], d3, w_down, F32, 512, 1408, "ffn_post_bwd_down_dx")
    tok = on_grad("w_down", _mm(act, d_dn, "tn", WIRE_DTYPE, 1408, 1024, "down_dw"))
    du_g, du_v, dcw_g, dcw_v, dcb_g, dcb_v = _conv_gate_bwd(u, conv_w, conv_b, d_act, "conv_gate_bwd", after=(tok,))
    gs["conv_w"] = jnp.concatenate([dcw_g, dcw_v], axis=1)
    gs["conv_b"] = jnp.concatenate([dcb_g, dcb_v], axis=1)
    ff_shard = w_up.shape[2]
    g_up = _dw_by_owner(h3, du_g, ff_shard, 0, None, 512, "up_dw_gate")
    tok = on_grad("w_up", _dw_by_owner(h3, du_v, ff_shard, 2, g_up, 512, "up_dw_value"))
    d2, gs["pre_ffn_g"] = _dx_norm_bwd([(du_g, 0), (du_g, 1), (du_v, 0), (du_v, 1)], w_up, x2, gains["pre_ffn_g"], d3, 512,
                                       "up_dx_pre_bwd", after=(tok,))
    d_ox, d_xo, gs["post_x_g"] = _norm_bwd_mm(xo, gains["post_x_g"], d2, w_xo, MXU_DTYPE, 512, 1024, "x_post_bwd_xo_dx")
    tok = on_grad("w_xo", _mm(ox, d_xo, "tn", WIRE_DTYPE, 512, 1024, "xo_dw"))
    d_xq, d_k, d_v = _xattn_bwd(xq, kv, d_ox, "xattn_bwd", after=(tok,))
    d_kv = jnp.concatenate([d_k, d_v], axis=1).astype(MXU_DTYPE)
    tok = on_grad("w_xq", _mm(h2, d_xq, "tn", WIRE_DTYPE, 512, 1024, "xq_dw"))
    tok_kv = on_grad("w_xkv", _dw_by_owner(mn, d_kv, w_xkv.shape[2], 0, None, 512, "xkv_dw"))
    d1, gs["pre_x_g"] = _dx_norm_bwd([(d_xq, 0)], w_xq[None], x1, gains["pre_x_g"], d2, 512, "xq_dx_pre_bwd", after=(tok, tok_kv))
    d_mn = _mm_nt_parts([(d_kv, s) for s in range(4)], w_xkv, F32, 256, 1024, "xkv_dx")
    _, gs["mem_norm_g"] = _norm_bwd(mem, gains["mem_norm_g"], d_mn, None, MXU_DTYPE, "mem_norm_bwd")
    d_cat, d_mixed, gs["post_mix_g"] = _norm_bwd_mm(mixed, gains["post_mix_g"], d1, w_out, MXU_DTYPE, 512, 1024, "mix_post_bwd_out_dx")
    tok = on_grad("w_out", _mm(cat, d_mixed, "tn", WIRE_DTYPE, 512, 1024, "out_dw"))
    d_o, d_hg, dg_hg = _hg_post_bwd(o2, p, gains["hg_out_norm_g"], d_cat, "hg_post_bwd", after=(tok,))
    gs["hg_out_norm_g"] = dg_hg.reshape(HG_HEADS, HG_HEAD_DIM).sum(axis=0, keepdims=True)
    dhq2, dz2, dhv2, dlb = _hgrn_bwd(p, a0, a1, masks, gp, gn, d_o, s0, hg_a, hg_e, hg_kept, "hgrn_bwd")
    lb = jax.nn.sigmoid(a0 - a1)
    da0 = dlb * lb * (1.0 - lb)
    gs["hg_lb"] = jnp.concatenate([da0, -da0], axis=1)
    d_qr, d_kh, d_vh = _attn_bwd(qr, kh, vh, cat, d_cat, "attn_bwd")
    unheads = lambda a: a.transpose(2, 0, 1).reshape(n, ATT_KV_DIM)
    d_aq, d_ak, dgq, dgk = _qk_prep_bwd(p, gq2, gk2, cos, sin, d_qr, unheads(d_kh), "qk_prep_bwd")
    gs["q_norm_g"] = dgq.reshape(ATT_HEADS, ATT_HEAD_DIM).sum(axis=0, keepdims=True)
    gs["k_norm_g"] = dgk.reshape(ATT_KV_HEADS, ATT_HEAD_DIM).sum(axis=0, keepdims=True)
    d_p = jnp.concatenate([d_aq, d_ak, unheads(d_vh).astype(MXU_DTYPE), (dhq2[0] + dhq2[1]).astype(MXU_DTYPE),
                           dz2[0].astype(MXU_DTYPE), dz2[1].astype(MXU_DTYPE), (dhv2[0] + dhv2[1]).astype(MXU_DTYPE), d_hg], axis=1)
    tok = on_grad("w_in", _mm(h1, d_p, "tn", WIRE_DTYPE, 512, 1664, "in_dw"))
    grad_x, gs["pre_mix_g"] = _dx_norm_bwd([(d_p, 0)], w_in[None], x, gains["pre_mix_g"], d1, 512, "in_dx_pre_bwd", after=(tok,))
    return loss, grad_x, gs


MATS = ("w_in", "w_out", "w_xq", "w_xkv", "w_xo", "w_up", "w_down")
GAINS = ("pre_mix_g", "q_norm_g", "k_norm_g", "hg_out_norm_g", "post_mix_g", "pre_x_g", "mem_norm_g", "post_x_g", "pre_ffn_g", "post_ffn_g")
WEIGHTS = ('pre_mix_g', 'w_in', 'q_norm_g', 'k_norm_g', 'hg_lb', 'hg_out_norm_g', 'w_out', 'post_mix_g', 'pre_x_g', 'mem_norm_g', 'w_xq',
           'w_xkv', 'w_xo', 'post_x_g', 'pre_ffn_g', 'w_up', 'conv_w', 'conv_b', 'w_down', 'post_ffn_g')


def kernel(x, mem, pre_mix_g, w_in, q_norm_g, k_norm_g, hg_lb, hg_out_norm_g, w_out, post_mix_g, pre_x_g, mem_norm_g, w_xq, w_xkv, w_xo, post_x_g, pre_ffn_g, w_up, conv_w, conv_b, w_down, post_ffn_g, loss_target, m_pre_mix_g, m_w_in, m_q_norm_g, m_k_norm_g, m_hg_lb, m_hg_out_norm_g, m_w_out, m_post_mix_g, m_pre_x_g, m_mem_norm_g, m_w_xq, m_w_xkv, m_w_xo, m_post_x_g, m_pre_ffn_g, m_w_up, m_conv_w, m_conv_b, m_w_down, m_post_ffn_g, v_pre_mix_g, v_w_in, v_q_norm_g, v_k_norm_g, v_hg_lb, v_hg_out_norm_g, v_w_out, v_post_mix_g, v_pre_x_g, v_mem_norm_g, v_w_xq, v_w_xkv, v_w_xo, v_post_x_g, v_pre_ffn_g, v_w_up, v_conv_w, v_conv_b, v_w_down, v_post_ffn_g):
    args = dict(locals())
    w = {k: args[k] for k in WEIGHTS}
    m = {k: args["m_" + k] for k in WEIGHTS}
    v = {k: args["v_" + k] for k in WEIGHTS}
    chip = 2 * lax.axis_index("x") + lax.axis_index("y")
    core = lax.axis_index("c")

    shards = {k: w[k][0].astype(WIRE_DTYPE) for k in MATS}

    def whole(k, g):
        return g if k in ("w_xkv", "w_up") else g.reshape(-1, g.shape[-1])

    w_in_shards = _gather_shards([shards["w_in"]], "gather_w_in")[0]
    w_in_full = jnp.concatenate([w_in_shards[s] for s in range(4)], axis=1)
    small_in = _exchange_small(_pack_small([w["conv_w"][0], w["hg_lb"]]), False, "gather_small")
    mid_names, ffn_names = ("w_out", "w_xq", "w_xkv", "w_xo"), ("w_up", "w_down")
    mid = _gather_start([shards[k] for k in mid_names], "gather_mid_start", after=(w_in_full, small_in))
    ffn = _gather_start([shards[k] for k in ffn_names], "gather_ffn_start", after=(mid[4],))

    def mid_weights(after):
        return [whole(k, g) for k, g in zip(mid_names, _gather_wait(*mid[:4], after, "gather_mid_wait"))]

    def ffn_weights(k, after):
        t = ffn_names.index(k)
        return whole(k, _gather_wait(*[part[t:t + 1] for part in ffn[:4]], after, "gather_wait_" + k)[0])

    cw_parts, lb_parts = [], []
    for s in range(4):
        cw_s, lb_s = _unpack_small(small_in[2 * s], [w["conv_w"][0].shape, w["hg_lb"].shape])
        cw_parts.append(cw_s)
        lb_parts.append(lb_s)
    conv_w_full = jnp.concatenate(cw_parts, axis=1)
    hg_lb_full = jnp.concatenate(lb_parts, axis=2)

    started = {}

    def on_grad(k, g):
        if k == "w_in":
            g = g.reshape(g.shape[0], 4, g.shape[1] // 4).transpose(1, 0, 2)
        elif g.ndim == 2:
            g = g.reshape(4, g.shape[0] // 4, g.shape[1])
        *started[k], token = _scatter_start(g, "grad_start_" + k)
        return token

    gains = {k: w[k] for k in GAINS}
    loss_part, grad_x, gs = _local_step(x[0], mem[0], loss_target[0], w_in_full, ffn[4], mid_weights, ffn_weights, on_grad, gains,
                                        conv_w_full, w["conv_b"], hg_lb_full)
    gs["loss"] = loss_part

    grads, delta, new_m, new_v = {}, {}, {}, {}

    def reduce_matrices(names, after, tag):
        sent, landed = _scatter_wait([started[k] for k in names], after, "grad_wait_" + tag)
        halves = [_sum_devices(g, land, chip, core, "grad_sum_" + k) for k, g, land in zip(names, sent, landed)]
        for k, r in zip(names, _join_halves(halves, "grad_join_" + tag)):
            grads[k] = r.reshape(1, -1, r.shape[-1])

    def adamw(names):
        for k in names:
            shape = w[k].shape
            keep = len(shape) == 3 and shape[0] == 1
            two_d = lambda a: a.reshape(shape) if keep else a.reshape(-1, shape[-1])
            d, mo, vo, go = _adamw(two_d(w[k]), two_d(grads[k]), two_d(m[k]), two_d(v[k]), "adamw_" + k)
            delta[k], new_m[k], new_v[k], grads[k] = d.reshape(shape), mo.reshape(shape), vo.reshape(shape), go.reshape(shape)

    early = tuple(k for k in MATS if k != "w_in")
    reduce_matrices(early, (grad_x,), "early")
    adamw(early)

    small_names = GAINS + ("conv_b", "conv_w", "hg_lb")
    packed = _pack_small([gs[k] for k in small_names + ("loss",)])
    reduced_small = _exchange_small(packed, True, "reduce_small", after=tuple(new_v[k] for k in early))
    *summed, loss = _unpack_small(reduced_small, [gs[k].shape for k in small_names + ("loss",)])
    loss = loss[0, 0]
    for k, g in zip(small_names, summed):
        grads[k] = g
    ncw = w["conv_w"].shape[2]
    grads["conv_w"] = lax.dynamic_slice_in_dim(grads["conv_w"], chip * ncw, ncw, axis=1)[None]
    nlb = w["hg_lb"].shape[2]
    grads["hg_lb"] = lax.dynamic_slice_in_dim(grads["hg_lb"], chip * nlb, nlb, axis=2)
    replicated = GAINS + ("conv_b",)
    shapes = [w[k].shape for k in replicated]
    rows = sum(int(np.prod(s)) for s in shapes) // 128
    pack = lambda d: jnp.concatenate([d[k].reshape(-1) for k in replicated]).reshape(rows, 128)
    outs = _adamw(pack(w), reduced_small[:rows], pack(m), pack(v), "adamw_replicated")
    for into, packed_out in zip((delta, new_m, new_v, grads), outs):
        for k, a in zip(replicated, _unpack_small(packed_out, shapes)):
            into[k] = a
    adamw(("conv_w", "hg_lb"))

    reduce_matrices(("w_in",), tuple(new_v[k] for k in early + small_names), "late")
    adamw(("w_in",))
    return (loss, grad_x[None], *[grads[k] for k in WEIGHTS], *[delta[k] for k in WEIGHTS],
            *[new_m[k] for k in WEIGHTS], *[new_v[k] for k in WEIGHTS])
```

```python
import numpy as np
import jax
import jax.numpy as jnp
from jax import lax
from jax.experimental import pallas as pl
from jax.experimental.pallas import tpu as pltpu

F32 = jnp.float32
MXU_DTYPE = jnp.bfloat16
WIRE_DTYPE = jnp.bfloat16
VMEM_LIMIT_BYTES = 56 * 1024 * 1024
ROWS_PER_16BIT_TILE = 16
ELEMENTWISE_ROWS = 256
EPS = 1e-6
MESH = pl.DeviceIdType.MESH

GRID_W = 64
ATT_HEADS, ATT_KV_HEADS, ATT_HEAD_DIM = 8, 2, 64
ATT_GROUP = ATT_HEADS // ATT_KV_HEADS
ATT_Q_DIM, ATT_KV_DIM = 512, 128
ROPE_THETA = 10000.0
HG_HEADS, HG_HEAD_DIM, HG_DIM = 4, 128, 512
HG_CHUNK = 128
HG_LEVELS = 7
HG_PAIR = 2 * HG_HEAD_DIM
HG_KEPT = 7
X_HEADS, X_HEAD_DIM = 4, 256
D_FF = 2816
FF_COLS = 256
FF_BLOCKS = D_FF // FF_COLS
OFF_AK, OFF_AV, OFF_HQ, OFF_ZF, OFF_ZB, OFF_HI, OFF_HG = 512, 640, 768, 1280, 1792, 2304, 2816

ADAM_LR, ADAM_B1, ADAM_B2, ADAM_EPS, ADAM_WD, ADAM_STEP = 0.001, 0.9, 0.999, 1e-08, 0.01, 10

SDS = jax.ShapeDtypeStruct


def _cp(*sem):
    return pltpu.CompilerParams(dimension_semantics=sem, vmem_limit_bytes=VMEM_LIMIT_BYTES)


def _row_tile(rows, cap):
    if rows <= cap:
        return rows
    return max(t for t in range(ROWS_PER_16BIT_TILE, cap + 1, ROWS_PER_16BIT_TILE) if rows % t == 0)


def _dot(a, b, form="nn"):
    dims = {"nn": (((1,), (0,)), ((), ())), "nt": (((1,), (1,)), ((), ())), "tn": (((0,), (0,)), ((), ()))}[form]
    return lax.dot_general(a.astype(MXU_DTYPE), b.astype(MXU_DTYPE), dims, preferred_element_type=F32)


def _sigmoid(x):
    return 1.0 / (1.0 + jnp.exp(-x))


def _rstd(x):
    return lax.rsqrt(jnp.mean(x * x, axis=-1, keepdims=True) + EPS)


def _rms_bwd(x, g, dy):
    r = _rstd(x)
    xh = x * r
    dn = dy * g
    dx = r * (dn - xh * jnp.mean(dn * xh, axis=-1, keepdims=True))
    return dx, jnp.sum(dy * xh, axis=0, keepdims=True)


def _unread(after):
    after = tuple(a for a in after if a is not None)
    return after, [pl.BlockSpec(memory_space=pl.ANY)] * len(after)


def _mm(a, b, form, out_dtype, tm, tn, name, after=()):
    after, after_specs = _unread(after)
    if form == "nn":
        (m, k), n = a.shape, b.shape[1]
    elif form == "nt":
        (m, k), n = a.shape, b.shape[0]
    else:
        (k, m), n = a.shape, b.shape[1]
    tm, tn = min(tm, m), min(tn, n)
    assert m % tm == 0 and n % tn == 0, (name, m, n, tm, tn)

    def body(a_ref, b_ref, *rest):
        o_ref = rest[-1]
        o_ref[...] = _dot(a_ref[...], b_ref[...], form).astype(o_ref.dtype)

    a_spec = pl.BlockSpec((k, tm), lambda i, j: (0, i)) if form == "tn" else pl.BlockSpec((tm, k), lambda i, j: (i, 0))
    b_spec = pl.BlockSpec((tn, k), lambda i, j: (j, 0)) if form == "nt" else pl.BlockSpec((k, tn), lambda i, j: (0, j))
    return pl.pallas_call(
        body, name=name, grid=(m // tm, n // tn), in_specs=[a_spec, b_spec] + after_specs,
        out_specs=pl.BlockSpec((tm, tn), lambda i, j: (i, j)), out_shape=SDS((m, n), out_dtype),
        compiler_params=_cp("parallel", "parallel"))(a, b, *after)


def _mm_nt_parts(a_parts, b, out_dtype, tm, tn, name, after=()):
    after, after_specs = _unread(after)
    parts, n, p = b.shape
    m = a_parts[0][0].shape[0]
    tm, tn = min(tm, m), min(tn, n)
    assert m % tm == 0 and n % tn == 0 and len(a_parts) == parts, (name, m, b.shape)

    def body(*refs):
        o_ref = refs[-1]
        acc = _dot(refs[0][...], refs[parts][0], "nt")
        for s in range(1, parts):
            acc = acc + _dot(refs[s][...], refs[parts + s][0], "nt")
        o_ref[...] = acc.astype(o_ref.dtype)

    a_specs = [pl.BlockSpec((tm, p), lambda i, j, cb=cb: (i, cb)) for _, cb in a_parts]
    b_specs = [pl.BlockSpec((1, tn, p), lambda i, j, s=s: (s, j, 0)) for s in range(parts)]
    return pl.pallas_call(
        body, name=name, grid=(m // tm, n // tn), in_specs=a_specs + b_specs + after_specs,
        out_specs=pl.BlockSpec((tm, tn), lambda i, j: (i, j)), out_shape=SDS((m, n), out_dtype),
        compiler_params=_cp("parallel", "parallel"))(*[arr for arr, _ in a_parts], *([b] * parts), *after)


def _norm_bwd_mm(y, g, d, w, out_dtype, tm, tn, name):
    n, dm = y.shape
    nn = w.shape[0]
    tm, tn = min(tm, n), min(tn, nn)
    assert n % tm == 0 and nn % tn == 0 and w.shape[1] == dm, (name, y.shape, w.shape)

    def body(y_ref, g_ref, d_ref, w_ref, dx_ref, dy_ref, dg_ref, dys):
        i, j = pl.program_id(0), pl.program_id(1)

        @pl.when(jnp.logical_and(i == 0, j == 0))
        def _():
            dg_ref[...] = jnp.zeros_like(dg_ref)

        @pl.when(j == 0)
        def _():
            dy, dg = _rms_bwd(y_ref[...], g_ref[...], d_ref[...])
            dy = dy.astype(MXU_DTYPE)
            dys[...] = dy
            dy_ref[...] = dy
            dg_ref[...] += dg

        dx_ref[...] = _dot(dys[...], w_ref[...], "nt").astype(dx_ref.dtype)

    row = pl.BlockSpec((tm, dm), lambda i, j: (i, 0))
    vec = pl.BlockSpec((1, dm), lambda i, j: (0, 0))
    return pl.pallas_call(
        body, name=name, grid=(n // tm, nn // tn), in_specs=[row, vec, row, pl.BlockSpec((tn, dm), lambda i, j: (j, 0))],
        out_specs=[pl.BlockSpec((tm, tn), lambda i, j: (i, j)), row, vec],
        out_shape=[SDS((n, nn), out_dtype), SDS((n, dm), MXU_DTYPE), SDS((1, dm), F32)],
        scratch_shapes=[pltpu.VMEM((tm, dm), MXU_DTYPE)],
        compiler_params=_cp("arbitrary", "arbitrary"))(y, g, d, w)


def _mm_resid_norm(a, b, x, g, tm, name, target=None):
    n, k = a.shape
    d = b.shape[1]
    tm = min(tm, n)
    assert n % tm == 0 and x.shape == (n, d), (name, a.shape, b.shape)
    with_loss = target is not None

    def body(a_ref, b_ref, x_ref, g_ref, *rest):
        y = _dot(a_ref[...], b_ref[...])
        out = x_ref[...] + y * _rstd(y) * g_ref[...]
        if not with_loss:
            y_ref, o_ref = rest
            y_ref[...] = y
            o_ref[...] = out
            return
        t_ref, y_ref, d_ref, l_ref = rest
        y_ref[...] = y
        diff = out - t_ref[...]
        d_ref[...] = diff * (1.0 / d)

        @pl.when(pl.program_id(0) == 0)
        def _():
            l_ref[...] = jnp.zeros_like(l_ref)

        l_ref[...] += 0.5 * jnp.sum(jnp.mean(diff * diff, axis=-1, keepdims=True), axis=0, keepdims=True)

    row = pl.BlockSpec((tm, d), lambda i: (i, 0))
    ins = [pl.BlockSpec((tm, k), lambda i: (i, 0)), pl.BlockSpec((k, d), lambda i: (0, 0)), row, pl.BlockSpec((1, d), lambda i: (0, 0))]
    out = SDS((n, d), F32)
    if with_loss:
        return pl.pallas_call(body, name=name, grid=(n // tm,), in_specs=ins + [row], out_specs=[row, row, pl.BlockSpec((1, 1), lambda i: (0, 0))],
                              out_shape=[out, out, SDS((1, 1), F32)], compiler_params=_cp("arbitrary"))(a, b, x, g, target)
    return pl.pallas_call(body, name=name, grid=(n // tm,), in_specs=ins, out_specs=[row, row], out_shape=[out, out],
                          compiler_params=_cp("parallel"))(a, b, x, g)


def _dx_norm_bwd(a_parts, b, x, g, res, tm, name, after=()):
    after, after_specs = _unread(after)
    parts, d, p = b.shape
    n = x.shape[0]
    tm = min(tm, n)
    assert n % tm == 0 and len(a_parts) == parts and x.shape[1] == d, (name, x.shape, b.shape)

    def body(*refs):
        x_ref, g_ref, res_ref = refs[2 * parts:2 * parts + 3]
        dx_ref, dg_ref = refs[-2:]
        dh = _dot(refs[0][...], refs[parts][0], "nt")
        for s in range(1, parts):
            dh = dh + _dot(refs[s][...], refs[parts + s][0], "nt")
        dx, dg = _rms_bwd(x_ref[...], g_ref[...], dh)
        dx_ref[...] = dx + res_ref[...]

        @pl.when(pl.program_id(0) == 0)
        def _():
            dg_ref[...] = jnp.zeros_like(dg_ref)

        dg_ref[...] += dg

    a_specs = [pl.BlockSpec((tm, p), lambda i, cb=cb: (i, cb)) for _, cb in a_parts]
    b_specs = [pl.BlockSpec((1, d, p), lambda i, s=s: (s, 0, 0)) for s in range(parts)]
    row = pl.BlockSpec((tm, d), lambda i: (i, 0))
    vec = pl.BlockSpec((1, d), lambda i: (0, 0))
    return pl.pallas_call(
        body, name=name, grid=(n // tm,), in_specs=a_specs + b_specs + [row, vec, row] + after_specs,
        out_specs=[row, vec], out_shape=[SDS((n, d), F32), SDS((1, d), F32)],
        compiler_params=_cp("arbitrary"))(*[arr for arr, _ in a_parts], *([b] * parts), x, g, res, *after)


def _dw_by_owner(a, b, tn, first, into, tm, name):
    k, m = a.shape
    cnt = b.shape[1] // tn
    tm = min(tm, m)
    assert m % tm == 0 and b.shape[1] == cnt * tn and first + cnt <= 4, (name, a.shape, b.shape)

    def body(a_ref, b_ref, *rest):
        rest[-1][0] = _dot(a_ref[...], b_ref[...], "tn").astype(rest[-1].dtype)

    extra = [] if into is None else [into]
    return pl.pallas_call(
        body, name=name, grid=(m // tm, cnt),
        in_specs=[pl.BlockSpec((k, tm), lambda i, j: (0, i)), pl.BlockSpec((k, tn), lambda i, j: (0, j))] + [pl.BlockSpec(memory_space=pl.ANY)] * len(extra),
        out_specs=pl.BlockSpec((1, tm, tn), lambda i, j: (first + j, i, 0)), out_shape=SDS((4, m, tn), WIRE_DTYPE),
        input_output_aliases={2: 0} if extra else {},
        compiler_params=_cp("parallel", "parallel"))(a, b, *extra)


def _norm_mm(x, g, w, out_dtype, tm, tn, name, after=()):
    after, after_specs = _unread(after)
    m, d = x.shape
    sharded = w.ndim == 3
    n = w.shape[-1] * (w.shape[0] if sharded else 1)
    tm, tn = min(tm, m), (w.shape[-1] if sharded else min(tn, n))
    assert m % tm == 0 and n % tn == 0, (name, m, n, tm, tn)

    def body(x_ref, g_ref, w_ref, *rest):
        o_ref, h_ref, hs = rest[-3:]

        @pl.when(pl.program_id(1) == 0)
        def _():
            xv = x_ref[...]
            h = (xv * _rstd(xv) * g_ref[...]).astype(MXU_DTYPE)
            hs[...] = h
            h_ref[...] = h

        o_ref[...] = _dot(hs[...], w_ref[0] if sharded else w_ref[...]).astype(o_ref.dtype)

    w_spec = pl.BlockSpec((1, d, tn), lambda i, j: (j, 0, 0)) if sharded else pl.BlockSpec((d, tn), lambda i, j: (0, j))
    return pl.pallas_call(
        body, name=name, grid=(m // tm, n // tn),
        in_specs=[pl.BlockSpec((tm, d), lambda i, j: (i, 0)), pl.BlockSpec((1, d), lambda i, j: (0, 0)), w_spec] + after_specs,
        out_specs=[pl.BlockSpec((tm, tn), lambda i, j: (i, j)), pl.BlockSpec((tm, d), lambda i, j: (i, 0))],
        out_shape=[SDS((m, n), out_dtype), SDS((m, d), MXU_DTYPE)],
        scratch_shapes=[pltpu.VMEM((tm, d), MXU_DTYPE)],
        compiler_params=_cp("parallel", "arbitrary"))(x, g, w, *after)


ROW_TILE = 512
TOKEN_TILE = 1024


def _norm_bwd(x, g, dy, res, out_dtype, name):
    n, d = x.shape
    tr = min(ROW_TILE, n)
    has_res = res is not None

    def body(*refs):
        x_ref, g_ref, dy_ref = refs[:3]
        dx_ref, dg_ref = refs[-2:]
        dx, dg = _rms_bwd(x_ref[...], g_ref[...], dy_ref[...].astype(F32))
        if has_res:
            dx = dx + refs[3][...]
        dx_ref[...] = dx.astype(dx_ref.dtype)

        @pl.when(pl.program_id(0) == 0)
        def _():
            dg_ref[...] = jnp.zeros_like(dg_ref)

        dg_ref[...] += dg

    row = pl.BlockSpec((tr, d), lambda i: (i, 0))
    vec = pl.BlockSpec((1, d), lambda i: (0, 0))
    ins = [x, g, dy] + ([res] if has_res else [])
    return pl.pallas_call(
        body, name=name, grid=(n // tr,), in_specs=[row, vec, row] + ([row] if has_res else []),
        out_specs=[row, vec], out_shape=[SDS((n, d), out_dtype), SDS((1, d), F32)],
        compiler_params=_cp("arbitrary"))(*ins)


def _rope_tables(n):
    pairs = ATT_HEAD_DIM // 4
    t = np.arange(n)
    inv = np.power(ROPE_THETA, -np.arange(pairs, dtype=np.float32) / pairs).astype(np.float32)
    ang = np.concatenate([(t // GRID_W)[:, None].astype(np.float32) * inv, (t % GRID_W)[:, None].astype(np.float32) * inv], axis=-1)
    cos = np.repeat(np.cos(ang), 2, axis=-1)
    sin = np.repeat(np.sin(ang), 2, axis=-1) * np.tile(np.array([-1.0, 1.0], np.float32), ATT_HEAD_DIM // 2)
    return jnp.asarray(np.tile(cos, 2), F32), jnp.asarray(np.tile(sin, 2), F32)


def _swap_pairs(x):
    lane = lax.broadcasted_iota(jnp.int32, x.shape, 1)
    return jnp.where((lane & 1) == 0, pltpu.roll(x, 127, axis=1), pltpu.roll(x, 1, axis=1))


def _head_mean(v):
    lane = lax.broadcasted_iota(jnp.int32, v.shape, 1)
    lo = jnp.where(lane < ATT_HEAD_DIM, v, 0.0)
    s0 = jnp.sum(lo, axis=-1, keepdims=True)
    s1 = jnp.sum(v - lo, axis=-1, keepdims=True)
    return jnp.where(lane < ATT_HEAD_DIM, s0, s1) * (1.0 / ATT_HEAD_DIM)


def _qk_prep(p, gq, gk, cos, sin, name):
    n = p.shape[0]
    tr = min(ROW_TILE, n)

    def one(xv, g, c, s):
        xn = xv * lax.rsqrt(_head_mean(xv * xv) + EPS) * g
        return xn * c + _swap_pairs(xn) * s

    def body(q_ref, k_ref, gq_ref, gk_ref, c_ref, s_ref, qo_ref, ko_ref):
        c, s = c_ref[...], s_ref[...]
        for j in range(ATT_Q_DIM // 128):
            qo_ref[:, j * 128:(j + 1) * 128] = one(q_ref[:, j * 128:(j + 1) * 128], gq_ref[...], c, s).astype(qo_ref.dtype)
        ko_ref[...] = one(k_ref[...], gk_ref[...], c, s).astype(ko_ref.dtype)

    vec = pl.BlockSpec((1, 128), lambda i: (0, 0))
    tab = pl.BlockSpec((tr, 128), lambda i: (i, 0))
    return pl.pallas_call(
        body, name=name, grid=(n // tr,),
        in_specs=[pl.BlockSpec((tr, ATT_Q_DIM), lambda i: (i, 0)), pl.BlockSpec((tr, 128), lambda i: (i, OFF_AK // 128)), vec, vec, tab, tab],
        out_specs=[pl.BlockSpec((tr, ATT_Q_DIM), lambda i: (i, 0)), tab],
        out_shape=[SDS((n, ATT_Q_DIM), MXU_DTYPE), SDS((n, ATT_KV_DIM), MXU_DTYPE)],
        compiler_params=_cp("parallel"))(p, p, gq, gk, cos, sin)


def _qk_prep_bwd(p, gq, gk, cos, sin, dq, dk, name):
    n = p.shape[0]
    tr = min(ROW_TILE, n)

    def one(xv, g, c, s, dout):
        dxn = dout * c + _swap_pairs(dout * s)
        r = lax.rsqrt(_head_mean(xv * xv) + EPS)
        xh = xv * r
        dn = dxn * g
        dx = r * (dn - xh * _head_mean(dn * xh))
        return dx, jnp.sum(dxn * xh, axis=0, keepdims=True)

    def body(q_ref, k_ref, gq_ref, gk_ref, c_ref, s_ref, dq_ref, dk_ref, dqo_ref, dko_ref, dgq_ref, dgk_ref):
        @pl.when(pl.program_id(0) == 0)
        def _():
            dgq_ref[...] = jnp.zeros_like(dgq_ref)
            dgk_ref[...] = jnp.zeros_like(dgk_ref)

        c, s = c_ref[...], s_ref[...]
        for j in range(ATT_Q_DIM // 128):
            sl = slice(j * 128, (j + 1) * 128)
            dx, dg = one(q_ref[:, sl], gq_ref[...], c, s, dq_ref[:, sl])
            dqo_ref[:, sl] = dx.astype(dqo_ref.dtype)
            dgq_ref[:, sl] += dg
        dx, dg = one(k_ref[...], gk_ref[...], c, s, dk_ref[...])
        dko_ref[...] = dx.astype(dko_ref.dtype)
        dgk_ref[...] += dg

    vec = pl.BlockSpec((1, 128), lambda i: (0, 0))
    tab = pl.BlockSpec((tr, 128), lambda i: (i, 0))
    qrow = pl.BlockSpec((tr, ATT_Q_DIM), lambda i: (i, 0))
    return pl.pallas_call(
        body, name=name, grid=(n // tr,),
        in_specs=[qrow, pl.BlockSpec((tr, 128), lambda i: (i, OFF_AK // 128)), vec, vec, tab, tab, qrow, tab],
        out_specs=[qrow, tab, pl.BlockSpec((1, ATT_Q_DIM), lambda i: (0, 0)), vec],
        out_shape=[SDS((n, ATT_Q_DIM), MXU_DTYPE), SDS((n, ATT_KV_DIM), MXU_DTYPE), SDS((1, ATT_Q_DIM), F32), SDS((1, 128), F32)],
        compiler_params=_cp("arbitrary"))(p, p, gq, gk, cos, sin, dq, dk)


ATT_TQ = 512
ATT_STEP_HEADS = 2


def _attn_fwd(q, k, v, name):
    n = q.shape[0]
    tq = min(ATT_TQ, n)
    scale = ATT_HEAD_DIM ** -0.5
    gw = ATT_STEP_HEADS * ATT_HEAD_DIM
    parts = ATT_GROUP // ATT_STEP_HEADS

    def body(q_ref, k_ref, v_ref, o_ref):
        kk, vv = k_ref[0], v_ref[0]
        v_ones = jnp.concatenate([vv, jnp.ones_like(vv)], axis=1)
        outs = []
        for g in range(ATT_STEP_HEADS):
            s = _dot(q_ref[:, g * ATT_HEAD_DIM:(g + 1) * ATT_HEAD_DIM] * scale, kk, "nt")
            e = jnp.exp(s - jnp.max(s, axis=-1, keepdims=True))
            ov = _dot(e, v_ones)
            outs.append(ov[:, :ATT_HEAD_DIM] / ov[:, ATT_HEAD_DIM:])
        o_ref[...] = jnp.concatenate(outs, axis=-1).astype(o_ref.dtype)

    kv = pl.BlockSpec((1, n, ATT_HEAD_DIM), lambda h, i, pr: (h, 0, 0))
    qb = pl.BlockSpec((tq, gw), lambda h, i, pr: (i, h * parts + pr))
    return pl.pallas_call(
        body, name=name, grid=(ATT_KV_HEADS, n // tq, parts), in_specs=[qb, kv, kv],
        out_specs=qb, out_shape=SDS((n, ATT_Q_DIM), MXU_DTYPE),
        compiler_params=_cp("parallel", "parallel", "parallel"))(q, k, v)


def _attn_bwd(q, k, v, o, do, name):
    n = q.shape[0]
    tq = min(ATT_TQ, n)
    scale = ATT_HEAD_DIM ** -0.5
    gw = ATT_STEP_HEADS * ATT_HEAD_DIM
    parts = ATT_GROUP // ATT_STEP_HEADS

    def body(q_ref, k_ref, v_ref, o_ref, do_ref, dq_ref, dk_ref, dv_ref):
        @pl.when(jnp.logical_and(pl.program_id(1) == 0, pl.program_id(2) == 0))
        def _():
            dk_ref[...] = jnp.zeros_like(dk_ref)
            dv_ref[...] = jnp.zeros_like(dv_ref)

        kk, vv = k_ref[0], v_ref[0]
        dqs = []
        dk_acc = jnp.zeros((ATT_HEAD_DIM, n), F32)
        dv_acc = jnp.zeros((ATT_HEAD_DIM, n), F32)
        for g in range(ATT_STEP_HEADS):
            sl = slice(g * ATT_HEAD_DIM, (g + 1) * ATT_HEAD_DIM)
            qg, dog = q_ref[:, sl] * scale, do_ref[:, sl].astype(F32)
            s = _dot(qg, kk, "nt")
            e = jnp.exp(s - jnp.max(s, axis=-1, keepdims=True))
            inv = 1.0 / jnp.sum(e, axis=-1, keepdims=True)
            delta = jnp.sum(dog * o_ref[:, sl].astype(F32), axis=-1, keepdims=True)
            dse = e * (_dot(dog, vv, "nt") - delta)
            dqs.append(_dot(dse, kk) * (inv * scale))
            dk_acc += _dot(qg.astype(F32) * inv, dse, "tn")
            dv_acc += _dot(dog * inv, e, "tn")
        dq_ref[...] = jnp.concatenate(dqs, axis=-1)
        dk_ref[0] += dk_acc
        dv_ref[0] += dv_acc

    kv = pl.BlockSpec((1, n, ATT_HEAD_DIM), lambda h, i, pr: (h, 0, 0))
    kvt = pl.BlockSpec((1, ATT_HEAD_DIM, n), lambda h, i, pr: (h, 0, 0))
    qb = pl.BlockSpec((tq, gw), lambda h, i, pr: (i, h * parts + pr))
    return pl.pallas_call(
        body, name=name, grid=(ATT_KV_HEADS, n // tq, parts), in_specs=[qb, kv, kv, qb, qb], out_specs=[qb, kvt, kvt],
        out_shape=[SDS((n, ATT_Q_DIM), F32), SDS((ATT_KV_HEADS, ATT_HEAD_DIM, n), F32), SDS((ATT_KV_HEADS, ATT_HEAD_DIM, n), F32)],
        compiler_params=_cp("parallel", "arbitrary", "arbitrary"))(q, k, v, o, do)


def _both_directions(mats, axis):
    fwd = np.concatenate(mats, axis=axis).astype(np.float32)
    bwd = np.concatenate([m[::-1, ::-1] for m in mats], axis=axis).astype(np.float32)
    return jnp.asarray(np.stack([fwd, bwd]), MXU_DTYPE)


def _hg_segments():
    c = HG_CHUNK
    t = np.arange(c)[:, None]
    r = np.arange(c)[None, :]
    mats = [(r <= t)]
    for lev in range(HG_LEVELS):
        h = c >> (lev + 1)
        mid = (t // (2 * h)) * (2 * h) + h - 1
        hi = (t // h) % 2 == 1
        mats.append(np.where(hi, (r > mid) & (r <= t), (r > t) & (r <= mid)))
    mats.append(r > t)
    return _both_directions(mats, 0)


def _hg_pair_sums():
    c = HG_CHUNK
    r = np.arange(c)[:, None]
    t = np.arange(c)[None, :]
    gp, gn = [t >= r], [t < r]
    for lev in range(HG_LEVELS):
        sh = HG_LEVELS - 1 - lev
        same = (r >> sh) == (t >> sh)
        gp.append(same & (t >= r))
        gn.append(same & (t < r))
    return _both_directions(gp, 1), _both_directions(gn, 1)


def _split_dot(mat, x):
    hi = x.astype(MXU_DTYPE)
    lo = (x - hi.astype(F32)).astype(MXU_DTYPE)
    return _dot(mat, hi) + _dot(mat, lo)


def _hg_gates(hq, z, a0, a1):
    q = hq * _sigmoid(hq)
    sg = _sigmoid(z)
    lb = _sigmoid(a0 - a1)
    f = lb + (1.0 - lb) * sg
    k = (1.0 - lb) * (1.0 - sg)
    return q, f, k, sg, lb


def _hg_level_masks():
    c = HG_CHUNK
    t = np.arange(c)
    later, same = [], []
    for lev in range(HG_LEVELS):
        sh = HG_LEVELS - 1 - lev
        later.append(np.broadcast_to((((t >> sh) & 1) == 1)[:, None], (c, HG_HEAD_DIM)))
        same.append((t[:, None] >> (sh + 1)) == (t[None, :] >> (sh + 1)))
    same.append(t[:, None] == t[None, :])
    later = np.stack(later).astype(np.float32)
    return jnp.asarray(np.stack([later, 1.0 - later]), F32), jnp.asarray(np.stack(same).astype(np.float32), F32)


def _hg_level(q, k, ex, later_ref, lev):
    e = ex[lev + 1]
    e_q = e * later_ref[0, lev]
    e_k = e - e_q
    return q * e_q, k * e_k, e_q, e_k


def _hg_intra(q, k, ex, later_ref, same_ref):
    a = same_ref[HG_LEVELS] * jnp.sum(q * k, axis=-1, keepdims=True)
    for lev in range(HG_LEVELS):
        qs, ks, _, _ = _hg_level(q, k, ex, later_ref, lev)
        a = a + same_ref[lev] * _dot(qs, ks, "nt")
    return a


def _hg_specs(n, with_time):
    c = HG_CHUNK
    nc = n // c

    def chunk(d, i):
        first = d if with_time else 1 - d
        return i + first * (nc - 1 - 2 * i)

    def pcols(off, dir_stride=0):
        return [pl.BlockSpec((c, HG_PAIR), lambda d, i, j=j: (chunk(d, i), off // HG_PAIR + dir_stride // HG_PAIR * d + j)) for j in range(2)]

    specs = dict(
        hq=pcols(OFF_HQ), v=pcols(OFF_HI), z=pcols(OFF_ZF, OFF_ZB - OFF_ZF),
        shared=pl.BlockSpec((c, HG_DIM), lambda d, i: (chunk(d, i), 0)),
        per_dir=pl.BlockSpec((1, c, HG_DIM), lambda d, i: (d, chunk(d, i), 0)),
        vec=pl.BlockSpec((1, 1, HG_DIM), lambda d, i: (d, 0, 0)),
        seg=pl.BlockSpec((1, (HG_LEVELS + 2) * c, c), lambda d, i: (d, 0, 0)),
        sums=pl.BlockSpec((1, c, (HG_LEVELS + 1) * c), lambda d, i: (d, 0, 0)),
        later=pl.BlockSpec((1, HG_LEVELS, c, HG_HEAD_DIM), lambda d, i: (d, 0, 0, 0)),
        same=pl.BlockSpec((HG_LEVELS + 1, c, c), lambda d, i: (0, 0, 0)),
        state=pl.BlockSpec((1, HG_HEADS, 1, HG_HEAD_DIM, HG_HEAD_DIM), lambda d, i: (d, 0, chunk(d, i), 0, 0)),
        weights=pl.BlockSpec((1, HG_HEADS, 1, c, c), lambda d, i: (d, 0, chunk(d, i), 0, 0)),
        levels=pl.BlockSpec((1, HG_HEADS, 1, HG_LEVELS, c, HG_HEAD_DIM), lambda d, i: (d, 0, chunk(d, i), 0, 0, 0)),
        kept=pl.BlockSpec((1, HG_KEPT, c, HG_DIM), lambda d, i: (d, 0, chunk(d, i), 0)))
    return nc, specs


def _hg_head(refs, hh):
    off = (hh % 2) * HG_HEAD_DIM
    return refs[hh // 2][:, off:off + HG_HEAD_DIM]


def _hg_lanes(hh):
    return slice(hh * HG_HEAD_DIM, (hh + 1) * HG_HEAD_DIM)


def _hg_exps(seg_ref, f):
    c = HG_CHUNK
    args = _split_dot(seg_ref[0], jnp.log(f))
    return [jnp.exp(args[j * c:(j + 1) * c]) for j in range(HG_LEVELS + 2)]


def _hg_last_row(a, mirrored):
    return jnp.where(mirrored, a[0:1, :], a[HG_CHUNK - 1:HG_CHUNK, :])


def _hgrn_fwd(p, a0, a1, seg, masks, name):
    n = p.shape[0]
    nc, sp = _hg_specs(n, True)

    def body(hq0, hq1, z0, z1, v0, v1, a0_ref, a1_ref, seg_ref, later_ref, same_ref, o_ref, s0_ref, a_ref, e_ref, g_ref, st):
        @pl.when(pl.program_id(1) == 0)
        def _():
            st[...] = jnp.zeros_like(st)

        mirrored = pl.program_id(0) == 1
        for hh in range(HG_HEADS):
            ln = _hg_lanes(hh)
            hqv = _hg_head((hq0, hq1), hh)
            q, f, k, sg, _ = _hg_gates(hqv, _hg_head((z0, z1), hh), a0_ref[0, :, ln], a1_ref[0, :, ln])
            vv = _hg_head((v0, v1), hh)
            ex = _hg_exps(seg_ref, f)
            for lev in range(HG_LEVELS):
                e_ref[0, hh, 0, lev] = ex[lev + 1].astype(e_ref.dtype)
            sq = _sigmoid(hqv)
            for j, kept in enumerate((q, k, f, sg, sq * (1.0 + hqv * (1.0 - sq)), ex[0], ex[HG_LEVELS + 1])):
                g_ref[0, j, :, ln] = kept
            a = _hg_intra(q, k, ex, later_ref, same_ref).astype(MXU_DTYPE)
            a_ref[0, hh, 0] = a
            s_t = st[hh]
            s0_ref[0, hh, 0] = s_t
            o_ref[0, :, ln] = _dot(a, vv) + _dot(q * ex[0], s_t, "nt")
            st[hh] = s_t * _hg_last_row(ex[0], mirrored) + _dot(vv, k * ex[HG_LEVELS + 1], "tn")

    return pl.pallas_call(
        body, name=name, grid=(2, nc), in_specs=sp["hq"] + sp["z"] + sp["v"] + [sp["vec"], sp["vec"], sp["seg"], sp["later"], sp["same"]],
        out_specs=[sp["per_dir"], sp["state"], sp["weights"], sp["levels"], sp["kept"]],
        out_shape=[SDS((2, n, HG_DIM), F32), SDS((2, HG_HEADS, nc, HG_HEAD_DIM, HG_HEAD_DIM), F32),
                   SDS((2, HG_HEADS, nc, HG_CHUNK, HG_CHUNK), MXU_DTYPE),
                   SDS((2, HG_HEADS, nc, HG_LEVELS, HG_CHUNK, HG_HEAD_DIM), MXU_DTYPE), SDS((2, HG_KEPT, n, HG_DIM), F32)],
        scratch_shapes=[pltpu.VMEM((HG_HEADS, HG_HEAD_DIM, HG_HEAD_DIM), F32)],
        compiler_params=_cp("parallel", "arbitrary"))(p, p, p, p, p, p, a0, a1, seg, *masks)


def _hgrn_bwd(p, a0, a1, masks, gp, gn, do, s0, a, e, kept, name):
    n = p.shape[0]
    nc, sp = _hg_specs(n, False)


    def body(v0, v1, a0_ref, a1_ref, later_ref, same_ref, gp_ref, gn_ref, do_ref, s0_ref, a_ref, e_ref, g_ref,
             dhq_ref, dz_ref, dv_ref, dlb_ref, rt):
        @pl.when(pl.program_id(1) == 0)
        def _():
            rt[...] = jnp.zeros_like(rt)
            dlb_ref[...] = jnp.zeros_like(dlb_ref)

        mirrored = pl.program_id(0) == 1
        for hh in range(HG_HEADS):
            ln = _hg_lanes(hh)
            q, k, f, sg, dsilu, e_first, e_last = (g_ref[0, j, :, ln] for j in range(HG_KEPT))
            lb = _sigmoid(a0_ref[0, :, ln] - a1_ref[0, :, ln])
            vv, dov = _hg_head((v0, v1), hh), do_ref[:, ln]
            ex = [e_first] + [e_ref[0, hh, 0, lev].astype(F32) for lev in range(HG_LEVELS)] + [e_last]
            a = a_ref[0, hh, 0]
            da = _dot(dov, vv, "nt")
            diag = jnp.sum(dov * vv, axis=-1, keepdims=True)
            s_t = s0_ref[0, hh, 0]
            r_t = rt[hh]
            k_end = k * ex[HG_LEVELS + 1]
            dv_ref[0, :, ln] = _dot(a, dov, "tn") + _dot(k_end, r_t, "nt")
            dq_inter = ex[0] * _dot(dov, s_t)
            dk_inter = ex[HG_LEVELS + 1] * _dot(vv, r_t)
            dq = diag * k + dq_inter
            dk = diag * q + dk_inter
            q_terms, k_terms = [q * dq_inter], [k * dk_inter]
            for lev in range(HG_LEVELS):
                qs, ks, e_q, e_k = _hg_level(q, k, ex, later_ref, lev)
                pairs = da * same_ref[lev]
                q_part = e_q * _dot(pairs, ks)
                k_part = e_k * _dot(pairs, qs, "tn")
                dq, dk = dq + q_part, dk + k_part
                q_terms.append(q * q_part)
                k_terms.append(k * k_part)
            decay = _hg_last_row(ex[0], mirrored)
            rt[hh] = r_t * decay + _dot(dov, q * ex[0], "tn")
            later = decay * jnp.sum(s_t * r_t, axis=0, keepdims=True)
            dlf = _dot(gp_ref[0], jnp.concatenate(q_terms, axis=0)) + _dot(gn_ref[0], jnp.concatenate(k_terms, axis=0)) + later
            df = dlf / f - dk
            dz_ref[0, :, ln] = df * (1.0 - lb) * sg * (1.0 - sg)
            dlb_ref[0, :, ln] += jnp.sum(df * (1.0 - sg), axis=0, keepdims=True)
            dhq_ref[0, :, ln] = dq * dsilu

    out = SDS((2, n, HG_DIM), F32)
    return pl.pallas_call(
        body, name=name, grid=(2, nc),
        in_specs=sp["v"] + [sp["vec"], sp["vec"], sp["later"], sp["same"], sp["sums"], sp["sums"],
                            sp["shared"], sp["state"], sp["weights"], sp["levels"], sp["kept"]],
        out_specs=[sp["per_dir"], sp["per_dir"], sp["per_dir"], sp["vec"]], out_shape=[out, out, out, SDS((2, 1, HG_DIM), F32)],
        scratch_shapes=[pltpu.VMEM((HG_HEADS, HG_HEAD_DIM, HG_HEAD_DIM), F32)],
        compiler_params=_cp("parallel", "arbitrary"))(p, p, a0, a1, *masks, gp, gn, do, s0, a, e, kept)


def _hg_post(o2, p, g, name):
    n = p.shape[0]
    tr = min(ROW_TILE, n)
    w = 2 * HG_HEAD_DIM

    def body(of_ref, ob_ref, hg_ref, g_ref, o_ref):
        for j in range(2):
            sl = slice(j * HG_HEAD_DIM, (j + 1) * HG_HEAD_DIM)
            o = of_ref[0, :, sl] + ob_ref[0, :, sl]
            hg = hg_ref[:, sl]
            o_ref[:, sl] = (o * _rstd(o) * g_ref[...] * (hg * _sigmoid(hg))).astype(o_ref.dtype)

    blk = pl.BlockSpec((tr, w), lambda i, j: (i, j))
    dirs = [pl.BlockSpec((1, tr, w), lambda i, j, d=d: (d, i, j)) for d in range(2)]
    return pl.pallas_call(
        body, name=name, grid=(n // tr, HG_DIM // w),
        in_specs=dirs + [pl.BlockSpec((tr, w), lambda i, j: (i, OFF_HG // w + j)), pl.BlockSpec((1, HG_HEAD_DIM), lambda i, j: (0, 0))],
        out_specs=blk, out_shape=SDS((n, HG_DIM), MXU_DTYPE), compiler_params=_cp("parallel", "parallel"))(o2, o2, p, g)


def _hg_post_bwd(o2, p, g, dcat, name, after=()):
    n = p.shape[0]
    tr = min(ROW_TILE, n)
    w = 2 * HG_HEAD_DIM
    after, after_specs = _unread(after)

    def body(of_ref, ob_ref, hg_ref, g_ref, d_ref, *rest):
        do_ref, dhg_ref, dg_ref = rest[len(after):]

        @pl.when(pl.program_id(1) == 0)
        def _():
            dg_ref[...] = jnp.zeros_like(dg_ref)

        for j in range(2):
            sl = slice(j * HG_HEAD_DIM, (j + 1) * HG_HEAD_DIM)
            o = of_ref[0, :, sl] + ob_ref[0, :, sl]
            hg = hg_ref[:, sl]
            d = d_ref[:, sl].astype(F32)
            sg = _sigmoid(hg)
            on = o * _rstd(o) * g_ref[...]
            dhg_ref[:, sl] = (d * on * sg * (1.0 + hg * (1.0 - sg))).astype(dhg_ref.dtype)
            dx, dg = _rms_bwd(o, g_ref[...], d * hg * sg)
            do_ref[:, sl] = dx
            dg_ref[0, :, sl] += dg

    blk = pl.BlockSpec((tr, w), lambda j, i: (i, j))
    dirs = [pl.BlockSpec((1, tr, w), lambda j, i, d=d: (d, i, j)) for d in range(2)]
    return pl.pallas_call(
        body, name=name, grid=(HG_DIM // w, n // tr),
        in_specs=dirs + [pl.BlockSpec((tr, w), lambda j, i: (i, OFF_HG // w + j)), pl.BlockSpec((1, HG_HEAD_DIM), lambda j, i: (0, 0)),
                         pl.BlockSpec((tr, w), lambda j, i: (i, ATT_Q_DIM // w + j))] + after_specs,
        out_specs=[blk, blk, pl.BlockSpec((1, 1, w), lambda j, i: (j, 0, 0))],
        out_shape=[SDS((n, HG_DIM), F32), SDS((n, HG_DIM), MXU_DTYPE), SDS((HG_DIM // w, 1, w), F32)],
        compiler_params=_cp("parallel", "arbitrary"))(o2, o2, p, g, dcat, *after)


XATT_TQ = 512


def _xattn_fwd(q, kv, name):
    n, nm = q.shape[0], kv.shape[0]
    tq = min(XATT_TQ, n)
    scale = X_HEAD_DIM ** -0.5

    def body(q_ref, k_ref, v_ref, o_ref):
        s = _dot(q_ref[...], k_ref[...], "nt") * scale
        e = jnp.exp(s - jnp.max(s, axis=-1, keepdims=True))
        o_ref[...] = _dot(e / jnp.sum(e, axis=-1, keepdims=True), v_ref[...]).astype(o_ref.dtype)

    qb = pl.BlockSpec((tq, X_HEAD_DIM), lambda h, i: (i, h))
    return pl.pallas_call(
        body, name=name, grid=(X_HEADS, n // tq),
        in_specs=[qb, pl.BlockSpec((nm, X_HEAD_DIM), lambda h, i: (0, h)), pl.BlockSpec((nm, X_HEAD_DIM), lambda h, i: (0, X_HEADS + h))],
        out_specs=qb, out_shape=SDS(q.shape, MXU_DTYPE), compiler_params=_cp("parallel", "parallel"))(q, kv, kv)


def _xattn_bwd(q, kv, do, name, after=()):
    n, nm = q.shape[0], kv.shape[0]
    tq = min(XATT_TQ, n)
    scale = X_HEAD_DIM ** -0.5
    after, after_specs = _unread(after)

    def body(q_ref, k_ref, v_ref, do_ref, *rest):
        dq_ref, dk_ref, dv_ref = rest[len(after):]

        @pl.when(pl.program_id(1) == 0)
        def _():
            dk_ref[...] = jnp.zeros_like(dk_ref)
            dv_ref[...] = jnp.zeros_like(dv_ref)

        qv, dov = q_ref[...], do_ref[...]
        s = _dot(qv, k_ref[...], "nt") * scale
        e = jnp.exp(s - jnp.max(s, axis=-1, keepdims=True))
        p = e / jnp.sum(e, axis=-1, keepdims=True)
        dp = _dot(dov, v_ref[...], "nt")
        ds = p * (dp - jnp.sum(p * dp, axis=-1, keepdims=True)) * scale
        dq_ref[...] = _dot(ds, k_ref[...]).astype(dq_ref.dtype)
        dk_ref[...] += _dot(ds, qv, "tn")
        dv_ref[...] += _dot(p, dov, "tn")

    qb = pl.BlockSpec((tq, X_HEAD_DIM), lambda h, i: (i, h))
    kb = pl.BlockSpec((nm, X_HEAD_DIM), lambda h, i: (0, h))
    return pl.pallas_call(
        body, name=name, grid=(X_HEADS, n // tq),
        in_specs=[qb, kb, pl.BlockSpec((nm, X_HEAD_DIM), lambda h, i: (0, X_HEADS + h)), qb] + after_specs, out_specs=[qb, kb, kb],
        out_shape=[SDS(q.shape, MXU_DTYPE), SDS((nm, X_HEADS * X_HEAD_DIM), F32), SDS((nm, X_HEADS * X_HEAD_DIM), F32)],
        compiler_params=_cp("parallel", "arbitrary"))(q, kv, kv, do, *after)


def _edge_rows(shape):
    row = lax.broadcasted_iota(jnp.int32, shape, 0)
    return row == 0, row == shape[0] - 1


def _shift_rows(u, down, edges):
    if down:
        return jnp.where(edges[0], 0.0, pltpu.roll(u, 1, axis=0))
    return jnp.where(edges[1], 0.0, pltpu.roll(u, u.shape[0] - 1, axis=0))


def _conv(u, w, b, edges):
    return b + _shift_rows(u, True, edges) * w[0:1, :] + u * w[1:2, :] + _shift_rows(u, False, edges) * w[2:3, :]


def _ff_specs(n):
    gate = lambda rows: pl.BlockSpec((rows, FF_COLS), lambda j: (0, j))
    val = lambda rows: pl.BlockSpec((rows, FF_COLS), lambda j: (0, FF_BLOCKS + j))
    return [gate(n), val(n), gate(3), val(3), gate(1), val(1)], gate


def _conv_gate(u, cw, cb, name):
    n = u.shape[0]
    ins, gate_blk = _ff_specs(n)

    def body(ug_ref, uv_ref, wg_ref, wv_ref, bg_ref, bv_ref, o_ref):
        edges = _edge_rows(ug_ref.shape)
        gate = _conv(ug_ref[...], wg_ref[...], bg_ref[...], edges)
        val = _conv(uv_ref[...], wv_ref[...], bv_ref[...], edges)
        o_ref[...] = (gate * _sigmoid(gate) * val).astype(o_ref.dtype)

    return pl.pallas_call(
        body, name=name, grid=(FF_BLOCKS,), in_specs=ins, out_specs=gate_blk(n), out_shape=SDS((n, D_FF), MXU_DTYPE),
        compiler_params=_cp("parallel"))(u, u, cw, cw, cb, cb)


def _conv_gate_bwd(u, cw, cb, da, name, after=()):
    n = u.shape[0]
    ins, gate_blk = _ff_specs(n)
    after, after_specs = _unread(after)

    def side(dacc, u, w, edges, du_ref, dw_ref, db_ref):
        nxt, prv = _shift_rows(dacc, False, edges), _shift_rows(dacc, True, edges)
        du_ref[...] = (nxt * w[0:1, :] + dacc * w[1:2, :] + prv * w[2:3, :]).astype(du_ref.dtype)
        db_ref[...] = jnp.sum(dacc, axis=0, keepdims=True)
        dw_ref[0:1, :] = jnp.sum(nxt * u, axis=0, keepdims=True)
        dw_ref[1:2, :] = jnp.sum(dacc * u, axis=0, keepdims=True)
        dw_ref[2:3, :] = jnp.sum(prv * u, axis=0, keepdims=True)

    def body(ug_ref, uv_ref, wg_ref, wv_ref, bg_ref, bv_ref, da_ref, *rest):
        dug_ref, duv_ref, dwg_ref, dwv_ref, dbg_ref, dbv_ref = rest[len(after):]
        ug, uv = ug_ref[...], uv_ref[...]
        edges = _edge_rows(ug.shape)
        gate = _conv(ug, wg_ref[...], bg_ref[...], edges)
        val = _conv(uv, wv_ref[...], bv_ref[...], edges)
        sg = _sigmoid(gate)
        dav = da_ref[...].astype(F32)
        side(dav * val * sg * (1.0 + gate * (1.0 - sg)), ug, wg_ref[...], edges, dug_ref, dwg_ref, dbg_ref)
        side(dav * gate * sg, uv, wv_ref[...], edges, duv_ref, dwv_ref, dbv_ref)

    return pl.pallas_call(
        body, name=name, grid=(FF_BLOCKS,), in_specs=ins + [gate_blk(n)] + after_specs,
        out_specs=[gate_blk(n), gate_blk(n), gate_blk(3), gate_blk(3), gate_blk(1), gate_blk(1)],
        out_shape=[SDS((n, D_FF), MXU_DTYPE)] * 2 + [SDS((3, D_FF), F32)] * 2 + [SDS((1, D_FF), F32)] * 2,
        compiler_params=_cp("parallel"))(u, u, cw, cw, cb, cb, da, *after)


def _adamw(w, g, m, v, name):
    r, c = w.shape[-2:]
    tr = _row_tile(r, ELEMENTWISE_ROWS)
    assert w.ndim == 2 or w.shape[:-2] == (1,), (name, w.shape)

    def body(w_ref, g_ref, m_ref, v_ref, d_ref, mo_ref, vo_ref, go_ref):
        gv = g_ref[...]
        go_ref[...] = gv
        mn = ADAM_B1 * m_ref[...] + (1.0 - ADAM_B1) * gv
        vn = ADAM_B2 * v_ref[...] + (1.0 - ADAM_B2) * gv * gv
        m_hat = mn / (1.0 - ADAM_B1 ** ADAM_STEP)
        v_hat = vn / (1.0 - ADAM_B2 ** ADAM_STEP)
        d_ref[...] = -ADAM_LR * (m_hat / (jnp.sqrt(v_hat) + ADAM_EPS) + ADAM_WD * w_ref[...])
        mo_ref[...] = mn
        vo_ref[...] = vn

    blk = pl.BlockSpec((tr, c), lambda i: (i, 0)) if w.ndim == 2 else pl.BlockSpec((1, tr, c), lambda i: (0, i, 0))
    out = SDS(w.shape, F32)
    return pl.pallas_call(body, name=name, grid=(r // tr,), in_specs=[blk] * 4, out_specs=[blk] * 4, out_shape=[out] * 4,
                          compiler_params=_cp("parallel"))(w, g, m, v)


ANY = pl.BlockSpec(memory_space=pl.ANY)


def _place():
    x, y, c = lax.axis_index("x"), lax.axis_index("y"), lax.axis_index("c")
    return x, y, c, [(1 - x, y), (x, 1 - y), (1 - x, 1 - y)]


def _gather_shards(shards, name):
    nt = len(shards)

    def body(*refs):
        ins, outs = refs[:nt], refs[nt:2 * nt]
        send, recv, fsend, frecv, osend, orecv = refs[2 * nt:]
        x, y, c, chips = _place()
        me = 2 * x + y

        def half(t, chip, cc):
            h = ins[t].shape[0] // 2
            return outs[t].at[chip, pl.ds(cc * h, h)]

        def ici(t, j):
            cx, cy = chips[j]
            h = ins[t].shape[0] // 2
            return pltpu.make_async_remote_copy(src_ref=ins[t].at[pl.ds(c * h, h)], dst_ref=half(t, me, c),
                                                send_sem=send.at[t, j], recv_sem=recv.at[t, j], device_id=(cx, cy, c), device_id_type=MESH)

        def landed(t, j):
            cx, cy = chips[j]
            blk = half(t, 2 * cx + cy, c)
            return pltpu.make_async_remote_copy(src_ref=blk, dst_ref=blk, send_sem=send.at[t, j], recv_sem=recv.at[t, j],
                                                device_id=(cx, cy, c), device_id_type=MESH)

        def d2d(t, j, cc):
            cx, cy = chips[j]
            blk = half(t, 2 * cx + cy, cc)
            return pltpu.make_async_remote_copy(src_ref=blk, dst_ref=blk, send_sem=fsend.at[t, j], recv_sem=frecv.at[t, j],
                                                device_id=(x, y, 1 - c), device_id_type=MESH)

        own = [pltpu.make_async_remote_copy(src_ref=ins[t], dst_ref=outs[t].at[me], send_sem=osend.at[t], recv_sem=orecv.at[t],
                                            device_id=(x, y, 1 - c), device_id_type=MESH) for t in range(nt)]
        for t in range(nt):
            for j in range(3):
                ici(t, j).start()
        for cp in own:
            cp.start()
        for t in range(nt):
            for j in range(3):
                landed(t, j).wait_recv()
                d2d(t, j, c).start()
        for t in range(nt):
            for j in range(3):
                d2d(t, j, 1 - c).wait_recv()
        for t in range(nt):
            for j in range(3):
                ici(t, j).wait_send()
                d2d(t, j, c).wait_send()
        for cp in own:
            cp.wait()

    return pl.pallas_call(
        body, name=name, in_specs=[ANY] * nt, out_specs=[ANY] * nt,
        out_shape=[SDS((4,) + s.shape, s.dtype) for s in shards],
        scratch_shapes=[pltpu.SemaphoreType.DMA((nt, 3))] * 4 + [pltpu.SemaphoreType.DMA((nt,))] * 2,
        compiler_params=pltpu.CompilerParams(has_side_effects=True))(*shards)


def _join_halves(bufs, name):
    nt = len(bufs)

    def body(*refs):
        outs = refs[nt:2 * nt]
        send, recv = refs[2 * nt:]
        x, y, c, _ = _place()
        cps = [pltpu.make_async_remote_copy(src_ref=outs[t].at[c], dst_ref=outs[t].at[c], send_sem=send.at[t], recv_sem=recv.at[t],
                                            device_id=(x, y, 1 - c), device_id_type=MESH) for t in range(nt)]
        for cp in cps:
            cp.start()
        for t in range(nt):
            theirs = outs[t].at[1 - c]
            pltpu.make_async_remote_copy(src_ref=theirs, dst_ref=theirs, send_sem=send.at[t], recv_sem=recv.at[t],
                                         device_id=(x, y, 1 - c), device_id_type=MESH).wait_recv()
        for cp in cps:
            cp.wait_send()

    return pl.pallas_call(
        body, name=name, in_specs=[ANY] * nt, out_specs=[ANY] * nt, out_shape=[SDS(b.shape, b.dtype) for b in bufs],
        input_output_aliases={t: t for t in range(nt)},
        scratch_shapes=[pltpu.SemaphoreType.DMA((nt,))] * 2,
        compiler_params=pltpu.CompilerParams(has_side_effects=True))(*bufs)


def _exchange_small(v, reduce, name, after=()):
    rows = v.shape[0]
    after, after_specs = _unread(after)

    def body(v_ref, *rest):
        o_ref, buf, send, recv = rest[-4:]
        x, y, c, _ = _place()
        me = 4 * x + 2 * y + c
        buf[me] = v_ref[...]

        def peer(dx, dy, dc):
            return (1 - x if dx else x, 1 - y if dy else y, 1 - c if dc else c)

        peers = [(dx, dy, dc) for dx in range(2) for dy in range(2) for dc in range(2) if (dx, dy, dc) != (0, 0, 0)]
        cps = []
        for j, (dx, dy, dc) in enumerate(peers):
            cps.append(pltpu.make_async_remote_copy(src_ref=v_ref, dst_ref=buf.at[me], send_sem=send.at[j], recv_sem=recv.at[j],
                                                    device_id=peer(dx, dy, dc), device_id_type=MESH))
        for cp in cps:
            cp.start()
        for j, (dx, dy, dc) in enumerate(peers):
            px, py, pc = peer(dx, dy, dc)
            blk = buf.at[4 * px + 2 * py + pc]
            pltpu.make_async_remote_copy(src_ref=blk, dst_ref=blk, send_sem=send.at[j], recv_sem=recv.at[j],
                                         device_id=(px, py, pc), device_id_type=MESH).wait_recv()
        for cp in cps:
            cp.wait_send()
        if reduce:
            acc = buf[0]
            for j in range(1, 8):
                acc = acc + buf[j]
            o_ref[...] = acc
        else:
            o_ref[...] = buf[...]

    vm = pl.BlockSpec(memory_space=pltpu.VMEM)
    return pl.pallas_call(
        body, name=name, in_specs=[vm] + after_specs, out_specs=vm, out_shape=SDS((rows, 128) if reduce else (8, rows, 128), F32),
        scratch_shapes=[pltpu.VMEM((8, rows, 128), F32), pltpu.SemaphoreType.DMA((7,)), pltpu.SemaphoreType.DMA((7,))],
        compiler_params=pltpu.CompilerParams(has_side_effects=True))(v, *after)


HBM = pl.BlockSpec(memory_space=pltpu.HBM)
SEM = pl.BlockSpec(memory_space=pltpu.SEMAPHORE)
TOKEN = pl.BlockSpec(memory_space=pltpu.VMEM)
TOKEN_SHAPE = SDS((8, 128), F32)
PEERS = 7


def _in_hbm(a):
    return pltpu.with_memory_space_constraint(a, pltpu.HBM)


def _split_params():
    return pltpu.CompilerParams(has_side_effects=pltpu.SideEffectType.DATAFLOW_SIDE_EFFECTING)


def _gather_start(shards, name, after=()):
    nt = len(shards)
    after, after_specs = _unread(after)

    def body(*refs):
        ins, lands = refs[:nt], refs[nt:2 * nt]
        outs = refs[2 * nt + len(after):]
        sends, recvs = outs[:nt], outs[nt:2 * nt]
        x, y, c, chips = _place()
        me = 2 * x + y
        for t in range(nt):
            h = ins[t].shape[0] // 2
            mine = pl.ds(c * h, h)
            for j, (cx, cy) in enumerate(chips):
                for dc in range(2):
                    pltpu.make_async_remote_copy(src_ref=ins[t].at[mine], dst_ref=lands[t].at[me, mine], send_sem=sends[t].at[2 * j + dc],
                                                 recv_sem=recvs[t].at[2 * j + c], device_id=(cx, cy, dc), device_id_type=MESH).start()
            pltpu.make_async_remote_copy(src_ref=ins[t], dst_ref=lands[t].at[me], send_sem=sends[t].at[PEERS - 1], recv_sem=recvs[t].at[PEERS - 1],
                                         device_id=(x, y, 1 - c), device_id_type=MESH).start()
        outs[-1][...] = jnp.zeros(TOKEN_SHAPE.shape, F32)

    lands = [lax.empty((4,) + s.shape, s.dtype) for s in shards]
    out = pl.pallas_call(
        body, name=name, in_specs=[HBM] * (2 * nt) + after_specs, out_specs=[SEM] * (2 * nt) + [HBM] * (2 * nt) + [TOKEN],
        out_shape=[pltpu.SemaphoreType.DMA((PEERS,))] * (2 * nt)
        + [pltpu.HBM(s.shape, s.dtype) for s in shards] + [pltpu.HBM(l.shape, l.dtype) for l in lands] + [TOKEN_SHAPE],
        input_output_aliases={t: 2 * nt + t for t in range(2 * nt)}, compiler_params=_split_params())(
            *[_in_hbm(s) for s in shards], *[_in_hbm(l) for l in lands], *after)
    return out[:nt], out[nt:2 * nt], out[2 * nt:3 * nt], out[3 * nt:4 * nt], out[-1]


def _gather_wait(sends, recvs, shards, lands, after, name):
    nt = len(shards)

    def body(*refs):
        ins, lands_ref = refs[:nt], refs[nt:2 * nt]
        send_refs, recv_refs = refs[2 * nt:3 * nt], refs[3 * nt:4 * nt]
        x, y, c, chips = _place()
        for t in range(nt):
            h = ins[t].shape[0] // 2
            for j, (cx, cy) in enumerate(chips):
                for cs in range(2):
                    blk = lands_ref[t].at[2 * cx + cy, pl.ds(cs * h, h)]
                    pltpu.make_async_remote_copy(src_ref=blk, dst_ref=blk, send_sem=send_refs[t].at[2 * j + cs], recv_sem=recv_refs[t].at[2 * j + cs],
                                                 device_id=(cx, cy, cs), device_id_type=MESH).wait()
            blk = lands_ref[t].at[2 * x + y]
            pltpu.make_async_remote_copy(src_ref=blk, dst_ref=blk, send_sem=send_refs[t].at[PEERS - 1], recv_sem=recv_refs[t].at[PEERS - 1],
                                         device_id=(x, y, 1 - c), device_id_type=MESH).wait()

    out = pl.pallas_call(
        body, name=name, in_specs=[HBM] * (2 * nt) + [SEM] * (2 * nt) + [ANY], out_specs=[HBM] * (2 * nt),
        out_shape=[pltpu.HBM(s.shape, s.dtype) for s in shards] + [pltpu.HBM(l.shape, l.dtype) for l in lands],
        input_output_aliases={t: t for t in range(2 * nt)}, compiler_params=_split_params())(*shards, *lands, *sends, *recvs, after)
    return out[nt:]


def _scatter_start(g, name):
    _, r, c_ = g.shape
    h = r // 2

    def body(g_ref, land, send, recv, g_thru, land_thru, token):
        x, y, c, chips = _place()
        for j, (cx, cy) in enumerate(chips):
            for dc in range(2):
                pltpu.make_async_remote_copy(src_ref=g_ref.at[2 * cx + cy, pl.ds(dc * h, h)], dst_ref=land.at[2 * j + c], send_sem=send.at[2 * j + dc],
                                             recv_sem=recv.at[2 * j + c], device_id=(cx, cy, dc), device_id_type=MESH).start()
        pltpu.make_async_remote_copy(src_ref=g_ref.at[2 * x + y, pl.ds((1 - c) * h, h)], dst_ref=land.at[PEERS - 1], send_sem=send.at[PEERS - 1],
                                     recv_sem=recv.at[PEERS - 1], device_id=(x, y, 1 - c), device_id_type=MESH).start()
        token[...] = jnp.zeros(TOKEN_SHAPE.shape, F32)

    land = lax.empty((PEERS, h, c_), g.dtype)
    return pl.pallas_call(
        body, name=name, in_specs=[HBM, HBM], out_specs=[SEM, SEM, HBM, HBM, TOKEN],
        out_shape=[pltpu.SemaphoreType.DMA((PEERS,)), pltpu.SemaphoreType.DMA((PEERS,)), pltpu.HBM(g.shape, g.dtype),
                   pltpu.HBM(land.shape, land.dtype), TOKEN_SHAPE],
        input_output_aliases={0: 2, 1: 3}, compiler_params=_split_params())(_in_hbm(g), _in_hbm(land))


def _scatter_wait(started, after, name):
    nt = len(started)

    def body(*refs):
        lands = refs[nt:2 * nt]
        sends, recvs = refs[2 * nt:3 * nt], refs[3 * nt:4 * nt]
        x, y, c, chips = _place()
        peers = [(cx, cy, dc) for cx, cy in chips for dc in range(2)] + [(x, y, 1 - c)]
        for t in range(nt):
            for k, peer in enumerate(peers):
                blk = lands[t].at[k]
                pltpu.make_async_remote_copy(src_ref=blk, dst_ref=blk, send_sem=sends[t].at[k], recv_sem=recvs[t].at[k],
                                             device_id=peer, device_id_type=MESH).wait()

    gs, lands = [s[2] for s in started], [s[3] for s in started]
    after, after_specs = _unread(after)
    out = pl.pallas_call(
        body, name=name, in_specs=[HBM] * (2 * nt) + [SEM] * (2 * nt) + after_specs, out_specs=[HBM] * (2 * nt),
        out_shape=[pltpu.HBM(a.shape, a.dtype) for a in gs + lands],
        input_output_aliases={t: t for t in range(2 * nt)}, compiler_params=_split_params())(
            *gs, *lands, *[s[0] for s in started], *[s[1] for s in started], *after)
    return out[:nt], out[nt:]


def _sum_devices(g, land, me, core, name):
    npeer, h, c = land.shape
    tr = _row_tile(h, 2 * ELEMENTWISE_ROWS)
    steps = h // tr

    def body(ix_ref, own_ref, land_ref, o_ref):
        acc = own_ref[0].astype(F32)
        for j in range(npeer):
            acc = acc + land_ref[j].astype(F32)
        o_ref[0] = acc

    grid_spec = pltpu.PrefetchScalarGridSpec(
        num_scalar_prefetch=1, grid=(steps,),
        in_specs=[pl.BlockSpec((1, tr, c), lambda i, ix: (ix[0], ix[1] * steps + i, 0)), pl.BlockSpec((npeer, tr, c), lambda i, ix: (0, i, 0))],
        out_specs=pl.BlockSpec((1, tr, c), lambda i, ix: (ix[1], i, 0)))
    return pl.pallas_call(body, name=name, grid_spec=grid_spec, out_shape=SDS((2, h, c), F32),
                          compiler_params=_cp("parallel"))(jnp.stack([me, core]), g, land)


def _pack_small(parts):
    flat = jnp.concatenate([p.reshape(-1) for p in parts])
    total = flat.shape[0]
    rows = -(-total // 1024) * 8
    return jnp.pad(flat, (0, rows * 128 - total)).reshape(rows, 128)


def _unpack_small(packed, shapes):
    flat = packed.reshape(-1)
    out, off = [], 0
    for s in shapes:
        size = int(np.prod(s))
        out.append(flat[off:off + size].reshape(s))
        off += size
    return out


def _local_step(x, mem, target, w_in, first_after, mid_weights, ffn_weights, on_grad, gains, conv_w, conv_b, hg_lb):
    n = x.shape[0]
    cos, sin = _rope_tables(n)
    seg = _hg_segments()
    gp, gn = _hg_pair_sums()
    masks = _hg_level_masks()
    gq2 = jnp.tile(gains["q_norm_g"], (1, 2))
    gk2 = jnp.tile(gains["k_norm_g"], (1, 2))
    a0 = hg_lb[:, 0:1, :]
    a1 = hg_lb[:, 1:2, :]

    p, h1 = _norm_mm(x, gains["pre_mix_g"], w_in, F32, TOKEN_TILE, 1664, "in_proj", after=(first_after,))
    qr, kr = _qk_prep(p, gq2, gk2, cos, sin, "qk_prep")
    heads = lambda a: a.reshape(n, ATT_KV_HEADS, ATT_HEAD_DIM).transpose(1, 0, 2)
    kh = heads(kr)
    vh = heads(p[:, OFF_AV:OFF_AV + ATT_KV_DIM].astype(MXU_DTYPE))
    att = _attn_fwd(qr, kh, vh, "attn_fwd")
    o2, s0, hg_a, hg_e, hg_kept = _hgrn_fwd(p, a0, a1, seg, masks, "hgrn_fwd")
    rec = _hg_post(o2, p, gains["hg_out_norm_g"], "hg_post")
    cat = jnp.concatenate([att, rec], axis=1)
    w_out, w_xq, w_xkv, w_xo = mid_weights(cat)
    mixed, x1 = _mm_resid_norm(cat, w_out, x, gains["post_mix_g"], 512, "out_proj_resid")
    xq, h2 = _norm_mm(x1, gains["pre_x_g"], w_xq, MXU_DTYPE, TOKEN_TILE, 1024, "xq_proj")
    kv, mn = _norm_mm(mem, gains["mem_norm_g"], w_xkv, MXU_DTYPE, 256, 2048, "xkv_proj")
    ox = _xattn_fwd(xq, kv, "xattn_fwd")
    xo, x2 = _mm_resid_norm(ox, w_xo, x1, gains["post_x_g"], 512, "xo_proj_resid")
    w_up = ffn_weights("w_up", x2)
    u, h3 = _norm_mm(x2, gains["pre_ffn_g"], w_up, F32, TOKEN_TILE, 1408, "up_proj")
    act = _conv_gate(u, conv_w, conv_b, "conv_gate")
    w_down = ffn_weights("w_down", act)
    dn, d3, loss = _mm_resid_norm(act, w_down, x2, gains["post_ffn_g"], 512, "down_proj_resid_loss", target=target)

    gs = {}
    d_act, d_dn, gs["post_ffn_g"] = _norm_bwd_mm(dn, gains["post_ffn_g"], d3, w_down, F32, 512, 1408, "ffn_post_bwd_down_dx")
    tok = on_grad("w_down", _mm(act, d_dn, "tn", WIRE_DTYPE, 1408, 1024, "down_dw"))
    du_g, du_v, dcw_g, dcw_v, dcb_g, dcb_v = _conv_gate_bwd(u, conv_w, conv_b, d_act, "conv_gate_bwd", after=(tok,))
    gs["conv_w"] = jnp.concatenate([dcw_g, dcw_v], axis=1)
    gs["conv_b"] = jnp.concatenate([dcb_g, dcb_v], axis=1)
    ff_shard = w_up.shape[2]
    g_up = _dw_by_owner(h3, du_g, ff_shard, 0, None, 512, "up_dw_gate")
    tok = on_grad("w_up", _dw_by_owner(h3, du_v, ff_shard, 2, g_up, 512, "up_dw_value"))
    d2, gs["pre_ffn_g"] = _dx_norm_bwd([(du_g, 0), (du_g, 1), (du_v, 0), (du_v, 1)], w_up, x2, gains["pre_ffn_g"], d3, 512,
                                       "up_dx_pre_bwd", after=(tok,))
    d_ox, d_xo, gs["post_x_g"] = _norm_bwd_mm(xo, gains["post_x_g"], d2, w_xo, MXU_DTYPE, 512, 1024, "x_post_bwd_xo_dx")
    tok = on_grad("w_xo", _mm(ox, d_xo, "tn", WIRE_DTYPE, 512, 1024, "xo_dw"))
    d_xq, d_k, d_v = _xattn_bwd(xq, kv, d_ox, "xattn_bwd", after=(tok,))
    d_kv = jnp.concatenate([d_k, d_v], axis=1).astype(MXU_DTYPE)
    tok = on_grad("w_xq", _mm(h2, d_xq, "tn", WIRE_DTYPE, 512, 1024, "xq_dw"))
    tok_kv = on_grad("w_xkv", _dw_by_owner(mn, d_kv, w_xkv.shape[2], 0, None, 512, "xkv_dw"))
    d1, gs["pre_x_g"] = _dx_norm_bwd([(d_xq, 0)], w_xq[None], x1, gains["pre_x_g"], d2, 512, "xq_dx_pre_bwd", after=(tok, tok_kv))
    d_mn = _mm_nt_parts([(d_kv, s) for s in range(4)], w_xkv, F32, 256, 1024, "xkv_dx")
    _, gs["mem_norm_g"] = _norm_bwd(mem, gains["mem_norm_g"], d_mn, None, MXU_DTYPE, "mem_norm_bwd")
    d_cat, d_mixed, gs["post_mix_g"] = _norm_bwd_mm(mixed, gains["post_mix_g"], d1, w_out, MXU_DTYPE, 512, 1024, "mix_post_bwd_out_dx")
    tok = on_grad("w_out", _mm(cat, d_mixed, "tn", WIRE_DTYPE, 512, 1024, "out_dw"))
    d_o, d_hg, dg_hg = _hg_post_bwd(o2, p, gains["hg_out_norm_g"], d_cat, "hg_post_bwd", after=(tok,))
    gs["hg_out_norm_g"] = dg_hg.reshape(HG_HEADS, HG_HEAD_DIM).sum(axis=0, keepdims=True)
    dhq2, dz2, dhv2, dlb = _hgrn_bwd(p, a0, a1, masks, gp, gn, d_o, s0, hg_a, hg_e, hg_kept, "hgrn_bwd")
    lb = jax.nn.sigmoid(a0 - a1)
    da0 = dlb * lb * (1.0 - lb)
    gs["hg_lb"] = jnp.concatenate([da0, -da0], axis=1)
    d_qr, d_kh, d_vh = _attn_bwd(qr, kh, vh, cat, d_cat, "attn_bwd")
    unheads = lambda a: a.transpose(2, 0, 1).reshape(n, ATT_KV_DIM)
    d_aq, d_ak, dgq, dgk = _qk_prep_bwd(p, gq2, gk2, cos, sin, d_qr, unheads(d_kh), "qk_prep_bwd")
    gs["q_norm_g"] = dgq.reshape(ATT_HEADS, ATT_HEAD_DIM).sum(axis=0, keepdims=True)
    gs["k_norm_g"] = dgk.reshape(ATT_KV_HEADS, ATT_HEAD_DIM).sum(axis=0, keepdims=True)
    d_p = jnp.concatenate([d_aq, d_ak, unheads(d_vh).astype(MXU_DTYPE), (dhq2[0] + dhq2[1]).astype(MXU_DTYPE),
                           dz2[0].astype(MXU_DTYPE), dz2[1].astype(MXU_DTYPE), (dhv2[0] + dhv2[1]).astype(MXU_DTYPE), d_hg], axis=1)
    tok = on_grad("w_in", _mm(h1, d_p, "tn", WIRE_DTYPE, 512, 1664, "in_dw"))
    grad_x, gs["pre_mix_g"] = _dx_norm_bwd([(d_p, 0)], w_in[None], x, gains["pre_mix_g"], d1, 512, "in_dx_pre_bwd", after=(tok,))
    return loss, grad_x, gs


MATS = ("w_in", "w_out", "w_xq", "w_xkv", "w_xo", "w_up", "w_down")
GAINS = ("pre_mix_g", "q_norm_g", "k_norm_g", "hg_out_norm_g", "post_mix_g", "pre_x_g", "mem_norm_g", "post_x_g", "pre_ffn_g", "post_ffn_g")
WEIGHTS = ('pre_mix_g', 'w_in', 'q_norm_g', 'k_norm_g', 'hg_lb', 'hg_out_norm_g', 'w_out', 'post_mix_g', 'pre_x_g', 'mem_norm_g', 'w_xq',
           'w_xkv', 'w_xo', 'post_x_g', 'pre_ffn_g', 'w_up', 'conv_w', 'conv_b', 'w_down', 'post_ffn_g')


def kernel(x, mem, pre_mix_g, w_in, q_norm_g, k_norm_g, hg_lb, hg_out_norm_g, w_out, post_mix_g, pre_x_g, mem_norm_g, w_xq, w_xkv, w_xo, post_x_g, pre_ffn_g, w_up, conv_w, conv_b, w_down, post_ffn_g, loss_target, m_pre_mix_g, m_w_in, m_q_norm_g, m_k_norm_g, m_hg_lb, m_hg_out_norm_g, m_w_out, m_post_mix_g, m_pre_x_g, m_mem_norm_g, m_w_xq, m_w_xkv, m_w_xo, m_post_x_g, m_pre_ffn_g, m_w_up, m_conv_w, m_conv_b, m_w_down, m_post_ffn_g, v_pre_mix_g, v_w_in, v_q_norm_g, v_k_norm_g, v_hg_lb, v_hg_out_norm_g, v_w_out, v_post_mix_g, v_pre_x_g, v_mem_norm_g, v_w_xq, v_w_xkv, v_w_xo, v_post_x_g, v_pre_ffn_g, v_w_up, v_conv_w, v_conv_b, v_w_down, v_post_ffn_g):
    args = dict(locals())
    w = {k: args[k] for k in WEIGHTS}
    m = {k: args["m_" + k] for k in WEIGHTS}
    v = {k: args["v_" + k] for k in WEIGHTS}
    chip = 2 * lax.axis_index("x") + lax.axis_index("y")
    core = lax.axis_index("c")

    shards = {k: w[k][0].astype(WIRE_DTYPE) for k in MATS}

    def whole(k, g):
        return g if k in ("w_xkv", "w_up") else g.reshape(-1, g.shape[-1])

    w_in_shards = _gather_shards([shards["w_in"]], "gather_w_in")[0]
    w_in_full = jnp.concatenate([w_in_shards[s] for s in range(4)], axis=1)
    small_in = _exchange_small(_pack_small([w["conv_w"][0], w["hg_lb"]]), False, "gather_small")
    mid_names, ffn_names = ("w_out", "w_xq", "w_xkv", "w_xo"), ("w_up", "w_down")
    mid = _gather_start([shards[k] for k in mid_names], "gather_mid_start", after=(w_in_full, small_in))
    ffn = _gather_start([shards[k] for k in ffn_names], "gather_ffn_start", after=(mid[4],))

    def mid_weights(after):
        return [whole(k, g) for k, g in zip(mid_names, _gather_wait(*mid[:4], after, "gather_mid_wait"))]

    def ffn_weights(k, after):
        t = ffn_names.index(k)
        return whole(k, _gather_wait(*[part[t:t + 1] for part in ffn[:4]], after, "gather_wait_" + k)[0])

    cw_parts, lb_parts = [], []
    for s in range(4):
        cw_s, lb_s = _unpack_small(small_in[2 * s], [w["conv_w"][0].shape, w["hg_lb"].shape])
        cw_parts.append(cw_s)
        lb_parts.append(lb_s)
    conv_w_full = jnp.concatenate(cw_parts, axis=1)
    hg_lb_full = jnp.concatenate(lb_parts, axis=2)

    started = {}

    def on_grad(k, g):
        if k == "w_in":
            g = g.reshape(g.shape[0], 4, g.shape[1] // 4).transpose(1, 0, 2)
        elif g.ndim == 2:
            g = g.reshape(4, g.shape[0] // 4, g.shape[1])
        *started[k], token = _scatter_start(g, "grad_start_" + k)
        return token

    gains = {k: w[k] for k in GAINS}
    loss_part, grad_x, gs = _local_step(x[0], mem[0], loss_target[0], w_in_full, ffn[4], mid_weights, ffn_weights, on_grad, gains,
                                        conv_w_full, w["conv_b"], hg_lb_full)
    gs["loss"] = loss_part

    grads, delta, new_m, new_v = {}, {}, {}, {}

    def reduce_matrices(names, after, tag):
        sent, landed = _scatter_wait([started[k] for k in names], after, "grad_wait_" + tag)
        halves = [_sum_devices(g, land, chip, core, "grad_sum_" + k) for k, g, land in zip(names, sent, landed)]
        for k, r in zip(names, _join_halves(halves, "grad_join_" + tag)):
            grads[k] = r.reshape(1, -1, r.shape[-1])

    def adamw(names):
        for k in names:
            shape = w[k].shape
            keep = len(shape) == 3 and shape[0] == 1
            two_d = lambda a: a.reshape(shape) if keep else a.reshape(-1, shape[-1])
            d, mo, vo, go = _adamw(two_d(w[k]), two_d(grads[k]), two_d(m[k]), two_d(v[k]), "adamw_" + k)
            delta[k], new_m[k], new_v[k], grads[k] = d.reshape(shape), mo.reshape(shape), vo.reshape(shape), go.reshape(shape)

    early = tuple(k for k in MATS if k != "w_in")
    reduce_matrices(early, (grad_x,), "early")
    adamw(early)

    small_names = GAINS + ("conv_b", "conv_w", "hg_lb")
    packed = _pack_small([gs[k] for k in small_names + ("loss",)])
    reduced_small = _exchange_small(packed, True, "reduce_small", after=tuple(new_v[k] for k in early))
    *summed, loss = _unpack_small(reduced_small, [gs[k].shape for k in small_names + ("loss",)])
    loss = loss[0, 0]
    for k, g in zip(small_names, summed):
        grads[k] = g
    ncw = w["conv_w"].shape[2]
    grads["conv_w"] = lax.dynamic_slice_in_dim(grads["conv_w"], chip * ncw, ncw, axis=1)[None]
    nlb = w["hg_lb"].shape[2]
    grads["hg_lb"] = lax.dynamic_slice_in_dim(grads["hg_lb"], chip * nlb, nlb, axis=2)
    replicated = GAINS + ("conv_b",)
    shapes = [w[k].shape for k in replicated]
    rows = sum(int(np.prod(s)) for s in shapes) // 128
    pack = lambda d: jnp.concatenate([d[k].reshape(-1) for k in replicated]).reshape(rows, 128)
    outs = _adamw(pack(w), reduced_small[:rows], pack(m), pack(v), "adamw_replicated")
    for into, packed_out in zip((delta, new_m, new_v, grads), outs):
        for k, a in zip(replicated, _unpack_small(packed_out, shapes)):
            into[k] = a
    adamw(("conv_w", "hg_lb"))

    reduce_matrices(("w_in",), tuple(new_v[k] for k in early + small_names), "late")
    adamw(("w_in",))
    return (loss, grad_x[None], *[grads[k] for k in WEIGHTS], *[delta[k] for k in WEIGHTS],
            *[new_m[k] for k in WEIGHTS], *[new_v[k] for k in WEIGHTS])
```

```python
import numpy as np
import jax
import jax.numpy as jnp
from jax import lax
from jax.experimental import pallas as pl
from jax.experimental.pallas import tpu as pltpu

F32 = jnp.float32
MXU_DTYPE = jnp.bfloat16
WIRE_DTYPE = jnp.bfloat16
VMEM_LIMIT_BYTES = 56 * 1024 * 1024
ROWS_PER_16BIT_TILE = 16
ELEMENTWISE_ROWS = 256
EPS = 1e-6
MESH = pl.DeviceIdType.MESH

GRID_W = 64
ATT_HEADS, ATT_KV_HEADS, ATT_HEAD_DIM = 8, 2, 64
ATT_GROUP = ATT_HEADS // ATT_KV_HEADS
ATT_Q_DIM, ATT_KV_DIM = 512, 128
ROPE_THETA = 10000.0
HG_HEADS, HG_HEAD_DIM, HG_DIM = 4, 128, 512
HG_CHUNK = 128
HG_LEVELS = 7
HG_PAIR = 2 * HG_HEAD_DIM
HG_KEPT = 7
X_HEADS, X_HEAD_DIM = 4, 256
D_FF = 2816
FF_COLS = 256
FF_BLOCKS = D_FF // FF_COLS
OFF_AK, OFF_AV, OFF_HQ, OFF_ZF, OFF_ZB, OFF_HI, OFF_HG = 512, 640, 768, 1280, 1792, 2304, 2816

ADAM_LR, ADAM_B1, ADAM_B2, ADAM_EPS, ADAM_WD, ADAM_STEP = 0.001, 0.9, 0.999, 1e-08, 0.01, 10

SDS = jax.ShapeDtypeStruct


def _cp(*sem):
    return pltpu.CompilerParams(dimension_semantics=sem, vmem_limit_bytes=VMEM_LIMIT_BYTES)


def _row_tile(rows, cap):
    if rows <= cap:
        return rows
    return max(t for t in range(ROWS_PER_16BIT_TILE, cap + 1, ROWS_PER_16BIT_TILE) if rows % t == 0)


def _dot(a, b, form="nn"):
    dims = {"nn": (((1,), (0,)), ((), ())), "nt": (((1,), (1,)), ((), ())), "tn": (((0,), (0,)), ((), ()))}[form]
    return lax.dot_general(a.astype(MXU_DTYPE), b.astype(MXU_DTYPE), dims, preferred_element_type=F32)


def _sigmoid(x):
    return 1.0 / (1.0 + jnp.exp(-x))


def _rstd(x):
    return lax.rsqrt(jnp.mean(x * x, axis=-1, keepdims=True) + EPS)


def _rms_bwd(x, g, dy):
    r = _rstd(x)
    xh = x * r
    dn = dy * g
    dx = r * (dn - xh * jnp.mean(dn * xh, axis=-1, keepdims=True))
    return dx, jnp.sum(dy * xh, axis=0, keepdims=True)


def _unread(after):
    after = tuple(a for a in after if a is not None)
    return after, [pl.BlockSpec(memory_space=pl.ANY)] * len(after)


def _mm(a, b, form, out_dtype, tm, tn, name, after=()):
    after, after_specs = _unread(after)
    if form == "nn":
        (m, k), n = a.shape, b.shape[1]
    elif form == "nt":
        (m, k), n = a.shape, b.shape[0]
    else:
        (k, m), n = a.shape, b.shape[1]
    tm, tn = min(tm, m), min(tn, n)
    assert m % tm == 0 and n % tn == 0, (name, m, n, tm, tn)

    def body(a_ref, b_ref, *rest):
        o_ref = rest[-1]
        o_ref[...] = _dot(a_ref[...], b_ref[...], form).astype(o_ref.dtype)

    a_spec = pl.BlockSpec((k, tm), lambda i, j: (0, i)) if form == "tn" else pl.BlockSpec((tm, k), lambda i, j: (i, 0))
    b_spec = pl.BlockSpec((tn, k), lambda i, j: (j, 0)) if form == "nt" else pl.BlockSpec((k, tn), lambda i, j: (0, j))
    return pl.pallas_call(
        body, name=name, grid=(m // tm, n // tn), in_specs=[a_spec, b_spec] + after_specs,
        out_specs=pl.BlockSpec((tm, tn), lambda i, j: (i, j)), out_shape=SDS((m, n), out_dtype),
        compiler_params=_cp("parallel", "parallel"))(a, b, *after)


def _mm_nt_parts(a_parts, b, out_dtype, tm, tn, name, after=()):
    after, after_specs = _unread(after)
    parts, n, p = b.shape
    m = a_parts[0][0].shape[0]
    tm, tn = min(tm, m), min(tn, n)
    assert m % tm == 0 and n % tn == 0 and len(a_parts) == parts, (name, m, b.shape)

    def body(*refs):
        o_ref = refs[-1]
        acc = _dot(refs[0][...], refs[parts][0], "nt")
        for s in range(1, parts):
            acc = acc + _dot(refs[s][...], refs[parts + s][0], "nt")
        o_ref[...] = acc.astype(o_ref.dtype)

    a_specs = [pl.BlockSpec((tm, p), lambda i, j, cb=cb: (i, cb)) for _, cb in a_parts]
    b_specs = [pl.BlockSpec((1, tn, p), lambda i, j, s=s: (s, j, 0)) for s in range(parts)]
    return pl.pallas_call(
        body, name=name, grid=(m // tm, n // tn), in_specs=a_specs + b_specs + after_specs,
        out_specs=pl.BlockSpec((tm, tn), lambda i, j: (i, j)), out_shape=SDS((m, n), out_dtype),
        compiler_params=_cp("parallel", "parallel"))(*[arr for arr, _ in a_parts], *([b] * parts), *after)


def _norm_bwd_mm(y, g, d, w, out_dtype, tm, tn, name):
    n, dm = y.shape
    nn = w.shape[0]
    tm, tn = min(tm, n), min(tn, nn)
    assert n % tm == 0 and nn % tn == 0 and w.shape[1] == dm, (name, y.shape, w.shape)

    def body(y_ref, g_ref, d_ref, w_ref, dx_ref, dy_ref, dg_ref, dys):
        i, j = pl.program_id(0), pl.program_id(1)

        @pl.when(jnp.logical_and(i == 0, j == 0))
        def _():
            dg_ref[...] = jnp.zeros_like(dg_ref)

        @pl.when(j == 0)
        def _():
            dy, dg = _rms_bwd(y_ref[...], g_ref[...], d_ref[...])
            dy = dy.astype(MXU_DTYPE)
            dys[...] = dy
            dy_ref[...] = dy
            dg_ref[...] += dg

        dx_ref[...] = _dot(dys[...], w_ref[...], "nt").astype(dx_ref.dtype)

    row = pl.BlockSpec((tm, dm), lambda i, j: (i, 0))
    vec = pl.BlockSpec((1, dm), lambda i, j: (0, 0))
    return pl.pallas_call(
        body, name=name, grid=(n // tm, nn // tn), in_specs=[row, vec, row, pl.BlockSpec((tn, dm), lambda i, j: (j, 0))],
        out_specs=[pl.BlockSpec((tm, tn), lambda i, j: (i, j)), row, vec],
        out_shape=[SDS((n, nn), out_dtype), SDS((n, dm), MXU_DTYPE), SDS((1, dm), F32)],
        scratch_shapes=[pltpu.VMEM((tm, dm), MXU_DTYPE)],
        compiler_params=_cp("arbitrary", "arbitrary"))(y, g, d, w)


def _mm_resid_norm(a, b, x, g, tm, name, target=None):
    n, k = a.shape
    d = b.shape[1]
    tm = min(tm, n)
    assert n % tm == 0 and x.shape == (n, d), (name, a.shape, b.shape)
    with_loss = target is not None

    def body(a_ref, b_ref, x_ref, g_ref, *rest):
        y = _dot(a_ref[...], b_ref[...])
        out = x_ref[...] + y * _rstd(y) * g_ref[...]
        if not with_loss:
            y_ref, o_ref = rest
            y_ref[...] = y
            o_ref[...] = out
            return
        t_ref, y_ref, d_ref, l_ref = rest
        y_ref[...] = y
        diff = out - t_ref[...]
        d_ref[...] = diff * (1.0 / d)

        @pl.when(pl.program_id(0) == 0)
        def _():
            l_ref[...] = jnp.zeros_like(l_ref)

        l_ref[...] += 0.5 * jnp.sum(jnp.mean(diff * diff, axis=-1, keepdims=True), axis=0, keepdims=True)

    row = pl.BlockSpec((tm, d), lambda i: (i, 0))
    ins = [pl.BlockSpec((tm, k), lambda i: (i, 0)), pl.BlockSpec((k, d), lambda i: (0, 0)), row, pl.BlockSpec((1, d), lambda i: (0, 0))]
    out = SDS((n, d), F32)
    if with_loss:
        return pl.pallas_call(body, name=name, grid=(n // tm,), in_specs=ins + [row], out_specs=[row, row, pl.BlockSpec((1, 1), lambda i: (0, 0))],
                              out_shape=[out, out, SDS((1, 1), F32)], compiler_params=_cp("arbitrary"))(a, b, x, g, target)
    return pl.pallas_call(body, name=name, grid=(n // tm,), in_specs=ins, out_specs=[row, row], out_shape=[out, out],
                          compiler_params=_cp("parallel"))(a, b, x, g)


def _dx_norm_bwd(a_parts, b, x, g, res, tm, name, after=()):
    after, after_specs = _unread(after)
    parts, d, p = b.shape
    n = x.shape[0]
    tm = min(tm, n)
    assert n % tm == 0 and len(a_parts) == parts and x.shape[1] == d, (name, x.shape, b.shape)

    def body(*refs):
        x_ref, g_ref, res_ref = refs[2 * parts:2 * parts + 3]
        dx_ref, dg_ref = refs[-2:]
        dh = _dot(refs[0][...], refs[parts][0], "nt")
        for s in range(1, parts):
            dh = dh + _dot(refs[s][...], refs[parts + s][0], "nt")
        dx, dg = _rms_bwd(x_ref[...], g_ref[...], dh)
        dx_ref[...] = dx + res_ref[...]

        @pl.when(pl.program_id(0) == 0)
        def _():
            dg_ref[...] = jnp.zeros_like(dg_ref)

        dg_ref[...] += dg

    a_specs = [pl.BlockSpec((tm, p), lambda i, cb=cb: (i, cb)) for _, cb in a_parts]
    b_specs = [pl.BlockSpec((1, d, p), lambda i, s=s: (s, 0, 0)) for s in range(parts)]
    row = pl.BlockSpec((tm, d), lambda i: (i, 0))
    vec = pl.BlockSpec((1, d), lambda i: (0, 0))
    return pl.pallas_call(
        body, name=name, grid=(n // tm,), in_specs=a_specs + b_specs + [row, vec, row] + after_specs,
        out_specs=[row, vec], out_shape=[SDS((n, d), F32), SDS((1, d), F32)],
        compiler_params=_cp("arbitrary"))(*[arr for arr, _ in a_parts], *([b] * parts), x, g, res, *after)


def _dw_by_owner(a, b, tn, first, into, tm, name):
    k, m = a.shape
    cnt = b.shape[1] // tn
    tm = min(tm, m)
    assert m % tm == 0 and b.shape[1] == cnt * tn and first + cnt <= 4, (name, a.shape, b.shape)

    def body(a_ref, b_ref, *rest):
        rest[-1][0] = _dot(a_ref[...], b_ref[...], "tn").astype(rest[-1].dtype)

    extra = [] if into is None else [into]
    return pl.pallas_call(
        body, name=name, grid=(m // tm, cnt),
        in_specs=[pl.BlockSpec((k, tm), lambda i, j: (0, i)), pl.BlockSpec((k, tn), lambda i, j: (0, j))] + [pl.BlockSpec(memory_space=pl.ANY)] * len(extra),
        out_specs=pl.BlockSpec((1, tm, tn), lambda i, j: (first + j, i, 0)), out_shape=SDS((4, m, tn), WIRE_DTYPE),
        input_output_aliases={2: 0} if extra else {},
        compiler_params=_cp("parallel", "parallel"))(a, b, *extra)


def _norm_mm(x, g, w, out_dtype, tm, tn, name, after=()):
    after, after_specs = _unread(after)
    m, d = x.shape
    sharded = w.ndim == 3
    n = w.shape[-1] * (w.shape[0] if sharded else 1)
    tm, tn = min(tm, m), (w.shape[-1] if sharded else min(tn, n))
    assert m % tm == 0 and n % tn == 0, (name, m, n, tm, tn)

    def body(x_ref, g_ref, w_ref, *rest):
        o_ref, h_ref, hs = rest[-3:]

        @pl.when(pl.program_id(1) == 0)
        def _():
            xv = x_ref[...]
            h = (xv * _rstd(xv) * g_ref[...]).astype(MXU_DTYPE)
            hs[...] = h
            h_ref[...] = h

        o_ref[...] = _dot(hs[...], w_ref[0] if sharded else w_ref[...]).astype(o_ref.dtype)

    w_spec = pl.BlockSpec((1, d, tn), lambda i, j: (j, 0, 0)) if sharded else pl.BlockSpec((d, tn), lambda i, j: (0, j))
    return pl.pallas_call(
        body, name=name, grid=(m // tm, n // tn),
        in_specs=[pl.BlockSpec((tm, d), lambda i, j: (i, 0)), pl.BlockSpec((1, d), lambda i, j: (0, 0)), w_spec] + after_specs,
        out_specs=[pl.BlockSpec((tm, tn), lambda i, j: (i, j)), pl.BlockSpec((tm, d), lambda i, j: (i, 0))],
        out_shape=[SDS((m, n), out_dtype), SDS((m, d), MXU_DTYPE)],
        scratch_shapes=[pltpu.VMEM((tm, d), MXU_DTYPE)],
        compiler_params=_cp("parallel", "arbitrary"))(x, g, w, *after)


ROW_TILE = 512
TOKEN_TILE = 1024


def _norm_bwd(x, g, dy, res, out_dtype, name):
    n, d = x.shape
    tr = min(ROW_TILE, n)
    has_res = res is not None

    def body(*refs):
        x_ref, g_ref, dy_ref = refs[:3]
        dx_ref, dg_ref = refs[-2:]
        dx, dg = _rms_bwd(x_ref[...], g_ref[...], dy_ref[...].astype(F32))
        if has_res:
            dx = dx + refs[3][...]
        dx_ref[...] = dx.astype(dx_ref.dtype)

        @pl.when(pl.program_id(0) == 0)
        def _():
            dg_ref[...] = jnp.zeros_like(dg_ref)

        dg_ref[...] += dg

    row = pl.BlockSpec((tr, d), lambda i: (i, 0))
    vec = pl.BlockSpec((1, d), lambda i: (0, 0))
    ins = [x, g, dy] + ([res] if has_res else [])
    return pl.pallas_call(
        body, name=name, grid=(n // tr,), in_specs=[row, vec, row] + ([row] if has_res else []),
        out_specs=[row, vec], out_shape=[SDS((n, d), out_dtype), SDS((1, d), F32)],
        compiler_params=_cp("arbitrary"))(*ins)


def _rope_tables(n):
    pairs = ATT_HEAD_DIM // 4
    t = np.arange(n)
    inv = np.power(ROPE_THETA, -np.arange(pairs, dtype=np.float32) / pairs).astype(np.float32)
    ang = np.concatenate([(t // GRID_W)[:, None].astype(np.float32) * inv, (t % GRID_W)[:, None].astype(np.float32) * inv], axis=-1)
    cos = np.repeat(np.cos(ang), 2, axis=-1)
    sin = np.repeat(np.sin(ang), 2, axis=-1) * np.tile(np.array([-1.0, 1.0], np.float32), ATT_HEAD_DIM // 2)
    return jnp.asarray(np.tile(cos, 2), F32), jnp.asarray(np.tile(sin, 2), F32)


def _swap_pairs(x):
    lane = lax.broadcasted_iota(jnp.int32, x.shape, 1)
    return jnp.where((lane & 1) == 0, pltpu.roll(x, 127, axis=1), pltpu.roll(x, 1, axis=1))


def _head_mean(v):
    lane = lax.broadcasted_iota(jnp.int32, v.shape, 1)
    lo = jnp.where(lane < ATT_HEAD_DIM, v, 0.0)
    s0 = jnp.sum(lo, axis=-1, keepdims=True)
    s1 = jnp.sum(v - lo, axis=-1, keepdims=True)
    return jnp.where(lane < ATT_HEAD_DIM, s0, s1) * (1.0 / ATT_HEAD_DIM)


def _qk_prep(p, gq, gk, cos, sin, name):
    n = p.shape[0]
    tr = min(ROW_TILE, n)

    def one(xv, g, c, s):
        xn = xv * lax.rsqrt(_head_mean(xv * xv) + EPS) * g
        return xn * c + _swap_pairs(xn) * s

    def body(q_ref, k_ref, gq_ref, gk_ref, c_ref, s_ref, qo_ref, ko_ref):
        c, s = c_ref[...], s_ref[...]
        for j in range(ATT_Q_DIM // 128):
            qo_ref[:, j * 128:(j + 1) * 128] = one(q_ref[:, j * 128:(j + 1) * 128], gq_ref[...], c, s).astype(qo_ref.dtype)
        ko_ref[...] = one(k_ref[...], gk_ref[...], c, s).astype(ko_ref.dtype)

    vec = pl.BlockSpec((1, 128), lambda i: (0, 0))
    tab = pl.BlockSpec((tr, 128), lambda i: (i, 0))
    return pl.pallas_call(
        body, name=name, grid=(n // tr,),
        in_specs=[pl.BlockSpec((tr, ATT_Q_DIM), lambda i: (i, 0)), pl.BlockSpec((tr, 128), lambda i: (i, OFF_AK // 128)), vec, vec, tab, tab],
        out_specs=[pl.BlockSpec((tr, ATT_Q_DIM), lambda i: (i, 0)), tab],
        out_shape=[SDS((n, ATT_Q_DIM), MXU_DTYPE), SDS((n, ATT_KV_DIM), MXU_DTYPE)],
        compiler_params=_cp("parallel"))(p, p, gq, gk, cos, sin)


def _qk_prep_bwd(p, gq, gk, cos, sin, dq, dk, name):
    n = p.shape[0]
    tr = min(ROW_TILE, n)

    def one(xv, g, c, s, dout):
        dxn = dout * c + _swap_pairs(dout * s)
        r = lax.rsqrt(_head_mean(xv * xv) + EPS)
        xh = xv * r
        dn = dxn * g
        dx = r * (dn - xh * _head_mean(dn * xh))
        return dx, jnp.sum(dxn * xh, axis=0, keepdims=True)

    def body(q_ref, k_ref, gq_ref, gk_ref, c_ref, s_ref, dq_ref, dk_ref, dqo_ref, dko_ref, dgq_ref, dgk_ref):
        @pl.when(pl.program_id(0) == 0)
        def _():
            dgq_ref[...] = jnp.zeros_like(dgq_ref)
            dgk_ref[...] = jnp.zeros_like(dgk_ref)

        c, s = c_ref[...], s_ref[...]
        for j in range(ATT_Q_DIM // 128):
            sl = slice(j * 128, (j + 1) * 128)
            dx, dg = one(q_ref[:, sl], gq_ref[...], c, s, dq_ref[:, sl])
            dqo_ref[:, sl] = dx.astype(dqo_ref.dtype)
            dgq_ref[:, sl] += dg
        dx, dg = one(k_ref[...], gk_ref[...], c, s, dk_ref[...])
        dko_ref[...] = dx.astype(dko_ref.dtype)
        dgk_ref[...] += dg

    vec = pl.BlockSpec((1, 128), lambda i: (0, 0))
    tab = pl.BlockSpec((tr, 128), lambda i: (i, 0))
    qrow = pl.BlockSpec((tr, ATT_Q_DIM), lambda i: (i, 0))
    return pl.pallas_call(
        body, name=name, grid=(n // tr,),
        in_specs=[qrow, pl.BlockSpec((tr, 128), lambda i: (i, OFF_AK // 128)), vec, vec, tab, tab, qrow, tab],
        out_specs=[qrow, tab, pl.BlockSpec((1, ATT_Q_DIM), lambda i: (0, 0)), vec],
        out_shape=[SDS((n, ATT_Q_DIM), MXU_DTYPE), SDS((n, ATT_KV_DIM), MXU_DTYPE), SDS((1, ATT_Q_DIM), F32), SDS((1, 128), F32)],
        compiler_params=_cp("arbitrary"))(p, p, gq, gk, cos, sin, dq, dk)


ATT_FWD_STEP = (256, 4)
ATT_BWD_STEP = (512, 2)


def _attn_fwd(q, k, v, name):
    n = q.shape[0]
    tq, step_heads = min(ATT_FWD_STEP[0], n), ATT_FWD_STEP[1]
    scale = ATT_HEAD_DIM ** -0.5
    gw = step_heads * ATT_HEAD_DIM
    parts = ATT_GROUP // step_heads

    def body(q_ref, k_ref, v_ref, o_ref):
        kk, vv = k_ref[0], v_ref[0]
        v_ones = jnp.concatenate([vv, jnp.ones_like(vv)], axis=1)
        outs = []
        for g in range(step_heads):
            s = _dot(q_ref[:, g * ATT_HEAD_DIM:(g + 1) * ATT_HEAD_DIM] * scale, kk, "nt")
            e = jnp.exp(s - jnp.max(s, axis=-1, keepdims=True))
            ov = _dot(e, v_ones)
            outs.append(ov[:, :ATT_HEAD_DIM] / ov[:, ATT_HEAD_DIM:])
        o_ref[...] = jnp.concatenate(outs, axis=-1).astype(o_ref.dtype)

    kv = pl.BlockSpec((1, n, ATT_HEAD_DIM), lambda h, i, pr: (h, 0, 0))
    qb = pl.BlockSpec((tq, gw), lambda h, i, pr: (i, h * parts + pr))
    return pl.pallas_call(
        body, name=name, grid=(ATT_KV_HEADS, n // tq, parts), in_specs=[qb, kv, kv],
        out_specs=qb, out_shape=SDS((n, ATT_Q_DIM), MXU_DTYPE),
        compiler_params=_cp("parallel", "parallel", "parallel"))(q, k, v)


def _attn_bwd(q, k, v, o, do, name):
    n = q.shape[0]
    tq, step_heads = min(ATT_BWD_STEP[0], n), ATT_BWD_STEP[1]
    scale = ATT_HEAD_DIM ** -0.5
    gw = step_heads * ATT_HEAD_DIM
    parts = ATT_GROUP // step_heads

    def body(q_ref, k_ref, v_ref, o_ref, do_ref, dq_ref, dk_ref, dv_ref):
        @pl.when(jnp.logical_and(pl.program_id(1) == 0, pl.program_id(2) == 0))
        def _():
            dk_ref[...] = jnp.zeros_like(dk_ref)
            dv_ref[...] = jnp.zeros_like(dv_ref)

        kk, vv = k_ref[0], v_ref[0]
        dqs = []
        dk_acc = jnp.zeros((ATT_HEAD_DIM, n), F32)
        dv_acc = jnp.zeros((ATT_HEAD_DIM, n), F32)
        for g in range(step_heads):
            sl = slice(g * ATT_HEAD_DIM, (g + 1) * ATT_HEAD_DIM)
            qg, dog = q_ref[:, sl] * scale, do_ref[:, sl].astype(F32)
            s = _dot(qg, kk, "nt")
            e = jnp.exp(s - jnp.max(s, axis=-1, keepdims=True))
            inv = 1.0 / jnp.sum(e, axis=-1, keepdims=True)
            delta = jnp.sum(dog * o_ref[:, sl].astype(F32), axis=-1, keepdims=True)
            dse = e * (_dot(dog, vv, "nt") - delta)
            dqs.append(_dot(dse, kk) * (inv * scale))
            dk_acc += _dot(qg.astype(F32) * inv, dse, "tn")
            dv_acc += _dot(dog * inv, e, "tn")
        dq_ref[...] = jnp.concatenate(dqs, axis=-1)
        dk_ref[0] += dk_acc
        dv_ref[0] += dv_acc

    kv = pl.BlockSpec((1, n, ATT_HEAD_DIM), lambda h, i, pr: (h, 0, 0))
    kvt = pl.BlockSpec((1, ATT_HEAD_DIM, n), lambda h, i, pr: (h, 0, 0))
    qb = pl.BlockSpec((tq, gw), lambda h, i, pr: (i, h * parts + pr))
    return pl.pallas_call(
        body, name=name, grid=(ATT_KV_HEADS, n // tq, parts), in_specs=[qb, kv, kv, qb, qb], out_specs=[qb, kvt, kvt],
        out_shape=[SDS((n, ATT_Q_DIM), F32), SDS((ATT_KV_HEADS, ATT_HEAD_DIM, n), F32), SDS((ATT_KV_HEADS, ATT_HEAD_DIM, n), F32)],
        compiler_params=_cp("parallel", "arbitrary", "arbitrary"))(q, k, v, o, do)


def _both_directions(mats, axis):
    fwd = np.concatenate(mats, axis=axis).astype(np.float32)
    bwd = np.concatenate([m[::-1, ::-1] for m in mats], axis=axis).astype(np.float32)
    return jnp.asarray(np.stack([fwd, bwd]), MXU_DTYPE)


def _hg_segments():
    c = HG_CHUNK
    t = np.arange(c)[:, None]
    r = np.arange(c)[None, :]
    mats = [(r <= t)]
    for lev in range(HG_LEVELS):
        h = c >> (lev + 1)
        mid = (t // (2 * h)) * (2 * h) + h - 1
        hi = (t // h) % 2 == 1
        mats.append(np.where(hi, (r > mid) & (r <= t), (r > t) & (r <= mid)))
    mats.append(r > t)
    return _both_directions(mats, 0)


def _hg_pair_sums():
    c = HG_CHUNK
    r = np.arange(c)[:, None]
    t = np.arange(c)[None, :]
    gp, gn = [t >= r], [t < r]
    for lev in range(HG_LEVELS):
        sh = HG_LEVELS - 1 - lev
        same = (r >> sh) == (t >> sh)
        gp.append(same & (t >= r))
        gn.append(same & (t < r))
    return _both_directions(gp, 1), _both_directions(gn, 1)


def _split_dot(mat, x):
    hi = x.astype(MXU_DTYPE)
    lo = (x - hi.astype(F32)).astype(MXU_DTYPE)
    return _dot(mat, hi) + _dot(mat, lo)


def _hg_gates(hq, z, a0, a1):
    q = hq * _sigmoid(hq)
    sg = _sigmoid(z)
    lb = _sigmoid(a0 - a1)
    f = lb + (1.0 - lb) * sg
    k = (1.0 - lb) * (1.0 - sg)
    return q, f, k, sg, lb


def _hg_level_masks():
    c = HG_CHUNK
    t = np.arange(c)
    later, same = [], []
    for lev in range(HG_LEVELS):
        sh = HG_LEVELS - 1 - lev
        later.append(np.broadcast_to((((t >> sh) & 1) == 1)[:, None], (c, HG_HEAD_DIM)))
        same.append((t[:, None] >> (sh + 1)) == (t[None, :] >> (sh + 1)))
    same.append(t[:, None] == t[None, :])
    later = np.stack(later).astype(np.float32)
    return jnp.asarray(np.stack([later, 1.0 - later]), F32), jnp.asarray(np.stack(same).astype(np.float32), F32)


def _hg_level(q, k, ex, later_ref, lev):
    e = ex[lev + 1]
    e_q = e * later_ref[0, lev]
    e_k = e - e_q
    return q * e_q, k * e_k, e_q, e_k


def _hg_intra(q, k, ex, later_ref, same_ref):
    a = same_ref[HG_LEVELS] * jnp.sum(q * k, axis=-1, keepdims=True)
    for lev in range(HG_LEVELS):
        qs, ks, _, _ = _hg_level(q, k, ex, later_ref, lev)
        a = a + same_ref[lev] * _dot(qs, ks, "nt")
    return a


def _hg_specs(n, with_time):
    c = HG_CHUNK
    nc = n // c

    def chunk(d, i):
        first = d if with_time else 1 - d
        return i + first * (nc - 1 - 2 * i)

    def pcols(off, dir_stride=0):
        return [pl.BlockSpec((c, HG_PAIR), lambda d, i, j=j: (chunk(d, i), off // HG_PAIR + dir_stride // HG_PAIR * d + j)) for j in range(2)]

    specs = dict(
        hq=pcols(OFF_HQ), v=pcols(OFF_HI), z=pcols(OFF_ZF, OFF_ZB - OFF_ZF),
        shared=pl.BlockSpec((c, HG_DIM), lambda d, i: (chunk(d, i), 0)),
        per_dir=pl.BlockSpec((1, c, HG_DIM), lambda d, i: (d, chunk(d, i), 0)),
        vec=pl.BlockSpec((1, 1, HG_DIM), lambda d, i: (d, 0, 0)),
        seg=pl.BlockSpec((1, (HG_LEVELS + 2) * c, c), lambda d, i: (d, 0, 0)),
        sums=pl.BlockSpec((1, c, (HG_LEVELS + 1) * c), lambda d, i: (d, 0, 0)),
        later=pl.BlockSpec((1, HG_LEVELS, c, HG_HEAD_DIM), lambda d, i: (d, 0, 0, 0)),
        same=pl.BlockSpec((HG_LEVELS + 1, c, c), lambda d, i: (0, 0, 0)),
        state=pl.BlockSpec((1, HG_HEADS, 1, HG_HEAD_DIM, HG_HEAD_DIM), lambda d, i: (d, 0, chunk(d, i), 0, 0)),
        weights=pl.BlockSpec((1, HG_HEADS, 1, c, c), lambda d, i: (d, 0, chunk(d, i), 0, 0)),
        levels=pl.BlockSpec((1, HG_HEADS, 1, HG_LEVELS, c, HG_HEAD_DIM), lambda d, i: (d, 0, chunk(d, i), 0, 0, 0)),
        kept=pl.BlockSpec((1, HG_KEPT, c, HG_DIM), lambda d, i: (d, 0, chunk(d, i), 0)))
    return nc, specs


def _hg_head(refs, hh):
    off = (hh % 2) * HG_HEAD_DIM
    return refs[hh // 2][:, off:off + HG_HEAD_DIM]


def _hg_lanes(hh):
    return slice(hh * HG_HEAD_DIM, (hh + 1) * HG_HEAD_DIM)


def _hg_exps(seg_ref, f):
    c = HG_CHUNK
    args = _split_dot(seg_ref[0], jnp.log(f))
    return [jnp.exp(args[j * c:(j + 1) * c]) for j in range(HG_LEVELS + 2)]


def _hg_last_row(a, mirrored):
    return jnp.where(mirrored, a[0:1, :], a[HG_CHUNK - 1:HG_CHUNK, :])


def _hgrn_fwd(p, a0, a1, seg, masks, name):
    n = p.shape[0]
    nc, sp = _hg_specs(n, True)

    def body(hq0, hq1, z0, z1, v0, v1, a0_ref, a1_ref, seg_ref, later_ref, same_ref, o_ref, s0_ref, a_ref, e_ref, g_ref, st):
        @pl.when(pl.program_id(1) == 0)
        def _():
            st[...] = jnp.zeros_like(st)

        mirrored = pl.program_id(0) == 1
        for hh in range(HG_HEADS):
            ln = _hg_lanes(hh)
            hqv = _hg_head((hq0, hq1), hh)
            q, f, k, sg, _ = _hg_gates(hqv, _hg_head((z0, z1), hh), a0_ref[0, :, ln], a1_ref[0, :, ln])
            vv = _hg_head((v0, v1), hh)
            ex = _hg_exps(seg_ref, f)
            for lev in range(HG_LEVELS):
                e_ref[0, hh, 0, lev] = ex[lev + 1].astype(e_ref.dtype)
            sq = _sigmoid(hqv)
            for j, kept in enumerate((q, k, f, sg, sq * (1.0 + hqv * (1.0 - sq)), ex[0], ex[HG_LEVELS + 1])):
                g_ref[0, j, :, ln] = kept
            a = _hg_intra(q, k, ex, later_ref, same_ref).astype(MXU_DTYPE)
            a_ref[0, hh, 0] = a
            s_t = st[hh]
            s0_ref[0, hh, 0] = s_t
            o_ref[0, :, ln] = _dot(a, vv) + _dot(q * ex[0], s_t, "nt")
            st[hh] = s_t * _hg_last_row(ex[0], mirrored) + _dot(vv, k * ex[HG_LEVELS + 1], "tn")

    return pl.pallas_call(
        body, name=name, grid=(2, nc), in_specs=sp["hq"] + sp["z"] + sp["v"] + [sp["vec"], sp["vec"], sp["seg"], sp["later"], sp["same"]],
        out_specs=[sp["per_dir"], sp["state"], sp["weights"], sp["levels"], sp["kept"]],
        out_shape=[SDS((2, n, HG_DIM), F32), SDS((2, HG_HEADS, nc, HG_HEAD_DIM, HG_HEAD_DIM), F32),
                   SDS((2, HG_HEADS, nc, HG_CHUNK, HG_CHUNK), MXU_DTYPE),
                   SDS((2, HG_HEADS, nc, HG_LEVELS, HG_CHUNK, HG_HEAD_DIM), MXU_DTYPE), SDS((2, HG_KEPT, n, HG_DIM), F32)],
        scratch_shapes=[pltpu.VMEM((HG_HEADS, HG_HEAD_DIM, HG_HEAD_DIM), F32)],
        compiler_params=_cp("parallel", "arbitrary"))(p, p, p, p, p, p, a0, a1, seg, *masks)


def _hgrn_bwd(p, a0, a1, masks, gp, gn, do, s0, a, e, kept, name):
    n = p.shape[0]
    nc, sp = _hg_specs(n, False)


    def body(v0, v1, a0_ref, a1_ref, later_ref, same_ref, gp_ref, gn_ref, do_ref, s0_ref, a_ref, e_ref, g_ref,
             dhq_ref, dz_ref, dv_ref, dlb_ref, rt):
        @pl.when(pl.program_id(1) == 0)
        def _():
            rt[...] = jnp.zeros_like(rt)
            dlb_ref[...] = jnp.zeros_like(dlb_ref)

        mirrored = pl.program_id(0) == 1
        for hh in range(HG_HEADS):
            ln = _hg_lanes(hh)
            q, k, f, sg, dsilu, e_first, e_last = (g_ref[0, j, :, ln] for j in range(HG_KEPT))
            lb = _sigmoid(a0_ref[0, :, ln] - a1_ref[0, :, ln])
            vv, dov = _hg_head((v0, v1), hh), do_ref[:, ln]
            ex = [e_first] + [e_ref[0, hh, 0, lev].astype(F32) for lev in range(HG_LEVELS)] + [e_last]
            a = a_ref[0, hh, 0]
            da = _dot(dov, vv, "nt")
            diag = jnp.sum(dov * vv, axis=-1, keepdims=True)
            s_t = s0_ref[0, hh, 0]
            r_t = rt[hh]
            k_end = k * ex[HG_LEVELS + 1]
            dv_ref[0, :, ln] = _dot(a, dov, "tn") + _dot(k_end, r_t, "nt")
            dq_inter = ex[0] * _dot(dov, s_t)
            dk_inter = ex[HG_LEVELS + 1] * _dot(vv, r_t)
            dq = diag * k + dq_inter
            dk = diag * q + dk_inter
            q_terms, k_terms = [q * dq_inter], [k * dk_inter]
            for lev in range(HG_LEVELS):
                qs, ks, e_q, e_k = _hg_level(q, k, ex, later_ref, lev)
                pairs = da * same_ref[lev]
                q_part = e_q * _dot(pairs, ks)
                k_part = e_k * _dot(pairs, qs, "tn")
                dq, dk = dq + q_part, dk + k_part
                q_terms.append(q * q_part)
                k_terms.append(k * k_part)
            decay = _hg_last_row(ex[0], mirrored)
            rt[hh] = r_t * decay + _dot(dov, q * ex[0], "tn")
            later = decay * jnp.sum(s_t * r_t, axis=0, keepdims=True)
            dlf = _dot(gp_ref[0], jnp.concatenate(q_terms, axis=0)) + _dot(gn_ref[0], jnp.concatenate(k_terms, axis=0)) + later
            df = dlf / f - dk
            dz_ref[0, :, ln] = df * (1.0 - lb) * sg * (1.0 - sg)
            dlb_ref[0, :, ln] += jnp.sum(df * (1.0 - sg), axis=0, keepdims=True)
            dhq_ref[0, :, ln] = dq * dsilu

    out = SDS((2, n, HG_DIM), F32)
    return pl.pallas_call(
        body, name=name, grid=(2, nc),
        in_specs=sp["v"] + [sp["vec"], sp["vec"], sp["later"], sp["same"], sp["sums"], sp["sums"],
                            sp["shared"], sp["state"], sp["weights"], sp["levels"], sp["kept"]],
        out_specs=[sp["per_dir"], sp["per_dir"], sp["per_dir"], sp["vec"]], out_shape=[out, out, out, SDS((2, 1, HG_DIM), F32)],
        scratch_shapes=[pltpu.VMEM((HG_HEADS, HG_HEAD_DIM, HG_HEAD_DIM), F32)],
        compiler_params=_cp("parallel", "arbitrary"))(p, p, a0, a1, *masks, gp, gn, do, s0, a, e, kept)


def _hg_post(o2, p, g, name):
    n = p.shape[0]
    tr = min(ROW_TILE, n)
    w = 2 * HG_HEAD_DIM

    def body(of_ref, ob_ref, hg_ref, g_ref, o_ref):
        for j in range(2):
            sl = slice(j * HG_HEAD_DIM, (j + 1) * HG_HEAD_DIM)
            o = of_ref[0, :, sl] + ob_ref[0, :, sl]
            hg = hg_ref[:, sl]
            o_ref[:, sl] = (o * _rstd(o) * g_ref[...] * (hg * _sigmoid(hg))).astype(o_ref.dtype)

    blk = pl.BlockSpec((tr, w), lambda i, j: (i, j))
    dirs = [pl.BlockSpec((1, tr, w), lambda i, j, d=d: (d, i, j)) for d in range(2)]
    return pl.pallas_call(
        body, name=name, grid=(n // tr, HG_DIM // w),
        in_specs=dirs + [pl.BlockSpec((tr, w), lambda i, j: (i, OFF_HG // w + j)), pl.BlockSpec((1, HG_HEAD_DIM), lambda i, j: (0, 0))],
        out_specs=blk, out_shape=SDS((n, HG_DIM), MXU_DTYPE), compiler_params=_cp("parallel", "parallel"))(o2, o2, p, g)


def _hg_post_bwd(o2, p, g, dcat, name, after=()):
    n = p.shape[0]
    tr = min(ROW_TILE, n)
    w = 2 * HG_HEAD_DIM
    after, after_specs = _unread(after)

    def body(of_ref, ob_ref, hg_ref, g_ref, d_ref, *rest):
        do_ref, dhg_ref, dg_ref = rest[len(after):]

        @pl.when(pl.program_id(1) == 0)
        def _():
            dg_ref[...] = jnp.zeros_like(dg_ref)

        for j in range(2):
            sl = slice(j * HG_HEAD_DIM, (j + 1) * HG_HEAD_DIM)
            o = of_ref[0, :, sl] + ob_ref[0, :, sl]
            hg = hg_ref[:, sl]
            d = d_ref[:, sl].astype(F32)
            sg = _sigmoid(hg)
            on = o * _rstd(o) * g_ref[...]
            dhg_ref[:, sl] = (d * on * sg * (1.0 + hg * (1.0 - sg))).astype(dhg_ref.dtype)
            dx, dg = _rms_bwd(o, g_ref[...], d * hg * sg)
            do_ref[:, sl] = dx
            dg_ref[0, :, sl] += dg

    blk = pl.BlockSpec((tr, w), lambda j, i: (i, j))
    dirs = [pl.BlockSpec((1, tr, w), lambda j, i, d=d: (d, i, j)) for d in range(2)]
    return pl.pallas_call(
        body, name=name, grid=(HG_DIM // w, n // tr),
        in_specs=dirs + [pl.BlockSpec((tr, w), lambda j, i: (i, OFF_HG // w + j)), pl.BlockSpec((1, HG_HEAD_DIM), lambda j, i: (0, 0)),
                         pl.BlockSpec((tr, w), lambda j, i: (i, ATT_Q_DIM // w + j))] + after_specs,
        out_specs=[blk, blk, pl.BlockSpec((1, 1, w), lambda j, i: (j, 0, 0))],
        out_shape=[SDS((n, HG_DIM), F32), SDS((n, HG_DIM), MXU_DTYPE), SDS((HG_DIM // w, 1, w), F32)],
        compiler_params=_cp("parallel", "arbitrary"))(o2, o2, p, g, dcat, *after)


XATT_TQ = 512


def _xattn_fwd(q, kv, name):
    n, nm = q.shape[0], kv.shape[0]
    tq = min(XATT_TQ, n)
    scale = X_HEAD_DIM ** -0.5

    def body(q_ref, k_ref, v_ref, o_ref):
        s = _dot(q_ref[...], k_ref[...], "nt") * scale
        e = jnp.exp(s - jnp.max(s, axis=-1, keepdims=True))
        o_ref[...] = _dot(e / jnp.sum(e, axis=-1, keepdims=True), v_ref[...]).astype(o_ref.dtype)

    qb = pl.BlockSpec((tq, X_HEAD_DIM), lambda h, i: (i, h))
    return pl.pallas_call(
        body, name=name, grid=(X_HEADS, n // tq),
        in_specs=[qb, pl.BlockSpec((nm, X_HEAD_DIM), lambda h, i: (0, h)), pl.BlockSpec((nm, X_HEAD_DIM), lambda h, i: (0, X_HEADS + h))],
        out_specs=qb, out_shape=SDS(q.shape, MXU_DTYPE), compiler_params=_cp("parallel", "parallel"))(q, kv, kv)


def _xattn_bwd(q, kv, do, name, after=()):
    n, nm = q.shape[0], kv.shape[0]
    tq = min(XATT_TQ, n)
    scale = X_HEAD_DIM ** -0.5
    after, after_specs = _unread(after)

    def body(q_ref, k_ref, v_ref, do_ref, *rest):
        dq_ref, dk_ref, dv_ref = rest[len(after):]

        @pl.when(pl.program_id(1) == 0)
        def _():
            dk_ref[...] = jnp.zeros_like(dk_ref)
            dv_ref[...] = jnp.zeros_like(dv_ref)

        qv, dov = q_ref[...], do_ref[...]
        s = _dot(qv, k_ref[...], "nt") * scale
        e = jnp.exp(s - jnp.max(s, axis=-1, keepdims=True))
        p = e / jnp.sum(e, axis=-1, keepdims=True)
        dp = _dot(dov, v_ref[...], "nt")
        ds = p * (dp - jnp.sum(p * dp, axis=-1, keepdims=True)) * scale
        dq_ref[...] = _dot(ds, k_ref[...]).astype(dq_ref.dtype)
        dk_ref[...] += _dot(ds, qv, "tn")
        dv_ref[...] += _dot(p, dov, "tn")

    qb = pl.BlockSpec((tq, X_HEAD_DIM), lambda h, i: (i, h))
    kb = pl.BlockSpec((nm, X_HEAD_DIM), lambda h, i: (0, h))
    return pl.pallas_call(
        body, name=name, grid=(X_HEADS, n // tq),
        in_specs=[qb, kb, pl.BlockSpec((nm, X_HEAD_DIM), lambda h, i: (0, X_HEADS + h)), qb] + after_specs, out_specs=[qb, kb, kb],
        out_shape=[SDS(q.shape, MXU_DTYPE), SDS((nm, X_HEADS * X_HEAD_DIM), F32), SDS((nm, X_HEADS * X_HEAD_DIM), F32)],
        compiler_params=_cp("parallel", "arbitrary"))(q, kv, kv, do, *after)


def _edge_rows(shape):
    row = lax.broadcasted_iota(jnp.int32, shape, 0)
    return row == 0, row == shape[0] - 1


def _shift_rows(u, down, edges):
    if down:
        return jnp.where(edges[0], 0.0, pltpu.roll(u, 1, axis=0))
    return jnp.where(edges[1], 0.0, pltpu.roll(u, u.shape[0] - 1, axis=0))


def _conv(u, w, b, edges):
    return b + _shift_rows(u, True, edges) * w[0:1, :] + u * w[1:2, :] + _shift_rows(u, False, edges) * w[2:3, :]


def _ff_specs(n):
    gate = lambda rows: pl.BlockSpec((rows, FF_COLS), lambda j: (0, j))
    val = lambda rows: pl.BlockSpec((rows, FF_COLS), lambda j: (0, FF_BLOCKS + j))
    return [gate(n), val(n), gate(3), val(3), gate(1), val(1)], gate


def _conv_gate(u, cw, cb, name):
    n = u.shape[0]
    ins, gate_blk = _ff_specs(n)

    def body(ug_ref, uv_ref, wg_ref, wv_ref, bg_ref, bv_ref, o_ref):
        edges = _edge_rows(ug_ref.shape)
        gate = _conv(ug_ref[...], wg_ref[...], bg_ref[...], edges)
        val = _conv(uv_ref[...], wv_ref[...], bv_ref[...], edges)
        o_ref[...] = (gate * _sigmoid(gate) * val).astype(o_ref.dtype)

    return pl.pallas_call(
        body, name=name, grid=(FF_BLOCKS,), in_specs=ins, out_specs=gate_blk(n), out_shape=SDS((n, D_FF), MXU_DTYPE),
        compiler_params=_cp("parallel"))(u, u, cw, cw, cb, cb)


def _conv_gate_bwd(u, cw, cb, da, name, after=()):
    n = u.shape[0]
    ins, gate_blk = _ff_specs(n)
    after, after_specs = _unread(after)

    def side(dacc, u, w, edges, du_ref, dw_ref, db_ref):
        nxt, prv = _shift_rows(dacc, False, edges), _shift_rows(dacc, True, edges)
        du_ref[...] = (nxt * w[0:1, :] + dacc * w[1:2, :] + prv * w[2:3, :]).astype(du_ref.dtype)
        db_ref[...] = jnp.sum(dacc, axis=0, keepdims=True)
        dw_ref[0:1, :] = jnp.sum(nxt * u, axis=0, keepdims=True)
        dw_ref[1:2, :] = jnp.sum(dacc * u, axis=0, keepdims=True)
        dw_ref[2:3, :] = jnp.sum(prv * u, axis=0, keepdims=True)

    def body(ug_ref, uv_ref, wg_ref, wv_ref, bg_ref, bv_ref, da_ref, *rest):
        dug_ref, duv_ref, dwg_ref, dwv_ref, dbg_ref, dbv_ref = rest[len(after):]
        ug, uv = ug_ref[...], uv_ref[...]
        edges = _edge_rows(ug.shape)
        gate = _conv(ug, wg_ref[...], bg_ref[...], edges)
        val = _conv(uv, wv_ref[...], bv_ref[...], edges)
        sg = _sigmoid(gate)
        dav = da_ref[...].astype(F32)
        side(dav * val * sg * (1.0 + gate * (1.0 - sg)), ug, wg_ref[...], edges, dug_ref, dwg_ref, dbg_ref)
        side(dav * gate * sg, uv, wv_ref[...], edges, duv_ref, dwv_ref, dbv_ref)

    return pl.pallas_call(
        body, name=name, grid=(FF_BLOCKS,), in_specs=ins + [gate_blk(n)] + after_specs,
        out_specs=[gate_blk(n), gate_blk(n), gate_blk(3), gate_blk(3), gate_blk(1), gate_blk(1)],
        out_shape=[SDS((n, D_FF), MXU_DTYPE)] * 2 + [SDS((3, D_FF), F32)] * 2 + [SDS((1, D_FF), F32)] * 2,
        compiler_params=_cp("parallel"))(u, u, cw, cw, cb, cb, da, *after)


def _adamw(w, g, m, v, name):
    r, c = w.shape[-2:]
    tr = _row_tile(r, ELEMENTWISE_ROWS)
    assert w.ndim == 2 or w.shape[:-2] == (1,), (name, w.shape)

    def body(w_ref, g_ref, m_ref, v_ref, d_ref, mo_ref, vo_ref, go_ref):
        gv = g_ref[...]
        go_ref[...] = gv
        mn = ADAM_B1 * m_ref[...] + (1.0 - ADAM_B1) * gv
        vn = ADAM_B2 * v_ref[...] + (1.0 - ADAM_B2) * gv * gv
        m_hat = mn / (1.0 - ADAM_B1 ** ADAM_STEP)
        v_hat = vn / (1.0 - ADAM_B2 ** ADAM_STEP)
        d_ref[...] = -ADAM_LR * (m_hat / (jnp.sqrt(v_hat) + ADAM_EPS) + ADAM_WD * w_ref[...])
        mo_ref[...] = mn
        vo_ref[...] = vn

    blk = pl.BlockSpec((tr, c), lambda i: (i, 0)) if w.ndim == 2 else pl.BlockSpec((1, tr, c), lambda i: (0, i, 0))
    out = SDS(w.shape, F32)
    return pl.pallas_call(body, name=name, grid=(r // tr,), in_specs=[blk] * 4, out_specs=[blk] * 4, out_shape=[out] * 4,
                          compiler_params=_cp("parallel"))(w, g, m, v)


ANY = pl.BlockSpec(memory_space=pl.ANY)


def _place():
    x, y, c = lax.axis_index("x"), lax.axis_index("y"), lax.axis_index("c")
    return x, y, c, [(1 - x, y), (x, 1 - y), (1 - x, 1 - y)]


def _gather_shards(shards, name):
    nt = len(shards)

    def body(*refs):
        ins, outs = refs[:nt], refs[nt:2 * nt]
        send, recv, fsend, frecv, osend, orecv = refs[2 * nt:]
        x, y, c, chips = _place()
        me = 2 * x + y

        def half(t, chip, cc):
            h = ins[t].shape[0] // 2
            return outs[t].at[chip, pl.ds(cc * h, h)]

        def ici(t, j):
            cx, cy = chips[j]
            h = ins[t].shape[0] // 2
            return pltpu.make_async_remote_copy(src_ref=ins[t].at[pl.ds(c * h, h)], dst_ref=half(t, me, c),
                                                send_sem=send.at[t, j], recv_sem=recv.at[t, j], device_id=(cx, cy, c), device_id_type=MESH)

        def landed(t, j):
            cx, cy = chips[j]
            blk = half(t, 2 * cx + cy, c)
            return pltpu.make_async_remote_copy(src_ref=blk, dst_ref=blk, send_sem=send.at[t, j], recv_sem=recv.at[t, j],
                                                device_id=(cx, cy, c), device_id_type=MESH)

        def d2d(t, j, cc):
            cx, cy = chips[j]
            blk = half(t, 2 * cx + cy, cc)
            return pltpu.make_async_remote_copy(src_ref=blk, dst_ref=blk, send_sem=fsend.at[t, j], recv_sem=frecv.at[t, j],
                                                device_id=(x, y, 1 - c), device_id_type=MESH)

        own = [pltpu.make_async_remote_copy(src_ref=ins[t], dst_ref=outs[t].at[me], send_sem=osend.at[t], recv_sem=orecv.at[t],
                                            device_id=(x, y, 1 - c), device_id_type=MESH) for t in range(nt)]
        for t in range(nt):
            for j in range(3):
                ici(t, j).start()
        for cp in own:
            cp.start()
        for t in range(nt):
            for j in range(3):
                landed(t, j).wait_recv()
                d2d(t, j, c).start()
        for t in range(nt):
            for j in range(3):
                d2d(t, j, 1 - c).wait_recv()
        for t in range(nt):
            for j in range(3):
                ici(t, j).wait_send()
                d2d(t, j, c).wait_send()
        for cp in own:
            cp.wait()

    return pl.pallas_call(
        body, name=name, in_specs=[ANY] * nt, out_specs=[ANY] * nt,
        out_shape=[SDS((4,) + s.shape, s.dtype) for s in shards],
        scratch_shapes=[pltpu.SemaphoreType.DMA((nt, 3))] * 4 + [pltpu.SemaphoreType.DMA((nt,))] * 2,
        compiler_params=pltpu.CompilerParams(has_side_effects=True))(*shards)


def _join_halves(bufs, name):
    nt = len(bufs)

    def body(*refs):
        outs = refs[nt:2 * nt]
        send, recv = refs[2 * nt:]
        x, y, c, _ = _place()
        cps = [pltpu.make_async_remote_copy(src_ref=outs[t].at[c], dst_ref=outs[t].at[c], send_sem=send.at[t], recv_sem=recv.at[t],
                                            device_id=(x, y, 1 - c), device_id_type=MESH) for t in range(nt)]
        for cp in cps:
            cp.start()
        for t in range(nt):
            theirs = outs[t].at[1 - c]
            pltpu.make_async_remote_copy(src_ref=theirs, dst_ref=theirs, send_sem=send.at[t], recv_sem=recv.at[t],
                                         device_id=(x, y, 1 - c), device_id_type=MESH).wait_recv()
        for cp in cps:
            cp.wait_send()

    return pl.pallas_call(
        body, name=name, in_specs=[ANY] * nt, out_specs=[ANY] * nt, out_shape=[SDS(b.shape, b.dtype) for b in bufs],
        input_output_aliases={t: t for t in range(nt)},
        scratch_shapes=[pltpu.SemaphoreType.DMA((nt,))] * 2,
        compiler_params=pltpu.CompilerParams(has_side_effects=True))(*bufs)


def _exchange_small(v, reduce, name, after=()):
    rows = v.shape[0]
    after, after_specs = _unread(after)

    def body(v_ref, *rest):
        o_ref, buf, send, recv = rest[-4:]
        x, y, c, _ = _place()
        me = 4 * x + 2 * y + c
        buf[me] = v_ref[...]

        def peer(dx, dy, dc):
            return (1 - x if dx else x, 1 - y if dy else y, 1 - c if dc else c)

        peers = [(dx, dy, dc) for dx in range(2) for dy in range(2) for dc in range(2) if (dx, dy, dc) != (0, 0, 0)]
        cps = []
        for j, (dx, dy, dc) in enumerate(peers):
            cps.append(pltpu.make_async_remote_copy(src_ref=v_ref, dst_ref=buf.at[me], send_sem=send.at[j], recv_sem=recv.at[j],
                                                    device_id=peer(dx, dy, dc), device_id_type=MESH))
        for cp in cps:
            cp.start()
        for j, (dx, dy, dc) in enumerate(peers):
            px, py, pc = peer(dx, dy, dc)
            blk = buf.at[4 * px + 2 * py + pc]
            pltpu.make_async_remote_copy(src_ref=blk, dst_ref=blk, send_sem=send.at[j], recv_sem=recv.at[j],
                                         device_id=(px, py, pc), device_id_type=MESH).wait_recv()
        for cp in cps:
            cp.wait_send()
        if reduce:
            acc = buf[0]
            for j in range(1, 8):
                acc = acc + buf[j]
            o_ref[...] = acc
        else:
            o_ref[...] = buf[...]

    vm = pl.BlockSpec(memory_space=pltpu.VMEM)
    return pl.pallas_call(
        body, name=name, in_specs=[vm] + after_specs, out_specs=vm, out_shape=SDS((rows, 128) if reduce else (8, rows, 128), F32),
        scratch_shapes=[pltpu.VMEM((8, rows, 128), F32), pltpu.SemaphoreType.DMA((7,)), pltpu.SemaphoreType.DMA((7,))],
        compiler_params=pltpu.CompilerParams(has_side_effects=True))(v, *after)


HBM = pl.BlockSpec(memory_space=pltpu.HBM)
SEM = pl.BlockSpec(memory_space=pltpu.SEMAPHORE)
TOKEN = pl.BlockSpec(memory_space=pltpu.VMEM)
TOKEN_SHAPE = SDS((8, 128), F32)
PEERS = 7


def _in_hbm(a):
    return pltpu.with_memory_space_constraint(a, pltpu.HBM)


def _split_params():
    return pltpu.CompilerParams(has_side_effects=pltpu.SideEffectType.DATAFLOW_SIDE_EFFECTING)


def _gather_start(shards, name, after=()):
    nt = len(shards)
    after, after_specs = _unread(after)

    def body(*refs):
        ins, lands = refs[:nt], refs[nt:2 * nt]
        outs = refs[2 * nt + len(after):]
        sends, recvs = outs[:nt], outs[nt:2 * nt]
        x, y, c, chips = _place()
        me = 2 * x + y
        for t in range(nt):
            h = ins[t].shape[0] // 2
            mine = pl.ds(c * h, h)
            for j, (cx, cy) in enumerate(chips):
                for dc in range(2):
                    pltpu.make_async_remote_copy(src_ref=ins[t].at[mine], dst_ref=lands[t].at[me, mine], send_sem=sends[t].at[2 * j + dc],
                                                 recv_sem=recvs[t].at[2 * j + c], device_id=(cx, cy, dc), device_id_type=MESH).start()
            pltpu.make_async_remote_copy(src_ref=ins[t], dst_ref=lands[t].at[me], send_sem=sends[t].at[PEERS - 1], recv_sem=recvs[t].at[PEERS - 1],
                                         device_id=(x, y, 1 - c), device_id_type=MESH).start()
        outs[-1][...] = jnp.zeros(TOKEN_SHAPE.shape, F32)

    lands = [lax.empty((4,) + s.shape, s.dtype) for s in shards]
    out = pl.pallas_call(
        body, name=name, in_specs=[HBM] * (2 * nt) + after_specs, out_specs=[SEM] * (2 * nt) + [HBM] * (2 * nt) + [TOKEN],
        out_shape=[pltpu.SemaphoreType.DMA((PEERS,))] * (2 * nt)
        + [pltpu.HBM(s.shape, s.dtype) for s in shards] + [pltpu.HBM(l.shape, l.dtype) for l in lands] + [TOKEN_SHAPE],
        input_output_aliases={t: 2 * nt + t for t in range(2 * nt)}, compiler_params=_split_params())(
            *[_in_hbm(s) for s in shards], *[_in_hbm(l) for l in lands], *after)
    return out[:nt], out[nt:2 * nt], out[2 * nt:3 * nt], out[3 * nt:4 * nt], out[-1]


def _gather_wait(sends, recvs, shards, lands, after, name):
    nt = len(shards)

    def body(*refs):
        ins, lands_ref = refs[:nt], refs[nt:2 * nt]
        send_refs, recv_refs = refs[2 * nt:3 * nt], refs[3 * nt:4 * nt]
        x, y, c, chips = _place()
        for t in range(nt):
            h = ins[t].shape[0] // 2
            for j, (cx, cy) in enumerate(chips):
                for cs in range(2):
                    blk = lands_ref[t].at[2 * cx + cy, pl.ds(cs * h, h)]
                    pltpu.make_async_remote_copy(src_ref=blk, dst_ref=blk, send_sem=send_refs[t].at[2 * j + cs], recv_sem=recv_refs[t].at[2 * j + cs],
                                                 device_id=(cx, cy, cs), device_id_type=MESH).wait()
            blk = lands_ref[t].at[2 * x + y]
            pltpu.make_async_remote_copy(src_ref=blk, dst_ref=blk, send_sem=send_refs[t].at[PEERS - 1], recv_sem=recv_refs[t].at[PEERS - 1],
                                         device_id=(x, y, 1 - c), device_id_type=MESH).wait()

    out = pl.pallas_call(
        body, name=name, in_specs=[HBM] * (2 * nt) + [SEM] * (2 * nt) + [ANY], out_specs=[HBM] * (2 * nt),
        out_shape=[pltpu.HBM(s.shape, s.dtype) for s in shards] + [pltpu.HBM(l.shape, l.dtype) for l in lands],
        input_output_aliases={t: t for t in range(2 * nt)}, compiler_params=_split_params())(*shards, *lands, *sends, *recvs, after)
    return out[nt:]


def _scatter_start(g, name):
    _, r, c_ = g.shape
    h = r // 2

    def body(g_ref, land, send, recv, g_thru, land_thru, token):
        x, y, c, chips = _place()
        for j, (cx, cy) in enumerate(chips):
            for dc in range(2):
                pltpu.make_async_remote_copy(src_ref=g_ref.at[2 * cx + cy, pl.ds(dc * h, h)], dst_ref=land.at[2 * j + c], send_sem=send.at[2 * j + dc],
                                             recv_sem=recv.at[2 * j + c], device_id=(cx, cy, dc), device_id_type=MESH).start()
        pltpu.make_async_remote_copy(src_ref=g_ref.at[2 * x + y, pl.ds((1 - c) * h, h)], dst_ref=land.at[PEERS - 1], send_sem=send.at[PEERS - 1],
                                     recv_sem=recv.at[PEERS - 1], device_id=(x, y, 1 - c), device_id_type=MESH).start()
        token[...] = jnp.zeros(TOKEN_SHAPE.shape, F32)

    land = lax.empty((PEERS, h, c_), g.dtype)
    return pl.pallas_call(
        body, name=name, in_specs=[HBM, HBM], out_specs=[SEM, SEM, HBM, HBM, TOKEN],
        out_shape=[pltpu.SemaphoreType.DMA((PEERS,)), pltpu.SemaphoreType.DMA((PEERS,)), pltpu.HBM(g.shape, g.dtype),
                   pltpu.HBM(land.shape, land.dtype), TOKEN_SHAPE],
        input_output_aliases={0: 2, 1: 3}, compiler_params=_split_params())(_in_hbm(g), _in_hbm(land))


def _scatter_wait(started, after, name):
    nt = len(started)

    def body(*refs):
        lands = refs[nt:2 * nt]
        sends, recvs = refs[2 * nt:3 * nt], refs[3 * nt:4 * nt]
        x, y, c, chips = _place()
        peers = [(cx, cy, dc) for cx, cy in chips for dc in range(2)] + [(x, y, 1 - c)]
        for t in range(nt):
            for k, peer in enumerate(peers):
                blk = lands[t].at[k]
                pltpu.make_async_remote_copy(src_ref=blk, dst_ref=blk, send_sem=sends[t].at[k], recv_sem=recvs[t].at[k],
                                             device_id=peer, device_id_type=MESH).wait()

    gs, lands = [s[2] for s in started], [s[3] for s in started]
    after, after_specs = _unread(after)
    out = pl.pallas_call(
        body, name=name, in_specs=[HBM] * (2 * nt) + [SEM] * (2 * nt) + after_specs, out_specs=[HBM] * (2 * nt),
        out_shape=[pltpu.HBM(a.shape, a.dtype) for a in gs + lands],
        input_output_aliases={t: t for t in range(2 * nt)}, compiler_params=_split_params())(
            *gs, *lands, *[s[0] for s in started], *[s[1] for s in started], *after)
    return out[:nt], out[nt:]


def _sum_devices(g, land, me, core, name):
    npeer, h, c = land.shape
    tr = _row_tile(h, 2 * ELEMENTWISE_ROWS)
    steps = h // tr

    def body(ix_ref, own_ref, land_ref, o_ref):
        acc = own_ref[0].astype(F32)
        for j in range(npeer):
            acc = acc + land_ref[j].astype(F32)
        o_ref[0] = acc

    grid_spec = pltpu.PrefetchScalarGridSpec(
        num_scalar_prefetch=1, grid=(steps,),
        in_specs=[pl.BlockSpec((1, tr, c), lambda i, ix: (ix[0], ix[1] * steps + i, 0)), pl.BlockSpec((npeer, tr, c), lambda i, ix: (0, i, 0))],
        out_specs=pl.BlockSpec((1, tr, c), lambda i, ix: (ix[1], i, 0)))
    return pl.pallas_call(body, name=name, grid_spec=grid_spec, out_shape=SDS((2, h, c), F32),
                          compiler_params=_cp("parallel"))(jnp.stack([me, core]), g, land)


def _pack_small(parts):
    flat = jnp.concatenate([p.reshape(-1) for p in parts])
    total = flat.shape[0]
    rows = -(-total // 1024) * 8
    return jnp.pad(flat, (0, rows * 128 - total)).reshape(rows, 128)


def _unpack_small(packed, shapes):
    flat = packed.reshape(-1)
    out, off = [], 0
    for s in shapes:
        size = int(np.prod(s))
        out.append(flat[off:off + size].reshape(s))
        off += size
    return out


def _local_step(x, mem, target, w_in, first_after, mid_weights, ffn_weights, on_grad, gains, conv_w, conv_b, hg_lb):
    n = x.shape[0]
    cos, sin = _rope_tables(n)
    seg = _hg_segments()
    gp, gn = _hg_pair_sums()
    masks = _hg_level_masks()
    gq2 = jnp.tile(gains["q_norm_g"], (1, 2))
    gk2 = jnp.tile(gains["k_norm_g"], (1, 2))
    a0 = hg_lb[:, 0:1, :]
    a1 = hg_lb[:, 1:2, :]

    p, h1 = _norm_mm(x, gains["pre_mix_g"], w_in, F32, TOKEN_TILE, 1664, "in_proj", after=(first_after,))
    qr, kr = _qk_prep(p, gq2, gk2, cos, sin, "qk_prep")
    heads = lambda a: a.reshape(n, ATT_KV_HEADS, ATT_HEAD_DIM).transpose(1, 0, 2)
    kh = heads(kr)
    vh = heads(p[:, OFF_AV:OFF_AV + ATT_KV_DIM].astype(MXU_DTYPE))
    att = _attn_fwd(qr, kh, vh, "attn_fwd")
    o2, s0, hg_a, hg_e, hg_kept = _hgrn_fwd(p, a0, a1, seg, masks, "hgrn_fwd")
    rec = _hg_post(o2, p, gains["hg_out_norm_g"], "hg_post")
    cat = jnp.concatenate([att, rec], axis=1)
    w_out, w_xq, w_xkv, w_xo = mid_weights(cat)
    mixed, x1 = _mm_resid_norm(cat, w_out, x, gains["post_mix_g"], 512, "out_proj_resid")
    xq, h2 = _norm_mm(x1, gains["pre_x_g"], w_xq, MXU_DTYPE, TOKEN_TILE, 1024, "xq_proj")
    kv, mn = _norm_mm(mem, gains["mem_norm_g"], w_xkv, MXU_DTYPE, 256, 2048, "xkv_proj")
    ox = _xattn_fwd(xq, kv, "xattn_fwd")
    xo, x2 = _mm_resid_norm(ox, w_xo, x1, gains["post_x_g"], 512, "xo_proj_resid")
    w_up = ffn_weights("w_up", x2)
    u, h3 = _norm_mm(x2, gains["pre_ffn_g"], w_up, F32, TOKEN_TILE, 1408, "up_proj")
    act = _conv_gate(u, conv_w, conv_b, "conv_gate")
    w_down = ffn_weights("w_down", act)
    dn, d3, loss = _mm_resid_norm(act, w_down, x2, gains["post_ffn_g"], 512, "down_proj_resid_loss", target=target)

    gs = {}
    d_act, d_dn, gs["post_ffn_g"] = _norm_bwd_mm(dn, gains["post_ffn_g"], d3, w_down, F32, 512, 1408, "ffn_post_bwd_down_dx")
    tok = on_grad("w_down", _mm(act, d_dn, "tn", WIRE_DTYPE, 1408, 1024, "down_dw"))
    du_g, du_v, dcw_g, dcw_v, dcb_g, dcb_v = _conv_gate_bwd(u, conv_w, conv_b, d_act, "conv_gate_bwd", after=(tok,))
    gs["conv_w"] = jnp.concatenate([dcw_g, dcw_v], axis=1)
    gs["conv_b"] = jnp.concatenate([dcb_g, dcb_v], axis=1)
    ff_shard = w_up.shape[2]
    g_up = _dw_by_owner(h3, du_g, ff_shard, 0, None, 512, "up_dw_gate")
    tok = on_grad("w_up", _dw_by_owner(h3, du_v, ff_shard, 2, g_up, 512, "up_dw_value"))
    d2, gs["pre_ffn_g"] = _dx_norm_bwd([(du_g, 0), (du_g, 1), (du_v, 0), (du_v, 1)], w_up, x2, gains["pre_ffn_g"], d3, 512,
                                       "up_dx_pre_bwd", after=(tok,))
    d_ox, d_xo, gs["post_x_g"] = _norm_bwd_mm(xo, gains["post_x_g"], d2, w_xo, MXU_DTYPE, 512, 1024, "x_post_bwd_xo_dx")
    tok = on_grad("w_xo", _mm(ox, d_xo, "tn", WIRE_DTYPE, 512, 1024, "xo_dw"))
    d_xq, d_k, d_v = _xattn_bwd(xq, kv, d_ox, "xattn_bwd", after=(tok,))
    d_kv = jnp.concatenate([d_k, d_v], axis=1).astype(MXU_DTYPE)
    tok = on_grad("w_xq", _mm(h2, d_xq, "tn", WIRE_DTYPE, 512, 1024, "xq_dw"))
    tok_kv = on_grad("w_xkv", _dw_by_owner(mn, d_kv, w_xkv.shape[2], 0, None, 512, "xkv_dw"))
    d1, gs["pre_x_g"] = _dx_norm_bwd([(d_xq, 0)], w_xq[None], x1, gains["pre_x_g"], d2, 512, "xq_dx_pre_bwd", after=(tok, tok_kv))
    d_mn = _mm_nt_parts([(d_kv, s) for s in range(4)], w_xkv, F32, 256, 1024, "xkv_dx")
    _, gs["mem_norm_g"] = _norm_bwd(mem, gains["mem_norm_g"], d_mn, None, MXU_DTYPE, "mem_norm_bwd")
    d_cat, d_mixed, gs["post_mix_g"] = _norm_bwd_mm(mixed, gains["post_mix_g"], d1, w_out, MXU_DTYPE, 512, 1024, "mix_post_bwd_out_dx")
    tok = on_grad("w_out", _mm(cat, d_mixed, "tn", WIRE_DTYPE, 512, 1024, "out_dw"))
    d_o, d_hg, dg_hg = _hg_post_bwd(o2, p, gains["hg_out_norm_g"], d_cat, "hg_post_bwd", after=(tok,))
    gs["hg_out_norm_g"] = dg_hg.reshape(HG_HEADS, HG_HEAD_DIM).sum(axis=0, keepdims=True)
    dhq2, dz2, dhv2, dlb = _hgrn_bwd(p, a0, a1, masks, gp, gn, d_o, s0, hg_a, hg_e, hg_kept, "hgrn_bwd")
    lb = jax.nn.sigmoid(a0 - a1)
    da0 = dlb * lb * (1.0 - lb)
    gs["hg_lb"] = jnp.concatenate([da0, -da0], axis=1)
    d_qr, d_kh, d_vh = _attn_bwd(qr, kh, vh, cat, d_cat, "attn_bwd")
    unheads = lambda a: a.transpose(2, 0, 1).reshape(n, ATT_KV_DIM)
    d_aq, d_ak, dgq, dgk = _qk_prep_bwd(p, gq2, gk2, cos, sin, d_qr, unheads(d_kh), "qk_prep_bwd")
    gs["q_norm_g"] = dgq.reshape(ATT_HEADS, ATT_HEAD_DIM).sum(axis=0, keepdims=True)
    gs["k_norm_g"] = dgk.reshape(ATT_KV_HEADS, ATT_HEAD_DIM).sum(axis=0, keepdims=True)
    d_p = jnp.concatenate([d_aq, d_ak, unheads(d_vh).astype(MXU_DTYPE), (dhq2[0] + dhq2[1]).astype(MXU_DTYPE),
                           dz2[0].astype(MXU_DTYPE), dz2[1].astype(MXU_DTYPE), (dhv2[0] + dhv2[1]).astype(MXU_DTYPE), d_hg], axis=1)
    tok = on_grad("w_in", _mm(h1, d_p, "tn", WIRE_DTYPE, 512, 1664, "in_dw"))
    grad_x, gs["pre_mix_g"] = _dx_norm_bwd([(d_p, 0)], w_in[None], x, gains["pre_mix_g"], d1, 512, "in_dx_pre_bwd", after=(tok,))
    return loss, grad_x, gs


MATS = ("w_in", "w_out", "w_xq", "w_xkv", "w_xo", "w_up", "w_down")
GAINS = ("pre_mix_g", "q_norm_g", "k_norm_g", "hg_out_norm_g", "post_mix_g", "pre_x_g", "mem_norm_g", "post_x_g", "pre_ffn_g", "post_ffn_g")
WEIGHTS = ('pre_mix_g', 'w_in', 'q_norm_g', 'k_norm_g', 'hg_lb', 'hg_out_norm_g', 'w_out', 'post_mix_g', 'pre_x_g', 'mem_norm_g', 'w_xq',
           'w_xkv', 'w_xo', 'post_x_g', 'pre_ffn_g', 'w_up', 'conv_w', 'conv_b', 'w_down', 'post_ffn_g')


def kernel(x, mem, pre_mix_g, w_in, q_norm_g, k_norm_g, hg_lb, hg_out_norm_g, w_out, post_mix_g, pre_x_g, mem_norm_g, w_xq, w_xkv, w_xo, post_x_g, pre_ffn_g, w_up, conv_w, conv_b, w_down, post_ffn_g, loss_target, m_pre_mix_g, m_w_in, m_q_norm_g, m_k_norm_g, m_hg_lb, m_hg_out_norm_g, m_w_out, m_post_mix_g, m_pre_x_g, m_mem_norm_g, m_w_xq, m_w_xkv, m_w_xo, m_post_x_g, m_pre_ffn_g, m_w_up, m_conv_w, m_conv_b, m_w_down, m_post_ffn_g, v_pre_mix_g, v_w_in, v_q_norm_g, v_k_norm_g, v_hg_lb, v_hg_out_norm_g, v_w_out, v_post_mix_g, v_pre_x_g, v_mem_norm_g, v_w_xq, v_w_xkv, v_w_xo, v_post_x_g, v_pre_ffn_g, v_w_up, v_conv_w, v_conv_b, v_w_down, v_post_ffn_g):
    args = dict(locals())
    w = {k: args[k] for k in WEIGHTS}
    m = {k: args["m_" + k] for k in WEIGHTS}
    v = {k: args["v_" + k] for k in WEIGHTS}
    chip = 2 * lax.axis_index("x") + lax.axis_index("y")
    core = lax.axis_index("c")

    shards = {k: w[k][0].astype(WIRE_DTYPE) for k in MATS}

    def whole(k, g):
        return g if k in ("w_xkv", "w_up") else g.reshape(-1, g.shape[-1])

    w_in_shards = _gather_shards([shards["w_in"]], "gather_w_in")[0]
    w_in_full = jnp.concatenate([w_in_shards[s] for s in range(4)], axis=1)
    small_in = _exchange_small(_pack_small([w["conv_w"][0], w["hg_lb"]]), False, "gather_small")
    mid_names, ffn_names = ("w_out", "w_xq", "w_xkv", "w_xo"), ("w_up", "w_down")
    mid = _gather_start([shards[k] for k in mid_names], "gather_mid_start", after=(w_in_full, small_in))
    ffn = _gather_start([shards[k] for k in ffn_names], "gather_ffn_start", after=(mid[4],))

    def mid_weights(after):
        return [whole(k, g) for k, g in zip(mid_names, _gather_wait(*mid[:4], after, "gather_mid_wait"))]

    def ffn_weights(k, after):
        t = ffn_names.index(k)
        return whole(k, _gather_wait(*[part[t:t + 1] for part in ffn[:4]], after, "gather_wait_" + k)[0])

    cw_parts, lb_parts = [], []
    for s in range(4):
        cw_s, lb_s = _unpack_small(small_in[2 * s], [w["conv_w"][0].shape, w["hg_lb"].shape])
        cw_parts.append(cw_s)
        lb_parts.append(lb_s)
    conv_w_full = jnp.concatenate(cw_parts, axis=1)
    hg_lb_full = jnp.concatenate(lb_parts, axis=2)

    started = {}

    def on_grad(k, g):
        if k == "w_in":
            g = g.reshape(g.shape[0], 4, g.shape[1] // 4).transpose(1, 0, 2)
        elif g.ndim == 2:
            g = g.reshape(4, g.shape[0] // 4, g.shape[1])
        *started[k], token = _scatter_start(g, "grad_start_" + k)
        return token

    gains = {k: w[k] for k in GAINS}
    loss_part, grad_x, gs = _local_step(x[0], mem[0], loss_target[0], w_in_full, ffn[4], mid_weights, ffn_weights, on_grad, gains,
                                        conv_w_full, w["conv_b"], hg_lb_full)
    gs["loss"] = loss_part

    grads, delta, new_m, new_v = {}, {}, {}, {}

    def reduce_matrices(names, after, tag):
        sent, landed = _scatter_wait([started[k] for k in names], after, "grad_wait_" + tag)
        halves = [_sum_devices(g, land, chip, core, "grad_sum_" + k) for k, g, land in zip(names, sent, landed)]
        for k, r in zip(names, _join_halves(halves, "grad_join_" + tag)):
            grads[k] = r.reshape(1, -1, r.shape[-1])

    def adamw(names):
        for k in names:
            shape = w[k].shape
            keep = len(shape) == 3 and shape[0] == 1
            two_d = lambda a: a.reshape(shape) if keep else a.reshape(-1, shape[-1])
            d, mo, vo, go = _adamw(two_d(w[k]), two_d(grads[k]), two_d(m[k]), two_d(v[k]), "adamw_" + k)
            delta[k], new_m[k], new_v[k], grads[k] = d.reshape(shape), mo.reshape(shape), vo.reshape(shape), go.reshape(shape)

    early = tuple(k for k in MATS if k != "w_in")
    reduce_matrices(early, (grad_x,), "early")
    adamw(early)

    small_names = GAINS + ("conv_b", "conv_w", "hg_lb")
    packed = _pack_small([gs[k] for k in small_names + ("loss",)])
    reduced_small = _exchange_small(packed, True, "reduce_small", after=tuple(new_v[k] for k in early))
    *summed, loss = _unpack_small(reduced_small, [gs[k].shape for k in small_names + ("loss",)])
    loss = loss[0, 0]
    for k, g in zip(small_names, summed):
        grads[k] = g
    ncw = w["conv_w"].shape[2]
    grads["conv_w"] = lax.dynamic_slice_in_dim(grads["conv_w"], chip * ncw, ncw, axis=1)[None]
    nlb = w["hg_lb"].shape[2]
    grads["hg_lb"] = lax.dynamic_slice_in_dim(grads["hg_lb"], chip * nlb, nlb, axis=2)
    replicated = GAINS + ("conv_b",)
    shapes = [w[k].shape for k in replicated]
    rows = sum(int(np.prod(s)) for s in shapes) // 128
    pack = lambda d: jnp.concatenate([d[k].reshape(-1) for k in replicated]).reshape(rows, 128)
    outs = _adamw(pack(w), reduced_small[:rows], pack(m), pack(v), "adamw_replicated")
    for into, packed_out in zip((delta, new_m, new_v, grads), outs):
        for k, a in zip(replicated, _unpack_small(packed_out, shapes)):
            into[k] = a
    adamw(("conv_w", "hg_lb"))

    reduce_matrices(("w_in",), tuple(new_v[k] for k in early + small_names), "late")
    adamw(("w_in",))
    return (loss, grad_x[None], *[grads[k] for k in WEIGHTS], *[delta[k] for k in WEIGHTS],
            *[new_m[k] for k in WEIGHTS], *[new_v[k] for k in WEIGHTS])
```

```python
import numpy as np
import jax
import jax.numpy as jnp
from jax import lax
from jax.experimental import pallas as pl
from jax.experimental.pallas import tpu as pltpu

F32 = jnp.float32
MXU_DTYPE = jnp.bfloat16
WIRE_DTYPE = jnp.bfloat16
VMEM_LIMIT_BYTES = 56 * 1024 * 1024
ROWS_PER_16BIT_TILE = 16
ELEMENTWISE_ROWS = 256
EPS = 1e-6
MESH = pl.DeviceIdType.MESH

GRID_W = 64
ATT_HEADS, ATT_KV_HEADS, ATT_HEAD_DIM = 8, 2, 64
ATT_GROUP = ATT_HEADS // ATT_KV_HEADS
ATT_Q_DIM, ATT_KV_DIM = 512, 128
ROPE_THETA = 10000.0
HG_HEADS, HG_HEAD_DIM, HG_DIM = 4, 128, 512
HG_CHUNK = 128
HG_LEVELS = 7
HG_PAIR = 2 * HG_HEAD_DIM
HG_KEPT = 7
X_HEADS, X_HEAD_DIM = 4, 256
D_FF = 2816
FF_COLS = 256
FF_BLOCKS = D_FF // FF_COLS
OFF_AK, OFF_AV, OFF_HQ, OFF_ZF, OFF_ZB, OFF_HI, OFF_HG = 512, 640, 768, 1280, 1792, 2304, 2816

ADAM_LR, ADAM_B1, ADAM_B2, ADAM_EPS, ADAM_WD, ADAM_STEP = 0.001, 0.9, 0.999, 1e-08, 0.01, 10

SDS = jax.ShapeDtypeStruct


def _cp(*sem):
    return pltpu.CompilerParams(dimension_semantics=sem, vmem_limit_bytes=VMEM_LIMIT_BYTES)


def _row_tile(rows, cap):
    if rows <= cap:
        return rows
    return max(t for t in range(ROWS_PER_16BIT_TILE, cap + 1, ROWS_PER_16BIT_TILE) if rows % t == 0)


def _dot(a, b, form="nn"):
    dims = {"nn": (((1,), (0,)), ((), ())), "nt": (((1,), (1,)), ((), ())), "tn": (((0,), (0,)), ((), ()))}[form]
    return lax.dot_general(a.astype(MXU_DTYPE), b.astype(MXU_DTYPE), dims, preferred_element_type=F32)


def _sigmoid(x):
    return 1.0 / (1.0 + jnp.exp(-x))


def _rstd(x):
    return lax.rsqrt(jnp.mean(x * x, axis=-1, keepdims=True) + EPS)


def _rms_bwd(x, g, dy):
    r = _rstd(x)
    xh = x * r
    dn = dy * g
    dx = r * (dn - xh * jnp.mean(dn * xh, axis=-1, keepdims=True))
    return dx, jnp.sum(dy * xh, axis=0, keepdims=True)


def _unread(after):
    after = tuple(a for a in after if a is not None)
    return after, [pl.BlockSpec(memory_space=pl.ANY)] * len(after)


def _mm(a, b, form, out_dtype, tm, tn, name, after=()):
    after, after_specs = _unread(after)
    if form == "nn":
        (m, k), n = a.shape, b.shape[1]
    elif form == "nt":
        (m, k), n = a.shape, b.shape[0]
    else:
        (k, m), n = a.shape, b.shape[1]
    tm, tn = min(tm, m), min(tn, n)
    assert m % tm == 0 and n % tn == 0, (name, m, n, tm, tn)

    def body(a_ref, b_ref, *rest):
        o_ref = rest[-1]
        o_ref[...] = _dot(a_ref[...], b_ref[...], form).astype(o_ref.dtype)

    a_spec = pl.BlockSpec((k, tm), lambda i, j: (0, i)) if form == "tn" else pl.BlockSpec((tm, k), lambda i, j: (i, 0))
    b_spec = pl.BlockSpec((tn, k), lambda i, j: (j, 0)) if form == "nt" else pl.BlockSpec((k, tn), lambda i, j: (0, j))
    return pl.pallas_call(
        body, name=name, grid=(m // tm, n // tn), in_specs=[a_spec, b_spec] + after_specs,
        out_specs=pl.BlockSpec((tm, tn), lambda i, j: (i, j)), out_shape=SDS((m, n), out_dtype),
        compiler_params=_cp("parallel", "parallel"))(a, b, *after)


def _mm_nt_parts(a_parts, b, out_dtype, tm, tn, name, after=()):
    after, after_specs = _unread(after)
    parts, n, p = b.shape
    m = a_parts[0][0].shape[0]
    tm, tn = min(tm, m), min(tn, n)
    assert m % tm == 0 and n % tn == 0 and len(a_parts) == parts, (name, m, b.shape)

    def body(*refs):
        o_ref = refs[-1]
        acc = _dot(refs[0][...], refs[parts][0], "nt")
        for s in range(1, parts):
            acc = acc + _dot(refs[s][...], refs[parts + s][0], "nt")
        o_ref[...] = acc.astype(o_ref.dtype)

    a_specs = [pl.BlockSpec((tm, p), lambda i, j, cb=cb: (i, cb)) for _, cb in a_parts]
    b_specs = [pl.BlockSpec((1, tn, p), lambda i, j, s=s: (s, j, 0)) for s in range(parts)]
    return pl.pallas_call(
        body, name=name, grid=(m // tm, n // tn), in_specs=a_specs + b_specs + after_specs,
        out_specs=pl.BlockSpec((tm, tn), lambda i, j: (i, j)), out_shape=SDS((m, n), out_dtype),
        compiler_params=_cp("parallel", "parallel"))(*[arr for arr, _ in a_parts], *([b] * parts), *after)


def _norm_bwd_mm(y, g, d, w, out_dtype, tm, tn, name):
    n, dm = y.shape
    nn = w.shape[0]
    tm, tn = min(tm, n), min(tn, nn)
    assert n % tm == 0 and nn % tn == 0 and w.shape[1] == dm, (name, y.shape, w.shape)

    def body(y_ref, g_ref, d_ref, w_ref, dx_ref, dy_ref, dg_ref, dys):
        i, j = pl.program_id(0), pl.program_id(1)

        @pl.when(jnp.logical_and(i == 0, j == 0))
        def _():
            dg_ref[...] = jnp.zeros_like(dg_ref)

        @pl.when(j == 0)
        def _():
            dy, dg = _rms_bwd(y_ref[...], g_ref[...], d_ref[...])
            dy = dy.astype(MXU_DTYPE)
            dys[...] = dy
            dy_ref[...] = dy
            dg_ref[...] += dg

        dx_ref[...] = _dot(dys[...], w_ref[...], "nt").astype(dx_ref.dtype)

    row = pl.BlockSpec((tm, dm), lambda i, j: (i, 0))
    vec = pl.BlockSpec((1, dm), lambda i, j: (0, 0))
    return pl.pallas_call(
        body, name=name, grid=(n // tm, nn // tn), in_specs=[row, vec, row, pl.BlockSpec((tn, dm), lambda i, j: (j, 0))],
        out_specs=[pl.BlockSpec((tm, tn), lambda i, j: (i, j)), row, vec],
        out_shape=[SDS((n, nn), out_dtype), SDS((n, dm), MXU_DTYPE), SDS((1, dm), F32)],
        scratch_shapes=[pltpu.VMEM((tm, dm), MXU_DTYPE)],
        compiler_params=_cp("arbitrary", "arbitrary"))(y, g, d, w)


def _mm_resid_norm(a, b, x, g, tm, name, target=None):
    n, k = a.shape
    d = b.shape[1]
    tm = min(tm, n)
    assert n % tm == 0 and x.shape == (n, d), (name, a.shape, b.shape)
    with_loss = target is not None

    def body(a_ref, b_ref, x_ref, g_ref, *rest):
        y = _dot(a_ref[...], b_ref[...])
        out = x_ref[...] + y * _rstd(y) * g_ref[...]
        if not with_loss:
            y_ref, o_ref = rest
            y_ref[...] = y
            o_ref[...] = out
            return
        t_ref, y_ref, d_ref, l_ref = rest
        y_ref[...] = y
        diff = out - t_ref[...]
        d_ref[...] = diff * (1.0 / d)

        @pl.when(pl.program_id(0) == 0)
        def _():
            l_ref[...] = jnp.zeros_like(l_ref)

        l_ref[...] += 0.5 * jnp.sum(jnp.mean(diff * diff, axis=-1, keepdims=True), axis=0, keepdims=True)

    row = pl.BlockSpec((tm, d), lambda i: (i, 0))
    ins = [pl.BlockSpec((tm, k), lambda i: (i, 0)), pl.BlockSpec((k, d), lambda i: (0, 0)), row, pl.BlockSpec((1, d), lambda i: (0, 0))]
    out = SDS((n, d), F32)
    if with_loss:
        return pl.pallas_call(body, name=name, grid=(n // tm,), in_specs=ins + [row], out_specs=[row, row, pl.BlockSpec((1, 1), lambda i: (0, 0))],
                              out_shape=[out, out, SDS((1, 1), F32)], compiler_params=_cp("arbitrary"))(a, b, x, g, target)
    return pl.pallas_call(body, name=name, grid=(n // tm,), in_specs=ins, out_specs=[row, row], out_shape=[out, out],
                          compiler_params=_cp("parallel"))(a, b, x, g)


def _dx_norm_bwd(a_parts, b, x, g, res, tm, name, after=()):
    after, after_specs = _unread(after)
    parts, d, p = b.shape
    n = x.shape[0]
    tm = min(tm, n)
    assert n % tm == 0 and len(a_parts) == parts and x.shape[1] == d, (name, x.shape, b.shape)

    def body(*refs):
        x_ref, g_ref, res_ref = refs[2 * parts:2 * parts + 3]
        dx_ref, dg_ref = refs[-2:]
        dh = _dot(refs[0][...], refs[parts][0], "nt")
        for s in range(1, parts):
            dh = dh + _dot(refs[s][...], refs[parts + s][0], "nt")
        dx, dg = _rms_bwd(x_ref[...], g_ref[...], dh)
        dx_ref[...] = dx + res_ref[...]

        @pl.when(pl.program_id(0) == 0)
        def _():
            dg_ref[...] = jnp.zeros_like(dg_ref)

        dg_ref[...] += dg

    a_specs = [pl.BlockSpec((tm, p), lambda i, cb=cb: (i, cb)) for _, cb in a_parts]
    b_specs = [pl.BlockSpec((1, d, p), lambda i, s=s: (s, 0, 0)) for s in range(parts)]
    row = pl.BlockSpec((tm, d), lambda i: (i, 0))
    vec = pl.BlockSpec((1, d), lambda i: (0, 0))
    return pl.pallas_call(
        body, name=name, grid=(n // tm,), in_specs=a_specs + b_specs + [row, vec, row] + after_specs,
        out_specs=[row, vec], out_shape=[SDS((n, d), F32), SDS((1, d), F32)],
        compiler_params=_cp("arbitrary"))(*[arr for arr, _ in a_parts], *([b] * parts), x, g, res, *after)


def _dw_by_owner(a, b, tn, first, into, tm, name):
    k, m = a.shape
    cnt = b.shape[1] // tn
    tm = min(tm, m)
    assert m % tm == 0 and b.shape[1] == cnt * tn and first + cnt <= 4, (name, a.shape, b.shape)

    def body(a_ref, b_ref, *rest):
        rest[-1][0] = _dot(a_ref[...], b_ref[...], "tn").astype(rest[-1].dtype)

    extra = [] if into is None else [into]
    return pl.pallas_call(
        body, name=name, grid=(m // tm, cnt),
        in_specs=[pl.BlockSpec((k, tm), lambda i, j: (0, i)), pl.BlockSpec((k, tn), lambda i, j: (0, j))] + [pl.BlockSpec(memory_space=pl.ANY)] * len(extra),
        out_specs=pl.BlockSpec((1, tm, tn), lambda i, j: (first + j, i, 0)), out_shape=SDS((4, m, tn), WIRE_DTYPE),
        input_output_aliases={2: 0} if extra else {},
        compiler_params=_cp("parallel", "parallel"))(a, b, *extra)


def _norm_mm(x, g, w, out_dtype, tm, tn, name, after=()):
    after, after_specs = _unread(after)
    m, d = x.shape
    sharded = w.ndim == 3
    n = w.shape[-1] * (w.shape[0] if sharded else 1)
    tm, tn = min(tm, m), (w.shape[-1] if sharded else min(tn, n))
    assert m % tm == 0 and n % tn == 0, (name, m, n, tm, tn)

    def body(x_ref, g_ref, w_ref, *rest):
        o_ref, h_ref, hs = rest[-3:]

        @pl.when(pl.program_id(1) == 0)
        def _():
            xv = x_ref[...]
            h = (xv * _rstd(xv) * g_ref[...]).astype(MXU_DTYPE)
            hs[...] = h
            h_ref[...] = h

        o_ref[...] = _dot(hs[...], w_ref[0] if sharded else w_ref[...]).astype(o_ref.dtype)

    w_spec = pl.BlockSpec((1, d, tn), lambda i, j: (j, 0, 0)) if sharded else pl.BlockSpec((d, tn), lambda i, j: (0, j))
    return pl.pallas_call(
        body, name=name, grid=(m // tm, n // tn),
        in_specs=[pl.BlockSpec((tm, d), lambda i, j: (i, 0)), pl.BlockSpec((1, d), lambda i, j: (0, 0)), w_spec] + after_specs,
        out_specs=[pl.BlockSpec((tm, tn), lambda i, j: (i, j)), pl.BlockSpec((tm, d), lambda i, j: (i, 0))],
        out_shape=[SDS((m, n), out_dtype), SDS((m, d), MXU_DTYPE)],
        scratch_shapes=[pltpu.VMEM((tm, d), MXU_DTYPE)],
        compiler_params=_cp("parallel", "arbitrary"))(x, g, w, *after)


ROW_TILE = 512
TOKEN_TILE = 1024
WEIGHT_ROW_TILE = 1024


def _norm_bwd(x, g, dy, res, out_dtype, name):
    n, d = x.shape
    tr = min(ROW_TILE, n)
    has_res = res is not None

    def body(*refs):
        x_ref, g_ref, dy_ref = refs[:3]
        dx_ref, dg_ref = refs[-2:]
        dx, dg = _rms_bwd(x_ref[...], g_ref[...], dy_ref[...].astype(F32))
        if has_res:
            dx = dx + refs[3][...]
        dx_ref[...] = dx.astype(dx_ref.dtype)

        @pl.when(pl.program_id(0) == 0)
        def _():
            dg_ref[...] = jnp.zeros_like(dg_ref)

        dg_ref[...] += dg

    row = pl.BlockSpec((tr, d), lambda i: (i, 0))
    vec = pl.BlockSpec((1, d), lambda i: (0, 0))
    ins = [x, g, dy] + ([res] if has_res else [])
    return pl.pallas_call(
        body, name=name, grid=(n // tr,), in_specs=[row, vec, row] + ([row] if has_res else []),
        out_specs=[row, vec], out_shape=[SDS((n, d), out_dtype), SDS((1, d), F32)],
        compiler_params=_cp("arbitrary"))(*ins)


def _rope_tables(n):
    pairs = ATT_HEAD_DIM // 4
    t = np.arange(n)
    inv = np.power(ROPE_THETA, -np.arange(pairs, dtype=np.float32) / pairs).astype(np.float32)
    ang = np.concatenate([(t // GRID_W)[:, None].astype(np.float32) * inv, (t % GRID_W)[:, None].astype(np.float32) * inv], axis=-1)
    cos = np.repeat(np.cos(ang), 2, axis=-1)
    sin = np.repeat(np.sin(ang), 2, axis=-1) * np.tile(np.array([-1.0, 1.0], np.float32), ATT_HEAD_DIM // 2)
    return jnp.asarray(np.tile(cos, 2), F32), jnp.asarray(np.tile(sin, 2), F32)


def _swap_pairs(x):
    lane = lax.broadcasted_iota(jnp.int32, x.shape, 1)
    return jnp.where((lane & 1) == 0, pltpu.roll(x, 127, axis=1), pltpu.roll(x, 1, axis=1))


def _head_mean(v):
    lane = lax.broadcasted_iota(jnp.int32, v.shape, 1)
    lo = jnp.where(lane < ATT_HEAD_DIM, v, 0.0)
    s0 = jnp.sum(lo, axis=-1, keepdims=True)
    s1 = jnp.sum(v - lo, axis=-1, keepdims=True)
    return jnp.where(lane < ATT_HEAD_DIM, s0, s1) * (1.0 / ATT_HEAD_DIM)


def _qk_prep(p, gq, gk, cos, sin, name):
    n = p.shape[0]
    tr = min(ROW_TILE, n)

    def one(xv, g, c, s):
        xn = xv * lax.rsqrt(_head_mean(xv * xv) + EPS) * g
        return xn * c + _swap_pairs(xn) * s

    def body(q_ref, k_ref, gq_ref, gk_ref, c_ref, s_ref, qo_ref, ko_ref):
        c, s = c_ref[...], s_ref[...]
        for j in range(ATT_Q_DIM // 128):
            qo_ref[:, j * 128:(j + 1) * 128] = one(q_ref[:, j * 128:(j + 1) * 128], gq_ref[...], c, s).astype(qo_ref.dtype)
        ko_ref[...] = one(k_ref[...], gk_ref[...], c, s).astype(ko_ref.dtype)

    vec = pl.BlockSpec((1, 128), lambda i: (0, 0))
    tab = pl.BlockSpec((tr, 128), lambda i: (i, 0))
    return pl.pallas_call(
        body, name=name, grid=(n // tr,),
        in_specs=[pl.BlockSpec((tr, ATT_Q_DIM), lambda i: (i, 0)), pl.BlockSpec((tr, 128), lambda i: (i, OFF_AK // 128)), vec, vec, tab, tab],
        out_specs=[pl.BlockSpec((tr, ATT_Q_DIM), lambda i: (i, 0)), tab],
        out_shape=[SDS((n, ATT_Q_DIM), MXU_DTYPE), SDS((n, ATT_KV_DIM), MXU_DTYPE)],
        compiler_params=_cp("parallel"))(p, p, gq, gk, cos, sin)


def _qk_prep_bwd(p, gq, gk, cos, sin, dq, dk, name):
    n = p.shape[0]
    tr = min(ROW_TILE, n)

    def one(xv, g, c, s, dout):
        dxn = dout * c + _swap_pairs(dout * s)
        r = lax.rsqrt(_head_mean(xv * xv) + EPS)
        xh = xv * r
        dn = dxn * g
        dx = r * (dn - xh * _head_mean(dn * xh))
        return dx, jnp.sum(dxn * xh, axis=0, keepdims=True)

    def body(q_ref, k_ref, gq_ref, gk_ref, c_ref, s_ref, dq_ref, dk_ref, dqo_ref, dko_ref, dgq_ref, dgk_ref):
        @pl.when(pl.program_id(0) == 0)
        def _():
            dgq_ref[...] = jnp.zeros_like(dgq_ref)
            dgk_ref[...] = jnp.zeros_like(dgk_ref)

        c, s = c_ref[...], s_ref[...]
        for j in range(ATT_Q_DIM // 128):
            sl = slice(j * 128, (j + 1) * 128)
            dx, dg = one(q_ref[:, sl], gq_ref[...], c, s, dq_ref[:, sl])
            dqo_ref[:, sl] = dx.astype(dqo_ref.dtype)
            dgq_ref[:, sl] += dg
        dx, dg = one(k_ref[...], gk_ref[...], c, s, dk_ref[...])
        dko_ref[...] = dx.astype(dko_ref.dtype)
        dgk_ref[...] += dg

    vec = pl.BlockSpec((1, 128), lambda i: (0, 0))
    tab = pl.BlockSpec((tr, 128), lambda i: (i, 0))
    qrow = pl.BlockSpec((tr, ATT_Q_DIM), lambda i: (i, 0))
    return pl.pallas_call(
        body, name=name, grid=(n // tr,),
        in_specs=[qrow, pl.BlockSpec((tr, 128), lambda i: (i, OFF_AK // 128)), vec, vec, tab, tab, qrow, tab],
        out_specs=[qrow, tab, pl.BlockSpec((1, ATT_Q_DIM), lambda i: (0, 0)), vec],
        out_shape=[SDS((n, ATT_Q_DIM), MXU_DTYPE), SDS((n, ATT_KV_DIM), MXU_DTYPE), SDS((1, ATT_Q_DIM), F32), SDS((1, 128), F32)],
        compiler_params=_cp("arbitrary"))(p, p, gq, gk, cos, sin, dq, dk)


ATT_FWD_STEP = (256, 4)
ATT_BWD_STEP = (512, 2)


def _attn_fwd(q, k, v, name):
    n = q.shape[0]
    tq, step_heads = min(ATT_FWD_STEP[0], n), ATT_FWD_STEP[1]
    scale = ATT_HEAD_DIM ** -0.5
    gw = step_heads * ATT_HEAD_DIM
    parts = ATT_GROUP // step_heads

    def body(q_ref, k_ref, v_ref, o_ref):
        kk, vv = k_ref[0], v_ref[0]
        v_ones = jnp.concatenate([vv, jnp.ones_like(vv)], axis=1)
        outs = []
        for g in range(step_heads):
            s = _dot(q_ref[:, g * ATT_HEAD_DIM:(g + 1) * ATT_HEAD_DIM] * scale, kk, "nt")
            e = jnp.exp(s - jnp.max(s, axis=-1, keepdims=True))
            ov = _dot(e, v_ones)
            outs.append(ov[:, :ATT_HEAD_DIM] / ov[:, ATT_HEAD_DIM:])
        o_ref[...] = jnp.concatenate(outs, axis=-1).astype(o_ref.dtype)

    kv = pl.BlockSpec((1, n, ATT_HEAD_DIM), lambda h, i, pr: (h, 0, 0))
    qb = pl.BlockSpec((tq, gw), lambda h, i, pr: (i, h * parts + pr))
    return pl.pallas_call(
        body, name=name, grid=(ATT_KV_HEADS, n // tq, parts), in_specs=[qb, kv, kv],
        out_specs=qb, out_shape=SDS((n, ATT_Q_DIM), MXU_DTYPE),
        compiler_params=_cp("parallel", "parallel", "parallel"))(q, k, v)


def _attn_bwd(q, k, v, o, do, name):
    n = q.shape[0]
    tq, step_heads = min(ATT_BWD_STEP[0], n), ATT_BWD_STEP[1]
    scale = ATT_HEAD_DIM ** -0.5
    gw = step_heads * ATT_HEAD_DIM
    parts = ATT_GROUP // step_heads

    def body(q_ref, k_ref, v_ref, o_ref, do_ref, dq_ref, dk_ref, dv_ref):
        @pl.when(jnp.logical_and(pl.program_id(1) == 0, pl.program_id(2) == 0))
        def _():
            dk_ref[...] = jnp.zeros_like(dk_ref)
            dv_ref[...] = jnp.zeros_like(dv_ref)

        kk, vv = k_ref[0], v_ref[0]
        dqs = []
        dk_acc = jnp.zeros((ATT_HEAD_DIM, n), F32)
        dv_acc = jnp.zeros((ATT_HEAD_DIM, n), F32)
        for g in range(step_heads):
            sl = slice(g * ATT_HEAD_DIM, (g + 1) * ATT_HEAD_DIM)
            qg, dog = q_ref[:, sl] * scale, do_ref[:, sl].astype(F32)
            s = _dot(qg, kk, "nt")
            e = jnp.exp(s - jnp.max(s, axis=-1, keepdims=True))
            inv = 1.0 / jnp.sum(e, axis=-1, keepdims=True)
            delta = jnp.sum(dog * o_ref[:, sl].astype(F32), axis=-1, keepdims=True)
            dse = e * (_dot(dog, vv, "nt") - delta)
            dqs.append(_dot(dse, kk) * (inv * scale))
            dk_acc += _dot(qg.astype(F32) * inv, dse, "tn")
            dv_acc += _dot(dog * inv, e, "tn")
        dq_ref[...] = jnp.concatenate(dqs, axis=-1)
        dk_ref[0] += dk_acc
        dv_ref[0] += dv_acc

    kv = pl.BlockSpec((1, n, ATT_HEAD_DIM), lambda h, i, pr: (h, 0, 0))
    kvt = pl.BlockSpec((1, ATT_HEAD_DIM, n), lambda h, i, pr: (h, 0, 0))
    qb = pl.BlockSpec((tq, gw), lambda h, i, pr: (i, h * parts + pr))
    return pl.pallas_call(
        body, name=name, grid=(ATT_KV_HEADS, n // tq, parts), in_specs=[qb, kv, kv, qb, qb], out_specs=[qb, kvt, kvt],
        out_shape=[SDS((n, ATT_Q_DIM), F32), SDS((ATT_KV_HEADS, ATT_HEAD_DIM, n), F32), SDS((ATT_KV_HEADS, ATT_HEAD_DIM, n), F32)],
        compiler_params=_cp("parallel", "arbitrary", "arbitrary"))(q, k, v, o, do)


def _both_directions(mats, axis):
    fwd = np.concatenate(mats, axis=axis).astype(np.float32)
    bwd = np.concatenate([m[::-1, ::-1] for m in mats], axis=axis).astype(np.float32)
    return jnp.asarray(np.stack([fwd, bwd]), MXU_DTYPE)


def _hg_segments():
    c = HG_CHUNK
    t = np.arange(c)[:, None]
    r = np.arange(c)[None, :]
    mats = [(r <= t)]
    for lev in range(HG_LEVELS):
        h = c >> (lev + 1)
        mid = (t // (2 * h)) * (2 * h) + h - 1
        hi = (t // h) % 2 == 1
        mats.append(np.where(hi, (r > mid) & (r <= t), (r > t) & (r <= mid)))
    mats.append(r > t)
    return _both_directions(mats, 0)


def _hg_pair_sums():
    c = HG_CHUNK
    r = np.arange(c)[:, None]
    t = np.arange(c)[None, :]
    gp, gn = [t >= r], [t < r]
    for lev in range(HG_LEVELS):
        sh = HG_LEVELS - 1 - lev
        same = (r >> sh) == (t >> sh)
        gp.append(same & (t >= r))
        gn.append(same & (t < r))
    return _both_directions(gp, 1), _both_directions(gn, 1)


def _split_dot(mat, x):
    hi = x.astype(MXU_DTYPE)
    lo = (x - hi.astype(F32)).astype(MXU_DTYPE)
    return _dot(mat, hi) + _dot(mat, lo)


def _hg_gates(hq, z, a0, a1):
    q = hq * _sigmoid(hq)
    sg = _sigmoid(z)
    lb = _sigmoid(a0 - a1)
    f = lb + (1.0 - lb) * sg
    k = (1.0 - lb) * (1.0 - sg)
    return q, f, k, sg, lb


def _hg_level_masks():
    c = HG_CHUNK
    t = np.arange(c)
    later, same = [], []
    for lev in range(HG_LEVELS):
        sh = HG_LEVELS - 1 - lev
        later.append(np.broadcast_to((((t >> sh) & 1) == 1)[:, None], (c, HG_HEAD_DIM)))
        same.append((t[:, None] >> (sh + 1)) == (t[None, :] >> (sh + 1)))
    same.append(t[:, None] == t[None, :])
    later = np.stack(later).astype(np.float32)
    return jnp.asarray(np.stack([later, 1.0 - later]), F32), jnp.asarray(np.stack(same).astype(np.float32), F32)


def _hg_level(q, k, ex, later_ref, lev):
    e = ex[lev + 1]
    e_q = e * later_ref[0, lev]
    e_k = e - e_q
    return q * e_q, k * e_k, e_q, e_k


def _hg_intra(q, k, ex, later_ref, same_ref):
    a = same_ref[HG_LEVELS] * jnp.sum(q * k, axis=-1, keepdims=True)
    for lev in range(HG_LEVELS):
        qs, ks, _, _ = _hg_level(q, k, ex, later_ref, lev)
        a = a + same_ref[lev] * _dot(qs, ks, "nt")
    return a


def _hg_specs(n, with_time):
    c = HG_CHUNK
    nc = n // c

    def chunk(d, i):
        first = d if with_time else 1 - d
        return i + first * (nc - 1 - 2 * i)

    def pcols(off, dir_stride=0):
        return [pl.BlockSpec((c, HG_PAIR), lambda d, i, j=j: (chunk(d, i), off // HG_PAIR + dir_stride // HG_PAIR * d + j)) for j in range(2)]

    specs = dict(
        hq=pcols(OFF_HQ), v=pcols(OFF_HI), z=pcols(OFF_ZF, OFF_ZB - OFF_ZF),
        shared=pl.BlockSpec((c, HG_DIM), lambda d, i: (chunk(d, i), 0)),
        per_dir=pl.BlockSpec((1, c, HG_DIM), lambda d, i: (d, chunk(d, i), 0)),
        vec=pl.BlockSpec((1, 1, HG_DIM), lambda d, i: (d, 0, 0)),
        seg=pl.BlockSpec((1, (HG_LEVELS + 2) * c, c), lambda d, i: (d, 0, 0)),
        sums=pl.BlockSpec((1, c, (HG_LEVELS + 1) * c), lambda d, i: (d, 0, 0)),
        later=pl.BlockSpec((1, HG_LEVELS, c, HG_HEAD_DIM), lambda d, i: (d, 0, 0, 0)),
        same=pl.BlockSpec((HG_LEVELS + 1, c, c), lambda d, i: (0, 0, 0)),
        state=pl.BlockSpec((1, HG_HEADS, 1, HG_HEAD_DIM, HG_HEAD_DIM), lambda d, i: (d, 0, chunk(d, i), 0, 0)),
        weights=pl.BlockSpec((1, HG_HEADS, 1, c, c), lambda d, i: (d, 0, chunk(d, i), 0, 0)),
        levels=pl.BlockSpec((1, HG_HEADS, 1, HG_LEVELS, c, HG_HEAD_DIM), lambda d, i: (d, 0, chunk(d, i), 0, 0, 0)),
        kept=pl.BlockSpec((1, HG_KEPT, c, HG_DIM), lambda d, i: (d, 0, chunk(d, i), 0)))
    return nc, specs


def _hg_head(refs, hh):
    off = (hh % 2) * HG_HEAD_DIM
    return refs[hh // 2][:, off:off + HG_HEAD_DIM]


def _hg_lanes(hh):
    return slice(hh * HG_HEAD_DIM, (hh + 1) * HG_HEAD_DIM)


def _hg_exps(seg_ref, f):
    c = HG_CHUNK
    args = _split_dot(seg_ref[0], jnp.log(f))
    return [jnp.exp(args[j * c:(j + 1) * c]) for j in range(HG_LEVELS + 2)]


def _hg_last_row(a, mirrored):
    return jnp.where(mirrored, a[0:1, :], a[HG_CHUNK - 1:HG_CHUNK, :])


def _hgrn_fwd(p, a0, a1, seg, masks, name):
    n = p.shape[0]
    nc, sp = _hg_specs(n, True)

    def body(hq0, hq1, z0, z1, v0, v1, a0_ref, a1_ref, seg_ref, later_ref, same_ref, o_ref, s0_ref, a_ref, e_ref, g_ref, st):
        @pl.when(pl.program_id(1) == 0)
        def _():
            st[...] = jnp.zeros_like(st)

        mirrored = pl.program_id(0) == 1
        for hh in range(HG_HEADS):
            ln = _hg_lanes(hh)
            hqv = _hg_head((hq0, hq1), hh)
            q, f, k, sg, _ = _hg_gates(hqv, _hg_head((z0, z1), hh), a0_ref[0, :, ln], a1_ref[0, :, ln])
            vv = _hg_head((v0, v1), hh)
            ex = _hg_exps(seg_ref, f)
            for lev in range(HG_LEVELS):
                e_ref[0, hh, 0, lev] = ex[lev + 1].astype(e_ref.dtype)
            sq = _sigmoid(hqv)
            for j, kept in enumerate((q, k, f, sg, sq * (1.0 + hqv * (1.0 - sq)), ex[0], ex[HG_LEVELS + 1])):
                g_ref[0, j, :, ln] = kept
            a = _hg_intra(q, k, ex, later_ref, same_ref).astype(MXU_DTYPE)
            a_ref[0, hh, 0] = a
            s_t = st[hh]
            s0_ref[0, hh, 0] = s_t
            o_ref[0, :, ln] = _dot(a, vv) + _dot(q * ex[0], s_t, "nt")
            st[hh] = s_t * _hg_last_row(ex[0], mirrored) + _dot(vv, k * ex[HG_LEVELS + 1], "tn")

    return pl.pallas_call(
        body, name=name, grid=(2, nc), in_specs=sp["hq"] + sp["z"] + sp["v"] + [sp["vec"], sp["vec"], sp["seg"], sp["later"], sp["same"]],
        out_specs=[sp["per_dir"], sp["state"], sp["weights"], sp["levels"], sp["kept"]],
        out_shape=[SDS((2, n, HG_DIM), F32), SDS((2, HG_HEADS, nc, HG_HEAD_DIM, HG_HEAD_DIM), F32),
                   SDS((2, HG_HEADS, nc, HG_CHUNK, HG_CHUNK), MXU_DTYPE),
                   SDS((2, HG_HEADS, nc, HG_LEVELS, HG_CHUNK, HG_HEAD_DIM), MXU_DTYPE), SDS((2, HG_KEPT, n, HG_DIM), F32)],
        scratch_shapes=[pltpu.VMEM((HG_HEADS, HG_HEAD_DIM, HG_HEAD_DIM), F32)],
        compiler_params=_cp("parallel", "arbitrary"))(p, p, p, p, p, p, a0, a1, seg, *masks)


def _hgrn_bwd(p, a0, a1, masks, gp, gn, do, s0, a, e, kept, name):
    n = p.shape[0]
    nc, sp = _hg_specs(n, False)


    def body(v0, v1, a0_ref, a1_ref, later_ref, same_ref, gp_ref, gn_ref, do_ref, s0_ref, a_ref, e_ref, g_ref,
             dhq_ref, dz_ref, dv_ref, dlb_ref, rt):
        @pl.when(pl.program_id(1) == 0)
        def _():
            rt[...] = jnp.zeros_like(rt)
            dlb_ref[...] = jnp.zeros_like(dlb_ref)

        mirrored = pl.program_id(0) == 1
        for hh in range(HG_HEADS):
            ln = _hg_lanes(hh)
            q, k, f, sg, dsilu, e_first, e_last = (g_ref[0, j, :, ln] for j in range(HG_KEPT))
            lb = _sigmoid(a0_ref[0, :, ln] - a1_ref[0, :, ln])
            vv, dov = _hg_head((v0, v1), hh), do_ref[:, ln]
            ex = [e_first] + [e_ref[0, hh, 0, lev].astype(F32) for lev in range(HG_LEVELS)] + [e_last]
            a = a_ref[0, hh, 0]
            da = _dot(dov, vv, "nt")
            diag = jnp.sum(dov * vv, axis=-1, keepdims=True)
            s_t = s0_ref[0, hh, 0]
            r_t = rt[hh]
            k_end = k * ex[HG_LEVELS + 1]
            dv_ref[0, :, ln] = _dot(a, dov, "tn") + _dot(k_end, r_t, "nt")
            dq_inter = ex[0] * _dot(dov, s_t)
            dk_inter = ex[HG_LEVELS + 1] * _dot(vv, r_t)
            dq = diag * k + dq_inter
            dk = diag * q + dk_inter
            q_terms, k_terms = [q * dq_inter], [k * dk_inter]
            for lev in range(HG_LEVELS):
                qs, ks, e_q, e_k = _hg_level(q, k, ex, later_ref, lev)
                pairs = da * same_ref[lev]
                q_part = e_q * _dot(pairs, ks)
                k_part = e_k * _dot(pairs, qs, "tn")
                dq, dk = dq + q_part, dk + k_part
                q_terms.append(q * q_part)
                k_terms.append(k * k_part)
            decay = _hg_last_row(ex[0], mirrored)
            rt[hh] = r_t * decay + _dot(dov, q * ex[0], "tn")
            later = decay * jnp.sum(s_t * r_t, axis=0, keepdims=True)
            dlf = _dot(gp_ref[0], jnp.concatenate(q_terms, axis=0)) + _dot(gn_ref[0], jnp.concatenate(k_terms, axis=0)) + later
            df = dlf / f - dk
            dz_ref[0, :, ln] = df * (1.0 - lb) * sg * (1.0 - sg)
            dlb_ref[0, :, ln] += jnp.sum(df * (1.0 - sg), axis=0, keepdims=True)
            dhq_ref[0, :, ln] = dq * dsilu

    out = SDS((2, n, HG_DIM), F32)
    return pl.pallas_call(
        body, name=name, grid=(2, nc),
        in_specs=sp["v"] + [sp["vec"], sp["vec"], sp["later"], sp["same"], sp["sums"], sp["sums"],
                            sp["shared"], sp["state"], sp["weights"], sp["levels"], sp["kept"]],
        out_specs=[sp["per_dir"], sp["per_dir"], sp["per_dir"], sp["vec"]], out_shape=[out, out, out, SDS((2, 1, HG_DIM), F32)],
        scratch_shapes=[pltpu.VMEM((HG_HEADS, HG_HEAD_DIM, HG_HEAD_DIM), F32)],
        compiler_params=_cp("parallel", "arbitrary"))(p, p, a0, a1, *masks, gp, gn, do, s0, a, e, kept)


def _hg_post(o2, p, g, name):
    n = p.shape[0]
    tr = min(ROW_TILE, n)
    w = 2 * HG_HEAD_DIM

    def body(of_ref, ob_ref, hg_ref, g_ref, o_ref):
        for j in range(2):
            sl = slice(j * HG_HEAD_DIM, (j + 1) * HG_HEAD_DIM)
            o = of_ref[0, :, sl] + ob_ref[0, :, sl]
            hg = hg_ref[:, sl]
            o_ref[:, sl] = (o * _rstd(o) * g_ref[...] * (hg * _sigmoid(hg))).astype(o_ref.dtype)

    blk = pl.BlockSpec((tr, w), lambda i, j: (i, j))
    dirs = [pl.BlockSpec((1, tr, w), lambda i, j, d=d: (d, i, j)) for d in range(2)]
    return pl.pallas_call(
        body, name=name, grid=(n // tr, HG_DIM // w),
        in_specs=dirs + [pl.BlockSpec((tr, w), lambda i, j: (i, OFF_HG // w + j)), pl.BlockSpec((1, HG_HEAD_DIM), lambda i, j: (0, 0))],
        out_specs=blk, out_shape=SDS((n, HG_DIM), MXU_DTYPE), compiler_params=_cp("parallel", "parallel"))(o2, o2, p, g)


def _hg_post_bwd(o2, p, g, dcat, name, after=()):
    n = p.shape[0]
    tr = min(ROW_TILE, n)
    w = 2 * HG_HEAD_DIM
    after, after_specs = _unread(after)

    def body(of_ref, ob_ref, hg_ref, g_ref, d_ref, *rest):
        do_ref, dhg_ref, dg_ref = rest[len(after):]

        @pl.when(pl.program_id(1) == 0)
        def _():
            dg_ref[...] = jnp.zeros_like(dg_ref)

        for j in range(2):
            sl = slice(j * HG_HEAD_DIM, (j + 1) * HG_HEAD_DIM)
            o = of_ref[0, :, sl] + ob_ref[0, :, sl]
            hg = hg_ref[:, sl]
            d = d_ref[:, sl].astype(F32)
            sg = _sigmoid(hg)
            on = o * _rstd(o) * g_ref[...]
            dhg_ref[:, sl] = (d * on * sg * (1.0 + hg * (1.0 - sg))).astype(dhg_ref.dtype)
            dx, dg = _rms_bwd(o, g_ref[...], d * hg * sg)
            do_ref[:, sl] = dx
            dg_ref[0, :, sl] += dg

    blk = pl.BlockSpec((tr, w), lambda j, i: (i, j))
    dirs = [pl.BlockSpec((1, tr, w), lambda j, i, d=d: (d, i, j)) for d in range(2)]
    return pl.pallas_call(
        body, name=name, grid=(HG_DIM // w, n // tr),
        in_specs=dirs + [pl.BlockSpec((tr, w), lambda j, i: (i, OFF_HG // w + j)), pl.BlockSpec((1, HG_HEAD_DIM), lambda j, i: (0, 0)),
                         pl.BlockSpec((tr, w), lambda j, i: (i, ATT_Q_DIM // w + j))] + after_specs,
        out_specs=[blk, blk, pl.BlockSpec((1, 1, w), lambda j, i: (j, 0, 0))],
        out_shape=[SDS((n, HG_DIM), F32), SDS((n, HG_DIM), MXU_DTYPE), SDS((HG_DIM // w, 1, w), F32)],
        compiler_params=_cp("parallel", "arbitrary"))(o2, o2, p, g, dcat, *after)


XATT_TQ = 512


def _xattn_fwd(q, kv, name):
    n, nm = q.shape[0], kv.shape[0]
    tq = min(XATT_TQ, n)
    scale = X_HEAD_DIM ** -0.5

    def body(q_ref, k_ref, v_ref, o_ref):
        s = _dot(q_ref[...], k_ref[...], "nt") * scale
        e = jnp.exp(s - jnp.max(s, axis=-1, keepdims=True))
        o_ref[...] = _dot(e / jnp.sum(e, axis=-1, keepdims=True), v_ref[...]).astype(o_ref.dtype)

    qb = pl.BlockSpec((tq, X_HEAD_DIM), lambda h, i: (i, h))
    return pl.pallas_call(
        body, name=name, grid=(X_HEADS, n // tq),
        in_specs=[qb, pl.BlockSpec((nm, X_HEAD_DIM), lambda h, i: (0, h)), pl.BlockSpec((nm, X_HEAD_DIM), lambda h, i: (0, X_HEADS + h))],
        out_specs=qb, out_shape=SDS(q.shape, MXU_DTYPE), compiler_params=_cp("parallel", "parallel"))(q, kv, kv)


def _xattn_bwd(q, kv, do, name, after=()):
    n, nm = q.shape[0], kv.shape[0]
    tq = min(XATT_TQ, n)
    scale = X_HEAD_DIM ** -0.5
    after, after_specs = _unread(after)

    def body(q_ref, k_ref, v_ref, do_ref, *rest):
        dq_ref, dk_ref, dv_ref = rest[len(after):]

        @pl.when(pl.program_id(1) == 0)
        def _():
            dk_ref[...] = jnp.zeros_like(dk_ref)
            dv_ref[...] = jnp.zeros_like(dv_ref)

        qv, dov = q_ref[...], do_ref[...]
        s = _dot(qv, k_ref[...], "nt") * scale
        e = jnp.exp(s - jnp.max(s, axis=-1, keepdims=True))
        p = e / jnp.sum(e, axis=-1, keepdims=True)
        dp = _dot(dov, v_ref[...], "nt")
        ds = p * (dp - jnp.sum(p * dp, axis=-1, keepdims=True)) * scale
        dq_ref[...] = _dot(ds, k_ref[...]).astype(dq_ref.dtype)
        dk_ref[...] += _dot(ds, qv, "tn")
        dv_ref[...] += _dot(p, dov, "tn")

    qb = pl.BlockSpec((tq, X_HEAD_DIM), lambda h, i: (i, h))
    kb = pl.BlockSpec((nm, X_HEAD_DIM), lambda h, i: (0, h))
    return pl.pallas_call(
        body, name=name, grid=(X_HEADS, n // tq),
        in_specs=[qb, kb, pl.BlockSpec((nm, X_HEAD_DIM), lambda h, i: (0, X_HEADS + h)), qb] + after_specs, out_specs=[qb, kb, kb],
        out_shape=[SDS(q.shape, MXU_DTYPE), SDS((nm, X_HEADS * X_HEAD_DIM), F32), SDS((nm, X_HEADS * X_HEAD_DIM), F32)],
        compiler_params=_cp("parallel", "arbitrary"))(q, kv, kv, do, *after)


def _edge_rows(shape):
    row = lax.broadcasted_iota(jnp.int32, shape, 0)
    return row == 0, row == shape[0] - 1


def _shift_rows(u, down, edges):
    if down:
        return jnp.where(edges[0], 0.0, pltpu.roll(u, 1, axis=0))
    return jnp.where(edges[1], 0.0, pltpu.roll(u, u.shape[0] - 1, axis=0))


def _conv(u, w, b, edges):
    return b + _shift_rows(u, True, edges) * w[0:1, :] + u * w[1:2, :] + _shift_rows(u, False, edges) * w[2:3, :]


def _ff_specs(n):
    gate = lambda rows: pl.BlockSpec((rows, FF_COLS), lambda j: (0, j))
    val = lambda rows: pl.BlockSpec((rows, FF_COLS), lambda j: (0, FF_BLOCKS + j))
    return [gate(n), val(n), gate(3), val(3), gate(1), val(1)], gate


def _conv_gate(u, cw, cb, name):
    n = u.shape[0]
    ins, gate_blk = _ff_specs(n)

    def body(ug_ref, uv_ref, wg_ref, wv_ref, bg_ref, bv_ref, o_ref):
        edges = _edge_rows(ug_ref.shape)
        gate = _conv(ug_ref[...], wg_ref[...], bg_ref[...], edges)
        val = _conv(uv_ref[...], wv_ref[...], bv_ref[...], edges)
        o_ref[...] = (gate * _sigmoid(gate) * val).astype(o_ref.dtype)

    return pl.pallas_call(
        body, name=name, grid=(FF_BLOCKS,), in_specs=ins, out_specs=gate_blk(n), out_shape=SDS((n, D_FF), MXU_DTYPE),
        compiler_params=_cp("parallel"))(u, u, cw, cw, cb, cb)


def _conv_gate_bwd(u, cw, cb, da, name, after=()):
    n = u.shape[0]
    ins, gate_blk = _ff_specs(n)
    after, after_specs = _unread(after)

    def side(dacc, u, w, edges, du_ref, dw_ref, db_ref):
        nxt, prv = _shift_rows(dacc, False, edges), _shift_rows(dacc, True, edges)
        du_ref[...] = (nxt * w[0:1, :] + dacc * w[1:2, :] + prv * w[2:3, :]).astype(du_ref.dtype)
        db_ref[...] = jnp.sum(dacc, axis=0, keepdims=True)
        dw_ref[0:1, :] = jnp.sum(nxt * u, axis=0, keepdims=True)
        dw_ref[1:2, :] = jnp.sum(dacc * u, axis=0, keepdims=True)
        dw_ref[2:3, :] = jnp.sum(prv * u, axis=0, keepdims=True)

    def body(ug_ref, uv_ref, wg_ref, wv_ref, bg_ref, bv_ref, da_ref, *rest):
        dug_ref, duv_ref, dwg_ref, dwv_ref, dbg_ref, dbv_ref = rest[len(after):]
        ug, uv = ug_ref[...], uv_ref[...]
        edges = _edge_rows(ug.shape)
        gate = _conv(ug, wg_ref[...], bg_ref[...], edges)
        val = _conv(uv, wv_ref[...], bv_ref[...], edges)
        sg = _sigmoid(gate)
        dav = da_ref[...].astype(F32)
        side(dav * val * sg * (1.0 + gate * (1.0 - sg)), ug, wg_ref[...], edges, dug_ref, dwg_ref, dbg_ref)
        side(dav * gate * sg, uv, wv_ref[...], edges, duv_ref, dwv_ref, dbv_ref)

    return pl.pallas_call(
        body, name=name, grid=(FF_BLOCKS,), in_specs=ins + [gate_blk(n)] + after_specs,
        out_specs=[gate_blk(n), gate_blk(n), gate_blk(3), gate_blk(3), gate_blk(1), gate_blk(1)],
        out_shape=[SDS((n, D_FF), MXU_DTYPE)] * 2 + [SDS((3, D_FF), F32)] * 2 + [SDS((1, D_FF), F32)] * 2,
        compiler_params=_cp("parallel"))(u, u, cw, cw, cb, cb, da, *after)


def _adamw(w, g, m, v, name):
    r, c = w.shape[-2:]
    tr = _row_tile(r, ELEMENTWISE_ROWS)
    assert w.ndim == 2 or w.shape[:-2] == (1,), (name, w.shape)

    def body(w_ref, g_ref, m_ref, v_ref, d_ref, mo_ref, vo_ref, go_ref):
        gv = g_ref[...]
        go_ref[...] = gv
        mn = ADAM_B1 * m_ref[...] + (1.0 - ADAM_B1) * gv
        vn = ADAM_B2 * v_ref[...] + (1.0 - ADAM_B2) * gv * gv
        m_hat = mn / (1.0 - ADAM_B1 ** ADAM_STEP)
        v_hat = vn / (1.0 - ADAM_B2 ** ADAM_STEP)
        d_ref[...] = -ADAM_LR * (m_hat / (jnp.sqrt(v_hat) + ADAM_EPS) + ADAM_WD * w_ref[...])
        mo_ref[...] = mn
        vo_ref[...] = vn

    blk = pl.BlockSpec((tr, c), lambda i: (i, 0)) if w.ndim == 2 else pl.BlockSpec((1, tr, c), lambda i: (0, i, 0))
    out = SDS(w.shape, F32)
    return pl.pallas_call(body, name=name, grid=(r // tr,), in_specs=[blk] * 4, out_specs=[blk] * 4, out_shape=[out] * 4,
                          compiler_params=_cp("parallel"))(w, g, m, v)


ANY = pl.BlockSpec(memory_space=pl.ANY)


def _place():
    x, y, c = lax.axis_index("x"), lax.axis_index("y"), lax.axis_index("c")
    return x, y, c, [(1 - x, y), (x, 1 - y), (1 - x, 1 - y)]


def _gather_shards(shards, name):
    nt = len(shards)

    def body(*refs):
        ins, outs = refs[:nt], refs[nt:2 * nt]
        send, recv, fsend, frecv, osend, orecv = refs[2 * nt:]
        x, y, c, chips = _place()
        me = 2 * x + y

        def half(t, chip, cc):
            h = ins[t].shape[0] // 2
            return outs[t].at[chip, pl.ds(cc * h, h)]

        def ici(t, j):
            cx, cy = chips[j]
            h = ins[t].shape[0] // 2
            return pltpu.make_async_remote_copy(src_ref=ins[t].at[pl.ds(c * h, h)], dst_ref=half(t, me, c),
                                                send_sem=send.at[t, j], recv_sem=recv.at[t, j], device_id=(cx, cy, c), device_id_type=MESH)

        def landed(t, j):
            cx, cy = chips[j]
            blk = half(t, 2 * cx + cy, c)
            return pltpu.make_async_remote_copy(src_ref=blk, dst_ref=blk, send_sem=send.at[t, j], recv_sem=recv.at[t, j],
                                                device_id=(cx, cy, c), device_id_type=MESH)

        def d2d(t, j, cc):
            cx, cy = chips[j]
            blk = half(t, 2 * cx + cy, cc)
            return pltpu.make_async_remote_copy(src_ref=blk, dst_ref=blk, send_sem=fsend.at[t, j], recv_sem=frecv.at[t, j],
                                                device_id=(x, y, 1 - c), device_id_type=MESH)

        own = [pltpu.make_async_remote_copy(src_ref=ins[t], dst_ref=outs[t].at[me], send_sem=osend.at[t], recv_sem=orecv.at[t],
                                            device_id=(x, y, 1 - c), device_id_type=MESH) for t in range(nt)]
        for t in range(nt):
            for j in range(3):
                ici(t, j).start()
        for cp in own:
            cp.start()
        for t in range(nt):
            for j in range(3):
                landed(t, j).wait_recv()
                d2d(t, j, c).start()
        for t in range(nt):
            for j in range(3):
                d2d(t, j, 1 - c).wait_recv()
        for t in range(nt):
            for j in range(3):
                ici(t, j).wait_send()
                d2d(t, j, c).wait_send()
        for cp in own:
            cp.wait()

    return pl.pallas_call(
        body, name=name, in_specs=[ANY] * nt, out_specs=[ANY] * nt,
        out_shape=[SDS((4,) + s.shape, s.dtype) for s in shards],
        scratch_shapes=[pltpu.SemaphoreType.DMA((nt, 3))] * 4 + [pltpu.SemaphoreType.DMA((nt,))] * 2,
        compiler_params=pltpu.CompilerParams(has_side_effects=True))(*shards)


def _join_halves(bufs, name):
    nt = len(bufs)

    def body(*refs):
        outs = refs[nt:2 * nt]
        send, recv = refs[2 * nt:]
        x, y, c, _ = _place()
        cps = [pltpu.make_async_remote_copy(src_ref=outs[t].at[c], dst_ref=outs[t].at[c], send_sem=send.at[t], recv_sem=recv.at[t],
                                            device_id=(x, y, 1 - c), device_id_type=MESH) for t in range(nt)]
        for cp in cps:
            cp.start()
        for t in range(nt):
            theirs = outs[t].at[1 - c]
            pltpu.make_async_remote_copy(src_ref=theirs, dst_ref=theirs, send_sem=send.at[t], recv_sem=recv.at[t],
                                         device_id=(x, y, 1 - c), device_id_type=MESH).wait_recv()
        for cp in cps:
            cp.wait_send()

    return pl.pallas_call(
        body, name=name, in_specs=[ANY] * nt, out_specs=[ANY] * nt, out_shape=[SDS(b.shape, b.dtype) for b in bufs],
        input_output_aliases={t: t for t in range(nt)},
        scratch_shapes=[pltpu.SemaphoreType.DMA((nt,))] * 2,
        compiler_params=pltpu.CompilerParams(has_side_effects=True))(*bufs)


def _exchange_small(v, reduce, name, after=()):
    rows = v.shape[0]
    after, after_specs = _unread(after)

    def body(v_ref, *rest):
        o_ref, buf, send, recv = rest[-4:]
        x, y, c, _ = _place()
        me = 4 * x + 2 * y + c
        buf[me] = v_ref[...]

        def peer(dx, dy, dc):
            return (1 - x if dx else x, 1 - y if dy else y, 1 - c if dc else c)

        peers = [(dx, dy, dc) for dx in range(2) for dy in range(2) for dc in range(2) if (dx, dy, dc) != (0, 0, 0)]
        cps = []
        for j, (dx, dy, dc) in enumerate(peers):
            cps.append(pltpu.make_async_remote_copy(src_ref=v_ref, dst_ref=buf.at[me], send_sem=send.at[j], recv_sem=recv.at[j],
                                                    device_id=peer(dx, dy, dc), device_id_type=MESH))
        for cp in cps:
            cp.start()
        for j, (dx, dy, dc) in enumerate(peers):
            px, py, pc = peer(dx, dy, dc)
            blk = buf.at[4 * px + 2 * py + pc]
            pltpu.make_async_remote_copy(src_ref=blk, dst_ref=blk, send_sem=send.at[j], recv_sem=recv.at[j],
                                         device_id=(px, py, pc), device_id_type=MESH).wait_recv()
        for cp in cps:
            cp.wait_send()
        if reduce:
            acc = buf[0]
            for j in range(1, 8):
                acc = acc + buf[j]
            o_ref[...] = acc
        else:
            o_ref[...] = buf[...]

    vm = pl.BlockSpec(memory_space=pltpu.VMEM)
    return pl.pallas_call(
        body, name=name, in_specs=[vm] + after_specs, out_specs=vm, out_shape=SDS((rows, 128) if reduce else (8, rows, 128), F32),
        scratch_shapes=[pltpu.VMEM((8, rows, 128), F32), pltpu.SemaphoreType.DMA((7,)), pltpu.SemaphoreType.DMA((7,))],
        compiler_params=pltpu.CompilerParams(has_side_effects=True))(v, *after)


HBM = pl.BlockSpec(memory_space=pltpu.HBM)
SEM = pl.BlockSpec(memory_space=pltpu.SEMAPHORE)
TOKEN = pl.BlockSpec(memory_space=pltpu.VMEM)
TOKEN_SHAPE = SDS((8, 128), F32)
PEERS = 7


def _in_hbm(a):
    return pltpu.with_memory_space_constraint(a, pltpu.HBM)


def _split_params():
    return pltpu.CompilerParams(has_side_effects=pltpu.SideEffectType.DATAFLOW_SIDE_EFFECTING)


def _gather_start(shards, name, after=()):
    nt = len(shards)
    after, after_specs = _unread(after)

    def body(*refs):
        ins, lands = refs[:nt], refs[nt:2 * nt]
        outs = refs[2 * nt + len(after):]
        sends, recvs = outs[:nt], outs[nt:2 * nt]
        x, y, c, chips = _place()
        me = 2 * x + y
        for t in range(nt):
            h = ins[t].shape[0] // 2
            mine = pl.ds(c * h, h)
            for j, (cx, cy) in enumerate(chips):
                for dc in range(2):
                    pltpu.make_async_remote_copy(src_ref=ins[t].at[mine], dst_ref=lands[t].at[me, mine], send_sem=sends[t].at[2 * j + dc],
                                                 recv_sem=recvs[t].at[2 * j + c], device_id=(cx, cy, dc), device_id_type=MESH).start()
            pltpu.make_async_remote_copy(src_ref=ins[t], dst_ref=lands[t].at[me], send_sem=sends[t].at[PEERS - 1], recv_sem=recvs[t].at[PEERS - 1],
                                         device_id=(x, y, 1 - c), device_id_type=MESH).start()
        outs[-1][...] = jnp.zeros(TOKEN_SHAPE.shape, F32)

    lands = [lax.empty((4,) + s.shape, s.dtype) for s in shards]
    out = pl.pallas_call(
        body, name=name, in_specs=[HBM] * (2 * nt) + after_specs, out_specs=[SEM] * (2 * nt) + [HBM] * (2 * nt) + [TOKEN],
        out_shape=[pltpu.SemaphoreType.DMA((PEERS,))] * (2 * nt)
        + [pltpu.HBM(s.shape, s.dtype) for s in shards] + [pltpu.HBM(l.shape, l.dtype) for l in lands] + [TOKEN_SHAPE],
        input_output_aliases={t: 2 * nt + t for t in range(2 * nt)}, compiler_params=_split_params())(
            *[_in_hbm(s) for s in shards], *[_in_hbm(l) for l in lands], *after)
    return out[:nt], out[nt:2 * nt], out[2 * nt:3 * nt], out[3 * nt:4 * nt], out[-1]


def _gather_wait(sends, recvs, shards, lands, after, name):
    nt = len(shards)

    def body(*refs):
        ins, lands_ref = refs[:nt], refs[nt:2 * nt]
        send_refs, recv_refs = refs[2 * nt:3 * nt], refs[3 * nt:4 * nt]
        x, y, c, chips = _place()
        for t in range(nt):
            h = ins[t].shape[0] // 2
            for j, (cx, cy) in enumerate(chips):
                for cs in range(2):
                    blk = lands_ref[t].at[2 * cx + cy, pl.ds(cs * h, h)]
                    pltpu.make_async_remote_copy(src_ref=blk, dst_ref=blk, send_sem=send_refs[t].at[2 * j + cs], recv_sem=recv_refs[t].at[2 * j + cs],
                                                 device_id=(cx, cy, cs), device_id_type=MESH).wait()
            blk = lands_ref[t].at[2 * x + y]
            pltpu.make_async_remote_copy(src_ref=blk, dst_ref=blk, send_sem=send_refs[t].at[PEERS - 1], recv_sem=recv_refs[t].at[PEERS - 1],
                                         device_id=(x, y, 1 - c), device_id_type=MESH).wait()

    out = pl.pallas_call(
        body, name=name, in_specs=[HBM] * (2 * nt) + [SEM] * (2 * nt) + [ANY], out_specs=[HBM] * (2 * nt),
        out_shape=[pltpu.HBM(s.shape, s.dtype) for s in shards] + [pltpu.HBM(l.shape, l.dtype) for l in lands],
        input_output_aliases={t: t for t in range(2 * nt)}, compiler_params=_split_params())(*shards, *lands, *sends, *recvs, after)
    return out[nt:]


def _scatter_start(g, name):
    _, r, c_ = g.shape
    h = r // 2

    def body(g_ref, land, send, recv, g_thru, land_thru, token):
        x, y, c, chips = _place()
        for j, (cx, cy) in enumerate(chips):
            for dc in range(2):
                pltpu.make_async_remote_copy(src_ref=g_ref.at[2 * cx + cy, pl.ds(dc * h, h)], dst_ref=land.at[2 * j + c], send_sem=send.at[2 * j + dc],
                                             recv_sem=recv.at[2 * j + c], device_id=(cx, cy, dc), device_id_type=MESH).start()
        pltpu.make_async_remote_copy(src_ref=g_ref.at[2 * x + y, pl.ds((1 - c) * h, h)], dst_ref=land.at[PEERS - 1], send_sem=send.at[PEERS - 1],
                                     recv_sem=recv.at[PEERS - 1], device_id=(x, y, 1 - c), device_id_type=MESH).start()
        token[...] = jnp.zeros(TOKEN_SHAPE.shape, F32)

    land = lax.empty((PEERS, h, c_), g.dtype)
    return pl.pallas_call(
        body, name=name, in_specs=[HBM, HBM], out_specs=[SEM, SEM, HBM, HBM, TOKEN],
        out_shape=[pltpu.SemaphoreType.DMA((PEERS,)), pltpu.SemaphoreType.DMA((PEERS,)), pltpu.HBM(g.shape, g.dtype),
                   pltpu.HBM(land.shape, land.dtype), TOKEN_SHAPE],
        input_output_aliases={0: 2, 1: 3}, compiler_params=_split_params())(_in_hbm(g), _in_hbm(land))


def _scatter_wait(started, after, name):
    nt = len(started)

    def body(*refs):
        lands = refs[nt:2 * nt]
        sends, recvs = refs[2 * nt:3 * nt], refs[3 * nt:4 * nt]
        x, y, c, chips = _place()
        peers = [(cx, cy, dc) for cx, cy in chips for dc in range(2)] + [(x, y, 1 - c)]
        for t in range(nt):
            for k, peer in enumerate(peers):
                blk = lands[t].at[k]
                pltpu.make_async_remote_copy(src_ref=blk, dst_ref=blk, send_sem=sends[t].at[k], recv_sem=recvs[t].at[k],
                                             device_id=peer, device_id_type=MESH).wait()

    gs, lands = [s[2] for s in started], [s[3] for s in started]
    after, after_specs = _unread(after)
    out = pl.pallas_call(
        body, name=name, in_specs=[HBM] * (2 * nt) + [SEM] * (2 * nt) + after_specs, out_specs=[HBM] * (2 * nt),
        out_shape=[pltpu.HBM(a.shape, a.dtype) for a in gs + lands],
        input_output_aliases={t: t for t in range(2 * nt)}, compiler_params=_split_params())(
            *gs, *lands, *[s[0] for s in started], *[s[1] for s in started], *after)
    return out[:nt], out[nt:]


def _sum_devices(g, land, me, core, name):
    npeer, h, c = land.shape
    tr = _row_tile(h, 2 * ELEMENTWISE_ROWS)
    steps = h // tr

    def body(ix_ref, own_ref, land_ref, o_ref):
        acc = own_ref[0].astype(F32)
        for j in range(npeer):
            acc = acc + land_ref[j].astype(F32)
        o_ref[0] = acc

    grid_spec = pltpu.PrefetchScalarGridSpec(
        num_scalar_prefetch=1, grid=(steps,),
        in_specs=[pl.BlockSpec((1, tr, c), lambda i, ix: (ix[0], ix[1] * steps + i, 0)), pl.BlockSpec((npeer, tr, c), lambda i, ix: (0, i, 0))],
        out_specs=pl.BlockSpec((1, tr, c), lambda i, ix: (ix[1], i, 0)))
    return pl.pallas_call(body, name=name, grid_spec=grid_spec, out_shape=SDS((2, h, c), F32),
                          compiler_params=_cp("parallel"))(jnp.stack([me, core]), g, land)


def _pack_small(parts):
    flat = jnp.concatenate([p.reshape(-1) for p in parts])
    total = flat.shape[0]
    rows = -(-total // 1024) * 8
    return jnp.pad(flat, (0, rows * 128 - total)).reshape(rows, 128)


def _unpack_small(packed, shapes):
    flat = packed.reshape(-1)
    out, off = [], 0
    for s in shapes:
        size = int(np.prod(s))
        out.append(flat[off:off + size].reshape(s))
        off += size
    return out


def _local_step(x, mem, target, w_in, first_after, mid_weights, ffn_weights, on_grad, gains, conv_w, conv_b, hg_lb):
    n = x.shape[0]
    cos, sin = _rope_tables(n)
    seg = _hg_segments()
    gp, gn = _hg_pair_sums()
    masks = _hg_level_masks()
    gq2 = jnp.tile(gains["q_norm_g"], (1, 2))
    gk2 = jnp.tile(gains["k_norm_g"], (1, 2))
    a0 = hg_lb[:, 0:1, :]
    a1 = hg_lb[:, 1:2, :]

    p, h1 = _norm_mm(x, gains["pre_mix_g"], w_in, F32, TOKEN_TILE, 1664, "in_proj", after=(first_after,))
    qr, kr = _qk_prep(p, gq2, gk2, cos, sin, "qk_prep")
    heads = lambda a: a.reshape(n, ATT_KV_HEADS, ATT_HEAD_DIM).transpose(1, 0, 2)
    kh = heads(kr)
    vh = heads(p[:, OFF_AV:OFF_AV + ATT_KV_DIM].astype(MXU_DTYPE))
    att = _attn_fwd(qr, kh, vh, "attn_fwd")
    o2, s0, hg_a, hg_e, hg_kept = _hgrn_fwd(p, a0, a1, seg, masks, "hgrn_fwd")
    rec = _hg_post(o2, p, gains["hg_out_norm_g"], "hg_post")
    cat = jnp.concatenate([att, rec], axis=1)
    w_out, w_xq, w_xkv, w_xo = mid_weights(cat)
    mixed, x1 = _mm_resid_norm(cat, w_out, x, gains["post_mix_g"], 512, "out_proj_resid")
    xq, h2 = _norm_mm(x1, gains["pre_x_g"], w_xq, MXU_DTYPE, TOKEN_TILE, 1024, "xq_proj")
    kv, mn = _norm_mm(mem, gains["mem_norm_g"], w_xkv, MXU_DTYPE, 256, 2048, "xkv_proj")
    ox = _xattn_fwd(xq, kv, "xattn_fwd")
    xo, x2 = _mm_resid_norm(ox, w_xo, x1, gains["post_x_g"], 512, "xo_proj_resid")
    w_up = ffn_weights("w_up", x2)
    u, h3 = _norm_mm(x2, gains["pre_ffn_g"], w_up, F32, TOKEN_TILE, 1408, "up_proj")
    act = _conv_gate(u, conv_w, conv_b, "conv_gate")
    w_down = ffn_weights("w_down", act)
    dn, d3, loss = _mm_resid_norm(act, w_down, x2, gains["post_ffn_g"], 512, "down_proj_resid_loss", target=target)

    gs = {}
    d_act, d_dn, gs["post_ffn_g"] = _norm_bwd_mm(dn, gains["post_ffn_g"], d3, w_down, F32, 512, 1408, "ffn_post_bwd_down_dx")
    tok = on_grad("w_down", _mm(act, d_dn, "tn", WIRE_DTYPE, 1408, 1024, "down_dw"))
    du_g, du_v, dcw_g, dcw_v, dcb_g, dcb_v = _conv_gate_bwd(u, conv_w, conv_b, d_act, "conv_gate_bwd", after=(tok,))
    gs["conv_w"] = jnp.concatenate([dcw_g, dcw_v], axis=1)
    gs["conv_b"] = jnp.concatenate([dcb_g, dcb_v], axis=1)
    ff_shard = w_up.shape[2]
    g_up = _dw_by_owner(h3, du_g, ff_shard, 0, None, WEIGHT_ROW_TILE, "up_dw_gate")
    tok = on_grad("w_up", _dw_by_owner(h3, du_v, ff_shard, 2, g_up, WEIGHT_ROW_TILE, "up_dw_value"))
    d2, gs["pre_ffn_g"] = _dx_norm_bwd([(du_g, 0), (du_g, 1), (du_v, 0), (du_v, 1)], w_up, x2, gains["pre_ffn_g"], d3, 512,
                                       "up_dx_pre_bwd", after=(tok,))
    d_ox, d_xo, gs["post_x_g"] = _norm_bwd_mm(xo, gains["post_x_g"], d2, w_xo, MXU_DTYPE, 512, 1024, "x_post_bwd_xo_dx")
    tok = on_grad("w_xo", _mm(ox, d_xo, "tn", WIRE_DTYPE, WEIGHT_ROW_TILE, 1024, "xo_dw"))
    d_xq, d_k, d_v = _xattn_bwd(xq, kv, d_ox, "xattn_bwd", after=(tok,))
    d_kv = jnp.concatenate([d_k, d_v], axis=1).astype(MXU_DTYPE)
    tok = on_grad("w_xq", _mm(h2, d_xq, "tn", WIRE_DTYPE, WEIGHT_ROW_TILE, 1024, "xq_dw"))
    tok_kv = on_grad("w_xkv", _dw_by_owner(mn, d_kv, w_xkv.shape[2], 0, None, WEIGHT_ROW_TILE, "xkv_dw"))
    d1, gs["pre_x_g"] = _dx_norm_bwd([(d_xq, 0)], w_xq[None], x1, gains["pre_x_g"], d2, 512, "xq_dx_pre_bwd", after=(tok, tok_kv))
    d_mn = _mm_nt_parts([(d_kv, s) for s in range(4)], w_xkv, F32, 256, 1024, "xkv_dx")
    _, gs["mem_norm_g"] = _norm_bwd(mem, gains["mem_norm_g"], d_mn, None, MXU_DTYPE, "mem_norm_bwd")
    d_cat, d_mixed, gs["post_mix_g"] = _norm_bwd_mm(mixed, gains["post_mix_g"], d1, w_out, MXU_DTYPE, 512, 1024, "mix_post_bwd_out_dx")
    tok = on_grad("w_out", _mm(cat, d_mixed, "tn", WIRE_DTYPE, WEIGHT_ROW_TILE, 1024, "out_dw"))
    d_o, d_hg, dg_hg = _hg_post_bwd(o2, p, gains["hg_out_norm_g"], d_cat, "hg_post_bwd", after=(tok,))
    gs["hg_out_norm_g"] = dg_hg.reshape(HG_HEADS, HG_HEAD_DIM).sum(axis=0, keepdims=True)
    dhq2, dz2, dhv2, dlb = _hgrn_bwd(p, a0, a1, masks, gp, gn, d_o, s0, hg_a, hg_e, hg_kept, "hgrn_bwd")
    lb = jax.nn.sigmoid(a0 - a1)
    da0 = dlb * lb * (1.0 - lb)
    gs["hg_lb"] = jnp.concatenate([da0, -da0], axis=1)
    d_qr, d_kh, d_vh = _attn_bwd(qr, kh, vh, cat, d_cat, "attn_bwd")
    unheads = lambda a: a.transpose(2, 0, 1).reshape(n, ATT_KV_DIM)
    d_aq, d_ak, dgq, dgk = _qk_prep_bwd(p, gq2, gk2, cos, sin, d_qr, unheads(d_kh), "qk_prep_bwd")
    gs["q_norm_g"] = dgq.reshape(ATT_HEADS, ATT_HEAD_DIM).sum(axis=0, keepdims=True)
    gs["k_norm_g"] = dgk.reshape(ATT_KV_HEADS, ATT_HEAD_DIM).sum(axis=0, keepdims=True)
    d_p = jnp.concatenate([d_aq, d_ak, unheads(d_vh).astype(MXU_DTYPE), (dhq2[0] + dhq2[1]).astype(MXU_DTYPE),
                           dz2[0].astype(MXU_DTYPE), dz2[1].astype(MXU_DTYPE), (dhv2[0] + dhv2[1]).astype(MXU_DTYPE), d_hg], axis=1)
    tok = on_grad("w_in", _mm(h1, d_p, "tn", WIRE_DTYPE, WEIGHT_ROW_TILE, 1664, "in_dw"))
    grad_x, gs["pre_mix_g"] = _dx_norm_bwd([(d_p, 0)], w_in[None], x, gains["pre_mix_g"], d1, 512, "in_dx_pre_bwd", after=(tok,))
    return loss, grad_x, gs


MATS = ("w_in", "w_out", "w_xq", "w_xkv", "w_xo", "w_up", "w_down")
GAINS = ("pre_mix_g", "q_norm_g", "k_norm_g", "hg_out_norm_g", "post_mix_g", "pre_x_g", "mem_norm_g", "post_x_g", "pre_ffn_g", "post_ffn_g")
WEIGHTS = ('pre_mix_g', 'w_in', 'q_norm_g', 'k_norm_g', 'hg_lb', 'hg_out_norm_g', 'w_out', 'post_mix_g', 'pre_x_g', 'mem_norm_g', 'w_xq',
           'w_xkv', 'w_xo', 'post_x_g', 'pre_ffn_g', 'w_up', 'conv_w', 'conv_b', 'w_down', 'post_ffn_g')


def kernel(x, mem, pre_mix_g, w_in, q_norm_g, k_norm_g, hg_lb, hg_out_norm_g, w_out, post_mix_g, pre_x_g, mem_norm_g, w_xq, w_xkv, w_xo, post_x_g, pre_ffn_g, w_up, conv_w, conv_b, w_down, post_ffn_g, loss_target, m_pre_mix_g, m_w_in, m_q_norm_g, m_k_norm_g, m_hg_lb, m_hg_out_norm_g, m_w_out, m_post_mix_g, m_pre_x_g, m_mem_norm_g, m_w_xq, m_w_xkv, m_w_xo, m_post_x_g, m_pre_ffn_g, m_w_up, m_conv_w, m_conv_b, m_w_down, m_post_ffn_g, v_pre_mix_g, v_w_in, v_q_norm_g, v_k_norm_g, v_hg_lb, v_hg_out_norm_g, v_w_out, v_post_mix_g, v_pre_x_g, v_mem_norm_g, v_w_xq, v_w_xkv, v_w_xo, v_post_x_g, v_pre_ffn_g, v_w_up, v_conv_w, v_conv_b, v_w_down, v_post_ffn_g):
    args = dict(locals())
    w = {k: args[k] for k in WEIGHTS}
    m = {k: args["m_" + k] for k in WEIGHTS}
    v = {k: args["v_" + k] for k in WEIGHTS}
    chip = 2 * lax.axis_index("x") + lax.axis_index("y")
    core = lax.axis_index("c")

    shards = {k: w[k][0].astype(WIRE_DTYPE) for k in MATS}

    def whole(k, g):
        return g if k in ("w_xkv", "w_up") else g.reshape(-1, g.shape[-1])

    w_in_shards = _gather_shards([shards["w_in"]], "gather_w_in")[0]
    w_in_full = jnp.concatenate([w_in_shards[s] for s in range(4)], axis=1)
    small_in = _exchange_small(_pack_small([w["conv_w"][0], w["hg_lb"]]), False, "gather_small")
    mid_names, ffn_names = ("w_out", "w_xq", "w_xkv", "w_xo"), ("w_up", "w_down")
    mid = _gather_start([shards[k] for k in mid_names], "gather_mid_start", after=(w_in_full, small_in))
    ffn = _gather_start([shards[k] for k in ffn_names], "gather_ffn_start", after=(mid[4],))

    def mid_weights(after):
        return [whole(k, g) for k, g in zip(mid_names, _gather_wait(*mid[:4], after, "gather_mid_wait"))]

    def ffn_weights(k, after):
        t = ffn_names.index(k)
        return whole(k, _gather_wait(*[part[t:t + 1] for part in ffn[:4]], after, "gather_wait_" + k)[0])

    cw_parts, lb_parts = [], []
    for s in range(4):
        cw_s, lb_s = _unpack_small(small_in[2 * s], [w["conv_w"][0].shape, w["hg_lb"].shape])
        cw_parts.append(cw_s)
        lb_parts.append(lb_s)
    conv_w_full = jnp.concatenate(cw_parts, axis=1)
    hg_lb_full = jnp.concatenate(lb_parts, axis=2)

    started = {}

    def on_grad(k, g):
        if k == "w_in":
            g = g.reshape(g.shape[0], 4, g.shape[1] // 4).transpose(1, 0, 2)
        elif g.ndim == 2:
            g = g.reshape(4, g.shape[0] // 4, g.shape[1])
        *started[k], token = _scatter_start(g, "grad_start_" + k)
        return token

    gains = {k: w[k] for k in GAINS}
    loss_part, grad_x, gs = _local_step(x[0], mem[0], loss_target[0], w_in_full, ffn[4], mid_weights, ffn_weights, on_grad, gains,
                                        conv_w_full, w["conv_b"], hg_lb_full)
    gs["loss"] = loss_part

    grads, delta, new_m, new_v = {}, {}, {}, {}

    def reduce_matrices(names, after, tag):
        sent, landed = _scatter_wait([started[k] for k in names], after, "grad_wait_" + tag)
        halves = [_sum_devices(g, land, chip, core, "grad_sum_" + k) for k, g, land in zip(names, sent, landed)]
        for k, r in zip(names, _join_halves(halves, "grad_join_" + tag)):
            grads[k] = r.reshape(1, -1, r.shape[-1])

    def adamw(names):
        for k in names:
            shape = w[k].shape
            keep = len(shape) == 3 and shape[0] == 1
            two_d = lambda a: a.reshape(shape) if keep else a.reshape(-1, shape[-1])
            d, mo, vo, go = _adamw(two_d(w[k]), two_d(grads[k]), two_d(m[k]), two_d(v[k]), "adamw_" + k)
            delta[k], new_m[k], new_v[k], grads[k] = d.reshape(shape), mo.reshape(shape), vo.reshape(shape), go.reshape(shape)

    early = tuple(k for k in MATS if k != "w_in")
    reduce_matrices(early, (grad_x,), "early")
    adamw(early)

    small_names = GAINS + ("conv_b", "conv_w", "hg_lb")
    packed = _pack_small([gs[k] for k in small_names + ("loss",)])
    reduced_small = _exchange_small(packed, True, "reduce_small", after=tuple(new_v[k] for k in early))
    *summed, loss = _unpack_small(reduced_small, [gs[k].shape for k in small_names + ("loss",)])
    loss = loss[0, 0]
    for k, g in zip(small_names, summed):
        grads[k] = g
    ncw = w["conv_w"].shape[2]
    grads["conv_w"] = lax.dynamic_slice_in_dim(grads["conv_w"], chip * ncw, ncw, axis=1)[None]
    nlb = w["hg_lb"].shape[2]
    grads["hg_lb"] = lax.dynamic_slice_in_dim(grads["hg_lb"], chip * nlb, nlb, axis=2)
    replicated = GAINS + ("conv_b",)
    shapes = [w[k].shape for k in replicated]
    rows = sum(int(np.prod(s)) for s in shapes) // 128
    pack = lambda d: jnp.concatenate([d[k].reshape(-1) for k in replicated]).reshape(rows, 128)
    outs = _adamw(pack(w), reduced_small[:rows], pack(m), pack(v), "adamw_replicated")
    for into, packed_out in zip((delta, new_m, new_v, grads), outs):
        for k, a in zip(replicated, _unpack_small(packed_out, shapes)):
            into[k] = a
    adamw(("conv_w", "hg_lb"))

    reduce_matrices(("w_in",), tuple(new_v[k] for k in early + small_names), "late")
    adamw(("w_in",))
    return (loss, grad_x[None], *[grads[k] for k in WEIGHTS], *[delta[k] for k in WEIGHTS],
            *[new_m[k] for k in WEIGHTS], *[new_v[k] for k in WEIGHTS])
```

```python
import numpy as np
import jax
import jax.numpy as jnp
from jax import lax
from jax.experimental import pallas as pl
from jax.experimental.pallas import tpu as pltpu

F32 = jnp.float32
MXU_DTYPE = jnp.bfloat16
WIRE_DTYPE = jnp.bfloat16
VMEM_LIMIT_BYTES = 56 * 1024 * 1024
ROWS_PER_16BIT_TILE = 16
ELEMENTWISE_ROWS = 256
EPS = 1e-6
MESH = pl.DeviceIdType.MESH

GRID_W = 64
ATT_HEADS, ATT_KV_HEADS, ATT_HEAD_DIM = 8, 2, 64
ATT_GROUP = ATT_HEADS // ATT_KV_HEADS
ATT_Q_DIM, ATT_KV_DIM = 512, 128
ROPE_THETA = 10000.0
HG_HEADS, HG_HEAD_DIM, HG_DIM = 4, 128, 512
HG_CHUNK = 128
HG_LEVELS = 7
HG_PAIR = 2 * HG_HEAD_DIM
HG_KEPT = 7
X_HEADS, X_HEAD_DIM = 4, 256
D_FF = 2816
FF_COLS = 256
FF_BLOCKS = D_FF // FF_COLS
OFF_AK, OFF_AV, OFF_HQ, OFF_ZF, OFF_ZB, OFF_HI, OFF_HG = 512, 640, 768, 1280, 1792, 2304, 2816

ADAM_LR, ADAM_B1, ADAM_B2, ADAM_EPS, ADAM_WD, ADAM_STEP = 0.001, 0.9, 0.999, 1e-08, 0.01, 10

SDS = jax.ShapeDtypeStruct


def _cp(*sem):
    return pltpu.CompilerParams(dimension_semantics=sem, vmem_limit_bytes=VMEM_LIMIT_BYTES)


def _row_tile(rows, cap):
    if rows <= cap:
        return rows
    return max(t for t in range(ROWS_PER_16BIT_TILE, cap + 1, ROWS_PER_16BIT_TILE) if rows % t == 0)


def _dot(a, b, form="nn"):
    dims = {"nn": (((1,), (0,)), ((), ())), "nt": (((1,), (1,)), ((), ())), "tn": (((0,), (0,)), ((), ()))}[form]
    return lax.dot_general(a.astype(MXU_DTYPE), b.astype(MXU_DTYPE), dims, preferred_element_type=F32)


def _sigmoid(x):
    return 1.0 / (1.0 + jnp.exp(-x))


def _rstd(x):
    return lax.rsqrt(jnp.mean(x * x, axis=-1, keepdims=True) + EPS)


def _rms_bwd(x, g, dy):
    r = _rstd(x)
    xh = x * r
    dn = dy * g
    dx = r * (dn - xh * jnp.mean(dn * xh, axis=-1, keepdims=True))
    return dx, jnp.sum(dy * xh, axis=0, keepdims=True)


def _unread(after):
    after = tuple(a for a in after if a is not None)
    return after, [pl.BlockSpec(memory_space=pl.ANY)] * len(after)


def _mm(a, b, form, out_dtype, tm, tn, name, after=()):
    after, after_specs = _unread(after)
    if form == "nn":
        (m, k), n = a.shape, b.shape[1]
    elif form == "nt":
        (m, k), n = a.shape, b.shape[0]
    else:
        (k, m), n = a.shape, b.shape[1]
    tm, tn = min(tm, m), min(tn, n)
    assert m % tm == 0 and n % tn == 0, (name, m, n, tm, tn)

    def body(a_ref, b_ref, *rest):
        o_ref = rest[-1]
        o_ref[...] = _dot(a_ref[...], b_ref[...], form).astype(o_ref.dtype)

    a_spec = pl.BlockSpec((k, tm), lambda i, j: (0, i)) if form == "tn" else pl.BlockSpec((tm, k), lambda i, j: (i, 0))
    b_spec = pl.BlockSpec((tn, k), lambda i, j: (j, 0)) if form == "nt" else pl.BlockSpec((k, tn), lambda i, j: (0, j))
    return pl.pallas_call(
        body, name=name, grid=(m // tm, n // tn), in_specs=[a_spec, b_spec] + after_specs,
        out_specs=pl.BlockSpec((tm, tn), lambda i, j: (i, j)), out_shape=SDS((m, n), out_dtype),
        compiler_params=_cp("parallel", "parallel"))(a, b, *after)


def _mm_nt_parts(a_parts, b, out_dtype, tm, tn, name, after=()):
    after, after_specs = _unread(after)
    parts, n, p = b.shape
    m = a_parts[0][0].shape[0]
    tm, tn = min(tm, m), min(tn, n)
    assert m % tm == 0 and n % tn == 0 and len(a_parts) == parts, (name, m, b.shape)

    def body(*refs):
        o_ref = refs[-1]
        acc = _dot(refs[0][...], refs[parts][0], "nt")
        for s in range(1, parts):
            acc = acc + _dot(refs[s][...], refs[parts + s][0], "nt")
        o_ref[...] = acc.astype(o_ref.dtype)

    a_specs = [pl.BlockSpec((tm, p), lambda i, j, cb=cb: (i, cb)) for _, cb in a_parts]
    b_specs = [pl.BlockSpec((1, tn, p), lambda i, j, s=s: (s, j, 0)) for s in range(parts)]
    return pl.pallas_call(
        body, name=name, grid=(m // tm, n // tn), in_specs=a_specs + b_specs + after_specs,
        out_specs=pl.BlockSpec((tm, tn), lambda i, j: (i, j)), out_shape=SDS((m, n), out_dtype),
        compiler_params=_cp("parallel", "parallel"))(*[arr for arr, _ in a_parts], *([b] * parts), *after)


def _norm_bwd_mm(y, g, d, w, out_dtype, tm, tn, name):
    n, dm = y.shape
    nn = w.shape[0]
    tm, tn = min(tm, n), min(tn, nn)
    assert n % tm == 0 and nn % tn == 0 and w.shape[1] == dm, (name, y.shape, w.shape)

    def body(y_ref, g_ref, d_ref, w_ref, dx_ref, dy_ref, dg_ref, dys):
        i, j = pl.program_id(0), pl.program_id(1)

        @pl.when(jnp.logical_and(i == 0, j == 0))
        def _():
            dg_ref[...] = jnp.zeros_like(dg_ref)

        @pl.when(j == 0)
        def _():
            dy, dg = _rms_bwd(y_ref[...], g_ref[...], d_ref[...])
            dy = dy.astype(MXU_DTYPE)
            dys[...] = dy
            dy_ref[...] = dy
            dg_ref[...] += dg

        dx_ref[...] = _dot(dys[...], w_ref[...], "nt").astype(dx_ref.dtype)

    row = pl.BlockSpec((tm, dm), lambda i, j: (i, 0))
    vec = pl.BlockSpec((1, dm), lambda i, j: (0, 0))
    return pl.pallas_call(
        body, name=name, grid=(n // tm, nn // tn), in_specs=[row, vec, row, pl.BlockSpec((tn, dm), lambda i, j: (j, 0))],
        out_specs=[pl.BlockSpec((tm, tn), lambda i, j: (i, j)), row, vec],
        out_shape=[SDS((n, nn), out_dtype), SDS((n, dm), MXU_DTYPE), SDS((1, dm), F32)],
        scratch_shapes=[pltpu.VMEM((tm, dm), MXU_DTYPE)],
        compiler_params=_cp("arbitrary", "arbitrary"))(y, g, d, w)


def _mm_resid_norm(a, b, x, g, tm, name, target=None):
    n, k = a.shape
    d = b.shape[1]
    tm = min(tm, n)
    assert n % tm == 0 and x.shape == (n, d), (name, a.shape, b.shape)
    with_loss = target is not None

    def body(a_ref, b_ref, x_ref, g_ref, *rest):
        y = _dot(a_ref[...], b_ref[...])
        out = x_ref[...] + y * _rstd(y) * g_ref[...]
        if not with_loss:
            y_ref, o_ref = rest
            y_ref[...] = y
            o_ref[...] = out
            return
        t_ref, y_ref, d_ref, l_ref = rest
        y_ref[...] = y
        diff = out - t_ref[...]
        d_ref[...] = diff * (1.0 / d)

        @pl.when(pl.program_id(0) == 0)
        def _():
            l_ref[...] = jnp.zeros_like(l_ref)

        l_ref[...] += 0.5 * jnp.sum(jnp.mean(diff * diff, axis=-1, keepdims=True), axis=0, keepdims=True)

    row = pl.BlockSpec((tm, d), lambda i: (i, 0))
    ins = [pl.BlockSpec((tm, k), lambda i: (i, 0)), pl.BlockSpec((k, d), lambda i: (0, 0)), row, pl.BlockSpec((1, d), lambda i: (0, 0))]
    out = SDS((n, d), F32)
    if with_loss:
        return pl.pallas_call(body, name=name, grid=(n // tm,), in_specs=ins + [row], out_specs=[row, row, pl.BlockSpec((1, 1), lambda i: (0, 0))],
                              out_shape=[out, out, SDS((1, 1), F32)], compiler_params=_cp("arbitrary"))(a, b, x, g, target)
    return pl.pallas_call(body, name=name, grid=(n // tm,), in_specs=ins, out_specs=[row, row], out_shape=[out, out],
                          compiler_params=_cp("parallel"))(a, b, x, g)


def _dx_norm_bwd(a_parts, b, x, g, res, tm, name, after=()):
    after, after_specs = _unread(after)
    parts, d, p = b.shape
    n = x.shape[0]
    tm = min(tm, n)
    assert n % tm == 0 and len(a_parts) == parts and x.shape[1] == d, (name, x.shape, b.shape)

    def body(*refs):
        x_ref, g_ref, res_ref = refs[2 * parts:2 * parts + 3]
        dx_ref, dg_ref = refs[-2:]
        dh = _dot(refs[0][...], refs[parts][0], "nt")
        for s in range(1, parts):
            dh = dh + _dot(refs[s][...], refs[parts + s][0], "nt")
        dx, dg = _rms_bwd(x_ref[...], g_ref[...], dh)
        dx_ref[...] = dx + res_ref[...]

        @pl.when(pl.program_id(0) == 0)
        def _():
            dg_ref[...] = jnp.zeros_like(dg_ref)

        dg_ref[...] += dg

    a_specs = [pl.BlockSpec((tm, p), lambda i, cb=cb: (i, cb)) for _, cb in a_parts]
    b_specs = [pl.BlockSpec((1, d, p), lambda i, s=s: (s, 0, 0)) for s in range(parts)]
    row = pl.BlockSpec((tm, d), lambda i: (i, 0))
    vec = pl.BlockSpec((1, d), lambda i: (0, 0))
    return pl.pallas_call(
        body, name=name, grid=(n // tm,), in_specs=a_specs + b_specs + [row, vec, row] + after_specs,
        out_specs=[row, vec], out_shape=[SDS((n, d), F32), SDS((1, d), F32)],
        compiler_params=_cp("arbitrary"))(*[arr for arr, _ in a_parts], *([b] * parts), x, g, res, *after)


def _dw_by_owner(a, b, tn, first, into, tm, name):
    k, m = a.shape
    cnt = b.shape[1] // tn
    tm = min(tm, m)
    assert m % tm == 0 and b.shape[1] == cnt * tn and first + cnt <= 4, (name, a.shape, b.shape)

    def body(a_ref, b_ref, *rest):
        rest[-1][0] = _dot(a_ref[...], b_ref[...], "tn").astype(rest[-1].dtype)

    extra = [] if into is None else [into]
    return pl.pallas_call(
        body, name=name, grid=(m // tm, cnt),
        in_specs=[pl.BlockSpec((k, tm), lambda i, j: (0, i)), pl.BlockSpec((k, tn), lambda i, j: (0, j))] + [pl.BlockSpec(memory_space=pl.ANY)] * len(extra),
        out_specs=pl.BlockSpec((1, tm, tn), lambda i, j: (first + j, i, 0)), out_shape=SDS((4, m, tn), WIRE_DTYPE),
        input_output_aliases={2: 0} if extra else {},
        compiler_params=_cp("parallel", "parallel"))(a, b, *extra)


def _norm_mm(x, g, w, out_dtype, tm, tn, name, after=()):
    after, after_specs = _unread(after)
    m, d = x.shape
    sharded = w.ndim == 3
    n = w.shape[-1] * (w.shape[0] if sharded else 1)
    tm, tn = min(tm, m), (w.shape[-1] if sharded else min(tn, n))
    assert m % tm == 0 and n % tn == 0, (name, m, n, tm, tn)

    def body(x_ref, g_ref, w_ref, *rest):
        o_ref, h_ref, hs = rest[-3:]

        @pl.when(pl.program_id(1) == 0)
        def _():
            xv = x_ref[...]
            h = (xv * _rstd(xv) * g_ref[...]).astype(MXU_DTYPE)
            hs[...] = h
            h_ref[...] = h

        o_ref[...] = _dot(hs[...], w_ref[0] if sharded else w_ref[...]).astype(o_ref.dtype)

    w_spec = pl.BlockSpec((1, d, tn), lambda i, j: (j, 0, 0)) if sharded else pl.BlockSpec((d, tn), lambda i, j: (0, j))
    return pl.pallas_call(
        body, name=name, grid=(m // tm, n // tn),
        in_specs=[pl.BlockSpec((tm, d), lambda i, j: (i, 0)), pl.BlockSpec((1, d), lambda i, j: (0, 0)), w_spec] + after_specs,
        out_specs=[pl.BlockSpec((tm, tn), lambda i, j: (i, j)), pl.BlockSpec((tm, d), lambda i, j: (i, 0))],
        out_shape=[SDS((m, n), out_dtype), SDS((m, d), MXU_DTYPE)],
        scratch_shapes=[pltpu.VMEM((tm, d), MXU_DTYPE)],
        compiler_params=_cp("parallel", "arbitrary"))(x, g, w, *after)


ROW_TILE = 512
TOKEN_TILE = 1024


def _norm_bwd(x, g, dy, res, out_dtype, name):
    n, d = x.shape
    tr = min(ROW_TILE, n)
    has_res = res is not None

    def body(*refs):
        x_ref, g_ref, dy_ref = refs[:3]
        dx_ref, dg_ref = refs[-2:]
        dx, dg = _rms_bwd(x_ref[...], g_ref[...], dy_ref[...].astype(F32))
        if has_res:
            dx = dx + refs[3][...]
        dx_ref[...] = dx.astype(dx_ref.dtype)

        @pl.when(pl.program_id(0) == 0)
        def _():
            dg_ref[...] = jnp.zeros_like(dg_ref)

        dg_ref[...] += dg

    row = pl.BlockSpec((tr, d), lambda i: (i, 0))
    vec = pl.BlockSpec((1, d), lambda i: (0, 0))
    ins = [x, g, dy] + ([res] if has_res else [])
    return pl.pallas_call(
        body, name=name, grid=(n // tr,), in_specs=[row, vec, row] + ([row] if has_res else []),
        out_specs=[row, vec], out_shape=[SDS((n, d), out_dtype), SDS((1, d), F32)],
        compiler_params=_cp("arbitrary"))(*ins)


def _rope_tables(n):
    pairs = ATT_HEAD_DIM // 4
    t = np.arange(n)
    inv = np.power(ROPE_THETA, -np.arange(pairs, dtype=np.float32) / pairs).astype(np.float32)
    ang = np.concatenate([(t // GRID_W)[:, None].astype(np.float32) * inv, (t % GRID_W)[:, None].astype(np.float32) * inv], axis=-1)
    cos = np.repeat(np.cos(ang), 2, axis=-1)
    sin = np.repeat(np.sin(ang), 2, axis=-1) * np.tile(np.array([-1.0, 1.0], np.float32), ATT_HEAD_DIM // 2)
    return jnp.asarray(np.tile(cos, 2), F32), jnp.asarray(np.tile(sin, 2), F32)


def _swap_pairs(x):
    lane = lax.broadcasted_iota(jnp.int32, x.shape, 1)
    return jnp.where((lane & 1) == 0, pltpu.roll(x, 127, axis=1), pltpu.roll(x, 1, axis=1))


def _head_mean(v):
    lane = lax.broadcasted_iota(jnp.int32, v.shape, 1)
    lo = jnp.where(lane < ATT_HEAD_DIM, v, 0.0)
    s0 = jnp.sum(lo, axis=-1, keepdims=True)
    s1 = jnp.sum(v - lo, axis=-1, keepdims=True)
    return jnp.where(lane < ATT_HEAD_DIM, s0, s1) * (1.0 / ATT_HEAD_DIM)


def _qk_prep(p, gq, gk, cos, sin, name):
    n = p.shape[0]
    tr = min(ROW_TILE, n)

    def one(xv, g, c, s):
        xn = xv * lax.rsqrt(_head_mean(xv * xv) + EPS) * g
        return xn * c + _swap_pairs(xn) * s

    def body(q_ref, k_ref, gq_ref, gk_ref, c_ref, s_ref, qo_ref, ko_ref):
        c, s = c_ref[...], s_ref[...]
        for j in range(ATT_Q_DIM // 128):
            qo_ref[:, j * 128:(j + 1) * 128] = one(q_ref[:, j * 128:(j + 1) * 128], gq_ref[...], c, s).astype(qo_ref.dtype)
        ko_ref[...] = one(k_ref[...], gk_ref[...], c, s).astype(ko_ref.dtype)

    vec = pl.BlockSpec((1, 128), lambda i: (0, 0))
    tab = pl.BlockSpec((tr, 128), lambda i: (i, 0))
    return pl.pallas_call(
        body, name=name, grid=(n // tr,),
        in_specs=[pl.BlockSpec((tr, ATT_Q_DIM), lambda i: (i, 0)), pl.BlockSpec((tr, 128), lambda i: (i, OFF_AK // 128)), vec, vec, tab, tab],
        out_specs=[pl.BlockSpec((tr, ATT_Q_DIM), lambda i: (i, 0)), tab],
        out_shape=[SDS((n, ATT_Q_DIM), MXU_DTYPE), SDS((n, ATT_KV_DIM), MXU_DTYPE)],
        compiler_params=_cp("parallel"))(p, p, gq, gk, cos, sin)


def _qk_prep_bwd(p, gq, gk, cos, sin, dq, dk, name):
    n = p.shape[0]
    tr = min(ROW_TILE, n)

    def one(xv, g, c, s, dout):
        dxn = dout * c + _swap_pairs(dout * s)
        r = lax.rsqrt(_head_mean(xv * xv) + EPS)
        xh = xv * r
        dn = dxn * g
        dx = r * (dn - xh * _head_mean(dn * xh))
        return dx, jnp.sum(dxn * xh, axis=0, keepdims=True)

    def body(q_ref, k_ref, gq_ref, gk_ref, c_ref, s_ref, dq_ref, dk_ref, dqo_ref, dko_ref, dgq_ref, dgk_ref):
        @pl.when(pl.program_id(0) == 0)
        def _():
            dgq_ref[...] = jnp.zeros_like(dgq_ref)
            dgk_ref[...] = jnp.zeros_like(dgk_ref)

        c, s = c_ref[...], s_ref[...]
        for j in range(ATT_Q_DIM // 128):
            sl = slice(j * 128, (j + 1) * 128)
            dx, dg = one(q_ref[:, sl], gq_ref[...], c, s, dq_ref[:, sl])
            dqo_ref[:, sl] = dx.astype(dqo_ref.dtype)
            dgq_ref[:, sl] += dg
        dx, dg = one(k_ref[...], gk_ref[...], c, s, dk_ref[...])
        dko_ref[...] = dx.astype(dko_ref.dtype)
        dgk_ref[...] += dg

    vec = pl.BlockSpec((1, 128), lambda i: (0, 0))
    tab = pl.BlockSpec((tr, 128), lambda i: (i, 0))
    qrow = pl.BlockSpec((tr, ATT_Q_DIM), lambda i: (i, 0))
    return pl.pallas_call(
        body, name=name, grid=(n // tr,),
        in_specs=[qrow, pl.BlockSpec((tr, 128), lambda i: (i, OFF_AK // 128)), vec, vec, tab, tab, qrow, tab],
        out_specs=[qrow, tab, pl.BlockSpec((1, ATT_Q_DIM), lambda i: (0, 0)), vec],
        out_shape=[SDS((n, ATT_Q_DIM), MXU_DTYPE), SDS((n, ATT_KV_DIM), MXU_DTYPE), SDS((1, ATT_Q_DIM), F32), SDS((1, 128), F32)],
        compiler_params=_cp("arbitrary"))(p, p, gq, gk, cos, sin, dq, dk)


ATT_FWD_STEP = (256, 4)
ATT_BWD_STEP = (512, 2)


def _attn_fwd(q, k, v, name):
    n = q.shape[0]
    tq, step_heads = min(ATT_FWD_STEP[0], n), ATT_FWD_STEP[1]
    scale = ATT_HEAD_DIM ** -0.5
    gw = step_heads * ATT_HEAD_DIM
    parts = ATT_GROUP // step_heads

    def body(q_ref, k_ref, v_ref, o_ref):
        kk, vv = k_ref[0], v_ref[0]
        v_ones = jnp.concatenate([vv, jnp.ones_like(vv)], axis=1)
        outs = []
        for g in range(step_heads):
            s = _dot(q_ref[:, g * ATT_HEAD_DIM:(g + 1) * ATT_HEAD_DIM] * scale, kk, "nt")
            e = jnp.exp(s - jnp.max(s, axis=-1, keepdims=True))
            ov = _dot(e, v_ones)
            outs.append(ov[:, :ATT_HEAD_DIM] / ov[:, ATT_HEAD_DIM:])
        o_ref[...] = jnp.concatenate(outs, axis=-1).astype(o_ref.dtype)

    kv = pl.BlockSpec((1, n, ATT_HEAD_DIM), lambda h, i, pr: (h, 0, 0))
    qb = pl.BlockSpec((tq, gw), lambda h, i, pr: (i, h * parts + pr))
    return pl.pallas_call(
        body, name=name, grid=(ATT_KV_HEADS, n // tq, parts), in_specs=[qb, kv, kv],
        out_specs=qb, out_shape=SDS((n, ATT_Q_DIM), MXU_DTYPE),
        compiler_params=_cp("parallel", "parallel", "parallel"))(q, k, v)


def _attn_bwd(q, k, v, o, do, name):
    n = q.shape[0]
    tq, step_heads = min(ATT_BWD_STEP[0], n), ATT_BWD_STEP[1]
    scale = ATT_HEAD_DIM ** -0.5
    gw = step_heads * ATT_HEAD_DIM
    parts = ATT_GROUP // step_heads

    def body(q_ref, k_ref, v_ref, o_ref, do_ref, dq_ref, dk_ref, dv_ref):
        @pl.when(jnp.logical_and(pl.program_id(1) == 0, pl.program_id(2) == 0))
        def _():
            dk_ref[...] = jnp.zeros_like(dk_ref)
            dv_ref[...] = jnp.zeros_like(dv_ref)

        kk, vv = k_ref[0], v_ref[0]
        dqs = []
        dk_acc = jnp.zeros((ATT_HEAD_DIM, n), F32)
        dv_acc = jnp.zeros((ATT_HEAD_DIM, n), F32)
        for g in range(step_heads):
            sl = slice(g * ATT_HEAD_DIM, (g + 1) * ATT_HEAD_DIM)
            qg, dog = q_ref[:, sl] * scale, do_ref[:, sl].astype(F32)
            s = _dot(qg, kk, "nt")
            e = jnp.exp(s - jnp.max(s, axis=-1, keepdims=True))
            inv = 1.0 / jnp.sum(e, axis=-1, keepdims=True)
            delta = jnp.sum(dog * o_ref[:, sl].astype(F32), axis=-1, keepdims=True)
            dse = e * (_dot(dog, vv, "nt") - delta)
            dqs.append(_dot(dse, kk) * (inv * scale))
            dk_acc += _dot(qg.astype(F32) * inv, dse, "tn")
            dv_acc += _dot(dog * inv, e, "tn")
        dq_ref[...] = jnp.concatenate(dqs, axis=-1)
        dk_ref[0] += dk_acc
        dv_ref[0] += dv_acc

    kv = pl.BlockSpec((1, n, ATT_HEAD_DIM), lambda h, i, pr: (h, 0, 0))
    kvt = pl.BlockSpec((1, ATT_HEAD_DIM, n), lambda h, i, pr: (h, 0, 0))
    qb = pl.BlockSpec((tq, gw), lambda h, i, pr: (i, h * parts + pr))
    return pl.pallas_call(
        body, name=name, grid=(ATT_KV_HEADS, n // tq, parts), in_specs=[qb, kv, kv, qb, qb], out_specs=[qb, kvt, kvt],
        out_shape=[SDS((n, ATT_Q_DIM), F32), SDS((ATT_KV_HEADS, ATT_HEAD_DIM, n), F32), SDS((ATT_KV_HEADS, ATT_HEAD_DIM, n), F32)],
        compiler_params=_cp("parallel", "arbitrary", "arbitrary"))(q, k, v, o, do)


def _both_directions(mats, axis):
    fwd = np.concatenate(mats, axis=axis).astype(np.float32)
    bwd = np.concatenate([m[::-1, ::-1] for m in mats], axis=axis).astype(np.float32)
    return jnp.asarray(np.stack([fwd, bwd]), MXU_DTYPE)


def _hg_segments():
    c = HG_CHUNK
    t = np.arange(c)[:, None]
    r = np.arange(c)[None, :]
    mats = [(r <= t)]
    for lev in range(HG_LEVELS):
        h = c >> (lev + 1)
        mid = (t // (2 * h)) * (2 * h) + h - 1
        hi = (t // h) % 2 == 1
        mats.append(np.where(hi, (r > mid) & (r <= t), (r > t) & (r <= mid)))
    mats.append(r > t)
    return _both_directions(mats, 0)


def _hg_pair_sums():
    c = HG_CHUNK
    r = np.arange(c)[:, None]
    t = np.arange(c)[None, :]
    gp, gn = [t >= r], [t < r]
    for lev in range(HG_LEVELS):
        sh = HG_LEVELS - 1 - lev
        same = (r >> sh) == (t >> sh)
        gp.append(same & (t >= r))
        gn.append(same & (t < r))
    return _both_directions(gp, 1), _both_directions(gn, 1)


def _split_dot(mat, x):
    hi = x.astype(MXU_DTYPE)
    lo = (x - hi.astype(F32)).astype(MXU_DTYPE)
    return _dot(mat, hi) + _dot(mat, lo)


def _hg_gates(hq, z, a0, a1):
    q = hq * _sigmoid(hq)
    sg = _sigmoid(z)
    lb = _sigmoid(a0 - a1)
    f = lb + (1.0 - lb) * sg
    k = (1.0 - lb) * (1.0 - sg)
    return q, f, k, sg, lb


def _hg_level_masks():
    c = HG_CHUNK
    t = np.arange(c)
    later, same = [], []
    for lev in range(HG_LEVELS):
        sh = HG_LEVELS - 1 - lev
        later.append(np.broadcast_to((((t >> sh) & 1) == 1)[:, None], (c, HG_HEAD_DIM)))
        same.append((t[:, None] >> (sh + 1)) == (t[None, :] >> (sh + 1)))
    same.append(t[:, None] == t[None, :])
    later = np.stack(later).astype(np.float32)
    return jnp.asarray(np.stack([later, 1.0 - later]), F32), jnp.asarray(np.stack(same).astype(np.float32), F32)


def _hg_level(q, k, ex, later_ref, lev):
    e = ex[lev + 1]
    e_q = e * later_ref[0, lev]
    e_k = e - e_q
    return q * e_q, k * e_k, e_q, e_k


def _hg_intra(q, k, ex, later_ref, same_ref):
    a = same_ref[HG_LEVELS] * jnp.sum(q * k, axis=-1, keepdims=True)
    for lev in range(HG_LEVELS):
        qs, ks, _, _ = _hg_level(q, k, ex, later_ref, lev)
        a = a + same_ref[lev] * _dot(qs, ks, "nt")
    return a


def _hg_specs(n, with_time):
    c = HG_CHUNK
    nc = n // c

    def chunk(d, i):
        first = d if with_time else 1 - d
        return i + first * (nc - 1 - 2 * i)

    def pcols(off, dir_stride=0):
        return [pl.BlockSpec((c, HG_PAIR), lambda d, i, j=j: (chunk(d, i), off // HG_PAIR + dir_stride // HG_PAIR * d + j)) for j in range(2)]

    specs = dict(
        hq=pcols(OFF_HQ), v=pcols(OFF_HI), z=pcols(OFF_ZF, OFF_ZB - OFF_ZF),
        shared=pl.BlockSpec((c, HG_DIM), lambda d, i: (chunk(d, i), 0)),
        per_dir=pl.BlockSpec((1, c, HG_DIM), lambda d, i: (d, chunk(d, i), 0)),
        vec=pl.BlockSpec((1, 1, HG_DIM), lambda d, i: (d, 0, 0)),
        seg=pl.BlockSpec((1, (HG_LEVELS + 2) * c, c), lambda d, i: (d, 0, 0)),
        sums=pl.BlockSpec((1, c, (HG_LEVELS + 1) * c), lambda d, i: (d, 0, 0)),
        later=pl.BlockSpec((1, HG_LEVELS, c, HG_HEAD_DIM), lambda d, i: (d, 0, 0, 0)),
        same=pl.BlockSpec((HG_LEVELS + 1, c, c), lambda d, i: (0, 0, 0)),
        state=pl.BlockSpec((1, HG_HEADS, 1, HG_HEAD_DIM, HG_HEAD_DIM), lambda d, i: (d, 0, chunk(d, i), 0, 0)),
        weights=pl.BlockSpec((1, HG_HEADS, 1, c, c), lambda d, i: (d, 0, chunk(d, i), 0, 0)),
        levels=pl.BlockSpec((1, HG_HEADS, 1, HG_LEVELS, c, HG_HEAD_DIM), lambda d, i: (d, 0, chunk(d, i), 0, 0, 0)),
        kept=pl.BlockSpec((1, HG_KEPT, c, HG_DIM), lambda d, i: (d, 0, chunk(d, i), 0)))
    return nc, specs


def _hg_head(refs, hh):
    off = (hh % 2) * HG_HEAD_DIM
    return refs[hh // 2][:, off:off + HG_HEAD_DIM]


def _hg_lanes(hh):
    return slice(hh * HG_HEAD_DIM, (hh + 1) * HG_HEAD_DIM)


def _hg_exps(seg_ref, f):
    c = HG_CHUNK
    args = _split_dot(seg_ref[0], jnp.log(f))
    return [jnp.exp(args[j * c:(j + 1) * c]) for j in range(HG_LEVELS + 2)]


def _hg_last_row(a, mirrored):
    return jnp.where(mirrored, a[0:1, :], a[HG_CHUNK - 1:HG_CHUNK, :])


def _hgrn_fwd(p, a0, a1, seg, masks, name):
    n = p.shape[0]
    nc, sp = _hg_specs(n, True)

    def body(hq0, hq1, z0, z1, v0, v1, a0_ref, a1_ref, seg_ref, later_ref, same_ref, o_ref, s0_ref, a_ref, e_ref, g_ref, st):
        @pl.when(pl.program_id(1) == 0)
        def _():
            st[...] = jnp.zeros_like(st)

        mirrored = pl.program_id(0) == 1
        for hh in range(HG_HEADS):
            ln = _hg_lanes(hh)
            hqv = _hg_head((hq0, hq1), hh)
            q, f, k, sg, _ = _hg_gates(hqv, _hg_head((z0, z1), hh), a0_ref[0, :, ln], a1_ref[0, :, ln])
            vv = _hg_head((v0, v1), hh)
            ex = _hg_exps(seg_ref, f)
            for lev in range(HG_LEVELS):
                e_ref[0, hh, 0, lev] = ex[lev + 1].astype(e_ref.dtype)
            sq = _sigmoid(hqv)
            for j, kept in enumerate((q, k, f, sg, sq * (1.0 + hqv * (1.0 - sq)), ex[0], ex[HG_LEVELS + 1])):
                g_ref[0, j, :, ln] = kept
            a = _hg_intra(q, k, ex, later_ref, same_ref).astype(MXU_DTYPE)
            a_ref[0, hh, 0] = a
            s_t = st[hh]
            s0_ref[0, hh, 0] = s_t
            o_ref[0, :, ln] = _dot(a, vv) + _dot(q * ex[0], s_t, "nt")
            st[hh] = s_t * _hg_last_row(ex[0], mirrored) + _dot(vv, k * ex[HG_LEVELS + 1], "tn")

    return pl.pallas_call(
        body, name=name, grid=(2, nc), in_specs=sp["hq"] + sp["z"] + sp["v"] + [sp["vec"], sp["vec"], sp["seg"], sp["later"], sp["same"]],
        out_specs=[sp["per_dir"], sp["state"], sp["weights"], sp["levels"], sp["kept"]],
        out_shape=[SDS((2, n, HG_DIM), F32), SDS((2, HG_HEADS, nc, HG_HEAD_DIM, HG_HEAD_DIM), F32),
                   SDS((2, HG_HEADS, nc, HG_CHUNK, HG_CHUNK), MXU_DTYPE),
                   SDS((2, HG_HEADS, nc, HG_LEVELS, HG_CHUNK, HG_HEAD_DIM), MXU_DTYPE), SDS((2, HG_KEPT, n, HG_DIM), F32)],
        scratch_shapes=[pltpu.VMEM((HG_HEADS, HG_HEAD_DIM, HG_HEAD_DIM), F32)],
        compiler_params=_cp("parallel", "arbitrary"))(p, p, p, p, p, p, a0, a1, seg, *masks)


def _hgrn_bwd(p, a0, a1, masks, gp, gn, do, s0, a, e, kept, name):
    n = p.shape[0]
    nc, sp = _hg_specs(n, False)


    def body(v0, v1, a0_ref, a1_ref, later_ref, same_ref, gp_ref, gn_ref, do_ref, s0_ref, a_ref, e_ref, g_ref,
             dhq_ref, dz_ref, dv_ref, dlb_ref, rt):
        @pl.when(pl.program_id(1) == 0)
        def _():
            rt[...] = jnp.zeros_like(rt)
            dlb_ref[...] = jnp.zeros_like(dlb_ref)

        mirrored = pl.program_id(0) == 1
        for hh in range(HG_HEADS):
            ln = _hg_lanes(hh)
            q, k, f, sg, dsilu, e_first, e_last = (g_ref[0, j, :, ln] for j in range(HG_KEPT))
            lb = _sigmoid(a0_ref[0, :, ln] - a1_ref[0, :, ln])
            vv, dov = _hg_head((v0, v1), hh), do_ref[:, ln]
            ex = [e_first] + [e_ref[0, hh, 0, lev].astype(F32) for lev in range(HG_LEVELS)] + [e_last]
            a = a_ref[0, hh, 0]
            da = _dot(dov, vv, "nt")
            diag = jnp.sum(dov * vv, axis=-1, keepdims=True)
            s_t = s0_ref[0, hh, 0]
            r_t = rt[hh]
            k_end = k * ex[HG_LEVELS + 1]
            dv_ref[0, :, ln] = _dot(a, dov, "tn") + _dot(k_end, r_t, "nt")
            dq_inter = ex[0] * _dot(dov, s_t)
            dk_inter = ex[HG_LEVELS + 1] * _dot(vv, r_t)
            dq = diag * k + dq_inter
            dk = diag * q + dk_inter
            q_terms, k_terms = [q * dq_inter], [k * dk_inter]
            for lev in range(HG_LEVELS):
                qs, ks, e_q, e_k = _hg_level(q, k, ex, later_ref, lev)
                pairs = da * same_ref[lev]
                q_part = e_q * _dot(pairs, ks)
                k_part = e_k * _dot(pairs, qs, "tn")
                dq, dk = dq + q_part, dk + k_part
                q_terms.append(q * q_part)
                k_terms.append(k * k_part)
            decay = _hg_last_row(ex[0], mirrored)
            rt[hh] = r_t * decay + _dot(dov, q * ex[0], "tn")
            later = decay * jnp.sum(s_t * r_t, axis=0, keepdims=True)
            dlf = _dot(gp_ref[0], jnp.concatenate(q_terms, axis=0)) + _dot(gn_ref[0], jnp.concatenate(k_terms, axis=0)) + later
            df = dlf / f - dk
            dz_ref[0, :, ln] = df * (1.0 - lb) * sg * (1.0 - sg)
            dlb_ref[0, :, ln] += jnp.sum(df * (1.0 - sg), axis=0, keepdims=True)
            dhq_ref[0, :, ln] = dq * dsilu

    out = SDS((2, n, HG_DIM), F32)
    return pl.pallas_call(
        body, name=name, grid=(2, nc),
        in_specs=sp["v"] + [sp["vec"], sp["vec"], sp["later"], sp["same"], sp["sums"], sp["sums"],
                            sp["shared"], sp["state"], sp["weights"], sp["levels"], sp["kept"]],
        out_specs=[sp["per_dir"], sp["per_dir"], sp["per_dir"], sp["vec"]], out_shape=[out, out, out, SDS((2, 1, HG_DIM), F32)],
        scratch_shapes=[pltpu.VMEM((HG_HEADS, HG_HEAD_DIM, HG_HEAD_DIM), F32)],
        compiler_params=_cp("parallel", "arbitrary"))(p, p, a0, a1, *masks, gp, gn, do, s0, a, e, kept)


def _hg_post(o2, p, g, name):
    n = p.shape[0]
    tr = min(ROW_TILE, n)
    w = 2 * HG_HEAD_DIM

    def body(of_ref, ob_ref, hg_ref, g_ref, o_ref):
        for j in range(2):
            sl = slice(j * HG_HEAD_DIM, (j + 1) * HG_HEAD_DIM)
            o = of_ref[0, :, sl] + ob_ref[0, :, sl]
            hg = hg_ref[:, sl]
            o_ref[:, sl] = (o * _rstd(o) * g_ref[...] * (hg * _sigmoid(hg))).astype(o_ref.dtype)

    blk = pl.BlockSpec((tr, w), lambda i, j: (i, j))
    dirs = [pl.BlockSpec((1, tr, w), lambda i, j, d=d: (d, i, j)) for d in range(2)]
    return pl.pallas_call(
        body, name=name, grid=(n // tr, HG_DIM // w),
        in_specs=dirs + [pl.BlockSpec((tr, w), lambda i, j: (i, OFF_HG // w + j)), pl.BlockSpec((1, HG_HEAD_DIM), lambda i, j: (0, 0))],
        out_specs=blk, out_shape=SDS((n, HG_DIM), MXU_DTYPE), compiler_params=_cp("parallel", "parallel"))(o2, o2, p, g)


def _hg_post_bwd(o2, p, g, dcat, name, after=()):
    n = p.shape[0]
    tr = min(ROW_TILE, n)
    w = 2 * HG_HEAD_DIM
    after, after_specs = _unread(after)

    def body(of_ref, ob_ref, hg_ref, g_ref, d_ref, *rest):
        do_ref, dhg_ref, dg_ref = rest[len(after):]

        @pl.when(pl.program_id(1) == 0)
        def _():
            dg_ref[...] = jnp.zeros_like(dg_ref)

        for j in range(2):
            sl = slice(j * HG_HEAD_DIM, (j + 1) * HG_HEAD_DIM)
            o = of_ref[0, :, sl] + ob_ref[0, :, sl]
            hg = hg_ref[:, sl]
            d = d_ref[:, sl].astype(F32)
            sg = _sigmoid(hg)
            on = o * _rstd(o) * g_ref[...]
            dhg_ref[:, sl] = (d * on * sg * (1.0 + hg * (1.0 - sg))).astype(dhg_ref.dtype)
            dx, dg = _rms_bwd(o, g_ref[...], d * hg * sg)
            do_ref[:, sl] = dx
            dg_ref[0, :, sl] += dg

    blk = pl.BlockSpec((tr, w), lambda j, i: (i, j))
    dirs = [pl.BlockSpec((1, tr, w), lambda j, i, d=d: (d, i, j)) for d in range(2)]
    return pl.pallas_call(
        body, name=name, grid=(HG_DIM // w, n // tr),
        in_specs=dirs + [pl.BlockSpec((tr, w), lambda j, i: (i, OFF_HG // w + j)), pl.BlockSpec((1, HG_HEAD_DIM), lambda j, i: (0, 0)),
                         pl.BlockSpec((tr, w), lambda j, i: (i, ATT_Q_DIM // w + j))] + after_specs,
        out_specs=[blk, blk, pl.BlockSpec((1, 1, w), lambda j, i: (j, 0, 0))],
        out_shape=[SDS((n, HG_DIM), F32), SDS((n, HG_DIM), MXU_DTYPE), SDS((HG_DIM // w, 1, w), F32)],
        compiler_params=_cp("parallel", "arbitrary"))(o2, o2, p, g, dcat, *after)


XATT_TQ = 512


def _xattn_fwd(q, kv, name):
    n, nm = q.shape[0], kv.shape[0]
    tq = min(XATT_TQ, n)
    scale = X_HEAD_DIM ** -0.5

    def body(q_ref, k_ref, v_ref, o_ref):
        s = _dot(q_ref[...], k_ref[...], "nt") * scale
        e = jnp.exp(s - jnp.max(s, axis=-1, keepdims=True))
        o_ref[...] = _dot(e / jnp.sum(e, axis=-1, keepdims=True), v_ref[...]).astype(o_ref.dtype)

    qb = pl.BlockSpec((tq, X_HEAD_DIM), lambda h, i: (i, h))
    return pl.pallas_call(
        body, name=name, grid=(X_HEADS, n // tq),
        in_specs=[qb, pl.BlockSpec((nm, X_HEAD_DIM), lambda h, i: (0, h)), pl.BlockSpec((nm, X_HEAD_DIM), lambda h, i: (0, X_HEADS + h))],
        out_specs=qb, out_shape=SDS(q.shape, MXU_DTYPE), compiler_params=_cp("parallel", "parallel"))(q, kv, kv)


def _xattn_bwd(q, kv, do, name, after=()):
    n, nm = q.shape[0], kv.shape[0]
    tq = min(XATT_TQ, n)
    scale = X_HEAD_DIM ** -0.5
    after, after_specs = _unread(after)

    def body(q_ref, k_ref, v_ref, do_ref, *rest):
        dq_ref, dk_ref, dv_ref = rest[len(after):]

        @pl.when(pl.program_id(1) == 0)
        def _():
            dk_ref[...] = jnp.zeros_like(dk_ref)
            dv_ref[...] = jnp.zeros_like(dv_ref)

        qv, dov = q_ref[...], do_ref[...]
        s = _dot(qv, k_ref[...], "nt") * scale
        e = jnp.exp(s - jnp.max(s, axis=-1, keepdims=True))
        p = e / jnp.sum(e, axis=-1, keepdims=True)
        dp = _dot(dov, v_ref[...], "nt")
        ds = p * (dp - jnp.sum(p * dp, axis=-1, keepdims=True)) * scale
        dq_ref[...] = _dot(ds, k_ref[...]).astype(dq_ref.dtype)
        dk_ref[...] += _dot(ds, qv, "tn")
        dv_ref[...] += _dot(p, dov, "tn")

    qb = pl.BlockSpec((tq, X_HEAD_DIM), lambda h, i: (i, h))
    kb = pl.BlockSpec((nm, X_HEAD_DIM), lambda h, i: (0, h))
    return pl.pallas_call(
        body, name=name, grid=(X_HEADS, n // tq),
        in_specs=[qb, kb, pl.BlockSpec((nm, X_HEAD_DIM), lambda h, i: (0, X_HEADS + h)), qb] + after_specs, out_specs=[qb, kb, kb],
        out_shape=[SDS(q.shape, MXU_DTYPE), SDS((nm, X_HEADS * X_HEAD_DIM), F32), SDS((nm, X_HEADS * X_HEAD_DIM), F32)],
        compiler_params=_cp("parallel", "arbitrary"))(q, kv, kv, do, *after)


def _edge_rows(shape):
    row = lax.broadcasted_iota(jnp.int32, shape, 0)
    return row == 0, row == shape[0] - 1


def _shift_rows(u, down, edges):
    if down:
        return jnp.where(edges[0], 0.0, pltpu.roll(u, 1, axis=0))
    return jnp.where(edges[1], 0.0, pltpu.roll(u, u.shape[0] - 1, axis=0))


def _conv(u, w, b, edges):
    return b + _shift_rows(u, True, edges) * w[0:1, :] + u * w[1:2, :] + _shift_rows(u, False, edges) * w[2:3, :]


def _ff_specs(n):
    gate = lambda rows: pl.BlockSpec((rows, FF_COLS), lambda j: (0, j))
    val = lambda rows: pl.BlockSpec((rows, FF_COLS), lambda j: (0, FF_BLOCKS + j))
    return [gate(n), val(n), gate(3), val(3), gate(1), val(1)], gate


def _conv_gate(u, cw, cb, name):
    n = u.shape[0]
    ins, gate_blk = _ff_specs(n)

    def body(ug_ref, uv_ref, wg_ref, wv_ref, bg_ref, bv_ref, o_ref):
        edges = _edge_rows(ug_ref.shape)
        gate = _conv(ug_ref[...], wg_ref[...], bg_ref[...], edges)
        val = _conv(uv_ref[...], wv_ref[...], bv_ref[...], edges)
        o_ref[...] = (gate * _sigmoid(gate) * val).astype(o_ref.dtype)

    return pl.pallas_call(
        body, name=name, grid=(FF_BLOCKS,), in_specs=ins, out_specs=gate_blk(n), out_shape=SDS((n, D_FF), MXU_DTYPE),
        compiler_params=_cp("parallel"))(u, u, cw, cw, cb, cb)


def _conv_gate_bwd(u, cw, cb, da, name, after=()):
    n = u.shape[0]
    ins, gate_blk = _ff_specs(n)
    after, after_specs = _unread(after)

    def side(dacc, u, w, edges, du_ref, dw_ref, db_ref):
        nxt, prv = _shift_rows(dacc, False, edges), _shift_rows(dacc, True, edges)
        du_ref[...] = (nxt * w[0:1, :] + dacc * w[1:2, :] + prv * w[2:3, :]).astype(du_ref.dtype)
        db_ref[...] = jnp.sum(dacc, axis=0, keepdims=True)
        dw_ref[0:1, :] = jnp.sum(nxt * u, axis=0, keepdims=True)
        dw_ref[1:2, :] = jnp.sum(dacc * u, axis=0, keepdims=True)
        dw_ref[2:3, :] = jnp.sum(prv * u, axis=0, keepdims=True)

    def body(ug_ref, uv_ref, wg_ref, wv_ref, bg_ref, bv_ref, da_ref, *rest):
        dug_ref, duv_ref, dwg_ref, dwv_ref, dbg_ref, dbv_ref = rest[len(after):]
        ug, uv = ug_ref[...], uv_ref[...]
        edges = _edge_rows(ug.shape)
        gate = _conv(ug, wg_ref[...], bg_ref[...], edges)
        val = _conv(uv, wv_ref[...], bv_ref[...], edges)
        sg = _sigmoid(gate)
        dav = da_ref[...].astype(F32)
        side(dav * val * sg * (1.0 + gate * (1.0 - sg)), ug, wg_ref[...], edges, dug_ref, dwg_ref, dbg_ref)
        side(dav * gate * sg, uv, wv_ref[...], edges, duv_ref, dwv_ref, dbv_ref)

    return pl.pallas_call(
        body, name=name, grid=(FF_BLOCKS,), in_specs=ins + [gate_blk(n)] + after_specs,
        out_specs=[gate_blk(n), gate_blk(n), gate_blk(3), gate_blk(3), gate_blk(1), gate_blk(1)],
        out_shape=[SDS((n, D_FF), MXU_DTYPE)] * 2 + [SDS((3, D_FF), F32)] * 2 + [SDS((1, D_FF), F32)] * 2,
        compiler_params=_cp("parallel"))(u, u, cw, cw, cb, cb, da, *after)


def _adamw(w, g, m, v, name):
    r, c = w.shape[-2:]
    tr = _row_tile(r, ELEMENTWISE_ROWS)
    assert w.ndim == 2 or w.shape[:-2] == (1,), (name, w.shape)

    def body(w_ref, g_ref, m_ref, v_ref, d_ref, mo_ref, vo_ref, go_ref):
        gv = g_ref[...]
        go_ref[...] = gv
        mn = ADAM_B1 * m_ref[...] + (1.0 - ADAM_B1) * gv
        vn = ADAM_B2 * v_ref[...] + (1.0 - ADAM_B2) * gv * gv
        m_hat = mn / (1.0 - ADAM_B1 ** ADAM_STEP)
        v_hat = vn / (1.0 - ADAM_B2 ** ADAM_STEP)
        d_ref[...] = -ADAM_LR * (m_hat / (jnp.sqrt(v_hat) + ADAM_EPS) + ADAM_WD * w_ref[...])
        mo_ref[...] = mn
        vo_ref[...] = vn

    blk = pl.BlockSpec((tr, c), lambda i: (i, 0)) if w.ndim == 2 else pl.BlockSpec((1, tr, c), lambda i: (0, i, 0))
    out = SDS(w.shape, F32)
    return pl.pallas_call(body, name=name, grid=(r // tr,), in_specs=[blk] * 4, out_specs=[blk] * 4, out_shape=[out] * 4,
                          compiler_params=_cp("parallel"))(w, g, m, v)


ANY = pl.BlockSpec(memory_space=pl.ANY)


def _place():
    x, y, c = lax.axis_index("x"), lax.axis_index("y"), lax.axis_index("c")
    return x, y, c, [(1 - x, y), (x, 1 - y), (1 - x, 1 - y)]


def _gather_shards(shards, name):
    nt = len(shards)

    def body(*refs):
        ins, outs = refs[:nt], refs[nt:2 * nt]
        send, recv, fsend, frecv, osend, orecv = refs[2 * nt:]
        x, y, c, chips = _place()
        me = 2 * x + y

        def half(t, chip, cc):
            h = ins[t].shape[0] // 2
            return outs[t].at[chip, pl.ds(cc * h, h)]

        def ici(t, j):
            cx, cy = chips[j]
            h = ins[t].shape[0] // 2
            return pltpu.make_async_remote_copy(src_ref=ins[t].at[pl.ds(c * h, h)], dst_ref=half(t, me, c),
                                                send_sem=send.at[t, j], recv_sem=recv.at[t, j], device_id=(cx, cy, c), device_id_type=MESH)

        def landed(t, j):
            cx, cy = chips[j]
            blk = half(t, 2 * cx + cy, c)
            return pltpu.make_async_remote_copy(src_ref=blk, dst_ref=blk, send_sem=send.at[t, j], recv_sem=recv.at[t, j],
                                                device_id=(cx, cy, c), device_id_type=MESH)

        def d2d(t, j, cc):
            cx, cy = chips[j]
            blk = half(t, 2 * cx + cy, cc)
            return pltpu.make_async_remote_copy(src_ref=blk, dst_ref=blk, send_sem=fsend.at[t, j], recv_sem=frecv.at[t, j],
                                                device_id=(x, y, 1 - c), device_id_type=MESH)

        own = [pltpu.make_async_remote_copy(src_ref=ins[t], dst_ref=outs[t].at[me], send_sem=osend.at[t], recv_sem=orecv.at[t],
                                            device_id=(x, y, 1 - c), device_id_type=MESH) for t in range(nt)]
        for t in range(nt):
            for j in range(3):
                ici(t, j).start()
        for cp in own:
            cp.start()
        for t in range(nt):
            for j in range(3):
                landed(t, j).wait_recv()
                d2d(t, j, c).start()
        for t in range(nt):
            for j in range(3):
                d2d(t, j, 1 - c).wait_recv()
        for t in range(nt):
            for j in range(3):
                ici(t, j).wait_send()
                d2d(t, j, c).wait_send()
        for cp in own:
            cp.wait()

    return pl.pallas_call(
        body, name=name, in_specs=[ANY] * nt, out_specs=[ANY] * nt,
        out_shape=[SDS((4,) + s.shape, s.dtype) for s in shards],
        scratch_shapes=[pltpu.SemaphoreType.DMA((nt, 3))] * 4 + [pltpu.SemaphoreType.DMA((nt,))] * 2,
        compiler_params=pltpu.CompilerParams(has_side_effects=True))(*shards)


def _join_halves(bufs, name):
    nt = len(bufs)

    def body(*refs):
        outs = refs[nt:2 * nt]
        send, recv = refs[2 * nt:]
        x, y, c, _ = _place()
        cps = [pltpu.make_async_remote_copy(src_ref=outs[t].at[c], dst_ref=outs[t].at[c], send_sem=send.at[t], recv_sem=recv.at[t],
                                            device_id=(x, y, 1 - c), device_id_type=MESH) for t in range(nt)]
        for cp in cps:
            cp.start()
        for t in range(nt):
            theirs = outs[t].at[1 - c]
            pltpu.make_async_remote_copy(src_ref=theirs, dst_ref=theirs, send_sem=send.at[t], recv_sem=recv.at[t],
                                         device_id=(x, y, 1 - c), device_id_type=MESH).wait_recv()
        for cp in cps:
            cp.wait_send()

    return pl.pallas_call(
        body, name=name, in_specs=[ANY] * nt, out_specs=[ANY] * nt, out_shape=[SDS(b.shape, b.dtype) for b in bufs],
        input_output_aliases={t: t for t in range(nt)},
        scratch_shapes=[pltpu.SemaphoreType.DMA((nt,))] * 2,
        compiler_params=pltpu.CompilerParams(has_side_effects=True))(*bufs)


def _exchange_small(v, reduce, name, after=()):
    rows = v.shape[0]
    after, after_specs = _unread(after)

    def body(v_ref, *rest):
        o_ref, buf, send, recv = rest[-4:]
        x, y, c, _ = _place()
        me = 4 * x + 2 * y + c
        buf[me] = v_ref[...]

        def peer(dx, dy, dc):
            return (1 - x if dx else x, 1 - y if dy else y, 1 - c if dc else c)

        peers = [(dx, dy, dc) for dx in range(2) for dy in range(2) for dc in range(2) if (dx, dy, dc) != (0, 0, 0)]
        cps = []
        for j, (dx, dy, dc) in enumerate(peers):
            cps.append(pltpu.make_async_remote_copy(src_ref=v_ref, dst_ref=buf.at[me], send_sem=send.at[j], recv_sem=recv.at[j],
                                                    device_id=peer(dx, dy, dc), device_id_type=MESH))
        for cp in cps:
            cp.start()
        for j, (dx, dy, dc) in enumerate(peers):
            px, py, pc = peer(dx, dy, dc)
            blk = buf.at[4 * px + 2 * py + pc]
            pltpu.make_async_remote_copy(src_ref=blk, dst_ref=blk, send_sem=send.at[j], recv_sem=recv.at[j],
                                         device_id=(px, py, pc), device_id_type=MESH).wait_recv()
        for cp in cps:
            cp.wait_send()
        if reduce:
            acc = buf[0]
            for j in range(1, 8):
                acc = acc + buf[j]
            o_ref[...] = acc
        else:
            o_ref[...] = buf[...]

    vm = pl.BlockSpec(memory_space=pltpu.VMEM)
    return pl.pallas_call(
        body, name=name, in_specs=[vm] + after_specs, out_specs=vm, out_shape=SDS((rows, 128) if reduce else (8, rows, 128), F32),
        scratch_shapes=[pltpu.VMEM((8, rows, 128), F32), pltpu.SemaphoreType.DMA((7,)), pltpu.SemaphoreType.DMA((7,))],
        compiler_params=pltpu.CompilerParams(has_side_effects=True))(v, *after)


HBM = pl.BlockSpec(memory_space=pltpu.HBM)
SEM = pl.BlockSpec(memory_space=pltpu.SEMAPHORE)
TOKEN = pl.BlockSpec(memory_space=pltpu.VMEM)
TOKEN_SHAPE = SDS((8, 128), F32)
PEERS = 7


def _in_hbm(a):
    return pltpu.with_memory_space_constraint(a, pltpu.HBM)


def _split_params():
    return pltpu.CompilerParams(has_side_effects=pltpu.SideEffectType.DATAFLOW_SIDE_EFFECTING)


def _gather_start(shards, name, after=()):
    nt = len(shards)
    after, after_specs = _unread(after)

    def body(*refs):
        ins, lands = refs[:nt], refs[nt:2 * nt]
        outs = refs[2 * nt + len(after):]
        sends, recvs = outs[:nt], outs[nt:2 * nt]
        x, y, c, chips = _place()
        me = 2 * x + y
        for t in range(nt):
            h = ins[t].shape[0] // 2
            mine = pl.ds(c * h, h)
            for j, (cx, cy) in enumerate(chips):
                for dc in range(2):
                    pltpu.make_async_remote_copy(src_ref=ins[t].at[mine], dst_ref=lands[t].at[me, mine], send_sem=sends[t].at[2 * j + dc],
                                                 recv_sem=recvs[t].at[2 * j + c], device_id=(cx, cy, dc), device_id_type=MESH).start()
            pltpu.make_async_remote_copy(src_ref=ins[t], dst_ref=lands[t].at[me], send_sem=sends[t].at[PEERS - 1], recv_sem=recvs[t].at[PEERS - 1],
                                         device_id=(x, y, 1 - c), device_id_type=MESH).start()
        outs[-1][...] = jnp.zeros(TOKEN_SHAPE.shape, F32)

    lands = [lax.empty((4,) + s.shape, s.dtype) for s in shards]
    out = pl.pallas_call(
        body, name=name, in_specs=[HBM] * (2 * nt) + after_specs, out_specs=[SEM] * (2 * nt) + [HBM] * (2 * nt) + [TOKEN],
        out_shape=[pltpu.SemaphoreType.DMA((PEERS,))] * (2 * nt)
        + [pltpu.HBM(s.shape, s.dtype) for s in shards] + [pltpu.HBM(l.shape, l.dtype) for l in lands] + [TOKEN_SHAPE],
        input_output_aliases={t: 2 * nt + t for t in range(2 * nt)}, compiler_params=_split_params())(
            *[_in_hbm(s) for s in shards], *[_in_hbm(l) for l in lands], *after)
    return out[:nt], out[nt:2 * nt], out[2 * nt:3 * nt], out[3 * nt:4 * nt], out[-1]


def _gather_wait(sends, recvs, shards, lands, after, name):
    nt = len(shards)

    def body(*refs):
        ins, lands_ref = refs[:nt], refs[nt:2 * nt]
        send_refs, recv_refs = refs[2 * nt:3 * nt], refs[3 * nt:4 * nt]
        x, y, c, chips = _place()
        for t in range(nt):
            h = ins[t].shape[0] // 2
            for j, (cx, cy) in enumerate(chips):
                for cs in range(2):
                    blk = lands_ref[t].at[2 * cx + cy, pl.ds(cs * h, h)]
                    pltpu.make_async_remote_copy(src_ref=blk, dst_ref=blk, send_sem=send_refs[t].at[2 * j + cs], recv_sem=recv_refs[t].at[2 * j + cs],
                                                 device_id=(cx, cy, cs), device_id_type=MESH).wait()
            blk = lands_ref[t].at[2 * x + y]
            pltpu.make_async_remote_copy(src_ref=blk, dst_ref=blk, send_sem=send_refs[t].at[PEERS - 1], recv_sem=recv_refs[t].at[PEERS - 1],
                                         device_id=(x, y, 1 - c), device_id_type=MESH).wait()

    out = pl.pallas_call(
        body, name=name, in_specs=[HBM] * (2 * nt) + [SEM] * (2 * nt) + [ANY], out_specs=[HBM] * (2 * nt),
        out_shape=[pltpu.HBM(s.shape, s.dtype) for s in shards] + [pltpu.HBM(l.shape, l.dtype) for l in lands],
        input_output_aliases={t: t for t in range(2 * nt)}, compiler_params=_split_params())(*shards, *lands, *sends, *recvs, after)
    return out[nt:]


def _scatter_start(g, name):
    _, r, c_ = g.shape
    h = r // 2

    def body(g_ref, land, send, recv, g_thru, land_thru, token):
        x, y, c, chips = _place()
        for j, (cx, cy) in enumerate(chips):
            for dc in range(2):
                pltpu.make_async_remote_copy(src_ref=g_ref.at[2 * cx + cy, pl.ds(dc * h, h)], dst_ref=land.at[2 * j + c], send_sem=send.at[2 * j + dc],
                                             recv_sem=recv.at[2 * j + c], device_id=(cx, cy, dc), device_id_type=MESH).start()
        pltpu.make_async_remote_copy(src_ref=g_ref.at[2 * x + y, pl.ds((1 - c) * h, h)], dst_ref=land.at[PEERS - 1], send_sem=send.at[PEERS - 1],
                                     recv_sem=recv.at[PEERS - 1], device_id=(x, y, 1 - c), device_id_type=MESH).start()
        token[...] = jnp.zeros(TOKEN_SHAPE.shape, F32)

    land = lax.empty((PEERS, h, c_), g.dtype)
    return pl.pallas_call(
        body, name=name, in_specs=[HBM, HBM], out_specs=[SEM, SEM, HBM, HBM, TOKEN],
        out_shape=[pltpu.SemaphoreType.DMA((PEERS,)), pltpu.SemaphoreType.DMA((PEERS,)), pltpu.HBM(g.shape, g.dtype),
                   pltpu.HBM(land.shape, land.dtype), TOKEN_SHAPE],
        input_output_aliases={0: 2, 1: 3}, compiler_params=_split_params())(_in_hbm(g), _in_hbm(land))


def _scatter_wait(started, after, name):
    nt = len(started)

    def body(*refs):
        lands = refs[nt:2 * nt]
        sends, recvs = refs[2 * nt:3 * nt], refs[3 * nt:4 * nt]
        x, y, c, chips = _place()
        peers = [(cx, cy, dc) for cx, cy in chips for dc in range(2)] + [(x, y, 1 - c)]
        for t in range(nt):
            for k, peer in enumerate(peers):
                blk = lands[t].at[k]
                pltpu.make_async_remote_copy(src_ref=blk, dst_ref=blk, send_sem=sends[t].at[k], recv_sem=recvs[t].at[k],
                                             device_id=peer, device_id_type=MESH).wait()

    gs, lands = [s[2] for s in started], [s[3] for s in started]
    after, after_specs = _unread(after)
    out = pl.pallas_call(
        body, name=name, in_specs=[HBM] * (2 * nt) + [SEM] * (2 * nt) + after_specs, out_specs=[HBM] * (2 * nt),
        out_shape=[pltpu.HBM(a.shape, a.dtype) for a in gs + lands],
        input_output_aliases={t: t for t in range(2 * nt)}, compiler_params=_split_params())(
            *gs, *lands, *[s[0] for s in started], *[s[1] for s in started], *after)
    return out[:nt], out[nt:]


def _flips():
    return [(dx, dy, dc) for dx in range(2) for dy in range(2) for dc in range(2) if (dx, dy, dc) != (0, 0, 0)]


def _flipped(x, y, c, flips):
    dx, dy, dc = flips
    return (1 - x if dx else x, 1 - y if dy else y, 1 - c if dc else c)


def _small_start(v, name, after=()):
    after, after_specs = _unread(after)

    def body(v_ref, land, *rest):
        send, recv = rest[len(after):len(after) + 2]
        x, y, c, _ = _place()
        for j, flips in enumerate(_flips()):
            pltpu.make_async_remote_copy(src_ref=v_ref, dst_ref=land.at[4 * x + 2 * y + c], send_sem=send.at[j], recv_sem=recv.at[j],
                                         device_id=_flipped(x, y, c, flips), device_id_type=MESH).start()
        rest[-1][...] = jnp.zeros(TOKEN_SHAPE.shape, F32)

    land = lax.empty((8,) + v.shape, v.dtype)
    return pl.pallas_call(
        body, name=name, in_specs=[HBM, HBM] + after_specs, out_specs=[SEM, SEM, HBM, HBM, TOKEN],
        out_shape=[pltpu.SemaphoreType.DMA((PEERS,)), pltpu.SemaphoreType.DMA((PEERS,)), pltpu.HBM(v.shape, v.dtype),
                   pltpu.HBM(land.shape, land.dtype), TOKEN_SHAPE],
        input_output_aliases={0: 2, 1: 3}, compiler_params=_split_params())(_in_hbm(v), _in_hbm(land), *after)


def _small_wait(send, recv, v, land, after, name):
    after, after_specs = _unread(after)

    def body(v_ref, land_ref, send_ref, recv_ref, *rest):
        x, y, c, _ = _place()
        for j, flips in enumerate(_flips()):
            px, py, pc = _flipped(x, y, c, flips)
            blk = land_ref.at[4 * px + 2 * py + pc]
            pltpu.make_async_remote_copy(src_ref=blk, dst_ref=blk, send_sem=send_ref.at[j], recv_sem=recv_ref.at[j],
                                         device_id=(px, py, pc), device_id_type=MESH).wait()

    return pl.pallas_call(
        body, name=name, in_specs=[HBM, HBM, SEM, SEM] + after_specs, out_specs=[HBM, HBM],
        out_shape=[pltpu.HBM(v.shape, v.dtype), pltpu.HBM(land.shape, land.dtype)],
        input_output_aliases={0: 0, 1: 1}, compiler_params=_split_params())(v, land, send, recv, *after)


def _sum_small(v, land, name):
    def body(v_ref, land_ref, o_ref):
        x, y, c, _ = _place()
        me = 4 * x + 2 * y + c
        acc = jnp.where(me == 0, v_ref[...], land_ref[0])
        for d in range(1, 8):
            acc = acc + jnp.where(me == d, v_ref[...], land_ref[d])
        o_ref[...] = acc

    vm = pl.BlockSpec(memory_space=pltpu.VMEM)
    return pl.pallas_call(body, name=name, in_specs=[vm, vm], out_specs=vm, out_shape=SDS(v.shape, F32))(v, land)


def _sum_devices(g, land, me, core, name):
    npeer, h, c = land.shape
    tr = _row_tile(h, 2 * ELEMENTWISE_ROWS)
    steps = h // tr

    def body(ix_ref, own_ref, land_ref, o_ref):
        acc = own_ref[0].astype(F32)
        for j in range(npeer):
            acc = acc + land_ref[j].astype(F32)
        o_ref[0] = acc

    grid_spec = pltpu.PrefetchScalarGridSpec(
        num_scalar_prefetch=1, grid=(steps,),
        in_specs=[pl.BlockSpec((1, tr, c), lambda i, ix: (ix[0], ix[1] * steps + i, 0)), pl.BlockSpec((npeer, tr, c), lambda i, ix: (0, i, 0))],
        out_specs=pl.BlockSpec((1, tr, c), lambda i, ix: (ix[1], i, 0)))
    return pl.pallas_call(body, name=name, grid_spec=grid_spec, out_shape=SDS((2, h, c), F32),
                          compiler_params=_cp("parallel"))(jnp.stack([me, core]), g, land)


def _pack_small(parts):
    flat = jnp.concatenate([p.reshape(-1) for p in parts])
    total = flat.shape[0]
    rows = -(-total // 1024) * 8
    return jnp.pad(flat, (0, rows * 128 - total)).reshape(rows, 128)


def _unpack_small(packed, shapes):
    flat = packed.reshape(-1)
    out, off = [], 0
    for s in shapes:
        size = int(np.prod(s))
        out.append(flat[off:off + size].reshape(s))
        off += size
    return out


def _local_step(x, mem, target, w_in, first_after, mid_weights, ffn_weights, on_grad, gains, conv_w, conv_b, hg_lb):
    n = x.shape[0]
    cos, sin = _rope_tables(n)
    seg = _hg_segments()
    gp, gn = _hg_pair_sums()
    masks = _hg_level_masks()
    gq2 = jnp.tile(gains["q_norm_g"], (1, 2))
    gk2 = jnp.tile(gains["k_norm_g"], (1, 2))
    a0 = hg_lb[:, 0:1, :]
    a1 = hg_lb[:, 1:2, :]

    p, h1 = _norm_mm(x, gains["pre_mix_g"], w_in, F32, TOKEN_TILE, 1664, "in_proj", after=(first_after,))
    qr, kr = _qk_prep(p, gq2, gk2, cos, sin, "qk_prep")
    heads = lambda a: a.reshape(n, ATT_KV_HEADS, ATT_HEAD_DIM).transpose(1, 0, 2)
    kh = heads(kr)
    vh = heads(p[:, OFF_AV:OFF_AV + ATT_KV_DIM].astype(MXU_DTYPE))
    att = _attn_fwd(qr, kh, vh, "attn_fwd")
    o2, s0, hg_a, hg_e, hg_kept = _hgrn_fwd(p, a0, a1, seg, masks, "hgrn_fwd")
    rec = _hg_post(o2, p, gains["hg_out_norm_g"], "hg_post")
    cat = jnp.concatenate([att, rec], axis=1)
    w_out, w_xq, w_xkv, w_xo = mid_weights(cat)
    mixed, x1 = _mm_resid_norm(cat, w_out, x, gains["post_mix_g"], 512, "out_proj_resid")
    xq, h2 = _norm_mm(x1, gains["pre_x_g"], w_xq, MXU_DTYPE, TOKEN_TILE, 1024, "xq_proj")
    kv, mn = _norm_mm(mem, gains["mem_norm_g"], w_xkv, MXU_DTYPE, 256, 2048, "xkv_proj")
    ox = _xattn_fwd(xq, kv, "xattn_fwd")
    xo, x2 = _mm_resid_norm(ox, w_xo, x1, gains["post_x_g"], 512, "xo_proj_resid")
    w_up = ffn_weights("w_up", x2)
    u, h3 = _norm_mm(x2, gains["pre_ffn_g"], w_up, F32, TOKEN_TILE, 1408, "up_proj")
    act = _conv_gate(u, conv_w, conv_b, "conv_gate")
    w_down = ffn_weights("w_down", act)
    dn, d3, loss = _mm_resid_norm(act, w_down, x2, gains["post_ffn_g"], 512, "down_proj_resid_loss", target=target)

    gs = {}
    d_act, d_dn, gs["post_ffn_g"] = _norm_bwd_mm(dn, gains["post_ffn_g"], d3, w_down, F32, 512, 1408, "ffn_post_bwd_down_dx")
    tok = on_grad("w_down", _mm(act, d_dn, "tn", WIRE_DTYPE, 1408, 1024, "down_dw"))
    du_g, du_v, dcw_g, dcw_v, dcb_g, dcb_v = _conv_gate_bwd(u, conv_w, conv_b, d_act, "conv_gate_bwd", after=(tok,))
    gs["conv_w"] = jnp.concatenate([dcw_g, dcw_v], axis=1)
    gs["conv_b"] = jnp.concatenate([dcb_g, dcb_v], axis=1)
    ff_shard = w_up.shape[2]
    g_up = _dw_by_owner(h3, du_g, ff_shard, 0, None, 512, "up_dw_gate")
    tok = on_grad("w_up", _dw_by_owner(h3, du_v, ff_shard, 2, g_up, 512, "up_dw_value"))
    d2, gs["pre_ffn_g"] = _dx_norm_bwd([(du_g, 0), (du_g, 1), (du_v, 0), (du_v, 1)], w_up, x2, gains["pre_ffn_g"], d3, 512,
                                       "up_dx_pre_bwd", after=(tok,))
    d_ox, d_xo, gs["post_x_g"] = _norm_bwd_mm(xo, gains["post_x_g"], d2, w_xo, MXU_DTYPE, 512, 1024, "x_post_bwd_xo_dx")
    tok = on_grad("w_xo", _mm(ox, d_xo, "tn", WIRE_DTYPE, 512, 1024, "xo_dw"))
    d_xq, d_k, d_v = _xattn_bwd(xq, kv, d_ox, "xattn_bwd", after=(tok,))
    d_kv = jnp.concatenate([d_k, d_v], axis=1).astype(MXU_DTYPE)
    tok = on_grad("w_xq", _mm(h2, d_xq, "tn", WIRE_DTYPE, 512, 1024, "xq_dw"))
    tok_kv = on_grad("w_xkv", _dw_by_owner(mn, d_kv, w_xkv.shape[2], 0, None, 512, "xkv_dw"))
    d1, gs["pre_x_g"] = _dx_norm_bwd([(d_xq, 0)], w_xq[None], x1, gains["pre_x_g"], d2, 512, "xq_dx_pre_bwd", after=(tok, tok_kv))
    d_mn = _mm_nt_parts([(d_kv, s) for s in range(4)], w_xkv, F32, 256, 1024, "xkv_dx")
    _, gs["mem_norm_g"] = _norm_bwd(mem, gains["mem_norm_g"], d_mn, None, MXU_DTYPE, "mem_norm_bwd")
    d_cat, d_mixed, gs["post_mix_g"] = _norm_bwd_mm(mixed, gains["post_mix_g"], d1, w_out, MXU_DTYPE, 512, 1024, "mix_post_bwd_out_dx")
    tok = on_grad("w_out", _mm(cat, d_mixed, "tn", WIRE_DTYPE, 512, 1024, "out_dw"))
    d_o, d_hg, dg_hg = _hg_post_bwd(o2, p, gains["hg_out_norm_g"], d_cat, "hg_post_bwd", after=(tok,))
    gs["hg_out_norm_g"] = dg_hg.reshape(HG_HEADS, HG_HEAD_DIM).sum(axis=0, keepdims=True)
    dhq2, dz2, dhv2, dlb = _hgrn_bwd(p, a0, a1, masks, gp, gn, d_o, s0, hg_a, hg_e, hg_kept, "hgrn_bwd")
    lb = jax.nn.sigmoid(a0 - a1)
    da0 = dlb * lb * (1.0 - lb)
    gs["hg_lb"] = jnp.concatenate([da0, -da0], axis=1)
    d_qr, d_kh, d_vh = _attn_bwd(qr, kh, vh, cat, d_cat, "attn_bwd")
    unheads = lambda a: a.transpose(2, 0, 1).reshape(n, ATT_KV_DIM)
    d_aq, d_ak, dgq, dgk = _qk_prep_bwd(p, gq2, gk2, cos, sin, d_qr, unheads(d_kh), "qk_prep_bwd")
    gs["q_norm_g"] = dgq.reshape(ATT_HEADS, ATT_HEAD_DIM).sum(axis=0, keepdims=True)
    gs["k_norm_g"] = dgk.reshape(ATT_KV_HEADS, ATT_HEAD_DIM).sum(axis=0, keepdims=True)
    d_p = jnp.concatenate([d_aq, d_ak, unheads(d_vh).astype(MXU_DTYPE), (dhq2[0] + dhq2[1]).astype(MXU_DTYPE),
                           dz2[0].astype(MXU_DTYPE), dz2[1].astype(MXU_DTYPE), (dhv2[0] + dhv2[1]).astype(MXU_DTYPE), d_hg], axis=1)
    tok = on_grad("w_in", _mm(h1, d_p, "tn", WIRE_DTYPE, 512, 1664, "in_dw"))
    grad_x, gs["pre_mix_g"] = _dx_norm_bwd([(d_p, 0)], w_in[None], x, gains["pre_mix_g"], d1, 512, "in_dx_pre_bwd", after=(tok,))
    return loss, grad_x, gs


MATS = ("w_in", "w_out", "w_xq", "w_xkv", "w_xo", "w_up", "w_down")
GAINS = ("pre_mix_g", "q_norm_g", "k_norm_g", "hg_out_norm_g", "post_mix_g", "pre_x_g", "mem_norm_g", "post_x_g", "pre_ffn_g", "post_ffn_g")
WEIGHTS = ('pre_mix_g', 'w_in', 'q_norm_g', 'k_norm_g', 'hg_lb', 'hg_out_norm_g', 'w_out', 'post_mix_g', 'pre_x_g', 'mem_norm_g', 'w_xq',
           'w_xkv', 'w_xo', 'post_x_g', 'pre_ffn_g', 'w_up', 'conv_w', 'conv_b', 'w_down', 'post_ffn_g')


def kernel(x, mem, pre_mix_g, w_in, q_norm_g, k_norm_g, hg_lb, hg_out_norm_g, w_out, post_mix_g, pre_x_g, mem_norm_g, w_xq, w_xkv, w_xo, post_x_g, pre_ffn_g, w_up, conv_w, conv_b, w_down, post_ffn_g, loss_target, m_pre_mix_g, m_w_in, m_q_norm_g, m_k_norm_g, m_hg_lb, m_hg_out_norm_g, m_w_out, m_post_mix_g, m_pre_x_g, m_mem_norm_g, m_w_xq, m_w_xkv, m_w_xo, m_post_x_g, m_pre_ffn_g, m_w_up, m_conv_w, m_conv_b, m_w_down, m_post_ffn_g, v_pre_mix_g, v_w_in, v_q_norm_g, v_k_norm_g, v_hg_lb, v_hg_out_norm_g, v_w_out, v_post_mix_g, v_pre_x_g, v_mem_norm_g, v_w_xq, v_w_xkv, v_w_xo, v_post_x_g, v_pre_ffn_g, v_w_up, v_conv_w, v_conv_b, v_w_down, v_post_ffn_g):
    args = dict(locals())
    w = {k: args[k] for k in WEIGHTS}
    m = {k: args["m_" + k] for k in WEIGHTS}
    v = {k: args["v_" + k] for k in WEIGHTS}
    chip = 2 * lax.axis_index("x") + lax.axis_index("y")
    core = lax.axis_index("c")

    shards = {k: w[k][0].astype(WIRE_DTYPE) for k in MATS}

    def whole(k, g):
        return g if k in ("w_xkv", "w_up") else g.reshape(-1, g.shape[-1])

    w_in_shards = _gather_shards([shards["w_in"]], "gather_w_in")[0]
    w_in_full = jnp.concatenate([w_in_shards[s] for s in range(4)], axis=1)
    small_in = _exchange_small(_pack_small([w["conv_w"][0], w["hg_lb"]]), False, "gather_small")
    mid_names, ffn_names = ("w_out", "w_xq", "w_xkv", "w_xo"), ("w_up", "w_down")
    mid = _gather_start([shards[k] for k in mid_names], "gather_mid_start", after=(w_in_full, small_in))
    ffn = _gather_start([shards[k] for k in ffn_names], "gather_ffn_start", after=(mid[4],))

    def mid_weights(after):
        return [whole(k, g) for k, g in zip(mid_names, _gather_wait(*mid[:4], after, "gather_mid_wait"))]

    def ffn_weights(k, after):
        t = ffn_names.index(k)
        return whole(k, _gather_wait(*[part[t:t + 1] for part in ffn[:4]], after, "gather_wait_" + k)[0])

    cw_parts, lb_parts = [], []
    for s in range(4):
        cw_s, lb_s = _unpack_small(small_in[2 * s], [w["conv_w"][0].shape, w["hg_lb"].shape])
        cw_parts.append(cw_s)
        lb_parts.append(lb_s)
    conv_w_full = jnp.concatenate(cw_parts, axis=1)
    hg_lb_full = jnp.concatenate(lb_parts, axis=2)

    started = {}

    def on_grad(k, g):
        if k == "w_in":
            g = g.reshape(g.shape[0], 4, g.shape[1] // 4).transpose(1, 0, 2)
        elif g.ndim == 2:
            g = g.reshape(4, g.shape[0] // 4, g.shape[1])
        *started[k], token = _scatter_start(g, "grad_start_" + k)
        return token

    gains = {k: w[k] for k in GAINS}
    loss_part, grad_x, gs = _local_step(x[0], mem[0], loss_target[0], w_in_full, ffn[4], mid_weights, ffn_weights, on_grad, gains,
                                        conv_w_full, w["conv_b"], hg_lb_full)
    gs["loss"] = loss_part

    grads, delta, new_m, new_v = {}, {}, {}, {}

    def reduce_matrices(names, after, tag):
        sent, landed = _scatter_wait([started[k] for k in names], after, "grad_wait_" + tag)
        halves = [_sum_devices(g, land, chip, core, "grad_sum_" + k) for k, g, land in zip(names, sent, landed)]
        for k, r in zip(names, _join_halves(halves, "grad_join_" + tag)):
            grads[k] = r.reshape(1, -1, r.shape[-1])

    def adamw(names):
        for k in names:
            shape = w[k].shape
            keep = len(shape) == 3 and shape[0] == 1
            two_d = lambda a: a.reshape(shape) if keep else a.reshape(-1, shape[-1])
            d, mo, vo, go = _adamw(two_d(w[k]), two_d(grads[k]), two_d(m[k]), two_d(v[k]), "adamw_" + k)
            delta[k], new_m[k], new_v[k], grads[k] = d.reshape(shape), mo.reshape(shape), vo.reshape(shape), go.reshape(shape)

    small_names = GAINS + ("conv_b", "conv_w", "hg_lb")
    packed = _pack_small([gs[k] for k in small_names + ("loss",)])
    small = _small_start(packed, "reduce_small_start", after=(grad_x,))

    early = tuple(k for k in MATS if k != "w_in")
    reduce_matrices(early, (grad_x, small[4]), "early")
    adamw(early)

    mine, others = _small_wait(*small[:4], tuple(new_v[k] for k in early), "reduce_small_wait")
    reduced_small = _sum_small(mine, others, "reduce_small_sum")
    *summed, loss = _unpack_small(reduced_small, [gs[k].shape for k in small_names + ("loss",)])
    loss = loss[0, 0]
    for k, g in zip(small_names, summed):
        grads[k] = g
    ncw = w["conv_w"].shape[2]
    grads["conv_w"] = lax.dynamic_slice_in_dim(grads["conv_w"], chip * ncw, ncw, axis=1)[None]
    nlb = w["hg_lb"].shape[2]
    grads["hg_lb"] = lax.dynamic_slice_in_dim(grads["hg_lb"], chip * nlb, nlb, axis=2)
    replicated = GAINS + ("conv_b",)
    shapes = [w[k].shape for k in replicated]
    rows = sum(int(np.prod(s)) for s in shapes) // 128
    pack = lambda d: jnp.concatenate([d[k].reshape(-1) for k in replicated]).reshape(rows, 128)
    outs = _adamw(pack(w), reduced_small[:rows], pack(m), pack(v), "adamw_replicated")
    for into, packed_out in zip((delta, new_m, new_v, grads), outs):
        for k, a in zip(replicated, _unpack_small(packed_out, shapes)):
            into[k] = a
    adamw(("conv_w", "hg_lb"))

    reduce_matrices(("w_in",), tuple(new_v[k] for k in early + small_names), "late")
    adamw(("w_in",))
    return (loss, grad_x[None], *[grads[k] for k in WEIGHTS], *[delta[k] for k in WEIGHTS],
            *[new_m[k] for k in WEIGHTS], *[new_v[k] for k in WEIGHTS])
```

```python
import numpy as np
import jax
import jax.numpy as jnp
from jax import lax
from jax.experimental import pallas as pl
from jax.experimental.pallas import tpu as pltpu

F32 = jnp.float32
MXU_DTYPE = jnp.bfloat16
WIRE_DTYPE = jnp.bfloat16
VMEM_LIMIT_BYTES = 56 * 1024 * 1024
LANES = 128
ROWS_PER_16BIT_TILE = 16
ELEMENTWISE_ROWS = 256
EPS = 1e-6
MESH = pl.DeviceIdType.MESH

GRID_W = 64
ATT_HEADS, ATT_KV_HEADS, ATT_HEAD_DIM = 8, 2, 64
ATT_GROUP = ATT_HEADS // ATT_KV_HEADS
ATT_Q_DIM, ATT_KV_DIM = 512, 128
ROPE_THETA = 10000.0
HG_HEADS, HG_HEAD_DIM, HG_DIM = 4, 128, 512
HG_CHUNK = 128
HG_LEVELS = 7
HG_PAIR = 2 * HG_HEAD_DIM
HG_KEPT = 7
X_HEADS, X_HEAD_DIM = 4, 256
D_FF = 2816
FF_COLS = 256
FF_BLOCKS = D_FF // FF_COLS
OFF_AK, OFF_AV, OFF_HQ, OFF_ZF, OFF_ZB, OFF_HI, OFF_HG = 512, 640, 768, 1280, 1792, 2304, 2816

ADAM_LR, ADAM_B1, ADAM_B2, ADAM_EPS, ADAM_WD, ADAM_STEP = 0.001, 0.9, 0.999, 1e-08, 0.01, 10

SDS = jax.ShapeDtypeStruct


def _cp(*sem):
    return pltpu.CompilerParams(dimension_semantics=sem, vmem_limit_bytes=VMEM_LIMIT_BYTES)


def _row_tile(rows, cap):
    if rows <= cap:
        return rows
    return max(t for t in range(ROWS_PER_16BIT_TILE, cap + 1, ROWS_PER_16BIT_TILE) if rows % t == 0)


def _dot(a, b, form="nn"):
    dims = {"nn": (((1,), (0,)), ((), ())), "nt": (((1,), (1,)), ((), ())), "tn": (((0,), (0,)), ((), ()))}[form]
    return lax.dot_general(a.astype(MXU_DTYPE), b.astype(MXU_DTYPE), dims, preferred_element_type=F32)


def _sigmoid(x):
    return 1.0 / (1.0 + jnp.exp(-x))


def _rstd(x):
    return lax.rsqrt(jnp.mean(x * x, axis=-1, keepdims=True) + EPS)


def _rms_bwd(x, g, dy):
    r = _rstd(x)
    xh = x * r
    dn = dy * g
    dx = r * (dn - xh * jnp.mean(dn * xh, axis=-1, keepdims=True))
    return dx, jnp.sum(dy * xh, axis=0, keepdims=True)


def _unread(after):
    after = tuple(a for a in after if a is not None)
    return after, [pl.BlockSpec(memory_space=pl.ANY)] * len(after)


def _mm(a, b, form, out_dtype, tm, tn, name, after=()):
    after, after_specs = _unread(after)
    if form == "nn":
        (m, k), n = a.shape, b.shape[1]
    elif form == "nt":
        (m, k), n = a.shape, b.shape[0]
    else:
        (k, m), n = a.shape, b.shape[1]
    tm, tn = min(tm, m), min(tn, n)
    assert m % tm == 0 and n % tn == 0, (name, m, n, tm, tn)

    def body(a_ref, b_ref, *rest):
        o_ref = rest[-1]
        o_ref[...] = _dot(a_ref[...], b_ref[...], form).astype(o_ref.dtype)

    a_spec = pl.BlockSpec((k, tm), lambda i, j: (0, i)) if form == "tn" else pl.BlockSpec((tm, k), lambda i, j: (i, 0))
    b_spec = pl.BlockSpec((tn, k), lambda i, j: (j, 0)) if form == "nt" else pl.BlockSpec((k, tn), lambda i, j: (0, j))
    return pl.pallas_call(
        body, name=name, grid=(m // tm, n // tn), in_specs=[a_spec, b_spec] + after_specs,
        out_specs=pl.BlockSpec((tm, tn), lambda i, j: (i, j)), out_shape=SDS((m, n), out_dtype),
        compiler_params=_cp("parallel", "parallel"))(a, b, *after)


def _mm_nt_parts(a_parts, b, out_dtype, tm, tn, name, after=()):
    after, after_specs = _unread(after)
    parts, n, p = b.shape
    m = a_parts[0][0].shape[0]
    tm, tn = min(tm, m), min(tn, n)
    assert m % tm == 0 and n % tn == 0 and len(a_parts) == parts, (name, m, b.shape)

    def body(*refs):
        o_ref = refs[-1]
        acc = _dot(refs[0][...], refs[parts][0], "nt")
        for s in range(1, parts):
            acc = acc + _dot(refs[s][...], refs[parts + s][0], "nt")
        o_ref[...] = acc.astype(o_ref.dtype)

    a_specs = [pl.BlockSpec((tm, p), lambda i, j, cb=cb: (i, cb)) for _, cb in a_parts]
    b_specs = [pl.BlockSpec((1, tn, p), lambda i, j, s=s: (s, j, 0)) for s in range(parts)]
    return pl.pallas_call(
        body, name=name, grid=(m // tm, n // tn), in_specs=a_specs + b_specs + after_specs,
        out_specs=pl.BlockSpec((tm, tn), lambda i, j: (i, j)), out_shape=SDS((m, n), out_dtype),
        compiler_params=_cp("parallel", "parallel"))(*[arr for arr, _ in a_parts], *([b] * parts), *after)


def _norm_bwd_mm(y, g, d, w, out_dtype, tm, tn, name):
    n, dm = y.shape
    nn = w.shape[0]
    tm, tn = min(tm, n), min(tn, nn)
    assert n % tm == 0 and nn % tn == 0 and w.shape[1] == dm, (name, y.shape, w.shape)

    def body(y_ref, g_ref, d_ref, w_ref, dx_ref, dy_ref, dg_ref, dys):
        i, j = pl.program_id(0), pl.program_id(1)

        @pl.when(jnp.logical_and(i == 0, j == 0))
        def _():
            dg_ref[...] = jnp.zeros_like(dg_ref)

        @pl.when(j == 0)
        def _():
            dy, dg = _rms_bwd(y_ref[...], g_ref[...], d_ref[...])
            dy = dy.astype(MXU_DTYPE)
            dys[...] = dy
            dy_ref[...] = dy
            dg_ref[...] += dg

        dx_ref[...] = _dot(dys[...], w_ref[...], "nt").astype(dx_ref.dtype)

    row = pl.BlockSpec((tm, dm), lambda i, j: (i, 0))
    vec = pl.BlockSpec((1, dm), lambda i, j: (0, 0))
    return pl.pallas_call(
        body, name=name, grid=(n // tm, nn // tn), in_specs=[row, vec, row, pl.BlockSpec((tn, dm), lambda i, j: (j, 0))],
        out_specs=[pl.BlockSpec((tm, tn), lambda i, j: (i, j)), row, vec],
        out_shape=[SDS((n, nn), out_dtype), SDS((n, dm), MXU_DTYPE), SDS((1, dm), F32)],
        scratch_shapes=[pltpu.VMEM((tm, dm), MXU_DTYPE)],
        compiler_params=_cp("arbitrary", "arbitrary"))(y, g, d, w)


def _mm_resid_norm(a, b, x, g, tm, name, target=None):
    n, k = a.shape
    d = b.shape[1]
    tm = min(tm, n)
    assert n % tm == 0 and x.shape == (n, d), (name, a.shape, b.shape)
    with_loss = target is not None

    def body(a_ref, b_ref, x_ref, g_ref, *rest):
        y = _dot(a_ref[...], b_ref[...])
        out = x_ref[...] + y * _rstd(y) * g_ref[...]
        if not with_loss:
            y_ref, o_ref = rest
            y_ref[...] = y
            o_ref[...] = out
            return
        t_ref, y_ref, d_ref, l_ref = rest
        y_ref[...] = y
        diff = out - t_ref[...]
        d_ref[...] = diff * (1.0 / d)

        @pl.when(pl.program_id(0) == 0)
        def _():
            l_ref[...] = jnp.zeros_like(l_ref)

        l_ref[...] += 0.5 * jnp.sum(jnp.mean(diff * diff, axis=-1, keepdims=True), axis=0, keepdims=True)

    row = pl.BlockSpec((tm, d), lambda i: (i, 0))
    ins = [pl.BlockSpec((tm, k), lambda i: (i, 0)), pl.BlockSpec((k, d), lambda i: (0, 0)), row, pl.BlockSpec((1, d), lambda i: (0, 0))]
    out = SDS((n, d), F32)
    if with_loss:
        return pl.pallas_call(body, name=name, grid=(n // tm,), in_specs=ins + [row], out_specs=[row, row, pl.BlockSpec((1, 1), lambda i: (0, 0))],
                              out_shape=[out, out, SDS((1, 1), F32)], compiler_params=_cp("arbitrary"))(a, b, x, g, target)
    return pl.pallas_call(body, name=name, grid=(n // tm,), in_specs=ins, out_specs=[row, row], out_shape=[out, out],
                          compiler_params=_cp("parallel"))(a, b, x, g)


def _dx_norm_bwd(a_parts, b, x, g, res, tm, name, after=()):
    after, after_specs = _unread(after)
    parts, d, p = b.shape
    n = x.shape[0]
    tm = min(tm, n)
    assert n % tm == 0 and len(a_parts) == parts and x.shape[1] == d, (name, x.shape, b.shape)

    def body(*refs):
        x_ref, g_ref, res_ref = refs[2 * parts:2 * parts + 3]
        dx_ref, dg_ref = refs[-2:]
        dh = _dot(refs[0][...], refs[parts][0], "nt")
        for s in range(1, parts):
            dh = dh + _dot(refs[s][...], refs[parts + s][0], "nt")
        dx, dg = _rms_bwd(x_ref[...], g_ref[...], dh)
        dx_ref[...] = dx + res_ref[...]

        @pl.when(pl.program_id(0) == 0)
        def _():
            dg_ref[...] = jnp.zeros_like(dg_ref)

        dg_ref[...] += dg

    a_specs = [pl.BlockSpec((tm, p), lambda i, cb=cb: (i, cb)) for _, cb in a_parts]
    b_specs = [pl.BlockSpec((1, d, p), lambda i, s=s: (s, 0, 0)) for s in range(parts)]
    row = pl.BlockSpec((tm, d), lambda i: (i, 0))
    vec = pl.BlockSpec((1, d), lambda i: (0, 0))
    return pl.pallas_call(
        body, name=name, grid=(n // tm,), in_specs=a_specs + b_specs + [row, vec, row] + after_specs,
        out_specs=[row, vec], out_shape=[SDS((n, d), F32), SDS((1, d), F32)],
        compiler_params=_cp("arbitrary"))(*[arr for arr, _ in a_parts], *([b] * parts), x, g, res, *after)


def _dw_by_owner(a, b, tn, first, into, tm, name):
    k, m = a.shape
    cnt = b.shape[1] // tn
    tm = min(tm, m)
    assert m % tm == 0 and b.shape[1] == cnt * tn and first + cnt <= 4, (name, a.shape, b.shape)

    def body(a_ref, b_ref, *rest):
        rest[-1][0] = _dot(a_ref[...], b_ref[...], "tn").astype(rest[-1].dtype)

    extra = [] if into is None else [into]
    return pl.pallas_call(
        body, name=name, grid=(m // tm, cnt),
        in_specs=[pl.BlockSpec((k, tm), lambda i, j: (0, i)), pl.BlockSpec((k, tn), lambda i, j: (0, j))] + [pl.BlockSpec(memory_space=pl.ANY)] * len(extra),
        out_specs=pl.BlockSpec((1, tm, tn), lambda i, j: (first + j, i, 0)), out_shape=SDS((4, m, tn), WIRE_DTYPE),
        input_output_aliases={2: 0} if extra else {},
        compiler_params=_cp("parallel", "parallel"))(a, b, *extra)


def _norm_mm(x, g, w, out_dtype, tm, tn, name, after=()):
    after, after_specs = _unread(after)
    m, d = x.shape
    sharded = w.ndim == 3
    n = w.shape[-1] * (w.shape[0] if sharded else 1)
    tm, tn = min(tm, m), (w.shape[-1] if sharded else min(tn, n))
    assert m % tm == 0 and n % tn == 0, (name, m, n, tm, tn)

    def body(x_ref, g_ref, w_ref, *rest):
        o_ref, h_ref, hs = rest[-3:]

        @pl.when(pl.program_id(1) == 0)
        def _():
            xv = x_ref[...]
            h = (xv * _rstd(xv) * g_ref[...]).astype(MXU_DTYPE)
            hs[...] = h
            h_ref[...] = h

        o_ref[...] = _dot(hs[...], w_ref[0] if sharded else w_ref[...]).astype(o_ref.dtype)

    w_spec = pl.BlockSpec((1, d, tn), lambda i, j: (j, 0, 0)) if sharded else pl.BlockSpec((d, tn), lambda i, j: (0, j))
    return pl.pallas_call(
        body, name=name, grid=(m // tm, n // tn),
        in_specs=[pl.BlockSpec((tm, d), lambda i, j: (i, 0)), pl.BlockSpec((1, d), lambda i, j: (0, 0)), w_spec] + after_specs,
        out_specs=[pl.BlockSpec((tm, tn), lambda i, j: (i, j)), pl.BlockSpec((tm, d), lambda i, j: (i, 0))],
        out_shape=[SDS((m, n), out_dtype), SDS((m, d), MXU_DTYPE)],
        scratch_shapes=[pltpu.VMEM((tm, d), MXU_DTYPE)],
        compiler_params=_cp("parallel", "arbitrary"))(x, g, w, *after)


ROW_TILE = 512
TOKEN_TILE = 1024


def _norm_bwd(x, g, dy, res, out_dtype, name):
    n, d = x.shape
    tr = min(ROW_TILE, n)
    has_res = res is not None

    def body(*refs):
        x_ref, g_ref, dy_ref = refs[:3]
        dx_ref, dg_ref = refs[-2:]
        dx, dg = _rms_bwd(x_ref[...], g_ref[...], dy_ref[...].astype(F32))
        if has_res:
            dx = dx + refs[3][...]
        dx_ref[...] = dx.astype(dx_ref.dtype)

        @pl.when(pl.program_id(0) == 0)
        def _():
            dg_ref[...] = jnp.zeros_like(dg_ref)

        dg_ref[...] += dg

    row = pl.BlockSpec((tr, d), lambda i: (i, 0))
    vec = pl.BlockSpec((1, d), lambda i: (0, 0))
    ins = [x, g, dy] + ([res] if has_res else [])
    return pl.pallas_call(
        body, name=name, grid=(n // tr,), in_specs=[row, vec, row] + ([row] if has_res else []),
        out_specs=[row, vec], out_shape=[SDS((n, d), out_dtype), SDS((1, d), F32)],
        compiler_params=_cp("arbitrary"))(*ins)


def _rope_tables(n):
    pairs = ATT_HEAD_DIM // 4
    t = np.arange(n)
    inv = np.power(ROPE_THETA, -np.arange(pairs, dtype=np.float32) / pairs).astype(np.float32)
    ang = np.concatenate([(t // GRID_W)[:, None].astype(np.float32) * inv, (t % GRID_W)[:, None].astype(np.float32) * inv], axis=-1)
    cos = np.repeat(np.cos(ang), 2, axis=-1)
    sin = np.repeat(np.sin(ang), 2, axis=-1) * np.tile(np.array([-1.0, 1.0], np.float32), ATT_HEAD_DIM // 2)
    return jnp.asarray(np.tile(cos, 2), F32), jnp.asarray(np.tile(sin, 2), F32)


def _swap_pairs(x):
    lane = lax.broadcasted_iota(jnp.int32, x.shape, 1)
    return jnp.where((lane & 1) == 0, pltpu.roll(x, 127, axis=1), pltpu.roll(x, 1, axis=1))


def _head_mean(v):
    lane = lax.broadcasted_iota(jnp.int32, v.shape, 1)
    lo = jnp.where(lane < ATT_HEAD_DIM, v, 0.0)
    s0 = jnp.sum(lo, axis=-1, keepdims=True)
    s1 = jnp.sum(v - lo, axis=-1, keepdims=True)
    return jnp.where(lane < ATT_HEAD_DIM, s0, s1) * (1.0 / ATT_HEAD_DIM)


def _qk_prep(p, gq, gk, cos, sin, name):
    n = p.shape[0]
    tr = min(ROW_TILE, n)

    def one(xv, g, c, s):
        xn = xv * lax.rsqrt(_head_mean(xv * xv) + EPS) * g
        return xn * c + _swap_pairs(xn) * s

    def body(q_ref, k_ref, gq_ref, gk_ref, c_ref, s_ref, qo_ref, ko_ref):
        c, s = c_ref[...], s_ref[...]
        for j in range(ATT_Q_DIM // 128):
            qo_ref[:, j * 128:(j + 1) * 128] = one(q_ref[:, j * 128:(j + 1) * 128], gq_ref[...], c, s).astype(qo_ref.dtype)
        ko_ref[...] = one(k_ref[...], gk_ref[...], c, s).astype(ko_ref.dtype)

    vec = pl.BlockSpec((1, 128), lambda i: (0, 0))
    tab = pl.BlockSpec((tr, 128), lambda i: (i, 0))
    return pl.pallas_call(
        body, name=name, grid=(n // tr,),
        in_specs=[pl.BlockSpec((tr, ATT_Q_DIM), lambda i: (i, 0)), pl.BlockSpec((tr, 128), lambda i: (i, OFF_AK // 128)), vec, vec, tab, tab],
        out_specs=[pl.BlockSpec((tr, ATT_Q_DIM), lambda i: (i, 0)), tab],
        out_shape=[SDS((n, ATT_Q_DIM), MXU_DTYPE), SDS((n, ATT_KV_DIM), MXU_DTYPE)],
        compiler_params=_cp("parallel"))(p, p, gq, gk, cos, sin)


def _qk_prep_bwd(p, gq, gk, cos, sin, dq, dk, name):
    n = p.shape[0]
    tr = min(ROW_TILE, n)

    def one(xv, g, c, s, dout):
        dxn = dout * c + _swap_pairs(dout * s)
        r = lax.rsqrt(_head_mean(xv * xv) + EPS)
        xh = xv * r
        dn = dxn * g
        dx = r * (dn - xh * _head_mean(dn * xh))
        return dx, jnp.sum(dxn * xh, axis=0, keepdims=True)

    def body(q_ref, k_ref, gq_ref, gk_ref, c_ref, s_ref, dq_ref, dk_ref, dqo_ref, dko_ref, dgq_ref, dgk_ref):
        @pl.when(pl.program_id(0) == 0)
        def _():
            dgq_ref[...] = jnp.zeros_like(dgq_ref)
            dgk_ref[...] = jnp.zeros_like(dgk_ref)

        c, s = c_ref[...], s_ref[...]
        for j in range(ATT_Q_DIM // 128):
            sl = slice(j * 128, (j + 1) * 128)
            dx, dg = one(q_ref[:, sl], gq_ref[...], c, s, dq_ref[:, sl])
            dqo_ref[:, sl] = dx.astype(dqo_ref.dtype)
            dgq_ref[:, sl] += dg
        dx, dg = one(k_ref[...], gk_ref[...], c, s, dk_ref[...])
        dko_ref[...] = dx.astype(dko_ref.dtype)
        dgk_ref[...] += dg

    vec = pl.BlockSpec((1, 128), lambda i: (0, 0))
    tab = pl.BlockSpec((tr, 128), lambda i: (i, 0))
    qrow = pl.BlockSpec((tr, ATT_Q_DIM), lambda i: (i, 0))
    return pl.pallas_call(
        body, name=name, grid=(n // tr,),
        in_specs=[qrow, pl.BlockSpec((tr, 128), lambda i: (i, OFF_AK // 128)), vec, vec, tab, tab, qrow, tab],
        out_specs=[qrow, tab, pl.BlockSpec((1, ATT_Q_DIM), lambda i: (0, 0)), vec],
        out_shape=[SDS((n, ATT_Q_DIM), MXU_DTYPE), SDS((n, ATT_KV_DIM), MXU_DTYPE), SDS((1, ATT_Q_DIM), F32), SDS((1, 128), F32)],
        compiler_params=_cp("arbitrary"))(p, p, gq, gk, cos, sin, dq, dk)


ATT_FWD_STEP = (256, 4)
ATT_BWD_STEP = (512, 2)


def _attn_fwd(q, k, v, name):
    n = q.shape[0]
    tq, step_heads = min(ATT_FWD_STEP[0], n), ATT_FWD_STEP[1]
    scale = ATT_HEAD_DIM ** -0.5
    gw = step_heads * ATT_HEAD_DIM
    parts = ATT_GROUP // step_heads

    def body(q_ref, k_ref, v_ref, o_ref):
        kk, vv = k_ref[0], v_ref[0]
        v_ones = jnp.concatenate([vv, jnp.ones_like(vv)], axis=1)
        outs = []
        for g in range(step_heads):
            s = _dot(q_ref[:, g * ATT_HEAD_DIM:(g + 1) * ATT_HEAD_DIM] * scale, kk, "nt")
            e = jnp.exp(s - jnp.max(s, axis=-1, keepdims=True))
            ov = _dot(e, v_ones)
            outs.append(ov[:, :ATT_HEAD_DIM] / ov[:, ATT_HEAD_DIM:])
        o_ref[...] = jnp.concatenate(outs, axis=-1).astype(o_ref.dtype)

    kv = pl.BlockSpec((1, n, ATT_HEAD_DIM), lambda h, i, pr: (h, 0, 0))
    qb = pl.BlockSpec((tq, gw), lambda h, i, pr: (i, h * parts + pr))
    return pl.pallas_call(
        body, name=name, grid=(ATT_KV_HEADS, n // tq, parts), in_specs=[qb, kv, kv],
        out_specs=qb, out_shape=SDS((n, ATT_Q_DIM), MXU_DTYPE),
        compiler_params=_cp("parallel", "parallel", "parallel"))(q, k, v)


def _attn_bwd(q, k, v, o, do, name):
    n = q.shape[0]
    tq, step_heads = min(ATT_BWD_STEP[0], n), ATT_BWD_STEP[1]
    scale = ATT_HEAD_DIM ** -0.5
    gw = step_heads * ATT_HEAD_DIM
    parts = ATT_GROUP // step_heads

    def body(q_ref, k_ref, v_ref, o_ref, do_ref, dq_ref, dk_ref, dv_ref):
        @pl.when(jnp.logical_and(pl.program_id(1) == 0, pl.program_id(2) == 0))
        def _():
            dk_ref[...] = jnp.zeros_like(dk_ref)
            dv_ref[...] = jnp.zeros_like(dv_ref)

        kk, vv = k_ref[0], v_ref[0]
        dqs = []
        dk_acc = jnp.zeros((ATT_HEAD_DIM, n), F32)
        dv_acc = jnp.zeros((ATT_HEAD_DIM, n), F32)
        for g in range(step_heads):
            sl = slice(g * ATT_HEAD_DIM, (g + 1) * ATT_HEAD_DIM)
            qg, dog = q_ref[:, sl] * scale, do_ref[:, sl].astype(F32)
            s = _dot(qg, kk, "nt")
            e = jnp.exp(s - jnp.max(s, axis=-1, keepdims=True))
            inv = 1.0 / jnp.sum(e, axis=-1, keepdims=True)
            delta = jnp.sum(dog * o_ref[:, sl].astype(F32), axis=-1, keepdims=True)
            dse = e * (_dot(dog, vv, "nt") - delta)
            dqs.append(_dot(dse, kk) * (inv * scale))
            dk_acc += _dot(qg.astype(F32) * inv, dse, "tn")
            dv_acc += _dot(dog * inv, e, "tn")
        dq_ref[...] = jnp.concatenate(dqs, axis=-1)
        dk_ref[0] += dk_acc
        dv_ref[0] += dv_acc

    kv = pl.BlockSpec((1, n, ATT_HEAD_DIM), lambda h, i, pr: (h, 0, 0))
    kvt = pl.BlockSpec((1, ATT_HEAD_DIM, n), lambda h, i, pr: (h, 0, 0))
    qb = pl.BlockSpec((tq, gw), lambda h, i, pr: (i, h * parts + pr))
    return pl.pallas_call(
        body, name=name, grid=(ATT_KV_HEADS, n // tq, parts), in_specs=[qb, kv, kv, qb, qb], out_specs=[qb, kvt, kvt],
        out_shape=[SDS((n, ATT_Q_DIM), F32), SDS((ATT_KV_HEADS, ATT_HEAD_DIM, n), F32), SDS((ATT_KV_HEADS, ATT_HEAD_DIM, n), F32)],
        compiler_params=_cp("parallel", "arbitrary", "arbitrary"))(q, k, v, o, do)


def _both_directions(mats, axis):
    fwd = np.concatenate(mats, axis=axis).astype(np.float32)
    bwd = np.concatenate([m[::-1, ::-1] for m in mats], axis=axis).astype(np.float32)
    return jnp.asarray(np.stack([fwd, bwd]), MXU_DTYPE)


def _hg_segments():
    c = HG_CHUNK
    t = np.arange(c)[:, None]
    r = np.arange(c)[None, :]
    mats = [(r <= t)]
    for lev in range(HG_LEVELS):
        h = c >> (lev + 1)
        mid = (t // (2 * h)) * (2 * h) + h - 1
        hi = (t // h) % 2 == 1
        mats.append(np.where(hi, (r > mid) & (r <= t), (r > t) & (r <= mid)))
    mats.append(r > t)
    return _both_directions(mats, 0)


def _hg_pair_sums():
    c = HG_CHUNK
    r = np.arange(c)[:, None]
    t = np.arange(c)[None, :]
    gp, gn = [t >= r], [t < r]
    for lev in range(HG_LEVELS):
        sh = HG_LEVELS - 1 - lev
        same = (r >> sh) == (t >> sh)
        gp.append(same & (t >= r))
        gn.append(same & (t < r))
    return _both_directions(gp, 1), _both_directions(gn, 1)


def _split_dot(mat, x):
    hi = x.astype(MXU_DTYPE)
    lo = (x - hi.astype(F32)).astype(MXU_DTYPE)
    return _dot(mat, hi) + _dot(mat, lo)


def _hg_gates(hq, z, a0, a1):
    q = hq * _sigmoid(hq)
    sg = _sigmoid(z)
    lb = _sigmoid(a0 - a1)
    f = lb + (1.0 - lb) * sg
    k = (1.0 - lb) * (1.0 - sg)
    return q, f, k, sg, lb


def _hg_level_masks():
    c = HG_CHUNK
    t = np.arange(c)
    later, same = [], []
    for lev in range(HG_LEVELS):
        sh = HG_LEVELS - 1 - lev
        later.append(np.broadcast_to((((t >> sh) & 1) == 1)[:, None], (c, HG_HEAD_DIM)))
        same.append((t[:, None] >> (sh + 1)) == (t[None, :] >> (sh + 1)))
    same.append(t[:, None] == t[None, :])
    later = np.stack(later).astype(np.float32)
    return jnp.asarray(np.stack([later, 1.0 - later]), F32), jnp.asarray(np.stack(same).astype(np.float32), F32)


def _hg_level(q, k, ex, later_ref, lev):
    e = ex[lev + 1]
    e_q = e * later_ref[0, lev]
    e_k = e - e_q
    return q * e_q, k * e_k, e_q, e_k


def _hg_intra(q, k, ex, later_ref, same_ref):
    a = same_ref[HG_LEVELS] * jnp.sum(q * k, axis=-1, keepdims=True)
    for lev in range(HG_LEVELS):
        qs, ks, _, _ = _hg_level(q, k, ex, later_ref, lev)
        a = a + same_ref[lev] * _dot(qs, ks, "nt")
    return a


def _hg_specs(n, with_time):
    c = HG_CHUNK
    nc = n // c

    def chunk(d, i):
        first = d if with_time else 1 - d
        return i + first * (nc - 1 - 2 * i)

    def pcols(off, dir_stride=0):
        return [pl.BlockSpec((c, HG_PAIR), lambda d, i, j=j: (chunk(d, i), off // HG_PAIR + dir_stride // HG_PAIR * d + j)) for j in range(2)]

    specs = dict(
        hq=pcols(OFF_HQ), v=pcols(OFF_HI), z=pcols(OFF_ZF, OFF_ZB - OFF_ZF),
        shared=pl.BlockSpec((c, HG_DIM), lambda d, i: (chunk(d, i), 0)),
        per_dir=pl.BlockSpec((1, c, HG_DIM), lambda d, i: (d, chunk(d, i), 0)),
        vec=pl.BlockSpec((1, 1, HG_DIM), lambda d, i: (d, 0, 0)),
        seg=pl.BlockSpec((1, (HG_LEVELS + 2) * c, c), lambda d, i: (d, 0, 0)),
        sums=pl.BlockSpec((1, c, (HG_LEVELS + 1) * c), lambda d, i: (d, 0, 0)),
        later=pl.BlockSpec((1, HG_LEVELS, c, HG_HEAD_DIM), lambda d, i: (d, 0, 0, 0)),
        same=pl.BlockSpec((HG_LEVELS + 1, c, c), lambda d, i: (0, 0, 0)),
        state=pl.BlockSpec((1, HG_HEADS, 1, HG_HEAD_DIM, HG_HEAD_DIM), lambda d, i: (d, 0, chunk(d, i), 0, 0)),
        weights=pl.BlockSpec((1, HG_HEADS, 1, c, c), lambda d, i: (d, 0, chunk(d, i), 0, 0)),
        levels=pl.BlockSpec((1, HG_HEADS, 1, HG_LEVELS, c, HG_HEAD_DIM), lambda d, i: (d, 0, chunk(d, i), 0, 0, 0)),
        kept=pl.BlockSpec((1, HG_KEPT, c, HG_DIM), lambda d, i: (d, 0, chunk(d, i), 0)))
    return nc, specs


def _hg_head(refs, hh):
    off = (hh % 2) * HG_HEAD_DIM
    return refs[hh // 2][:, off:off + HG_HEAD_DIM]


def _hg_lanes(hh):
    return slice(hh * HG_HEAD_DIM, (hh + 1) * HG_HEAD_DIM)


def _hg_exps(seg_ref, f):
    c = HG_CHUNK
    args = _split_dot(seg_ref[0], jnp.log(f))
    return [jnp.exp(args[j * c:(j + 1) * c]) for j in range(HG_LEVELS + 2)]


def _hg_last_row(a, mirrored):
    return jnp.where(mirrored, a[0:1, :], a[HG_CHUNK - 1:HG_CHUNK, :])


def _hgrn_fwd(p, a0, a1, seg, masks, name):
    n = p.shape[0]
    nc, sp = _hg_specs(n, True)

    def body(hq0, hq1, z0, z1, v0, v1, a0_ref, a1_ref, seg_ref, later_ref, same_ref, o_ref, s0_ref, a_ref, e_ref, g_ref, st):
        @pl.when(pl.program_id(1) == 0)
        def _():
            st[...] = jnp.zeros_like(st)

        mirrored = pl.program_id(0) == 1
        for hh in range(HG_HEADS):
            ln = _hg_lanes(hh)
            hqv = _hg_head((hq0, hq1), hh)
            q, f, k, sg, _ = _hg_gates(hqv, _hg_head((z0, z1), hh), a0_ref[0, :, ln], a1_ref[0, :, ln])
            vv = _hg_head((v0, v1), hh)
            ex = _hg_exps(seg_ref, f)
            for lev in range(HG_LEVELS):
                e_ref[0, hh, 0, lev] = ex[lev + 1].astype(e_ref.dtype)
            sq = _sigmoid(hqv)
            for j, kept in enumerate((q, k, f, sg, sq * (1.0 + hqv * (1.0 - sq)), ex[0], ex[HG_LEVELS + 1])):
                g_ref[0, j, :, ln] = kept
            a = _hg_intra(q, k, ex, later_ref, same_ref).astype(MXU_DTYPE)
            a_ref[0, hh, 0] = a
            s_t = st[hh]
            s0_ref[0, hh, 0] = s_t
            o_ref[0, :, ln] = _dot(a, vv) + _dot(q * ex[0], s_t, "nt")
            st[hh] = s_t * _hg_last_row(ex[0], mirrored) + _dot(vv, k * ex[HG_LEVELS + 1], "tn")

    return pl.pallas_call(
        body, name=name, grid=(2, nc), in_specs=sp["hq"] + sp["z"] + sp["v"] + [sp["vec"], sp["vec"], sp["seg"], sp["later"], sp["same"]],
        out_specs=[sp["per_dir"], sp["state"], sp["weights"], sp["levels"], sp["kept"]],
        out_shape=[SDS((2, n, HG_DIM), F32), SDS((2, HG_HEADS, nc, HG_HEAD_DIM, HG_HEAD_DIM), F32),
                   SDS((2, HG_HEADS, nc, HG_CHUNK, HG_CHUNK), MXU_DTYPE),
                   SDS((2, HG_HEADS, nc, HG_LEVELS, HG_CHUNK, HG_HEAD_DIM), MXU_DTYPE), SDS((2, HG_KEPT, n, HG_DIM), F32)],
        scratch_shapes=[pltpu.VMEM((HG_HEADS, HG_HEAD_DIM, HG_HEAD_DIM), F32)],
        compiler_params=_cp("parallel", "arbitrary"))(p, p, p, p, p, p, a0, a1, seg, *masks)


def _hgrn_bwd(p, a0, a1, masks, gp, gn, do, s0, a, e, kept, name):
    n = p.shape[0]
    nc, sp = _hg_specs(n, False)


    def body(v0, v1, a0_ref, a1_ref, later_ref, same_ref, gp_ref, gn_ref, do_ref, s0_ref, a_ref, e_ref, g_ref,
             dhq_ref, dz_ref, dv_ref, dlb_ref, rt):
        @pl.when(pl.program_id(1) == 0)
        def _():
            rt[...] = jnp.zeros_like(rt)
            dlb_ref[...] = jnp.zeros_like(dlb_ref)

        mirrored = pl.program_id(0) == 1
        for hh in range(HG_HEADS):
            ln = _hg_lanes(hh)
            q, k, f, sg, dsilu, e_first, e_last = (g_ref[0, j, :, ln] for j in range(HG_KEPT))
            lb = _sigmoid(a0_ref[0, :, ln] - a1_ref[0, :, ln])
            vv, dov = _hg_head((v0, v1), hh), do_ref[:, ln]
            ex = [e_first] + [e_ref[0, hh, 0, lev].astype(F32) for lev in range(HG_LEVELS)] + [e_last]
            a = a_ref[0, hh, 0]
            da = _dot(dov, vv, "nt")
            diag = jnp.sum(dov * vv, axis=-1, keepdims=True)
            s_t = s0_ref[0, hh, 0]
            r_t = rt[hh]
            k_end = k * ex[HG_LEVELS + 1]
            dv_ref[0, :, ln] = _dot(a, dov, "tn") + _dot(k_end, r_t, "nt")
            dq_inter = ex[0] * _dot(dov, s_t)
            dk_inter = ex[HG_LEVELS + 1] * _dot(vv, r_t)
            dq = diag * k + dq_inter
            dk = diag * q + dk_inter
            q_terms, k_terms = [q * dq_inter], [k * dk_inter]
            for lev in range(HG_LEVELS):
                qs, ks, e_q, e_k = _hg_level(q, k, ex, later_ref, lev)
                pairs = da * same_ref[lev]
                q_part = e_q * _dot(pairs, ks)
                k_part = e_k * _dot(pairs, qs, "tn")
                dq, dk = dq + q_part, dk + k_part
                q_terms.append(q * q_part)
                k_terms.append(k * k_part)
            decay = _hg_last_row(ex[0], mirrored)
            rt[hh] = r_t * decay + _dot(dov, q * ex[0], "tn")
            later = decay * jnp.sum(s_t * r_t, axis=0, keepdims=True)
            dlf = _dot(gp_ref[0], jnp.concatenate(q_terms, axis=0)) + _dot(gn_ref[0], jnp.concatenate(k_terms, axis=0)) + later
            df = dlf / f - dk
            dz_ref[0, :, ln] = df * (1.0 - lb) * sg * (1.0 - sg)
            dlb_ref[0, :, ln] += jnp.sum(df * (1.0 - sg), axis=0, keepdims=True)
            dhq_ref[0, :, ln] = dq * dsilu

    out = SDS((2, n, HG_DIM), F32)
    return pl.pallas_call(
        body, name=name, grid=(2, nc),
        in_specs=sp["v"] + [sp["vec"], sp["vec"], sp["later"], sp["same"], sp["sums"], sp["sums"],
                            sp["shared"], sp["state"], sp["weights"], sp["levels"], sp["kept"]],
        out_specs=[sp["per_dir"], sp["per_dir"], sp["per_dir"], sp["vec"]], out_shape=[out, out, out, SDS((2, 1, HG_DIM), F32)],
        scratch_shapes=[pltpu.VMEM((HG_HEADS, HG_HEAD_DIM, HG_HEAD_DIM), F32)],
        compiler_params=_cp("parallel", "arbitrary"))(p, p, a0, a1, *masks, gp, gn, do, s0, a, e, kept)


def _hg_post(o2, p, g, name):
    n = p.shape[0]
    tr = min(ROW_TILE, n)
    w = 2 * HG_HEAD_DIM

    def body(of_ref, ob_ref, hg_ref, g_ref, o_ref):
        for j in range(2):
            sl = slice(j * HG_HEAD_DIM, (j + 1) * HG_HEAD_DIM)
            o = of_ref[0, :, sl] + ob_ref[0, :, sl]
            hg = hg_ref[:, sl]
            o_ref[:, sl] = (o * _rstd(o) * g_ref[...] * (hg * _sigmoid(hg))).astype(o_ref.dtype)

    blk = pl.BlockSpec((tr, w), lambda i, j: (i, j))
    dirs = [pl.BlockSpec((1, tr, w), lambda i, j, d=d: (d, i, j)) for d in range(2)]
    return pl.pallas_call(
        body, name=name, grid=(n // tr, HG_DIM // w),
        in_specs=dirs + [pl.BlockSpec((tr, w), lambda i, j: (i, OFF_HG // w + j)), pl.BlockSpec((1, HG_HEAD_DIM), lambda i, j: (0, 0))],
        out_specs=blk, out_shape=SDS((n, HG_DIM), MXU_DTYPE), compiler_params=_cp("parallel", "parallel"))(o2, o2, p, g)


def _hg_post_bwd(o2, p, g, dcat, name, after=()):
    n = p.shape[0]
    tr = min(ROW_TILE, n)
    w = 2 * HG_HEAD_DIM
    after, after_specs = _unread(after)

    def body(of_ref, ob_ref, hg_ref, g_ref, d_ref, *rest):
        do_ref, dhg_ref, dg_ref = rest[len(after):]

        @pl.when(pl.program_id(1) == 0)
        def _():
            dg_ref[...] = jnp.zeros_like(dg_ref)

        for j in range(2):
            sl = slice(j * HG_HEAD_DIM, (j + 1) * HG_HEAD_DIM)
            o = of_ref[0, :, sl] + ob_ref[0, :, sl]
            hg = hg_ref[:, sl]
            d = d_ref[:, sl].astype(F32)
            sg = _sigmoid(hg)
            on = o * _rstd(o) * g_ref[...]
            dhg_ref[:, sl] = (d * on * sg * (1.0 + hg * (1.0 - sg))).astype(dhg_ref.dtype)
            dx, dg = _rms_bwd(o, g_ref[...], d * hg * sg)
            do_ref[:, sl] = dx
            dg_ref[0, :, sl] += dg

    blk = pl.BlockSpec((tr, w), lambda j, i: (i, j))
    dirs = [pl.BlockSpec((1, tr, w), lambda j, i, d=d: (d, i, j)) for d in range(2)]
    return pl.pallas_call(
        body, name=name, grid=(HG_DIM // w, n // tr),
        in_specs=dirs + [pl.BlockSpec((tr, w), lambda j, i: (i, OFF_HG // w + j)), pl.BlockSpec((1, HG_HEAD_DIM), lambda j, i: (0, 0)),
                         pl.BlockSpec((tr, w), lambda j, i: (i, ATT_Q_DIM // w + j))] + after_specs,
        out_specs=[blk, blk, pl.BlockSpec((1, 1, w), lambda j, i: (j, 0, 0))],
        out_shape=[SDS((n, HG_DIM), F32), SDS((n, HG_DIM), MXU_DTYPE), SDS((HG_DIM // w, 1, w), F32)],
        compiler_params=_cp("parallel", "arbitrary"))(o2, o2, p, g, dcat, *after)


XATT_TQ = 512


def _xattn_fwd(q, kv, name):
    n, nm = q.shape[0], kv.shape[0]
    tq = min(XATT_TQ, n)
    scale = X_HEAD_DIM ** -0.5

    def body(q_ref, k_ref, v_ref, o_ref):
        s = _dot(q_ref[...], k_ref[...], "nt") * scale
        e = jnp.exp(s - jnp.max(s, axis=-1, keepdims=True))
        o_ref[...] = _dot(e / jnp.sum(e, axis=-1, keepdims=True), v_ref[...]).astype(o_ref.dtype)

    qb = pl.BlockSpec((tq, X_HEAD_DIM), lambda h, i: (i, h))
    return pl.pallas_call(
        body, name=name, grid=(X_HEADS, n // tq),
        in_specs=[qb, pl.BlockSpec((nm, X_HEAD_DIM), lambda h, i: (0, h)), pl.BlockSpec((nm, X_HEAD_DIM), lambda h, i: (0, X_HEADS + h))],
        out_specs=qb, out_shape=SDS(q.shape, MXU_DTYPE), compiler_params=_cp("parallel", "parallel"))(q, kv, kv)


def _xattn_bwd(q, kv, do, name, after=()):
    n, nm = q.shape[0], kv.shape[0]
    tq = min(XATT_TQ, n)
    scale = X_HEAD_DIM ** -0.5
    after, after_specs = _unread(after)

    def body(q_ref, k_ref, v_ref, do_ref, *rest):
        dq_ref, dk_ref, dv_ref = rest[len(after):]

        @pl.when(pl.program_id(1) == 0)
        def _():
            dk_ref[...] = jnp.zeros_like(dk_ref)
            dv_ref[...] = jnp.zeros_like(dv_ref)

        qv, dov = q_ref[...], do_ref[...]
        s = _dot(qv, k_ref[...], "nt") * scale
        e = jnp.exp(s - jnp.max(s, axis=-1, keepdims=True))
        p = e / jnp.sum(e, axis=-1, keepdims=True)
        dp = _dot(dov, v_ref[...], "nt")
        ds = p * (dp - jnp.sum(p * dp, axis=-1, keepdims=True)) * scale
        dq_ref[...] = _dot(ds, k_ref[...]).astype(dq_ref.dtype)
        dk_ref[...] += _dot(ds, qv, "tn")
        dv_ref[...] += _dot(p, dov, "tn")

    qb = pl.BlockSpec((tq, X_HEAD_DIM), lambda h, i: (i, h))
    kb = pl.BlockSpec((nm, X_HEAD_DIM), lambda h, i: (0, h))
    return pl.pallas_call(
        body, name=name, grid=(X_HEADS, n // tq),
        in_specs=[qb, kb, pl.BlockSpec((nm, X_HEAD_DIM), lambda h, i: (0, X_HEADS + h)), qb] + after_specs, out_specs=[qb, kb, kb],
        out_shape=[SDS(q.shape, MXU_DTYPE), SDS((nm, X_HEADS * X_HEAD_DIM), F32), SDS((nm, X_HEADS * X_HEAD_DIM), F32)],
        compiler_params=_cp("parallel", "arbitrary"))(q, kv, kv, do, *after)


def _edge_rows(shape):
    row = lax.broadcasted_iota(jnp.int32, shape, 0)
    return row == 0, row == shape[0] - 1


def _shift_rows(u, down, edges):
    if down:
        return jnp.where(edges[0], 0.0, pltpu.roll(u, 1, axis=0))
    return jnp.where(edges[1], 0.0, pltpu.roll(u, u.shape[0] - 1, axis=0))


def _conv(u, w, b, edges):
    return b + _shift_rows(u, True, edges) * w[0:1, :] + u * w[1:2, :] + _shift_rows(u, False, edges) * w[2:3, :]


def _ff_specs(n):
    gate = lambda rows: pl.BlockSpec((rows, FF_COLS), lambda j: (0, j))
    val = lambda rows: pl.BlockSpec((rows, FF_COLS), lambda j: (0, FF_BLOCKS + j))
    return [gate(n), val(n), gate(3), val(3), gate(1), val(1)], gate


def _conv_gate(u, cw, cb, name):
    n = u.shape[0]
    ins, gate_blk = _ff_specs(n)

    def body(ug_ref, uv_ref, wg_ref, wv_ref, bg_ref, bv_ref, o_ref):
        edges = _edge_rows(ug_ref.shape)
        gate = _conv(ug_ref[...], wg_ref[...], bg_ref[...], edges)
        val = _conv(uv_ref[...], wv_ref[...], bv_ref[...], edges)
        o_ref[...] = (gate * _sigmoid(gate) * val).astype(o_ref.dtype)

    return pl.pallas_call(
        body, name=name, grid=(FF_BLOCKS,), in_specs=ins, out_specs=gate_blk(n), out_shape=SDS((n, D_FF), MXU_DTYPE),
        compiler_params=_cp("parallel"))(u, u, cw, cw, cb, cb)


def _conv_gate_bwd(u, cw, cb, da, name, after=()):
    n = u.shape[0]
    ins, gate_blk = _ff_specs(n)
    after, after_specs = _unread(after)

    def side(dacc, u, w, edges, du_ref, dw_ref, db_ref):
        nxt, prv = _shift_rows(dacc, False, edges), _shift_rows(dacc, True, edges)
        du_ref[...] = (nxt * w[0:1, :] + dacc * w[1:2, :] + prv * w[2:3, :]).astype(du_ref.dtype)
        db_ref[...] = jnp.sum(dacc, axis=0, keepdims=True)
        dw_ref[0:1, :] = jnp.sum(nxt * u, axis=0, keepdims=True)
        dw_ref[1:2, :] = jnp.sum(dacc * u, axis=0, keepdims=True)
        dw_ref[2:3, :] = jnp.sum(prv * u, axis=0, keepdims=True)

    def body(ug_ref, uv_ref, wg_ref, wv_ref, bg_ref, bv_ref, da_ref, *rest):
        dug_ref, duv_ref, dwg_ref, dwv_ref, dbg_ref, dbv_ref = rest[len(after):]
        ug, uv = ug_ref[...], uv_ref[...]
        edges = _edge_rows(ug.shape)
        gate = _conv(ug, wg_ref[...], bg_ref[...], edges)
        val = _conv(uv, wv_ref[...], bv_ref[...], edges)
        sg = _sigmoid(gate)
        dav = da_ref[...].astype(F32)
        side(dav * val * sg * (1.0 + gate * (1.0 - sg)), ug, wg_ref[...], edges, dug_ref, dwg_ref, dbg_ref)
        side(dav * gate * sg, uv, wv_ref[...], edges, duv_ref, dwv_ref, dbv_ref)

    return pl.pallas_call(
        body, name=name, grid=(FF_BLOCKS,), in_specs=ins + [gate_blk(n)] + after_specs,
        out_specs=[gate_blk(n), gate_blk(n), gate_blk(3), gate_blk(3), gate_blk(1), gate_blk(1)],
        out_shape=[SDS((n, D_FF), MXU_DTYPE)] * 2 + [SDS((3, D_FF), F32)] * 2 + [SDS((1, D_FF), F32)] * 2,
        compiler_params=_cp("parallel"))(u, u, cw, cw, cb, cb, da, *after)


def _adamw(w, g, m, v, name):
    r, c = w.shape[-2:]
    tr = _row_tile(r, ELEMENTWISE_ROWS)
    assert w.ndim == 2 or w.shape[:-2] == (1,), (name, w.shape)

    def body(w_ref, g_ref, m_ref, v_ref, d_ref, mo_ref, vo_ref, go_ref):
        gv = g_ref[...]
        go_ref[...] = gv
        mn = ADAM_B1 * m_ref[...] + (1.0 - ADAM_B1) * gv
        vn = ADAM_B2 * v_ref[...] + (1.0 - ADAM_B2) * gv * gv
        m_hat = mn / (1.0 - ADAM_B1 ** ADAM_STEP)
        v_hat = vn / (1.0 - ADAM_B2 ** ADAM_STEP)
        d_ref[...] = -ADAM_LR * (m_hat / (jnp.sqrt(v_hat) + ADAM_EPS) + ADAM_WD * w_ref[...])
        mo_ref[...] = mn
        vo_ref[...] = vn

    blk = pl.BlockSpec((tr, c), lambda i: (i, 0)) if w.ndim == 2 else pl.BlockSpec((1, tr, c), lambda i: (0, i, 0))
    out = SDS(w.shape, F32)
    return pl.pallas_call(body, name=name, grid=(r // tr,), in_specs=[blk] * 4, out_specs=[blk] * 4, out_shape=[out] * 4,
                          compiler_params=_cp("parallel"))(w, g, m, v)


ANY = pl.BlockSpec(memory_space=pl.ANY)


def _place():
    x, y, c = lax.axis_index("x"), lax.axis_index("y"), lax.axis_index("c")
    return x, y, c, [(1 - x, y), (x, 1 - y), (1 - x, 1 - y)]


def _gather_shards(shards, name):
    nt = len(shards)

    def body(*refs):
        ins, outs = refs[:nt], refs[nt:2 * nt]
        send, recv, fsend, frecv, osend, orecv = refs[2 * nt:]
        x, y, c, chips = _place()
        me = 2 * x + y

        def half(t, chip, cc):
            h = ins[t].shape[0] // 2
            return outs[t].at[chip, pl.ds(cc * h, h)]

        def ici(t, j):
            cx, cy = chips[j]
            h = ins[t].shape[0] // 2
            return pltpu.make_async_remote_copy(src_ref=ins[t].at[pl.ds(c * h, h)], dst_ref=half(t, me, c),
                                                send_sem=send.at[t, j], recv_sem=recv.at[t, j], device_id=(cx, cy, c), device_id_type=MESH)

        def landed(t, j):
            cx, cy = chips[j]
            blk = half(t, 2 * cx + cy, c)
            return pltpu.make_async_remote_copy(src_ref=blk, dst_ref=blk, send_sem=send.at[t, j], recv_sem=recv.at[t, j],
                                                device_id=(cx, cy, c), device_id_type=MESH)

        def d2d(t, j, cc):
            cx, cy = chips[j]
            blk = half(t, 2 * cx + cy, cc)
            return pltpu.make_async_remote_copy(src_ref=blk, dst_ref=blk, send_sem=fsend.at[t, j], recv_sem=frecv.at[t, j],
                                                device_id=(x, y, 1 - c), device_id_type=MESH)

        own = [pltpu.make_async_remote_copy(src_ref=ins[t], dst_ref=outs[t].at[me], send_sem=osend.at[t], recv_sem=orecv.at[t],
                                            device_id=(x, y, 1 - c), device_id_type=MESH) for t in range(nt)]
        for t in range(nt):
            for j in range(3):
                ici(t, j).start()
        for cp in own:
            cp.start()
        for t in range(nt):
            for j in range(3):
                landed(t, j).wait_recv()
                d2d(t, j, c).start()
        for t in range(nt):
            for j in range(3):
                d2d(t, j, 1 - c).wait_recv()
        for t in range(nt):
            for j in range(3):
                ici(t, j).wait_send()
                d2d(t, j, c).wait_send()
        for cp in own:
            cp.wait()

    return pl.pallas_call(
        body, name=name, in_specs=[ANY] * nt, out_specs=[ANY] * nt,
        out_shape=[SDS((4,) + s.shape, s.dtype) for s in shards],
        scratch_shapes=[pltpu.SemaphoreType.DMA((nt, 3))] * 4 + [pltpu.SemaphoreType.DMA((nt,))] * 2,
        compiler_params=pltpu.CompilerParams(has_side_effects=True))(*shards)


def _join_halves(bufs, name):
    nt = len(bufs)

    def body(*refs):
        outs = refs[nt:2 * nt]
        send, recv = refs[2 * nt:]
        x, y, c, _ = _place()
        cps = [pltpu.make_async_remote_copy(src_ref=outs[t].at[c], dst_ref=outs[t].at[c], send_sem=send.at[t], recv_sem=recv.at[t],
                                            device_id=(x, y, 1 - c), device_id_type=MESH) for t in range(nt)]
        for cp in cps:
            cp.start()
        for t in range(nt):
            theirs = outs[t].at[1 - c]
            pltpu.make_async_remote_copy(src_ref=theirs, dst_ref=theirs, send_sem=send.at[t], recv_sem=recv.at[t],
                                         device_id=(x, y, 1 - c), device_id_type=MESH).wait_recv()
        for cp in cps:
            cp.wait_send()

    return pl.pallas_call(
        body, name=name, in_specs=[ANY] * nt, out_specs=[ANY] * nt, out_shape=[SDS(b.shape, b.dtype) for b in bufs],
        input_output_aliases={t: t for t in range(nt)},
        scratch_shapes=[pltpu.SemaphoreType.DMA((nt,))] * 2,
        compiler_params=pltpu.CompilerParams(has_side_effects=True))(*bufs)


def _exchange_small(v, reduce, name, after=()):
    rows = v.shape[0]
    after, after_specs = _unread(after)

    def body(v_ref, *rest):
        o_ref, buf, send, recv = rest[-4:]
        x, y, c, _ = _place()
        me = 4 * x + 2 * y + c
        buf[me] = v_ref[...]

        def peer(dx, dy, dc):
            return (1 - x if dx else x, 1 - y if dy else y, 1 - c if dc else c)

        peers = [(dx, dy, dc) for dx in range(2) for dy in range(2) for dc in range(2) if (dx, dy, dc) != (0, 0, 0)]
        cps = []
        for j, (dx, dy, dc) in enumerate(peers):
            cps.append(pltpu.make_async_remote_copy(src_ref=v_ref, dst_ref=buf.at[me], send_sem=send.at[j], recv_sem=recv.at[j],
                                                    device_id=peer(dx, dy, dc), device_id_type=MESH))
        for cp in cps:
            cp.start()
        for j, (dx, dy, dc) in enumerate(peers):
            px, py, pc = peer(dx, dy, dc)
            blk = buf.at[4 * px + 2 * py + pc]
            pltpu.make_async_remote_copy(src_ref=blk, dst_ref=blk, send_sem=send.at[j], recv_sem=recv.at[j],
                                         device_id=(px, py, pc), device_id_type=MESH).wait_recv()
        for cp in cps:
            cp.wait_send()
        if reduce:
            acc = buf[0]
            for j in range(1, 8):
                acc = acc + buf[j]
            o_ref[...] = acc
        else:
            o_ref[...] = buf[...]

    vm = pl.BlockSpec(memory_space=pltpu.VMEM)
    return pl.pallas_call(
        body, name=name, in_specs=[vm] + after_specs, out_specs=vm, out_shape=SDS((rows, 128) if reduce else (8, rows, 128), F32),
        scratch_shapes=[pltpu.VMEM((8, rows, 128), F32), pltpu.SemaphoreType.DMA((7,)), pltpu.SemaphoreType.DMA((7,))],
        compiler_params=pltpu.CompilerParams(has_side_effects=True))(v, *after)


HBM = pl.BlockSpec(memory_space=pltpu.HBM)
SEM = pl.BlockSpec(memory_space=pltpu.SEMAPHORE)
TOKEN = pl.BlockSpec(memory_space=pltpu.VMEM)
TOKEN_SHAPE = SDS((8, 128), F32)
PEERS = 7


def _in_hbm(a):
    return pltpu.with_memory_space_constraint(a, pltpu.HBM)


def _split_params():
    return pltpu.CompilerParams(has_side_effects=pltpu.SideEffectType.DATAFLOW_SIDE_EFFECTING)


def _gather_start(shards, name, after=()):
    nt = len(shards)
    after, after_specs = _unread(after)

    def body(*refs):
        ins, lands = refs[:nt], refs[nt:2 * nt]
        outs = refs[2 * nt + len(after):]
        sends, recvs = outs[:nt], outs[nt:2 * nt]
        x, y, c, chips = _place()
        me = 2 * x + y
        for t in range(nt):
            h = ins[t].shape[0] // 2
            mine = pl.ds(c * h, h)
            for j, (cx, cy) in enumerate(chips):
                for dc in range(2):
                    pltpu.make_async_remote_copy(src_ref=ins[t].at[mine], dst_ref=lands[t].at[me, mine], send_sem=sends[t].at[2 * j + dc],
                                                 recv_sem=recvs[t].at[2 * j + c], device_id=(cx, cy, dc), device_id_type=MESH).start()
            pltpu.make_async_remote_copy(src_ref=ins[t], dst_ref=lands[t].at[me], send_sem=sends[t].at[PEERS - 1], recv_sem=recvs[t].at[PEERS - 1],
                                         device_id=(x, y, 1 - c), device_id_type=MESH).start()
        outs[-1][...] = jnp.zeros(TOKEN_SHAPE.shape, F32)

    lands = [lax.empty((4,) + s.shape, s.dtype) for s in shards]
    out = pl.pallas_call(
        body, name=name, in_specs=[HBM] * (2 * nt) + after_specs, out_specs=[SEM] * (2 * nt) + [HBM] * (2 * nt) + [TOKEN],
        out_shape=[pltpu.SemaphoreType.DMA((PEERS,))] * (2 * nt)
        + [pltpu.HBM(s.shape, s.dtype) for s in shards] + [pltpu.HBM(l.shape, l.dtype) for l in lands] + [TOKEN_SHAPE],
        input_output_aliases={t: 2 * nt + t for t in range(2 * nt)}, compiler_params=_split_params())(
            *[_in_hbm(s) for s in shards], *[_in_hbm(l) for l in lands], *after)
    return out[:nt], out[nt:2 * nt], out[2 * nt:3 * nt], out[3 * nt:4 * nt], out[-1]


def _gather_wait(sends, recvs, shards, lands, after, name):
    nt = len(shards)

    def body(*refs):
        ins, lands_ref = refs[:nt], refs[nt:2 * nt]
        send_refs, recv_refs = refs[2 * nt:3 * nt], refs[3 * nt:4 * nt]
        x, y, c, chips = _place()
        for t in range(nt):
            h = ins[t].shape[0] // 2
            for j, (cx, cy) in enumerate(chips):
                for cs in range(2):
                    blk = lands_ref[t].at[2 * cx + cy, pl.ds(cs * h, h)]
                    pltpu.make_async_remote_copy(src_ref=blk, dst_ref=blk, send_sem=send_refs[t].at[2 * j + cs], recv_sem=recv_refs[t].at[2 * j + cs],
                                                 device_id=(cx, cy, cs), device_id_type=MESH).wait()
            blk = lands_ref[t].at[2 * x + y]
            pltpu.make_async_remote_copy(src_ref=blk, dst_ref=blk, send_sem=send_refs[t].at[PEERS - 1], recv_sem=recv_refs[t].at[PEERS - 1],
                                         device_id=(x, y, 1 - c), device_id_type=MESH).wait()

    out = pl.pallas_call(
        body, name=name, in_specs=[HBM] * (2 * nt) + [SEM] * (2 * nt) + [ANY], out_specs=[HBM] * (2 * nt),
        out_shape=[pltpu.HBM(s.shape, s.dtype) for s in shards] + [pltpu.HBM(l.shape, l.dtype) for l in lands],
        input_output_aliases={t: t for t in range(2 * nt)}, compiler_params=_split_params())(*shards, *lands, *sends, *recvs, after)
    return out[nt:]


def _scatter_start(g, name):
    _, r, c_ = g.shape
    h = r // 2

    def body(g_ref, land, send, recv, g_thru, land_thru, token):
        x, y, c, chips = _place()
        for j, (cx, cy) in enumerate(chips):
            for dc in range(2):
                pltpu.make_async_remote_copy(src_ref=g_ref.at[2 * cx + cy, pl.ds(dc * h, h)], dst_ref=land.at[2 * j + c], send_sem=send.at[2 * j + dc],
                                             recv_sem=recv.at[2 * j + c], device_id=(cx, cy, dc), device_id_type=MESH).start()
        pltpu.make_async_remote_copy(src_ref=g_ref.at[2 * x + y, pl.ds((1 - c) * h, h)], dst_ref=land.at[PEERS - 1], send_sem=send.at[PEERS - 1],
                                     recv_sem=recv.at[PEERS - 1], device_id=(x, y, 1 - c), device_id_type=MESH).start()
        token[...] = jnp.zeros(TOKEN_SHAPE.shape, F32)

    land = lax.empty((PEERS, h, c_), g.dtype)
    return pl.pallas_call(
        body, name=name, in_specs=[HBM, HBM], out_specs=[SEM, SEM, HBM, HBM, TOKEN],
        out_shape=[pltpu.SemaphoreType.DMA((PEERS,)), pltpu.SemaphoreType.DMA((PEERS,)), pltpu.HBM(g.shape, g.dtype),
                   pltpu.HBM(land.shape, land.dtype), TOKEN_SHAPE],
        input_output_aliases={0: 2, 1: 3}, compiler_params=_split_params())(_in_hbm(g), _in_hbm(land))


def _scatter_wait(started, after, name):
    nt = len(started)

    def body(*refs):
        lands = refs[nt:2 * nt]
        sends, recvs = refs[2 * nt:3 * nt], refs[3 * nt:4 * nt]
        x, y, c, chips = _place()
        peers = [(cx, cy, dc) for cx, cy in chips for dc in range(2)] + [(x, y, 1 - c)]
        for t in range(nt):
            for k, peer in enumerate(peers):
                blk = lands[t].at[k]
                pltpu.make_async_remote_copy(src_ref=blk, dst_ref=blk, send_sem=sends[t].at[k], recv_sem=recvs[t].at[k],
                                             device_id=peer, device_id_type=MESH).wait()

    gs, lands = [s[2] for s in started], [s[3] for s in started]
    after, after_specs = _unread(after)
    out = pl.pallas_call(
        body, name=name, in_specs=[HBM] * (2 * nt) + [SEM] * (2 * nt) + after_specs, out_specs=[HBM] * (2 * nt),
        out_shape=[pltpu.HBM(a.shape, a.dtype) for a in gs + lands],
        input_output_aliases={t: t for t in range(2 * nt)}, compiler_params=_split_params())(
            *gs, *lands, *[s[0] for s in started], *[s[1] for s in started], *after)
    return out[:nt], out[nt:]


def _flips():
    return [(dx, dy, dc) for dx in range(2) for dy in range(2) for dc in range(2) if (dx, dy, dc) != (0, 0, 0)]


def _flipped(x, y, c, flips):
    dx, dy, dc = flips
    return (1 - x if dx else x, 1 - y if dy else y, 1 - c if dc else c)


def _small_start(v, name, after=()):
    after, after_specs = _unread(after)

    def body(v_ref, land, *rest):
        send, recv = rest[len(after):len(after) + 2]
        x, y, c, _ = _place()
        for j, flips in enumerate(_flips()):
            pltpu.make_async_remote_copy(src_ref=v_ref, dst_ref=land.at[4 * x + 2 * y + c], send_sem=send.at[j], recv_sem=recv.at[j],
                                         device_id=_flipped(x, y, c, flips), device_id_type=MESH).start()
        rest[-1][...] = jnp.zeros(TOKEN_SHAPE.shape, F32)

    land = lax.empty((8,) + v.shape, v.dtype)
    return pl.pallas_call(
        body, name=name, in_specs=[HBM, HBM] + after_specs, out_specs=[SEM, SEM, HBM, HBM, TOKEN],
        out_shape=[pltpu.SemaphoreType.DMA((PEERS,)), pltpu.SemaphoreType.DMA((PEERS,)), pltpu.HBM(v.shape, v.dtype),
                   pltpu.HBM(land.shape, land.dtype), TOKEN_SHAPE],
        input_output_aliases={0: 2, 1: 3}, compiler_params=_split_params())(_in_hbm(v), _in_hbm(land), *after)


def _small_wait(send, recv, v, land, after, name):
    after, after_specs = _unread(after)

    def body(v_ref, land_ref, send_ref, recv_ref, *rest):
        x, y, c, _ = _place()
        for j, flips in enumerate(_flips()):
            px, py, pc = _flipped(x, y, c, flips)
            blk = land_ref.at[4 * px + 2 * py + pc]
            pltpu.make_async_remote_copy(src_ref=blk, dst_ref=blk, send_sem=send_ref.at[j], recv_sem=recv_ref.at[j],
                                         device_id=(px, py, pc), device_id_type=MESH).wait()

    return pl.pallas_call(
        body, name=name, in_specs=[HBM, HBM, SEM, SEM] + after_specs, out_specs=[HBM, HBM],
        out_shape=[pltpu.HBM(v.shape, v.dtype), pltpu.HBM(land.shape, land.dtype)],
        input_output_aliases={0: 0, 1: 1}, compiler_params=_split_params())(v, land, send, recv, *after)


def _sum_small(v, land, name):
    def body(v_ref, land_ref, o_ref):
        x, y, c, _ = _place()
        me = 4 * x + 2 * y + c
        acc = jnp.where(me == 0, v_ref[...], land_ref[0])
        for d in range(1, 8):
            acc = acc + jnp.where(me == d, v_ref[...], land_ref[d])
        o_ref[...] = acc

    vm = pl.BlockSpec(memory_space=pltpu.VMEM)
    return pl.pallas_call(body, name=name, in_specs=[vm, vm], out_specs=vm, out_shape=SDS(v.shape, F32))(v, land)


def _sum_devices(g, land, me, core, name):
    npeer, h, c = land.shape
    tr = _row_tile(h, 2 * ELEMENTWISE_ROWS)
    steps = h // tr

    def body(ix_ref, own_ref, land_ref, o_ref):
        acc = own_ref[0].astype(F32)
        for j in range(npeer):
            acc = acc + land_ref[j].astype(F32)
        o_ref[0] = acc

    grid_spec = pltpu.PrefetchScalarGridSpec(
        num_scalar_prefetch=1, grid=(steps,),
        in_specs=[pl.BlockSpec((1, tr, c), lambda i, ix: (ix[0], ix[1] * steps + i, 0)), pl.BlockSpec((npeer, tr, c), lambda i, ix: (0, i, 0))],
        out_specs=pl.BlockSpec((1, tr, c), lambda i, ix: (ix[1], i, 0)))
    return pl.pallas_call(body, name=name, grid_spec=grid_spec, out_shape=SDS((2, h, c), F32),
                          compiler_params=_cp("parallel"))(jnp.stack([me, core]), g, land)


def _pack_small(parts):
    flat = jnp.concatenate([p.reshape(-1) for p in parts])
    total = flat.shape[0]
    rows = -(-total // 1024) * 8
    return jnp.pad(flat, (0, rows * 128 - total)).reshape(rows, 128)


def _unpack_small(packed, shapes):
    flat = packed.reshape(-1)
    out, off = [], 0
    for s in shapes:
        size = int(np.prod(s))
        out.append(flat[off:off + size].reshape(s))
        off += size
    return out


def _local_step(x, mem, target, w_in, first_after, mid_weights, ffn_weights, on_grad, gains, conv_w, conv_b, hg_lb):
    n = x.shape[0]
    cos, sin = _rope_tables(n)
    seg = _hg_segments()
    gp, gn = _hg_pair_sums()
    masks = _hg_level_masks()
    gq2 = jnp.tile(gains["q_norm_g"], (1, 2))
    gk2 = jnp.tile(gains["k_norm_g"], (1, 2))
    a0 = hg_lb[:, 0:1, :]
    a1 = hg_lb[:, 1:2, :]

    p, h1 = _norm_mm(x, gains["pre_mix_g"], w_in, F32, TOKEN_TILE, 1664, "in_proj", after=(first_after,))
    qr, kr = _qk_prep(p, gq2, gk2, cos, sin, "qk_prep")
    heads = lambda a: a.reshape(n, ATT_KV_HEADS, ATT_HEAD_DIM).transpose(1, 0, 2)
    kh = heads(kr)
    vh = heads(p[:, OFF_AV:OFF_AV + ATT_KV_DIM].astype(MXU_DTYPE))
    att = _attn_fwd(qr, kh, vh, "attn_fwd")
    o2, s0, hg_a, hg_e, hg_kept = _hgrn_fwd(p, a0, a1, seg, masks, "hgrn_fwd")
    rec = _hg_post(o2, p, gains["hg_out_norm_g"], "hg_post")
    cat = jnp.concatenate([att, rec], axis=1)
    w_out, w_xq, w_xkv, w_xo = mid_weights(cat)
    mixed, x1 = _mm_resid_norm(cat, w_out, x, gains["post_mix_g"], 512, "out_proj_resid")
    xq, h2 = _norm_mm(x1, gains["pre_x_g"], w_xq, MXU_DTYPE, TOKEN_TILE, 1024, "xq_proj")
    kv, mn = _norm_mm(mem, gains["mem_norm_g"], w_xkv, MXU_DTYPE, 256, 2048, "xkv_proj")
    ox = _xattn_fwd(xq, kv, "xattn_fwd")
    xo, x2 = _mm_resid_norm(ox, w_xo, x1, gains["post_x_g"], 512, "xo_proj_resid")
    w_up = ffn_weights("w_up", x2)
    u, h3 = _norm_mm(x2, gains["pre_ffn_g"], w_up, F32, TOKEN_TILE, 1408, "up_proj")
    act = _conv_gate(u, conv_w, conv_b, "conv_gate")
    w_down = ffn_weights("w_down", act)
    dn, d3, loss = _mm_resid_norm(act, w_down, x2, gains["post_ffn_g"], 512, "down_proj_resid_loss", target=target)

    gs = {}
    d_act, d_dn, gs["post_ffn_g"] = _norm_bwd_mm(dn, gains["post_ffn_g"], d3, w_down, F32, 512, 1408, "ffn_post_bwd_down_dx")
    tok = on_grad("w_down", _mm(act, d_dn, "tn", WIRE_DTYPE, 1408, 1024, "down_dw"))
    du_g, du_v, dcw_g, dcw_v, dcb_g, dcb_v = _conv_gate_bwd(u, conv_w, conv_b, d_act, "conv_gate_bwd", after=(tok,))
    gs["conv_w"] = jnp.concatenate([dcw_g, dcw_v], axis=1)
    gs["conv_b"] = jnp.concatenate([dcb_g, dcb_v], axis=1)
    ff_shard = w_up.shape[2]
    g_up = _dw_by_owner(h3, du_g, ff_shard, 0, None, 512, "up_dw_gate")
    tok = on_grad("w_up", _dw_by_owner(h3, du_v, ff_shard, 2, g_up, 512, "up_dw_value"))
    d2, gs["pre_ffn_g"] = _dx_norm_bwd([(du_g, 0), (du_g, 1), (du_v, 0), (du_v, 1)], w_up, x2, gains["pre_ffn_g"], d3, 512,
                                       "up_dx_pre_bwd", after=(tok,))
    d_ox, d_xo, gs["post_x_g"] = _norm_bwd_mm(xo, gains["post_x_g"], d2, w_xo, MXU_DTYPE, 512, 1024, "x_post_bwd_xo_dx")
    tok = on_grad("w_xo", _mm(ox, d_xo, "tn", WIRE_DTYPE, 512, 1024, "xo_dw"))
    d_xq, d_k, d_v = _xattn_bwd(xq, kv, d_ox, "xattn_bwd", after=(tok,))
    d_kv = jnp.concatenate([d_k, d_v], axis=1).astype(MXU_DTYPE)
    tok = on_grad("w_xq", _mm(h2, d_xq, "tn", WIRE_DTYPE, 512, 1024, "xq_dw"))
    tok_kv = on_grad("w_xkv", _dw_by_owner(mn, d_kv, w_xkv.shape[2], 0, None, 512, "xkv_dw"))
    d1, gs["pre_x_g"] = _dx_norm_bwd([(d_xq, 0)], w_xq[None], x1, gains["pre_x_g"], d2, 512, "xq_dx_pre_bwd", after=(tok, tok_kv))
    d_mn = _mm_nt_parts([(d_kv, s) for s in range(4)], w_xkv, F32, 256, 1024, "xkv_dx")
    _, gs["mem_norm_g"] = _norm_bwd(mem, gains["mem_norm_g"], d_mn, None, MXU_DTYPE, "mem_norm_bwd")
    d_cat, d_mixed, gs["post_mix_g"] = _norm_bwd_mm(mixed, gains["post_mix_g"], d1, w_out, MXU_DTYPE, 512, 1024, "mix_post_bwd_out_dx")
    tok = on_grad("w_out", _mm(cat, d_mixed, "tn", WIRE_DTYPE, 512, 1024, "out_dw"))
    d_o, d_hg, dg_hg = _hg_post_bwd(o2, p, gains["hg_out_norm_g"], d_cat, "hg_post_bwd", after=(tok,))
    gs["hg_out_norm_g"] = dg_hg.reshape(HG_HEADS, HG_HEAD_DIM).sum(axis=0, keepdims=True)
    dhq2, dz2, dhv2, dlb = _hgrn_bwd(p, a0, a1, masks, gp, gn, d_o, s0, hg_a, hg_e, hg_kept, "hgrn_bwd")
    lb = jax.nn.sigmoid(a0 - a1)
    da0 = dlb * lb * (1.0 - lb)
    gs["hg_lb"] = jnp.concatenate([da0, -da0], axis=1)
    d_qr, d_kh, d_vh = _attn_bwd(qr, kh, vh, cat, d_cat, "attn_bwd")
    unheads = lambda a: a.transpose(2, 0, 1).reshape(n, ATT_KV_DIM)
    d_aq, d_ak, dgq, dgk = _qk_prep_bwd(p, gq2, gk2, cos, sin, d_qr, unheads(d_kh), "qk_prep_bwd")
    gs["q_norm_g"] = dgq.reshape(ATT_HEADS, ATT_HEAD_DIM).sum(axis=0, keepdims=True)
    gs["k_norm_g"] = dgk.reshape(ATT_KV_HEADS, ATT_HEAD_DIM).sum(axis=0, keepdims=True)
    d_p = jnp.concatenate([d_aq, d_ak, unheads(d_vh).astype(MXU_DTYPE), (dhq2[0] + dhq2[1]).astype(MXU_DTYPE),
                           dz2[0].astype(MXU_DTYPE), dz2[1].astype(MXU_DTYPE), (dhv2[0] + dhv2[1]).astype(MXU_DTYPE), d_hg], axis=1)
    tok = on_grad("w_in", _mm(h1, d_p, "tn", WIRE_DTYPE, 512, 1664, "in_dw"))
    grad_x, gs["pre_mix_g"] = _dx_norm_bwd([(d_p, 0)], w_in[None], x, gains["pre_mix_g"], d1, 512, "in_dx_pre_bwd", after=(tok,))
    return loss, grad_x, gs


MATS = ("w_in", "w_out", "w_xq", "w_xkv", "w_xo", "w_up", "w_down")
GAINS = ("pre_mix_g", "q_norm_g", "k_norm_g", "hg_out_norm_g", "post_mix_g", "pre_x_g", "mem_norm_g", "post_x_g", "pre_ffn_g", "post_ffn_g")
WEIGHTS = ('pre_mix_g', 'w_in', 'q_norm_g', 'k_norm_g', 'hg_lb', 'hg_out_norm_g', 'w_out', 'post_mix_g', 'pre_x_g', 'mem_norm_g', 'w_xq',
           'w_xkv', 'w_xo', 'post_x_g', 'pre_ffn_g', 'w_up', 'conv_w', 'conv_b', 'w_down', 'post_ffn_g')


def kernel(x, mem, pre_mix_g, w_in, q_norm_g, k_norm_g, hg_lb, hg_out_norm_g, w_out, post_mix_g, pre_x_g, mem_norm_g, w_xq, w_xkv, w_xo, post_x_g, pre_ffn_g, w_up, conv_w, conv_b, w_down, post_ffn_g, loss_target, m_pre_mix_g, m_w_in, m_q_norm_g, m_k_norm_g, m_hg_lb, m_hg_out_norm_g, m_w_out, m_post_mix_g, m_pre_x_g, m_mem_norm_g, m_w_xq, m_w_xkv, m_w_xo, m_post_x_g, m_pre_ffn_g, m_w_up, m_conv_w, m_conv_b, m_w_down, m_post_ffn_g, v_pre_mix_g, v_w_in, v_q_norm_g, v_k_norm_g, v_hg_lb, v_hg_out_norm_g, v_w_out, v_post_mix_g, v_pre_x_g, v_mem_norm_g, v_w_xq, v_w_xkv, v_w_xo, v_post_x_g, v_pre_ffn_g, v_w_up, v_conv_w, v_conv_b, v_w_down, v_post_ffn_g):
    args = dict(locals())
    w = {k: args[k] for k in WEIGHTS}
    m = {k: args["m_" + k] for k in WEIGHTS}
    v = {k: args["v_" + k] for k in WEIGHTS}
    chip = 2 * lax.axis_index("x") + lax.axis_index("y")
    core = lax.axis_index("c")

    shards = {k: w[k][0].astype(WIRE_DTYPE) for k in MATS}

    def whole(k, g):
        return g if k in ("w_xkv", "w_up") else g.reshape(-1, g.shape[-1])

    w_in_shards = _gather_shards([shards["w_in"]], "gather_w_in")[0]
    w_in_full = jnp.concatenate([w_in_shards[s] for s in range(4)], axis=1)
    small_in = _exchange_small(_pack_small([w["conv_w"][0], w["hg_lb"]]), False, "gather_small")
    mid_names, ffn_names = ("w_out", "w_xq", "w_xkv", "w_xo"), ("w_up", "w_down")
    mid = _gather_start([shards[k] for k in mid_names], "gather_mid_start", after=(w_in_full, small_in))
    ffn = _gather_start([shards[k] for k in ffn_names], "gather_ffn_start", after=(mid[4],))

    def mid_weights(after):
        return [whole(k, g) for k, g in zip(mid_names, _gather_wait(*mid[:4], after, "gather_mid_wait"))]

    def ffn_weights(k, after):
        t = ffn_names.index(k)
        return whole(k, _gather_wait(*[part[t:t + 1] for part in ffn[:4]], after, "gather_wait_" + k)[0])

    cw_parts, lb_parts = [], []
    for s in range(4):
        cw_s, lb_s = _unpack_small(small_in[2 * s], [w["conv_w"][0].shape, w["hg_lb"].shape])
        cw_parts.append(cw_s)
        lb_parts.append(lb_s)
    conv_w_full = jnp.concatenate(cw_parts, axis=1)
    hg_lb_full = jnp.concatenate(lb_parts, axis=2)

    started = {}

    def on_grad(k, g):
        if k == "w_in":
            g = g.reshape(g.shape[0], 4, g.shape[1] // 4).transpose(1, 0, 2)
        elif g.ndim == 2:
            g = g.reshape(4, g.shape[0] // 4, g.shape[1])
        *started[k], token = _scatter_start(g, "grad_start_" + k)
        return token

    gains = {k: w[k] for k in GAINS}
    loss_part, grad_x, gs = _local_step(x[0], mem[0], loss_target[0], w_in_full, ffn[4], mid_weights, ffn_weights, on_grad, gains,
                                        conv_w_full, w["conv_b"], hg_lb_full)
    gs["loss"] = loss_part

    grads, delta, new_m, new_v = {}, {}, {}, {}

    def reduce_matrices(names, after, tag):
        sent, landed = _scatter_wait([started[k] for k in names], after, "grad_wait_" + tag)
        halves = [_sum_devices(g, land, chip, core, "grad_sum_" + k) for k, g, land in zip(names, sent, landed)]
        for k, r in zip(names, _join_halves(halves, "grad_join_" + tag)):
            grads[k] = r.reshape(1, -1, r.shape[-1])

    def adamw(names):
        for k in names:
            shape = w[k].shape
            keep = len(shape) == 3 and shape[0] == 1
            turned = keep and shape[2] % LANES != 0 and shape[1] % LANES == 0
            if turned:
                view, back = (lambda a: jnp.swapaxes(a.reshape(shape), 1, 2)), (lambda a: jnp.swapaxes(a, 1, 2))
            elif keep:
                view, back = (lambda a: a.reshape(shape)), (lambda a: a)
            else:
                view, back = (lambda a: a.reshape(-1, shape[-1])), (lambda a: a.reshape(shape))
            d, mo, vo, go = _adamw(view(w[k]), view(grads[k]), view(m[k]), view(v[k]), "adamw_" + k)
            delta[k], new_m[k], new_v[k], grads[k] = back(d), back(mo), back(vo), back(go)

    small_names = GAINS + ("conv_b", "conv_w", "hg_lb")
    packed = _pack_small([gs[k] for k in small_names + ("loss",)])
    small = _small_start(packed, "reduce_small_start", after=(grad_x,))

    early = tuple(k for k in MATS if k != "w_in")
    reduce_matrices(early, (grad_x, small[4]), "early")
    adamw(early)

    mine, others = _small_wait(*small[:4], tuple(new_v[k] for k in early), "reduce_small_wait")
    reduced_small = _sum_small(mine, others, "reduce_small_sum")
    *summed, loss = _unpack_small(reduced_small, [gs[k].shape for k in small_names + ("loss",)])
    loss = loss[0, 0]
    for k, g in zip(small_names, summed):
        grads[k] = g
    ncw = w["conv_w"].shape[2]
    grads["conv_w"] = lax.dynamic_slice_in_dim(grads["conv_w"], chip * ncw, ncw, axis=1)[None]
    nlb = w["hg_lb"].shape[2]
    grads["hg_lb"] = lax.dynamic_slice_in_dim(grads["hg_lb"], chip * nlb, nlb, axis=2)
    replicated = GAINS + ("conv_b",)
    shapes = [w[k].shape for k in replicated]
    rows = sum(int(np.prod(s)) for s in shapes) // 128
    pack = lambda d: jnp.concatenate([d[k].reshape(-1) for k in replicated]).reshape(rows, 128)
    outs = _adamw(pack(w), reduced_small[:rows], pack(m), pack(v), "adamw_replicated")
    for into, packed_out in zip((delta, new_m, new_v, grads), outs):
        for k, a in zip(replicated, _unpack_small(packed_out, shapes)):
            into[k] = a
    adamw(("conv_w", "hg_lb"))

    reduce_matrices(("w_in",), tuple(new_v[k] for k in early + small_names), "late")
    adamw(("w_in",))
    return (loss, grad_x[None], *[grads[k] for k in WEIGHTS], *[delta[k] for k in WEIGHTS],
            *[new_m[k] for k in WEIGHTS], *[new_v[k] for k in WEIGHTS])
```

```python
import numpy as np
import jax
import jax.numpy as jnp
from jax import lax
from jax.experimental import pallas as pl
from jax.experimental.pallas import tpu as pltpu

F32 = jnp.float32
MXU_DTYPE = jnp.bfloat16
WIRE_DTYPE = jnp.bfloat16
VMEM_LIMIT_BYTES = 56 * 1024 * 1024
ROWS_PER_16BIT_TILE = 16
ELEMENTWISE_ROWS = 256
EPS = 1e-6
MESH = pl.DeviceIdType.MESH

GRID_W = 64
ATT_HEADS, ATT_KV_HEADS, ATT_HEAD_DIM = 8, 2, 64
ATT_GROUP = ATT_HEADS // ATT_KV_HEADS
ATT_Q_DIM, ATT_KV_DIM = 512, 128
ROPE_THETA = 10000.0
HG_HEADS, HG_HEAD_DIM, HG_DIM = 4, 128, 512
HG_CHUNK = 128
HG_LEVELS = 7
HG_PAIR = 2 * HG_HEAD_DIM
HG_KEPT = 7
X_HEADS, X_HEAD_DIM = 4, 256
D_FF = 2816
FF_COLS = 256
FF_BLOCKS = D_FF // FF_COLS
OFF_AK, OFF_AV, OFF_HQ, OFF_ZF, OFF_ZB, OFF_HI, OFF_HG = 512, 640, 768, 1280, 1792, 2304, 2816

ADAM_LR, ADAM_B1, ADAM_B2, ADAM_EPS, ADAM_WD, ADAM_STEP = 0.001, 0.9, 0.999, 1e-08, 0.01, 10

SDS = jax.ShapeDtypeStruct


def _cp(*sem):
    return pltpu.CompilerParams(dimension_semantics=sem, vmem_limit_bytes=VMEM_LIMIT_BYTES)


def _row_tile(rows, cap):
    if rows <= cap:
        return rows
    return max(t for t in range(ROWS_PER_16BIT_TILE, cap + 1, ROWS_PER_16BIT_TILE) if rows % t == 0)


def _dot(a, b, form="nn"):
    dims = {"nn": (((1,), (0,)), ((), ())), "nt": (((1,), (1,)), ((), ())), "tn": (((0,), (0,)), ((), ()))}[form]
    return lax.dot_general(a.astype(MXU_DTYPE), b.astype(MXU_DTYPE), dims, preferred_element_type=F32)


def _sigmoid(x):
    return 1.0 / (1.0 + jnp.exp(-x))


def _rstd(x):
    return lax.rsqrt(jnp.mean(x * x, axis=-1, keepdims=True) + EPS)


def _rms_bwd(x, g, dy):
    r = _rstd(x)
    xh = x * r
    dn = dy * g
    dx = r * (dn - xh * jnp.mean(dn * xh, axis=-1, keepdims=True))
    return dx, jnp.sum(dy * xh, axis=0, keepdims=True)


def _unread(after):
    after = tuple(a for a in after if a is not None)
    return after, [pl.BlockSpec(memory_space=pl.ANY)] * len(after)


def _mm(a, b, form, out_dtype, tm, tn, name, after=()):
    after, after_specs = _unread(after)
    if form == "nn":
        (m, k), n = a.shape, b.shape[1]
    elif form == "nt":
        (m, k), n = a.shape, b.shape[0]
    else:
        (k, m), n = a.shape, b.shape[1]
    tm, tn = min(tm, m), min(tn, n)
    assert m % tm == 0 and n % tn == 0, (name, m, n, tm, tn)

    def body(a_ref, b_ref, *rest):
        o_ref = rest[-1]
        o_ref[...] = _dot(a_ref[...], b_ref[...], form).astype(o_ref.dtype)

    a_spec = pl.BlockSpec((k, tm), lambda i, j: (0, i)) if form == "tn" else pl.BlockSpec((tm, k), lambda i, j: (i, 0))
    b_spec = pl.BlockSpec((tn, k), lambda i, j: (j, 0)) if form == "nt" else pl.BlockSpec((k, tn), lambda i, j: (0, j))
    return pl.pallas_call(
        body, name=name, grid=(m // tm, n // tn), in_specs=[a_spec, b_spec] + after_specs,
        out_specs=pl.BlockSpec((tm, tn), lambda i, j: (i, j)), out_shape=SDS((m, n), out_dtype),
        compiler_params=_cp("parallel", "parallel"))(a, b, *after)


def _mm_nt_parts(a_parts, b, out_dtype, tm, tn, name, after=()):
    after, after_specs = _unread(after)
    parts, n, p = b.shape
    m = a_parts[0][0].shape[0]
    tm, tn = min(tm, m), min(tn, n)
    assert m % tm == 0 and n % tn == 0 and len(a_parts) == parts, (name, m, b.shape)

    def body(*refs):
        o_ref = refs[-1]
        acc = _dot(refs[0][...], refs[parts][0], "nt")
        for s in range(1, parts):
            acc = acc + _dot(refs[s][...], refs[parts + s][0], "nt")
        o_ref[...] = acc.astype(o_ref.dtype)

    a_specs = [pl.BlockSpec((tm, p), lambda i, j, cb=cb: (i, cb)) for _, cb in a_parts]
    b_specs = [pl.BlockSpec((1, tn, p), lambda i, j, s=s: (s, j, 0)) for s in range(parts)]
    return pl.pallas_call(
        body, name=name, grid=(m // tm, n // tn), in_specs=a_specs + b_specs + after_specs,
        out_specs=pl.BlockSpec((tm, tn), lambda i, j: (i, j)), out_shape=SDS((m, n), out_dtype),
        compiler_params=_cp("parallel", "parallel"))(*[arr for arr, _ in a_parts], *([b] * parts), *after)


def _norm_bwd_mm(y, g, d, w, out_dtype, tm, tn, name):
    n, dm = y.shape
    nn = w.shape[0]
    tm, tn = min(tm, n), min(tn, nn)
    assert n % tm == 0 and nn % tn == 0 and w.shape[1] == dm, (name, y.shape, w.shape)

    def body(y_ref, g_ref, d_ref, w_ref, dx_ref, dy_ref, dg_ref, dys):
        i, j = pl.program_id(0), pl.program_id(1)

        @pl.when(jnp.logical_and(i == 0, j == 0))
        def _():
            dg_ref[...] = jnp.zeros_like(dg_ref)

        @pl.when(j == 0)
        def _():
            dy, dg = _rms_bwd(y_ref[...], g_ref[...], d_ref[...])
            dy = dy.astype(MXU_DTYPE)
            dys[...] = dy
            dy_ref[...] = dy
            dg_ref[...] += dg

        dx_ref[...] = _dot(dys[...], w_ref[...], "nt").astype(dx_ref.dtype)

    row = pl.BlockSpec((tm, dm), lambda i, j: (i, 0))
    vec = pl.BlockSpec((1, dm), lambda i, j: (0, 0))
    return pl.pallas_call(
        body, name=name, grid=(n // tm, nn // tn), in_specs=[row, vec, row, pl.BlockSpec((tn, dm), lambda i, j: (j, 0))],
        out_specs=[pl.BlockSpec((tm, tn), lambda i, j: (i, j)), row, vec],
        out_shape=[SDS((n, nn), out_dtype), SDS((n, dm), MXU_DTYPE), SDS((1, dm), F32)],
        scratch_shapes=[pltpu.VMEM((tm, dm), MXU_DTYPE)],
        compiler_params=_cp("arbitrary", "arbitrary"))(y, g, d, w)


def _mm_resid_norm(a, b, x, g, tm, name, target=None):
    n, k = a.shape
    d = b.shape[1]
    tm = min(tm, n)
    assert n % tm == 0 and x.shape == (n, d), (name, a.shape, b.shape)
    with_loss = target is not None

    def body(a_ref, b_ref, x_ref, g_ref, *rest):
        y = _dot(a_ref[...], b_ref[...])
        out = x_ref[...] + y * _rstd(y) * g_ref[...]
        if not with_loss:
            y_ref, o_ref = rest
            y_ref[...] = y
            o_ref[...] = out
            return
        t_ref, y_ref, d_ref, l_ref = rest
        y_ref[...] = y
        diff = out - t_ref[...]
        d_ref[...] = diff * (1.0 / d)

        @pl.when(pl.program_id(0) == 0)
        def _():
            l_ref[...] = jnp.zeros_like(l_ref)

        l_ref[...] += 0.5 * jnp.sum(jnp.mean(diff * diff, axis=-1, keepdims=True), axis=0, keepdims=True)

    row = pl.BlockSpec((tm, d), lambda i: (i, 0))
    ins = [pl.BlockSpec((tm, k), lambda i: (i, 0)), pl.BlockSpec((k, d), lambda i: (0, 0)), row, pl.BlockSpec((1, d), lambda i: (0, 0))]
    out = SDS((n, d), F32)
    if with_loss:
        return pl.pallas_call(body, name=name, grid=(n // tm,), in_specs=ins + [row], out_specs=[row, row, pl.BlockSpec((1, 1), lambda i: (0, 0))],
                              out_shape=[out, out, SDS((1, 1), F32)], compiler_params=_cp("arbitrary"))(a, b, x, g, target)
    return pl.pallas_call(body, name=name, grid=(n // tm,), in_specs=ins, out_specs=[row, row], out_shape=[out, out],
                          compiler_params=_cp("parallel"))(a, b, x, g)


def _dx_norm_bwd(a_parts, b, x, g, res, tm, name, after=(), b_turned=False):
    after, after_specs = _unread(after)
    parts, d, p = (b.shape[0], b.shape[2], b.shape[1]) if b_turned else b.shape
    form = "nn" if b_turned else "nt"
    n = x.shape[0]
    tm = min(tm, n)
    assert n % tm == 0 and len(a_parts) == parts and x.shape[1] == d, (name, x.shape, b.shape)

    def body(*refs):
        x_ref, g_ref, res_ref = refs[2 * parts:2 * parts + 3]
        dx_ref, dg_ref = refs[-2:]
        dh = _dot(refs[0][...], refs[parts][0], form)
        for s in range(1, parts):
            dh = dh + _dot(refs[s][...], refs[parts + s][0], form)
        dx, dg = _rms_bwd(x_ref[...], g_ref[...], dh)
        dx_ref[...] = dx + res_ref[...]

        @pl.when(pl.program_id(0) == 0)
        def _():
            dg_ref[...] = jnp.zeros_like(dg_ref)

        dg_ref[...] += dg

    a_specs = [pl.BlockSpec((tm, p), lambda i, cb=cb: (i, cb)) for _, cb in a_parts]
    b_specs = [pl.BlockSpec((1,) + b.shape[1:], lambda i, s=s: (s, 0, 0)) for s in range(parts)]
    row = pl.BlockSpec((tm, d), lambda i: (i, 0))
    vec = pl.BlockSpec((1, d), lambda i: (0, 0))
    return pl.pallas_call(
        body, name=name, grid=(n // tm,), in_specs=a_specs + b_specs + [row, vec, row] + after_specs,
        out_specs=[row, vec], out_shape=[SDS((n, d), F32), SDS((1, d), F32)],
        compiler_params=_cp("arbitrary"))(*[arr for arr, _ in a_parts], *([b] * parts), x, g, res, *after)


def _dw_by_owner(a, b, tn, first, into, tm, name):
    k, m = a.shape
    cnt = b.shape[1] // tn
    tm = min(tm, m)
    assert m % tm == 0 and b.shape[1] == cnt * tn and first + cnt <= 4, (name, a.shape, b.shape)

    def body(a_ref, b_ref, *rest):
        rest[-1][0] = _dot(a_ref[...], b_ref[...], "tn").astype(rest[-1].dtype)

    extra = [] if into is None else [into]
    return pl.pallas_call(
        body, name=name, grid=(m // tm, cnt),
        in_specs=[pl.BlockSpec((k, tm), lambda i, j: (0, i)), pl.BlockSpec((k, tn), lambda i, j: (0, j))] + [pl.BlockSpec(memory_space=pl.ANY)] * len(extra),
        out_specs=pl.BlockSpec((1, tm, tn), lambda i, j: (first + j, i, 0)), out_shape=SDS((4, m, tn), WIRE_DTYPE),
        input_output_aliases={2: 0} if extra else {},
        compiler_params=_cp("parallel", "parallel"))(a, b, *extra)


def _norm_mm(x, g, w, out_dtype, tm, tn, name, after=(), w_turned=False):
    after, after_specs = _unread(after)
    m, d = x.shape
    sharded = w.ndim == 3
    n = w.shape[0] if w_turned else w.shape[-1] * (w.shape[0] if sharded else 1)
    tm, tn = min(tm, m), (w.shape[-1] if sharded else min(tn, n))
    assert m % tm == 0 and n % tn == 0 and not (sharded and w_turned), (name, m, n, tm, tn)

    def body(x_ref, g_ref, w_ref, *rest):
        o_ref, h_ref, hs = rest[-3:]

        @pl.when(pl.program_id(1) == 0)
        def _():
            xv = x_ref[...]
            h = (xv * _rstd(xv) * g_ref[...]).astype(MXU_DTYPE)
            hs[...] = h
            h_ref[...] = h

        o_ref[...] = _dot(hs[...], w_ref[0] if sharded else w_ref[...], "nt" if w_turned else "nn").astype(o_ref.dtype)

    if w_turned:
        w_spec = pl.BlockSpec((tn, d), lambda i, j: (j, 0))
    else:
        w_spec = pl.BlockSpec((1, d, tn), lambda i, j: (j, 0, 0)) if sharded else pl.BlockSpec((d, tn), lambda i, j: (0, j))
    return pl.pallas_call(
        body, name=name, grid=(m // tm, n // tn),
        in_specs=[pl.BlockSpec((tm, d), lambda i, j: (i, 0)), pl.BlockSpec((1, d), lambda i, j: (0, 0)), w_spec] + after_specs,
        out_specs=[pl.BlockSpec((tm, tn), lambda i, j: (i, j)), pl.BlockSpec((tm, d), lambda i, j: (i, 0))],
        out_shape=[SDS((m, n), out_dtype), SDS((m, d), MXU_DTYPE)],
        scratch_shapes=[pltpu.VMEM((tm, d), MXU_DTYPE)],
        compiler_params=_cp("parallel", "arbitrary"))(x, g, w, *after)


ROW_TILE = 512
TOKEN_TILE = 1024


def _norm_bwd(x, g, dy, res, out_dtype, name):
    n, d = x.shape
    tr = min(ROW_TILE, n)
    has_res = res is not None

    def body(*refs):
        x_ref, g_ref, dy_ref = refs[:3]
        dx_ref, dg_ref = refs[-2:]
        dx, dg = _rms_bwd(x_ref[...], g_ref[...], dy_ref[...].astype(F32))
        if has_res:
            dx = dx + refs[3][...]
        dx_ref[...] = dx.astype(dx_ref.dtype)

        @pl.when(pl.program_id(0) == 0)
        def _():
            dg_ref[...] = jnp.zeros_like(dg_ref)

        dg_ref[...] += dg

    row = pl.BlockSpec((tr, d), lambda i: (i, 0))
    vec = pl.BlockSpec((1, d), lambda i: (0, 0))
    ins = [x, g, dy] + ([res] if has_res else [])
    return pl.pallas_call(
        body, name=name, grid=(n // tr,), in_specs=[row, vec, row] + ([row] if has_res else []),
        out_specs=[row, vec], out_shape=[SDS((n, d), out_dtype), SDS((1, d), F32)],
        compiler_params=_cp("arbitrary"))(*ins)


def _rope_tables(n):
    pairs = ATT_HEAD_DIM // 4
    t = np.arange(n)
    inv = np.power(ROPE_THETA, -np.arange(pairs, dtype=np.float32) / pairs).astype(np.float32)
    ang = np.concatenate([(t // GRID_W)[:, None].astype(np.float32) * inv, (t % GRID_W)[:, None].astype(np.float32) * inv], axis=-1)
    cos = np.repeat(np.cos(ang), 2, axis=-1)
    sin = np.repeat(np.sin(ang), 2, axis=-1) * np.tile(np.array([-1.0, 1.0], np.float32), ATT_HEAD_DIM // 2)
    return jnp.asarray(np.tile(cos, 2), F32), jnp.asarray(np.tile(sin, 2), F32)


def _swap_pairs(x):
    lane = lax.broadcasted_iota(jnp.int32, x.shape, 1)
    return jnp.where((lane & 1) == 0, pltpu.roll(x, 127, axis=1), pltpu.roll(x, 1, axis=1))


def _head_mean(v):
    lane = lax.broadcasted_iota(jnp.int32, v.shape, 1)
    lo = jnp.where(lane < ATT_HEAD_DIM, v, 0.0)
    s0 = jnp.sum(lo, axis=-1, keepdims=True)
    s1 = jnp.sum(v - lo, axis=-1, keepdims=True)
    return jnp.where(lane < ATT_HEAD_DIM, s0, s1) * (1.0 / ATT_HEAD_DIM)


def _qk_prep(p, gq, gk, cos, sin, name):
    n = p.shape[0]
    tr = min(ROW_TILE, n)

    def one(xv, g, c, s):
        xn = xv * lax.rsqrt(_head_mean(xv * xv) + EPS) * g
        return xn * c + _swap_pairs(xn) * s

    def body(q_ref, k_ref, gq_ref, gk_ref, c_ref, s_ref, qo_ref, ko_ref):
        c, s = c_ref[...], s_ref[...]
        for j in range(ATT_Q_DIM // 128):
            qo_ref[:, j * 128:(j + 1) * 128] = one(q_ref[:, j * 128:(j + 1) * 128], gq_ref[...], c, s).astype(qo_ref.dtype)
        ko_ref[...] = one(k_ref[...], gk_ref[...], c, s).astype(ko_ref.dtype)

    vec = pl.BlockSpec((1, 128), lambda i: (0, 0))
    tab = pl.BlockSpec((tr, 128), lambda i: (i, 0))
    return pl.pallas_call(
        body, name=name, grid=(n // tr,),
        in_specs=[pl.BlockSpec((tr, ATT_Q_DIM), lambda i: (i, 0)), pl.BlockSpec((tr, 128), lambda i: (i, OFF_AK // 128)), vec, vec, tab, tab],
        out_specs=[pl.BlockSpec((tr, ATT_Q_DIM), lambda i: (i, 0)), tab],
        out_shape=[SDS((n, ATT_Q_DIM), MXU_DTYPE), SDS((n, ATT_KV_DIM), MXU_DTYPE)],
        compiler_params=_cp("parallel"))(p, p, gq, gk, cos, sin)


def _qk_prep_bwd(p, gq, gk, cos, sin, dq, dk, name):
    n = p.shape[0]
    tr = min(ROW_TILE, n)

    def one(xv, g, c, s, dout):
        dxn = dout * c + _swap_pairs(dout * s)
        r = lax.rsqrt(_head_mean(xv * xv) + EPS)
        xh = xv * r
        dn = dxn * g
        dx = r * (dn - xh * _head_mean(dn * xh))
        return dx, jnp.sum(dxn * xh, axis=0, keepdims=True)

    def body(q_ref, k_ref, gq_ref, gk_ref, c_ref, s_ref, dq_ref, dk_ref, dqo_ref, dko_ref, dgq_ref, dgk_ref):
        @pl.when(pl.program_id(0) == 0)
        def _():
            dgq_ref[...] = jnp.zeros_like(dgq_ref)
            dgk_ref[...] = jnp.zeros_like(dgk_ref)

        c, s = c_ref[...], s_ref[...]
        for j in range(ATT_Q_DIM // 128):
            sl = slice(j * 128, (j + 1) * 128)
            dx, dg = one(q_ref[:, sl], gq_ref[...], c, s, dq_ref[:, sl])
            dqo_ref[:, sl] = dx.astype(dqo_ref.dtype)
            dgq_ref[:, sl] += dg
        dx, dg = one(k_ref[...], gk_ref[...], c, s, dk_ref[...])
        dko_ref[...] = dx.astype(dko_ref.dtype)
        dgk_ref[...] += dg

    vec = pl.BlockSpec((1, 128), lambda i: (0, 0))
    tab = pl.BlockSpec((tr, 128), lambda i: (i, 0))
    qrow = pl.BlockSpec((tr, ATT_Q_DIM), lambda i: (i, 0))
    return pl.pallas_call(
        body, name=name, grid=(n // tr,),
        in_specs=[qrow, pl.BlockSpec((tr, 128), lambda i: (i, OFF_AK // 128)), vec, vec, tab, tab, qrow, tab],
        out_specs=[qrow, tab, pl.BlockSpec((1, ATT_Q_DIM), lambda i: (0, 0)), vec],
        out_shape=[SDS((n, ATT_Q_DIM), MXU_DTYPE), SDS((n, ATT_KV_DIM), MXU_DTYPE), SDS((1, ATT_Q_DIM), F32), SDS((1, 128), F32)],
        compiler_params=_cp("arbitrary"))(p, p, gq, gk, cos, sin, dq, dk)


ATT_FWD_STEP = (256, 4)
ATT_BWD_STEP = (512, 2)


def _attn_fwd(q, k, v, name):
    n = q.shape[0]
    tq, step_heads = min(ATT_FWD_STEP[0], n), ATT_FWD_STEP[1]
    scale = ATT_HEAD_DIM ** -0.5
    gw = step_heads * ATT_HEAD_DIM
    parts = ATT_GROUP // step_heads

    def body(q_ref, k_ref, v_ref, o_ref):
        kk, vv = k_ref[0], v_ref[0]
        v_ones = jnp.concatenate([vv, jnp.ones_like(vv)], axis=1)
        outs = []
        for g in range(step_heads):
            s = _dot(q_ref[:, g * ATT_HEAD_DIM:(g + 1) * ATT_HEAD_DIM] * scale, kk, "nt")
            e = jnp.exp(s - jnp.max(s, axis=-1, keepdims=True))
            ov = _dot(e, v_ones)
            outs.append(ov[:, :ATT_HEAD_DIM] / ov[:, ATT_HEAD_DIM:])
        o_ref[...] = jnp.concatenate(outs, axis=-1).astype(o_ref.dtype)

    kv = pl.BlockSpec((1, n, ATT_HEAD_DIM), lambda h, i, pr: (h, 0, 0))
    qb = pl.BlockSpec((tq, gw), lambda h, i, pr: (i, h * parts + pr))
    return pl.pallas_call(
        body, name=name, grid=(ATT_KV_HEADS, n // tq, parts), in_specs=[qb, kv, kv],
        out_specs=qb, out_shape=SDS((n, ATT_Q_DIM), MXU_DTYPE),
        compiler_params=_cp("parallel", "parallel", "parallel"))(q, k, v)


def _attn_bwd(q, k, v, o, do, name):
    n = q.shape[0]
    tq, step_heads = min(ATT_BWD_STEP[0], n), ATT_BWD_STEP[1]
    scale = ATT_HEAD_DIM ** -0.5
    gw = step_heads * ATT_HEAD_DIM
    parts = ATT_GROUP // step_heads

    def body(q_ref, k_ref, v_ref, o_ref, do_ref, dq_ref, dk_ref, dv_ref):
        @pl.when(jnp.logical_and(pl.program_id(1) == 0, pl.program_id(2) == 0))
        def _():
            dk_ref[...] = jnp.zeros_like(dk_ref)
            dv_ref[...] = jnp.zeros_like(dv_ref)

        kk, vv = k_ref[0], v_ref[0]
        dqs = []
        dk_acc = jnp.zeros((ATT_HEAD_DIM, n), F32)
        dv_acc = jnp.zeros((ATT_HEAD_DIM, n), F32)
        for g in range(step_heads):
            sl = slice(g * ATT_HEAD_DIM, (g + 1) * ATT_HEAD_DIM)
            qg, dog = q_ref[:, sl] * scale, do_ref[:, sl].astype(F32)
            s = _dot(qg, kk, "nt")
            e = jnp.exp(s - jnp.max(s, axis=-1, keepdims=True))
            inv = 1.0 / jnp.sum(e, axis=-1, keepdims=True)
            delta = jnp.sum(dog * o_ref[:, sl].astype(F32), axis=-1, keepdims=True)
            dse = e * (_dot(dog, vv, "nt") - delta)
            dqs.append(_dot(dse, kk) * (inv * scale))
            dk_acc += _dot(qg.astype(F32) * inv, dse, "tn")
            dv_acc += _dot(dog * inv, e, "tn")
        dq_ref[...] = jnp.concatenate(dqs, axis=-1)
        dk_ref[0] += dk_acc
        dv_ref[0] += dv_acc

    kv = pl.BlockSpec((1, n, ATT_HEAD_DIM), lambda h, i, pr: (h, 0, 0))
    kvt = pl.BlockSpec((1, ATT_HEAD_DIM, n), lambda h, i, pr: (h, 0, 0))
    qb = pl.BlockSpec((tq, gw), lambda h, i, pr: (i, h * parts + pr))
    return pl.pallas_call(
        body, name=name, grid=(ATT_KV_HEADS, n // tq, parts), in_specs=[qb, kv, kv, qb, qb], out_specs=[qb, kvt, kvt],
        out_shape=[SDS((n, ATT_Q_DIM), F32), SDS((ATT_KV_HEADS, ATT_HEAD_DIM, n), F32), SDS((ATT_KV_HEADS, ATT_HEAD_DIM, n), F32)],
        compiler_params=_cp("parallel", "arbitrary", "arbitrary"))(q, k, v, o, do)


def _both_directions(mats, axis):
    fwd = np.concatenate(mats, axis=axis).astype(np.float32)
    bwd = np.concatenate([m[::-1, ::-1] for m in mats], axis=axis).astype(np.float32)
    return jnp.asarray(np.stack([fwd, bwd]), MXU_DTYPE)


def _hg_segments():
    c = HG_CHUNK
    t = np.arange(c)[:, None]
    r = np.arange(c)[None, :]
    mats = [(r <= t)]
    for lev in range(HG_LEVELS):
        h = c >> (lev + 1)
        mid = (t // (2 * h)) * (2 * h) + h - 1
        hi = (t // h) % 2 == 1
        mats.append(np.where(hi, (r > mid) & (r <= t), (r > t) & (r <= mid)))
    mats.append(r > t)
    return _both_directions(mats, 0)


def _hg_pair_sums():
    c = HG_CHUNK
    r = np.arange(c)[:, None]
    t = np.arange(c)[None, :]
    gp, gn = [t >= r], [t < r]
    for lev in range(HG_LEVELS):
        sh = HG_LEVELS - 1 - lev
        same = (r >> sh) == (t >> sh)
        gp.append(same & (t >= r))
        gn.append(same & (t < r))
    return _both_directions(gp, 1), _both_directions(gn, 1)


def _split_dot(mat, x):
    hi = x.astype(MXU_DTYPE)
    lo = (x - hi.astype(F32)).astype(MXU_DTYPE)
    return _dot(mat, hi) + _dot(mat, lo)


def _hg_gates(hq, z, a0, a1):
    q = hq * _sigmoid(hq)
    sg = _sigmoid(z)
    lb = _sigmoid(a0 - a1)
    f = lb + (1.0 - lb) * sg
    k = (1.0 - lb) * (1.0 - sg)
    return q, f, k, sg, lb


def _hg_level_masks():
    c = HG_CHUNK
    t = np.arange(c)
    later, same = [], []
    for lev in range(HG_LEVELS):
        sh = HG_LEVELS - 1 - lev
        later.append(np.broadcast_to((((t >> sh) & 1) == 1)[:, None], (c, HG_HEAD_DIM)))
        same.append((t[:, None] >> (sh + 1)) == (t[None, :] >> (sh + 1)))
    same.append(t[:, None] == t[None, :])
    later = np.stack(later).astype(np.float32)
    return jnp.asarray(np.stack([later, 1.0 - later]), F32), jnp.asarray(np.stack(same).astype(np.float32), F32)


def _hg_level(q, k, ex, later_ref, lev):
    e = ex[lev + 1]
    e_q = e * later_ref[0, lev]
    e_k = e - e_q
    return q * e_q, k * e_k, e_q, e_k


def _hg_intra(q, k, ex, later_ref, same_ref):
    a = same_ref[HG_LEVELS] * jnp.sum(q * k, axis=-1, keepdims=True)
    for lev in range(HG_LEVELS):
        qs, ks, _, _ = _hg_level(q, k, ex, later_ref, lev)
        a = a + same_ref[lev] * _dot(qs, ks, "nt")
    return a


def _hg_specs(n, with_time):
    c = HG_CHUNK
    nc = n // c

    def chunk(d, i):
        first = d if with_time else 1 - d
        return i + first * (nc - 1 - 2 * i)

    def pcols(off, dir_stride=0):
        return [pl.BlockSpec((c, HG_PAIR), lambda d, i, j=j: (chunk(d, i), off // HG_PAIR + dir_stride // HG_PAIR * d + j)) for j in range(2)]

    specs = dict(
        hq=pcols(OFF_HQ), v=pcols(OFF_HI), z=pcols(OFF_ZF, OFF_ZB - OFF_ZF),
        shared=pl.BlockSpec((c, HG_DIM), lambda d, i: (chunk(d, i), 0)),
        per_dir=pl.BlockSpec((1, c, HG_DIM), lambda d, i: (d, chunk(d, i), 0)),
        vec=pl.BlockSpec((1, 1, HG_DIM), lambda d, i: (d, 0, 0)),
        seg=pl.BlockSpec((1, (HG_LEVELS + 2) * c, c), lambda d, i: (d, 0, 0)),
        sums=pl.BlockSpec((1, c, (HG_LEVELS + 1) * c), lambda d, i: (d, 0, 0)),
        later=pl.BlockSpec((1, HG_LEVELS, c, HG_HEAD_DIM), lambda d, i: (d, 0, 0, 0)),
        same=pl.BlockSpec((HG_LEVELS + 1, c, c), lambda d, i: (0, 0, 0)),
        state=pl.BlockSpec((1, HG_HEADS, 1, HG_HEAD_DIM, HG_HEAD_DIM), lambda d, i: (d, 0, chunk(d, i), 0, 0)),
        weights=pl.BlockSpec((1, HG_HEADS, 1, c, c), lambda d, i: (d, 0, chunk(d, i), 0, 0)),
        levels=pl.BlockSpec((1, HG_HEADS, 1, HG_LEVELS, c, HG_HEAD_DIM), lambda d, i: (d, 0, chunk(d, i), 0, 0, 0)),
        kept=pl.BlockSpec((1, HG_KEPT, c, HG_DIM), lambda d, i: (d, 0, chunk(d, i), 0)))
    return nc, specs


def _hg_head(refs, hh):
    off = (hh % 2) * HG_HEAD_DIM
    return refs[hh // 2][:, off:off + HG_HEAD_DIM]


def _hg_lanes(hh):
    return slice(hh * HG_HEAD_DIM, (hh + 1) * HG_HEAD_DIM)


def _hg_exps(seg_ref, f):
    c = HG_CHUNK
    args = _split_dot(seg_ref[0], jnp.log(f))
    return [jnp.exp(args[j * c:(j + 1) * c]) for j in range(HG_LEVELS + 2)]


def _hg_last_row(a, mirrored):
    return jnp.where(mirrored, a[0:1, :], a[HG_CHUNK - 1:HG_CHUNK, :])


def _hgrn_fwd(p, a0, a1, seg, masks, name):
    n = p.shape[0]
    nc, sp = _hg_specs(n, True)

    def body(hq0, hq1, z0, z1, v0, v1, a0_ref, a1_ref, seg_ref, later_ref, same_ref, o_ref, s0_ref, a_ref, e_ref, g_ref, st):
        @pl.when(pl.program_id(1) == 0)
        def _():
            st[...] = jnp.zeros_like(st)

        mirrored = pl.program_id(0) == 1
        for hh in range(HG_HEADS):
            ln = _hg_lanes(hh)
            hqv = _hg_head((hq0, hq1), hh)
            q, f, k, sg, _ = _hg_gates(hqv, _hg_head((z0, z1), hh), a0_ref[0, :, ln], a1_ref[0, :, ln])
            vv = _hg_head((v0, v1), hh)
            ex = _hg_exps(seg_ref, f)
            for lev in range(HG_LEVELS):
                e_ref[0, hh, 0, lev] = ex[lev + 1].astype(e_ref.dtype)
            sq = _sigmoid(hqv)
            for j, kept in enumerate((q, k, f, sg, sq * (1.0 + hqv * (1.0 - sq)), ex[0], ex[HG_LEVELS + 1])):
                g_ref[0, j, :, ln] = kept
            a = _hg_intra(q, k, ex, later_ref, same_ref).astype(MXU_DTYPE)
            a_ref[0, hh, 0] = a
            s_t = st[hh]
            s0_ref[0, hh, 0] = s_t
            o_ref[0, :, ln] = _dot(a, vv) + _dot(q * ex[0], s_t, "nt")
            st[hh] = s_t * _hg_last_row(ex[0], mirrored) + _dot(vv, k * ex[HG_LEVELS + 1], "tn")

    return pl.pallas_call(
        body, name=name, grid=(2, nc), in_specs=sp["hq"] + sp["z"] + sp["v"] + [sp["vec"], sp["vec"], sp["seg"], sp["later"], sp["same"]],
        out_specs=[sp["per_dir"], sp["state"], sp["weights"], sp["levels"], sp["kept"]],
        out_shape=[SDS((2, n, HG_DIM), F32), SDS((2, HG_HEADS, nc, HG_HEAD_DIM, HG_HEAD_DIM), F32),
                   SDS((2, HG_HEADS, nc, HG_CHUNK, HG_CHUNK), MXU_DTYPE),
                   SDS((2, HG_HEADS, nc, HG_LEVELS, HG_CHUNK, HG_HEAD_DIM), MXU_DTYPE), SDS((2, HG_KEPT, n, HG_DIM), F32)],
        scratch_shapes=[pltpu.VMEM((HG_HEADS, HG_HEAD_DIM, HG_HEAD_DIM), F32)],
        compiler_params=_cp("parallel", "arbitrary"))(p, p, p, p, p, p, a0, a1, seg, *masks)


def _hgrn_bwd(p, a0, a1, masks, gp, gn, do, s0, a, e, kept, name):
    n = p.shape[0]
    nc, sp = _hg_specs(n, False)


    def body(v0, v1, a0_ref, a1_ref, later_ref, same_ref, gp_ref, gn_ref, do_ref, s0_ref, a_ref, e_ref, g_ref,
             dhq_ref, dz_ref, dv_ref, dlb_ref, rt):
        @pl.when(pl.program_id(1) == 0)
        def _():
            rt[...] = jnp.zeros_like(rt)
            dlb_ref[...] = jnp.zeros_like(dlb_ref)

        mirrored = pl.program_id(0) == 1
        for hh in range(HG_HEADS):
            ln = _hg_lanes(hh)
            q, k, f, sg, dsilu, e_first, e_last = (g_ref[0, j, :, ln] for j in range(HG_KEPT))
            lb = _sigmoid(a0_ref[0, :, ln] - a1_ref[0, :, ln])
            vv, dov = _hg_head((v0, v1), hh), do_ref[:, ln]
            ex = [e_first] + [e_ref[0, hh, 0, lev].astype(F32) for lev in range(HG_LEVELS)] + [e_last]
            a = a_ref[0, hh, 0]
            da = _dot(dov, vv, "nt")
            diag = jnp.sum(dov * vv, axis=-1, keepdims=True)
            s_t = s0_ref[0, hh, 0]
            r_t = rt[hh]
            k_end = k * ex[HG_LEVELS + 1]
            dv_ref[0, :, ln] = _dot(a, dov, "tn") + _dot(k_end, r_t, "nt")
            dq_inter = ex[0] * _dot(dov, s_t)
            dk_inter = ex[HG_LEVELS + 1] * _dot(vv, r_t)
            dq = diag * k + dq_inter
            dk = diag * q + dk_inter
            q_terms, k_terms = [q * dq_inter], [k * dk_inter]
            for lev in range(HG_LEVELS):
                qs, ks, e_q, e_k = _hg_level(q, k, ex, later_ref, lev)
                pairs = da * same_ref[lev]
                q_part = e_q * _dot(pairs, ks)
                k_part = e_k * _dot(pairs, qs, "tn")
                dq, dk = dq + q_part, dk + k_part
                q_terms.append(q * q_part)
                k_terms.append(k * k_part)
            decay = _hg_last_row(ex[0], mirrored)
            rt[hh] = r_t * decay + _dot(dov, q * ex[0], "tn")
            later = decay * jnp.sum(s_t * r_t, axis=0, keepdims=True)
            dlf = _dot(gp_ref[0], jnp.concatenate(q_terms, axis=0)) + _dot(gn_ref[0], jnp.concatenate(k_terms, axis=0)) + later
            df = dlf / f - dk
            dz_ref[0, :, ln] = df * (1.0 - lb) * sg * (1.0 - sg)
            dlb_ref[0, :, ln] += jnp.sum(df * (1.0 - sg), axis=0, keepdims=True)
            dhq_ref[0, :, ln] = dq * dsilu

    out = SDS((2, n, HG_DIM), F32)
    return pl.pallas_call(
        body, name=name, grid=(2, nc),
        in_specs=sp["v"] + [sp["vec"], sp["vec"], sp["later"], sp["same"], sp["sums"], sp["sums"],
                            sp["shared"], sp["state"], sp["weights"], sp["levels"], sp["kept"]],
        out_specs=[sp["per_dir"], sp["per_dir"], sp["per_dir"], sp["vec"]], out_shape=[out, out, out, SDS((2, 1, HG_DIM), F32)],
        scratch_shapes=[pltpu.VMEM((HG_HEADS, HG_HEAD_DIM, HG_HEAD_DIM), F32)],
        compiler_params=_cp("parallel", "arbitrary"))(p, p, a0, a1, *masks, gp, gn, do, s0, a, e, kept)


def _hg_post(o2, p, g, name):
    n = p.shape[0]
    tr = min(ROW_TILE, n)
    w = 2 * HG_HEAD_DIM

    def body(of_ref, ob_ref, hg_ref, g_ref, o_ref):
        for j in range(2):
            sl = slice(j * HG_HEAD_DIM, (j + 1) * HG_HEAD_DIM)
            o = of_ref[0, :, sl] + ob_ref[0, :, sl]
            hg = hg_ref[:, sl]
            o_ref[:, sl] = (o * _rstd(o) * g_ref[...] * (hg * _sigmoid(hg))).astype(o_ref.dtype)

    blk = pl.BlockSpec((tr, w), lambda i, j: (i, j))
    dirs = [pl.BlockSpec((1, tr, w), lambda i, j, d=d: (d, i, j)) for d in range(2)]
    return pl.pallas_call(
        body, name=name, grid=(n // tr, HG_DIM // w),
        in_specs=dirs + [pl.BlockSpec((tr, w), lambda i, j: (i, OFF_HG // w + j)), pl.BlockSpec((1, HG_HEAD_DIM), lambda i, j: (0, 0))],
        out_specs=blk, out_shape=SDS((n, HG_DIM), MXU_DTYPE), compiler_params=_cp("parallel", "parallel"))(o2, o2, p, g)


def _hg_post_bwd(o2, p, g, dcat, name, after=()):
    n = p.shape[0]
    tr = min(ROW_TILE, n)
    w = 2 * HG_HEAD_DIM
    after, after_specs = _unread(after)

    def body(of_ref, ob_ref, hg_ref, g_ref, d_ref, *rest):
        do_ref, dhg_ref, dg_ref = rest[len(after):]

        @pl.when(pl.program_id(1) == 0)
        def _():
            dg_ref[...] = jnp.zeros_like(dg_ref)

        for j in range(2):
            sl = slice(j * HG_HEAD_DIM, (j + 1) * HG_HEAD_DIM)
            o = of_ref[0, :, sl] + ob_ref[0, :, sl]
            hg = hg_ref[:, sl]
            d = d_ref[:, sl].astype(F32)
            sg = _sigmoid(hg)
            on = o * _rstd(o) * g_ref[...]
            dhg_ref[:, sl] = (d * on * sg * (1.0 + hg * (1.0 - sg))).astype(dhg_ref.dtype)
            dx, dg = _rms_bwd(o, g_ref[...], d * hg * sg)
            do_ref[:, sl] = dx
            dg_ref[0, :, sl] += dg

    blk = pl.BlockSpec((tr, w), lambda j, i: (i, j))
    dirs = [pl.BlockSpec((1, tr, w), lambda j, i, d=d: (d, i, j)) for d in range(2)]
    return pl.pallas_call(
        body, name=name, grid=(HG_DIM // w, n // tr),
        in_specs=dirs + [pl.BlockSpec((tr, w), lambda j, i: (i, OFF_HG // w + j)), pl.BlockSpec((1, HG_HEAD_DIM), lambda j, i: (0, 0)),
                         pl.BlockSpec((tr, w), lambda j, i: (i, ATT_Q_DIM // w + j))] + after_specs,
        out_specs=[blk, blk, pl.BlockSpec((1, 1, w), lambda j, i: (j, 0, 0))],
        out_shape=[SDS((n, HG_DIM), F32), SDS((n, HG_DIM), MXU_DTYPE), SDS((HG_DIM // w, 1, w), F32)],
        compiler_params=_cp("parallel", "arbitrary"))(o2, o2, p, g, dcat, *after)


XATT_TQ = 512


def _xattn_fwd(q, kv, name):
    n, nm = q.shape[0], kv.shape[0]
    tq = min(XATT_TQ, n)
    scale = X_HEAD_DIM ** -0.5

    def body(q_ref, k_ref, v_ref, o_ref):
        s = _dot(q_ref[...], k_ref[...], "nt") * scale
        e = jnp.exp(s - jnp.max(s, axis=-1, keepdims=True))
        o_ref[...] = _dot(e / jnp.sum(e, axis=-1, keepdims=True), v_ref[...]).astype(o_ref.dtype)

    qb = pl.BlockSpec((tq, X_HEAD_DIM), lambda h, i: (i, h))
    return pl.pallas_call(
        body, name=name, grid=(X_HEADS, n // tq),
        in_specs=[qb, pl.BlockSpec((nm, X_HEAD_DIM), lambda h, i: (0, h)), pl.BlockSpec((nm, X_HEAD_DIM), lambda h, i: (0, X_HEADS + h))],
        out_specs=qb, out_shape=SDS(q.shape, MXU_DTYPE), compiler_params=_cp("parallel", "parallel"))(q, kv, kv)


def _xattn_bwd(q, kv, do, name, after=()):
    n, nm = q.shape[0], kv.shape[0]
    tq = min(XATT_TQ, n)
    scale = X_HEAD_DIM ** -0.5
    after, after_specs = _unread(after)

    def body(q_ref, k_ref, v_ref, do_ref, *rest):
        dq_ref, dk_ref, dv_ref = rest[len(after):]

        @pl.when(pl.program_id(1) == 0)
        def _():
            dk_ref[...] = jnp.zeros_like(dk_ref)
            dv_ref[...] = jnp.zeros_like(dv_ref)

        qv, dov = q_ref[...], do_ref[...]
        s = _dot(qv, k_ref[...], "nt") * scale
        e = jnp.exp(s - jnp.max(s, axis=-1, keepdims=True))
        p = e / jnp.sum(e, axis=-1, keepdims=True)
        dp = _dot(dov, v_ref[...], "nt")
        ds = p * (dp - jnp.sum(p * dp, axis=-1, keepdims=True)) * scale
        dq_ref[...] = _dot(ds, k_ref[...]).astype(dq_ref.dtype)
        dk_ref[...] += _dot(ds, qv, "tn")
        dv_ref[...] += _dot(p, dov, "tn")

    qb = pl.BlockSpec((tq, X_HEAD_DIM), lambda h, i: (i, h))
    kb = pl.BlockSpec((nm, X_HEAD_DIM), lambda h, i: (0, h))
    return pl.pallas_call(
        body, name=name, grid=(X_HEADS, n // tq),
        in_specs=[qb, kb, pl.BlockSpec((nm, X_HEAD_DIM), lambda h, i: (0, X_HEADS + h)), qb] + after_specs, out_specs=[qb, kb, kb],
        out_shape=[SDS(q.shape, MXU_DTYPE), SDS((nm, X_HEADS * X_HEAD_DIM), F32), SDS((nm, X_HEADS * X_HEAD_DIM), F32)],
        compiler_params=_cp("parallel", "arbitrary"))(q, kv, kv, do, *after)


def _edge_rows(shape):
    row = lax.broadcasted_iota(jnp.int32, shape, 0)
    return row == 0, row == shape[0] - 1


def _shift_rows(u, down, edges):
    if down:
        return jnp.where(edges[0], 0.0, pltpu.roll(u, 1, axis=0))
    return jnp.where(edges[1], 0.0, pltpu.roll(u, u.shape[0] - 1, axis=0))


def _conv(u, w, b, edges):
    return b + _shift_rows(u, True, edges) * w[0:1, :] + u * w[1:2, :] + _shift_rows(u, False, edges) * w[2:3, :]


def _ff_specs(n):
    gate = lambda rows: pl.BlockSpec((rows, FF_COLS), lambda j: (0, j))
    val = lambda rows: pl.BlockSpec((rows, FF_COLS), lambda j: (0, FF_BLOCKS + j))
    return [gate(n), val(n), gate(3), val(3), gate(1), val(1)], gate


def _conv_gate(u, cw, cb, name):
    n = u.shape[0]
    ins, gate_blk = _ff_specs(n)

    def body(ug_ref, uv_ref, wg_ref, wv_ref, bg_ref, bv_ref, o_ref):
        edges = _edge_rows(ug_ref.shape)
        gate = _conv(ug_ref[...], wg_ref[...], bg_ref[...], edges)
        val = _conv(uv_ref[...], wv_ref[...], bv_ref[...], edges)
        o_ref[...] = (gate * _sigmoid(gate) * val).astype(o_ref.dtype)

    return pl.pallas_call(
        body, name=name, grid=(FF_BLOCKS,), in_specs=ins, out_specs=gate_blk(n), out_shape=SDS((n, D_FF), MXU_DTYPE),
        compiler_params=_cp("parallel"))(u, u, cw, cw, cb, cb)


def _conv_gate_bwd(u, cw, cb, da, name, after=()):
    n = u.shape[0]
    ins, gate_blk = _ff_specs(n)
    after, after_specs = _unread(after)

    def side(dacc, u, w, edges, du_ref, dw_ref, db_ref):
        nxt, prv = _shift_rows(dacc, False, edges), _shift_rows(dacc, True, edges)
        du_ref[...] = (nxt * w[0:1, :] + dacc * w[1:2, :] + prv * w[2:3, :]).astype(du_ref.dtype)
        db_ref[...] = jnp.sum(dacc, axis=0, keepdims=True)
        dw_ref[0:1, :] = jnp.sum(nxt * u, axis=0, keepdims=True)
        dw_ref[1:2, :] = jnp.sum(dacc * u, axis=0, keepdims=True)
        dw_ref[2:3, :] = jnp.sum(prv * u, axis=0, keepdims=True)

    def body(ug_ref, uv_ref, wg_ref, wv_ref, bg_ref, bv_ref, da_ref, *rest):
        dug_ref, duv_ref, dwg_ref, dwv_ref, dbg_ref, dbv_ref = rest[len(after):]
        ug, uv = ug_ref[...], uv_ref[...]
        edges = _edge_rows(ug.shape)
        gate = _conv(ug, wg_ref[...], bg_ref[...], edges)
        val = _conv(uv, wv_ref[...], bv_ref[...], edges)
        sg = _sigmoid(gate)
        dav = da_ref[...].astype(F32)
        side(dav * val * sg * (1.0 + gate * (1.0 - sg)), ug, wg_ref[...], edges, dug_ref, dwg_ref, dbg_ref)
        side(dav * gate * sg, uv, wv_ref[...], edges, duv_ref, dwv_ref, dbv_ref)

    return pl.pallas_call(
        body, name=name, grid=(FF_BLOCKS,), in_specs=ins + [gate_blk(n)] + after_specs,
        out_specs=[gate_blk(n), gate_blk(n), gate_blk(3), gate_blk(3), gate_blk(1), gate_blk(1)],
        out_shape=[SDS((n, D_FF), MXU_DTYPE)] * 2 + [SDS((3, D_FF), F32)] * 2 + [SDS((1, D_FF), F32)] * 2,
        compiler_params=_cp("parallel"))(u, u, cw, cw, cb, cb, da, *after)


def _adamw(w, g, m, v, name):
    r, c = w.shape[-2:]
    tr = _row_tile(r, ELEMENTWISE_ROWS)
    assert w.ndim == 2 or w.shape[:-2] == (1,), (name, w.shape)

    def body(w_ref, g_ref, m_ref, v_ref, d_ref, mo_ref, vo_ref, go_ref):
        gv = g_ref[...]
        go_ref[...] = gv
        mn = ADAM_B1 * m_ref[...] + (1.0 - ADAM_B1) * gv
        vn = ADAM_B2 * v_ref[...] + (1.0 - ADAM_B2) * gv * gv
        m_hat = mn / (1.0 - ADAM_B1 ** ADAM_STEP)
        v_hat = vn / (1.0 - ADAM_B2 ** ADAM_STEP)
        d_ref[...] = -ADAM_LR * (m_hat / (jnp.sqrt(v_hat) + ADAM_EPS) + ADAM_WD * w_ref[...])
        mo_ref[...] = mn
        vo_ref[...] = vn

    blk = pl.BlockSpec((tr, c), lambda i: (i, 0)) if w.ndim == 2 else pl.BlockSpec((1, tr, c), lambda i: (0, i, 0))
    out = SDS(w.shape, F32)
    return pl.pallas_call(body, name=name, grid=(r // tr,), in_specs=[blk] * 4, out_specs=[blk] * 4, out_shape=[out] * 4,
                          compiler_params=_cp("parallel"))(w, g, m, v)


ANY = pl.BlockSpec(memory_space=pl.ANY)


def _place():
    x, y, c = lax.axis_index("x"), lax.axis_index("y"), lax.axis_index("c")
    return x, y, c, [(1 - x, y), (x, 1 - y), (1 - x, 1 - y)]


def _gather_shards(shards, name):
    nt = len(shards)

    def body(*refs):
        ins, outs = refs[:nt], refs[nt:2 * nt]
        send, recv, fsend, frecv, osend, orecv = refs[2 * nt:]
        x, y, c, chips = _place()
        me = 2 * x + y

        def half(t, chip, cc):
            h = ins[t].shape[0] // 2
            return outs[t].at[chip, pl.ds(cc * h, h)]

        def ici(t, j):
            cx, cy = chips[j]
            h = ins[t].shape[0] // 2
            return pltpu.make_async_remote_copy(src_ref=ins[t].at[pl.ds(c * h, h)], dst_ref=half(t, me, c),
                                                send_sem=send.at[t, j], recv_sem=recv.at[t, j], device_id=(cx, cy, c), device_id_type=MESH)

        def landed(t, j):
            cx, cy = chips[j]
            blk = half(t, 2 * cx + cy, c)
            return pltpu.make_async_remote_copy(src_ref=blk, dst_ref=blk, send_sem=send.at[t, j], recv_sem=recv.at[t, j],
                                                device_id=(cx, cy, c), device_id_type=MESH)

        def d2d(t, j, cc):
            cx, cy = chips[j]
            blk = half(t, 2 * cx + cy, cc)
            return pltpu.make_async_remote_copy(src_ref=blk, dst_ref=blk, send_sem=fsend.at[t, j], recv_sem=frecv.at[t, j],
                                                device_id=(x, y, 1 - c), device_id_type=MESH)

        own = [pltpu.make_async_remote_copy(src_ref=ins[t], dst_ref=outs[t].at[me], send_sem=osend.at[t], recv_sem=orecv.at[t],
                                            device_id=(x, y, 1 - c), device_id_type=MESH) for t in range(nt)]
        for t in range(nt):
            for j in range(3):
                ici(t, j).start()
        for cp in own:
            cp.start()
        for t in range(nt):
            for j in range(3):
                landed(t, j).wait_recv()
                d2d(t, j, c).start()
        for t in range(nt):
            for j in range(3):
                d2d(t, j, 1 - c).wait_recv()
        for t in range(nt):
            for j in range(3):
                ici(t, j).wait_send()
                d2d(t, j, c).wait_send()
        for cp in own:
            cp.wait()

    return pl.pallas_call(
        body, name=name, in_specs=[ANY] * nt, out_specs=[ANY] * nt,
        out_shape=[SDS((4,) + s.shape, s.dtype) for s in shards],
        scratch_shapes=[pltpu.SemaphoreType.DMA((nt, 3))] * 4 + [pltpu.SemaphoreType.DMA((nt,))] * 2,
        compiler_params=pltpu.CompilerParams(has_side_effects=True))(*shards)


def _join_halves(bufs, name):
    nt = len(bufs)

    def body(*refs):
        outs = refs[nt:2 * nt]
        send, recv = refs[2 * nt:]
        x, y, c, _ = _place()
        cps = [pltpu.make_async_remote_copy(src_ref=outs[t].at[c], dst_ref=outs[t].at[c], send_sem=send.at[t], recv_sem=recv.at[t],
                                            device_id=(x, y, 1 - c), device_id_type=MESH) for t in range(nt)]
        for cp in cps:
            cp.start()
        for t in range(nt):
            theirs = outs[t].at[1 - c]
            pltpu.make_async_remote_copy(src_ref=theirs, dst_ref=theirs, send_sem=send.at[t], recv_sem=recv.at[t],
                                         device_id=(x, y, 1 - c), device_id_type=MESH).wait_recv()
        for cp in cps:
            cp.wait_send()

    return pl.pallas_call(
        body, name=name, in_specs=[ANY] * nt, out_specs=[ANY] * nt, out_shape=[SDS(b.shape, b.dtype) for b in bufs],
        input_output_aliases={t: t for t in range(nt)},
        scratch_shapes=[pltpu.SemaphoreType.DMA((nt,))] * 2,
        compiler_params=pltpu.CompilerParams(has_side_effects=True))(*bufs)


def _exchange_small(v, reduce, name, after=()):
    rows = v.shape[0]
    after, after_specs = _unread(after)

    def body(v_ref, *rest):
        o_ref, buf, send, recv = rest[-4:]
        x, y, c, _ = _place()
        me = 4 * x + 2 * y + c
        buf[me] = v_ref[...]

        def peer(dx, dy, dc):
            return (1 - x if dx else x, 1 - y if dy else y, 1 - c if dc else c)

        peers = [(dx, dy, dc) for dx in range(2) for dy in range(2) for dc in range(2) if (dx, dy, dc) != (0, 0, 0)]
        cps = []
        for j, (dx, dy, dc) in enumerate(peers):
            cps.append(pltpu.make_async_remote_copy(src_ref=v_ref, dst_ref=buf.at[me], send_sem=send.at[j], recv_sem=recv.at[j],
                                                    device_id=peer(dx, dy, dc), device_id_type=MESH))
        for cp in cps:
            cp.start()
        for j, (dx, dy, dc) in enumerate(peers):
            px, py, pc = peer(dx, dy, dc)
            blk = buf.at[4 * px + 2 * py + pc]
            pltpu.make_async_remote_copy(src_ref=blk, dst_ref=blk, send_sem=send.at[j], recv_sem=recv.at[j],
                                         device_id=(px, py, pc), device_id_type=MESH).wait_recv()
        for cp in cps:
            cp.wait_send()
        if reduce:
            acc = buf[0]
            for j in range(1, 8):
                acc = acc + buf[j]
            o_ref[...] = acc
        else:
            o_ref[...] = buf[...]

    vm = pl.BlockSpec(memory_space=pltpu.VMEM)
    return pl.pallas_call(
        body, name=name, in_specs=[vm] + after_specs, out_specs=vm, out_shape=SDS((rows, 128) if reduce else (8, rows, 128), F32),
        scratch_shapes=[pltpu.VMEM((8, rows, 128), F32), pltpu.SemaphoreType.DMA((7,)), pltpu.SemaphoreType.DMA((7,))],
        compiler_params=pltpu.CompilerParams(has_side_effects=True))(v, *after)


HBM = pl.BlockSpec(memory_space=pltpu.HBM)
SEM = pl.BlockSpec(memory_space=pltpu.SEMAPHORE)
TOKEN = pl.BlockSpec(memory_space=pltpu.VMEM)
TOKEN_SHAPE = SDS((8, 128), F32)
PEERS = 7


def _in_hbm(a):
    return pltpu.with_memory_space_constraint(a, pltpu.HBM)


def _split_params():
    return pltpu.CompilerParams(has_side_effects=pltpu.SideEffectType.DATAFLOW_SIDE_EFFECTING)


def _gather_start(shards, name, after=()):
    nt = len(shards)
    after, after_specs = _unread(after)

    def body(*refs):
        ins, lands = refs[:nt], refs[nt:2 * nt]
        outs = refs[2 * nt + len(after):]
        sends, recvs = outs[:nt], outs[nt:2 * nt]
        x, y, c, chips = _place()
        me = 2 * x + y
        for t in range(nt):
            h = ins[t].shape[0] // 2
            mine = pl.ds(c * h, h)
            for j, (cx, cy) in enumerate(chips):
                for dc in range(2):
                    pltpu.make_async_remote_copy(src_ref=ins[t].at[mine], dst_ref=lands[t].at[me, mine], send_sem=sends[t].at[2 * j + dc],
                                                 recv_sem=recvs[t].at[2 * j + c], device_id=(cx, cy, dc), device_id_type=MESH).start()
            pltpu.make_async_remote_copy(src_ref=ins[t], dst_ref=lands[t].at[me], send_sem=sends[t].at[PEERS - 1], recv_sem=recvs[t].at[PEERS - 1],
                                         device_id=(x, y, 1 - c), device_id_type=MESH).start()
        outs[-1][...] = jnp.zeros(TOKEN_SHAPE.shape, F32)

    lands = [lax.empty((4,) + s.shape, s.dtype) for s in shards]
    out = pl.pallas_call(
        body, name=name, in_specs=[HBM] * (2 * nt) + after_specs, out_specs=[SEM] * (2 * nt) + [HBM] * (2 * nt) + [TOKEN],
        out_shape=[pltpu.SemaphoreType.DMA((PEERS,))] * (2 * nt)
        + [pltpu.HBM(s.shape, s.dtype) for s in shards] + [pltpu.HBM(l.shape, l.dtype) for l in lands] + [TOKEN_SHAPE],
        input_output_aliases={t: 2 * nt + t for t in range(2 * nt)}, compiler_params=_split_params())(
            *[_in_hbm(s) for s in shards], *[_in_hbm(l) for l in lands], *after)
    return out[:nt], out[nt:2 * nt], out[2 * nt:3 * nt], out[3 * nt:4 * nt], out[-1]


def _gather_wait(sends, recvs, shards, lands, after, name):
    nt = len(shards)

    def body(*refs):
        ins, lands_ref = refs[:nt], refs[nt:2 * nt]
        send_refs, recv_refs = refs[2 * nt:3 * nt], refs[3 * nt:4 * nt]
        x, y, c, chips = _place()
        for t in range(nt):
            h = ins[t].shape[0] // 2
            for j, (cx, cy) in enumerate(chips):
                for cs in range(2):
                    blk = lands_ref[t].at[2 * cx + cy, pl.ds(cs * h, h)]
                    pltpu.make_async_remote_copy(src_ref=blk, dst_ref=blk, send_sem=send_refs[t].at[2 * j + cs], recv_sem=recv_refs[t].at[2 * j + cs],
                                                 device_id=(cx, cy, cs), device_id_type=MESH).wait()
            blk = lands_ref[t].at[2 * x + y]
            pltpu.make_async_remote_copy(src_ref=blk, dst_ref=blk, send_sem=send_refs[t].at[PEERS - 1], recv_sem=recv_refs[t].at[PEERS - 1],
                                         device_id=(x, y, 1 - c), device_id_type=MESH).wait()

    out = pl.pallas_call(
        body, name=name, in_specs=[HBM] * (2 * nt) + [SEM] * (2 * nt) + [ANY], out_specs=[HBM] * (2 * nt),
        out_shape=[pltpu.HBM(s.shape, s.dtype) for s in shards] + [pltpu.HBM(l.shape, l.dtype) for l in lands],
        input_output_aliases={t: t for t in range(2 * nt)}, compiler_params=_split_params())(*shards, *lands, *sends, *recvs, after)
    return out[nt:]


def _scatter_start(g, name):
    _, r, c_ = g.shape
    h = r // 2

    def body(g_ref, land, send, recv, g_thru, land_thru, token):
        x, y, c, chips = _place()
        for j, (cx, cy) in enumerate(chips):
            for dc in range(2):
                pltpu.make_async_remote_copy(src_ref=g_ref.at[2 * cx + cy, pl.ds(dc * h, h)], dst_ref=land.at[2 * j + c], send_sem=send.at[2 * j + dc],
                                             recv_sem=recv.at[2 * j + c], device_id=(cx, cy, dc), device_id_type=MESH).start()
        pltpu.make_async_remote_copy(src_ref=g_ref.at[2 * x + y, pl.ds((1 - c) * h, h)], dst_ref=land.at[PEERS - 1], send_sem=send.at[PEERS - 1],
                                     recv_sem=recv.at[PEERS - 1], device_id=(x, y, 1 - c), device_id_type=MESH).start()
        token[...] = jnp.zeros(TOKEN_SHAPE.shape, F32)

    land = lax.empty((PEERS, h, c_), g.dtype)
    return pl.pallas_call(
        body, name=name, in_specs=[HBM, HBM], out_specs=[SEM, SEM, HBM, HBM, TOKEN],
        out_shape=[pltpu.SemaphoreType.DMA((PEERS,)), pltpu.SemaphoreType.DMA((PEERS,)), pltpu.HBM(g.shape, g.dtype),
                   pltpu.HBM(land.shape, land.dtype), TOKEN_SHAPE],
        input_output_aliases={0: 2, 1: 3}, compiler_params=_split_params())(_in_hbm(g), _in_hbm(land))


def _scatter_wait(started, after, name):
    nt = len(started)

    def body(*refs):
        lands = refs[nt:2 * nt]
        sends, recvs = refs[2 * nt:3 * nt], refs[3 * nt:4 * nt]
        x, y, c, chips = _place()
        peers = [(cx, cy, dc) for cx, cy in chips for dc in range(2)] + [(x, y, 1 - c)]
        for t in range(nt):
            for k, peer in enumerate(peers):
                blk = lands[t].at[k]
                pltpu.make_async_remote_copy(src_ref=blk, dst_ref=blk, send_sem=sends[t].at[k], recv_sem=recvs[t].at[k],
                                             device_id=peer, device_id_type=MESH).wait()

    gs, lands = [s[2] for s in started], [s[3] for s in started]
    after, after_specs = _unread(after)
    out = pl.pallas_call(
        body, name=name, in_specs=[HBM] * (2 * nt) + [SEM] * (2 * nt) + after_specs, out_specs=[HBM] * (2 * nt),
        out_shape=[pltpu.HBM(a.shape, a.dtype) for a in gs + lands],
        input_output_aliases={t: t for t in range(2 * nt)}, compiler_params=_split_params())(
            *gs, *lands, *[s[0] for s in started], *[s[1] for s in started], *after)
    return out[:nt], out[nt:]


def _flips():
    return [(dx, dy, dc) for dx in range(2) for dy in range(2) for dc in range(2) if (dx, dy, dc) != (0, 0, 0)]


def _flipped(x, y, c, flips):
    dx, dy, dc = flips
    return (1 - x if dx else x, 1 - y if dy else y, 1 - c if dc else c)


def _small_start(v, name, after=()):
    after, after_specs = _unread(after)

    def body(v_ref, land, *rest):
        send, recv = rest[len(after):len(after) + 2]
        x, y, c, _ = _place()
        for j, flips in enumerate(_flips()):
            pltpu.make_async_remote_copy(src_ref=v_ref, dst_ref=land.at[4 * x + 2 * y + c], send_sem=send.at[j], recv_sem=recv.at[j],
                                         device_id=_flipped(x, y, c, flips), device_id_type=MESH).start()
        rest[-1][...] = jnp.zeros(TOKEN_SHAPE.shape, F32)

    land = lax.empty((8,) + v.shape, v.dtype)
    return pl.pallas_call(
        body, name=name, in_specs=[HBM, HBM] + after_specs, out_specs=[SEM, SEM, HBM, HBM, TOKEN],
        out_shape=[pltpu.SemaphoreType.DMA((PEERS,)), pltpu.SemaphoreType.DMA((PEERS,)), pltpu.HBM(v.shape, v.dtype),
                   pltpu.HBM(land.shape, land.dtype), TOKEN_SHAPE],
        input_output_aliases={0: 2, 1: 3}, compiler_params=_split_params())(_in_hbm(v), _in_hbm(land), *after)


def _small_wait(send, recv, v, land, after, name):
    after, after_specs = _unread(after)

    def body(v_ref, land_ref, send_ref, recv_ref, *rest):
        x, y, c, _ = _place()
        for j, flips in enumerate(_flips()):
            px, py, pc = _flipped(x, y, c, flips)
            blk = land_ref.at[4 * px + 2 * py + pc]
            pltpu.make_async_remote_copy(src_ref=blk, dst_ref=blk, send_sem=send_ref.at[j], recv_sem=recv_ref.at[j],
                                         device_id=(px, py, pc), device_id_type=MESH).wait()

    return pl.pallas_call(
        body, name=name, in_specs=[HBM, HBM, SEM, SEM] + after_specs, out_specs=[HBM, HBM],
        out_shape=[pltpu.HBM(v.shape, v.dtype), pltpu.HBM(land.shape, land.dtype)],
        input_output_aliases={0: 0, 1: 1}, compiler_params=_split_params())(v, land, send, recv, *after)


def _sum_small(v, land, name):
    def body(v_ref, land_ref, o_ref):
        x, y, c, _ = _place()
        me = 4 * x + 2 * y + c
        acc = jnp.where(me == 0, v_ref[...], land_ref[0])
        for d in range(1, 8):
            acc = acc + jnp.where(me == d, v_ref[...], land_ref[d])
        o_ref[...] = acc

    vm = pl.BlockSpec(memory_space=pltpu.VMEM)
    return pl.pallas_call(body, name=name, in_specs=[vm, vm], out_specs=vm, out_shape=SDS(v.shape, F32))(v, land)


def _sum_devices(g, land, me, core, name):
    npeer, h, c = land.shape
    tr = _row_tile(h, 2 * ELEMENTWISE_ROWS)
    steps = h // tr

    def body(ix_ref, own_ref, land_ref, o_ref):
        acc = own_ref[0].astype(F32)
        for j in range(npeer):
            acc = acc + land_ref[j].astype(F32)
        o_ref[0] = acc

    grid_spec = pltpu.PrefetchScalarGridSpec(
        num_scalar_prefetch=1, grid=(steps,),
        in_specs=[pl.BlockSpec((1, tr, c), lambda i, ix: (ix[0], ix[1] * steps + i, 0)), pl.BlockSpec((npeer, tr, c), lambda i, ix: (0, i, 0))],
        out_specs=pl.BlockSpec((1, tr, c), lambda i, ix: (ix[1], i, 0)))
    return pl.pallas_call(body, name=name, grid_spec=grid_spec, out_shape=SDS((2, h, c), F32),
                          compiler_params=_cp("parallel"))(jnp.stack([me, core]), g, land)


def _pack_small(parts):
    flat = jnp.concatenate([p.reshape(-1) for p in parts])
    total = flat.shape[0]
    rows = -(-total // 1024) * 8
    return jnp.pad(flat, (0, rows * 128 - total)).reshape(rows, 128)


def _unpack_small(packed, shapes):
    flat = packed.reshape(-1)
    out, off = [], 0
    for s in shapes:
        size = int(np.prod(s))
        out.append(flat[off:off + size].reshape(s))
        off += size
    return out


def _local_step(x, mem, target, w_in_t, first_after, mid_weights, ffn_weights, on_grad, gains, conv_w, conv_b, hg_lb):
    n = x.shape[0]
    cos, sin = _rope_tables(n)
    seg = _hg_segments()
    gp, gn = _hg_pair_sums()
    masks = _hg_level_masks()
    gq2 = jnp.tile(gains["q_norm_g"], (1, 2))
    gk2 = jnp.tile(gains["k_norm_g"], (1, 2))
    a0 = hg_lb[:, 0:1, :]
    a1 = hg_lb[:, 1:2, :]

    p, h1 = _norm_mm(x, gains["pre_mix_g"], w_in_t, F32, TOKEN_TILE, 1664, "in_proj", after=(first_after,), w_turned=True)
    qr, kr = _qk_prep(p, gq2, gk2, cos, sin, "qk_prep")
    heads = lambda a: a.reshape(n, ATT_KV_HEADS, ATT_HEAD_DIM).transpose(1, 0, 2)
    kh = heads(kr)
    vh = heads(p[:, OFF_AV:OFF_AV + ATT_KV_DIM].astype(MXU_DTYPE))
    att = _attn_fwd(qr, kh, vh, "attn_fwd")
    o2, s0, hg_a, hg_e, hg_kept = _hgrn_fwd(p, a0, a1, seg, masks, "hgrn_fwd")
    rec = _hg_post(o2, p, gains["hg_out_norm_g"], "hg_post")
    cat = jnp.concatenate([att, rec], axis=1)
    w_out, w_xq, w_xkv, w_xo = mid_weights(cat)
    mixed, x1 = _mm_resid_norm(cat, w_out, x, gains["post_mix_g"], 512, "out_proj_resid")
    xq, h2 = _norm_mm(x1, gains["pre_x_g"], w_xq, MXU_DTYPE, TOKEN_TILE, 1024, "xq_proj")
    kv, mn = _norm_mm(mem, gains["mem_norm_g"], w_xkv, MXU_DTYPE, 256, 2048, "xkv_proj")
    ox = _xattn_fwd(xq, kv, "xattn_fwd")
    xo, x2 = _mm_resid_norm(ox, w_xo, x1, gains["post_x_g"], 512, "xo_proj_resid")
    w_up = ffn_weights("w_up", x2)
    u, h3 = _norm_mm(x2, gains["pre_ffn_g"], w_up, F32, TOKEN_TILE, 1408, "up_proj")
    act = _conv_gate(u, conv_w, conv_b, "conv_gate")
    w_down = ffn_weights("w_down", act)
    dn, d3, loss = _mm_resid_norm(act, w_down, x2, gains["post_ffn_g"], 512, "down_proj_resid_loss", target=target)

    gs = {}
    d_act, d_dn, gs["post_ffn_g"] = _norm_bwd_mm(dn, gains["post_ffn_g"], d3, w_down, F32, 512, 1408, "ffn_post_bwd_down_dx")
    tok = on_grad("w_down", _mm(act, d_dn, "tn", WIRE_DTYPE, 1408, 1024, "down_dw"))
    du_g, du_v, dcw_g, dcw_v, dcb_g, dcb_v = _conv_gate_bwd(u, conv_w, conv_b, d_act, "conv_gate_bwd", after=(tok,))
    gs["conv_w"] = jnp.concatenate([dcw_g, dcw_v], axis=1)
    gs["conv_b"] = jnp.concatenate([dcb_g, dcb_v], axis=1)
    ff_shard = w_up.shape[2]
    g_up = _dw_by_owner(h3, du_g, ff_shard, 0, None, 512, "up_dw_gate")
    tok = on_grad("w_up", _dw_by_owner(h3, du_v, ff_shard, 2, g_up, 512, "up_dw_value"))
    d2, gs["pre_ffn_g"] = _dx_norm_bwd([(du_g, 0), (du_g, 1), (du_v, 0), (du_v, 1)], w_up, x2, gains["pre_ffn_g"], d3, 512,
                                       "up_dx_pre_bwd", after=(tok,))
    d_ox, d_xo, gs["post_x_g"] = _norm_bwd_mm(xo, gains["post_x_g"], d2, w_xo, MXU_DTYPE, 512, 1024, "x_post_bwd_xo_dx")
    tok = on_grad("w_xo", _mm(ox, d_xo, "tn", WIRE_DTYPE, 512, 1024, "xo_dw"))
    d_xq, d_k, d_v = _xattn_bwd(xq, kv, d_ox, "xattn_bwd", after=(tok,))
    d_kv = jnp.concatenate([d_k, d_v], axis=1).astype(MXU_DTYPE)
    tok = on_grad("w_xq", _mm(h2, d_xq, "tn", WIRE_DTYPE, 512, 1024, "xq_dw"))
    tok_kv = on_grad("w_xkv", _dw_by_owner(mn, d_kv, w_xkv.shape[2], 0, None, 512, "xkv_dw"))
    d1, gs["pre_x_g"] = _dx_norm_bwd([(d_xq, 0)], w_xq[None], x1, gains["pre_x_g"], d2, 512, "xq_dx_pre_bwd", after=(tok, tok_kv))
    d_mn = _mm_nt_parts([(d_kv, s) for s in range(4)], w_xkv, F32, 256, 1024, "xkv_dx")
    _, gs["mem_norm_g"] = _norm_bwd(mem, gains["mem_norm_g"], d_mn, None, MXU_DTYPE, "mem_norm_bwd")
    d_cat, d_mixed, gs["post_mix_g"] = _norm_bwd_mm(mixed, gains["post_mix_g"], d1, w_out, MXU_DTYPE, 512, 1024, "mix_post_bwd_out_dx")
    tok = on_grad("w_out", _mm(cat, d_mixed, "tn", WIRE_DTYPE, 512, 1024, "out_dw"))
    d_o, d_hg, dg_hg = _hg_post_bwd(o2, p, gains["hg_out_norm_g"], d_cat, "hg_post_bwd", after=(tok,))
    gs["hg_out_norm_g"] = dg_hg.reshape(HG_HEADS, HG_HEAD_DIM).sum(axis=0, keepdims=True)
    dhq2, dz2, dhv2, dlb = _hgrn_bwd(p, a0, a1, masks, gp, gn, d_o, s0, hg_a, hg_e, hg_kept, "hgrn_bwd")
    lb = jax.nn.sigmoid(a0 - a1)
    da0 = dlb * lb * (1.0 - lb)
    gs["hg_lb"] = jnp.concatenate([da0, -da0], axis=1)
    d_qr, d_kh, d_vh = _attn_bwd(qr, kh, vh, cat, d_cat, "attn_bwd")
    unheads = lambda a: a.transpose(2, 0, 1).reshape(n, ATT_KV_DIM)
    d_aq, d_ak, dgq, dgk = _qk_prep_bwd(p, gq2, gk2, cos, sin, d_qr, unheads(d_kh), "qk_prep_bwd")
    gs["q_norm_g"] = dgq.reshape(ATT_HEADS, ATT_HEAD_DIM).sum(axis=0, keepdims=True)
    gs["k_norm_g"] = dgk.reshape(ATT_KV_HEADS, ATT_HEAD_DIM).sum(axis=0, keepdims=True)
    d_p = jnp.concatenate([d_aq, d_ak, unheads(d_vh).astype(MXU_DTYPE), (dhq2[0] + dhq2[1]).astype(MXU_DTYPE),
                           dz2[0].astype(MXU_DTYPE), dz2[1].astype(MXU_DTYPE), (dhv2[0] + dhv2[1]).astype(MXU_DTYPE), d_hg], axis=1)
    tok = on_grad("w_in", _mm(d_p, h1, "tn", WIRE_DTYPE, 1664, 1024, "in_dw"))
    grad_x, gs["pre_mix_g"] = _dx_norm_bwd([(d_p, 0)], w_in_t[None], x, gains["pre_mix_g"], d1, 512, "in_dx_pre_bwd", after=(tok,),
                                           b_turned=True)
    return loss, grad_x, gs


MATS = ("w_in", "w_out", "w_xq", "w_xkv", "w_xo", "w_up", "w_down")
GAINS = ("pre_mix_g", "q_norm_g", "k_norm_g", "hg_out_norm_g", "post_mix_g", "pre_x_g", "mem_norm_g", "post_x_g", "pre_ffn_g", "post_ffn_g")
WEIGHTS = ('pre_mix_g', 'w_in', 'q_norm_g', 'k_norm_g', 'hg_lb', 'hg_out_norm_g', 'w_out', 'post_mix_g', 'pre_x_g', 'mem_norm_g', 'w_xq',
           'w_xkv', 'w_xo', 'post_x_g', 'pre_ffn_g', 'w_up', 'conv_w', 'conv_b', 'w_down', 'post_ffn_g')


def kernel(x, mem, pre_mix_g, w_in, q_norm_g, k_norm_g, hg_lb, hg_out_norm_g, w_out, post_mix_g, pre_x_g, mem_norm_g, w_xq, w_xkv, w_xo, post_x_g, pre_ffn_g, w_up, conv_w, conv_b, w_down, post_ffn_g, loss_target, m_pre_mix_g, m_w_in, m_q_norm_g, m_k_norm_g, m_hg_lb, m_hg_out_norm_g, m_w_out, m_post_mix_g, m_pre_x_g, m_mem_norm_g, m_w_xq, m_w_xkv, m_w_xo, m_post_x_g, m_pre_ffn_g, m_w_up, m_conv_w, m_conv_b, m_w_down, m_post_ffn_g, v_pre_mix_g, v_w_in, v_q_norm_g, v_k_norm_g, v_hg_lb, v_hg_out_norm_g, v_w_out, v_post_mix_g, v_pre_x_g, v_mem_norm_g, v_w_xq, v_w_xkv, v_w_xo, v_post_x_g, v_pre_ffn_g, v_w_up, v_conv_w, v_conv_b, v_w_down, v_post_ffn_g):
    args = dict(locals())
    w = {k: args[k] for k in WEIGHTS}
    m = {k: args["m_" + k] for k in WEIGHTS}
    v = {k: args["v_" + k] for k in WEIGHTS}
    chip = 2 * lax.axis_index("x") + lax.axis_index("y")
    core = lax.axis_index("c")

    turned = ("w_in",)
    shards = {k: (jnp.swapaxes(w[k], 1, 2) if k in turned else w[k])[0].astype(WIRE_DTYPE) for k in MATS}

    def whole(k, g):
        return g if k in ("w_xkv", "w_up") else g.reshape(-1, g.shape[-1])

    w_in_t = whole("w_in", _gather_shards([shards["w_in"]], "gather_w_in")[0])
    small_in = _exchange_small(_pack_small([w["conv_w"][0], w["hg_lb"]]), False, "gather_small")
    mid_names, ffn_names = ("w_out", "w_xq", "w_xkv", "w_xo"), ("w_up", "w_down")
    mid = _gather_start([shards[k] for k in mid_names], "gather_mid_start", after=(w_in_t, small_in))
    ffn = _gather_start([shards[k] for k in ffn_names], "gather_ffn_start", after=(mid[4],))

    def mid_weights(after):
        return [whole(k, g) for k, g in zip(mid_names, _gather_wait(*mid[:4], after, "gather_mid_wait"))]

    def ffn_weights(k, after):
        t = ffn_names.index(k)
        return whole(k, _gather_wait(*[part[t:t + 1] for part in ffn[:4]], after, "gather_wait_" + k)[0])

    cw_parts, lb_parts = [], []
    for s in range(4):
        cw_s, lb_s = _unpack_small(small_in[2 * s], [w["conv_w"][0].shape, w["hg_lb"].shape])
        cw_parts.append(cw_s)
        lb_parts.append(lb_s)
    conv_w_full = jnp.concatenate(cw_parts, axis=1)
    hg_lb_full = jnp.concatenate(lb_parts, axis=2)

    started = {}

    def on_grad(k, g):
        if g.ndim == 2:
            g = g.reshape(4, g.shape[0] // 4, g.shape[1])
        *started[k], token = _scatter_start(g, "grad_start_" + k)
        return token

    gains = {k: w[k] for k in GAINS}
    loss_part, grad_x, gs = _local_step(x[0], mem[0], loss_target[0], w_in_t, ffn[4], mid_weights, ffn_weights, on_grad, gains,
                                        conv_w_full, w["conv_b"], hg_lb_full)
    gs["loss"] = loss_part

    grads, delta, new_m, new_v = {}, {}, {}, {}

    def reduce_matrices(names, after, tag):
        sent, landed = _scatter_wait([started[k] for k in names], after, "grad_wait_" + tag)
        halves = [_sum_devices(g, land, chip, core, "grad_sum_" + k) for k, g, land in zip(names, sent, landed)]
        for k, r in zip(names, _join_halves(halves, "grad_join_" + tag)):
            grads[k] = r.reshape(1, -1, r.shape[-1])

    def adamw(names):
        for k in names:
            shape = w[k].shape
            keep = len(shape) == 3 and shape[0] == 1
            if k in turned:
                view, back = (lambda a: jnp.swapaxes(a, 1, 2)), (lambda a: jnp.swapaxes(a, 1, 2))
                g = grads[k]
            else:
                view = (lambda a: a.reshape(shape)) if keep else (lambda a: a.reshape(-1, shape[-1]))
                back = lambda a: a.reshape(shape)
                g = view(grads[k])
            d, mo, vo, go = _adamw(view(w[k]), g, view(m[k]), view(v[k]), "adamw_" + k)
            delta[k], new_m[k], new_v[k], grads[k] = back(d), back(mo), back(vo), back(go)

    small_names = GAINS + ("conv_b", "conv_w", "hg_lb")
    packed = _pack_small([gs[k] for k in small_names + ("loss",)])
    small = _small_start(packed, "reduce_small_start", after=(grad_x,))

    early = tuple(k for k in MATS if k != "w_in")
    reduce_matrices(early, (grad_x, small[4]), "early")
    adamw(early)

    mine, others = _small_wait(*small[:4], tuple(new_v[k] for k in early), "reduce_small_wait")
    reduced_small = _sum_small(mine, others, "reduce_small_sum")
    *summed, loss = _unpack_small(reduced_small, [gs[k].shape for k in small_names + ("loss",)])
    loss = loss[0, 0]
    for k, g in zip(small_names, summed):
        grads[k] = g
    ncw = w["conv_w"].shape[2]
    grads["conv_w"] = lax.dynamic_slice_in_dim(grads["conv_w"], chip * ncw, ncw, axis=1)[None]
    nlb = w["hg_lb"].shape[2]
    grads["hg_lb"] = lax.dynamic_slice_in_dim(grads["hg_lb"], chip * nlb, nlb, axis=2)
    replicated = GAINS + ("conv_b",)
    shapes = [w[k].shape for k in replicated]
    rows = sum(int(np.prod(s)) for s in shapes) // 128
    pack = lambda d: jnp.concatenate([d[k].reshape(-1) for k in replicated]).reshape(rows, 128)
    outs = _adamw(pack(w), reduced_small[:rows], pack(m), pack(v), "adamw_replicated")
    for into, packed_out in zip((delta, new_m, new_v, grads), outs):
        for k, a in zip(replicated, _unpack_small(packed_out, shapes)):
            into[k] = a
    adamw(("conv_w", "hg_lb"))

    reduce_matrices(("w_in",), tuple(new_v[k] for k in early + small_names), "late")
    adamw(("w_in",))
    return (loss, grad_x[None], *[grads[k] for k in WEIGHTS], *[delta[k] for k in WEIGHTS],
            *[new_m[k] for k in WEIGHTS], *[new_v[k] for k in WEIGHTS])
```

```python
import numpy as np
import jax
import jax.numpy as jnp
from jax import lax
from jax.experimental import pallas as pl
from jax.experimental.pallas import tpu as pltpu

F32 = jnp.float32
MXU_DTYPE = jnp.bfloat16
WIRE_DTYPE = jnp.bfloat16
VMEM_LIMIT_BYTES = 56 * 1024 * 1024
ROWS_PER_16BIT_TILE = 16
ELEMENTWISE_ROWS = 256
EPS = 1e-6
MESH = pl.DeviceIdType.MESH

GRID_W = 64
ATT_HEADS, ATT_KV_HEADS, ATT_HEAD_DIM = 8, 2, 64
ATT_GROUP = ATT_HEADS // ATT_KV_HEADS
ATT_Q_DIM, ATT_KV_DIM = 512, 128
ROPE_THETA = 10000.0
HG_HEADS, HG_HEAD_DIM, HG_DIM = 4, 128, 512
HG_CHUNK = 128
HG_LEVELS = 7
HG_PAIR = 2 * HG_HEAD_DIM
HG_KEPT = 7
X_HEADS, X_HEAD_DIM = 4, 256
D_FF = 2816
FF_COLS = 256
FF_BLOCKS = D_FF // FF_COLS
OFF_AK, OFF_AV, OFF_HQ, OFF_ZF, OFF_ZB, OFF_HI, OFF_HG = 512, 640, 768, 1280, 1792, 2304, 2816

ADAM_LR, ADAM_B1, ADAM_B2, ADAM_EPS, ADAM_WD, ADAM_STEP = 0.001, 0.9, 0.999, 1e-08, 0.01, 10

SDS = jax.ShapeDtypeStruct


def _cp(*sem):
    return pltpu.CompilerParams(dimension_semantics=sem, vmem_limit_bytes=VMEM_LIMIT_BYTES)


def _row_tile(rows, cap):
    if rows <= cap:
        return rows
    return max(t for t in range(ROWS_PER_16BIT_TILE, cap + 1, ROWS_PER_16BIT_TILE) if rows % t == 0)


def _dot(a, b, form="nn"):
    dims = {"nn": (((1,), (0,)), ((), ())), "nt": (((1,), (1,)), ((), ())), "tn": (((0,), (0,)), ((), ()))}[form]
    return lax.dot_general(a.astype(MXU_DTYPE), b.astype(MXU_DTYPE), dims, preferred_element_type=F32)


def _sigmoid(x):
    return 1.0 / (1.0 + jnp.exp(-x))


def _rstd(x):
    return lax.rsqrt(jnp.mean(x * x, axis=-1, keepdims=True) + EPS)


def _rms_bwd(x, g, dy):
    r = _rstd(x)
    xh = x * r
    dn = dy * g
    dx = r * (dn - xh * jnp.mean(dn * xh, axis=-1, keepdims=True))
    return dx, jnp.sum(dy * xh, axis=0, keepdims=True)


def _unread(after):
    after = tuple(a for a in after if a is not None)
    return after, [pl.BlockSpec(memory_space=pl.ANY)] * len(after)


def _mm(a, b, form, out_dtype, tm, tn, name, after=()):
    after, after_specs = _unread(after)
    if form == "nn":
        (m, k), n = a.shape, b.shape[1]
    elif form == "nt":
        (m, k), n = a.shape, b.shape[0]
    else:
        (k, m), n = a.shape, b.shape[1]
    tm, tn = min(tm, m), min(tn, n)
    assert m % tm == 0 and n % tn == 0, (name, m, n, tm, tn)

    def body(a_ref, b_ref, *rest):
        o_ref = rest[-1]
        o_ref[...] = _dot(a_ref[...], b_ref[...], form).astype(o_ref.dtype)

    a_spec = pl.BlockSpec((k, tm), lambda i, j: (0, i)) if form == "tn" else pl.BlockSpec((tm, k), lambda i, j: (i, 0))
    b_spec = pl.BlockSpec((tn, k), lambda i, j: (j, 0)) if form == "nt" else pl.BlockSpec((k, tn), lambda i, j: (0, j))
    return pl.pallas_call(
        body, name=name, grid=(m // tm, n // tn), in_specs=[a_spec, b_spec] + after_specs,
        out_specs=pl.BlockSpec((tm, tn), lambda i, j: (i, j)), out_shape=SDS((m, n), out_dtype),
        compiler_params=_cp("parallel", "parallel"))(a, b, *after)


def _mm_nt_parts(a_parts, b, out_dtype, tm, tn, name, after=()):
    after, after_specs = _unread(after)
    parts, n, p = b.shape
    m = a_parts[0][0].shape[0]
    tm, tn = min(tm, m), min(tn, n)
    assert m % tm == 0 and n % tn == 0 and len(a_parts) == parts, (name, m, b.shape)

    def body(*refs):
        o_ref = refs[-1]
        acc = _dot(refs[0][...], refs[parts][0], "nt")
        for s in range(1, parts):
            acc = acc + _dot(refs[s][...], refs[parts + s][0], "nt")
        o_ref[...] = acc.astype(o_ref.dtype)

    a_specs = [pl.BlockSpec((tm, p), lambda i, j, cb=cb: (i, cb)) for _, cb in a_parts]
    b_specs = [pl.BlockSpec((1, tn, p), lambda i, j, s=s: (s, j, 0)) for s in range(parts)]
    return pl.pallas_call(
        body, name=name, grid=(m // tm, n // tn), in_specs=a_specs + b_specs + after_specs,
        out_specs=pl.BlockSpec((tm, tn), lambda i, j: (i, j)), out_shape=SDS((m, n), out_dtype),
        compiler_params=_cp("parallel", "parallel"))(*[arr for arr, _ in a_parts], *([b] * parts), *after)


def _norm_bwd_mm(y, g, d, w, out_dtype, tm, tn, name):
    n, dm = y.shape
    nn = w.shape[0]
    tm, tn = min(tm, n), min(tn, nn)
    assert n % tm == 0 and nn % tn == 0 and w.shape[1] == dm, (name, y.shape, w.shape)

    def body(y_ref, g_ref, d_ref, w_ref, dx_ref, dy_ref, dg_ref, dys):
        i, j = pl.program_id(0), pl.program_id(1)

        @pl.when(jnp.logical_and(i == 0, j == 0))
        def _():
            dg_ref[...] = jnp.zeros_like(dg_ref)

        @pl.when(j == 0)
        def _():
            dy, dg = _rms_bwd(y_ref[...], g_ref[...], d_ref[...])
            dy = dy.astype(MXU_DTYPE)
            dys[...] = dy
            dy_ref[...] = dy
            dg_ref[...] += dg

        dx_ref[...] = _dot(dys[...], w_ref[...], "nt").astype(dx_ref.dtype)

    row = pl.BlockSpec((tm, dm), lambda i, j: (i, 0))
    vec = pl.BlockSpec((1, dm), lambda i, j: (0, 0))
    return pl.pallas_call(
        body, name=name, grid=(n // tm, nn // tn), in_specs=[row, vec, row, pl.BlockSpec((tn, dm), lambda i, j: (j, 0))],
        out_specs=[pl.BlockSpec((tm, tn), lambda i, j: (i, j)), row, vec],
        out_shape=[SDS((n, nn), out_dtype), SDS((n, dm), MXU_DTYPE), SDS((1, dm), F32)],
        scratch_shapes=[pltpu.VMEM((tm, dm), MXU_DTYPE)],
        compiler_params=_cp("arbitrary", "arbitrary"))(y, g, d, w)


def _mm_resid_norm(a, b, x, g, tm, name, target=None):
    n, k = a.shape
    d = b.shape[1]
    tm = min(tm, n)
    assert n % tm == 0 and x.shape == (n, d), (name, a.shape, b.shape)
    with_loss = target is not None

    def body(a_ref, b_ref, x_ref, g_ref, *rest):
        y = _dot(a_ref[...], b_ref[...])
        out = x_ref[...] + y * _rstd(y) * g_ref[...]
        if not with_loss:
            y_ref, o_ref = rest
            y_ref[...] = y
            o_ref[...] = out
            return
        t_ref, y_ref, d_ref, l_ref = rest
        y_ref[...] = y
        diff = out - t_ref[...]
        d_ref[...] = diff * (1.0 / d)

        @pl.when(pl.program_id(0) == 0)
        def _():
            l_ref[...] = jnp.zeros_like(l_ref)

        l_ref[...] += 0.5 * jnp.sum(jnp.mean(diff * diff, axis=-1, keepdims=True), axis=0, keepdims=True)

    row = pl.BlockSpec((tm, d), lambda i: (i, 0))
    ins = [pl.BlockSpec((tm, k), lambda i: (i, 0)), pl.BlockSpec((k, d), lambda i: (0, 0)), row, pl.BlockSpec((1, d), lambda i: (0, 0))]
    out = SDS((n, d), F32)
    if with_loss:
        return pl.pallas_call(body, name=name, grid=(n // tm,), in_specs=ins + [row], out_specs=[row, row, pl.BlockSpec((1, 1), lambda i: (0, 0))],
                              out_shape=[out, out, SDS((1, 1), F32)], compiler_params=_cp("arbitrary"))(a, b, x, g, target)
    return pl.pallas_call(body, name=name, grid=(n // tm,), in_specs=ins, out_specs=[row, row], out_shape=[out, out],
                          compiler_params=_cp("parallel"))(a, b, x, g)


def _dx_norm_bwd(a_parts, b, x, g, res, tm, name, after=(), b_turned=False):
    after, after_specs = _unread(after)
    parts, d, p = (b.shape[0], b.shape[2], b.shape[1]) if b_turned else b.shape
    form = "nn" if b_turned else "nt"
    n = x.shape[0]
    tm = min(tm, n)
    assert n % tm == 0 and len(a_parts) == parts and x.shape[1] == d, (name, x.shape, b.shape)

    def body(*refs):
        x_ref, g_ref, res_ref = refs[2 * parts:2 * parts + 3]
        dx_ref, dg_ref = refs[-2:]
        dh = _dot(refs[0][...], refs[parts][0], form)
        for s in range(1, parts):
            dh = dh + _dot(refs[s][...], refs[parts + s][0], form)
        dx, dg = _rms_bwd(x_ref[...], g_ref[...], dh)
        dx_ref[...] = dx + res_ref[...]

        @pl.when(pl.program_id(0) == 0)
        def _():
            dg_ref[...] = jnp.zeros_like(dg_ref)

        dg_ref[...] += dg

    a_specs = [pl.BlockSpec((tm, p), lambda i, cb=cb: (i, cb)) for _, cb in a_parts]
    b_specs = [pl.BlockSpec((1,) + b.shape[1:], lambda i, s=s: (s, 0, 0)) for s in range(parts)]
    row = pl.BlockSpec((tm, d), lambda i: (i, 0))
    vec = pl.BlockSpec((1, d), lambda i: (0, 0))
    return pl.pallas_call(
        body, name=name, grid=(n // tm,), in_specs=a_specs + b_specs + [row, vec, row] + after_specs,
        out_specs=[row, vec], out_shape=[SDS((n, d), F32), SDS((1, d), F32)],
        compiler_params=_cp("arbitrary"))(*[arr for arr, _ in a_parts], *([b] * parts), x, g, res, *after)


def _dw_by_owner(a, b, tn, first, into, tm, name):
    k, m = a.shape
    cnt = b.shape[1] // tn
    tm = min(tm, m)
    assert m % tm == 0 and b.shape[1] == cnt * tn and first + cnt <= 4, (name, a.shape, b.shape)

    def body(a_ref, b_ref, *rest):
        rest[-1][0] = _dot(a_ref[...], b_ref[...], "tn").astype(rest[-1].dtype)

    extra = [] if into is None else [into]
    return pl.pallas_call(
        body, name=name, grid=(m // tm, cnt),
        in_specs=[pl.BlockSpec((k, tm), lambda i, j: (0, i)), pl.BlockSpec((k, tn), lambda i, j: (0, j))] + [pl.BlockSpec(memory_space=pl.ANY)] * len(extra),
        out_specs=pl.BlockSpec((1, tm, tn), lambda i, j: (first + j, i, 0)), out_shape=SDS((4, m, tn), WIRE_DTYPE),
        input_output_aliases={2: 0} if extra else {},
        compiler_params=_cp("parallel", "parallel"))(a, b, *extra)


def _norm_mm(x, g, w, out_dtype, tm, tn, name, after=(), w_turned=False):
    after, after_specs = _unread(after)
    m, d = x.shape
    sharded = w.ndim == 3
    n = w.shape[0] if w_turned else w.shape[-1] * (w.shape[0] if sharded else 1)
    tm, tn = min(tm, m), (w.shape[-1] if sharded else min(tn, n))
    assert m % tm == 0 and n % tn == 0 and not (sharded and w_turned), (name, m, n, tm, tn)

    def body(x_ref, g_ref, w_ref, *rest):
        o_ref, h_ref, hs = rest[-3:]

        @pl.when(pl.program_id(1) == 0)
        def _():
            xv = x_ref[...]
            h = (xv * _rstd(xv) * g_ref[...]).astype(MXU_DTYPE)
            hs[...] = h
            h_ref[...] = h

        o_ref[...] = _dot(hs[...], w_ref[0] if sharded else w_ref[...], "nt" if w_turned else "nn").astype(o_ref.dtype)

    if w_turned:
        w_spec = pl.BlockSpec((tn, d), lambda i, j: (j, 0))
    else:
        w_spec = pl.BlockSpec((1, d, tn), lambda i, j: (j, 0, 0)) if sharded else pl.BlockSpec((d, tn), lambda i, j: (0, j))
    return pl.pallas_call(
        body, name=name, grid=(m // tm, n // tn),
        in_specs=[pl.BlockSpec((tm, d), lambda i, j: (i, 0)), pl.BlockSpec((1, d), lambda i, j: (0, 0)), w_spec] + after_specs,
        out_specs=[pl.BlockSpec((tm, tn), lambda i, j: (i, j)), pl.BlockSpec((tm, d), lambda i, j: (i, 0))],
        out_shape=[SDS((m, n), out_dtype), SDS((m, d), MXU_DTYPE)],
        scratch_shapes=[pltpu.VMEM((tm, d), MXU_DTYPE)],
        compiler_params=_cp("parallel", "arbitrary"))(x, g, w, *after)


ROW_TILE = 512
TOKEN_TILE = 1024


def _norm_bwd(x, g, dy, res, out_dtype, name):
    n, d = x.shape
    tr = min(ROW_TILE, n)
    has_res = res is not None

    def body(*refs):
        x_ref, g_ref, dy_ref = refs[:3]
        dx_ref, dg_ref = refs[-2:]
        dx, dg = _rms_bwd(x_ref[...], g_ref[...], dy_ref[...].astype(F32))
        if has_res:
            dx = dx + refs[3][...]
        dx_ref[...] = dx.astype(dx_ref.dtype)

        @pl.when(pl.program_id(0) == 0)
        def _():
            dg_ref[...] = jnp.zeros_like(dg_ref)

        dg_ref[...] += dg

    row = pl.BlockSpec((tr, d), lambda i: (i, 0))
    vec = pl.BlockSpec((1, d), lambda i: (0, 0))
    ins = [x, g, dy] + ([res] if has_res else [])
    return pl.pallas_call(
        body, name=name, grid=(n // tr,), in_specs=[row, vec, row] + ([row] if has_res else []),
        out_specs=[row, vec], out_shape=[SDS((n, d), out_dtype), SDS((1, d), F32)],
        compiler_params=_cp("arbitrary"))(*ins)


def _rope_tables(n):
    pairs = ATT_HEAD_DIM // 4
    t = np.arange(n)
    inv = np.power(ROPE_THETA, -np.arange(pairs, dtype=np.float32) / pairs).astype(np.float32)
    ang = np.concatenate([(t // GRID_W)[:, None].astype(np.float32) * inv, (t % GRID_W)[:, None].astype(np.float32) * inv], axis=-1)
    cos = np.repeat(np.cos(ang), 2, axis=-1)
    sin = np.repeat(np.sin(ang), 2, axis=-1) * np.tile(np.array([-1.0, 1.0], np.float32), ATT_HEAD_DIM // 2)
    return jnp.asarray(np.tile(cos, 2), F32), jnp.asarray(np.tile(sin, 2), F32)


def _swap_pairs(x):
    lane = lax.broadcasted_iota(jnp.int32, x.shape, 1)
    return jnp.where((lane & 1) == 0, pltpu.roll(x, 127, axis=1), pltpu.roll(x, 1, axis=1))


def _head_mean(v):
    lane = lax.broadcasted_iota(jnp.int32, v.shape, 1)
    lo = jnp.where(lane < ATT_HEAD_DIM, v, 0.0)
    s0 = jnp.sum(lo, axis=-1, keepdims=True)
    s1 = jnp.sum(v - lo, axis=-1, keepdims=True)
    return jnp.where(lane < ATT_HEAD_DIM, s0, s1) * (1.0 / ATT_HEAD_DIM)


def _qk_prep(p, gq, gk, cos, sin, name):
    n = p.shape[0]
    tr = min(ROW_TILE, n)

    def one(xv, g, c, s):
        xn = xv * lax.rsqrt(_head_mean(xv * xv) + EPS) * g
        return xn * c + _swap_pairs(xn) * s

    def body(q_ref, k_ref, gq_ref, gk_ref, c_ref, s_ref, qo_ref, ko_ref):
        c, s = c_ref[...], s_ref[...]
        for j in range(ATT_Q_DIM // 128):
            qo_ref[:, j * 128:(j + 1) * 128] = one(q_ref[:, j * 128:(j + 1) * 128], gq_ref[...], c, s).astype(qo_ref.dtype)
        ko_ref[...] = one(k_ref[...], gk_ref[...], c, s).astype(ko_ref.dtype)

    vec = pl.BlockSpec((1, 128), lambda i: (0, 0))
    tab = pl.BlockSpec((tr, 128), lambda i: (i, 0))
    return pl.pallas_call(
        body, name=name, grid=(n // tr,),
        in_specs=[pl.BlockSpec((tr, ATT_Q_DIM), lambda i: (i, 0)), pl.BlockSpec((tr, 128), lambda i: (i, OFF_AK // 128)), vec, vec, tab, tab],
        out_specs=[pl.BlockSpec((tr, ATT_Q_DIM), lambda i: (i, 0)), tab],
        out_shape=[SDS((n, ATT_Q_DIM), MXU_DTYPE), SDS((n, ATT_KV_DIM), MXU_DTYPE)],
        compiler_params=_cp("parallel"))(p, p, gq, gk, cos, sin)


def _qk_prep_bwd(p, gq, gk, cos, sin, dq, dk, name):
    n = p.shape[0]
    tr = min(ROW_TILE, n)

    def one(xv, g, c, s, dout):
        dxn = dout * c + _swap_pairs(dout * s)
        r = lax.rsqrt(_head_mean(xv * xv) + EPS)
        xh = xv * r
        dn = dxn * g
        dx = r * (dn - xh * _head_mean(dn * xh))
        return dx, jnp.sum(dxn * xh, axis=0, keepdims=True)

    def body(q_ref, k_ref, gq_ref, gk_ref, c_ref, s_ref, dq_ref, dk_ref, dqo_ref, dko_ref, dgq_ref, dgk_ref):
        @pl.when(pl.program_id(0) == 0)
        def _():
            dgq_ref[...] = jnp.zeros_like(dgq_ref)
            dgk_ref[...] = jnp.zeros_like(dgk_ref)

        c, s = c_ref[...], s_ref[...]
        for j in range(ATT_Q_DIM // 128):
            sl = slice(j * 128, (j + 1) * 128)
            dx, dg = one(q_ref[:, sl], gq_ref[...], c, s, dq_ref[:, sl])
            dqo_ref[:, sl] = dx.astype(dqo_ref.dtype)
            dgq_ref[:, sl] += dg
        dx, dg = one(k_ref[...], gk_ref[...], c, s, dk_ref[...])
        dko_ref[...] = dx.astype(dko_ref.dtype)
        dgk_ref[...] += dg

    vec = pl.BlockSpec((1, 128), lambda i: (0, 0))
    tab = pl.BlockSpec((tr, 128), lambda i: (i, 0))
    qrow = pl.BlockSpec((tr, ATT_Q_DIM), lambda i: (i, 0))
    return pl.pallas_call(
        body, name=name, grid=(n // tr,),
        in_specs=[qrow, pl.BlockSpec((tr, 128), lambda i: (i, OFF_AK // 128)), vec, vec, tab, tab, qrow, tab],
        out_specs=[qrow, tab, pl.BlockSpec((1, ATT_Q_DIM), lambda i: (0, 0)), vec],
        out_shape=[SDS((n, ATT_Q_DIM), MXU_DTYPE), SDS((n, ATT_KV_DIM), MXU_DTYPE), SDS((1, ATT_Q_DIM), F32), SDS((1, 128), F32)],
        compiler_params=_cp("arbitrary"))(p, p, gq, gk, cos, sin, dq, dk)


ATT_FWD_STEP = (256, 4)
ATT_BWD_STEP = (512, 2)


def _attn_fwd(q, k, v, name):
    n = q.shape[0]
    tq, step_heads = min(ATT_FWD_STEP[0], n), ATT_FWD_STEP[1]
    scale = ATT_HEAD_DIM ** -0.5
    gw = step_heads * ATT_HEAD_DIM
    parts = ATT_GROUP // step_heads

    def body(q_ref, k_ref, v_ref, o_ref):
        kk, vv = k_ref[0], v_ref[0]
        v_ones = jnp.concatenate([vv, jnp.ones_like(vv)], axis=1)
        outs = []
        for g in range(step_heads):
            s = _dot(q_ref[:, g * ATT_HEAD_DIM:(g + 1) * ATT_HEAD_DIM] * scale, kk, "nt")
            e = jnp.exp(s - jnp.max(s, axis=-1, keepdims=True))
            ov = _dot(e, v_ones)
            outs.append(ov[:, :ATT_HEAD_DIM] / ov[:, ATT_HEAD_DIM:])
        o_ref[...] = jnp.concatenate(outs, axis=-1).astype(o_ref.dtype)

    kv = pl.BlockSpec((1, n, ATT_HEAD_DIM), lambda h, i, pr: (h, 0, 0))
    qb = pl.BlockSpec((tq, gw), lambda h, i, pr: (i, h * parts + pr))
    return pl.pallas_call(
        body, name=name, grid=(ATT_KV_HEADS, n // tq, parts), in_specs=[qb, kv, kv],
        out_specs=qb, out_shape=SDS((n, ATT_Q_DIM), MXU_DTYPE),
        compiler_params=_cp("parallel", "parallel", "parallel"))(q, k, v)


def _attn_bwd(q, k, v, o, do, name):
    n = q.shape[0]
    tq, step_heads = min(ATT_BWD_STEP[0], n), ATT_BWD_STEP[1]
    scale = ATT_HEAD_DIM ** -0.5
    gw = step_heads * ATT_HEAD_DIM
    parts = ATT_GROUP // step_heads

    def body(q_ref, k_ref, v_ref, o_ref, do_ref, dq_ref, dk_ref, dv_ref):
        @pl.when(jnp.logical_and(pl.program_id(1) == 0, pl.program_id(2) == 0))
        def _():
            dk_ref[...] = jnp.zeros_like(dk_ref)
            dv_ref[...] = jnp.zeros_like(dv_ref)

        kk, vv = k_ref[0], v_ref[0]
        dqs = []
        dk_acc = jnp.zeros((ATT_HEAD_DIM, n), F32)
        dv_acc = jnp.zeros((ATT_HEAD_DIM, n), F32)
        for g in range(step_heads):
            sl = slice(g * ATT_HEAD_DIM, (g + 1) * ATT_HEAD_DIM)
            qg, dog = q_ref[:, sl] * scale, do_ref[:, sl].astype(F32)
            s = _dot(qg, kk, "nt")
            e = jnp.exp(s - jnp.max(s, axis=-1, keepdims=True))
            inv = 1.0 / jnp.sum(e, axis=-1, keepdims=True)
            delta = jnp.sum(dog * o_ref[:, sl].astype(F32), axis=-1, keepdims=True)
            dse = e * (_dot(dog, vv, "nt") - delta)
            dqs.append(_dot(dse, kk) * (inv * scale))
            dk_acc += _dot(qg.astype(F32) * inv, dse, "tn")
            dv_acc += _dot(dog * inv, e, "tn")
        dq_ref[...] = jnp.concatenate(dqs, axis=-1)
        dk_ref[0] += dk_acc
        dv_ref[0] += dv_acc

    kv = pl.BlockSpec((1, n, ATT_HEAD_DIM), lambda h, i, pr: (h, 0, 0))
    kvt = pl.BlockSpec((1, ATT_HEAD_DIM, n), lambda h, i, pr: (h, 0, 0))
    qb = pl.BlockSpec((tq, gw), lambda h, i, pr: (i, h * parts + pr))
    return pl.pallas_call(
        body, name=name, grid=(ATT_KV_HEADS, n // tq, parts), in_specs=[qb, kv, kv, qb, qb], out_specs=[qb, kvt, kvt],
        out_shape=[SDS((n, ATT_Q_DIM), F32), SDS((ATT_KV_HEADS, ATT_HEAD_DIM, n), F32), SDS((ATT_KV_HEADS, ATT_HEAD_DIM, n), F32)],
        compiler_params=_cp("parallel", "arbitrary", "arbitrary"))(q, k, v, o, do)


def _both_directions(mats, axis):
    fwd = np.concatenate(mats, axis=axis).astype(np.float32)
    bwd = np.concatenate([m[::-1, ::-1] for m in mats], axis=axis).astype(np.float32)
    return jnp.asarray(np.stack([fwd, bwd]), MXU_DTYPE)


def _hg_segments():
    c = HG_CHUNK
    t = np.arange(c)[:, None]
    r = np.arange(c)[None, :]
    mats = [(r <= t)]
    for lev in range(HG_LEVELS):
        h = c >> (lev + 1)
        mid = (t // (2 * h)) * (2 * h) + h - 1
        hi = (t // h) % 2 == 1
        mats.append(np.where(hi, (r > mid) & (r <= t), (r > t) & (r <= mid)))
    mats.append(r > t)
    return _both_directions(mats, 0)


def _hg_pair_sums():
    c = HG_CHUNK
    r = np.arange(c)[:, None]
    t = np.arange(c)[None, :]
    gp, gn = [t >= r], [t < r]
    for lev in range(HG_LEVELS):
        sh = HG_LEVELS - 1 - lev
        same = (r >> sh) == (t >> sh)
        gp.append(same & (t >= r))
        gn.append(same & (t < r))
    return _both_directions(gp, 1), _both_directions(gn, 1)


def _split_dot(mat, x):
    hi = x.astype(MXU_DTYPE)
    lo = (x - hi.astype(F32)).astype(MXU_DTYPE)
    return _dot(mat, hi) + _dot(mat, lo)


def _hg_gates(hq, z, a0, a1):
    q = hq * _sigmoid(hq)
    sg = _sigmoid(z)
    lb = _sigmoid(a0 - a1)
    f = lb + (1.0 - lb) * sg
    k = (1.0 - lb) * (1.0 - sg)
    return q, f, k, sg, lb


def _hg_level_masks():
    c = HG_CHUNK
    t = np.arange(c)
    later, same = [], []
    for lev in range(HG_LEVELS):
        sh = HG_LEVELS - 1 - lev
        later.append(np.broadcast_to((((t >> sh) & 1) == 1)[:, None], (c, HG_HEAD_DIM)))
        same.append((t[:, None] >> (sh + 1)) == (t[None, :] >> (sh + 1)))
    same.append(t[:, None] == t[None, :])
    later = np.stack(later).astype(np.float32)
    return jnp.asarray(np.stack([later, 1.0 - later]), F32), jnp.asarray(np.stack(same).astype(np.float32), F32)


def _hg_level(q, k, ex, later_ref, lev):
    e = ex[lev + 1]
    e_q = e * later_ref[0, lev]
    e_k = e - e_q
    return q * e_q, k * e_k, e_q, e_k


def _hg_intra(q, k, ex, later_ref, same_ref):
    a = same_ref[HG_LEVELS] * jnp.sum(q * k, axis=-1, keepdims=True)
    for lev in range(HG_LEVELS):
        qs, ks, _, _ = _hg_level(q, k, ex, later_ref, lev)
        a = a + same_ref[lev] * _dot(qs, ks, "nt")
    return a


def _hg_specs(n, with_time):
    c = HG_CHUNK
    nc = n // c

    def chunk(d, i):
        first = d if with_time else 1 - d
        return i + first * (nc - 1 - 2 * i)

    def pcols(off, dir_stride=0):
        return [pl.BlockSpec((c, HG_PAIR), lambda d, i, j=j: (chunk(d, i), off // HG_PAIR + dir_stride // HG_PAIR * d + j)) for j in range(2)]

    specs = dict(
        hq=pcols(OFF_HQ), v=pcols(OFF_HI), z=pcols(OFF_ZF, OFF_ZB - OFF_ZF),
        shared=pl.BlockSpec((c, HG_DIM), lambda d, i: (chunk(d, i), 0)),
        per_dir=pl.BlockSpec((1, c, HG_DIM), lambda d, i: (d, chunk(d, i), 0)),
        vec=pl.BlockSpec((1, 1, HG_DIM), lambda d, i: (d, 0, 0)),
        seg=pl.BlockSpec((1, (HG_LEVELS + 2) * c, c), lambda d, i: (d, 0, 0)),
        sums=pl.BlockSpec((1, c, (HG_LEVELS + 1) * c), lambda d, i: (d, 0, 0)),
        later=pl.BlockSpec((1, HG_LEVELS, c, HG_HEAD_DIM), lambda d, i: (d, 0, 0, 0)),
        same=pl.BlockSpec((HG_LEVELS + 1, c, c), lambda d, i: (0, 0, 0)),
        state=pl.BlockSpec((1, HG_HEADS, 1, HG_HEAD_DIM, HG_HEAD_DIM), lambda d, i: (d, 0, chunk(d, i), 0, 0)),
        weights=pl.BlockSpec((1, HG_HEADS, 1, c, c), lambda d, i: (d, 0, chunk(d, i), 0, 0)),
        levels=pl.BlockSpec((1, HG_HEADS, 1, HG_LEVELS, c, HG_HEAD_DIM), lambda d, i: (d, 0, chunk(d, i), 0, 0, 0)),
        kept=pl.BlockSpec((1, HG_KEPT, c, HG_DIM), lambda d, i: (d, 0, chunk(d, i), 0)))
    return nc, specs


def _hg_head(refs, hh):
    off = (hh % 2) * HG_HEAD_DIM
    return refs[hh // 2][:, off:off + HG_HEAD_DIM]


def _hg_lanes(hh):
    return slice(hh * HG_HEAD_DIM, (hh + 1) * HG_HEAD_DIM)


def _hg_exps(seg_ref, f):
    c = HG_CHUNK
    args = _split_dot(seg_ref[0], jnp.log(f))
    return [jnp.exp(args[j * c:(j + 1) * c]) for j in range(HG_LEVELS + 2)]


def _hg_last_row(a, mirrored):
    return jnp.where(mirrored, a[0:1, :], a[HG_CHUNK - 1:HG_CHUNK, :])


def _hgrn_fwd(p, a0, a1, seg, masks, name):
    n = p.shape[0]
    nc, sp = _hg_specs(n, True)

    def body(hq0, hq1, z0, z1, v0, v1, a0_ref, a1_ref, seg_ref, later_ref, same_ref, o_ref, s0_ref, a_ref, e_ref, g_ref, st):
        @pl.when(pl.program_id(1) == 0)
        def _():
            st[...] = jnp.zeros_like(st)

        mirrored = pl.program_id(0) == 1
        for hh in range(HG_HEADS):
            ln = _hg_lanes(hh)
            hqv = _hg_head((hq0, hq1), hh)
            q, f, k, sg, _ = _hg_gates(hqv, _hg_head((z0, z1), hh), a0_ref[0, :, ln], a1_ref[0, :, ln])
            vv = _hg_head((v0, v1), hh)
            ex = _hg_exps(seg_ref, f)
            for lev in range(HG_LEVELS):
                e_ref[0, hh, 0, lev] = ex[lev + 1].astype(e_ref.dtype)
            sq = _sigmoid(hqv)
            for j, kept in enumerate((q, k, f, sg, sq * (1.0 + hqv * (1.0 - sq)), ex[0], ex[HG_LEVELS + 1])):
                g_ref[0, j, :, ln] = kept
            a = _hg_intra(q, k, ex, later_ref, same_ref).astype(MXU_DTYPE)
            a_ref[0, hh, 0] = a
            s_t = st[hh]
            s0_ref[0, hh, 0] = s_t
            o_ref[0, :, ln] = _dot(a, vv) + _dot(q * ex[0], s_t, "nt")
            st[hh] = s_t * _hg_last_row(ex[0], mirrored) + _dot(vv, k * ex[HG_LEVELS + 1], "tn")

    return pl.pallas_call(
        body, name=name, grid=(2, nc), in_specs=sp["hq"] + sp["z"] + sp["v"] + [sp["vec"], sp["vec"], sp["seg"], sp["later"], sp["same"]],
        out_specs=[sp["per_dir"], sp["state"], sp["weights"], sp["levels"], sp["kept"]],
        out_shape=[SDS((2, n, HG_DIM), F32), SDS((2, HG_HEADS, nc, HG_HEAD_DIM, HG_HEAD_DIM), F32),
                   SDS((2, HG_HEADS, nc, HG_CHUNK, HG_CHUNK), MXU_DTYPE),
                   SDS((2, HG_HEADS, nc, HG_LEVELS, HG_CHUNK, HG_HEAD_DIM), MXU_DTYPE), SDS((2, HG_KEPT, n, HG_DIM), F32)],
        scratch_shapes=[pltpu.VMEM((HG_HEADS, HG_HEAD_DIM, HG_HEAD_DIM), F32)],
        compiler_params=_cp("parallel", "arbitrary"))(p, p, p, p, p, p, a0, a1, seg, *masks)


def _hgrn_bwd(p, a0, a1, masks, gp, gn, do, s0, a, e, kept, name):
    n = p.shape[0]
    nc, sp = _hg_specs(n, False)


    def body(v0, v1, a0_ref, a1_ref, later_ref, same_ref, gp_ref, gn_ref, do_ref, s0_ref, a_ref, e_ref, g_ref,
             dhq_ref, dz_ref, dv_ref, dlb_ref, rt):
        @pl.when(pl.program_id(1) == 0)
        def _():
            rt[...] = jnp.zeros_like(rt)
            dlb_ref[...] = jnp.zeros_like(dlb_ref)

        mirrored = pl.program_id(0) == 1
        for hh in range(HG_HEADS):
            ln = _hg_lanes(hh)
            q, k, f, sg, dsilu, e_first, e_last = (g_ref[0, j, :, ln] for j in range(HG_KEPT))
            lb = _sigmoid(a0_ref[0, :, ln] - a1_ref[0, :, ln])
            vv, dov = _hg_head((v0, v1), hh), do_ref[:, ln]
            ex = [e_first] + [e_ref[0, hh, 0, lev].astype(F32) for lev in range(HG_LEVELS)] + [e_last]
            a = a_ref[0, hh, 0]
            da = _dot(dov, vv, "nt")
            diag = jnp.sum(dov * vv, axis=-1, keepdims=True)
            s_t = s0_ref[0, hh, 0]
            r_t = rt[hh]
            k_end = k * ex[HG_LEVELS + 1]
            dv_ref[0, :, ln] = _dot(a, dov, "tn") + _dot(k_end, r_t, "nt")
            dq_inter = ex[0] * _dot(dov, s_t)
            dk_inter = ex[HG_LEVELS + 1] * _dot(vv, r_t)
            dq = diag * k + dq_inter
            dk = diag * q + dk_inter
            q_terms, k_terms = [q * dq_inter], [k * dk_inter]
            for lev in range(HG_LEVELS):
                qs, ks, e_q, e_k = _hg_level(q, k, ex, later_ref, lev)
                pairs = da * same_ref[lev]
                q_part = e_q * _dot(pairs, ks)
                k_part = e_k * _dot(pairs, qs, "tn")
                dq, dk = dq + q_part, dk + k_part
                q_terms.append(q * q_part)
                k_terms.append(k * k_part)
            decay = _hg_last_row(ex[0], mirrored)
            rt[hh] = r_t * decay + _dot(dov, q * ex[0], "tn")
            later = decay * jnp.sum(s_t * r_t, axis=0, keepdims=True)
            dlf = _dot(gp_ref[0], jnp.concatenate(q_terms, axis=0)) + _dot(gn_ref[0], jnp.concatenate(k_terms, axis=0)) + later
            df = dlf / f - dk
            dz_ref[0, :, ln] = df * (1.0 - lb) * sg * (1.0 - sg)
            dlb_ref[0, :, ln] += jnp.sum(df * (1.0 - sg), axis=0, keepdims=True)
            dhq_ref[0, :, ln] = dq * dsilu

    out = SDS((2, n, HG_DIM), F32)
    return pl.pallas_call(
        body, name=name, grid=(2, nc),
        in_specs=sp["v"] + [sp["vec"], sp["vec"], sp["later"], sp["same"], sp["sums"], sp["sums"],
                            sp["shared"], sp["state"], sp["weights"], sp["levels"], sp["kept"]],
        out_specs=[sp["per_dir"], sp["per_dir"], sp["per_dir"], sp["vec"]], out_shape=[out, out, out, SDS((2, 1, HG_DIM), F32)],
        scratch_shapes=[pltpu.VMEM((HG_HEADS, HG_HEAD_DIM, HG_HEAD_DIM), F32)],
        compiler_params=_cp("parallel", "arbitrary"))(p, p, a0, a1, *masks, gp, gn, do, s0, a, e, kept)


def _hg_post(o2, p, g, name):
    n = p.shape[0]
    tr = min(ROW_TILE, n)
    w = 2 * HG_HEAD_DIM

    def body(of_ref, ob_ref, hg_ref, g_ref, o_ref):
        for j in range(2):
            sl = slice(j * HG_HEAD_DIM, (j + 1) * HG_HEAD_DIM)
            o = of_ref[0, :, sl] + ob_ref[0, :, sl]
            hg = hg_ref[:, sl]
            o_ref[:, sl] = (o * _rstd(o) * g_ref[...] * (hg * _sigmoid(hg))).astype(o_ref.dtype)

    blk = pl.BlockSpec((tr, w), lambda i, j: (i, j))
    dirs = [pl.BlockSpec((1, tr, w), lambda i, j, d=d: (d, i, j)) for d in range(2)]
    return pl.pallas_call(
        body, name=name, grid=(n // tr, HG_DIM // w),
        in_specs=dirs + [pl.BlockSpec((tr, w), lambda i, j: (i, OFF_HG // w + j)), pl.BlockSpec((1, HG_HEAD_DIM), lambda i, j: (0, 0))],
        out_specs=blk, out_shape=SDS((n, HG_DIM), MXU_DTYPE), compiler_params=_cp("parallel", "parallel"))(o2, o2, p, g)


def _hg_post_bwd(o2, p, g, dcat, name, after=()):
    n = p.shape[0]
    tr = min(ROW_TILE, n)
    w = 2 * HG_HEAD_DIM
    after, after_specs = _unread(after)

    def body(of_ref, ob_ref, hg_ref, g_ref, d_ref, *rest):
        do_ref, dhg_ref, dg_ref = rest[len(after):]

        @pl.when(pl.program_id(1) == 0)
        def _():
            dg_ref[...] = jnp.zeros_like(dg_ref)

        for j in range(2):
            sl = slice(j * HG_HEAD_DIM, (j + 1) * HG_HEAD_DIM)
            o = of_ref[0, :, sl] + ob_ref[0, :, sl]
            hg = hg_ref[:, sl]
            d = d_ref[:, sl].astype(F32)
            sg = _sigmoid(hg)
            on = o * _rstd(o) * g_ref[...]
            dhg_ref[:, sl] = (d * on * sg * (1.0 + hg * (1.0 - sg))).astype(dhg_ref.dtype)
            dx, dg = _rms_bwd(o, g_ref[...], d * hg * sg)
            do_ref[:, sl] = dx
            dg_ref[0, :, sl] += dg

    blk = pl.BlockSpec((tr, w), lambda j, i: (i, j))
    dirs = [pl.BlockSpec((1, tr, w), lambda j, i, d=d: (d, i, j)) for d in range(2)]
    return pl.pallas_call(
        body, name=name, grid=(HG_DIM // w, n // tr),
        in_specs=dirs + [pl.BlockSpec((tr, w), lambda j, i: (i, OFF_HG // w + j)), pl.BlockSpec((1, HG_HEAD_DIM), lambda j, i: (0, 0)),
                         pl.BlockSpec((tr, w), lambda j, i: (i, ATT_Q_DIM // w + j))] + after_specs,
        out_specs=[blk, blk, pl.BlockSpec((1, 1, w), lambda j, i: (j, 0, 0))],
        out_shape=[SDS((n, HG_DIM), F32), SDS((n, HG_DIM), MXU_DTYPE), SDS((HG_DIM // w, 1, w), F32)],
        compiler_params=_cp("parallel", "arbitrary"))(o2, o2, p, g, dcat, *after)


XATT_TQ = 512


def _xattn_fwd(q, kv, name):
    n, nm = q.shape[0], kv.shape[0]
    tq = min(XATT_TQ, n)
    scale = X_HEAD_DIM ** -0.5

    def body(q_ref, k_ref, v_ref, o_ref):
        s = _dot(q_ref[...], k_ref[...], "nt") * scale
        e = jnp.exp(s - jnp.max(s, axis=-1, keepdims=True))
        o_ref[...] = _dot(e / jnp.sum(e, axis=-1, keepdims=True), v_ref[...]).astype(o_ref.dtype)

    qb = pl.BlockSpec((tq, X_HEAD_DIM), lambda h, i: (i, h))
    return pl.pallas_call(
        body, name=name, grid=(X_HEADS, n // tq),
        in_specs=[qb, pl.BlockSpec((nm, X_HEAD_DIM), lambda h, i: (0, h)), pl.BlockSpec((nm, X_HEAD_DIM), lambda h, i: (0, X_HEADS + h))],
        out_specs=qb, out_shape=SDS(q.shape, MXU_DTYPE), compiler_params=_cp("parallel", "parallel"))(q, kv, kv)


def _xattn_bwd(q, kv, do, name, after=()):
    n, nm = q.shape[0], kv.shape[0]
    tq = min(XATT_TQ, n)
    scale = X_HEAD_DIM ** -0.5
    after, after_specs = _unread(after)

    def body(q_ref, k_ref, v_ref, do_ref, *rest):
        dq_ref, dk_ref, dv_ref = rest[len(after):]

        @pl.when(pl.program_id(1) == 0)
        def _():
            dk_ref[...] = jnp.zeros_like(dk_ref)
            dv_ref[...] = jnp.zeros_like(dv_ref)

        qv, dov = q_ref[...], do_ref[...]
        s = _dot(qv, k_ref[...], "nt") * scale
        e = jnp.exp(s - jnp.max(s, axis=-1, keepdims=True))
        p = e / jnp.sum(e, axis=-1, keepdims=True)
        dp = _dot(dov, v_ref[...], "nt")
        ds = p * (dp - jnp.sum(p * dp, axis=-1, keepdims=True)) * scale
        dq_ref[...] = _dot(ds, k_ref[...]).astype(dq_ref.dtype)
        dk_ref[...] += _dot(ds, qv, "tn")
        dv_ref[...] += _dot(p, dov, "tn")

    qb = pl.BlockSpec((tq, X_HEAD_DIM), lambda h, i: (i, h))
    kb = pl.BlockSpec((nm, X_HEAD_DIM), lambda h, i: (0, h))
    return pl.pallas_call(
        body, name=name, grid=(X_HEADS, n // tq),
        in_specs=[qb, kb, pl.BlockSpec((nm, X_HEAD_DIM), lambda h, i: (0, X_HEADS + h)), qb] + after_specs, out_specs=[qb, kb, kb],
        out_shape=[SDS(q.shape, MXU_DTYPE), SDS((nm, X_HEADS * X_HEAD_DIM), F32), SDS((nm, X_HEADS * X_HEAD_DIM), F32)],
        compiler_params=_cp("parallel", "arbitrary"))(q, kv, kv, do, *after)


def _edge_rows(shape):
    row = lax.broadcasted_iota(jnp.int32, shape, 0)
    return row == 0, row == shape[0] - 1


def _shift_rows(u, down, edges):
    if down:
        return jnp.where(edges[0], 0.0, pltpu.roll(u, 1, axis=0))
    return jnp.where(edges[1], 0.0, pltpu.roll(u, u.shape[0] - 1, axis=0))


def _conv(u, w, b, edges):
    return b + _shift_rows(u, True, edges) * w[0:1, :] + u * w[1:2, :] + _shift_rows(u, False, edges) * w[2:3, :]


def _ff_specs(n):
    gate = lambda rows: pl.BlockSpec((rows, FF_COLS), lambda j: (0, j))
    val = lambda rows: pl.BlockSpec((rows, FF_COLS), lambda j: (0, FF_BLOCKS + j))
    return [gate(n), val(n), gate(3), val(3), gate(1), val(1)], gate


def _conv_gate(u, cw, cb, name):
    n = u.shape[0]
    ins, gate_blk = _ff_specs(n)

    def body(ug_ref, uv_ref, wg_ref, wv_ref, bg_ref, bv_ref, o_ref):
        edges = _edge_rows(ug_ref.shape)
        gate = _conv(ug_ref[...], wg_ref[...], bg_ref[...], edges)
        val = _conv(uv_ref[...], wv_ref[...], bv_ref[...], edges)
        o_ref[...] = (gate * _sigmoid(gate) * val).astype(o_ref.dtype)

    return pl.pallas_call(
        body, name=name, grid=(FF_BLOCKS,), in_specs=ins, out_specs=gate_blk(n), out_shape=SDS((n, D_FF), MXU_DTYPE),
        compiler_params=_cp("parallel"))(u, u, cw, cw, cb, cb)


def _conv_gate_bwd(u, cw, cb, da, name, after=()):
    n = u.shape[0]
    ins, gate_blk = _ff_specs(n)
    after, after_specs = _unread(after)

    def side(dacc, u, w, edges, du_ref, dw_ref, db_ref):
        nxt, prv = _shift_rows(dacc, False, edges), _shift_rows(dacc, True, edges)
        du_ref[...] = (nxt * w[0:1, :] + dacc * w[1:2, :] + prv * w[2:3, :]).astype(du_ref.dtype)
        db_ref[...] = jnp.sum(dacc, axis=0, keepdims=True)
        dw_ref[0:1, :] = jnp.sum(nxt * u, axis=0, keepdims=True)
        dw_ref[1:2, :] = jnp.sum(dacc * u, axis=0, keepdims=True)
        dw_ref[2:3, :] = jnp.sum(prv * u, axis=0, keepdims=True)

    def body(ug_ref, uv_ref, wg_ref, wv_ref, bg_ref, bv_ref, da_ref, *rest):
        dug_ref, duv_ref, dwg_ref, dwv_ref, dbg_ref, dbv_ref = rest[len(after):]
        ug, uv = ug_ref[...], uv_ref[...]
        edges = _edge_rows(ug.shape)
        gate = _conv(ug, wg_ref[...], bg_ref[...], edges)
        val = _conv(uv, wv_ref[...], bv_ref[...], edges)
        sg = _sigmoid(gate)
        dav = da_ref[...].astype(F32)
        side(dav * val * sg * (1.0 + gate * (1.0 - sg)), ug, wg_ref[...], edges, dug_ref, dwg_ref, dbg_ref)
        side(dav * gate * sg, uv, wv_ref[...], edges, duv_ref, dwv_ref, dbv_ref)

    return pl.pallas_call(
        body, name=name, grid=(FF_BLOCKS,), in_specs=ins + [gate_blk(n)] + after_specs,
        out_specs=[gate_blk(n), gate_blk(n), gate_blk(3), gate_blk(3), gate_blk(1), gate_blk(1)],
        out_shape=[SDS((n, D_FF), MXU_DTYPE)] * 2 + [SDS((3, D_FF), F32)] * 2 + [SDS((1, D_FF), F32)] * 2,
        compiler_params=_cp("parallel"))(u, u, cw, cw, cb, cb, da, *after)


def _adamw(w, g, m, v, name):
    r, c = w.shape[-2:]
    tr = _row_tile(r, ELEMENTWISE_ROWS)
    assert w.ndim == 2 or w.shape[:-2] == (1,), (name, w.shape)

    def body(w_ref, g_ref, m_ref, v_ref, d_ref, mo_ref, vo_ref, go_ref):
        gv = g_ref[...]
        go_ref[...] = gv
        mn = ADAM_B1 * m_ref[...] + (1.0 - ADAM_B1) * gv
        vn = ADAM_B2 * v_ref[...] + (1.0 - ADAM_B2) * gv * gv
        m_hat = mn / (1.0 - ADAM_B1 ** ADAM_STEP)
        v_hat = vn / (1.0 - ADAM_B2 ** ADAM_STEP)
        d_ref[...] = -ADAM_LR * (m_hat / (jnp.sqrt(v_hat) + ADAM_EPS) + ADAM_WD * w_ref[...])
        mo_ref[...] = mn
        vo_ref[...] = vn

    blk = pl.BlockSpec((tr, c), lambda i: (i, 0)) if w.ndim == 2 else pl.BlockSpec((1, tr, c), lambda i: (0, i, 0))
    out = SDS(w.shape, F32)
    return pl.pallas_call(body, name=name, grid=(r // tr,), in_specs=[blk] * 4, out_specs=[blk] * 4, out_shape=[out] * 4,
                          compiler_params=_cp("parallel"))(w, g, m, v)


ANY = pl.BlockSpec(memory_space=pl.ANY)


def _place():
    x, y, c = lax.axis_index("x"), lax.axis_index("y"), lax.axis_index("c")
    return x, y, c, [(1 - x, y), (x, 1 - y), (1 - x, 1 - y)]


def _join_halves(bufs, name):
    nt = len(bufs)

    def body(*refs):
        outs = refs[nt:2 * nt]
        send, recv = refs[2 * nt:]
        x, y, c, _ = _place()
        cps = [pltpu.make_async_remote_copy(src_ref=outs[t].at[c], dst_ref=outs[t].at[c], send_sem=send.at[t], recv_sem=recv.at[t],
                                            device_id=(x, y, 1 - c), device_id_type=MESH) for t in range(nt)]
        for cp in cps:
            cp.start()
        for t in range(nt):
            theirs = outs[t].at[1 - c]
            pltpu.make_async_remote_copy(src_ref=theirs, dst_ref=theirs, send_sem=send.at[t], recv_sem=recv.at[t],
                                         device_id=(x, y, 1 - c), device_id_type=MESH).wait_recv()
        for cp in cps:
            cp.wait_send()

    return pl.pallas_call(
        body, name=name, in_specs=[ANY] * nt, out_specs=[ANY] * nt, out_shape=[SDS(b.shape, b.dtype) for b in bufs],
        input_output_aliases={t: t for t in range(nt)},
        scratch_shapes=[pltpu.SemaphoreType.DMA((nt,))] * 2,
        compiler_params=pltpu.CompilerParams(has_side_effects=True))(*bufs)


def _exchange_small(v, reduce, name, after=()):
    rows = v.shape[0]
    after, after_specs = _unread(after)

    def body(v_ref, *rest):
        o_ref, buf, send, recv = rest[-4:]
        x, y, c, _ = _place()
        me = 4 * x + 2 * y + c
        buf[me] = v_ref[...]

        def peer(dx, dy, dc):
            return (1 - x if dx else x, 1 - y if dy else y, 1 - c if dc else c)

        peers = [(dx, dy, dc) for dx in range(2) for dy in range(2) for dc in range(2) if (dx, dy, dc) != (0, 0, 0)]
        cps = []
        for j, (dx, dy, dc) in enumerate(peers):
            cps.append(pltpu.make_async_remote_copy(src_ref=v_ref, dst_ref=buf.at[me], send_sem=send.at[j], recv_sem=recv.at[j],
                                                    device_id=peer(dx, dy, dc), device_id_type=MESH))
        for cp in cps:
            cp.start()
        for j, (dx, dy, dc) in enumerate(peers):
            px, py, pc = peer(dx, dy, dc)
            blk = buf.at[4 * px + 2 * py + pc]
            pltpu.make_async_remote_copy(src_ref=blk, dst_ref=blk, send_sem=send.at[j], recv_sem=recv.at[j],
                                         device_id=(px, py, pc), device_id_type=MESH).wait_recv()
        for cp in cps:
            cp.wait_send()
        if reduce:
            acc = buf[0]
            for j in range(1, 8):
                acc = acc + buf[j]
            o_ref[...] = acc
        else:
            o_ref[...] = buf[...]

    vm = pl.BlockSpec(memory_space=pltpu.VMEM)
    return pl.pallas_call(
        body, name=name, in_specs=[vm] + after_specs, out_specs=vm, out_shape=SDS((rows, 128) if reduce else (8, rows, 128), F32),
        scratch_shapes=[pltpu.VMEM((8, rows, 128), F32), pltpu.SemaphoreType.DMA((7,)), pltpu.SemaphoreType.DMA((7,))],
        compiler_params=pltpu.CompilerParams(has_side_effects=True))(v, *after)


HBM = pl.BlockSpec(memory_space=pltpu.HBM)
SEM = pl.BlockSpec(memory_space=pltpu.SEMAPHORE)
TOKEN = pl.BlockSpec(memory_space=pltpu.VMEM)
TOKEN_SHAPE = SDS((8, 128), F32)
PEERS = 7


def _in_hbm(a):
    return pltpu.with_memory_space_constraint(a, pltpu.HBM)


def _split_params():
    return pltpu.CompilerParams(has_side_effects=pltpu.SideEffectType.DATAFLOW_SIDE_EFFECTING)


def _gather_start(shards, name, after=()):
    nt = len(shards)
    after, after_specs = _unread(after)

    def body(*refs):
        ins, lands = refs[:nt], refs[nt:2 * nt]
        outs = refs[2 * nt + len(after):]
        sends, recvs = outs[:nt], outs[nt:2 * nt]
        x, y, c, chips = _place()
        me = 2 * x + y
        for t in range(nt):
            h = ins[t].shape[0] // 2
            mine = pl.ds(c * h, h)
            for j, (cx, cy) in enumerate(chips):
                for dc in range(2):
                    pltpu.make_async_remote_copy(src_ref=ins[t].at[mine], dst_ref=lands[t].at[me, mine], send_sem=sends[t].at[2 * j + dc],
                                                 recv_sem=recvs[t].at[2 * j + c], device_id=(cx, cy, dc), device_id_type=MESH).start()
            pltpu.make_async_remote_copy(src_ref=ins[t], dst_ref=lands[t].at[me], send_sem=sends[t].at[PEERS - 1], recv_sem=recvs[t].at[PEERS - 1],
                                         device_id=(x, y, 1 - c), device_id_type=MESH).start()
        outs[-1][...] = jnp.zeros(TOKEN_SHAPE.shape, F32)

    lands = [lax.empty((4,) + s.shape, s.dtype) for s in shards]
    out = pl.pallas_call(
        body, name=name, in_specs=[HBM] * (2 * nt) + after_specs, out_specs=[SEM] * (2 * nt) + [HBM] * (2 * nt) + [TOKEN],
        out_shape=[pltpu.SemaphoreType.DMA((PEERS,))] * (2 * nt)
        + [pltpu.HBM(s.shape, s.dtype) for s in shards] + [pltpu.HBM(l.shape, l.dtype) for l in lands] + [TOKEN_SHAPE],
        input_output_aliases={t: 2 * nt + t for t in range(2 * nt)}, compiler_params=_split_params())(
            *[_in_hbm(s) for s in shards], *[_in_hbm(l) for l in lands], *after)
    return out[:nt], out[nt:2 * nt], out[2 * nt:3 * nt], out[3 * nt:4 * nt], out[-1]


def _gather_wait(sends, recvs, shards, lands, after, name):
    nt = len(shards)

    def body(*refs):
        ins, lands_ref = refs[:nt], refs[nt:2 * nt]
        send_refs, recv_refs = refs[2 * nt:3 * nt], refs[3 * nt:4 * nt]
        x, y, c, chips = _place()
        for t in range(nt):
            h = ins[t].shape[0] // 2
            for j, (cx, cy) in enumerate(chips):
                for cs in range(2):
                    blk = lands_ref[t].at[2 * cx + cy, pl.ds(cs * h, h)]
                    pltpu.make_async_remote_copy(src_ref=blk, dst_ref=blk, send_sem=send_refs[t].at[2 * j + cs], recv_sem=recv_refs[t].at[2 * j + cs],
                                                 device_id=(cx, cy, cs), device_id_type=MESH).wait()
            blk = lands_ref[t].at[2 * x + y]
            pltpu.make_async_remote_copy(src_ref=blk, dst_ref=blk, send_sem=send_refs[t].at[PEERS - 1], recv_sem=recv_refs[t].at[PEERS - 1],
                                         device_id=(x, y, 1 - c), device_id_type=MESH).wait()

    out = pl.pallas_call(
        body, name=name, in_specs=[HBM] * (2 * nt) + [SEM] * (2 * nt) + [ANY], out_specs=[HBM] * (2 * nt),
        out_shape=[pltpu.HBM(s.shape, s.dtype) for s in shards] + [pltpu.HBM(l.shape, l.dtype) for l in lands],
        input_output_aliases={t: t for t in range(2 * nt)}, compiler_params=_split_params())(*shards, *lands, *sends, *recvs, after)
    return out[nt:]


def _gather_pieces_start(shard, name):
    h = shard.shape[0] // 2

    def body(src, land, send, recv, src_thru, land_thru, token):
        x, y, c, chips = _place()
        me = 2 * x + y
        mine = pl.ds(c * h, h)
        for j, (cx, cy) in enumerate(chips):
            pltpu.make_async_remote_copy(src_ref=src.at[mine], dst_ref=land.at[me, mine], send_sem=send.at[j], recv_sem=recv.at[j],
                                         device_id=(cx, cy, c), device_id_type=MESH).start()
        pltpu.make_async_remote_copy(src_ref=src, dst_ref=land.at[me], send_sem=send.at[len(chips)], recv_sem=recv.at[len(chips)],
                                     device_id=(x, y, 1 - c), device_id_type=MESH).start()
        token[...] = jnp.zeros(TOKEN_SHAPE.shape, F32)

    land = lax.empty((4,) + shard.shape, shard.dtype)
    return pl.pallas_call(
        body, name=name, in_specs=[HBM, HBM], out_specs=[SEM, SEM, HBM, HBM, TOKEN],
        out_shape=[pltpu.SemaphoreType.DMA((4,)), pltpu.SemaphoreType.DMA((4,)), pltpu.HBM(shard.shape, shard.dtype),
                   pltpu.HBM(land.shape, land.dtype), TOKEN_SHAPE],
        input_output_aliases={0: 2, 1: 3}, compiler_params=_split_params())(_in_hbm(shard), _in_hbm(land))


def _gather_pieces_wait(send, recv, shard, land, after, name):
    h = shard.shape[0] // 2
    after, after_specs = _unread(after)

    def body(*refs):
        land_ref, send_ref, recv_ref = refs[1:4]
        x, y, c, chips = _place()
        for j, (cx, cy) in enumerate(chips):
            blk = land_ref.at[2 * cx + cy, pl.ds(c * h, h)]
            pltpu.make_async_remote_copy(src_ref=blk, dst_ref=blk, send_sem=send_ref.at[j], recv_sem=recv_ref.at[j],
                                         device_id=(cx, cy, c), device_id_type=MESH).wait()
        own = land_ref.at[2 * x + y]
        pltpu.make_async_remote_copy(src_ref=own, dst_ref=own, send_sem=send_ref.at[len(chips)], recv_sem=recv_ref.at[len(chips)],
                                     device_id=(x, y, 1 - c), device_id_type=MESH).wait()

    out = pl.pallas_call(
        body, name=name, in_specs=[HBM, HBM, SEM, SEM] + after_specs, out_specs=[HBM, HBM],
        out_shape=[pltpu.HBM(shard.shape, shard.dtype), pltpu.HBM(land.shape, land.dtype)],
        input_output_aliases={0: 0, 1: 1}, compiler_params=_split_params())(shard, land, send, recv, *after)
    return out[1]


def _pass_pieces(land, name):
    h = land.shape[1] // 2

    def body(_, out, send, recv):
        x, y, c, chips = _place()

        def piece(j, cc):
            cx, cy = chips[j]
            blk = out.at[2 * cx + cy, pl.ds(cc * h, h)]
            return pltpu.make_async_remote_copy(src_ref=blk, dst_ref=blk, send_sem=send.at[j], recv_sem=recv.at[j],
                                                device_id=(x, y, 1 - c), device_id_type=MESH)

        for j in range(len(chips)):
            piece(j, c).start()
        for j in range(len(chips)):
            piece(j, 1 - c).wait_recv()
        for j in range(len(chips)):
            piece(j, c).wait_send()

    return pl.pallas_call(
        body, name=name, in_specs=[ANY], out_specs=ANY, out_shape=SDS(land.shape, land.dtype), input_output_aliases={0: 0},
        scratch_shapes=[pltpu.SemaphoreType.DMA((3,))] * 2, compiler_params=pltpu.CompilerParams(has_side_effects=True))(land)


def _scatter_start(g, name):
    _, r, c_ = g.shape
    h = r // 2

    def body(g_ref, land, send, recv, g_thru, land_thru, token):
        x, y, c, chips = _place()
        for j, (cx, cy) in enumerate(chips):
            for dc in range(2):
                pltpu.make_async_remote_copy(src_ref=g_ref.at[2 * cx + cy, pl.ds(dc * h, h)], dst_ref=land.at[2 * j + c], send_sem=send.at[2 * j + dc],
                                             recv_sem=recv.at[2 * j + c], device_id=(cx, cy, dc), device_id_type=MESH).start()
        pltpu.make_async_remote_copy(src_ref=g_ref.at[2 * x + y, pl.ds((1 - c) * h, h)], dst_ref=land.at[PEERS - 1], send_sem=send.at[PEERS - 1],
                                     recv_sem=recv.at[PEERS - 1], device_id=(x, y, 1 - c), device_id_type=MESH).start()
        token[...] = jnp.zeros(TOKEN_SHAPE.shape, F32)

    land = lax.empty((PEERS, h, c_), g.dtype)
    return pl.pallas_call(
        body, name=name, in_specs=[HBM, HBM], out_specs=[SEM, SEM, HBM, HBM, TOKEN],
        out_shape=[pltpu.SemaphoreType.DMA((PEERS,)), pltpu.SemaphoreType.DMA((PEERS,)), pltpu.HBM(g.shape, g.dtype),
                   pltpu.HBM(land.shape, land.dtype), TOKEN_SHAPE],
        input_output_aliases={0: 2, 1: 3}, compiler_params=_split_params())(_in_hbm(g), _in_hbm(land))


def _scatter_wait(started, after, name):
    nt = len(started)

    def body(*refs):
        lands = refs[nt:2 * nt]
        sends, recvs = refs[2 * nt:3 * nt], refs[3 * nt:4 * nt]
        x, y, c, chips = _place()
        peers = [(cx, cy, dc) for cx, cy in chips for dc in range(2)] + [(x, y, 1 - c)]
        for t in range(nt):
            for k, peer in enumerate(peers):
                blk = lands[t].at[k]
                pltpu.make_async_remote_copy(src_ref=blk, dst_ref=blk, send_sem=sends[t].at[k], recv_sem=recvs[t].at[k],
                                             device_id=peer, device_id_type=MESH).wait()

    gs, lands = [s[2] for s in started], [s[3] for s in started]
    after, after_specs = _unread(after)
    out = pl.pallas_call(
        body, name=name, in_specs=[HBM] * (2 * nt) + [SEM] * (2 * nt) + after_specs, out_specs=[HBM] * (2 * nt),
        out_shape=[pltpu.HBM(a.shape, a.dtype) for a in gs + lands],
        input_output_aliases={t: t for t in range(2 * nt)}, compiler_params=_split_params())(
            *gs, *lands, *[s[0] for s in started], *[s[1] for s in started], *after)
    return out[:nt], out[nt:]


def _flips():
    return [(dx, dy, dc) for dx in range(2) for dy in range(2) for dc in range(2) if (dx, dy, dc) != (0, 0, 0)]


def _flipped(x, y, c, flips):
    dx, dy, dc = flips
    return (1 - x if dx else x, 1 - y if dy else y, 1 - c if dc else c)


def _small_start(v, name, after=()):
    after, after_specs = _unread(after)

    def body(v_ref, land, *rest):
        send, recv = rest[len(after):len(after) + 2]
        x, y, c, _ = _place()
        for j, flips in enumerate(_flips()):
            pltpu.make_async_remote_copy(src_ref=v_ref, dst_ref=land.at[4 * x + 2 * y + c], send_sem=send.at[j], recv_sem=recv.at[j],
                                         device_id=_flipped(x, y, c, flips), device_id_type=MESH).start()
        rest[-1][...] = jnp.zeros(TOKEN_SHAPE.shape, F32)

    land = lax.empty((8,) + v.shape, v.dtype)
    return pl.pallas_call(
        body, name=name, in_specs=[HBM, HBM] + after_specs, out_specs=[SEM, SEM, HBM, HBM, TOKEN],
        out_shape=[pltpu.SemaphoreType.DMA((PEERS,)), pltpu.SemaphoreType.DMA((PEERS,)), pltpu.HBM(v.shape, v.dtype),
                   pltpu.HBM(land.shape, land.dtype), TOKEN_SHAPE],
        input_output_aliases={0: 2, 1: 3}, compiler_params=_split_params())(_in_hbm(v), _in_hbm(land), *after)


def _small_wait(send, recv, v, land, after, name):
    after, after_specs = _unread(after)

    def body(v_ref, land_ref, send_ref, recv_ref, *rest):
        x, y, c, _ = _place()
        for j, flips in enumerate(_flips()):
            px, py, pc = _flipped(x, y, c, flips)
            blk = land_ref.at[4 * px + 2 * py + pc]
            pltpu.make_async_remote_copy(src_ref=blk, dst_ref=blk, send_sem=send_ref.at[j], recv_sem=recv_ref.at[j],
                                         device_id=(px, py, pc), device_id_type=MESH).wait()

    return pl.pallas_call(
        body, name=name, in_specs=[HBM, HBM, SEM, SEM] + after_specs, out_specs=[HBM, HBM],
        out_shape=[pltpu.HBM(v.shape, v.dtype), pltpu.HBM(land.shape, land.dtype)],
        input_output_aliases={0: 0, 1: 1}, compiler_params=_split_params())(v, land, send, recv, *after)


def _sum_small(v, land, name):
    def body(v_ref, land_ref, o_ref):
        x, y, c, _ = _place()
        me = 4 * x + 2 * y + c
        acc = jnp.where(me == 0, v_ref[...], land_ref[0])
        for d in range(1, 8):
            acc = acc + jnp.where(me == d, v_ref[...], land_ref[d])
        o_ref[...] = acc

    vm = pl.BlockSpec(memory_space=pltpu.VMEM)
    return pl.pallas_call(body, name=name, in_specs=[vm, vm], out_specs=vm, out_shape=SDS(v.shape, F32))(v, land)


def _sum_devices(g, land, me, core, name):
    npeer, h, c = land.shape
    tr = _row_tile(h, 2 * ELEMENTWISE_ROWS)
    steps = h // tr

    def body(ix_ref, own_ref, land_ref, o_ref):
        acc = own_ref[0].astype(F32)
        for j in range(npeer):
            acc = acc + land_ref[j].astype(F32)
        o_ref[0] = acc

    grid_spec = pltpu.PrefetchScalarGridSpec(
        num_scalar_prefetch=1, grid=(steps,),
        in_specs=[pl.BlockSpec((1, tr, c), lambda i, ix: (ix[0], ix[1] * steps + i, 0)), pl.BlockSpec((npeer, tr, c), lambda i, ix: (0, i, 0))],
        out_specs=pl.BlockSpec((1, tr, c), lambda i, ix: (ix[1], i, 0)))
    return pl.pallas_call(body, name=name, grid_spec=grid_spec, out_shape=SDS((2, h, c), F32),
                          compiler_params=_cp("parallel"))(jnp.stack([me, core]), g, land)


def _pack_small(parts):
    flat = jnp.concatenate([p.reshape(-1) for p in parts])
    total = flat.shape[0]
    rows = -(-total // 1024) * 8
    return jnp.pad(flat, (0, rows * 128 - total)).reshape(rows, 128)


def _unpack_small(packed, shapes):
    flat = packed.reshape(-1)
    out, off = [], 0
    for s in shapes:
        size = int(np.prod(s))
        out.append(flat[off:off + size].reshape(s))
        off += size
    return out


def _local_step(x, mem, target, w_in_t, first_after, mid_weights, ffn_weights, on_grad, gains, conv_w, conv_b, hg_lb):
    n = x.shape[0]
    cos, sin = _rope_tables(n)
    seg = _hg_segments()
    gp, gn = _hg_pair_sums()
    masks = _hg_level_masks()
    gq2 = jnp.tile(gains["q_norm_g"], (1, 2))
    gk2 = jnp.tile(gains["k_norm_g"], (1, 2))
    a0 = hg_lb[:, 0:1, :]
    a1 = hg_lb[:, 1:2, :]

    p, h1 = _norm_mm(x, gains["pre_mix_g"], w_in_t, F32, TOKEN_TILE, 1664, "in_proj", after=(first_after,), w_turned=True)
    qr, kr = _qk_prep(p, gq2, gk2, cos, sin, "qk_prep")
    heads = lambda a: a.reshape(n, ATT_KV_HEADS, ATT_HEAD_DIM).transpose(1, 0, 2)
    kh = heads(kr)
    vh = heads(p[:, OFF_AV:OFF_AV + ATT_KV_DIM].astype(MXU_DTYPE))
    att = _attn_fwd(qr, kh, vh, "attn_fwd")
    o2, s0, hg_a, hg_e, hg_kept = _hgrn_fwd(p, a0, a1, seg, masks, "hgrn_fwd")
    rec = _hg_post(o2, p, gains["hg_out_norm_g"], "hg_post")
    cat = jnp.concatenate([att, rec], axis=1)
    w_out, w_xq, w_xkv, w_xo = mid_weights(cat)
    mixed, x1 = _mm_resid_norm(cat, w_out, x, gains["post_mix_g"], 512, "out_proj_resid")
    xq, h2 = _norm_mm(x1, gains["pre_x_g"], w_xq, MXU_DTYPE, TOKEN_TILE, 1024, "xq_proj")
    kv, mn = _norm_mm(mem, gains["mem_norm_g"], w_xkv, MXU_DTYPE, 256, 2048, "xkv_proj")
    ox = _xattn_fwd(xq, kv, "xattn_fwd")
    xo, x2 = _mm_resid_norm(ox, w_xo, x1, gains["post_x_g"], 512, "xo_proj_resid")
    w_up = ffn_weights("w_up", x2)
    u, h3 = _norm_mm(x2, gains["pre_ffn_g"], w_up, F32, TOKEN_TILE, 1408, "up_proj")
    act = _conv_gate(u, conv_w, conv_b, "conv_gate")
    w_down = ffn_weights("w_down", act)
    dn, d3, loss = _mm_resid_norm(act, w_down, x2, gains["post_ffn_g"], 512, "down_proj_resid_loss", target=target)

    gs = {}
    d_act, d_dn, gs["post_ffn_g"] = _norm_bwd_mm(dn, gains["post_ffn_g"], d3, w_down, F32, 512, 1408, "ffn_post_bwd_down_dx")
    tok = on_grad("w_down", _mm(act, d_dn, "tn", WIRE_DTYPE, 1408, 1024, "down_dw"))
    du_g, du_v, dcw_g, dcw_v, dcb_g, dcb_v = _conv_gate_bwd(u, conv_w, conv_b, d_act, "conv_gate_bwd", after=(tok,))
    gs["conv_w"] = jnp.concatenate([dcw_g, dcw_v], axis=1)
    gs["conv_b"] = jnp.concatenate([dcb_g, dcb_v], axis=1)
    ff_shard = w_up.shape[2]
    g_up = _dw_by_owner(h3, du_g, ff_shard, 0, None, 512, "up_dw_gate")
    tok = on_grad("w_up", _dw_by_owner(h3, du_v, ff_shard, 2, g_up, 512, "up_dw_value"))
    d2, gs["pre_ffn_g"] = _dx_norm_bwd([(du_g, 0), (du_g, 1), (du_v, 0), (du_v, 1)], w_up, x2, gains["pre_ffn_g"], d3, 512,
                                       "up_dx_pre_bwd", after=(tok,))
    d_ox, d_xo, gs["post_x_g"] = _norm_bwd_mm(xo, gains["post_x_g"], d2, w_xo, MXU_DTYPE, 512, 1024, "x_post_bwd_xo_dx")
    tok = on_grad("w_xo", _mm(ox, d_xo, "tn", WIRE_DTYPE, 512, 1024, "xo_dw"))
    d_xq, d_k, d_v = _xattn_bwd(xq, kv, d_ox, "xattn_bwd", after=(tok,))
    d_kv = jnp.concatenate([d_k, d_v], axis=1).astype(MXU_DTYPE)
    tok = on_grad("w_xq", _mm(h2, d_xq, "tn", WIRE_DTYPE, 512, 1024, "xq_dw"))
    tok_kv = on_grad("w_xkv", _dw_by_owner(mn, d_kv, w_xkv.shape[2], 0, None, 512, "xkv_dw"))
    d1, gs["pre_x_g"] = _dx_norm_bwd([(d_xq, 0)], w_xq[None], x1, gains["pre_x_g"], d2, 512, "xq_dx_pre_bwd", after=(tok, tok_kv))
    d_mn = _mm_nt_parts([(d_kv, s) for s in range(4)], w_xkv, F32, 256, 1024, "xkv_dx")
    _, gs["mem_norm_g"] = _norm_bwd(mem, gains["mem_norm_g"], d_mn, None, MXU_DTYPE, "mem_norm_bwd")
    d_cat, d_mixed, gs["post_mix_g"] = _norm_bwd_mm(mixed, gains["post_mix_g"], d1, w_out, MXU_DTYPE, 512, 1024, "mix_post_bwd_out_dx")
    tok = on_grad("w_out", _mm(cat, d_mixed, "tn", WIRE_DTYPE, 512, 1024, "out_dw"))
    d_o, d_hg, dg_hg = _hg_post_bwd(o2, p, gains["hg_out_norm_g"], d_cat, "hg_post_bwd", after=(tok,))
    gs["hg_out_norm_g"] = dg_hg.reshape(HG_HEADS, HG_HEAD_DIM).sum(axis=0, keepdims=True)
    dhq2, dz2, dhv2, dlb = _hgrn_bwd(p, a0, a1, masks, gp, gn, d_o, s0, hg_a, hg_e, hg_kept, "hgrn_bwd")
    lb = jax.nn.sigmoid(a0 - a1)
    da0 = dlb * lb * (1.0 - lb)
    gs["hg_lb"] = jnp.concatenate([da0, -da0], axis=1)
    d_qr, d_kh, d_vh = _attn_bwd(qr, kh, vh, cat, d_cat, "attn_bwd")
    unheads = lambda a: a.transpose(2, 0, 1).reshape(n, ATT_KV_DIM)
    d_aq, d_ak, dgq, dgk = _qk_prep_bwd(p, gq2, gk2, cos, sin, d_qr, unheads(d_kh), "qk_prep_bwd")
    gs["q_norm_g"] = dgq.reshape(ATT_HEADS, ATT_HEAD_DIM).sum(axis=0, keepdims=True)
    gs["k_norm_g"] = dgk.reshape(ATT_KV_HEADS, ATT_HEAD_DIM).sum(axis=0, keepdims=True)
    d_p = jnp.concatenate([d_aq, d_ak, unheads(d_vh).astype(MXU_DTYPE), (dhq2[0] + dhq2[1]).astype(MXU_DTYPE),
                           dz2[0].astype(MXU_DTYPE), dz2[1].astype(MXU_DTYPE), (dhv2[0] + dhv2[1]).astype(MXU_DTYPE), d_hg], axis=1)
    tok = on_grad("w_in", _mm(d_p, h1, "tn", WIRE_DTYPE, 1664, 1024, "in_dw"))
    grad_x, gs["pre_mix_g"] = _dx_norm_bwd([(d_p, 0)], w_in_t[None], x, gains["pre_mix_g"], d1, 512, "in_dx_pre_bwd", after=(tok,),
                                           b_turned=True)
    return loss, grad_x, gs


MATS = ("w_in", "w_out", "w_xq", "w_xkv", "w_xo", "w_up", "w_down")
GAINS = ("pre_mix_g", "q_norm_g", "k_norm_g", "hg_out_norm_g", "post_mix_g", "pre_x_g", "mem_norm_g", "post_x_g", "pre_ffn_g", "post_ffn_g")
WEIGHTS = ('pre_mix_g', 'w_in', 'q_norm_g', 'k_norm_g', 'hg_lb', 'hg_out_norm_g', 'w_out', 'post_mix_g', 'pre_x_g', 'mem_norm_g', 'w_xq',
           'w_xkv', 'w_xo', 'post_x_g', 'pre_ffn_g', 'w_up', 'conv_w', 'conv_b', 'w_down', 'post_ffn_g')


def kernel(x, mem, pre_mix_g, w_in, q_norm_g, k_norm_g, hg_lb, hg_out_norm_g, w_out, post_mix_g, pre_x_g, mem_norm_g, w_xq, w_xkv, w_xo, post_x_g, pre_ffn_g, w_up, conv_w, conv_b, w_down, post_ffn_g, loss_target, m_pre_mix_g, m_w_in, m_q_norm_g, m_k_norm_g, m_hg_lb, m_hg_out_norm_g, m_w_out, m_post_mix_g, m_pre_x_g, m_mem_norm_g, m_w_xq, m_w_xkv, m_w_xo, m_post_x_g, m_pre_ffn_g, m_w_up, m_conv_w, m_conv_b, m_w_down, m_post_ffn_g, v_pre_mix_g, v_w_in, v_q_norm_g, v_k_norm_g, v_hg_lb, v_hg_out_norm_g, v_w_out, v_post_mix_g, v_pre_x_g, v_mem_norm_g, v_w_xq, v_w_xkv, v_w_xo, v_post_x_g, v_pre_ffn_g, v_w_up, v_conv_w, v_conv_b, v_w_down, v_post_ffn_g):
    args = dict(locals())
    w = {k: args[k] for k in WEIGHTS}
    m = {k: args["m_" + k] for k in WEIGHTS}
    v = {k: args["v_" + k] for k in WEIGHTS}
    chip = 2 * lax.axis_index("x") + lax.axis_index("y")
    core = lax.axis_index("c")

    turned = ("w_in",)
    shards = {k: (jnp.swapaxes(w[k], 1, 2) if k in turned else w[k])[0].astype(WIRE_DTYPE) for k in MATS}

    def whole(k, g):
        return g if k in ("w_xkv", "w_up") else g.reshape(-1, g.shape[-1])

    mid_names, ffn_names = ("w_out", "w_xq", "w_xkv", "w_xo"), ("w_up", "w_down")
    w_in_pieces = _gather_pieces_start(shards["w_in"], "gather_w_in_start")
    small_in = _exchange_small(_pack_small([w["conv_w"][0], w["hg_lb"]]), False, "gather_small", after=(w_in_pieces[4],))
    cast_first = (small_in, *[shards[k] for k in mid_names + ffn_names])
    w_in_t = whole("w_in", _pass_pieces(_gather_pieces_wait(*w_in_pieces[:4], cast_first, "gather_w_in_wait"), "gather_w_in_pass"))
    mid = _gather_start([shards[k] for k in mid_names], "gather_mid_start", after=(w_in_t, small_in))
    ffn = _gather_start([shards[k] for k in ffn_names], "gather_ffn_start", after=(mid[4],))

    def mid_weights(after):
        return [whole(k, g) for k, g in zip(mid_names, _gather_wait(*mid[:4], after, "gather_mid_wait"))]

    def ffn_weights(k, after):
        t = ffn_names.index(k)
        return whole(k, _gather_wait(*[part[t:t + 1] for part in ffn[:4]], after, "gather_wait_" + k)[0])

    cw_parts, lb_parts = [], []
    for s in range(4):
        cw_s, lb_s = _unpack_small(small_in[2 * s], [w["conv_w"][0].shape, w["hg_lb"].shape])
        cw_parts.append(cw_s)
        lb_parts.append(lb_s)
    conv_w_full = jnp.concatenate(cw_parts, axis=1)
    hg_lb_full = jnp.concatenate(lb_parts, axis=2)

    started = {}

    def on_grad(k, g):
        if g.ndim == 2:
            g = g.reshape(4, g.shape[0] // 4, g.shape[1])
        *started[k], token = _scatter_start(g, "grad_start_" + k)
        return token

    gains = {k: w[k] for k in GAINS}
    loss_part, grad_x, gs = _local_step(x[0], mem[0], loss_target[0], w_in_t, ffn[4], mid_weights, ffn_weights, on_grad, gains,
                                        conv_w_full, w["conv_b"], hg_lb_full)
    gs["loss"] = loss_part

    grads, delta, new_m, new_v = {}, {}, {}, {}

    def reduce_matrices(names, after, tag):
        sent, landed = _scatter_wait([started[k] for k in names], after, "grad_wait_" + tag)
        halves = [_sum_devices(g, land, chip, core, "grad_sum_" + k) for k, g, land in zip(names, sent, landed)]
        for k, r in zip(names, _join_halves(halves, "grad_join_" + tag)):
            grads[k] = r.reshape(1, -1, r.shape[-1])

    def adamw(names):
        for k in names:
            shape = w[k].shape
            keep = len(shape) == 3 and shape[0] == 1
            if k in turned:
                view, back = (lambda a: jnp.swapaxes(a, 1, 2)), (lambda a: jnp.swapaxes(a, 1, 2))
                g = grads[k]
            else:
                view = (lambda a: a.reshape(shape)) if keep else (lambda a: a.reshape(-1, shape[-1]))
                back = lambda a: a.reshape(shape)
                g = view(grads[k])
            d, mo, vo, go = _adamw(view(w[k]), g, view(m[k]), view(v[k]), "adamw_" + k)
            delta[k], new_m[k], new_v[k], grads[k] = back(d), back(mo), back(vo), back(go)

    small_names = GAINS + ("conv_b", "conv_w", "hg_lb")
    packed = _pack_small([gs[k] for k in small_names + ("loss",)])
    small = _small_start(packed, "reduce_small_start", after=(grad_x,))

    early = tuple(k for k in MATS if k != "w_in")
    reduce_matrices(early, (grad_x, small[4]), "early")
    adamw(early)

    mine, others = _small_wait(*small[:4], tuple(new_v[k] for k in early), "reduce_small_wait")
    reduced_small = _sum_small(mine, others, "reduce_small_sum")
    *summed, loss = _unpack_small(reduced_small, [gs[k].shape for k in small_names + ("loss",)])
    loss = loss[0, 0]
    for k, g in zip(small_names, summed):
        grads[k] = g
    ncw = w["conv_w"].shape[2]
    grads["conv_w"] = lax.dynamic_slice_in_dim(grads["conv_w"], chip * ncw, ncw, axis=1)[None]
    nlb = w["hg_lb"].shape[2]
    grads["hg_lb"] = lax.dynamic_slice_in_dim(grads["hg_lb"], chip * nlb, nlb, axis=2)
    replicated = GAINS + ("conv_b",)
    shapes = [w[k].shape for k in replicated]
    rows = sum(int(np.prod(s)) for s in shapes) // 128
    pack = lambda d: jnp.concatenate([d[k].reshape(-1) for k in replicated]).reshape(rows, 128)
    outs = _adamw(pack(w), reduced_small[:rows], pack(m), pack(v), "adamw_replicated")
    for into, packed_out in zip((delta, new_m, new_v, grads), outs):
        for k, a in zip(replicated, _unpack_small(packed_out, shapes)):
            into[k] = a
    adamw(("conv_w", "hg_lb"))

    reduce_matrices(("w_in",), tuple(new_v[k] for k in early + small_names), "late")
    adamw(("w_in",))
    return (loss, grad_x[None], *[grads[k] for k in WEIGHTS], *[delta[k] for k in WEIGHTS],
            *[new_m[k] for k in WEIGHTS], *[new_v[k] for k in WEIGHTS])
```

```python
import numpy as np
import jax
import jax.numpy as jnp
from jax import lax
from jax.experimental import pallas as pl
from jax.experimental.pallas import tpu as pltpu

F32 = jnp.float32
MXU_DTYPE = jnp.bfloat16
WIRE_DTYPE = jnp.bfloat16
VMEM_LIMIT_BYTES = 56 * 1024 * 1024
ROWS_PER_16BIT_TILE = 16
ELEMENTWISE_ROWS = 256
EPS = 1e-6
MESH = pl.DeviceIdType.MESH

GRID_W = 64
ATT_HEADS, ATT_KV_HEADS, ATT_HEAD_DIM = 8, 2, 64
ATT_GROUP = ATT_HEADS // ATT_KV_HEADS
ATT_Q_DIM, ATT_KV_DIM = 512, 128
ROPE_THETA = 10000.0
HG_HEADS, HG_HEAD_DIM, HG_DIM = 4, 128, 512
HG_CHUNK = 128
HG_LEVELS = 7
HG_PAIR = 2 * HG_HEAD_DIM
HG_KEPT = 7
X_HEADS, X_HEAD_DIM = 4, 256
D_FF = 2816
FF_COLS = 256
FF_BLOCKS = D_FF // FF_COLS
OFF_AK, OFF_AV, OFF_HQ, OFF_ZF, OFF_ZB, OFF_HI, OFF_HG = 512, 640, 768, 1280, 1792, 2304, 2816

ADAM_LR, ADAM_B1, ADAM_B2, ADAM_EPS, ADAM_WD, ADAM_STEP = 0.001, 0.9, 0.999, 1e-08, 0.01, 10

SDS = jax.ShapeDtypeStruct


def _cp(*sem):
    return pltpu.CompilerParams(dimension_semantics=sem, vmem_limit_bytes=VMEM_LIMIT_BYTES)


def _row_tile(rows, cap):
    if rows <= cap:
        return rows
    return max(t for t in range(ROWS_PER_16BIT_TILE, cap + 1, ROWS_PER_16BIT_TILE) if rows % t == 0)


def _dot(a, b, form="nn"):
    dims = {"nn": (((1,), (0,)), ((), ())), "nt": (((1,), (1,)), ((), ())), "tn": (((0,), (0,)), ((), ()))}[form]
    return lax.dot_general(a.astype(MXU_DTYPE), b.astype(MXU_DTYPE), dims, preferred_element_type=F32)


def _sigmoid(x):
    return 1.0 / (1.0 + jnp.exp(-x))


def _rstd(x):
    return lax.rsqrt(jnp.mean(x * x, axis=-1, keepdims=True) + EPS)


def _rms_bwd(x, g, dy):
    r = _rstd(x)
    xh = x * r
    dn = dy * g
    dx = r * (dn - xh * jnp.mean(dn * xh, axis=-1, keepdims=True))
    return dx, jnp.sum(dy * xh, axis=0, keepdims=True)


def _unread(after):
    after = tuple(a for a in after if a is not None)
    return after, [pl.BlockSpec(memory_space=pl.ANY)] * len(after)


def _mm(a, b, form, out_dtype, tm, tn, name, after=()):
    after, after_specs = _unread(after)
    if form == "nn":
        (m, k), n = a.shape, b.shape[1]
    elif form == "nt":
        (m, k), n = a.shape, b.shape[0]
    else:
        (k, m), n = a.shape, b.shape[1]
    tm, tn = min(tm, m), min(tn, n)
    assert m % tm == 0 and n % tn == 0, (name, m, n, tm, tn)

    def body(a_ref, b_ref, *rest):
        o_ref = rest[-1]
        o_ref[...] = _dot(a_ref[...], b_ref[...], form).astype(o_ref.dtype)

    a_spec = pl.BlockSpec((k, tm), lambda i, j: (0, i)) if form == "tn" else pl.BlockSpec((tm, k), lambda i, j: (i, 0))
    b_spec = pl.BlockSpec((tn, k), lambda i, j: (j, 0)) if form == "nt" else pl.BlockSpec((k, tn), lambda i, j: (0, j))
    return pl.pallas_call(
        body, name=name, grid=(m // tm, n // tn), in_specs=[a_spec, b_spec] + after_specs,
        out_specs=pl.BlockSpec((tm, tn), lambda i, j: (i, j)), out_shape=SDS((m, n), out_dtype),
        compiler_params=_cp("parallel", "parallel"))(a, b, *after)


def _mm_nt_parts(a_parts, b, out_dtype, tm, tn, name, after=()):
    after, after_specs = _unread(after)
    parts, n, p = b.shape
    m = a_parts[0][0].shape[0]
    tm, tn = min(tm, m), min(tn, n)
    assert m % tm == 0 and n % tn == 0 and len(a_parts) == parts, (name, m, b.shape)

    def body(*refs):
        o_ref = refs[-1]
        acc = _dot(refs[0][...], refs[parts][0], "nt")
        for s in range(1, parts):
            acc = acc + _dot(refs[s][...], refs[parts + s][0], "nt")
        o_ref[...] = acc.astype(o_ref.dtype)

    a_specs = [pl.BlockSpec((tm, p), lambda i, j, cb=cb: (i, cb)) for _, cb in a_parts]
    b_specs = [pl.BlockSpec((1, tn, p), lambda i, j, s=s: (s, j, 0)) for s in range(parts)]
    return pl.pallas_call(
        body, name=name, grid=(m // tm, n // tn), in_specs=a_specs + b_specs + after_specs,
        out_specs=pl.BlockSpec((tm, tn), lambda i, j: (i, j)), out_shape=SDS((m, n), out_dtype),
        compiler_params=_cp("parallel", "parallel"))(*[arr for arr, _ in a_parts], *([b] * parts), *after)


def _norm_bwd_mm(y, g, d, w, out_dtype, tm, tn, name):
    n, dm = y.shape
    nn = w.shape[0]
    tm, tn = min(tm, n), min(tn, nn)
    assert n % tm == 0 and nn % tn == 0 and w.shape[1] == dm, (name, y.shape, w.shape)

    def body(y_ref, g_ref, d_ref, w_ref, dx_ref, dy_ref, dg_ref, dys):
        i, j = pl.program_id(0), pl.program_id(1)

        @pl.when(jnp.logical_and(i == 0, j == 0))
        def _():
            dg_ref[...] = jnp.zeros_like(dg_ref)

        @pl.when(j == 0)
        def _():
            dy, dg = _rms_bwd(y_ref[...], g_ref[...], d_ref[...])
            dy = dy.astype(MXU_DTYPE)
            dys[...] = dy
            dy_ref[...] = dy
            dg_ref[...] += dg

        dx_ref[...] = _dot(dys[...], w_ref[...], "nt").astype(dx_ref.dtype)

    row = pl.BlockSpec((tm, dm), lambda i, j: (i, 0))
    vec = pl.BlockSpec((1, dm), lambda i, j: (0, 0))
    return pl.pallas_call(
        body, name=name, grid=(n // tm, nn // tn), in_specs=[row, vec, row, pl.BlockSpec((tn, dm), lambda i, j: (j, 0))],
        out_specs=[pl.BlockSpec((tm, tn), lambda i, j: (i, j)), row, vec],
        out_shape=[SDS((n, nn), out_dtype), SDS((n, dm), MXU_DTYPE), SDS((1, dm), F32)],
        scratch_shapes=[pltpu.VMEM((tm, dm), MXU_DTYPE)],
        compiler_params=_cp("arbitrary", "arbitrary"))(y, g, d, w)


def _mm_resid_norm(a, b, x, g, tm, name, target=None):
    n, k = a.shape
    d = b.shape[1]
    tm = min(tm, n)
    assert n % tm == 0 and x.shape == (n, d), (name, a.shape, b.shape)
    with_loss = target is not None

    def body(a_ref, b_ref, x_ref, g_ref, *rest):
        y = _dot(a_ref[...], b_ref[...])
        out = x_ref[...] + y * _rstd(y) * g_ref[...]
        if not with_loss:
            y_ref, o_ref = rest
            y_ref[...] = y
            o_ref[...] = out
            return
        t_ref, y_ref, d_ref, l_ref = rest
        y_ref[...] = y
        diff = out - t_ref[...]
        d_ref[...] = diff * (1.0 / d)

        @pl.when(pl.program_id(0) == 0)
        def _():
            l_ref[...] = jnp.zeros_like(l_ref)

        l_ref[...] += 0.5 * jnp.sum(jnp.mean(diff * diff, axis=-1, keepdims=True), axis=0, keepdims=True)

    row = pl.BlockSpec((tm, d), lambda i: (i, 0))
    ins = [pl.BlockSpec((tm, k), lambda i: (i, 0)), pl.BlockSpec((k, d), lambda i: (0, 0)), row, pl.BlockSpec((1, d), lambda i: (0, 0))]
    out = SDS((n, d), F32)
    if with_loss:
        return pl.pallas_call(body, name=name, grid=(n // tm,), in_specs=ins + [row], out_specs=[row, row, pl.BlockSpec((1, 1), lambda i: (0, 0))],
                              out_shape=[out, out, SDS((1, 1), F32)], compiler_params=_cp("arbitrary"))(a, b, x, g, target)
    return pl.pallas_call(body, name=name, grid=(n // tm,), in_specs=ins, out_specs=[row, row], out_shape=[out, out],
                          compiler_params=_cp("parallel"))(a, b, x, g)


def _dx_norm_bwd(a_parts, b, x, g, res, tm, name, after=(), b_turned=False):
    after, after_specs = _unread(after)
    parts, d, p = (b.shape[0], b.shape[2], b.shape[1]) if b_turned else b.shape
    form = "nn" if b_turned else "nt"
    n = x.shape[0]
    tm = min(tm, n)
    assert n % tm == 0 and len(a_parts) == parts and x.shape[1] == d, (name, x.shape, b.shape)

    def body(*refs):
        x_ref, g_ref, res_ref = refs[2 * parts:2 * parts + 3]
        dx_ref, dg_ref = refs[-2:]
        dh = _dot(refs[0][...], refs[parts][0], form)
        for s in range(1, parts):
            dh = dh + _dot(refs[s][...], refs[parts + s][0], form)
        dx, dg = _rms_bwd(x_ref[...], g_ref[...], dh)
        dx_ref[...] = dx + res_ref[...]

        @pl.when(pl.program_id(0) == 0)
        def _():
            dg_ref[...] = jnp.zeros_like(dg_ref)

        dg_ref[...] += dg

    a_specs = [pl.BlockSpec((tm, p), lambda i, cb=cb: (i, cb)) for _, cb in a_parts]
    b_specs = [pl.BlockSpec((1,) + b.shape[1:], lambda i, s=s: (s, 0, 0)) for s in range(parts)]
    row = pl.BlockSpec((tm, d), lambda i: (i, 0))
    vec = pl.BlockSpec((1, d), lambda i: (0, 0))
    return pl.pallas_call(
        body, name=name, grid=(n // tm,), in_specs=a_specs + b_specs + [row, vec, row] + after_specs,
        out_specs=[row, vec], out_shape=[SDS((n, d), F32), SDS((1, d), F32)],
        compiler_params=_cp("arbitrary"))(*[arr for arr, _ in a_parts], *([b] * parts), x, g, res, *after)


def _dw_by_owner(a, b, tn, first, into, tm, name):
    k, m = a.shape
    cnt = b.shape[1] // tn
    tm = min(tm, m)
    assert m % tm == 0 and b.shape[1] == cnt * tn and first + cnt <= 4, (name, a.shape, b.shape)

    def body(a_ref, b_ref, *rest):
        rest[-1][0] = _dot(a_ref[...], b_ref[...], "tn").astype(rest[-1].dtype)

    extra = [] if into is None else [into]
    return pl.pallas_call(
        body, name=name, grid=(m // tm, cnt),
        in_specs=[pl.BlockSpec((k, tm), lambda i, j: (0, i)), pl.BlockSpec((k, tn), lambda i, j: (0, j))] + [pl.BlockSpec(memory_space=pl.ANY)] * len(extra),
        out_specs=pl.BlockSpec((1, tm, tn), lambda i, j: (first + j, i, 0)), out_shape=SDS((4, m, tn), WIRE_DTYPE),
        input_output_aliases={2: 0} if extra else {},
        compiler_params=_cp("parallel", "parallel"))(a, b, *extra)


def _norm_mm(x, g, w, out_dtype, tm, tn, name, after=(), w_turned=False):
    after, after_specs = _unread(after)
    m, d = x.shape
    sharded = w.ndim == 3
    n = w.shape[0] if w_turned else w.shape[-1] * (w.shape[0] if sharded else 1)
    tm, tn = min(tm, m), (w.shape[-1] if sharded else min(tn, n))
    assert m % tm == 0 and n % tn == 0 and not (sharded and w_turned), (name, m, n, tm, tn)

    def body(x_ref, g_ref, w_ref, *rest):
        o_ref, h_ref, hs = rest[-3:]

        @pl.when(pl.program_id(1) == 0)
        def _():
            xv = x_ref[...]
            h = (xv * _rstd(xv) * g_ref[...]).astype(MXU_DTYPE)
            hs[...] = h
            h_ref[...] = h

        o_ref[...] = _dot(hs[...], w_ref[0] if sharded else w_ref[...], "nt" if w_turned else "nn").astype(o_ref.dtype)

    if w_turned:
        w_spec = pl.BlockSpec((tn, d), lambda i, j: (j, 0))
    else:
        w_spec = pl.BlockSpec((1, d, tn), lambda i, j: (j, 0, 0)) if sharded else pl.BlockSpec((d, tn), lambda i, j: (0, j))
    return pl.pallas_call(
        body, name=name, grid=(m // tm, n // tn),
        in_specs=[pl.BlockSpec((tm, d), lambda i, j: (i, 0)), pl.BlockSpec((1, d), lambda i, j: (0, 0)), w_spec] + after_specs,
        out_specs=[pl.BlockSpec((tm, tn), lambda i, j: (i, j)), pl.BlockSpec((tm, d), lambda i, j: (i, 0))],
        out_shape=[SDS((m, n), out_dtype), SDS((m, d), MXU_DTYPE)],
        scratch_shapes=[pltpu.VMEM((tm, d), MXU_DTYPE)],
        compiler_params=_cp("parallel", "arbitrary"))(x, g, w, *after)


ROW_TILE = 512
TOKEN_TILE = 1024


def _norm_bwd(x, g, dy, res, out_dtype, name):
    n, d = x.shape
    tr = min(ROW_TILE, n)
    has_res = res is not None

    def body(*refs):
        x_ref, g_ref, dy_ref = refs[:3]
        dx_ref, dg_ref = refs[-2:]
        dx, dg = _rms_bwd(x_ref[...], g_ref[...], dy_ref[...].astype(F32))
        if has_res:
            dx = dx + refs[3][...]
        dx_ref[...] = dx.astype(dx_ref.dtype)

        @pl.when(pl.program_id(0) == 0)
        def _():
            dg_ref[...] = jnp.zeros_like(dg_ref)

        dg_ref[...] += dg

    row = pl.BlockSpec((tr, d), lambda i: (i, 0))
    vec = pl.BlockSpec((1, d), lambda i: (0, 0))
    ins = [x, g, dy] + ([res] if has_res else [])
    return pl.pallas_call(
        body, name=name, grid=(n // tr,), in_specs=[row, vec, row] + ([row] if has_res else []),
        out_specs=[row, vec], out_shape=[SDS((n, d), out_dtype), SDS((1, d), F32)],
        compiler_params=_cp("arbitrary"))(*ins)


def _rope_tables(n):
    pairs = ATT_HEAD_DIM // 4
    t = np.arange(n)
    inv = np.power(ROPE_THETA, -np.arange(pairs, dtype=np.float32) / pairs).astype(np.float32)
    ang = np.concatenate([(t // GRID_W)[:, None].astype(np.float32) * inv, (t % GRID_W)[:, None].astype(np.float32) * inv], axis=-1)
    cos = np.repeat(np.cos(ang), 2, axis=-1)
    sin = np.repeat(np.sin(ang), 2, axis=-1) * np.tile(np.array([-1.0, 1.0], np.float32), ATT_HEAD_DIM // 2)
    return jnp.asarray(np.tile(cos, 2), F32), jnp.asarray(np.tile(sin, 2), F32)


def _swap_pairs(x):
    lane = lax.broadcasted_iota(jnp.int32, x.shape, 1)
    return jnp.where((lane & 1) == 0, pltpu.roll(x, 127, axis=1), pltpu.roll(x, 1, axis=1))


def _head_mean(v):
    lane = lax.broadcasted_iota(jnp.int32, v.shape, 1)
    lo = jnp.where(lane < ATT_HEAD_DIM, v, 0.0)
    s0 = jnp.sum(lo, axis=-1, keepdims=True)
    s1 = jnp.sum(v - lo, axis=-1, keepdims=True)
    return jnp.where(lane < ATT_HEAD_DIM, s0, s1) * (1.0 / ATT_HEAD_DIM)


def _qk_prep(p, gq, gk, cos, sin, name):
    n = p.shape[0]
    tr = min(ROW_TILE, n)

    def one(xv, g, c, s):
        xn = xv * lax.rsqrt(_head_mean(xv * xv) + EPS) * g
        return xn * c + _swap_pairs(xn) * s

    def body(q_ref, k_ref, gq_ref, gk_ref, c_ref, s_ref, qo_ref, ko_ref):
        c, s = c_ref[...], s_ref[...]
        for j in range(ATT_Q_DIM // 128):
            qo_ref[:, j * 128:(j + 1) * 128] = one(q_ref[:, j * 128:(j + 1) * 128], gq_ref[...], c, s).astype(qo_ref.dtype)
        ko_ref[...] = one(k_ref[...], gk_ref[...], c, s).astype(ko_ref.dtype)

    vec = pl.BlockSpec((1, 128), lambda i: (0, 0))
    tab = pl.BlockSpec((tr, 128), lambda i: (i, 0))
    return pl.pallas_call(
        body, name=name, grid=(n // tr,),
        in_specs=[pl.BlockSpec((tr, ATT_Q_DIM), lambda i: (i, 0)), pl.BlockSpec((tr, 128), lambda i: (i, OFF_AK // 128)), vec, vec, tab, tab],
        out_specs=[pl.BlockSpec((tr, ATT_Q_DIM), lambda i: (i, 0)), tab],
        out_shape=[SDS((n, ATT_Q_DIM), MXU_DTYPE), SDS((n, ATT_KV_DIM), MXU_DTYPE)],
        compiler_params=_cp("parallel"))(p, p, gq, gk, cos, sin)


def _qk_prep_bwd(p, gq, gk, cos, sin, dq, dk, name):
    n = p.shape[0]
    tr = min(ROW_TILE, n)

    def one(xv, g, c, s, dout):
        dxn = dout * c + _swap_pairs(dout * s)
        r = lax.rsqrt(_head_mean(xv * xv) + EPS)
        xh = xv * r
        dn = dxn * g
        dx = r * (dn - xh * _head_mean(dn * xh))
        return dx, jnp.sum(dxn * xh, axis=0, keepdims=True)

    def body(q_ref, k_ref, gq_ref, gk_ref, c_ref, s_ref, dq_ref, dk_ref, dqo_ref, dko_ref, dgq_ref, dgk_ref):
        @pl.when(pl.program_id(0) == 0)
        def _():
            dgq_ref[...] = jnp.zeros_like(dgq_ref)
            dgk_ref[...] = jnp.zeros_like(dgk_ref)

        c, s = c_ref[...], s_ref[...]
        for j in range(ATT_Q_DIM // 128):
            sl = slice(j * 128, (j + 1) * 128)
            dx, dg = one(q_ref[:, sl], gq_ref[...], c, s, dq_ref[:, sl])
            dqo_ref[:, sl] = dx.astype(dqo_ref.dtype)
            dgq_ref[:, sl] += dg
        dx, dg = one(k_ref[...], gk_ref[...], c, s, dk_ref[...])
        dko_ref[...] = dx.astype(dko_ref.dtype)
        dgk_ref[...] += dg

    vec = pl.BlockSpec((1, 128), lambda i: (0, 0))
    tab = pl.BlockSpec((tr, 128), lambda i: (i, 0))
    qrow = pl.BlockSpec((tr, ATT_Q_DIM), lambda i: (i, 0))
    return pl.pallas_call(
        body, name=name, grid=(n // tr,),
        in_specs=[qrow, pl.BlockSpec((tr, 128), lambda i: (i, OFF_AK // 128)), vec, vec, tab, tab, qrow, tab],
        out_specs=[qrow, tab, pl.BlockSpec((1, ATT_Q_DIM), lambda i: (0, 0)), vec],
        out_shape=[SDS((n, ATT_Q_DIM), MXU_DTYPE), SDS((n, ATT_KV_DIM), MXU_DTYPE), SDS((1, ATT_Q_DIM), F32), SDS((1, 128), F32)],
        compiler_params=_cp("arbitrary"))(p, p, gq, gk, cos, sin, dq, dk)


ATT_FWD_STEP = (256, 4)
ATT_BWD_STEP = (512, 2)


def _attn_fwd(q, k, v, name):
    n = q.shape[0]
    tq, step_heads = min(ATT_FWD_STEP[0], n), ATT_FWD_STEP[1]
    scale = ATT_HEAD_DIM ** -0.5
    gw = step_heads * ATT_HEAD_DIM
    parts = ATT_GROUP // step_heads

    def body(q_ref, k_ref, v_ref, o_ref):
        kk, vv = k_ref[0], v_ref[0]
        v_ones = jnp.concatenate([vv, jnp.ones_like(vv)], axis=1)
        outs = []
        for g in range(step_heads):
            s = _dot(q_ref[:, g * ATT_HEAD_DIM:(g + 1) * ATT_HEAD_DIM] * scale, kk, "nt")
            e = jnp.exp(s - jnp.max(s, axis=-1, keepdims=True))
            ov = _dot(e, v_ones)
            outs.append(ov[:, :ATT_HEAD_DIM] / ov[:, ATT_HEAD_DIM:])
        o_ref[...] = jnp.concatenate(outs, axis=-1).astype(o_ref.dtype)

    kv = pl.BlockSpec((1, n, ATT_HEAD_DIM), lambda h, i, pr: (h, 0, 0))
    qb = pl.BlockSpec((tq, gw), lambda h, i, pr: (i, h * parts + pr))
    return pl.pallas_call(
        body, name=name, grid=(ATT_KV_HEADS, n // tq, parts), in_specs=[qb, kv, kv],
        out_specs=qb, out_shape=SDS((n, ATT_Q_DIM), MXU_DTYPE),
        compiler_params=_cp("parallel", "parallel", "parallel"))(q, k, v)


def _attn_bwd(q, k, v, o, do, name):
    n = q.shape[0]
    tq, step_heads = min(ATT_BWD_STEP[0], n), ATT_BWD_STEP[1]
    scale = ATT_HEAD_DIM ** -0.5
    gw = step_heads * ATT_HEAD_DIM
    parts = ATT_GROUP // step_heads

    def body(q_ref, k_ref, v_ref, o_ref, do_ref, dq_ref, dk_ref, dv_ref):
        @pl.when(jnp.logical_and(pl.program_id(1) == 0, pl.program_id(2) == 0))
        def _():
            dk_ref[...] = jnp.zeros_like(dk_ref)
            dv_ref[...] = jnp.zeros_like(dv_ref)

        kk, vv = k_ref[0], v_ref[0]
        dqs = []
        dk_acc = jnp.zeros((ATT_HEAD_DIM, n), F32)
        dv_acc = jnp.zeros((ATT_HEAD_DIM, n), F32)
        for g in range(step_heads):
            sl = slice(g * ATT_HEAD_DIM, (g + 1) * ATT_HEAD_DIM)
            qg, dog = q_ref[:, sl] * scale, do_ref[:, sl].astype(F32)
            s = _dot(qg, kk, "nt")
            e = jnp.exp(s - jnp.max(s, axis=-1, keepdims=True))
            inv = 1.0 / jnp.sum(e, axis=-1, keepdims=True)
            delta = jnp.sum(dog * o_ref[:, sl].astype(F32), axis=-1, keepdims=True)
            dse = e * (_dot(dog, vv, "nt") - delta)
            dqs.append(_dot(dse, kk) * (inv * scale))
            dk_acc += _dot(qg.astype(F32) * inv, dse, "tn")
            dv_acc += _dot(dog * inv, e, "tn")
        dq_ref[...] = jnp.concatenate(dqs, axis=-1)
        dk_ref[0] += dk_acc
        dv_ref[0] += dv_acc

    kv = pl.BlockSpec((1, n, ATT_HEAD_DIM), lambda h, i, pr: (h, 0, 0))
    kvt = pl.BlockSpec((1, ATT_HEAD_DIM, n), lambda h, i, pr: (h, 0, 0))
    qb = pl.BlockSpec((tq, gw), lambda h, i, pr: (i, h * parts + pr))
    return pl.pallas_call(
        body, name=name, grid=(ATT_KV_HEADS, n // tq, parts), in_specs=[qb, kv, kv, qb, qb], out_specs=[qb, kvt, kvt],
        out_shape=[SDS((n, ATT_Q_DIM), F32), SDS((ATT_KV_HEADS, ATT_HEAD_DIM, n), F32), SDS((ATT_KV_HEADS, ATT_HEAD_DIM, n), F32)],
        compiler_params=_cp("parallel", "arbitrary", "arbitrary"))(q, k, v, o, do)


def _both_directions(mats, axis):
    fwd = np.concatenate(mats, axis=axis).astype(np.float32)
    bwd = np.concatenate([m[::-1, ::-1] for m in mats], axis=axis).astype(np.float32)
    return jnp.asarray(np.stack([fwd, bwd]), MXU_DTYPE)


def _hg_segments():
    c = HG_CHUNK
    t = np.arange(c)[:, None]
    r = np.arange(c)[None, :]
    mats = [(r <= t)]
    for lev in range(HG_LEVELS):
        h = c >> (lev + 1)
        mid = (t // (2 * h)) * (2 * h) + h - 1
        hi = (t // h) % 2 == 1
        mats.append(np.where(hi, (r > mid) & (r <= t), (r > t) & (r <= mid)))
    mats.append(r > t)
    return _both_directions(mats, 0)


def _hg_pair_sums():
    c = HG_CHUNK
    r = np.arange(c)[:, None]
    t = np.arange(c)[None, :]
    gp, gn = [t >= r], [t < r]
    for lev in range(HG_LEVELS):
        sh = HG_LEVELS - 1 - lev
        same = (r >> sh) == (t >> sh)
        gp.append(same & (t >= r))
        gn.append(same & (t < r))
    return _both_directions(gp, 1), _both_directions(gn, 1)


def _split_dot(mat, x):
    hi = x.astype(MXU_DTYPE)
    lo = (x - hi.astype(F32)).astype(MXU_DTYPE)
    return _dot(mat, hi) + _dot(mat, lo)


def _hg_gates(hq, z, a0, a1):
    q = hq * _sigmoid(hq)
    sg = _sigmoid(z)
    lb = _sigmoid(a0 - a1)
    f = lb + (1.0 - lb) * sg
    k = (1.0 - lb) * (1.0 - sg)
    return q, f, k, sg, lb


def _hg_level_masks():
    c = HG_CHUNK
    t = np.arange(c)
    later, same = [], []
    for lev in range(HG_LEVELS):
        sh = HG_LEVELS - 1 - lev
        later.append(np.broadcast_to((((t >> sh) & 1) == 1)[:, None], (c, HG_HEAD_DIM)))
        same.append((t[:, None] >> (sh + 1)) == (t[None, :] >> (sh + 1)))
    same.append(t[:, None] == t[None, :])
    later = np.stack(later).astype(np.float32)
    return jnp.asarray(np.stack([later, 1.0 - later]), F32), jnp.asarray(np.stack(same).astype(np.float32), F32)


def _hg_level(q, k, ex, later_ref, lev):
    e = ex[lev + 1]
    e_q = e * later_ref[0, lev]
    e_k = e - e_q
    return q * e_q, k * e_k, e_q, e_k


def _hg_intra(q, k, ex, later_ref, same_ref):
    a = same_ref[HG_LEVELS] * jnp.sum(q * k, axis=-1, keepdims=True)
    for lev in range(HG_LEVELS):
        qs, ks, _, _ = _hg_level(q, k, ex, later_ref, lev)
        a = a + same_ref[lev] * _dot(qs, ks, "nt")
    return a


def _hg_specs(n, with_time):
    c = HG_CHUNK
    nc = n // c

    def chunk(d, i):
        first = d if with_time else 1 - d
        return i + first * (nc - 1 - 2 * i)

    def pcols(off, dir_stride=0):
        return [pl.BlockSpec((c, HG_PAIR), lambda d, i, j=j: (chunk(d, i), off // HG_PAIR + dir_stride // HG_PAIR * d + j)) for j in range(2)]

    specs = dict(
        hq=pcols(OFF_HQ), v=pcols(OFF_HI), z=pcols(OFF_ZF, OFF_ZB - OFF_ZF),
        shared=pl.BlockSpec((c, HG_DIM), lambda d, i: (chunk(d, i), 0)),
        per_dir=pl.BlockSpec((1, c, HG_DIM), lambda d, i: (d, chunk(d, i), 0)),
        vec=pl.BlockSpec((1, 1, HG_DIM), lambda d, i: (d, 0, 0)),
        seg=pl.BlockSpec((1, (HG_LEVELS + 2) * c, c), lambda d, i: (d, 0, 0)),
        sums=pl.BlockSpec((1, c, (HG_LEVELS + 1) * c), lambda d, i: (d, 0, 0)),
        later=pl.BlockSpec((1, HG_LEVELS, c, HG_HEAD_DIM), lambda d, i: (d, 0, 0, 0)),
        same=pl.BlockSpec((HG_LEVELS + 1, c, c), lambda d, i: (0, 0, 0)),
        state=pl.BlockSpec((1, HG_HEADS, 1, HG_HEAD_DIM, HG_HEAD_DIM), lambda d, i: (d, 0, chunk(d, i), 0, 0)),
        weights=pl.BlockSpec((1, HG_HEADS, 1, c, c), lambda d, i: (d, 0, chunk(d, i), 0, 0)),
        levels=pl.BlockSpec((1, HG_HEADS, 1, HG_LEVELS, c, HG_HEAD_DIM), lambda d, i: (d, 0, chunk(d, i), 0, 0, 0)),
        kept=pl.BlockSpec((1, HG_KEPT, c, HG_DIM), lambda d, i: (d, 0, chunk(d, i), 0)))
    return nc, specs


def _hg_head(refs, hh):
    off = (hh % 2) * HG_HEAD_DIM
    return refs[hh // 2][:, off:off + HG_HEAD_DIM]


def _hg_lanes(hh):
    return slice(hh * HG_HEAD_DIM, (hh + 1) * HG_HEAD_DIM)


def _hg_exps(seg_ref, f):
    c = HG_CHUNK
    args = _split_dot(seg_ref[0], jnp.log(f))
    return [jnp.exp(args[j * c:(j + 1) * c]) for j in range(HG_LEVELS + 2)]


def _hg_last_row(a, mirrored):
    return jnp.where(mirrored, a[0:1, :], a[HG_CHUNK - 1:HG_CHUNK, :])


def _hgrn_fwd(p, a0, a1, seg, masks, name):
    n = p.shape[0]
    nc, sp = _hg_specs(n, True)

    def body(hq0, hq1, z0, z1, v0, v1, a0_ref, a1_ref, seg_ref, later_ref, same_ref, o_ref, s0_ref, a_ref, e_ref, g_ref, st):
        @pl.when(pl.program_id(1) == 0)
        def _():
            st[...] = jnp.zeros_like(st)

        mirrored = pl.program_id(0) == 1
        for hh in range(HG_HEADS):
            ln = _hg_lanes(hh)
            hqv = _hg_head((hq0, hq1), hh)
            q, f, k, sg, _ = _hg_gates(hqv, _hg_head((z0, z1), hh), a0_ref[0, :, ln], a1_ref[0, :, ln])
            vv = _hg_head((v0, v1), hh)
            ex = _hg_exps(seg_ref, f)
            for lev in range(HG_LEVELS):
                e_ref[0, hh, 0, lev] = ex[lev + 1].astype(e_ref.dtype)
            sq = _sigmoid(hqv)
            for j, kept in enumerate((q, k, f, sg, sq * (1.0 + hqv * (1.0 - sq)), ex[0], ex[HG_LEVELS + 1])):
                g_ref[0, j, :, ln] = kept
            a = _hg_intra(q, k, ex, later_ref, same_ref).astype(MXU_DTYPE)
            a_ref[0, hh, 0] = a
            s_t = st[hh]
            s0_ref[0, hh, 0] = s_t
            o_ref[0, :, ln] = _dot(a, vv) + _dot(q * ex[0], s_t, "nt")
            st[hh] = s_t * _hg_last_row(ex[0], mirrored) + _dot(vv, k * ex[HG_LEVELS + 1], "tn")

    return pl.pallas_call(
        body, name=name, grid=(2, nc), in_specs=sp["hq"] + sp["z"] + sp["v"] + [sp["vec"], sp["vec"], sp["seg"], sp["later"], sp["same"]],
        out_specs=[sp["per_dir"], sp["state"], sp["weights"], sp["levels"], sp["kept"]],
        out_shape=[SDS((2, n, HG_DIM), F32), SDS((2, HG_HEADS, nc, HG_HEAD_DIM, HG_HEAD_DIM), F32),
                   SDS((2, HG_HEADS, nc, HG_CHUNK, HG_CHUNK), MXU_DTYPE),
                   SDS((2, HG_HEADS, nc, HG_LEVELS, HG_CHUNK, HG_HEAD_DIM), MXU_DTYPE), SDS((2, HG_KEPT, n, HG_DIM), F32)],
        scratch_shapes=[pltpu.VMEM((HG_HEADS, HG_HEAD_DIM, HG_HEAD_DIM), F32)],
        compiler_params=_cp("parallel", "arbitrary"))(p, p, p, p, p, p, a0, a1, seg, *masks)


def _hgrn_bwd(p, a0, a1, masks, gp, gn, do, s0, a, e, kept, name):
    n = p.shape[0]
    nc, sp = _hg_specs(n, False)


    def body(v0, v1, a0_ref, a1_ref, later_ref, same_ref, gp_ref, gn_ref, do_ref, s0_ref, a_ref, e_ref, g_ref,
             dhq_ref, dz_ref, dv_ref, dlb_ref, rt):
        @pl.when(pl.program_id(1) == 0)
        def _():
            rt[...] = jnp.zeros_like(rt)
            dlb_ref[...] = jnp.zeros_like(dlb_ref)

        mirrored = pl.program_id(0) == 1
        for hh in range(HG_HEADS):
            ln = _hg_lanes(hh)
            q, k, f, sg, dsilu, e_first, e_last = (g_ref[0, j, :, ln] for j in range(HG_KEPT))
            lb = _sigmoid(a0_ref[0, :, ln] - a1_ref[0, :, ln])
            vv, dov = _hg_head((v0, v1), hh), do_ref[:, ln]
            ex = [e_first] + [e_ref[0, hh, 0, lev].astype(F32) for lev in range(HG_LEVELS)] + [e_last]
            a = a_ref[0, hh, 0]
            da = _dot(dov, vv, "nt")
            diag = jnp.sum(dov * vv, axis=-1, keepdims=True)
            s_t = s0_ref[0, hh, 0]
            r_t = rt[hh]
            k_end = k * ex[HG_LEVELS + 1]
            dv_ref[0, :, ln] = _dot(a, dov, "tn") + _dot(k_end, r_t, "nt")
            dq_inter = ex[0] * _dot(dov, s_t)
            dk_inter = ex[HG_LEVELS + 1] * _dot(vv, r_t)
            dq = diag * k + dq_inter
            dk = diag * q + dk_inter
            q_terms, k_terms = [q * dq_inter], [k * dk_inter]
            for lev in range(HG_LEVELS):
                qs, ks, e_q, e_k = _hg_level(q, k, ex, later_ref, lev)
                pairs = da * same_ref[lev]
                q_part = e_q * _dot(pairs, ks)
                k_part = e_k * _dot(pairs, qs, "tn")
                dq, dk = dq + q_part, dk + k_part
                q_terms.append(q * q_part)
                k_terms.append(k * k_part)
            decay = _hg_last_row(ex[0], mirrored)
            rt[hh] = r_t * decay + _dot(dov, q * ex[0], "tn")
            later = decay * jnp.sum(s_t * r_t, axis=0, keepdims=True)
            dlf = _dot(gp_ref[0], jnp.concatenate(q_terms, axis=0)) + _dot(gn_ref[0], jnp.concatenate(k_terms, axis=0)) + later
            df = dlf / f - dk
            dz_ref[0, :, ln] = df * (1.0 - lb) * sg * (1.0 - sg)
            dlb_ref[0, :, ln] += jnp.sum(df * (1.0 - sg), axis=0, keepdims=True)
            dhq_ref[0, :, ln] = dq * dsilu

    out = SDS((2, n, HG_DIM), F32)
    return pl.pallas_call(
        body, name=name, grid=(2, nc),
        in_specs=sp["v"] + [sp["vec"], sp["vec"], sp["later"], sp["same"], sp["sums"], sp["sums"],
                            sp["shared"], sp["state"], sp["weights"], sp["levels"], sp["kept"]],
        out_specs=[sp["per_dir"], sp["per_dir"], sp["per_dir"], sp["vec"]], out_shape=[out, out, out, SDS((2, 1, HG_DIM), F32)],
        scratch_shapes=[pltpu.VMEM((HG_HEADS, HG_HEAD_DIM, HG_HEAD_DIM), F32)],
        compiler_params=_cp("parallel", "arbitrary"))(p, p, a0, a1, *masks, gp, gn, do, s0, a, e, kept)


def _hg_post(o2, p, g, name):
    n = p.shape[0]
    tr = min(ROW_TILE, n)
    w = 2 * HG_HEAD_DIM

    def body(of_ref, ob_ref, hg_ref, g_ref, o_ref):
        for j in range(2):
            sl = slice(j * HG_HEAD_DIM, (j + 1) * HG_HEAD_DIM)
            o = of_ref[0, :, sl] + ob_ref[0, :, sl]
            hg = hg_ref[:, sl]
            o_ref[:, sl] = (o * _rstd(o) * g_ref[...] * (hg * _sigmoid(hg))).astype(o_ref.dtype)

    blk = pl.BlockSpec((tr, w), lambda i, j: (i, j))
    dirs = [pl.BlockSpec((1, tr, w), lambda i, j, d=d: (d, i, j)) for d in range(2)]
    return pl.pallas_call(
        body, name=name, grid=(n // tr, HG_DIM // w),
        in_specs=dirs + [pl.BlockSpec((tr, w), lambda i, j: (i, OFF_HG // w + j)), pl.BlockSpec((1, HG_HEAD_DIM), lambda i, j: (0, 0))],
        out_specs=blk, out_shape=SDS((n, HG_DIM), MXU_DTYPE), compiler_params=_cp("parallel", "parallel"))(o2, o2, p, g)


def _hg_post_bwd(o2, p, g, dcat, name, after=()):
    n = p.shape[0]
    tr = min(ROW_TILE, n)
    w = 2 * HG_HEAD_DIM
    after, after_specs = _unread(after)

    def body(of_ref, ob_ref, hg_ref, g_ref, d_ref, *rest):
        do_ref, dhg_ref, dg_ref = rest[len(after):]

        @pl.when(pl.program_id(1) == 0)
        def _():
            dg_ref[...] = jnp.zeros_like(dg_ref)

        for j in range(2):
            sl = slice(j * HG_HEAD_DIM, (j + 1) * HG_HEAD_DIM)
            o = of_ref[0, :, sl] + ob_ref[0, :, sl]
            hg = hg_ref[:, sl]
            d = d_ref[:, sl].astype(F32)
            sg = _sigmoid(hg)
            on = o * _rstd(o) * g_ref[...]
            dhg_ref[:, sl] = (d * on * sg * (1.0 + hg * (1.0 - sg))).astype(dhg_ref.dtype)
            dx, dg = _rms_bwd(o, g_ref[...], d * hg * sg)
            do_ref[:, sl] = dx
            dg_ref[0, :, sl] += dg

    blk = pl.BlockSpec((tr, w), lambda j, i: (i, j))
    dirs = [pl.BlockSpec((1, tr, w), lambda j, i, d=d: (d, i, j)) for d in range(2)]
    return pl.pallas_call(
        body, name=name, grid=(HG_DIM // w, n // tr),
        in_specs=dirs + [pl.BlockSpec((tr, w), lambda j, i: (i, OFF_HG // w + j)), pl.BlockSpec((1, HG_HEAD_DIM), lambda j, i: (0, 0)),
                         pl.BlockSpec((tr, w), lambda j, i: (i, ATT_Q_DIM // w + j))] + after_specs,
        out_specs=[blk, blk, pl.BlockSpec((1, 1, w), lambda j, i: (j, 0, 0))],
        out_shape=[SDS((n, HG_DIM), F32), SDS((n, HG_DIM), MXU_DTYPE), SDS((HG_DIM // w, 1, w), F32)],
        compiler_params=_cp("parallel", "arbitrary"))(o2, o2, p, g, dcat, *after)


XATT_TQ = 512


def _xattn_fwd(q, kv, name):
    n, nm = q.shape[0], kv.shape[0]
    tq = min(XATT_TQ, n)
    scale = X_HEAD_DIM ** -0.5

    def body(q_ref, k_ref, v_ref, o_ref):
        s = _dot(q_ref[...], k_ref[...], "nt") * scale
        e = jnp.exp(s - jnp.max(s, axis=-1, keepdims=True))
        o_ref[...] = _dot(e / jnp.sum(e, axis=-1, keepdims=True), v_ref[...]).astype(o_ref.dtype)

    qb = pl.BlockSpec((tq, X_HEAD_DIM), lambda h, i: (i, h))
    return pl.pallas_call(
        body, name=name, grid=(X_HEADS, n // tq),
        in_specs=[qb, pl.BlockSpec((nm, X_HEAD_DIM), lambda h, i: (0, h)), pl.BlockSpec((nm, X_HEAD_DIM), lambda h, i: (0, X_HEADS + h))],
        out_specs=qb, out_shape=SDS(q.shape, MXU_DTYPE), compiler_params=_cp("parallel", "parallel"))(q, kv, kv)


def _xattn_bwd(q, kv, do, name, after=()):
    n, nm = q.shape[0], kv.shape[0]
    tq = min(XATT_TQ, n)
    scale = X_HEAD_DIM ** -0.5
    after, after_specs = _unread(after)

    def body(q_ref, k_ref, v_ref, do_ref, *rest):
        dq_ref, dk_ref, dv_ref = rest[len(after):]

        @pl.when(pl.program_id(1) == 0)
        def _():
            dk_ref[...] = jnp.zeros_like(dk_ref)
            dv_ref[...] = jnp.zeros_like(dv_ref)

        qv, dov = q_ref[...], do_ref[...]
        s = _dot(qv, k_ref[...], "nt") * scale
        e = jnp.exp(s - jnp.max(s, axis=-1, keepdims=True))
        p = e / jnp.sum(e, axis=-1, keepdims=True)
        dp = _dot(dov, v_ref[...], "nt")
        ds = p * (dp - jnp.sum(p * dp, axis=-1, keepdims=True)) * scale
        dq_ref[...] = _dot(ds, k_ref[...]).astype(dq_ref.dtype)
        dk_ref[...] += _dot(ds, qv, "tn")
        dv_ref[...] += _dot(p, dov, "tn")

    qb = pl.BlockSpec((tq, X_HEAD_DIM), lambda h, i: (i, h))
    kb = pl.BlockSpec((nm, X_HEAD_DIM), lambda h, i: (0, h))
    return pl.pallas_call(
        body, name=name, grid=(X_HEADS, n // tq),
        in_specs=[qb, kb, pl.BlockSpec((nm, X_HEAD_DIM), lambda h, i: (0, X_HEADS + h)), qb] + after_specs, out_specs=[qb, kb, kb],
        out_shape=[SDS(q.shape, MXU_DTYPE), SDS((nm, X_HEADS * X_HEAD_DIM), F32), SDS((nm, X_HEADS * X_HEAD_DIM), F32)],
        compiler_params=_cp("parallel", "arbitrary"))(q, kv, kv, do, *after)


def _edge_rows(shape):
    row = lax.broadcasted_iota(jnp.int32, shape, 0)
    return row == 0, row == shape[0] - 1


def _shift_rows(u, down, edges):
    if down:
        return jnp.where(edges[0], 0.0, pltpu.roll(u, 1, axis=0))
    return jnp.where(edges[1], 0.0, pltpu.roll(u, u.shape[0] - 1, axis=0))


def _conv(u, w, b, edges):
    return b + _shift_rows(u, True, edges) * w[0:1, :] + u * w[1:2, :] + _shift_rows(u, False, edges) * w[2:3, :]


def _ff_specs(n):
    gate = lambda rows: pl.BlockSpec((rows, FF_COLS), lambda j: (0, j))
    val = lambda rows: pl.BlockSpec((rows, FF_COLS), lambda j: (0, FF_BLOCKS + j))
    return [gate(n), val(n), gate(3), val(3), gate(1), val(1)], gate


def _conv_gate(u, cw, cb, name):
    n = u.shape[0]
    ins, gate_blk = _ff_specs(n)

    def body(ug_ref, uv_ref, wg_ref, wv_ref, bg_ref, bv_ref, o_ref):
        edges = _edge_rows(ug_ref.shape)
        gate = _conv(ug_ref[...], wg_ref[...], bg_ref[...], edges)
        val = _conv(uv_ref[...], wv_ref[...], bv_ref[...], edges)
        o_ref[...] = (gate * _sigmoid(gate) * val).astype(o_ref.dtype)

    return pl.pallas_call(
        body, name=name, grid=(FF_BLOCKS,), in_specs=ins, out_specs=gate_blk(n), out_shape=SDS((n, D_FF), MXU_DTYPE),
        compiler_params=_cp("parallel"))(u, u, cw, cw, cb, cb)


def _conv_gate_bwd(u, cw, cb, da, name, after=()):
    n = u.shape[0]
    ins, gate_blk = _ff_specs(n)
    after, after_specs = _unread(after)

    def side(dacc, u, w, edges, du_ref, dw_ref, db_ref):
        nxt, prv = _shift_rows(dacc, False, edges), _shift_rows(dacc, True, edges)
        du_ref[...] = (nxt * w[0:1, :] + dacc * w[1:2, :] + prv * w[2:3, :]).astype(du_ref.dtype)
        db_ref[...] = jnp.sum(dacc, axis=0, keepdims=True)
        dw_ref[0:1, :] = jnp.sum(nxt * u, axis=0, keepdims=True)
        dw_ref[1:2, :] = jnp.sum(dacc * u, axis=0, keepdims=True)
        dw_ref[2:3, :] = jnp.sum(prv * u, axis=0, keepdims=True)

    def body(ug_ref, uv_ref, wg_ref, wv_ref, bg_ref, bv_ref, da_ref, *rest):
        dug_ref, duv_ref, dwg_ref, dwv_ref, dbg_ref, dbv_ref = rest[len(after):]
        ug, uv = ug_ref[...], uv_ref[...]
        edges = _edge_rows(ug.shape)
        gate = _conv(ug, wg_ref[...], bg_ref[...], edges)
        val = _conv(uv, wv_ref[...], bv_ref[...], edges)
        sg = _sigmoid(gate)
        dav = da_ref[...].astype(F32)
        side(dav * val * sg * (1.0 + gate * (1.0 - sg)), ug, wg_ref[...], edges, dug_ref, dwg_ref, dbg_ref)
        side(dav * gate * sg, uv, wv_ref[...], edges, duv_ref, dwv_ref, dbv_ref)

    return pl.pallas_call(
        body, name=name, grid=(FF_BLOCKS,), in_specs=ins + [gate_blk(n)] + after_specs,
        out_specs=[gate_blk(n), gate_blk(n), gate_blk(3), gate_blk(3), gate_blk(1), gate_blk(1)],
        out_shape=[SDS((n, D_FF), MXU_DTYPE)] * 2 + [SDS((3, D_FF), F32)] * 2 + [SDS((1, D_FF), F32)] * 2,
        compiler_params=_cp("parallel"))(u, u, cw, cw, cb, cb, da, *after)


def _adamw(w, g, m, v, name):
    r, c = w.shape[-2:]
    tr = _row_tile(r, ELEMENTWISE_ROWS)
    assert w.ndim == 2 or w.shape[:-2] == (1,), (name, w.shape)

    def body(w_ref, g_ref, m_ref, v_ref, d_ref, mo_ref, vo_ref, go_ref):
        gv = g_ref[...]
        go_ref[...] = gv
        mn = ADAM_B1 * m_ref[...] + (1.0 - ADAM_B1) * gv
        vn = ADAM_B2 * v_ref[...] + (1.0 - ADAM_B2) * gv * gv
        m_hat = mn / (1.0 - ADAM_B1 ** ADAM_STEP)
        v_hat = vn / (1.0 - ADAM_B2 ** ADAM_STEP)
        d_ref[...] = -ADAM_LR * (m_hat / (jnp.sqrt(v_hat) + ADAM_EPS) + ADAM_WD * w_ref[...])
        mo_ref[...] = mn
        vo_ref[...] = vn

    blk = pl.BlockSpec((tr, c), lambda i: (i, 0)) if w.ndim == 2 else pl.BlockSpec((1, tr, c), lambda i: (0, i, 0))
    out = SDS(w.shape, F32)
    return pl.pallas_call(body, name=name, grid=(r // tr,), in_specs=[blk] * 4, out_specs=[blk] * 4, out_shape=[out] * 4,
                          compiler_params=_cp("parallel"))(w, g, m, v)


ANY = pl.BlockSpec(memory_space=pl.ANY)


def _place():
    x, y, c = lax.axis_index("x"), lax.axis_index("y"), lax.axis_index("c")
    return x, y, c, [(1 - x, y), (x, 1 - y), (1 - x, 1 - y)]


def _join_halves(bufs, name):
    nt = len(bufs)

    def body(*refs):
        outs = refs[nt:2 * nt]
        send, recv = refs[2 * nt:]
        x, y, c, _ = _place()
        cps = [pltpu.make_async_remote_copy(src_ref=outs[t].at[c], dst_ref=outs[t].at[c], send_sem=send.at[t], recv_sem=recv.at[t],
                                            device_id=(x, y, 1 - c), device_id_type=MESH) for t in range(nt)]
        for cp in cps:
            cp.start()
        for t in range(nt):
            theirs = outs[t].at[1 - c]
            pltpu.make_async_remote_copy(src_ref=theirs, dst_ref=theirs, send_sem=send.at[t], recv_sem=recv.at[t],
                                         device_id=(x, y, 1 - c), device_id_type=MESH).wait_recv()
        for cp in cps:
            cp.wait_send()

    return pl.pallas_call(
        body, name=name, in_specs=[ANY] * nt, out_specs=[ANY] * nt, out_shape=[SDS(b.shape, b.dtype) for b in bufs],
        input_output_aliases={t: t for t in range(nt)},
        scratch_shapes=[pltpu.SemaphoreType.DMA((nt,))] * 2,
        compiler_params=pltpu.CompilerParams(has_side_effects=True))(*bufs)


HBM = pl.BlockSpec(memory_space=pltpu.HBM)
SEM = pl.BlockSpec(memory_space=pltpu.SEMAPHORE)
TOKEN = pl.BlockSpec(memory_space=pltpu.VMEM)
TOKEN_SHAPE = SDS((8, 128), F32)
PEERS = 7


def _in_hbm(a):
    return pltpu.with_memory_space_constraint(a, pltpu.HBM)


def _split_params():
    return pltpu.CompilerParams(has_side_effects=pltpu.SideEffectType.DATAFLOW_SIDE_EFFECTING)


def _gather_start(shards, name, after=()):
    nt = len(shards)
    after, after_specs = _unread(after)

    def body(*refs):
        ins, lands = refs[:nt], refs[nt:2 * nt]
        outs = refs[2 * nt + len(after):]
        sends, recvs = outs[:nt], outs[nt:2 * nt]
        x, y, c, chips = _place()
        me = 2 * x + y
        for t in range(nt):
            h = ins[t].shape[0] // 2
            mine = pl.ds(c * h, h)
            for j, (cx, cy) in enumerate(chips):
                for dc in range(2):
                    pltpu.make_async_remote_copy(src_ref=ins[t].at[mine], dst_ref=lands[t].at[me, mine], send_sem=sends[t].at[2 * j + dc],
                                                 recv_sem=recvs[t].at[2 * j + c], device_id=(cx, cy, dc), device_id_type=MESH).start()
            pltpu.make_async_remote_copy(src_ref=ins[t], dst_ref=lands[t].at[me], send_sem=sends[t].at[PEERS - 1], recv_sem=recvs[t].at[PEERS - 1],
                                         device_id=(x, y, 1 - c), device_id_type=MESH).start()
        outs[-1][...] = jnp.zeros(TOKEN_SHAPE.shape, F32)

    lands = [lax.empty((4,) + s.shape, s.dtype) for s in shards]
    out = pl.pallas_call(
        body, name=name, in_specs=[HBM] * (2 * nt) + after_specs, out_specs=[SEM] * (2 * nt) + [HBM] * (2 * nt) + [TOKEN],
        out_shape=[pltpu.SemaphoreType.DMA((PEERS,))] * (2 * nt)
        + [pltpu.HBM(s.shape, s.dtype) for s in shards] + [pltpu.HBM(l.shape, l.dtype) for l in lands] + [TOKEN_SHAPE],
        input_output_aliases={t: 2 * nt + t for t in range(2 * nt)}, compiler_params=_split_params())(
            *[_in_hbm(s) for s in shards], *[_in_hbm(l) for l in lands], *after)
    return out[:nt], out[nt:2 * nt], out[2 * nt:3 * nt], out[3 * nt:4 * nt], out[-1]


def _gather_wait(sends, recvs, shards, lands, after, name):
    nt = len(shards)

    def body(*refs):
        ins, lands_ref = refs[:nt], refs[nt:2 * nt]
        send_refs, recv_refs = refs[2 * nt:3 * nt], refs[3 * nt:4 * nt]
        x, y, c, chips = _place()
        for t in range(nt):
            h = ins[t].shape[0] // 2
            for j, (cx, cy) in enumerate(chips):
                for cs in range(2):
                    blk = lands_ref[t].at[2 * cx + cy, pl.ds(cs * h, h)]
                    pltpu.make_async_remote_copy(src_ref=blk, dst_ref=blk, send_sem=send_refs[t].at[2 * j + cs], recv_sem=recv_refs[t].at[2 * j + cs],
                                                 device_id=(cx, cy, cs), device_id_type=MESH).wait()
            blk = lands_ref[t].at[2 * x + y]
            pltpu.make_async_remote_copy(src_ref=blk, dst_ref=blk, send_sem=send_refs[t].at[PEERS - 1], recv_sem=recv_refs[t].at[PEERS - 1],
                                         device_id=(x, y, 1 - c), device_id_type=MESH).wait()

    out = pl.pallas_call(
        body, name=name, in_specs=[HBM] * (2 * nt) + [SEM] * (2 * nt) + [ANY], out_specs=[HBM] * (2 * nt),
        out_shape=[pltpu.HBM(s.shape, s.dtype) for s in shards] + [pltpu.HBM(l.shape, l.dtype) for l in lands],
        input_output_aliases={t: t for t in range(2 * nt)}, compiler_params=_split_params())(*shards, *lands, *sends, *recvs, after)
    return out[nt:]


def _gather_pieces_start(shard, name, after=()):
    h = shard.shape[0] // 2
    after, after_specs = _unread(after)

    def body(src, land, *rest):
        send, recv = rest[len(after):len(after) + 2]
        x, y, c, chips = _place()
        me = 2 * x + y
        mine = pl.ds(c * h, h)
        for j, (cx, cy) in enumerate(chips):
            pltpu.make_async_remote_copy(src_ref=src.at[mine], dst_ref=land.at[me, mine], send_sem=send.at[j], recv_sem=recv.at[j],
                                         device_id=(cx, cy, c), device_id_type=MESH).start()
        pltpu.make_async_remote_copy(src_ref=src, dst_ref=land.at[me], send_sem=send.at[len(chips)], recv_sem=recv.at[len(chips)],
                                     device_id=(x, y, 1 - c), device_id_type=MESH).start()
        rest[-1][...] = jnp.zeros(TOKEN_SHAPE.shape, F32)

    land = lax.empty((4,) + shard.shape, shard.dtype)
    return pl.pallas_call(
        body, name=name, in_specs=[HBM, HBM] + after_specs, out_specs=[SEM, SEM, HBM, HBM, TOKEN],
        out_shape=[pltpu.SemaphoreType.DMA((4,)), pltpu.SemaphoreType.DMA((4,)), pltpu.HBM(shard.shape, shard.dtype),
                   pltpu.HBM(land.shape, land.dtype), TOKEN_SHAPE],
        input_output_aliases={0: 2, 1: 3}, compiler_params=_split_params())(_in_hbm(shard), _in_hbm(land), *after)


def _gather_pieces_wait(send, recv, shard, land, after, name):
    h = shard.shape[0] // 2
    after, after_specs = _unread(after)

    def body(*refs):
        land_ref, send_ref, recv_ref = refs[1:4]
        x, y, c, chips = _place()
        for j, (cx, cy) in enumerate(chips):
            blk = land_ref.at[2 * cx + cy, pl.ds(c * h, h)]
            pltpu.make_async_remote_copy(src_ref=blk, dst_ref=blk, send_sem=send_ref.at[j], recv_sem=recv_ref.at[j],
                                         device_id=(cx, cy, c), device_id_type=MESH).wait()
        own = land_ref.at[2 * x + y]
        pltpu.make_async_remote_copy(src_ref=own, dst_ref=own, send_sem=send_ref.at[len(chips)], recv_sem=recv_ref.at[len(chips)],
                                     device_id=(x, y, 1 - c), device_id_type=MESH).wait()

    out = pl.pallas_call(
        body, name=name, in_specs=[HBM, HBM, SEM, SEM] + after_specs, out_specs=[HBM, HBM],
        out_shape=[pltpu.HBM(shard.shape, shard.dtype), pltpu.HBM(land.shape, land.dtype)],
        input_output_aliases={0: 0, 1: 1}, compiler_params=_split_params())(shard, land, send, recv, *after)
    return out[1]


def _pass_pieces(land, name):
    h = land.shape[1] // 2

    def body(_, out, send, recv):
        x, y, c, chips = _place()

        def piece(j, cc):
            cx, cy = chips[j]
            blk = out.at[2 * cx + cy, pl.ds(cc * h, h)]
            return pltpu.make_async_remote_copy(src_ref=blk, dst_ref=blk, send_sem=send.at[j], recv_sem=recv.at[j],
                                                device_id=(x, y, 1 - c), device_id_type=MESH)

        for j in range(len(chips)):
            piece(j, c).start()
        for j in range(len(chips)):
            piece(j, 1 - c).wait_recv()
        for j in range(len(chips)):
            piece(j, c).wait_send()

    return pl.pallas_call(
        body, name=name, in_specs=[ANY], out_specs=ANY, out_shape=SDS(land.shape, land.dtype), input_output_aliases={0: 0},
        scratch_shapes=[pltpu.SemaphoreType.DMA((3,))] * 2, compiler_params=pltpu.CompilerParams(has_side_effects=True))(land)


def _scatter_start(g, name):
    _, r, c_ = g.shape
    h = r // 2

    def body(g_ref, land, send, recv, g_thru, land_thru, token):
        x, y, c, chips = _place()
        for j, (cx, cy) in enumerate(chips):
            for dc in range(2):
                pltpu.make_async_remote_copy(src_ref=g_ref.at[2 * cx + cy, pl.ds(dc * h, h)], dst_ref=land.at[2 * j + c], send_sem=send.at[2 * j + dc],
                                             recv_sem=recv.at[2 * j + c], device_id=(cx, cy, dc), device_id_type=MESH).start()
        pltpu.make_async_remote_copy(src_ref=g_ref.at[2 * x + y, pl.ds((1 - c) * h, h)], dst_ref=land.at[PEERS - 1], send_sem=send.at[PEERS - 1],
                                     recv_sem=recv.at[PEERS - 1], device_id=(x, y, 1 - c), device_id_type=MESH).start()
        token[...] = jnp.zeros(TOKEN_SHAPE.shape, F32)

    land = lax.empty((PEERS, h, c_), g.dtype)
    return pl.pallas_call(
        body, name=name, in_specs=[HBM, HBM], out_specs=[SEM, SEM, HBM, HBM, TOKEN],
        out_shape=[pltpu.SemaphoreType.DMA((PEERS,)), pltpu.SemaphoreType.DMA((PEERS,)), pltpu.HBM(g.shape, g.dtype),
                   pltpu.HBM(land.shape, land.dtype), TOKEN_SHAPE],
        input_output_aliases={0: 2, 1: 3}, compiler_params=_split_params())(_in_hbm(g), _in_hbm(land))


def _scatter_wait(started, after, name):
    nt = len(started)

    def body(*refs):
        lands = refs[nt:2 * nt]
        sends, recvs = refs[2 * nt:3 * nt], refs[3 * nt:4 * nt]
        x, y, c, chips = _place()
        peers = [(cx, cy, dc) for cx, cy in chips for dc in range(2)] + [(x, y, 1 - c)]
        for t in range(nt):
            for k, peer in enumerate(peers):
                blk = lands[t].at[k]
                pltpu.make_async_remote_copy(src_ref=blk, dst_ref=blk, send_sem=sends[t].at[k], recv_sem=recvs[t].at[k],
                                             device_id=peer, device_id_type=MESH).wait()

    gs, lands = [s[2] for s in started], [s[3] for s in started]
    after, after_specs = _unread(after)
    out = pl.pallas_call(
        body, name=name, in_specs=[HBM] * (2 * nt) + [SEM] * (2 * nt) + after_specs, out_specs=[HBM] * (2 * nt),
        out_shape=[pltpu.HBM(a.shape, a.dtype) for a in gs + lands],
        input_output_aliases={t: t for t in range(2 * nt)}, compiler_params=_split_params())(
            *gs, *lands, *[s[0] for s in started], *[s[1] for s in started], *after)
    return out[:nt], out[nt:]


def _flips():
    return [(dx, dy, dc) for dx in range(2) for dy in range(2) for dc in range(2) if (dx, dy, dc) != (0, 0, 0)]


def _flipped(x, y, c, flips):
    dx, dy, dc = flips
    return (1 - x if dx else x, 1 - y if dy else y, 1 - c if dc else c)


def _small_start(v, name, after=()):
    after, after_specs = _unread(after)

    def body(v_ref, land, *rest):
        send, recv = rest[len(after):len(after) + 2]
        x, y, c, _ = _place()
        for j, flips in enumerate(_flips()):
            pltpu.make_async_remote_copy(src_ref=v_ref, dst_ref=land.at[4 * x + 2 * y + c], send_sem=send.at[j], recv_sem=recv.at[j],
                                         device_id=_flipped(x, y, c, flips), device_id_type=MESH).start()
        rest[-1][...] = jnp.zeros(TOKEN_SHAPE.shape, F32)

    land = lax.empty((8,) + v.shape, v.dtype)
    return pl.pallas_call(
        body, name=name, in_specs=[HBM, HBM] + after_specs, out_specs=[SEM, SEM, HBM, HBM, TOKEN],
        out_shape=[pltpu.SemaphoreType.DMA((PEERS,)), pltpu.SemaphoreType.DMA((PEERS,)), pltpu.HBM(v.shape, v.dtype),
                   pltpu.HBM(land.shape, land.dtype), TOKEN_SHAPE],
        input_output_aliases={0: 2, 1: 3}, compiler_params=_split_params())(_in_hbm(v), _in_hbm(land), *after)


def _small_wait(send, recv, v, land, after, name):
    after, after_specs = _unread(after)

    def body(v_ref, land_ref, send_ref, recv_ref, *rest):
        x, y, c, _ = _place()
        for j, flips in enumerate(_flips()):
            px, py, pc = _flipped(x, y, c, flips)
            blk = land_ref.at[4 * px + 2 * py + pc]
            pltpu.make_async_remote_copy(src_ref=blk, dst_ref=blk, send_sem=send_ref.at[j], recv_sem=recv_ref.at[j],
                                         device_id=(px, py, pc), device_id_type=MESH).wait()

    return pl.pallas_call(
        body, name=name, in_specs=[HBM, HBM, SEM, SEM] + after_specs, out_specs=[HBM, HBM],
        out_shape=[pltpu.HBM(v.shape, v.dtype), pltpu.HBM(land.shape, land.dtype)],
        input_output_aliases={0: 0, 1: 1}, compiler_params=_split_params())(v, land, send, recv, *after)


def _sum_small(v, land, name):
    def body(v_ref, land_ref, o_ref):
        x, y, c, _ = _place()
        me = 4 * x + 2 * y + c
        acc = jnp.where(me == 0, v_ref[...], land_ref[0])
        for d in range(1, 8):
            acc = acc + jnp.where(me == d, v_ref[...], land_ref[d])
        o_ref[...] = acc

    vm = pl.BlockSpec(memory_space=pltpu.VMEM)
    return pl.pallas_call(body, name=name, in_specs=[vm, vm], out_specs=vm, out_shape=SDS(v.shape, F32))(v, land)


def _sum_devices(g, land, me, core, name):
    npeer, h, c = land.shape
    tr = _row_tile(h, 2 * ELEMENTWISE_ROWS)
    steps = h // tr

    def body(ix_ref, own_ref, land_ref, o_ref):
        acc = own_ref[0].astype(F32)
        for j in range(npeer):
            acc = acc + land_ref[j].astype(F32)
        o_ref[0] = acc

    grid_spec = pltpu.PrefetchScalarGridSpec(
        num_scalar_prefetch=1, grid=(steps,),
        in_specs=[pl.BlockSpec((1, tr, c), lambda i, ix: (ix[0], ix[1] * steps + i, 0)), pl.BlockSpec((npeer, tr, c), lambda i, ix: (0, i, 0))],
        out_specs=pl.BlockSpec((1, tr, c), lambda i, ix: (ix[1], i, 0)))
    return pl.pallas_call(body, name=name, grid_spec=grid_spec, out_shape=SDS((2, h, c), F32),
                          compiler_params=_cp("parallel"))(jnp.stack([me, core]), g, land)


def _pack_small(parts):
    flat = jnp.concatenate([p.reshape(-1) for p in parts])
    total = flat.shape[0]
    rows = -(-total // 1024) * 8
    return jnp.pad(flat, (0, rows * 128 - total)).reshape(rows, 128)


def _unpack_small(packed, shapes):
    flat = packed.reshape(-1)
    out, off = [], 0
    for s in shapes:
        size = int(np.prod(s))
        out.append(flat[off:off + size].reshape(s))
        off += size
    return out


def _local_step(x, mem, target, w_in_t, first_after, mid_weights, ffn_weights, on_grad, gains, conv_w, conv_b, hg_lb):
    n = x.shape[0]
    cos, sin = _rope_tables(n)
    seg = _hg_segments()
    gp, gn = _hg_pair_sums()
    masks = _hg_level_masks()
    gq2 = jnp.tile(gains["q_norm_g"], (1, 2))
    gk2 = jnp.tile(gains["k_norm_g"], (1, 2))
    a0 = hg_lb[:, 0:1, :]
    a1 = hg_lb[:, 1:2, :]

    p, h1 = _norm_mm(x, gains["pre_mix_g"], w_in_t, F32, TOKEN_TILE, 1664, "in_proj", after=(first_after,), w_turned=True)
    qr, kr = _qk_prep(p, gq2, gk2, cos, sin, "qk_prep")
    heads = lambda a: a.reshape(n, ATT_KV_HEADS, ATT_HEAD_DIM).transpose(1, 0, 2)
    kh = heads(kr)
    vh = heads(p[:, OFF_AV:OFF_AV + ATT_KV_DIM].astype(MXU_DTYPE))
    att = _attn_fwd(qr, kh, vh, "attn_fwd")
    o2, s0, hg_a, hg_e, hg_kept = _hgrn_fwd(p, a0, a1, seg, masks, "hgrn_fwd")
    rec = _hg_post(o2, p, gains["hg_out_norm_g"], "hg_post")
    cat = jnp.concatenate([att, rec], axis=1)
    w_out, w_xq, w_xkv, w_xo = mid_weights(cat)
    mixed, x1 = _mm_resid_norm(cat, w_out, x, gains["post_mix_g"], 512, "out_proj_resid")
    xq, h2 = _norm_mm(x1, gains["pre_x_g"], w_xq, MXU_DTYPE, TOKEN_TILE, 1024, "xq_proj")
    kv, mn = _norm_mm(mem, gains["mem_norm_g"], w_xkv, MXU_DTYPE, 256, 2048, "xkv_proj")
    ox = _xattn_fwd(xq, kv, "xattn_fwd")
    xo, x2 = _mm_resid_norm(ox, w_xo, x1, gains["post_x_g"], 512, "xo_proj_resid")
    w_up = ffn_weights("w_up", x2)
    u, h3 = _norm_mm(x2, gains["pre_ffn_g"], w_up, F32, TOKEN_TILE, 1408, "up_proj")
    act = _conv_gate(u, conv_w, conv_b, "conv_gate")
    w_down = ffn_weights("w_down", act)
    dn, d3, loss = _mm_resid_norm(act, w_down, x2, gains["post_ffn_g"], 512, "down_proj_resid_loss", target=target)

    gs = {}
    d_act, d_dn, gs["post_ffn_g"] = _norm_bwd_mm(dn, gains["post_ffn_g"], d3, w_down, F32, 512, 1408, "ffn_post_bwd_down_dx")
    tok = on_grad("w_down", _mm(act, d_dn, "tn", WIRE_DTYPE, 1408, 1024, "down_dw"))
    du_g, du_v, dcw_g, dcw_v, dcb_g, dcb_v = _conv_gate_bwd(u, conv_w, conv_b, d_act, "conv_gate_bwd", after=(tok,))
    gs["conv_w"] = jnp.concatenate([dcw_g, dcw_v], axis=1)
    gs["conv_b"] = jnp.concatenate([dcb_g, dcb_v], axis=1)
    ff_shard = w_up.shape[2]
    g_up = _dw_by_owner(h3, du_g, ff_shard, 0, None, 512, "up_dw_gate")
    tok = on_grad("w_up", _dw_by_owner(h3, du_v, ff_shard, 2, g_up, 512, "up_dw_value"))
    d2, gs["pre_ffn_g"] = _dx_norm_bwd([(du_g, 0), (du_g, 1), (du_v, 0), (du_v, 1)], w_up, x2, gains["pre_ffn_g"], d3, 512,
                                       "up_dx_pre_bwd", after=(tok,))
    d_ox, d_xo, gs["post_x_g"] = _norm_bwd_mm(xo, gains["post_x_g"], d2, w_xo, MXU_DTYPE, 512, 1024, "x_post_bwd_xo_dx")
    tok = on_grad("w_xo", _mm(ox, d_xo, "tn", WIRE_DTYPE, 512, 1024, "xo_dw"))
    d_xq, d_k, d_v = _xattn_bwd(xq, kv, d_ox, "xattn_bwd", after=(tok,))
    d_kv = jnp.concatenate([d_k, d_v], axis=1).astype(MXU_DTYPE)
    tok = on_grad("w_xq", _mm(h2, d_xq, "tn", WIRE_DTYPE, 512, 1024, "xq_dw"))
    tok_kv = on_grad("w_xkv", _dw_by_owner(mn, d_kv, w_xkv.shape[2], 0, None, 512, "xkv_dw"))
    d1, gs["pre_x_g"] = _dx_norm_bwd([(d_xq, 0)], w_xq[None], x1, gains["pre_x_g"], d2, 512, "xq_dx_pre_bwd", after=(tok, tok_kv))
    d_mn = _mm_nt_parts([(d_kv, s) for s in range(4)], w_xkv, F32, 256, 1024, "xkv_dx")
    _, gs["mem_norm_g"] = _norm_bwd(mem, gains["mem_norm_g"], d_mn, None, MXU_DTYPE, "mem_norm_bwd")
    d_cat, d_mixed, gs["post_mix_g"] = _norm_bwd_mm(mixed, gains["post_mix_g"], d1, w_out, MXU_DTYPE, 512, 1024, "mix_post_bwd_out_dx")
    tok = on_grad("w_out", _mm(cat, d_mixed, "tn", WIRE_DTYPE, 512, 1024, "out_dw"))
    d_o, d_hg, dg_hg = _hg_post_bwd(o2, p, gains["hg_out_norm_g"], d_cat, "hg_post_bwd", after=(tok,))
    gs["hg_out_norm_g"] = dg_hg.reshape(HG_HEADS, HG_HEAD_DIM).sum(axis=0, keepdims=True)
    dhq2, dz2, dhv2, dlb = _hgrn_bwd(p, a0, a1, masks, gp, gn, d_o, s0, hg_a, hg_e, hg_kept, "hgrn_bwd")
    lb = jax.nn.sigmoid(a0 - a1)
    da0 = dlb * lb * (1.0 - lb)
    gs["hg_lb"] = jnp.concatenate([da0, -da0], axis=1)
    d_qr, d_kh, d_vh = _attn_bwd(qr, kh, vh, cat, d_cat, "attn_bwd")
    unheads = lambda a: a.transpose(2, 0, 1).reshape(n, ATT_KV_DIM)
    d_aq, d_ak, dgq, dgk = _qk_prep_bwd(p, gq2, gk2, cos, sin, d_qr, unheads(d_kh), "qk_prep_bwd")
    gs["q_norm_g"] = dgq.reshape(ATT_HEADS, ATT_HEAD_DIM).sum(axis=0, keepdims=True)
    gs["k_norm_g"] = dgk.reshape(ATT_KV_HEADS, ATT_HEAD_DIM).sum(axis=0, keepdims=True)
    d_p = jnp.concatenate([d_aq, d_ak, unheads(d_vh).astype(MXU_DTYPE), (dhq2[0] + dhq2[1]).astype(MXU_DTYPE),
                           dz2[0].astype(MXU_DTYPE), dz2[1].astype(MXU_DTYPE), (dhv2[0] + dhv2[1]).astype(MXU_DTYPE), d_hg], axis=1)
    tok = on_grad("w_in", _mm(d_p, h1, "tn", WIRE_DTYPE, 1664, 1024, "in_dw"))
    grad_x, gs["pre_mix_g"] = _dx_norm_bwd([(d_p, 0)], w_in_t[None], x, gains["pre_mix_g"], d1, 512, "in_dx_pre_bwd", after=(tok,),
                                           b_turned=True)
    return loss, grad_x, gs


MATS = ("w_in", "w_out", "w_xq", "w_xkv", "w_xo", "w_up", "w_down")
GAINS = ("pre_mix_g", "q_norm_g", "k_norm_g", "hg_out_norm_g", "post_mix_g", "pre_x_g", "mem_norm_g", "post_x_g", "pre_ffn_g", "post_ffn_g")
WEIGHTS = ('pre_mix_g', 'w_in', 'q_norm_g', 'k_norm_g', 'hg_lb', 'hg_out_norm_g', 'w_out', 'post_mix_g', 'pre_x_g', 'mem_norm_g', 'w_xq',
           'w_xkv', 'w_xo', 'post_x_g', 'pre_ffn_g', 'w_up', 'conv_w', 'conv_b', 'w_down', 'post_ffn_g')


def kernel(x, mem, pre_mix_g, w_in, q_norm_g, k_norm_g, hg_lb, hg_out_norm_g, w_out, post_mix_g, pre_x_g, mem_norm_g, w_xq, w_xkv, w_xo, post_x_g, pre_ffn_g, w_up, conv_w, conv_b, w_down, post_ffn_g, loss_target, m_pre_mix_g, m_w_in, m_q_norm_g, m_k_norm_g, m_hg_lb, m_hg_out_norm_g, m_w_out, m_post_mix_g, m_pre_x_g, m_mem_norm_g, m_w_xq, m_w_xkv, m_w_xo, m_post_x_g, m_pre_ffn_g, m_w_up, m_conv_w, m_conv_b, m_w_down, m_post_ffn_g, v_pre_mix_g, v_w_in, v_q_norm_g, v_k_norm_g, v_hg_lb, v_hg_out_norm_g, v_w_out, v_post_mix_g, v_pre_x_g, v_mem_norm_g, v_w_xq, v_w_xkv, v_w_xo, v_post_x_g, v_pre_ffn_g, v_w_up, v_conv_w, v_conv_b, v_w_down, v_post_ffn_g):
    args = dict(locals())
    w = {k: args[k] for k in WEIGHTS}
    m = {k: args["m_" + k] for k in WEIGHTS}
    v = {k: args["v_" + k] for k in WEIGHTS}
    chip = 2 * lax.axis_index("x") + lax.axis_index("y")
    core = lax.axis_index("c")

    turned = ("w_in",)
    shards = {k: (jnp.swapaxes(w[k], 1, 2) if k in turned else w[k])[0].astype(WIRE_DTYPE) for k in MATS}

    def whole(k, g):
        return g if k in ("w_xkv", "w_up") else g.reshape(-1, g.shape[-1])

    mid_names, ffn_names = ("w_out", "w_xq", "w_xkv", "w_xo"), ("w_up", "w_down")
    small = _small_start(_pack_small([w["conv_w"][0], w["hg_lb"]]), "gather_small_start")
    w_in_pieces = _gather_pieces_start(shards["w_in"], "gather_w_in_start", after=(small[4],))
    cast_first = tuple(shards[k] for k in mid_names + ffn_names)
    w_in_t = whole("w_in", _pass_pieces(_gather_pieces_wait(*w_in_pieces[:4], cast_first, "gather_w_in_wait"), "gather_w_in_pass"))
    mine, others = _small_wait(*small[:4], (w_in_t,), "gather_small_wait")
    small_in = lax.dynamic_update_slice_in_dim(others, mine[None], 2 * chip + core, axis=0)
    mid = _gather_start([shards[k] for k in mid_names], "gather_mid_start", after=(w_in_t, small_in))
    ffn = _gather_start([shards[k] for k in ffn_names], "gather_ffn_start", after=(mid[4],))

    def mid_weights(after):
        return [whole(k, g) for k, g in zip(mid_names, _gather_wait(*mid[:4], after, "gather_mid_wait"))]

    def ffn_weights(k, after):
        t = ffn_names.index(k)
        return whole(k, _gather_wait(*[part[t:t + 1] for part in ffn[:4]], after, "gather_wait_" + k)[0])

    cw_parts, lb_parts = [], []
    for s in range(4):
        cw_s, lb_s = _unpack_small(small_in[2 * s], [w["conv_w"][0].shape, w["hg_lb"].shape])
        cw_parts.append(cw_s)
        lb_parts.append(lb_s)
    conv_w_full = jnp.concatenate(cw_parts, axis=1)
    hg_lb_full = jnp.concatenate(lb_parts, axis=2)

    started = {}

    def on_grad(k, g):
        if g.ndim == 2:
            g = g.reshape(4, g.shape[0] // 4, g.shape[1])
        *started[k], token = _scatter_start(g, "grad_start_" + k)
        return token

    gains = {k: w[k] for k in GAINS}
    loss_part, grad_x, gs = _local_step(x[0], mem[0], loss_target[0], w_in_t, ffn[4], mid_weights, ffn_weights, on_grad, gains,
                                        conv_w_full, w["conv_b"], hg_lb_full)
    gs["loss"] = loss_part

    grads, delta, new_m, new_v = {}, {}, {}, {}

    def reduce_matrices(names, after, tag):
        sent, landed = _scatter_wait([started[k] for k in names], after, "grad_wait_" + tag)
        halves = [_sum_devices(g, land, chip, core, "grad_sum_" + k) for k, g, land in zip(names, sent, landed)]
        for k, r in zip(names, _join_halves(halves, "grad_join_" + tag)):
            grads[k] = r.reshape(1, -1, r.shape[-1])

    def adamw(names):
        for k in names:
            shape = w[k].shape
            keep = len(shape) == 3 and shape[0] == 1
            if k in turned:
                view, back = (lambda a: jnp.swapaxes(a, 1, 2)), (lambda a: jnp.swapaxes(a, 1, 2))
                g = grads[k]
            else:
                view = (lambda a: a.reshape(shape)) if keep else (lambda a: a.reshape(-1, shape[-1]))
                back = lambda a: a.reshape(shape)
                g = view(grads[k])
            d, mo, vo, go = _adamw(view(w[k]), g, view(m[k]), view(v[k]), "adamw_" + k)
            delta[k], new_m[k], new_v[k], grads[k] = back(d), back(mo), back(vo), back(go)

    small_names = GAINS + ("conv_b", "conv_w", "hg_lb")
    packed = _pack_small([gs[k] for k in small_names + ("loss",)])
    small = _small_start(packed, "reduce_small_start", after=(grad_x,))

    early = tuple(k for k in MATS if k != "w_in")
    reduce_matrices(early, (grad_x, small[4]), "early")
    adamw(early)

    mine, others = _small_wait(*small[:4], tuple(new_v[k] for k in early), "reduce_small_wait")
    reduced_small = _sum_small(mine, others, "reduce_small_sum")
    *summed, loss = _unpack_small(reduced_small, [gs[k].shape for k in small_names + ("loss",)])
    loss = loss[0, 0]
    for k, g in zip(small_names, summed):
        grads[k] = g
    ncw = w["conv_w"].shape[2]
    grads["conv_w"] = lax.dynamic_slice_in_dim(grads["conv_w"], chip * ncw, ncw, axis=1)[None]
    nlb = w["hg_lb"].shape[2]
    grads["hg_lb"] = lax.dynamic_slice_in_dim(grads["hg_lb"], chip * nlb, nlb, axis=2)
    replicated = GAINS + ("conv_b",)
    shapes = [w[k].shape for k in replicated]
    rows = sum(int(np.prod(s)) for s in shapes) // 128
    pack = lambda d: jnp.concatenate([d[k].reshape(-1) for k in replicated]).reshape(rows, 128)
    outs = _adamw(pack(w), reduced_small[:rows], pack(m), pack(v), "adamw_replicated")
    for into, packed_out in zip((delta, new_m, new_v, grads), outs):
        for k, a in zip(replicated, _unpack_small(packed_out, shapes)):
            into[k] = a
    adamw(("conv_w", "hg_lb"))

    reduce_matrices(("w_in",), tuple(new_v[k] for k in early + small_names), "late")
    adamw(("w_in",))
    return (loss, grad_x[None], *[grads[k] for k in WEIGHTS], *[delta[k] for k in WEIGHTS],
            *[new_m[k] for k in WEIGHTS], *[new_v[k] for k in WEIGHTS])
```

```python
import numpy as np
import jax
import jax.numpy as jnp
from jax import lax
from jax.experimental import pallas as pl
from jax.experimental.pallas import tpu as pltpu

F32 = jnp.float32
MXU_DTYPE = jnp.bfloat16
WIRE_DTYPE = jnp.bfloat16
VMEM_LIMIT_BYTES = 56 * 1024 * 1024
ROWS_PER_16BIT_TILE = 16
ELEMENTWISE_ROWS = 256
EPS = 1e-6
MESH = pl.DeviceIdType.MESH

GRID_W = 64
ATT_HEADS, ATT_KV_HEADS, ATT_HEAD_DIM = 8, 2, 64
ATT_GROUP = ATT_HEADS // ATT_KV_HEADS
ATT_Q_DIM, ATT_KV_DIM = 512, 128
ROPE_THETA = 10000.0
HG_HEADS, HG_HEAD_DIM, HG_DIM = 4, 128, 512
HG_CHUNK = 128
HG_LEVELS = 7
HG_PAIR = 2 * HG_HEAD_DIM
HG_KEPT = 7
X_HEADS, X_HEAD_DIM = 4, 256
D_FF = 2816
FF_COLS = 256
FF_BLOCKS = D_FF // FF_COLS
OFF_AK, OFF_AV, OFF_HQ, OFF_ZF, OFF_ZB, OFF_HI, OFF_HG = 512, 640, 768, 1280, 1792, 2304, 2816

ADAM_LR, ADAM_B1, ADAM_B2, ADAM_EPS, ADAM_WD, ADAM_STEP = 0.001, 0.9, 0.999, 1e-08, 0.01, 10

SDS = jax.ShapeDtypeStruct


def _cp(*sem):
    return pltpu.CompilerParams(dimension_semantics=sem, vmem_limit_bytes=VMEM_LIMIT_BYTES)


def _row_tile(rows, cap):
    if rows <= cap:
        return rows
    return max(t for t in range(ROWS_PER_16BIT_TILE, cap + 1, ROWS_PER_16BIT_TILE) if rows % t == 0)


def _dot(a, b, form="nn"):
    dims = {"nn": (((1,), (0,)), ((), ())), "nt": (((1,), (1,)), ((), ())), "tn": (((0,), (0,)), ((), ()))}[form]
    return lax.dot_general(a.astype(MXU_DTYPE), b.astype(MXU_DTYPE), dims, preferred_element_type=F32)


def _sigmoid(x):
    return 1.0 / (1.0 + jnp.exp(-x))


def _rstd(x):
    return lax.rsqrt(jnp.mean(x * x, axis=-1, keepdims=True) + EPS)


def _rms_bwd(x, g, dy):
    r = _rstd(x)
    xh = x * r
    dn = dy * g
    dx = r * (dn - xh * jnp.mean(dn * xh, axis=-1, keepdims=True))
    return dx, jnp.sum(dy * xh, axis=0, keepdims=True)


def _unread(after):
    after = tuple(a for a in after if a is not None)
    return after, [pl.BlockSpec(memory_space=pl.ANY)] * len(after)


def _mm(a, b, form, out_dtype, tm, tn, name, after=()):
    after, after_specs = _unread(after)
    if form == "nn":
        (m, k), n = a.shape, b.shape[1]
    elif form == "nt":
        (m, k), n = a.shape, b.shape[0]
    else:
        (k, m), n = a.shape, b.shape[1]
    tm, tn = min(tm, m), min(tn, n)
    assert m % tm == 0 and n % tn == 0, (name, m, n, tm, tn)

    def body(a_ref, b_ref, *rest):
        o_ref = rest[-1]
        o_ref[...] = _dot(a_ref[...], b_ref[...], form).astype(o_ref.dtype)

    a_spec = pl.BlockSpec((k, tm), lambda i, j: (0, i)) if form == "tn" else pl.BlockSpec((tm, k), lambda i, j: (i, 0))
    b_spec = pl.BlockSpec((tn, k), lambda i, j: (j, 0)) if form == "nt" else pl.BlockSpec((k, tn), lambda i, j: (0, j))
    return pl.pallas_call(
        body, name=name, grid=(m // tm, n // tn), in_specs=[a_spec, b_spec] + after_specs,
        out_specs=pl.BlockSpec((tm, tn), lambda i, j: (i, j)), out_shape=SDS((m, n), out_dtype),
        compiler_params=_cp("parallel", "parallel"))(a, b, *after)


def _mm_nt_parts(a_parts, b, out_dtype, tm, tn, name, after=()):
    after, after_specs = _unread(after)
    parts, n, p = b.shape
    m = a_parts[0][0].shape[0]
    tm, tn = min(tm, m), min(tn, n)
    assert m % tm == 0 and n % tn == 0 and len(a_parts) == parts, (name, m, b.shape)

    def body(*refs):
        o_ref = refs[-1]
        acc = _dot(refs[0][...], refs[parts][0], "nt")
        for s in range(1, parts):
            acc = acc + _dot(refs[s][...], refs[parts + s][0], "nt")
        o_ref[...] = acc.astype(o_ref.dtype)

    a_specs = [pl.BlockSpec((tm, p), lambda i, j, cb=cb: (i, cb)) for _, cb in a_parts]
    b_specs = [pl.BlockSpec((1, tn, p), lambda i, j, s=s: (s, j, 0)) for s in range(parts)]
    return pl.pallas_call(
        body, name=name, grid=(m // tm, n // tn), in_specs=a_specs + b_specs + after_specs,
        out_specs=pl.BlockSpec((tm, tn), lambda i, j: (i, j)), out_shape=SDS((m, n), out_dtype),
        compiler_params=_cp("parallel", "parallel"))(*[arr for arr, _ in a_parts], *([b] * parts), *after)


def _norm_bwd_mm(y, g, d, w, out_dtype, tm, tn, name):
    n, dm = y.shape
    nn = w.shape[0]
    tm, tn = min(tm, n), min(tn, nn)
    assert n % tm == 0 and nn % tn == 0 and w.shape[1] == dm, (name, y.shape, w.shape)

    def body(y_ref, g_ref, d_ref, w_ref, dx_ref, dy_ref, dg_ref, dys):
        i, j = pl.program_id(0), pl.program_id(1)

        @pl.when(jnp.logical_and(i == 0, j == 0))
        def _():
            dg_ref[...] = jnp.zeros_like(dg_ref)

        @pl.when(j == 0)
        def _():
            dy, dg = _rms_bwd(y_ref[...], g_ref[...], d_ref[...])
            dy = dy.astype(MXU_DTYPE)
            dys[...] = dy
            dy_ref[...] = dy
            dg_ref[...] += dg

        dx_ref[...] = _dot(dys[...], w_ref[...], "nt").astype(dx_ref.dtype)

    row = pl.BlockSpec((tm, dm), lambda i, j: (i, 0))
    vec = pl.BlockSpec((1, dm), lambda i, j: (0, 0))
    return pl.pallas_call(
        body, name=name, grid=(n // tm, nn // tn), in_specs=[row, vec, row, pl.BlockSpec((tn, dm), lambda i, j: (j, 0))],
        out_specs=[pl.BlockSpec((tm, tn), lambda i, j: (i, j)), row, vec],
        out_shape=[SDS((n, nn), out_dtype), SDS((n, dm), MXU_DTYPE), SDS((1, dm), F32)],
        scratch_shapes=[pltpu.VMEM((tm, dm), MXU_DTYPE)],
        compiler_params=_cp("arbitrary", "arbitrary"))(y, g, d, w)


def _mm_resid_norm(a, b, x, g, tm, name, target=None):
    n, k = a.shape
    d = b.shape[1]
    tm = min(tm, n)
    assert n % tm == 0 and x.shape == (n, d), (name, a.shape, b.shape)
    with_loss = target is not None

    def body(a_ref, b_ref, x_ref, g_ref, *rest):
        y = _dot(a_ref[...], b_ref[...])
        out = x_ref[...] + y * _rstd(y) * g_ref[...]
        if not with_loss:
            y_ref, o_ref = rest
            y_ref[...] = y
            o_ref[...] = out
            return
        t_ref, y_ref, d_ref, l_ref = rest
        y_ref[...] = y
        diff = out - t_ref[...]
        d_ref[...] = diff * (1.0 / d)

        @pl.when(pl.program_id(0) == 0)
        def _():
            l_ref[...] = jnp.zeros_like(l_ref)

        l_ref[...] += 0.5 * jnp.sum(jnp.mean(diff * diff, axis=-1, keepdims=True), axis=0, keepdims=True)

    row = pl.BlockSpec((tm, d), lambda i: (i, 0))
    ins = [pl.BlockSpec((tm, k), lambda i: (i, 0)), pl.BlockSpec((k, d), lambda i: (0, 0)), row, pl.BlockSpec((1, d), lambda i: (0, 0))]
    out = SDS((n, d), F32)
    if with_loss:
        return pl.pallas_call(body, name=name, grid=(n // tm,), in_specs=ins + [row], out_specs=[row, row, pl.BlockSpec((1, 1), lambda i: (0, 0))],
                              out_shape=[out, out, SDS((1, 1), F32)], compiler_params=_cp("arbitrary"))(a, b, x, g, target)
    return pl.pallas_call(body, name=name, grid=(n // tm,), in_specs=ins, out_specs=[row, row], out_shape=[out, out],
                          compiler_params=_cp("parallel"))(a, b, x, g)


def _dx_norm_bwd(a_parts, b, x, g, res, tm, name, after=(), b_turned=False):
    after, after_specs = _unread(after)
    parts, d, p = (b.shape[0], b.shape[2], b.shape[1]) if b_turned else b.shape
    form = "nn" if b_turned else "nt"
    n = x.shape[0]
    tm = min(tm, n)
    assert n % tm == 0 and len(a_parts) == parts and x.shape[1] == d, (name, x.shape, b.shape)

    def body(*refs):
        x_ref, g_ref, res_ref = refs[2 * parts:2 * parts + 3]
        dx_ref, dg_ref = refs[-2:]
        dh = _dot(refs[0][...], refs[parts][0], form)
        for s in range(1, parts):
            dh = dh + _dot(refs[s][...], refs[parts + s][0], form)
        dx, dg = _rms_bwd(x_ref[...], g_ref[...], dh)
        dx_ref[...] = dx + res_ref[...]

        @pl.when(pl.program_id(0) == 0)
        def _():
            dg_ref[...] = jnp.zeros_like(dg_ref)

        dg_ref[...] += dg

    a_specs = [pl.BlockSpec((tm, p), lambda i, cb=cb: (i, cb)) for _, cb in a_parts]
    b_specs = [pl.BlockSpec((1,) + b.shape[1:], lambda i, s=s: (s, 0, 0)) for s in range(parts)]
    row = pl.BlockSpec((tm, d), lambda i: (i, 0))
    vec = pl.BlockSpec((1, d), lambda i: (0, 0))
    return pl.pallas_call(
        body, name=name, grid=(n // tm,), in_specs=a_specs + b_specs + [row, vec, row] + after_specs,
        out_specs=[row, vec], out_shape=[SDS((n, d), F32), SDS((1, d), F32)],
        compiler_params=_cp("arbitrary"))(*[arr for arr, _ in a_parts], *([b] * parts), x, g, res, *after)


def _dw_by_owner(a, b, tn, first, into, tm, name):
    k, m = a.shape
    cnt = b.shape[1] // tn
    tm = min(tm, m)
    assert m % tm == 0 and b.shape[1] == cnt * tn and first + cnt <= 4, (name, a.shape, b.shape)

    def body(a_ref, b_ref, *rest):
        rest[-1][0] = _dot(a_ref[...], b_ref[...], "tn").astype(rest[-1].dtype)

    extra = [] if into is None else [into]
    return pl.pallas_call(
        body, name=name, grid=(m // tm, cnt),
        in_specs=[pl.BlockSpec((k, tm), lambda i, j: (0, i)), pl.BlockSpec((k, tn), lambda i, j: (0, j))] + [pl.BlockSpec(memory_space=pl.ANY)] * len(extra),
        out_specs=pl.BlockSpec((1, tm, tn), lambda i, j: (first + j, i, 0)), out_shape=SDS((4, m, tn), WIRE_DTYPE),
        input_output_aliases={2: 0} if extra else {},
        compiler_params=_cp("parallel", "parallel"))(a, b, *extra)


def _norm_mm(x, g, w, out_dtype, tm, tn, name, after=(), w_turned=False):
    after, after_specs = _unread(after)
    m, d = x.shape
    sharded = w.ndim == 3
    n = w.shape[0] if w_turned else w.shape[-1] * (w.shape[0] if sharded else 1)
    tm, tn = min(tm, m), (w.shape[-1] if sharded else min(tn, n))
    assert m % tm == 0 and n % tn == 0 and not (sharded and w_turned), (name, m, n, tm, tn)

    def body(x_ref, g_ref, w_ref, *rest):
        o_ref, h_ref, hs = rest[-3:]

        @pl.when(pl.program_id(1) == 0)
        def _():
            xv = x_ref[...]
            h = (xv * _rstd(xv) * g_ref[...]).astype(MXU_DTYPE)
            hs[...] = h
            h_ref[...] = h

        o_ref[...] = _dot(hs[...], w_ref[0] if sharded else w_ref[...], "nt" if w_turned else "nn").astype(o_ref.dtype)

    if w_turned:
        w_spec = pl.BlockSpec((tn, d), lambda i, j: (j, 0))
    else:
        w_spec = pl.BlockSpec((1, d, tn), lambda i, j: (j, 0, 0)) if sharded else pl.BlockSpec((d, tn), lambda i, j: (0, j))
    return pl.pallas_call(
        body, name=name, grid=(m // tm, n // tn),
        in_specs=[pl.BlockSpec((tm, d), lambda i, j: (i, 0)), pl.BlockSpec((1, d), lambda i, j: (0, 0)), w_spec] + after_specs,
        out_specs=[pl.BlockSpec((tm, tn), lambda i, j: (i, j)), pl.BlockSpec((tm, d), lambda i, j: (i, 0))],
        out_shape=[SDS((m, n), out_dtype), SDS((m, d), MXU_DTYPE)],
        scratch_shapes=[pltpu.VMEM((tm, d), MXU_DTYPE)],
        compiler_params=_cp("parallel", "arbitrary"))(x, g, w, *after)


ROW_TILE = 512
TOKEN_TILE = 1024


def _norm_bwd(x, g, dy, res, out_dtype, name):
    n, d = x.shape
    tr = min(ROW_TILE, n)
    has_res = res is not None

    def body(*refs):
        x_ref, g_ref, dy_ref = refs[:3]
        dx_ref, dg_ref = refs[-2:]
        dx, dg = _rms_bwd(x_ref[...], g_ref[...], dy_ref[...].astype(F32))
        if has_res:
            dx = dx + refs[3][...]
        dx_ref[...] = dx.astype(dx_ref.dtype)

        @pl.when(pl.program_id(0) == 0)
        def _():
            dg_ref[...] = jnp.zeros_like(dg_ref)

        dg_ref[...] += dg

    row = pl.BlockSpec((tr, d), lambda i: (i, 0))
    vec = pl.BlockSpec((1, d), lambda i: (0, 0))
    ins = [x, g, dy] + ([res] if has_res else [])
    return pl.pallas_call(
        body, name=name, grid=(n // tr,), in_specs=[row, vec, row] + ([row] if has_res else []),
        out_specs=[row, vec], out_shape=[SDS((n, d), out_dtype), SDS((1, d), F32)],
        compiler_params=_cp("arbitrary"))(*ins)


def _rope_tables(n):
    pairs = ATT_HEAD_DIM // 4
    t = np.arange(n)
    inv = np.power(ROPE_THETA, -np.arange(pairs, dtype=np.float32) / pairs).astype(np.float32)
    ang = np.concatenate([(t // GRID_W)[:, None].astype(np.float32) * inv, (t % GRID_W)[:, None].astype(np.float32) * inv], axis=-1)
    cos = np.repeat(np.cos(ang), 2, axis=-1)
    sin = np.repeat(np.sin(ang), 2, axis=-1) * np.tile(np.array([-1.0, 1.0], np.float32), ATT_HEAD_DIM // 2)
    return jnp.asarray(np.tile(cos, 2), F32), jnp.asarray(np.tile(sin, 2), F32)


def _swap_pairs(x):
    lane = lax.broadcasted_iota(jnp.int32, x.shape, 1)
    return jnp.where((lane & 1) == 0, pltpu.roll(x, 127, axis=1), pltpu.roll(x, 1, axis=1))


def _head_mean(v):
    lane = lax.broadcasted_iota(jnp.int32, v.shape, 1)
    lo = jnp.where(lane < ATT_HEAD_DIM, v, 0.0)
    s0 = jnp.sum(lo, axis=-1, keepdims=True)
    s1 = jnp.sum(v - lo, axis=-1, keepdims=True)
    return jnp.where(lane < ATT_HEAD_DIM, s0, s1) * (1.0 / ATT_HEAD_DIM)


def _qk_prep(p, gq, gk, cos, sin, name):
    n = p.shape[0]
    tr = min(ROW_TILE, n)

    def one(xv, g, c, s):
        xn = xv * lax.rsqrt(_head_mean(xv * xv) + EPS) * g
        return xn * c + _swap_pairs(xn) * s

    def body(q_ref, k_ref, gq_ref, gk_ref, c_ref, s_ref, qo_ref, ko_ref):
        c, s = c_ref[...], s_ref[...]
        for j in range(ATT_Q_DIM // 128):
            qo_ref[:, j * 128:(j + 1) * 128] = one(q_ref[:, j * 128:(j + 1) * 128], gq_ref[...], c, s).astype(qo_ref.dtype)
        ko_ref[...] = one(k_ref[...], gk_ref[...], c, s).astype(ko_ref.dtype)

    vec = pl.BlockSpec((1, 128), lambda i: (0, 0))
    tab = pl.BlockSpec((tr, 128), lambda i: (i, 0))
    return pl.pallas_call(
        body, name=name, grid=(n // tr,),
        in_specs=[pl.BlockSpec((tr, ATT_Q_DIM), lambda i: (i, 0)), pl.BlockSpec((tr, 128), lambda i: (i, OFF_AK // 128)), vec, vec, tab, tab],
        out_specs=[pl.BlockSpec((tr, ATT_Q_DIM), lambda i: (i, 0)), tab],
        out_shape=[SDS((n, ATT_Q_DIM), MXU_DTYPE), SDS((n, ATT_KV_DIM), MXU_DTYPE)],
        compiler_params=_cp("parallel"))(p, p, gq, gk, cos, sin)


def _qk_prep_bwd(p, gq, gk, cos, sin, dq, dk, name):
    n = p.shape[0]
    tr = min(ROW_TILE, n)

    def one(xv, g, c, s, dout):
        dxn = dout * c + _swap_pairs(dout * s)
        r = lax.rsqrt(_head_mean(xv * xv) + EPS)
        xh = xv * r
        dn = dxn * g
        dx = r * (dn - xh * _head_mean(dn * xh))
        return dx, jnp.sum(dxn * xh, axis=0, keepdims=True)

    def body(q_ref, k_ref, gq_ref, gk_ref, c_ref, s_ref, dq_ref, dk_ref, dqo_ref, dko_ref, dgq_ref, dgk_ref):
        @pl.when(pl.program_id(0) == 0)
        def _():
            dgq_ref[...] = jnp.zeros_like(dgq_ref)
            dgk_ref[...] = jnp.zeros_like(dgk_ref)

        c, s = c_ref[...], s_ref[...]
        for j in range(ATT_Q_DIM // 128):
            sl = slice(j * 128, (j + 1) * 128)
            dx, dg = one(q_ref[:, sl], gq_ref[...], c, s, dq_ref[:, sl])
            dqo_ref[:, sl] = dx.astype(dqo_ref.dtype)
            dgq_ref[:, sl] += dg
        dx, dg = one(k_ref[...], gk_ref[...], c, s, dk_ref[...])
        dko_ref[...] = dx.astype(dko_ref.dtype)
        dgk_ref[...] += dg

    vec = pl.BlockSpec((1, 128), lambda i: (0, 0))
    tab = pl.BlockSpec((tr, 128), lambda i: (i, 0))
    qrow = pl.BlockSpec((tr, ATT_Q_DIM), lambda i: (i, 0))
    return pl.pallas_call(
        body, name=name, grid=(n // tr,),
        in_specs=[qrow, pl.BlockSpec((tr, 128), lambda i: (i, OFF_AK // 128)), vec, vec, tab, tab, qrow, tab],
        out_specs=[qrow, tab, pl.BlockSpec((1, ATT_Q_DIM), lambda i: (0, 0)), vec],
        out_shape=[SDS((n, ATT_Q_DIM), MXU_DTYPE), SDS((n, ATT_KV_DIM), MXU_DTYPE), SDS((1, ATT_Q_DIM), F32), SDS((1, 128), F32)],
        compiler_params=_cp("arbitrary"))(p, p, gq, gk, cos, sin, dq, dk)


ATT_FWD_STEP = (256, 4)
ATT_BWD_STEP = (512, 2)


def _attn_fwd(q, k, v, name):
    n = q.shape[0]
    tq, step_heads = min(ATT_FWD_STEP[0], n), ATT_FWD_STEP[1]
    scale = ATT_HEAD_DIM ** -0.5
    gw = step_heads * ATT_HEAD_DIM
    parts = ATT_GROUP // step_heads

    def body(q_ref, k_ref, v_ref, o_ref):
        kk, vv = k_ref[0], v_ref[0]
        v_ones = jnp.concatenate([vv, jnp.ones_like(vv)], axis=1)
        outs = []
        for g in range(step_heads):
            s = _dot(q_ref[:, g * ATT_HEAD_DIM:(g + 1) * ATT_HEAD_DIM] * scale, kk, "nt")
            e = jnp.exp(s - jnp.max(s, axis=-1, keepdims=True))
            ov = _dot(e, v_ones)
            outs.append(ov[:, :ATT_HEAD_DIM] / ov[:, ATT_HEAD_DIM:])
        o_ref[...] = jnp.concatenate(outs, axis=-1).astype(o_ref.dtype)

    kv = pl.BlockSpec((1, n, ATT_HEAD_DIM), lambda h, i, pr: (h, 0, 0))
    qb = pl.BlockSpec((tq, gw), lambda h, i, pr: (i, h * parts + pr))
    return pl.pallas_call(
        body, name=name, grid=(ATT_KV_HEADS, n // tq, parts), in_specs=[qb, kv, kv],
        out_specs=qb, out_shape=SDS((n, ATT_Q_DIM), MXU_DTYPE),
        compiler_params=_cp("parallel", "parallel", "parallel"))(q, k, v)


def _attn_bwd(q, k, v, o, do, name):
    n = q.shape[0]
    tq, step_heads = min(ATT_BWD_STEP[0], n), ATT_BWD_STEP[1]
    scale = ATT_HEAD_DIM ** -0.5
    gw = step_heads * ATT_HEAD_DIM
    parts = ATT_GROUP // step_heads

    def body(q_ref, k_ref, v_ref, o_ref, do_ref, dq_ref, dk_ref, dv_ref):
        @pl.when(jnp.logical_and(pl.program_id(1) == 0, pl.program_id(2) == 0))
        def _():
            dk_ref[...] = jnp.zeros_like(dk_ref)
            dv_ref[...] = jnp.zeros_like(dv_ref)

        kk, vv = k_ref[0], v_ref[0]
        dqs = []
        dk_acc = jnp.zeros((ATT_HEAD_DIM, n), F32)
        dv_acc = jnp.zeros((ATT_HEAD_DIM, n), F32)
        for g in range(step_heads):
            sl = slice(g * ATT_HEAD_DIM, (g + 1) * ATT_HEAD_DIM)
            qg, dog = q_ref[:, sl] * scale, do_ref[:, sl].astype(F32)
            s = _dot(qg, kk, "nt")
            e = jnp.exp(s - jnp.max(s, axis=-1, keepdims=True))
            inv = 1.0 / jnp.sum(e, axis=-1, keepdims=True)
            delta = jnp.sum(dog * o_ref[:, sl].astype(F32), axis=-1, keepdims=True)
            dse = e * (_dot(dog, vv, "nt") - delta)
            dqs.append(_dot(dse, kk) * (inv * scale))
            dk_acc += _dot(qg.astype(F32) * inv, dse, "tn")
            dv_acc += _dot(dog * inv, e, "tn")
        dq_ref[...] = jnp.concatenate(dqs, axis=-1)
        dk_ref[0] += dk_acc
        dv_ref[0] += dv_acc

    kv = pl.BlockSpec((1, n, ATT_HEAD_DIM), lambda h, i, pr: (h, 0, 0))
    kvt = pl.BlockSpec((1, ATT_HEAD_DIM, n), lambda h, i, pr: (h, 0, 0))
    qb = pl.BlockSpec((tq, gw), lambda h, i, pr: (i, h * parts + pr))
    return pl.pallas_call(
        body, name=name, grid=(ATT_KV_HEADS, n // tq, parts), in_specs=[qb, kv, kv, qb, qb], out_specs=[qb, kvt, kvt],
        out_shape=[SDS((n, ATT_Q_DIM), F32), SDS((ATT_KV_HEADS, ATT_HEAD_DIM, n), F32), SDS((ATT_KV_HEADS, ATT_HEAD_DIM, n), F32)],
        compiler_params=_cp("parallel", "arbitrary", "arbitrary"))(q, k, v, o, do)


def _both_directions(mats, axis):
    fwd = np.concatenate(mats, axis=axis).astype(np.float32)
    bwd = np.concatenate([m[::-1, ::-1] for m in mats], axis=axis).astype(np.float32)
    return jnp.asarray(np.stack([fwd, bwd]), MXU_DTYPE)


def _hg_segments():
    c = HG_CHUNK
    t = np.arange(c)[:, None]
    r = np.arange(c)[None, :]
    mats = [(r <= t)]
    for lev in range(HG_LEVELS):
        h = c >> (lev + 1)
        mid = (t // (2 * h)) * (2 * h) + h - 1
        hi = (t // h) % 2 == 1
        mats.append(np.where(hi, (r > mid) & (r <= t), (r > t) & (r <= mid)))
    mats.append(r > t)
    return _both_directions(mats, 0)


def _hg_pair_sums():
    c = HG_CHUNK
    r = np.arange(c)[:, None]
    t = np.arange(c)[None, :]
    gp, gn = [t >= r], [t < r]
    for lev in range(HG_LEVELS):
        sh = HG_LEVELS - 1 - lev
        same = (r >> sh) == (t >> sh)
        gp.append(same & (t >= r))
        gn.append(same & (t < r))
    return _both_directions(gp, 1), _both_directions(gn, 1)


def _split_dot(mat, x):
    hi = x.astype(MXU_DTYPE)
    lo = (x - hi.astype(F32)).astype(MXU_DTYPE)
    return _dot(mat, hi) + _dot(mat, lo)


def _hg_gates(hq, z, a0, a1):
    q = hq * _sigmoid(hq)
    sg = _sigmoid(z)
    lb = _sigmoid(a0 - a1)
    f = lb + (1.0 - lb) * sg
    k = (1.0 - lb) * (1.0 - sg)
    return q, f, k, sg, lb


def _hg_level_masks():
    c = HG_CHUNK
    t = np.arange(c)
    later, same = [], []
    for lev in range(HG_LEVELS):
        sh = HG_LEVELS - 1 - lev
        later.append(np.broadcast_to((((t >> sh) & 1) == 1)[:, None], (c, HG_HEAD_DIM)))
        same.append((t[:, None] >> (sh + 1)) == (t[None, :] >> (sh + 1)))
    same.append(t[:, None] == t[None, :])
    later = np.stack(later).astype(np.float32)
    return jnp.asarray(np.stack([later, 1.0 - later]), F32), jnp.asarray(np.stack(same).astype(np.float32), F32)


def _hg_level(q, k, ex, later_ref, lev):
    e = ex[lev + 1]
    e_q = e * later_ref[0, lev]
    e_k = e - e_q
    return q * e_q, k * e_k, e_q, e_k


def _hg_intra(q, k, ex, later_ref, same_ref):
    a = same_ref[HG_LEVELS] * jnp.sum(q * k, axis=-1, keepdims=True)
    for lev in range(HG_LEVELS):
        qs, ks, _, _ = _hg_level(q, k, ex, later_ref, lev)
        a = a + same_ref[lev] * _dot(qs, ks, "nt")
    return a


def _hg_specs(n, with_time):
    c = HG_CHUNK
    nc = n // c

    def chunk(d, i):
        first = d if with_time else 1 - d
        return i + first * (nc - 1 - 2 * i)

    def pcols(off, dir_stride=0):
        return [pl.BlockSpec((c, HG_PAIR), lambda d, i, j=j: (chunk(d, i), off // HG_PAIR + dir_stride // HG_PAIR * d + j)) for j in range(2)]

    specs = dict(
        hq=pcols(OFF_HQ), v=pcols(OFF_HI), z=pcols(OFF_ZF, OFF_ZB - OFF_ZF),
        shared=pl.BlockSpec((c, HG_DIM), lambda d, i: (chunk(d, i), 0)),
        per_dir=pl.BlockSpec((1, c, HG_DIM), lambda d, i: (d, chunk(d, i), 0)),
        vec=pl.BlockSpec((1, 1, HG_DIM), lambda d, i: (d, 0, 0)),
        seg=pl.BlockSpec((1, (HG_LEVELS + 2) * c, c), lambda d, i: (d, 0, 0)),
        sums=pl.BlockSpec((1, c, (HG_LEVELS + 1) * c), lambda d, i: (d, 0, 0)),
        later=pl.BlockSpec((1, HG_LEVELS, c, HG_HEAD_DIM), lambda d, i: (d, 0, 0, 0)),
        same=pl.BlockSpec((HG_LEVELS + 1, c, c), lambda d, i: (0, 0, 0)),
        state=pl.BlockSpec((1, HG_HEADS, 1, HG_HEAD_DIM, HG_HEAD_DIM), lambda d, i: (d, 0, chunk(d, i), 0, 0)),
        weights=pl.BlockSpec((1, HG_HEADS, 1, c, c), lambda d, i: (d, 0, chunk(d, i), 0, 0)),
        levels=pl.BlockSpec((1, HG_HEADS, 1, HG_LEVELS, c, HG_HEAD_DIM), lambda d, i: (d, 0, chunk(d, i), 0, 0, 0)),
        kept=pl.BlockSpec((1, HG_KEPT, c, HG_DIM), lambda d, i: (d, 0, chunk(d, i), 0)))
    return nc, specs


def _hg_head(refs, hh):
    off = (hh % 2) * HG_HEAD_DIM
    return refs[hh // 2][:, off:off + HG_HEAD_DIM]


def _hg_lanes(hh):
    return slice(hh * HG_HEAD_DIM, (hh + 1) * HG_HEAD_DIM)


def _hg_exps(seg_ref, f):
    c = HG_CHUNK
    args = _split_dot(seg_ref[0], jnp.log(f))
    return [jnp.exp(args[j * c:(j + 1) * c]) for j in range(HG_LEVELS + 2)]


def _hg_last_row(a, mirrored):
    return jnp.where(mirrored, a[0:1, :], a[HG_CHUNK - 1:HG_CHUNK, :])


def _hgrn_fwd(p, a0, a1, seg, masks, name):
    n = p.shape[0]
    nc, sp = _hg_specs(n, True)

    def body(hq0, hq1, z0, z1, v0, v1, a0_ref, a1_ref, seg_ref, later_ref, same_ref, o_ref, s0_ref, a_ref, e_ref, g_ref, st):
        @pl.when(pl.program_id(1) == 0)
        def _():
            st[...] = jnp.zeros_like(st)

        mirrored = pl.program_id(0) == 1
        for hh in range(HG_HEADS):
            ln = _hg_lanes(hh)
            hqv = _hg_head((hq0, hq1), hh)
            q, f, k, sg, _ = _hg_gates(hqv, _hg_head((z0, z1), hh), a0_ref[0, :, ln], a1_ref[0, :, ln])
            vv = _hg_head((v0, v1), hh)
            ex = _hg_exps(seg_ref, f)
            for lev in range(HG_LEVELS):
                e_ref[0, hh, 0, lev] = ex[lev + 1].astype(e_ref.dtype)
            sq = _sigmoid(hqv)
            for j, kept in enumerate((q, k, f, sg, sq * (1.0 + hqv * (1.0 - sq)), ex[0], ex[HG_LEVELS + 1])):
                g_ref[0, j, :, ln] = kept
            a = _hg_intra(q, k, ex, later_ref, same_ref).astype(MXU_DTYPE)
            a_ref[0, hh, 0] = a
            s_t = st[hh]
            s0_ref[0, hh, 0] = s_t
            o_ref[0, :, ln] = _dot(a, vv) + _dot(q * ex[0], s_t, "nt")
            st[hh] = s_t * _hg_last_row(ex[0], mirrored) + _dot(vv, k * ex[HG_LEVELS + 1], "tn")

    return pl.pallas_call(
        body, name=name, grid=(2, nc), in_specs=sp["hq"] + sp["z"] + sp["v"] + [sp["vec"], sp["vec"], sp["seg"], sp["later"], sp["same"]],
        out_specs=[sp["per_dir"], sp["state"], sp["weights"], sp["levels"], sp["kept"]],
        out_shape=[SDS((2, n, HG_DIM), F32), SDS((2, HG_HEADS, nc, HG_HEAD_DIM, HG_HEAD_DIM), F32),
                   SDS((2, HG_HEADS, nc, HG_CHUNK, HG_CHUNK), MXU_DTYPE),
                   SDS((2, HG_HEADS, nc, HG_LEVELS, HG_CHUNK, HG_HEAD_DIM), MXU_DTYPE), SDS((2, HG_KEPT, n, HG_DIM), F32)],
        scratch_shapes=[pltpu.VMEM((HG_HEADS, HG_HEAD_DIM, HG_HEAD_DIM), F32)],
        compiler_params=_cp("parallel", "arbitrary"))(p, p, p, p, p, p, a0, a1, seg, *masks)


def _hgrn_bwd(p, a0, a1, masks, gp, gn, do, s0, a, e, kept, name):
    n = p.shape[0]
    nc, sp = _hg_specs(n, False)


    def body(v0, v1, a0_ref, a1_ref, later_ref, same_ref, gp_ref, gn_ref, do_ref, s0_ref, a_ref, e_ref, g_ref,
             dhq_ref, dz_ref, dv_ref, dlb_ref, rt):
        @pl.when(pl.program_id(1) == 0)
        def _():
            rt[...] = jnp.zeros_like(rt)
            dlb_ref[...] = jnp.zeros_like(dlb_ref)

        mirrored = pl.program_id(0) == 1
        for hh in range(HG_HEADS):
            ln = _hg_lanes(hh)
            q, k, f, sg, dsilu, e_first, e_last = (g_ref[0, j, :, ln] for j in range(HG_KEPT))
            lb = _sigmoid(a0_ref[0, :, ln] - a1_ref[0, :, ln])
            vv, dov = _hg_head((v0, v1), hh), do_ref[:, ln]
            ex = [e_first] + [e_ref[0, hh, 0, lev].astype(F32) for lev in range(HG_LEVELS)] + [e_last]
            a = a_ref[0, hh, 0]
            da = _dot(dov, vv, "nt")
            diag = jnp.sum(dov * vv, axis=-1, keepdims=True)
            s_t = s0_ref[0, hh, 0]
            r_t = rt[hh]
            k_end = k * ex[HG_LEVELS + 1]
            dv_ref[0, :, ln] = _dot(a, dov, "tn") + _dot(k_end, r_t, "nt")
            dq_inter = ex[0] * _dot(dov, s_t)
            dk_inter = ex[HG_LEVELS + 1] * _dot(vv, r_t)
            dq = diag * k + dq_inter
            dk = diag * q + dk_inter
            q_terms, k_terms = [q * dq_inter], [k * dk_inter]
            for lev in range(HG_LEVELS):
                qs, ks, e_q, e_k = _hg_level(q, k, ex, later_ref, lev)
                pairs = da * same_ref[lev]
                q_part = e_q * _dot(pairs, ks)
                k_part = e_k * _dot(pairs, qs, "tn")
                dq, dk = dq + q_part, dk + k_part
                q_terms.append(q * q_part)
                k_terms.append(k * k_part)
            decay = _hg_last_row(ex[0], mirrored)
            rt[hh] = r_t * decay + _dot(dov, q * ex[0], "tn")
            later = decay * jnp.sum(s_t * r_t, axis=0, keepdims=True)
            dlf = _dot(gp_ref[0], jnp.concatenate(q_terms, axis=0)) + _dot(gn_ref[0], jnp.concatenate(k_terms, axis=0)) + later
            df = dlf / f - dk
            dz_ref[0, :, ln] = df * (1.0 - lb) * sg * (1.0 - sg)
            dlb_ref[0, :, ln] += jnp.sum(df * (1.0 - sg), axis=0, keepdims=True)
            dhq_ref[0, :, ln] = dq * dsilu

    out = SDS((2, n, HG_DIM), F32)
    return pl.pallas_call(
        body, name=name, grid=(2, nc),
        in_specs=sp["v"] + [sp["vec"], sp["vec"], sp["later"], sp["same"], sp["sums"], sp["sums"],
                            sp["shared"], sp["state"], sp["weights"], sp["levels"], sp["kept"]],
        out_specs=[sp["per_dir"], sp["per_dir"], sp["per_dir"], sp["vec"]], out_shape=[out, out, out, SDS((2, 1, HG_DIM), F32)],
        scratch_shapes=[pltpu.VMEM((HG_HEADS, HG_HEAD_DIM, HG_HEAD_DIM), F32)],
        compiler_params=_cp("parallel", "arbitrary"))(p, p, a0, a1, *masks, gp, gn, do, s0, a, e, kept)


def _hg_post(o2, p, g, name):
    n = p.shape[0]
    tr = min(ROW_TILE, n)
    w = 2 * HG_HEAD_DIM

    def body(of_ref, ob_ref, hg_ref, g_ref, o_ref):
        for j in range(2):
            sl = slice(j * HG_HEAD_DIM, (j + 1) * HG_HEAD_DIM)
            o = of_ref[0, :, sl] + ob_ref[0, :, sl]
            hg = hg_ref[:, sl]
            o_ref[:, sl] = (o * _rstd(o) * g_ref[...] * (hg * _sigmoid(hg))).astype(o_ref.dtype)

    blk = pl.BlockSpec((tr, w), lambda i, j: (i, j))
    dirs = [pl.BlockSpec((1, tr, w), lambda i, j, d=d: (d, i, j)) for d in range(2)]
    return pl.pallas_call(
        body, name=name, grid=(n // tr, HG_DIM // w),
        in_specs=dirs + [pl.BlockSpec((tr, w), lambda i, j: (i, OFF_HG // w + j)), pl.BlockSpec((1, HG_HEAD_DIM), lambda i, j: (0, 0))],
        out_specs=blk, out_shape=SDS((n, HG_DIM), MXU_DTYPE), compiler_params=_cp("parallel", "parallel"))(o2, o2, p, g)


def _hg_post_bwd(o2, p, g, dcat, name, after=()):
    n = p.shape[0]
    tr = min(ROW_TILE, n)
    w = 2 * HG_HEAD_DIM
    after, after_specs = _unread(after)

    def body(of_ref, ob_ref, hg_ref, g_ref, d_ref, *rest):
        do_ref, dhg_ref, dg_ref = rest[len(after):]

        @pl.when(pl.program_id(1) == 0)
        def _():
            dg_ref[...] = jnp.zeros_like(dg_ref)

        for j in range(2):
            sl = slice(j * HG_HEAD_DIM, (j + 1) * HG_HEAD_DIM)
            o = of_ref[0, :, sl] + ob_ref[0, :, sl]
            hg = hg_ref[:, sl]
            d = d_ref[:, sl].astype(F32)
            sg = _sigmoid(hg)
            on = o * _rstd(o) * g_ref[...]
            dhg_ref[:, sl] = (d * on * sg * (1.0 + hg * (1.0 - sg))).astype(dhg_ref.dtype)
            dx, dg = _rms_bwd(o, g_ref[...], d * hg * sg)
            do_ref[:, sl] = dx
            dg_ref[0, :, sl] += dg

    blk = pl.BlockSpec((tr, w), lambda j, i: (i, j))
    dirs = [pl.BlockSpec((1, tr, w), lambda j, i, d=d: (d, i, j)) for d in range(2)]
    return pl.pallas_call(
        body, name=name, grid=(HG_DIM // w, n // tr),
        in_specs=dirs + [pl.BlockSpec((tr, w), lambda j, i: (i, OFF_HG // w + j)), pl.BlockSpec((1, HG_HEAD_DIM), lambda j, i: (0, 0)),
                         pl.BlockSpec((tr, w), lambda j, i: (i, ATT_Q_DIM // w + j))] + after_specs,
        out_specs=[blk, blk, pl.BlockSpec((1, 1, w), lambda j, i: (j, 0, 0))],
        out_shape=[SDS((n, HG_DIM), F32), SDS((n, HG_DIM), MXU_DTYPE), SDS((HG_DIM // w, 1, w), F32)],
        compiler_params=_cp("parallel", "arbitrary"))(o2, o2, p, g, dcat, *after)


XATT_TQ = 512


def _xattn_fwd(q, kv, name):
    n, nm = q.shape[0], kv.shape[0]
    tq = min(XATT_TQ, n)
    scale = X_HEAD_DIM ** -0.5

    def body(q_ref, k_ref, v_ref, o_ref):
        s = _dot(q_ref[...], k_ref[...], "nt") * scale
        e = jnp.exp(s - jnp.max(s, axis=-1, keepdims=True))
        o_ref[...] = _dot(e / jnp.sum(e, axis=-1, keepdims=True), v_ref[...]).astype(o_ref.dtype)

    qb = pl.BlockSpec((tq, X_HEAD_DIM), lambda h, i: (i, h))
    return pl.pallas_call(
        body, name=name, grid=(X_HEADS, n // tq),
        in_specs=[qb, pl.BlockSpec((nm, X_HEAD_DIM), lambda h, i: (0, h)), pl.BlockSpec((nm, X_HEAD_DIM), lambda h, i: (0, X_HEADS + h))],
        out_specs=qb, out_shape=SDS(q.shape, MXU_DTYPE), compiler_params=_cp("parallel", "parallel"))(q, kv, kv)


def _xattn_bwd(q, kv, do, name, after=()):
    n, nm = q.shape[0], kv.shape[0]
    tq = min(XATT_TQ, n)
    scale = X_HEAD_DIM ** -0.5
    after, after_specs = _unread(after)

    def body(q_ref, k_ref, v_ref, do_ref, *rest):
        dq_ref, dk_ref, dv_ref = rest[len(after):]

        @pl.when(pl.program_id(1) == 0)
        def _():
            dk_ref[...] = jnp.zeros_like(dk_ref)
            dv_ref[...] = jnp.zeros_like(dv_ref)

        qv, dov = q_ref[...], do_ref[...]
        s = _dot(qv, k_ref[...], "nt") * scale
        e = jnp.exp(s - jnp.max(s, axis=-1, keepdims=True))
        p = e / jnp.sum(e, axis=-1, keepdims=True)
        dp = _dot(dov, v_ref[...], "nt")
        ds = p * (dp - jnp.sum(p * dp, axis=-1, keepdims=True)) * scale
        dq_ref[...] = _dot(ds, k_ref[...]).astype(dq_ref.dtype)
        dk_ref[...] += _dot(ds, qv, "tn")
        dv_ref[...] += _dot(p, dov, "tn")

    qb = pl.BlockSpec((tq, X_HEAD_DIM), lambda h, i: (i, h))
    kb = pl.BlockSpec((nm, X_HEAD_DIM), lambda h, i: (0, h))
    return pl.pallas_call(
        body, name=name, grid=(X_HEADS, n // tq),
        in_specs=[qb, kb, pl.BlockSpec((nm, X_HEAD_DIM), lambda h, i: (0, X_HEADS + h)), qb] + after_specs, out_specs=[qb, kb, kb],
        out_shape=[SDS(q.shape, MXU_DTYPE), SDS((nm, X_HEADS * X_HEAD_DIM), F32), SDS((nm, X_HEADS * X_HEAD_DIM), F32)],
        compiler_params=_cp("parallel", "arbitrary"))(q, kv, kv, do, *after)


def _edge_rows(shape):
    row = lax.broadcasted_iota(jnp.int32, shape, 0)
    return row == 0, row == shape[0] - 1


def _shift_rows(u, down, edges):
    if down:
        return jnp.where(edges[0], 0.0, pltpu.roll(u, 1, axis=0))
    return jnp.where(edges[1], 0.0, pltpu.roll(u, u.shape[0] - 1, axis=0))


def _conv(u, w, b, edges):
    return b + _shift_rows(u, True, edges) * w[0:1, :] + u * w[1:2, :] + _shift_rows(u, False, edges) * w[2:3, :]


def _ff_specs(n):
    gate = lambda rows: pl.BlockSpec((rows, FF_COLS), lambda j: (0, j))
    val = lambda rows: pl.BlockSpec((rows, FF_COLS), lambda j: (0, FF_BLOCKS + j))
    return [gate(n), val(n), gate(3), val(3), gate(1), val(1)], gate


def _conv_gate(u, cw, cb, name):
    n = u.shape[0]
    ins, gate_blk = _ff_specs(n)

    def body(ug_ref, uv_ref, wg_ref, wv_ref, bg_ref, bv_ref, o_ref):
        edges = _edge_rows(ug_ref.shape)
        gate = _conv(ug_ref[...], wg_ref[...], bg_ref[...], edges)
        val = _conv(uv_ref[...], wv_ref[...], bv_ref[...], edges)
        o_ref[...] = (gate * _sigmoid(gate) * val).astype(o_ref.dtype)

    return pl.pallas_call(
        body, name=name, grid=(FF_BLOCKS,), in_specs=ins, out_specs=gate_blk(n), out_shape=SDS((n, D_FF), MXU_DTYPE),
        compiler_params=_cp("parallel"))(u, u, cw, cw, cb, cb)


def _conv_gate_bwd(u, cw, cb, da, name, after=()):
    n = u.shape[0]
    ins, gate_blk = _ff_specs(n)
    after, after_specs = _unread(after)

    def side(dacc, u, w, edges, du_ref, dw_ref, db_ref):
        nxt, prv = _shift_rows(dacc, False, edges), _shift_rows(dacc, True, edges)
        du_ref[...] = (nxt * w[0:1, :] + dacc * w[1:2, :] + prv * w[2:3, :]).astype(du_ref.dtype)
        db_ref[...] = jnp.sum(dacc, axis=0, keepdims=True)
        dw_ref[0:1, :] = jnp.sum(nxt * u, axis=0, keepdims=True)
        dw_ref[1:2, :] = jnp.sum(dacc * u, axis=0, keepdims=True)
        dw_ref[2:3, :] = jnp.sum(prv * u, axis=0, keepdims=True)

    def body(ug_ref, uv_ref, wg_ref, wv_ref, bg_ref, bv_ref, da_ref, *rest):
        dug_ref, duv_ref, dwg_ref, dwv_ref, dbg_ref, dbv_ref = rest[len(after):]
        ug, uv = ug_ref[...], uv_ref[...]
        edges = _edge_rows(ug.shape)
        gate = _conv(ug, wg_ref[...], bg_ref[...], edges)
        val = _conv(uv, wv_ref[...], bv_ref[...], edges)
        sg = _sigmoid(gate)
        dav = da_ref[...].astype(F32)
        side(dav * val * sg * (1.0 + gate * (1.0 - sg)), ug, wg_ref[...], edges, dug_ref, dwg_ref, dbg_ref)
        side(dav * gate * sg, uv, wv_ref[...], edges, duv_ref, dwv_ref, dbv_ref)

    return pl.pallas_call(
        body, name=name, grid=(FF_BLOCKS,), in_specs=ins + [gate_blk(n)] + after_specs,
        out_specs=[gate_blk(n), gate_blk(n), gate_blk(3), gate_blk(3), gate_blk(1), gate_blk(1)],
        out_shape=[SDS((n, D_FF), MXU_DTYPE)] * 2 + [SDS((3, D_FF), F32)] * 2 + [SDS((1, D_FF), F32)] * 2,
        compiler_params=_cp("parallel"))(u, u, cw, cw, cb, cb, da, *after)


def _adamw(w, g, m, v, name):
    r, c = w.shape[-2:]
    tr = _row_tile(r, ELEMENTWISE_ROWS)
    assert w.ndim == 2 or w.shape[:-2] == (1,), (name, w.shape)

    def body(w_ref, g_ref, m_ref, v_ref, d_ref, mo_ref, vo_ref, go_ref):
        gv = g_ref[...]
        go_ref[...] = gv
        mn = ADAM_B1 * m_ref[...] + (1.0 - ADAM_B1) * gv
        vn = ADAM_B2 * v_ref[...] + (1.0 - ADAM_B2) * gv * gv
        m_hat = mn / (1.0 - ADAM_B1 ** ADAM_STEP)
        v_hat = vn / (1.0 - ADAM_B2 ** ADAM_STEP)
        d_ref[...] = -ADAM_LR * (m_hat / (jnp.sqrt(v_hat) + ADAM_EPS) + ADAM_WD * w_ref[...])
        mo_ref[...] = mn
        vo_ref[...] = vn

    blk = pl.BlockSpec((tr, c), lambda i: (i, 0)) if w.ndim == 2 else pl.BlockSpec((1, tr, c), lambda i: (0, i, 0))
    out = SDS(w.shape, F32)
    return pl.pallas_call(body, name=name, grid=(r // tr,), in_specs=[blk] * 4, out_specs=[blk] * 4, out_shape=[out] * 4,
                          compiler_params=_cp("parallel"))(w, g, m, v)


ANY = pl.BlockSpec(memory_space=pl.ANY)


def _place():
    x, y, c = lax.axis_index("x"), lax.axis_index("y"), lax.axis_index("c")
    return x, y, c, [(1 - x, y), (x, 1 - y), (1 - x, 1 - y)]


def _join_halves(bufs, name):
    nt = len(bufs)

    def body(*refs):
        outs = refs[nt:2 * nt]
        send, recv = refs[2 * nt:]
        x, y, c, _ = _place()
        cps = [pltpu.make_async_remote_copy(src_ref=outs[t].at[c], dst_ref=outs[t].at[c], send_sem=send.at[t], recv_sem=recv.at[t],
                                            device_id=(x, y, 1 - c), device_id_type=MESH) for t in range(nt)]
        for cp in cps:
            cp.start()
        for t in range(nt):
            theirs = outs[t].at[1 - c]
            pltpu.make_async_remote_copy(src_ref=theirs, dst_ref=theirs, send_sem=send.at[t], recv_sem=recv.at[t],
                                         device_id=(x, y, 1 - c), device_id_type=MESH).wait_recv()
        for cp in cps:
            cp.wait_send()

    return pl.pallas_call(
        body, name=name, in_specs=[ANY] * nt, out_specs=[ANY] * nt, out_shape=[SDS(b.shape, b.dtype) for b in bufs],
        input_output_aliases={t: t for t in range(nt)},
        scratch_shapes=[pltpu.SemaphoreType.DMA((nt,))] * 2,
        compiler_params=pltpu.CompilerParams(has_side_effects=True))(*bufs)


HBM = pl.BlockSpec(memory_space=pltpu.HBM)
SEM = pl.BlockSpec(memory_space=pltpu.SEMAPHORE)
TOKEN = pl.BlockSpec(memory_space=pltpu.VMEM)
TOKEN_SHAPE = SDS((8, 128), F32)
PEERS = 7


def _in_hbm(a):
    return pltpu.with_memory_space_constraint(a, pltpu.HBM)


def _split_params():
    return pltpu.CompilerParams(has_side_effects=pltpu.SideEffectType.DATAFLOW_SIDE_EFFECTING)


def _gather_start(shards, name, after=()):
    nt = len(shards)
    after, after_specs = _unread(after)

    def body(*refs):
        ins, lands = refs[:nt], refs[nt:2 * nt]
        outs = refs[2 * nt + len(after):]
        sends, recvs = outs[:nt], outs[nt:2 * nt]
        x, y, c, chips = _place()
        me = 2 * x + y
        for t in range(nt):
            h = ins[t].shape[0] // 2
            mine = pl.ds(c * h, h)
            for j, (cx, cy) in enumerate(chips):
                for dc in range(2):
                    pltpu.make_async_remote_copy(src_ref=ins[t].at[mine], dst_ref=lands[t].at[me, mine], send_sem=sends[t].at[2 * j + dc],
                                                 recv_sem=recvs[t].at[2 * j + c], device_id=(cx, cy, dc), device_id_type=MESH).start()
            pltpu.make_async_remote_copy(src_ref=ins[t], dst_ref=lands[t].at[me], send_sem=sends[t].at[PEERS - 1], recv_sem=recvs[t].at[PEERS - 1],
                                         device_id=(x, y, 1 - c), device_id_type=MESH).start()
        outs[-1][...] = jnp.zeros(TOKEN_SHAPE.shape, F32)

    lands = [lax.empty((4,) + s.shape, s.dtype) for s in shards]
    out = pl.pallas_call(
        body, name=name, in_specs=[HBM] * (2 * nt) + after_specs, out_specs=[SEM] * (2 * nt) + [HBM] * (2 * nt) + [TOKEN],
        out_shape=[pltpu.SemaphoreType.DMA((PEERS,))] * (2 * nt)
        + [pltpu.HBM(s.shape, s.dtype) for s in shards] + [pltpu.HBM(l.shape, l.dtype) for l in lands] + [TOKEN_SHAPE],
        input_output_aliases={t: 2 * nt + t for t in range(2 * nt)}, compiler_params=_split_params())(
            *[_in_hbm(s) for s in shards], *[_in_hbm(l) for l in lands], *after)
    return out[:nt], out[nt:2 * nt], out[2 * nt:3 * nt], out[3 * nt:4 * nt], out[-1]


def _gather_wait(sends, recvs, shards, lands, after, name):
    nt = len(shards)

    def body(*refs):
        ins, lands_ref = refs[:nt], refs[nt:2 * nt]
        send_refs, recv_refs = refs[2 * nt:3 * nt], refs[3 * nt:4 * nt]
        x, y, c, chips = _place()
        for t in range(nt):
            h = ins[t].shape[0] // 2
            for j, (cx, cy) in enumerate(chips):
                for cs in range(2):
                    blk = lands_ref[t].at[2 * cx + cy, pl.ds(cs * h, h)]
                    pltpu.make_async_remote_copy(src_ref=blk, dst_ref=blk, send_sem=send_refs[t].at[2 * j + cs], recv_sem=recv_refs[t].at[2 * j + cs],
                                                 device_id=(cx, cy, cs), device_id_type=MESH).wait()
            blk = lands_ref[t].at[2 * x + y]
            pltpu.make_async_remote_copy(src_ref=blk, dst_ref=blk, send_sem=send_refs[t].at[PEERS - 1], recv_sem=recv_refs[t].at[PEERS - 1],
                                         device_id=(x, y, 1 - c), device_id_type=MESH).wait()

    out = pl.pallas_call(
        body, name=name, in_specs=[HBM] * (2 * nt) + [SEM] * (2 * nt) + [ANY], out_specs=[HBM] * (2 * nt),
        out_shape=[pltpu.HBM(s.shape, s.dtype) for s in shards] + [pltpu.HBM(l.shape, l.dtype) for l in lands],
        input_output_aliases={t: t for t in range(2 * nt)}, compiler_params=_split_params())(*shards, *lands, *sends, *recvs, after)
    return out[nt:]


def _gather_pieces_start(shard, name, after=()):
    h = shard.shape[0] // 2
    after, after_specs = _unread(after)

    def body(src, land, *rest):
        send, recv = rest[len(after):len(after) + 2]
        x, y, c, chips = _place()
        me = 2 * x + y
        mine = pl.ds(c * h, h)
        for j, (cx, cy) in enumerate(chips):
            pltpu.make_async_remote_copy(src_ref=src.at[mine], dst_ref=land.at[me, mine], send_sem=send.at[j], recv_sem=recv.at[j],
                                         device_id=(cx, cy, c), device_id_type=MESH).start()
        pltpu.make_async_remote_copy(src_ref=src, dst_ref=land.at[me], send_sem=send.at[len(chips)], recv_sem=recv.at[len(chips)],
                                     device_id=(x, y, 1 - c), device_id_type=MESH).start()
        rest[-1][...] = jnp.zeros(TOKEN_SHAPE.shape, F32)

    land = lax.empty((4,) + shard.shape, shard.dtype)
    return pl.pallas_call(
        body, name=name, in_specs=[HBM, HBM] + after_specs, out_specs=[SEM, SEM, HBM, HBM, TOKEN],
        out_shape=[pltpu.SemaphoreType.DMA((4,)), pltpu.SemaphoreType.DMA((4,)), pltpu.HBM(shard.shape, shard.dtype),
                   pltpu.HBM(land.shape, land.dtype), TOKEN_SHAPE],
        input_output_aliases={0: 2, 1: 3}, compiler_params=_split_params())(_in_hbm(shard), _in_hbm(land), *after)


def _gather_pieces_wait(send, recv, shard, land, after, name):
    h = shard.shape[0] // 2
    after, after_specs = _unread(after)

    def body(*refs):
        land_ref, send_ref, recv_ref = refs[1:4]
        x, y, c, chips = _place()
        for j, (cx, cy) in enumerate(chips):
            blk = land_ref.at[2 * cx + cy, pl.ds(c * h, h)]
            pltpu.make_async_remote_copy(src_ref=blk, dst_ref=blk, send_sem=send_ref.at[j], recv_sem=recv_ref.at[j],
                                         device_id=(cx, cy, c), device_id_type=MESH).wait()
        own = land_ref.at[2 * x + y]
        pltpu.make_async_remote_copy(src_ref=own, dst_ref=own, send_sem=send_ref.at[len(chips)], recv_sem=recv_ref.at[len(chips)],
                                     device_id=(x, y, 1 - c), device_id_type=MESH).wait()

    out = pl.pallas_call(
        body, name=name, in_specs=[HBM, HBM, SEM, SEM] + after_specs, out_specs=[HBM, HBM],
        out_shape=[pltpu.HBM(shard.shape, shard.dtype), pltpu.HBM(land.shape, land.dtype)],
        input_output_aliases={0: 0, 1: 1}, compiler_params=_split_params())(shard, land, send, recv, *after)
    return out[1]


def _pass_pieces(land, name):
    h = land.shape[1] // 2

    def body(_, out, send, recv):
        x, y, c, chips = _place()

        def piece(j, cc):
            cx, cy = chips[j]
            blk = out.at[2 * cx + cy, pl.ds(cc * h, h)]
            return pltpu.make_async_remote_copy(src_ref=blk, dst_ref=blk, send_sem=send.at[j], recv_sem=recv.at[j],
                                                device_id=(x, y, 1 - c), device_id_type=MESH)

        for j in range(len(chips)):
            piece(j, c).start()
        for j in range(len(chips)):
            piece(j, 1 - c).wait_recv()
        for j in range(len(chips)):
            piece(j, c).wait_send()

    return pl.pallas_call(
        body, name=name, in_specs=[ANY], out_specs=ANY, out_shape=SDS(land.shape, land.dtype), input_output_aliases={0: 0},
        scratch_shapes=[pltpu.SemaphoreType.DMA((3,))] * 2, compiler_params=pltpu.CompilerParams(has_side_effects=True))(land)


def _scatter_start(g, name):
    _, r, c_ = g.shape
    h = r // 2

    def body(g_ref, land, send, recv, g_thru, land_thru, token):
        x, y, c, chips = _place()
        for j, (cx, cy) in enumerate(chips):
            for dc in range(2):
                pltpu.make_async_remote_copy(src_ref=g_ref.at[2 * cx + cy, pl.ds(dc * h, h)], dst_ref=land.at[2 * j + c], send_sem=send.at[2 * j + dc],
                                             recv_sem=recv.at[2 * j + c], device_id=(cx, cy, dc), device_id_type=MESH).start()
        pltpu.make_async_remote_copy(src_ref=g_ref.at[2 * x + y, pl.ds((1 - c) * h, h)], dst_ref=land.at[PEERS - 1], send_sem=send.at[PEERS - 1],
                                     recv_sem=recv.at[PEERS - 1], device_id=(x, y, 1 - c), device_id_type=MESH).start()
        token[...] = jnp.zeros(TOKEN_SHAPE.shape, F32)

    land = lax.empty((PEERS, h, c_), g.dtype)
    return pl.pallas_call(
        body, name=name, in_specs=[HBM, HBM], out_specs=[SEM, SEM, HBM, HBM, TOKEN],
        out_shape=[pltpu.SemaphoreType.DMA((PEERS,)), pltpu.SemaphoreType.DMA((PEERS,)), pltpu.HBM(g.shape, g.dtype),
                   pltpu.HBM(land.shape, land.dtype), TOKEN_SHAPE],
        input_output_aliases={0: 2, 1: 3}, compiler_params=_split_params())(_in_hbm(g), _in_hbm(land))


def _scatter_wait(started, after, name):
    nt = len(started)

    def body(*refs):
        lands = refs[nt:2 * nt]
        sends, recvs = refs[2 * nt:3 * nt], refs[3 * nt:4 * nt]
        x, y, c, chips = _place()
        peers = [(cx, cy, dc) for cx, cy in chips for dc in range(2)] + [(x, y, 1 - c)]
        for t in range(nt):
            for k, peer in enumerate(peers):
                blk = lands[t].at[k]
                pltpu.make_async_remote_copy(src_ref=blk, dst_ref=blk, send_sem=sends[t].at[k], recv_sem=recvs[t].at[k],
                                             device_id=peer, device_id_type=MESH).wait()

    gs, lands = [s[2] for s in started], [s[3] for s in started]
    after, after_specs = _unread(after)
    out = pl.pallas_call(
        body, name=name, in_specs=[HBM] * (2 * nt) + [SEM] * (2 * nt) + after_specs, out_specs=[HBM] * (2 * nt),
        out_shape=[pltpu.HBM(a.shape, a.dtype) for a in gs + lands],
        input_output_aliases={t: t for t in range(2 * nt)}, compiler_params=_split_params())(
            *gs, *lands, *[s[0] for s in started], *[s[1] for s in started], *after)
    return out[:nt], out[nt:]


def _flips():
    return [(dx, dy, dc) for dx in range(2) for dy in range(2) for dc in range(2) if (dx, dy, dc) != (0, 0, 0)]


def _flipped(x, y, c, flips):
    dx, dy, dc = flips
    return (1 - x if dx else x, 1 - y if dy else y, 1 - c if dc else c)


def _small_start(v, name, after=()):
    after, after_specs = _unread(after)

    def body(v_ref, land, *rest):
        send, recv = rest[len(after):len(after) + 2]
        x, y, c, _ = _place()
        for j, flips in enumerate(_flips()):
            pltpu.make_async_remote_copy(src_ref=v_ref, dst_ref=land.at[4 * x + 2 * y + c], send_sem=send.at[j], recv_sem=recv.at[j],
                                         device_id=_flipped(x, y, c, flips), device_id_type=MESH).start()
        rest[-1][...] = jnp.zeros(TOKEN_SHAPE.shape, F32)

    land = lax.empty((8,) + v.shape, v.dtype)
    return pl.pallas_call(
        body, name=name, in_specs=[HBM, HBM] + after_specs, out_specs=[SEM, SEM, HBM, HBM, TOKEN],
        out_shape=[pltpu.SemaphoreType.DMA((PEERS,)), pltpu.SemaphoreType.DMA((PEERS,)), pltpu.HBM(v.shape, v.dtype),
                   pltpu.HBM(land.shape, land.dtype), TOKEN_SHAPE],
        input_output_aliases={0: 2, 1: 3}, compiler_params=_split_params())(_in_hbm(v), _in_hbm(land), *after)


def _small_wait(send, recv, v, land, after, name):
    after, after_specs = _unread(after)

    def body(v_ref, land_ref, send_ref, recv_ref, *rest):
        x, y, c, _ = _place()
        for j, flips in enumerate(_flips()):
            px, py, pc = _flipped(x, y, c, flips)
            blk = land_ref.at[4 * px + 2 * py + pc]
            pltpu.make_async_remote_copy(src_ref=blk, dst_ref=blk, send_sem=send_ref.at[j], recv_sem=recv_ref.at[j],
                                         device_id=(px, py, pc), device_id_type=MESH).wait()

    return pl.pallas_call(
        body, name=name, in_specs=[HBM, HBM, SEM, SEM] + after_specs, out_specs=[HBM, HBM],
        out_shape=[pltpu.HBM(v.shape, v.dtype), pltpu.HBM(land.shape, land.dtype)],
        input_output_aliases={0: 0, 1: 1}, compiler_params=_split_params())(v, land, send, recv, *after)


def _sum_small(v, land, name):
    def body(v_ref, land_ref, o_ref):
        x, y, c, _ = _place()
        me = 4 * x + 2 * y + c
        acc = jnp.where(me == 0, v_ref[...], land_ref[0])
        for d in range(1, 8):
            acc = acc + jnp.where(me == d, v_ref[...], land_ref[d])
        o_ref[...] = acc

    vm = pl.BlockSpec(memory_space=pltpu.VMEM)
    return pl.pallas_call(body, name=name, in_specs=[vm, vm], out_specs=vm, out_shape=SDS(v.shape, F32))(v, land)


def _sum_devices(g, land, me, core, name):
    npeer, h, c = land.shape
    tr = _row_tile(h, 2 * ELEMENTWISE_ROWS)
    steps = h // tr

    def body(ix_ref, own_ref, land_ref, o_ref):
        acc = own_ref[0].astype(F32)
        for j in range(npeer):
            acc = acc + land_ref[j].astype(F32)
        o_ref[0] = acc

    grid_spec = pltpu.PrefetchScalarGridSpec(
        num_scalar_prefetch=1, grid=(steps,),
        in_specs=[pl.BlockSpec((1, tr, c), lambda i, ix: (ix[0], ix[1] * steps + i, 0)), pl.BlockSpec((npeer, tr, c), lambda i, ix: (0, i, 0))],
        out_specs=pl.BlockSpec((1, tr, c), lambda i, ix: (ix[1], i, 0)))
    return pl.pallas_call(body, name=name, grid_spec=grid_spec, out_shape=SDS((2, h, c), F32),
                          compiler_params=_cp("parallel"))(jnp.stack([me, core]), g, land)


def _pack_small(parts):
    flat = jnp.concatenate([p.reshape(-1) for p in parts])
    total = flat.shape[0]
    rows = -(-total // 1024) * 8
    return jnp.pad(flat, (0, rows * 128 - total)).reshape(rows, 128)


def _unpack_small(packed, shapes):
    flat = packed.reshape(-1)
    out, off = [], 0
    for s in shapes:
        size = int(np.prod(s))
        out.append(flat[off:off + size].reshape(s))
        off += size
    return out


def _local_step(x, mem, target, w_in_t, first_after, mid_weights, ffn_weights, on_grad, gains, conv_w, conv_b, hg_lb):
    n = x.shape[0]
    cos, sin = _rope_tables(n)
    seg = _hg_segments()
    gp, gn = _hg_pair_sums()
    masks = _hg_level_masks()
    gq2 = jnp.tile(gains["q_norm_g"], (1, 2))
    gk2 = jnp.tile(gains["k_norm_g"], (1, 2))
    a0 = hg_lb[:, 0:1, :]
    a1 = hg_lb[:, 1:2, :]

    p, h1 = _norm_mm(x, gains["pre_mix_g"], w_in_t, F32, TOKEN_TILE, 1664, "in_proj", after=(first_after,), w_turned=True)
    qr, kr = _qk_prep(p, gq2, gk2, cos, sin, "qk_prep")
    heads = lambda a: a.reshape(n, ATT_KV_HEADS, ATT_HEAD_DIM).transpose(1, 0, 2)
    kh = heads(kr)
    vh = heads(p[:, OFF_AV:OFF_AV + ATT_KV_DIM].astype(MXU_DTYPE))
    att = _attn_fwd(qr, kh, vh, "attn_fwd")
    o2, s0, hg_a, hg_e, hg_kept = _hgrn_fwd(p, a0, a1, seg, masks, "hgrn_fwd")
    rec = _hg_post(o2, p, gains["hg_out_norm_g"], "hg_post")
    cat = jnp.concatenate([att, rec], axis=1)
    w_out, w_xq, w_xkv, w_xo = mid_weights(cat)
    mixed, x1 = _mm_resid_norm(cat, w_out, x, gains["post_mix_g"], 512, "out_proj_resid")
    xq, h2 = _norm_mm(x1, gains["pre_x_g"], w_xq, MXU_DTYPE, TOKEN_TILE, 1024, "xq_proj")
    kv, mn = _norm_mm(mem, gains["mem_norm_g"], w_xkv, MXU_DTYPE, 256, 2048, "xkv_proj")
    ox = _xattn_fwd(xq, kv, "xattn_fwd")
    xo, x2 = _mm_resid_norm(ox, w_xo, x1, gains["post_x_g"], 512, "xo_proj_resid")
    w_up = ffn_weights("w_up", x2)
    u, h3 = _norm_mm(x2, gains["pre_ffn_g"], w_up, F32, TOKEN_TILE, 1408, "up_proj")
    act = _conv_gate(u, conv_w, conv_b, "conv_gate")
    w_down = ffn_weights("w_down", act)
    dn, d3, loss = _mm_resid_norm(act, w_down, x2, gains["post_ffn_g"], 512, "down_proj_resid_loss", target=target)

    gs = {}
    d_act, d_dn, gs["post_ffn_g"] = _norm_bwd_mm(dn, gains["post_ffn_g"], d3, w_down, F32, 512, 1408, "ffn_post_bwd_down_dx")
    tok = on_grad("w_down", _mm(act, d_dn, "tn", WIRE_DTYPE, 1408, 1024, "down_dw"))
    du_g, du_v, dcw_g, dcw_v, dcb_g, dcb_v = _conv_gate_bwd(u, conv_w, conv_b, d_act, "conv_gate_bwd", after=(tok,))
    gs["conv_w"] = jnp.concatenate([dcw_g, dcw_v], axis=1)
    gs["conv_b"] = jnp.concatenate([dcb_g, dcb_v], axis=1)
    ff_shard = w_up.shape[2]
    g_up = _dw_by_owner(h3, du_g, ff_shard, 0, None, 512, "up_dw_gate")
    tok = on_grad("w_up", _dw_by_owner(h3, du_v, ff_shard, 2, g_up, 512, "up_dw_value"))
    d2, gs["pre_ffn_g"] = _dx_norm_bwd([(du_g, 0), (du_g, 1), (du_v, 0), (du_v, 1)], w_up, x2, gains["pre_ffn_g"], d3, 512,
                                       "up_dx_pre_bwd", after=(tok,))
    d_ox, d_xo, gs["post_x_g"] = _norm_bwd_mm(xo, gains["post_x_g"], d2, w_xo, MXU_DTYPE, 512, 1024, "x_post_bwd_xo_dx")
    tok = on_grad("w_xo", _mm(ox, d_xo, "tn", WIRE_DTYPE, 512, 1024, "xo_dw"))
    d_xq, d_k, d_v = _xattn_bwd(xq, kv, d_ox, "xattn_bwd", after=(tok,))
    d_kv = jnp.concatenate([d_k, d_v], axis=1).astype(MXU_DTYPE)
    tok = on_grad("w_xq", _mm(h2, d_xq, "tn", WIRE_DTYPE, 512, 1024, "xq_dw"))
    tok_kv = on_grad("w_xkv", _dw_by_owner(mn, d_kv, w_xkv.shape[2], 0, None, 512, "xkv_dw"))
    d1, gs["pre_x_g"] = _dx_norm_bwd([(d_xq, 0)], w_xq[None], x1, gains["pre_x_g"], d2, 512, "xq_dx_pre_bwd", after=(tok, tok_kv))
    d_mn = _mm_nt_parts([(d_kv, s) for s in range(4)], w_xkv, F32, 256, 1024, "xkv_dx")
    _, gs["mem_norm_g"] = _norm_bwd(mem, gains["mem_norm_g"], d_mn, None, MXU_DTYPE, "mem_norm_bwd")
    d_cat, d_mixed, gs["post_mix_g"] = _norm_bwd_mm(mixed, gains["post_mix_g"], d1, w_out, MXU_DTYPE, 512, 1024, "mix_post_bwd_out_dx")
    tok = on_grad("w_out", _mm(cat, d_mixed, "tn", WIRE_DTYPE, 512, 1024, "out_dw"))
    d_o, d_hg, dg_hg = _hg_post_bwd(o2, p, gains["hg_out_norm_g"], d_cat, "hg_post_bwd", after=(tok,))
    gs["hg_out_norm_g"] = dg_hg.reshape(HG_HEADS, HG_HEAD_DIM).sum(axis=0, keepdims=True)
    dhq2, dz2, dhv2, dlb = _hgrn_bwd(p, a0, a1, masks, gp, gn, d_o, s0, hg_a, hg_e, hg_kept, "hgrn_bwd")
    lb = jax.nn.sigmoid(a0 - a1)
    da0 = dlb * lb * (1.0 - lb)
    gs["hg_lb"] = jnp.concatenate([da0, -da0], axis=1)
    d_qr, d_kh, d_vh = _attn_bwd(qr, kh, vh, cat, d_cat, "attn_bwd")
    unheads = lambda a: a.transpose(2, 0, 1).reshape(n, ATT_KV_DIM)
    d_aq, d_ak, dgq, dgk = _qk_prep_bwd(p, gq2, gk2, cos, sin, d_qr, unheads(d_kh), "qk_prep_bwd")
    gs["q_norm_g"] = dgq.reshape(ATT_HEADS, ATT_HEAD_DIM).sum(axis=0, keepdims=True)
    gs["k_norm_g"] = dgk.reshape(ATT_KV_HEADS, ATT_HEAD_DIM).sum(axis=0, keepdims=True)
    d_p = jnp.concatenate([d_aq, d_ak, unheads(d_vh).astype(MXU_DTYPE), (dhq2[0] + dhq2[1]).astype(MXU_DTYPE),
                           dz2[0].astype(MXU_DTYPE), dz2[1].astype(MXU_DTYPE), (dhv2[0] + dhv2[1]).astype(MXU_DTYPE), d_hg], axis=1)
    tok = on_grad("w_in", _mm(d_p, h1, "tn", WIRE_DTYPE, 1664, 1024, "in_dw"))
    grad_x, gs["pre_mix_g"] = _dx_norm_bwd([(d_p, 0)], w_in_t[None], x, gains["pre_mix_g"], d1, 512, "in_dx_pre_bwd", after=(tok,),
                                           b_turned=True)
    return loss, grad_x, gs


MATS = ("w_in", "w_out", "w_xq", "w_xkv", "w_xo", "w_up", "w_down")
GAINS = ("pre_mix_g", "q_norm_g", "k_norm_g", "hg_out_norm_g", "post_mix_g", "pre_x_g", "mem_norm_g", "post_x_g", "pre_ffn_g", "post_ffn_g")
WEIGHTS = ('pre_mix_g', 'w_in', 'q_norm_g', 'k_norm_g', 'hg_lb', 'hg_out_norm_g', 'w_out', 'post_mix_g', 'pre_x_g', 'mem_norm_g', 'w_xq',
           'w_xkv', 'w_xo', 'post_x_g', 'pre_ffn_g', 'w_up', 'conv_w', 'conv_b', 'w_down', 'post_ffn_g')


def kernel(x, mem, pre_mix_g, w_in, q_norm_g, k_norm_g, hg_lb, hg_out_norm_g, w_out, post_mix_g, pre_x_g, mem_norm_g, w_xq, w_xkv, w_xo, post_x_g, pre_ffn_g, w_up, conv_w, conv_b, w_down, post_ffn_g, loss_target, m_pre_mix_g, m_w_in, m_q_norm_g, m_k_norm_g, m_hg_lb, m_hg_out_norm_g, m_w_out, m_post_mix_g, m_pre_x_g, m_mem_norm_g, m_w_xq, m_w_xkv, m_w_xo, m_post_x_g, m_pre_ffn_g, m_w_up, m_conv_w, m_conv_b, m_w_down, m_post_ffn_g, v_pre_mix_g, v_w_in, v_q_norm_g, v_k_norm_g, v_hg_lb, v_hg_out_norm_g, v_w_out, v_post_mix_g, v_pre_x_g, v_mem_norm_g, v_w_xq, v_w_xkv, v_w_xo, v_post_x_g, v_pre_ffn_g, v_w_up, v_conv_w, v_conv_b, v_w_down, v_post_ffn_g):
    args = dict(locals())
    w = {k: args[k] for k in WEIGHTS}
    m = {k: args["m_" + k] for k in WEIGHTS}
    v = {k: args["v_" + k] for k in WEIGHTS}
    chip = 2 * lax.axis_index("x") + lax.axis_index("y")
    core = lax.axis_index("c")

    turned = ("w_in",)
    shards = {k: (jnp.swapaxes(w[k], 1, 2) if k in turned else w[k])[0].astype(WIRE_DTYPE) for k in MATS}

    def whole(k, g):
        return g if k in ("w_xkv", "w_up") else g.reshape(-1, g.shape[-1])

    mid_names, ffn_names = ("w_out", "w_xq", "w_xkv", "w_xo"), ("w_up", "w_down")
    small = _small_start(_pack_small([w["conv_w"][0], w["hg_lb"]]), "gather_small_start")
    w_in_pieces = _gather_pieces_start(shards["w_in"], "gather_w_in_start", after=(small[4],))
    mid = _gather_start([shards[k] for k in mid_names], "gather_mid_start", after=(w_in_pieces[4],))
    ffn = _gather_start([shards[k] for k in ffn_names], "gather_ffn_start", after=(mid[4],))
    w_in_t = whole("w_in", _pass_pieces(_gather_pieces_wait(*w_in_pieces[:4], (ffn[4],), "gather_w_in_wait"), "gather_w_in_pass"))
    mine, others = _small_wait(*small[:4], (w_in_t,), "gather_small_wait")
    small_in = lax.dynamic_update_slice_in_dim(others, mine[None], 2 * chip + core, axis=0)

    def mid_weights(after):
        return [whole(k, g) for k, g in zip(mid_names, _gather_wait(*mid[:4], after, "gather_mid_wait"))]

    def ffn_weights(k, after):
        t = ffn_names.index(k)
        return whole(k, _gather_wait(*[part[t:t + 1] for part in ffn[:4]], after, "gather_wait_" + k)[0])

    cw_parts, lb_parts = [], []
    for s in range(4):
        cw_s, lb_s = _unpack_small(small_in[2 * s], [w["conv_w"][0].shape, w["hg_lb"].shape])
        cw_parts.append(cw_s)
        lb_parts.append(lb_s)
    conv_w_full = jnp.concatenate(cw_parts, axis=1)
    hg_lb_full = jnp.concatenate(lb_parts, axis=2)

    started = {}

    def on_grad(k, g):
        if g.ndim == 2:
            g = g.reshape(4, g.shape[0] // 4, g.shape[1])
        *started[k], token = _scatter_start(g, "grad_start_" + k)
        return token

    gains = {k: w[k] for k in GAINS}
    loss_part, grad_x, gs = _local_step(x[0], mem[0], loss_target[0], w_in_t, ffn[4], mid_weights, ffn_weights, on_grad, gains,
                                        conv_w_full, w["conv_b"], hg_lb_full)
    gs["loss"] = loss_part

    grads, delta, new_m, new_v = {}, {}, {}, {}

    def reduce_matrices(names, after, tag):
        sent, landed = _scatter_wait([started[k] for k in names], after, "grad_wait_" + tag)
        halves = [_sum_devices(g, land, chip, core, "grad_sum_" + k) for k, g, land in zip(names, sent, landed)]
        for k, r in zip(names, _join_halves(halves, "grad_join_" + tag)):
            grads[k] = r.reshape(1, -1, r.shape[-1])

    def adamw(names):
        for k in names:
            shape = w[k].shape
            keep = len(shape) == 3 and shape[0] == 1
            if k in turned:
                view, back = (lambda a: jnp.swapaxes(a, 1, 2)), (lambda a: jnp.swapaxes(a, 1, 2))
                g = grads[k]
            else:
                view = (lambda a: a.reshape(shape)) if keep else (lambda a: a.reshape(-1, shape[-1]))
                back = lambda a: a.reshape(shape)
                g = view(grads[k])
            d, mo, vo, go = _adamw(view(w[k]), g, view(m[k]), view(v[k]), "adamw_" + k)
            delta[k], new_m[k], new_v[k], grads[k] = back(d), back(mo), back(vo), back(go)

    small_names = GAINS + ("conv_b", "conv_w", "hg_lb")
    packed = _pack_small([gs[k] for k in small_names + ("loss",)])
    small = _small_start(packed, "reduce_small_start", after=(grad_x,))

    early = tuple(k for k in MATS if k != "w_in")
    reduce_matrices(early, (grad_x, small[4]), "early")
    adamw(early)

    mine, others = _small_wait(*small[:4], tuple(new_v[k] for k in early), "reduce_small_wait")
    reduced_small = _sum_small(mine, others, "reduce_small_sum")
    *summed, loss = _unpack_small(reduced_small, [gs[k].shape for k in small_names + ("loss",)])
    loss = loss[0, 0]
    for k, g in zip(small_names, summed):
        grads[k] = g
    ncw = w["conv_w"].shape[2]
    grads["conv_w"] = lax.dynamic_slice_in_dim(grads["conv_w"], chip * ncw, ncw, axis=1)[None]
    nlb = w["hg_lb"].shape[2]
    grads["hg_lb"] = lax.dynamic_slice_in_dim(grads["hg_lb"], chip * nlb, nlb, axis=2)
    replicated = GAINS + ("conv_b",)
    shapes = [w[k].shape for k in replicated]
    rows = sum(int(np.prod(s)) for s in shapes) // 128
    pack = lambda d: jnp.concatenate([d[k].reshape(-1) for k in replicated]).reshape(rows, 128)
    outs = _adamw(pack(w), reduced_small[:rows], pack(m), pack(v), "adamw_replicated")
    for into, packed_out in zip((delta, new_m, new_v, grads), outs):
        for k, a in zip(replicated, _unpack_small(packed_out, shapes)):
            into[k] = a
    adamw(("conv_w", "hg_lb"))

    reduce_matrices(("w_in",), tuple(new_v[k] for k in early + small_names), "late")
    adamw(("w_in",))
    return (loss, grad_x[None], *[grads[k] for k in WEIGHTS], *[delta[k] for k in WEIGHTS],
            *[new_m[k] for k in WEIGHTS], *[new_v[k] for k in WEIGHTS])
```

```python
import numpy as np
import jax
import jax.numpy as jnp
from jax import lax
from jax.experimental import pallas as pl
from jax.experimental.pallas import tpu as pltpu

F32 = jnp.float32
MXU_DTYPE = jnp.bfloat16
WIRE_DTYPE = jnp.bfloat16
VMEM_LIMIT_BYTES = 56 * 1024 * 1024
ROWS_PER_16BIT_TILE = 16
ELEMENTWISE_ROWS = 256
EPS = 1e-6
MESH = pl.DeviceIdType.MESH

GRID_W = 64
ATT_HEADS, ATT_KV_HEADS, ATT_HEAD_DIM = 8, 2, 64
ATT_GROUP = ATT_HEADS // ATT_KV_HEADS
ATT_Q_DIM, ATT_KV_DIM = 512, 128
ROPE_THETA = 10000.0
HG_HEADS, HG_HEAD_DIM, HG_DIM = 4, 128, 512
HG_CHUNK = 128
HG_LEVELS = 7
HG_PAIR = 2 * HG_HEAD_DIM
HG_KEPT = 7
X_HEADS, X_HEAD_DIM = 4, 256
D_FF = 2816
FF_COLS = 256
FF_BLOCKS = D_FF // FF_COLS
OFF_AK, OFF_AV, OFF_HQ, OFF_ZF, OFF_ZB, OFF_HI, OFF_HG = 512, 640, 768, 1280, 1792, 2304, 2816

ADAM_LR, ADAM_B1, ADAM_B2, ADAM_EPS, ADAM_WD, ADAM_STEP = 0.001, 0.9, 0.999, 1e-08, 0.01, 10

SDS = jax.ShapeDtypeStruct


def _cp(*sem):
    return pltpu.CompilerParams(dimension_semantics=sem, vmem_limit_bytes=VMEM_LIMIT_BYTES)


def _row_tile(rows, cap):
    if rows <= cap:
        return rows
    return max(t for t in range(ROWS_PER_16BIT_TILE, cap + 1, ROWS_PER_16BIT_TILE) if rows % t == 0)


def _dot(a, b, form="nn"):
    dims = {"nn": (((1,), (0,)), ((), ())), "nt": (((1,), (1,)), ((), ())), "tn": (((0,), (0,)), ((), ()))}[form]
    return lax.dot_general(a.astype(MXU_DTYPE), b.astype(MXU_DTYPE), dims, preferred_element_type=F32)


def _sigmoid(x):
    return 1.0 / (1.0 + jnp.exp(-x))


def _rstd(x):
    return lax.rsqrt(jnp.mean(x * x, axis=-1, keepdims=True) + EPS)


def _rms_bwd(x, g, dy):
    r = _rstd(x)
    xh = x * r
    dn = dy * g
    dx = r * (dn - xh * jnp.mean(dn * xh, axis=-1, keepdims=True))
    return dx, jnp.sum(dy * xh, axis=0, keepdims=True)


def _unread(after):
    after = tuple(a for a in after if a is not None)
    return after, [pl.BlockSpec(memory_space=pl.ANY)] * len(after)


def _mm(a, b, form, out_dtype, tm, tn, name, after=()):
    after, after_specs = _unread(after)
    if form == "nn":
        (m, k), n = a.shape, b.shape[1]
    elif form == "nt":
        (m, k), n = a.shape, b.shape[0]
    else:
        (k, m), n = a.shape, b.shape[1]
    tm, tn = min(tm, m), min(tn, n)
    assert m % tm == 0 and n % tn == 0, (name, m, n, tm, tn)

    def body(a_ref, b_ref, *rest):
        o_ref = rest[-1]
        o_ref[...] = _dot(a_ref[...], b_ref[...], form).astype(o_ref.dtype)

    a_spec = pl.BlockSpec((k, tm), lambda i, j: (0, i)) if form == "tn" else pl.BlockSpec((tm, k), lambda i, j: (i, 0))
    b_spec = pl.BlockSpec((tn, k), lambda i, j: (j, 0)) if form == "nt" else pl.BlockSpec((k, tn), lambda i, j: (0, j))
    return pl.pallas_call(
        body, name=name, grid=(m // tm, n // tn), in_specs=[a_spec, b_spec] + after_specs,
        out_specs=pl.BlockSpec((tm, tn), lambda i, j: (i, j)), out_shape=SDS((m, n), out_dtype),
        compiler_params=_cp("parallel", "parallel"))(a, b, *after)


def _mm_nt_parts(a_parts, b, out_dtype, tm, tn, name, after=()):
    after, after_specs = _unread(after)
    parts, n, p = b.shape
    m = a_parts[0][0].shape[0]
    tm, tn = min(tm, m), min(tn, n)
    assert m % tm == 0 and n % tn == 0 and len(a_parts) == parts, (name, m, b.shape)

    def body(*refs):
        o_ref = refs[-1]
        acc = _dot(refs[0][...], refs[parts][0], "nt")
        for s in range(1, parts):
            acc = acc + _dot(refs[s][...], refs[parts + s][0], "nt")
        o_ref[...] = acc.astype(o_ref.dtype)

    a_specs = [pl.BlockSpec((tm, p), lambda i, j, cb=cb: (i, cb)) for _, cb in a_parts]
    b_specs = [pl.BlockSpec((1, tn, p), lambda i, j, s=s: (s, j, 0)) for s in range(parts)]
    return pl.pallas_call(
        body, name=name, grid=(m // tm, n // tn), in_specs=a_specs + b_specs + after_specs,
        out_specs=pl.BlockSpec((tm, tn), lambda i, j: (i, j)), out_shape=SDS((m, n), out_dtype),
        compiler_params=_cp("parallel", "parallel"))(*[arr for arr, _ in a_parts], *([b] * parts), *after)


def _norm_bwd_mm(y, g, d, w, out_dtype, tm, tn, name):
    n, dm = y.shape
    nn = w.shape[0]
    tm, tn = min(tm, n), min(tn, nn)
    assert n % tm == 0 and nn % tn == 0 and w.shape[1] == dm, (name, y.shape, w.shape)

    def body(y_ref, g_ref, d_ref, w_ref, dx_ref, dy_ref, dg_ref, dys):
        i, j = pl.program_id(0), pl.program_id(1)

        @pl.when(jnp.logical_and(i == 0, j == 0))
        def _():
            dg_ref[...] = jnp.zeros_like(dg_ref)

        @pl.when(j == 0)
        def _():
            dy, dg = _rms_bwd(y_ref[...], g_ref[...], d_ref[...])
            dy = dy.astype(MXU_DTYPE)
            dys[...] = dy
            dy_ref[...] = dy
            dg_ref[...] += dg

        dx_ref[...] = _dot(dys[...], w_ref[...], "nt").astype(dx_ref.dtype)

    row = pl.BlockSpec((tm, dm), lambda i, j: (i, 0))
    vec = pl.BlockSpec((1, dm), lambda i, j: (0, 0))
    return pl.pallas_call(
        body, name=name, grid=(n // tm, nn // tn), in_specs=[row, vec, row, pl.BlockSpec((tn, dm), lambda i, j: (j, 0))],
        out_specs=[pl.BlockSpec((tm, tn), lambda i, j: (i, j)), row, vec],
        out_shape=[SDS((n, nn), out_dtype), SDS((n, dm), MXU_DTYPE), SDS((1, dm), F32)],
        scratch_shapes=[pltpu.VMEM((tm, dm), MXU_DTYPE)],
        compiler_params=_cp("arbitrary", "arbitrary"))(y, g, d, w)


def _mm_resid_norm(a, b, x, g, tm, name, target=None):
    n, k = a.shape
    d = b.shape[1]
    tm = min(tm, n)
    assert n % tm == 0 and x.shape == (n, d), (name, a.shape, b.shape)
    with_loss = target is not None

    def body(a_ref, b_ref, x_ref, g_ref, *rest):
        y = _dot(a_ref[...], b_ref[...])
        out = x_ref[...] + y * _rstd(y) * g_ref[...]
        if not with_loss:
            y_ref, o_ref = rest
            y_ref[...] = y
            o_ref[...] = out
            return
        t_ref, y_ref, d_ref, l_ref = rest
        y_ref[...] = y
        diff = out - t_ref[...]
        d_ref[...] = diff * (1.0 / d)

        @pl.when(pl.program_id(0) == 0)
        def _():
            l_ref[...] = jnp.zeros_like(l_ref)

        l_ref[...] += 0.5 * jnp.sum(jnp.mean(diff * diff, axis=-1, keepdims=True), axis=0, keepdims=True)

    row = pl.BlockSpec((tm, d), lambda i: (i, 0))
    ins = [pl.BlockSpec((tm, k), lambda i: (i, 0)), pl.BlockSpec((k, d), lambda i: (0, 0)), row, pl.BlockSpec((1, d), lambda i: (0, 0))]
    out = SDS((n, d), F32)
    if with_loss:
        return pl.pallas_call(body, name=name, grid=(n // tm,), in_specs=ins + [row], out_specs=[row, row, pl.BlockSpec((1, 1), lambda i: (0, 0))],
                              out_shape=[out, out, SDS((1, 1), F32)], compiler_params=_cp("arbitrary"))(a, b, x, g, target)
    return pl.pallas_call(body, name=name, grid=(n // tm,), in_specs=ins, out_specs=[row, row], out_shape=[out, out],
                          compiler_params=_cp("parallel"))(a, b, x, g)


def _dx_norm_bwd(a_parts, b, x, g, res, tm, name, after=(), b_turned=False):
    after, after_specs = _unread(after)
    parts, d, p = (b.shape[0], b.shape[2], b.shape[1]) if b_turned else b.shape
    form = "nn" if b_turned else "nt"
    n = x.shape[0]
    tm = min(tm, n)
    assert n % tm == 0 and len(a_parts) == parts and x.shape[1] == d, (name, x.shape, b.shape)

    def body(*refs):
        x_ref, g_ref, res_ref = refs[2 * parts:2 * parts + 3]
        dx_ref, dg_ref = refs[-2:]
        dh = _dot(refs[0][...], refs[parts][0], form)
        for s in range(1, parts):
            dh = dh + _dot(refs[s][...], refs[parts + s][0], form)
        dx, dg = _rms_bwd(x_ref[...], g_ref[...], dh)
        dx_ref[...] = dx + res_ref[...]

        @pl.when(pl.program_id(0) == 0)
        def _():
            dg_ref[...] = jnp.zeros_like(dg_ref)

        dg_ref[...] += dg

    a_specs = [pl.BlockSpec((tm, p), lambda i, cb=cb: (i, cb)) for _, cb in a_parts]
    b_specs = [pl.BlockSpec((1,) + b.shape[1:], lambda i, s=s: (s, 0, 0)) for s in range(parts)]
    row = pl.BlockSpec((tm, d), lambda i: (i, 0))
    vec = pl.BlockSpec((1, d), lambda i: (0, 0))
    return pl.pallas_call(
        body, name=name, grid=(n // tm,), in_specs=a_specs + b_specs + [row, vec, row] + after_specs,
        out_specs=[row, vec], out_shape=[SDS((n, d), F32), SDS((1, d), F32)],
        compiler_params=_cp("arbitrary"))(*[arr for arr, _ in a_parts], *([b] * parts), x, g, res, *after)


def _dw_by_owner(a, b, tn, first, into, tm, name):
    k, m = a.shape
    cnt = b.shape[1] // tn
    tm = min(tm, m)
    assert m % tm == 0 and b.shape[1] == cnt * tn and first + cnt <= 4, (name, a.shape, b.shape)

    def body(a_ref, b_ref, *rest):
        rest[-1][0] = _dot(a_ref[...], b_ref[...], "tn").astype(rest[-1].dtype)

    extra = [] if into is None else [into]
    return pl.pallas_call(
        body, name=name, grid=(m // tm, cnt),
        in_specs=[pl.BlockSpec((k, tm), lambda i, j: (0, i)), pl.BlockSpec((k, tn), lambda i, j: (0, j))] + [pl.BlockSpec(memory_space=pl.ANY)] * len(extra),
        out_specs=pl.BlockSpec((1, tm, tn), lambda i, j: (first + j, i, 0)), out_shape=SDS((4, m, tn), WIRE_DTYPE),
        input_output_aliases={2: 0} if extra else {},
        compiler_params=_cp("parallel", "parallel"))(a, b, *extra)


def _norm_mm(x, g, w, out_dtype, tm, tn, name, after=(), w_turned=False):
    after, after_specs = _unread(after)
    m, d = x.shape
    sharded = w.ndim == 3
    n = w.shape[0] if w_turned else w.shape[-1] * (w.shape[0] if sharded else 1)
    tm, tn = min(tm, m), (w.shape[-1] if sharded else min(tn, n))
    assert m % tm == 0 and n % tn == 0 and not (sharded and w_turned), (name, m, n, tm, tn)

    def body(x_ref, g_ref, w_ref, *rest):
        o_ref, h_ref, hs = rest[-3:]

        @pl.when(pl.program_id(1) == 0)
        def _():
            xv = x_ref[...]
            h = (xv * _rstd(xv) * g_ref[...]).astype(MXU_DTYPE)
            hs[...] = h
            h_ref[...] = h

        o_ref[...] = _dot(hs[...], w_ref[0] if sharded else w_ref[...], "nt" if w_turned else "nn").astype(o_ref.dtype)

    if w_turned:
        w_spec = pl.BlockSpec((tn, d), lambda i, j: (j, 0))
    else:
        w_spec = pl.BlockSpec((1, d, tn), lambda i, j: (j, 0, 0)) if sharded else pl.BlockSpec((d, tn), lambda i, j: (0, j))
    return pl.pallas_call(
        body, name=name, grid=(m // tm, n // tn),
        in_specs=[pl.BlockSpec((tm, d), lambda i, j: (i, 0)), pl.BlockSpec((1, d), lambda i, j: (0, 0)), w_spec] + after_specs,
        out_specs=[pl.BlockSpec((tm, tn), lambda i, j: (i, j)), pl.BlockSpec((tm, d), lambda i, j: (i, 0))],
        out_shape=[SDS((m, n), out_dtype), SDS((m, d), MXU_DTYPE)],
        scratch_shapes=[pltpu.VMEM((tm, d), MXU_DTYPE)],
        compiler_params=_cp("parallel", "arbitrary"))(x, g, w, *after)


ROW_TILE = 512
TOKEN_TILE = 1024


def _norm_bwd(x, g, dy, res, out_dtype, name):
    n, d = x.shape
    tr = min(ROW_TILE, n)
    has_res = res is not None

    def body(*refs):
        x_ref, g_ref, dy_ref = refs[:3]
        dx_ref, dg_ref = refs[-2:]
        dx, dg = _rms_bwd(x_ref[...], g_ref[...], dy_ref[...].astype(F32))
        if has_res:
            dx = dx + refs[3][...]
        dx_ref[...] = dx.astype(dx_ref.dtype)

        @pl.when(pl.program_id(0) == 0)
        def _():
            dg_ref[...] = jnp.zeros_like(dg_ref)

        dg_ref[...] += dg

    row = pl.BlockSpec((tr, d), lambda i: (i, 0))
    vec = pl.BlockSpec((1, d), lambda i: (0, 0))
    ins = [x, g, dy] + ([res] if has_res else [])
    return pl.pallas_call(
        body, name=name, grid=(n // tr,), in_specs=[row, vec, row] + ([row] if has_res else []),
        out_specs=[row, vec], out_shape=[SDS((n, d), out_dtype), SDS((1, d), F32)],
        compiler_params=_cp("arbitrary"))(*ins)


def _rope_tables(n):
    pairs = ATT_HEAD_DIM // 4
    t = np.arange(n)
    inv = np.power(ROPE_THETA, -np.arange(pairs, dtype=np.float32) / pairs).astype(np.float32)
    ang = np.concatenate([(t // GRID_W)[:, None].astype(np.float32) * inv, (t % GRID_W)[:, None].astype(np.float32) * inv], axis=-1)
    cos = np.repeat(np.cos(ang), 2, axis=-1)
    sin = np.repeat(np.sin(ang), 2, axis=-1) * np.tile(np.array([-1.0, 1.0], np.float32), ATT_HEAD_DIM // 2)
    return jnp.asarray(np.tile(cos, 2), F32), jnp.asarray(np.tile(sin, 2), F32)


def _swap_pairs(x):
    lane = lax.broadcasted_iota(jnp.int32, x.shape, 1)
    return jnp.where((lane & 1) == 0, pltpu.roll(x, 127, axis=1), pltpu.roll(x, 1, axis=1))


def _head_mean(v):
    lane = lax.broadcasted_iota(jnp.int32, v.shape, 1)
    lo = jnp.where(lane < ATT_HEAD_DIM, v, 0.0)
    s0 = jnp.sum(lo, axis=-1, keepdims=True)
    s1 = jnp.sum(v - lo, axis=-1, keepdims=True)
    return jnp.where(lane < ATT_HEAD_DIM, s0, s1) * (1.0 / ATT_HEAD_DIM)


def _qk_prep(p, gq, gk, cos, sin, name):
    n = p.shape[0]
    tr = min(ROW_TILE, n)

    def one(xv, g, c, s):
        xn = xv * lax.rsqrt(_head_mean(xv * xv) + EPS) * g
        return xn * c + _swap_pairs(xn) * s

    def body(q_ref, k_ref, gq_ref, gk_ref, c_ref, s_ref, qo_ref, ko_ref):
        c, s = c_ref[...], s_ref[...]
        for j in range(ATT_Q_DIM // 128):
            qo_ref[:, j * 128:(j + 1) * 128] = one(q_ref[:, j * 128:(j + 1) * 128], gq_ref[...], c, s).astype(qo_ref.dtype)
        ko_ref[...] = one(k_ref[...], gk_ref[...], c, s).astype(ko_ref.dtype)

    vec = pl.BlockSpec((1, 128), lambda i: (0, 0))
    tab = pl.BlockSpec((tr, 128), lambda i: (i, 0))
    return pl.pallas_call(
        body, name=name, grid=(n // tr,),
        in_specs=[pl.BlockSpec((tr, ATT_Q_DIM), lambda i: (i, 0)), pl.BlockSpec((tr, 128), lambda i: (i, OFF_AK // 128)), vec, vec, tab, tab],
        out_specs=[pl.BlockSpec((tr, ATT_Q_DIM), lambda i: (i, 0)), tab],
        out_shape=[SDS((n, ATT_Q_DIM), MXU_DTYPE), SDS((n, ATT_KV_DIM), MXU_DTYPE)],
        compiler_params=_cp("parallel"))(p, p, gq, gk, cos, sin)


def _qk_prep_bwd(p, gq, gk, cos, sin, dq, dk, name):
    n = p.shape[0]
    tr = min(ROW_TILE, n)

    def one(xv, g, c, s, dout):
        dxn = dout * c + _swap_pairs(dout * s)
        r = lax.rsqrt(_head_mean(xv * xv) + EPS)
        xh = xv * r
        dn = dxn * g
        dx = r * (dn - xh * _head_mean(dn * xh))
        return dx, jnp.sum(dxn * xh, axis=0, keepdims=True)

    def body(q_ref, k_ref, gq_ref, gk_ref, c_ref, s_ref, dq_ref, dk_ref, dqo_ref, dko_ref, dgq_ref, dgk_ref):
        @pl.when(pl.program_id(0) == 0)
        def _():
            dgq_ref[...] = jnp.zeros_like(dgq_ref)
            dgk_ref[...] = jnp.zeros_like(dgk_ref)

        c, s = c_ref[...], s_ref[...]
        for j in range(ATT_Q_DIM // 128):
            sl = slice(j * 128, (j + 1) * 128)
            dx, dg = one(q_ref[:, sl], gq_ref[...], c, s, dq_ref[:, sl])
            dqo_ref[:, sl] = dx.astype(dqo_ref.dtype)
            dgq_ref[:, sl] += dg
        dx, dg = one(k_ref[...], gk_ref[...], c, s, dk_ref[...])
        dko_ref[...] = dx.astype(dko_ref.dtype)
        dgk_ref[...] += dg

    vec = pl.BlockSpec((1, 128), lambda i: (0, 0))
    tab = pl.BlockSpec((tr, 128), lambda i: (i, 0))
    qrow = pl.BlockSpec((tr, ATT_Q_DIM), lambda i: (i, 0))
    return pl.pallas_call(
        body, name=name, grid=(n // tr,),
        in_specs=[qrow, pl.BlockSpec((tr, 128), lambda i: (i, OFF_AK // 128)), vec, vec, tab, tab, qrow, tab],
        out_specs=[qrow, tab, pl.BlockSpec((1, ATT_Q_DIM), lambda i: (0, 0)), vec],
        out_shape=[SDS((n, ATT_Q_DIM), MXU_DTYPE), SDS((n, ATT_KV_DIM), MXU_DTYPE), SDS((1, ATT_Q_DIM), F32), SDS((1, 128), F32)],
        compiler_params=_cp("arbitrary"))(p, p, gq, gk, cos, sin, dq, dk)


ATT_FWD_STEP = (256, 4)
ATT_BWD_STEP = (512, 2)


def _attn_fwd(q, k, v, name):
    n = q.shape[0]
    tq, step_heads = min(ATT_FWD_STEP[0], n), ATT_FWD_STEP[1]
    scale = ATT_HEAD_DIM ** -0.5
    gw = step_heads * ATT_HEAD_DIM
    parts = ATT_GROUP // step_heads

    def body(q_ref, k_ref, v_ref, o_ref):
        kk, vv = k_ref[0], v_ref[0]
        v_ones = jnp.concatenate([vv, jnp.ones_like(vv)], axis=1)
        outs = []
        for g in range(step_heads):
            s = _dot(q_ref[:, g * ATT_HEAD_DIM:(g + 1) * ATT_HEAD_DIM] * scale, kk, "nt")
            e = jnp.exp(s - jnp.max(s, axis=-1, keepdims=True))
            ov = _dot(e, v_ones)
            outs.append(ov[:, :ATT_HEAD_DIM] / ov[:, ATT_HEAD_DIM:])
        o_ref[...] = jnp.concatenate(outs, axis=-1).astype(o_ref.dtype)

    kv = pl.BlockSpec((1, n, ATT_HEAD_DIM), lambda h, i, pr: (h, 0, 0))
    qb = pl.BlockSpec((tq, gw), lambda h, i, pr: (i, h * parts + pr))
    return pl.pallas_call(
        body, name=name, grid=(ATT_KV_HEADS, n // tq, parts), in_specs=[qb, kv, kv],
        out_specs=qb, out_shape=SDS((n, ATT_Q_DIM), MXU_DTYPE),
        compiler_params=_cp("parallel", "parallel", "parallel"))(q, k, v)


def _attn_bwd(q, k, v, o, do, name):
    n = q.shape[0]
    tq, step_heads = min(ATT_BWD_STEP[0], n), ATT_BWD_STEP[1]
    scale = ATT_HEAD_DIM ** -0.5
    gw = step_heads * ATT_HEAD_DIM
    parts = ATT_GROUP // step_heads

    def body(q_ref, k_ref, v_ref, o_ref, do_ref, dq_ref, dk_ref, dv_ref):
        @pl.when(jnp.logical_and(pl.program_id(1) == 0, pl.program_id(2) == 0))
        def _():
            dk_ref[...] = jnp.zeros_like(dk_ref)
            dv_ref[...] = jnp.zeros_like(dv_ref)

        kk, vv = k_ref[0], v_ref[0]
        dqs = []
        dk_acc = jnp.zeros((ATT_HEAD_DIM, n), F32)
        dv_acc = jnp.zeros((ATT_HEAD_DIM, n), F32)
        for g in range(step_heads):
            sl = slice(g * ATT_HEAD_DIM, (g + 1) * ATT_HEAD_DIM)
            qg, dog = q_ref[:, sl] * scale, do_ref[:, sl].astype(F32)
            s = _dot(qg, kk, "nt")
            e = jnp.exp(s - jnp.max(s, axis=-1, keepdims=True))
            inv = 1.0 / jnp.sum(e, axis=-1, keepdims=True)
            delta = jnp.sum(dog * o_ref[:, sl].astype(F32), axis=-1, keepdims=True)
            dse = e * (_dot(dog, vv, "nt") - delta)
            dqs.append(_dot(dse, kk) * (inv * scale))
            dk_acc += _dot(qg.astype(F32) * inv, dse, "tn")
            dv_acc += _dot(dog * inv, e, "tn")
        dq_ref[...] = jnp.concatenate(dqs, axis=-1)
        dk_ref[0] += dk_acc
        dv_ref[0] += dv_acc

    kv = pl.BlockSpec((1, n, ATT_HEAD_DIM), lambda h, i, pr: (h, 0, 0))
    kvt = pl.BlockSpec((1, ATT_HEAD_DIM, n), lambda h, i, pr: (h, 0, 0))
    qb = pl.BlockSpec((tq, gw), lambda h, i, pr: (i, h * parts + pr))
    return pl.pallas_call(
        body, name=name, grid=(ATT_KV_HEADS, n // tq, parts), in_specs=[qb, kv, kv, qb, qb], out_specs=[qb, kvt, kvt],
        out_shape=[SDS((n, ATT_Q_DIM), F32), SDS((ATT_KV_HEADS, ATT_HEAD_DIM, n), F32), SDS((ATT_KV_HEADS, ATT_HEAD_DIM, n), F32)],
        compiler_params=_cp("parallel", "arbitrary", "arbitrary"))(q, k, v, o, do)


def _both_directions(mats, axis):
    fwd = np.concatenate(mats, axis=axis).astype(np.float32)
    bwd = np.concatenate([m[::-1, ::-1] for m in mats], axis=axis).astype(np.float32)
    return jnp.asarray(np.stack([fwd, bwd]), MXU_DTYPE)


def _hg_segments():
    c = HG_CHUNK
    t = np.arange(c)[:, None]
    r = np.arange(c)[None, :]
    mats = [(r <= t)]
    for lev in range(HG_LEVELS):
        h = c >> (lev + 1)
        mid = (t // (2 * h)) * (2 * h) + h - 1
        hi = (t // h) % 2 == 1
        mats.append(np.where(hi, (r > mid) & (r <= t), (r > t) & (r <= mid)))
    mats.append(r > t)
    return _both_directions(mats, 0)


def _hg_pair_sums():
    c = HG_CHUNK
    r = np.arange(c)[:, None]
    t = np.arange(c)[None, :]
    gp, gn = [t >= r], [t < r]
    for lev in range(HG_LEVELS):
        sh = HG_LEVELS - 1 - lev
        same = (r >> sh) == (t >> sh)
        gp.append(same & (t >= r))
        gn.append(same & (t < r))
    return _both_directions(gp, 1), _both_directions(gn, 1)


def _split_dot(mat, x):
    hi = x.astype(MXU_DTYPE)
    lo = (x - hi.astype(F32)).astype(MXU_DTYPE)
    return _dot(mat, hi) + _dot(mat, lo)


def _hg_gates(hq, z, a0, a1):
    q = hq * _sigmoid(hq)
    sg = _sigmoid(z)
    lb = _sigmoid(a0 - a1)
    f = lb + (1.0 - lb) * sg
    k = (1.0 - lb) * (1.0 - sg)
    return q, f, k, sg, lb


def _hg_level_masks():
    c = HG_CHUNK
    t = np.arange(c)
    later, same = [], []
    for lev in range(HG_LEVELS):
        sh = HG_LEVELS - 1 - lev
        later.append(np.broadcast_to((((t >> sh) & 1) == 1)[:, None], (c, HG_HEAD_DIM)))
        same.append((t[:, None] >> (sh + 1)) == (t[None, :] >> (sh + 1)))
    same.append(t[:, None] == t[None, :])
    later = np.stack(later).astype(np.float32)
    return jnp.asarray(np.stack([later, 1.0 - later]), F32), jnp.asarray(np.stack(same).astype(np.float32), F32)


def _hg_level(q, k, ex, later_ref, lev):
    e = ex[lev + 1]
    e_q = e * later_ref[0, lev]
    e_k = e - e_q
    return q * e_q, k * e_k, e_q, e_k


def _hg_intra(q, k, ex, later_ref, same_ref):
    a = same_ref[HG_LEVELS] * jnp.sum(q * k, axis=-1, keepdims=True)
    for lev in range(HG_LEVELS):
        qs, ks, _, _ = _hg_level(q, k, ex, later_ref, lev)
        a = a + same_ref[lev] * _dot(qs, ks, "nt")
    return a


def _hg_specs(n, with_time):
    c = HG_CHUNK
    nc = n // c

    def chunk(d, i):
        first = d if with_time else 1 - d
        return i + first * (nc - 1 - 2 * i)

    def pcols(off, dir_stride=0):
        return [pl.BlockSpec((c, HG_PAIR), lambda d, i, j=j: (chunk(d, i), off // HG_PAIR + dir_stride // HG_PAIR * d + j)) for j in range(2)]

    specs = dict(
        hq=pcols(OFF_HQ), v=pcols(OFF_HI), z=pcols(OFF_ZF, OFF_ZB - OFF_ZF),
        shared=pl.BlockSpec((c, HG_DIM), lambda d, i: (chunk(d, i), 0)),
        per_dir=pl.BlockSpec((1, c, HG_DIM), lambda d, i: (d, chunk(d, i), 0)),
        vec=pl.BlockSpec((1, 1, HG_DIM), lambda d, i: (d, 0, 0)),
        seg=pl.BlockSpec((1, (HG_LEVELS + 2) * c, c), lambda d, i: (d, 0, 0)),
        sums=pl.BlockSpec((1, c, (HG_LEVELS + 1) * c), lambda d, i: (d, 0, 0)),
        later=pl.BlockSpec((1, HG_LEVELS, c, HG_HEAD_DIM), lambda d, i: (d, 0, 0, 0)),
        same=pl.BlockSpec((HG_LEVELS + 1, c, c), lambda d, i: (0, 0, 0)),
        state=pl.BlockSpec((1, HG_HEADS, 1, HG_HEAD_DIM, HG_HEAD_DIM), lambda d, i: (d, 0, chunk(d, i), 0, 0)),
        weights=pl.BlockSpec((1, HG_HEADS, 1, c, c), lambda d, i: (d, 0, chunk(d, i), 0, 0)),
        levels=pl.BlockSpec((1, HG_HEADS, 1, HG_LEVELS, c, HG_HEAD_DIM), lambda d, i: (d, 0, chunk(d, i), 0, 0, 0)),
        kept=pl.BlockSpec((1, HG_KEPT, c, HG_DIM), lambda d, i: (d, 0, chunk(d, i), 0)))
    return nc, specs


def _hg_head(refs, hh):
    off = (hh % 2) * HG_HEAD_DIM
    return refs[hh // 2][:, off:off + HG_HEAD_DIM]


def _hg_lanes(hh):
    return slice(hh * HG_HEAD_DIM, (hh + 1) * HG_HEAD_DIM)


def _hg_exps(seg_ref, f):
    c = HG_CHUNK
    args = _split_dot(seg_ref[0], jnp.log(f))
    return [jnp.exp(args[j * c:(j + 1) * c]) for j in range(HG_LEVELS + 2)]


def _hg_last_row(a, mirrored):
    return jnp.where(mirrored, a[0:1, :], a[HG_CHUNK - 1:HG_CHUNK, :])


def _hgrn_fwd(p, a0, a1, seg, masks, name):
    n = p.shape[0]
    nc, sp = _hg_specs(n, True)

    def body(hq0, hq1, z0, z1, v0, v1, a0_ref, a1_ref, seg_ref, later_ref, same_ref, o_ref, s0_ref, a_ref, e_ref, g_ref, st):
        @pl.when(pl.program_id(1) == 0)
        def _():
            st[...] = jnp.zeros_like(st)

        mirrored = pl.program_id(0) == 1
        for hh in range(HG_HEADS):
            ln = _hg_lanes(hh)
            hqv = _hg_head((hq0, hq1), hh)
            q, f, k, sg, _ = _hg_gates(hqv, _hg_head((z0, z1), hh), a0_ref[0, :, ln], a1_ref[0, :, ln])
            vv = _hg_head((v0, v1), hh)
            ex = _hg_exps(seg_ref, f)
            for lev in range(HG_LEVELS):
                e_ref[0, hh, 0, lev] = ex[lev + 1].astype(e_ref.dtype)
            sq = _sigmoid(hqv)
            for j, kept in enumerate((q, k, f, sg, sq * (1.0 + hqv * (1.0 - sq)), ex[0], ex[HG_LEVELS + 1])):
                g_ref[0, j, :, ln] = kept
            a = _hg_intra(q, k, ex, later_ref, same_ref).astype(MXU_DTYPE)
            a_ref[0, hh, 0] = a
            s_t = st[hh]
            s0_ref[0, hh, 0] = s_t
            o_ref[0, :, ln] = _dot(a, vv) + _dot(q * ex[0], s_t, "nt")
            st[hh] = s_t * _hg_last_row(ex[0], mirrored) + _dot(vv, k * ex[HG_LEVELS + 1], "tn")

    return pl.pallas_call(
        body, name=name, grid=(2, nc), in_specs=sp["hq"] + sp["z"] + sp["v"] + [sp["vec"], sp["vec"], sp["seg"], sp["later"], sp["same"]],
        out_specs=[sp["per_dir"], sp["state"], sp["weights"], sp["levels"], sp["kept"]],
        out_shape=[SDS((2, n, HG_DIM), F32), SDS((2, HG_HEADS, nc, HG_HEAD_DIM, HG_HEAD_DIM), F32),
                   SDS((2, HG_HEADS, nc, HG_CHUNK, HG_CHUNK), MXU_DTYPE),
                   SDS((2, HG_HEADS, nc, HG_LEVELS, HG_CHUNK, HG_HEAD_DIM), MXU_DTYPE), SDS((2, HG_KEPT, n, HG_DIM), F32)],
        scratch_shapes=[pltpu.VMEM((HG_HEADS, HG_HEAD_DIM, HG_HEAD_DIM), F32)],
        compiler_params=_cp("parallel", "arbitrary"))(p, p, p, p, p, p, a0, a1, seg, *masks)


def _hgrn_bwd(p, a0, a1, masks, gp, gn, do, s0, a, e, kept, name):
    n = p.shape[0]
    nc, sp = _hg_specs(n, False)


    def body(v0, v1, a0_ref, a1_ref, later_ref, same_ref, gp_ref, gn_ref, do_ref, s0_ref, a_ref, e_ref, g_ref,
             dhq_ref, dz_ref, dv_ref, dlb_ref, rt):
        @pl.when(pl.program_id(1) == 0)
        def _():
            rt[...] = jnp.zeros_like(rt)
            dlb_ref[...] = jnp.zeros_like(dlb_ref)

        mirrored = pl.program_id(0) == 1
        for hh in range(HG_HEADS):
            ln = _hg_lanes(hh)
            q, k, f, sg, dsilu, e_first, e_last = (g_ref[0, j, :, ln] for j in range(HG_KEPT))
            lb = _sigmoid(a0_ref[0, :, ln] - a1_ref[0, :, ln])
            vv, dov = _hg_head((v0, v1), hh), do_ref[:, ln]
            ex = [e_first] + [e_ref[0, hh, 0, lev].astype(F32) for lev in range(HG_LEVELS)] + [e_last]
            a = a_ref[0, hh, 0]
            da = _dot(dov, vv, "nt")
            diag = jnp.sum(dov * vv, axis=-1, keepdims=True)
            s_t = s0_ref[0, hh, 0]
            r_t = rt[hh]
            k_end = k * ex[HG_LEVELS + 1]
            dv_ref[0, :, ln] = _dot(a, dov, "tn") + _dot(k_end, r_t, "nt")
            dq_inter = ex[0] * _dot(dov, s_t)
            dk_inter = ex[HG_LEVELS + 1] * _dot(vv, r_t)
            dq = diag * k + dq_inter
            dk = diag * q + dk_inter
            q_terms, k_terms = [q * dq_inter], [k * dk_inter]
            for lev in range(HG_LEVELS):
                qs, ks, e_q, e_k = _hg_level(q, k, ex, later_ref, lev)
                pairs = da * same_ref[lev]
                q_part = e_q * _dot(pairs, ks)
                k_part = e_k * _dot(pairs, qs, "tn")
                dq, dk = dq + q_part, dk + k_part
                q_terms.append(q * q_part)
                k_terms.append(k * k_part)
            decay = _hg_last_row(ex[0], mirrored)
            rt[hh] = r_t * decay + _dot(dov, q * ex[0], "tn")
            later = decay * jnp.sum(s_t * r_t, axis=0, keepdims=True)
            dlf = _dot(gp_ref[0], jnp.concatenate(q_terms, axis=0)) + _dot(gn_ref[0], jnp.concatenate(k_terms, axis=0)) + later
            df = dlf / f - dk
            dz_ref[0, :, ln] = df * (1.0 - lb) * sg * (1.0 - sg)
            dlb_ref[0, :, ln] += jnp.sum(df * (1.0 - sg), axis=0, keepdims=True)
            dhq_ref[0, :, ln] = dq * dsilu

    out = SDS((2, n, HG_DIM), F32)
    return pl.pallas_call(
        body, name=name, grid=(2, nc),
        in_specs=sp["v"] + [sp["vec"], sp["vec"], sp["later"], sp["same"], sp["sums"], sp["sums"],
                            sp["shared"], sp["state"], sp["weights"], sp["levels"], sp["kept"]],
        out_specs=[sp["per_dir"], sp["per_dir"], sp["per_dir"], sp["vec"]], out_shape=[out, out, out, SDS((2, 1, HG_DIM), F32)],
        scratch_shapes=[pltpu.VMEM((HG_HEADS, HG_HEAD_DIM, HG_HEAD_DIM), F32)],
        compiler_params=_cp("parallel", "arbitrary"))(p, p, a0, a1, *masks, gp, gn, do, s0, a, e, kept)


def _hg_post(o2, p, g, name):
    n = p.shape[0]
    tr = min(ROW_TILE, n)
    w = 2 * HG_HEAD_DIM

    def body(of_ref, ob_ref, hg_ref, g_ref, o_ref):
        for j in range(2):
            sl = slice(j * HG_HEAD_DIM, (j + 1) * HG_HEAD_DIM)
            o = of_ref[0, :, sl] + ob_ref[0, :, sl]
            hg = hg_ref[:, sl]
            o_ref[:, sl] = (o * _rstd(o) * g_ref[...] * (hg * _sigmoid(hg))).astype(o_ref.dtype)

    blk = pl.BlockSpec((tr, w), lambda i, j: (i, j))
    dirs = [pl.BlockSpec((1, tr, w), lambda i, j, d=d: (d, i, j)) for d in range(2)]
    return pl.pallas_call(
        body, name=name, grid=(n // tr, HG_DIM // w),
        in_specs=dirs + [pl.BlockSpec((tr, w), lambda i, j: (i, OFF_HG // w + j)), pl.BlockSpec((1, HG_HEAD_DIM), lambda i, j: (0, 0))],
        out_specs=blk, out_shape=SDS((n, HG_DIM), MXU_DTYPE), compiler_params=_cp("parallel", "parallel"))(o2, o2, p, g)


def _hg_post_bwd(o2, p, g, dcat, name, after=()):
    n = p.shape[0]
    tr = min(ROW_TILE, n)
    w = 2 * HG_HEAD_DIM
    after, after_specs = _unread(after)

    def body(of_ref, ob_ref, hg_ref, g_ref, d_ref, *rest):
        do_ref, dhg_ref, dg_ref = rest[len(after):]

        @pl.when(pl.program_id(1) == 0)
        def _():
            dg_ref[...] = jnp.zeros_like(dg_ref)

        for j in range(2):
            sl = slice(j * HG_HEAD_DIM, (j + 1) * HG_HEAD_DIM)
            o = of_ref[0, :, sl] + ob_ref[0, :, sl]
            hg = hg_ref[:, sl]
            d = d_ref[:, sl].astype(F32)
            sg = _sigmoid(hg)
            on = o * _rstd(o) * g_ref[...]
            dhg_ref[:, sl] = (d * on * sg * (1.0 + hg * (1.0 - sg))).astype(dhg_ref.dtype)
            dx, dg = _rms_bwd(o, g_ref[...], d * hg * sg)
            do_ref[:, sl] = dx
            dg_ref[0, :, sl] += dg

    blk = pl.BlockSpec((tr, w), lambda j, i: (i, j))
    dirs = [pl.BlockSpec((1, tr, w), lambda j, i, d=d: (d, i, j)) for d in range(2)]
    return pl.pallas_call(
        body, name=name, grid=(HG_DIM // w, n // tr),
        in_specs=dirs + [pl.BlockSpec((tr, w), lambda j, i: (i, OFF_HG // w + j)), pl.BlockSpec((1, HG_HEAD_DIM), lambda j, i: (0, 0)),
                         pl.BlockSpec((tr, w), lambda j, i: (i, ATT_Q_DIM // w + j))] + after_specs,
        out_specs=[blk, blk, pl.BlockSpec((1, 1, w), lambda j, i: (j, 0, 0))],
        out_shape=[SDS((n, HG_DIM), F32), SDS((n, HG_DIM), MXU_DTYPE), SDS((HG_DIM // w, 1, w), F32)],
        compiler_params=_cp("parallel", "arbitrary"))(o2, o2, p, g, dcat, *after)


XATT_TQ = 512


def _xattn_fwd(q, kv, name):
    n, nm = q.shape[0], kv.shape[0]
    tq = min(XATT_TQ, n)
    scale = X_HEAD_DIM ** -0.5

    def body(q_ref, k_ref, v_ref, o_ref):
        s = _dot(q_ref[...], k_ref[...], "nt") * scale
        e = jnp.exp(s - jnp.max(s, axis=-1, keepdims=True))
        o_ref[...] = _dot(e / jnp.sum(e, axis=-1, keepdims=True), v_ref[...]).astype(o_ref.dtype)

    qb = pl.BlockSpec((tq, X_HEAD_DIM), lambda h, i: (i, h))
    return pl.pallas_call(
        body, name=name, grid=(X_HEADS, n // tq),
        in_specs=[qb, pl.BlockSpec((nm, X_HEAD_DIM), lambda h, i: (0, h)), pl.BlockSpec((nm, X_HEAD_DIM), lambda h, i: (0, X_HEADS + h))],
        out_specs=qb, out_shape=SDS(q.shape, MXU_DTYPE), compiler_params=_cp("parallel", "parallel"))(q, kv, kv)


def _xattn_bwd(q, kv, do, name, after=()):
    n, nm = q.shape[0], kv.shape[0]
    tq = min(XATT_TQ, n)
    scale = X_HEAD_DIM ** -0.5
    after, after_specs = _unread(after)

    def body(q_ref, k_ref, v_ref, do_ref, *rest):
        dq_ref, dk_ref, dv_ref = rest[len(after):]

        @pl.when(pl.program_id(1) == 0)
        def _():
            dk_ref[...] = jnp.zeros_like(dk_ref)
            dv_ref[...] = jnp.zeros_like(dv_ref)

        qv, dov = q_ref[...], do_ref[...]
        s = _dot(qv, k_ref[...], "nt") * scale
        e = jnp.exp(s - jnp.max(s, axis=-1, keepdims=True))
        p = e / jnp.sum(e, axis=-1, keepdims=True)
        dp = _dot(dov, v_ref[...], "nt")
        ds = p * (dp - jnp.sum(p * dp, axis=-1, keepdims=True)) * scale
        dq_ref[...] = _dot(ds, k_ref[...]).astype(dq_ref.dtype)
        dk_ref[...] += _dot(ds, qv, "tn")
        dv_ref[...] += _dot(p, dov, "tn")

    qb = pl.BlockSpec((tq, X_HEAD_DIM), lambda h, i: (i, h))
    kb = pl.BlockSpec((nm, X_HEAD_DIM), lambda h, i: (0, h))
    return pl.pallas_call(
        body, name=name, grid=(X_HEADS, n // tq),
        in_specs=[qb, kb, pl.BlockSpec((nm, X_HEAD_DIM), lambda h, i: (0, X_HEADS + h)), qb] + after_specs, out_specs=[qb, kb, kb],
        out_shape=[SDS(q.shape, MXU_DTYPE), SDS((nm, X_HEADS * X_HEAD_DIM), F32), SDS((nm, X_HEADS * X_HEAD_DIM), F32)],
        compiler_params=_cp("parallel", "arbitrary"))(q, kv, kv, do, *after)


def _edge_rows(shape):
    row = lax.broadcasted_iota(jnp.int32, shape, 0)
    return row == 0, row == shape[0] - 1


def _shift_rows(u, down, edges):
    if down:
        return jnp.where(edges[0], 0.0, pltpu.roll(u, 1, axis=0))
    return jnp.where(edges[1], 0.0, pltpu.roll(u, u.shape[0] - 1, axis=0))


def _conv(u, w, b, edges):
    return b + _shift_rows(u, True, edges) * w[0:1, :] + u * w[1:2, :] + _shift_rows(u, False, edges) * w[2:3, :]


def _ff_specs(n):
    gate = lambda rows: pl.BlockSpec((rows, FF_COLS), lambda j: (0, j))
    val = lambda rows: pl.BlockSpec((rows, FF_COLS), lambda j: (0, FF_BLOCKS + j))
    return [gate(n), val(n), gate(3), val(3), gate(1), val(1)], gate


def _conv_gate(u, cw, cb, name):
    n = u.shape[0]
    ins, gate_blk = _ff_specs(n)

    def body(ug_ref, uv_ref, wg_ref, wv_ref, bg_ref, bv_ref, o_ref):
        edges = _edge_rows(ug_ref.shape)
        gate = _conv(ug_ref[...], wg_ref[...], bg_ref[...], edges)
        val = _conv(uv_ref[...], wv_ref[...], bv_ref[...], edges)
        o_ref[...] = (gate * _sigmoid(gate) * val).astype(o_ref.dtype)

    return pl.pallas_call(
        body, name=name, grid=(FF_BLOCKS,), in_specs=ins, out_specs=gate_blk(n), out_shape=SDS((n, D_FF), MXU_DTYPE),
        compiler_params=_cp("parallel"))(u, u, cw, cw, cb, cb)


def _conv_gate_bwd(u, cw, cb, da, name, after=()):
    n = u.shape[0]
    ins, gate_blk = _ff_specs(n)
    after, after_specs = _unread(after)

    def side(dacc, u, w, edges, du_ref, dw_ref, db_ref):
        nxt, prv = _shift_rows(dacc, False, edges), _shift_rows(dacc, True, edges)
        du_ref[...] = (nxt * w[0:1, :] + dacc * w[1:2, :] + prv * w[2:3, :]).astype(du_ref.dtype)
        db_ref[...] = jnp.sum(dacc, axis=0, keepdims=True)
        dw_ref[0:1, :] = jnp.sum(nxt * u, axis=0, keepdims=True)
        dw_ref[1:2, :] = jnp.sum(dacc * u, axis=0, keepdims=True)
        dw_ref[2:3, :] = jnp.sum(prv * u, axis=0, keepdims=True)

    def body(ug_ref, uv_ref, wg_ref, wv_ref, bg_ref, bv_ref, da_ref, *rest):
        dug_ref, duv_ref, dwg_ref, dwv_ref, dbg_ref, dbv_ref = rest[len(after):]
        ug, uv = ug_ref[...], uv_ref[...]
        edges = _edge_rows(ug.shape)
        gate = _conv(ug, wg_ref[...], bg_ref[...], edges)
        val = _conv(uv, wv_ref[...], bv_ref[...], edges)
        sg = _sigmoid(gate)
        dav = da_ref[...].astype(F32)
        side(dav * val * sg * (1.0 + gate * (1.0 - sg)), ug, wg_ref[...], edges, dug_ref, dwg_ref, dbg_ref)
        side(dav * gate * sg, uv, wv_ref[...], edges, duv_ref, dwv_ref, dbv_ref)

    return pl.pallas_call(
        body, name=name, grid=(FF_BLOCKS,), in_specs=ins + [gate_blk(n)] + after_specs,
        out_specs=[gate_blk(n), gate_blk(n), gate_blk(3), gate_blk(3), gate_blk(1), gate_blk(1)],
        out_shape=[SDS((n, D_FF), MXU_DTYPE)] * 2 + [SDS((3, D_FF), F32)] * 2 + [SDS((1, D_FF), F32)] * 2,
        compiler_params=_cp("parallel"))(u, u, cw, cw, cb, cb, da, *after)


def _adamw(w, g, m, v, name):
    r, c = w.shape[-2:]
    tr = _row_tile(r, ELEMENTWISE_ROWS)
    assert w.ndim == 2 or w.shape[:-2] == (1,), (name, w.shape)

    def body(w_ref, g_ref, m_ref, v_ref, d_ref, mo_ref, vo_ref, go_ref):
        gv = g_ref[...]
        go_ref[...] = gv
        mn = ADAM_B1 * m_ref[...] + (1.0 - ADAM_B1) * gv
        vn = ADAM_B2 * v_ref[...] + (1.0 - ADAM_B2) * gv * gv
        m_hat = mn / (1.0 - ADAM_B1 ** ADAM_STEP)
        v_hat = vn / (1.0 - ADAM_B2 ** ADAM_STEP)
        d_ref[...] = -ADAM_LR * (m_hat / (jnp.sqrt(v_hat) + ADAM_EPS) + ADAM_WD * w_ref[...])
        mo_ref[...] = mn
        vo_ref[...] = vn

    blk = pl.BlockSpec((tr, c), lambda i: (i, 0)) if w.ndim == 2 else pl.BlockSpec((1, tr, c), lambda i: (0, i, 0))
    out = SDS(w.shape, F32)
    return pl.pallas_call(body, name=name, grid=(r // tr,), in_specs=[blk] * 4, out_specs=[blk] * 4, out_shape=[out] * 4,
                          compiler_params=_cp("parallel"))(w, g, m, v)


ANY = pl.BlockSpec(memory_space=pl.ANY)


def _place():
    x, y, c = lax.axis_index("x"), lax.axis_index("y"), lax.axis_index("c")
    return x, y, c, [(1 - x, y), (x, 1 - y), (1 - x, 1 - y)]


HBM = pl.BlockSpec(memory_space=pltpu.HBM)
SEM = pl.BlockSpec(memory_space=pltpu.SEMAPHORE)
TOKEN = pl.BlockSpec(memory_space=pltpu.VMEM)
TOKEN_SHAPE = SDS((8, 128), F32)
PEERS = 7


def _in_hbm(a):
    return pltpu.with_memory_space_constraint(a, pltpu.HBM)


def _split_params():
    return pltpu.CompilerParams(has_side_effects=pltpu.SideEffectType.DATAFLOW_SIDE_EFFECTING)


def _gather_start(shards, name, after=()):
    nt = len(shards)
    after, after_specs = _unread(after)

    def body(*refs):
        ins, lands = refs[:nt], refs[nt:2 * nt]
        outs = refs[2 * nt + len(after):]
        sends, recvs = outs[:nt], outs[nt:2 * nt]
        x, y, c, chips = _place()
        me = 2 * x + y
        for t in range(nt):
            h = ins[t].shape[0] // 2
            mine = pl.ds(c * h, h)
            for j, (cx, cy) in enumerate(chips):
                for dc in range(2):
                    pltpu.make_async_remote_copy(src_ref=ins[t].at[mine], dst_ref=lands[t].at[me, mine], send_sem=sends[t].at[2 * j + dc],
                                                 recv_sem=recvs[t].at[2 * j + c], device_id=(cx, cy, dc), device_id_type=MESH).start()
            pltpu.make_async_remote_copy(src_ref=ins[t], dst_ref=lands[t].at[me], send_sem=sends[t].at[PEERS - 1], recv_sem=recvs[t].at[PEERS - 1],
                                         device_id=(x, y, 1 - c), device_id_type=MESH).start()
        outs[-1][...] = jnp.zeros(TOKEN_SHAPE.shape, F32)

    lands = [lax.empty((4,) + s.shape, s.dtype) for s in shards]
    out = pl.pallas_call(
        body, name=name, in_specs=[HBM] * (2 * nt) + after_specs, out_specs=[SEM] * (2 * nt) + [HBM] * (2 * nt) + [TOKEN],
        out_shape=[pltpu.SemaphoreType.DMA((PEERS,))] * (2 * nt)
        + [pltpu.HBM(s.shape, s.dtype) for s in shards] + [pltpu.HBM(l.shape, l.dtype) for l in lands] + [TOKEN_SHAPE],
        input_output_aliases={t: 2 * nt + t for t in range(2 * nt)}, compiler_params=_split_params())(
            *[_in_hbm(s) for s in shards], *[_in_hbm(l) for l in lands], *after)
    return out[:nt], out[nt:2 * nt], out[2 * nt:3 * nt], out[3 * nt:4 * nt], out[-1]


def _gather_wait(sends, recvs, shards, lands, after, name):
    nt = len(shards)

    def body(*refs):
        ins, lands_ref = refs[:nt], refs[nt:2 * nt]
        send_refs, recv_refs = refs[2 * nt:3 * nt], refs[3 * nt:4 * nt]
        x, y, c, chips = _place()
        for t in range(nt):
            h = ins[t].shape[0] // 2
            for j, (cx, cy) in enumerate(chips):
                for cs in range(2):
                    blk = lands_ref[t].at[2 * cx + cy, pl.ds(cs * h, h)]
                    pltpu.make_async_remote_copy(src_ref=blk, dst_ref=blk, send_sem=send_refs[t].at[2 * j + cs], recv_sem=recv_refs[t].at[2 * j + cs],
                                                 device_id=(cx, cy, cs), device_id_type=MESH).wait()
            blk = lands_ref[t].at[2 * x + y]
            pltpu.make_async_remote_copy(src_ref=blk, dst_ref=blk, send_sem=send_refs[t].at[PEERS - 1], recv_sem=recv_refs[t].at[PEERS - 1],
                                         device_id=(x, y, 1 - c), device_id_type=MESH).wait()

    out = pl.pallas_call(
        body, name=name, in_specs=[HBM] * (2 * nt) + [SEM] * (2 * nt) + [ANY], out_specs=[HBM] * (2 * nt),
        out_shape=[pltpu.HBM(s.shape, s.dtype) for s in shards] + [pltpu.HBM(l.shape, l.dtype) for l in lands],
        input_output_aliases={t: t for t in range(2 * nt)}, compiler_params=_split_params())(*shards, *lands, *sends, *recvs, after)
    return out[nt:]


def _gather_pieces_start(shard, name, after=()):
    h = shard.shape[0] // 2
    after, after_specs = _unread(after)

    def body(src, land, *rest):
        send, recv = rest[len(after):len(after) + 2]
        x, y, c, chips = _place()
        me = 2 * x + y
        mine = pl.ds(c * h, h)
        for j, (cx, cy) in enumerate(chips):
            pltpu.make_async_remote_copy(src_ref=src.at[mine], dst_ref=land.at[me, mine], send_sem=send.at[j], recv_sem=recv.at[j],
                                         device_id=(cx, cy, c), device_id_type=MESH).start()
        pltpu.make_async_remote_copy(src_ref=src, dst_ref=land.at[me], send_sem=send.at[len(chips)], recv_sem=recv.at[len(chips)],
                                     device_id=(x, y, 1 - c), device_id_type=MESH).start()
        rest[-1][...] = jnp.zeros(TOKEN_SHAPE.shape, F32)

    land = lax.empty((4,) + shard.shape, shard.dtype)
    return pl.pallas_call(
        body, name=name, in_specs=[HBM, HBM] + after_specs, out_specs=[SEM, SEM, HBM, HBM, TOKEN],
        out_shape=[pltpu.SemaphoreType.DMA((4,)), pltpu.SemaphoreType.DMA((4,)), pltpu.HBM(shard.shape, shard.dtype),
                   pltpu.HBM(land.shape, land.dtype), TOKEN_SHAPE],
        input_output_aliases={0: 2, 1: 3}, compiler_params=_split_params())(_in_hbm(shard), _in_hbm(land), *after)


def _gather_pieces_wait(send, recv, shard, land, after, name):
    h = shard.shape[0] // 2
    after, after_specs = _unread(after)

    def body(*refs):
        land_ref, send_ref, recv_ref = refs[1:4]
        x, y, c, chips = _place()
        for j, (cx, cy) in enumerate(chips):
            blk = land_ref.at[2 * cx + cy, pl.ds(c * h, h)]
            pltpu.make_async_remote_copy(src_ref=blk, dst_ref=blk, send_sem=send_ref.at[j], recv_sem=recv_ref.at[j],
                                         device_id=(cx, cy, c), device_id_type=MESH).wait()
        own = land_ref.at[2 * x + y]
        pltpu.make_async_remote_copy(src_ref=own, dst_ref=own, send_sem=send_ref.at[len(chips)], recv_sem=recv_ref.at[len(chips)],
                                     device_id=(x, y, 1 - c), device_id_type=MESH).wait()

    out = pl.pallas_call(
        body, name=name, in_specs=[HBM, HBM, SEM, SEM] + after_specs, out_specs=[HBM, HBM],
        out_shape=[pltpu.HBM(shard.shape, shard.dtype), pltpu.HBM(land.shape, land.dtype)],
        input_output_aliases={0: 0, 1: 1}, compiler_params=_split_params())(shard, land, send, recv, *after)
    return out[1]


def _pass_pieces(land, name):
    h = land.shape[1] // 2

    def body(_, out, send, recv):
        x, y, c, chips = _place()

        def piece(j, cc):
            cx, cy = chips[j]
            blk = out.at[2 * cx + cy, pl.ds(cc * h, h)]
            return pltpu.make_async_remote_copy(src_ref=blk, dst_ref=blk, send_sem=send.at[j], recv_sem=recv.at[j],
                                                device_id=(x, y, 1 - c), device_id_type=MESH)

        for j in range(len(chips)):
            piece(j, c).start()
        for j in range(len(chips)):
            piece(j, 1 - c).wait_recv()
        for j in range(len(chips)):
            piece(j, c).wait_send()

    return pl.pallas_call(
        body, name=name, in_specs=[ANY], out_specs=ANY, out_shape=SDS(land.shape, land.dtype), input_output_aliases={0: 0},
        scratch_shapes=[pltpu.SemaphoreType.DMA((3,))] * 2, compiler_params=pltpu.CompilerParams(has_side_effects=True))(land)


def _scatter_start(g, name):
    _, r, c_ = g.shape
    h = r // 2

    def body(g_ref, land, send, recv, g_thru, land_thru, token):
        x, y, c, chips = _place()
        for j, (cx, cy) in enumerate(chips):
            for dc in range(2):
                pltpu.make_async_remote_copy(src_ref=g_ref.at[2 * cx + cy, pl.ds(dc * h, h)], dst_ref=land.at[2 * j + c], send_sem=send.at[2 * j + dc],
                                             recv_sem=recv.at[2 * j + c], device_id=(cx, cy, dc), device_id_type=MESH).start()
        pltpu.make_async_remote_copy(src_ref=g_ref.at[2 * x + y, pl.ds((1 - c) * h, h)], dst_ref=land.at[PEERS - 1], send_sem=send.at[PEERS - 1],
                                     recv_sem=recv.at[PEERS - 1], device_id=(x, y, 1 - c), device_id_type=MESH).start()
        token[...] = jnp.zeros(TOKEN_SHAPE.shape, F32)

    land = lax.empty((PEERS, h, c_), g.dtype)
    return pl.pallas_call(
        body, name=name, in_specs=[HBM, HBM], out_specs=[SEM, SEM, HBM, HBM, TOKEN],
        out_shape=[pltpu.SemaphoreType.DMA((PEERS,)), pltpu.SemaphoreType.DMA((PEERS,)), pltpu.HBM(g.shape, g.dtype),
                   pltpu.HBM(land.shape, land.dtype), TOKEN_SHAPE],
        input_output_aliases={0: 2, 1: 3}, compiler_params=_split_params())(_in_hbm(g), _in_hbm(land))


def _scatter_wait(started, after, name):
    nt = len(started)

    def body(*refs):
        lands = refs[nt:2 * nt]
        sends, recvs = refs[2 * nt:3 * nt], refs[3 * nt:4 * nt]
        x, y, c, chips = _place()
        peers = [(cx, cy, dc) for cx, cy in chips for dc in range(2)] + [(x, y, 1 - c)]
        for t in range(nt):
            for k, peer in enumerate(peers):
                blk = lands[t].at[k]
                pltpu.make_async_remote_copy(src_ref=blk, dst_ref=blk, send_sem=sends[t].at[k], recv_sem=recvs[t].at[k],
                                             device_id=peer, device_id_type=MESH).wait()

    gs, lands = [s[2] for s in started], [s[3] for s in started]
    after, after_specs = _unread(after)
    out = pl.pallas_call(
        body, name=name, in_specs=[HBM] * (2 * nt) + [SEM] * (2 * nt) + after_specs, out_specs=[HBM] * (2 * nt),
        out_shape=[pltpu.HBM(a.shape, a.dtype) for a in gs + lands],
        input_output_aliases={t: t for t in range(2 * nt)}, compiler_params=_split_params())(
            *gs, *lands, *[s[0] for s in started], *[s[1] for s in started], *after)
    return out[:nt], out[nt:]


def _join_start(bufs, name, after=()):
    nt = len(bufs)
    after, after_specs = _unread(after)

    def body(*refs):
        send, recv = refs[nt + len(after):nt + len(after) + 2]
        x, y, c, _ = _place()
        for t in range(nt):
            pltpu.make_async_remote_copy(src_ref=refs[t].at[c], dst_ref=refs[t].at[c], send_sem=send.at[t], recv_sem=recv.at[t],
                                         device_id=(x, y, 1 - c), device_id_type=MESH).start()
        refs[-1][...] = jnp.zeros(TOKEN_SHAPE.shape, F32)

    out = pl.pallas_call(
        body, name=name, in_specs=[HBM] * nt + after_specs, out_specs=[SEM, SEM] + [HBM] * nt + [TOKEN],
        out_shape=[pltpu.SemaphoreType.DMA((nt,))] * 2 + [pltpu.HBM(b.shape, b.dtype) for b in bufs] + [TOKEN_SHAPE],
        input_output_aliases={t: 2 + t for t in range(nt)}, compiler_params=_split_params())(*[_in_hbm(b) for b in bufs], *after)
    return out[0], out[1], out[2:2 + nt], out[-1]


def _join_wait(send, recv, bufs, after, name):
    nt = len(bufs)
    after, after_specs = _unread(after)

    def body(*refs):
        send_ref, recv_ref = refs[nt:nt + 2]
        x, y, c, _ = _place()
        for t in range(nt):
            theirs = refs[t].at[1 - c]
            pltpu.make_async_remote_copy(src_ref=theirs, dst_ref=theirs, send_sem=send_ref.at[t], recv_sem=recv_ref.at[t],
                                         device_id=(x, y, 1 - c), device_id_type=MESH).wait()

    return pl.pallas_call(
        body, name=name, in_specs=[HBM] * nt + [SEM, SEM] + after_specs, out_specs=[HBM] * nt,
        out_shape=[pltpu.HBM(b.shape, b.dtype) for b in bufs],
        input_output_aliases={t: t for t in range(nt)}, compiler_params=_split_params())(*bufs, send, recv, *after)


def _flips():
    return [(dx, dy, dc) for dx in range(2) for dy in range(2) for dc in range(2) if (dx, dy, dc) != (0, 0, 0)]


def _flipped(x, y, c, flips):
    dx, dy, dc = flips
    return (1 - x if dx else x, 1 - y if dy else y, 1 - c if dc else c)


def _small_start(v, name, after=()):
    after, after_specs = _unread(after)

    def body(v_ref, land, *rest):
        send, recv = rest[len(after):len(after) + 2]
        x, y, c, _ = _place()
        for j, flips in enumerate(_flips()):
            pltpu.make_async_remote_copy(src_ref=v_ref, dst_ref=land.at[4 * x + 2 * y + c], send_sem=send.at[j], recv_sem=recv.at[j],
                                         device_id=_flipped(x, y, c, flips), device_id_type=MESH).start()
        rest[-1][...] = jnp.zeros(TOKEN_SHAPE.shape, F32)

    land = lax.empty((8,) + v.shape, v.dtype)
    return pl.pallas_call(
        body, name=name, in_specs=[HBM, HBM] + after_specs, out_specs=[SEM, SEM, HBM, HBM, TOKEN],
        out_shape=[pltpu.SemaphoreType.DMA((PEERS,)), pltpu.SemaphoreType.DMA((PEERS,)), pltpu.HBM(v.shape, v.dtype),
                   pltpu.HBM(land.shape, land.dtype), TOKEN_SHAPE],
        input_output_aliases={0: 2, 1: 3}, compiler_params=_split_params())(_in_hbm(v), _in_hbm(land), *after)


def _small_wait(send, recv, v, land, after, name):
    after, after_specs = _unread(after)

    def body(v_ref, land_ref, send_ref, recv_ref, *rest):
        x, y, c, _ = _place()
        for j, flips in enumerate(_flips()):
            px, py, pc = _flipped(x, y, c, flips)
            blk = land_ref.at[4 * px + 2 * py + pc]
            pltpu.make_async_remote_copy(src_ref=blk, dst_ref=blk, send_sem=send_ref.at[j], recv_sem=recv_ref.at[j],
                                         device_id=(px, py, pc), device_id_type=MESH).wait()

    return pl.pallas_call(
        body, name=name, in_specs=[HBM, HBM, SEM, SEM] + after_specs, out_specs=[HBM, HBM],
        out_shape=[pltpu.HBM(v.shape, v.dtype), pltpu.HBM(land.shape, land.dtype)],
        input_output_aliases={0: 0, 1: 1}, compiler_params=_split_params())(v, land, send, recv, *after)


def _sum_small(v, land, name):
    def body(v_ref, land_ref, o_ref):
        x, y, c, _ = _place()
        me = 4 * x + 2 * y + c
        acc = jnp.where(me == 0, v_ref[...], land_ref[0])
        for d in range(1, 8):
            acc = acc + jnp.where(me == d, v_ref[...], land_ref[d])
        o_ref[...] = acc

    vm = pl.BlockSpec(memory_space=pltpu.VMEM)
    return pl.pallas_call(body, name=name, in_specs=[vm, vm], out_specs=vm, out_shape=SDS(v.shape, F32))(v, land)


def _sum_devices(g, land, me, core, name):
    npeer, h, c = land.shape
    tr = _row_tile(h, 2 * ELEMENTWISE_ROWS)
    steps = h // tr

    def body(ix_ref, own_ref, land_ref, o_ref):
        acc = own_ref[0].astype(F32)
        for j in range(npeer):
            acc = acc + land_ref[j].astype(F32)
        o_ref[0] = acc

    grid_spec = pltpu.PrefetchScalarGridSpec(
        num_scalar_prefetch=1, grid=(steps,),
        in_specs=[pl.BlockSpec((1, tr, c), lambda i, ix: (ix[0], ix[1] * steps + i, 0)), pl.BlockSpec((npeer, tr, c), lambda i, ix: (0, i, 0))],
        out_specs=pl.BlockSpec((1, tr, c), lambda i, ix: (ix[1], i, 0)))
    return pl.pallas_call(body, name=name, grid_spec=grid_spec, out_shape=SDS((2, h, c), F32),
                          compiler_params=_cp("parallel"))(jnp.stack([me, core]), g, land)


def _pack_small(parts):
    flat = jnp.concatenate([p.reshape(-1) for p in parts])
    total = flat.shape[0]
    rows = -(-total // 1024) * 8
    return jnp.pad(flat, (0, rows * 128 - total)).reshape(rows, 128)


def _unpack_small(packed, shapes):
    flat = packed.reshape(-1)
    out, off = [], 0
    for s in shapes:
        size = int(np.prod(s))
        out.append(flat[off:off + size].reshape(s))
        off += size
    return out


def _local_step(x, mem, target, w_in_t, first_after, mid_weights, ffn_weights, on_grad, gains, conv_w, conv_b, hg_lb):
    n = x.shape[0]
    cos, sin = _rope_tables(n)
    seg = _hg_segments()
    gp, gn = _hg_pair_sums()
    masks = _hg_level_masks()
    gq2 = jnp.tile(gains["q_norm_g"], (1, 2))
    gk2 = jnp.tile(gains["k_norm_g"], (1, 2))
    a0 = hg_lb[:, 0:1, :]
    a1 = hg_lb[:, 1:2, :]

    p, h1 = _norm_mm(x, gains["pre_mix_g"], w_in_t, F32, TOKEN_TILE, 1664, "in_proj", after=(first_after,), w_turned=True)
    qr, kr = _qk_prep(p, gq2, gk2, cos, sin, "qk_prep")
    heads = lambda a: a.reshape(n, ATT_KV_HEADS, ATT_HEAD_DIM).transpose(1, 0, 2)
    kh = heads(kr)
    vh = heads(p[:, OFF_AV:OFF_AV + ATT_KV_DIM].astype(MXU_DTYPE))
    att = _attn_fwd(qr, kh, vh, "attn_fwd")
    o2, s0, hg_a, hg_e, hg_kept = _hgrn_fwd(p, a0, a1, seg, masks, "hgrn_fwd")
    rec = _hg_post(o2, p, gains["hg_out_norm_g"], "hg_post")
    cat = jnp.concatenate([att, rec], axis=1)
    w_out, w_xq, w_xkv, w_xo = mid_weights(cat)
    mixed, x1 = _mm_resid_norm(cat, w_out, x, gains["post_mix_g"], 512, "out_proj_resid")
    xq, h2 = _norm_mm(x1, gains["pre_x_g"], w_xq, MXU_DTYPE, TOKEN_TILE, 1024, "xq_proj")
    kv, mn = _norm_mm(mem, gains["mem_norm_g"], w_xkv, MXU_DTYPE, 256, 2048, "xkv_proj")
    ox = _xattn_fwd(xq, kv, "xattn_fwd")
    xo, x2 = _mm_resid_norm(ox, w_xo, x1, gains["post_x_g"], 512, "xo_proj_resid")
    w_up = ffn_weights("w_up", x2)
    u, h3 = _norm_mm(x2, gains["pre_ffn_g"], w_up, F32, TOKEN_TILE, 1408, "up_proj")
    act = _conv_gate(u, conv_w, conv_b, "conv_gate")
    w_down = ffn_weights("w_down", act)
    dn, d3, loss = _mm_resid_norm(act, w_down, x2, gains["post_ffn_g"], 512, "down_proj_resid_loss", target=target)

    gs = {}
    d_act, d_dn, gs["post_ffn_g"] = _norm_bwd_mm(dn, gains["post_ffn_g"], d3, w_down, F32, 512, 1408, "ffn_post_bwd_down_dx")
    tok = on_grad("w_down", _mm(act, d_dn, "tn", WIRE_DTYPE, 1408, 1024, "down_dw"))
    du_g, du_v, dcw_g, dcw_v, dcb_g, dcb_v = _conv_gate_bwd(u, conv_w, conv_b, d_act, "conv_gate_bwd", after=(tok,))
    gs["conv_w"] = jnp.concatenate([dcw_g, dcw_v], axis=1)
    gs["conv_b"] = jnp.concatenate([dcb_g, dcb_v], axis=1)
    ff_shard = w_up.shape[2]
    g_up = _dw_by_owner(h3, du_g, ff_shard, 0, None, 512, "up_dw_gate")
    tok = on_grad("w_up", _dw_by_owner(h3, du_v, ff_shard, 2, g_up, 512, "up_dw_value"))
    d2, gs["pre_ffn_g"] = _dx_norm_bwd([(du_g, 0), (du_g, 1), (du_v, 0), (du_v, 1)], w_up, x2, gains["pre_ffn_g"], d3, 512,
                                       "up_dx_pre_bwd", after=(tok,))
    d_ox, d_xo, gs["post_x_g"] = _norm_bwd_mm(xo, gains["post_x_g"], d2, w_xo, MXU_DTYPE, 512, 1024, "x_post_bwd_xo_dx")
    tok = on_grad("w_xo", _mm(ox, d_xo, "tn", WIRE_DTYPE, 512, 1024, "xo_dw"))
    d_xq, d_k, d_v = _xattn_bwd(xq, kv, d_ox, "xattn_bwd", after=(tok,))
    d_kv = jnp.concatenate([d_k, d_v], axis=1).astype(MXU_DTYPE)
    tok = on_grad("w_xq", _mm(h2, d_xq, "tn", WIRE_DTYPE, 512, 1024, "xq_dw"))
    tok_kv = on_grad("w_xkv", _dw_by_owner(mn, d_kv, w_xkv.shape[2], 0, None, 512, "xkv_dw"))
    d1, gs["pre_x_g"] = _dx_norm_bwd([(d_xq, 0)], w_xq[None], x1, gains["pre_x_g"], d2, 512, "xq_dx_pre_bwd", after=(tok, tok_kv))
    d_mn = _mm_nt_parts([(d_kv, s) for s in range(4)], w_xkv, F32, 256, 1024, "xkv_dx")
    _, gs["mem_norm_g"] = _norm_bwd(mem, gains["mem_norm_g"], d_mn, None, MXU_DTYPE, "mem_norm_bwd")
    d_cat, d_mixed, gs["post_mix_g"] = _norm_bwd_mm(mixed, gains["post_mix_g"], d1, w_out, MXU_DTYPE, 512, 1024, "mix_post_bwd_out_dx")
    tok = on_grad("w_out", _mm(cat, d_mixed, "tn", WIRE_DTYPE, 512, 1024, "out_dw"))
    d_o, d_hg, dg_hg = _hg_post_bwd(o2, p, gains["hg_out_norm_g"], d_cat, "hg_post_bwd", after=(tok,))
    gs["hg_out_norm_g"] = dg_hg.reshape(HG_HEADS, HG_HEAD_DIM).sum(axis=0, keepdims=True)
    dhq2, dz2, dhv2, dlb = _hgrn_bwd(p, a0, a1, masks, gp, gn, d_o, s0, hg_a, hg_e, hg_kept, "hgrn_bwd")
    lb = jax.nn.sigmoid(a0 - a1)
    da0 = dlb * lb * (1.0 - lb)
    gs["hg_lb"] = jnp.concatenate([da0, -da0], axis=1)
    d_qr, d_kh, d_vh = _attn_bwd(qr, kh, vh, cat, d_cat, "attn_bwd")
    unheads = lambda a: a.transpose(2, 0, 1).reshape(n, ATT_KV_DIM)
    d_aq, d_ak, dgq, dgk = _qk_prep_bwd(p, gq2, gk2, cos, sin, d_qr, unheads(d_kh), "qk_prep_bwd")
    gs["q_norm_g"] = dgq.reshape(ATT_HEADS, ATT_HEAD_DIM).sum(axis=0, keepdims=True)
    gs["k_norm_g"] = dgk.reshape(ATT_KV_HEADS, ATT_HEAD_DIM).sum(axis=0, keepdims=True)
    d_p = jnp.concatenate([d_aq, d_ak, unheads(d_vh).astype(MXU_DTYPE), (dhq2[0] + dhq2[1]).astype(MXU_DTYPE),
                           dz2[0].astype(MXU_DTYPE), dz2[1].astype(MXU_DTYPE), (dhv2[0] + dhv2[1]).astype(MXU_DTYPE), d_hg], axis=1)
    tok = on_grad("w_in", _mm(d_p, h1, "tn", WIRE_DTYPE, 1664, 1024, "in_dw"))
    grad_x, gs["pre_mix_g"] = _dx_norm_bwd([(d_p, 0)], w_in_t[None], x, gains["pre_mix_g"], d1, 512, "in_dx_pre_bwd", after=(tok,),
                                           b_turned=True)
    return loss, grad_x, gs


MATS = ("w_in", "w_out", "w_xq", "w_xkv", "w_xo", "w_up", "w_down")
GAINS = ("pre_mix_g", "q_norm_g", "k_norm_g", "hg_out_norm_g", "post_mix_g", "pre_x_g", "mem_norm_g", "post_x_g", "pre_ffn_g", "post_ffn_g")
WEIGHTS = ('pre_mix_g', 'w_in', 'q_norm_g', 'k_norm_g', 'hg_lb', 'hg_out_norm_g', 'w_out', 'post_mix_g', 'pre_x_g', 'mem_norm_g', 'w_xq',
           'w_xkv', 'w_xo', 'post_x_g', 'pre_ffn_g', 'w_up', 'conv_w', 'conv_b', 'w_down', 'post_ffn_g')


def kernel(x, mem, pre_mix_g, w_in, q_norm_g, k_norm_g, hg_lb, hg_out_norm_g, w_out, post_mix_g, pre_x_g, mem_norm_g, w_xq, w_xkv, w_xo, post_x_g, pre_ffn_g, w_up, conv_w, conv_b, w_down, post_ffn_g, loss_target, m_pre_mix_g, m_w_in, m_q_norm_g, m_k_norm_g, m_hg_lb, m_hg_out_norm_g, m_w_out, m_post_mix_g, m_pre_x_g, m_mem_norm_g, m_w_xq, m_w_xkv, m_w_xo, m_post_x_g, m_pre_ffn_g, m_w_up, m_conv_w, m_conv_b, m_w_down, m_post_ffn_g, v_pre_mix_g, v_w_in, v_q_norm_g, v_k_norm_g, v_hg_lb, v_hg_out_norm_g, v_w_out, v_post_mix_g, v_pre_x_g, v_mem_norm_g, v_w_xq, v_w_xkv, v_w_xo, v_post_x_g, v_pre_ffn_g, v_w_up, v_conv_w, v_conv_b, v_w_down, v_post_ffn_g):
    args = dict(locals())
    w = {k: args[k] for k in WEIGHTS}
    m = {k: args["m_" + k] for k in WEIGHTS}
    v = {k: args["v_" + k] for k in WEIGHTS}
    chip = 2 * lax.axis_index("x") + lax.axis_index("y")
    core = lax.axis_index("c")

    turned = ("w_in",)
    shards = {k: (jnp.swapaxes(w[k], 1, 2) if k in turned else w[k])[0].astype(WIRE_DTYPE) for k in MATS}

    def whole(k, g):
        return g if k in ("w_xkv", "w_up") else g.reshape(-1, g.shape[-1])

    mid_names, ffn_names = ("w_out", "w_xq", "w_xkv", "w_xo"), ("w_up", "w_down")
    small = _small_start(_pack_small([w["conv_w"][0], w["hg_lb"]]), "gather_small_start")
    w_in_pieces = _gather_pieces_start(shards["w_in"], "gather_w_in_start", after=(small[4],))
    mid = _gather_start([shards[k] for k in mid_names], "gather_mid_start", after=(w_in_pieces[4],))
    ffn = _gather_start([shards[k] for k in ffn_names], "gather_ffn_start", after=(mid[4],))
    w_in_t = whole("w_in", _pass_pieces(_gather_pieces_wait(*w_in_pieces[:4], (ffn[4],), "gather_w_in_wait"), "gather_w_in_pass"))
    mine, others = _small_wait(*small[:4], (w_in_t,), "gather_small_wait")
    small_in = lax.dynamic_update_slice_in_dim(others, mine[None], 2 * chip + core, axis=0)

    def mid_weights(after):
        return [whole(k, g) for k, g in zip(mid_names, _gather_wait(*mid[:4], after, "gather_mid_wait"))]

    def ffn_weights(k, after):
        t = ffn_names.index(k)
        return whole(k, _gather_wait(*[part[t:t + 1] for part in ffn[:4]], after, "gather_wait_" + k)[0])

    cw_parts, lb_parts = [], []
    for s in range(4):
        cw_s, lb_s = _unpack_small(small_in[2 * s], [w["conv_w"][0].shape, w["hg_lb"].shape])
        cw_parts.append(cw_s)
        lb_parts.append(lb_s)
    conv_w_full = jnp.concatenate(cw_parts, axis=1)
    hg_lb_full = jnp.concatenate(lb_parts, axis=2)

    started = {}

    def on_grad(k, g):
        if g.ndim == 2:
            g = g.reshape(4, g.shape[0] // 4, g.shape[1])
        *started[k], token = _scatter_start(g, "grad_start_" + k)
        return token

    gains = {k: w[k] for k in GAINS}
    loss_part, grad_x, gs = _local_step(x[0], mem[0], loss_target[0], w_in_t, ffn[4], mid_weights, ffn_weights, on_grad, gains,
                                        conv_w_full, w["conv_b"], hg_lb_full)
    gs["loss"] = loss_part

    grads, delta, new_m, new_v = {}, {}, {}, {}

    def sum_and_send(names, after, tag):
        sent, landed = _scatter_wait([started[k] for k in names], after, "grad_wait_" + tag)
        halves = [_sum_devices(g, land, chip, core, "grad_sum_" + k) for k, g, land in zip(names, sent, landed)]
        return _join_start(halves, "grad_join_start_" + tag)

    def joined(names, join, after, tag):
        for k, r in zip(names, _join_wait(*join[:3], after, "grad_join_wait_" + tag)):
            grads[k] = r.reshape(1, -1, r.shape[-1])

    def adamw(names):
        for k in names:
            shape = w[k].shape
            keep = len(shape) == 3 and shape[0] == 1
            if k in turned:
                view, back = (lambda a: jnp.swapaxes(a, 1, 2)), (lambda a: jnp.swapaxes(a, 1, 2))
                g = grads[k]
            else:
                view = (lambda a: a.reshape(shape)) if keep else (lambda a: a.reshape(-1, shape[-1]))
                back = lambda a: a.reshape(shape)
                g = view(grads[k])
            d, mo, vo, go = _adamw(view(w[k]), g, view(m[k]), view(v[k]), "adamw_" + k)
            delta[k], new_m[k], new_v[k], grads[k] = back(d), back(mo), back(vo), back(go)

    small_names = GAINS + ("conv_b", "conv_w", "hg_lb")
    packed = _pack_small([gs[k] for k in small_names + ("loss",)])
    small = _small_start(packed, "reduce_small_start", after=(grad_x,))

    ffn_join = sum_and_send(ffn_names, (grad_x, small[4]), "ffn")
    mid_join = sum_and_send(mid_names, (ffn_join[3],), "mid")
    joined(ffn_names, ffn_join, (mid_join[3],), "ffn")
    adamw(ffn_names)
    joined(mid_names, mid_join, tuple(new_v[k] for k in ffn_names), "mid")
    adamw(mid_names)
    early = mid_names + ffn_names
    w_in_join = sum_and_send(("w_in",), tuple(new_v[k] for k in early), "w_in")

    mine, others = _small_wait(*small[:4], (w_in_join[3],), "reduce_small_wait")
    reduced_small = _sum_small(mine, others, "reduce_small_sum")
    *summed, loss = _unpack_small(reduced_small, [gs[k].shape for k in small_names + ("loss",)])
    loss = loss[0, 0]
    for k, g in zip(small_names, summed):
        grads[k] = g
    ncw = w["conv_w"].shape[2]
    grads["conv_w"] = lax.dynamic_slice_in_dim(grads["conv_w"], chip * ncw, ncw, axis=1)[None]
    nlb = w["hg_lb"].shape[2]
    grads["hg_lb"] = lax.dynamic_slice_in_dim(grads["hg_lb"], chip * nlb, nlb, axis=2)
    replicated = GAINS + ("conv_b",)
    shapes = [w[k].shape for k in replicated]
    rows = sum(int(np.prod(s)) for s in shapes) // 128
    pack = lambda d: jnp.concatenate([d[k].reshape(-1) for k in replicated]).reshape(rows, 128)
    outs = _adamw(pack(w), reduced_small[:rows], pack(m), pack(v), "adamw_replicated")
    for into, packed_out in zip((delta, new_m, new_v, grads), outs):
        for k, a in zip(replicated, _unpack_small(packed_out, shapes)):
            into[k] = a
    adamw(("conv_w", "hg_lb"))

    joined(("w_in",), w_in_join, tuple(new_v[k] for k in small_names), "w_in")
    adamw(("w_in",))
    return (loss, grad_x[None], *[grads[k] for k in WEIGHTS], *[delta[k] for k in WEIGHTS],
            *[new_m[k] for k in WEIGHTS], *[new_v[k] for k in WEIGHTS])
```

```python
import numpy as np
import jax
import jax.numpy as jnp
from jax import lax
from jax.experimental import pallas as pl
from jax.experimental.pallas import tpu as pltpu

F32 = jnp.float32
MXU_DTYPE = jnp.bfloat16
WIRE_DTYPE = jnp.bfloat16
VMEM_LIMIT_BYTES = 56 * 1024 * 1024
ROWS_PER_16BIT_TILE = 16
ELEMENTWISE_ROWS = 256
EPS = 1e-6
MESH = pl.DeviceIdType.MESH

GRID_W = 64
ATT_HEADS, ATT_KV_HEADS, ATT_HEAD_DIM = 8, 2, 64
ATT_GROUP = ATT_HEADS // ATT_KV_HEADS
ATT_Q_DIM, ATT_KV_DIM = 512, 128
ROPE_THETA = 10000.0
HG_HEADS, HG_HEAD_DIM, HG_DIM = 4, 128, 512
HG_CHUNK = 128
HG_LEVELS = 7
HG_PAIR = 2 * HG_HEAD_DIM
HG_KEPT = 7
X_HEADS, X_HEAD_DIM = 4, 256
D_FF = 2816
FF_COLS = 256
FF_BLOCKS = D_FF // FF_COLS
OFF_AK, OFF_AV, OFF_HQ, OFF_ZF, OFF_ZB, OFF_HI, OFF_HG = 512, 640, 768, 1280, 1792, 2304, 2816

ADAM_LR, ADAM_B1, ADAM_B2, ADAM_EPS, ADAM_WD, ADAM_STEP = 0.001, 0.9, 0.999, 1e-08, 0.01, 10

SDS = jax.ShapeDtypeStruct


def _cp(*sem):
    return pltpu.CompilerParams(dimension_semantics=sem, vmem_limit_bytes=VMEM_LIMIT_BYTES)


def _row_tile(rows, cap):
    if rows <= cap:
        return rows
    return max(t for t in range(ROWS_PER_16BIT_TILE, cap + 1, ROWS_PER_16BIT_TILE) if rows % t == 0)


def _dot(a, b, form="nn"):
    dims = {"nn": (((1,), (0,)), ((), ())), "nt": (((1,), (1,)), ((), ())), "tn": (((0,), (0,)), ((), ()))}[form]
    return lax.dot_general(a.astype(MXU_DTYPE), b.astype(MXU_DTYPE), dims, preferred_element_type=F32)


def _sigmoid(x):
    return 1.0 / (1.0 + jnp.exp(-x))


def _rstd(x):
    return lax.rsqrt(jnp.mean(x * x, axis=-1, keepdims=True) + EPS)


def _rms_bwd(x, g, dy):
    r = _rstd(x)
    xh = x * r
    dn = dy * g
    dx = r * (dn - xh * jnp.mean(dn * xh, axis=-1, keepdims=True))
    return dx, jnp.sum(dy * xh, axis=0, keepdims=True)


def _unread(after):
    after = tuple(a for a in after if a is not None)
    return after, [pl.BlockSpec(memory_space=pl.ANY)] * len(after)


def _mm(a, b, form, out_dtype, tm, tn, name, after=()):
    after, after_specs = _unread(after)
    if form == "nn":
        (m, k), n = a.shape, b.shape[1]
    elif form == "nt":
        (m, k), n = a.shape, b.shape[0]
    else:
        (k, m), n = a.shape, b.shape[1]
    tm, tn = min(tm, m), min(tn, n)
    assert m % tm == 0 and n % tn == 0, (name, m, n, tm, tn)

    def body(a_ref, b_ref, *rest):
        o_ref = rest[-1]
        o_ref[...] = _dot(a_ref[...], b_ref[...], form).astype(o_ref.dtype)

    a_spec = pl.BlockSpec((k, tm), lambda i, j: (0, i)) if form == "tn" else pl.BlockSpec((tm, k), lambda i, j: (i, 0))
    b_spec = pl.BlockSpec((tn, k), lambda i, j: (j, 0)) if form == "nt" else pl.BlockSpec((k, tn), lambda i, j: (0, j))
    return pl.pallas_call(
        body, name=name, grid=(m // tm, n // tn), in_specs=[a_spec, b_spec] + after_specs,
        out_specs=pl.BlockSpec((tm, tn), lambda i, j: (i, j)), out_shape=SDS((m, n), out_dtype),
        compiler_params=_cp("parallel", "parallel"))(a, b, *after)


def _mm_nt_parts(a_parts, b, out_dtype, tm, tn, name, after=()):
    after, after_specs = _unread(after)
    parts, n, p = b.shape
    m = a_parts[0][0].shape[0]
    tm, tn = min(tm, m), min(tn, n)
    assert m % tm == 0 and n % tn == 0 and len(a_parts) == parts, (name, m, b.shape)

    def body(*refs):
        o_ref = refs[-1]
        acc = _dot(refs[0][...], refs[parts][0], "nt")
        for s in range(1, parts):
            acc = acc + _dot(refs[s][...], refs[parts + s][0], "nt")
        o_ref[...] = acc.astype(o_ref.dtype)

    a_specs = [pl.BlockSpec((tm, p), lambda i, j, cb=cb: (i, cb)) for _, cb in a_parts]
    b_specs = [pl.BlockSpec((1, tn, p), lambda i, j, s=s: (s, j, 0)) for s in range(parts)]
    return pl.pallas_call(
        body, name=name, grid=(m // tm, n // tn), in_specs=a_specs + b_specs + after_specs,
        out_specs=pl.BlockSpec((tm, tn), lambda i, j: (i, j)), out_shape=SDS((m, n), out_dtype),
        compiler_params=_cp("parallel", "parallel"))(*[arr for arr, _ in a_parts], *([b] * parts), *after)


def _norm_bwd_mm(y, g, d, w, out_dtype, tm, tn, name):
    n, dm = y.shape
    nn = w.shape[0]
    tm, tn = min(tm, n), min(tn, nn)
    assert n % tm == 0 and nn % tn == 0 and w.shape[1] == dm, (name, y.shape, w.shape)

    def body(y_ref, g_ref, d_ref, w_ref, dx_ref, dy_ref, dg_ref, dys):
        i, j = pl.program_id(0), pl.program_id(1)

        @pl.when(jnp.logical_and(i == 0, j == 0))
        def _():
            dg_ref[...] = jnp.zeros_like(dg_ref)

        @pl.when(j == 0)
        def _():
            dy, dg = _rms_bwd(y_ref[...], g_ref[...], d_ref[...])
            dy = dy.astype(MXU_DTYPE)
            dys[...] = dy
            dy_ref[...] = dy
            dg_ref[...] += dg

        dx_ref[...] = _dot(dys[...], w_ref[...], "nt").astype(dx_ref.dtype)

    row = pl.BlockSpec((tm, dm), lambda i, j: (i, 0))
    vec = pl.BlockSpec((1, dm), lambda i, j: (0, 0))
    return pl.pallas_call(
        body, name=name, grid=(n // tm, nn // tn), in_specs=[row, vec, row, pl.BlockSpec((tn, dm), lambda i, j: (j, 0))],
        out_specs=[pl.BlockSpec((tm, tn), lambda i, j: (i, j)), row, vec],
        out_shape=[SDS((n, nn), out_dtype), SDS((n, dm), MXU_DTYPE), SDS((1, dm), F32)],
        scratch_shapes=[pltpu.VMEM((tm, dm), MXU_DTYPE)],
        compiler_params=_cp("arbitrary", "arbitrary"))(y, g, d, w)


def _mm_resid_norm(a, b, x, g, tm, name, target=None):
    n, k = a.shape
    d = b.shape[1]
    tm = min(tm, n)
    assert n % tm == 0 and x.shape == (n, d), (name, a.shape, b.shape)
    with_loss = target is not None

    def body(a_ref, b_ref, x_ref, g_ref, *rest):
        y = _dot(a_ref[...], b_ref[...])
        out = x_ref[...] + y * _rstd(y) * g_ref[...]
        if not with_loss:
            y_ref, o_ref = rest
            y_ref[...] = y
            o_ref[...] = out
            return
        t_ref, y_ref, d_ref, l_ref = rest
        y_ref[...] = y
        diff = out - t_ref[...]
        d_ref[...] = diff * (1.0 / d)

        @pl.when(pl.program_id(0) == 0)
        def _():
            l_ref[...] = jnp.zeros_like(l_ref)

        l_ref[...] += 0.5 * jnp.sum(jnp.mean(diff * diff, axis=-1, keepdims=True), axis=0, keepdims=True)

    row = pl.BlockSpec((tm, d), lambda i: (i, 0))
    ins = [pl.BlockSpec((tm, k), lambda i: (i, 0)), pl.BlockSpec((k, d), lambda i: (0, 0)), row, pl.BlockSpec((1, d), lambda i: (0, 0))]
    out = SDS((n, d), F32)
    if with_loss:
        return pl.pallas_call(body, name=name, grid=(n // tm,), in_specs=ins + [row], out_specs=[row, row, pl.BlockSpec((1, 1), lambda i: (0, 0))],
                              out_shape=[out, out, SDS((1, 1), F32)], compiler_params=_cp("arbitrary"))(a, b, x, g, target)
    return pl.pallas_call(body, name=name, grid=(n // tm,), in_specs=ins, out_specs=[row, row], out_shape=[out, out],
                          compiler_params=_cp("parallel"))(a, b, x, g)


def _dx_norm_bwd(a_parts, b, x, g, res, tm, name, after=(), b_turned=False):
    after, after_specs = _unread(after)
    parts, d, p = (b.shape[0], b.shape[2], b.shape[1]) if b_turned else b.shape
    form = "nn" if b_turned else "nt"
    n = x.shape[0]
    tm = min(tm, n)
    assert n % tm == 0 and len(a_parts) == parts and x.shape[1] == d, (name, x.shape, b.shape)

    def body(*refs):
        x_ref, g_ref, res_ref = refs[2 * parts:2 * parts + 3]
        dx_ref, dg_ref = refs[-2:]
        dh = _dot(refs[0][...], refs[parts][0], form)
        for s in range(1, parts):
            dh = dh + _dot(refs[s][...], refs[parts + s][0], form)
        dx, dg = _rms_bwd(x_ref[...], g_ref[...], dh)
        dx_ref[...] = dx + res_ref[...]

        @pl.when(pl.program_id(0) == 0)
        def _():
            dg_ref[...] = jnp.zeros_like(dg_ref)

        dg_ref[...] += dg

    a_specs = [pl.BlockSpec((tm, p), lambda i, cb=cb: (i, cb)) for _, cb in a_parts]
    b_specs = [pl.BlockSpec((1,) + b.shape[1:], lambda i, s=s: (s, 0, 0)) for s in range(parts)]
    row = pl.BlockSpec((tm, d), lambda i: (i, 0))
    vec = pl.BlockSpec((1, d), lambda i: (0, 0))
    return pl.pallas_call(
        body, name=name, grid=(n // tm,), in_specs=a_specs + b_specs + [row, vec, row] + after_specs,
        out_specs=[row, vec], out_shape=[SDS((n, d), F32), SDS((1, d), F32)],
        compiler_params=_cp("arbitrary"))(*[arr for arr, _ in a_parts], *([b] * parts), x, g, res, *after)


def _dw_by_owner(a, b, tn, first, into, tm, name):
    k, m = a.shape
    cnt = b.shape[1] // tn
    tm = min(tm, m)
    assert m % tm == 0 and b.shape[1] == cnt * tn and first + cnt <= 4, (name, a.shape, b.shape)

    def body(a_ref, b_ref, *rest):
        rest[-1][0] = _dot(a_ref[...], b_ref[...], "tn").astype(rest[-1].dtype)

    extra = [] if into is None else [into]
    return pl.pallas_call(
        body, name=name, grid=(m // tm, cnt),
        in_specs=[pl.BlockSpec((k, tm), lambda i, j: (0, i)), pl.BlockSpec((k, tn), lambda i, j: (0, j))] + [pl.BlockSpec(memory_space=pl.ANY)] * len(extra),
        out_specs=pl.BlockSpec((1, tm, tn), lambda i, j: (first + j, i, 0)), out_shape=SDS((4, m, tn), WIRE_DTYPE),
        input_output_aliases={2: 0} if extra else {},
        compiler_params=_cp("parallel", "parallel"))(a, b, *extra)


def _norm_mm(x, g, w, out_dtype, tm, tn, name, after=(), w_turned=False):
    after, after_specs = _unread(after)
    m, d = x.shape
    sharded = w.ndim == 3
    n = w.shape[0] if w_turned else w.shape[-1] * (w.shape[0] if sharded else 1)
    tm, tn = min(tm, m), (w.shape[-1] if sharded else min(tn, n))
    assert m % tm == 0 and n % tn == 0 and not (sharded and w_turned), (name, m, n, tm, tn)

    def body(x_ref, g_ref, w_ref, *rest):
        o_ref, h_ref, hs = rest[-3:]

        @pl.when(pl.program_id(1) == 0)
        def _():
            xv = x_ref[...]
            h = (xv * _rstd(xv) * g_ref[...]).astype(MXU_DTYPE)
            hs[...] = h
            h_ref[...] = h

        o_ref[...] = _dot(hs[...], w_ref[0] if sharded else w_ref[...], "nt" if w_turned else "nn").astype(o_ref.dtype)

    if w_turned:
        w_spec = pl.BlockSpec((tn, d), lambda i, j: (j, 0))
    else:
        w_spec = pl.BlockSpec((1, d, tn), lambda i, j: (j, 0, 0)) if sharded else pl.BlockSpec((d, tn), lambda i, j: (0, j))
    return pl.pallas_call(
        body, name=name, grid=(m // tm, n // tn),
        in_specs=[pl.BlockSpec((tm, d), lambda i, j: (i, 0)), pl.BlockSpec((1, d), lambda i, j: (0, 0)), w_spec] + after_specs,
        out_specs=[pl.BlockSpec((tm, tn), lambda i, j: (i, j)), pl.BlockSpec((tm, d), lambda i, j: (i, 0))],
        out_shape=[SDS((m, n), out_dtype), SDS((m, d), MXU_DTYPE)],
        scratch_shapes=[pltpu.VMEM((tm, d), MXU_DTYPE)],
        compiler_params=_cp("parallel", "arbitrary"))(x, g, w, *after)


ROW_TILE = 512
TOKEN_TILE = 1024


def _norm_bwd(x, g, dy, res, out_dtype, name):
    n, d = x.shape
    tr = min(ROW_TILE, n)
    has_res = res is not None

    def body(*refs):
        x_ref, g_ref, dy_ref = refs[:3]
        dx_ref, dg_ref = refs[-2:]
        dx, dg = _rms_bwd(x_ref[...], g_ref[...], dy_ref[...].astype(F32))
        if has_res:
            dx = dx + refs[3][...]
        dx_ref[...] = dx.astype(dx_ref.dtype)

        @pl.when(pl.program_id(0) == 0)
        def _():
            dg_ref[...] = jnp.zeros_like(dg_ref)

        dg_ref[...] += dg

    row = pl.BlockSpec((tr, d), lambda i: (i, 0))
    vec = pl.BlockSpec((1, d), lambda i: (0, 0))
    ins = [x, g, dy] + ([res] if has_res else [])
    return pl.pallas_call(
        body, name=name, grid=(n // tr,), in_specs=[row, vec, row] + ([row] if has_res else []),
        out_specs=[row, vec], out_shape=[SDS((n, d), out_dtype), SDS((1, d), F32)],
        compiler_params=_cp("arbitrary"))(*ins)


def _rope_tables(n):
    pairs = ATT_HEAD_DIM // 4
    t = np.arange(n)
    inv = np.power(ROPE_THETA, -np.arange(pairs, dtype=np.float32) / pairs).astype(np.float32)
    ang = np.concatenate([(t // GRID_W)[:, None].astype(np.float32) * inv, (t % GRID_W)[:, None].astype(np.float32) * inv], axis=-1)
    cos = np.repeat(np.cos(ang), 2, axis=-1)
    sin = np.repeat(np.sin(ang), 2, axis=-1) * np.tile(np.array([-1.0, 1.0], np.float32), ATT_HEAD_DIM // 2)
    return jnp.asarray(np.tile(cos, 2), F32), jnp.asarray(np.tile(sin, 2), F32)


def _swap_pairs(x):
    lane = lax.broadcasted_iota(jnp.int32, x.shape, 1)
    return jnp.where((lane & 1) == 0, pltpu.roll(x, 127, axis=1), pltpu.roll(x, 1, axis=1))


def _head_mean(v):
    lane = lax.broadcasted_iota(jnp.int32, v.shape, 1)
    lo = jnp.where(lane < ATT_HEAD_DIM, v, 0.0)
    s0 = jnp.sum(lo, axis=-1, keepdims=True)
    s1 = jnp.sum(v - lo, axis=-1, keepdims=True)
    return jnp.where(lane < ATT_HEAD_DIM, s0, s1) * (1.0 / ATT_HEAD_DIM)


def _qk_prep(p, gq, gk, cos, sin, name):
    n = p.shape[0]
    tr = min(ROW_TILE, n)

    def one(xv, g, c, s):
        xn = xv * lax.rsqrt(_head_mean(xv * xv) + EPS) * g
        return xn * c + _swap_pairs(xn) * s

    def body(q_ref, k_ref, gq_ref, gk_ref, c_ref, s_ref, qo_ref, ko_ref):
        c, s = c_ref[...], s_ref[...]
        for j in range(ATT_Q_DIM // 128):
            qo_ref[:, j * 128:(j + 1) * 128] = one(q_ref[:, j * 128:(j + 1) * 128], gq_ref[...], c, s).astype(qo_ref.dtype)
        ko_ref[...] = one(k_ref[...], gk_ref[...], c, s).astype(ko_ref.dtype)

    vec = pl.BlockSpec((1, 128), lambda i: (0, 0))
    tab = pl.BlockSpec((tr, 128), lambda i: (i, 0))
    return pl.pallas_call(
        body, name=name, grid=(n // tr,),
        in_specs=[pl.BlockSpec((tr, ATT_Q_DIM), lambda i: (i, 0)), pl.BlockSpec((tr, 128), lambda i: (i, OFF_AK // 128)), vec, vec, tab, tab],
        out_specs=[pl.BlockSpec((tr, ATT_Q_DIM), lambda i: (i, 0)), tab],
        out_shape=[SDS((n, ATT_Q_DIM), MXU_DTYPE), SDS((n, ATT_KV_DIM), MXU_DTYPE)],
        compiler_params=_cp("parallel"))(p, p, gq, gk, cos, sin)


def _qk_prep_bwd(p, gq, gk, cos, sin, dq, dk, name):
    n = p.shape[0]
    tr = min(ROW_TILE, n)

    def one(xv, g, c, s, dout):
        dxn = dout * c + _swap_pairs(dout * s)
        r = lax.rsqrt(_head_mean(xv * xv) + EPS)
        xh = xv * r
        dn = dxn * g
        dx = r * (dn - xh * _head_mean(dn * xh))
        return dx, jnp.sum(dxn * xh, axis=0, keepdims=True)

    def body(q_ref, k_ref, gq_ref, gk_ref, c_ref, s_ref, dq_ref, dk_ref, dqo_ref, dko_ref, dgq_ref, dgk_ref):
        @pl.when(pl.program_id(0) == 0)
        def _():
            dgq_ref[...] = jnp.zeros_like(dgq_ref)
            dgk_ref[...] = jnp.zeros_like(dgk_ref)

        c, s = c_ref[...], s_ref[...]
        for j in range(ATT_Q_DIM // 128):
            sl = slice(j * 128, (j + 1) * 128)
            dx, dg = one(q_ref[:, sl], gq_ref[...], c, s, dq_ref[:, sl])
            dqo_ref[:, sl] = dx.astype(dqo_ref.dtype)
            dgq_ref[:, sl] += dg
        dx, dg = one(k_ref[...], gk_ref[...], c, s, dk_ref[...])
        dko_ref[...] = dx.astype(dko_ref.dtype)
        dgk_ref[...] += dg

    vec = pl.BlockSpec((1, 128), lambda i: (0, 0))
    tab = pl.BlockSpec((tr, 128), lambda i: (i, 0))
    qrow = pl.BlockSpec((tr, ATT_Q_DIM), lambda i: (i, 0))
    return pl.pallas_call(
        body, name=name, grid=(n // tr,),
        in_specs=[qrow, pl.BlockSpec((tr, 128), lambda i: (i, OFF_AK // 128)), vec, vec, tab, tab, qrow, tab],
        out_specs=[qrow, tab, pl.BlockSpec((1, ATT_Q_DIM), lambda i: (0, 0)), vec],
        out_shape=[SDS((n, ATT_Q_DIM), MXU_DTYPE), SDS((n, ATT_KV_DIM), MXU_DTYPE), SDS((1, ATT_Q_DIM), F32), SDS((1, 128), F32)],
        compiler_params=_cp("arbitrary"))(p, p, gq, gk, cos, sin, dq, dk)


ATT_FWD_STEP = (256, 4)
ATT_BWD_STEP = (512, 2)


def _attn_fwd(q, k, v, name):
    n = q.shape[0]
    tq, step_heads = min(ATT_FWD_STEP[0], n), ATT_FWD_STEP[1]
    scale = ATT_HEAD_DIM ** -0.5
    gw = step_heads * ATT_HEAD_DIM
    parts = ATT_GROUP // step_heads

    def body(q_ref, k_ref, v_ref, o_ref):
        kk, vv = k_ref[0], v_ref[0]
        v_ones = jnp.concatenate([vv, jnp.ones_like(vv)], axis=1)
        outs = []
        for g in range(step_heads):
            s = _dot(q_ref[:, g * ATT_HEAD_DIM:(g + 1) * ATT_HEAD_DIM] * scale, kk, "nt")
            e = jnp.exp(s - jnp.max(s, axis=-1, keepdims=True))
            ov = _dot(e, v_ones)
            outs.append(ov[:, :ATT_HEAD_DIM] / ov[:, ATT_HEAD_DIM:])
        o_ref[...] = jnp.concatenate(outs, axis=-1).astype(o_ref.dtype)

    kv = pl.BlockSpec((1, n, ATT_HEAD_DIM), lambda h, i, pr: (h, 0, 0))
    qb = pl.BlockSpec((tq, gw), lambda h, i, pr: (i, h * parts + pr))
    return pl.pallas_call(
        body, name=name, grid=(ATT_KV_HEADS, n // tq, parts), in_specs=[qb, kv, kv],
        out_specs=qb, out_shape=SDS((n, ATT_Q_DIM), MXU_DTYPE),
        compiler_params=_cp("parallel", "parallel", "parallel"))(q, k, v)


def _attn_bwd(q, k, v, o, do, name):
    n = q.shape[0]
    tq, step_heads = min(ATT_BWD_STEP[0], n), ATT_BWD_STEP[1]
    scale = ATT_HEAD_DIM ** -0.5
    gw = step_heads * ATT_HEAD_DIM
    parts = ATT_GROUP // step_heads

    def body(q_ref, k_ref, v_ref, o_ref, do_ref, dq_ref, dk_ref, dv_ref):
        @pl.when(jnp.logical_and(pl.program_id(1) == 0, pl.program_id(2) == 0))
        def _():
            dk_ref[...] = jnp.zeros_like(dk_ref)
            dv_ref[...] = jnp.zeros_like(dv_ref)

        kk, vv = k_ref[0], v_ref[0]
        dqs = []
        dk_acc = jnp.zeros((ATT_HEAD_DIM, n), F32)
        dv_acc = jnp.zeros((ATT_HEAD_DIM, n), F32)
        for g in range(step_heads):
            sl = slice(g * ATT_HEAD_DIM, (g + 1) * ATT_HEAD_DIM)
            qg, dog = q_ref[:, sl] * scale, do_ref[:, sl].astype(F32)
            s = _dot(qg, kk, "nt")
            e = jnp.exp(s - jnp.max(s, axis=-1, keepdims=True))
            inv = 1.0 / jnp.sum(e, axis=-1, keepdims=True)
            delta = jnp.sum(dog * o_ref[:, sl].astype(F32), axis=-1, keepdims=True)
            dse = e * (_dot(dog, vv, "nt") - delta)
            dqs.append(_dot(dse, kk) * (inv * scale))
            dk_acc += _dot(qg.astype(F32) * inv, dse, "tn")
            dv_acc += _dot(dog * inv, e, "tn")
        dq_ref[...] = jnp.concatenate(dqs, axis=-1)
        dk_ref[0] += dk_acc
        dv_ref[0] += dv_acc

    kv = pl.BlockSpec((1, n, ATT_HEAD_DIM), lambda h, i, pr: (h, 0, 0))
    kvt = pl.BlockSpec((1, ATT_HEAD_DIM, n), lambda h, i, pr: (h, 0, 0))
    qb = pl.BlockSpec((tq, gw), lambda h, i, pr: (i, h * parts + pr))
    return pl.pallas_call(
        body, name=name, grid=(ATT_KV_HEADS, n // tq, parts), in_specs=[qb, kv, kv, qb, qb], out_specs=[qb, kvt, kvt],
        out_shape=[SDS((n, ATT_Q_DIM), F32), SDS((ATT_KV_HEADS, ATT_HEAD_DIM, n), F32), SDS((ATT_KV_HEADS, ATT_HEAD_DIM, n), F32)],
        compiler_params=_cp("parallel", "arbitrary", "arbitrary"))(q, k, v, o, do)


def _both_directions(mats, axis):
    fwd = np.concatenate(mats, axis=axis).astype(np.float32)
    bwd = np.concatenate([m[::-1, ::-1] for m in mats], axis=axis).astype(np.float32)
    return jnp.asarray(np.stack([fwd, bwd]), MXU_DTYPE)


def _hg_segments():
    c = HG_CHUNK
    t = np.arange(c)[:, None]
    r = np.arange(c)[None, :]
    mats = [(r <= t)]
    for lev in range(HG_LEVELS):
        h = c >> (lev + 1)
        mid = (t // (2 * h)) * (2 * h) + h - 1
        hi = (t // h) % 2 == 1
        mats.append(np.where(hi, (r > mid) & (r <= t), (r > t) & (r <= mid)))
    mats.append(r > t)
    return _both_directions(mats, 0)


def _hg_pair_sums():
    c = HG_CHUNK
    r = np.arange(c)[:, None]
    t = np.arange(c)[None, :]
    gp, gn = [t >= r], [t < r]
    for lev in range(HG_LEVELS):
        sh = HG_LEVELS - 1 - lev
        same = (r >> sh) == (t >> sh)
        gp.append(same & (t >= r))
        gn.append(same & (t < r))
    return _both_directions(gp, 1), _both_directions(gn, 1)


def _split_dot(mat, x):
    hi = x.astype(MXU_DTYPE)
    lo = (x - hi.astype(F32)).astype(MXU_DTYPE)
    return _dot(mat, hi) + _dot(mat, lo)


def _hg_gates(hq, z, a0, a1):
    q = hq * _sigmoid(hq)
    sg = _sigmoid(z)
    lb = _sigmoid(a0 - a1)
    f = lb + (1.0 - lb) * sg
    k = (1.0 - lb) * (1.0 - sg)
    return q, f, k, sg, lb


def _hg_level_masks():
    c = HG_CHUNK
    t = np.arange(c)
    later, same = [], []
    for lev in range(HG_LEVELS):
        sh = HG_LEVELS - 1 - lev
        later.append(np.broadcast_to((((t >> sh) & 1) == 1)[:, None], (c, HG_HEAD_DIM)))
        same.append((t[:, None] >> (sh + 1)) == (t[None, :] >> (sh + 1)))
    same.append(t[:, None] == t[None, :])
    later = np.stack(later).astype(np.float32)
    return jnp.asarray(np.stack([later, 1.0 - later]), F32), jnp.asarray(np.stack(same).astype(np.float32), F32)


def _hg_level(q, k, ex, later_ref, lev):
    e = ex[lev + 1]
    e_q = e * later_ref[0, lev]
    e_k = e - e_q
    return q * e_q, k * e_k, e_q, e_k


def _hg_intra(q, k, ex, later_ref, same_ref):
    a = same_ref[HG_LEVELS] * jnp.sum(q * k, axis=-1, keepdims=True)
    for lev in range(HG_LEVELS):
        qs, ks, _, _ = _hg_level(q, k, ex, later_ref, lev)
        a = a + same_ref[lev] * _dot(qs, ks, "nt")
    return a


def _hg_specs(n, with_time):
    c = HG_CHUNK
    nc = n // c

    def chunk(d, i):
        first = d if with_time else 1 - d
        return i + first * (nc - 1 - 2 * i)

    def pcols(off, dir_stride=0):
        return [pl.BlockSpec((c, HG_PAIR), lambda d, i, j=j: (chunk(d, i), off // HG_PAIR + dir_stride // HG_PAIR * d + j)) for j in range(2)]

    specs = dict(
        hq=pcols(OFF_HQ), v=pcols(OFF_HI), z=pcols(OFF_ZF, OFF_ZB - OFF_ZF),
        shared=pl.BlockSpec((c, HG_DIM), lambda d, i: (chunk(d, i), 0)),
        per_dir=pl.BlockSpec((1, c, HG_DIM), lambda d, i: (d, chunk(d, i), 0)),
        vec=pl.BlockSpec((1, 1, HG_DIM), lambda d, i: (d, 0, 0)),
        seg=pl.BlockSpec((1, (HG_LEVELS + 2) * c, c), lambda d, i: (d, 0, 0)),
        sums=pl.BlockSpec((1, c, (HG_LEVELS + 1) * c), lambda d, i: (d, 0, 0)),
        later=pl.BlockSpec((1, HG_LEVELS, c, HG_HEAD_DIM), lambda d, i: (d, 0, 0, 0)),
        same=pl.BlockSpec((HG_LEVELS + 1, c, c), lambda d, i: (0, 0, 0)),
        state=pl.BlockSpec((1, HG_HEADS, 1, HG_HEAD_DIM, HG_HEAD_DIM), lambda d, i: (d, 0, chunk(d, i), 0, 0)),
        weights=pl.BlockSpec((1, HG_HEADS, 1, c, c), lambda d, i: (d, 0, chunk(d, i), 0, 0)),
        levels=pl.BlockSpec((1, HG_HEADS, 1, HG_LEVELS, c, HG_HEAD_DIM), lambda d, i: (d, 0, chunk(d, i), 0, 0, 0)),
        kept=pl.BlockSpec((1, HG_KEPT, c, HG_DIM), lambda d, i: (d, 0, chunk(d, i), 0)))
    return nc, specs


def _hg_head(refs, hh):
    off = (hh % 2) * HG_HEAD_DIM
    return refs[hh // 2][:, off:off + HG_HEAD_DIM]


def _hg_lanes(hh):
    return slice(hh * HG_HEAD_DIM, (hh + 1) * HG_HEAD_DIM)


def _hg_exps(seg_ref, f):
    c = HG_CHUNK
    args = _split_dot(seg_ref[0], jnp.log(f))
    return [jnp.exp(args[j * c:(j + 1) * c]) for j in range(HG_LEVELS + 2)]


def _hg_last_row(a, mirrored):
    return jnp.where(mirrored, a[0:1, :], a[HG_CHUNK - 1:HG_CHUNK, :])


def _hgrn_fwd(p, a0, a1, seg, masks, name):
    n = p.shape[0]
    nc, sp = _hg_specs(n, True)

    def body(hq0, hq1, z0, z1, v0, v1, a0_ref, a1_ref, seg_ref, later_ref, same_ref, o_ref, s0_ref, a_ref, e_ref, g_ref, st):
        @pl.when(pl.program_id(1) == 0)
        def _():
            st[...] = jnp.zeros_like(st)

        mirrored = pl.program_id(0) == 1
        for hh in range(HG_HEADS):
            ln = _hg_lanes(hh)
            hqv = _hg_head((hq0, hq1), hh)
            q, f, k, sg, _ = _hg_gates(hqv, _hg_head((z0, z1), hh), a0_ref[0, :, ln], a1_ref[0, :, ln])
            vv = _hg_head((v0, v1), hh)
            ex = _hg_exps(seg_ref, f)
            for lev in range(HG_LEVELS):
                e_ref[0, hh, 0, lev] = ex[lev + 1].astype(e_ref.dtype)
            sq = _sigmoid(hqv)
            for j, kept in enumerate((q, k, f, sg, sq * (1.0 + hqv * (1.0 - sq)), ex[0], ex[HG_LEVELS + 1])):
                g_ref[0, j, :, ln] = kept
            a = _hg_intra(q, k, ex, later_ref, same_ref).astype(MXU_DTYPE)
            a_ref[0, hh, 0] = a
            s_t = st[hh]
            s0_ref[0, hh, 0] = s_t
            o_ref[0, :, ln] = _dot(a, vv) + _dot(q * ex[0], s_t, "nt")
            st[hh] = s_t * _hg_last_row(ex[0], mirrored) + _dot(vv, k * ex[HG_LEVELS + 1], "tn")

    return pl.pallas_call(
        body, name=name, grid=(2, nc), in_specs=sp["hq"] + sp["z"] + sp["v"] + [sp["vec"], sp["vec"], sp["seg"], sp["later"], sp["same"]],
        out_specs=[sp["per_dir"], sp["state"], sp["weights"], sp["levels"], sp["kept"]],
        out_shape=[SDS((2, n, HG_DIM), F32), SDS((2, HG_HEADS, nc, HG_HEAD_DIM, HG_HEAD_DIM), F32),
                   SDS((2, HG_HEADS, nc, HG_CHUNK, HG_CHUNK), MXU_DTYPE),
                   SDS((2, HG_HEADS, nc, HG_LEVELS, HG_CHUNK, HG_HEAD_DIM), MXU_DTYPE), SDS((2, HG_KEPT, n, HG_DIM), F32)],
        scratch_shapes=[pltpu.VMEM((HG_HEADS, HG_HEAD_DIM, HG_HEAD_DIM), F32)],
        compiler_params=_cp("parallel", "arbitrary"))(p, p, p, p, p, p, a0, a1, seg, *masks)


def _hgrn_bwd(p, a0, a1, masks, gp, gn, do, s0, a, e, kept, name):
    n = p.shape[0]
    nc, sp = _hg_specs(n, False)


    def body(v0, v1, a0_ref, a1_ref, later_ref, same_ref, gp_ref, gn_ref, do_ref, s0_ref, a_ref, e_ref, g_ref,
             dhq_ref, dz_ref, dv_ref, dlb_ref, rt):
        @pl.when(pl.program_id(1) == 0)
        def _():
            rt[...] = jnp.zeros_like(rt)
            dlb_ref[...] = jnp.zeros_like(dlb_ref)

        mirrored = pl.program_id(0) == 1
        for hh in range(HG_HEADS):
            ln = _hg_lanes(hh)
            q, k, f, sg, dsilu, e_first, e_last = (g_ref[0, j, :, ln] for j in range(HG_KEPT))
            lb = _sigmoid(a0_ref[0, :, ln] - a1_ref[0, :, ln])
            vv, dov = _hg_head((v0, v1), hh), do_ref[:, ln]
            ex = [e_first] + [e_ref[0, hh, 0, lev].astype(F32) for lev in range(HG_LEVELS)] + [e_last]
            a = a_ref[0, hh, 0]
            da = _dot(dov, vv, "nt")
            diag = jnp.sum(dov * vv, axis=-1, keepdims=True)
            s_t = s0_ref[0, hh, 0]
            r_t = rt[hh]
            k_end = k * ex[HG_LEVELS + 1]
            dv_ref[0, :, ln] = _dot(a, dov, "tn") + _dot(k_end, r_t, "nt")
            dq_inter = ex[0] * _dot(dov, s_t)
            dk_inter = ex[HG_LEVELS + 1] * _dot(vv, r_t)
            dq = diag * k + dq_inter
            dk = diag * q + dk_inter
            q_terms, k_terms = [q * dq_inter], [k * dk_inter]
            for lev in range(HG_LEVELS):
                qs, ks, e_q, e_k = _hg_level(q, k, ex, later_ref, lev)
                pairs = da * same_ref[lev]
                q_part = e_q * _dot(pairs, ks)
                k_part = e_k * _dot(pairs, qs, "tn")
                dq, dk = dq + q_part, dk + k_part
                q_terms.append(q * q_part)
                k_terms.append(k * k_part)
            decay = _hg_last_row(ex[0], mirrored)
            rt[hh] = r_t * decay + _dot(dov, q * ex[0], "tn")
            later = decay * jnp.sum(s_t * r_t, axis=0, keepdims=True)
            dlf = _dot(gp_ref[0], jnp.concatenate(q_terms, axis=0)) + _dot(gn_ref[0], jnp.concatenate(k_terms, axis=0)) + later
            df = dlf / f - dk
            dz_ref[0, :, ln] = df * (1.0 - lb) * sg * (1.0 - sg)
            dlb_ref[0, :, ln] += jnp.sum(df * (1.0 - sg), axis=0, keepdims=True)
            dhq_ref[0, :, ln] = dq * dsilu

    out = SDS((2, n, HG_DIM), F32)
    return pl.pallas_call(
        body, name=name, grid=(2, nc),
        in_specs=sp["v"] + [sp["vec"], sp["vec"], sp["later"], sp["same"], sp["sums"], sp["sums"],
                            sp["shared"], sp["state"], sp["weights"], sp["levels"], sp["kept"]],
        out_specs=[sp["per_dir"], sp["per_dir"], sp["per_dir"], sp["vec"]], out_shape=[out, out, out, SDS((2, 1, HG_DIM), F32)],
        scratch_shapes=[pltpu.VMEM((HG_HEADS, HG_HEAD_DIM, HG_HEAD_DIM), F32)],
        compiler_params=_cp("parallel", "arbitrary"))(p, p, a0, a1, *masks, gp, gn, do, s0, a, e, kept)


def _hg_post(o2, p, g, name):
    n = p.shape[0]
    tr = min(ROW_TILE, n)
    w = 2 * HG_HEAD_DIM

    def body(of_ref, ob_ref, hg_ref, g_ref, o_ref):
        for j in range(2):
            sl = slice(j * HG_HEAD_DIM, (j + 1) * HG_HEAD_DIM)
            o = of_ref[0, :, sl] + ob_ref[0, :, sl]
            hg = hg_ref[:, sl]
            o_ref[:, sl] = (o * _rstd(o) * g_ref[...] * (hg * _sigmoid(hg))).astype(o_ref.dtype)

    blk = pl.BlockSpec((tr, w), lambda i, j: (i, j))
    dirs = [pl.BlockSpec((1, tr, w), lambda i, j, d=d: (d, i, j)) for d in range(2)]
    return pl.pallas_call(
        body, name=name, grid=(n // tr, HG_DIM // w),
        in_specs=dirs + [pl.BlockSpec((tr, w), lambda i, j: (i, OFF_HG // w + j)), pl.BlockSpec((1, HG_HEAD_DIM), lambda i, j: (0, 0))],
        out_specs=blk, out_shape=SDS((n, HG_DIM), MXU_DTYPE), compiler_params=_cp("parallel", "parallel"))(o2, o2, p, g)


def _hg_post_bwd(o2, p, g, dcat, name, after=()):
    n = p.shape[0]
    tr = min(ROW_TILE, n)
    w = 2 * HG_HEAD_DIM
    after, after_specs = _unread(after)

    def body(of_ref, ob_ref, hg_ref, g_ref, d_ref, *rest):
        do_ref, dhg_ref, dg_ref = rest[len(after):]

        @pl.when(pl.program_id(1) == 0)
        def _():
            dg_ref[...] = jnp.zeros_like(dg_ref)

        for j in range(2):
            sl = slice(j * HG_HEAD_DIM, (j + 1) * HG_HEAD_DIM)
            o = of_ref[0, :, sl] + ob_ref[0, :, sl]
            hg = hg_ref[:, sl]
            d = d_ref[:, sl].astype(F32)
            sg = _sigmoid(hg)
            on = o * _rstd(o) * g_ref[...]
            dhg_ref[:, sl] = (d * on * sg * (1.0 + hg * (1.0 - sg))).astype(dhg_ref.dtype)
            dx, dg = _rms_bwd(o, g_ref[...], d * hg * sg)
            do_ref[:, sl] = dx
            dg_ref[0, :, sl] += dg

    blk = pl.BlockSpec((tr, w), lambda j, i: (i, j))
    dirs = [pl.BlockSpec((1, tr, w), lambda j, i, d=d: (d, i, j)) for d in range(2)]
    return pl.pallas_call(
        body, name=name, grid=(HG_DIM // w, n // tr),
        in_specs=dirs + [pl.BlockSpec((tr, w), lambda j, i: (i, OFF_HG // w + j)), pl.BlockSpec((1, HG_HEAD_DIM), lambda j, i: (0, 0)),
                         pl.BlockSpec((tr, w), lambda j, i: (i, ATT_Q_DIM // w + j))] + after_specs,
        out_specs=[blk, blk, pl.BlockSpec((1, 1, w), lambda j, i: (j, 0, 0))],
        out_shape=[SDS((n, HG_DIM), F32), SDS((n, HG_DIM), MXU_DTYPE), SDS((HG_DIM // w, 1, w), F32)],
        compiler_params=_cp("parallel", "arbitrary"))(o2, o2, p, g, dcat, *after)


XATT_TQ = 512


def _xattn_fwd(q, kv, name):
    n, nm = q.shape[0], kv.shape[0]
    tq = min(XATT_TQ, n)
    scale = X_HEAD_DIM ** -0.5

    def body(q_ref, k_ref, v_ref, o_ref):
        s = _dot(q_ref[...], k_ref[...], "nt") * scale
        e = jnp.exp(s - jnp.max(s, axis=-1, keepdims=True))
        o_ref[...] = _dot(e / jnp.sum(e, axis=-1, keepdims=True), v_ref[...]).astype(o_ref.dtype)

    qb = pl.BlockSpec((tq, X_HEAD_DIM), lambda h, i: (i, h))
    return pl.pallas_call(
        body, name=name, grid=(X_HEADS, n // tq),
        in_specs=[qb, pl.BlockSpec((nm, X_HEAD_DIM), lambda h, i: (0, h)), pl.BlockSpec((nm, X_HEAD_DIM), lambda h, i: (0, X_HEADS + h))],
        out_specs=qb, out_shape=SDS(q.shape, MXU_DTYPE), compiler_params=_cp("parallel", "parallel"))(q, kv, kv)


def _xattn_bwd(q, kv, do, name, after=()):
    n, nm = q.shape[0], kv.shape[0]
    tq = min(XATT_TQ, n)
    scale = X_HEAD_DIM ** -0.5
    after, after_specs = _unread(after)

    def body(q_ref, k_ref, v_ref, do_ref, *rest):
        dq_ref, dk_ref, dv_ref = rest[len(after):]

        @pl.when(pl.program_id(1) == 0)
        def _():
            dk_ref[...] = jnp.zeros_like(dk_ref)
            dv_ref[...] = jnp.zeros_like(dv_ref)

        qv, dov = q_ref[...], do_ref[...]
        s = _dot(qv, k_ref[...], "nt") * scale
        e = jnp.exp(s - jnp.max(s, axis=-1, keepdims=True))
        p = e / jnp.sum(e, axis=-1, keepdims=True)
        dp = _dot(dov, v_ref[...], "nt")
        ds = p * (dp - jnp.sum(p * dp, axis=-1, keepdims=True)) * scale
        dq_ref[...] = _dot(ds, k_ref[...]).astype(dq_ref.dtype)
        dk_ref[...] += _dot(ds, qv, "tn")
        dv_ref[...] += _dot(p, dov, "tn")

    qb = pl.BlockSpec((tq, X_HEAD_DIM), lambda h, i: (i, h))
    kb = pl.BlockSpec((nm, X_HEAD_DIM), lambda h, i: (0, h))
    return pl.pallas_call(
        body, name=name, grid=(X_HEADS, n // tq),
        in_specs=[qb, kb, pl.BlockSpec((nm, X_HEAD_DIM), lambda h, i: (0, X_HEADS + h)), qb] + after_specs, out_specs=[qb, kb, kb],
        out_shape=[SDS(q.shape, MXU_DTYPE), SDS((nm, X_HEADS * X_HEAD_DIM), F32), SDS((nm, X_HEADS * X_HEAD_DIM), F32)],
        compiler_params=_cp("parallel", "arbitrary"))(q, kv, kv, do, *after)


def _edge_rows(shape):
    row = lax.broadcasted_iota(jnp.int32, shape, 0)
    return row == 0, row == shape[0] - 1


def _shift_rows(u, down, edges):
    if down:
        return jnp.where(edges[0], 0.0, pltpu.roll(u, 1, axis=0))
    return jnp.where(edges[1], 0.0, pltpu.roll(u, u.shape[0] - 1, axis=0))


def _conv(u, w, b, edges):
    return b + _shift_rows(u, True, edges) * w[0:1, :] + u * w[1:2, :] + _shift_rows(u, False, edges) * w[2:3, :]


def _ff_specs(n):
    gate = lambda rows: pl.BlockSpec((rows, FF_COLS), lambda j: (0, j))
    val = lambda rows: pl.BlockSpec((rows, FF_COLS), lambda j: (0, FF_BLOCKS + j))
    return [gate(n), val(n), gate(3), val(3), gate(1), val(1)], gate


def _conv_gate(u, cw, cb, name):
    n = u.shape[0]
    ins, gate_blk = _ff_specs(n)

    def body(ug_ref, uv_ref, wg_ref, wv_ref, bg_ref, bv_ref, o_ref):
        edges = _edge_rows(ug_ref.shape)
        gate = _conv(ug_ref[...], wg_ref[...], bg_ref[...], edges)
        val = _conv(uv_ref[...], wv_ref[...], bv_ref[...], edges)
        o_ref[...] = (gate * _sigmoid(gate) * val).astype(o_ref.dtype)

    return pl.pallas_call(
        body, name=name, grid=(FF_BLOCKS,), in_specs=ins, out_specs=gate_blk(n), out_shape=SDS((n, D_FF), MXU_DTYPE),
        compiler_params=_cp("parallel"))(u, u, cw, cw, cb, cb)


def _conv_gate_bwd(u, cw, cb, da, name, after=()):
    n = u.shape[0]
    ins, gate_blk = _ff_specs(n)
    after, after_specs = _unread(after)

    def side(dacc, u, w, edges, du_ref, dw_ref, db_ref):
        nxt, prv = _shift_rows(dacc, False, edges), _shift_rows(dacc, True, edges)
        du_ref[...] = (nxt * w[0:1, :] + dacc * w[1:2, :] + prv * w[2:3, :]).astype(du_ref.dtype)
        db_ref[...] = jnp.sum(dacc, axis=0, keepdims=True)
        dw_ref[0:1, :] = jnp.sum(nxt * u, axis=0, keepdims=True)
        dw_ref[1:2, :] = jnp.sum(dacc * u, axis=0, keepdims=True)
        dw_ref[2:3, :] = jnp.sum(prv * u, axis=0, keepdims=True)

    def body(ug_ref, uv_ref, wg_ref, wv_ref, bg_ref, bv_ref, da_ref, *rest):
        dug_ref, duv_ref, dwg_ref, dwv_ref, dbg_ref, dbv_ref = rest[len(after):]
        ug, uv = ug_ref[...], uv_ref[...]
        edges = _edge_rows(ug.shape)
        gate = _conv(ug, wg_ref[...], bg_ref[...], edges)
        val = _conv(uv, wv_ref[...], bv_ref[...], edges)
        sg = _sigmoid(gate)
        dav = da_ref[...].astype(F32)
        side(dav * val * sg * (1.0 + gate * (1.0 - sg)), ug, wg_ref[...], edges, dug_ref, dwg_ref, dbg_ref)
        side(dav * gate * sg, uv, wv_ref[...], edges, duv_ref, dwv_ref, dbv_ref)

    return pl.pallas_call(
        body, name=name, grid=(FF_BLOCKS,), in_specs=ins + [gate_blk(n)] + after_specs,
        out_specs=[gate_blk(n), gate_blk(n), gate_blk(3), gate_blk(3), gate_blk(1), gate_blk(1)],
        out_shape=[SDS((n, D_FF), MXU_DTYPE)] * 2 + [SDS((3, D_FF), F32)] * 2 + [SDS((1, D_FF), F32)] * 2,
        compiler_params=_cp("parallel"))(u, u, cw, cw, cb, cb, da, *after)


def _adamw(w, g, m, v, name):
    r, c = w.shape[-2:]
    tr = _row_tile(r, ELEMENTWISE_ROWS)
    assert w.ndim == 2 or w.shape[:-2] == (1,), (name, w.shape)

    def body(w_ref, g_ref, m_ref, v_ref, d_ref, mo_ref, vo_ref, go_ref):
        gv = g_ref[...]
        go_ref[...] = gv
        mn = ADAM_B1 * m_ref[...] + (1.0 - ADAM_B1) * gv
        vn = ADAM_B2 * v_ref[...] + (1.0 - ADAM_B2) * gv * gv
        m_hat = mn / (1.0 - ADAM_B1 ** ADAM_STEP)
        v_hat = vn / (1.0 - ADAM_B2 ** ADAM_STEP)
        d_ref[...] = -ADAM_LR * (m_hat / (jnp.sqrt(v_hat) + ADAM_EPS) + ADAM_WD * w_ref[...])
        mo_ref[...] = mn
        vo_ref[...] = vn

    blk = pl.BlockSpec((tr, c), lambda i: (i, 0)) if w.ndim == 2 else pl.BlockSpec((1, tr, c), lambda i: (0, i, 0))
    out = SDS(w.shape, F32)
    return pl.pallas_call(body, name=name, grid=(r // tr,), in_specs=[blk] * 4, out_specs=[blk] * 4, out_shape=[out] * 4,
                          compiler_params=_cp("parallel"))(w, g, m, v)


ANY = pl.BlockSpec(memory_space=pl.ANY)


def _place():
    x, y, c = lax.axis_index("x"), lax.axis_index("y"), lax.axis_index("c")
    return x, y, c, [(1 - x, y), (x, 1 - y), (1 - x, 1 - y)]


HBM = pl.BlockSpec(memory_space=pltpu.HBM)
SEM = pl.BlockSpec(memory_space=pltpu.SEMAPHORE)
TOKEN = pl.BlockSpec(memory_space=pltpu.VMEM)
TOKEN_SHAPE = SDS((8, 128), F32)
PEERS = 7


def _in_hbm(a):
    return pltpu.with_memory_space_constraint(a, pltpu.HBM)


def _split_params():
    return pltpu.CompilerParams(has_side_effects=pltpu.SideEffectType.DATAFLOW_SIDE_EFFECTING)


def _gather_start(shards, name, after=()):
    nt = len(shards)
    after, after_specs = _unread(after)

    def body(*refs):
        ins, lands = refs[:nt], refs[nt:2 * nt]
        outs = refs[2 * nt + len(after):]
        sends, recvs = outs[:nt], outs[nt:2 * nt]
        x, y, c, chips = _place()
        me = 2 * x + y
        for t in range(nt):
            h = ins[t].shape[0] // 2
            mine = pl.ds(c * h, h)
            for j, (cx, cy) in enumerate(chips):
                for dc in range(2):
                    pltpu.make_async_remote_copy(src_ref=ins[t].at[mine], dst_ref=lands[t].at[me, mine], send_sem=sends[t].at[2 * j + dc],
                                                 recv_sem=recvs[t].at[2 * j + c], device_id=(cx, cy, dc), device_id_type=MESH).start()
            pltpu.make_async_remote_copy(src_ref=ins[t], dst_ref=lands[t].at[me], send_sem=sends[t].at[PEERS - 1], recv_sem=recvs[t].at[PEERS - 1],
                                         device_id=(x, y, 1 - c), device_id_type=MESH).start()
        outs[-1][...] = jnp.zeros(TOKEN_SHAPE.shape, F32)

    lands = [lax.empty((4,) + s.shape, s.dtype) for s in shards]
    out = pl.pallas_call(
        body, name=name, in_specs=[HBM] * (2 * nt) + after_specs, out_specs=[SEM] * (2 * nt) + [HBM] * (2 * nt) + [TOKEN],
        out_shape=[pltpu.SemaphoreType.DMA((PEERS,))] * (2 * nt)
        + [pltpu.HBM(s.shape, s.dtype) for s in shards] + [pltpu.HBM(l.shape, l.dtype) for l in lands] + [TOKEN_SHAPE],
        input_output_aliases={t: 2 * nt + t for t in range(2 * nt)}, compiler_params=_split_params())(
            *[_in_hbm(s) for s in shards], *[_in_hbm(l) for l in lands], *after)
    return out[:nt], out[nt:2 * nt], out[2 * nt:3 * nt], out[3 * nt:4 * nt], out[-1]


def _gather_wait(sends, recvs, shards, lands, after, name):
    nt = len(shards)

    def body(*refs):
        ins, lands_ref = refs[:nt], refs[nt:2 * nt]
        send_refs, recv_refs = refs[2 * nt:3 * nt], refs[3 * nt:4 * nt]
        x, y, c, chips = _place()
        for t in range(nt):
            h = ins[t].shape[0] // 2
            for j, (cx, cy) in enumerate(chips):
                for cs in range(2):
                    blk = lands_ref[t].at[2 * cx + cy, pl.ds(cs * h, h)]
                    pltpu.make_async_remote_copy(src_ref=blk, dst_ref=blk, send_sem=send_refs[t].at[2 * j + cs], recv_sem=recv_refs[t].at[2 * j + cs],
                                                 device_id=(cx, cy, cs), device_id_type=MESH).wait()
            blk = lands_ref[t].at[2 * x + y]
            pltpu.make_async_remote_copy(src_ref=blk, dst_ref=blk, send_sem=send_refs[t].at[PEERS - 1], recv_sem=recv_refs[t].at[PEERS - 1],
                                         device_id=(x, y, 1 - c), device_id_type=MESH).wait()

    out = pl.pallas_call(
        body, name=name, in_specs=[HBM] * (2 * nt) + [SEM] * (2 * nt) + [ANY], out_specs=[HBM] * (2 * nt),
        out_shape=[pltpu.HBM(s.shape, s.dtype) for s in shards] + [pltpu.HBM(l.shape, l.dtype) for l in lands],
        input_output_aliases={t: t for t in range(2 * nt)}, compiler_params=_split_params())(*shards, *lands, *sends, *recvs, after)
    return out[nt:]


def _gather_pieces_start(shard, name, after=()):
    h = shard.shape[0] // 2
    after, after_specs = _unread(after)

    def body(src, land, *rest):
        send, recv = rest[len(after):len(after) + 2]
        x, y, c, chips = _place()
        me = 2 * x + y
        mine = pl.ds(c * h, h)
        for j, (cx, cy) in enumerate(chips):
            pltpu.make_async_remote_copy(src_ref=src.at[mine], dst_ref=land.at[me, mine], send_sem=send.at[j], recv_sem=recv.at[j],
                                         device_id=(cx, cy, c), device_id_type=MESH).start()
        pltpu.make_async_remote_copy(src_ref=src, dst_ref=land.at[me], send_sem=send.at[len(chips)], recv_sem=recv.at[len(chips)],
                                     device_id=(x, y, 1 - c), device_id_type=MESH).start()
        rest[-1][...] = jnp.zeros(TOKEN_SHAPE.shape, F32)

    land = lax.empty((4,) + shard.shape, shard.dtype)
    return pl.pallas_call(
        body, name=name, in_specs=[HBM, HBM] + after_specs, out_specs=[SEM, SEM, HBM, HBM, TOKEN],
        out_shape=[pltpu.SemaphoreType.DMA((4,)), pltpu.SemaphoreType.DMA((4,)), pltpu.HBM(shard.shape, shard.dtype),
                   pltpu.HBM(land.shape, land.dtype), TOKEN_SHAPE],
        input_output_aliases={0: 2, 1: 3}, compiler_params=_split_params())(_in_hbm(shard), _in_hbm(land), *after)


def _gather_pieces_wait(send, recv, shard, land, after, name):
    h = shard.shape[0] // 2
    after, after_specs = _unread(after)

    def body(*refs):
        land_ref, send_ref, recv_ref = refs[1:4]
        x, y, c, chips = _place()
        for j, (cx, cy) in enumerate(chips):
            blk = land_ref.at[2 * cx + cy, pl.ds(c * h, h)]
            pltpu.make_async_remote_copy(src_ref=blk, dst_ref=blk, send_sem=send_ref.at[j], recv_sem=recv_ref.at[j],
                                         device_id=(cx, cy, c), device_id_type=MESH).wait()
        own = land_ref.at[2 * x + y]
        pltpu.make_async_remote_copy(src_ref=own, dst_ref=own, send_sem=send_ref.at[len(chips)], recv_sem=recv_ref.at[len(chips)],
                                     device_id=(x, y, 1 - c), device_id_type=MESH).wait()

    out = pl.pallas_call(
        body, name=name, in_specs=[HBM, HBM, SEM, SEM] + after_specs, out_specs=[HBM, HBM],
        out_shape=[pltpu.HBM(shard.shape, shard.dtype), pltpu.HBM(land.shape, land.dtype)],
        input_output_aliases={0: 0, 1: 1}, compiler_params=_split_params())(shard, land, send, recv, *after)
    return out[1]


def _pass_pieces(land, name):
    h = land.shape[1] // 2

    def body(_, out, send, recv):
        x, y, c, chips = _place()

        def piece(j, cc):
            cx, cy = chips[j]
            blk = out.at[2 * cx + cy, pl.ds(cc * h, h)]
            return pltpu.make_async_remote_copy(src_ref=blk, dst_ref=blk, send_sem=send.at[j], recv_sem=recv.at[j],
                                                device_id=(x, y, 1 - c), device_id_type=MESH)

        for j in range(len(chips)):
            piece(j, c).start()
        for j in range(len(chips)):
            piece(j, 1 - c).wait_recv()
        for j in range(len(chips)):
            piece(j, c).wait_send()

    return pl.pallas_call(
        body, name=name, in_specs=[ANY], out_specs=ANY, out_shape=SDS(land.shape, land.dtype), input_output_aliases={0: 0},
        scratch_shapes=[pltpu.SemaphoreType.DMA((3,))] * 2, compiler_params=pltpu.CompilerParams(has_side_effects=True))(land)


def _scatter_start(gs, name):
    nt = len(gs)

    def body(*refs):
        g_refs, lands = refs[:nt], refs[nt:2 * nt]
        sends, recvs = refs[2 * nt:3 * nt], refs[3 * nt:4 * nt]
        x, y, c, chips = _place()
        for g_ref, land, send, recv in zip(g_refs, lands, sends, recvs):
            h = land.shape[1]
            for j, (cx, cy) in enumerate(chips):
                for dc in range(2):
                    pltpu.make_async_remote_copy(src_ref=g_ref.at[2 * cx + cy, pl.ds(dc * h, h)], dst_ref=land.at[2 * j + c],
                                                 send_sem=send.at[2 * j + dc], recv_sem=recv.at[2 * j + c], device_id=(cx, cy, dc),
                                                 device_id_type=MESH).start()
            pltpu.make_async_remote_copy(src_ref=g_ref.at[2 * x + y, pl.ds((1 - c) * h, h)], dst_ref=land.at[PEERS - 1], send_sem=send.at[PEERS - 1],
                                         recv_sem=recv.at[PEERS - 1], device_id=(x, y, 1 - c), device_id_type=MESH).start()
        refs[-1][...] = jnp.zeros(TOKEN_SHAPE.shape, F32)

    lands = [lax.empty((PEERS, g.shape[1] // 2, g.shape[2]), g.dtype) for g in gs]
    out = pl.pallas_call(
        body, name=name, in_specs=[HBM] * (2 * nt), out_specs=[SEM] * (2 * nt) + [HBM] * (2 * nt) + [TOKEN],
        out_shape=[pltpu.SemaphoreType.DMA((PEERS,))] * (2 * nt) + [pltpu.HBM(a.shape, a.dtype) for a in gs + lands] + [TOKEN_SHAPE],
        input_output_aliases={t: 2 * nt + t for t in range(2 * nt)}, compiler_params=_split_params())(
            *[_in_hbm(a) for a in gs + lands])
    return [tuple(out[q * nt + t] for q in range(4)) for t in range(nt)], out[-1]


def _scatter_wait(started, after, name):
    nt = len(started)

    def body(*refs):
        lands = refs[nt:2 * nt]
        sends, recvs = refs[2 * nt:3 * nt], refs[3 * nt:4 * nt]
        x, y, c, chips = _place()
        peers = [(cx, cy, dc) for cx, cy in chips for dc in range(2)] + [(x, y, 1 - c)]
        for t in range(nt):
            for k, peer in enumerate(peers):
                blk = lands[t].at[k]
                pltpu.make_async_remote_copy(src_ref=blk, dst_ref=blk, send_sem=sends[t].at[k], recv_sem=recvs[t].at[k],
                                             device_id=peer, device_id_type=MESH).wait()

    gs, lands = [s[2] for s in started], [s[3] for s in started]
    after, after_specs = _unread(after)
    out = pl.pallas_call(
        body, name=name, in_specs=[HBM] * (2 * nt) + [SEM] * (2 * nt) + after_specs, out_specs=[HBM] * (2 * nt),
        out_shape=[pltpu.HBM(a.shape, a.dtype) for a in gs + lands],
        input_output_aliases={t: t for t in range(2 * nt)}, compiler_params=_split_params())(
            *gs, *lands, *[s[0] for s in started], *[s[1] for s in started], *after)
    return out[:nt], out[nt:]


def _join_start(bufs, name, after=()):
    nt = len(bufs)
    after, after_specs = _unread(after)

    def body(*refs):
        send, recv = refs[nt + len(after):nt + len(after) + 2]
        x, y, c, _ = _place()
        for t in range(nt):
            pltpu.make_async_remote_copy(src_ref=refs[t].at[c], dst_ref=refs[t].at[c], send_sem=send.at[t], recv_sem=recv.at[t],
                                         device_id=(x, y, 1 - c), device_id_type=MESH).start()
        refs[-1][...] = jnp.zeros(TOKEN_SHAPE.shape, F32)

    out = pl.pallas_call(
        body, name=name, in_specs=[HBM] * nt + after_specs, out_specs=[SEM, SEM] + [HBM] * nt + [TOKEN],
        out_shape=[pltpu.SemaphoreType.DMA((nt,))] * 2 + [pltpu.HBM(b.shape, b.dtype) for b in bufs] + [TOKEN_SHAPE],
        input_output_aliases={t: 2 + t for t in range(nt)}, compiler_params=_split_params())(*[_in_hbm(b) for b in bufs], *after)
    return out[0], out[1], out[2:2 + nt], out[-1]


def _join_wait(send, recv, bufs, after, name):
    nt = len(bufs)
    after, after_specs = _unread(after)

    def body(*refs):
        send_ref, recv_ref = refs[nt:nt + 2]
        x, y, c, _ = _place()
        for t in range(nt):
            theirs = refs[t].at[1 - c]
            pltpu.make_async_remote_copy(src_ref=theirs, dst_ref=theirs, send_sem=send_ref.at[t], recv_sem=recv_ref.at[t],
                                         device_id=(x, y, 1 - c), device_id_type=MESH).wait()

    return pl.pallas_call(
        body, name=name, in_specs=[HBM] * nt + [SEM, SEM] + after_specs, out_specs=[HBM] * nt,
        out_shape=[pltpu.HBM(b.shape, b.dtype) for b in bufs],
        input_output_aliases={t: t for t in range(nt)}, compiler_params=_split_params())(*bufs, send, recv, *after)


def _flips():
    return [(dx, dy, dc) for dx in range(2) for dy in range(2) for dc in range(2) if (dx, dy, dc) != (0, 0, 0)]


def _flipped(x, y, c, flips):
    dx, dy, dc = flips
    return (1 - x if dx else x, 1 - y if dy else y, 1 - c if dc else c)


def _small_start(v, name, after=()):
    after, after_specs = _unread(after)

    def body(v_ref, land, *rest):
        send, recv = rest[len(after):len(after) + 2]
        x, y, c, _ = _place()
        for j, flips in enumerate(_flips()):
            pltpu.make_async_remote_copy(src_ref=v_ref, dst_ref=land.at[4 * x + 2 * y + c], send_sem=send.at[j], recv_sem=recv.at[j],
                                         device_id=_flipped(x, y, c, flips), device_id_type=MESH).start()
        rest[-1][...] = jnp.zeros(TOKEN_SHAPE.shape, F32)

    land = lax.empty((8,) + v.shape, v.dtype)
    return pl.pallas_call(
        body, name=name, in_specs=[HBM, HBM] + after_specs, out_specs=[SEM, SEM, HBM, HBM, TOKEN],
        out_shape=[pltpu.SemaphoreType.DMA((PEERS,)), pltpu.SemaphoreType.DMA((PEERS,)), pltpu.HBM(v.shape, v.dtype),
                   pltpu.HBM(land.shape, land.dtype), TOKEN_SHAPE],
        input_output_aliases={0: 2, 1: 3}, compiler_params=_split_params())(_in_hbm(v), _in_hbm(land), *after)


def _small_wait(send, recv, v, land, after, name):
    after, after_specs = _unread(after)

    def body(v_ref, land_ref, send_ref, recv_ref, *rest):
        x, y, c, _ = _place()
        for j, flips in enumerate(_flips()):
            px, py, pc = _flipped(x, y, c, flips)
            blk = land_ref.at[4 * px + 2 * py + pc]
            pltpu.make_async_remote_copy(src_ref=blk, dst_ref=blk, send_sem=send_ref.at[j], recv_sem=recv_ref.at[j],
                                         device_id=(px, py, pc), device_id_type=MESH).wait()

    return pl.pallas_call(
        body, name=name, in_specs=[HBM, HBM, SEM, SEM] + after_specs, out_specs=[HBM, HBM],
        out_shape=[pltpu.HBM(v.shape, v.dtype), pltpu.HBM(land.shape, land.dtype)],
        input_output_aliases={0: 0, 1: 1}, compiler_params=_split_params())(v, land, send, recv, *after)


def _sum_small(v, land, name):
    def body(v_ref, land_ref, o_ref):
        x, y, c, _ = _place()
        me = 4 * x + 2 * y + c
        acc = jnp.where(me == 0, v_ref[...], land_ref[0])
        for d in range(1, 8):
            acc = acc + jnp.where(me == d, v_ref[...], land_ref[d])
        o_ref[...] = acc

    vm = pl.BlockSpec(memory_space=pltpu.VMEM)
    return pl.pallas_call(body, name=name, in_specs=[vm, vm], out_specs=vm, out_shape=SDS(v.shape, F32))(v, land)


def _sum_devices(g, land, me, core, name):
    npeer, h, c = land.shape
    tr = _row_tile(h, 2 * ELEMENTWISE_ROWS)
    steps = h // tr

    def body(ix_ref, own_ref, land_ref, o_ref):
        acc = own_ref[0].astype(F32)
        for j in range(npeer):
            acc = acc + land_ref[j].astype(F32)
        o_ref[0] = acc

    grid_spec = pltpu.PrefetchScalarGridSpec(
        num_scalar_prefetch=1, grid=(steps,),
        in_specs=[pl.BlockSpec((1, tr, c), lambda i, ix: (ix[0], ix[1] * steps + i, 0)), pl.BlockSpec((npeer, tr, c), lambda i, ix: (0, i, 0))],
        out_specs=pl.BlockSpec((1, tr, c), lambda i, ix: (ix[1], i, 0)))
    return pl.pallas_call(body, name=name, grid_spec=grid_spec, out_shape=SDS((2, h, c), F32),
                          compiler_params=_cp("parallel"))(jnp.stack([me, core]), g, land)


def _pack_small(parts):
    flat = jnp.concatenate([p.reshape(-1) for p in parts])
    total = flat.shape[0]
    rows = -(-total // 1024) * 8
    return jnp.pad(flat, (0, rows * 128 - total)).reshape(rows, 128)


def _unpack_small(packed, shapes):
    flat = packed.reshape(-1)
    out, off = [], 0
    for s in shapes:
        size = int(np.prod(s))
        out.append(flat[off:off + size].reshape(s))
        off += size
    return out


def _local_step(x, mem, target, w_in_t, first_after, mid_weights, ffn_weights, on_grad, gains, conv_w, conv_b, hg_lb):
    n = x.shape[0]
    cos, sin = _rope_tables(n)
    seg = _hg_segments()
    gp, gn = _hg_pair_sums()
    masks = _hg_level_masks()
    gq2 = jnp.tile(gains["q_norm_g"], (1, 2))
    gk2 = jnp.tile(gains["k_norm_g"], (1, 2))
    a0 = hg_lb[:, 0:1, :]
    a1 = hg_lb[:, 1:2, :]

    p, h1 = _norm_mm(x, gains["pre_mix_g"], w_in_t, F32, TOKEN_TILE, 1664, "in_proj", after=(first_after,), w_turned=True)
    qr, kr = _qk_prep(p, gq2, gk2, cos, sin, "qk_prep")
    heads = lambda a: a.reshape(n, ATT_KV_HEADS, ATT_HEAD_DIM).transpose(1, 0, 2)
    kh = heads(kr)
    vh = heads(p[:, OFF_AV:OFF_AV + ATT_KV_DIM].astype(MXU_DTYPE))
    att = _attn_fwd(qr, kh, vh, "attn_fwd")
    o2, s0, hg_a, hg_e, hg_kept = _hgrn_fwd(p, a0, a1, seg, masks, "hgrn_fwd")
    rec = _hg_post(o2, p, gains["hg_out_norm_g"], "hg_post")
    cat = jnp.concatenate([att, rec], axis=1)
    w_out, w_xq, w_xkv, w_xo = mid_weights(cat)
    mixed, x1 = _mm_resid_norm(cat, w_out, x, gains["post_mix_g"], 512, "out_proj_resid")
    xq, h2 = _norm_mm(x1, gains["pre_x_g"], w_xq, MXU_DTYPE, TOKEN_TILE, 1024, "xq_proj")
    kv, mn = _norm_mm(mem, gains["mem_norm_g"], w_xkv, MXU_DTYPE, 256, 2048, "xkv_proj")
    ox = _xattn_fwd(xq, kv, "xattn_fwd")
    xo, x2 = _mm_resid_norm(ox, w_xo, x1, gains["post_x_g"], 512, "xo_proj_resid")
    w_up = ffn_weights("w_up", x2)
    u, h3 = _norm_mm(x2, gains["pre_ffn_g"], w_up, F32, TOKEN_TILE, 1408, "up_proj")
    act = _conv_gate(u, conv_w, conv_b, "conv_gate")
    w_down = ffn_weights("w_down", act)
    dn, d3, loss = _mm_resid_norm(act, w_down, x2, gains["post_ffn_g"], 512, "down_proj_resid_loss", target=target)

    gs = {}
    d_act, d_dn, gs["post_ffn_g"] = _norm_bwd_mm(dn, gains["post_ffn_g"], d3, w_down, F32, 512, 1408, "ffn_post_bwd_down_dx")
    tok = on_grad("w_down", _mm(act, d_dn, "tn", WIRE_DTYPE, 1408, 1024, "down_dw"))
    du_g, du_v, dcw_g, dcw_v, dcb_g, dcb_v = _conv_gate_bwd(u, conv_w, conv_b, d_act, "conv_gate_bwd", after=(tok,))
    gs["conv_w"] = jnp.concatenate([dcw_g, dcw_v], axis=1)
    gs["conv_b"] = jnp.concatenate([dcb_g, dcb_v], axis=1)
    ff_shard = w_up.shape[2]
    g_up = _dw_by_owner(h3, du_g, ff_shard, 0, None, 512, "up_dw_gate")
    tok = on_grad("w_up", _dw_by_owner(h3, du_v, ff_shard, 2, g_up, 512, "up_dw_value"))
    d2, gs["pre_ffn_g"] = _dx_norm_bwd([(du_g, 0), (du_g, 1), (du_v, 0), (du_v, 1)], w_up, x2, gains["pre_ffn_g"], d3, 512,
                                       "up_dx_pre_bwd", after=(tok,))
    d_ox, d_xo, gs["post_x_g"] = _norm_bwd_mm(xo, gains["post_x_g"], d2, w_xo, MXU_DTYPE, 512, 1024, "x_post_bwd_xo_dx")
    tok = on_grad("w_xo", _mm(ox, d_xo, "tn", WIRE_DTYPE, 512, 1024, "xo_dw"))
    d_xq, d_k, d_v = _xattn_bwd(xq, kv, d_ox, "xattn_bwd", after=(tok,))
    d_kv = jnp.concatenate([d_k, d_v], axis=1).astype(MXU_DTYPE)
    tok = on_grad("w_xq", _mm(h2, d_xq, "tn", WIRE_DTYPE, 512, 1024, "xq_dw"))
    tok_kv = on_grad("w_xkv", _dw_by_owner(mn, d_kv, w_xkv.shape[2], 0, None, 512, "xkv_dw"))
    d1, gs["pre_x_g"] = _dx_norm_bwd([(d_xq, 0)], w_xq[None], x1, gains["pre_x_g"], d2, 512, "xq_dx_pre_bwd", after=(tok, tok_kv))
    d_mn = _mm_nt_parts([(d_kv, s) for s in range(4)], w_xkv, F32, 256, 1024, "xkv_dx")
    _, gs["mem_norm_g"] = _norm_bwd(mem, gains["mem_norm_g"], d_mn, None, MXU_DTYPE, "mem_norm_bwd")
    d_cat, d_mixed, gs["post_mix_g"] = _norm_bwd_mm(mixed, gains["post_mix_g"], d1, w_out, MXU_DTYPE, 512, 1024, "mix_post_bwd_out_dx")
    tok = on_grad("w_out", _mm(cat, d_mixed, "tn", WIRE_DTYPE, 512, 1024, "out_dw"))
    d_o, d_hg, dg_hg = _hg_post_bwd(o2, p, gains["hg_out_norm_g"], d_cat, "hg_post_bwd", after=(tok,))
    gs["hg_out_norm_g"] = dg_hg.reshape(HG_HEADS, HG_HEAD_DIM).sum(axis=0, keepdims=True)
    dhq2, dz2, dhv2, dlb = _hgrn_bwd(p, a0, a1, masks, gp, gn, d_o, s0, hg_a, hg_e, hg_kept, "hgrn_bwd")
    lb = jax.nn.sigmoid(a0 - a1)
    da0 = dlb * lb * (1.0 - lb)
    gs["hg_lb"] = jnp.concatenate([da0, -da0], axis=1)
    d_qr, d_kh, d_vh = _attn_bwd(qr, kh, vh, cat, d_cat, "attn_bwd")
    unheads = lambda a: a.transpose(2, 0, 1).reshape(n, ATT_KV_DIM)
    d_aq, d_ak, dgq, dgk = _qk_prep_bwd(p, gq2, gk2, cos, sin, d_qr, unheads(d_kh), "qk_prep_bwd")
    gs["q_norm_g"] = dgq.reshape(ATT_HEADS, ATT_HEAD_DIM).sum(axis=0, keepdims=True)
    gs["k_norm_g"] = dgk.reshape(ATT_KV_HEADS, ATT_HEAD_DIM).sum(axis=0, keepdims=True)
    d_p = jnp.concatenate([d_aq, d_ak, unheads(d_vh).astype(MXU_DTYPE), (dhq2[0] + dhq2[1]).astype(MXU_DTYPE),
                           dz2[0].astype(MXU_DTYPE), dz2[1].astype(MXU_DTYPE), (dhv2[0] + dhv2[1]).astype(MXU_DTYPE), d_hg], axis=1)
    tok = on_grad("w_in", _mm(d_p, h1, "tn", WIRE_DTYPE, 1664, 1024, "in_dw"))
    grad_x, gs["pre_mix_g"] = _dx_norm_bwd([(d_p, 0)], w_in_t[None], x, gains["pre_mix_g"], d1, 512, "in_dx_pre_bwd", after=(tok,),
                                           b_turned=True)
    return loss, grad_x, gs


MATS = ("w_in", "w_out", "w_xq", "w_xkv", "w_xo", "w_up", "w_down")
GAINS = ("pre_mix_g", "q_norm_g", "k_norm_g", "hg_out_norm_g", "post_mix_g", "pre_x_g", "mem_norm_g", "post_x_g", "pre_ffn_g", "post_ffn_g")
WEIGHTS = ('pre_mix_g', 'w_in', 'q_norm_g', 'k_norm_g', 'hg_lb', 'hg_out_norm_g', 'w_out', 'post_mix_g', 'pre_x_g', 'mem_norm_g', 'w_xq',
           'w_xkv', 'w_xo', 'post_x_g', 'pre_ffn_g', 'w_up', 'conv_w', 'conv_b', 'w_down', 'post_ffn_g')


def kernel(x, mem, pre_mix_g, w_in, q_norm_g, k_norm_g, hg_lb, hg_out_norm_g, w_out, post_mix_g, pre_x_g, mem_norm_g, w_xq, w_xkv, w_xo, post_x_g, pre_ffn_g, w_up, conv_w, conv_b, w_down, post_ffn_g, loss_target, m_pre_mix_g, m_w_in, m_q_norm_g, m_k_norm_g, m_hg_lb, m_hg_out_norm_g, m_w_out, m_post_mix_g, m_pre_x_g, m_mem_norm_g, m_w_xq, m_w_xkv, m_w_xo, m_post_x_g, m_pre_ffn_g, m_w_up, m_conv_w, m_conv_b, m_w_down, m_post_ffn_g, v_pre_mix_g, v_w_in, v_q_norm_g, v_k_norm_g, v_hg_lb, v_hg_out_norm_g, v_w_out, v_post_mix_g, v_pre_x_g, v_mem_norm_g, v_w_xq, v_w_xkv, v_w_xo, v_post_x_g, v_pre_ffn_g, v_w_up, v_conv_w, v_conv_b, v_w_down, v_post_ffn_g):
    args = dict(locals())
    w = {k: args[k] for k in WEIGHTS}
    m = {k: args["m_" + k] for k in WEIGHTS}
    v = {k: args["v_" + k] for k in WEIGHTS}
    chip = 2 * lax.axis_index("x") + lax.axis_index("y")
    core = lax.axis_index("c")

    turned = ("w_in",)
    shards = {k: (jnp.swapaxes(w[k], 1, 2) if k in turned else w[k])[0].astype(WIRE_DTYPE) for k in MATS}

    def whole(k, g):
        return g if k in ("w_xkv", "w_up") else g.reshape(-1, g.shape[-1])

    mid_names, ffn_names = ("w_out", "w_xq", "w_xkv", "w_xo"), ("w_up", "w_down")
    small = _small_start(_pack_small([w["conv_w"][0], w["hg_lb"]]), "gather_small_start")
    w_in_pieces = _gather_pieces_start(shards["w_in"], "gather_w_in_start", after=(small[4],))
    mid = _gather_start([shards[k] for k in mid_names], "gather_mid_start", after=(w_in_pieces[4],))
    ffn = _gather_start([shards[k] for k in ffn_names], "gather_ffn_start", after=(mid[4],))
    w_in_t = whole("w_in", _pass_pieces(_gather_pieces_wait(*w_in_pieces[:4], (ffn[4],), "gather_w_in_wait"), "gather_w_in_pass"))
    mine, others = _small_wait(*small[:4], (w_in_t,), "gather_small_wait")
    small_in = lax.dynamic_update_slice_in_dim(others, mine[None], 2 * chip + core, axis=0)

    def mid_weights(after):
        return [whole(k, g) for k, g in zip(mid_names, _gather_wait(*mid[:4], after, "gather_mid_wait"))]

    def ffn_weights(k, after):
        t = ffn_names.index(k)
        return whole(k, _gather_wait(*[part[t:t + 1] for part in ffn[:4]], after, "gather_wait_" + k)[0])

    cw_parts, lb_parts = [], []
    for s in range(4):
        cw_s, lb_s = _unpack_small(small_in[2 * s], [w["conv_w"][0].shape, w["hg_lb"].shape])
        cw_parts.append(cw_s)
        lb_parts.append(lb_s)
    conv_w_full = jnp.concatenate(cw_parts, axis=1)
    hg_lb_full = jnp.concatenate(lb_parts, axis=2)

    started, held = {}, {}
    leaves_with_next = ("w_down", "w_xo", "w_xq")

    def on_grad(k, g):
        if g.ndim == 2:
            g = g.reshape(4, g.shape[0] // 4, g.shape[1])
        held[k] = g
        if k in leaves_with_next:
            return None
        names = tuple(held)
        per_tensor, token = _scatter_start([held.pop(n) for n in names], "grad_start_" + k)
        started.update(zip(names, per_tensor))
        return token

    gains = {k: w[k] for k in GAINS}
    loss_part, grad_x, gs = _local_step(x[0], mem[0], loss_target[0], w_in_t, ffn[4], mid_weights, ffn_weights, on_grad, gains,
                                        conv_w_full, w["conv_b"], hg_lb_full)
    gs["loss"] = loss_part

    grads, delta, new_m, new_v = {}, {}, {}, {}

    def sum_and_send(names, after, tag):
        sent, landed = _scatter_wait([started[k] for k in names], after, "grad_wait_" + tag)
        halves = [_sum_devices(g, land, chip, core, "grad_sum_" + k) for k, g, land in zip(names, sent, landed)]
        return _join_start(halves, "grad_join_start_" + tag)

    def joined(names, join, after, tag):
        for k, r in zip(names, _join_wait(*join[:3], after, "grad_join_wait_" + tag)):
            grads[k] = r.reshape(1, -1, r.shape[-1])

    def adamw(names):
        for k in names:
            shape = w[k].shape
            keep = len(shape) == 3 and shape[0] == 1
            if k in turned:
                view, back = (lambda a: jnp.swapaxes(a, 1, 2)), (lambda a: jnp.swapaxes(a, 1, 2))
                g = grads[k]
            else:
                view = (lambda a: a.reshape(shape)) if keep else (lambda a: a.reshape(-1, shape[-1]))
                back = lambda a: a.reshape(shape)
                g = view(grads[k])
            d, mo, vo, go = _adamw(view(w[k]), g, view(m[k]), view(v[k]), "adamw_" + k)
            delta[k], new_m[k], new_v[k], grads[k] = back(d), back(mo), back(vo), back(go)

    small_names = GAINS + ("conv_b", "conv_w", "hg_lb")
    packed = _pack_small([gs[k] for k in small_names + ("loss",)])
    small = _small_start(packed, "reduce_small_start", after=(grad_x,))

    ffn_join = sum_and_send(ffn_names, (grad_x, small[4]), "ffn")
    mid_join = sum_and_send(mid_names, (ffn_join[3],), "mid")
    joined(ffn_names, ffn_join, (mid_join[3],), "ffn")
    adamw(ffn_names)
    joined(mid_names, mid_join, tuple(new_v[k] for k in ffn_names), "mid")
    adamw(mid_names)
    early = mid_names + ffn_names
    w_in_join = sum_and_send(("w_in",), tuple(new_v[k] for k in early), "w_in")

    mine, others = _small_wait(*small[:4], (w_in_join[3],), "reduce_small_wait")
    reduced_small = _sum_small(mine, others, "reduce_small_sum")
    *summed, loss = _unpack_small(reduced_small, [gs[k].shape for k in small_names + ("loss",)])
    loss = loss[0, 0]
    for k, g in zip(small_names, summed):
        grads[k] = g
    ncw = w["conv_w"].shape[2]
    grads["conv_w"] = lax.dynamic_slice_in_dim(grads["conv_w"], chip * ncw, ncw, axis=1)[None]
    nlb = w["hg_lb"].shape[2]
    grads["hg_lb"] = lax.dynamic_slice_in_dim(grads["hg_lb"], chip * nlb, nlb, axis=2)
    replicated = GAINS + ("conv_b",)
    shapes = [w[k].shape for k in replicated]
    rows = sum(int(np.prod(s)) for s in shapes) // 128
    pack = lambda d: jnp.concatenate([d[k].reshape(-1) for k in replicated]).reshape(rows, 128)
    outs = _adamw(pack(w), reduced_small[:rows], pack(m), pack(v), "adamw_replicated")
    for into, packed_out in zip((delta, new_m, new_v, grads), outs):
        for k, a in zip(replicated, _unpack_small(packed_out, shapes)):
            into[k] = a
    adamw(("conv_w", "hg_lb"))

    joined(("w_in",), w_in_join, tuple(new_v[k] for k in small_names), "w_in")
    adamw(("w_in",))
    return (loss, grad_x[None], *[grads[k] for k in WEIGHTS], *[delta[k] for k in WEIGHTS],
            *[new_m[k] for k in WEIGHTS], *[new_v[k] for k in WEIGHTS])
```

```python
import numpy as np
import jax
import jax.numpy as jnp
from jax import lax
from jax.experimental import pallas as pl
from jax.experimental.pallas import tpu as pltpu

F32 = jnp.float32
MXU_DTYPE = jnp.bfloat16
WIRE_DTYPE = jnp.bfloat16
VMEM_LIMIT_BYTES = 56 * 1024 * 1024
ROWS_PER_16BIT_TILE = 16
ELEMENTWISE_ROWS = 256
EPS = 1e-6
MESH = pl.DeviceIdType.MESH

GRID_W = 64
ATT_HEADS, ATT_KV_HEADS, ATT_HEAD_DIM = 8, 2, 64
ATT_GROUP = ATT_HEADS // ATT_KV_HEADS
ATT_Q_DIM, ATT_KV_DIM = 512, 128
ROPE_THETA = 10000.0
HG_HEADS, HG_HEAD_DIM, HG_DIM = 4, 128, 512
HG_CHUNK = 128
HG_LEVELS = 7
HG_PAIR = 2 * HG_HEAD_DIM
HG_KEPT = 7
X_HEADS, X_HEAD_DIM = 4, 256
D_FF = 2816
FF_COLS = 256
FF_BLOCKS = D_FF // FF_COLS
OFF_AK, OFF_AV, OFF_HQ, OFF_ZF, OFF_ZB, OFF_HI, OFF_HG = 512, 640, 768, 1280, 1792, 2304, 2816

ADAM_LR, ADAM_B1, ADAM_B2, ADAM_EPS, ADAM_WD, ADAM_STEP = 0.001, 0.9, 0.999, 1e-08, 0.01, 10

SDS = jax.ShapeDtypeStruct


def _cp(*sem):
    return pltpu.CompilerParams(dimension_semantics=sem, vmem_limit_bytes=VMEM_LIMIT_BYTES)


def _row_tile(rows, cap):
    if rows <= cap:
        return rows
    return max(t for t in range(ROWS_PER_16BIT_TILE, cap + 1, ROWS_PER_16BIT_TILE) if rows % t == 0)


def _dot(a, b, form="nn"):
    dims = {"nn": (((1,), (0,)), ((), ())), "nt": (((1,), (1,)), ((), ())), "tn": (((0,), (0,)), ((), ()))}[form]
    return lax.dot_general(a.astype(MXU_DTYPE), b.astype(MXU_DTYPE), dims, preferred_element_type=F32)


def _sigmoid(x):
    return 1.0 / (1.0 + jnp.exp(-x))


def _rstd(x):
    return lax.rsqrt(jnp.mean(x * x, axis=-1, keepdims=True) + EPS)


def _rms_bwd(x, g, dy):
    r = _rstd(x)
    xh = x * r
    dn = dy * g
    dx = r * (dn - xh * jnp.mean(dn * xh, axis=-1, keepdims=True))
    return dx, jnp.sum(dy * xh, axis=0, keepdims=True)


def _unread(after):
    after = tuple(a for a in after if a is not None)
    return after, [pl.BlockSpec(memory_space=pl.ANY)] * len(after)


def _mm(a, b, form, out_dtype, tm, tn, name, after=()):
    after, after_specs = _unread(after)
    if form == "nn":
        (m, k), n = a.shape, b.shape[1]
    elif form == "nt":
        (m, k), n = a.shape, b.shape[0]
    else:
        (k, m), n = a.shape, b.shape[1]
    tm, tn = min(tm, m), min(tn, n)
    assert m % tm == 0 and n % tn == 0, (name, m, n, tm, tn)

    def body(a_ref, b_ref, *rest):
        o_ref = rest[-1]
        o_ref[...] = _dot(a_ref[...], b_ref[...], form).astype(o_ref.dtype)

    a_spec = pl.BlockSpec((k, tm), lambda i, j: (0, i)) if form == "tn" else pl.BlockSpec((tm, k), lambda i, j: (i, 0))
    b_spec = pl.BlockSpec((tn, k), lambda i, j: (j, 0)) if form == "nt" else pl.BlockSpec((k, tn), lambda i, j: (0, j))
    return pl.pallas_call(
        body, name=name, grid=(m // tm, n // tn), in_specs=[a_spec, b_spec] + after_specs,
        out_specs=pl.BlockSpec((tm, tn), lambda i, j: (i, j)), out_shape=SDS((m, n), out_dtype),
        compiler_params=_cp("parallel", "parallel"))(a, b, *after)


def _mm_nt_parts(a_parts, b, out_dtype, tm, tn, name, after=()):
    after, after_specs = _unread(after)
    parts, n, p = b.shape
    m = a_parts[0][0].shape[0]
    tm, tn = min(tm, m), min(tn, n)
    assert m % tm == 0 and n % tn == 0 and len(a_parts) == parts, (name, m, b.shape)

    def body(*refs):
        o_ref = refs[-1]
        acc = _dot(refs[0][...], refs[parts][0], "nt")
        for s in range(1, parts):
            acc = acc + _dot(refs[s][...], refs[parts + s][0], "nt")
        o_ref[...] = acc.astype(o_ref.dtype)

    a_specs = [pl.BlockSpec((tm, p), lambda i, j, cb=cb: (i, cb)) for _, cb in a_parts]
    b_specs = [pl.BlockSpec((1, tn, p), lambda i, j, s=s: (s, j, 0)) for s in range(parts)]
    return pl.pallas_call(
        body, name=name, grid=(m // tm, n // tn), in_specs=a_specs + b_specs + after_specs,
        out_specs=pl.BlockSpec((tm, tn), lambda i, j: (i, j)), out_shape=SDS((m, n), out_dtype),
        compiler_params=_cp("parallel", "parallel"))(*[arr for arr, _ in a_parts], *([b] * parts), *after)


def _norm_bwd_mm(y, g, d, w, out_dtype, tm, tn, name):
    n, dm = y.shape
    nn = w.shape[0]
    tm, tn = min(tm, n), min(tn, nn)
    assert n % tm == 0 and nn % tn == 0 and w.shape[1] == dm, (name, y.shape, w.shape)

    def body(y_ref, g_ref, d_ref, w_ref, dx_ref, dy_ref, dg_ref, dys):
        i, j = pl.program_id(0), pl.program_id(1)

        @pl.when(jnp.logical_and(i == 0, j == 0))
        def _():
            dg_ref[...] = jnp.zeros_like(dg_ref)

        @pl.when(j == 0)
        def _():
            dy, dg = _rms_bwd(y_ref[...], g_ref[...], d_ref[...])
            dy = dy.astype(MXU_DTYPE)
            dys[...] = dy
            dy_ref[...] = dy
            dg_ref[...] += dg

        dx_ref[...] = _dot(dys[...], w_ref[...], "nt").astype(dx_ref.dtype)

    row = pl.BlockSpec((tm, dm), lambda i, j: (i, 0))
    vec = pl.BlockSpec((1, dm), lambda i, j: (0, 0))
    return pl.pallas_call(
        body, name=name, grid=(n // tm, nn // tn), in_specs=[row, vec, row, pl.BlockSpec((tn, dm), lambda i, j: (j, 0))],
        out_specs=[pl.BlockSpec((tm, tn), lambda i, j: (i, j)), row, vec],
        out_shape=[SDS((n, nn), out_dtype), SDS((n, dm), MXU_DTYPE), SDS((1, dm), F32)],
        scratch_shapes=[pltpu.VMEM((tm, dm), MXU_DTYPE)],
        compiler_params=_cp("arbitrary", "arbitrary"))(y, g, d, w)


def _mm_resid_norm(a, b, x, g, tm, name, target=None):
    n, k = a.shape
    d = b.shape[1]
    tm = min(tm, n)
    assert n % tm == 0 and x.shape == (n, d), (name, a.shape, b.shape)
    with_loss = target is not None

    def body(a_ref, b_ref, x_ref, g_ref, *rest):
        y = _dot(a_ref[...], b_ref[...])
        out = x_ref[...] + y * _rstd(y) * g_ref[...]
        if not with_loss:
            y_ref, o_ref = rest
            y_ref[...] = y
            o_ref[...] = out
            return
        t_ref, y_ref, d_ref, l_ref = rest
        y_ref[...] = y
        diff = out - t_ref[...]
        d_ref[...] = diff * (1.0 / d)

        @pl.when(pl.program_id(0) == 0)
        def _():
            l_ref[...] = jnp.zeros_like(l_ref)

        l_ref[...] += 0.5 * jnp.sum(jnp.mean(diff * diff, axis=-1, keepdims=True), axis=0, keepdims=True)

    row = pl.BlockSpec((tm, d), lambda i: (i, 0))
    ins = [pl.BlockSpec((tm, k), lambda i: (i, 0)), pl.BlockSpec((k, d), lambda i: (0, 0)), row, pl.BlockSpec((1, d), lambda i: (0, 0))]
    out = SDS((n, d), F32)
    if with_loss:
        return pl.pallas_call(body, name=name, grid=(n // tm,), in_specs=ins + [row], out_specs=[row, row, pl.BlockSpec((1, 1), lambda i: (0, 0))],
                              out_shape=[out, out, SDS((1, 1), F32)], compiler_params=_cp("arbitrary"))(a, b, x, g, target)
    return pl.pallas_call(body, name=name, grid=(n // tm,), in_specs=ins, out_specs=[row, row], out_shape=[out, out],
                          compiler_params=_cp("parallel"))(a, b, x, g)


def _dx_norm_bwd(a_parts, b, x, g, res, tm, name, after=(), b_turned=False):
    after, after_specs = _unread(after)
    parts, d, p = (b.shape[0], b.shape[2], b.shape[1]) if b_turned else b.shape
    form = "nn" if b_turned else "nt"
    n = x.shape[0]
    tm = min(tm, n)
    assert n % tm == 0 and len(a_parts) == parts and x.shape[1] == d, (name, x.shape, b.shape)

    def body(*refs):
        x_ref, g_ref, res_ref = refs[2 * parts:2 * parts + 3]
        dx_ref, dg_ref = refs[-2:]
        dh = _dot(refs[0][...], refs[parts][0], form)
        for s in range(1, parts):
            dh = dh + _dot(refs[s][...], refs[parts + s][0], form)
        dx, dg = _rms_bwd(x_ref[...], g_ref[...], dh)
        dx_ref[...] = dx + res_ref[...]

        @pl.when(pl.program_id(0) == 0)
        def _():
            dg_ref[...] = jnp.zeros_like(dg_ref)

        dg_ref[...] += dg

    a_specs = [pl.BlockSpec((tm, p), lambda i, cb=cb: (i, cb)) for _, cb in a_parts]
    b_specs = [pl.BlockSpec((1,) + b.shape[1:], lambda i, s=s: (s, 0, 0)) for s in range(parts)]
    row = pl.BlockSpec((tm, d), lambda i: (i, 0))
    vec = pl.BlockSpec((1, d), lambda i: (0, 0))
    return pl.pallas_call(
        body, name=name, grid=(n // tm,), in_specs=a_specs + b_specs + [row, vec, row] + after_specs,
        out_specs=[row, vec], out_shape=[SDS((n, d), F32), SDS((1, d), F32)],
        compiler_params=_cp("arbitrary"))(*[arr for arr, _ in a_parts], *([b] * parts), x, g, res, *after)


def _dw_by_owner(a, b, tn, first, into, tm, name):
    k, m = a.shape
    cnt = b.shape[1] // tn
    tm = min(tm, m)
    assert m % tm == 0 and b.shape[1] == cnt * tn and first + cnt <= 4, (name, a.shape, b.shape)

    def body(a_ref, b_ref, *rest):
        rest[-1][0] = _dot(a_ref[...], b_ref[...], "tn").astype(rest[-1].dtype)

    extra = [] if into is None else [into]
    return pl.pallas_call(
        body, name=name, grid=(m // tm, cnt),
        in_specs=[pl.BlockSpec((k, tm), lambda i, j: (0, i)), pl.BlockSpec((k, tn), lambda i, j: (0, j))] + [pl.BlockSpec(memory_space=pl.ANY)] * len(extra),
        out_specs=pl.BlockSpec((1, tm, tn), lambda i, j: (first + j, i, 0)), out_shape=SDS((4, m, tn), WIRE_DTYPE),
        input_output_aliases={2: 0} if extra else {},
        compiler_params=_cp("parallel", "parallel"))(a, b, *extra)


def _norm_mm(x, g, w, out_dtype, tm, tn, name, after=(), w_turned=False):
    after, after_specs = _unread(after)
    m, d = x.shape
    sharded = w.ndim == 3
    n = w.shape[0] if w_turned else w.shape[-1] * (w.shape[0] if sharded else 1)
    tm, tn = min(tm, m), (w.shape[-1] if sharded else min(tn, n))
    assert m % tm == 0 and n % tn == 0 and not (sharded and w_turned), (name, m, n, tm, tn)

    def body(x_ref, g_ref, w_ref, *rest):
        o_ref, h_ref, hs = rest[-3:]

        @pl.when(pl.program_id(1) == 0)
        def _():
            xv = x_ref[...]
            h = (xv * _rstd(xv) * g_ref[...]).astype(MXU_DTYPE)
            hs[...] = h
            h_ref[...] = h

        o_ref[...] = _dot(hs[...], w_ref[0] if sharded else w_ref[...], "nt" if w_turned else "nn").astype(o_ref.dtype)

    if w_turned:
        w_spec = pl.BlockSpec((tn, d), lambda i, j: (j, 0))
    else:
        w_spec = pl.BlockSpec((1, d, tn), lambda i, j: (j, 0, 0)) if sharded else pl.BlockSpec((d, tn), lambda i, j: (0, j))
    return pl.pallas_call(
        body, name=name, grid=(m // tm, n // tn),
        in_specs=[pl.BlockSpec((tm, d), lambda i, j: (i, 0)), pl.BlockSpec((1, d), lambda i, j: (0, 0)), w_spec] + after_specs,
        out_specs=[pl.BlockSpec((tm, tn), lambda i, j: (i, j)), pl.BlockSpec((tm, d), lambda i, j: (i, 0))],
        out_shape=[SDS((m, n), out_dtype), SDS((m, d), MXU_DTYPE)],
        scratch_shapes=[pltpu.VMEM((tm, d), MXU_DTYPE)],
        compiler_params=_cp("parallel", "arbitrary"))(x, g, w, *after)


ROW_TILE = 512
TOKEN_TILE = 1024


def _norm_bwd(x, g, dy, res, out_dtype, name):
    n, d = x.shape
    tr = min(ROW_TILE, n)
    has_res = res is not None

    def body(*refs):
        x_ref, g_ref, dy_ref = refs[:3]
        dx_ref, dg_ref = refs[-2:]
        dx, dg = _rms_bwd(x_ref[...], g_ref[...], dy_ref[...].astype(F32))
        if has_res:
            dx = dx + refs[3][...]
        dx_ref[...] = dx.astype(dx_ref.dtype)

        @pl.when(pl.program_id(0) == 0)
        def _():
            dg_ref[...] = jnp.zeros_like(dg_ref)

        dg_ref[...] += dg

    row = pl.BlockSpec((tr, d), lambda i: (i, 0))
    vec = pl.BlockSpec((1, d), lambda i: (0, 0))
    ins = [x, g, dy] + ([res] if has_res else [])
    return pl.pallas_call(
        body, name=name, grid=(n // tr,), in_specs=[row, vec, row] + ([row] if has_res else []),
        out_specs=[row, vec], out_shape=[SDS((n, d), out_dtype), SDS((1, d), F32)],
        compiler_params=_cp("arbitrary"))(*ins)


def _rope_tables(n):
    pairs = ATT_HEAD_DIM // 4
    t = np.arange(n)
    inv = np.power(ROPE_THETA, -np.arange(pairs, dtype=np.float32) / pairs).astype(np.float32)
    ang = np.concatenate([(t // GRID_W)[:, None].astype(np.float32) * inv, (t % GRID_W)[:, None].astype(np.float32) * inv], axis=-1)
    cos = np.repeat(np.cos(ang), 2, axis=-1)
    sin = np.repeat(np.sin(ang), 2, axis=-1) * np.tile(np.array([-1.0, 1.0], np.float32), ATT_HEAD_DIM // 2)
    return jnp.asarray(np.tile(cos, 2), F32), jnp.asarray(np.tile(sin, 2), F32)


def _swap_pairs(x):
    lane = lax.broadcasted_iota(jnp.int32, x.shape, 1)
    return jnp.where((lane & 1) == 0, pltpu.roll(x, 127, axis=1), pltpu.roll(x, 1, axis=1))


def _head_mean(v):
    lane = lax.broadcasted_iota(jnp.int32, v.shape, 1)
    lo = jnp.where(lane < ATT_HEAD_DIM, v, 0.0)
    s0 = jnp.sum(lo, axis=-1, keepdims=True)
    s1 = jnp.sum(v - lo, axis=-1, keepdims=True)
    return jnp.where(lane < ATT_HEAD_DIM, s0, s1) * (1.0 / ATT_HEAD_DIM)


def _qk_prep(p, gq, gk, cos, sin, name):
    n = p.shape[0]
    tr = min(ROW_TILE, n)

    def one(xv, g, c, s):
        xn = xv * lax.rsqrt(_head_mean(xv * xv) + EPS) * g
        return xn * c + _swap_pairs(xn) * s

    def body(q_ref, k_ref, gq_ref, gk_ref, c_ref, s_ref, qo_ref, ko_ref):
        c, s = c_ref[...], s_ref[...]
        for j in range(ATT_Q_DIM // 128):
            qo_ref[:, j * 128:(j + 1) * 128] = one(q_ref[:, j * 128:(j + 1) * 128], gq_ref[...], c, s).astype(qo_ref.dtype)
        ko_ref[...] = one(k_ref[...], gk_ref[...], c, s).astype(ko_ref.dtype)

    vec = pl.BlockSpec((1, 128), lambda i: (0, 0))
    tab = pl.BlockSpec((tr, 128), lambda i: (i, 0))
    return pl.pallas_call(
        body, name=name, grid=(n // tr,),
        in_specs=[pl.BlockSpec((tr, ATT_Q_DIM), lambda i: (i, 0)), pl.BlockSpec((tr, 128), lambda i: (i, OFF_AK // 128)), vec, vec, tab, tab],
        out_specs=[pl.BlockSpec((tr, ATT_Q_DIM), lambda i: (i, 0)), tab],
        out_shape=[SDS((n, ATT_Q_DIM), MXU_DTYPE), SDS((n, ATT_KV_DIM), MXU_DTYPE)],
        compiler_params=_cp("parallel"))(p, p, gq, gk, cos, sin)


def _qk_prep_bwd(p, gq, gk, cos, sin, dq, dk, name):
    n = p.shape[0]
    tr = min(ROW_TILE, n)

    def one(xv, g, c, s, dout):
        dxn = dout * c + _swap_pairs(dout * s)
        r = lax.rsqrt(_head_mean(xv * xv) + EPS)
        xh = xv * r
        dn = dxn * g
        dx = r * (dn - xh * _head_mean(dn * xh))
        return dx, jnp.sum(dxn * xh, axis=0, keepdims=True)

    def body(q_ref, k_ref, gq_ref, gk_ref, c_ref, s_ref, dq_ref, dk_ref, dqo_ref, dko_ref, dgq_ref, dgk_ref):
        @pl.when(pl.program_id(0) == 0)
        def _():
            dgq_ref[...] = jnp.zeros_like(dgq_ref)
            dgk_ref[...] = jnp.zeros_like(dgk_ref)

        c, s = c_ref[...], s_ref[...]
        for j in range(ATT_Q_DIM // 128):
            sl = slice(j * 128, (j + 1) * 128)
            dx, dg = one(q_ref[:, sl], gq_ref[...], c, s, dq_ref[:, sl])
            dqo_ref[:, sl] = dx.astype(dqo_ref.dtype)
            dgq_ref[:, sl] += dg
        dx, dg = one(k_ref[...], gk_ref[...], c, s, dk_ref[...])
        dko_ref[...] = dx.astype(dko_ref.dtype)
        dgk_ref[...] += dg

    vec = pl.BlockSpec((1, 128), lambda i: (0, 0))
    tab = pl.BlockSpec((tr, 128), lambda i: (i, 0))
    qrow = pl.BlockSpec((tr, ATT_Q_DIM), lambda i: (i, 0))
    return pl.pallas_call(
        body, name=name, grid=(n // tr,),
        in_specs=[qrow, pl.BlockSpec((tr, 128), lambda i: (i, OFF_AK // 128)), vec, vec, tab, tab, qrow, tab],
        out_specs=[qrow, tab, pl.BlockSpec((1, ATT_Q_DIM), lambda i: (0, 0)), vec],
        out_shape=[SDS((n, ATT_Q_DIM), MXU_DTYPE), SDS((n, ATT_KV_DIM), MXU_DTYPE), SDS((1, ATT_Q_DIM), F32), SDS((1, 128), F32)],
        compiler_params=_cp("arbitrary"))(p, p, gq, gk, cos, sin, dq, dk)


ATT_FWD_STEP = (256, 4)
ATT_BWD_STEP = (512, 2)


def _attn_fwd(q, k, v, name):
    n = q.shape[0]
    tq, step_heads = min(ATT_FWD_STEP[0], n), ATT_FWD_STEP[1]
    scale = ATT_HEAD_DIM ** -0.5
    gw = step_heads * ATT_HEAD_DIM
    parts = ATT_GROUP // step_heads

    def body(q_ref, k_ref, v_ref, o_ref):
        kk, vv = k_ref[0], v_ref[0]
        v_ones = jnp.concatenate([vv, jnp.ones_like(vv)], axis=1)
        outs = []
        for g in range(step_heads):
            s = _dot(q_ref[:, g * ATT_HEAD_DIM:(g + 1) * ATT_HEAD_DIM] * scale, kk, "nt")
            e = jnp.exp(s - jnp.max(s, axis=-1, keepdims=True))
            ov = _dot(e, v_ones)
            outs.append(ov[:, :ATT_HEAD_DIM] / ov[:, ATT_HEAD_DIM:])
        o_ref[...] = jnp.concatenate(outs, axis=-1).astype(o_ref.dtype)

    kv = pl.BlockSpec((1, n, ATT_HEAD_DIM), lambda h, i, pr: (h, 0, 0))
    qb = pl.BlockSpec((tq, gw), lambda h, i, pr: (i, h * parts + pr))
    return pl.pallas_call(
        body, name=name, grid=(ATT_KV_HEADS, n // tq, parts), in_specs=[qb, kv, kv],
        out_specs=qb, out_shape=SDS((n, ATT_Q_DIM), MXU_DTYPE),
        compiler_params=_cp("parallel", "parallel", "parallel"))(q, k, v)


def _attn_bwd(q, k, v, o, do, name):
    n = q.shape[0]
    tq, step_heads = min(ATT_BWD_STEP[0], n), ATT_BWD_STEP[1]
    scale = ATT_HEAD_DIM ** -0.5
    gw = step_heads * ATT_HEAD_DIM
    parts = ATT_GROUP // step_heads

    def body(q_ref, k_ref, v_ref, o_ref, do_ref, dq_ref, dk_ref, dv_ref):
        @pl.when(jnp.logical_and(pl.program_id(1) == 0, pl.program_id(2) == 0))
        def _():
            dk_ref[...] = jnp.zeros_like(dk_ref)
            dv_ref[...] = jnp.zeros_like(dv_ref)

        kk, vv = k_ref[0], v_ref[0]
        dqs = []
        dk_acc = jnp.zeros((ATT_HEAD_DIM, n), F32)
        dv_acc = jnp.zeros((ATT_HEAD_DIM, n), F32)
        for g in range(step_heads):
            sl = slice(g * ATT_HEAD_DIM, (g + 1) * ATT_HEAD_DIM)
            qg, dog = q_ref[:, sl] * scale, do_ref[:, sl].astype(F32)
            s = _dot(qg, kk, "nt")
            e = jnp.exp(s - jnp.max(s, axis=-1, keepdims=True))
            inv = 1.0 / jnp.sum(e, axis=-1, keepdims=True)
            delta = jnp.sum(dog * o_ref[:, sl].astype(F32), axis=-1, keepdims=True)
            dse = e * (_dot(dog, vv, "nt") - delta)
            dqs.append(_dot(dse, kk) * (inv * scale))
            dk_acc += _dot(qg.astype(F32) * inv, dse, "tn")
            dv_acc += _dot(dog * inv, e, "tn")
        dq_ref[...] = jnp.concatenate(dqs, axis=-1)
        dk_ref[0] += dk_acc
        dv_ref[0] += dv_acc

    kv = pl.BlockSpec((1, n, ATT_HEAD_DIM), lambda h, i, pr: (h, 0, 0))
    kvt = pl.BlockSpec((1, ATT_HEAD_DIM, n), lambda h, i, pr: (h, 0, 0))
    qb = pl.BlockSpec((tq, gw), lambda h, i, pr: (i, h * parts + pr))
    return pl.pallas_call(
        body, name=name, grid=(ATT_KV_HEADS, n // tq, parts), in_specs=[qb, kv, kv, qb, qb], out_specs=[qb, kvt, kvt],
        out_shape=[SDS((n, ATT_Q_DIM), F32), SDS((ATT_KV_HEADS, ATT_HEAD_DIM, n), F32), SDS((ATT_KV_HEADS, ATT_HEAD_DIM, n), F32)],
        compiler_params=_cp("parallel", "arbitrary", "arbitrary"))(q, k, v, o, do)


def _both_directions(mats, axis):
    fwd = np.concatenate(mats, axis=axis).astype(np.float32)
    bwd = np.concatenate([m[::-1, ::-1] for m in mats], axis=axis).astype(np.float32)
    return jnp.asarray(np.stack([fwd, bwd]), MXU_DTYPE)


def _hg_segments():
    c = HG_CHUNK
    t = np.arange(c)[:, None]
    r = np.arange(c)[None, :]
    mats = [(r <= t)]
    for lev in range(HG_LEVELS):
        h = c >> (lev + 1)
        mid = (t // (2 * h)) * (2 * h) + h - 1
        hi = (t // h) % 2 == 1
        mats.append(np.where(hi, (r > mid) & (r <= t), (r > t) & (r <= mid)))
    mats.append(r > t)
    return _both_directions(mats, 0)


def _hg_pair_sums():
    c = HG_CHUNK
    r = np.arange(c)[:, None]
    t = np.arange(c)[None, :]
    gp, gn = [t >= r], [t < r]
    for lev in range(HG_LEVELS):
        sh = HG_LEVELS - 1 - lev
        same = (r >> sh) == (t >> sh)
        gp.append(same & (t >= r))
        gn.append(same & (t < r))
    return _both_directions(gp, 1), _both_directions(gn, 1)


def _split_dot(mat, x):
    hi = x.astype(MXU_DTYPE)
    lo = (x - hi.astype(F32)).astype(MXU_DTYPE)
    return _dot(mat, hi) + _dot(mat, lo)


def _hg_gates(hq, z, a0, a1):
    q = hq * _sigmoid(hq)
    sg = _sigmoid(z)
    lb = _sigmoid(a0 - a1)
    f = lb + (1.0 - lb) * sg
    k = (1.0 - lb) * (1.0 - sg)
    return q, f, k, sg, lb


def _hg_level_masks():
    c = HG_CHUNK
    t = np.arange(c)
    later, same = [], []
    for lev in range(HG_LEVELS):
        sh = HG_LEVELS - 1 - lev
        later.append(np.broadcast_to((((t >> sh) & 1) == 1)[:, None], (c, HG_HEAD_DIM)))
        same.append((t[:, None] >> (sh + 1)) == (t[None, :] >> (sh + 1)))
    same.append(t[:, None] == t[None, :])
    later = np.stack(later).astype(np.float32)
    return jnp.asarray(np.stack([later, 1.0 - later]), F32), jnp.asarray(np.stack(same).astype(np.float32), F32)


def _hg_level(q, k, ex, later_ref, lev):
    e = ex[lev + 1]
    e_q = e * later_ref[0, lev]
    e_k = e - e_q
    return q * e_q, k * e_k, e_q, e_k


def _hg_intra(q, k, ex, later_ref, same_ref):
    a = same_ref[HG_LEVELS] * jnp.sum(q * k, axis=-1, keepdims=True)
    for lev in range(HG_LEVELS):
        qs, ks, _, _ = _hg_level(q, k, ex, later_ref, lev)
        a = a + same_ref[lev] * _dot(qs, ks, "nt")
    return a


def _hg_specs(n, with_time):
    c = HG_CHUNK
    nc = n // c

    def chunk(d, i):
        first = d if with_time else 1 - d
        return i + first * (nc - 1 - 2 * i)

    def pcols(off, dir_stride=0):
        return [pl.BlockSpec((c, HG_PAIR), lambda d, i, j=j: (chunk(d, i), off // HG_PAIR + dir_stride // HG_PAIR * d + j)) for j in range(2)]

    specs = dict(
        hq=pcols(OFF_HQ), v=pcols(OFF_HI), z=pcols(OFF_ZF, OFF_ZB - OFF_ZF),
        shared=pl.BlockSpec((c, HG_DIM), lambda d, i: (chunk(d, i), 0)),
        per_dir=pl.BlockSpec((1, c, HG_DIM), lambda d, i: (d, chunk(d, i), 0)),
        vec=pl.BlockSpec((1, 1, HG_DIM), lambda d, i: (d, 0, 0)),
        seg=pl.BlockSpec((1, (HG_LEVELS + 2) * c, c), lambda d, i: (d, 0, 0)),
        sums=pl.BlockSpec((1, c, (HG_LEVELS + 1) * c), lambda d, i: (d, 0, 0)),
        later=pl.BlockSpec((1, HG_LEVELS, c, HG_HEAD_DIM), lambda d, i: (d, 0, 0, 0)),
        same=pl.BlockSpec((HG_LEVELS + 1, c, c), lambda d, i: (0, 0, 0)),
        state=pl.BlockSpec((1, HG_HEADS, 1, HG_HEAD_DIM, HG_HEAD_DIM), lambda d, i: (d, 0, chunk(d, i), 0, 0)),
        weights=pl.BlockSpec((1, HG_HEADS, 1, c, c), lambda d, i: (d, 0, chunk(d, i), 0, 0)),
        levels=pl.BlockSpec((1, HG_HEADS, 1, HG_LEVELS, c, HG_HEAD_DIM), lambda d, i: (d, 0, chunk(d, i), 0, 0, 0)),
        kept=pl.BlockSpec((1, HG_KEPT, c, HG_DIM), lambda d, i: (d, 0, chunk(d, i), 0)))
    return nc, specs


def _hg_head(refs, hh):
    off = (hh % 2) * HG_HEAD_DIM
    return refs[hh // 2][:, off:off + HG_HEAD_DIM]


def _hg_lanes(hh):
    return slice(hh * HG_HEAD_DIM, (hh + 1) * HG_HEAD_DIM)


def _hg_exps(seg_ref, f):
    c = HG_CHUNK
    args = _split_dot(seg_ref[0], jnp.log(f))
    return [jnp.exp(args[j * c:(j + 1) * c]) for j in range(HG_LEVELS + 2)]


def _hg_last_row(a, mirrored):
    return jnp.where(mirrored, a[0:1, :], a[HG_CHUNK - 1:HG_CHUNK, :])


def _hgrn_fwd(p, a0, a1, seg, masks, name):
    n = p.shape[0]
    nc, sp = _hg_specs(n, True)

    def body(hq0, hq1, z0, z1, v0, v1, a0_ref, a1_ref, seg_ref, later_ref, same_ref, o_ref, s0_ref, a_ref, e_ref, g_ref, st):
        @pl.when(pl.program_id(1) == 0)
        def _():
            st[...] = jnp.zeros_like(st)

        mirrored = pl.program_id(0) == 1
        for hh in range(HG_HEADS):
            ln = _hg_lanes(hh)
            hqv = _hg_head((hq0, hq1), hh)
            q, f, k, sg, _ = _hg_gates(hqv, _hg_head((z0, z1), hh), a0_ref[0, :, ln], a1_ref[0, :, ln])
            vv = _hg_head((v0, v1), hh)
            ex = _hg_exps(seg_ref, f)
            for lev in range(HG_LEVELS):
                e_ref[0, hh, 0, lev] = ex[lev + 1].astype(e_ref.dtype)
            sq = _sigmoid(hqv)
            for j, kept in enumerate((q, k, f, sg, sq * (1.0 + hqv * (1.0 - sq)), ex[0], ex[HG_LEVELS + 1])):
                g_ref[0, j, :, ln] = kept
            a = _hg_intra(q, k, ex, later_ref, same_ref).astype(MXU_DTYPE)
            a_ref[0, hh, 0] = a
            s_t = st[hh]
            s0_ref[0, hh, 0] = s_t
            o_ref[0, :, ln] = _dot(a, vv) + _dot(q * ex[0], s_t, "nt")
            st[hh] = s_t * _hg_last_row(ex[0], mirrored) + _dot(vv, k * ex[HG_LEVELS + 1], "tn")

    return pl.pallas_call(
        body, name=name, grid=(2, nc), in_specs=sp["hq"] + sp["z"] + sp["v"] + [sp["vec"], sp["vec"], sp["seg"], sp["later"], sp["same"]],
        out_specs=[sp["per_dir"], sp["state"], sp["weights"], sp["levels"], sp["kept"]],
        out_shape=[SDS((2, n, HG_DIM), F32), SDS((2, HG_HEADS, nc, HG_HEAD_DIM, HG_HEAD_DIM), F32),
                   SDS((2, HG_HEADS, nc, HG_CHUNK, HG_CHUNK), MXU_DTYPE),
                   SDS((2, HG_HEADS, nc, HG_LEVELS, HG_CHUNK, HG_HEAD_DIM), MXU_DTYPE), SDS((2, HG_KEPT, n, HG_DIM), F32)],
        scratch_shapes=[pltpu.VMEM((HG_HEADS, HG_HEAD_DIM, HG_HEAD_DIM), F32)],
        compiler_params=_cp("parallel", "arbitrary"))(p, p, p, p, p, p, a0, a1, seg, *masks)


def _hgrn_bwd(p, a0, a1, masks, gp, gn, do, s0, a, e, kept, name):
    n = p.shape[0]
    nc, sp = _hg_specs(n, False)


    def body(v0, v1, a0_ref, a1_ref, later_ref, same_ref, gp_ref, gn_ref, do_ref, s0_ref, a_ref, e_ref, g_ref,
             dhq_ref, dz_ref, dv_ref, dlb_ref, rt):
        @pl.when(pl.program_id(1) == 0)
        def _():
            rt[...] = jnp.zeros_like(rt)
            dlb_ref[...] = jnp.zeros_like(dlb_ref)

        mirrored = pl.program_id(0) == 1
        for hh in range(HG_HEADS):
            ln = _hg_lanes(hh)
            q, k, f, sg, dsilu, e_first, e_last = (g_ref[0, j, :, ln] for j in range(HG_KEPT))
            lb = _sigmoid(a0_ref[0, :, ln] - a1_ref[0, :, ln])
            vv, dov = _hg_head((v0, v1), hh), do_ref[:, ln]
            ex = [e_first] + [e_ref[0, hh, 0, lev].astype(F32) for lev in range(HG_LEVELS)] + [e_last]
            a = a_ref[0, hh, 0]
            da = _dot(dov, vv, "nt")
            diag = jnp.sum(dov * vv, axis=-1, keepdims=True)
            s_t = s0_ref[0, hh, 0]
            r_t = rt[hh]
            k_end = k * ex[HG_LEVELS + 1]
            dv_ref[0, :, ln] = _dot(a, dov, "tn") + _dot(k_end, r_t, "nt")
            dq_inter = ex[0] * _dot(dov, s_t)
            dk_inter = ex[HG_LEVELS + 1] * _dot(vv, r_t)
            dq = diag * k + dq_inter
            dk = diag * q + dk_inter
            q_terms, k_terms = [q * dq_inter], [k * dk_inter]
            for lev in range(HG_LEVELS):
                qs, ks, e_q, e_k = _hg_level(q, k, ex, later_ref, lev)
                pairs = da * same_ref[lev]
                q_part = e_q * _dot(pairs, ks)
                k_part = e_k * _dot(pairs, qs, "tn")
                dq, dk = dq + q_part, dk + k_part
                q_terms.append(q * q_part)
                k_terms.append(k * k_part)
            decay = _hg_last_row(ex[0], mirrored)
            rt[hh] = r_t * decay + _dot(dov, q * ex[0], "tn")
            later = decay * jnp.sum(s_t * r_t, axis=0, keepdims=True)
            dlf = _dot(gp_ref[0], jnp.concatenate(q_terms, axis=0)) + _dot(gn_ref[0], jnp.concatenate(k_terms, axis=0)) + later
            df = dlf / f - dk
            dz_ref[0, :, ln] = df * (1.0 - lb) * sg * (1.0 - sg)
            dlb_ref[0, :, ln] += jnp.sum(df * (1.0 - sg), axis=0, keepdims=True)
            dhq_ref[0, :, ln] = dq * dsilu

    out = SDS((2, n, HG_DIM), F32)
    return pl.pallas_call(
        body, name=name, grid=(2, nc),
        in_specs=sp["v"] + [sp["vec"], sp["vec"], sp["later"], sp["same"], sp["sums"], sp["sums"],
                            sp["shared"], sp["state"], sp["weights"], sp["levels"], sp["kept"]],
        out_specs=[sp["per_dir"], sp["per_dir"], sp["per_dir"], sp["vec"]], out_shape=[out, out, out, SDS((2, 1, HG_DIM), F32)],
        scratch_shapes=[pltpu.VMEM((HG_HEADS, HG_HEAD_DIM, HG_HEAD_DIM), F32)],
        compiler_params=_cp("parallel", "arbitrary"))(p, p, a0, a1, *masks, gp, gn, do, s0, a, e, kept)


def _hg_post(o2, p, g, name):
    n = p.shape[0]
    tr = min(ROW_TILE, n)
    w = 2 * HG_HEAD_DIM

    def body(of_ref, ob_ref, hg_ref, g_ref, o_ref):
        for j in range(2):
            sl = slice(j * HG_HEAD_DIM, (j + 1) * HG_HEAD_DIM)
            o = of_ref[0, :, sl] + ob_ref[0, :, sl]
            hg = hg_ref[:, sl]
            o_ref[:, sl] = (o * _rstd(o) * g_ref[...] * (hg * _sigmoid(hg))).astype(o_ref.dtype)

    blk = pl.BlockSpec((tr, w), lambda i, j: (i, j))
    dirs = [pl.BlockSpec((1, tr, w), lambda i, j, d=d: (d, i, j)) for d in range(2)]
    return pl.pallas_call(
        body, name=name, grid=(n // tr, HG_DIM // w),
        in_specs=dirs + [pl.BlockSpec((tr, w), lambda i, j: (i, OFF_HG // w + j)), pl.BlockSpec((1, HG_HEAD_DIM), lambda i, j: (0, 0))],
        out_specs=blk, out_shape=SDS((n, HG_DIM), MXU_DTYPE), compiler_params=_cp("parallel", "parallel"))(o2, o2, p, g)


def _hg_post_bwd(o2, p, g, dcat, name, after=()):
    n = p.shape[0]
    tr = min(ROW_TILE, n)
    w = 2 * HG_HEAD_DIM
    after, after_specs = _unread(after)

    def body(of_ref, ob_ref, hg_ref, g_ref, d_ref, *rest):
        do_ref, dhg_ref, dg_ref = rest[len(after):]

        @pl.when(pl.program_id(1) == 0)
        def _():
            dg_ref[...] = jnp.zeros_like(dg_ref)

        for j in range(2):
            sl = slice(j * HG_HEAD_DIM, (j + 1) * HG_HEAD_DIM)
            o = of_ref[0, :, sl] + ob_ref[0, :, sl]
            hg = hg_ref[:, sl]
            d = d_ref[:, sl].astype(F32)
            sg = _sigmoid(hg)
            on = o * _rstd(o) * g_ref[...]
            dhg_ref[:, sl] = (d * on * sg * (1.0 + hg * (1.0 - sg))).astype(dhg_ref.dtype)
            dx, dg = _rms_bwd(o, g_ref[...], d * hg * sg)
            do_ref[:, sl] = dx
            dg_ref[0, :, sl] += dg

    blk = pl.BlockSpec((tr, w), lambda j, i: (i, j))
    dirs = [pl.BlockSpec((1, tr, w), lambda j, i, d=d: (d, i, j)) for d in range(2)]
    return pl.pallas_call(
        body, name=name, grid=(HG_DIM // w, n // tr),
        in_specs=dirs + [pl.BlockSpec((tr, w), lambda j, i: (i, OFF_HG // w + j)), pl.BlockSpec((1, HG_HEAD_DIM), lambda j, i: (0, 0)),
                         pl.BlockSpec((tr, w), lambda j, i: (i, ATT_Q_DIM // w + j))] + after_specs,
        out_specs=[blk, blk, pl.BlockSpec((1, 1, w), lambda j, i: (j, 0, 0))],
        out_shape=[SDS((n, HG_DIM), F32), SDS((n, HG_DIM), MXU_DTYPE), SDS((HG_DIM // w, 1, w), F32)],
        compiler_params=_cp("parallel", "arbitrary"))(o2, o2, p, g, dcat, *after)


XATT_TQ = 512


def _xattn_fwd(q, kv, name):
    n, nm = q.shape[0], kv.shape[0]
    tq = min(XATT_TQ, n)
    scale = X_HEAD_DIM ** -0.5

    def body(q_ref, k_ref, v_ref, o_ref):
        s = _dot(q_ref[...], k_ref[...], "nt") * scale
        e = jnp.exp(s - jnp.max(s, axis=-1, keepdims=True))
        o_ref[...] = _dot(e / jnp.sum(e, axis=-1, keepdims=True), v_ref[...]).astype(o_ref.dtype)

    qb = pl.BlockSpec((tq, X_HEAD_DIM), lambda h, i: (i, h))
    return pl.pallas_call(
        body, name=name, grid=(X_HEADS, n // tq),
        in_specs=[qb, pl.BlockSpec((nm, X_HEAD_DIM), lambda h, i: (0, h)), pl.BlockSpec((nm, X_HEAD_DIM), lambda h, i: (0, X_HEADS + h))],
        out_specs=qb, out_shape=SDS(q.shape, MXU_DTYPE), compiler_params=_cp("parallel", "parallel"))(q, kv, kv)


def _xattn_bwd(q, kv, do, name, after=()):
    n, nm = q.shape[0], kv.shape[0]
    tq = min(XATT_TQ, n)
    scale = X_HEAD_DIM ** -0.5
    after, after_specs = _unread(after)

    def body(q_ref, k_ref, v_ref, do_ref, *rest):
        dq_ref, dk_ref, dv_ref = rest[len(after):]

        @pl.when(pl.program_id(1) == 0)
        def _():
            dk_ref[...] = jnp.zeros_like(dk_ref)
            dv_ref[...] = jnp.zeros_like(dv_ref)

        qv, dov = q_ref[...], do_ref[...]
        s = _dot(qv, k_ref[...], "nt") * scale
        e = jnp.exp(s - jnp.max(s, axis=-1, keepdims=True))
        p = e / jnp.sum(e, axis=-1, keepdims=True)
        dp = _dot(dov, v_ref[...], "nt")
        ds = p * (dp - jnp.sum(p * dp, axis=-1, keepdims=True)) * scale
        dq_ref[...] = _dot(ds, k_ref[...]).astype(dq_ref.dtype)
        dk_ref[...] += _dot(ds, qv, "tn")
        dv_ref[...] += _dot(p, dov, "tn")

    qb = pl.BlockSpec((tq, X_HEAD_DIM), lambda h, i: (i, h))
    kb = pl.BlockSpec((nm, X_HEAD_DIM), lambda h, i: (0, h))
    return pl.pallas_call(
        body, name=name, grid=(X_HEADS, n // tq),
        in_specs=[qb, kb, pl.BlockSpec((nm, X_HEAD_DIM), lambda h, i: (0, X_HEADS + h)), qb] + after_specs, out_specs=[qb, kb, kb],
        out_shape=[SDS(q.shape, MXU_DTYPE), SDS((nm, X_HEADS * X_HEAD_DIM), F32), SDS((nm, X_HEADS * X_HEAD_DIM), F32)],
        compiler_params=_cp("parallel", "arbitrary"))(q, kv, kv, do, *after)


def _edge_rows(shape):
    row = lax.broadcasted_iota(jnp.int32, shape, 0)
    return row == 0, row == shape[0] - 1


def _shift_rows(u, down, edges):
    if down:
        return jnp.where(edges[0], 0.0, pltpu.roll(u, 1, axis=0))
    return jnp.where(edges[1], 0.0, pltpu.roll(u, u.shape[0] - 1, axis=0))


def _conv(u, w, b, edges):
    return b + _shift_rows(u, True, edges) * w[0:1, :] + u * w[1:2, :] + _shift_rows(u, False, edges) * w[2:3, :]


def _ff_specs(n):
    gate = lambda rows: pl.BlockSpec((rows, FF_COLS), lambda j: (0, j))
    val = lambda rows: pl.BlockSpec((rows, FF_COLS), lambda j: (0, FF_BLOCKS + j))
    return [gate(n), val(n), gate(3), val(3), gate(1), val(1)], gate


def _conv_gate(u, cw, cb, name):
    n = u.shape[0]
    ins, gate_blk = _ff_specs(n)

    def body(ug_ref, uv_ref, wg_ref, wv_ref, bg_ref, bv_ref, o_ref):
        edges = _edge_rows(ug_ref.shape)
        gate = _conv(ug_ref[...], wg_ref[...], bg_ref[...], edges)
        val = _conv(uv_ref[...], wv_ref[...], bv_ref[...], edges)
        o_ref[...] = (gate * _sigmoid(gate) * val).astype(o_ref.dtype)

    return pl.pallas_call(
        body, name=name, grid=(FF_BLOCKS,), in_specs=ins, out_specs=gate_blk(n), out_shape=SDS((n, D_FF), MXU_DTYPE),
        compiler_params=_cp("parallel"))(u, u, cw, cw, cb, cb)


def _conv_gate_bwd(u, cw, cb, da, name, after=()):
    n = u.shape[0]
    ins, gate_blk = _ff_specs(n)
    after, after_specs = _unread(after)

    def side(dacc, u, w, edges, du_ref, dw_ref, db_ref):
        nxt, prv = _shift_rows(dacc, False, edges), _shift_rows(dacc, True, edges)
        du_ref[...] = (nxt * w[0:1, :] + dacc * w[1:2, :] + prv * w[2:3, :]).astype(du_ref.dtype)
        db_ref[...] = jnp.sum(dacc, axis=0, keepdims=True)
        dw_ref[0:1, :] = jnp.sum(nxt * u, axis=0, keepdims=True)
        dw_ref[1:2, :] = jnp.sum(dacc * u, axis=0, keepdims=True)
        dw_ref[2:3, :] = jnp.sum(prv * u, axis=0, keepdims=True)

    def body(ug_ref, uv_ref, wg_ref, wv_ref, bg_ref, bv_ref, da_ref, *rest):
        dug_ref, duv_ref, dwg_ref, dwv_ref, dbg_ref, dbv_ref = rest[len(after):]
        ug, uv = ug_ref[...], uv_ref[...]
        edges = _edge_rows(ug.shape)
        gate = _conv(ug, wg_ref[...], bg_ref[...], edges)
        val = _conv(uv, wv_ref[...], bv_ref[...], edges)
        sg = _sigmoid(gate)
        dav = da_ref[...].astype(F32)
        side(dav * val * sg * (1.0 + gate * (1.0 - sg)), ug, wg_ref[...], edges, dug_ref, dwg_ref, dbg_ref)
        side(dav * gate * sg, uv, wv_ref[...], edges, duv_ref, dwv_ref, dbv_ref)

    return pl.pallas_call(
        body, name=name, grid=(FF_BLOCKS,), in_specs=ins + [gate_blk(n)] + after_specs,
        out_specs=[gate_blk(n), gate_blk(n), gate_blk(3), gate_blk(3), gate_blk(1), gate_blk(1)],
        out_shape=[SDS((n, D_FF), MXU_DTYPE)] * 2 + [SDS((3, D_FF), F32)] * 2 + [SDS((1, D_FF), F32)] * 2,
        compiler_params=_cp("parallel"))(u, u, cw, cw, cb, cb, da, *after)


def _adamw(w, g, m, v, name):
    r, c = w.shape[-2:]
    tr = _row_tile(r, ELEMENTWISE_ROWS)
    assert w.ndim == 2 or w.shape[:-2] == (1,), (name, w.shape)

    def body(w_ref, g_ref, m_ref, v_ref, d_ref, mo_ref, vo_ref, go_ref):
        gv = g_ref[...]
        go_ref[...] = gv
        mn = ADAM_B1 * m_ref[...] + (1.0 - ADAM_B1) * gv
        vn = ADAM_B2 * v_ref[...] + (1.0 - ADAM_B2) * gv * gv
        m_hat = mn / (1.0 - ADAM_B1 ** ADAM_STEP)
        v_hat = vn / (1.0 - ADAM_B2 ** ADAM_STEP)
        d_ref[...] = -ADAM_LR * (m_hat / (jnp.sqrt(v_hat) + ADAM_EPS) + ADAM_WD * w_ref[...])
        mo_ref[...] = mn
        vo_ref[...] = vn

    blk = pl.BlockSpec((tr, c), lambda i: (i, 0)) if w.ndim == 2 else pl.BlockSpec((1, tr, c), lambda i: (0, i, 0))
    out = SDS(w.shape, F32)
    return pl.pallas_call(body, name=name, grid=(r // tr,), in_specs=[blk] * 4, out_specs=[blk] * 4, out_shape=[out] * 4,
                          compiler_params=_cp("parallel"))(w, g, m, v)


ANY = pl.BlockSpec(memory_space=pl.ANY)


def _place():
    x, y, c = lax.axis_index("x"), lax.axis_index("y"), lax.axis_index("c")
    return x, y, c, [(1 - x, y), (x, 1 - y), (1 - x, 1 - y)]


HBM = pl.BlockSpec(memory_space=pltpu.HBM)
SEM = pl.BlockSpec(memory_space=pltpu.SEMAPHORE)
TOKEN = pl.BlockSpec(memory_space=pltpu.VMEM)
TOKEN_SHAPE = SDS((8, 128), F32)
PEERS = 7


def _in_hbm(a):
    return pltpu.with_memory_space_constraint(a, pltpu.HBM)


def _split_params():
    return pltpu.CompilerParams(has_side_effects=pltpu.SideEffectType.DATAFLOW_SIDE_EFFECTING)


def _gather_start(shards, name, after=()):
    nt = len(shards)
    after, after_specs = _unread(after)

    def body(*refs):
        ins, lands = refs[:nt], refs[nt:2 * nt]
        outs = refs[2 * nt + len(after):]
        sends, recvs = outs[:nt], outs[nt:2 * nt]
        x, y, c, chips = _place()
        me = 2 * x + y
        for t in range(nt):
            h = ins[t].shape[0] // 2
            mine = pl.ds(c * h, h)
            for j, (cx, cy) in enumerate(chips):
                for dc in range(2):
                    pltpu.make_async_remote_copy(src_ref=ins[t].at[mine], dst_ref=lands[t].at[me, mine], send_sem=sends[t].at[2 * j + dc],
                                                 recv_sem=recvs[t].at[2 * j + c], device_id=(cx, cy, dc), device_id_type=MESH).start()
            pltpu.make_async_remote_copy(src_ref=ins[t], dst_ref=lands[t].at[me], send_sem=sends[t].at[PEERS - 1], recv_sem=recvs[t].at[PEERS - 1],
                                         device_id=(x, y, 1 - c), device_id_type=MESH).start()
        outs[-1][...] = jnp.zeros(TOKEN_SHAPE.shape, F32)

    lands = [lax.empty((4,) + s.shape, s.dtype) for s in shards]
    out = pl.pallas_call(
        body, name=name, in_specs=[HBM] * (2 * nt) + after_specs, out_specs=[SEM] * (2 * nt) + [HBM] * (2 * nt) + [TOKEN],
        out_shape=[pltpu.SemaphoreType.DMA((PEERS,))] * (2 * nt)
        + [pltpu.HBM(s.shape, s.dtype) for s in shards] + [pltpu.HBM(l.shape, l.dtype) for l in lands] + [TOKEN_SHAPE],
        input_output_aliases={t: 2 * nt + t for t in range(2 * nt)}, compiler_params=_split_params())(
            *[_in_hbm(s) for s in shards], *[_in_hbm(l) for l in lands], *after)
    return out[:nt], out[nt:2 * nt], out[2 * nt:3 * nt], out[3 * nt:4 * nt], out[-1]


def _gather_wait(sends, recvs, shards, lands, after, name):
    nt = len(shards)

    def body(*refs):
        ins, lands_ref = refs[:nt], refs[nt:2 * nt]
        send_refs, recv_refs = refs[2 * nt:3 * nt], refs[3 * nt:4 * nt]
        x, y, c, chips = _place()
        for t in range(nt):
            h = ins[t].shape[0] // 2
            for j, (cx, cy) in enumerate(chips):
                for cs in range(2):
                    blk = lands_ref[t].at[2 * cx + cy, pl.ds(cs * h, h)]
                    pltpu.make_async_remote_copy(src_ref=blk, dst_ref=blk, send_sem=send_refs[t].at[2 * j + cs], recv_sem=recv_refs[t].at[2 * j + cs],
                                                 device_id=(cx, cy, cs), device_id_type=MESH).wait()
            blk = lands_ref[t].at[2 * x + y]
            pltpu.make_async_remote_copy(src_ref=blk, dst_ref=blk, send_sem=send_refs[t].at[PEERS - 1], recv_sem=recv_refs[t].at[PEERS - 1],
                                         device_id=(x, y, 1 - c), device_id_type=MESH).wait()

    out = pl.pallas_call(
        body, name=name, in_specs=[HBM] * (2 * nt) + [SEM] * (2 * nt) + [ANY], out_specs=[HBM] * (2 * nt),
        out_shape=[pltpu.HBM(s.shape, s.dtype) for s in shards] + [pltpu.HBM(l.shape, l.dtype) for l in lands],
        input_output_aliases={t: t for t in range(2 * nt)}, compiler_params=_split_params())(*shards, *lands, *sends, *recvs, after)
    return out[nt:]


def _gather_pieces_start(v, shard, name):
    h = shard.shape[0] // 2

    def body(v_ref, v_land, src, land, v_send, v_recv, send, recv, *rest):
        x, y, c, chips = _place()
        for j, flips in enumerate(_flips()):
            pltpu.make_async_remote_copy(src_ref=v_ref, dst_ref=v_land.at[4 * x + 2 * y + c], send_sem=v_send.at[j], recv_sem=v_recv.at[j],
                                         device_id=_flipped(x, y, c, flips), device_id_type=MESH).start()
        me = 2 * x + y
        mine = pl.ds(c * h, h)
        for j, (cx, cy) in enumerate(chips):
            pltpu.make_async_remote_copy(src_ref=src.at[mine], dst_ref=land.at[me, mine], send_sem=send.at[j], recv_sem=recv.at[j],
                                         device_id=(cx, cy, c), device_id_type=MESH).start()
        pltpu.make_async_remote_copy(src_ref=src, dst_ref=land.at[me], send_sem=send.at[len(chips)], recv_sem=recv.at[len(chips)],
                                     device_id=(x, y, 1 - c), device_id_type=MESH).start()
        rest[-1][...] = jnp.zeros(TOKEN_SHAPE.shape, F32)

    v_land = lax.empty((8,) + v.shape, v.dtype)
    land = lax.empty((4,) + shard.shape, shard.dtype)
    passed = [v, v_land, shard, land]
    out = pl.pallas_call(
        body, name=name, in_specs=[HBM] * 4, out_specs=[SEM] * 4 + [HBM] * 4 + [TOKEN],
        out_shape=[pltpu.SemaphoreType.DMA((PEERS,))] * 2 + [pltpu.SemaphoreType.DMA((4,))] * 2
        + [pltpu.HBM(a.shape, a.dtype) for a in passed] + [TOKEN_SHAPE],
        input_output_aliases={t: 4 + t for t in range(4)}, compiler_params=_split_params())(*[_in_hbm(a) for a in passed])
    return (out[0], out[1], out[4], out[5]), (out[2], out[3], out[6], out[7]), out[8]


def _gather_pieces_wait(send, recv, shard, land, after, name):
    h = shard.shape[0] // 2
    after, after_specs = _unread(after)

    def body(*refs):
        land_ref, send_ref, recv_ref = refs[1:4]
        x, y, c, chips = _place()
        for j, (cx, cy) in enumerate(chips):
            blk = land_ref.at[2 * cx + cy, pl.ds(c * h, h)]
            pltpu.make_async_remote_copy(src_ref=blk, dst_ref=blk, send_sem=send_ref.at[j], recv_sem=recv_ref.at[j],
                                         device_id=(cx, cy, c), device_id_type=MESH).wait()
        own = land_ref.at[2 * x + y]
        pltpu.make_async_remote_copy(src_ref=own, dst_ref=own, send_sem=send_ref.at[len(chips)], recv_sem=recv_ref.at[len(chips)],
                                     device_id=(x, y, 1 - c), device_id_type=MESH).wait()

    out = pl.pallas_call(
        body, name=name, in_specs=[HBM, HBM, SEM, SEM] + after_specs, out_specs=[HBM, HBM],
        out_shape=[pltpu.HBM(shard.shape, shard.dtype), pltpu.HBM(land.shape, land.dtype)],
        input_output_aliases={0: 0, 1: 1}, compiler_params=_split_params())(shard, land, send, recv, *after)
    return out[1]


def _pass_pieces(land, name):
    h = land.shape[1] // 2

    def body(_, out, send, recv):
        x, y, c, chips = _place()

        def piece(j, cc):
            cx, cy = chips[j]
            blk = out.at[2 * cx + cy, pl.ds(cc * h, h)]
            return pltpu.make_async_remote_copy(src_ref=blk, dst_ref=blk, send_sem=send.at[j], recv_sem=recv.at[j],
                                                device_id=(x, y, 1 - c), device_id_type=MESH)

        for j in range(len(chips)):
            piece(j, c).start()
        for j in range(len(chips)):
            piece(j, 1 - c).wait_recv()
        for j in range(len(chips)):
            piece(j, c).wait_send()

    return pl.pallas_call(
        body, name=name, in_specs=[ANY], out_specs=ANY, out_shape=SDS(land.shape, land.dtype), input_output_aliases={0: 0},
        scratch_shapes=[pltpu.SemaphoreType.DMA((3,))] * 2, compiler_params=pltpu.CompilerParams(has_side_effects=True))(land)


def _scatter_start(gs, name):
    nt = len(gs)

    def body(*refs):
        g_refs, lands = refs[:nt], refs[nt:2 * nt]
        sends, recvs = refs[2 * nt:3 * nt], refs[3 * nt:4 * nt]
        x, y, c, chips = _place()
        for g_ref, land, send, recv in zip(g_refs, lands, sends, recvs):
            h = land.shape[1]
            for j, (cx, cy) in enumerate(chips):
                for dc in range(2):
                    pltpu.make_async_remote_copy(src_ref=g_ref.at[2 * cx + cy, pl.ds(dc * h, h)], dst_ref=land.at[2 * j + c],
                                                 send_sem=send.at[2 * j + dc], recv_sem=recv.at[2 * j + c], device_id=(cx, cy, dc),
                                                 device_id_type=MESH).start()
            pltpu.make_async_remote_copy(src_ref=g_ref.at[2 * x + y, pl.ds((1 - c) * h, h)], dst_ref=land.at[PEERS - 1], send_sem=send.at[PEERS - 1],
                                         recv_sem=recv.at[PEERS - 1], device_id=(x, y, 1 - c), device_id_type=MESH).start()
        refs[-1][...] = jnp.zeros(TOKEN_SHAPE.shape, F32)

    lands = [lax.empty((PEERS, g.shape[1] // 2, g.shape[2]), g.dtype) for g in gs]
    out = pl.pallas_call(
        body, name=name, in_specs=[HBM] * (2 * nt), out_specs=[SEM] * (2 * nt) + [HBM] * (2 * nt) + [TOKEN],
        out_shape=[pltpu.SemaphoreType.DMA((PEERS,))] * (2 * nt) + [pltpu.HBM(a.shape, a.dtype) for a in gs + lands] + [TOKEN_SHAPE],
        input_output_aliases={t: 2 * nt + t for t in range(2 * nt)}, compiler_params=_split_params())(
            *[_in_hbm(a) for a in gs + lands])
    return [tuple(out[q * nt + t] for q in range(4)) for t in range(nt)], out[-1]


def _scatter_wait(started, after, name):
    nt = len(started)

    def body(*refs):
        lands = refs[nt:2 * nt]
        sends, recvs = refs[2 * nt:3 * nt], refs[3 * nt:4 * nt]
        x, y, c, chips = _place()
        peers = [(cx, cy, dc) for cx, cy in chips for dc in range(2)] + [(x, y, 1 - c)]
        for t in range(nt):
            for k, peer in enumerate(peers):
                blk = lands[t].at[k]
                pltpu.make_async_remote_copy(src_ref=blk, dst_ref=blk, send_sem=sends[t].at[k], recv_sem=recvs[t].at[k],
                                             device_id=peer, device_id_type=MESH).wait()

    gs, lands = [s[2] for s in started], [s[3] for s in started]
    after, after_specs = _unread(after)
    out = pl.pallas_call(
        body, name=name, in_specs=[HBM] * (2 * nt) + [SEM] * (2 * nt) + after_specs, out_specs=[HBM] * (2 * nt),
        out_shape=[pltpu.HBM(a.shape, a.dtype) for a in gs + lands],
        input_output_aliases={t: t for t in range(2 * nt)}, compiler_params=_split_params())(
            *gs, *lands, *[s[0] for s in started], *[s[1] for s in started], *after)
    return out[:nt], out[nt:]


def _join_start(bufs, name, after=()):
    nt = len(bufs)
    after, after_specs = _unread(after)

    def body(*refs):
        send, recv = refs[nt + len(after):nt + len(after) + 2]
        x, y, c, _ = _place()
        for t in range(nt):
            pltpu.make_async_remote_copy(src_ref=refs[t].at[c], dst_ref=refs[t].at[c], send_sem=send.at[t], recv_sem=recv.at[t],
                                         device_id=(x, y, 1 - c), device_id_type=MESH).start()
        refs[-1][...] = jnp.zeros(TOKEN_SHAPE.shape, F32)

    out = pl.pallas_call(
        body, name=name, in_specs=[HBM] * nt + after_specs, out_specs=[SEM, SEM] + [HBM] * nt + [TOKEN],
        out_shape=[pltpu.SemaphoreType.DMA((nt,))] * 2 + [pltpu.HBM(b.shape, b.dtype) for b in bufs] + [TOKEN_SHAPE],
        input_output_aliases={t: 2 + t for t in range(nt)}, compiler_params=_split_params())(*[_in_hbm(b) for b in bufs], *after)
    return out[0], out[1], out[2:2 + nt], out[-1]


def _join_wait(send, recv, bufs, after, name):
    nt = len(bufs)
    after, after_specs = _unread(after)

    def body(*refs):
        send_ref, recv_ref = refs[nt:nt + 2]
        x, y, c, _ = _place()
        for t in range(nt):
            theirs = refs[t].at[1 - c]
            pltpu.make_async_remote_copy(src_ref=theirs, dst_ref=theirs, send_sem=send_ref.at[t], recv_sem=recv_ref.at[t],
                                         device_id=(x, y, 1 - c), device_id_type=MESH).wait()

    return pl.pallas_call(
        body, name=name, in_specs=[HBM] * nt + [SEM, SEM] + after_specs, out_specs=[HBM] * nt,
        out_shape=[pltpu.HBM(b.shape, b.dtype) for b in bufs],
        input_output_aliases={t: t for t in range(nt)}, compiler_params=_split_params())(*bufs, send, recv, *after)


def _flips():
    return [(dx, dy, dc) for dx in range(2) for dy in range(2) for dc in range(2) if (dx, dy, dc) != (0, 0, 0)]


def _flipped(x, y, c, flips):
    dx, dy, dc = flips
    return (1 - x if dx else x, 1 - y if dy else y, 1 - c if dc else c)


def _small_start(v, name, after=()):
    after, after_specs = _unread(after)

    def body(v_ref, land, *rest):
        send, recv = rest[len(after):len(after) + 2]
        x, y, c, _ = _place()
        for j, flips in enumerate(_flips()):
            pltpu.make_async_remote_copy(src_ref=v_ref, dst_ref=land.at[4 * x + 2 * y + c], send_sem=send.at[j], recv_sem=recv.at[j],
                                         device_id=_flipped(x, y, c, flips), device_id_type=MESH).start()
        rest[-1][...] = jnp.zeros(TOKEN_SHAPE.shape, F32)

    land = lax.empty((8,) + v.shape, v.dtype)
    return pl.pallas_call(
        body, name=name, in_specs=[HBM, HBM] + after_specs, out_specs=[SEM, SEM, HBM, HBM, TOKEN],
        out_shape=[pltpu.SemaphoreType.DMA((PEERS,)), pltpu.SemaphoreType.DMA((PEERS,)), pltpu.HBM(v.shape, v.dtype),
                   pltpu.HBM(land.shape, land.dtype), TOKEN_SHAPE],
        input_output_aliases={0: 2, 1: 3}, compiler_params=_split_params())(_in_hbm(v), _in_hbm(land), *after)


def _small_wait(send, recv, v, land, after, name):
    after, after_specs = _unread(after)

    def body(v_ref, land_ref, send_ref, recv_ref, *rest):
        x, y, c, _ = _place()
        for j, flips in enumerate(_flips()):
            px, py, pc = _flipped(x, y, c, flips)
            blk = land_ref.at[4 * px + 2 * py + pc]
            pltpu.make_async_remote_copy(src_ref=blk, dst_ref=blk, send_sem=send_ref.at[j], recv_sem=recv_ref.at[j],
                                         device_id=(px, py, pc), device_id_type=MESH).wait()

    return pl.pallas_call(
        body, name=name, in_specs=[HBM, HBM, SEM, SEM] + after_specs, out_specs=[HBM, HBM],
        out_shape=[pltpu.HBM(v.shape, v.dtype), pltpu.HBM(land.shape, land.dtype)],
        input_output_aliases={0: 0, 1: 1}, compiler_params=_split_params())(v, land, send, recv, *after)


def _sum_small(v, land, name):
    def body(v_ref, land_ref, o_ref):
        x, y, c, _ = _place()
        me = 4 * x + 2 * y + c
        acc = jnp.where(me == 0, v_ref[...], land_ref[0])
        for d in range(1, 8):
            acc = acc + jnp.where(me == d, v_ref[...], land_ref[d])
        o_ref[...] = acc

    vm = pl.BlockSpec(memory_space=pltpu.VMEM)
    return pl.pallas_call(body, name=name, in_specs=[vm, vm], out_specs=vm, out_shape=SDS(v.shape, F32))(v, land)


SUM_STEPS = 2


def _sum_devices(gs, lands, me, core, name):
    nt = len(gs)

    def body(ix_ref, *refs):
        for own_ref, land_ref, o_ref in zip(refs[:nt], refs[nt:2 * nt], refs[2 * nt:]):
            acc = own_ref[0].astype(F32)
            for j in range(land_ref.shape[0]):
                acc = acc + land_ref[j].astype(F32)
            o_ref[0] = acc

    tiles = [(land.shape[1] // SUM_STEPS, land.shape[2]) for land in lands]
    assert all(land.shape[1] == tr * SUM_STEPS and tr % ROWS_PER_16BIT_TILE == 0 for land, (tr, _) in zip(lands, tiles)), name
    grid_spec = pltpu.PrefetchScalarGridSpec(
        num_scalar_prefetch=1, grid=(SUM_STEPS,),
        in_specs=[pl.BlockSpec((1, tr, c), lambda i, ix: (ix[0], ix[1] * SUM_STEPS + i, 0)) for tr, c in tiles]
        + [pl.BlockSpec((land.shape[0], tr, c), lambda i, ix: (0, i, 0)) for land, (tr, c) in zip(lands, tiles)],
        out_specs=[pl.BlockSpec((1, tr, c), lambda i, ix: (ix[1], i, 0)) for tr, c in tiles])
    return pl.pallas_call(body, name=name, grid_spec=grid_spec, out_shape=[SDS((2,) + land.shape[1:], F32) for land in lands],
                          compiler_params=_cp("parallel"))(jnp.stack([me, core]), *gs, *lands)


def _pack_small(parts):
    flat = jnp.concatenate([p.reshape(-1) for p in parts])
    total = flat.shape[0]
    rows = -(-total // 1024) * 8
    return jnp.pad(flat, (0, rows * 128 - total)).reshape(rows, 128)


def _unpack_small(packed, shapes):
    flat = packed.reshape(-1)
    out, off = [], 0
    for s in shapes:
        size = int(np.prod(s))
        out.append(flat[off:off + size].reshape(s))
        off += size
    return out


def _local_step(x, mem, target, w_in_t, first_after, mid_weights, ffn_weights, on_grad, gains, conv_w, conv_b, hg_lb):
    n = x.shape[0]
    cos, sin = _rope_tables(n)
    seg = _hg_segments()
    gp, gn = _hg_pair_sums()
    masks = _hg_level_masks()
    gq2 = jnp.tile(gains["q_norm_g"], (1, 2))
    gk2 = jnp.tile(gains["k_norm_g"], (1, 2))
    a0 = hg_lb[:, 0:1, :]
    a1 = hg_lb[:, 1:2, :]

    p, h1 = _norm_mm(x, gains["pre_mix_g"], w_in_t, F32, TOKEN_TILE, 1664, "in_proj", after=(first_after,), w_turned=True)
    qr, kr = _qk_prep(p, gq2, gk2, cos, sin, "qk_prep")
    heads = lambda a: a.reshape(n, ATT_KV_HEADS, ATT_HEAD_DIM).transpose(1, 0, 2)
    kh = heads(kr)
    vh = heads(p[:, OFF_AV:OFF_AV + ATT_KV_DIM].astype(MXU_DTYPE))
    att = _attn_fwd(qr, kh, vh, "attn_fwd")
    o2, s0, hg_a, hg_e, hg_kept = _hgrn_fwd(p, a0, a1, seg, masks, "hgrn_fwd")
    rec = _hg_post(o2, p, gains["hg_out_norm_g"], "hg_post")
    cat = jnp.concatenate([att, rec], axis=1)
    w_out, w_xq, w_xkv, w_xo = mid_weights(cat)
    mixed, x1 = _mm_resid_norm(cat, w_out, x, gains["post_mix_g"], 512, "out_proj_resid")
    xq, h2 = _norm_mm(x1, gains["pre_x_g"], w_xq, MXU_DTYPE, TOKEN_TILE, 1024, "xq_proj")
    kv, mn = _norm_mm(mem, gains["mem_norm_g"], w_xkv, MXU_DTYPE, 256, 2048, "xkv_proj")
    ox = _xattn_fwd(xq, kv, "xattn_fwd")
    xo, x2 = _mm_resid_norm(ox, w_xo, x1, gains["post_x_g"], 512, "xo_proj_resid")
    w_up = ffn_weights("w_up", x2)
    u, h3 = _norm_mm(x2, gains["pre_ffn_g"], w_up, F32, TOKEN_TILE, 1408, "up_proj")
    act = _conv_gate(u, conv_w, conv_b, "conv_gate")
    w_down = ffn_weights("w_down", act)
    dn, d3, loss = _mm_resid_norm(act, w_down, x2, gains["post_ffn_g"], 512, "down_proj_resid_loss", target=target)

    gs = {}
    d_act, d_dn, gs["post_ffn_g"] = _norm_bwd_mm(dn, gains["post_ffn_g"], d3, w_down, F32, 512, 1408, "ffn_post_bwd_down_dx")
    tok = on_grad("w_down", _mm(act, d_dn, "tn", WIRE_DTYPE, 1408, 1024, "down_dw"))
    du_g, du_v, dcw_g, dcw_v, dcb_g, dcb_v = _conv_gate_bwd(u, conv_w, conv_b, d_act, "conv_gate_bwd", after=(tok,))
    gs["conv_w"] = jnp.concatenate([dcw_g, dcw_v], axis=1)
    gs["conv_b"] = jnp.concatenate([dcb_g, dcb_v], axis=1)
    ff_shard = w_up.shape[2]
    g_up = _dw_by_owner(h3, du_g, ff_shard, 0, None, 512, "up_dw_gate")
    tok = on_grad("w_up", _dw_by_owner(h3, du_v, ff_shard, 2, g_up, 512, "up_dw_value"))
    d2, gs["pre_ffn_g"] = _dx_norm_bwd([(du_g, 0), (du_g, 1), (du_v, 0), (du_v, 1)], w_up, x2, gains["pre_ffn_g"], d3, 512,
                                       "up_dx_pre_bwd", after=(tok,))
    d_ox, d_xo, gs["post_x_g"] = _norm_bwd_mm(xo, gains["post_x_g"], d2, w_xo, MXU_DTYPE, 512, 1024, "x_post_bwd_xo_dx")
    tok = on_grad("w_xo", _mm(ox, d_xo, "tn", WIRE_DTYPE, 512, 1024, "xo_dw"))
    d_xq, d_k, d_v = _xattn_bwd(xq, kv, d_ox, "xattn_bwd", after=(tok,))
    d_kv = jnp.concatenate([d_k, d_v], axis=1).astype(MXU_DTYPE)
    tok = on_grad("w_xq", _mm(h2, d_xq, "tn", WIRE_DTYPE, 512, 1024, "xq_dw"))
    tok_kv = on_grad("w_xkv", _dw_by_owner(mn, d_kv, w_xkv.shape[2], 0, None, 512, "xkv_dw"))
    d1, gs["pre_x_g"] = _dx_norm_bwd([(d_xq, 0)], w_xq[None], x1, gains["pre_x_g"], d2, 512, "xq_dx_pre_bwd", after=(tok, tok_kv))
    d_mn = _mm_nt_parts([(d_kv, s) for s in range(4)], w_xkv, F32, 256, 1024, "xkv_dx")
    _, gs["mem_norm_g"] = _norm_bwd(mem, gains["mem_norm_g"], d_mn, None, MXU_DTYPE, "mem_norm_bwd")
    d_cat, d_mixed, gs["post_mix_g"] = _norm_bwd_mm(mixed, gains["post_mix_g"], d1, w_out, MXU_DTYPE, 512, 1024, "mix_post_bwd_out_dx")
    tok = on_grad("w_out", _mm(cat, d_mixed, "tn", WIRE_DTYPE, 512, 1024, "out_dw"))
    d_o, d_hg, dg_hg = _hg_post_bwd(o2, p, gains["hg_out_norm_g"], d_cat, "hg_post_bwd", after=(tok,))
    gs["hg_out_norm_g"] = dg_hg.reshape(HG_HEADS, HG_HEAD_DIM).sum(axis=0, keepdims=True)
    dhq2, dz2, dhv2, dlb = _hgrn_bwd(p, a0, a1, masks, gp, gn, d_o, s0, hg_a, hg_e, hg_kept, "hgrn_bwd")
    lb = jax.nn.sigmoid(a0 - a1)
    da0 = dlb * lb * (1.0 - lb)
    gs["hg_lb"] = jnp.concatenate([da0, -da0], axis=1)
    d_qr, d_kh, d_vh = _attn_bwd(qr, kh, vh, cat, d_cat, "attn_bwd")
    unheads = lambda a: a.transpose(2, 0, 1).reshape(n, ATT_KV_DIM)
    d_aq, d_ak, dgq, dgk = _qk_prep_bwd(p, gq2, gk2, cos, sin, d_qr, unheads(d_kh), "qk_prep_bwd")
    gs["q_norm_g"] = dgq.reshape(ATT_HEADS, ATT_HEAD_DIM).sum(axis=0, keepdims=True)
    gs["k_norm_g"] = dgk.reshape(ATT_KV_HEADS, ATT_HEAD_DIM).sum(axis=0, keepdims=True)
    d_p = jnp.concatenate([d_aq, d_ak, unheads(d_vh).astype(MXU_DTYPE), (dhq2[0] + dhq2[1]).astype(MXU_DTYPE),
                           dz2[0].astype(MXU_DTYPE), dz2[1].astype(MXU_DTYPE), (dhv2[0] + dhv2[1]).astype(MXU_DTYPE), d_hg], axis=1)
    tok = on_grad("w_in", _mm(d_p, h1, "tn", WIRE_DTYPE, 1664, 1024, "in_dw"))
    grad_x, gs["pre_mix_g"] = _dx_norm_bwd([(d_p, 0)], w_in_t[None], x, gains["pre_mix_g"], d1, 512, "in_dx_pre_bwd", after=(tok,),
                                           b_turned=True)
    return loss, grad_x, gs


MATS = ("w_in", "w_out", "w_xq", "w_xkv", "w_xo", "w_up", "w_down")
GAINS = ("pre_mix_g", "q_norm_g", "k_norm_g", "hg_out_norm_g", "post_mix_g", "pre_x_g", "mem_norm_g", "post_x_g", "pre_ffn_g", "post_ffn_g")
WEIGHTS = ('pre_mix_g', 'w_in', 'q_norm_g', 'k_norm_g', 'hg_lb', 'hg_out_norm_g', 'w_out', 'post_mix_g', 'pre_x_g', 'mem_norm_g', 'w_xq',
           'w_xkv', 'w_xo', 'post_x_g', 'pre_ffn_g', 'w_up', 'conv_w', 'conv_b', 'w_down', 'post_ffn_g')


def kernel(x, mem, pre_mix_g, w_in, q_norm_g, k_norm_g, hg_lb, hg_out_norm_g, w_out, post_mix_g, pre_x_g, mem_norm_g, w_xq, w_xkv, w_xo, post_x_g, pre_ffn_g, w_up, conv_w, conv_b, w_down, post_ffn_g, loss_target, m_pre_mix_g, m_w_in, m_q_norm_g, m_k_norm_g, m_hg_lb, m_hg_out_norm_g, m_w_out, m_post_mix_g, m_pre_x_g, m_mem_norm_g, m_w_xq, m_w_xkv, m_w_xo, m_post_x_g, m_pre_ffn_g, m_w_up, m_conv_w, m_conv_b, m_w_down, m_post_ffn_g, v_pre_mix_g, v_w_in, v_q_norm_g, v_k_norm_g, v_hg_lb, v_hg_out_norm_g, v_w_out, v_post_mix_g, v_pre_x_g, v_mem_norm_g, v_w_xq, v_w_xkv, v_w_xo, v_post_x_g, v_pre_ffn_g, v_w_up, v_conv_w, v_conv_b, v_w_down, v_post_ffn_g):
    args = dict(locals())
    w = {k: args[k] for k in WEIGHTS}
    m = {k: args["m_" + k] for k in WEIGHTS}
    v = {k: args["v_" + k] for k in WEIGHTS}
    chip = 2 * lax.axis_index("x") + lax.axis_index("y")
    core = lax.axis_index("c")

    turned = ("w_in",)
    shards = {k: (jnp.swapaxes(w[k], 1, 2) if k in turned else w[k])[0].astype(WIRE_DTYPE) for k in MATS}

    def whole(k, g):
        return g if k in ("w_xkv", "w_up") else g.reshape(-1, g.shape[-1])

    mid_names, ffn_names = ("w_out", "w_xq", "w_xkv", "w_xo"), ("w_up", "w_down")
    small, w_in_pieces, token = _gather_pieces_start(_pack_small([w["conv_w"][0], w["hg_lb"]]), shards["w_in"], "gather_first_start")
    mid = _gather_start([shards[k] for k in mid_names], "gather_mid_start", after=(token,))
    ffn = _gather_start([shards[k] for k in ffn_names], "gather_ffn_start", after=(mid[4],))
    w_in_t = whole("w_in", _pass_pieces(_gather_pieces_wait(*w_in_pieces, (ffn[4],), "gather_w_in_wait"), "gather_w_in_pass"))
    mine, others = _small_wait(*small, (w_in_t,), "gather_small_wait")
    small_in = lax.dynamic_update_slice_in_dim(others, mine[None], 2 * chip + core, axis=0)

    def mid_weights(after):
        return [whole(k, g) for k, g in zip(mid_names, _gather_wait(*mid[:4], after, "gather_mid_wait"))]

    def ffn_weights(k, after):
        t = ffn_names.index(k)
        return whole(k, _gather_wait(*[part[t:t + 1] for part in ffn[:4]], after, "gather_wait_" + k)[0])

    cw_parts, lb_parts = [], []
    for s in range(4):
        cw_s, lb_s = _unpack_small(small_in[2 * s], [w["conv_w"][0].shape, w["hg_lb"].shape])
        cw_parts.append(cw_s)
        lb_parts.append(lb_s)
    conv_w_full = jnp.concatenate(cw_parts, axis=1)
    hg_lb_full = jnp.concatenate(lb_parts, axis=2)

    started, held = {}, {}
    leaves_with_next = ("w_down", "w_xo", "w_xq")

    def on_grad(k, g):
        if g.ndim == 2:
            g = g.reshape(4, g.shape[0] // 4, g.shape[1])
        held[k] = g
        if k in leaves_with_next:
            return None
        names = tuple(held)
        per_tensor, token = _scatter_start([held.pop(n) for n in names], "grad_start_" + k)
        started.update(zip(names, per_tensor))
        return token

    gains = {k: w[k] for k in GAINS}
    loss_part, grad_x, gs = _local_step(x[0], mem[0], loss_target[0], w_in_t, ffn[4], mid_weights, ffn_weights, on_grad, gains,
                                        conv_w_full, w["conv_b"], hg_lb_full)
    gs["loss"] = loss_part

    grads, delta, new_m, new_v = {}, {}, {}, {}

    def sum_and_send(names, after, tag):
        sent, landed = _scatter_wait([started[k] for k in names], after, "grad_wait_" + tag)
        return _join_start(_sum_devices(sent, landed, chip, core, "grad_sum_" + tag), "grad_join_start_" + tag)

    def joined(names, join, after, tag):
        for k, r in zip(names, _join_wait(*join[:3], after, "grad_join_wait_" + tag)):
            grads[k] = r.reshape(1, -1, r.shape[-1])

    def adamw(names):
        for k in names:
            shape = w[k].shape
            keep = len(shape) == 3 and shape[0] == 1
            if k in turned:
                view, back = (lambda a: jnp.swapaxes(a, 1, 2)), (lambda a: jnp.swapaxes(a, 1, 2))
                g = grads[k]
            else:
                view = (lambda a: a.reshape(shape)) if keep else (lambda a: a.reshape(-1, shape[-1]))
                back = lambda a: a.reshape(shape)
                g = view(grads[k])
            d, mo, vo, go = _adamw(view(w[k]), g, view(m[k]), view(v[k]), "adamw_" + k)
            delta[k], new_m[k], new_v[k], grads[k] = back(d), back(mo), back(vo), back(go)

    small_names = GAINS + ("conv_b", "conv_w", "hg_lb")
    packed = _pack_small([gs[k] for k in small_names + ("loss",)])
    small = _small_start(packed, "reduce_small_start", after=(grad_x,))

    ffn_join = sum_and_send(ffn_names, (grad_x, small[4]), "ffn")
    mid_join = sum_and_send(mid_names, (ffn_join[3],), "mid")
    joined(ffn_names, ffn_join, (mid_join[3],), "ffn")
    adamw(ffn_names)
    joined(mid_names, mid_join, tuple(new_v[k] for k in ffn_names), "mid")
    adamw(mid_names)
    early = mid_names + ffn_names
    w_in_join = sum_and_send(("w_in",), tuple(new_v[k] for k in early), "w_in")

    mine, others = _small_wait(*small[:4], (w_in_join[3],), "reduce_small_wait")
    reduced_small = _sum_small(mine, others, "reduce_small_sum")
    *summed, loss = _unpack_small(reduced_small, [gs[k].shape for k in small_names + ("loss",)])
    loss = loss[0, 0]
    for k, g in zip(small_names, summed):
        grads[k] = g
    ncw = w["conv_w"].shape[2]
    grads["conv_w"] = lax.dynamic_slice_in_dim(grads["conv_w"], chip * ncw, ncw, axis=1)[None]
    nlb = w["hg_lb"].shape[2]
    grads["hg_lb"] = lax.dynamic_slice_in_dim(grads["hg_lb"], chip * nlb, nlb, axis=2)
    replicated = GAINS + ("conv_b",)
    shapes = [w[k].shape for k in replicated]
    rows = sum(int(np.prod(s)) for s in shapes) // 128
    pack = lambda d: jnp.concatenate([d[k].reshape(-1) for k in replicated]).reshape(rows, 128)
    outs = _adamw(pack(w), reduced_small[:rows], pack(m), pack(v), "adamw_replicated")
    for into, packed_out in zip((delta, new_m, new_v, grads), outs):
        for k, a in zip(replicated, _unpack_small(packed_out, shapes)):
            into[k] = a
    adamw(("conv_w", "hg_lb"))

    joined(("w_in",), w_in_join, tuple(new_v[k] for k in small_names), "w_in")
    adamw(("w_in",))
    return (loss, grad_x[None], *[grads[k] for k in WEIGHTS], *[delta[k] for k in WEIGHTS],
            *[new_m[k] for k in WEIGHTS], *[new_v[k] for k in WEIGHTS])
```

```python
import numpy as np
import jax
import jax.numpy as jnp
from jax import lax
from jax.experimental import pallas as pl
from jax.experimental.pallas import tpu as pltpu

F32 = jnp.float32
MXU_DTYPE = jnp.bfloat16
WIRE_DTYPE = jnp.bfloat16
VMEM_LIMIT_BYTES = 56 * 1024 * 1024
ROWS_PER_16BIT_TILE = 16
ELEMENTWISE_ROWS = 256
EPS = 1e-6
MESH = pl.DeviceIdType.MESH

GRID_W = 64
ATT_HEADS, ATT_KV_HEADS, ATT_HEAD_DIM = 8, 2, 64
ATT_GROUP = ATT_HEADS // ATT_KV_HEADS
ATT_Q_DIM, ATT_KV_DIM = 512, 128
ROPE_THETA = 10000.0
HG_HEADS, HG_HEAD_DIM, HG_DIM = 4, 128, 512
HG_CHUNK = 128
HG_LEVELS = 7
HG_PAIR = 2 * HG_HEAD_DIM
HG_KEPT = 7
X_HEADS, X_HEAD_DIM = 4, 256
D_FF = 2816
FF_COLS = 256
FF_BLOCKS = D_FF // FF_COLS
OFF_AK, OFF_AV, OFF_HQ, OFF_ZF, OFF_ZB, OFF_HI, OFF_HG = 512, 640, 768, 1280, 1792, 2304, 2816

ADAM_LR, ADAM_B1, ADAM_B2, ADAM_EPS, ADAM_WD, ADAM_STEP = 0.001, 0.9, 0.999, 1e-08, 0.01, 10

SDS = jax.ShapeDtypeStruct


def _cp(*sem):
    return pltpu.CompilerParams(dimension_semantics=sem, vmem_limit_bytes=VMEM_LIMIT_BYTES)


def _dot(a, b, form="nn"):
    dims = {"nn": (((1,), (0,)), ((), ())), "nt": (((1,), (1,)), ((), ())), "tn": (((0,), (0,)), ((), ()))}[form]
    return lax.dot_general(a.astype(MXU_DTYPE), b.astype(MXU_DTYPE), dims, preferred_element_type=F32)


def _sigmoid(x):
    return 1.0 / (1.0 + jnp.exp(-x))


def _rstd(x):
    return lax.rsqrt(jnp.mean(x * x, axis=-1, keepdims=True) + EPS)


def _rms_bwd(x, g, dy):
    r = _rstd(x)
    xh = x * r
    dn = dy * g
    dx = r * (dn - xh * jnp.mean(dn * xh, axis=-1, keepdims=True))
    return dx, jnp.sum(dy * xh, axis=0, keepdims=True)


def _unread(after):
    after = tuple(a for a in after if a is not None)
    return after, [pl.BlockSpec(memory_space=pl.ANY)] * len(after)


def _mm(a, b, form, out_dtype, tm, tn, name, after=()):
    after, after_specs = _unread(after)
    if form == "nn":
        (m, k), n = a.shape, b.shape[1]
    elif form == "nt":
        (m, k), n = a.shape, b.shape[0]
    else:
        (k, m), n = a.shape, b.shape[1]
    tm, tn = min(tm, m), min(tn, n)
    assert m % tm == 0 and n % tn == 0, (name, m, n, tm, tn)

    def body(a_ref, b_ref, *rest):
        o_ref = rest[-1]
        o_ref[...] = _dot(a_ref[...], b_ref[...], form).astype(o_ref.dtype)

    a_spec = pl.BlockSpec((k, tm), lambda i, j: (0, i)) if form == "tn" else pl.BlockSpec((tm, k), lambda i, j: (i, 0))
    b_spec = pl.BlockSpec((tn, k), lambda i, j: (j, 0)) if form == "nt" else pl.BlockSpec((k, tn), lambda i, j: (0, j))
    return pl.pallas_call(
        body, name=name, grid=(m // tm, n // tn), in_specs=[a_spec, b_spec] + after_specs,
        out_specs=pl.BlockSpec((tm, tn), lambda i, j: (i, j)), out_shape=SDS((m, n), out_dtype),
        compiler_params=_cp("parallel", "parallel"))(a, b, *after)


def _mm_nt_parts(a_parts, b, out_dtype, tm, tn, name, after=()):
    after, after_specs = _unread(after)
    parts, n, p = b.shape
    m = a_parts[0][0].shape[0]
    tm, tn = min(tm, m), min(tn, n)
    assert m % tm == 0 and n % tn == 0 and len(a_parts) == parts, (name, m, b.shape)

    def body(*refs):
        o_ref = refs[-1]
        acc = _dot(refs[0][...], refs[parts][0], "nt")
        for s in range(1, parts):
            acc = acc + _dot(refs[s][...], refs[parts + s][0], "nt")
        o_ref[...] = acc.astype(o_ref.dtype)

    a_specs = [pl.BlockSpec((tm, p), lambda i, j, cb=cb: (i, cb)) for _, cb in a_parts]
    b_specs = [pl.BlockSpec((1, tn, p), lambda i, j, s=s: (s, j, 0)) for s in range(parts)]
    return pl.pallas_call(
        body, name=name, grid=(m // tm, n // tn), in_specs=a_specs + b_specs + after_specs,
        out_specs=pl.BlockSpec((tm, tn), lambda i, j: (i, j)), out_shape=SDS((m, n), out_dtype),
        compiler_params=_cp("parallel", "parallel"))(*[arr for arr, _ in a_parts], *([b] * parts), *after)


def _norm_bwd_mm(y, g, d, w, out_dtype, tm, tn, name):
    n, dm = y.shape
    nn = w.shape[0]
    tm, tn = min(tm, n), min(tn, nn)
    assert n % tm == 0 and nn % tn == 0 and w.shape[1] == dm, (name, y.shape, w.shape)

    def body(y_ref, g_ref, d_ref, w_ref, dx_ref, dy_ref, dg_ref, dys):
        i, j = pl.program_id(0), pl.program_id(1)

        @pl.when(jnp.logical_and(i == 0, j == 0))
        def _():
            dg_ref[...] = jnp.zeros_like(dg_ref)

        @pl.when(j == 0)
        def _():
            dy, dg = _rms_bwd(y_ref[...], g_ref[...], d_ref[...])
            dy = dy.astype(MXU_DTYPE)
            dys[...] = dy
            dy_ref[...] = dy
            dg_ref[...] += dg

        dx_ref[...] = _dot(dys[...], w_ref[...], "nt").astype(dx_ref.dtype)

    row = pl.BlockSpec((tm, dm), lambda i, j: (i, 0))
    vec = pl.BlockSpec((1, dm), lambda i, j: (0, 0))
    return pl.pallas_call(
        body, name=name, grid=(n // tm, nn // tn), in_specs=[row, vec, row, pl.BlockSpec((tn, dm), lambda i, j: (j, 0))],
        out_specs=[pl.BlockSpec((tm, tn), lambda i, j: (i, j)), row, vec],
        out_shape=[SDS((n, nn), out_dtype), SDS((n, dm), MXU_DTYPE), SDS((1, dm), F32)],
        scratch_shapes=[pltpu.VMEM((tm, dm), MXU_DTYPE)],
        compiler_params=_cp("arbitrary", "arbitrary"))(y, g, d, w)


def _mm_resid_norm(a, b, x, g, tm, name, target=None):
    n, k = a.shape
    d = b.shape[1]
    tm = min(tm, n)
    assert n % tm == 0 and x.shape == (n, d), (name, a.shape, b.shape)
    with_loss = target is not None

    def body(a_ref, b_ref, x_ref, g_ref, *rest):
        y = _dot(a_ref[...], b_ref[...])
        out = x_ref[...] + y * _rstd(y) * g_ref[...]
        if not with_loss:
            y_ref, o_ref = rest
            y_ref[...] = y
            o_ref[...] = out
            return
        t_ref, y_ref, d_ref, l_ref = rest
        y_ref[...] = y
        diff = out - t_ref[...]
        d_ref[...] = diff * (1.0 / d)

        @pl.when(pl.program_id(0) == 0)
        def _():
            l_ref[...] = jnp.zeros_like(l_ref)

        l_ref[...] += 0.5 * jnp.sum(jnp.mean(diff * diff, axis=-1, keepdims=True), axis=0, keepdims=True)

    row = pl.BlockSpec((tm, d), lambda i: (i, 0))
    ins = [pl.BlockSpec((tm, k), lambda i: (i, 0)), pl.BlockSpec((k, d), lambda i: (0, 0)), row, pl.BlockSpec((1, d), lambda i: (0, 0))]
    out = SDS((n, d), F32)
    if with_loss:
        return pl.pallas_call(body, name=name, grid=(n // tm,), in_specs=ins + [row], out_specs=[row, row, pl.BlockSpec((1, 1), lambda i: (0, 0))],
                              out_shape=[out, out, SDS((1, 1), F32)], compiler_params=_cp("arbitrary"))(a, b, x, g, target)
    return pl.pallas_call(body, name=name, grid=(n // tm,), in_specs=ins, out_specs=[row, row], out_shape=[out, out],
                          compiler_params=_cp("parallel"))(a, b, x, g)


def _dx_norm_bwd(a_parts, b, x, g, res, tm, name, after=(), b_turned=False):
    after, after_specs = _unread(after)
    parts, d, p = (b.shape[0], b.shape[2], b.shape[1]) if b_turned else b.shape
    form = "nn" if b_turned else "nt"
    n = x.shape[0]
    tm = min(tm, n)
    assert n % tm == 0 and len(a_parts) == parts and x.shape[1] == d, (name, x.shape, b.shape)

    def body(*refs):
        x_ref, g_ref, res_ref = refs[2 * parts:2 * parts + 3]
        dx_ref, dg_ref = refs[-2:]
        dh = _dot(refs[0][...], refs[parts][0], form)
        for s in range(1, parts):
            dh = dh + _dot(refs[s][...], refs[parts + s][0], form)
        dx, dg = _rms_bwd(x_ref[...], g_ref[...], dh)
        dx_ref[...] = dx + res_ref[...]

        @pl.when(pl.program_id(0) == 0)
        def _():
            dg_ref[...] = jnp.zeros_like(dg_ref)

        dg_ref[...] += dg

    a_specs = [pl.BlockSpec((tm, p), lambda i, cb=cb: (i, cb)) for _, cb in a_parts]
    b_specs = [pl.BlockSpec((1,) + b.shape[1:], lambda i, s=s: (s, 0, 0)) for s in range(parts)]
    row = pl.BlockSpec((tm, d), lambda i: (i, 0))
    vec = pl.BlockSpec((1, d), lambda i: (0, 0))
    return pl.pallas_call(
        body, name=name, grid=(n // tm,), in_specs=a_specs + b_specs + [row, vec, row] + after_specs,
        out_specs=[row, vec], out_shape=[SDS((n, d), F32), SDS((1, d), F32)],
        compiler_params=_cp("arbitrary"))(*[arr for arr, _ in a_parts], *([b] * parts), x, g, res, *after)


def _dw_by_owner(a, b, tn, first, into, tm, name):
    k, m = a.shape
    cnt = b.shape[1] // tn
    tm = min(tm, m)
    assert m % tm == 0 and b.shape[1] == cnt * tn and first + cnt <= 4, (name, a.shape, b.shape)

    def body(a_ref, b_ref, *rest):
        rest[-1][0] = _dot(a_ref[...], b_ref[...], "tn").astype(rest[-1].dtype)

    extra = [] if into is None else [into]
    return pl.pallas_call(
        body, name=name, grid=(m // tm, cnt),
        in_specs=[pl.BlockSpec((k, tm), lambda i, j: (0, i)), pl.BlockSpec((k, tn), lambda i, j: (0, j))] + [pl.BlockSpec(memory_space=pl.ANY)] * len(extra),
        out_specs=pl.BlockSpec((1, tm, tn), lambda i, j: (first + j, i, 0)), out_shape=SDS((4, m, tn), WIRE_DTYPE),
        input_output_aliases={2: 0} if extra else {},
        compiler_params=_cp("parallel", "parallel"))(a, b, *extra)


def _norm_mm(x, g, w, out_dtype, tm, tn, name, after=(), w_turned=False):
    after, after_specs = _unread(after)
    m, d = x.shape
    sharded = w.ndim == 3
    n = w.shape[0] if w_turned else w.shape[-1] * (w.shape[0] if sharded else 1)
    tm, tn = min(tm, m), (w.shape[-1] if sharded else min(tn, n))
    assert m % tm == 0 and n % tn == 0 and not (sharded and w_turned), (name, m, n, tm, tn)

    def body(x_ref, g_ref, w_ref, *rest):
        o_ref, h_ref, hs = rest[-3:]

        @pl.when(pl.program_id(1) == 0)
        def _():
            xv = x_ref[...]
            h = (xv * _rstd(xv) * g_ref[...]).astype(MXU_DTYPE)
            hs[...] = h
            h_ref[...] = h

        o_ref[...] = _dot(hs[...], w_ref[0] if sharded else w_ref[...], "nt" if w_turned else "nn").astype(o_ref.dtype)

    if w_turned:
        w_spec = pl.BlockSpec((tn, d), lambda i, j: (j, 0))
    else:
        w_spec = pl.BlockSpec((1, d, tn), lambda i, j: (j, 0, 0)) if sharded else pl.BlockSpec((d, tn), lambda i, j: (0, j))
    return pl.pallas_call(
        body, name=name, grid=(m // tm, n // tn),
        in_specs=[pl.BlockSpec((tm, d), lambda i, j: (i, 0)), pl.BlockSpec((1, d), lambda i, j: (0, 0)), w_spec] + after_specs,
        out_specs=[pl.BlockSpec((tm, tn), lambda i, j: (i, j)), pl.BlockSpec((tm, d), lambda i, j: (i, 0))],
        out_shape=[SDS((m, n), out_dtype), SDS((m, d), MXU_DTYPE)],
        scratch_shapes=[pltpu.VMEM((tm, d), MXU_DTYPE)],
        compiler_params=_cp("parallel", "arbitrary"))(x, g, w, *after)


ROW_TILE = 512
TOKEN_TILE = 1024


def _norm_bwd(x, g, dy, res, out_dtype, name):
    n, d = x.shape
    tr = min(ROW_TILE, n)
    has_res = res is not None

    def body(*refs):
        x_ref, g_ref, dy_ref = refs[:3]
        dx_ref, dg_ref = refs[-2:]
        dx, dg = _rms_bwd(x_ref[...], g_ref[...], dy_ref[...].astype(F32))
        if has_res:
            dx = dx + refs[3][...]
        dx_ref[...] = dx.astype(dx_ref.dtype)

        @pl.when(pl.program_id(0) == 0)
        def _():
            dg_ref[...] = jnp.zeros_like(dg_ref)

        dg_ref[...] += dg

    row = pl.BlockSpec((tr, d), lambda i: (i, 0))
    vec = pl.BlockSpec((1, d), lambda i: (0, 0))
    ins = [x, g, dy] + ([res] if has_res else [])
    return pl.pallas_call(
        body, name=name, grid=(n // tr,), in_specs=[row, vec, row] + ([row] if has_res else []),
        out_specs=[row, vec], out_shape=[SDS((n, d), out_dtype), SDS((1, d), F32)],
        compiler_params=_cp("arbitrary"))(*ins)


def _rope_tables(n):
    pairs = ATT_HEAD_DIM // 4
    t = np.arange(n)
    inv = np.power(ROPE_THETA, -np.arange(pairs, dtype=np.float32) / pairs).astype(np.float32)
    ang = np.concatenate([(t // GRID_W)[:, None].astype(np.float32) * inv, (t % GRID_W)[:, None].astype(np.float32) * inv], axis=-1)
    cos = np.repeat(np.cos(ang), 2, axis=-1)
    sin = np.repeat(np.sin(ang), 2, axis=-1) * np.tile(np.array([-1.0, 1.0], np.float32), ATT_HEAD_DIM // 2)
    return jnp.asarray(np.tile(cos, 2), F32), jnp.asarray(np.tile(sin, 2), F32)


def _swap_pairs(x):
    lane = lax.broadcasted_iota(jnp.int32, x.shape, 1)
    return jnp.where((lane & 1) == 0, pltpu.roll(x, 127, axis=1), pltpu.roll(x, 1, axis=1))


def _head_mean(v):
    lane = lax.broadcasted_iota(jnp.int32, v.shape, 1)
    lo = jnp.where(lane < ATT_HEAD_DIM, v, 0.0)
    s0 = jnp.sum(lo, axis=-1, keepdims=True)
    s1 = jnp.sum(v - lo, axis=-1, keepdims=True)
    return jnp.where(lane < ATT_HEAD_DIM, s0, s1) * (1.0 / ATT_HEAD_DIM)


def _qk_prep(p, gq, gk, cos, sin, name):
    n = p.shape[0]
    tr = min(ROW_TILE, n)

    def one(xv, g, c, s):
        xn = xv * lax.rsqrt(_head_mean(xv * xv) + EPS) * g
        return xn * c + _swap_pairs(xn) * s

    def body(q_ref, k_ref, gq_ref, gk_ref, c_ref, s_ref, qo_ref, ko_ref):
        c, s = c_ref[...], s_ref[...]
        for j in range(ATT_Q_DIM // 128):
            qo_ref[:, j * 128:(j + 1) * 128] = one(q_ref[:, j * 128:(j + 1) * 128], gq_ref[...], c, s).astype(qo_ref.dtype)
        ko_ref[...] = one(k_ref[...], gk_ref[...], c, s).astype(ko_ref.dtype)

    vec = pl.BlockSpec((1, 128), lambda i: (0, 0))
    tab = pl.BlockSpec((tr, 128), lambda i: (i, 0))
    return pl.pallas_call(
        body, name=name, grid=(n // tr,),
        in_specs=[pl.BlockSpec((tr, ATT_Q_DIM), lambda i: (i, 0)), pl.BlockSpec((tr, 128), lambda i: (i, OFF_AK // 128)), vec, vec, tab, tab],
        out_specs=[pl.BlockSpec((tr, ATT_Q_DIM), lambda i: (i, 0)), tab],
        out_shape=[SDS((n, ATT_Q_DIM), MXU_DTYPE), SDS((n, ATT_KV_DIM), MXU_DTYPE)],
        compiler_params=_cp("parallel"))(p, p, gq, gk, cos, sin)


def _qk_prep_bwd(p, gq, gk, cos, sin, dq, dk, name):
    n = p.shape[0]
    tr = min(ROW_TILE, n)

    def one(xv, g, c, s, dout):
        dxn = dout * c + _swap_pairs(dout * s)
        r = lax.rsqrt(_head_mean(xv * xv) + EPS)
        xh = xv * r
        dn = dxn * g
        dx = r * (dn - xh * _head_mean(dn * xh))
        return dx, jnp.sum(dxn * xh, axis=0, keepdims=True)

    def body(q_ref, k_ref, gq_ref, gk_ref, c_ref, s_ref, dq_ref, dk_ref, dqo_ref, dko_ref, dgq_ref, dgk_ref):
        @pl.when(pl.program_id(0) == 0)
        def _():
            dgq_ref[...] = jnp.zeros_like(dgq_ref)
            dgk_ref[...] = jnp.zeros_like(dgk_ref)

        c, s = c_ref[...], s_ref[...]
        for j in range(ATT_Q_DIM // 128):
            sl = slice(j * 128, (j + 1) * 128)
            dx, dg = one(q_ref[:, sl], gq_ref[...], c, s, dq_ref[:, sl])
            dqo_ref[:, sl] = dx.astype(dqo_ref.dtype)
            dgq_ref[:, sl] += dg
        dx, dg = one(k_ref[...], gk_ref[...], c, s, dk_ref[...])
        dko_ref[...] = dx.astype(dko_ref.dtype)
        dgk_ref[...] += dg

    vec = pl.BlockSpec((1, 128), lambda i: (0, 0))
    tab = pl.BlockSpec((tr, 128), lambda i: (i, 0))
    qrow = pl.BlockSpec((tr, ATT_Q_DIM), lambda i: (i, 0))
    return pl.pallas_call(
        body, name=name, grid=(n // tr,),
        in_specs=[qrow, pl.BlockSpec((tr, 128), lambda i: (i, OFF_AK // 128)), vec, vec, tab, tab, qrow, tab],
        out_specs=[qrow, tab, pl.BlockSpec((1, ATT_Q_DIM), lambda i: (0, 0)), vec],
        out_shape=[SDS((n, ATT_Q_DIM), MXU_DTYPE), SDS((n, ATT_KV_DIM), MXU_DTYPE), SDS((1, ATT_Q_DIM), F32), SDS((1, 128), F32)],
        compiler_params=_cp("arbitrary"))(p, p, gq, gk, cos, sin, dq, dk)


ATT_FWD_STEP = (256, 4)
ATT_BWD_STEP = (512, 2)


def _attn_fwd(q, k, v, name):
    n = q.shape[0]
    tq, step_heads = min(ATT_FWD_STEP[0], n), ATT_FWD_STEP[1]
    scale = ATT_HEAD_DIM ** -0.5
    gw = step_heads * ATT_HEAD_DIM
    parts = ATT_GROUP // step_heads

    def body(q_ref, k_ref, v_ref, o_ref):
        kk, vv = k_ref[0], v_ref[0]
        v_ones = jnp.concatenate([vv, jnp.ones_like(vv)], axis=1)
        outs = []
        for g in range(step_heads):
            s = _dot(q_ref[:, g * ATT_HEAD_DIM:(g + 1) * ATT_HEAD_DIM] * scale, kk, "nt")
            e = jnp.exp(s - jnp.max(s, axis=-1, keepdims=True))
            ov = _dot(e, v_ones)
            outs.append(ov[:, :ATT_HEAD_DIM] / ov[:, ATT_HEAD_DIM:])
        o_ref[...] = jnp.concatenate(outs, axis=-1).astype(o_ref.dtype)

    kv = pl.BlockSpec((1, n, ATT_HEAD_DIM), lambda h, i, pr: (h, 0, 0))
    qb = pl.BlockSpec((tq, gw), lambda h, i, pr: (i, h * parts + pr))
    return pl.pallas_call(
        body, name=name, grid=(ATT_KV_HEADS, n // tq, parts), in_specs=[qb, kv, kv],
        out_specs=qb, out_shape=SDS((n, ATT_Q_DIM), MXU_DTYPE),
        compiler_params=_cp("parallel", "parallel", "parallel"))(q, k, v)


def _attn_bwd(q, k, v, o, do, name):
    n = q.shape[0]
    tq, step_heads = min(ATT_BWD_STEP[0], n), ATT_BWD_STEP[1]
    scale = ATT_HEAD_DIM ** -0.5
    gw = step_heads * ATT_HEAD_DIM
    parts = ATT_GROUP // step_heads

    def body(q_ref, k_ref, v_ref, o_ref, do_ref, dq_ref, dk_ref, dv_ref):
        @pl.when(jnp.logical_and(pl.program_id(1) == 0, pl.program_id(2) == 0))
        def _():
            dk_ref[...] = jnp.zeros_like(dk_ref)
            dv_ref[...] = jnp.zeros_like(dv_ref)

        kk, vv = k_ref[0], v_ref[0]
        dqs = []
        dk_acc = jnp.zeros((ATT_HEAD_DIM, n), F32)
        dv_acc = jnp.zeros((ATT_HEAD_DIM, n), F32)
        for g in range(step_heads):
            sl = slice(g * ATT_HEAD_DIM, (g + 1) * ATT_HEAD_DIM)
            qg, dog = q_ref[:, sl] * scale, do_ref[:, sl].astype(F32)
            s = _dot(qg, kk, "nt")
            e = jnp.exp(s - jnp.max(s, axis=-1, keepdims=True))
            inv = 1.0 / jnp.sum(e, axis=-1, keepdims=True)
            delta = jnp.sum(dog * o_ref[:, sl].astype(F32), axis=-1, keepdims=True)
            dse = e * (_dot(dog, vv, "nt") - delta)
            dqs.append(_dot(dse, kk) * (inv * scale))
            dk_acc += _dot(qg.astype(F32) * inv, dse, "tn")
            dv_acc += _dot(dog * inv, e, "tn")
        dq_ref[...] = jnp.concatenate(dqs, axis=-1)
        dk_ref[0] += dk_acc
        dv_ref[0] += dv_acc

    kv = pl.BlockSpec((1, n, ATT_HEAD_DIM), lambda h, i, pr: (h, 0, 0))
    kvt = pl.BlockSpec((1, ATT_HEAD_DIM, n), lambda h, i, pr: (h, 0, 0))
    qb = pl.BlockSpec((tq, gw), lambda h, i, pr: (i, h * parts + pr))
    return pl.pallas_call(
        body, name=name, grid=(ATT_KV_HEADS, n // tq, parts), in_specs=[qb, kv, kv, qb, qb], out_specs=[qb, kvt, kvt],
        out_shape=[SDS((n, ATT_Q_DIM), F32), SDS((ATT_KV_HEADS, ATT_HEAD_DIM, n), F32), SDS((ATT_KV_HEADS, ATT_HEAD_DIM, n), F32)],
        compiler_params=_cp("parallel", "arbitrary", "arbitrary"))(q, k, v, o, do)


def _both_directions(mats, axis):
    fwd = np.concatenate(mats, axis=axis).astype(np.float32)
    bwd = np.concatenate([m[::-1, ::-1] for m in mats], axis=axis).astype(np.float32)
    return jnp.asarray(np.stack([fwd, bwd]), MXU_DTYPE)


def _hg_segments():
    c = HG_CHUNK
    t = np.arange(c)[:, None]
    r = np.arange(c)[None, :]
    mats = [(r <= t)]
    for lev in range(HG_LEVELS):
        h = c >> (lev + 1)
        mid = (t // (2 * h)) * (2 * h) + h - 1
        hi = (t // h) % 2 == 1
        mats.append(np.where(hi, (r > mid) & (r <= t), (r > t) & (r <= mid)))
    mats.append(r > t)
    return _both_directions(mats, 0)


def _hg_pair_sums():
    c = HG_CHUNK
    r = np.arange(c)[:, None]
    t = np.arange(c)[None, :]
    gp, gn = [t >= r], [t < r]
    for lev in range(HG_LEVELS):
        sh = HG_LEVELS - 1 - lev
        same = (r >> sh) == (t >> sh)
        gp.append(same & (t >= r))
        gn.append(same & (t < r))
    return _both_directions(gp, 1), _both_directions(gn, 1)


def _split_dot(mat, x):
    hi = x.astype(MXU_DTYPE)
    lo = (x - hi.astype(F32)).astype(MXU_DTYPE)
    return _dot(mat, hi) + _dot(mat, lo)


def _hg_gates(hq, z, a0, a1):
    q = hq * _sigmoid(hq)
    sg = _sigmoid(z)
    lb = _sigmoid(a0 - a1)
    f = lb + (1.0 - lb) * sg
    k = (1.0 - lb) * (1.0 - sg)
    return q, f, k, sg, lb


def _hg_level_masks():
    c = HG_CHUNK
    t = np.arange(c)
    later, same = [], []
    for lev in range(HG_LEVELS):
        sh = HG_LEVELS - 1 - lev
        later.append(np.broadcast_to((((t >> sh) & 1) == 1)[:, None], (c, HG_HEAD_DIM)))
        same.append((t[:, None] >> (sh + 1)) == (t[None, :] >> (sh + 1)))
    same.append(t[:, None] == t[None, :])
    later = np.stack(later).astype(np.float32)
    return jnp.asarray(np.stack([later, 1.0 - later]), F32), jnp.asarray(np.stack(same).astype(np.float32), F32)


def _hg_level(q, k, ex, later_ref, lev):
    e = ex[lev + 1]
    e_q = e * later_ref[0, lev]
    e_k = e - e_q
    return q * e_q, k * e_k, e_q, e_k


def _hg_intra(q, k, ex, later_ref, same_ref):
    a = same_ref[HG_LEVELS] * jnp.sum(q * k, axis=-1, keepdims=True)
    for lev in range(HG_LEVELS):
        qs, ks, _, _ = _hg_level(q, k, ex, later_ref, lev)
        a = a + same_ref[lev] * _dot(qs, ks, "nt")
    return a


def _hg_specs(n, with_time):
    c = HG_CHUNK
    nc = n // c

    def chunk(d, i):
        first = d if with_time else 1 - d
        return i + first * (nc - 1 - 2 * i)

    def pcols(off, dir_stride=0):
        return [pl.BlockSpec((c, HG_PAIR), lambda d, i, j=j: (chunk(d, i), off // HG_PAIR + dir_stride // HG_PAIR * d + j)) for j in range(2)]

    specs = dict(
        hq=pcols(OFF_HQ), v=pcols(OFF_HI), z=pcols(OFF_ZF, OFF_ZB - OFF_ZF),
        shared=pl.BlockSpec((c, HG_DIM), lambda d, i: (chunk(d, i), 0)),
        per_dir=pl.BlockSpec((1, c, HG_DIM), lambda d, i: (d, chunk(d, i), 0)),
        vec=pl.BlockSpec((1, 1, HG_DIM), lambda d, i: (d, 0, 0)),
        seg=pl.BlockSpec((1, (HG_LEVELS + 2) * c, c), lambda d, i: (d, 0, 0)),
        sums=pl.BlockSpec((1, c, (HG_LEVELS + 1) * c), lambda d, i: (d, 0, 0)),
        later=pl.BlockSpec((1, HG_LEVELS, c, HG_HEAD_DIM), lambda d, i: (d, 0, 0, 0)),
        same=pl.BlockSpec((HG_LEVELS + 1, c, c), lambda d, i: (0, 0, 0)),
        state=pl.BlockSpec((1, HG_HEADS, 1, HG_HEAD_DIM, HG_HEAD_DIM), lambda d, i: (d, 0, chunk(d, i), 0, 0)),
        weights=pl.BlockSpec((1, HG_HEADS, 1, c, c), lambda d, i: (d, 0, chunk(d, i), 0, 0)),
        levels=pl.BlockSpec((1, HG_HEADS, 1, HG_LEVELS, c, HG_HEAD_DIM), lambda d, i: (d, 0, chunk(d, i), 0, 0, 0)),
        kept=pl.BlockSpec((1, HG_KEPT, c, HG_DIM), lambda d, i: (d, 0, chunk(d, i), 0)))
    return nc, specs


def _hg_head(refs, hh):
    off = (hh % 2) * HG_HEAD_DIM
    return refs[hh // 2][:, off:off + HG_HEAD_DIM]


def _hg_lanes(hh):
    return slice(hh * HG_HEAD_DIM, (hh + 1) * HG_HEAD_DIM)


def _hg_exps(seg_ref, f):
    c = HG_CHUNK
    args = _split_dot(seg_ref[0], jnp.log(f))
    return [jnp.exp(args[j * c:(j + 1) * c]) for j in range(HG_LEVELS + 2)]


def _hg_last_row(a, mirrored):
    return jnp.where(mirrored, a[0:1, :], a[HG_CHUNK - 1:HG_CHUNK, :])


def _hgrn_fwd(p, a0, a1, seg, masks, name):
    n = p.shape[0]
    nc, sp = _hg_specs(n, True)

    def body(hq0, hq1, z0, z1, v0, v1, a0_ref, a1_ref, seg_ref, later_ref, same_ref, o_ref, s0_ref, a_ref, e_ref, g_ref, st):
        @pl.when(pl.program_id(1) == 0)
        def _():
            st[...] = jnp.zeros_like(st)

        mirrored = pl.program_id(0) == 1
        for hh in range(HG_HEADS):
            ln = _hg_lanes(hh)
            hqv = _hg_head((hq0, hq1), hh)
            q, f, k, sg, _ = _hg_gates(hqv, _hg_head((z0, z1), hh), a0_ref[0, :, ln], a1_ref[0, :, ln])
            vv = _hg_head((v0, v1), hh)
            ex = _hg_exps(seg_ref, f)
            for lev in range(HG_LEVELS):
                e_ref[0, hh, 0, lev] = ex[lev + 1].astype(e_ref.dtype)
            sq = _sigmoid(hqv)
            for j, kept in enumerate((q, k, f, sg, sq * (1.0 + hqv * (1.0 - sq)), ex[0], ex[HG_LEVELS + 1])):
                g_ref[0, j, :, ln] = kept
            a = _hg_intra(q, k, ex, later_ref, same_ref).astype(MXU_DTYPE)
            a_ref[0, hh, 0] = a
            s_t = st[hh]
            s0_ref[0, hh, 0] = s_t
            o_ref[0, :, ln] = _dot(a, vv) + _dot(q * ex[0], s_t, "nt")
            st[hh] = s_t * _hg_last_row(ex[0], mirrored) + _dot(vv, k * ex[HG_LEVELS + 1], "tn")

    return pl.pallas_call(
        body, name=name, grid=(2, nc), in_specs=sp["hq"] + sp["z"] + sp["v"] + [sp["vec"], sp["vec"], sp["seg"], sp["later"], sp["same"]],
        out_specs=[sp["per_dir"], sp["state"], sp["weights"], sp["levels"], sp["kept"]],
        out_shape=[SDS((2, n, HG_DIM), F32), SDS((2, HG_HEADS, nc, HG_HEAD_DIM, HG_HEAD_DIM), F32),
                   SDS((2, HG_HEADS, nc, HG_CHUNK, HG_CHUNK), MXU_DTYPE),
                   SDS((2, HG_HEADS, nc, HG_LEVELS, HG_CHUNK, HG_HEAD_DIM), MXU_DTYPE), SDS((2, HG_KEPT, n, HG_DIM), F32)],
        scratch_shapes=[pltpu.VMEM((HG_HEADS, HG_HEAD_DIM, HG_HEAD_DIM), F32)],
        compiler_params=_cp("parallel", "arbitrary"))(p, p, p, p, p, p, a0, a1, seg, *masks)


def _hgrn_bwd(p, a0, a1, masks, gp, gn, do, s0, a, e, kept, name):
    n = p.shape[0]
    nc, sp = _hg_specs(n, False)


    def body(v0, v1, a0_ref, a1_ref, later_ref, same_ref, gp_ref, gn_ref, do_ref, s0_ref, a_ref, e_ref, g_ref,
             dhq_ref, dz_ref, dv_ref, dlb_ref, rt):
        @pl.when(pl.program_id(1) == 0)
        def _():
            rt[...] = jnp.zeros_like(rt)
            dlb_ref[...] = jnp.zeros_like(dlb_ref)

        mirrored = pl.program_id(0) == 1
        for hh in range(HG_HEADS):
            ln = _hg_lanes(hh)
            q, k, f, sg, dsilu, e_first, e_last = (g_ref[0, j, :, ln] for j in range(HG_KEPT))
            lb = _sigmoid(a0_ref[0, :, ln] - a1_ref[0, :, ln])
            vv, dov = _hg_head((v0, v1), hh), do_ref[:, ln]
            ex = [e_first] + [e_ref[0, hh, 0, lev].astype(F32) for lev in range(HG_LEVELS)] + [e_last]
            a = a_ref[0, hh, 0]
            da = _dot(dov, vv, "nt")
            diag = jnp.sum(dov * vv, axis=-1, keepdims=True)
            s_t = s0_ref[0, hh, 0]
            r_t = rt[hh]
            k_end = k * ex[HG_LEVELS + 1]
            dv_ref[0, :, ln] = _dot(a, dov, "tn") + _dot(k_end, r_t, "nt")
            dq_inter = ex[0] * _dot(dov, s_t)
            dk_inter = ex[HG_LEVELS + 1] * _dot(vv, r_t)
            dq = diag * k + dq_inter
            dk = diag * q + dk_inter
            q_terms, k_terms = [q * dq_inter], [k * dk_inter]
            for lev in range(HG_LEVELS):
                qs, ks, e_q, e_k = _hg_level(q, k, ex, later_ref, lev)
                pairs = da * same_ref[lev]
                q_part = e_q * _dot(pairs, ks)
                k_part = e_k * _dot(pairs, qs, "tn")
                dq, dk = dq + q_part, dk + k_part
                q_terms.append(q * q_part)
                k_terms.append(k * k_part)
            decay = _hg_last_row(ex[0], mirrored)
            rt[hh] = r_t * decay + _dot(dov, q * ex[0], "tn")
            later = decay * jnp.sum(s_t * r_t, axis=0, keepdims=True)
            dlf = _dot(gp_ref[0], jnp.concatenate(q_terms, axis=0)) + _dot(gn_ref[0], jnp.concatenate(k_terms, axis=0)) + later
            df = dlf / f - dk
            dz_ref[0, :, ln] = df * (1.0 - lb) * sg * (1.0 - sg)
            dlb_ref[0, :, ln] += jnp.sum(df * (1.0 - sg), axis=0, keepdims=True)
            dhq_ref[0, :, ln] = dq * dsilu

    out = SDS((2, n, HG_DIM), F32)
    return pl.pallas_call(
        body, name=name, grid=(2, nc),
        in_specs=sp["v"] + [sp["vec"], sp["vec"], sp["later"], sp["same"], sp["sums"], sp["sums"],
                            sp["shared"], sp["state"], sp["weights"], sp["levels"], sp["kept"]],
        out_specs=[sp["per_dir"], sp["per_dir"], sp["per_dir"], sp["vec"]], out_shape=[out, out, out, SDS((2, 1, HG_DIM), F32)],
        scratch_shapes=[pltpu.VMEM((HG_HEADS, HG_HEAD_DIM, HG_HEAD_DIM), F32)],
        compiler_params=_cp("parallel", "arbitrary"))(p, p, a0, a1, *masks, gp, gn, do, s0, a, e, kept)


def _hg_post(o2, p, g, name):
    n = p.shape[0]
    tr = min(ROW_TILE, n)
    w = 2 * HG_HEAD_DIM

    def body(of_ref, ob_ref, hg_ref, g_ref, o_ref):
        for j in range(2):
            sl = slice(j * HG_HEAD_DIM, (j + 1) * HG_HEAD_DIM)
            o = of_ref[0, :, sl] + ob_ref[0, :, sl]
            hg = hg_ref[:, sl]
            o_ref[:, sl] = (o * _rstd(o) * g_ref[...] * (hg * _sigmoid(hg))).astype(o_ref.dtype)

    blk = pl.BlockSpec((tr, w), lambda i, j: (i, j))
    dirs = [pl.BlockSpec((1, tr, w), lambda i, j, d=d: (d, i, j)) for d in range(2)]
    return pl.pallas_call(
        body, name=name, grid=(n // tr, HG_DIM // w),
        in_specs=dirs + [pl.BlockSpec((tr, w), lambda i, j: (i, OFF_HG // w + j)), pl.BlockSpec((1, HG_HEAD_DIM), lambda i, j: (0, 0))],
        out_specs=blk, out_shape=SDS((n, HG_DIM), MXU_DTYPE), compiler_params=_cp("parallel", "parallel"))(o2, o2, p, g)


def _hg_post_bwd(o2, p, g, dcat, name, after=()):
    n = p.shape[0]
    tr = min(ROW_TILE, n)
    w = 2 * HG_HEAD_DIM
    after, after_specs = _unread(after)

    def body(of_ref, ob_ref, hg_ref, g_ref, d_ref, *rest):
        do_ref, dhg_ref, dg_ref = rest[len(after):]

        @pl.when(pl.program_id(1) == 0)
        def _():
            dg_ref[...] = jnp.zeros_like(dg_ref)

        for j in range(2):
            sl = slice(j * HG_HEAD_DIM, (j + 1) * HG_HEAD_DIM)
            o = of_ref[0, :, sl] + ob_ref[0, :, sl]
            hg = hg_ref[:, sl]
            d = d_ref[:, sl].astype(F32)
            sg = _sigmoid(hg)
            on = o * _rstd(o) * g_ref[...]
            dhg_ref[:, sl] = (d * on * sg * (1.0 + hg * (1.0 - sg))).astype(dhg_ref.dtype)
            dx, dg = _rms_bwd(o, g_ref[...], d * hg * sg)
            do_ref[:, sl] = dx
            dg_ref[0, :, sl] += dg

    blk = pl.BlockSpec((tr, w), lambda j, i: (i, j))
    dirs = [pl.BlockSpec((1, tr, w), lambda j, i, d=d: (d, i, j)) for d in range(2)]
    return pl.pallas_call(
        body, name=name, grid=(HG_DIM // w, n // tr),
        in_specs=dirs + [pl.BlockSpec((tr, w), lambda j, i: (i, OFF_HG // w + j)), pl.BlockSpec((1, HG_HEAD_DIM), lambda j, i: (0, 0)),
                         pl.BlockSpec((tr, w), lambda j, i: (i, ATT_Q_DIM // w + j))] + after_specs,
        out_specs=[blk, blk, pl.BlockSpec((1, 1, w), lambda j, i: (j, 0, 0))],
        out_shape=[SDS((n, HG_DIM), F32), SDS((n, HG_DIM), MXU_DTYPE), SDS((HG_DIM // w, 1, w), F32)],
        compiler_params=_cp("parallel", "arbitrary"))(o2, o2, p, g, dcat, *after)


XATT_TQ = 512


def _xattn_fwd(q, kv, name):
    n, nm = q.shape[0], kv.shape[0]
    tq = min(XATT_TQ, n)
    scale = X_HEAD_DIM ** -0.5

    def body(q_ref, k_ref, v_ref, o_ref):
        s = _dot(q_ref[...], k_ref[...], "nt") * scale
        e = jnp.exp(s - jnp.max(s, axis=-1, keepdims=True))
        o_ref[...] = _dot(e / jnp.sum(e, axis=-1, keepdims=True), v_ref[...]).astype(o_ref.dtype)

    qb = pl.BlockSpec((tq, X_HEAD_DIM), lambda h, i: (i, h))
    return pl.pallas_call(
        body, name=name, grid=(X_HEADS, n // tq),
        in_specs=[qb, pl.BlockSpec((nm, X_HEAD_DIM), lambda h, i: (0, h)), pl.BlockSpec((nm, X_HEAD_DIM), lambda h, i: (0, X_HEADS + h))],
        out_specs=qb, out_shape=SDS(q.shape, MXU_DTYPE), compiler_params=_cp("parallel", "parallel"))(q, kv, kv)


def _xattn_bwd(q, kv, do, name, after=()):
    n, nm = q.shape[0], kv.shape[0]
    tq = min(XATT_TQ, n)
    scale = X_HEAD_DIM ** -0.5
    after, after_specs = _unread(after)

    def body(q_ref, k_ref, v_ref, do_ref, *rest):
        dq_ref, dk_ref, dv_ref = rest[len(after):]

        @pl.when(pl.program_id(1) == 0)
        def _():
            dk_ref[...] = jnp.zeros_like(dk_ref)
            dv_ref[...] = jnp.zeros_like(dv_ref)

        qv, dov = q_ref[...], do_ref[...]
        s = _dot(qv, k_ref[...], "nt") * scale
        e = jnp.exp(s - jnp.max(s, axis=-1, keepdims=True))
        p = e / jnp.sum(e, axis=-1, keepdims=True)
        dp = _dot(dov, v_ref[...], "nt")
        ds = p * (dp - jnp.sum(p * dp, axis=-1, keepdims=True)) * scale
        dq_ref[...] = _dot(ds, k_ref[...]).astype(dq_ref.dtype)
        dk_ref[...] += _dot(ds, qv, "tn")
        dv_ref[...] += _dot(p, dov, "tn")

    qb = pl.BlockSpec((tq, X_HEAD_DIM), lambda h, i: (i, h))
    kb = pl.BlockSpec((nm, X_HEAD_DIM), lambda h, i: (0, h))
    return pl.pallas_call(
        body, name=name, grid=(X_HEADS, n // tq),
        in_specs=[qb, kb, pl.BlockSpec((nm, X_HEAD_DIM), lambda h, i: (0, X_HEADS + h)), qb] + after_specs, out_specs=[qb, kb, kb],
        out_shape=[SDS(q.shape, MXU_DTYPE), SDS((nm, X_HEADS * X_HEAD_DIM), F32), SDS((nm, X_HEADS * X_HEAD_DIM), F32)],
        compiler_params=_cp("parallel", "arbitrary"))(q, kv, kv, do, *after)


def _edge_rows(shape):
    row = lax.broadcasted_iota(jnp.int32, shape, 0)
    return row == 0, row == shape[0] - 1


def _shift_rows(u, down, edges):
    if down:
        return jnp.where(edges[0], 0.0, pltpu.roll(u, 1, axis=0))
    return jnp.where(edges[1], 0.0, pltpu.roll(u, u.shape[0] - 1, axis=0))


def _conv(u, w, b, edges):
    return b + _shift_rows(u, True, edges) * w[0:1, :] + u * w[1:2, :] + _shift_rows(u, False, edges) * w[2:3, :]


def _ff_specs(n):
    gate = lambda rows: pl.BlockSpec((rows, FF_COLS), lambda j: (0, j))
    val = lambda rows: pl.BlockSpec((rows, FF_COLS), lambda j: (0, FF_BLOCKS + j))
    return [gate(n), val(n), gate(3), val(3), gate(1), val(1)], gate


def _conv_gate(u, cw, cb, name):
    n = u.shape[0]
    ins, gate_blk = _ff_specs(n)

    def body(ug_ref, uv_ref, wg_ref, wv_ref, bg_ref, bv_ref, o_ref):
        edges = _edge_rows(ug_ref.shape)
        gate = _conv(ug_ref[...], wg_ref[...], bg_ref[...], edges)
        val = _conv(uv_ref[...], wv_ref[...], bv_ref[...], edges)
        o_ref[...] = (gate * _sigmoid(gate) * val).astype(o_ref.dtype)

    return pl.pallas_call(
        body, name=name, grid=(FF_BLOCKS,), in_specs=ins, out_specs=gate_blk(n), out_shape=SDS((n, D_FF), MXU_DTYPE),
        compiler_params=_cp("parallel"))(u, u, cw, cw, cb, cb)


def _conv_gate_bwd(u, cw, cb, da, name, after=()):
    n = u.shape[0]
    ins, gate_blk = _ff_specs(n)
    after, after_specs = _unread(after)

    def side(dacc, u, w, edges, du_ref, dw_ref, db_ref):
        nxt, prv = _shift_rows(dacc, False, edges), _shift_rows(dacc, True, edges)
        du_ref[...] = (nxt * w[0:1, :] + dacc * w[1:2, :] + prv * w[2:3, :]).astype(du_ref.dtype)
        db_ref[...] = jnp.sum(dacc, axis=0, keepdims=True)
        dw_ref[0:1, :] = jnp.sum(nxt * u, axis=0, keepdims=True)
        dw_ref[1:2, :] = jnp.sum(dacc * u, axis=0, keepdims=True)
        dw_ref[2:3, :] = jnp.sum(prv * u, axis=0, keepdims=True)

    def body(ug_ref, uv_ref, wg_ref, wv_ref, bg_ref, bv_ref, da_ref, *rest):
        dug_ref, duv_ref, dwg_ref, dwv_ref, dbg_ref, dbv_ref = rest[len(after):]
        ug, uv = ug_ref[...], uv_ref[...]
        edges = _edge_rows(ug.shape)
        gate = _conv(ug, wg_ref[...], bg_ref[...], edges)
        val = _conv(uv, wv_ref[...], bv_ref[...], edges)
        sg = _sigmoid(gate)
        dav = da_ref[...].astype(F32)
        side(dav * val * sg * (1.0 + gate * (1.0 - sg)), ug, wg_ref[...], edges, dug_ref, dwg_ref, dbg_ref)
        side(dav * gate * sg, uv, wv_ref[...], edges, duv_ref, dwv_ref, dbv_ref)

    return pl.pallas_call(
        body, name=name, grid=(FF_BLOCKS,), in_specs=ins + [gate_blk(n)] + after_specs,
        out_specs=[gate_blk(n), gate_blk(n), gate_blk(3), gate_blk(3), gate_blk(1), gate_blk(1)],
        out_shape=[SDS((n, D_FF), MXU_DTYPE)] * 2 + [SDS((3, D_FF), F32)] * 2 + [SDS((1, D_FF), F32)] * 2,
        compiler_params=_cp("parallel"))(u, u, cw, cw, cb, cb, da, *after)


def _adamw(params, name):
    nt = len(params)
    rows = [p[0].shape[-2] for p in params]
    assert all(p[0].ndim == 2 or p[0].shape[:-2] == (1,) for p in params), name
    fits = lambda s: max(rows) <= s * ELEMENTWISE_ROWS and all(r % s == 0 and (s == 1 or r // s % 8 == 0) for r in rows)
    steps = next(s for s in range(1, max(rows) + 1) if fits(s))

    def body(*refs):
        for t in range(nt):
            w_ref, g_ref, m_ref, v_ref = refs[4 * t:4 * t + 4]
            d_ref, mo_ref, vo_ref, go_ref = refs[4 * (nt + t):4 * (nt + t) + 4]
            gv = g_ref[...]
            go_ref[...] = gv
            mn = ADAM_B1 * m_ref[...] + (1.0 - ADAM_B1) * gv
            vn = ADAM_B2 * v_ref[...] + (1.0 - ADAM_B2) * gv * gv
            m_hat = mn / (1.0 - ADAM_B1 ** ADAM_STEP)
            v_hat = vn / (1.0 - ADAM_B2 ** ADAM_STEP)
            d_ref[...] = -ADAM_LR * (m_hat / (jnp.sqrt(v_hat) + ADAM_EPS) + ADAM_WD * w_ref[...])
            mo_ref[...] = mn
            vo_ref[...] = vn

    def blk(w):
        tr, c = w.shape[-2] // steps, w.shape[-1]
        return pl.BlockSpec((tr, c), lambda i: (i, 0)) if w.ndim == 2 else pl.BlockSpec((1, tr, c), lambda i: (0, i, 0))

    specs = [blk(p[0]) for p in params for _ in range(4)]
    out = pl.pallas_call(body, name=name, grid=(steps,), in_specs=specs, out_specs=specs,
                         out_shape=[SDS(p[0].shape, F32) for p in params for _ in range(4)],
                         compiler_params=_cp("parallel"))(*[a for p in params for a in p])
    return [tuple(out[4 * t:4 * t + 4]) for t in range(nt)]


ANY = pl.BlockSpec(memory_space=pl.ANY)


def _place():
    x, y, c = lax.axis_index("x"), lax.axis_index("y"), lax.axis_index("c")
    return x, y, c, [(1 - x, y), (x, 1 - y), (1 - x, 1 - y)]


HBM = pl.BlockSpec(memory_space=pltpu.HBM)
SEM = pl.BlockSpec(memory_space=pltpu.SEMAPHORE)
TOKEN = pl.BlockSpec(memory_space=pltpu.VMEM)
TOKEN_SHAPE = SDS((8, 128), F32)
PEERS = 7


def _in_hbm(a):
    return pltpu.with_memory_space_constraint(a, pltpu.HBM)


def _split_params():
    return pltpu.CompilerParams(has_side_effects=pltpu.SideEffectType.DATAFLOW_SIDE_EFFECTING)


def _gather_start(shards, name, after=()):
    nt = len(shards)
    after, after_specs = _unread(after)

    def body(*refs):
        ins, lands = refs[:nt], refs[nt:2 * nt]
        outs = refs[2 * nt + len(after):]
        sends, recvs = outs[:nt], outs[nt:2 * nt]
        x, y, c, chips = _place()
        me = 2 * x + y
        for t in range(nt):
            h = ins[t].shape[0] // 2
            mine = pl.ds(c * h, h)
            for j, (cx, cy) in enumerate(chips):
                for dc in range(2):
                    pltpu.make_async_remote_copy(src_ref=ins[t].at[mine], dst_ref=lands[t].at[me, mine], send_sem=sends[t].at[2 * j + dc],
                                                 recv_sem=recvs[t].at[2 * j + c], device_id=(cx, cy, dc), device_id_type=MESH).start()
            pltpu.make_async_remote_copy(src_ref=ins[t], dst_ref=lands[t].at[me], send_sem=sends[t].at[PEERS - 1], recv_sem=recvs[t].at[PEERS - 1],
                                         device_id=(x, y, 1 - c), device_id_type=MESH).start()
        outs[-1][...] = jnp.zeros(TOKEN_SHAPE.shape, F32)

    lands = [lax.empty((4,) + s.shape, s.dtype) for s in shards]
    out = pl.pallas_call(
        body, name=name, in_specs=[HBM] * (2 * nt) + after_specs, out_specs=[SEM] * (2 * nt) + [HBM] * (2 * nt) + [TOKEN],
        out_shape=[pltpu.SemaphoreType.DMA((PEERS,))] * (2 * nt)
        + [pltpu.HBM(s.shape, s.dtype) for s in shards] + [pltpu.HBM(l.shape, l.dtype) for l in lands] + [TOKEN_SHAPE],
        input_output_aliases={t: 2 * nt + t for t in range(2 * nt)}, compiler_params=_split_params())(
            *[_in_hbm(s) for s in shards], *[_in_hbm(l) for l in lands], *after)
    return out[:nt], out[nt:2 * nt], out[2 * nt:3 * nt], out[3 * nt:4 * nt], out[-1]


def _gather_wait(sends, recvs, shards, lands, after, name):
    nt = len(shards)

    def body(*refs):
        ins, lands_ref = refs[:nt], refs[nt:2 * nt]
        send_refs, recv_refs = refs[2 * nt:3 * nt], refs[3 * nt:4 * nt]
        x, y, c, chips = _place()
        for t in range(nt):
            h = ins[t].shape[0] // 2
            for j, (cx, cy) in enumerate(chips):
                for cs in range(2):
                    blk = lands_ref[t].at[2 * cx + cy, pl.ds(cs * h, h)]
                    pltpu.make_async_remote_copy(src_ref=blk, dst_ref=blk, send_sem=send_refs[t].at[2 * j + cs], recv_sem=recv_refs[t].at[2 * j + cs],
                                                 device_id=(cx, cy, cs), device_id_type=MESH).wait()
            blk = lands_ref[t].at[2 * x + y]
            pltpu.make_async_remote_copy(src_ref=blk, dst_ref=blk, send_sem=send_refs[t].at[PEERS - 1], recv_sem=recv_refs[t].at[PEERS - 1],
                                         device_id=(x, y, 1 - c), device_id_type=MESH).wait()

    out = pl.pallas_call(
        body, name=name, in_specs=[HBM] * (2 * nt) + [SEM] * (2 * nt) + [ANY], out_specs=[HBM] * (2 * nt),
        out_shape=[pltpu.HBM(s.shape, s.dtype) for s in shards] + [pltpu.HBM(l.shape, l.dtype) for l in lands],
        input_output_aliases={t: t for t in range(2 * nt)}, compiler_params=_split_params())(*shards, *lands, *sends, *recvs, after)
    return out[nt:]


def _gather_pieces_start(v, shard, name):
    h = shard.shape[0] // 2

    def body(v_ref, v_land, src, land, v_send, v_recv, send, recv, *rest):
        x, y, c, chips = _place()
        for j, flips in enumerate(_flips()):
            pltpu.make_async_remote_copy(src_ref=v_ref, dst_ref=v_land.at[4 * x + 2 * y + c], send_sem=v_send.at[j], recv_sem=v_recv.at[j],
                                         device_id=_flipped(x, y, c, flips), device_id_type=MESH).start()
        me = 2 * x + y
        mine = pl.ds(c * h, h)
        for j, (cx, cy) in enumerate(chips):
            pltpu.make_async_remote_copy(src_ref=src.at[mine], dst_ref=land.at[me, mine], send_sem=send.at[j], recv_sem=recv.at[j],
                                         device_id=(cx, cy, c), device_id_type=MESH).start()
        pltpu.make_async_remote_copy(src_ref=src, dst_ref=land.at[me], send_sem=send.at[len(chips)], recv_sem=recv.at[len(chips)],
                                     device_id=(x, y, 1 - c), device_id_type=MESH).start()
        rest[-1][...] = jnp.zeros(TOKEN_SHAPE.shape, F32)

    v_land = lax.empty((8,) + v.shape, v.dtype)
    land = lax.empty((4,) + shard.shape, shard.dtype)
    passed = [v, v_land, shard, land]
    out = pl.pallas_call(
        body, name=name, in_specs=[HBM] * 4, out_specs=[SEM] * 4 + [HBM] * 4 + [TOKEN],
        out_shape=[pltpu.SemaphoreType.DMA((PEERS,))] * 2 + [pltpu.SemaphoreType.DMA((4,))] * 2
        + [pltpu.HBM(a.shape, a.dtype) for a in passed] + [TOKEN_SHAPE],
        input_output_aliases={t: 4 + t for t in range(4)}, compiler_params=_split_params())(*[_in_hbm(a) for a in passed])
    return (out[0], out[1], out[4], out[5]), (out[2], out[3], out[6], out[7]), out[8]


def _gather_pieces_wait(send, recv, shard, land, after, name):
    h = shard.shape[0] // 2
    after, after_specs = _unread(after)

    def body(*refs):
        land_ref, send_ref, recv_ref = refs[1:4]
        x, y, c, chips = _place()
        for j, (cx, cy) in enumerate(chips):
            blk = land_ref.at[2 * cx + cy, pl.ds(c * h, h)]
            pltpu.make_async_remote_copy(src_ref=blk, dst_ref=blk, send_sem=send_ref.at[j], recv_sem=recv_ref.at[j],
                                         device_id=(cx, cy, c), device_id_type=MESH).wait()
        own = land_ref.at[2 * x + y]
        pltpu.make_async_remote_copy(src_ref=own, dst_ref=own, send_sem=send_ref.at[len(chips)], recv_sem=recv_ref.at[len(chips)],
                                     device_id=(x, y, 1 - c), device_id_type=MESH).wait()

    out = pl.pallas_call(
        body, name=name, in_specs=[HBM, HBM, SEM, SEM] + after_specs, out_specs=[HBM, HBM],
        out_shape=[pltpu.HBM(shard.shape, shard.dtype), pltpu.HBM(land.shape, land.dtype)],
        input_output_aliases={0: 0, 1: 1}, compiler_params=_split_params())(shard, land, send, recv, *after)
    return out[1]


def _pass_pieces(land, name):
    h = land.shape[1] // 2

    def body(_, out, send, recv):
        x, y, c, chips = _place()

        def piece(j, cc):
            cx, cy = chips[j]
            blk = out.at[2 * cx + cy, pl.ds(cc * h, h)]
            return pltpu.make_async_remote_copy(src_ref=blk, dst_ref=blk, send_sem=send.at[j], recv_sem=recv.at[j],
                                                device_id=(x, y, 1 - c), device_id_type=MESH)

        for j in range(len(chips)):
            piece(j, c).start()
        for j in range(len(chips)):
            piece(j, 1 - c).wait_recv()
        for j in range(len(chips)):
            piece(j, c).wait_send()

    return pl.pallas_call(
        body, name=name, in_specs=[ANY], out_specs=ANY, out_shape=SDS(land.shape, land.dtype), input_output_aliases={0: 0},
        scratch_shapes=[pltpu.SemaphoreType.DMA((3,))] * 2, compiler_params=pltpu.CompilerParams(has_side_effects=True))(land)


def _scatter_start(gs, name):
    nt = len(gs)

    def body(*refs):
        g_refs, lands = refs[:nt], refs[nt:2 * nt]
        sends, recvs = refs[2 * nt:3 * nt], refs[3 * nt:4 * nt]
        x, y, c, chips = _place()
        for g_ref, land, send, recv in zip(g_refs, lands, sends, recvs):
            h = land.shape[1]
            for j, (cx, cy) in enumerate(chips):
                for dc in range(2):
                    pltpu.make_async_remote_copy(src_ref=g_ref.at[2 * cx + cy, pl.ds(dc * h, h)], dst_ref=land.at[2 * j + c],
                                                 send_sem=send.at[2 * j + dc], recv_sem=recv.at[2 * j + c], device_id=(cx, cy, dc),
                                                 device_id_type=MESH).start()
            pltpu.make_async_remote_copy(src_ref=g_ref.at[2 * x + y, pl.ds((1 - c) * h, h)], dst_ref=land.at[PEERS - 1], send_sem=send.at[PEERS - 1],
                                         recv_sem=recv.at[PEERS - 1], device_id=(x, y, 1 - c), device_id_type=MESH).start()
        refs[-1][...] = jnp.zeros(TOKEN_SHAPE.shape, F32)

    lands = [lax.empty((PEERS, g.shape[1] // 2, g.shape[2]), g.dtype) for g in gs]
    out = pl.pallas_call(
        body, name=name, in_specs=[HBM] * (2 * nt), out_specs=[SEM] * (2 * nt) + [HBM] * (2 * nt) + [TOKEN],
        out_shape=[pltpu.SemaphoreType.DMA((PEERS,))] * (2 * nt) + [pltpu.HBM(a.shape, a.dtype) for a in gs + lands] + [TOKEN_SHAPE],
        input_output_aliases={t: 2 * nt + t for t in range(2 * nt)}, compiler_params=_split_params())(
            *[_in_hbm(a) for a in gs + lands])
    return [tuple(out[q * nt + t] for q in range(4)) for t in range(nt)], out[-1]


def _scatter_wait(started, after, name):
    nt = len(started)

    def body(*refs):
        lands = refs[nt:2 * nt]
        sends, recvs = refs[2 * nt:3 * nt], refs[3 * nt:4 * nt]
        x, y, c, chips = _place()
        peers = [(cx, cy, dc) for cx, cy in chips for dc in range(2)] + [(x, y, 1 - c)]
        for t in range(nt):
            for k, peer in enumerate(peers):
                blk = lands[t].at[k]
                pltpu.make_async_remote_copy(src_ref=blk, dst_ref=blk, send_sem=sends[t].at[k], recv_sem=recvs[t].at[k],
                                             device_id=peer, device_id_type=MESH).wait()

    gs, lands = [s[2] for s in started], [s[3] for s in started]
    after, after_specs = _unread(after)
    out = pl.pallas_call(
        body, name=name, in_specs=[HBM] * (2 * nt) + [SEM] * (2 * nt) + after_specs, out_specs=[HBM] * (2 * nt),
        out_shape=[pltpu.HBM(a.shape, a.dtype) for a in gs + lands],
        input_output_aliases={t: t for t in range(2 * nt)}, compiler_params=_split_params())(
            *gs, *lands, *[s[0] for s in started], *[s[1] for s in started], *after)
    return out[:nt], out[nt:]


def _join_start(bufs, name, after=()):
    nt = len(bufs)
    after, after_specs = _unread(after)

    def body(*refs):
        send, recv = refs[nt + len(after):nt + len(after) + 2]
        x, y, c, _ = _place()
        for t in range(nt):
            pltpu.make_async_remote_copy(src_ref=refs[t].at[c], dst_ref=refs[t].at[c], send_sem=send.at[t], recv_sem=recv.at[t],
                                         device_id=(x, y, 1 - c), device_id_type=MESH).start()
        refs[-1][...] = jnp.zeros(TOKEN_SHAPE.shape, F32)

    out = pl.pallas_call(
        body, name=name, in_specs=[HBM] * nt + after_specs, out_specs=[SEM, SEM] + [HBM] * nt + [TOKEN],
        out_shape=[pltpu.SemaphoreType.DMA((nt,))] * 2 + [pltpu.HBM(b.shape, b.dtype) for b in bufs] + [TOKEN_SHAPE],
        input_output_aliases={t: 2 + t for t in range(nt)}, compiler_params=_split_params())(*[_in_hbm(b) for b in bufs], *after)
    return out[0], out[1], out[2:2 + nt], out[-1]


def _join_wait(send, recv, bufs, after, name):
    nt = len(bufs)
    after, after_specs = _unread(after)

    def body(*refs):
        send_ref, recv_ref = refs[nt:nt + 2]
        x, y, c, _ = _place()
        for t in range(nt):
            theirs = refs[t].at[1 - c]
            pltpu.make_async_remote_copy(src_ref=theirs, dst_ref=theirs, send_sem=send_ref.at[t], recv_sem=recv_ref.at[t],
                                         device_id=(x, y, 1 - c), device_id_type=MESH).wait()

    return pl.pallas_call(
        body, name=name, in_specs=[HBM] * nt + [SEM, SEM] + after_specs, out_specs=[HBM] * nt,
        out_shape=[pltpu.HBM(b.shape, b.dtype) for b in bufs],
        input_output_aliases={t: t for t in range(nt)}, compiler_params=_split_params())(*bufs, send, recv, *after)


def _flips():
    return [(dx, dy, dc) for dx in range(2) for dy in range(2) for dc in range(2) if (dx, dy, dc) != (0, 0, 0)]


def _flipped(x, y, c, flips):
    dx, dy, dc = flips
    return (1 - x if dx else x, 1 - y if dy else y, 1 - c if dc else c)


def _small_start(v, name, after=()):
    after, after_specs = _unread(after)

    def body(v_ref, land, *rest):
        send, recv = rest[len(after):len(after) + 2]
        x, y, c, _ = _place()
        for j, flips in enumerate(_flips()):
            pltpu.make_async_remote_copy(src_ref=v_ref, dst_ref=land.at[4 * x + 2 * y + c], send_sem=send.at[j], recv_sem=recv.at[j],
                                         device_id=_flipped(x, y, c, flips), device_id_type=MESH).start()
        rest[-1][...] = jnp.zeros(TOKEN_SHAPE.shape, F32)

    land = lax.empty((8,) + v.shape, v.dtype)
    return pl.pallas_call(
        body, name=name, in_specs=[HBM, HBM] + after_specs, out_specs=[SEM, SEM, HBM, HBM, TOKEN],
        out_shape=[pltpu.SemaphoreType.DMA((PEERS,)), pltpu.SemaphoreType.DMA((PEERS,)), pltpu.HBM(v.shape, v.dtype),
                   pltpu.HBM(land.shape, land.dtype), TOKEN_SHAPE],
        input_output_aliases={0: 2, 1: 3}, compiler_params=_split_params())(_in_hbm(v), _in_hbm(land), *after)


def _small_wait(send, recv, v, land, after, name):
    after, after_specs = _unread(after)

    def body(v_ref, land_ref, send_ref, recv_ref, *rest):
        x, y, c, _ = _place()
        for j, flips in enumerate(_flips()):
            px, py, pc = _flipped(x, y, c, flips)
            blk = land_ref.at[4 * px + 2 * py + pc]
            pltpu.make_async_remote_copy(src_ref=blk, dst_ref=blk, send_sem=send_ref.at[j], recv_sem=recv_ref.at[j],
                                         device_id=(px, py, pc), device_id_type=MESH).wait()

    return pl.pallas_call(
        body, name=name, in_specs=[HBM, HBM, SEM, SEM] + after_specs, out_specs=[HBM, HBM],
        out_shape=[pltpu.HBM(v.shape, v.dtype), pltpu.HBM(land.shape, land.dtype)],
        input_output_aliases={0: 0, 1: 1}, compiler_params=_split_params())(v, land, send, recv, *after)


def _sum_small(v, land, name):
    def body(v_ref, land_ref, o_ref):
        x, y, c, _ = _place()
        me = 4 * x + 2 * y + c
        acc = jnp.where(me == 0, v_ref[...], land_ref[0])
        for d in range(1, 8):
            acc = acc + jnp.where(me == d, v_ref[...], land_ref[d])
        o_ref[...] = acc

    vm = pl.BlockSpec(memory_space=pltpu.VMEM)
    return pl.pallas_call(body, name=name, in_specs=[vm, vm], out_specs=vm, out_shape=SDS(v.shape, F32))(v, land)


SUM_STEPS = 2


def _sum_devices(gs, lands, me, core, name):
    nt = len(gs)

    def body(ix_ref, *refs):
        for own_ref, land_ref, o_ref in zip(refs[:nt], refs[nt:2 * nt], refs[2 * nt:]):
            acc = own_ref[0].astype(F32)
            for j in range(land_ref.shape[0]):
                acc = acc + land_ref[j].astype(F32)
            o_ref[0] = acc

    tiles = [(land.shape[1] // SUM_STEPS, land.shape[2]) for land in lands]
    assert all(land.shape[1] == tr * SUM_STEPS and tr % ROWS_PER_16BIT_TILE == 0 for land, (tr, _) in zip(lands, tiles)), name
    grid_spec = pltpu.PrefetchScalarGridSpec(
        num_scalar_prefetch=1, grid=(SUM_STEPS,),
        in_specs=[pl.BlockSpec((1, tr, c), lambda i, ix: (ix[0], ix[1] * SUM_STEPS + i, 0)) for tr, c in tiles]
        + [pl.BlockSpec((land.shape[0], tr, c), lambda i, ix: (0, i, 0)) for land, (tr, c) in zip(lands, tiles)],
        out_specs=[pl.BlockSpec((1, tr, c), lambda i, ix: (ix[1], i, 0)) for tr, c in tiles])
    return pl.pallas_call(body, name=name, grid_spec=grid_spec, out_shape=[SDS((2,) + land.shape[1:], F32) for land in lands],
                          compiler_params=_cp("parallel"))(jnp.stack([me, core]), *gs, *lands)


def _pack_small(parts):
    flat = jnp.concatenate([p.reshape(-1) for p in parts])
    total = flat.shape[0]
    rows = -(-total // 1024) * 8
    return jnp.pad(flat, (0, rows * 128 - total)).reshape(rows, 128)


def _unpack_small(packed, shapes):
    flat = packed.reshape(-1)
    out, off = [], 0
    for s in shapes:
        size = int(np.prod(s))
        out.append(flat[off:off + size].reshape(s))
        off += size
    return out


def _local_step(x, mem, target, w_in_t, first_after, mid_weights, ffn_weights, on_grad, gains, conv_w, conv_b, hg_lb):
    n = x.shape[0]
    cos, sin = _rope_tables(n)
    seg = _hg_segments()
    gp, gn = _hg_pair_sums()
    masks = _hg_level_masks()
    gq2 = jnp.tile(gains["q_norm_g"], (1, 2))
    gk2 = jnp.tile(gains["k_norm_g"], (1, 2))
    a0 = hg_lb[:, 0:1, :]
    a1 = hg_lb[:, 1:2, :]

    p, h1 = _norm_mm(x, gains["pre_mix_g"], w_in_t, F32, TOKEN_TILE, 1664, "in_proj", after=(first_after,), w_turned=True)
    qr, kr = _qk_prep(p, gq2, gk2, cos, sin, "qk_prep")
    heads = lambda a: a.reshape(n, ATT_KV_HEADS, ATT_HEAD_DIM).transpose(1, 0, 2)
    kh = heads(kr)
    vh = heads(p[:, OFF_AV:OFF_AV + ATT_KV_DIM].astype(MXU_DTYPE))
    att = _attn_fwd(qr, kh, vh, "attn_fwd")
    o2, s0, hg_a, hg_e, hg_kept = _hgrn_fwd(p, a0, a1, seg, masks, "hgrn_fwd")
    rec = _hg_post(o2, p, gains["hg_out_norm_g"], "hg_post")
    cat = jnp.concatenate([att, rec], axis=1)
    w_out, w_xq, w_xkv, w_xo = mid_weights(cat)
    mixed, x1 = _mm_resid_norm(cat, w_out, x, gains["post_mix_g"], 512, "out_proj_resid")
    xq, h2 = _norm_mm(x1, gains["pre_x_g"], w_xq, MXU_DTYPE, TOKEN_TILE, 1024, "xq_proj")
    kv, mn = _norm_mm(mem, gains["mem_norm_g"], w_xkv, MXU_DTYPE, 256, 2048, "xkv_proj")
    ox = _xattn_fwd(xq, kv, "xattn_fwd")
    xo, x2 = _mm_resid_norm(ox, w_xo, x1, gains["post_x_g"], 512, "xo_proj_resid")
    w_up = ffn_weights("w_up", x2)
    u, h3 = _norm_mm(x2, gains["pre_ffn_g"], w_up, F32, TOKEN_TILE, 1408, "up_proj")
    act = _conv_gate(u, conv_w, conv_b, "conv_gate")
    w_down = ffn_weights("w_down", act)
    dn, d3, loss = _mm_resid_norm(act, w_down, x2, gains["post_ffn_g"], 512, "down_proj_resid_loss", target=target)

    gs = {}
    d_act, d_dn, gs["post_ffn_g"] = _norm_bwd_mm(dn, gains["post_ffn_g"], d3, w_down, F32, 512, 1408, "ffn_post_bwd_down_dx")
    tok = on_grad("w_down", _mm(act, d_dn, "tn", WIRE_DTYPE, 1408, 1024, "down_dw"))
    du_g, du_v, dcw_g, dcw_v, dcb_g, dcb_v = _conv_gate_bwd(u, conv_w, conv_b, d_act, "conv_gate_bwd", after=(tok,))
    gs["conv_w"] = jnp.concatenate([dcw_g, dcw_v], axis=1)
    gs["conv_b"] = jnp.concatenate([dcb_g, dcb_v], axis=1)
    ff_shard = w_up.shape[2]
    g_up = _dw_by_owner(h3, du_g, ff_shard, 0, None, 512, "up_dw_gate")
    tok = on_grad("w_up", _dw_by_owner(h3, du_v, ff_shard, 2, g_up, 512, "up_dw_value"))
    d2, gs["pre_ffn_g"] = _dx_norm_bwd([(du_g, 0), (du_g, 1), (du_v, 0), (du_v, 1)], w_up, x2, gains["pre_ffn_g"], d3, 512,
                                       "up_dx_pre_bwd", after=(tok,))
    d_ox, d_xo, gs["post_x_g"] = _norm_bwd_mm(xo, gains["post_x_g"], d2, w_xo, MXU_DTYPE, 512, 1024, "x_post_bwd_xo_dx")
    tok = on_grad("w_xo", _mm(ox, d_xo, "tn", WIRE_DTYPE, 512, 1024, "xo_dw"))
    d_xq, d_k, d_v = _xattn_bwd(xq, kv, d_ox, "xattn_bwd", after=(tok,))
    d_kv = jnp.concatenate([d_k, d_v], axis=1).astype(MXU_DTYPE)
    tok = on_grad("w_xq", _mm(h2, d_xq, "tn", WIRE_DTYPE, 512, 1024, "xq_dw"))
    tok_kv = on_grad("w_xkv", _dw_by_owner(mn, d_kv, w_xkv.shape[2], 0, None, 512, "xkv_dw"))
    d1, gs["pre_x_g"] = _dx_norm_bwd([(d_xq, 0)], w_xq[None], x1, gains["pre_x_g"], d2, 512, "xq_dx_pre_bwd", after=(tok, tok_kv))
    d_mn = _mm_nt_parts([(d_kv, s) for s in range(4)], w_xkv, F32, 256, 1024, "xkv_dx")
    _, gs["mem_norm_g"] = _norm_bwd(mem, gains["mem_norm_g"], d_mn, None, MXU_DTYPE, "mem_norm_bwd")
    d_cat, d_mixed, gs["post_mix_g"] = _norm_bwd_mm(mixed, gains["post_mix_g"], d1, w_out, MXU_DTYPE, 512, 1024, "mix_post_bwd_out_dx")
    tok = on_grad("w_out", _mm(cat, d_mixed, "tn", WIRE_DTYPE, 512, 1024, "out_dw"))
    d_o, d_hg, dg_hg = _hg_post_bwd(o2, p, gains["hg_out_norm_g"], d_cat, "hg_post_bwd", after=(tok,))
    gs["hg_out_norm_g"] = dg_hg.reshape(HG_HEADS, HG_HEAD_DIM).sum(axis=0, keepdims=True)
    dhq2, dz2, dhv2, dlb = _hgrn_bwd(p, a0, a1, masks, gp, gn, d_o, s0, hg_a, hg_e, hg_kept, "hgrn_bwd")
    lb = jax.nn.sigmoid(a0 - a1)
    da0 = dlb * lb * (1.0 - lb)
    gs["hg_lb"] = jnp.concatenate([da0, -da0], axis=1)
    d_qr, d_kh, d_vh = _attn_bwd(qr, kh, vh, cat, d_cat, "attn_bwd")
    unheads = lambda a: a.transpose(2, 0, 1).reshape(n, ATT_KV_DIM)
    d_aq, d_ak, dgq, dgk = _qk_prep_bwd(p, gq2, gk2, cos, sin, d_qr, unheads(d_kh), "qk_prep_bwd")
    gs["q_norm_g"] = dgq.reshape(ATT_HEADS, ATT_HEAD_DIM).sum(axis=0, keepdims=True)
    gs["k_norm_g"] = dgk.reshape(ATT_KV_HEADS, ATT_HEAD_DIM).sum(axis=0, keepdims=True)
    d_p = jnp.concatenate([d_aq, d_ak, unheads(d_vh).astype(MXU_DTYPE), (dhq2[0] + dhq2[1]).astype(MXU_DTYPE),
                           dz2[0].astype(MXU_DTYPE), dz2[1].astype(MXU_DTYPE), (dhv2[0] + dhv2[1]).astype(MXU_DTYPE), d_hg], axis=1)
    tok = on_grad("w_in", _mm(d_p, h1, "tn", WIRE_DTYPE, 1664, 1024, "in_dw"))
    grad_x, gs["pre_mix_g"] = _dx_norm_bwd([(d_p, 0)], w_in_t[None], x, gains["pre_mix_g"], d1, 512, "in_dx_pre_bwd", after=(tok,),
                                           b_turned=True)
    return loss, grad_x, gs


MATS = ("w_in", "w_out", "w_xq", "w_xkv", "w_xo", "w_up", "w_down")
GAINS = ("pre_mix_g", "q_norm_g", "k_norm_g", "hg_out_norm_g", "post_mix_g", "pre_x_g", "mem_norm_g", "post_x_g", "pre_ffn_g", "post_ffn_g")
WEIGHTS = ('pre_mix_g', 'w_in', 'q_norm_g', 'k_norm_g', 'hg_lb', 'hg_out_norm_g', 'w_out', 'post_mix_g', 'pre_x_g', 'mem_norm_g', 'w_xq',
           'w_xkv', 'w_xo', 'post_x_g', 'pre_ffn_g', 'w_up', 'conv_w', 'conv_b', 'w_down', 'post_ffn_g')


def kernel(x, mem, pre_mix_g, w_in, q_norm_g, k_norm_g, hg_lb, hg_out_norm_g, w_out, post_mix_g, pre_x_g, mem_norm_g, w_xq, w_xkv, w_xo, post_x_g, pre_ffn_g, w_up, conv_w, conv_b, w_down, post_ffn_g, loss_target, m_pre_mix_g, m_w_in, m_q_norm_g, m_k_norm_g, m_hg_lb, m_hg_out_norm_g, m_w_out, m_post_mix_g, m_pre_x_g, m_mem_norm_g, m_w_xq, m_w_xkv, m_w_xo, m_post_x_g, m_pre_ffn_g, m_w_up, m_conv_w, m_conv_b, m_w_down, m_post_ffn_g, v_pre_mix_g, v_w_in, v_q_norm_g, v_k_norm_g, v_hg_lb, v_hg_out_norm_g, v_w_out, v_post_mix_g, v_pre_x_g, v_mem_norm_g, v_w_xq, v_w_xkv, v_w_xo, v_post_x_g, v_pre_ffn_g, v_w_up, v_conv_w, v_conv_b, v_w_down, v_post_ffn_g):
    args = dict(locals())
    w = {k: args[k] for k in WEIGHTS}
    m = {k: args["m_" + k] for k in WEIGHTS}
    v = {k: args["v_" + k] for k in WEIGHTS}
    chip = 2 * lax.axis_index("x") + lax.axis_index("y")
    core = lax.axis_index("c")

    turned = ("w_in",)
    shards = {k: (jnp.swapaxes(w[k], 1, 2) if k in turned else w[k])[0].astype(WIRE_DTYPE) for k in MATS}

    def whole(k, g):
        return g if k in ("w_xkv", "w_up") else g.reshape(-1, g.shape[-1])

    mid_names, ffn_names = ("w_out", "w_xq", "w_xkv", "w_xo"), ("w_up", "w_down")
    small, w_in_pieces, token = _gather_pieces_start(_pack_small([w["conv_w"][0], w["hg_lb"]]), shards["w_in"], "gather_first_start")
    mid = _gather_start([shards[k] for k in mid_names], "gather_mid_start", after=(token,))
    ffn = _gather_start([shards[k] for k in ffn_names], "gather_ffn_start", after=(mid[4],))
    w_in_t = whole("w_in", _pass_pieces(_gather_pieces_wait(*w_in_pieces, (ffn[4],), "gather_w_in_wait"), "gather_w_in_pass"))
    mine, others = _small_wait(*small, (w_in_t,), "gather_small_wait")
    small_in = lax.dynamic_update_slice_in_dim(others, mine[None], 2 * chip + core, axis=0)

    def mid_weights(after):
        return [whole(k, g) for k, g in zip(mid_names, _gather_wait(*mid[:4], after, "gather_mid_wait"))]

    def ffn_weights(k, after):
        t = ffn_names.index(k)
        return whole(k, _gather_wait(*[part[t:t + 1] for part in ffn[:4]], after, "gather_wait_" + k)[0])

    cw_parts, lb_parts = [], []
    for s in range(4):
        cw_s, lb_s = _unpack_small(small_in[2 * s], [w["conv_w"][0].shape, w["hg_lb"].shape])
        cw_parts.append(cw_s)
        lb_parts.append(lb_s)
    conv_w_full = jnp.concatenate(cw_parts, axis=1)
    hg_lb_full = jnp.concatenate(lb_parts, axis=2)

    started, held = {}, {}
    leaves_with_next = ("w_down", "w_xo", "w_xq")

    def on_grad(k, g):
        if g.ndim == 2:
            g = g.reshape(4, g.shape[0] // 4, g.shape[1])
        held[k] = g
        if k in leaves_with_next:
            return None
        names = tuple(held)
        per_tensor, token = _scatter_start([held.pop(n) for n in names], "grad_start_" + k)
        started.update(zip(names, per_tensor))
        return token

    gains = {k: w[k] for k in GAINS}
    loss_part, grad_x, gs = _local_step(x[0], mem[0], loss_target[0], w_in_t, ffn[4], mid_weights, ffn_weights, on_grad, gains,
                                        conv_w_full, w["conv_b"], hg_lb_full)
    gs["loss"] = loss_part

    grads, delta, new_m, new_v = {}, {}, {}, {}

    def sum_and_send(names, after, tag):
        sent, landed = _scatter_wait([started[k] for k in names], after, "grad_wait_" + tag)
        return _join_start(_sum_devices(sent, landed, chip, core, "grad_sum_" + tag), "grad_join_start_" + tag)

    def joined(names, join, after, tag):
        for k, r in zip(names, _join_wait(*join[:3], after, "grad_join_wait_" + tag)):
            grads[k] = r.reshape(1, -1, r.shape[-1])

    def adamw(names, tag, packed=None):
        params, backs = [], []
        for k in names:
            shape = w[k].shape
            keep = len(shape) == 3 and shape[0] == 1
            if k in turned:
                view, back = (lambda a: jnp.swapaxes(a, 1, 2)), (lambda a: jnp.swapaxes(a, 1, 2))
                g = grads[k]
            else:
                view = (lambda a, shape=shape: a.reshape(shape)) if keep else (lambda a, shape=shape: a.reshape(-1, shape[-1]))
                back = lambda a, shape=shape: a.reshape(shape)
                g = view(grads[k])
            params.append((view(w[k]), g, view(m[k]), view(v[k])))
            backs.append(back)
        outs = _adamw(params + ([packed] if packed else []), "adamw_" + tag)
        for k, back, (d, mo, vo, go) in zip(names, backs, outs):
            delta[k], new_m[k], new_v[k], grads[k] = back(d), back(mo), back(vo), back(go)
        return outs[-1]

    small_names = GAINS + ("conv_b", "conv_w", "hg_lb")
    packed = _pack_small([gs[k] for k in small_names + ("loss",)])
    small = _small_start(packed, "reduce_small_start", after=(grad_x,))

    ffn_join = sum_and_send(ffn_names, (grad_x, small[4]), "ffn")
    mid_join = sum_and_send(mid_names, (ffn_join[3],), "mid")
    joined(ffn_names, ffn_join, (mid_join[3],), "ffn")
    adamw(ffn_names, "ffn")
    joined(mid_names, mid_join, tuple(new_v[k] for k in ffn_names), "mid")
    adamw(mid_names, "mid")
    early = mid_names + ffn_names
    w_in_join = sum_and_send(("w_in",), tuple(new_v[k] for k in early), "w_in")

    mine, others = _small_wait(*small[:4], (w_in_join[3],), "reduce_small_wait")
    reduced_small = _sum_small(mine, others, "reduce_small_sum")
    *summed, loss = _unpack_small(reduced_small, [gs[k].shape for k in small_names + ("loss",)])
    loss = loss[0, 0]
    for k, g in zip(small_names, summed):
        grads[k] = g
    ncw = w["conv_w"].shape[2]
    grads["conv_w"] = lax.dynamic_slice_in_dim(grads["conv_w"], chip * ncw, ncw, axis=1)[None]
    nlb = w["hg_lb"].shape[2]
    grads["hg_lb"] = lax.dynamic_slice_in_dim(grads["hg_lb"], chip * nlb, nlb, axis=2)
    replicated = GAINS + ("conv_b",)
    shapes = [w[k].shape for k in replicated]
    rows = sum(int(np.prod(s)) for s in shapes) // 128
    pack = lambda d: jnp.concatenate([d[k].reshape(-1) for k in replicated]).reshape(rows, 128)
    outs = adamw(("conv_w", "hg_lb"), "small", packed=(pack(w), reduced_small[:rows], pack(m), pack(v)))
    for into, packed_out in zip((delta, new_m, new_v, grads), outs):
        for k, a in zip(replicated, _unpack_small(packed_out, shapes)):
            into[k] = a

    joined(("w_in",), w_in_join, tuple(new_v[k] for k in small_names), "w_in")
    adamw(("w_in",), "w_in")
    return (loss, grad_x[None], *[grads[k] for k in WEIGHTS], *[delta[k] for k in WEIGHTS],
            *[new_m[k] for k in WEIGHTS], *[new_v[k] for k in WEIGHTS])
```

```python
import numpy as np
import jax
import jax.numpy as jnp
from jax import lax
from jax.experimental import pallas as pl
from jax.experimental.pallas import tpu as pltpu

F32 = jnp.float32
MXU_DTYPE = jnp.bfloat16
WIRE_DTYPE = jnp.bfloat16
VMEM_LIMIT_BYTES = 56 * 1024 * 1024
ROWS_PER_16BIT_TILE = 16
ELEMENTWISE_ROWS = 256
EPS = 1e-6
MESH = pl.DeviceIdType.MESH

GRID_W = 64
ATT_HEADS, ATT_KV_HEADS, ATT_HEAD_DIM = 8, 2, 64
ATT_GROUP = ATT_HEADS // ATT_KV_HEADS
ATT_Q_DIM, ATT_KV_DIM = 512, 128
ROPE_THETA = 10000.0
HG_HEADS, HG_HEAD_DIM, HG_DIM = 4, 128, 512
HG_CHUNK = 128
HG_LEVELS = 7
HG_PAIR = 2 * HG_HEAD_DIM
HG_KEPT = 7
X_HEADS, X_HEAD_DIM = 4, 256
D_FF = 2816
FF_COLS = 256
FF_BLOCKS = D_FF // FF_COLS
OFF_AK, OFF_AV, OFF_HQ, OFF_ZF, OFF_ZB, OFF_HI, OFF_HG = 512, 640, 768, 1280, 1792, 2304, 2816

ADAM_LR, ADAM_B1, ADAM_B2, ADAM_EPS, ADAM_WD, ADAM_STEP = 0.001, 0.9, 0.999, 1e-08, 0.01, 10

SDS = jax.ShapeDtypeStruct


def _cp(*sem):
    return pltpu.CompilerParams(dimension_semantics=sem, vmem_limit_bytes=VMEM_LIMIT_BYTES)


def _dot(a, b, form="nn"):
    dims = {"nn": (((1,), (0,)), ((), ())), "nt": (((1,), (1,)), ((), ())), "tn": (((0,), (0,)), ((), ()))}[form]
    return lax.dot_general(a.astype(MXU_DTYPE), b.astype(MXU_DTYPE), dims, preferred_element_type=F32)


def _sigmoid(x):
    return 1.0 / (1.0 + jnp.exp(-x))


def _rstd(x):
    return lax.rsqrt(jnp.mean(x * x, axis=-1, keepdims=True) + EPS)


def _rms_bwd(x, g, dy):
    r = _rstd(x)
    xh = x * r
    dn = dy * g
    dx = r * (dn - xh * jnp.mean(dn * xh, axis=-1, keepdims=True))
    return dx, jnp.sum(dy * xh, axis=0, keepdims=True)


def _unread(after):
    after = tuple(a for a in after if a is not None)
    return after, [pl.BlockSpec(memory_space=pl.ANY)] * len(after)


def _mm(a, b, form, out_dtype, tm, tn, name, after=()):
    after, after_specs = _unread(after)
    if form == "nn":
        (m, k), n = a.shape, b.shape[1]
    elif form == "nt":
        (m, k), n = a.shape, b.shape[0]
    else:
        (k, m), n = a.shape, b.shape[1]
    tm, tn = min(tm, m), min(tn, n)
    assert m % tm == 0 and n % tn == 0, (name, m, n, tm, tn)

    def body(a_ref, b_ref, *rest):
        o_ref = rest[-1]
        o_ref[...] = _dot(a_ref[...], b_ref[...], form).astype(o_ref.dtype)

    a_spec = pl.BlockSpec((k, tm), lambda i, j: (0, i)) if form == "tn" else pl.BlockSpec((tm, k), lambda i, j: (i, 0))
    b_spec = pl.BlockSpec((tn, k), lambda i, j: (j, 0)) if form == "nt" else pl.BlockSpec((k, tn), lambda i, j: (0, j))
    return pl.pallas_call(
        body, name=name, grid=(m // tm, n // tn), in_specs=[a_spec, b_spec] + after_specs,
        out_specs=pl.BlockSpec((tm, tn), lambda i, j: (i, j)), out_shape=SDS((m, n), out_dtype),
        compiler_params=_cp("parallel", "parallel"))(a, b, *after)


def _mm_nt_parts(a_parts, b, out_dtype, tm, tn, name, after=()):
    after, after_specs = _unread(after)
    parts, n, p = b.shape
    m = a_parts[0][0].shape[0]
    tm, tn = min(tm, m), min(tn, n)
    assert m % tm == 0 and n % tn == 0 and len(a_parts) == parts, (name, m, b.shape)

    def body(*refs):
        o_ref = refs[-1]
        acc = _dot(refs[0][...], refs[parts][0], "nt")
        for s in range(1, parts):
            acc = acc + _dot(refs[s][...], refs[parts + s][0], "nt")
        o_ref[...] = acc.astype(o_ref.dtype)

    a_specs = [pl.BlockSpec((tm, p), lambda i, j, cb=cb: (i, cb)) for _, cb in a_parts]
    b_specs = [pl.BlockSpec((1, tn, p), lambda i, j, s=s: (s, j, 0)) for s in range(parts)]
    return pl.pallas_call(
        body, name=name, grid=(m // tm, n // tn), in_specs=a_specs + b_specs + after_specs,
        out_specs=pl.BlockSpec((tm, tn), lambda i, j: (i, j)), out_shape=SDS((m, n), out_dtype),
        compiler_params=_cp("parallel", "parallel"))(*[arr for arr, _ in a_parts], *([b] * parts), *after)


def _norm_bwd_mm(y, g, d, w, out_dtype, tm, tn, name):
    n, dm = y.shape
    nn = w.shape[0]
    tm, tn = min(tm, n), min(tn, nn)
    assert n % tm == 0 and nn % tn == 0 and w.shape[1] == dm, (name, y.shape, w.shape)

    def body(y_ref, g_ref, d_ref, w_ref, dx_ref, dy_ref, dg_ref, dys):
        i, j = pl.program_id(0), pl.program_id(1)

        @pl.when(jnp.logical_and(i == 0, j == 0))
        def _():
            dg_ref[...] = jnp.zeros_like(dg_ref)

        @pl.when(j == 0)
        def _():
            dy, dg = _rms_bwd(y_ref[...], g_ref[...], d_ref[...])
            dy = dy.astype(MXU_DTYPE)
            dys[...] = dy
            dy_ref[...] = dy
            dg_ref[...] += dg

        dx_ref[...] = _dot(dys[...], w_ref[...], "nt").astype(dx_ref.dtype)

    row = pl.BlockSpec((tm, dm), lambda i, j: (i, 0))
    vec = pl.BlockSpec((1, dm), lambda i, j: (0, 0))
    return pl.pallas_call(
        body, name=name, grid=(n // tm, nn // tn), in_specs=[row, vec, row, pl.BlockSpec((tn, dm), lambda i, j: (j, 0))],
        out_specs=[pl.BlockSpec((tm, tn), lambda i, j: (i, j)), row, vec],
        out_shape=[SDS((n, nn), out_dtype), SDS((n, dm), MXU_DTYPE), SDS((1, dm), F32)],
        scratch_shapes=[pltpu.VMEM((tm, dm), MXU_DTYPE)],
        compiler_params=_cp("arbitrary", "arbitrary"))(y, g, d, w)


def _mm_resid_norm(a, b, x, g, tm, name, target=None):
    n, k = a.shape
    d = b.shape[1]
    tm = min(tm, n)
    assert n % tm == 0 and x.shape == (n, d), (name, a.shape, b.shape)
    with_loss = target is not None

    def body(a_ref, b_ref, x_ref, g_ref, *rest):
        y = _dot(a_ref[...], b_ref[...])
        out = x_ref[...] + y * _rstd(y) * g_ref[...]
        if not with_loss:
            y_ref, o_ref = rest
            y_ref[...] = y
            o_ref[...] = out
            return
        t_ref, y_ref, d_ref, l_ref = rest
        y_ref[...] = y
        diff = out - t_ref[...]
        d_ref[...] = diff * (1.0 / d)

        @pl.when(pl.program_id(0) == 0)
        def _():
            l_ref[...] = jnp.zeros_like(l_ref)

        l_ref[...] += 0.5 * jnp.sum(jnp.mean(diff * diff, axis=-1, keepdims=True), axis=0, keepdims=True)

    row = pl.BlockSpec((tm, d), lambda i: (i, 0))
    ins = [pl.BlockSpec((tm, k), lambda i: (i, 0)), pl.BlockSpec((k, d), lambda i: (0, 0)), row, pl.BlockSpec((1, d), lambda i: (0, 0))]
    out = SDS((n, d), F32)
    if with_loss:
        return pl.pallas_call(body, name=name, grid=(n // tm,), in_specs=ins + [row], out_specs=[row, row, pl.BlockSpec((1, 1), lambda i: (0, 0))],
                              out_shape=[out, out, SDS((1, 1), F32)], compiler_params=_cp("arbitrary"))(a, b, x, g, target)
    return pl.pallas_call(body, name=name, grid=(n // tm,), in_specs=ins, out_specs=[row, row], out_shape=[out, out],
                          compiler_params=_cp("parallel"))(a, b, x, g)


def _dx_norm_bwd(a_parts, b, x, g, res, tm, name, after=(), b_turned=False):
    after, after_specs = _unread(after)
    parts, d, p = (b.shape[0], b.shape[2], b.shape[1]) if b_turned else b.shape
    form = "nn" if b_turned else "nt"
    n = x.shape[0]
    tm = min(tm, n)
    assert n % tm == 0 and len(a_parts) == parts and x.shape[1] == d, (name, x.shape, b.shape)

    def body(*refs):
        x_ref, g_ref, res_ref = refs[2 * parts:2 * parts + 3]
        dx_ref, dg_ref = refs[-2:]
        dh = _dot(refs[0][...], refs[parts][0], form)
        for s in range(1, parts):
            dh = dh + _dot(refs[s][...], refs[parts + s][0], form)
        dx, dg = _rms_bwd(x_ref[...], g_ref[...], dh)
        dx_ref[...] = dx + res_ref[...]

        @pl.when(pl.program_id(0) == 0)
        def _():
            dg_ref[...] = jnp.zeros_like(dg_ref)

        dg_ref[...] += dg

    a_specs = [pl.BlockSpec((tm, p), lambda i, cb=cb: (i, cb)) for _, cb in a_parts]
    b_specs = [pl.BlockSpec((1,) + b.shape[1:], lambda i, s=s: (s, 0, 0)) for s in range(parts)]
    row = pl.BlockSpec((tm, d), lambda i: (i, 0))
    vec = pl.BlockSpec((1, d), lambda i: (0, 0))
    return pl.pallas_call(
        body, name=name, grid=(n // tm,), in_specs=a_specs + b_specs + [row, vec, row] + after_specs,
        out_specs=[row, vec], out_shape=[SDS((n, d), F32), SDS((1, d), F32)],
        compiler_params=_cp("arbitrary"))(*[arr for arr, _ in a_parts], *([b] * parts), x, g, res, *after)


def _dw_by_owner(a, b, tn, first, into, tm, name):
    k, m = a.shape
    cnt = b.shape[1] // tn
    tm = min(tm, m)
    assert m % tm == 0 and b.shape[1] == cnt * tn and first + cnt <= 4, (name, a.shape, b.shape)

    def body(a_ref, b_ref, *rest):
        rest[-1][0] = _dot(a_ref[...], b_ref[...], "tn").astype(rest[-1].dtype)

    extra = [] if into is None else [into]
    return pl.pallas_call(
        body, name=name, grid=(m // tm, cnt),
        in_specs=[pl.BlockSpec((k, tm), lambda i, j: (0, i)), pl.BlockSpec((k, tn), lambda i, j: (0, j))] + [pl.BlockSpec(memory_space=pl.ANY)] * len(extra),
        out_specs=pl.BlockSpec((1, tm, tn), lambda i, j: (first + j, i, 0)), out_shape=SDS((4, m, tn), WIRE_DTYPE),
        input_output_aliases={2: 0} if extra else {},
        compiler_params=_cp("parallel", "parallel"))(a, b, *extra)


def _norm_mm(x, g, w, out_dtype, tm, tn, name, after=(), w_turned=False):
    after, after_specs = _unread(after)
    m, d = x.shape
    sharded = w.ndim == 3
    n = w.shape[0] if w_turned else w.shape[-1] * (w.shape[0] if sharded else 1)
    tm, tn = min(tm, m), (w.shape[-1] if sharded else min(tn, n))
    assert m % tm == 0 and n % tn == 0 and not (sharded and w_turned), (name, m, n, tm, tn)

    def body(x_ref, g_ref, w_ref, *rest):
        o_ref, h_ref, hs = rest[-3:]

        @pl.when(pl.program_id(1) == 0)
        def _():
            xv = x_ref[...]
            h = (xv * _rstd(xv) * g_ref[...]).astype(MXU_DTYPE)
            hs[...] = h
            h_ref[...] = h

        o_ref[...] = _dot(hs[...], w_ref[0] if sharded else w_ref[...], "nt" if w_turned else "nn").astype(o_ref.dtype)

    if w_turned:
        w_spec = pl.BlockSpec((tn, d), lambda i, j: (j, 0))
    else:
        w_spec = pl.BlockSpec((1, d, tn), lambda i, j: (j, 0, 0)) if sharded else pl.BlockSpec((d, tn), lambda i, j: (0, j))
    return pl.pallas_call(
        body, name=name, grid=(m // tm, n // tn),
        in_specs=[pl.BlockSpec((tm, d), lambda i, j: (i, 0)), pl.BlockSpec((1, d), lambda i, j: (0, 0)), w_spec] + after_specs,
        out_specs=[pl.BlockSpec((tm, tn), lambda i, j: (i, j)), pl.BlockSpec((tm, d), lambda i, j: (i, 0))],
        out_shape=[SDS((m, n), out_dtype), SDS((m, d), MXU_DTYPE)],
        scratch_shapes=[pltpu.VMEM((tm, d), MXU_DTYPE)],
        compiler_params=_cp("parallel", "arbitrary"))(x, g, w, *after)


ROW_TILE = 512
TOKEN_TILE = 1024


def _norm_bwd(x, g, dy, res, out_dtype, name):
    n, d = x.shape
    tr = min(ROW_TILE, n)
    has_res = res is not None

    def body(*refs):
        x_ref, g_ref, dy_ref = refs[:3]
        dx_ref, dg_ref = refs[-2:]
        dx, dg = _rms_bwd(x_ref[...], g_ref[...], dy_ref[...].astype(F32))
        if has_res:
            dx = dx + refs[3][...]
        dx_ref[...] = dx.astype(dx_ref.dtype)

        @pl.when(pl.program_id(0) == 0)
        def _():
            dg_ref[...] = jnp.zeros_like(dg_ref)

        dg_ref[...] += dg

    row = pl.BlockSpec((tr, d), lambda i: (i, 0))
    vec = pl.BlockSpec((1, d), lambda i: (0, 0))
    ins = [x, g, dy] + ([res] if has_res else [])
    return pl.pallas_call(
        body, name=name, grid=(n // tr,), in_specs=[row, vec, row] + ([row] if has_res else []),
        out_specs=[row, vec], out_shape=[SDS((n, d), out_dtype), SDS((1, d), F32)],
        compiler_params=_cp("arbitrary"))(*ins)


def _rope_tables(n):
    pairs = ATT_HEAD_DIM // 4
    t = np.arange(n)
    inv = np.power(ROPE_THETA, -np.arange(pairs, dtype=np.float32) / pairs).astype(np.float32)
    ang = np.concatenate([(t // GRID_W)[:, None].astype(np.float32) * inv, (t % GRID_W)[:, None].astype(np.float32) * inv], axis=-1)
    cos = np.repeat(np.cos(ang), 2, axis=-1)
    sin = np.repeat(np.sin(ang), 2, axis=-1) * np.tile(np.array([-1.0, 1.0], np.float32), ATT_HEAD_DIM // 2)
    return jnp.asarray(np.tile(cos, 2), F32), jnp.asarray(np.tile(sin, 2), F32)


def _swap_pairs(x):
    lane = lax.broadcasted_iota(jnp.int32, x.shape, 1)
    return jnp.where((lane & 1) == 0, pltpu.roll(x, 127, axis=1), pltpu.roll(x, 1, axis=1))


def _head_mean(v):
    lane = lax.broadcasted_iota(jnp.int32, v.shape, 1)
    lo = jnp.where(lane < ATT_HEAD_DIM, v, 0.0)
    s0 = jnp.sum(lo, axis=-1, keepdims=True)
    s1 = jnp.sum(v - lo, axis=-1, keepdims=True)
    return jnp.where(lane < ATT_HEAD_DIM, s0, s1) * (1.0 / ATT_HEAD_DIM)


def _qk_prep(p, gq, gk, cos, sin, name):
    n = p.shape[0]
    tr = min(ROW_TILE, n)

    def one(xv, g, c, s):
        xn = xv * lax.rsqrt(_head_mean(xv * xv) + EPS) * g
        return xn * c + _swap_pairs(xn) * s

    def body(q_ref, k_ref, gq_ref, gk_ref, c_ref, s_ref, qo_ref, ko_ref):
        c, s = c_ref[...], s_ref[...]
        for j in range(ATT_Q_DIM // 128):
            qo_ref[:, j * 128:(j + 1) * 128] = one(q_ref[:, j * 128:(j + 1) * 128], gq_ref[...], c, s).astype(qo_ref.dtype)
        ko_ref[...] = one(k_ref[...], gk_ref[...], c, s).astype(ko_ref.dtype)

    vec = pl.BlockSpec((1, 128), lambda i: (0, 0))
    tab = pl.BlockSpec((tr, 128), lambda i: (i, 0))
    return pl.pallas_call(
        body, name=name, grid=(n // tr,),
        in_specs=[pl.BlockSpec((tr, ATT_Q_DIM), lambda i: (i, 0)), pl.BlockSpec((tr, 128), lambda i: (i, OFF_AK // 128)), vec, vec, tab, tab],
        out_specs=[pl.BlockSpec((tr, ATT_Q_DIM), lambda i: (i, 0)), tab],
        out_shape=[SDS((n, ATT_Q_DIM), MXU_DTYPE), SDS((n, ATT_KV_DIM), MXU_DTYPE)],
        compiler_params=_cp("parallel"))(p, p, gq, gk, cos, sin)


def _qk_prep_bwd(p, gq, gk, cos, sin, dq, dk, name):
    n = p.shape[0]
    tr = min(ROW_TILE, n)

    def one(xv, g, c, s, dout):
        dxn = dout * c + _swap_pairs(dout * s)
        r = lax.rsqrt(_head_mean(xv * xv) + EPS)
        xh = xv * r
        dn = dxn * g
        dx = r * (dn - xh * _head_mean(dn * xh))
        return dx, jnp.sum(dxn * xh, axis=0, keepdims=True)

    def body(q_ref, k_ref, gq_ref, gk_ref, c_ref, s_ref, dq_ref, dk_ref, dqo_ref, dko_ref, dgq_ref, dgk_ref):
        @pl.when(pl.program_id(0) == 0)
        def _():
            dgq_ref[...] = jnp.zeros_like(dgq_ref)
            dgk_ref[...] = jnp.zeros_like(dgk_ref)

        c, s = c_ref[...], s_ref[...]
        for j in range(ATT_Q_DIM // 128):
            sl = slice(j * 128, (j + 1) * 128)
            dx, dg = one(q_ref[:, sl], gq_ref[...], c, s, dq_ref[:, sl])
            dqo_ref[:, sl] = dx.astype(dqo_ref.dtype)
            dgq_ref[:, sl] += dg
        dx, dg = one(k_ref[...], gk_ref[...], c, s, dk_ref[...])
        dko_ref[...] = dx.astype(dko_ref.dtype)
        dgk_ref[...] += dg

    vec = pl.BlockSpec((1, 128), lambda i: (0, 0))
    tab = pl.BlockSpec((tr, 128), lambda i: (i, 0))
    qrow = pl.BlockSpec((tr, ATT_Q_DIM), lambda i: (i, 0))
    return pl.pallas_call(
        body, name=name, grid=(n // tr,),
        in_specs=[qrow, pl.BlockSpec((tr, 128), lambda i: (i, OFF_AK // 128)), vec, vec, tab, tab, qrow, tab],
        out_specs=[qrow, tab, pl.BlockSpec((1, ATT_Q_DIM), lambda i: (0, 0)), vec],
        out_shape=[SDS((n, ATT_Q_DIM), MXU_DTYPE), SDS((n, ATT_KV_DIM), MXU_DTYPE), SDS((1, ATT_Q_DIM), F32), SDS((1, 128), F32)],
        compiler_params=_cp("arbitrary"))(p, p, gq, gk, cos, sin, dq, dk)


ATT_FWD_STEP = (256, 4)
ATT_BWD_STEP = (512, 2)


def _attn_fwd(q, k, v, name):
    n = q.shape[0]
    tq, step_heads = min(ATT_FWD_STEP[0], n), ATT_FWD_STEP[1]
    scale = ATT_HEAD_DIM ** -0.5
    gw = step_heads * ATT_HEAD_DIM
    parts = ATT_GROUP // step_heads

    def body(q_ref, k_ref, v_ref, o_ref):
        kk, vv = k_ref[0], v_ref[0]
        v_ones = jnp.concatenate([vv, jnp.ones_like(vv)], axis=1)
        outs = []
        for g in range(step_heads):
            s = _dot(q_ref[:, g * ATT_HEAD_DIM:(g + 1) * ATT_HEAD_DIM] * scale, kk, "nt")
            e = jnp.exp(s - jnp.max(s, axis=-1, keepdims=True))
            ov = _dot(e, v_ones)
            outs.append(ov[:, :ATT_HEAD_DIM] / ov[:, ATT_HEAD_DIM:])
        o_ref[...] = jnp.concatenate(outs, axis=-1).astype(o_ref.dtype)

    kv = pl.BlockSpec((1, n, ATT_HEAD_DIM), lambda h, i, pr: (h, 0, 0))
    qb = pl.BlockSpec((tq, gw), lambda h, i, pr: (i, h * parts + pr))
    return pl.pallas_call(
        body, name=name, grid=(ATT_KV_HEADS, n // tq, parts), in_specs=[qb, kv, kv],
        out_specs=qb, out_shape=SDS((n, ATT_Q_DIM), MXU_DTYPE),
        compiler_params=_cp("parallel", "parallel", "parallel"))(q, k, v)


def _attn_bwd(q, k, v, o, do, name):
    n = q.shape[0]
    tq, step_heads = min(ATT_BWD_STEP[0], n), ATT_BWD_STEP[1]
    scale = ATT_HEAD_DIM ** -0.5
    gw = step_heads * ATT_HEAD_DIM
    parts = ATT_GROUP // step_heads

    def body(q_ref, k_ref, v_ref, o_ref, do_ref, dq_ref, dk_ref, dv_ref):
        @pl.when(jnp.logical_and(pl.program_id(1) == 0, pl.program_id(2) == 0))
        def _():
            dk_ref[...] = jnp.zeros_like(dk_ref)
            dv_ref[...] = jnp.zeros_like(dv_ref)

        kk, vv = k_ref[0], v_ref[0]
        dqs = []
        dk_acc = jnp.zeros((ATT_HEAD_DIM, n), F32)
        dv_acc = jnp.zeros((ATT_HEAD_DIM, n), F32)
        for g in range(step_heads):
            sl = slice(g * ATT_HEAD_DIM, (g + 1) * ATT_HEAD_DIM)
            qg, dog = q_ref[:, sl] * scale, do_ref[:, sl].astype(F32)
            s = _dot(qg, kk, "nt")
            e = jnp.exp(s - jnp.max(s, axis=-1, keepdims=True))
            inv = 1.0 / jnp.sum(e, axis=-1, keepdims=True)
            delta = jnp.sum(dog * o_ref[:, sl].astype(F32), axis=-1, keepdims=True)
            dse = e * (_dot(dog, vv, "nt") - delta)
            dqs.append(_dot(dse, kk) * (inv * scale))
            dk_acc += _dot(qg.astype(F32) * inv, dse, "tn")
            dv_acc += _dot(dog * inv, e, "tn")
        dq_ref[...] = jnp.concatenate(dqs, axis=-1)
        dk_ref[0] += dk_acc
        dv_ref[0] += dv_acc

    kv = pl.BlockSpec((1, n, ATT_HEAD_DIM), lambda h, i, pr: (h, 0, 0))
    kvt = pl.BlockSpec((1, ATT_HEAD_DIM, n), lambda h, i, pr: (h, 0, 0))
    qb = pl.BlockSpec((tq, gw), lambda h, i, pr: (i, h * parts + pr))
    return pl.pallas_call(
        body, name=name, grid=(ATT_KV_HEADS, n // tq, parts), in_specs=[qb, kv, kv, qb, qb], out_specs=[qb, kvt, kvt],
        out_shape=[SDS((n, ATT_Q_DIM), F32), SDS((ATT_KV_HEADS, ATT_HEAD_DIM, n), F32), SDS((ATT_KV_HEADS, ATT_HEAD_DIM, n), F32)],
        compiler_params=_cp("parallel", "arbitrary", "arbitrary"))(q, k, v, o, do)


def _both_directions(mats, axis):
    fwd = np.concatenate(mats, axis=axis).astype(np.float32)
    bwd = np.concatenate([m[::-1, ::-1] for m in mats], axis=axis).astype(np.float32)
    return jnp.asarray(np.stack([fwd, bwd]), MXU_DTYPE)


def _hg_segments():
    c = HG_CHUNK
    t = np.arange(c)[:, None]
    r = np.arange(c)[None, :]
    mats = [(r <= t)]
    for lev in range(HG_LEVELS):
        h = c >> (lev + 1)
        mid = (t // (2 * h)) * (2 * h) + h - 1
        hi = (t // h) % 2 == 1
        mats.append(np.where(hi, (r > mid) & (r <= t), (r > t) & (r <= mid)))
    mats.append(r > t)
    return _both_directions(mats, 0)


def _hg_pair_sums():
    c = HG_CHUNK
    r = np.arange(c)[:, None]
    t = np.arange(c)[None, :]
    gp, gn = [t >= r], [t < r]
    for lev in range(HG_LEVELS):
        sh = HG_LEVELS - 1 - lev
        same = (r >> sh) == (t >> sh)
        gp.append(same & (t >= r))
        gn.append(same & (t < r))
    return _both_directions(gp, 1), _both_directions(gn, 1)


def _split_dot(mat, x):
    hi = x.astype(MXU_DTYPE)
    lo = (x - hi.astype(F32)).astype(MXU_DTYPE)
    return _dot(mat, hi) + _dot(mat, lo)


def _hg_gates(hq, z, a0, a1):
    q = hq * _sigmoid(hq)
    sg = _sigmoid(z)
    lb = _sigmoid(a0 - a1)
    f = lb + (1.0 - lb) * sg
    k = (1.0 - lb) * (1.0 - sg)
    return q, f, k, sg, lb


def _hg_level_masks():
    c = HG_CHUNK
    t = np.arange(c)
    later, same = [], []
    for lev in range(HG_LEVELS):
        sh = HG_LEVELS - 1 - lev
        later.append(np.broadcast_to((((t >> sh) & 1) == 1)[:, None], (c, HG_HEAD_DIM)))
        same.append((t[:, None] >> (sh + 1)) == (t[None, :] >> (sh + 1)))
    same.append(t[:, None] == t[None, :])
    later = np.stack(later).astype(np.float32)
    return jnp.asarray(np.stack([later, 1.0 - later]), F32), jnp.asarray(np.stack(same).astype(np.float32), F32)


def _hg_level(q, k, ex, later_ref, lev):
    e = ex[lev + 1]
    e_q = e * later_ref[0, lev]
    e_k = e - e_q
    return q * e_q, k * e_k, e_q, e_k


def _hg_intra(q, k, ex, later_ref, same_ref):
    a = same_ref[HG_LEVELS] * jnp.sum(q * k, axis=-1, keepdims=True)
    for lev in range(HG_LEVELS):
        qs, ks, _, _ = _hg_level(q, k, ex, later_ref, lev)
        a = a + same_ref[lev] * _dot(qs, ks, "nt")
    return a


def _hg_specs(n, with_time):
    c = HG_CHUNK
    nc = n // c

    def chunk(d, i):
        first = d if with_time else 1 - d
        return i + first * (nc - 1 - 2 * i)

    def pcols(off, dir_stride=0):
        return [pl.BlockSpec((c, HG_PAIR), lambda d, i, j=j: (chunk(d, i), off // HG_PAIR + dir_stride // HG_PAIR * d + j)) for j in range(2)]

    specs = dict(
        hq=pcols(OFF_HQ), v=pcols(OFF_HI), z=pcols(OFF_ZF, OFF_ZB - OFF_ZF),
        shared=pl.BlockSpec((c, HG_DIM), lambda d, i: (chunk(d, i), 0)),
        per_dir=pl.BlockSpec((1, c, HG_DIM), lambda d, i: (d, chunk(d, i), 0)),
        vec=pl.BlockSpec((1, 1, HG_DIM), lambda d, i: (d, 0, 0)),
        seg=pl.BlockSpec((1, (HG_LEVELS + 2) * c, c), lambda d, i: (d, 0, 0)),
        sums=pl.BlockSpec((1, c, (HG_LEVELS + 1) * c), lambda d, i: (d, 0, 0)),
        later=pl.BlockSpec((1, HG_LEVELS, c, HG_HEAD_DIM), lambda d, i: (d, 0, 0, 0)),
        same=pl.BlockSpec((HG_LEVELS + 1, c, c), lambda d, i: (0, 0, 0)),
        state=pl.BlockSpec((1, HG_HEADS, 1, HG_HEAD_DIM, HG_HEAD_DIM), lambda d, i: (d, 0, chunk(d, i), 0, 0)),
        weights=pl.BlockSpec((1, HG_HEADS, 1, c, c), lambda d, i: (d, 0, chunk(d, i), 0, 0)),
        levels=pl.BlockSpec((1, HG_HEADS, 1, HG_LEVELS, c, HG_HEAD_DIM), lambda d, i: (d, 0, chunk(d, i), 0, 0, 0)),
        kept=pl.BlockSpec((1, HG_KEPT, c, HG_DIM), lambda d, i: (d, 0, chunk(d, i), 0)))
    return nc, specs


def _hg_head(refs, hh):
    off = (hh % 2) * HG_HEAD_DIM
    return refs[hh // 2][:, off:off + HG_HEAD_DIM]


def _hg_lanes(hh):
    return slice(hh * HG_HEAD_DIM, (hh + 1) * HG_HEAD_DIM)


def _hg_exps(seg_ref, f):
    c = HG_CHUNK
    args = _split_dot(seg_ref[0], jnp.log(f))
    return [jnp.exp(args[j * c:(j + 1) * c]) for j in range(HG_LEVELS + 2)]


def _hg_last_row(a, mirrored):
    return jnp.where(mirrored, a[0:1, :], a[HG_CHUNK - 1:HG_CHUNK, :])


def _hgrn_fwd(p, a0, a1, seg, masks, name):
    n = p.shape[0]
    nc, sp = _hg_specs(n, True)

    def body(hq0, hq1, z0, z1, v0, v1, a0_ref, a1_ref, seg_ref, later_ref, same_ref, o_ref, s0_ref, a_ref, e_ref, g_ref, st):
        @pl.when(pl.program_id(1) == 0)
        def _():
            st[...] = jnp.zeros_like(st)

        mirrored = pl.program_id(0) == 1
        for hh in range(HG_HEADS):
            ln = _hg_lanes(hh)
            hqv = _hg_head((hq0, hq1), hh)
            q, f, k, sg, _ = _hg_gates(hqv, _hg_head((z0, z1), hh), a0_ref[0, :, ln], a1_ref[0, :, ln])
            vv = _hg_head((v0, v1), hh)
            ex = _hg_exps(seg_ref, f)
            for lev in range(HG_LEVELS):
                e_ref[0, hh, 0, lev] = ex[lev + 1].astype(e_ref.dtype)
            sq = _sigmoid(hqv)
            for j, kept in enumerate((q, k, f, sg, sq * (1.0 + hqv * (1.0 - sq)), ex[0], ex[HG_LEVELS + 1])):
                g_ref[0, j, :, ln] = kept
            a = _hg_intra(q, k, ex, later_ref, same_ref).astype(MXU_DTYPE)
            a_ref[0, hh, 0] = a
            s_t = st[hh]
            s0_ref[0, hh, 0] = s_t
            o_ref[0, :, ln] = _dot(a, vv) + _dot(q * ex[0], s_t, "nt")
            st[hh] = s_t * _hg_last_row(ex[0], mirrored) + _dot(vv, k * ex[HG_LEVELS + 1], "tn")

    return pl.pallas_call(
        body, name=name, grid=(2, nc), in_specs=sp["hq"] + sp["z"] + sp["v"] + [sp["vec"], sp["vec"], sp["seg"], sp["later"], sp["same"]],
        out_specs=[sp["per_dir"], sp["state"], sp["weights"], sp["levels"], sp["kept"]],
        out_shape=[SDS((2, n, HG_DIM), F32), SDS((2, HG_HEADS, nc, HG_HEAD_DIM, HG_HEAD_DIM), F32),
                   SDS((2, HG_HEADS, nc, HG_CHUNK, HG_CHUNK), MXU_DTYPE),
                   SDS((2, HG_HEADS, nc, HG_LEVELS, HG_CHUNK, HG_HEAD_DIM), MXU_DTYPE), SDS((2, HG_KEPT, n, HG_DIM), F32)],
        scratch_shapes=[pltpu.VMEM((HG_HEADS, HG_HEAD_DIM, HG_HEAD_DIM), F32)],
        compiler_params=_cp("parallel", "arbitrary"))(p, p, p, p, p, p, a0, a1, seg, *masks)


def _hgrn_bwd(p, a0, a1, masks, gp, gn, do, s0, a, e, kept, name):
    n = p.shape[0]
    nc, sp = _hg_specs(n, False)


    def body(v0, v1, a0_ref, a1_ref, later_ref, same_ref, gp_ref, gn_ref, do_ref, s0_ref, a_ref, e_ref, g_ref,
             dhq_ref, dz_ref, dv_ref, dlb_ref, rt):
        @pl.when(pl.program_id(1) == 0)
        def _():
            rt[...] = jnp.zeros_like(rt)
            dlb_ref[...] = jnp.zeros_like(dlb_ref)

        mirrored = pl.program_id(0) == 1
        for hh in range(HG_HEADS):
            ln = _hg_lanes(hh)
            q, k, f, sg, dsilu, e_first, e_last = (g_ref[0, j, :, ln] for j in range(HG_KEPT))
            lb = _sigmoid(a0_ref[0, :, ln] - a1_ref[0, :, ln])
            vv, dov = _hg_head((v0, v1), hh), do_ref[:, ln]
            ex = [e_first] + [e_ref[0, hh, 0, lev].astype(F32) for lev in range(HG_LEVELS)] + [e_last]
            a = a_ref[0, hh, 0]
            da = _dot(dov, vv, "nt")
            diag = jnp.sum(dov * vv, axis=-1, keepdims=True)
            s_t = s0_ref[0, hh, 0]
            r_t = rt[hh]
            k_end = k * ex[HG_LEVELS + 1]
            dv_ref[0, :, ln] = _dot(a, dov, "tn") + _dot(k_end, r_t, "nt")
            dq_inter = ex[0] * _dot(dov, s_t)
            dk_inter = ex[HG_LEVELS + 1] * _dot(vv, r_t)
            dq = diag * k + dq_inter
            dk = diag * q + dk_inter
            q_terms, k_terms = [q * dq_inter], [k * dk_inter]
            for lev in range(HG_LEVELS):
                qs, ks, e_q, e_k = _hg_level(q, k, ex, later_ref, lev)
                pairs = da * same_ref[lev]
                q_part = e_q * _dot(pairs, ks)
                k_part = e_k * _dot(pairs, qs, "tn")
                dq, dk = dq + q_part, dk + k_part
                q_terms.append(q * q_part)
                k_terms.append(k * k_part)
            decay = _hg_last_row(ex[0], mirrored)
            rt[hh] = r_t * decay + _dot(dov, q * ex[0], "tn")
            later = decay * jnp.sum(s_t * r_t, axis=0, keepdims=True)
            dlf = _dot(gp_ref[0], jnp.concatenate(q_terms, axis=0)) + _dot(gn_ref[0], jnp.concatenate(k_terms, axis=0)) + later
            df = dlf / f - dk
            dz_ref[0, :, ln] = df * (1.0 - lb) * sg * (1.0 - sg)
            dlb_ref[0, :, ln] += jnp.sum(df * (1.0 - sg), axis=0, keepdims=True)
            dhq_ref[0, :, ln] = dq * dsilu

    out = SDS((2, n, HG_DIM), F32)
    return pl.pallas_call(
        body, name=name, grid=(2, nc),
        in_specs=sp["v"] + [sp["vec"], sp["vec"], sp["later"], sp["same"], sp["sums"], sp["sums"],
                            sp["shared"], sp["state"], sp["weights"], sp["levels"], sp["kept"]],
        out_specs=[sp["per_dir"], sp["per_dir"], sp["per_dir"], sp["vec"]], out_shape=[out, out, out, SDS((2, 1, HG_DIM), F32)],
        scratch_shapes=[pltpu.VMEM((HG_HEADS, HG_HEAD_DIM, HG_HEAD_DIM), F32)],
        compiler_params=_cp("parallel", "arbitrary"))(p, p, a0, a1, *masks, gp, gn, do, s0, a, e, kept)


def _hg_post(o2, p, g, name):
    n = p.shape[0]
    tr = min(ROW_TILE, n)
    w = 2 * HG_HEAD_DIM

    def body(of_ref, ob_ref, hg_ref, g_ref, o_ref):
        for j in range(2):
            sl = slice(j * HG_HEAD_DIM, (j + 1) * HG_HEAD_DIM)
            o = of_ref[0, :, sl] + ob_ref[0, :, sl]
            hg = hg_ref[:, sl]
            o_ref[:, sl] = (o * _rstd(o) * g_ref[...] * (hg * _sigmoid(hg))).astype(o_ref.dtype)

    blk = pl.BlockSpec((tr, w), lambda i, j: (i, j))
    dirs = [pl.BlockSpec((1, tr, w), lambda i, j, d=d: (d, i, j)) for d in range(2)]
    return pl.pallas_call(
        body, name=name, grid=(n // tr, HG_DIM // w),
        in_specs=dirs + [pl.BlockSpec((tr, w), lambda i, j: (i, OFF_HG // w + j)), pl.BlockSpec((1, HG_HEAD_DIM), lambda i, j: (0, 0))],
        out_specs=blk, out_shape=SDS((n, HG_DIM), MXU_DTYPE), compiler_params=_cp("parallel", "parallel"))(o2, o2, p, g)


def _hg_post_bwd(o2, p, g, dcat, name, after=()):
    n = p.shape[0]
    tr = min(ROW_TILE, n)
    w = 2 * HG_HEAD_DIM
    after, after_specs = _unread(after)

    def body(of_ref, ob_ref, hg_ref, g_ref, d_ref, *rest):
        do_ref, dhg_ref, dg_ref = rest[len(after):]

        @pl.when(pl.program_id(1) == 0)
        def _():
            dg_ref[...] = jnp.zeros_like(dg_ref)

        for j in range(2):
            sl = slice(j * HG_HEAD_DIM, (j + 1) * HG_HEAD_DIM)
            o = of_ref[0, :, sl] + ob_ref[0, :, sl]
            hg = hg_ref[:, sl]
            d = d_ref[:, sl].astype(F32)
            sg = _sigmoid(hg)
            on = o * _rstd(o) * g_ref[...]
            dhg_ref[:, sl] = (d * on * sg * (1.0 + hg * (1.0 - sg))).astype(dhg_ref.dtype)
            dx, dg = _rms_bwd(o, g_ref[...], d * hg * sg)
            do_ref[:, sl] = dx
            dg_ref[0, :, sl] += dg

    blk = pl.BlockSpec((tr, w), lambda j, i: (i, j))
    dirs = [pl.BlockSpec((1, tr, w), lambda j, i, d=d: (d, i, j)) for d in range(2)]
    return pl.pallas_call(
        body, name=name, grid=(HG_DIM // w, n // tr),
        in_specs=dirs + [pl.BlockSpec((tr, w), lambda j, i: (i, OFF_HG // w + j)), pl.BlockSpec((1, HG_HEAD_DIM), lambda j, i: (0, 0)),
                         pl.BlockSpec((tr, w), lambda j, i: (i, ATT_Q_DIM // w + j))] + after_specs,
        out_specs=[blk, blk, pl.BlockSpec((1, 1, w), lambda j, i: (j, 0, 0))],
        out_shape=[SDS((n, HG_DIM), F32), SDS((n, HG_DIM), MXU_DTYPE), SDS((HG_DIM // w, 1, w), F32)],
        compiler_params=_cp("parallel", "arbitrary"))(o2, o2, p, g, dcat, *after)


XATT_TQ = 512


def _xattn_fwd(q, kv, name):
    n, nm = q.shape[0], kv.shape[0]
    tq = min(XATT_TQ, n)
    scale = X_HEAD_DIM ** -0.5

    def body(q_ref, k_ref, v_ref, o_ref):
        s = _dot(q_ref[...], k_ref[...], "nt") * scale
        e = jnp.exp(s - jnp.max(s, axis=-1, keepdims=True))
        o_ref[...] = _dot(e / jnp.sum(e, axis=-1, keepdims=True), v_ref[...]).astype(o_ref.dtype)

    qb = pl.BlockSpec((tq, X_HEAD_DIM), lambda h, i: (i, h))
    return pl.pallas_call(
        body, name=name, grid=(X_HEADS, n // tq),
        in_specs=[qb, pl.BlockSpec((nm, X_HEAD_DIM), lambda h, i: (0, h)), pl.BlockSpec((nm, X_HEAD_DIM), lambda h, i: (0, X_HEADS + h))],
        out_specs=qb, out_shape=SDS(q.shape, MXU_DTYPE), compiler_params=_cp("parallel", "parallel"))(q, kv, kv)


def _xattn_bwd(q, kv, do, name, after=()):
    n, nm = q.shape[0], kv.shape[0]
    tq = min(XATT_TQ, n)
    scale = X_HEAD_DIM ** -0.5
    after, after_specs = _unread(after)

    def body(q_ref, k_ref, v_ref, do_ref, *rest):
        dq_ref, dk_ref, dv_ref = rest[len(after):]

        @pl.when(pl.program_id(1) == 0)
        def _():
            dk_ref[...] = jnp.zeros_like(dk_ref)
            dv_ref[...] = jnp.zeros_like(dv_ref)

        qv, dov = q_ref[...], do_ref[...]
        s = _dot(qv, k_ref[...], "nt") * scale
        e = jnp.exp(s - jnp.max(s, axis=-1, keepdims=True))
        p = e / jnp.sum(e, axis=-1, keepdims=True)
        dp = _dot(dov, v_ref[...], "nt")
        ds = p * (dp - jnp.sum(p * dp, axis=-1, keepdims=True)) * scale
        dq_ref[...] = _dot(ds, k_ref[...]).astype(dq_ref.dtype)
        dk_ref[...] += _dot(ds, qv, "tn")
        dv_ref[...] += _dot(p, dov, "tn")

    qb = pl.BlockSpec((tq, X_HEAD_DIM), lambda h, i: (i, h))
    kb = pl.BlockSpec((nm, X_HEAD_DIM), lambda h, i: (0, h))
    return pl.pallas_call(
        body, name=name, grid=(X_HEADS, n // tq),
        in_specs=[qb, kb, pl.BlockSpec((nm, X_HEAD_DIM), lambda h, i: (0, X_HEADS + h)), qb] + after_specs, out_specs=[qb, kb, kb],
        out_shape=[SDS(q.shape, MXU_DTYPE), SDS((nm, X_HEADS * X_HEAD_DIM), F32), SDS((nm, X_HEADS * X_HEAD_DIM), F32)],
        compiler_params=_cp("parallel", "arbitrary"))(q, kv, kv, do, *after)


def _edge_rows(shape):
    row = lax.broadcasted_iota(jnp.int32, shape, 0)
    return row == 0, row == shape[0] - 1


def _shift_rows(u, down, edges):
    if down:
        return jnp.where(edges[0], 0.0, pltpu.roll(u, 1, axis=0))
    return jnp.where(edges[1], 0.0, pltpu.roll(u, u.shape[0] - 1, axis=0))


def _conv(u, w, b, edges):
    return b + _shift_rows(u, True, edges) * w[0:1, :] + u * w[1:2, :] + _shift_rows(u, False, edges) * w[2:3, :]


def _ff_specs(n):
    gate = lambda rows: pl.BlockSpec((rows, FF_COLS), lambda j: (0, j))
    val = lambda rows: pl.BlockSpec((rows, FF_COLS), lambda j: (0, FF_BLOCKS + j))
    return [gate(n), val(n), gate(3), val(3), gate(1), val(1)], gate


def _conv_gate(u, cw, cb, name):
    n = u.shape[0]
    ins, gate_blk = _ff_specs(n)

    def body(ug_ref, uv_ref, wg_ref, wv_ref, bg_ref, bv_ref, o_ref):
        edges = _edge_rows(ug_ref.shape)
        gate = _conv(ug_ref[...], wg_ref[...], bg_ref[...], edges)
        val = _conv(uv_ref[...], wv_ref[...], bv_ref[...], edges)
        o_ref[...] = (gate * _sigmoid(gate) * val).astype(o_ref.dtype)

    return pl.pallas_call(
        body, name=name, grid=(FF_BLOCKS,), in_specs=ins, out_specs=gate_blk(n), out_shape=SDS((n, D_FF), MXU_DTYPE),
        compiler_params=_cp("parallel"))(u, u, cw, cw, cb, cb)


def _conv_gate_bwd(u, cw, cb, da, name, after=()):
    n = u.shape[0]
    ins, gate_blk = _ff_specs(n)
    after, after_specs = _unread(after)

    def side(dacc, u, w, edges, du_ref, dw_ref, db_ref):
        nxt, prv = _shift_rows(dacc, False, edges), _shift_rows(dacc, True, edges)
        du_ref[...] = (nxt * w[0:1, :] + dacc * w[1:2, :] + prv * w[2:3, :]).astype(du_ref.dtype)
        db_ref[...] = jnp.sum(dacc, axis=0, keepdims=True)
        dw_ref[0:1, :] = jnp.sum(nxt * u, axis=0, keepdims=True)
        dw_ref[1:2, :] = jnp.sum(dacc * u, axis=0, keepdims=True)
        dw_ref[2:3, :] = jnp.sum(prv * u, axis=0, keepdims=True)

    def body(ug_ref, uv_ref, wg_ref, wv_ref, bg_ref, bv_ref, da_ref, *rest):
        dug_ref, duv_ref, dwg_ref, dwv_ref, dbg_ref, dbv_ref = rest[len(after):]
        ug, uv = ug_ref[...], uv_ref[...]
        edges = _edge_rows(ug.shape)
        gate = _conv(ug, wg_ref[...], bg_ref[...], edges)
        val = _conv(uv, wv_ref[...], bv_ref[...], edges)
        sg = _sigmoid(gate)
        dav = da_ref[...].astype(F32)
        side(dav * val * sg * (1.0 + gate * (1.0 - sg)), ug, wg_ref[...], edges, dug_ref, dwg_ref, dbg_ref)
        side(dav * gate * sg, uv, wv_ref[...], edges, duv_ref, dwv_ref, dbv_ref)

    return pl.pallas_call(
        body, name=name, grid=(FF_BLOCKS,), in_specs=ins + [gate_blk(n)] + after_specs,
        out_specs=[gate_blk(n), gate_blk(n), gate_blk(3), gate_blk(3), gate_blk(1), gate_blk(1)],
        out_shape=[SDS((n, D_FF), MXU_DTYPE)] * 2 + [SDS((3, D_FF), F32)] * 2 + [SDS((1, D_FF), F32)] * 2,
        compiler_params=_cp("parallel"))(u, u, cw, cw, cb, cb, da, *after)


def _adamw_update(w_ref, gv, m_ref, v_ref, d_ref, mo_ref, vo_ref, go_ref):
    go_ref[...] = gv
    mn = ADAM_B1 * m_ref[...] + (1.0 - ADAM_B1) * gv
    vn = ADAM_B2 * v_ref[...] + (1.0 - ADAM_B2) * gv * gv
    m_hat = mn / (1.0 - ADAM_B1 ** ADAM_STEP)
    v_hat = vn / (1.0 - ADAM_B2 ** ADAM_STEP)
    d_ref[...] = -ADAM_LR * (m_hat / (jnp.sqrt(v_hat) + ADAM_EPS) + ADAM_WD * w_ref[...])
    mo_ref[...] = mn
    vo_ref[...] = vn


def _adamw_rows(summed, spans, params, name):
    nt = len(params)

    def body(s_ref, *refs):
        for t, (off, d) in enumerate(spans):
            w_ref, m_ref, v_ref = refs[3 * t:3 * t + 3]
            _adamw_update(w_ref, s_ref[:, off:off + d], m_ref, v_ref, *refs[3 * nt + 4 * t:3 * nt + 4 * t + 4])

    assert all(p[0].shape == (1, d) for p, (_, d) in zip(params, spans)), name
    vm = pl.BlockSpec(memory_space=pltpu.VMEM)
    out = pl.pallas_call(body, name=name, in_specs=[vm] * (1 + 3 * nt), out_specs=[vm] * (4 * nt),
                         out_shape=[SDS(p[0].shape, F32) for p in params for _ in range(4)])(summed, *[a for p in params for a in p])
    return [tuple(out[4 * t:4 * t + 4]) for t in range(nt)]


def _adamw(params, name):
    nt = len(params)
    rows = [p[0].shape[-2] for p in params]
    assert all(p[0].ndim == 2 or p[0].shape[:-2] == (1,) for p in params), name
    fits = lambda s: max(rows) <= s * ELEMENTWISE_ROWS and all(r % s == 0 and (s == 1 or r // s % 8 == 0) for r in rows)
    steps = next(s for s in range(1, max(rows) + 1) if fits(s))

    def body(*refs):
        for t in range(nt):
            w_ref, g_ref, m_ref, v_ref = refs[4 * t:4 * t + 4]
            _adamw_update(w_ref, g_ref[...], m_ref, v_ref, *refs[4 * (nt + t):4 * (nt + t) + 4])

    def blk(w):
        tr, c = w.shape[-2] // steps, w.shape[-1]
        return pl.BlockSpec((tr, c), lambda i: (i, 0)) if w.ndim == 2 else pl.BlockSpec((1, tr, c), lambda i: (0, i, 0))

    specs = [blk(p[0]) for p in params for _ in range(4)]
    out = pl.pallas_call(body, name=name, grid=(steps,), in_specs=specs, out_specs=specs,
                         out_shape=[SDS(p[0].shape, F32) for p in params for _ in range(4)],
                         compiler_params=_cp("parallel"))(*[a for p in params for a in p])
    return [tuple(out[4 * t:4 * t + 4]) for t in range(nt)]


ANY = pl.BlockSpec(memory_space=pl.ANY)


def _place():
    x, y, c = lax.axis_index("x"), lax.axis_index("y"), lax.axis_index("c")
    return x, y, c, [(1 - x, y), (x, 1 - y), (1 - x, 1 - y)]


HBM = pl.BlockSpec(memory_space=pltpu.HBM)
SEM = pl.BlockSpec(memory_space=pltpu.SEMAPHORE)
TOKEN = pl.BlockSpec(memory_space=pltpu.VMEM)
TOKEN_SHAPE = SDS((8, 128), F32)
PEERS = 7


def _in_hbm(a):
    return pltpu.with_memory_space_constraint(a, pltpu.HBM)


def _split_params():
    return pltpu.CompilerParams(has_side_effects=pltpu.SideEffectType.DATAFLOW_SIDE_EFFECTING)


def _gather_start(shards, name, after=()):
    nt = len(shards)
    after, after_specs = _unread(after)

    def body(*refs):
        ins, lands = refs[:nt], refs[nt:2 * nt]
        outs = refs[2 * nt + len(after):]
        sends, recvs = outs[:nt], outs[nt:2 * nt]
        x, y, c, chips = _place()
        me = 2 * x + y
        for t in range(nt):
            h = ins[t].shape[0] // 2
            mine = pl.ds(c * h, h)
            for j, (cx, cy) in enumerate(chips):
                for dc in range(2):
                    pltpu.make_async_remote_copy(src_ref=ins[t].at[mine], dst_ref=lands[t].at[me, mine], send_sem=sends[t].at[2 * j + dc],
                                                 recv_sem=recvs[t].at[2 * j + c], device_id=(cx, cy, dc), device_id_type=MESH).start()
            pltpu.make_async_remote_copy(src_ref=ins[t], dst_ref=lands[t].at[me], send_sem=sends[t].at[PEERS - 1], recv_sem=recvs[t].at[PEERS - 1],
                                         device_id=(x, y, 1 - c), device_id_type=MESH).start()
        outs[-1][...] = jnp.zeros(TOKEN_SHAPE.shape, F32)

    lands = [lax.empty((4,) + s.shape, s.dtype) for s in shards]
    out = pl.pallas_call(
        body, name=name, in_specs=[HBM] * (2 * nt) + after_specs, out_specs=[SEM] * (2 * nt) + [HBM] * (2 * nt) + [TOKEN],
        out_shape=[pltpu.SemaphoreType.DMA((PEERS,))] * (2 * nt)
        + [pltpu.HBM(s.shape, s.dtype) for s in shards] + [pltpu.HBM(l.shape, l.dtype) for l in lands] + [TOKEN_SHAPE],
        input_output_aliases={t: 2 * nt + t for t in range(2 * nt)}, compiler_params=_split_params())(
            *[_in_hbm(s) for s in shards], *[_in_hbm(l) for l in lands], *after)
    return out[:nt], out[nt:2 * nt], out[2 * nt:3 * nt], out[3 * nt:4 * nt], out[-1]


def _gather_wait(sends, recvs, shards, lands, after, name):
    nt = len(shards)

    def body(*refs):
        ins, lands_ref = refs[:nt], refs[nt:2 * nt]
        send_refs, recv_refs = refs[2 * nt:3 * nt], refs[3 * nt:4 * nt]
        x, y, c, chips = _place()
        for t in range(nt):
            h = ins[t].shape[0] // 2
            for j, (cx, cy) in enumerate(chips):
                for cs in range(2):
                    blk = lands_ref[t].at[2 * cx + cy, pl.ds(cs * h, h)]
                    pltpu.make_async_remote_copy(src_ref=blk, dst_ref=blk, send_sem=send_refs[t].at[2 * j + cs], recv_sem=recv_refs[t].at[2 * j + cs],
                                                 device_id=(cx, cy, cs), device_id_type=MESH).wait()
            blk = lands_ref[t].at[2 * x + y]
            pltpu.make_async_remote_copy(src_ref=blk, dst_ref=blk, send_sem=send_refs[t].at[PEERS - 1], recv_sem=recv_refs[t].at[PEERS - 1],
                                         device_id=(x, y, 1 - c), device_id_type=MESH).wait()

    out = pl.pallas_call(
        body, name=name, in_specs=[HBM] * (2 * nt) + [SEM] * (2 * nt) + [ANY], out_specs=[HBM] * (2 * nt),
        out_shape=[pltpu.HBM(s.shape, s.dtype) for s in shards] + [pltpu.HBM(l.shape, l.dtype) for l in lands],
        input_output_aliases={t: t for t in range(2 * nt)}, compiler_params=_split_params())(*shards, *lands, *sends, *recvs, after)
    return out[nt:]


def _gather_pieces_start(v, shard, name):
    h = shard.shape[0] // 2

    def body(v_ref, v_land, src, land, v_send, v_recv, send, recv, *rest):
        x, y, c, chips = _place()
        for j, flips in enumerate(_flips()):
            pltpu.make_async_remote_copy(src_ref=v_ref, dst_ref=v_land.at[4 * x + 2 * y + c], send_sem=v_send.at[j], recv_sem=v_recv.at[j],
                                         device_id=_flipped(x, y, c, flips), device_id_type=MESH).start()
        me = 2 * x + y
        mine = pl.ds(c * h, h)
        for j, (cx, cy) in enumerate(chips):
            pltpu.make_async_remote_copy(src_ref=src.at[mine], dst_ref=land.at[me, mine], send_sem=send.at[j], recv_sem=recv.at[j],
                                         device_id=(cx, cy, c), device_id_type=MESH).start()
        pltpu.make_async_remote_copy(src_ref=src, dst_ref=land.at[me], send_sem=send.at[len(chips)], recv_sem=recv.at[len(chips)],
                                     device_id=(x, y, 1 - c), device_id_type=MESH).start()
        rest[-1][...] = jnp.zeros(TOKEN_SHAPE.shape, F32)

    v_land = lax.empty((8,) + v.shape, v.dtype)
    land = lax.empty((4,) + shard.shape, shard.dtype)
    passed = [v, v_land, shard, land]
    out = pl.pallas_call(
        body, name=name, in_specs=[HBM] * 4, out_specs=[SEM] * 4 + [HBM] * 4 + [TOKEN],
        out_shape=[pltpu.SemaphoreType.DMA((PEERS,))] * 2 + [pltpu.SemaphoreType.DMA((4,))] * 2
        + [pltpu.HBM(a.shape, a.dtype) for a in passed] + [TOKEN_SHAPE],
        input_output_aliases={t: 4 + t for t in range(4)}, compiler_params=_split_params())(*[_in_hbm(a) for a in passed])
    return (out[0], out[1], out[4], out[5]), (out[2], out[3], out[6], out[7]), out[8]


def _gather_pieces_wait(send, recv, shard, land, after, name):
    h = shard.shape[0] // 2
    after, after_specs = _unread(after)

    def body(*refs):
        land_ref, send_ref, recv_ref = refs[1:4]
        x, y, c, chips = _place()
        for j, (cx, cy) in enumerate(chips):
            blk = land_ref.at[2 * cx + cy, pl.ds(c * h, h)]
            pltpu.make_async_remote_copy(src_ref=blk, dst_ref=blk, send_sem=send_ref.at[j], recv_sem=recv_ref.at[j],
                                         device_id=(cx, cy, c), device_id_type=MESH).wait()
        own = land_ref.at[2 * x + y]
        pltpu.make_async_remote_copy(src_ref=own, dst_ref=own, send_sem=send_ref.at[len(chips)], recv_sem=recv_ref.at[len(chips)],
                                     device_id=(x, y, 1 - c), device_id_type=MESH).wait()

    out = pl.pallas_call(
        body, name=name, in_specs=[HBM, HBM, SEM, SEM] + after_specs, out_specs=[HBM, HBM],
        out_shape=[pltpu.HBM(shard.shape, shard.dtype), pltpu.HBM(land.shape, land.dtype)],
        input_output_aliases={0: 0, 1: 1}, compiler_params=_split_params())(shard, land, send, recv, *after)
    return out[1]


def _pass_pieces(land, name):
    h = land.shape[1] // 2

    def body(_, out, send, recv):
        x, y, c, chips = _place()

        def piece(j, cc):
            cx, cy = chips[j]
            blk = out.at[2 * cx + cy, pl.ds(cc * h, h)]
            return pltpu.make_async_remote_copy(src_ref=blk, dst_ref=blk, send_sem=send.at[j], recv_sem=recv.at[j],
                                                device_id=(x, y, 1 - c), device_id_type=MESH)

        for j in range(len(chips)):
            piece(j, c).start()
        for j in range(len(chips)):
            piece(j, 1 - c).wait_recv()
        for j in range(len(chips)):
            piece(j, c).wait_send()

    return pl.pallas_call(
        body, name=name, in_specs=[ANY], out_specs=ANY, out_shape=SDS(land.shape, land.dtype), input_output_aliases={0: 0},
        scratch_shapes=[pltpu.SemaphoreType.DMA((3,))] * 2, compiler_params=pltpu.CompilerParams(has_side_effects=True))(land)


def _scatter_start(gs, name):
    nt = len(gs)

    def body(*refs):
        g_refs, lands = refs[:nt], refs[nt:2 * nt]
        sends, recvs = refs[2 * nt:3 * nt], refs[3 * nt:4 * nt]
        x, y, c, chips = _place()
        for g_ref, land, send, recv in zip(g_refs, lands, sends, recvs):
            h = land.shape[1]
            for j, (cx, cy) in enumerate(chips):
                for dc in range(2):
                    pltpu.make_async_remote_copy(src_ref=g_ref.at[2 * cx + cy, pl.ds(dc * h, h)], dst_ref=land.at[2 * j + c],
                                                 send_sem=send.at[2 * j + dc], recv_sem=recv.at[2 * j + c], device_id=(cx, cy, dc),
                                                 device_id_type=MESH).start()
            pltpu.make_async_remote_copy(src_ref=g_ref.at[2 * x + y, pl.ds((1 - c) * h, h)], dst_ref=land.at[PEERS - 1], send_sem=send.at[PEERS - 1],
                                         recv_sem=recv.at[PEERS - 1], device_id=(x, y, 1 - c), device_id_type=MESH).start()
        refs[-1][...] = jnp.zeros(TOKEN_SHAPE.shape, F32)

    lands = [lax.empty((PEERS, g.shape[1] // 2, g.shape[2]), g.dtype) for g in gs]
    out = pl.pallas_call(
        body, name=name, in_specs=[HBM] * (2 * nt), out_specs=[SEM] * (2 * nt) + [HBM] * (2 * nt) + [TOKEN],
        out_shape=[pltpu.SemaphoreType.DMA((PEERS,))] * (2 * nt) + [pltpu.HBM(a.shape, a.dtype) for a in gs + lands] + [TOKEN_SHAPE],
        input_output_aliases={t: 2 * nt + t for t in range(2 * nt)}, compiler_params=_split_params())(
            *[_in_hbm(a) for a in gs + lands])
    return [tuple(out[q * nt + t] for q in range(4)) for t in range(nt)], out[-1]


def _scatter_wait(started, after, name):
    nt = len(started)

    def body(*refs):
        lands = refs[nt:2 * nt]
        sends, recvs = refs[2 * nt:3 * nt], refs[3 * nt:4 * nt]
        x, y, c, chips = _place()
        peers = [(cx, cy, dc) for cx, cy in chips for dc in range(2)] + [(x, y, 1 - c)]
        for t in range(nt):
            for k, peer in enumerate(peers):
                blk = lands[t].at[k]
                pltpu.make_async_remote_copy(src_ref=blk, dst_ref=blk, send_sem=sends[t].at[k], recv_sem=recvs[t].at[k],
                                             device_id=peer, device_id_type=MESH).wait()

    gs, lands = [s[2] for s in started], [s[3] for s in started]
    after, after_specs = _unread(after)
    out = pl.pallas_call(
        body, name=name, in_specs=[HBM] * (2 * nt) + [SEM] * (2 * nt) + after_specs, out_specs=[HBM] * (2 * nt),
        out_shape=[pltpu.HBM(a.shape, a.dtype) for a in gs + lands],
        input_output_aliases={t: t for t in range(2 * nt)}, compiler_params=_split_params())(
            *gs, *lands, *[s[0] for s in started], *[s[1] for s in started], *after)
    return out[:nt], out[nt:]


def _join_start(bufs, name, after=()):
    nt = len(bufs)
    after, after_specs = _unread(after)

    def body(*refs):
        send, recv = refs[nt + len(after):nt + len(after) + 2]
        x, y, c, _ = _place()
        for t in range(nt):
            pltpu.make_async_remote_copy(src_ref=refs[t].at[c], dst_ref=refs[t].at[c], send_sem=send.at[t], recv_sem=recv.at[t],
                                         device_id=(x, y, 1 - c), device_id_type=MESH).start()
        refs[-1][...] = jnp.zeros(TOKEN_SHAPE.shape, F32)

    out = pl.pallas_call(
        body, name=name, in_specs=[HBM] * nt + after_specs, out_specs=[SEM, SEM] + [HBM] * nt + [TOKEN],
        out_shape=[pltpu.SemaphoreType.DMA((nt,))] * 2 + [pltpu.HBM(b.shape, b.dtype) for b in bufs] + [TOKEN_SHAPE],
        input_output_aliases={t: 2 + t for t in range(nt)}, compiler_params=_split_params())(*[_in_hbm(b) for b in bufs], *after)
    return out[0], out[1], out[2:2 + nt], out[-1]


def _join_wait(send, recv, bufs, after, name):
    nt = len(bufs)
    after, after_specs = _unread(after)

    def body(*refs):
        send_ref, recv_ref = refs[nt:nt + 2]
        x, y, c, _ = _place()
        for t in range(nt):
            theirs = refs[t].at[1 - c]
            pltpu.make_async_remote_copy(src_ref=theirs, dst_ref=theirs, send_sem=send_ref.at[t], recv_sem=recv_ref.at[t],
                                         device_id=(x, y, 1 - c), device_id_type=MESH).wait()

    return pl.pallas_call(
        body, name=name, in_specs=[HBM] * nt + [SEM, SEM] + after_specs, out_specs=[HBM] * nt,
        out_shape=[pltpu.HBM(b.shape, b.dtype) for b in bufs],
        input_output_aliases={t: t for t in range(nt)}, compiler_params=_split_params())(*bufs, send, recv, *after)


def _flips():
    return [(dx, dy, dc) for dx in range(2) for dy in range(2) for dc in range(2) if (dx, dy, dc) != (0, 0, 0)]


def _flipped(x, y, c, flips):
    dx, dy, dc = flips
    return (1 - x if dx else x, 1 - y if dy else y, 1 - c if dc else c)


def _small_start(v, name, after=()):
    after, after_specs = _unread(after)

    def body(v_ref, land, *rest):
        send, recv = rest[len(after):len(after) + 2]
        x, y, c, _ = _place()
        for j, flips in enumerate(_flips()):
            pltpu.make_async_remote_copy(src_ref=v_ref, dst_ref=land.at[4 * x + 2 * y + c], send_sem=send.at[j], recv_sem=recv.at[j],
                                         device_id=_flipped(x, y, c, flips), device_id_type=MESH).start()
        rest[-1][...] = jnp.zeros(TOKEN_SHAPE.shape, F32)

    land = lax.empty((8,) + v.shape, v.dtype)
    return pl.pallas_call(
        body, name=name, in_specs=[HBM, HBM] + after_specs, out_specs=[SEM, SEM, HBM, HBM, TOKEN],
        out_shape=[pltpu.SemaphoreType.DMA((PEERS,)), pltpu.SemaphoreType.DMA((PEERS,)), pltpu.HBM(v.shape, v.dtype),
                   pltpu.HBM(land.shape, land.dtype), TOKEN_SHAPE],
        input_output_aliases={0: 2, 1: 3}, compiler_params=_split_params())(_in_hbm(v), _in_hbm(land), *after)


def _small_wait(send, recv, v, land, after, name):
    after, after_specs = _unread(after)

    def body(v_ref, land_ref, send_ref, recv_ref, *rest):
        x, y, c, _ = _place()
        for j, flips in enumerate(_flips()):
            px, py, pc = _flipped(x, y, c, flips)
            blk = land_ref.at[4 * px + 2 * py + pc]
            pltpu.make_async_remote_copy(src_ref=blk, dst_ref=blk, send_sem=send_ref.at[j], recv_sem=recv_ref.at[j],
                                         device_id=(px, py, pc), device_id_type=MESH).wait()

    return pl.pallas_call(
        body, name=name, in_specs=[HBM, HBM, SEM, SEM] + after_specs, out_specs=[HBM, HBM],
        out_shape=[pltpu.HBM(v.shape, v.dtype), pltpu.HBM(land.shape, land.dtype)],
        input_output_aliases={0: 0, 1: 1}, compiler_params=_split_params())(v, land, send, recv, *after)


def _sum_small(v, land, name):
    def body(v_ref, land_ref, o_ref):
        x, y, c, _ = _place()
        me = 4 * x + 2 * y + c
        acc = jnp.where(me == 0, v_ref[...], land_ref[0])
        for d in range(1, 8):
            acc = acc + jnp.where(me == d, v_ref[...], land_ref[d])
        o_ref[...] = acc

    vm = pl.BlockSpec(memory_space=pltpu.VMEM)
    return pl.pallas_call(body, name=name, in_specs=[vm, vm], out_specs=vm, out_shape=SDS(v.shape, F32))(v, land)


SUM_STEPS = 2


def _sum_devices(gs, lands, me, core, name):
    nt = len(gs)

    def body(ix_ref, *refs):
        for own_ref, land_ref, o_ref in zip(refs[:nt], refs[nt:2 * nt], refs[2 * nt:]):
            acc = own_ref[0].astype(F32)
            for j in range(land_ref.shape[0]):
                acc = acc + land_ref[j].astype(F32)
            o_ref[0] = acc

    tiles = [(land.shape[1] // SUM_STEPS, land.shape[2]) for land in lands]
    assert all(land.shape[1] == tr * SUM_STEPS and tr % ROWS_PER_16BIT_TILE == 0 for land, (tr, _) in zip(lands, tiles)), name
    grid_spec = pltpu.PrefetchScalarGridSpec(
        num_scalar_prefetch=1, grid=(SUM_STEPS,),
        in_specs=[pl.BlockSpec((1, tr, c), lambda i, ix: (ix[0], ix[1] * SUM_STEPS + i, 0)) for tr, c in tiles]
        + [pl.BlockSpec((land.shape[0], tr, c), lambda i, ix: (0, i, 0)) for land, (tr, c) in zip(lands, tiles)],
        out_specs=[pl.BlockSpec((1, tr, c), lambda i, ix: (ix[1], i, 0)) for tr, c in tiles])
    return pl.pallas_call(body, name=name, grid_spec=grid_spec, out_shape=[SDS((2,) + land.shape[1:], F32) for land in lands],
                          compiler_params=_cp("parallel"))(jnp.stack([me, core]), *gs, *lands)


def _pack_small(parts):
    flat = jnp.concatenate([p.reshape(-1) for p in parts])
    total = flat.shape[0]
    rows = -(-total // 1024) * 8
    return jnp.pad(flat, (0, rows * 128 - total)).reshape(rows, 128)


def _pack_lanes(parts):
    cols, spans, off = [], [], 0
    for p in parts:
        size = int(np.prod(p.shape))
        width = -(-size // 128) * 128
        cols.append(jnp.pad(p.reshape(1, size), ((0, 0), (0, width - size))))
        spans.append((off, size))
        off += width
    return jnp.concatenate(cols, axis=1), spans


def _unpack_small(packed, shapes):
    flat = packed.reshape(-1)
    out, off = [], 0
    for s in shapes:
        size = int(np.prod(s))
        out.append(flat[off:off + size].reshape(s))
        off += size
    return out


def _local_step(x, mem, target, w_in_t, first_after, mid_weights, ffn_weights, on_grad, gains, conv_w, conv_b, hg_lb):
    n = x.shape[0]
    cos, sin = _rope_tables(n)
    seg = _hg_segments()
    gp, gn = _hg_pair_sums()
    masks = _hg_level_masks()
    gq2 = jnp.tile(gains["q_norm_g"], (1, 2))
    gk2 = jnp.tile(gains["k_norm_g"], (1, 2))
    a0 = hg_lb[:, 0:1, :]
    a1 = hg_lb[:, 1:2, :]

    p, h1 = _norm_mm(x, gains["pre_mix_g"], w_in_t, F32, TOKEN_TILE, 1664, "in_proj", after=(first_after,), w_turned=True)
    qr, kr = _qk_prep(p, gq2, gk2, cos, sin, "qk_prep")
    heads = lambda a: a.reshape(n, ATT_KV_HEADS, ATT_HEAD_DIM).transpose(1, 0, 2)
    kh = heads(kr)
    vh = heads(p[:, OFF_AV:OFF_AV + ATT_KV_DIM].astype(MXU_DTYPE))
    att = _attn_fwd(qr, kh, vh, "attn_fwd")
    o2, s0, hg_a, hg_e, hg_kept = _hgrn_fwd(p, a0, a1, seg, masks, "hgrn_fwd")
    rec = _hg_post(o2, p, gains["hg_out_norm_g"], "hg_post")
    cat = jnp.concatenate([att, rec], axis=1)
    w_out, w_xq, w_xkv, w_xo = mid_weights(cat)
    mixed, x1 = _mm_resid_norm(cat, w_out, x, gains["post_mix_g"], 512, "out_proj_resid")
    xq, h2 = _norm_mm(x1, gains["pre_x_g"], w_xq, MXU_DTYPE, TOKEN_TILE, 1024, "xq_proj")
    kv, mn = _norm_mm(mem, gains["mem_norm_g"], w_xkv, MXU_DTYPE, 256, 2048, "xkv_proj")
    ox = _xattn_fwd(xq, kv, "xattn_fwd")
    xo, x2 = _mm_resid_norm(ox, w_xo, x1, gains["post_x_g"], 512, "xo_proj_resid")
    w_up = ffn_weights("w_up", x2)
    u, h3 = _norm_mm(x2, gains["pre_ffn_g"], w_up, F32, TOKEN_TILE, 1408, "up_proj")
    act = _conv_gate(u, conv_w, conv_b, "conv_gate")
    w_down = ffn_weights("w_down", act)
    dn, d3, loss = _mm_resid_norm(act, w_down, x2, gains["post_ffn_g"], 512, "down_proj_resid_loss", target=target)

    gs = {}
    d_act, d_dn, gs["post_ffn_g"] = _norm_bwd_mm(dn, gains["post_ffn_g"], d3, w_down, F32, 512, 1408, "ffn_post_bwd_down_dx")
    tok = on_grad("w_down", _mm(act, d_dn, "tn", WIRE_DTYPE, 1408, 1024, "down_dw"))
    du_g, du_v, dcw_g, dcw_v, dcb_g, dcb_v = _conv_gate_bwd(u, conv_w, conv_b, d_act, "conv_gate_bwd", after=(tok,))
    gs["conv_w"] = jnp.concatenate([dcw_g, dcw_v], axis=1)
    gs["conv_b"] = jnp.concatenate([dcb_g, dcb_v], axis=1)
    ff_shard = w_up.shape[2]
    g_up = _dw_by_owner(h3, du_g, ff_shard, 0, None, 512, "up_dw_gate")
    tok = on_grad("w_up", _dw_by_owner(h3, du_v, ff_shard, 2, g_up, 512, "up_dw_value"))
    d2, gs["pre_ffn_g"] = _dx_norm_bwd([(du_g, 0), (du_g, 1), (du_v, 0), (du_v, 1)], w_up, x2, gains["pre_ffn_g"], d3, 512,
                                       "up_dx_pre_bwd", after=(tok,))
    d_ox, d_xo, gs["post_x_g"] = _norm_bwd_mm(xo, gains["post_x_g"], d2, w_xo, MXU_DTYPE, 512, 1024, "x_post_bwd_xo_dx")
    tok = on_grad("w_xo", _mm(ox, d_xo, "tn", WIRE_DTYPE, 512, 1024, "xo_dw"))
    d_xq, d_k, d_v = _xattn_bwd(xq, kv, d_ox, "xattn_bwd", after=(tok,))
    d_kv = jnp.concatenate([d_k, d_v], axis=1).astype(MXU_DTYPE)
    tok = on_grad("w_xq", _mm(h2, d_xq, "tn", WIRE_DTYPE, 512, 1024, "xq_dw"))
    tok_kv = on_grad("w_xkv", _dw_by_owner(mn, d_kv, w_xkv.shape[2], 0, None, 512, "xkv_dw"))
    d1, gs["pre_x_g"] = _dx_norm_bwd([(d_xq, 0)], w_xq[None], x1, gains["pre_x_g"], d2, 512, "xq_dx_pre_bwd", after=(tok, tok_kv))
    d_mn = _mm_nt_parts([(d_kv, s) for s in range(4)], w_xkv, F32, 256, 1024, "xkv_dx")
    _, gs["mem_norm_g"] = _norm_bwd(mem, gains["mem_norm_g"], d_mn, None, MXU_DTYPE, "mem_norm_bwd")
    d_cat, d_mixed, gs["post_mix_g"] = _norm_bwd_mm(mixed, gains["post_mix_g"], d1, w_out, MXU_DTYPE, 512, 1024, "mix_post_bwd_out_dx")
    tok = on_grad("w_out", _mm(cat, d_mixed, "tn", WIRE_DTYPE, 512, 1024, "out_dw"))
    d_o, d_hg, dg_hg = _hg_post_bwd(o2, p, gains["hg_out_norm_g"], d_cat, "hg_post_bwd", after=(tok,))
    gs["hg_out_norm_g"] = dg_hg.reshape(HG_HEADS, HG_HEAD_DIM).sum(axis=0, keepdims=True)
    dhq2, dz2, dhv2, dlb = _hgrn_bwd(p, a0, a1, masks, gp, gn, d_o, s0, hg_a, hg_e, hg_kept, "hgrn_bwd")
    lb = jax.nn.sigmoid(a0 - a1)
    da0 = dlb * lb * (1.0 - lb)
    gs["hg_lb"] = jnp.concatenate([da0, -da0], axis=1)
    d_qr, d_kh, d_vh = _attn_bwd(qr, kh, vh, cat, d_cat, "attn_bwd")
    unheads = lambda a: a.transpose(2, 0, 1).reshape(n, ATT_KV_DIM)
    d_aq, d_ak, dgq, dgk = _qk_prep_bwd(p, gq2, gk2, cos, sin, d_qr, unheads(d_kh), "qk_prep_bwd")
    gs["q_norm_g"] = dgq.reshape(ATT_HEADS, ATT_HEAD_DIM).sum(axis=0, keepdims=True)
    gs["k_norm_g"] = dgk.reshape(ATT_KV_HEADS, ATT_HEAD_DIM).sum(axis=0, keepdims=True)
    d_p = jnp.concatenate([d_aq, d_ak, unheads(d_vh).astype(MXU_DTYPE), (dhq2[0] + dhq2[1]).astype(MXU_DTYPE),
                           dz2[0].astype(MXU_DTYPE), dz2[1].astype(MXU_DTYPE), (dhv2[0] + dhv2[1]).astype(MXU_DTYPE), d_hg], axis=1)
    tok = on_grad("w_in", _mm(d_p, h1, "tn", WIRE_DTYPE, 1664, 1024, "in_dw"))
    grad_x, gs["pre_mix_g"] = _dx_norm_bwd([(d_p, 0)], w_in_t[None], x, gains["pre_mix_g"], d1, 512, "in_dx_pre_bwd", after=(tok,),
                                           b_turned=True)
    return loss, grad_x, gs


MATS = ("w_in", "w_out", "w_xq", "w_xkv", "w_xo", "w_up", "w_down")
GAINS = ("pre_mix_g", "q_norm_g", "k_norm_g", "hg_out_norm_g", "post_mix_g", "pre_x_g", "mem_norm_g", "post_x_g", "pre_ffn_g", "post_ffn_g")
WEIGHTS = ('pre_mix_g', 'w_in', 'q_norm_g', 'k_norm_g', 'hg_lb', 'hg_out_norm_g', 'w_out', 'post_mix_g', 'pre_x_g', 'mem_norm_g', 'w_xq',
           'w_xkv', 'w_xo', 'post_x_g', 'pre_ffn_g', 'w_up', 'conv_w', 'conv_b', 'w_down', 'post_ffn_g')


def kernel(x, mem, pre_mix_g, w_in, q_norm_g, k_norm_g, hg_lb, hg_out_norm_g, w_out, post_mix_g, pre_x_g, mem_norm_g, w_xq, w_xkv, w_xo, post_x_g, pre_ffn_g, w_up, conv_w, conv_b, w_down, post_ffn_g, loss_target, m_pre_mix_g, m_w_in, m_q_norm_g, m_k_norm_g, m_hg_lb, m_hg_out_norm_g, m_w_out, m_post_mix_g, m_pre_x_g, m_mem_norm_g, m_w_xq, m_w_xkv, m_w_xo, m_post_x_g, m_pre_ffn_g, m_w_up, m_conv_w, m_conv_b, m_w_down, m_post_ffn_g, v_pre_mix_g, v_w_in, v_q_norm_g, v_k_norm_g, v_hg_lb, v_hg_out_norm_g, v_w_out, v_post_mix_g, v_pre_x_g, v_mem_norm_g, v_w_xq, v_w_xkv, v_w_xo, v_post_x_g, v_pre_ffn_g, v_w_up, v_conv_w, v_conv_b, v_w_down, v_post_ffn_g):
    args = dict(locals())
    w = {k: args[k] for k in WEIGHTS}
    m = {k: args["m_" + k] for k in WEIGHTS}
    v = {k: args["v_" + k] for k in WEIGHTS}
    chip = 2 * lax.axis_index("x") + lax.axis_index("y")
    core = lax.axis_index("c")

    turned = ("w_in",)
    shards = {k: (jnp.swapaxes(w[k], 1, 2) if k in turned else w[k])[0].astype(WIRE_DTYPE) for k in MATS}

    def whole(k, g):
        return g if k in ("w_xkv", "w_up") else g.reshape(-1, g.shape[-1])

    mid_names, ffn_names = ("w_out", "w_xq", "w_xkv", "w_xo"), ("w_up", "w_down")
    small, w_in_pieces, token = _gather_pieces_start(_pack_small([w["conv_w"][0], w["hg_lb"]]), shards["w_in"], "gather_first_start")
    mid = _gather_start([shards[k] for k in mid_names], "gather_mid_start", after=(token,))
    ffn = _gather_start([shards[k] for k in ffn_names], "gather_ffn_start", after=(mid[4],))
    w_in_t = whole("w_in", _pass_pieces(_gather_pieces_wait(*w_in_pieces, (ffn[4],), "gather_w_in_wait"), "gather_w_in_pass"))
    mine, others = _small_wait(*small, (w_in_t,), "gather_small_wait")
    small_in = lax.dynamic_update_slice_in_dim(others, mine[None], 2 * chip + core, axis=0)

    def mid_weights(after):
        return [whole(k, g) for k, g in zip(mid_names, _gather_wait(*mid[:4], after, "gather_mid_wait"))]

    def ffn_weights(k, after):
        t = ffn_names.index(k)
        return whole(k, _gather_wait(*[part[t:t + 1] for part in ffn[:4]], after, "gather_wait_" + k)[0])

    cw_parts, lb_parts = [], []
    for s in range(4):
        cw_s, lb_s = _unpack_small(small_in[2 * s], [w["conv_w"][0].shape, w["hg_lb"].shape])
        cw_parts.append(cw_s)
        lb_parts.append(lb_s)
    conv_w_full = jnp.concatenate(cw_parts, axis=1)
    hg_lb_full = jnp.concatenate(lb_parts, axis=2)

    started, held = {}, {}
    leaves_with_next = ("w_down", "w_xo", "w_xq")

    def on_grad(k, g):
        if g.ndim == 2:
            g = g.reshape(4, g.shape[0] // 4, g.shape[1])
        held[k] = g
        if k in leaves_with_next:
            return None
        names = tuple(held)
        per_tensor, token = _scatter_start([held.pop(n) for n in names], "grad_start_" + k)
        started.update(zip(names, per_tensor))
        return token

    gains = {k: w[k] for k in GAINS}
    loss_part, grad_x, gs = _local_step(x[0], mem[0], loss_target[0], w_in_t, ffn[4], mid_weights, ffn_weights, on_grad, gains,
                                        conv_w_full, w["conv_b"], hg_lb_full)
    gs["loss"] = loss_part

    grads, delta, new_m, new_v = {}, {}, {}, {}

    def sum_and_send(names, after, tag):
        sent, landed = _scatter_wait([started[k] for k in names], after, "grad_wait_" + tag)
        return _join_start(_sum_devices(sent, landed, chip, core, "grad_sum_" + tag), "grad_join_start_" + tag)

    def joined(names, join, after, tag):
        for k, r in zip(names, _join_wait(*join[:3], after, "grad_join_wait_" + tag)):
            grads[k] = r.reshape(1, -1, r.shape[-1])

    def adamw(names, tag):
        params, backs = [], []
        for k in names:
            shape = w[k].shape
            keep = len(shape) == 3 and shape[0] == 1
            if k in turned:
                view, back = (lambda a: jnp.swapaxes(a, 1, 2)), (lambda a: jnp.swapaxes(a, 1, 2))
                g = grads[k]
            else:
                view = (lambda a, shape=shape: a.reshape(shape)) if keep else (lambda a, shape=shape: a.reshape(-1, shape[-1]))
                back = lambda a, shape=shape: a.reshape(shape)
                g = view(grads[k])
            params.append((view(w[k]), g, view(m[k]), view(v[k])))
            backs.append(back)
        for k, back, (d, mo, vo, go) in zip(names, backs, _adamw(params, "adamw_" + tag)):
            delta[k], new_m[k], new_v[k], grads[k] = back(d), back(mo), back(vo), back(go)

    small_names = GAINS + ("conv_b", "conv_w", "hg_lb")
    packed, spans = _pack_lanes([gs[k] for k in small_names + ("loss",)])
    spans = dict(zip(small_names + ("loss",), spans))
    small = _small_start(packed, "reduce_small_start", after=(grad_x,))

    ffn_join = sum_and_send(ffn_names, (grad_x, small[4]), "ffn")
    mid_join = sum_and_send(mid_names, (ffn_join[3],), "mid")
    joined(ffn_names, ffn_join, (mid_join[3],), "ffn")
    adamw(ffn_names, "ffn")
    joined(mid_names, mid_join, tuple(new_v[k] for k in ffn_names), "mid")
    adamw(mid_names, "mid")
    early = mid_names + ffn_names
    w_in_join = sum_and_send(("w_in",), tuple(new_v[k] for k in early), "w_in")

    mine, others = _small_wait(*small[:4], (w_in_join[3],), "reduce_small_wait")
    reduced_small = _sum_small(mine, others, "reduce_small_sum")
    summed = lambda k: reduced_small[0, spans[k][0]:spans[k][0] + spans[k][1]].reshape(gs[k].shape)
    loss = summed("loss")[0, 0]
    ncw = w["conv_w"].shape[2]
    grads["conv_w"] = lax.dynamic_slice_in_dim(summed("conv_w"), chip * ncw, ncw, axis=1)[None]
    nlb = w["hg_lb"].shape[2]
    grads["hg_lb"] = lax.dynamic_slice_in_dim(summed("hg_lb"), chip * nlb, nlb, axis=2)
    adamw(("conv_w", "hg_lb"), "sharded_small")
    replicated = GAINS + ("conv_b",)
    outs = _adamw_rows(reduced_small, [spans[k] for k in replicated], [(w[k], m[k], v[k]) for k in replicated], "adamw_replicated")
    for k, (d, mo, vo, go) in zip(replicated, outs):
        delta[k], new_m[k], new_v[k], grads[k] = d, mo, vo, go

    joined(("w_in",), w_in_join, tuple(new_v[k] for k in small_names), "w_in")
    adamw(("w_in",), "w_in")
    return (loss, grad_x[None], *[grads[k] for k in WEIGHTS], *[delta[k] for k in WEIGHTS],
            *[new_m[k] for k in WEIGHTS], *[new_v[k] for k in WEIGHTS])
```

```python
import numpy as np
import jax
import jax.numpy as jnp
from jax import lax
from jax.experimental import pallas as pl
from jax.experimental.pallas import tpu as pltpu

F32 = jnp.float32
MXU_DTYPE = jnp.bfloat16
WIRE_DTYPE = jnp.bfloat16
VMEM_LIMIT_BYTES = 56 * 1024 * 1024
ROWS_PER_16BIT_TILE = 16
ELEMENTWISE_ROWS = 256
EPS = 1e-6
MESH = pl.DeviceIdType.MESH

GRID_W = 64
ATT_HEADS, ATT_KV_HEADS, ATT_HEAD_DIM = 8, 2, 64
ATT_GROUP = ATT_HEADS // ATT_KV_HEADS
ATT_Q_DIM, ATT_KV_DIM = 512, 128
ROPE_THETA = 10000.0
HG_HEADS, HG_HEAD_DIM, HG_DIM = 4, 128, 512
HG_CHUNK = 128
HG_LEVELS = 7
HG_PAIR = 2 * HG_HEAD_DIM
HG_KEPT = 7
X_HEADS, X_HEAD_DIM = 4, 256
D_FF = 2816
FF_COLS = 256
FF_BLOCKS = D_FF // FF_COLS
OFF_AK, OFF_AV, OFF_HQ, OFF_ZF, OFF_ZB, OFF_HI, OFF_HG = 512, 640, 768, 1280, 1792, 2304, 2816

ADAM_LR, ADAM_B1, ADAM_B2, ADAM_EPS, ADAM_WD, ADAM_STEP = 0.001, 0.9, 0.999, 1e-08, 0.01, 10

SDS = jax.ShapeDtypeStruct


def _cp(*sem):
    return pltpu.CompilerParams(dimension_semantics=sem, vmem_limit_bytes=VMEM_LIMIT_BYTES)


def _dot(a, b, form="nn"):
    dims = {"nn": (((1,), (0,)), ((), ())), "nt": (((1,), (1,)), ((), ())), "tn": (((0,), (0,)), ((), ()))}[form]
    return lax.dot_general(a.astype(MXU_DTYPE), b.astype(MXU_DTYPE), dims, preferred_element_type=F32)


def _sigmoid(x):
    return 1.0 / (1.0 + jnp.exp(-x))


def _rstd(x):
    return lax.rsqrt(jnp.mean(x * x, axis=-1, keepdims=True) + EPS)


def _rms_bwd(x, g, dy):
    r = _rstd(x)
    xh = x * r
    dn = dy * g
    dx = r * (dn - xh * jnp.mean(dn * xh, axis=-1, keepdims=True))
    return dx, jnp.sum(dy * xh, axis=0, keepdims=True)


def _unread(after):
    after = tuple(a for a in after if a is not None)
    return after, [pl.BlockSpec(memory_space=pl.ANY)] * len(after)


def _mm(a, b, form, out_dtype, tm, tn, name, after=()):
    after, after_specs = _unread(after)
    if form == "nn":
        (m, k), n = a.shape, b.shape[1]
    elif form == "nt":
        (m, k), n = a.shape, b.shape[0]
    else:
        (k, m), n = a.shape, b.shape[1]
    tm, tn = min(tm, m), min(tn, n)
    assert m % tm == 0 and n % tn == 0, (name, m, n, tm, tn)

    def body(a_ref, b_ref, *rest):
        o_ref = rest[-1]
        o_ref[...] = _dot(a_ref[...], b_ref[...], form).astype(o_ref.dtype)

    a_spec = pl.BlockSpec((k, tm), lambda i, j: (0, i)) if form == "tn" else pl.BlockSpec((tm, k), lambda i, j: (i, 0))
    b_spec = pl.BlockSpec((tn, k), lambda i, j: (j, 0)) if form == "nt" else pl.BlockSpec((k, tn), lambda i, j: (0, j))
    return pl.pallas_call(
        body, name=name, grid=(m // tm, n // tn), in_specs=[a_spec, b_spec] + after_specs,
        out_specs=pl.BlockSpec((tm, tn), lambda i, j: (i, j)), out_shape=SDS((m, n), out_dtype),
        compiler_params=_cp("parallel", "parallel"))(a, b, *after)


def _mm_nt_parts(a_parts, b, out_dtype, tm, tn, name, after=()):
    after, after_specs = _unread(after)
    parts, n, p = b.shape
    m = a_parts[0][0].shape[0]
    tm, tn = min(tm, m), min(tn, n)
    assert m % tm == 0 and n % tn == 0 and len(a_parts) == parts, (name, m, b.shape)

    def body(*refs):
        o_ref = refs[-1]
        acc = _dot(refs[0][...], refs[parts][0], "nt")
        for s in range(1, parts):
            acc = acc + _dot(refs[s][...], refs[parts + s][0], "nt")
        o_ref[...] = acc.astype(o_ref.dtype)

    a_specs = [pl.BlockSpec((tm, p), lambda i, j, cb=cb: (i, cb)) for _, cb in a_parts]
    b_specs = [pl.BlockSpec((1, tn, p), lambda i, j, s=s: (s, j, 0)) for s in range(parts)]
    return pl.pallas_call(
        body, name=name, grid=(m // tm, n // tn), in_specs=a_specs + b_specs + after_specs,
        out_specs=pl.BlockSpec((tm, tn), lambda i, j: (i, j)), out_shape=SDS((m, n), out_dtype),
        compiler_params=_cp("parallel", "parallel"))(*[arr for arr, _ in a_parts], *([b] * parts), *after)


def _norm_bwd_mm(y, g, d, w, out_dtype, tm, tn, name):
    n, dm = y.shape
    nn = w.shape[0]
    tm, tn = min(tm, n), min(tn, nn)
    assert n % tm == 0 and nn % tn == 0 and w.shape[1] == dm, (name, y.shape, w.shape)

    def body(y_ref, g_ref, d_ref, w_ref, dx_ref, dy_ref, dg_ref, dys):
        i, j = pl.program_id(0), pl.program_id(1)

        @pl.when(jnp.logical_and(i == 0, j == 0))
        def _():
            dg_ref[...] = jnp.zeros_like(dg_ref)

        @pl.when(j == 0)
        def _():
            dy, dg = _rms_bwd(y_ref[...], g_ref[...], d_ref[...])
            dy = dy.astype(MXU_DTYPE)
            dys[...] = dy
            dy_ref[...] = dy
            dg_ref[...] += dg

        dx_ref[...] = _dot(dys[...], w_ref[...], "nt").astype(dx_ref.dtype)

    row = pl.BlockSpec((tm, dm), lambda i, j: (i, 0))
    vec = pl.BlockSpec((1, dm), lambda i, j: (0, 0))
    return pl.pallas_call(
        body, name=name, grid=(n // tm, nn // tn), in_specs=[row, vec, row, pl.BlockSpec((tn, dm), lambda i, j: (j, 0))],
        out_specs=[pl.BlockSpec((tm, tn), lambda i, j: (i, j)), row, vec],
        out_shape=[SDS((n, nn), out_dtype), SDS((n, dm), MXU_DTYPE), SDS((1, dm), F32)],
        scratch_shapes=[pltpu.VMEM((tm, dm), MXU_DTYPE)],
        compiler_params=_cp("arbitrary", "arbitrary"))(y, g, d, w)


def _mm_resid_norm(a, b, x, g, tm, name, target=None):
    n, k = a.shape
    d = b.shape[1]
    tm = min(tm, n)
    assert n % tm == 0 and x.shape == (n, d), (name, a.shape, b.shape)
    with_loss = target is not None

    def body(a_ref, b_ref, x_ref, g_ref, *rest):
        y = _dot(a_ref[...], b_ref[...])
        out = x_ref[...] + y * _rstd(y) * g_ref[...]
        if not with_loss:
            y_ref, o_ref = rest
            y_ref[...] = y
            o_ref[...] = out
            return
        t_ref, y_ref, d_ref, l_ref = rest
        y_ref[...] = y
        diff = out - t_ref[...]
        d_ref[...] = diff * (1.0 / d)

        @pl.when(pl.program_id(0) == 0)
        def _():
            l_ref[...] = jnp.zeros_like(l_ref)

        l_ref[...] += 0.5 * jnp.sum(jnp.mean(diff * diff, axis=-1, keepdims=True), axis=0, keepdims=True)

    row = pl.BlockSpec((tm, d), lambda i: (i, 0))
    ins = [pl.BlockSpec((tm, k), lambda i: (i, 0)), pl.BlockSpec((k, d), lambda i: (0, 0)), row, pl.BlockSpec((1, d), lambda i: (0, 0))]
    out = SDS((n, d), F32)
    if with_loss:
        return pl.pallas_call(body, name=name, grid=(n // tm,), in_specs=ins + [row], out_specs=[row, row, pl.BlockSpec((1, 1), lambda i: (0, 0))],
                              out_shape=[out, out, SDS((1, 1), F32)], compiler_params=_cp("arbitrary"))(a, b, x, g, target)
    return pl.pallas_call(body, name=name, grid=(n // tm,), in_specs=ins, out_specs=[row, row], out_shape=[out, out],
                          compiler_params=_cp("parallel"))(a, b, x, g)


def _dx_norm_bwd(a_parts, b, x, g, res, tm, name, after=(), b_turned=False):
    after, after_specs = _unread(after)
    parts, d, p = (b.shape[0], b.shape[2], b.shape[1]) if b_turned else b.shape
    form = "nn" if b_turned else "nt"
    n = x.shape[0]
    tm = min(tm, n)
    assert n % tm == 0 and len(a_parts) == parts and x.shape[1] == d, (name, x.shape, b.shape)

    def body(*refs):
        x_ref, g_ref, res_ref = refs[2 * parts:2 * parts + 3]
        dx_ref, dg_ref = refs[-2:]
        dh = _dot(refs[0][...], refs[parts][0], form)
        for s in range(1, parts):
            dh = dh + _dot(refs[s][...], refs[parts + s][0], form)
        dx, dg = _rms_bwd(x_ref[...], g_ref[...], dh)
        dx_ref[...] = dx + res_ref[...]

        @pl.when(pl.program_id(0) == 0)
        def _():
            dg_ref[...] = jnp.zeros_like(dg_ref)

        dg_ref[...] += dg

    a_specs = [pl.BlockSpec((tm, p), lambda i, cb=cb: (i, cb)) for _, cb in a_parts]
    b_specs = [pl.BlockSpec((1,) + b.shape[1:], lambda i, s=s: (s, 0, 0)) for s in range(parts)]
    row = pl.BlockSpec((tm, d), lambda i: (i, 0))
    vec = pl.BlockSpec((1, d), lambda i: (0, 0))
    return pl.pallas_call(
        body, name=name, grid=(n // tm,), in_specs=a_specs + b_specs + [row, vec, row] + after_specs,
        out_specs=[row, vec], out_shape=[SDS((n, d), F32), SDS((1, d), F32)],
        compiler_params=_cp("arbitrary"))(*[arr for arr, _ in a_parts], *([b] * parts), x, g, res, *after)


def _dw_by_owner(a, b, tn, first, into, tm, name):
    k, m = a.shape
    cnt = b.shape[1] // tn
    tm = min(tm, m)
    assert m % tm == 0 and b.shape[1] == cnt * tn and first + cnt <= 4, (name, a.shape, b.shape)

    def body(a_ref, b_ref, *rest):
        rest[-1][0] = _dot(a_ref[...], b_ref[...], "tn").astype(rest[-1].dtype)

    extra = [] if into is None else [into]
    return pl.pallas_call(
        body, name=name, grid=(m // tm, cnt),
        in_specs=[pl.BlockSpec((k, tm), lambda i, j: (0, i)), pl.BlockSpec((k, tn), lambda i, j: (0, j))] + [pl.BlockSpec(memory_space=pl.ANY)] * len(extra),
        out_specs=pl.BlockSpec((1, tm, tn), lambda i, j: (first + j, i, 0)), out_shape=SDS((4, m, tn), WIRE_DTYPE),
        input_output_aliases={2: 0} if extra else {},
        compiler_params=_cp("parallel", "parallel"))(a, b, *extra)


def _norm_mm(x, g, w, out_dtype, tm, tn, name, after=(), w_turned=False):
    after, after_specs = _unread(after)
    m, d = x.shape
    sharded = w.ndim == 3
    n = w.shape[0] if w_turned else w.shape[-1] * (w.shape[0] if sharded else 1)
    tm, tn = min(tm, m), (w.shape[-1] if sharded else min(tn, n))
    assert m % tm == 0 and n % tn == 0 and not (sharded and w_turned), (name, m, n, tm, tn)

    def body(x_ref, g_ref, w_ref, *rest):
        o_ref, h_ref, hs = rest[-3:]

        @pl.when(pl.program_id(1) == 0)
        def _():
            xv = x_ref[...]
            h = (xv * _rstd(xv) * g_ref[...]).astype(MXU_DTYPE)
            hs[...] = h
            h_ref[...] = h

        o_ref[...] = _dot(hs[...], w_ref[0] if sharded else w_ref[...], "nt" if w_turned else "nn").astype(o_ref.dtype)

    if w_turned:
        w_spec = pl.BlockSpec((tn, d), lambda i, j: (j, 0))
    else:
        w_spec = pl.BlockSpec((1, d, tn), lambda i, j: (j, 0, 0)) if sharded else pl.BlockSpec((d, tn), lambda i, j: (0, j))
    return pl.pallas_call(
        body, name=name, grid=(m // tm, n // tn),
        in_specs=[pl.BlockSpec((tm, d), lambda i, j: (i, 0)), pl.BlockSpec((1, d), lambda i, j: (0, 0)), w_spec] + after_specs,
        out_specs=[pl.BlockSpec((tm, tn), lambda i, j: (i, j)), pl.BlockSpec((tm, d), lambda i, j: (i, 0))],
        out_shape=[SDS((m, n), out_dtype), SDS((m, d), MXU_DTYPE)],
        scratch_shapes=[pltpu.VMEM((tm, d), MXU_DTYPE)],
        compiler_params=_cp("parallel", "arbitrary"))(x, g, w, *after)


ROW_TILE = 512
TOKEN_TILE = 1024


def _norm_bwd(x, g, dy, res, out_dtype, name):
    n, d = x.shape
    tr = min(ROW_TILE, n)
    has_res = res is not None

    def body(*refs):
        x_ref, g_ref, dy_ref = refs[:3]
        dx_ref, dg_ref = refs[-2:]
        dx, dg = _rms_bwd(x_ref[...], g_ref[...], dy_ref[...].astype(F32))
        if has_res:
            dx = dx + refs[3][...]
        dx_ref[...] = dx.astype(dx_ref.dtype)

        @pl.when(pl.program_id(0) == 0)
        def _():
            dg_ref[...] = jnp.zeros_like(dg_ref)

        dg_ref[...] += dg

    row = pl.BlockSpec((tr, d), lambda i: (i, 0))
    vec = pl.BlockSpec((1, d), lambda i: (0, 0))
    ins = [x, g, dy] + ([res] if has_res else [])
    return pl.pallas_call(
        body, name=name, grid=(n // tr,), in_specs=[row, vec, row] + ([row] if has_res else []),
        out_specs=[row, vec], out_shape=[SDS((n, d), out_dtype), SDS((1, d), F32)],
        compiler_params=_cp("arbitrary"))(*ins)


def _rope_tables(n):
    pairs = ATT_HEAD_DIM // 4
    t = np.arange(n)
    inv = np.power(ROPE_THETA, -np.arange(pairs, dtype=np.float32) / pairs).astype(np.float32)
    ang = np.concatenate([(t // GRID_W)[:, None].astype(np.float32) * inv, (t % GRID_W)[:, None].astype(np.float32) * inv], axis=-1)
    cos = np.repeat(np.cos(ang), 2, axis=-1)
    sin = np.repeat(np.sin(ang), 2, axis=-1) * np.tile(np.array([-1.0, 1.0], np.float32), ATT_HEAD_DIM // 2)
    return jnp.asarray(np.tile(cos, 2), F32), jnp.asarray(np.tile(sin, 2), F32)


def _swap_pairs(x):
    lane = lax.broadcasted_iota(jnp.int32, x.shape, 1)
    return jnp.where((lane & 1) == 0, pltpu.roll(x, 127, axis=1), pltpu.roll(x, 1, axis=1))


def _head_mean(v):
    lane = lax.broadcasted_iota(jnp.int32, v.shape, 1)
    lo = jnp.where(lane < ATT_HEAD_DIM, v, 0.0)
    s0 = jnp.sum(lo, axis=-1, keepdims=True)
    s1 = jnp.sum(v - lo, axis=-1, keepdims=True)
    return jnp.where(lane < ATT_HEAD_DIM, s0, s1) * (1.0 / ATT_HEAD_DIM)


def _qk_prep(p, gq, gk, cos, sin, name):
    n = p.shape[0]
    tr = min(ROW_TILE, n)

    def one(xv, g, c, s):
        xn = xv * lax.rsqrt(_head_mean(xv * xv) + EPS) * g
        return xn * c + _swap_pairs(xn) * s

    def body(q_ref, k_ref, gq_ref, gk_ref, c_ref, s_ref, qo_ref, ko_ref):
        c, s = c_ref[...], s_ref[...]
        for j in range(ATT_Q_DIM // 128):
            qo_ref[:, j * 128:(j + 1) * 128] = one(q_ref[:, j * 128:(j + 1) * 128], gq_ref[...], c, s).astype(qo_ref.dtype)
        ko_ref[...] = one(k_ref[...], gk_ref[...], c, s).astype(ko_ref.dtype)

    vec = pl.BlockSpec((1, 128), lambda i: (0, 0))
    tab = pl.BlockSpec((tr, 128), lambda i: (i, 0))
    return pl.pallas_call(
        body, name=name, grid=(n // tr,),
        in_specs=[pl.BlockSpec((tr, ATT_Q_DIM), lambda i: (i, 0)), pl.BlockSpec((tr, 128), lambda i: (i, OFF_AK // 128)), vec, vec, tab, tab],
        out_specs=[pl.BlockSpec((tr, ATT_Q_DIM), lambda i: (i, 0)), tab],
        out_shape=[SDS((n, ATT_Q_DIM), MXU_DTYPE), SDS((n, ATT_KV_DIM), MXU_DTYPE)],
        compiler_params=_cp("parallel"))(p, p, gq, gk, cos, sin)


def _qk_prep_bwd(p, gq, gk, cos, sin, dq, dk, name):
    n = p.shape[0]
    tr = min(ROW_TILE, n)

    def one(xv, g, c, s, dout):
        dxn = dout * c + _swap_pairs(dout * s)
        r = lax.rsqrt(_head_mean(xv * xv) + EPS)
        xh = xv * r
        dn = dxn * g
        dx = r * (dn - xh * _head_mean(dn * xh))
        return dx, jnp.sum(dxn * xh, axis=0, keepdims=True)

    def body(q_ref, k_ref, gq_ref, gk_ref, c_ref, s_ref, dq_ref, dk_ref, dqo_ref, dko_ref, dgq_ref, dgk_ref):
        @pl.when(pl.program_id(0) == 0)
        def _():
            dgq_ref[...] = jnp.zeros_like(dgq_ref)
            dgk_ref[...] = jnp.zeros_like(dgk_ref)

        c, s = c_ref[...], s_ref[...]
        for j in range(ATT_Q_DIM // 128):
            sl = slice(j * 128, (j + 1) * 128)
            dx, dg = one(q_ref[:, sl], gq_ref[...], c, s, dq_ref[:, sl])
            dqo_ref[:, sl] = dx.astype(dqo_ref.dtype)
            dgq_ref[:, sl] += dg
        dx, dg = one(k_ref[...], gk_ref[...], c, s, dk_ref[...])
        dko_ref[...] = dx.astype(dko_ref.dtype)
        dgk_ref[...] += dg

    vec = pl.BlockSpec((1, 128), lambda i: (0, 0))
    tab = pl.BlockSpec((tr, 128), lambda i: (i, 0))
    qrow = pl.BlockSpec((tr, ATT_Q_DIM), lambda i: (i, 0))
    return pl.pallas_call(
        body, name=name, grid=(n // tr,),
        in_specs=[qrow, pl.BlockSpec((tr, 128), lambda i: (i, OFF_AK // 128)), vec, vec, tab, tab, qrow, tab],
        out_specs=[qrow, tab, pl.BlockSpec((1, ATT_Q_DIM), lambda i: (0, 0)), vec],
        out_shape=[SDS((n, ATT_Q_DIM), MXU_DTYPE), SDS((n, ATT_KV_DIM), MXU_DTYPE), SDS((1, ATT_Q_DIM), F32), SDS((1, 128), F32)],
        compiler_params=_cp("arbitrary"))(p, p, gq, gk, cos, sin, dq, dk)


ATT_FWD_STEP = (256, 4)
ATT_BWD_STEP = (512, 2)


def _attn_fwd(q, k, v, name):
    n = q.shape[0]
    tq, step_heads = min(ATT_FWD_STEP[0], n), ATT_FWD_STEP[1]
    scale = ATT_HEAD_DIM ** -0.5
    gw = step_heads * ATT_HEAD_DIM
    parts = ATT_GROUP // step_heads

    def body(q_ref, k_ref, v_ref, o_ref):
        kk, vv = k_ref[0], v_ref[0]
        v_ones = jnp.concatenate([vv, jnp.ones_like(vv)], axis=1)
        outs = []
        for g in range(step_heads):
            s = _dot(q_ref[:, g * ATT_HEAD_DIM:(g + 1) * ATT_HEAD_DIM] * scale, kk, "nt")
            e = jnp.exp(s - jnp.max(s, axis=-1, keepdims=True))
            ov = _dot(e, v_ones)
            outs.append(ov[:, :ATT_HEAD_DIM] / ov[:, ATT_HEAD_DIM:])
        o_ref[...] = jnp.concatenate(outs, axis=-1).astype(o_ref.dtype)

    kv = pl.BlockSpec((1, n, ATT_HEAD_DIM), lambda h, i, pr: (h, 0, 0))
    qb = pl.BlockSpec((tq, gw), lambda h, i, pr: (i, h * parts + pr))
    return pl.pallas_call(
        body, name=name, grid=(ATT_KV_HEADS, n // tq, parts), in_specs=[qb, kv, kv],
        out_specs=qb, out_shape=SDS((n, ATT_Q_DIM), MXU_DTYPE),
        compiler_params=_cp("parallel", "parallel", "parallel"))(q, k, v)


def _attn_bwd(q, k, v, o, do, name):
    n = q.shape[0]
    tq, step_heads = min(ATT_BWD_STEP[0], n), ATT_BWD_STEP[1]
    scale = ATT_HEAD_DIM ** -0.5
    gw = step_heads * ATT_HEAD_DIM
    parts = ATT_GROUP // step_heads

    def body(q_ref, k_ref, v_ref, o_ref, do_ref, dq_ref, dk_ref, dv_ref):
        @pl.when(jnp.logical_and(pl.program_id(1) == 0, pl.program_id(2) == 0))
        def _():
            dk_ref[...] = jnp.zeros_like(dk_ref)
            dv_ref[...] = jnp.zeros_like(dv_ref)

        kk, vv = k_ref[0], v_ref[0]
        dqs = []
        dk_acc = jnp.zeros((ATT_HEAD_DIM, n), F32)
        dv_acc = jnp.zeros((ATT_HEAD_DIM, n), F32)
        for g in range(step_heads):
            sl = slice(g * ATT_HEAD_DIM, (g + 1) * ATT_HEAD_DIM)
            qg, dog = q_ref[:, sl] * scale, do_ref[:, sl].astype(F32)
            s = _dot(qg, kk, "nt")
            e = jnp.exp(s - jnp.max(s, axis=-1, keepdims=True))
            inv = 1.0 / jnp.sum(e, axis=-1, keepdims=True)
            delta = jnp.sum(dog * o_ref[:, sl].astype(F32), axis=-1, keepdims=True)
            dse = e * (_dot(dog, vv, "nt") - delta)
            dqs.append(_dot(dse, kk) * (inv * scale))
            dk_acc += _dot(qg.astype(F32) * inv, dse, "tn")
            dv_acc += _dot(dog * inv, e, "tn")
        dq_ref[...] = jnp.concatenate(dqs, axis=-1)
        dk_ref[0] += dk_acc
        dv_ref[0] += dv_acc

    kv = pl.BlockSpec((1, n, ATT_HEAD_DIM), lambda h, i, pr: (h, 0, 0))
    kvt = pl.BlockSpec((1, ATT_HEAD_DIM, n), lambda h, i, pr: (h, 0, 0))
    qb = pl.BlockSpec((tq, gw), lambda h, i, pr: (i, h * parts + pr))
    return pl.pallas_call(
        body, name=name, grid=(ATT_KV_HEADS, n // tq, parts), in_specs=[qb, kv, kv, qb, qb], out_specs=[qb, kvt, kvt],
        out_shape=[SDS((n, ATT_Q_DIM), F32), SDS((ATT_KV_HEADS, ATT_HEAD_DIM, n), F32), SDS((ATT_KV_HEADS, ATT_HEAD_DIM, n), F32)],
        compiler_params=_cp("parallel", "arbitrary", "arbitrary"))(q, k, v, o, do)


def _both_directions(mats, axis):
    fwd = np.concatenate(mats, axis=axis).astype(np.float32)
    bwd = np.concatenate([m[::-1, ::-1] for m in mats], axis=axis).astype(np.float32)
    return jnp.asarray(np.stack([fwd, bwd]), MXU_DTYPE)


def _hg_segments():
    c = HG_CHUNK
    t = np.arange(c)[:, None]
    r = np.arange(c)[None, :]
    mats = [(r <= t)]
    for lev in range(HG_LEVELS):
        h = c >> (lev + 1)
        mid = (t // (2 * h)) * (2 * h) + h - 1
        hi = (t // h) % 2 == 1
        mats.append(np.where(hi, (r > mid) & (r <= t), (r > t) & (r <= mid)))
    mats.append(r > t)
    return _both_directions(mats, 0)


def _hg_pair_sums():
    c = HG_CHUNK
    r = np.arange(c)[:, None]
    t = np.arange(c)[None, :]
    gp, gn = [t >= r], [t < r]
    for lev in range(HG_LEVELS):
        sh = HG_LEVELS - 1 - lev
        same = (r >> sh) == (t >> sh)
        gp.append(same & (t >= r))
        gn.append(same & (t < r))
    return _both_directions(gp, 1), _both_directions(gn, 1)


def _split_dot(mat, x):
    hi = x.astype(MXU_DTYPE)
    lo = (x - hi.astype(F32)).astype(MXU_DTYPE)
    return _dot(mat, hi) + _dot(mat, lo)


def _hg_gates(hq, z, a0, a1):
    q = hq * _sigmoid(hq)
    sg = _sigmoid(z)
    lb = _sigmoid(a0 - a1)
    f = lb + (1.0 - lb) * sg
    k = (1.0 - lb) * (1.0 - sg)
    return q, f, k, sg, lb


def _hg_level_masks():
    c = HG_CHUNK
    t = np.arange(c)
    later, same = [], []
    for lev in range(HG_LEVELS):
        sh = HG_LEVELS - 1 - lev
        later.append(np.broadcast_to((((t >> sh) & 1) == 1)[:, None], (c, HG_HEAD_DIM)))
        same.append((t[:, None] >> (sh + 1)) == (t[None, :] >> (sh + 1)))
    same.append(t[:, None] == t[None, :])
    later = np.stack(later).astype(np.float32)
    return jnp.asarray(np.stack([later, 1.0 - later]), F32), jnp.asarray(np.stack(same).astype(np.float32), F32)


def _hg_level(q, k, ex, later_ref, lev):
    e = ex[lev + 1]
    e_q = e * later_ref[0, lev]
    e_k = e - e_q
    return q * e_q, k * e_k, e_q, e_k


def _hg_intra(q, k, ex, later_ref, same_ref):
    a = same_ref[HG_LEVELS] * jnp.sum(q * k, axis=-1, keepdims=True)
    for lev in range(HG_LEVELS):
        qs, ks, _, _ = _hg_level(q, k, ex, later_ref, lev)
        a = a + same_ref[lev] * _dot(qs, ks, "nt")
    return a


def _hg_specs(n, with_time):
    c = HG_CHUNK
    nc = n // c

    def chunk(d, i):
        first = d if with_time else 1 - d
        return i + first * (nc - 1 - 2 * i)

    def pcols(off, dir_stride=0):
        return [pl.BlockSpec((c, HG_PAIR), lambda d, i, j=j: (chunk(d, i), off // HG_PAIR + dir_stride // HG_PAIR * d + j)) for j in range(2)]

    specs = dict(
        hq=pcols(OFF_HQ), v=pcols(OFF_HI), z=pcols(OFF_ZF, OFF_ZB - OFF_ZF),
        shared=pl.BlockSpec((c, HG_DIM), lambda d, i: (chunk(d, i), 0)),
        per_dir=pl.BlockSpec((1, c, HG_DIM), lambda d, i: (d, chunk(d, i), 0)),
        vec=pl.BlockSpec((1, 1, HG_DIM), lambda d, i: (d, 0, 0)),
        seg=pl.BlockSpec((1, (HG_LEVELS + 2) * c, c), lambda d, i: (d, 0, 0)),
        sums=pl.BlockSpec((1, c, (HG_LEVELS + 1) * c), lambda d, i: (d, 0, 0)),
        later=pl.BlockSpec((1, HG_LEVELS, c, HG_HEAD_DIM), lambda d, i: (d, 0, 0, 0)),
        same=pl.BlockSpec((HG_LEVELS + 1, c, c), lambda d, i: (0, 0, 0)),
        state=pl.BlockSpec((1, HG_HEADS, 1, HG_HEAD_DIM, HG_HEAD_DIM), lambda d, i: (d, 0, chunk(d, i), 0, 0)),
        weights=pl.BlockSpec((1, HG_HEADS, 1, c, c), lambda d, i: (d, 0, chunk(d, i), 0, 0)),
        levels=pl.BlockSpec((1, HG_HEADS, 1, HG_LEVELS, c, HG_HEAD_DIM), lambda d, i: (d, 0, chunk(d, i), 0, 0, 0)),
        kept=pl.BlockSpec((1, HG_KEPT, c, HG_DIM), lambda d, i: (d, 0, chunk(d, i), 0)))
    return nc, specs


def _hg_head(refs, hh):
    off = (hh % 2) * HG_HEAD_DIM
    return refs[hh // 2][:, off:off + HG_HEAD_DIM]


def _hg_lanes(hh):
    return slice(hh * HG_HEAD_DIM, (hh + 1) * HG_HEAD_DIM)


def _hg_exps(seg_ref, f):
    c = HG_CHUNK
    args = _split_dot(seg_ref[0], jnp.log(f))
    return [jnp.exp(args[j * c:(j + 1) * c]) for j in range(HG_LEVELS + 2)]


def _hg_last_row(a, mirrored):
    return jnp.where(mirrored, a[0:1, :], a[HG_CHUNK - 1:HG_CHUNK, :])


def _hgrn_fwd(p, a0, a1, seg, masks, name):
    n = p.shape[0]
    nc, sp = _hg_specs(n, True)

    def body(hq0, hq1, z0, z1, v0, v1, a0_ref, a1_ref, seg_ref, later_ref, same_ref, o_ref, s0_ref, a_ref, e_ref, g_ref, st):
        @pl.when(pl.program_id(1) == 0)
        def _():
            st[...] = jnp.zeros_like(st)

        mirrored = pl.program_id(0) == 1
        for hh in range(HG_HEADS):
            ln = _hg_lanes(hh)
            hqv = _hg_head((hq0, hq1), hh)
            q, f, k, sg, _ = _hg_gates(hqv, _hg_head((z0, z1), hh), a0_ref[0, :, ln], a1_ref[0, :, ln])
            vv = _hg_head((v0, v1), hh)
            ex = _hg_exps(seg_ref, f)
            for lev in range(HG_LEVELS):
                e_ref[0, hh, 0, lev] = ex[lev + 1].astype(e_ref.dtype)
            sq = _sigmoid(hqv)
            for j, kept in enumerate((q, k, f, sg, sq * (1.0 + hqv * (1.0 - sq)), ex[0], ex[HG_LEVELS + 1])):
                g_ref[0, j, :, ln] = kept
            a = _hg_intra(q, k, ex, later_ref, same_ref).astype(MXU_DTYPE)
            a_ref[0, hh, 0] = a
            s_t = st[hh]
            s0_ref[0, hh, 0] = s_t
            o_ref[0, :, ln] = _dot(a, vv) + _dot(q * ex[0], s_t, "nt")
            st[hh] = s_t * _hg_last_row(ex[0], mirrored) + _dot(vv, k * ex[HG_LEVELS + 1], "tn")

    return pl.pallas_call(
        body, name=name, grid=(2, nc), in_specs=sp["hq"] + sp["z"] + sp["v"] + [sp["vec"], sp["vec"], sp["seg"], sp["later"], sp["same"]],
        out_specs=[sp["per_dir"], sp["state"], sp["weights"], sp["levels"], sp["kept"]],
        out_shape=[SDS((2, n, HG_DIM), F32), SDS((2, HG_HEADS, nc, HG_HEAD_DIM, HG_HEAD_DIM), F32),
                   SDS((2, HG_HEADS, nc, HG_CHUNK, HG_CHUNK), MXU_DTYPE),
                   SDS((2, HG_HEADS, nc, HG_LEVELS, HG_CHUNK, HG_HEAD_DIM), MXU_DTYPE), SDS((2, HG_KEPT, n, HG_DIM), F32)],
        scratch_shapes=[pltpu.VMEM((HG_HEADS, HG_HEAD_DIM, HG_HEAD_DIM), F32)],
        compiler_params=_cp("parallel", "arbitrary"))(p, p, p, p, p, p, a0, a1, seg, *masks)


def _hgrn_bwd(p, a0, a1, masks, gp, gn, do, s0, a, e, kept, name):
    n = p.shape[0]
    nc, sp = _hg_specs(n, False)


    def body(v0, v1, a0_ref, a1_ref, later_ref, same_ref, gp_ref, gn_ref, do_ref, s0_ref, a_ref, e_ref, g_ref,
             dhq_ref, dz_ref, dv_ref, dlb_ref, rt):
        @pl.when(pl.program_id(1) == 0)
        def _():
            rt[...] = jnp.zeros_like(rt)
            dlb_ref[...] = jnp.zeros_like(dlb_ref)

        mirrored = pl.program_id(0) == 1
        for hh in range(HG_HEADS):
            ln = _hg_lanes(hh)
            q, k, f, sg, dsilu, e_first, e_last = (g_ref[0, j, :, ln] for j in range(HG_KEPT))
            lb = _sigmoid(a0_ref[0, :, ln] - a1_ref[0, :, ln])
            vv, dov = _hg_head((v0, v1), hh), do_ref[:, ln]
            ex = [e_first] + [e_ref[0, hh, 0, lev].astype(F32) for lev in range(HG_LEVELS)] + [e_last]
            a = a_ref[0, hh, 0]
            da = _dot(dov, vv, "nt")
            diag = jnp.sum(dov * vv, axis=-1, keepdims=True)
            s_t = s0_ref[0, hh, 0]
            r_t = rt[hh]
            k_end = k * ex[HG_LEVELS + 1]
            dv_ref[0, :, ln] = _dot(a, dov, "tn") + _dot(k_end, r_t, "nt")
            dq_inter = ex[0] * _dot(dov, s_t)
            dk_inter = ex[HG_LEVELS + 1] * _dot(vv, r_t)
            dq = diag * k + dq_inter
            dk = diag * q + dk_inter
            q_terms, k_terms = [q * dq_inter], [k * dk_inter]
            for lev in range(HG_LEVELS):
                qs, ks, e_q, e_k = _hg_level(q, k, ex, later_ref, lev)
                pairs = da * same_ref[lev]
                q_part = e_q * _dot(pairs, ks)
                k_part = e_k * _dot(pairs, qs, "tn")
                dq, dk = dq + q_part, dk + k_part
                q_terms.append(q * q_part)
                k_terms.append(k * k_part)
            decay = _hg_last_row(ex[0], mirrored)
            rt[hh] = r_t * decay + _dot(dov, q * ex[0], "tn")
            later = decay * jnp.sum(s_t * r_t, axis=0, keepdims=True)
            dlf = _dot(gp_ref[0], jnp.concatenate(q_terms, axis=0)) + _dot(gn_ref[0], jnp.concatenate(k_terms, axis=0)) + later
            df = dlf / f - dk
            dz_ref[0, :, ln] = df * (1.0 - lb) * sg * (1.0 - sg)
            dlb_ref[0, :, ln] += jnp.sum(df * (1.0 - sg), axis=0, keepdims=True)
            dhq_ref[0, :, ln] = dq * dsilu

    out = SDS((2, n, HG_DIM), F32)
    return pl.pallas_call(
        body, name=name, grid=(2, nc),
        in_specs=sp["v"] + [sp["vec"], sp["vec"], sp["later"], sp["same"], sp["sums"], sp["sums"],
                            sp["shared"], sp["state"], sp["weights"], sp["levels"], sp["kept"]],
        out_specs=[sp["per_dir"], sp["per_dir"], sp["per_dir"], sp["vec"]], out_shape=[out, out, out, SDS((2, 1, HG_DIM), F32)],
        scratch_shapes=[pltpu.VMEM((HG_HEADS, HG_HEAD_DIM, HG_HEAD_DIM), F32)],
        compiler_params=_cp("parallel", "arbitrary"))(p, p, a0, a1, *masks, gp, gn, do, s0, a, e, kept)


def _hg_post(o2, p, g, name):
    n = p.shape[0]
    tr = min(ROW_TILE, n)
    w = 2 * HG_HEAD_DIM

    def body(of_ref, ob_ref, hg_ref, g_ref, o_ref):
        for j in range(2):
            sl = slice(j * HG_HEAD_DIM, (j + 1) * HG_HEAD_DIM)
            o = of_ref[0, :, sl] + ob_ref[0, :, sl]
            hg = hg_ref[:, sl]
            o_ref[:, sl] = (o * _rstd(o) * g_ref[...] * (hg * _sigmoid(hg))).astype(o_ref.dtype)

    blk = pl.BlockSpec((tr, w), lambda i, j: (i, j))
    dirs = [pl.BlockSpec((1, tr, w), lambda i, j, d=d: (d, i, j)) for d in range(2)]
    return pl.pallas_call(
        body, name=name, grid=(n // tr, HG_DIM // w),
        in_specs=dirs + [pl.BlockSpec((tr, w), lambda i, j: (i, OFF_HG // w + j)), pl.BlockSpec((1, HG_HEAD_DIM), lambda i, j: (0, 0))],
        out_specs=blk, out_shape=SDS((n, HG_DIM), MXU_DTYPE), compiler_params=_cp("parallel", "parallel"))(o2, o2, p, g)


def _hg_post_bwd(o2, p, g, dcat, name, after=()):
    n = p.shape[0]
    tr = min(ROW_TILE, n)
    w = 2 * HG_HEAD_DIM
    after, after_specs = _unread(after)

    def body(of_ref, ob_ref, hg_ref, g_ref, d_ref, *rest):
        do_ref, dhg_ref, dg_ref = rest[len(after):]

        @pl.when(pl.program_id(1) == 0)
        def _():
            dg_ref[...] = jnp.zeros_like(dg_ref)

        for j in range(2):
            sl = slice(j * HG_HEAD_DIM, (j + 1) * HG_HEAD_DIM)
            o = of_ref[0, :, sl] + ob_ref[0, :, sl]
            hg = hg_ref[:, sl]
            d = d_ref[:, sl].astype(F32)
            sg = _sigmoid(hg)
            on = o * _rstd(o) * g_ref[...]
            dhg_ref[:, sl] = (d * on * sg * (1.0 + hg * (1.0 - sg))).astype(dhg_ref.dtype)
            dx, dg = _rms_bwd(o, g_ref[...], d * hg * sg)
            do_ref[:, sl] = dx
            dg_ref[0, :, sl] += dg

    blk = pl.BlockSpec((tr, w), lambda j, i: (i, j))
    dirs = [pl.BlockSpec((1, tr, w), lambda j, i, d=d: (d, i, j)) for d in range(2)]
    return pl.pallas_call(
        body, name=name, grid=(HG_DIM // w, n // tr),
        in_specs=dirs + [pl.BlockSpec((tr, w), lambda j, i: (i, OFF_HG // w + j)), pl.BlockSpec((1, HG_HEAD_DIM), lambda j, i: (0, 0)),
                         pl.BlockSpec((tr, w), lambda j, i: (i, ATT_Q_DIM // w + j))] + after_specs,
        out_specs=[blk, blk, pl.BlockSpec((1, 1, w), lambda j, i: (j, 0, 0))],
        out_shape=[SDS((n, HG_DIM), F32), SDS((n, HG_DIM), MXU_DTYPE), SDS((HG_DIM // w, 1, w), F32)],
        compiler_params=_cp("parallel", "arbitrary"))(o2, o2, p, g, dcat, *after)


XATT_TQ = 512


def _xattn_fwd(q, kv, name):
    n, nm = q.shape[0], kv.shape[0]
    tq = min(XATT_TQ, n)
    scale = X_HEAD_DIM ** -0.5

    def body(q_ref, k_ref, v_ref, o_ref):
        s = _dot(q_ref[...], k_ref[...], "nt") * scale
        e = jnp.exp(s - jnp.max(s, axis=-1, keepdims=True))
        o_ref[...] = _dot(e / jnp.sum(e, axis=-1, keepdims=True), v_ref[...]).astype(o_ref.dtype)

    qb = pl.BlockSpec((tq, X_HEAD_DIM), lambda h, i: (i, h))
    return pl.pallas_call(
        body, name=name, grid=(X_HEADS, n // tq),
        in_specs=[qb, pl.BlockSpec((nm, X_HEAD_DIM), lambda h, i: (0, h)), pl.BlockSpec((nm, X_HEAD_DIM), lambda h, i: (0, X_HEADS + h))],
        out_specs=qb, out_shape=SDS(q.shape, MXU_DTYPE), compiler_params=_cp("parallel", "parallel"))(q, kv, kv)


def _xattn_bwd(q, kv, do, name, after=()):
    n, nm = q.shape[0], kv.shape[0]
    tq = min(XATT_TQ, n)
    scale = X_HEAD_DIM ** -0.5
    after, after_specs = _unread(after)

    def body(q_ref, k_ref, v_ref, do_ref, *rest):
        dq_ref, dk_ref, dv_ref = rest[len(after):]

        @pl.when(pl.program_id(1) == 0)
        def _():
            dk_ref[...] = jnp.zeros_like(dk_ref)
            dv_ref[...] = jnp.zeros_like(dv_ref)

        qv, dov = q_ref[...], do_ref[...]
        s = _dot(qv, k_ref[...], "nt") * scale
        e = jnp.exp(s - jnp.max(s, axis=-1, keepdims=True))
        p = e / jnp.sum(e, axis=-1, keepdims=True)
        dp = _dot(dov, v_ref[...], "nt")
        ds = p * (dp - jnp.sum(p * dp, axis=-1, keepdims=True)) * scale
        dq_ref[...] = _dot(ds, k_ref[...]).astype(dq_ref.dtype)
        dk_ref[...] += _dot(ds, qv, "tn")
        dv_ref[...] += _dot(p, dov, "tn")

    qb = pl.BlockSpec((tq, X_HEAD_DIM), lambda h, i: (i, h))
    kb = pl.BlockSpec((nm, X_HEAD_DIM), lambda h, i: (0, h))
    return pl.pallas_call(
        body, name=name, grid=(X_HEADS, n // tq),
        in_specs=[qb, kb, pl.BlockSpec((nm, X_HEAD_DIM), lambda h, i: (0, X_HEADS + h)), qb] + after_specs, out_specs=[qb, kb, kb],
        out_shape=[SDS(q.shape, MXU_DTYPE), SDS((nm, X_HEADS * X_HEAD_DIM), F32), SDS((nm, X_HEADS * X_HEAD_DIM), F32)],
        compiler_params=_cp("parallel", "arbitrary"))(q, kv, kv, do, *after)


def _edge_rows(shape):
    row = lax.broadcasted_iota(jnp.int32, shape, 0)
    return row == 0, row == shape[0] - 1


def _shift_rows(u, down, edges):
    if down:
        return jnp.where(edges[0], 0.0, pltpu.roll(u, 1, axis=0))
    return jnp.where(edges[1], 0.0, pltpu.roll(u, u.shape[0] - 1, axis=0))


def _conv(u, w, b, edges):
    return b + _shift_rows(u, True, edges) * w[0:1, :] + u * w[1:2, :] + _shift_rows(u, False, edges) * w[2:3, :]


def _ff_specs(n):
    gate = lambda rows: pl.BlockSpec((rows, FF_COLS), lambda j: (0, j))
    val = lambda rows: pl.BlockSpec((rows, FF_COLS), lambda j: (0, FF_BLOCKS + j))
    return [gate(n), val(n), gate(3), val(3), gate(1), val(1)], gate


def _conv_gate(u, cw, cb, name):
    n = u.shape[0]
    ins, gate_blk = _ff_specs(n)

    def body(ug_ref, uv_ref, wg_ref, wv_ref, bg_ref, bv_ref, o_ref):
        edges = _edge_rows(ug_ref.shape)
        gate = _conv(ug_ref[...], wg_ref[...], bg_ref[...], edges)
        val = _conv(uv_ref[...], wv_ref[...], bv_ref[...], edges)
        o_ref[...] = (gate * _sigmoid(gate) * val).astype(o_ref.dtype)

    return pl.pallas_call(
        body, name=name, grid=(FF_BLOCKS,), in_specs=ins, out_specs=gate_blk(n), out_shape=SDS((n, D_FF), MXU_DTYPE),
        compiler_params=_cp("parallel"))(u, u, cw, cw, cb, cb)


def _conv_gate_bwd(u, cw, cb, da, name, after=()):
    n = u.shape[0]
    ins, gate_blk = _ff_specs(n)
    after, after_specs = _unread(after)

    def side(dacc, u, w, edges, du_ref, dw_ref, db_ref):
        nxt, prv = _shift_rows(dacc, False, edges), _shift_rows(dacc, True, edges)
        du_ref[...] = (nxt * w[0:1, :] + dacc * w[1:2, :] + prv * w[2:3, :]).astype(du_ref.dtype)
        db_ref[...] = jnp.sum(dacc, axis=0, keepdims=True)
        dw_ref[0:1, :] = jnp.sum(nxt * u, axis=0, keepdims=True)
        dw_ref[1:2, :] = jnp.sum(dacc * u, axis=0, keepdims=True)
        dw_ref[2:3, :] = jnp.sum(prv * u, axis=0, keepdims=True)

    def body(ug_ref, uv_ref, wg_ref, wv_ref, bg_ref, bv_ref, da_ref, *rest):
        dug_ref, duv_ref, dwg_ref, dwv_ref, dbg_ref, dbv_ref = rest[len(after):]
        ug, uv = ug_ref[...], uv_ref[...]
        edges = _edge_rows(ug.shape)
        gate = _conv(ug, wg_ref[...], bg_ref[...], edges)
        val = _conv(uv, wv_ref[...], bv_ref[...], edges)
        sg = _sigmoid(gate)
        dav = da_ref[...].astype(F32)
        side(dav * val * sg * (1.0 + gate * (1.0 - sg)), ug, wg_ref[...], edges, dug_ref, dwg_ref, dbg_ref)
        side(dav * gate * sg, uv, wv_ref[...], edges, duv_ref, dwv_ref, dbv_ref)

    return pl.pallas_call(
        body, name=name, grid=(FF_BLOCKS,), in_specs=ins + [gate_blk(n)] + after_specs,
        out_specs=[gate_blk(n), gate_blk(n), gate_blk(3), gate_blk(3), gate_blk(1), gate_blk(1)],
        out_shape=[SDS((n, D_FF), MXU_DTYPE)] * 2 + [SDS((3, D_FF), F32)] * 2 + [SDS((1, D_FF), F32)] * 2,
        compiler_params=_cp("parallel"))(u, u, cw, cw, cb, cb, da, *after)


def _adamw_update(w_ref, gv, m_ref, v_ref, d_ref, mo_ref, vo_ref, go_ref):
    go_ref[...] = gv
    mn = ADAM_B1 * m_ref[...] + (1.0 - ADAM_B1) * gv
    vn = ADAM_B2 * v_ref[...] + (1.0 - ADAM_B2) * gv * gv
    m_hat = mn / (1.0 - ADAM_B1 ** ADAM_STEP)
    v_hat = vn / (1.0 - ADAM_B2 ** ADAM_STEP)
    d_ref[...] = -ADAM_LR * (m_hat / (jnp.sqrt(v_hat) + ADAM_EPS) + ADAM_WD * w_ref[...])
    mo_ref[...] = mn
    vo_ref[...] = vn


def _adamw_rows(summed, spans, params, name):
    nt = len(params)

    def body(s_ref, *refs):
        for t, (off, d) in enumerate(spans):
            w_ref, m_ref, v_ref = refs[3 * t:3 * t + 3]
            _adamw_update(w_ref, s_ref[:, off:off + d], m_ref, v_ref, *refs[3 * nt + 4 * t:3 * nt + 4 * t + 4])

    assert all(p[0].shape == (1, d) for p, (_, d) in zip(params, spans)), name
    vm = pl.BlockSpec(memory_space=pltpu.VMEM)
    out = pl.pallas_call(body, name=name, in_specs=[vm] * (1 + 3 * nt), out_specs=[vm] * (4 * nt),
                         out_shape=[SDS(p[0].shape, F32) for p in params for _ in range(4)])(summed, *[a for p in params for a in p])
    return [tuple(out[4 * t:4 * t + 4]) for t in range(nt)]


def _adamw(params, name):
    nt = len(params)
    rows = [p[0].shape[-2] for p in params]
    fits = lambda s: max(rows) <= s * ELEMENTWISE_ROWS and all(r % s == 0 and (s == 1 or r // s % 8 == 0) for r in rows)
    steps = next(s for s in range(1, max(rows) + 1) if fits(s))
    assert all(p[0].ndim == 2 or p[0].shape[:-2] == (1,) or (p[0].ndim == 3 and steps == 1) for p in params), name

    def body(*refs):
        for t in range(nt):
            w_ref, g_ref, m_ref, v_ref = refs[4 * t:4 * t + 4]
            _adamw_update(w_ref, g_ref[...], m_ref, v_ref, *refs[4 * (nt + t):4 * (nt + t) + 4])

    def blk(w):
        tr, c = w.shape[-2] // steps, w.shape[-1]
        return pl.BlockSpec((tr, c), lambda i: (i, 0)) if w.ndim == 2 else pl.BlockSpec((w.shape[0], tr, c), lambda i: (0, i, 0))

    specs = [blk(p[0]) for p in params for _ in range(4)]
    out = pl.pallas_call(body, name=name, grid=(steps,), in_specs=specs, out_specs=specs,
                         out_shape=[SDS(p[0].shape, F32) for p in params for _ in range(4)],
                         compiler_params=_cp("parallel"))(*[a for p in params for a in p])
    return [tuple(out[4 * t:4 * t + 4]) for t in range(nt)]


ANY = pl.BlockSpec(memory_space=pl.ANY)


def _place():
    x, y, c = lax.axis_index("x"), lax.axis_index("y"), lax.axis_index("c")
    return x, y, c, [(1 - x, y), (x, 1 - y), (1 - x, 1 - y)]


HBM = pl.BlockSpec(memory_space=pltpu.HBM)
SEM = pl.BlockSpec(memory_space=pltpu.SEMAPHORE)
TOKEN = pl.BlockSpec(memory_space=pltpu.VMEM)
TOKEN_SHAPE = SDS((8, 128), F32)
PEERS = 7


def _in_hbm(a):
    return pltpu.with_memory_space_constraint(a, pltpu.HBM)


def _split_params():
    return pltpu.CompilerParams(has_side_effects=pltpu.SideEffectType.DATAFLOW_SIDE_EFFECTING)


def _gather_start(shards, name, after=()):
    nt = len(shards)
    after, after_specs = _unread(after)

    def body(*refs):
        ins, lands = refs[:nt], refs[nt:2 * nt]
        outs = refs[2 * nt + len(after):]
        sends, recvs = outs[:nt], outs[nt:2 * nt]
        x, y, c, chips = _place()
        me = 2 * x + y
        for t in range(nt):
            h = ins[t].shape[0] // 2
            mine = pl.ds(c * h, h)
            for j, (cx, cy) in enumerate(chips):
                for dc in range(2):
                    pltpu.make_async_remote_copy(src_ref=ins[t].at[mine], dst_ref=lands[t].at[me, mine], send_sem=sends[t].at[2 * j + dc],
                                                 recv_sem=recvs[t].at[2 * j + c], device_id=(cx, cy, dc), device_id_type=MESH).start()
            pltpu.make_async_remote_copy(src_ref=ins[t], dst_ref=lands[t].at[me], send_sem=sends[t].at[PEERS - 1], recv_sem=recvs[t].at[PEERS - 1],
                                         device_id=(x, y, 1 - c), device_id_type=MESH).start()
        outs[-1][...] = jnp.zeros(TOKEN_SHAPE.shape, F32)

    lands = [lax.empty((4,) + s.shape, s.dtype) for s in shards]
    out = pl.pallas_call(
        body, name=name, in_specs=[HBM] * (2 * nt) + after_specs, out_specs=[SEM] * (2 * nt) + [HBM] * (2 * nt) + [TOKEN],
        out_shape=[pltpu.SemaphoreType.DMA((PEERS,))] * (2 * nt)
        + [pltpu.HBM(s.shape, s.dtype) for s in shards] + [pltpu.HBM(l.shape, l.dtype) for l in lands] + [TOKEN_SHAPE],
        input_output_aliases={t: 2 * nt + t for t in range(2 * nt)}, compiler_params=_split_params())(
            *[_in_hbm(s) for s in shards], *[_in_hbm(l) for l in lands], *after)
    return out[:nt], out[nt:2 * nt], out[2 * nt:3 * nt], out[3 * nt:4 * nt], out[-1]


def _gather_wait(sends, recvs, shards, lands, after, name):
    nt = len(shards)

    def body(*refs):
        ins, lands_ref = refs[:nt], refs[nt:2 * nt]
        send_refs, recv_refs = refs[2 * nt:3 * nt], refs[3 * nt:4 * nt]
        x, y, c, chips = _place()
        for t in range(nt):
            h = ins[t].shape[0] // 2
            for j, (cx, cy) in enumerate(chips):
                for cs in range(2):
                    blk = lands_ref[t].at[2 * cx + cy, pl.ds(cs * h, h)]
                    pltpu.make_async_remote_copy(src_ref=blk, dst_ref=blk, send_sem=send_refs[t].at[2 * j + cs], recv_sem=recv_refs[t].at[2 * j + cs],
                                                 device_id=(cx, cy, cs), device_id_type=MESH).wait()
            blk = lands_ref[t].at[2 * x + y]
            pltpu.make_async_remote_copy(src_ref=blk, dst_ref=blk, send_sem=send_refs[t].at[PEERS - 1], recv_sem=recv_refs[t].at[PEERS - 1],
                                         device_id=(x, y, 1 - c), device_id_type=MESH).wait()

    out = pl.pallas_call(
        body, name=name, in_specs=[HBM] * (2 * nt) + [SEM] * (2 * nt) + [ANY], out_specs=[HBM] * (2 * nt),
        out_shape=[pltpu.HBM(s.shape, s.dtype) for s in shards] + [pltpu.HBM(l.shape, l.dtype) for l in lands],
        input_output_aliases={t: t for t in range(2 * nt)}, compiler_params=_split_params())(*shards, *lands, *sends, *recvs, after)
    return out[nt:]


def _gather_pieces_start(v, shard, name):
    h = shard.shape[0] // 2

    def body(v_ref, v_land, src, land, v_send, v_recv, send, recv, *rest):
        x, y, c, chips = _place()
        for j, flips in enumerate(_flips()):
            pltpu.make_async_remote_copy(src_ref=v_ref, dst_ref=v_land.at[4 * x + 2 * y + c], send_sem=v_send.at[j], recv_sem=v_recv.at[j],
                                         device_id=_flipped(x, y, c, flips), device_id_type=MESH).start()
        me = 2 * x + y
        mine = pl.ds(c * h, h)
        for j, (cx, cy) in enumerate(chips):
            pltpu.make_async_remote_copy(src_ref=src.at[mine], dst_ref=land.at[me, mine], send_sem=send.at[j], recv_sem=recv.at[j],
                                         device_id=(cx, cy, c), device_id_type=MESH).start()
        pltpu.make_async_remote_copy(src_ref=src, dst_ref=land.at[me], send_sem=send.at[len(chips)], recv_sem=recv.at[len(chips)],
                                     device_id=(x, y, 1 - c), device_id_type=MESH).start()
        rest[-1][...] = jnp.zeros(TOKEN_SHAPE.shape, F32)

    v_land = lax.empty((8,) + v.shape, v.dtype)
    land = lax.empty((4,) + shard.shape, shard.dtype)
    passed = [v, v_land, shard, land]
    out = pl.pallas_call(
        body, name=name, in_specs=[HBM] * 4, out_specs=[SEM] * 4 + [HBM] * 4 + [TOKEN],
        out_shape=[pltpu.SemaphoreType.DMA((PEERS,))] * 2 + [pltpu.SemaphoreType.DMA((4,))] * 2
        + [pltpu.HBM(a.shape, a.dtype) for a in passed] + [TOKEN_SHAPE],
        input_output_aliases={t: 4 + t for t in range(4)}, compiler_params=_split_params())(*[_in_hbm(a) for a in passed])
    return (out[0], out[1], out[4], out[5]), (out[2], out[3], out[6], out[7]), out[8]


def _gather_pieces_wait(send, recv, shard, land, after, name):
    h = shard.shape[0] // 2
    after, after_specs = _unread(after)

    def body(*refs):
        land_ref, send_ref, recv_ref = refs[1:4]
        x, y, c, chips = _place()
        for j, (cx, cy) in enumerate(chips):
            blk = land_ref.at[2 * cx + cy, pl.ds(c * h, h)]
            pltpu.make_async_remote_copy(src_ref=blk, dst_ref=blk, send_sem=send_ref.at[j], recv_sem=recv_ref.at[j],
                                         device_id=(cx, cy, c), device_id_type=MESH).wait()
        own = land_ref.at[2 * x + y]
        pltpu.make_async_remote_copy(src_ref=own, dst_ref=own, send_sem=send_ref.at[len(chips)], recv_sem=recv_ref.at[len(chips)],
                                     device_id=(x, y, 1 - c), device_id_type=MESH).wait()

    out = pl.pallas_call(
        body, name=name, in_specs=[HBM, HBM, SEM, SEM] + after_specs, out_specs=[HBM, HBM],
        out_shape=[pltpu.HBM(shard.shape, shard.dtype), pltpu.HBM(land.shape, land.dtype)],
        input_output_aliases={0: 0, 1: 1}, compiler_params=_split_params())(shard, land, send, recv, *after)
    return out[1]


def _pass_pieces(land, name):
    h = land.shape[1] // 2

    def body(_, out, send, recv):
        x, y, c, chips = _place()

        def piece(j, cc):
            cx, cy = chips[j]
            blk = out.at[2 * cx + cy, pl.ds(cc * h, h)]
            return pltpu.make_async_remote_copy(src_ref=blk, dst_ref=blk, send_sem=send.at[j], recv_sem=recv.at[j],
                                                device_id=(x, y, 1 - c), device_id_type=MESH)

        for j in range(len(chips)):
            piece(j, c).start()
        for j in range(len(chips)):
            piece(j, 1 - c).wait_recv()
        for j in range(len(chips)):
            piece(j, c).wait_send()

    return pl.pallas_call(
        body, name=name, in_specs=[ANY], out_specs=ANY, out_shape=SDS(land.shape, land.dtype), input_output_aliases={0: 0},
        scratch_shapes=[pltpu.SemaphoreType.DMA((3,))] * 2, compiler_params=pltpu.CompilerParams(has_side_effects=True))(land)


def _scatter_start(gs, name):
    nt = len(gs)

    def body(*refs):
        g_refs, lands = refs[:nt], refs[nt:2 * nt]
        sends, recvs = refs[2 * nt:3 * nt], refs[3 * nt:4 * nt]
        x, y, c, chips = _place()
        for g_ref, land, send, recv in zip(g_refs, lands, sends, recvs):
            h = land.shape[1]
            for j, (cx, cy) in enumerate(chips):
                for dc in range(2):
                    pltpu.make_async_remote_copy(src_ref=g_ref.at[2 * cx + cy, pl.ds(dc * h, h)], dst_ref=land.at[2 * j + c],
                                                 send_sem=send.at[2 * j + dc], recv_sem=recv.at[2 * j + c], device_id=(cx, cy, dc),
                                                 device_id_type=MESH).start()
            pltpu.make_async_remote_copy(src_ref=g_ref.at[2 * x + y, pl.ds((1 - c) * h, h)], dst_ref=land.at[PEERS - 1], send_sem=send.at[PEERS - 1],
                                         recv_sem=recv.at[PEERS - 1], device_id=(x, y, 1 - c), device_id_type=MESH).start()
        refs[-1][...] = jnp.zeros(TOKEN_SHAPE.shape, F32)

    lands = [lax.empty((PEERS, g.shape[1] // 2, g.shape[2]), g.dtype) for g in gs]
    out = pl.pallas_call(
        body, name=name, in_specs=[HBM] * (2 * nt), out_specs=[SEM] * (2 * nt) + [HBM] * (2 * nt) + [TOKEN],
        out_shape=[pltpu.SemaphoreType.DMA((PEERS,))] * (2 * nt) + [pltpu.HBM(a.shape, a.dtype) for a in gs + lands] + [TOKEN_SHAPE],
        input_output_aliases={t: 2 * nt + t for t in range(2 * nt)}, compiler_params=_split_params())(
            *[_in_hbm(a) for a in gs + lands])
    return [tuple(out[q * nt + t] for q in range(4)) for t in range(nt)], out[-1]


def _scatter_wait(started, after, name):
    nt = len(started)

    def body(*refs):
        lands = refs[nt:2 * nt]
        sends, recvs = refs[2 * nt:3 * nt], refs[3 * nt:4 * nt]
        x, y, c, chips = _place()
        peers = [(cx, cy, dc) for cx, cy in chips for dc in range(2)] + [(x, y, 1 - c)]
        for t in range(nt):
            for k, peer in enumerate(peers):
                blk = lands[t].at[k]
                pltpu.make_async_remote_copy(src_ref=blk, dst_ref=blk, send_sem=sends[t].at[k], recv_sem=recvs[t].at[k],
                                             device_id=peer, device_id_type=MESH).wait()

    gs, lands = [s[2] for s in started], [s[3] for s in started]
    after, after_specs = _unread(after)
    out = pl.pallas_call(
        body, name=name, in_specs=[HBM] * (2 * nt) + [SEM] * (2 * nt) + after_specs, out_specs=[HBM] * (2 * nt),
        out_shape=[pltpu.HBM(a.shape, a.dtype) for a in gs + lands],
        input_output_aliases={t: t for t in range(2 * nt)}, compiler_params=_split_params())(
            *gs, *lands, *[s[0] for s in started], *[s[1] for s in started], *after)
    return out[:nt], out[nt:]


def _join_start(bufs, name, after=()):
    nt = len(bufs)
    after, after_specs = _unread(after)

    def body(*refs):
        send, recv = refs[nt + len(after):nt + len(after) + 2]
        x, y, c, _ = _place()
        for t in range(nt):
            pltpu.make_async_remote_copy(src_ref=refs[t].at[c], dst_ref=refs[t].at[c], send_sem=send.at[t], recv_sem=recv.at[t],
                                         device_id=(x, y, 1 - c), device_id_type=MESH).start()
        refs[-1][...] = jnp.zeros(TOKEN_SHAPE.shape, F32)

    out = pl.pallas_call(
        body, name=name, in_specs=[HBM] * nt + after_specs, out_specs=[SEM, SEM] + [HBM] * nt + [TOKEN],
        out_shape=[pltpu.SemaphoreType.DMA((nt,))] * 2 + [pltpu.HBM(b.shape, b.dtype) for b in bufs] + [TOKEN_SHAPE],
        input_output_aliases={t: 2 + t for t in range(nt)}, compiler_params=_split_params())(*[_in_hbm(b) for b in bufs], *after)
    return out[0], out[1], out[2:2 + nt], out[-1]


def _join_wait(send, recv, bufs, after, name):
    nt = len(bufs)
    after, after_specs = _unread(after)

    def body(*refs):
        send_ref, recv_ref = refs[nt:nt + 2]
        x, y, c, _ = _place()
        for t in range(nt):
            theirs = refs[t].at[1 - c]
            pltpu.make_async_remote_copy(src_ref=theirs, dst_ref=theirs, send_sem=send_ref.at[t], recv_sem=recv_ref.at[t],
                                         device_id=(x, y, 1 - c), device_id_type=MESH).wait()

    return pl.pallas_call(
        body, name=name, in_specs=[HBM] * nt + [SEM, SEM] + after_specs, out_specs=[HBM] * nt,
        out_shape=[pltpu.HBM(b.shape, b.dtype) for b in bufs],
        input_output_aliases={t: t for t in range(nt)}, compiler_params=_split_params())(*bufs, send, recv, *after)


def _flips():
    return [(dx, dy, dc) for dx in range(2) for dy in range(2) for dc in range(2) if (dx, dy, dc) != (0, 0, 0)]


def _flipped(x, y, c, flips):
    dx, dy, dc = flips
    return (1 - x if dx else x, 1 - y if dy else y, 1 - c if dc else c)


def _small_start(v, name, after=()):
    after, after_specs = _unread(after)

    def body(v_ref, land, *rest):
        send, recv = rest[len(after):len(after) + 2]
        x, y, c, _ = _place()
        for j, flips in enumerate(_flips()):
            pltpu.make_async_remote_copy(src_ref=v_ref, dst_ref=land.at[4 * x + 2 * y + c], send_sem=send.at[j], recv_sem=recv.at[j],
                                         device_id=_flipped(x, y, c, flips), device_id_type=MESH).start()
        rest[-1][...] = jnp.zeros(TOKEN_SHAPE.shape, F32)

    land = lax.empty((8,) + v.shape, v.dtype)
    return pl.pallas_call(
        body, name=name, in_specs=[HBM, HBM] + after_specs, out_specs=[SEM, SEM, HBM, HBM, TOKEN],
        out_shape=[pltpu.SemaphoreType.DMA((PEERS,)), pltpu.SemaphoreType.DMA((PEERS,)), pltpu.HBM(v.shape, v.dtype),
                   pltpu.HBM(land.shape, land.dtype), TOKEN_SHAPE],
        input_output_aliases={0: 2, 1: 3}, compiler_params=_split_params())(_in_hbm(v), _in_hbm(land), *after)


def _small_wait(send, recv, v, land, after, name):
    after, after_specs = _unread(after)

    def body(v_ref, land_ref, send_ref, recv_ref, *rest):
        x, y, c, _ = _place()
        for j, flips in enumerate(_flips()):
            px, py, pc = _flipped(x, y, c, flips)
            blk = land_ref.at[4 * px + 2 * py + pc]
            pltpu.make_async_remote_copy(src_ref=blk, dst_ref=blk, send_sem=send_ref.at[j], recv_sem=recv_ref.at[j],
                                         device_id=(px, py, pc), device_id_type=MESH).wait()

    return pl.pallas_call(
        body, name=name, in_specs=[HBM, HBM, SEM, SEM] + after_specs, out_specs=[HBM, HBM],
        out_shape=[pltpu.HBM(v.shape, v.dtype), pltpu.HBM(land.shape, land.dtype)],
        input_output_aliases={0: 0, 1: 1}, compiler_params=_split_params())(v, land, send, recv, *after)


def _sum_small(v, land, name):
    def body(v_ref, land_ref, o_ref):
        x, y, c, _ = _place()
        me = 4 * x + 2 * y + c
        acc = jnp.where(me == 0, v_ref[...], land_ref[0])
        for d in range(1, 8):
            acc = acc + jnp.where(me == d, v_ref[...], land_ref[d])
        o_ref[...] = acc

    vm = pl.BlockSpec(memory_space=pltpu.VMEM)
    return pl.pallas_call(body, name=name, in_specs=[vm, vm], out_specs=vm, out_shape=SDS(v.shape, F32))(v, land)


SUM_STEPS = 2


def _sum_devices(gs, lands, me, core, name):
    nt = len(gs)

    def body(ix_ref, *refs):
        for own_ref, land_ref, o_ref in zip(refs[:nt], refs[nt:2 * nt], refs[2 * nt:]):
            acc = own_ref[0].astype(F32)
            for j in range(land_ref.shape[0]):
                acc = acc + land_ref[j].astype(F32)
            o_ref[0] = acc

    tiles = [(land.shape[1] // SUM_STEPS, land.shape[2]) for land in lands]
    assert all(land.shape[1] == tr * SUM_STEPS and tr % ROWS_PER_16BIT_TILE == 0 for land, (tr, _) in zip(lands, tiles)), name
    grid_spec = pltpu.PrefetchScalarGridSpec(
        num_scalar_prefetch=1, grid=(SUM_STEPS,),
        in_specs=[pl.BlockSpec((1, tr, c), lambda i, ix: (ix[0], ix[1] * SUM_STEPS + i, 0)) for tr, c in tiles]
        + [pl.BlockSpec((land.shape[0], tr, c), lambda i, ix: (0, i, 0)) for land, (tr, c) in zip(lands, tiles)],
        out_specs=[pl.BlockSpec((1, tr, c), lambda i, ix: (ix[1], i, 0)) for tr, c in tiles])
    return pl.pallas_call(body, name=name, grid_spec=grid_spec, out_shape=[SDS((2,) + land.shape[1:], F32) for land in lands],
                          compiler_params=_cp("parallel"))(jnp.stack([me, core]), *gs, *lands)


def _pack_small(parts):
    flat = jnp.concatenate([p.reshape(-1) for p in parts])
    total = flat.shape[0]
    rows = -(-total // 1024) * 8
    return jnp.pad(flat, (0, rows * 128 - total)).reshape(rows, 128)


def _pack_lanes(parts):
    cols, spans, off = [], [], 0
    for p in parts:
        size = int(np.prod(p.shape))
        width = -(-size // 128) * 128
        cols.append(jnp.pad(p.reshape(1, size), ((0, 0), (0, width - size))))
        spans.append((off, size))
        off += width
    return jnp.concatenate(cols, axis=1), spans


def _unpack_small(packed, shapes):
    flat = packed.reshape(-1)
    out, off = [], 0
    for s in shapes:
        size = int(np.prod(s))
        out.append(flat[off:off + size].reshape(s))
        off += size
    return out


def _local_step(x, mem, target, w_in_t, first_after, mid_weights, ffn_weights, on_grad, gains, conv_w, conv_b, hg_lb):
    n = x.shape[0]
    cos, sin = _rope_tables(n)
    seg = _hg_segments()
    gp, gn = _hg_pair_sums()
    masks = _hg_level_masks()
    gq2 = jnp.tile(gains["q_norm_g"], (1, 2))
    gk2 = jnp.tile(gains["k_norm_g"], (1, 2))
    a0 = hg_lb[:, 0:1, :]
    a1 = hg_lb[:, 1:2, :]

    p, h1 = _norm_mm(x, gains["pre_mix_g"], w_in_t, F32, TOKEN_TILE, 1664, "in_proj", after=(first_after,), w_turned=True)
    qr, kr = _qk_prep(p, gq2, gk2, cos, sin, "qk_prep")
    heads = lambda a: a.reshape(n, ATT_KV_HEADS, ATT_HEAD_DIM).transpose(1, 0, 2)
    kh = heads(kr)
    vh = heads(p[:, OFF_AV:OFF_AV + ATT_KV_DIM].astype(MXU_DTYPE))
    att = _attn_fwd(qr, kh, vh, "attn_fwd")
    o2, s0, hg_a, hg_e, hg_kept = _hgrn_fwd(p, a0, a1, seg, masks, "hgrn_fwd")
    rec = _hg_post(o2, p, gains["hg_out_norm_g"], "hg_post")
    cat = jnp.concatenate([att, rec], axis=1)
    w_out, w_xq, w_xkv, w_xo = mid_weights(cat)
    mixed, x1 = _mm_resid_norm(cat, w_out, x, gains["post_mix_g"], 512, "out_proj_resid")
    xq, h2 = _norm_mm(x1, gains["pre_x_g"], w_xq, MXU_DTYPE, TOKEN_TILE, 1024, "xq_proj")
    kv, mn = _norm_mm(mem, gains["mem_norm_g"], w_xkv, MXU_DTYPE, 256, 2048, "xkv_proj")
    ox = _xattn_fwd(xq, kv, "xattn_fwd")
    xo, x2 = _mm_resid_norm(ox, w_xo, x1, gains["post_x_g"], 512, "xo_proj_resid")
    w_up = ffn_weights("w_up", x2)
    u, h3 = _norm_mm(x2, gains["pre_ffn_g"], w_up, F32, TOKEN_TILE, 1408, "up_proj")
    act = _conv_gate(u, conv_w, conv_b, "conv_gate")
    w_down = ffn_weights("w_down", act)
    dn, d3, loss = _mm_resid_norm(act, w_down, x2, gains["post_ffn_g"], 512, "down_proj_resid_loss", target=target)

    gs = {}
    d_act, d_dn, gs["post_ffn_g"] = _norm_bwd_mm(dn, gains["post_ffn_g"], d3, w_down, F32, 512, 1408, "ffn_post_bwd_down_dx")
    tok = on_grad("w_down", _mm(act, d_dn, "tn", WIRE_DTYPE, 1408, 1024, "down_dw"))
    du_g, du_v, dcw_g, dcw_v, dcb_g, dcb_v = _conv_gate_bwd(u, conv_w, conv_b, d_act, "conv_gate_bwd", after=(tok,))
    gs["conv_w"] = jnp.concatenate([dcw_g, dcw_v], axis=1)
    gs["conv_b"] = jnp.concatenate([dcb_g, dcb_v], axis=1)
    ff_shard = w_up.shape[2]
    g_up = _dw_by_owner(h3, du_g, ff_shard, 0, None, 512, "up_dw_gate")
    tok = on_grad("w_up", _dw_by_owner(h3, du_v, ff_shard, 2, g_up, 512, "up_dw_value"))
    d2, gs["pre_ffn_g"] = _dx_norm_bwd([(du_g, 0), (du_g, 1), (du_v, 0), (du_v, 1)], w_up, x2, gains["pre_ffn_g"], d3, 512,
                                       "up_dx_pre_bwd", after=(tok,))
    d_ox, d_xo, gs["post_x_g"] = _norm_bwd_mm(xo, gains["post_x_g"], d2, w_xo, MXU_DTYPE, 512, 1024, "x_post_bwd_xo_dx")
    tok = on_grad("w_xo", _mm(ox, d_xo, "tn", WIRE_DTYPE, 512, 1024, "xo_dw"))
    d_xq, d_k, d_v = _xattn_bwd(xq, kv, d_ox, "xattn_bwd", after=(tok,))
    d_kv = jnp.concatenate([d_k, d_v], axis=1).astype(MXU_DTYPE)
    tok = on_grad("w_xq", _mm(h2, d_xq, "tn", WIRE_DTYPE, 512, 1024, "xq_dw"))
    tok_kv = on_grad("w_xkv", _dw_by_owner(mn, d_kv, w_xkv.shape[2], 0, None, 512, "xkv_dw"))
    d1, gs["pre_x_g"] = _dx_norm_bwd([(d_xq, 0)], w_xq[None], x1, gains["pre_x_g"], d2, 512, "xq_dx_pre_bwd", after=(tok, tok_kv))
    d_mn = _mm_nt_parts([(d_kv, s) for s in range(4)], w_xkv, F32, 256, 1024, "xkv_dx")
    _, gs["mem_norm_g"] = _norm_bwd(mem, gains["mem_norm_g"], d_mn, None, MXU_DTYPE, "mem_norm_bwd")
    d_cat, d_mixed, gs["post_mix_g"] = _norm_bwd_mm(mixed, gains["post_mix_g"], d1, w_out, MXU_DTYPE, 512, 1024, "mix_post_bwd_out_dx")
    tok = on_grad("w_out", _mm(cat, d_mixed, "tn", WIRE_DTYPE, 512, 1024, "out_dw"))
    d_o, d_hg, dg_hg = _hg_post_bwd(o2, p, gains["hg_out_norm_g"], d_cat, "hg_post_bwd", after=(tok,))
    gs["hg_out_norm_g"] = dg_hg.reshape(HG_HEADS, HG_HEAD_DIM).sum(axis=0, keepdims=True)
    dhq2, dz2, dhv2, dlb = _hgrn_bwd(p, a0, a1, masks, gp, gn, d_o, s0, hg_a, hg_e, hg_kept, "hgrn_bwd")
    lb = jax.nn.sigmoid(a0 - a1)
    da0 = dlb * lb * (1.0 - lb)
    gs["hg_lb"] = jnp.concatenate([da0, -da0], axis=1)
    d_qr, d_kh, d_vh = _attn_bwd(qr, kh, vh, cat, d_cat, "attn_bwd")
    unheads = lambda a: a.transpose(2, 0, 1).reshape(n, ATT_KV_DIM)
    d_aq, d_ak, dgq, dgk = _qk_prep_bwd(p, gq2, gk2, cos, sin, d_qr, unheads(d_kh), "qk_prep_bwd")
    gs["q_norm_g"] = dgq.reshape(ATT_HEADS, ATT_HEAD_DIM).sum(axis=0, keepdims=True)
    gs["k_norm_g"] = dgk.reshape(ATT_KV_HEADS, ATT_HEAD_DIM).sum(axis=0, keepdims=True)
    d_p = jnp.concatenate([d_aq, d_ak, unheads(d_vh).astype(MXU_DTYPE), (dhq2[0] + dhq2[1]).astype(MXU_DTYPE),
                           dz2[0].astype(MXU_DTYPE), dz2[1].astype(MXU_DTYPE), (dhv2[0] + dhv2[1]).astype(MXU_DTYPE), d_hg], axis=1)
    tok = on_grad("w_in", _mm(d_p, h1, "tn", WIRE_DTYPE, 1664, 1024, "in_dw"))
    grad_x, gs["pre_mix_g"] = _dx_norm_bwd([(d_p, 0)], w_in_t[None], x, gains["pre_mix_g"], d1, 512, "in_dx_pre_bwd", after=(tok,),
                                           b_turned=True)
    return loss, grad_x, gs


MATS = ("w_in", "w_out", "w_xq", "w_xkv", "w_xo", "w_up", "w_down")
GAINS = ("pre_mix_g", "q_norm_g", "k_norm_g", "hg_out_norm_g", "post_mix_g", "pre_x_g", "mem_norm_g", "post_x_g", "pre_ffn_g", "post_ffn_g")
WEIGHTS = ('pre_mix_g', 'w_in', 'q_norm_g', 'k_norm_g', 'hg_lb', 'hg_out_norm_g', 'w_out', 'post_mix_g', 'pre_x_g', 'mem_norm_g', 'w_xq',
           'w_xkv', 'w_xo', 'post_x_g', 'pre_ffn_g', 'w_up', 'conv_w', 'conv_b', 'w_down', 'post_ffn_g')


def kernel(x, mem, pre_mix_g, w_in, q_norm_g, k_norm_g, hg_lb, hg_out_norm_g, w_out, post_mix_g, pre_x_g, mem_norm_g, w_xq, w_xkv, w_xo, post_x_g, pre_ffn_g, w_up, conv_w, conv_b, w_down, post_ffn_g, loss_target, m_pre_mix_g, m_w_in, m_q_norm_g, m_k_norm_g, m_hg_lb, m_hg_out_norm_g, m_w_out, m_post_mix_g, m_pre_x_g, m_mem_norm_g, m_w_xq, m_w_xkv, m_w_xo, m_post_x_g, m_pre_ffn_g, m_w_up, m_conv_w, m_conv_b, m_w_down, m_post_ffn_g, v_pre_mix_g, v_w_in, v_q_norm_g, v_k_norm_g, v_hg_lb, v_hg_out_norm_g, v_w_out, v_post_mix_g, v_pre_x_g, v_mem_norm_g, v_w_xq, v_w_xkv, v_w_xo, v_post_x_g, v_pre_ffn_g, v_w_up, v_conv_w, v_conv_b, v_w_down, v_post_ffn_g):
    args = dict(locals())
    w = {k: args[k] for k in WEIGHTS}
    m = {k: args["m_" + k] for k in WEIGHTS}
    v = {k: args["v_" + k] for k in WEIGHTS}
    chip = 2 * lax.axis_index("x") + lax.axis_index("y")
    core = lax.axis_index("c")

    turned = ("w_in",)
    shards = {k: (jnp.swapaxes(w[k], 1, 2) if k in turned else w[k])[0].astype(WIRE_DTYPE) for k in MATS}

    def whole(k, g):
        return g if k in ("w_xkv", "w_up") else g.reshape(-1, g.shape[-1])

    mid_names, ffn_names = ("w_out", "w_xq", "w_xkv", "w_xo"), ("w_up", "w_down")
    small, w_in_pieces, token = _gather_pieces_start(_pack_small([w["conv_w"][0], w["hg_lb"]]), shards["w_in"], "gather_first_start")
    mid = _gather_start([shards[k] for k in mid_names], "gather_mid_start", after=(token,))
    ffn = _gather_start([shards[k] for k in ffn_names], "gather_ffn_start", after=(mid[4],))
    w_in_t = whole("w_in", _pass_pieces(_gather_pieces_wait(*w_in_pieces, (ffn[4],), "gather_w_in_wait"), "gather_w_in_pass"))
    mine, others = _small_wait(*small, (w_in_t,), "gather_small_wait")
    small_in = lax.dynamic_update_slice_in_dim(others, mine[None], 2 * chip + core, axis=0)

    def mid_weights(after):
        return [whole(k, g) for k, g in zip(mid_names, _gather_wait(*mid[:4], after, "gather_mid_wait"))]

    def ffn_weights(k, after):
        t = ffn_names.index(k)
        return whole(k, _gather_wait(*[part[t:t + 1] for part in ffn[:4]], after, "gather_wait_" + k)[0])

    cw_parts, lb_parts = [], []
    for s in range(4):
        cw_s, lb_s = _unpack_small(small_in[2 * s], [w["conv_w"][0].shape, w["hg_lb"].shape])
        cw_parts.append(cw_s)
        lb_parts.append(lb_s)
    conv_w_full = jnp.concatenate(cw_parts, axis=1)
    hg_lb_full = jnp.concatenate(lb_parts, axis=2)

    started, held = {}, {}
    leaves_with_next = ("w_down", "w_xo", "w_xq")

    def on_grad(k, g):
        if g.ndim == 2:
            g = g.reshape(4, g.shape[0] // 4, g.shape[1])
        held[k] = g
        if k in leaves_with_next:
            return None
        names = tuple(held)
        per_tensor, token = _scatter_start([held.pop(n) for n in names], "grad_start_" + k)
        started.update(zip(names, per_tensor))
        return token

    gains = {k: w[k] for k in GAINS}
    loss_part, grad_x, gs = _local_step(x[0], mem[0], loss_target[0], w_in_t, ffn[4], mid_weights, ffn_weights, on_grad, gains,
                                        conv_w_full, w["conv_b"], hg_lb_full)
    gs["loss"] = loss_part

    grads, delta, new_m, new_v = {}, {}, {}, {}

    def sum_and_send(names, after, tag):
        sent, landed = _scatter_wait([started[k] for k in names], after, "grad_wait_" + tag)
        return _join_start(_sum_devices(sent, landed, chip, core, "grad_sum_" + tag), "grad_join_start_" + tag)

    def joined(names, join, after, tag):
        for k, r in zip(names, _join_wait(*join[:3], after, "grad_join_wait_" + tag)):
            grads[k] = r.reshape(1, -1, r.shape[-1])

    def adamw(names, tag):
        params, backs = [], []
        for k in names:
            shape = w[k].shape
            keep = len(shape) == 3
            if k in turned:
                view, back = (lambda a: jnp.swapaxes(a, 1, 2)), (lambda a: jnp.swapaxes(a, 1, 2))
                g = grads[k]
            else:
                view = (lambda a, shape=shape: a.reshape(shape)) if keep else (lambda a, shape=shape: a.reshape(-1, shape[-1]))
                back = lambda a, shape=shape: a.reshape(shape)
                g = view(grads[k])
            params.append((view(w[k]), g, view(m[k]), view(v[k])))
            backs.append(back)
        for k, back, (d, mo, vo, go) in zip(names, backs, _adamw(params, "adamw_" + tag)):
            delta[k], new_m[k], new_v[k], grads[k] = back(d), back(mo), back(vo), back(go)

    small_names = GAINS + ("conv_b", "conv_w", "hg_lb")
    packed, spans = _pack_lanes([gs[k] for k in small_names + ("loss",)])
    spans = dict(zip(small_names + ("loss",), spans))
    small = _small_start(packed, "reduce_small_start", after=(grad_x,))

    ffn_join = sum_and_send(ffn_names, (grad_x, small[4]), "ffn")
    mid_join = sum_and_send(mid_names, (ffn_join[3],), "mid")
    joined(ffn_names, ffn_join, (mid_join[3],), "ffn")
    adamw(ffn_names, "ffn")
    joined(mid_names, mid_join, tuple(new_v[k] for k in ffn_names), "mid")
    adamw(mid_names, "mid")
    early = mid_names + ffn_names
    w_in_join = sum_and_send(("w_in",), tuple(new_v[k] for k in early), "w_in")

    mine, others = _small_wait(*small[:4], (w_in_join[3],), "reduce_small_wait")
    reduced_small = _sum_small(mine, others, "reduce_small_sum")
    summed = lambda k: reduced_small[0, spans[k][0]:spans[k][0] + spans[k][1]].reshape(gs[k].shape)
    loss = summed("loss")[0, 0]
    ncw = w["conv_w"].shape[2]
    grads["conv_w"] = lax.dynamic_slice_in_dim(summed("conv_w"), chip * ncw, ncw, axis=1)[None]
    nlb = w["hg_lb"].shape[2]
    grads["hg_lb"] = lax.dynamic_slice_in_dim(summed("hg_lb"), chip * nlb, nlb, axis=2)
    adamw(("conv_w", "hg_lb"), "sharded_small")
    replicated = GAINS + ("conv_b",)
    outs = _adamw_rows(reduced_small, [spans[k] for k in replicated], [(w[k], m[k], v[k]) for k in replicated], "adamw_replicated")
    for k, (d, mo, vo, go) in zip(replicated, outs):
        delta[k], new_m[k], new_v[k], grads[k] = d, mo, vo, go

    joined(("w_in",), w_in_join, tuple(new_v[k] for k in small_names), "w_in")
    adamw(("w_in",), "w_in")
    return (loss, grad_x[None], *[grads[k] for k in WEIGHTS], *[delta[k] for k in WEIGHTS],
            *[new_m[k] for k in WEIGHTS], *[new_v[k] for k in WEIGHTS])
```

```python
import numpy as np
import jax
import jax.numpy as jnp
from jax import lax
from jax.experimental import pallas as pl
from jax.experimental.pallas import tpu as pltpu

F32 = jnp.float32
MXU_DTYPE = jnp.bfloat16
WIRE_DTYPE = jnp.bfloat16
VMEM_LIMIT_BYTES = 56 * 1024 * 1024
ROWS_PER_16BIT_TILE = 16
ELEMENTWISE_ROWS = 256
EPS = 1e-6
MESH = pl.DeviceIdType.MESH

GRID_W = 64
ATT_HEADS, ATT_KV_HEADS, ATT_HEAD_DIM = 8, 2, 64
ATT_GROUP = ATT_HEADS // ATT_KV_HEADS
ATT_Q_DIM, ATT_KV_DIM = 512, 128
ROPE_THETA = 10000.0
HG_HEADS, HG_HEAD_DIM, HG_DIM = 4, 128, 512
HG_CHUNK = 128
HG_LEVELS = 7
HG_PAIR = 2 * HG_HEAD_DIM
HG_KEPT = 7
X_HEADS, X_HEAD_DIM = 4, 256
D_FF = 2816
FF_COLS = 256
FF_BLOCKS = D_FF // FF_COLS
OFF_AK, OFF_AV, OFF_HQ, OFF_ZF, OFF_ZB, OFF_HI, OFF_HG = 512, 640, 768, 1280, 1792, 2304, 2816

ADAM_LR, ADAM_B1, ADAM_B2, ADAM_EPS, ADAM_WD, ADAM_STEP = 0.001, 0.9, 0.999, 1e-08, 0.01, 10

SDS = jax.ShapeDtypeStruct


def _cp(*sem):
    return pltpu.CompilerParams(dimension_semantics=sem, vmem_limit_bytes=VMEM_LIMIT_BYTES)


def _dot(a, b, form="nn"):
    dims = {"nn": (((1,), (0,)), ((), ())), "nt": (((1,), (1,)), ((), ())), "tn": (((0,), (0,)), ((), ()))}[form]
    return lax.dot_general(a.astype(MXU_DTYPE), b.astype(MXU_DTYPE), dims, preferred_element_type=F32)


def _sigmoid(x):
    return 1.0 / (1.0 + jnp.exp(-x))


def _rstd(x):
    return lax.rsqrt(jnp.mean(x * x, axis=-1, keepdims=True) + EPS)


def _rms_bwd(x, g, dy):
    r = _rstd(x)
    xh = x * r
    dn = dy * g
    dx = r * (dn - xh * jnp.mean(dn * xh, axis=-1, keepdims=True))
    return dx, jnp.sum(dy * xh, axis=0, keepdims=True)


def _unread(after):
    after = tuple(a for a in after if a is not None)
    return after, [pl.BlockSpec(memory_space=pl.ANY)] * len(after)


def _mm(a, b, form, out_dtype, tm, tn, name, after=()):
    after, after_specs = _unread(after)
    if form == "nn":
        (m, k), n = a.shape, b.shape[1]
    elif form == "nt":
        (m, k), n = a.shape, b.shape[0]
    else:
        (k, m), n = a.shape, b.shape[1]
    tm, tn = min(tm, m), min(tn, n)
    assert m % tm == 0 and n % tn == 0, (name, m, n, tm, tn)

    def body(a_ref, b_ref, *rest):
        o_ref = rest[-1]
        o_ref[...] = _dot(a_ref[...], b_ref[...], form).astype(o_ref.dtype)

    a_spec = pl.BlockSpec((k, tm), lambda i, j: (0, i)) if form == "tn" else pl.BlockSpec((tm, k), lambda i, j: (i, 0))
    b_spec = pl.BlockSpec((tn, k), lambda i, j: (j, 0)) if form == "nt" else pl.BlockSpec((k, tn), lambda i, j: (0, j))
    return pl.pallas_call(
        body, name=name, grid=(m // tm, n // tn), in_specs=[a_spec, b_spec] + after_specs,
        out_specs=pl.BlockSpec((tm, tn), lambda i, j: (i, j)), out_shape=SDS((m, n), out_dtype),
        compiler_params=_cp("parallel", "parallel"))(a, b, *after)


def _mm_nt_parts(a_parts, b, out_dtype, tm, tn, name, after=()):
    after, after_specs = _unread(after)
    parts, n, p = b.shape
    m = a_parts[0][0].shape[0]
    tm, tn = min(tm, m), min(tn, n)
    assert m % tm == 0 and n % tn == 0 and len(a_parts) == parts, (name, m, b.shape)

    def body(*refs):
        o_ref = refs[-1]
        acc = _dot(refs[0][...], refs[parts][0], "nt")
        for s in range(1, parts):
            acc = acc + _dot(refs[s][...], refs[parts + s][0], "nt")
        o_ref[...] = acc.astype(o_ref.dtype)

    a_specs = [pl.BlockSpec((tm, p), lambda i, j, cb=cb: (i, cb)) for _, cb in a_parts]
    b_specs = [pl.BlockSpec((1, tn, p), lambda i, j, s=s: (s, j, 0)) for s in range(parts)]
    return pl.pallas_call(
        body, name=name, grid=(m // tm, n // tn), in_specs=a_specs + b_specs + after_specs,
        out_specs=pl.BlockSpec((tm, tn), lambda i, j: (i, j)), out_shape=SDS((m, n), out_dtype),
        compiler_params=_cp("parallel", "parallel"))(*[arr for arr, _ in a_parts], *([b] * parts), *after)


def _norm_bwd_mm(y, g, d, w, out_dtype, tm, tn, name):
    n, dm = y.shape
    nn = w.shape[0]
    tm, tn = min(tm, n), min(tn, nn)
    assert n % tm == 0 and nn % tn == 0 and w.shape[1] == dm, (name, y.shape, w.shape)

    def body(y_ref, g_ref, d_ref, w_ref, dx_ref, dy_ref, dg_ref, dys):
        i, j = pl.program_id(0), pl.program_id(1)

        @pl.when(jnp.logical_and(i == 0, j == 0))
        def _():
            dg_ref[...] = jnp.zeros_like(dg_ref)

        @pl.when(j == 0)
        def _():
            dy, dg = _rms_bwd(y_ref[...], g_ref[...], d_ref[...])
            dy = dy.astype(MXU_DTYPE)
            dys[...] = dy
            dy_ref[...] = dy
            dg_ref[...] += dg

        dx_ref[...] = _dot(dys[...], w_ref[...], "nt").astype(dx_ref.dtype)

    row = pl.BlockSpec((tm, dm), lambda i, j: (i, 0))
    vec = pl.BlockSpec((1, dm), lambda i, j: (0, 0))
    return pl.pallas_call(
        body, name=name, grid=(n // tm, nn // tn), in_specs=[row, vec, row, pl.BlockSpec((tn, dm), lambda i, j: (j, 0))],
        out_specs=[pl.BlockSpec((tm, tn), lambda i, j: (i, j)), row, vec],
        out_shape=[SDS((n, nn), out_dtype), SDS((n, dm), MXU_DTYPE), SDS((1, dm), F32)],
        scratch_shapes=[pltpu.VMEM((tm, dm), MXU_DTYPE)],
        compiler_params=_cp("arbitrary", "arbitrary"))(y, g, d, w)


def _mm_resid_norm(a, b, x, g, tm, name, target=None):
    n, k = a.shape
    d = b.shape[1]
    tm = min(tm, n)
    assert n % tm == 0 and x.shape == (n, d), (name, a.shape, b.shape)
    with_loss = target is not None

    def body(a_ref, b_ref, x_ref, g_ref, *rest):
        y = _dot(a_ref[...], b_ref[...])
        out = x_ref[...] + y * _rstd(y) * g_ref[...]
        if not with_loss:
            y_ref, o_ref = rest
            y_ref[...] = y
            o_ref[...] = out
            return
        t_ref, y_ref, d_ref, l_ref = rest
        y_ref[...] = y
        diff = out - t_ref[...]
        d_ref[...] = diff * (1.0 / d)

        @pl.when(pl.program_id(0) == 0)
        def _():
            l_ref[...] = jnp.zeros_like(l_ref)

        l_ref[...] += 0.5 * jnp.sum(jnp.mean(diff * diff, axis=-1, keepdims=True), axis=0, keepdims=True)

    row = pl.BlockSpec((tm, d), lambda i: (i, 0))
    ins = [pl.BlockSpec((tm, k), lambda i: (i, 0)), pl.BlockSpec((k, d), lambda i: (0, 0)), row, pl.BlockSpec((1, d), lambda i: (0, 0))]
    out = SDS((n, d), F32)
    if with_loss:
        return pl.pallas_call(body, name=name, grid=(n // tm,), in_specs=ins + [row], out_specs=[row, row, pl.BlockSpec((1, 1), lambda i: (0, 0))],
                              out_shape=[out, out, SDS((1, 1), F32)], compiler_params=_cp("arbitrary"))(a, b, x, g, target)
    return pl.pallas_call(body, name=name, grid=(n // tm,), in_specs=ins, out_specs=[row, row], out_shape=[out, out],
                          compiler_params=_cp("parallel"))(a, b, x, g)


def _dx_norm_bwd(a_parts, b, x, g, res, tm, name, after=(), b_turned=False):
    after, after_specs = _unread(after)
    parts, d, p = (b.shape[0], b.shape[2], b.shape[1]) if b_turned else b.shape
    form = "nn" if b_turned else "nt"
    n = x.shape[0]
    tm = min(tm, n)
    assert n % tm == 0 and len(a_parts) == parts and x.shape[1] == d, (name, x.shape, b.shape)

    def body(*refs):
        x_ref, g_ref, res_ref = refs[2 * parts:2 * parts + 3]
        dx_ref, dg_ref = refs[-2:]
        dh = _dot(refs[0][...], refs[parts][0], form)
        for s in range(1, parts):
            dh = dh + _dot(refs[s][...], refs[parts + s][0], form)
        dx, dg = _rms_bwd(x_ref[...], g_ref[...], dh)
        dx_ref[...] = dx + res_ref[...]

        @pl.when(pl.program_id(0) == 0)
        def _():
            dg_ref[...] = jnp.zeros_like(dg_ref)

        dg_ref[...] += dg

    a_specs = [pl.BlockSpec((tm, p), lambda i, cb=cb: (i, cb)) for _, cb in a_parts]
    b_specs = [pl.BlockSpec((1,) + b.shape[1:], lambda i, s=s: (s, 0, 0)) for s in range(parts)]
    row = pl.BlockSpec((tm, d), lambda i: (i, 0))
    vec = pl.BlockSpec((1, d), lambda i: (0, 0))
    return pl.pallas_call(
        body, name=name, grid=(n // tm,), in_specs=a_specs + b_specs + [row, vec, row] + after_specs,
        out_specs=[row, vec], out_shape=[SDS((n, d), F32), SDS((1, d), F32)],
        compiler_params=_cp("arbitrary"))(*[arr for arr, _ in a_parts], *([b] * parts), x, g, res, *after)


def _dw_by_owner(a, b, tn, first, into, tm, name):
    k, m = a.shape
    cnt = b.shape[1] // tn
    tm = min(tm, m)
    assert m % tm == 0 and b.shape[1] == cnt * tn and first + cnt <= 4, (name, a.shape, b.shape)

    def body(a_ref, b_ref, *rest):
        rest[-1][0] = _dot(a_ref[...], b_ref[...], "tn").astype(rest[-1].dtype)

    extra = [] if into is None else [into]
    return pl.pallas_call(
        body, name=name, grid=(m // tm, cnt),
        in_specs=[pl.BlockSpec((k, tm), lambda i, j: (0, i)), pl.BlockSpec((k, tn), lambda i, j: (0, j))] + [pl.BlockSpec(memory_space=pl.ANY)] * len(extra),
        out_specs=pl.BlockSpec((1, tm, tn), lambda i, j: (first + j, i, 0)), out_shape=SDS((4, m, tn), WIRE_DTYPE),
        input_output_aliases={2: 0} if extra else {},
        compiler_params=_cp("parallel", "parallel"))(a, b, *extra)


def _norm_mm(x, g, w, out_dtype, tm, tn, name, after=(), w_turned=False):
    after, after_specs = _unread(after)
    m, d = x.shape
    sharded = w.ndim == 3
    n = w.shape[0] if w_turned else w.shape[-1] * (w.shape[0] if sharded else 1)
    tm, tn = min(tm, m), (w.shape[-1] if sharded else min(tn, n))
    assert m % tm == 0 and n % tn == 0 and not (sharded and w_turned), (name, m, n, tm, tn)

    def body(x_ref, g_ref, w_ref, *rest):
        o_ref, h_ref, hs = rest[-3:]

        @pl.when(pl.program_id(1) == 0)
        def _():
            xv = x_ref[...]
            h = (xv * _rstd(xv) * g_ref[...]).astype(MXU_DTYPE)
            hs[...] = h
            h_ref[...] = h

        o_ref[...] = _dot(hs[...], w_ref[0] if sharded else w_ref[...], "nt" if w_turned else "nn").astype(o_ref.dtype)

    if w_turned:
        w_spec = pl.BlockSpec((tn, d), lambda i, j: (j, 0))
    else:
        w_spec = pl.BlockSpec((1, d, tn), lambda i, j: (j, 0, 0)) if sharded else pl.BlockSpec((d, tn), lambda i, j: (0, j))
    return pl.pallas_call(
        body, name=name, grid=(m // tm, n // tn),
        in_specs=[pl.BlockSpec((tm, d), lambda i, j: (i, 0)), pl.BlockSpec((1, d), lambda i, j: (0, 0)), w_spec] + after_specs,
        out_specs=[pl.BlockSpec((tm, tn), lambda i, j: (i, j)), pl.BlockSpec((tm, d), lambda i, j: (i, 0))],
        out_shape=[SDS((m, n), out_dtype), SDS((m, d), MXU_DTYPE)],
        scratch_shapes=[pltpu.VMEM((tm, d), MXU_DTYPE)],
        compiler_params=_cp("parallel", "arbitrary"))(x, g, w, *after)


ROW_TILE = 512
TOKEN_TILE = 1024


def _norm_bwd(x, g, dy, res, out_dtype, name):
    n, d = x.shape
    tr = min(ROW_TILE, n)
    has_res = res is not None

    def body(*refs):
        x_ref, g_ref, dy_ref = refs[:3]
        dx_ref, dg_ref = refs[-2:]
        dx, dg = _rms_bwd(x_ref[...], g_ref[...], dy_ref[...].astype(F32))
        if has_res:
            dx = dx + refs[3][...]
        dx_ref[...] = dx.astype(dx_ref.dtype)

        @pl.when(pl.program_id(0) == 0)
        def _():
            dg_ref[...] = jnp.zeros_like(dg_ref)

        dg_ref[...] += dg

    row = pl.BlockSpec((tr, d), lambda i: (i, 0))
    vec = pl.BlockSpec((1, d), lambda i: (0, 0))
    ins = [x, g, dy] + ([res] if has_res else [])
    return pl.pallas_call(
        body, name=name, grid=(n // tr,), in_specs=[row, vec, row] + ([row] if has_res else []),
        out_specs=[row, vec], out_shape=[SDS((n, d), out_dtype), SDS((1, d), F32)],
        compiler_params=_cp("arbitrary"))(*ins)


def _rope_tables(n):
    pairs = ATT_HEAD_DIM // 4
    t = np.arange(n)
    inv = np.power(ROPE_THETA, -np.arange(pairs, dtype=np.float32) / pairs).astype(np.float32)
    ang = np.concatenate([(t // GRID_W)[:, None].astype(np.float32) * inv, (t % GRID_W)[:, None].astype(np.float32) * inv], axis=-1)
    cos = np.repeat(np.cos(ang), 2, axis=-1)
    sin = np.repeat(np.sin(ang), 2, axis=-1) * np.tile(np.array([-1.0, 1.0], np.float32), ATT_HEAD_DIM // 2)
    return jnp.asarray(np.tile(cos, 2), F32), jnp.asarray(np.tile(sin, 2), F32)


def _swap_pairs(x):
    lane = lax.broadcasted_iota(jnp.int32, x.shape, 1)
    return jnp.where((lane & 1) == 0, pltpu.roll(x, 127, axis=1), pltpu.roll(x, 1, axis=1))


def _head_mean(v):
    lane = lax.broadcasted_iota(jnp.int32, v.shape, 1)
    lo = jnp.where(lane < ATT_HEAD_DIM, v, 0.0)
    s0 = jnp.sum(lo, axis=-1, keepdims=True)
    s1 = jnp.sum(v - lo, axis=-1, keepdims=True)
    return jnp.where(lane < ATT_HEAD_DIM, s0, s1) * (1.0 / ATT_HEAD_DIM)


def _qk_prep(p, gq, gk, cos, sin, name):
    n = p.shape[0]
    tr = min(ROW_TILE, n)

    def one(xv, g, c, s):
        xn = xv * lax.rsqrt(_head_mean(xv * xv) + EPS) * g
        return xn * c + _swap_pairs(xn) * s

    def body(q_ref, k_ref, gq_ref, gk_ref, c_ref, s_ref, qo_ref, ko_ref):
        c, s = c_ref[...], s_ref[...]
        for j in range(ATT_Q_DIM // 128):
            qo_ref[:, j * 128:(j + 1) * 128] = one(q_ref[:, j * 128:(j + 1) * 128], gq_ref[...], c, s).astype(qo_ref.dtype)
        ko_ref[...] = one(k_ref[...], gk_ref[...], c, s).astype(ko_ref.dtype)

    vec = pl.BlockSpec((1, 128), lambda i: (0, 0))
    tab = pl.BlockSpec((tr, 128), lambda i: (i, 0))
    return pl.pallas_call(
        body, name=name, grid=(n // tr,),
        in_specs=[pl.BlockSpec((tr, ATT_Q_DIM), lambda i: (i, 0)), pl.BlockSpec((tr, 128), lambda i: (i, OFF_AK // 128)), vec, vec, tab, tab],
        out_specs=[pl.BlockSpec((tr, ATT_Q_DIM), lambda i: (i, 0)), tab],
        out_shape=[SDS((n, ATT_Q_DIM), MXU_DTYPE), SDS((n, ATT_KV_DIM), MXU_DTYPE)],
        compiler_params=_cp("parallel"))(p, p, gq, gk, cos, sin)


def _qk_prep_bwd(p, gq, gk, cos, sin, dq, dk, name):
    n = p.shape[0]
    tr = min(ROW_TILE, n)

    def one(xv, g, c, s, dout):
        dxn = dout * c + _swap_pairs(dout * s)
        r = lax.rsqrt(_head_mean(xv * xv) + EPS)
        xh = xv * r
        dn = dxn * g
        dx = r * (dn - xh * _head_mean(dn * xh))
        return dx, jnp.sum(dxn * xh, axis=0, keepdims=True)

    def body(q_ref, k_ref, gq_ref, gk_ref, c_ref, s_ref, dq_ref, dk_ref, dqo_ref, dko_ref, dgq_ref, dgk_ref):
        @pl.when(pl.program_id(0) == 0)
        def _():
            dgq_ref[...] = jnp.zeros_like(dgq_ref)
            dgk_ref[...] = jnp.zeros_like(dgk_ref)

        c, s = c_ref[...], s_ref[...]
        for j in range(ATT_Q_DIM // 128):
            sl = slice(j * 128, (j + 1) * 128)
            dx, dg = one(q_ref[:, sl], gq_ref[...], c, s, dq_ref[:, sl])
            dqo_ref[:, sl] = dx.astype(dqo_ref.dtype)
            dgq_ref[:, sl] += dg
        dx, dg = one(k_ref[...], gk_ref[...], c, s, dk_ref[...])
        dko_ref[...] = dx.astype(dko_ref.dtype)
        dgk_ref[...] += dg

    vec = pl.BlockSpec((1, 128), lambda i: (0, 0))
    tab = pl.BlockSpec((tr, 128), lambda i: (i, 0))
    qrow = pl.BlockSpec((tr, ATT_Q_DIM), lambda i: (i, 0))
    return pl.pallas_call(
        body, name=name, grid=(n // tr,),
        in_specs=[qrow, pl.BlockSpec((tr, 128), lambda i: (i, OFF_AK // 128)), vec, vec, tab, tab, qrow, tab],
        out_specs=[qrow, tab, pl.BlockSpec((1, ATT_Q_DIM), lambda i: (0, 0)), vec],
        out_shape=[SDS((n, ATT_Q_DIM), MXU_DTYPE), SDS((n, ATT_KV_DIM), MXU_DTYPE), SDS((1, ATT_Q_DIM), F32), SDS((1, 128), F32)],
        compiler_params=_cp("arbitrary"))(p, p, gq, gk, cos, sin, dq, dk)


ATT_FWD_STEP = (256, 4)
ATT_BWD_STEP = (512, 2)


def _attn_fwd(q, k, v, name):
    n = q.shape[0]
    tq, step_heads = min(ATT_FWD_STEP[0], n), ATT_FWD_STEP[1]
    scale = ATT_HEAD_DIM ** -0.5
    gw = step_heads * ATT_HEAD_DIM
    parts = ATT_GROUP // step_heads

    def body(q_ref, k_ref, v_ref, o_ref):
        kk, vv = k_ref[0], v_ref[0]
        v_ones = jnp.concatenate([vv, jnp.ones_like(vv)], axis=1)
        outs = []
        for g in range(step_heads):
            s = _dot(q_ref[:, g * ATT_HEAD_DIM:(g + 1) * ATT_HEAD_DIM] * scale, kk, "nt")
            e = jnp.exp(s - jnp.max(s, axis=-1, keepdims=True))
            ov = _dot(e, v_ones)
            outs.append(ov[:, :ATT_HEAD_DIM] / ov[:, ATT_HEAD_DIM:])
        o_ref[...] = jnp.concatenate(outs, axis=-1).astype(o_ref.dtype)

    kv = pl.BlockSpec((1, n, ATT_HEAD_DIM), lambda h, i, pr: (h, 0, 0))
    qb = pl.BlockSpec((tq, gw), lambda h, i, pr: (i, h * parts + pr))
    return pl.pallas_call(
        body, name=name, grid=(ATT_KV_HEADS, n // tq, parts), in_specs=[qb, kv, kv],
        out_specs=qb, out_shape=SDS((n, ATT_Q_DIM), MXU_DTYPE),
        compiler_params=_cp("parallel", "parallel", "parallel"))(q, k, v)


def _attn_bwd(q, k, v, o, do, name):
    n = q.shape[0]
    tq, step_heads = min(ATT_BWD_STEP[0], n), ATT_BWD_STEP[1]
    scale = ATT_HEAD_DIM ** -0.5
    gw = step_heads * ATT_HEAD_DIM
    parts = ATT_GROUP // step_heads

    def body(q_ref, k_ref, v_ref, o_ref, do_ref, dq_ref, dk_ref, dv_ref):
        @pl.when(jnp.logical_and(pl.program_id(1) == 0, pl.program_id(2) == 0))
        def _():
            dk_ref[...] = jnp.zeros_like(dk_ref)
            dv_ref[...] = jnp.zeros_like(dv_ref)

        kk, vv = k_ref[0], v_ref[0]
        dqs = []
        dk_acc = jnp.zeros((ATT_HEAD_DIM, n), F32)
        dv_acc = jnp.zeros((ATT_HEAD_DIM, n), F32)
        for g in range(step_heads):
            sl = slice(g * ATT_HEAD_DIM, (g + 1) * ATT_HEAD_DIM)
            qg, dog = q_ref[:, sl] * scale, do_ref[:, sl].astype(F32)
            s = _dot(qg, kk, "nt")
            e = jnp.exp(s - jnp.max(s, axis=-1, keepdims=True))
            inv = 1.0 / jnp.sum(e, axis=-1, keepdims=True)
            delta = jnp.sum(dog * o_ref[:, sl].astype(F32), axis=-1, keepdims=True)
            dse = e * (_dot(dog, vv, "nt") - delta)
            dqs.append(_dot(dse, kk) * (inv * scale))
            dk_acc += _dot(qg.astype(F32) * inv, dse, "tn")
            dv_acc += _dot(dog * inv, e, "tn")
        dq_ref[...] = jnp.concatenate(dqs, axis=-1)
        dk_ref[0] += dk_acc
        dv_ref[0] += dv_acc

    kv = pl.BlockSpec((1, n, ATT_HEAD_DIM), lambda h, i, pr: (h, 0, 0))
    kvt = pl.BlockSpec((1, ATT_HEAD_DIM, n), lambda h, i, pr: (h, 0, 0))
    qb = pl.BlockSpec((tq, gw), lambda h, i, pr: (i, h * parts + pr))
    return pl.pallas_call(
        body, name=name, grid=(ATT_KV_HEADS, n // tq, parts), in_specs=[qb, kv, kv, qb, qb], out_specs=[qb, kvt, kvt],
        out_shape=[SDS((n, ATT_Q_DIM), F32), SDS((ATT_KV_HEADS, ATT_HEAD_DIM, n), F32), SDS((ATT_KV_HEADS, ATT_HEAD_DIM, n), F32)],
        compiler_params=_cp("parallel", "arbitrary", "arbitrary"))(q, k, v, o, do)


def _both_directions(mats, axis):
    fwd = np.concatenate(mats, axis=axis).astype(np.float32)
    bwd = np.concatenate([m[::-1, ::-1] for m in mats], axis=axis).astype(np.float32)
    return jnp.asarray(np.stack([fwd, bwd]), MXU_DTYPE)


def _hg_segments():
    c = HG_CHUNK
    t = np.arange(c)[:, None]
    r = np.arange(c)[None, :]
    mats = [(r <= t)]
    for lev in range(HG_LEVELS):
        h = c >> (lev + 1)
        mid = (t // (2 * h)) * (2 * h) + h - 1
        hi = (t // h) % 2 == 1
        mats.append(np.where(hi, (r > mid) & (r <= t), (r > t) & (r <= mid)))
    mats.append(r > t)
    return _both_directions(mats, 0)


def _hg_pair_sums():
    c = HG_CHUNK
    r = np.arange(c)[:, None]
    t = np.arange(c)[None, :]
    gp, gn = [t >= r], [t < r]
    for lev in range(HG_LEVELS):
        sh = HG_LEVELS - 1 - lev
        same = (r >> sh) == (t >> sh)
        gp.append(same & (t >= r))
        gn.append(same & (t < r))
    return _both_directions(gp, 1), _both_directions(gn, 1)


def _split_dot(mat, x):
    hi = x.astype(MXU_DTYPE)
    lo = (x - hi.astype(F32)).astype(MXU_DTYPE)
    return _dot(mat, hi) + _dot(mat, lo)


def _hg_gates(hq, z, a0, a1):
    q = hq * _sigmoid(hq)
    sg = _sigmoid(z)
    lb = _sigmoid(a0 - a1)
    f = lb + (1.0 - lb) * sg
    k = (1.0 - lb) * (1.0 - sg)
    return q, f, k, sg, lb


def _hg_level_masks():
    c = HG_CHUNK
    t = np.arange(c)
    later, same = [], []
    for lev in range(HG_LEVELS):
        sh = HG_LEVELS - 1 - lev
        later.append(np.broadcast_to((((t >> sh) & 1) == 1)[:, None], (c, HG_HEAD_DIM)))
        same.append((t[:, None] >> (sh + 1)) == (t[None, :] >> (sh + 1)))
    same.append(t[:, None] == t[None, :])
    later = np.stack(later).astype(np.float32)
    return jnp.asarray(np.stack([later, 1.0 - later]), F32), jnp.asarray(np.stack(same).astype(np.float32), F32)


def _hg_level(q, k, ex, later_ref, lev):
    e = ex[lev + 1]
    e_q = e * later_ref[0, lev]
    e_k = e - e_q
    return q * e_q, k * e_k, e_q, e_k


def _hg_intra(q, k, ex, later_ref, same_ref):
    a = same_ref[HG_LEVELS] * jnp.sum(q * k, axis=-1, keepdims=True)
    for lev in range(HG_LEVELS):
        qs, ks, _, _ = _hg_level(q, k, ex, later_ref, lev)
        a = a + same_ref[lev] * _dot(qs, ks, "nt")
    return a


def _hg_specs(n, with_time):
    c = HG_CHUNK
    nc = n // c

    def chunk(d, i):
        first = d if with_time else 1 - d
        return i + first * (nc - 1 - 2 * i)

    def pcols(off, dir_stride=0):
        return [pl.BlockSpec((c, HG_PAIR), lambda d, i, j=j: (chunk(d, i), off // HG_PAIR + dir_stride // HG_PAIR * d + j)) for j in range(2)]

    specs = dict(
        hq=pcols(OFF_HQ), v=pcols(OFF_HI), z=pcols(OFF_ZF, OFF_ZB - OFF_ZF),
        shared=pl.BlockSpec((c, HG_DIM), lambda d, i: (chunk(d, i), 0)),
        per_dir=pl.BlockSpec((1, c, HG_DIM), lambda d, i: (d, chunk(d, i), 0)),
        vec=pl.BlockSpec((1, 1, HG_DIM), lambda d, i: (d, 0, 0)),
        seg=pl.BlockSpec((1, (HG_LEVELS + 2) * c, c), lambda d, i: (d, 0, 0)),
        sums=pl.BlockSpec((1, c, (HG_LEVELS + 1) * c), lambda d, i: (d, 0, 0)),
        later=pl.BlockSpec((1, HG_LEVELS, c, HG_HEAD_DIM), lambda d, i: (d, 0, 0, 0)),
        same=pl.BlockSpec((HG_LEVELS + 1, c, c), lambda d, i: (0, 0, 0)),
        state=pl.BlockSpec((1, HG_HEADS, 1, HG_HEAD_DIM, HG_HEAD_DIM), lambda d, i: (d, 0, chunk(d, i), 0, 0)),
        weights=pl.BlockSpec((1, HG_HEADS, 1, c, c), lambda d, i: (d, 0, chunk(d, i), 0, 0)),
        levels=pl.BlockSpec((1, HG_HEADS, 1, HG_LEVELS, c, HG_HEAD_DIM), lambda d, i: (d, 0, chunk(d, i), 0, 0, 0)),
        kept=pl.BlockSpec((1, HG_KEPT, c, HG_DIM), lambda d, i: (d, 0, chunk(d, i), 0)))
    return nc, specs


def _hg_head(refs, hh):
    off = (hh % 2) * HG_HEAD_DIM
    return refs[hh // 2][:, off:off + HG_HEAD_DIM]


def _hg_lanes(hh):
    return slice(hh * HG_HEAD_DIM, (hh + 1) * HG_HEAD_DIM)


def _hg_exps(seg_ref, f):
    c = HG_CHUNK
    args = _split_dot(seg_ref[0], jnp.log(f))
    return [jnp.exp(args[j * c:(j + 1) * c]) for j in range(HG_LEVELS + 2)]


def _hg_last_row(a, mirrored):
    return jnp.where(mirrored, a[0:1, :], a[HG_CHUNK - 1:HG_CHUNK, :])


def _hgrn_fwd(p, a0, a1, seg, masks, name):
    n = p.shape[0]
    nc, sp = _hg_specs(n, True)

    def body(hq0, hq1, z0, z1, v0, v1, a0_ref, a1_ref, seg_ref, later_ref, same_ref, o_ref, s0_ref, a_ref, e_ref, g_ref, st):
        @pl.when(pl.program_id(1) == 0)
        def _():
            st[...] = jnp.zeros_like(st)

        mirrored = pl.program_id(0) == 1
        for hh in range(HG_HEADS):
            ln = _hg_lanes(hh)
            hqv = _hg_head((hq0, hq1), hh)
            q, f, k, sg, _ = _hg_gates(hqv, _hg_head((z0, z1), hh), a0_ref[0, :, ln], a1_ref[0, :, ln])
            vv = _hg_head((v0, v1), hh)
            ex = _hg_exps(seg_ref, f)
            for lev in range(HG_LEVELS):
                e_ref[0, hh, 0, lev] = ex[lev + 1].astype(e_ref.dtype)
            sq = _sigmoid(hqv)
            for j, kept in enumerate((q, k, f, sg, sq * (1.0 + hqv * (1.0 - sq)), ex[0], ex[HG_LEVELS + 1])):
                g_ref[0, j, :, ln] = kept
            a = _hg_intra(q, k, ex, later_ref, same_ref).astype(MXU_DTYPE)
            a_ref[0, hh, 0] = a
            s_t = st[hh]
            s0_ref[0, hh, 0] = s_t
            o_ref[0, :, ln] = _dot(a, vv) + _dot(q * ex[0], s_t, "nt")
            st[hh] = s_t * _hg_last_row(ex[0], mirrored) + _dot(vv, k * ex[HG_LEVELS + 1], "tn")

    return pl.pallas_call(
        body, name=name, grid=(2, nc), in_specs=sp["hq"] + sp["z"] + sp["v"] + [sp["vec"], sp["vec"], sp["seg"], sp["later"], sp["same"]],
        out_specs=[sp["per_dir"], sp["state"], sp["weights"], sp["levels"], sp["kept"]],
        out_shape=[SDS((2, n, HG_DIM), F32), SDS((2, HG_HEADS, nc, HG_HEAD_DIM, HG_HEAD_DIM), F32),
                   SDS((2, HG_HEADS, nc, HG_CHUNK, HG_CHUNK), MXU_DTYPE),
                   SDS((2, HG_HEADS, nc, HG_LEVELS, HG_CHUNK, HG_HEAD_DIM), MXU_DTYPE), SDS((2, HG_KEPT, n, HG_DIM), F32)],
        scratch_shapes=[pltpu.VMEM((HG_HEADS, HG_HEAD_DIM, HG_HEAD_DIM), F32)],
        compiler_params=_cp("parallel", "arbitrary"))(p, p, p, p, p, p, a0, a1, seg, *masks)


def _hgrn_bwd(p, a0, a1, masks, gp, gn, do, s0, a, e, kept, name):
    n = p.shape[0]
    nc, sp = _hg_specs(n, False)


    def body(v0, v1, a0_ref, a1_ref, later_ref, same_ref, gp_ref, gn_ref, do_ref, s0_ref, a_ref, e_ref, g_ref,
             dhq_ref, dz_ref, dv_ref, dlb_ref, rt):
        @pl.when(pl.program_id(1) == 0)
        def _():
            rt[...] = jnp.zeros_like(rt)
            dlb_ref[...] = jnp.zeros_like(dlb_ref)

        mirrored = pl.program_id(0) == 1
        for hh in range(HG_HEADS):
            ln = _hg_lanes(hh)
            q, k, f, sg, dsilu, e_first, e_last = (g_ref[0, j, :, ln] for j in range(HG_KEPT))
            lb = _sigmoid(a0_ref[0, :, ln] - a1_ref[0, :, ln])
            vv, dov = _hg_head((v0, v1), hh), do_ref[:, ln]
            ex = [e_first] + [e_ref[0, hh, 0, lev].astype(F32) for lev in range(HG_LEVELS)] + [e_last]
            a = a_ref[0, hh, 0]
            da = _dot(dov, vv, "nt")
            diag = jnp.sum(dov * vv, axis=-1, keepdims=True)
            s_t = s0_ref[0, hh, 0]
            r_t = rt[hh]
            k_end = k * ex[HG_LEVELS + 1]
            dv_ref[0, :, ln] = _dot(a, dov, "tn") + _dot(k_end, r_t, "nt")
            dq_inter = ex[0] * _dot(dov, s_t)
            dk_inter = ex[HG_LEVELS + 1] * _dot(vv, r_t)
            dq = diag * k + dq_inter
            dk = diag * q + dk_inter
            q_terms, k_terms = [q * dq_inter], [k * dk_inter]
            for lev in range(HG_LEVELS):
                qs, ks, e_q, e_k = _hg_level(q, k, ex, later_ref, lev)
                pairs = da * same_ref[lev]
                q_part = e_q * _dot(pairs, ks)
                k_part = e_k * _dot(pairs, qs, "tn")
                dq, dk = dq + q_part, dk + k_part
                q_terms.append(q * q_part)
                k_terms.append(k * k_part)
            decay = _hg_last_row(ex[0], mirrored)
            rt[hh] = r_t * decay + _dot(dov, q * ex[0], "tn")
            later = decay * jnp.sum(s_t * r_t, axis=0, keepdims=True)
            dlf = _dot(gp_ref[0], jnp.concatenate(q_terms, axis=0)) + _dot(gn_ref[0], jnp.concatenate(k_terms, axis=0)) + later
            df = dlf / f - dk
            dz_ref[0, :, ln] = df * (1.0 - lb) * sg * (1.0 - sg)
            dlb_ref[0, :, ln] += jnp.sum(df * (1.0 - sg), axis=0, keepdims=True)
            dhq_ref[0, :, ln] = dq * dsilu

    out = SDS((2, n, HG_DIM), F32)
    return pl.pallas_call(
        body, name=name, grid=(2, nc),
        in_specs=sp["v"] + [sp["vec"], sp["vec"], sp["later"], sp["same"], sp["sums"], sp["sums"],
                            sp["shared"], sp["state"], sp["weights"], sp["levels"], sp["kept"]],
        out_specs=[sp["per_dir"], sp["per_dir"], sp["per_dir"], sp["vec"]], out_shape=[out, out, out, SDS((2, 1, HG_DIM), F32)],
        scratch_shapes=[pltpu.VMEM((HG_HEADS, HG_HEAD_DIM, HG_HEAD_DIM), F32)],
        compiler_params=_cp("parallel", "arbitrary"))(p, p, a0, a1, *masks, gp, gn, do, s0, a, e, kept)


def _hg_post(o2, p, g, name):
    n = p.shape[0]
    tr = min(ROW_TILE, n)
    w = 2 * HG_HEAD_DIM

    def body(of_ref, ob_ref, hg_ref, g_ref, o_ref):
        for j in range(2):
            sl = slice(j * HG_HEAD_DIM, (j + 1) * HG_HEAD_DIM)
            o = of_ref[0, :, sl] + ob_ref[0, :, sl]
            hg = hg_ref[:, sl]
            o_ref[:, sl] = (o * _rstd(o) * g_ref[...] * (hg * _sigmoid(hg))).astype(o_ref.dtype)

    blk = pl.BlockSpec((tr, w), lambda i, j: (i, j))
    dirs = [pl.BlockSpec((1, tr, w), lambda i, j, d=d: (d, i, j)) for d in range(2)]
    return pl.pallas_call(
        body, name=name, grid=(n // tr, HG_DIM // w),
        in_specs=dirs + [pl.BlockSpec((tr, w), lambda i, j: (i, OFF_HG // w + j)), pl.BlockSpec((1, HG_HEAD_DIM), lambda i, j: (0, 0))],
        out_specs=blk, out_shape=SDS((n, HG_DIM), MXU_DTYPE), compiler_params=_cp("parallel", "parallel"))(o2, o2, p, g)


def _hg_post_bwd(o2, p, g, dcat, name, after=()):
    n = p.shape[0]
    tr = min(ROW_TILE, n)
    w = 2 * HG_HEAD_DIM
    after, after_specs = _unread(after)

    def body(of_ref, ob_ref, hg_ref, g_ref, d_ref, *rest):
        do_ref, dhg_ref, dg_ref = rest[len(after):]

        @pl.when(pl.program_id(1) == 0)
        def _():
            dg_ref[...] = jnp.zeros_like(dg_ref)

        for j in range(2):
            sl = slice(j * HG_HEAD_DIM, (j + 1) * HG_HEAD_DIM)
            o = of_ref[0, :, sl] + ob_ref[0, :, sl]
            hg = hg_ref[:, sl]
            d = d_ref[:, sl].astype(F32)
            sg = _sigmoid(hg)
            on = o * _rstd(o) * g_ref[...]
            dhg_ref[:, sl] = (d * on * sg * (1.0 + hg * (1.0 - sg))).astype(dhg_ref.dtype)
            dx, dg = _rms_bwd(o, g_ref[...], d * hg * sg)
            do_ref[:, sl] = dx
            dg_ref[0, :, sl] += dg

    blk = pl.BlockSpec((tr, w), lambda j, i: (i, j))
    dirs = [pl.BlockSpec((1, tr, w), lambda j, i, d=d: (d, i, j)) for d in range(2)]
    return pl.pallas_call(
        body, name=name, grid=(HG_DIM // w, n // tr),
        in_specs=dirs + [pl.BlockSpec((tr, w), lambda j, i: (i, OFF_HG // w + j)), pl.BlockSpec((1, HG_HEAD_DIM), lambda j, i: (0, 0)),
                         pl.BlockSpec((tr, w), lambda j, i: (i, ATT_Q_DIM // w + j))] + after_specs,
        out_specs=[blk, blk, pl.BlockSpec((1, 1, w), lambda j, i: (j, 0, 0))],
        out_shape=[SDS((n, HG_DIM), F32), SDS((n, HG_DIM), MXU_DTYPE), SDS((HG_DIM // w, 1, w), F32)],
        compiler_params=_cp("parallel", "arbitrary"))(o2, o2, p, g, dcat, *after)


XATT_TQ = 512


def _xattn_fwd(q, kv, name):
    n, nm = q.shape[0], kv.shape[0]
    tq = min(XATT_TQ, n)
    scale = X_HEAD_DIM ** -0.5

    def body(q_ref, k_ref, v_ref, o_ref):
        s = _dot(q_ref[...], k_ref[...], "nt") * scale
        e = jnp.exp(s - jnp.max(s, axis=-1, keepdims=True))
        o_ref[...] = _dot(e / jnp.sum(e, axis=-1, keepdims=True), v_ref[...]).astype(o_ref.dtype)

    qb = pl.BlockSpec((tq, X_HEAD_DIM), lambda h, i: (i, h))
    return pl.pallas_call(
        body, name=name, grid=(X_HEADS, n // tq),
        in_specs=[qb, pl.BlockSpec((nm, X_HEAD_DIM), lambda h, i: (0, h)), pl.BlockSpec((nm, X_HEAD_DIM), lambda h, i: (0, X_HEADS + h))],
        out_specs=qb, out_shape=SDS(q.shape, MXU_DTYPE), compiler_params=_cp("parallel", "parallel"))(q, kv, kv)


def _xattn_bwd(q, kv, do, name, after=()):
    n, nm = q.shape[0], kv.shape[0]
    tq = min(XATT_TQ, n)
    scale = X_HEAD_DIM ** -0.5
    after, after_specs = _unread(after)

    def body(q_ref, k_ref, v_ref, do_ref, *rest):
        dq_ref, dk_ref, dv_ref = rest[len(after):]

        @pl.when(pl.program_id(1) == 0)
        def _():
            dk_ref[...] = jnp.zeros_like(dk_ref)
            dv_ref[...] = jnp.zeros_like(dv_ref)

        qv, dov = q_ref[...], do_ref[...]
        s = _dot(qv, k_ref[...], "nt") * scale
        e = jnp.exp(s - jnp.max(s, axis=-1, keepdims=True))
        p = e / jnp.sum(e, axis=-1, keepdims=True)
        dp = _dot(dov, v_ref[...], "nt")
        ds = p * (dp - jnp.sum(p * dp, axis=-1, keepdims=True)) * scale
        dq_ref[...] = _dot(ds, k_ref[...]).astype(dq_ref.dtype)
        dk_ref[...] += _dot(ds, qv, "tn")
        dv_ref[...] += _dot(p, dov, "tn")

    qb = pl.BlockSpec((tq, X_HEAD_DIM), lambda h, i: (i, h))
    kb = pl.BlockSpec((nm, X_HEAD_DIM), lambda h, i: (0, h))
    return pl.pallas_call(
        body, name=name, grid=(X_HEADS, n // tq),
        in_specs=[qb, kb, pl.BlockSpec((nm, X_HEAD_DIM), lambda h, i: (0, X_HEADS + h)), qb] + after_specs, out_specs=[qb, kb, kb],
        out_shape=[SDS(q.shape, MXU_DTYPE), SDS((nm, X_HEADS * X_HEAD_DIM), F32), SDS((nm, X_HEADS * X_HEAD_DIM), F32)],
        compiler_params=_cp("parallel", "arbitrary"))(q, kv, kv, do, *after)


def _edge_rows(shape):
    row = lax.broadcasted_iota(jnp.int32, shape, 0)
    return row == 0, row == shape[0] - 1


def _shift_rows(u, down, edges):
    if down:
        return jnp.where(edges[0], 0.0, pltpu.roll(u, 1, axis=0))
    return jnp.where(edges[1], 0.0, pltpu.roll(u, u.shape[0] - 1, axis=0))


def _conv(u, w, b, edges):
    return b + _shift_rows(u, True, edges) * w[0:1, :] + u * w[1:2, :] + _shift_rows(u, False, edges) * w[2:3, :]


def _ff_specs(n):
    gate = lambda rows: pl.BlockSpec((rows, FF_COLS), lambda j: (0, j))
    val = lambda rows: pl.BlockSpec((rows, FF_COLS), lambda j: (0, FF_BLOCKS + j))
    return [gate(n), val(n), gate(3), val(3), gate(1), val(1)], gate


def _conv_gate(u, cw, cb, name):
    n = u.shape[0]
    ins, gate_blk = _ff_specs(n)

    def body(ug_ref, uv_ref, wg_ref, wv_ref, bg_ref, bv_ref, o_ref):
        edges = _edge_rows(ug_ref.shape)
        gate = _conv(ug_ref[...], wg_ref[...], bg_ref[...], edges)
        val = _conv(uv_ref[...], wv_ref[...], bv_ref[...], edges)
        o_ref[...] = (gate * _sigmoid(gate) * val).astype(o_ref.dtype)

    return pl.pallas_call(
        body, name=name, grid=(FF_BLOCKS,), in_specs=ins, out_specs=gate_blk(n), out_shape=SDS((n, D_FF), MXU_DTYPE),
        compiler_params=_cp("parallel"))(u, u, cw, cw, cb, cb)


def _conv_gate_bwd(u, cw, cb, da, name, after=()):
    n = u.shape[0]
    ins, gate_blk = _ff_specs(n)
    after, after_specs = _unread(after)

    def side(dacc, u, w, edges, du_ref, dw_ref, db_ref):
        nxt, prv = _shift_rows(dacc, False, edges), _shift_rows(dacc, True, edges)
        du_ref[...] = (nxt * w[0:1, :] + dacc * w[1:2, :] + prv * w[2:3, :]).astype(du_ref.dtype)
        db_ref[...] = jnp.sum(dacc, axis=0, keepdims=True)
        dw_ref[0:1, :] = jnp.sum(nxt * u, axis=0, keepdims=True)
        dw_ref[1:2, :] = jnp.sum(dacc * u, axis=0, keepdims=True)
        dw_ref[2:3, :] = jnp.sum(prv * u, axis=0, keepdims=True)

    def body(ug_ref, uv_ref, wg_ref, wv_ref, bg_ref, bv_ref, da_ref, *rest):
        dug_ref, duv_ref, dwg_ref, dwv_ref, dbg_ref, dbv_ref = rest[len(after):]
        ug, uv = ug_ref[...], uv_ref[...]
        edges = _edge_rows(ug.shape)
        gate = _conv(ug, wg_ref[...], bg_ref[...], edges)
        val = _conv(uv, wv_ref[...], bv_ref[...], edges)
        sg = _sigmoid(gate)
        dav = da_ref[...].astype(F32)
        side(dav * val * sg * (1.0 + gate * (1.0 - sg)), ug, wg_ref[...], edges, dug_ref, dwg_ref, dbg_ref)
        side(dav * gate * sg, uv, wv_ref[...], edges, duv_ref, dwv_ref, dbv_ref)

    return pl.pallas_call(
        body, name=name, grid=(FF_BLOCKS,), in_specs=ins + [gate_blk(n)] + after_specs,
        out_specs=[gate_blk(n), gate_blk(n), gate_blk(3), gate_blk(3), gate_blk(1), gate_blk(1)],
        out_shape=[SDS((n, D_FF), MXU_DTYPE)] * 2 + [SDS((3, D_FF), F32)] * 2 + [SDS((1, D_FF), F32)] * 2,
        compiler_params=_cp("parallel"))(u, u, cw, cw, cb, cb, da, *after)


def _adamw_update(w_ref, gv, m_ref, v_ref, d_ref, mo_ref, vo_ref, go_ref):
    go_ref[...] = gv
    mn = ADAM_B1 * m_ref[...] + (1.0 - ADAM_B1) * gv
    vn = ADAM_B2 * v_ref[...] + (1.0 - ADAM_B2) * gv * gv
    m_hat = mn / (1.0 - ADAM_B1 ** ADAM_STEP)
    v_hat = vn / (1.0 - ADAM_B2 ** ADAM_STEP)
    d_ref[...] = -ADAM_LR * (m_hat / (jnp.sqrt(v_hat) + ADAM_EPS) + ADAM_WD * w_ref[...])
    mo_ref[...] = mn
    vo_ref[...] = vn


def _adamw_rows(summed, spans, params, name):
    nt = len(params)

    def body(s_ref, *refs):
        for t, (off, d) in enumerate(spans):
            w_ref, m_ref, v_ref = refs[3 * t:3 * t + 3]
            _adamw_update(w_ref, s_ref[:, off:off + d], m_ref, v_ref, *refs[3 * nt + 4 * t:3 * nt + 4 * t + 4])

    assert all(p[0].shape == (1, d) for p, (_, d) in zip(params, spans)), name
    vm = pl.BlockSpec(memory_space=pltpu.VMEM)
    out = pl.pallas_call(body, name=name, in_specs=[vm] * (1 + 3 * nt), out_specs=[vm] * (4 * nt),
                         out_shape=[SDS(p[0].shape, F32) for p in params for _ in range(4)])(summed, *[a for p in params for a in p])
    return [tuple(out[4 * t:4 * t + 4]) for t in range(nt)]


def _adamw(params, name):
    nt = len(params)
    rows = [p[0].shape[-2] for p in params]
    fits = lambda s: max(rows) <= s * ELEMENTWISE_ROWS and all(r % s == 0 and (s == 1 or r // s % 8 == 0) for r in rows)
    steps = next(s for s in range(1, max(rows) + 1) if fits(s))
    assert all(p[0].ndim == 2 or p[0].shape[:-2] == (1,) or (p[0].ndim == 3 and steps == 1) for p in params), name

    def body(*refs):
        for t in range(nt):
            w_ref, g_ref, m_ref, v_ref = refs[4 * t:4 * t + 4]
            _adamw_update(w_ref, g_ref[...], m_ref, v_ref, *refs[4 * (nt + t):4 * (nt + t) + 4])

    def blk(w):
        tr, c = w.shape[-2] // steps, w.shape[-1]
        return pl.BlockSpec((tr, c), lambda i: (i, 0)) if w.ndim == 2 else pl.BlockSpec((w.shape[0], tr, c), lambda i: (0, i, 0))

    specs = [blk(p[0]) for p in params for _ in range(4)]
    out = pl.pallas_call(body, name=name, grid=(steps,), in_specs=specs, out_specs=specs,
                         out_shape=[SDS(p[0].shape, F32) for p in params for _ in range(4)],
                         compiler_params=_cp("parallel"))(*[a for p in params for a in p])
    return [tuple(out[4 * t:4 * t + 4]) for t in range(nt)]


ANY = pl.BlockSpec(memory_space=pl.ANY)


def _place():
    x, y, c = lax.axis_index("x"), lax.axis_index("y"), lax.axis_index("c")
    return x, y, c, [(1 - x, y), (x, 1 - y), (1 - x, 1 - y)]


HBM = pl.BlockSpec(memory_space=pltpu.HBM)
SEM = pl.BlockSpec(memory_space=pltpu.SEMAPHORE)
TOKEN = pl.BlockSpec(memory_space=pltpu.VMEM)
TOKEN_SHAPE = SDS((8, 128), F32)
PEERS = 7


def _in_hbm(a):
    return pltpu.with_memory_space_constraint(a, pltpu.HBM)


def _split_params():
    return pltpu.CompilerParams(has_side_effects=pltpu.SideEffectType.DATAFLOW_SIDE_EFFECTING)


def _gather_start(shards, name, after=()):
    nt = len(shards)
    after, after_specs = _unread(after)

    def body(*refs):
        ins, lands = refs[:nt], refs[nt:2 * nt]
        outs = refs[2 * nt + len(after):]
        sends, recvs = outs[:nt], outs[nt:2 * nt]
        x, y, c, chips = _place()
        me = 2 * x + y
        for t in range(nt):
            h = ins[t].shape[0] // 2
            mine = pl.ds(c * h, h)
            for j, (cx, cy) in enumerate(chips):
                for dc in range(2):
                    pltpu.make_async_remote_copy(src_ref=ins[t].at[mine], dst_ref=lands[t].at[me, mine], send_sem=sends[t].at[2 * j + dc],
                                                 recv_sem=recvs[t].at[2 * j + c], device_id=(cx, cy, dc), device_id_type=MESH).start()
            pltpu.make_async_remote_copy(src_ref=ins[t], dst_ref=lands[t].at[me], send_sem=sends[t].at[PEERS - 1], recv_sem=recvs[t].at[PEERS - 1],
                                         device_id=(x, y, 1 - c), device_id_type=MESH).start()
        outs[-1][...] = jnp.zeros(TOKEN_SHAPE.shape, F32)

    lands = [lax.empty((4,) + s.shape, s.dtype) for s in shards]
    out = pl.pallas_call(
        body, name=name, in_specs=[HBM] * (2 * nt) + after_specs, out_specs=[SEM] * (2 * nt) + [HBM] * (2 * nt) + [TOKEN],
        out_shape=[pltpu.SemaphoreType.DMA((PEERS,))] * (2 * nt)
        + [pltpu.HBM(s.shape, s.dtype) for s in shards] + [pltpu.HBM(l.shape, l.dtype) for l in lands] + [TOKEN_SHAPE],
        input_output_aliases={t: 2 * nt + t for t in range(2 * nt)}, compiler_params=_split_params())(
            *[_in_hbm(s) for s in shards], *[_in_hbm(l) for l in lands], *after)
    return out[:nt], out[nt:2 * nt], out[2 * nt:3 * nt], out[3 * nt:4 * nt], out[-1]


def _gather_wait(sends, recvs, shards, lands, after, name):
    nt = len(shards)

    def body(*refs):
        ins, lands_ref = refs[:nt], refs[nt:2 * nt]
        send_refs, recv_refs = refs[2 * nt:3 * nt], refs[3 * nt:4 * nt]
        x, y, c, chips = _place()
        for t in range(nt):
            h = ins[t].shape[0] // 2
            for j, (cx, cy) in enumerate(chips):
                for cs in range(2):
                    blk = lands_ref[t].at[2 * cx + cy, pl.ds(cs * h, h)]
                    pltpu.make_async_remote_copy(src_ref=blk, dst_ref=blk, send_sem=send_refs[t].at[2 * j + cs], recv_sem=recv_refs[t].at[2 * j + cs],
                                                 device_id=(cx, cy, cs), device_id_type=MESH).wait()
            blk = lands_ref[t].at[2 * x + y]
            pltpu.make_async_remote_copy(src_ref=blk, dst_ref=blk, send_sem=send_refs[t].at[PEERS - 1], recv_sem=recv_refs[t].at[PEERS - 1],
                                         device_id=(x, y, 1 - c), device_id_type=MESH).wait()

    out = pl.pallas_call(
        body, name=name, in_specs=[HBM] * (2 * nt) + [SEM] * (2 * nt) + [ANY], out_specs=[HBM] * (2 * nt),
        out_shape=[pltpu.HBM(s.shape, s.dtype) for s in shards] + [pltpu.HBM(l.shape, l.dtype) for l in lands],
        input_output_aliases={t: t for t in range(2 * nt)}, compiler_params=_split_params())(*shards, *lands, *sends, *recvs, after)
    return out[nt:]


def _gather_pieces_start(v, shard, name):
    h = shard.shape[0] // 2

    def body(v_ref, v_land, src, land, v_send, v_recv, send, recv, *rest):
        x, y, c, chips = _place()
        for j, flips in enumerate(_flips()):
            pltpu.make_async_remote_copy(src_ref=v_ref, dst_ref=v_land.at[4 * x + 2 * y + c], send_sem=v_send.at[j], recv_sem=v_recv.at[j],
                                         device_id=_flipped(x, y, c, flips), device_id_type=MESH).start()
        me = 2 * x + y
        mine = pl.ds(c * h, h)
        for j, (cx, cy) in enumerate(chips):
            pltpu.make_async_remote_copy(src_ref=src.at[mine], dst_ref=land.at[me, mine], send_sem=send.at[j], recv_sem=recv.at[j],
                                         device_id=(cx, cy, c), device_id_type=MESH).start()
        pltpu.make_async_remote_copy(src_ref=src, dst_ref=land.at[me], send_sem=send.at[len(chips)], recv_sem=recv.at[len(chips)],
                                     device_id=(x, y, 1 - c), device_id_type=MESH).start()
        rest[-1][...] = jnp.zeros(TOKEN_SHAPE.shape, F32)

    v_land = lax.empty((8,) + v.shape, v.dtype)
    land = lax.empty((4,) + shard.shape, shard.dtype)
    passed = [v, v_land, shard, land]
    out = pl.pallas_call(
        body, name=name, in_specs=[HBM] * 4, out_specs=[SEM] * 4 + [HBM] * 4 + [TOKEN],
        out_shape=[pltpu.SemaphoreType.DMA((PEERS,))] * 2 + [pltpu.SemaphoreType.DMA((4,))] * 2
        + [pltpu.HBM(a.shape, a.dtype) for a in passed] + [TOKEN_SHAPE],
        input_output_aliases={t: 4 + t for t in range(4)}, compiler_params=_split_params())(*[_in_hbm(a) for a in passed])
    return (out[0], out[1], out[4], out[5]), (out[2], out[3], out[6], out[7]), out[8]


def _gather_pieces_wait(send, recv, shard, land, after, name):
    h = shard.shape[0] // 2
    after, after_specs = _unread(after)

    def body(*refs):
        land_ref, send_ref, recv_ref = refs[1:4]
        x, y, c, chips = _place()
        for j, (cx, cy) in enumerate(chips):
            blk = land_ref.at[2 * cx + cy, pl.ds(c * h, h)]
            pltpu.make_async_remote_copy(src_ref=blk, dst_ref=blk, send_sem=send_ref.at[j], recv_sem=recv_ref.at[j],
                                         device_id=(cx, cy, c), device_id_type=MESH).wait()
        own = land_ref.at[2 * x + y]
        pltpu.make_async_remote_copy(src_ref=own, dst_ref=own, send_sem=send_ref.at[len(chips)], recv_sem=recv_ref.at[len(chips)],
                                     device_id=(x, y, 1 - c), device_id_type=MESH).wait()

    out = pl.pallas_call(
        body, name=name, in_specs=[HBM, HBM, SEM, SEM] + after_specs, out_specs=[HBM, HBM],
        out_shape=[pltpu.HBM(shard.shape, shard.dtype), pltpu.HBM(land.shape, land.dtype)],
        input_output_aliases={0: 0, 1: 1}, compiler_params=_split_params())(shard, land, send, recv, *after)
    return out[1]


def _pass_pieces(land, name):
    h = land.shape[1] // 2

    def body(_, out, send, recv):
        x, y, c, chips = _place()

        def piece(j, cc):
            cx, cy = chips[j]
            blk = out.at[2 * cx + cy, pl.ds(cc * h, h)]
            return pltpu.make_async_remote_copy(src_ref=blk, dst_ref=blk, send_sem=send.at[j], recv_sem=recv.at[j],
                                                device_id=(x, y, 1 - c), device_id_type=MESH)

        for j in range(len(chips)):
            piece(j, c).start()
        for j in range(len(chips)):
            piece(j, 1 - c).wait_recv()
        for j in range(len(chips)):
            piece(j, c).wait_send()

    return pl.pallas_call(
        body, name=name, in_specs=[ANY], out_specs=ANY, out_shape=SDS(land.shape, land.dtype), input_output_aliases={0: 0},
        scratch_shapes=[pltpu.SemaphoreType.DMA((3,))] * 2, compiler_params=pltpu.CompilerParams(has_side_effects=True))(land)


def _scatter_start(gs, name):
    nt = len(gs)

    def body(*refs):
        g_refs, lands = refs[:nt], refs[nt:2 * nt]
        sends, recvs = refs[2 * nt:3 * nt], refs[3 * nt:4 * nt]
        x, y, c, chips = _place()
        for g_ref, land, send, recv in zip(g_refs, lands, sends, recvs):
            h = land.shape[1]
            for j, (cx, cy) in enumerate(chips):
                for dc in range(2):
                    pltpu.make_async_remote_copy(src_ref=g_ref.at[2 * cx + cy, pl.ds(dc * h, h)], dst_ref=land.at[2 * j + c],
                                                 send_sem=send.at[2 * j + dc], recv_sem=recv.at[2 * j + c], device_id=(cx, cy, dc),
                                                 device_id_type=MESH).start()
            pltpu.make_async_remote_copy(src_ref=g_ref.at[2 * x + y, pl.ds((1 - c) * h, h)], dst_ref=land.at[PEERS - 1], send_sem=send.at[PEERS - 1],
                                         recv_sem=recv.at[PEERS - 1], device_id=(x, y, 1 - c), device_id_type=MESH).start()
        refs[-1][...] = jnp.zeros(TOKEN_SHAPE.shape, F32)

    lands = [lax.empty((PEERS, g.shape[1] // 2, g.shape[2]), g.dtype) for g in gs]
    out = pl.pallas_call(
        body, name=name, in_specs=[HBM] * (2 * nt), out_specs=[SEM] * (2 * nt) + [HBM] * (2 * nt) + [TOKEN],
        out_shape=[pltpu.SemaphoreType.DMA((PEERS,))] * (2 * nt) + [pltpu.HBM(a.shape, a.dtype) for a in gs + lands] + [TOKEN_SHAPE],
        input_output_aliases={t: 2 * nt + t for t in range(2 * nt)}, compiler_params=_split_params())(
            *[_in_hbm(a) for a in gs + lands])
    return [tuple(out[q * nt + t] for q in range(4)) for t in range(nt)], out[-1]


def _scatter_wait(started, after, name):
    nt = len(started)

    def body(*refs):
        lands = refs[nt:2 * nt]
        sends, recvs = refs[2 * nt:3 * nt], refs[3 * nt:4 * nt]
        x, y, c, chips = _place()
        peers = [(cx, cy, dc) for cx, cy in chips for dc in range(2)] + [(x, y, 1 - c)]
        for t in range(nt):
            for k, peer in enumerate(peers):
                blk = lands[t].at[k]
                pltpu.make_async_remote_copy(src_ref=blk, dst_ref=blk, send_sem=sends[t].at[k], recv_sem=recvs[t].at[k],
                                             device_id=peer, device_id_type=MESH).wait()

    gs, lands = [s[2] for s in started], [s[3] for s in started]
    after, after_specs = _unread(after)
    out = pl.pallas_call(
        body, name=name, in_specs=[HBM] * (2 * nt) + [SEM] * (2 * nt) + after_specs, out_specs=[HBM] * (2 * nt),
        out_shape=[pltpu.HBM(a.shape, a.dtype) for a in gs + lands],
        input_output_aliases={t: t for t in range(2 * nt)}, compiler_params=_split_params())(
            *gs, *lands, *[s[0] for s in started], *[s[1] for s in started], *after)
    return out[:nt], out[nt:]


def _join_start(bufs, name):
    nt = len(bufs)

    def body(*refs):
        send, recv = refs[nt:nt + 2]
        x, y, c, _ = _place()
        for t in range(nt):
            pltpu.make_async_remote_copy(src_ref=refs[t].at[c], dst_ref=refs[t].at[c], send_sem=send.at[t], recv_sem=recv.at[t],
                                         device_id=(x, y, 1 - c), device_id_type=MESH).start()
        refs[-1][...] = jnp.zeros(TOKEN_SHAPE.shape, F32)

    out = pl.pallas_call(
        body, name=name, in_specs=[HBM] * nt, out_specs=[SEM, SEM] + [HBM] * nt + [TOKEN],
        out_shape=[pltpu.SemaphoreType.DMA((nt,))] * 2 + [pltpu.HBM(b.shape, b.dtype) for b in bufs] + [TOKEN_SHAPE],
        input_output_aliases={t: 2 + t for t in range(nt)}, compiler_params=_split_params())(*[_in_hbm(b) for b in bufs])
    return out[0], out[1], out[2:2 + nt], out[-1]


def _join_wait(send, recv, bufs, after, name):
    nt = len(bufs)
    after, after_specs = _unread(after)

    def body(*refs):
        send_ref, recv_ref = refs[nt:nt + 2]
        x, y, c, _ = _place()
        for t in range(nt):
            theirs = refs[t].at[1 - c]
            pltpu.make_async_remote_copy(src_ref=theirs, dst_ref=theirs, send_sem=send_ref.at[t], recv_sem=recv_ref.at[t],
                                         device_id=(x, y, 1 - c), device_id_type=MESH).wait()

    return pl.pallas_call(
        body, name=name, in_specs=[HBM] * nt + [SEM, SEM] + after_specs, out_specs=[HBM] * nt,
        out_shape=[pltpu.HBM(b.shape, b.dtype) for b in bufs],
        input_output_aliases={t: t for t in range(nt)}, compiler_params=_split_params())(*bufs, send, recv, *after)


def _flips():
    return [(dx, dy, dc) for dx in range(2) for dy in range(2) for dc in range(2) if (dx, dy, dc) != (0, 0, 0)]


def _flipped(x, y, c, flips):
    dx, dy, dc = flips
    return (1 - x if dx else x, 1 - y if dy else y, 1 - c if dc else c)


def _small_start(v, name, after=()):
    after, after_specs = _unread(after)

    def body(v_ref, land, *rest):
        send, recv = rest[len(after):len(after) + 2]
        x, y, c, _ = _place()
        for j, flips in enumerate(_flips()):
            pltpu.make_async_remote_copy(src_ref=v_ref, dst_ref=land.at[4 * x + 2 * y + c], send_sem=send.at[j], recv_sem=recv.at[j],
                                         device_id=_flipped(x, y, c, flips), device_id_type=MESH).start()
        rest[-1][...] = jnp.zeros(TOKEN_SHAPE.shape, F32)

    land = lax.empty((8,) + v.shape, v.dtype)
    return pl.pallas_call(
        body, name=name, in_specs=[HBM, HBM] + after_specs, out_specs=[SEM, SEM, HBM, HBM, TOKEN],
        out_shape=[pltpu.SemaphoreType.DMA((PEERS,)), pltpu.SemaphoreType.DMA((PEERS,)), pltpu.HBM(v.shape, v.dtype),
                   pltpu.HBM(land.shape, land.dtype), TOKEN_SHAPE],
        input_output_aliases={0: 2, 1: 3}, compiler_params=_split_params())(_in_hbm(v), _in_hbm(land), *after)


def _small_wait(send, recv, v, land, after, name):
    after, after_specs = _unread(after)

    def body(v_ref, land_ref, send_ref, recv_ref, *rest):
        x, y, c, _ = _place()
        for j, flips in enumerate(_flips()):
            px, py, pc = _flipped(x, y, c, flips)
            blk = land_ref.at[4 * px + 2 * py + pc]
            pltpu.make_async_remote_copy(src_ref=blk, dst_ref=blk, send_sem=send_ref.at[j], recv_sem=recv_ref.at[j],
                                         device_id=(px, py, pc), device_id_type=MESH).wait()

    return pl.pallas_call(
        body, name=name, in_specs=[HBM, HBM, SEM, SEM] + after_specs, out_specs=[HBM, HBM],
        out_shape=[pltpu.HBM(v.shape, v.dtype), pltpu.HBM(land.shape, land.dtype)],
        input_output_aliases={0: 0, 1: 1}, compiler_params=_split_params())(v, land, send, recv, *after)


def _sum_small(v, land, name):
    def body(v_ref, land_ref, o_ref):
        x, y, c, _ = _place()
        me = 4 * x + 2 * y + c
        acc = jnp.where(me == 0, v_ref[...], land_ref[0])
        for d in range(1, 8):
            acc = acc + jnp.where(me == d, v_ref[...], land_ref[d])
        o_ref[...] = acc

    vm = pl.BlockSpec(memory_space=pltpu.VMEM)
    return pl.pallas_call(body, name=name, in_specs=[vm, vm], out_specs=vm, out_shape=SDS(v.shape, F32))(v, land)


SUM_STEPS = 2


def _sum_devices(gs, lands, me, core, name):
    nt = len(gs)

    def body(ix_ref, *refs):
        for own_ref, land_ref, o_ref in zip(refs[:nt], refs[nt:2 * nt], refs[2 * nt:]):
            acc = own_ref[0].astype(F32)
            for j in range(land_ref.shape[0]):
                acc = acc + land_ref[j].astype(F32)
            o_ref[0] = acc

    tiles = [(land.shape[1] // SUM_STEPS, land.shape[2]) for land in lands]
    assert all(land.shape[1] == tr * SUM_STEPS and tr % ROWS_PER_16BIT_TILE == 0 for land, (tr, _) in zip(lands, tiles)), name
    grid_spec = pltpu.PrefetchScalarGridSpec(
        num_scalar_prefetch=1, grid=(SUM_STEPS,),
        in_specs=[pl.BlockSpec((1, tr, c), lambda i, ix: (ix[0], ix[1] * SUM_STEPS + i, 0)) for tr, c in tiles]
        + [pl.BlockSpec((land.shape[0], tr, c), lambda i, ix: (0, i, 0)) for land, (tr, c) in zip(lands, tiles)],
        out_specs=[pl.BlockSpec((1, tr, c), lambda i, ix: (ix[1], i, 0)) for tr, c in tiles])
    return pl.pallas_call(body, name=name, grid_spec=grid_spec, out_shape=[SDS((2,) + land.shape[1:], F32) for land in lands],
                          compiler_params=_cp("parallel"))(jnp.stack([me, core]), *gs, *lands)


def _pack_small(parts):
    flat = jnp.concatenate([p.reshape(-1) for p in parts])
    total = flat.shape[0]
    rows = -(-total // 1024) * 8
    return jnp.pad(flat, (0, rows * 128 - total)).reshape(rows, 128)


def _pack_lanes(parts):
    cols, spans, off = [], [], 0
    for p in parts:
        size = int(np.prod(p.shape))
        width = -(-size // 128) * 128
        cols.append(jnp.pad(p.reshape(1, size), ((0, 0), (0, width - size))))
        spans.append((off, size))
        off += width
    return jnp.concatenate(cols, axis=1), spans


def _unpack_small(packed, shapes):
    flat = packed.reshape(-1)
    out, off = [], 0
    for s in shapes:
        size = int(np.prod(s))
        out.append(flat[off:off + size].reshape(s))
        off += size
    return out


def _local_step(x, mem, target, w_in_t, first_after, mid_weights, ffn_weights, on_grad, gains, conv_w, conv_b, hg_lb):
    n = x.shape[0]
    cos, sin = _rope_tables(n)
    seg = _hg_segments()
    gp, gn = _hg_pair_sums()
    masks = _hg_level_masks()
    gq2 = jnp.tile(gains["q_norm_g"], (1, 2))
    gk2 = jnp.tile(gains["k_norm_g"], (1, 2))
    a0 = hg_lb[:, 0:1, :]
    a1 = hg_lb[:, 1:2, :]

    p, h1 = _norm_mm(x, gains["pre_mix_g"], w_in_t, F32, TOKEN_TILE, 1664, "in_proj", after=(first_after,), w_turned=True)
    qr, kr = _qk_prep(p, gq2, gk2, cos, sin, "qk_prep")
    heads = lambda a: a.reshape(n, ATT_KV_HEADS, ATT_HEAD_DIM).transpose(1, 0, 2)
    kh = heads(kr)
    vh = heads(p[:, OFF_AV:OFF_AV + ATT_KV_DIM].astype(MXU_DTYPE))
    att = _attn_fwd(qr, kh, vh, "attn_fwd")
    o2, s0, hg_a, hg_e, hg_kept = _hgrn_fwd(p, a0, a1, seg, masks, "hgrn_fwd")
    rec = _hg_post(o2, p, gains["hg_out_norm_g"], "hg_post")
    cat = jnp.concatenate([att, rec], axis=1)
    w_out, w_xq, w_xkv, w_xo = mid_weights(cat)
    mixed, x1 = _mm_resid_norm(cat, w_out, x, gains["post_mix_g"], 512, "out_proj_resid")
    xq, h2 = _norm_mm(x1, gains["pre_x_g"], w_xq, MXU_DTYPE, TOKEN_TILE, 1024, "xq_proj")
    kv, mn = _norm_mm(mem, gains["mem_norm_g"], w_xkv, MXU_DTYPE, 256, 2048, "xkv_proj")
    ox = _xattn_fwd(xq, kv, "xattn_fwd")
    xo, x2 = _mm_resid_norm(ox, w_xo, x1, gains["post_x_g"], 512, "xo_proj_resid")
    w_up = ffn_weights("w_up", x2)
    u, h3 = _norm_mm(x2, gains["pre_ffn_g"], w_up, F32, TOKEN_TILE, 1408, "up_proj")
    act = _conv_gate(u, conv_w, conv_b, "conv_gate")
    w_down = ffn_weights("w_down", act)
    dn, d3, loss = _mm_resid_norm(act, w_down, x2, gains["post_ffn_g"], 512, "down_proj_resid_loss", target=target)

    gs = {}
    d_act, d_dn, gs["post_ffn_g"] = _norm_bwd_mm(dn, gains["post_ffn_g"], d3, w_down, F32, 512, 1408, "ffn_post_bwd_down_dx")
    tok = on_grad("w_down", _mm(act, d_dn, "tn", WIRE_DTYPE, 1408, 1024, "down_dw"))
    du_g, du_v, dcw_g, dcw_v, dcb_g, dcb_v = _conv_gate_bwd(u, conv_w, conv_b, d_act, "conv_gate_bwd", after=(tok,))
    gs["conv_w"] = jnp.concatenate([dcw_g, dcw_v], axis=1)
    gs["conv_b"] = jnp.concatenate([dcb_g, dcb_v], axis=1)
    ff_shard = w_up.shape[2]
    g_up = _dw_by_owner(h3, du_g, ff_shard, 0, None, 512, "up_dw_gate")
    tok = on_grad("w_up", _dw_by_owner(h3, du_v, ff_shard, 2, g_up, 512, "up_dw_value"))
    d2, gs["pre_ffn_g"] = _dx_norm_bwd([(du_g, 0), (du_g, 1), (du_v, 0), (du_v, 1)], w_up, x2, gains["pre_ffn_g"], d3, 512,
                                       "up_dx_pre_bwd", after=(tok,))
    d_ox, d_xo, gs["post_x_g"] = _norm_bwd_mm(xo, gains["post_x_g"], d2, w_xo, MXU_DTYPE, 512, 1024, "x_post_bwd_xo_dx")
    tok = on_grad("w_xo", _mm(ox, d_xo, "tn", WIRE_DTYPE, 512, 1024, "xo_dw"))
    d_xq, d_k, d_v = _xattn_bwd(xq, kv, d_ox, "xattn_bwd", after=(tok,))
    d_kv = jnp.concatenate([d_k, d_v], axis=1).astype(MXU_DTYPE)
    tok = on_grad("w_xq", _mm(h2, d_xq, "tn", WIRE_DTYPE, 512, 1024, "xq_dw"))
    tok_kv = on_grad("w_xkv", _dw_by_owner(mn, d_kv, w_xkv.shape[2], 0, None, 512, "xkv_dw"))
    d1, gs["pre_x_g"] = _dx_norm_bwd([(d_xq, 0)], w_xq[None], x1, gains["pre_x_g"], d2, 512, "xq_dx_pre_bwd", after=(tok, tok_kv))
    d_mn = _mm_nt_parts([(d_kv, s) for s in range(4)], w_xkv, F32, 256, 1024, "xkv_dx")
    _, gs["mem_norm_g"] = _norm_bwd(mem, gains["mem_norm_g"], d_mn, None, MXU_DTYPE, "mem_norm_bwd")
    d_cat, d_mixed, gs["post_mix_g"] = _norm_bwd_mm(mixed, gains["post_mix_g"], d1, w_out, MXU_DTYPE, 512, 1024, "mix_post_bwd_out_dx")
    tok = on_grad("w_out", _mm(cat, d_mixed, "tn", WIRE_DTYPE, 512, 1024, "out_dw"))
    d_o, d_hg, dg_hg = _hg_post_bwd(o2, p, gains["hg_out_norm_g"], d_cat, "hg_post_bwd", after=(tok,))
    gs["hg_out_norm_g"] = dg_hg.reshape(HG_HEADS, HG_HEAD_DIM).sum(axis=0, keepdims=True)
    dhq2, dz2, dhv2, dlb = _hgrn_bwd(p, a0, a1, masks, gp, gn, d_o, s0, hg_a, hg_e, hg_kept, "hgrn_bwd")
    lb = jax.nn.sigmoid(a0 - a1)
    da0 = dlb * lb * (1.0 - lb)
    gs["hg_lb"] = jnp.concatenate([da0, -da0], axis=1)
    d_qr, d_kh, d_vh = _attn_bwd(qr, kh, vh, cat, d_cat, "attn_bwd")
    unheads = lambda a: a.transpose(2, 0, 1).reshape(n, ATT_KV_DIM)
    d_aq, d_ak, dgq, dgk = _qk_prep_bwd(p, gq2, gk2, cos, sin, d_qr, unheads(d_kh), "qk_prep_bwd")
    gs["q_norm_g"] = dgq.reshape(ATT_HEADS, ATT_HEAD_DIM).sum(axis=0, keepdims=True)
    gs["k_norm_g"] = dgk.reshape(ATT_KV_HEADS, ATT_HEAD_DIM).sum(axis=0, keepdims=True)
    d_p = jnp.concatenate([d_aq, d_ak, unheads(d_vh).astype(MXU_DTYPE), (dhq2[0] + dhq2[1]).astype(MXU_DTYPE),
                           dz2[0].astype(MXU_DTYPE), dz2[1].astype(MXU_DTYPE), (dhv2[0] + dhv2[1]).astype(MXU_DTYPE), d_hg], axis=1)
    tok = on_grad("w_in", _mm(d_p, h1, "tn", WIRE_DTYPE, 1664, 1024, "in_dw"))
    grad_x, gs["pre_mix_g"] = _dx_norm_bwd([(d_p, 0)], w_in_t[None], x, gains["pre_mix_g"], d1, 512, "in_dx_pre_bwd", after=(tok,),
                                           b_turned=True)
    return loss, grad_x, gs


MATS = ("w_in", "w_out", "w_xq", "w_xkv", "w_xo", "w_up", "w_down")
GAINS = ("pre_mix_g", "q_norm_g", "k_norm_g", "hg_out_norm_g", "post_mix_g", "pre_x_g", "mem_norm_g", "post_x_g", "pre_ffn_g", "post_ffn_g")
WEIGHTS = ('pre_mix_g', 'w_in', 'q_norm_g', 'k_norm_g', 'hg_lb', 'hg_out_norm_g', 'w_out', 'post_mix_g', 'pre_x_g', 'mem_norm_g', 'w_xq',
           'w_xkv', 'w_xo', 'post_x_g', 'pre_ffn_g', 'w_up', 'conv_w', 'conv_b', 'w_down', 'post_ffn_g')


def kernel(x, mem, pre_mix_g, w_in, q_norm_g, k_norm_g, hg_lb, hg_out_norm_g, w_out, post_mix_g, pre_x_g, mem_norm_g, w_xq, w_xkv, w_xo, post_x_g, pre_ffn_g, w_up, conv_w, conv_b, w_down, post_ffn_g, loss_target, m_pre_mix_g, m_w_in, m_q_norm_g, m_k_norm_g, m_hg_lb, m_hg_out_norm_g, m_w_out, m_post_mix_g, m_pre_x_g, m_mem_norm_g, m_w_xq, m_w_xkv, m_w_xo, m_post_x_g, m_pre_ffn_g, m_w_up, m_conv_w, m_conv_b, m_w_down, m_post_ffn_g, v_pre_mix_g, v_w_in, v_q_norm_g, v_k_norm_g, v_hg_lb, v_hg_out_norm_g, v_w_out, v_post_mix_g, v_pre_x_g, v_mem_norm_g, v_w_xq, v_w_xkv, v_w_xo, v_post_x_g, v_pre_ffn_g, v_w_up, v_conv_w, v_conv_b, v_w_down, v_post_ffn_g):
    args = dict(locals())
    w = {k: args[k] for k in WEIGHTS}
    m = {k: args["m_" + k] for k in WEIGHTS}
    v = {k: args["v_" + k] for k in WEIGHTS}
    chip = 2 * lax.axis_index("x") + lax.axis_index("y")
    core = lax.axis_index("c")

    turned = ("w_in",)
    shards = {k: (jnp.swapaxes(w[k], 1, 2) if k in turned else w[k])[0].astype(WIRE_DTYPE) for k in MATS}

    def whole(k, g):
        return g if k in ("w_xkv", "w_up") else g.reshape(-1, g.shape[-1])

    mid_names, ffn_names = ("w_out", "w_xq", "w_xkv", "w_xo"), ("w_up", "w_down")
    small, w_in_pieces, token = _gather_pieces_start(_pack_small([w["conv_w"][0], w["hg_lb"]]), shards["w_in"], "gather_first_start")
    mid = _gather_start([shards[k] for k in mid_names], "gather_mid_start", after=(token,))
    ffn = _gather_start([shards[k] for k in ffn_names], "gather_ffn_start", after=(mid[4],))
    w_in_t = whole("w_in", _pass_pieces(_gather_pieces_wait(*w_in_pieces, (ffn[4],), "gather_w_in_wait"), "gather_w_in_pass"))
    mine, others = _small_wait(*small, (w_in_t,), "gather_small_wait")
    small_in = lax.dynamic_update_slice_in_dim(others, mine[None], 2 * chip + core, axis=0)

    def mid_weights(after):
        return [whole(k, g) for k, g in zip(mid_names, _gather_wait(*mid[:4], after, "gather_mid_wait"))]

    def ffn_weights(k, after):
        t = ffn_names.index(k)
        return whole(k, _gather_wait(*[part[t:t + 1] for part in ffn[:4]], after, "gather_wait_" + k)[0])

    cw_parts, lb_parts = [], []
    for s in range(4):
        cw_s, lb_s = _unpack_small(small_in[2 * s], [w["conv_w"][0].shape, w["hg_lb"].shape])
        cw_parts.append(cw_s)
        lb_parts.append(lb_s)
    conv_w_full = jnp.concatenate(cw_parts, axis=1)
    hg_lb_full = jnp.concatenate(lb_parts, axis=2)

    started, held = {}, {}
    leaves_with_next = ("w_down", "w_xo", "w_xq", "w_xkv")

    def on_grad(k, g):
        if g.ndim == 2:
            g = g.reshape(4, g.shape[0] // 4, g.shape[1])
        held[k] = g
        if k in leaves_with_next:
            return None
        names = tuple(held)
        per_tensor, token = _scatter_start([held.pop(n) for n in names], "grad_start_" + k)
        started.update(zip(names, per_tensor))
        return token

    gains = {k: w[k] for k in GAINS}
    loss_part, grad_x, gs = _local_step(x[0], mem[0], loss_target[0], w_in_t, ffn[4], mid_weights, ffn_weights, on_grad, gains,
                                        conv_w_full, w["conv_b"], hg_lb_full)
    gs["loss"] = loss_part

    grads, delta, new_m, new_v = {}, {}, {}, {}

    def sum_and_send(names, after, tag):
        sent, landed = _scatter_wait([started[k] for k in names], after, "grad_wait_" + tag)
        return _join_start(_sum_devices(sent, landed, chip, core, "grad_sum_" + tag), "grad_join_start_" + tag)

    def joined(names, join, after, tag):
        for k, r in zip(names, _join_wait(*join[:3], after, "grad_join_wait_" + tag)):
            grads[k] = r.reshape(1, -1, r.shape[-1])

    def adamw(names, tag):
        params, backs = [], []
        for k in names:
            shape = w[k].shape
            keep = len(shape) == 3
            if k in turned:
                view, back = (lambda a: jnp.swapaxes(a, 1, 2)), (lambda a: jnp.swapaxes(a, 1, 2))
                g = grads[k]
            else:
                view = (lambda a, shape=shape: a.reshape(shape)) if keep else (lambda a, shape=shape: a.reshape(-1, shape[-1]))
                back = lambda a, shape=shape: a.reshape(shape)
                g = view(grads[k])
            params.append((view(w[k]), g, view(m[k]), view(v[k])))
            backs.append(back)
        for k, back, (d, mo, vo, go) in zip(names, backs, _adamw(params, "adamw_" + tag)):
            delta[k], new_m[k], new_v[k], grads[k] = back(d), back(mo), back(vo), back(go)

    small_names = GAINS + ("conv_b", "conv_w", "hg_lb")
    packed, spans = _pack_lanes([gs[k] for k in small_names + ("loss",)])
    spans = dict(zip(small_names + ("loss",), spans))
    small = _small_start(packed, "reduce_small_start", after=(grad_x,))

    ffn_join = sum_and_send(ffn_names, (grad_x, small[4]), "ffn")
    mid_join = sum_and_send(mid_names, (ffn_join[3],), "mid")
    joined(ffn_names, ffn_join, (mid_join[3],), "ffn")
    adamw(ffn_names, "ffn")
    joined(mid_names, mid_join, tuple(new_v[k] for k in ffn_names), "mid")
    adamw(mid_names, "mid")
    early = mid_names + ffn_names
    w_in_join = sum_and_send(("w_in",), tuple(new_v[k] for k in early), "w_in")

    mine, others = _small_wait(*small[:4], (w_in_join[3],), "reduce_small_wait")
    reduced_small = _sum_small(mine, others, "reduce_small_sum")
    summed = lambda k: reduced_small[0, spans[k][0]:spans[k][0] + spans[k][1]].reshape(gs[k].shape)
    loss = summed("loss")[0, 0]
    ncw = w["conv_w"].shape[2]
    grads["conv_w"] = lax.dynamic_slice_in_dim(summed("conv_w"), chip * ncw, ncw, axis=1)[None]
    nlb = w["hg_lb"].shape[2]
    grads["hg_lb"] = lax.dynamic_slice_in_dim(summed("hg_lb"), chip * nlb, nlb, axis=2)
    adamw(("conv_w", "hg_lb"), "sharded_small")
    replicated = GAINS + ("conv_b",)
    outs = _adamw_rows(reduced_small, [spans[k] for k in replicated], [(w[k], m[k], v[k]) for k in replicated], "adamw_replicated")
    for k, (d, mo, vo, go) in zip(replicated, outs):
        delta[k], new_m[k], new_v[k], grads[k] = d, mo, vo, go

    joined(("w_in",), w_in_join, tuple(new_v[k] for k in small_names), "w_in")
    adamw(("w_in",), "w_in")
    return (loss, grad_x[None], *[grads[k] for k in WEIGHTS], *[delta[k] for k in WEIGHTS],
            *[new_m[k] for k in WEIGHTS], *[new_v[k] for k in WEIGHTS])
```

```python
import numpy as np
import jax
import jax.numpy as jnp
from jax import lax
from jax.experimental import pallas as pl
from jax.experimental.pallas import tpu as pltpu

F32 = jnp.float32
MXU_DTYPE = jnp.bfloat16
WIRE_DTYPE = jnp.bfloat16
VMEM_LIMIT_BYTES = 56 * 1024 * 1024
ROWS_PER_16BIT_TILE = 16
ELEMENTWISE_ROWS = 256
EPS = 1e-6
MESH = pl.DeviceIdType.MESH

GRID_W = 64
ATT_HEADS, ATT_KV_HEADS, ATT_HEAD_DIM = 8, 2, 64
ATT_GROUP = ATT_HEADS // ATT_KV_HEADS
ATT_Q_DIM, ATT_KV_DIM = 512, 128
ROPE_THETA = 10000.0
HG_HEADS, HG_HEAD_DIM, HG_DIM = 4, 128, 512
HG_CHUNK = 128
HG_LEVELS = 7
HG_PAIR = 2 * HG_HEAD_DIM
HG_KEPT = 7
X_HEADS, X_HEAD_DIM = 4, 256
D_FF = 2816
FF_COLS = 256
FF_BLOCKS = D_FF // FF_COLS
OFF_AK, OFF_AV, OFF_HQ, OFF_ZF, OFF_ZB, OFF_HI, OFF_HG = 512, 640, 768, 1280, 1792, 2304, 2816

ADAM_LR, ADAM_B1, ADAM_B2, ADAM_EPS, ADAM_WD, ADAM_STEP = 0.001, 0.9, 0.999, 1e-08, 0.01, 10

SDS = jax.ShapeDtypeStruct


def _cp(*sem):
    return pltpu.CompilerParams(dimension_semantics=sem, vmem_limit_bytes=VMEM_LIMIT_BYTES)


def _dot(a, b, form="nn"):
    dims = {"nn": (((1,), (0,)), ((), ())), "nt": (((1,), (1,)), ((), ())), "tn": (((0,), (0,)), ((), ()))}[form]
    return lax.dot_general(a.astype(MXU_DTYPE), b.astype(MXU_DTYPE), dims, preferred_element_type=F32)


def _sigmoid(x):
    return 1.0 / (1.0 + jnp.exp(-x))


def _rstd(x):
    return lax.rsqrt(jnp.mean(x * x, axis=-1, keepdims=True) + EPS)


def _rms_bwd(x, g, dy):
    r = _rstd(x)
    xh = x * r
    dn = dy * g
    dx = r * (dn - xh * jnp.mean(dn * xh, axis=-1, keepdims=True))
    return dx, jnp.sum(dy * xh, axis=0, keepdims=True)


def _unread(after):
    after = tuple(a for a in after if a is not None)
    return after, [pl.BlockSpec(memory_space=pl.ANY)] * len(after)


def _mm(a, b, form, out_dtype, tm, tn, name, after=()):
    after, after_specs = _unread(after)
    if form == "nn":
        (m, k), n = a.shape, b.shape[1]
    elif form == "nt":
        (m, k), n = a.shape, b.shape[0]
    else:
        (k, m), n = a.shape, b.shape[1]
    tm, tn = min(tm, m), min(tn, n)
    assert m % tm == 0 and n % tn == 0, (name, m, n, tm, tn)

    def body(a_ref, b_ref, *rest):
        o_ref = rest[-1]
        o_ref[...] = _dot(a_ref[...], b_ref[...], form).astype(o_ref.dtype)

    a_spec = pl.BlockSpec((k, tm), lambda i, j: (0, i)) if form == "tn" else pl.BlockSpec((tm, k), lambda i, j: (i, 0))
    b_spec = pl.BlockSpec((tn, k), lambda i, j: (j, 0)) if form == "nt" else pl.BlockSpec((k, tn), lambda i, j: (0, j))
    return pl.pallas_call(
        body, name=name, grid=(m // tm, n // tn), in_specs=[a_spec, b_spec] + after_specs,
        out_specs=pl.BlockSpec((tm, tn), lambda i, j: (i, j)), out_shape=SDS((m, n), out_dtype),
        compiler_params=_cp("parallel", "parallel"))(a, b, *after)


def _mm_nt_parts(a_parts, b, out_dtype, tm, tn, name, after=()):
    after, after_specs = _unread(after)
    parts, n, p = b.shape
    m = a_parts[0][0].shape[0]
    tm, tn = min(tm, m), min(tn, n)
    assert m % tm == 0 and n % tn == 0 and len(a_parts) == parts, (name, m, b.shape)

    def body(*refs):
        o_ref = refs[-1]
        acc = _dot(refs[0][...], refs[parts][0], "nt")
        for s in range(1, parts):
            acc = acc + _dot(refs[s][...], refs[parts + s][0], "nt")
        o_ref[...] = acc.astype(o_ref.dtype)

    a_specs = [pl.BlockSpec((tm, p), lambda i, j, cb=cb: (i, cb)) for _, cb in a_parts]
    b_specs = [pl.BlockSpec((1, tn, p), lambda i, j, s=s: (s, j, 0)) for s in range(parts)]
    return pl.pallas_call(
        body, name=name, grid=(m // tm, n // tn), in_specs=a_specs + b_specs + after_specs,
        out_specs=pl.BlockSpec((tm, tn), lambda i, j: (i, j)), out_shape=SDS((m, n), out_dtype),
        compiler_params=_cp("parallel", "parallel"))(*[arr for arr, _ in a_parts], *([b] * parts), *after)


def _norm_bwd_mm(y, g, d, w, out_dtype, tm, tn, name):
    n, dm = y.shape
    nn = w.shape[0]
    tm, tn = min(tm, n), min(tn, nn)
    assert n % tm == 0 and nn % tn == 0 and w.shape[1] == dm, (name, y.shape, w.shape)

    def body(y_ref, g_ref, d_ref, w_ref, dx_ref, dy_ref, dg_ref, dys):
        i, j = pl.program_id(0), pl.program_id(1)

        @pl.when(jnp.logical_and(i == 0, j == 0))
        def _():
            dg_ref[...] = jnp.zeros_like(dg_ref)

        @pl.when(j == 0)
        def _():
            dy, dg = _rms_bwd(y_ref[...], g_ref[...], d_ref[...])
            dy = dy.astype(MXU_DTYPE)
            dys[...] = dy
            dy_ref[...] = dy
            dg_ref[...] += dg

        dx_ref[...] = _dot(dys[...], w_ref[...], "nt").astype(dx_ref.dtype)

    row = pl.BlockSpec((tm, dm), lambda i, j: (i, 0))
    vec = pl.BlockSpec((1, dm), lambda i, j: (0, 0))
    return pl.pallas_call(
        body, name=name, grid=(n // tm, nn // tn), in_specs=[row, vec, row, pl.BlockSpec((tn, dm), lambda i, j: (j, 0))],
        out_specs=[pl.BlockSpec((tm, tn), lambda i, j: (i, j)), row, vec],
        out_shape=[SDS((n, nn), out_dtype), SDS((n, dm), MXU_DTYPE), SDS((1, dm), F32)],
        scratch_shapes=[pltpu.VMEM((tm, dm), MXU_DTYPE)],
        compiler_params=_cp("arbitrary", "arbitrary"))(y, g, d, w)


def _mm_resid_norm(a, b, x, g, tm, name, target=None):
    n, k = a.shape
    d = b.shape[1]
    tm = min(tm, n)
    assert n % tm == 0 and x.shape == (n, d), (name, a.shape, b.shape)
    with_loss = target is not None

    def body(a_ref, b_ref, x_ref, g_ref, *rest):
        y = _dot(a_ref[...], b_ref[...])
        out = x_ref[...] + y * _rstd(y) * g_ref[...]
        if not with_loss:
            y_ref, o_ref = rest
            y_ref[...] = y
            o_ref[...] = out
            return
        t_ref, y_ref, d_ref, l_ref = rest
        y_ref[...] = y
        diff = out - t_ref[...]
        d_ref[...] = diff * (1.0 / d)

        @pl.when(pl.program_id(0) == 0)
        def _():
            l_ref[...] = jnp.zeros_like(l_ref)

        l_ref[...] += 0.5 * jnp.sum(jnp.mean(diff * diff, axis=-1, keepdims=True), axis=0, keepdims=True)

    row = pl.BlockSpec((tm, d), lambda i: (i, 0))
    ins = [pl.BlockSpec((tm, k), lambda i: (i, 0)), pl.BlockSpec((k, d), lambda i: (0, 0)), row, pl.BlockSpec((1, d), lambda i: (0, 0))]
    out = SDS((n, d), F32)
    if with_loss:
        return pl.pallas_call(body, name=name, grid=(n // tm,), in_specs=ins + [row], out_specs=[row, row, pl.BlockSpec((1, 1), lambda i: (0, 0))],
                              out_shape=[out, out, SDS((1, 1), F32)], compiler_params=_cp("arbitrary"))(a, b, x, g, target)
    return pl.pallas_call(body, name=name, grid=(n // tm,), in_specs=ins, out_specs=[row, row], out_shape=[out, out],
                          compiler_params=_cp("parallel"))(a, b, x, g)


def _dx_norm_bwd(a_parts, b, x, g, res, tm, name, after=(), b_turned=False):
    after, after_specs = _unread(after)
    parts, d, p = (b.shape[0], b.shape[2], b.shape[1]) if b_turned else b.shape
    form = "nn" if b_turned else "nt"
    n = x.shape[0]
    tm = min(tm, n)
    assert n % tm == 0 and len(a_parts) == parts and x.shape[1] == d, (name, x.shape, b.shape)

    def body(*refs):
        x_ref, g_ref, res_ref = refs[2 * parts:2 * parts + 3]
        dx_ref, dg_ref = refs[-2:]
        dh = _dot(refs[0][...], refs[parts][0], form)
        for s in range(1, parts):
            dh = dh + _dot(refs[s][...], refs[parts + s][0], form)
        dx, dg = _rms_bwd(x_ref[...], g_ref[...], dh)
        dx_ref[...] = dx + res_ref[...]

        @pl.when(pl.program_id(0) == 0)
        def _():
            dg_ref[...] = jnp.zeros_like(dg_ref)

        dg_ref[...] += dg

    a_specs = [pl.BlockSpec((tm, p), lambda i, cb=cb: (i, cb)) for _, cb in a_parts]
    b_specs = [pl.BlockSpec((1,) + b.shape[1:], lambda i, s=s: (s, 0, 0)) for s in range(parts)]
    row = pl.BlockSpec((tm, d), lambda i: (i, 0))
    vec = pl.BlockSpec((1, d), lambda i: (0, 0))
    return pl.pallas_call(
        body, name=name, grid=(n // tm,), in_specs=a_specs + b_specs + [row, vec, row] + after_specs,
        out_specs=[row, vec], out_shape=[SDS((n, d), F32), SDS((1, d), F32)],
        compiler_params=_cp("arbitrary"))(*[arr for arr, _ in a_parts], *([b] * parts), x, g, res, *after)


def _dw_by_owner(a, b, tn, first, into, tm, name):
    k, m = a.shape
    cnt = b.shape[1] // tn
    tm = min(tm, m)
    assert m % tm == 0 and b.shape[1] == cnt * tn and first + cnt <= 4, (name, a.shape, b.shape)

    def body(a_ref, b_ref, *rest):
        rest[-1][0] = _dot(a_ref[...], b_ref[...], "tn").astype(rest[-1].dtype)

    extra = [] if into is None else [into]
    return pl.pallas_call(
        body, name=name, grid=(m // tm, cnt),
        in_specs=[pl.BlockSpec((k, tm), lambda i, j: (0, i)), pl.BlockSpec((k, tn), lambda i, j: (0, j))] + [pl.BlockSpec(memory_space=pl.ANY)] * len(extra),
        out_specs=pl.BlockSpec((1, tm, tn), lambda i, j: (first + j, i, 0)), out_shape=SDS((4, m, tn), WIRE_DTYPE),
        input_output_aliases={2: 0} if extra else {},
        compiler_params=_cp("parallel", "parallel"))(a, b, *extra)


def _norm_mm(x, g, w, out_dtype, tm, tn, name, after=(), w_turned=False):
    after, after_specs = _unread(after)
    m, d = x.shape
    sharded = w.ndim == 3
    n = w.shape[0] if w_turned else w.shape[-1] * (w.shape[0] if sharded else 1)
    tm, tn = min(tm, m), (w.shape[-1] if sharded else min(tn, n))
    assert m % tm == 0 and n % tn == 0 and not (sharded and w_turned), (name, m, n, tm, tn)

    def body(x_ref, g_ref, w_ref, *rest):
        o_ref, h_ref, hs = rest[-3:]

        @pl.when(pl.program_id(1) == 0)
        def _():
            xv = x_ref[...]
            h = (xv * _rstd(xv) * g_ref[...]).astype(MXU_DTYPE)
            hs[...] = h
            h_ref[...] = h

        o_ref[...] = _dot(hs[...], w_ref[0] if sharded else w_ref[...], "nt" if w_turned else "nn").astype(o_ref.dtype)

    if w_turned:
        w_spec = pl.BlockSpec((tn, d), lambda i, j: (j, 0))
    else:
        w_spec = pl.BlockSpec((1, d, tn), lambda i, j: (j, 0, 0)) if sharded else pl.BlockSpec((d, tn), lambda i, j: (0, j))
    return pl.pallas_call(
        body, name=name, grid=(m // tm, n // tn),
        in_specs=[pl.BlockSpec((tm, d), lambda i, j: (i, 0)), pl.BlockSpec((1, d), lambda i, j: (0, 0)), w_spec] + after_specs,
        out_specs=[pl.BlockSpec((tm, tn), lambda i, j: (i, j)), pl.BlockSpec((tm, d), lambda i, j: (i, 0))],
        out_shape=[SDS((m, n), out_dtype), SDS((m, d), MXU_DTYPE)],
        scratch_shapes=[pltpu.VMEM((tm, d), MXU_DTYPE)],
        compiler_params=_cp("parallel", "arbitrary"))(x, g, w, *after)


ROW_TILE = 512
TOKEN_TILE = 1024


def _norm_bwd(x, g, dy, res, out_dtype, name):
    n, d = x.shape
    tr = min(ROW_TILE, n)
    has_res = res is not None

    def body(*refs):
        x_ref, g_ref, dy_ref = refs[:3]
        dx_ref, dg_ref = refs[-2:]
        dx, dg = _rms_bwd(x_ref[...], g_ref[...], dy_ref[...].astype(F32))
        if has_res:
            dx = dx + refs[3][...]
        dx_ref[...] = dx.astype(dx_ref.dtype)

        @pl.when(pl.program_id(0) == 0)
        def _():
            dg_ref[...] = jnp.zeros_like(dg_ref)

        dg_ref[...] += dg

    row = pl.BlockSpec((tr, d), lambda i: (i, 0))
    vec = pl.BlockSpec((1, d), lambda i: (0, 0))
    ins = [x, g, dy] + ([res] if has_res else [])
    return pl.pallas_call(
        body, name=name, grid=(n // tr,), in_specs=[row, vec, row] + ([row] if has_res else []),
        out_specs=[row, vec], out_shape=[SDS((n, d), out_dtype), SDS((1, d), F32)],
        compiler_params=_cp("arbitrary"))(*ins)


def _rope_tables(n):
    pairs = ATT_HEAD_DIM // 4
    t = np.arange(n)
    inv = np.power(ROPE_THETA, -np.arange(pairs, dtype=np.float32) / pairs).astype(np.float32)
    ang = np.concatenate([(t // GRID_W)[:, None].astype(np.float32) * inv, (t % GRID_W)[:, None].astype(np.float32) * inv], axis=-1)
    cos = np.repeat(np.cos(ang), 2, axis=-1)
    sin = np.repeat(np.sin(ang), 2, axis=-1) * np.tile(np.array([-1.0, 1.0], np.float32), ATT_HEAD_DIM // 2)
    return jnp.asarray(np.tile(cos, 2), F32), jnp.asarray(np.tile(sin, 2), F32)


def _swap_pairs(x):
    lane = lax.broadcasted_iota(jnp.int32, x.shape, 1)
    return jnp.where((lane & 1) == 0, pltpu.roll(x, 127, axis=1), pltpu.roll(x, 1, axis=1))


def _head_mean(v):
    lane = lax.broadcasted_iota(jnp.int32, v.shape, 1)
    lo = jnp.where(lane < ATT_HEAD_DIM, v, 0.0)
    s0 = jnp.sum(lo, axis=-1, keepdims=True)
    s1 = jnp.sum(v - lo, axis=-1, keepdims=True)
    return jnp.where(lane < ATT_HEAD_DIM, s0, s1) * (1.0 / ATT_HEAD_DIM)


def _qk_prep(p, gq, gk, cos, sin, name):
    n = p.shape[0]
    tr = min(ROW_TILE, n)

    def one(xv, g, c, s):
        xn = xv * lax.rsqrt(_head_mean(xv * xv) + EPS) * g
        return xn * c + _swap_pairs(xn) * s

    def body(q_ref, k_ref, gq_ref, gk_ref, c_ref, s_ref, qo_ref, ko_ref):
        c, s = c_ref[...], s_ref[...]
        for j in range(ATT_Q_DIM // 128):
            qo_ref[:, j * 128:(j + 1) * 128] = one(q_ref[:, j * 128:(j + 1) * 128], gq_ref[...], c, s).astype(qo_ref.dtype)
        ko_ref[...] = one(k_ref[...], gk_ref[...], c, s).astype(ko_ref.dtype)

    vec = pl.BlockSpec((1, 128), lambda i: (0, 0))
    tab = pl.BlockSpec((tr, 128), lambda i: (i, 0))
    return pl.pallas_call(
        body, name=name, grid=(n // tr,),
        in_specs=[pl.BlockSpec((tr, ATT_Q_DIM), lambda i: (i, 0)), pl.BlockSpec((tr, 128), lambda i: (i, OFF_AK // 128)), vec, vec, tab, tab],
        out_specs=[pl.BlockSpec((tr, ATT_Q_DIM), lambda i: (i, 0)), tab],
        out_shape=[SDS((n, ATT_Q_DIM), MXU_DTYPE), SDS((n, ATT_KV_DIM), MXU_DTYPE)],
        compiler_params=_cp("parallel"))(p, p, gq, gk, cos, sin)


def _qk_prep_bwd(p, gq, gk, cos, sin, dq, dk, name):
    n = p.shape[0]
    tr = min(ROW_TILE, n)

    def one(xv, g, c, s, dout):
        dxn = dout * c + _swap_pairs(dout * s)
        r = lax.rsqrt(_head_mean(xv * xv) + EPS)
        xh = xv * r
        dn = dxn * g
        dx = r * (dn - xh * _head_mean(dn * xh))
        return dx, jnp.sum(dxn * xh, axis=0, keepdims=True)

    def body(q_ref, k_ref, gq_ref, gk_ref, c_ref, s_ref, dq_ref, dk_ref, dqo_ref, dko_ref, dgq_ref, dgk_ref):
        @pl.when(pl.program_id(0) == 0)
        def _():
            dgq_ref[...] = jnp.zeros_like(dgq_ref)
            dgk_ref[...] = jnp.zeros_like(dgk_ref)

        c, s = c_ref[...], s_ref[...]
        for j in range(ATT_Q_DIM // 128):
            sl = slice(j * 128, (j + 1) * 128)
            dx, dg = one(q_ref[:, sl], gq_ref[...], c, s, dq_ref[:, sl])
            dqo_ref[:, sl] = dx.astype(dqo_ref.dtype)
            dgq_ref[:, sl] += dg
        dx, dg = one(k_ref[...], gk_ref[...], c, s, dk_ref[...])
        dko_ref[...] = dx.astype(dko_ref.dtype)
        dgk_ref[...] += dg

    vec = pl.BlockSpec((1, 128), lambda i: (0, 0))
    tab = pl.BlockSpec((tr, 128), lambda i: (i, 0))
    qrow = pl.BlockSpec((tr, ATT_Q_DIM), lambda i: (i, 0))
    return pl.pallas_call(
        body, name=name, grid=(n // tr,),
        in_specs=[qrow, pl.BlockSpec((tr, 128), lambda i: (i, OFF_AK // 128)), vec, vec, tab, tab, qrow, tab],
        out_specs=[qrow, tab, pl.BlockSpec((1, ATT_Q_DIM), lambda i: (0, 0)), vec],
        out_shape=[SDS((n, ATT_Q_DIM), MXU_DTYPE), SDS((n, ATT_KV_DIM), MXU_DTYPE), SDS((1, ATT_Q_DIM), F32), SDS((1, 128), F32)],
        compiler_params=_cp("arbitrary"))(p, p, gq, gk, cos, sin, dq, dk)


ATT_FWD_STEP = (256, 4)
ATT_BWD_STEP = (512, 2)


def _attn_fwd(q, k, v, name):
    n = q.shape[0]
    tq, step_heads = min(ATT_FWD_STEP[0], n), ATT_FWD_STEP[1]
    scale = ATT_HEAD_DIM ** -0.5
    gw = step_heads * ATT_HEAD_DIM
    parts = ATT_GROUP // step_heads

    def body(q_ref, k_ref, v_ref, o_ref):
        kk, vv = k_ref[0], v_ref[0]
        v_ones = jnp.concatenate([vv, jnp.ones_like(vv)], axis=1)
        outs = []
        for g in range(step_heads):
            s = _dot(q_ref[:, g * ATT_HEAD_DIM:(g + 1) * ATT_HEAD_DIM] * scale, kk, "nt")
            e = jnp.exp(s - jnp.max(s, axis=-1, keepdims=True))
            ov = _dot(e, v_ones)
            outs.append(ov[:, :ATT_HEAD_DIM] / ov[:, ATT_HEAD_DIM:])
        o_ref[...] = jnp.concatenate(outs, axis=-1).astype(o_ref.dtype)

    kv = pl.BlockSpec((1, n, ATT_HEAD_DIM), lambda h, i, pr: (h, 0, 0))
    qb = pl.BlockSpec((tq, gw), lambda h, i, pr: (i, h * parts + pr))
    return pl.pallas_call(
        body, name=name, grid=(ATT_KV_HEADS, n // tq, parts), in_specs=[qb, kv, kv],
        out_specs=qb, out_shape=SDS((n, ATT_Q_DIM), MXU_DTYPE),
        compiler_params=_cp("parallel", "parallel", "parallel"))(q, k, v)


def _attn_bwd(q, k, v, o, do, name):
    n = q.shape[0]
    tq, step_heads = min(ATT_BWD_STEP[0], n), ATT_BWD_STEP[1]
    scale = ATT_HEAD_DIM ** -0.5
    gw = step_heads * ATT_HEAD_DIM
    parts = ATT_GROUP // step_heads

    def body(q_ref, k_ref, v_ref, o_ref, do_ref, dq_ref, dk_ref, dv_ref):
        @pl.when(jnp.logical_and(pl.program_id(1) == 0, pl.program_id(2) == 0))
        def _():
            dk_ref[...] = jnp.zeros_like(dk_ref)
            dv_ref[...] = jnp.zeros_like(dv_ref)

        kk, vv = k_ref[0], v_ref[0]
        dqs = []
        dk_acc = jnp.zeros((ATT_HEAD_DIM, n), F32)
        dv_acc = jnp.zeros((ATT_HEAD_DIM, n), F32)
        for g in range(step_heads):
            sl = slice(g * ATT_HEAD_DIM, (g + 1) * ATT_HEAD_DIM)
            qg, dog = q_ref[:, sl] * scale, do_ref[:, sl].astype(F32)
            s = _dot(qg, kk, "nt")
            e = jnp.exp(s - jnp.max(s, axis=-1, keepdims=True))
            inv = 1.0 / jnp.sum(e, axis=-1, keepdims=True)
            delta = jnp.sum(dog * o_ref[:, sl].astype(F32), axis=-1, keepdims=True)
            dse = (e * (_dot(dog, vv, "nt") - delta)).astype(MXU_DTYPE)
            dqs.append(_dot(dse, kk) * (inv * scale))
            dk_acc += _dot(qg.astype(F32) * inv, dse, "tn")
            dv_acc += _dot(dog * inv, e, "tn")
        dq_ref[...] = jnp.concatenate(dqs, axis=-1)
        dk_ref[0] += dk_acc
        dv_ref[0] += dv_acc

    kv = pl.BlockSpec((1, n, ATT_HEAD_DIM), lambda h, i, pr: (h, 0, 0))
    kvt = pl.BlockSpec((1, ATT_HEAD_DIM, n), lambda h, i, pr: (h, 0, 0))
    qb = pl.BlockSpec((tq, gw), lambda h, i, pr: (i, h * parts + pr))
    return pl.pallas_call(
        body, name=name, grid=(ATT_KV_HEADS, n // tq, parts), in_specs=[qb, kv, kv, qb, qb], out_specs=[qb, kvt, kvt],
        out_shape=[SDS((n, ATT_Q_DIM), F32), SDS((ATT_KV_HEADS, ATT_HEAD_DIM, n), F32), SDS((ATT_KV_HEADS, ATT_HEAD_DIM, n), F32)],
        compiler_params=_cp("parallel", "arbitrary", "arbitrary"))(q, k, v, o, do)


def _both_directions(mats, axis):
    fwd = np.concatenate(mats, axis=axis).astype(np.float32)
    bwd = np.concatenate([m[::-1, ::-1] for m in mats], axis=axis).astype(np.float32)
    return jnp.asarray(np.stack([fwd, bwd]), MXU_DTYPE)


def _hg_segments():
    c = HG_CHUNK
    t = np.arange(c)[:, None]
    r = np.arange(c)[None, :]
    mats = [(r <= t)]
    for lev in range(HG_LEVELS):
        h = c >> (lev + 1)
        mid = (t // (2 * h)) * (2 * h) + h - 1
        hi = (t // h) % 2 == 1
        mats.append(np.where(hi, (r > mid) & (r <= t), (r > t) & (r <= mid)))
    mats.append(r > t)
    return _both_directions(mats, 0)


def _hg_pair_sums():
    c = HG_CHUNK
    r = np.arange(c)[:, None]
    t = np.arange(c)[None, :]
    gp, gn = [t >= r], [t < r]
    for lev in range(HG_LEVELS):
        sh = HG_LEVELS - 1 - lev
        same = (r >> sh) == (t >> sh)
        gp.append(same & (t >= r))
        gn.append(same & (t < r))
    return _both_directions(gp, 1), _both_directions(gn, 1)


def _split_dot(mat, x):
    hi = x.astype(MXU_DTYPE)
    lo = (x - hi.astype(F32)).astype(MXU_DTYPE)
    return _dot(mat, hi) + _dot(mat, lo)


def _hg_gates(hq, z, a0, a1):
    q = hq * _sigmoid(hq)
    sg = _sigmoid(z)
    lb = _sigmoid(a0 - a1)
    f = lb + (1.0 - lb) * sg
    k = (1.0 - lb) * (1.0 - sg)
    return q, f, k, sg, lb


def _hg_level_masks():
    c = HG_CHUNK
    t = np.arange(c)
    later, same = [], []
    for lev in range(HG_LEVELS):
        sh = HG_LEVELS - 1 - lev
        later.append(np.broadcast_to((((t >> sh) & 1) == 1)[:, None], (c, HG_HEAD_DIM)))
        same.append((t[:, None] >> (sh + 1)) == (t[None, :] >> (sh + 1)))
    same.append(t[:, None] == t[None, :])
    later = np.stack(later).astype(np.float32)
    return jnp.asarray(np.stack([later, 1.0 - later]), F32), jnp.asarray(np.stack(same).astype(np.float32), F32)


def _hg_level(q, k, ex, later_ref, lev):
    e = ex[lev + 1]
    e_q = e * later_ref[0, lev]
    e_k = e - e_q
    return q * e_q, k * e_k, e_q, e_k


def _hg_intra(q, k, ex, later_ref, same_ref):
    a = same_ref[HG_LEVELS] * jnp.sum(q * k, axis=-1, keepdims=True)
    for lev in range(HG_LEVELS):
        qs, ks, _, _ = _hg_level(q, k, ex, later_ref, lev)
        a = a + same_ref[lev] * _dot(qs, ks, "nt")
    return a


def _hg_specs(n, with_time):
    c = HG_CHUNK
    nc = n // c

    def chunk(d, i):
        first = d if with_time else 1 - d
        return i + first * (nc - 1 - 2 * i)

    def pcols(off, dir_stride=0):
        return [pl.BlockSpec((c, HG_PAIR), lambda d, i, j=j: (chunk(d, i), off // HG_PAIR + dir_stride // HG_PAIR * d + j)) for j in range(2)]

    specs = dict(
        hq=pcols(OFF_HQ), v=pcols(OFF_HI), z=pcols(OFF_ZF, OFF_ZB - OFF_ZF),
        shared=pl.BlockSpec((c, HG_DIM), lambda d, i: (chunk(d, i), 0)),
        per_dir=pl.BlockSpec((1, c, HG_DIM), lambda d, i: (d, chunk(d, i), 0)),
        vec=pl.BlockSpec((1, 1, HG_DIM), lambda d, i: (d, 0, 0)),
        seg=pl.BlockSpec((1, (HG_LEVELS + 2) * c, c), lambda d, i: (d, 0, 0)),
        sums=pl.BlockSpec((1, c, (HG_LEVELS + 1) * c), lambda d, i: (d, 0, 0)),
        later=pl.BlockSpec((1, HG_LEVELS, c, HG_HEAD_DIM), lambda d, i: (d, 0, 0, 0)),
        same=pl.BlockSpec((HG_LEVELS + 1, c, c), lambda d, i: (0, 0, 0)),
        state=pl.BlockSpec((1, HG_HEADS, 1, HG_HEAD_DIM, HG_HEAD_DIM), lambda d, i: (d, 0, chunk(d, i), 0, 0)),
        weights=pl.BlockSpec((1, HG_HEADS, 1, c, c), lambda d, i: (d, 0, chunk(d, i), 0, 0)),
        levels=pl.BlockSpec((1, HG_HEADS, 1, HG_LEVELS, c, HG_HEAD_DIM), lambda d, i: (d, 0, chunk(d, i), 0, 0, 0)),
        kept=pl.BlockSpec((1, HG_KEPT, c, HG_DIM), lambda d, i: (d, 0, chunk(d, i), 0)))
    return nc, specs


def _hg_head(refs, hh):
    off = (hh % 2) * HG_HEAD_DIM
    return refs[hh // 2][:, off:off + HG_HEAD_DIM]


def _hg_lanes(hh):
    return slice(hh * HG_HEAD_DIM, (hh + 1) * HG_HEAD_DIM)


def _hg_exps(seg_ref, f):
    c = HG_CHUNK
    args = _split_dot(seg_ref[0], jnp.log(f))
    return [jnp.exp(args[j * c:(j + 1) * c]) for j in range(HG_LEVELS + 2)]


def _hg_last_row(a, mirrored):
    return jnp.where(mirrored, a[0:1, :], a[HG_CHUNK - 1:HG_CHUNK, :])


def _hgrn_fwd(p, a0, a1, seg, masks, name):
    n = p.shape[0]
    nc, sp = _hg_specs(n, True)

    def body(hq0, hq1, z0, z1, v0, v1, a0_ref, a1_ref, seg_ref, later_ref, same_ref, o_ref, s0_ref, a_ref, e_ref, g_ref, st):
        @pl.when(pl.program_id(1) == 0)
        def _():
            st[...] = jnp.zeros_like(st)

        mirrored = pl.program_id(0) == 1
        for hh in range(HG_HEADS):
            ln = _hg_lanes(hh)
            hqv = _hg_head((hq0, hq1), hh)
            q, f, k, sg, _ = _hg_gates(hqv, _hg_head((z0, z1), hh), a0_ref[0, :, ln], a1_ref[0, :, ln])
            vv = _hg_head((v0, v1), hh)
            ex = _hg_exps(seg_ref, f)
            for lev in range(HG_LEVELS):
                e_ref[0, hh, 0, lev] = ex[lev + 1].astype(e_ref.dtype)
            sq = _sigmoid(hqv)
            for j, kept in enumerate((q, k, f, sg, sq * (1.0 + hqv * (1.0 - sq)), ex[0], ex[HG_LEVELS + 1])):
                g_ref[0, j, :, ln] = kept
            a = _hg_intra(q, k, ex, later_ref, same_ref).astype(MXU_DTYPE)
            a_ref[0, hh, 0] = a
            s_t = st[hh]
            s0_ref[0, hh, 0] = s_t
            o_ref[0, :, ln] = _dot(a, vv) + _dot(q * ex[0], s_t, "nt")
            st[hh] = s_t * _hg_last_row(ex[0], mirrored) + _dot(vv, k * ex[HG_LEVELS + 1], "tn")

    return pl.pallas_call(
        body, name=name, grid=(2, nc), in_specs=sp["hq"] + sp["z"] + sp["v"] + [sp["vec"], sp["vec"], sp["seg"], sp["later"], sp["same"]],
        out_specs=[sp["per_dir"], sp["state"], sp["weights"], sp["levels"], sp["kept"]],
        out_shape=[SDS((2, n, HG_DIM), F32), SDS((2, HG_HEADS, nc, HG_HEAD_DIM, HG_HEAD_DIM), F32),
                   SDS((2, HG_HEADS, nc, HG_CHUNK, HG_CHUNK), MXU_DTYPE),
                   SDS((2, HG_HEADS, nc, HG_LEVELS, HG_CHUNK, HG_HEAD_DIM), MXU_DTYPE), SDS((2, HG_KEPT, n, HG_DIM), F32)],
        scratch_shapes=[pltpu.VMEM((HG_HEADS, HG_HEAD_DIM, HG_HEAD_DIM), F32)],
        compiler_params=_cp("parallel", "arbitrary"))(p, p, p, p, p, p, a0, a1, seg, *masks)


def _hgrn_bwd(p, a0, a1, masks, gp, gn, do, s0, a, e, kept, name):
    n = p.shape[0]
    nc, sp = _hg_specs(n, False)


    def body(v0, v1, a0_ref, a1_ref, later_ref, same_ref, gp_ref, gn_ref, do_ref, s0_ref, a_ref, e_ref, g_ref,
             dhq_ref, dz_ref, dv_ref, dlb_ref, rt):
        @pl.when(pl.program_id(1) == 0)
        def _():
            rt[...] = jnp.zeros_like(rt)
            dlb_ref[...] = jnp.zeros_like(dlb_ref)

        mirrored = pl.program_id(0) == 1
        for hh in range(HG_HEADS):
            ln = _hg_lanes(hh)
            q, k, f, sg, dsilu, e_first, e_last = (g_ref[0, j, :, ln] for j in range(HG_KEPT))
            lb = _sigmoid(a0_ref[0, :, ln] - a1_ref[0, :, ln])
            vv, dov = _hg_head((v0, v1), hh), do_ref[:, ln]
            ex = [e_first] + [e_ref[0, hh, 0, lev].astype(F32) for lev in range(HG_LEVELS)] + [e_last]
            a = a_ref[0, hh, 0]
            da = _dot(dov, vv, "nt")
            diag = jnp.sum(dov * vv, axis=-1, keepdims=True)
            s_t = s0_ref[0, hh, 0]
            r_t = rt[hh]
            k_end = k * ex[HG_LEVELS + 1]
            dv_ref[0, :, ln] = _dot(a, dov, "tn") + _dot(k_end, r_t, "nt")
            dq_inter = ex[0] * _dot(dov, s_t)
            dk_inter = ex[HG_LEVELS + 1] * _dot(vv, r_t)
            dq = diag * k + dq_inter
            dk = diag * q + dk_inter
            q_terms, k_terms = [q * dq_inter], [k * dk_inter]
            for lev in range(HG_LEVELS):
                qs, ks, e_q, e_k = _hg_level(q, k, ex, later_ref, lev)
                pairs = da * same_ref[lev]
                q_part = e_q * _dot(pairs, ks)
                k_part = e_k * _dot(pairs, qs, "tn")
                dq, dk = dq + q_part, dk + k_part
                q_terms.append(q * q_part)
                k_terms.append(k * k_part)
            decay = _hg_last_row(ex[0], mirrored)
            rt[hh] = r_t * decay + _dot(dov, q * ex[0], "tn")
            later = decay * jnp.sum(s_t * r_t, axis=0, keepdims=True)
            dlf = _dot(gp_ref[0], jnp.concatenate(q_terms, axis=0)) + _dot(gn_ref[0], jnp.concatenate(k_terms, axis=0)) + later
            df = dlf / f - dk
            dz_ref[0, :, ln] = df * (1.0 - lb) * sg * (1.0 - sg)
            dlb_ref[0, :, ln] += jnp.sum(df * (1.0 - sg), axis=0, keepdims=True)
            dhq_ref[0, :, ln] = dq * dsilu

    out = SDS((2, n, HG_DIM), F32)
    return pl.pallas_call(
        body, name=name, grid=(2, nc),
        in_specs=sp["v"] + [sp["vec"], sp["vec"], sp["later"], sp["same"], sp["sums"], sp["sums"],
                            sp["shared"], sp["state"], sp["weights"], sp["levels"], sp["kept"]],
        out_specs=[sp["per_dir"], sp["per_dir"], sp["per_dir"], sp["vec"]], out_shape=[out, out, out, SDS((2, 1, HG_DIM), F32)],
        scratch_shapes=[pltpu.VMEM((HG_HEADS, HG_HEAD_DIM, HG_HEAD_DIM), F32)],
        compiler_params=_cp("parallel", "arbitrary"))(p, p, a0, a1, *masks, gp, gn, do, s0, a, e, kept)


def _hg_post(o2, p, g, name):
    n = p.shape[0]
    tr = min(ROW_TILE, n)
    w = 2 * HG_HEAD_DIM

    def body(of_ref, ob_ref, hg_ref, g_ref, o_ref):
        for j in range(2):
            sl = slice(j * HG_HEAD_DIM, (j + 1) * HG_HEAD_DIM)
            o = of_ref[0, :, sl] + ob_ref[0, :, sl]
            hg = hg_ref[:, sl]
            o_ref[:, sl] = (o * _rstd(o) * g_ref[...] * (hg * _sigmoid(hg))).astype(o_ref.dtype)

    blk = pl.BlockSpec((tr, w), lambda i, j: (i, j))
    dirs = [pl.BlockSpec((1, tr, w), lambda i, j, d=d: (d, i, j)) for d in range(2)]
    return pl.pallas_call(
        body, name=name, grid=(n // tr, HG_DIM // w),
        in_specs=dirs + [pl.BlockSpec((tr, w), lambda i, j: (i, OFF_HG // w + j)), pl.BlockSpec((1, HG_HEAD_DIM), lambda i, j: (0, 0))],
        out_specs=blk, out_shape=SDS((n, HG_DIM), MXU_DTYPE), compiler_params=_cp("parallel", "parallel"))(o2, o2, p, g)


def _hg_post_bwd(o2, p, g, dcat, name, after=()):
    n = p.shape[0]
    tr = min(ROW_TILE, n)
    w = 2 * HG_HEAD_DIM
    after, after_specs = _unread(after)

    def body(of_ref, ob_ref, hg_ref, g_ref, d_ref, *rest):
        do_ref, dhg_ref, dg_ref = rest[len(after):]

        @pl.when(pl.program_id(1) == 0)
        def _():
            dg_ref[...] = jnp.zeros_like(dg_ref)

        for j in range(2):
            sl = slice(j * HG_HEAD_DIM, (j + 1) * HG_HEAD_DIM)
            o = of_ref[0, :, sl] + ob_ref[0, :, sl]
            hg = hg_ref[:, sl]
            d = d_ref[:, sl].astype(F32)
            sg = _sigmoid(hg)
            on = o * _rstd(o) * g_ref[...]
            dhg_ref[:, sl] = (d * on * sg * (1.0 + hg * (1.0 - sg))).astype(dhg_ref.dtype)
            dx, dg = _rms_bwd(o, g_ref[...], d * hg * sg)
            do_ref[:, sl] = dx
            dg_ref[0, :, sl] += dg

    blk = pl.BlockSpec((tr, w), lambda j, i: (i, j))
    dirs = [pl.BlockSpec((1, tr, w), lambda j, i, d=d: (d, i, j)) for d in range(2)]
    return pl.pallas_call(
        body, name=name, grid=(HG_DIM // w, n // tr),
        in_specs=dirs + [pl.BlockSpec((tr, w), lambda j, i: (i, OFF_HG // w + j)), pl.BlockSpec((1, HG_HEAD_DIM), lambda j, i: (0, 0)),
                         pl.BlockSpec((tr, w), lambda j, i: (i, ATT_Q_DIM // w + j))] + after_specs,
        out_specs=[blk, blk, pl.BlockSpec((1, 1, w), lambda j, i: (j, 0, 0))],
        out_shape=[SDS((n, HG_DIM), F32), SDS((n, HG_DIM), MXU_DTYPE), SDS((HG_DIM // w, 1, w), F32)],
        compiler_params=_cp("parallel", "arbitrary"))(o2, o2, p, g, dcat, *after)


XATT_TQ = 512


def _xattn_fwd(q, kv, name):
    n, nm = q.shape[0], kv.shape[0]
    tq = min(XATT_TQ, n)
    scale = X_HEAD_DIM ** -0.5

    def body(q_ref, k_ref, v_ref, o_ref):
        s = _dot(q_ref[...], k_ref[...], "nt") * scale
        e = jnp.exp(s - jnp.max(s, axis=-1, keepdims=True))
        o_ref[...] = _dot(e / jnp.sum(e, axis=-1, keepdims=True), v_ref[...]).astype(o_ref.dtype)

    qb = pl.BlockSpec((tq, X_HEAD_DIM), lambda h, i: (i, h))
    return pl.pallas_call(
        body, name=name, grid=(X_HEADS, n // tq),
        in_specs=[qb, pl.BlockSpec((nm, X_HEAD_DIM), lambda h, i: (0, h)), pl.BlockSpec((nm, X_HEAD_DIM), lambda h, i: (0, X_HEADS + h))],
        out_specs=qb, out_shape=SDS(q.shape, MXU_DTYPE), compiler_params=_cp("parallel", "parallel"))(q, kv, kv)


def _xattn_bwd(q, kv, do, name, after=()):
    n, nm = q.shape[0], kv.shape[0]
    tq = min(XATT_TQ, n)
    scale = X_HEAD_DIM ** -0.5
    after, after_specs = _unread(after)

    def body(q_ref, k_ref, v_ref, do_ref, *rest):
        dq_ref, dk_ref, dv_ref = rest[len(after):]

        @pl.when(pl.program_id(1) == 0)
        def _():
            dk_ref[...] = jnp.zeros_like(dk_ref)
            dv_ref[...] = jnp.zeros_like(dv_ref)

        qv, dov = q_ref[...], do_ref[...]
        s = _dot(qv, k_ref[...], "nt") * scale
        e = jnp.exp(s - jnp.max(s, axis=-1, keepdims=True))
        p = e / jnp.sum(e, axis=-1, keepdims=True)
        dp = _dot(dov, v_ref[...], "nt")
        ds = p * (dp - jnp.sum(p * dp, axis=-1, keepdims=True)) * scale
        dq_ref[...] = _dot(ds, k_ref[...]).astype(dq_ref.dtype)
        dk_ref[...] += _dot(ds, qv, "tn")
        dv_ref[...] += _dot(p, dov, "tn")

    qb = pl.BlockSpec((tq, X_HEAD_DIM), lambda h, i: (i, h))
    kb = pl.BlockSpec((nm, X_HEAD_DIM), lambda h, i: (0, h))
    return pl.pallas_call(
        body, name=name, grid=(X_HEADS, n // tq),
        in_specs=[qb, kb, pl.BlockSpec((nm, X_HEAD_DIM), lambda h, i: (0, X_HEADS + h)), qb] + after_specs, out_specs=[qb, kb, kb],
        out_shape=[SDS(q.shape, MXU_DTYPE), SDS((nm, X_HEADS * X_HEAD_DIM), F32), SDS((nm, X_HEADS * X_HEAD_DIM), F32)],
        compiler_params=_cp("parallel", "arbitrary"))(q, kv, kv, do, *after)


def _edge_rows(shape):
    row = lax.broadcasted_iota(jnp.int32, shape, 0)
    return row == 0, row == shape[0] - 1


def _shift_rows(u, down, edges):
    if down:
        return jnp.where(edges[0], 0.0, pltpu.roll(u, 1, axis=0))
    return jnp.where(edges[1], 0.0, pltpu.roll(u, u.shape[0] - 1, axis=0))


def _conv(u, w, b, edges):
    return b + _shift_rows(u, True, edges) * w[0:1, :] + u * w[1:2, :] + _shift_rows(u, False, edges) * w[2:3, :]


def _ff_specs(n):
    gate = lambda rows: pl.BlockSpec((rows, FF_COLS), lambda j: (0, j))
    val = lambda rows: pl.BlockSpec((rows, FF_COLS), lambda j: (0, FF_BLOCKS + j))
    return [gate(n), val(n), gate(3), val(3), gate(1), val(1)], gate


def _conv_gate(u, cw, cb, name):
    n = u.shape[0]
    ins, gate_blk = _ff_specs(n)

    def body(ug_ref, uv_ref, wg_ref, wv_ref, bg_ref, bv_ref, o_ref):
        edges = _edge_rows(ug_ref.shape)
        gate = _conv(ug_ref[...], wg_ref[...], bg_ref[...], edges)
        val = _conv(uv_ref[...], wv_ref[...], bv_ref[...], edges)
        o_ref[...] = (gate * _sigmoid(gate) * val).astype(o_ref.dtype)

    return pl.pallas_call(
        body, name=name, grid=(FF_BLOCKS,), in_specs=ins, out_specs=gate_blk(n), out_shape=SDS((n, D_FF), MXU_DTYPE),
        compiler_params=_cp("parallel"))(u, u, cw, cw, cb, cb)


def _conv_gate_bwd(u, cw, cb, da, name, after=()):
    n = u.shape[0]
    ins, gate_blk = _ff_specs(n)
    after, after_specs = _unread(after)

    def side(dacc, u, w, edges, du_ref, dw_ref, db_ref):
        nxt, prv = _shift_rows(dacc, False, edges), _shift_rows(dacc, True, edges)
        du_ref[...] = (nxt * w[0:1, :] + dacc * w[1:2, :] + prv * w[2:3, :]).astype(du_ref.dtype)
        db_ref[...] = jnp.sum(dacc, axis=0, keepdims=True)
        dw_ref[0:1, :] = jnp.sum(nxt * u, axis=0, keepdims=True)
        dw_ref[1:2, :] = jnp.sum(dacc * u, axis=0, keepdims=True)
        dw_ref[2:3, :] = jnp.sum(prv * u, axis=0, keepdims=True)

    def body(ug_ref, uv_ref, wg_ref, wv_ref, bg_ref, bv_ref, da_ref, *rest):
        dug_ref, duv_ref, dwg_ref, dwv_ref, dbg_ref, dbv_ref = rest[len(after):]
        ug, uv = ug_ref[...], uv_ref[...]
        edges = _edge_rows(ug.shape)
        gate = _conv(ug, wg_ref[...], bg_ref[...], edges)
        val = _conv(uv, wv_ref[...], bv_ref[...], edges)
        sg = _sigmoid(gate)
        dav = da_ref[...].astype(F32)
        side(dav * val * sg * (1.0 + gate * (1.0 - sg)), ug, wg_ref[...], edges, dug_ref, dwg_ref, dbg_ref)
        side(dav * gate * sg, uv, wv_ref[...], edges, duv_ref, dwv_ref, dbv_ref)

    return pl.pallas_call(
        body, name=name, grid=(FF_BLOCKS,), in_specs=ins + [gate_blk(n)] + after_specs,
        out_specs=[gate_blk(n), gate_blk(n), gate_blk(3), gate_blk(3), gate_blk(1), gate_blk(1)],
        out_shape=[SDS((n, D_FF), MXU_DTYPE)] * 2 + [SDS((3, D_FF), F32)] * 2 + [SDS((1, D_FF), F32)] * 2,
        compiler_params=_cp("parallel"))(u, u, cw, cw, cb, cb, da, *after)


def _adamw_update(w_ref, gv, m_ref, v_ref, d_ref, mo_ref, vo_ref, go_ref):
    go_ref[...] = gv
    mn = ADAM_B1 * m_ref[...] + (1.0 - ADAM_B1) * gv
    vn = ADAM_B2 * v_ref[...] + (1.0 - ADAM_B2) * gv * gv
    m_hat = mn / (1.0 - ADAM_B1 ** ADAM_STEP)
    v_hat = vn / (1.0 - ADAM_B2 ** ADAM_STEP)
    d_ref[...] = -ADAM_LR * (m_hat / (jnp.sqrt(v_hat) + ADAM_EPS) + ADAM_WD * w_ref[...])
    mo_ref[...] = mn
    vo_ref[...] = vn


def _adamw_rows(summed, spans, params, name):
    nt = len(params)

    def body(s_ref, *refs):
        for t, (off, d) in enumerate(spans):
            w_ref, m_ref, v_ref = refs[3 * t:3 * t + 3]
            _adamw_update(w_ref, s_ref[:, off:off + d], m_ref, v_ref, *refs[3 * nt + 4 * t:3 * nt + 4 * t + 4])

    assert all(p[0].shape == (1, d) for p, (_, d) in zip(params, spans)), name
    vm = pl.BlockSpec(memory_space=pltpu.VMEM)
    out = pl.pallas_call(body, name=name, in_specs=[vm] * (1 + 3 * nt), out_specs=[vm] * (4 * nt),
                         out_shape=[SDS(p[0].shape, F32) for p in params for _ in range(4)])(summed, *[a for p in params for a in p])
    return [tuple(out[4 * t:4 * t + 4]) for t in range(nt)]


def _adamw(params, name):
    nt = len(params)
    rows = [p[0].shape[-2] for p in params]
    fits = lambda s: max(rows) <= s * ELEMENTWISE_ROWS and all(r % s == 0 and (s == 1 or r // s % 8 == 0) for r in rows)
    steps = next(s for s in range(1, max(rows) + 1) if fits(s))
    assert all(p[0].ndim == 2 or p[0].shape[:-2] == (1,) or (p[0].ndim == 3 and steps == 1) for p in params), name

    def body(*refs):
        for t in range(nt):
            w_ref, g_ref, m_ref, v_ref = refs[4 * t:4 * t + 4]
            _adamw_update(w_ref, g_ref[...], m_ref, v_ref, *refs[4 * (nt + t):4 * (nt + t) + 4])

    def blk(w):
        tr, c = w.shape[-2] // steps, w.shape[-1]
        return pl.BlockSpec((tr, c), lambda i: (i, 0)) if w.ndim == 2 else pl.BlockSpec((w.shape[0], tr, c), lambda i: (0, i, 0))

    specs = [blk(p[0]) for p in params for _ in range(4)]
    out = pl.pallas_call(body, name=name, grid=(steps,), in_specs=specs, out_specs=specs,
                         out_shape=[SDS(p[0].shape, F32) for p in params for _ in range(4)],
                         compiler_params=_cp("parallel"))(*[a for p in params for a in p])
    return [tuple(out[4 * t:4 * t + 4]) for t in range(nt)]


ANY = pl.BlockSpec(memory_space=pl.ANY)


def _place():
    x, y, c = lax.axis_index("x"), lax.axis_index("y"), lax.axis_index("c")
    return x, y, c, [(1 - x, y), (x, 1 - y), (1 - x, 1 - y)]


HBM = pl.BlockSpec(memory_space=pltpu.HBM)
SEM = pl.BlockSpec(memory_space=pltpu.SEMAPHORE)
TOKEN = pl.BlockSpec(memory_space=pltpu.VMEM)
TOKEN_SHAPE = SDS((8, 128), F32)
PEERS = 7


def _in_hbm(a):
    return pltpu.with_memory_space_constraint(a, pltpu.HBM)


def _split_params():
    return pltpu.CompilerParams(has_side_effects=pltpu.SideEffectType.DATAFLOW_SIDE_EFFECTING)


def _gather_start(shards, name, after=()):
    nt = len(shards)
    after, after_specs = _unread(after)

    def body(*refs):
        ins, lands = refs[:nt], refs[nt:2 * nt]
        outs = refs[2 * nt + len(after):]
        sends, recvs = outs[:nt], outs[nt:2 * nt]
        x, y, c, chips = _place()
        me = 2 * x + y
        for t in range(nt):
            h = ins[t].shape[0] // 2
            mine = pl.ds(c * h, h)
            for j, (cx, cy) in enumerate(chips):
                for dc in range(2):
                    pltpu.make_async_remote_copy(src_ref=ins[t].at[mine], dst_ref=lands[t].at[me, mine], send_sem=sends[t].at[2 * j + dc],
                                                 recv_sem=recvs[t].at[2 * j + c], device_id=(cx, cy, dc), device_id_type=MESH).start()
            pltpu.make_async_remote_copy(src_ref=ins[t], dst_ref=lands[t].at[me], send_sem=sends[t].at[PEERS - 1], recv_sem=recvs[t].at[PEERS - 1],
                                         device_id=(x, y, 1 - c), device_id_type=MESH).start()
        outs[-1][...] = jnp.zeros(TOKEN_SHAPE.shape, F32)

    lands = [lax.empty((4,) + s.shape, s.dtype) for s in shards]
    out = pl.pallas_call(
        body, name=name, in_specs=[HBM] * (2 * nt) + after_specs, out_specs=[SEM] * (2 * nt) + [HBM] * (2 * nt) + [TOKEN],
        out_shape=[pltpu.SemaphoreType.DMA((PEERS,))] * (2 * nt)
        + [pltpu.HBM(s.shape, s.dtype) for s in shards] + [pltpu.HBM(l.shape, l.dtype) for l in lands] + [TOKEN_SHAPE],
        input_output_aliases={t: 2 * nt + t for t in range(2 * nt)}, compiler_params=_split_params())(
            *[_in_hbm(s) for s in shards], *[_in_hbm(l) for l in lands], *after)
    return out[:nt], out[nt:2 * nt], out[2 * nt:3 * nt], out[3 * nt:4 * nt], out[-1]


def _gather_wait(sends, recvs, shards, lands, after, name):
    nt = len(shards)

    def body(*refs):
        ins, lands_ref = refs[:nt], refs[nt:2 * nt]
        send_refs, recv_refs = refs[2 * nt:3 * nt], refs[3 * nt:4 * nt]
        x, y, c, chips = _place()
        for t in range(nt):
            h = ins[t].shape[0] // 2
            for j, (cx, cy) in enumerate(chips):
                for cs in range(2):
                    blk = lands_ref[t].at[2 * cx + cy, pl.ds(cs * h, h)]
                    pltpu.make_async_remote_copy(src_ref=blk, dst_ref=blk, send_sem=send_refs[t].at[2 * j + cs], recv_sem=recv_refs[t].at[2 * j + cs],
                                                 device_id=(cx, cy, cs), device_id_type=MESH).wait()
            blk = lands_ref[t].at[2 * x + y]
            pltpu.make_async_remote_copy(src_ref=blk, dst_ref=blk, send_sem=send_refs[t].at[PEERS - 1], recv_sem=recv_refs[t].at[PEERS - 1],
                                         device_id=(x, y, 1 - c), device_id_type=MESH).wait()

    out = pl.pallas_call(
        body, name=name, in_specs=[HBM] * (2 * nt) + [SEM] * (2 * nt) + [ANY], out_specs=[HBM] * (2 * nt),
        out_shape=[pltpu.HBM(s.shape, s.dtype) for s in shards] + [pltpu.HBM(l.shape, l.dtype) for l in lands],
        input_output_aliases={t: t for t in range(2 * nt)}, compiler_params=_split_params())(*shards, *lands, *sends, *recvs, after)
    return out[nt:]


def _gather_pieces_start(v, shard, name):
    h = shard.shape[0] // 2

    def body(v_ref, v_land, src, land, v_send, v_recv, send, recv, *rest):
        x, y, c, chips = _place()
        for j, flips in enumerate(_flips()):
            pltpu.make_async_remote_copy(src_ref=v_ref, dst_ref=v_land.at[4 * x + 2 * y + c], send_sem=v_send.at[j], recv_sem=v_recv.at[j],
                                         device_id=_flipped(x, y, c, flips), device_id_type=MESH).start()
        me = 2 * x + y
        mine = pl.ds(c * h, h)
        for j, (cx, cy) in enumerate(chips):
            pltpu.make_async_remote_copy(src_ref=src.at[mine], dst_ref=land.at[me, mine], send_sem=send.at[j], recv_sem=recv.at[j],
                                         device_id=(cx, cy, c), device_id_type=MESH).start()
        pltpu.make_async_remote_copy(src_ref=src, dst_ref=land.at[me], send_sem=send.at[len(chips)], recv_sem=recv.at[len(chips)],
                                     device_id=(x, y, 1 - c), device_id_type=MESH).start()
        rest[-1][...] = jnp.zeros(TOKEN_SHAPE.shape, F32)

    v_land = lax.empty((8,) + v.shape, v.dtype)
    land = lax.empty((4,) + shard.shape, shard.dtype)
    passed = [v, v_land, shard, land]
    out = pl.pallas_call(
        body, name=name, in_specs=[HBM] * 4, out_specs=[SEM] * 4 + [HBM] * 4 + [TOKEN],
        out_shape=[pltpu.SemaphoreType.DMA((PEERS,))] * 2 + [pltpu.SemaphoreType.DMA((4,))] * 2
        + [pltpu.HBM(a.shape, a.dtype) for a in passed] + [TOKEN_SHAPE],
        input_output_aliases={t: 4 + t for t in range(4)}, compiler_params=_split_params())(*[_in_hbm(a) for a in passed])
    return (out[0], out[1], out[4], out[5]), (out[2], out[3], out[6], out[7]), out[8]


def _gather_pieces_wait(send, recv, shard, land, after, name):
    h = shard.shape[0] // 2
    after, after_specs = _unread(after)

    def body(*refs):
        land_ref, send_ref, recv_ref = refs[1:4]
        x, y, c, chips = _place()
        for j, (cx, cy) in enumerate(chips):
            blk = land_ref.at[2 * cx + cy, pl.ds(c * h, h)]
            pltpu.make_async_remote_copy(src_ref=blk, dst_ref=blk, send_sem=send_ref.at[j], recv_sem=recv_ref.at[j],
                                         device_id=(cx, cy, c), device_id_type=MESH).wait()
        own = land_ref.at[2 * x + y]
        pltpu.make_async_remote_copy(src_ref=own, dst_ref=own, send_sem=send_ref.at[len(chips)], recv_sem=recv_ref.at[len(chips)],
                                     device_id=(x, y, 1 - c), device_id_type=MESH).wait()

    out = pl.pallas_call(
        body, name=name, in_specs=[HBM, HBM, SEM, SEM] + after_specs, out_specs=[HBM, HBM],
        out_shape=[pltpu.HBM(shard.shape, shard.dtype), pltpu.HBM(land.shape, land.dtype)],
        input_output_aliases={0: 0, 1: 1}, compiler_params=_split_params())(shard, land, send, recv, *after)
    return out[1]


def _pass_pieces(land, name):
    h = land.shape[1] // 2

    def body(_, out, send, recv):
        x, y, c, chips = _place()

        def piece(j, cc):
            cx, cy = chips[j]
            blk = out.at[2 * cx + cy, pl.ds(cc * h, h)]
            return pltpu.make_async_remote_copy(src_ref=blk, dst_ref=blk, send_sem=send.at[j], recv_sem=recv.at[j],
                                                device_id=(x, y, 1 - c), device_id_type=MESH)

        for j in range(len(chips)):
            piece(j, c).start()
        for j in range(len(chips)):
            piece(j, 1 - c).wait_recv()
        for j in range(len(chips)):
            piece(j, c).wait_send()

    return pl.pallas_call(
        body, name=name, in_specs=[ANY], out_specs=ANY, out_shape=SDS(land.shape, land.dtype), input_output_aliases={0: 0},
        scratch_shapes=[pltpu.SemaphoreType.DMA((3,))] * 2, compiler_params=pltpu.CompilerParams(has_side_effects=True))(land)


def _scatter_start(gs, name):
    nt = len(gs)

    def body(*refs):
        g_refs, lands = refs[:nt], refs[nt:2 * nt]
        sends, recvs = refs[2 * nt:3 * nt], refs[3 * nt:4 * nt]
        x, y, c, chips = _place()
        for g_ref, land, send, recv in zip(g_refs, lands, sends, recvs):
            h = land.shape[1]
            for j, (cx, cy) in enumerate(chips):
                for dc in range(2):
                    pltpu.make_async_remote_copy(src_ref=g_ref.at[2 * cx + cy, pl.ds(dc * h, h)], dst_ref=land.at[2 * j + c],
                                                 send_sem=send.at[2 * j + dc], recv_sem=recv.at[2 * j + c], device_id=(cx, cy, dc),
                                                 device_id_type=MESH).start()
            pltpu.make_async_remote_copy(src_ref=g_ref.at[2 * x + y, pl.ds((1 - c) * h, h)], dst_ref=land.at[PEERS - 1], send_sem=send.at[PEERS - 1],
                                         recv_sem=recv.at[PEERS - 1], device_id=(x, y, 1 - c), device_id_type=MESH).start()
        refs[-1][...] = jnp.zeros(TOKEN_SHAPE.shape, F32)

    lands = [lax.empty((PEERS, g.shape[1] // 2, g.shape[2]), g.dtype) for g in gs]
    out = pl.pallas_call(
        body, name=name, in_specs=[HBM] * (2 * nt), out_specs=[SEM] * (2 * nt) + [HBM] * (2 * nt) + [TOKEN],
        out_shape=[pltpu.SemaphoreType.DMA((PEERS,))] * (2 * nt) + [pltpu.HBM(a.shape, a.dtype) for a in gs + lands] + [TOKEN_SHAPE],
        input_output_aliases={t: 2 * nt + t for t in range(2 * nt)}, compiler_params=_split_params())(
            *[_in_hbm(a) for a in gs + lands])
    return [tuple(out[q * nt + t] for q in range(4)) for t in range(nt)], out[-1]


def _scatter_wait(started, after, name):
    nt = len(started)

    def body(*refs):
        lands = refs[nt:2 * nt]
        sends, recvs = refs[2 * nt:3 * nt], refs[3 * nt:4 * nt]
        x, y, c, chips = _place()
        peers = [(cx, cy, dc) for cx, cy in chips for dc in range(2)] + [(x, y, 1 - c)]
        for t in range(nt):
            for k, peer in enumerate(peers):
                blk = lands[t].at[k]
                pltpu.make_async_remote_copy(src_ref=blk, dst_ref=blk, send_sem=sends[t].at[k], recv_sem=recvs[t].at[k],
                                             device_id=peer, device_id_type=MESH).wait()

    gs, lands = [s[2] for s in started], [s[3] for s in started]
    after, after_specs = _unread(after)
    out = pl.pallas_call(
        body, name=name, in_specs=[HBM] * (2 * nt) + [SEM] * (2 * nt) + after_specs, out_specs=[HBM] * (2 * nt),
        out_shape=[pltpu.HBM(a.shape, a.dtype) for a in gs + lands],
        input_output_aliases={t: t for t in range(2 * nt)}, compiler_params=_split_params())(
            *gs, *lands, *[s[0] for s in started], *[s[1] for s in started], *after)
    return out[:nt], out[nt:]


def _join_start(bufs, name):
    nt = len(bufs)

    def body(*refs):
        send, recv = refs[nt:nt + 2]
        x, y, c, _ = _place()
        for t in range(nt):
            pltpu.make_async_remote_copy(src_ref=refs[t].at[c], dst_ref=refs[t].at[c], send_sem=send.at[t], recv_sem=recv.at[t],
                                         device_id=(x, y, 1 - c), device_id_type=MESH).start()
        refs[-1][...] = jnp.zeros(TOKEN_SHAPE.shape, F32)

    out = pl.pallas_call(
        body, name=name, in_specs=[HBM] * nt, out_specs=[SEM, SEM] + [HBM] * nt + [TOKEN],
        out_shape=[pltpu.SemaphoreType.DMA((nt,))] * 2 + [pltpu.HBM(b.shape, b.dtype) for b in bufs] + [TOKEN_SHAPE],
        input_output_aliases={t: 2 + t for t in range(nt)}, compiler_params=_split_params())(*[_in_hbm(b) for b in bufs])
    return out[0], out[1], out[2:2 + nt], out[-1]


def _join_wait(send, recv, bufs, after, name):
    nt = len(bufs)
    after, after_specs = _unread(after)

    def body(*refs):
        send_ref, recv_ref = refs[nt:nt + 2]
        x, y, c, _ = _place()
        for t in range(nt):
            theirs = refs[t].at[1 - c]
            pltpu.make_async_remote_copy(src_ref=theirs, dst_ref=theirs, send_sem=send_ref.at[t], recv_sem=recv_ref.at[t],
                                         device_id=(x, y, 1 - c), device_id_type=MESH).wait()

    return pl.pallas_call(
        body, name=name, in_specs=[HBM] * nt + [SEM, SEM] + after_specs, out_specs=[HBM] * nt,
        out_shape=[pltpu.HBM(b.shape, b.dtype) for b in bufs],
        input_output_aliases={t: t for t in range(nt)}, compiler_params=_split_params())(*bufs, send, recv, *after)


def _flips():
    return [(dx, dy, dc) for dx in range(2) for dy in range(2) for dc in range(2) if (dx, dy, dc) != (0, 0, 0)]


def _flipped(x, y, c, flips):
    dx, dy, dc = flips
    return (1 - x if dx else x, 1 - y if dy else y, 1 - c if dc else c)


def _small_start(v, name, after=()):
    after, after_specs = _unread(after)

    def body(v_ref, land, *rest):
        send, recv = rest[len(after):len(after) + 2]
        x, y, c, _ = _place()
        for j, flips in enumerate(_flips()):
            pltpu.make_async_remote_copy(src_ref=v_ref, dst_ref=land.at[4 * x + 2 * y + c], send_sem=send.at[j], recv_sem=recv.at[j],
                                         device_id=_flipped(x, y, c, flips), device_id_type=MESH).start()
        rest[-1][...] = jnp.zeros(TOKEN_SHAPE.shape, F32)

    land = lax.empty((8,) + v.shape, v.dtype)
    return pl.pallas_call(
        body, name=name, in_specs=[HBM, HBM] + after_specs, out_specs=[SEM, SEM, HBM, HBM, TOKEN],
        out_shape=[pltpu.SemaphoreType.DMA((PEERS,)), pltpu.SemaphoreType.DMA((PEERS,)), pltpu.HBM(v.shape, v.dtype),
                   pltpu.HBM(land.shape, land.dtype), TOKEN_SHAPE],
        input_output_aliases={0: 2, 1: 3}, compiler_params=_split_params())(_in_hbm(v), _in_hbm(land), *after)


def _small_wait(send, recv, v, land, after, name):
    after, after_specs = _unread(after)

    def body(v_ref, land_ref, send_ref, recv_ref, *rest):
        x, y, c, _ = _place()
        for j, flips in enumerate(_flips()):
            px, py, pc = _flipped(x, y, c, flips)
            blk = land_ref.at[4 * px + 2 * py + pc]
            pltpu.make_async_remote_copy(src_ref=blk, dst_ref=blk, send_sem=send_ref.at[j], recv_sem=recv_ref.at[j],
                                         device_id=(px, py, pc), device_id_type=MESH).wait()

    return pl.pallas_call(
        body, name=name, in_specs=[HBM, HBM, SEM, SEM] + after_specs, out_specs=[HBM, HBM],
        out_shape=[pltpu.HBM(v.shape, v.dtype), pltpu.HBM(land.shape, land.dtype)],
        input_output_aliases={0: 0, 1: 1}, compiler_params=_split_params())(v, land, send, recv, *after)


def _sum_small(v, land, name):
    def body(v_ref, land_ref, o_ref):
        x, y, c, _ = _place()
        me = 4 * x + 2 * y + c
        acc = jnp.where(me == 0, v_ref[...], land_ref[0])
        for d in range(1, 8):
            acc = acc + jnp.where(me == d, v_ref[...], land_ref[d])
        o_ref[...] = acc

    vm = pl.BlockSpec(memory_space=pltpu.VMEM)
    return pl.pallas_call(body, name=name, in_specs=[vm, vm], out_specs=vm, out_shape=SDS(v.shape, F32))(v, land)


SUM_STEPS = 2


def _sum_devices(gs, lands, me, core, name):
    nt = len(gs)

    def body(ix_ref, *refs):
        for own_ref, land_ref, o_ref in zip(refs[:nt], refs[nt:2 * nt], refs[2 * nt:]):
            acc = own_ref[0].astype(F32)
            for j in range(land_ref.shape[0]):
                acc = acc + land_ref[j].astype(F32)
            o_ref[0] = acc

    tiles = [(land.shape[1] // SUM_STEPS, land.shape[2]) for land in lands]
    assert all(land.shape[1] == tr * SUM_STEPS and tr % ROWS_PER_16BIT_TILE == 0 for land, (tr, _) in zip(lands, tiles)), name
    grid_spec = pltpu.PrefetchScalarGridSpec(
        num_scalar_prefetch=1, grid=(SUM_STEPS,),
        in_specs=[pl.BlockSpec((1, tr, c), lambda i, ix: (ix[0], ix[1] * SUM_STEPS + i, 0)) for tr, c in tiles]
        + [pl.BlockSpec((land.shape[0], tr, c), lambda i, ix: (0, i, 0)) for land, (tr, c) in zip(lands, tiles)],
        out_specs=[pl.BlockSpec((1, tr, c), lambda i, ix: (ix[1], i, 0)) for tr, c in tiles])
    return pl.pallas_call(body, name=name, grid_spec=grid_spec, out_shape=[SDS((2,) + land.shape[1:], F32) for land in lands],
                          compiler_params=_cp("parallel"))(jnp.stack([me, core]), *gs, *lands)


def _pack_small(parts):
    flat = jnp.concatenate([p.reshape(-1) for p in parts])
    total = flat.shape[0]
    rows = -(-total // 1024) * 8
    return jnp.pad(flat, (0, rows * 128 - total)).reshape(rows, 128)


def _pack_lanes(parts):
    cols, spans, off = [], [], 0
    for p in parts:
        size = int(np.prod(p.shape))
        width = -(-size // 128) * 128
        cols.append(jnp.pad(p.reshape(1, size), ((0, 0), (0, width - size))))
        spans.append((off, size))
        off += width
    return jnp.concatenate(cols, axis=1), spans


def _unpack_small(packed, shapes):
    flat = packed.reshape(-1)
    out, off = [], 0
    for s in shapes:
        size = int(np.prod(s))
        out.append(flat[off:off + size].reshape(s))
        off += size
    return out


def _local_step(x, mem, target, w_in_t, first_after, mid_weights, ffn_weights, on_grad, gains, conv_w, conv_b, hg_lb):
    n = x.shape[0]
    cos, sin = _rope_tables(n)
    seg = _hg_segments()
    gp, gn = _hg_pair_sums()
    masks = _hg_level_masks()
    gq2 = jnp.tile(gains["q_norm_g"], (1, 2))
    gk2 = jnp.tile(gains["k_norm_g"], (1, 2))
    a0 = hg_lb[:, 0:1, :]
    a1 = hg_lb[:, 1:2, :]

    p, h1 = _norm_mm(x, gains["pre_mix_g"], w_in_t, F32, TOKEN_TILE, 1664, "in_proj", after=(first_after,), w_turned=True)
    qr, kr = _qk_prep(p, gq2, gk2, cos, sin, "qk_prep")
    heads = lambda a: a.reshape(n, ATT_KV_HEADS, ATT_HEAD_DIM).transpose(1, 0, 2)
    kh = heads(kr)
    vh = heads(p[:, OFF_AV:OFF_AV + ATT_KV_DIM].astype(MXU_DTYPE))
    att = _attn_fwd(qr, kh, vh, "attn_fwd")
    o2, s0, hg_a, hg_e, hg_kept = _hgrn_fwd(p, a0, a1, seg, masks, "hgrn_fwd")
    rec = _hg_post(o2, p, gains["hg_out_norm_g"], "hg_post")
    cat = jnp.concatenate([att, rec], axis=1)
    w_out, w_xq, w_xkv, w_xo = mid_weights(cat)
    mixed, x1 = _mm_resid_norm(cat, w_out, x, gains["post_mix_g"], 512, "out_proj_resid")
    xq, h2 = _norm_mm(x1, gains["pre_x_g"], w_xq, MXU_DTYPE, TOKEN_TILE, 1024, "xq_proj")
    kv, mn = _norm_mm(mem, gains["mem_norm_g"], w_xkv, MXU_DTYPE, 256, 2048, "xkv_proj")
    ox = _xattn_fwd(xq, kv, "xattn_fwd")
    xo, x2 = _mm_resid_norm(ox, w_xo, x1, gains["post_x_g"], 512, "xo_proj_resid")
    w_up = ffn_weights("w_up", x2)
    u, h3 = _norm_mm(x2, gains["pre_ffn_g"], w_up, F32, TOKEN_TILE, 1408, "up_proj")
    act = _conv_gate(u, conv_w, conv_b, "conv_gate")
    w_down = ffn_weights("w_down", act)
    dn, d3, loss = _mm_resid_norm(act, w_down, x2, gains["post_ffn_g"], 512, "down_proj_resid_loss", target=target)

    gs = {}
    d_act, d_dn, gs["post_ffn_g"] = _norm_bwd_mm(dn, gains["post_ffn_g"], d3, w_down, F32, 512, 1408, "ffn_post_bwd_down_dx")
    tok = on_grad("w_down", _mm(act, d_dn, "tn", WIRE_DTYPE, 1408, 1024, "down_dw"))
    du_g, du_v, dcw_g, dcw_v, dcb_g, dcb_v = _conv_gate_bwd(u, conv_w, conv_b, d_act, "conv_gate_bwd", after=(tok,))
    gs["conv_w"] = jnp.concatenate([dcw_g, dcw_v], axis=1)
    gs["conv_b"] = jnp.concatenate([dcb_g, dcb_v], axis=1)
    ff_shard = w_up.shape[2]
    g_up = _dw_by_owner(h3, du_g, ff_shard, 0, None, 512, "up_dw_gate")
    tok = on_grad("w_up", _dw_by_owner(h3, du_v, ff_shard, 2, g_up, 512, "up_dw_value"))
    d2, gs["pre_ffn_g"] = _dx_norm_bwd([(du_g, 0), (du_g, 1), (du_v, 0), (du_v, 1)], w_up, x2, gains["pre_ffn_g"], d3, 512,
                                       "up_dx_pre_bwd", after=(tok,))
    d_ox, d_xo, gs["post_x_g"] = _norm_bwd_mm(xo, gains["post_x_g"], d2, w_xo, MXU_DTYPE, 512, 1024, "x_post_bwd_xo_dx")
    tok = on_grad("w_xo", _mm(ox, d_xo, "tn", WIRE_DTYPE, 512, 1024, "xo_dw"))
    d_xq, d_k, d_v = _xattn_bwd(xq, kv, d_ox, "xattn_bwd", after=(tok,))
    d_kv = jnp.concatenate([d_k, d_v], axis=1).astype(MXU_DTYPE)
    tok = on_grad("w_xq", _mm(h2, d_xq, "tn", WIRE_DTYPE, 512, 1024, "xq_dw"))
    tok_kv = on_grad("w_xkv", _dw_by_owner(mn, d_kv, w_xkv.shape[2], 0, None, 512, "xkv_dw"))
    d1, gs["pre_x_g"] = _dx_norm_bwd([(d_xq, 0)], w_xq[None], x1, gains["pre_x_g"], d2, 512, "xq_dx_pre_bwd", after=(tok, tok_kv))
    d_mn = _mm_nt_parts([(d_kv, s) for s in range(4)], w_xkv, F32, 256, 1024, "xkv_dx")
    _, gs["mem_norm_g"] = _norm_bwd(mem, gains["mem_norm_g"], d_mn, None, MXU_DTYPE, "mem_norm_bwd")
    d_cat, d_mixed, gs["post_mix_g"] = _norm_bwd_mm(mixed, gains["post_mix_g"], d1, w_out, MXU_DTYPE, 512, 1024, "mix_post_bwd_out_dx")
    tok = on_grad("w_out", _mm(cat, d_mixed, "tn", WIRE_DTYPE, 512, 1024, "out_dw"))
    d_o, d_hg, dg_hg = _hg_post_bwd(o2, p, gains["hg_out_norm_g"], d_cat, "hg_post_bwd", after=(tok,))
    gs["hg_out_norm_g"] = dg_hg.reshape(HG_HEADS, HG_HEAD_DIM).sum(axis=0, keepdims=True)
    dhq2, dz2, dhv2, dlb = _hgrn_bwd(p, a0, a1, masks, gp, gn, d_o, s0, hg_a, hg_e, hg_kept, "hgrn_bwd")
    lb = jax.nn.sigmoid(a0 - a1)
    da0 = dlb * lb * (1.0 - lb)
    gs["hg_lb"] = jnp.concatenate([da0, -da0], axis=1)
    d_qr, d_kh, d_vh = _attn_bwd(qr, kh, vh, cat, d_cat, "attn_bwd")
    unheads = lambda a: a.transpose(2, 0, 1).reshape(n, ATT_KV_DIM)
    d_aq, d_ak, dgq, dgk = _qk_prep_bwd(p, gq2, gk2, cos, sin, d_qr, unheads(d_kh), "qk_prep_bwd")
    gs["q_norm_g"] = dgq.reshape(ATT_HEADS, ATT_HEAD_DIM).sum(axis=0, keepdims=True)
    gs["k_norm_g"] = dgk.reshape(ATT_KV_HEADS, ATT_HEAD_DIM).sum(axis=0, keepdims=True)
    d_p = jnp.concatenate([d_aq, d_ak, unheads(d_vh).astype(MXU_DTYPE), (dhq2[0] + dhq2[1]).astype(MXU_DTYPE),
                           dz2[0].astype(MXU_DTYPE), dz2[1].astype(MXU_DTYPE), (dhv2[0] + dhv2[1]).astype(MXU_DTYPE), d_hg], axis=1)
    tok = on_grad("w_in", _mm(d_p, h1, "tn", WIRE_DTYPE, 1664, 1024, "in_dw"))
    grad_x, gs["pre_mix_g"] = _dx_norm_bwd([(d_p, 0)], w_in_t[None], x, gains["pre_mix_g"], d1, 512, "in_dx_pre_bwd", after=(tok,),
                                           b_turned=True)
    return loss, grad_x, gs


MATS = ("w_in", "w_out", "w_xq", "w_xkv", "w_xo", "w_up", "w_down")
GAINS = ("pre_mix_g", "q_norm_g", "k_norm_g", "hg_out_norm_g", "post_mix_g", "pre_x_g", "mem_norm_g", "post_x_g", "pre_ffn_g", "post_ffn_g")
WEIGHTS = ('pre_mix_g', 'w_in', 'q_norm_g', 'k_norm_g', 'hg_lb', 'hg_out_norm_g', 'w_out', 'post_mix_g', 'pre_x_g', 'mem_norm_g', 'w_xq',
           'w_xkv', 'w_xo', 'post_x_g', 'pre_ffn_g', 'w_up', 'conv_w', 'conv_b', 'w_down', 'post_ffn_g')


def kernel(x, mem, pre_mix_g, w_in, q_norm_g, k_norm_g, hg_lb, hg_out_norm_g, w_out, post_mix_g, pre_x_g, mem_norm_g, w_xq, w_xkv, w_xo, post_x_g, pre_ffn_g, w_up, conv_w, conv_b, w_down, post_ffn_g, loss_target, m_pre_mix_g, m_w_in, m_q_norm_g, m_k_norm_g, m_hg_lb, m_hg_out_norm_g, m_w_out, m_post_mix_g, m_pre_x_g, m_mem_norm_g, m_w_xq, m_w_xkv, m_w_xo, m_post_x_g, m_pre_ffn_g, m_w_up, m_conv_w, m_conv_b, m_w_down, m_post_ffn_g, v_pre_mix_g, v_w_in, v_q_norm_g, v_k_norm_g, v_hg_lb, v_hg_out_norm_g, v_w_out, v_post_mix_g, v_pre_x_g, v_mem_norm_g, v_w_xq, v_w_xkv, v_w_xo, v_post_x_g, v_pre_ffn_g, v_w_up, v_conv_w, v_conv_b, v_w_down, v_post_ffn_g):
    args = dict(locals())
    w = {k: args[k] for k in WEIGHTS}
    m = {k: args["m_" + k] for k in WEIGHTS}
    v = {k: args["v_" + k] for k in WEIGHTS}
    chip = 2 * lax.axis_index("x") + lax.axis_index("y")
    core = lax.axis_index("c")

    turned = ("w_in",)
    shards = {k: (jnp.swapaxes(w[k], 1, 2) if k in turned else w[k])[0].astype(WIRE_DTYPE) for k in MATS}

    def whole(k, g):
        return g if k in ("w_xkv", "w_up") else g.reshape(-1, g.shape[-1])

    mid_names, ffn_names = ("w_out", "w_xq", "w_xkv", "w_xo"), ("w_up", "w_down")
    small, w_in_pieces, token = _gather_pieces_start(_pack_small([w["conv_w"][0], w["hg_lb"]]), shards["w_in"], "gather_first_start")
    mid = _gather_start([shards[k] for k in mid_names], "gather_mid_start", after=(token,))
    ffn = _gather_start([shards[k] for k in ffn_names], "gather_ffn_start", after=(mid[4],))
    w_in_t = whole("w_in", _pass_pieces(_gather_pieces_wait(*w_in_pieces, (ffn[4],), "gather_w_in_wait"), "gather_w_in_pass"))
    mine, others = _small_wait(*small, (w_in_t,), "gather_small_wait")
    small_in = lax.dynamic_update_slice_in_dim(others, mine[None], 2 * chip + core, axis=0)

    def mid_weights(after):
        return [whole(k, g) for k, g in zip(mid_names, _gather_wait(*mid[:4], after, "gather_mid_wait"))]

    def ffn_weights(k, after):
        t = ffn_names.index(k)
        return whole(k, _gather_wait(*[part[t:t + 1] for part in ffn[:4]], after, "gather_wait_" + k)[0])

    cw_parts, lb_parts = [], []
    for s in range(4):
        cw_s, lb_s = _unpack_small(small_in[2 * s], [w["conv_w"][0].shape, w["hg_lb"].shape])
        cw_parts.append(cw_s)
        lb_parts.append(lb_s)
    conv_w_full = jnp.concatenate(cw_parts, axis=1)
    hg_lb_full = jnp.concatenate(lb_parts, axis=2)

    started, held = {}, {}
    leaves_with_next = ("w_down", "w_xo", "w_xq", "w_xkv")

    def on_grad(k, g):
        if g.ndim == 2:
            g = g.reshape(4, g.shape[0] // 4, g.shape[1])
        held[k] = g
        if k in leaves_with_next:
            return None
        names = tuple(held)
        per_tensor, token = _scatter_start([held.pop(n) for n in names], "grad_start_" + k)
        started.update(zip(names, per_tensor))
        return token

    gains = {k: w[k] for k in GAINS}
    loss_part, grad_x, gs = _local_step(x[0], mem[0], loss_target[0], w_in_t, ffn[4], mid_weights, ffn_weights, on_grad, gains,
                                        conv_w_full, w["conv_b"], hg_lb_full)
    gs["loss"] = loss_part

    grads, delta, new_m, new_v = {}, {}, {}, {}

    def sum_and_send(names, after, tag):
        sent, landed = _scatter_wait([started[k] for k in names], after, "grad_wait_" + tag)
        return _join_start(_sum_devices(sent, landed, chip, core, "grad_sum_" + tag), "grad_join_start_" + tag)

    def joined(names, join, after, tag):
        for k, r in zip(names, _join_wait(*join[:3], after, "grad_join_wait_" + tag)):
            grads[k] = r.reshape(1, -1, r.shape[-1])

    def adamw(names, tag):
        params, backs = [], []
        for k in names:
            shape = w[k].shape
            keep = len(shape) == 3
            if k in turned:
                view, back = (lambda a: jnp.swapaxes(a, 1, 2)), (lambda a: jnp.swapaxes(a, 1, 2))
                g = grads[k]
            else:
                view = (lambda a, shape=shape: a.reshape(shape)) if keep else (lambda a, shape=shape: a.reshape(-1, shape[-1]))
                back = lambda a, shape=shape: a.reshape(shape)
                g = view(grads[k])
            params.append((view(w[k]), g, view(m[k]), view(v[k])))
            backs.append(back)
        for k, back, (d, mo, vo, go) in zip(names, backs, _adamw(params, "adamw_" + tag)):
            delta[k], new_m[k], new_v[k], grads[k] = back(d), back(mo), back(vo), back(go)

    small_names = GAINS + ("conv_b", "conv_w", "hg_lb")
    packed, spans = _pack_lanes([gs[k] for k in small_names + ("loss",)])
    spans = dict(zip(small_names + ("loss",), spans))
    small = _small_start(packed, "reduce_small_start", after=(grad_x,))

    ffn_join = sum_and_send(ffn_names, (grad_x, small[4]), "ffn")
    mid_join = sum_and_send(mid_names, (ffn_join[3],), "mid")
    joined(ffn_names, ffn_join, (mid_join[3],), "ffn")
    adamw(ffn_names, "ffn")
    joined(mid_names, mid_join, tuple(new_v[k] for k in ffn_names), "mid")
    adamw(mid_names, "mid")
    early = mid_names + ffn_names
    w_in_join = sum_and_send(("w_in",), tuple(new_v[k] for k in early), "w_in")

    mine, others = _small_wait(*small[:4], (w_in_join[3],), "reduce_small_wait")
    reduced_small = _sum_small(mine, others, "reduce_small_sum")
    summed = lambda k: reduced_small[0, spans[k][0]:spans[k][0] + spans[k][1]].reshape(gs[k].shape)
    loss = summed("loss")[0, 0]
    ncw = w["conv_w"].shape[2]
    grads["conv_w"] = lax.dynamic_slice_in_dim(summed("conv_w"), chip * ncw, ncw, axis=1)[None]
    nlb = w["hg_lb"].shape[2]
    grads["hg_lb"] = lax.dynamic_slice_in_dim(summed("hg_lb"), chip * nlb, nlb, axis=2)
    adamw(("conv_w", "hg_lb"), "sharded_small")
    replicated = GAINS + ("conv_b",)
    outs = _adamw_rows(reduced_small, [spans[k] for k in replicated], [(w[k], m[k], v[k]) for k in replicated], "adamw_replicated")
    for k, (d, mo, vo, go) in zip(replicated, outs):
        delta[k], new_m[k], new_v[k], grads[k] = d, mo, vo, go

    joined(("w_in",), w_in_join, tuple(new_v[k] for k in small_names), "w_in")
    adamw(("w_in",), "w_in")
    return (loss, grad_x[None], *[grads[k] for k in WEIGHTS], *[delta[k] for k in WEIGHTS],
            *[new_m[k] for k in WEIGHTS], *[new_v[k] for k in WEIGHTS])
```
